```python
import jax, jax.numpy as jnp
from jax import lax
import numpy as np

D_MODEL = 1024
BATCH = 8
SEQ = 4096
DEPTH = 2

N_MIXERS = 2
N_A_LAYERS = (DEPTH + 1) // 2
N_B_LAYERS = DEPTH // 2
CHUNK = 128
A_WIDTH = D_MODEL
A_GROUPS = 16
A_GROUP_DIM = A_WIDTH // A_GROUPS
B_HEADS = 16
B_HEAD_DIM = D_MODEL // B_HEADS
B_PATTERNS = ((128, 1), (512, 4), (2048, 16))
N_PAT = len(B_PATTERNS)
D_FF = 4 * D_MODEL
ALPHA = (2 * DEPTH) ** 0.25
BETA = (8 * DEPTH) ** -0.25
LN_EPS = 1e-5
ADA_SCALE = 0.1
NEG = -1e30

kernel_name = "hybrid_gmlp_dilated_attn_deepnorm_adaln"


def layer_norm(x, g, b):
    xf = x.astype(jnp.float32)
    mu = jnp.mean(xf, axis=-1, keepdims=True)
    var = jnp.mean(jnp.square(xf - mu), axis=-1, keepdims=True)
    y = (xf - mu) * lax.rsqrt(var + LN_EPS) * g.astype(jnp.float32) + b.astype(jnp.float32)
    return y.astype(x.dtype)


def ada_mod(c, w, b):
    m = jax.nn.silu(c) @ w + b
    shift, scale, gate = jnp.split(m, 3, axis=-1)
    return shift[:, None, :], scale[:, None, :], 1.0 + gate[:, None, :]


def alibi_slopes(n_heads):
    h = jnp.arange(1, n_heads + 1, dtype=jnp.float32)
    return jnp.exp2(-8.0 * h / n_heads)


def mixer_a(h, w_in, b_in, vn_g, vn_b, w_s, b_s, w_out):
    B, S, _ = h.shape
    uv = jax.nn.gelu(h @ w_in + b_in)
    u, v = jnp.split(uv, 2, axis=-1)
    v = layer_norm(v, vn_g, vn_b)
    v = v.reshape(B, S // CHUNK, CHUNK, A_GROUPS, A_GROUP_DIM)
    causal = jnp.tril(jnp.ones((CHUNK, CHUNK), dtype=bool))
    w_causal = jnp.where(causal, w_s, 0.0)
    z = jnp.einsum('gts,bnsgc->bntgc', w_causal, v) + b_s.T[:, :, None]
    z = z.reshape(B, S, A_WIDTH)
    return (u * z) @ w_out


def dilated_branch(q, k, v, window, dilation, slopes):
    B, S, H, E = q.shape
    span = window // dilation
    seg = span * dilation
    S_pad = -(-S // seg) * seg
    nb = S_pad // seg
    pad = ((0, 0), (0, S_pad - S), (0, 0), (0, 0))

    def to_blocks(t):
        return jnp.pad(t, pad).reshape(B, nb, span, dilation, H, E)

    def with_prev(t):
        prev = jnp.concatenate([jnp.zeros_like(t[:, :1]), t[:, :-1]], axis=1)
        return jnp.concatenate([prev, t], axis=2)

    qb = to_blocks(q)
    kb = with_prev(to_blocks(k))
    vb = with_prev(to_blocks(v))
    s = jnp.einsum('bnqrhe,bnkrhe->bnrhqk', qb, kb,
                   preferred_element_type=jnp.float32) * (E ** -0.5)
    qi = jnp.arange(span)[:, None]
    ki = jnp.arange(2 * span)[None, :]
    diff = span + qi - ki
    blk = jnp.arange(nb)[:, None, None]
    valid = (diff >= 0) & (diff <= span) & (blk * span + ki - span >= 0)
    bias = -slopes[:, None, None] * (dilation * diff).astype(jnp.float32)
    s = s + bias
    s = jnp.where(valid[None, :, None, None], s, NEG)
    m = jnp.max(s, axis=-1, keepdims=True)
    p = jnp.exp(s - m)
    l = jnp.sum(p, axis=-1, keepdims=True)
    o = jnp.einsum('bnrhqk,bnkrhe->bnqrhe', p / l, vb.astype(jnp.float32))
    lse = (m + jnp.log(l))[..., 0].transpose(0, 1, 4, 2, 3)
    o = o.reshape(B, S_pad, H, E)[:, :S]
    lse = lse.reshape(B, S_pad, H)[:, :S]
    return o, lse


def mixer_b(h, w_qkv, w_out, slopes):
    B, S, _ = h.shape
    qkv = (h @ w_qkv).reshape(B, S, N_PAT, 3, B_HEADS, B_HEAD_DIM)
    outs, lses = [], []
    for g, (window, dilation) in enumerate(B_PATTERNS):
        o, lse = dilated_branch(qkv[:, :, g, 0], qkv[:, :, g, 1], qkv[:, :, g, 2],
                                window, dilation, slopes)
        outs.append(o)
        lses.append(lse)
    wts = jax.nn.softmax(jnp.stack(lses, axis=0), axis=0)[..., None]
    o = jnp.sum(wts * jnp.stack(outs, axis=0), axis=0)
    return o.reshape(B, S, B_HEADS * B_HEAD_DIM).astype(h.dtype) @ w_out


def squared_relu_mlp(h, w_up, w_down):
    return jnp.square(jax.nn.relu(h @ w_up)) @ w_down


def _fwd_setup_inputs(seed: int = 0) -> dict:
    key = jax.random.key(seed)
    ks = jax.random.split(key, 18)
    f32 = jnp.float32
    nrm = lambda k, shape: jax.random.normal(k, shape, dtype=f32)
    return {
        "x": nrm(ks[0], (BATCH, SEQ, D_MODEL)),
        "c": nrm(ks[1], (BATCH, D_MODEL)),
        "ada_w": nrm(ks[2], (DEPTH, 2, D_MODEL, 3 * D_MODEL)) * (D_MODEL ** -0.5) * ADA_SCALE,
        "ada_b": nrm(ks[3], (DEPTH, 2, 3 * D_MODEL)) * 0.01,
        "ln_g": 1.0 + 0.02 * nrm(ks[4], (DEPTH, 2, D_MODEL)),
        "ln_b": 0.02 * nrm(ks[5], (DEPTH, 2, D_MODEL)),
        "a_w_in": nrm(ks[6], (N_A_LAYERS, D_MODEL, 2 * A_WIDTH)) * (D_MODEL ** -0.5),
        "a_b_in": 0.02 * nrm(ks[7], (N_A_LAYERS, 2 * A_WIDTH)),
        "a_vn_g": 1.0 + 0.02 * nrm(ks[8], (N_A_LAYERS, A_WIDTH)),
        "a_vn_b": 0.02 * nrm(ks[9], (N_A_LAYERS, A_WIDTH)),
        "a_w_s": nrm(ks[10], (N_A_LAYERS, A_GROUPS, CHUNK, CHUNK)) * (CHUNK ** -0.5),
        "a_b_s": 1.0 + 0.02 * nrm(ks[11], (N_A_LAYERS, A_GROUPS, CHUNK)),
        "a_w_out": nrm(ks[12], (N_A_LAYERS, A_WIDTH, D_MODEL)) * (A_WIDTH ** -0.5) * BETA,
        "b_w_qkv": nrm(ks[13], (N_B_LAYERS, D_MODEL, N_PAT * 3 * B_HEADS * B_HEAD_DIM)) * (D_MODEL ** -0.5),
        "b_w_out": nrm(ks[14], (N_B_LAYERS, B_HEADS * B_HEAD_DIM, D_MODEL)) * ((B_HEADS * B_HEAD_DIM) ** -0.5) * BETA,
        "mlp_w_up": nrm(ks[15], (DEPTH, D_MODEL, D_FF)) * (D_MODEL ** -0.5),
        "mlp_w_down": nrm(ks[16], (DEPTH, D_FF, D_MODEL)) * (D_FF ** -0.5) * BETA,
    }


def _fwd_reference(x, c, ada_w, ada_b, ln_g, ln_b, a_w_in, a_b_in, a_vn_g, a_vn_b, a_w_s, a_b_s,
              a_w_out, b_w_qkv, b_w_out, mlp_w_up, mlp_w_down):
    slopes = alibi_slopes(B_HEADS)
    for i in range(DEPTH):
        j = i // N_MIXERS
        shift, scale, gate = ada_mod(c, ada_w[i, 0], ada_b[i, 0])
        h = x * (1.0 + scale) + shift
        if i % N_MIXERS == 0:
            y = mixer_a(h, a_w_in[j], a_b_in[j], a_vn_g[j], a_vn_b[j], a_w_s[j], a_b_s[j], a_w_out[j])
        else:
            y = mixer_b(h, b_w_qkv[j], b_w_out[j], slopes)
        x = layer_norm(ALPHA * x + gate * y, ln_g[i, 0], ln_b[i, 0])
        shift, scale, gate = ada_mod(c, ada_w[i, 1], ada_b[i, 1])
        h = x * (1.0 + scale) + shift
        y = squared_relu_mlp(h, mlp_w_up[i], mlp_w_down[i])
        x = layer_norm(ALPHA * x + gate * y, ln_g[i, 1], ln_b[i, 1])
    return x


import jax as _jax
import jax.numpy as _jnp

TWIN_FORMAT = 'train_step'
FWD_PARAMS = ['x', 'c', 'ada_w', 'ada_b', 'ln_g', 'ln_b', 'a_w_in', 'a_b_in', 'a_vn_g', 'a_vn_b', 'a_w_s', 'a_b_s', 'a_w_out', 'b_w_qkv', 'b_w_out', 'mlp_w_up', 'mlp_w_down']
TWIN_WEIGHTS = ['ada_w', 'ada_b', 'ln_g', 'ln_b', 'a_w_in', 'a_b_in', 'a_vn_g', 'a_vn_b', 'a_w_s', 'a_b_s', 'a_w_out', 'b_w_qkv', 'b_w_out', 'mlp_w_up', 'mlp_w_down']
TWIN_DIFF_INPUT = 'x'
TWIN_INPUTS = ['x', 'c', 'ada_w', 'ada_b', 'ln_g', 'ln_b', 'a_w_in', 'a_b_in', 'a_vn_g', 'a_vn_b', 'a_w_s', 'a_b_s', 'a_w_out', 'b_w_qkv', 'b_w_out', 'mlp_w_up', 'mlp_w_down', 'loss_target', 'm_ada_w', 'm_ada_b', 'm_ln_g', 'm_ln_b', 'm_a_w_in', 'm_a_b_in', 'm_a_vn_g', 'm_a_vn_b', 'm_a_w_s', 'm_a_b_s', 'm_a_w_out', 'm_b_w_qkv', 'm_b_w_out', 'm_mlp_w_up', 'm_mlp_w_down', 'v_ada_w', 'v_ada_b', 'v_ln_g', 'v_ln_b', 'v_a_w_in', 'v_a_b_in', 'v_a_vn_g', 'v_a_vn_b', 'v_a_w_s', 'v_a_b_s', 'v_a_w_out', 'v_b_w_qkv', 'v_b_w_out', 'v_mlp_w_up', 'v_mlp_w_down']
TWIN_OUTPUTS = ['loss', 'grad_x', 'grad_ada_w', 'grad_ada_b', 'grad_ln_g', 'grad_ln_b', 'grad_a_w_in', 'grad_a_b_in', 'grad_a_vn_g', 'grad_a_vn_b', 'grad_a_w_s', 'grad_a_b_s', 'grad_a_w_out', 'grad_b_w_qkv', 'grad_b_w_out', 'grad_mlp_w_up', 'grad_mlp_w_down', 'delta_ada_w', 'delta_ada_b', 'delta_ln_g', 'delta_ln_b', 'delta_a_w_in', 'delta_a_b_in', 'delta_a_vn_g', 'delta_a_vn_b', 'delta_a_w_s', 'delta_a_b_s', 'delta_a_w_out', 'delta_b_w_qkv', 'delta_b_w_out', 'delta_mlp_w_up', 'delta_mlp_w_down', 'new_m_ada_w', 'new_m_ada_b', 'new_m_ln_g', 'new_m_ln_b', 'new_m_a_w_in', 'new_m_a_b_in', 'new_m_a_vn_g', 'new_m_a_vn_b', 'new_m_a_w_s', 'new_m_a_b_s', 'new_m_a_w_out', 'new_m_b_w_qkv', 'new_m_b_w_out', 'new_m_mlp_w_up', 'new_m_mlp_w_down', 'new_v_ada_w', 'new_v_ada_b', 'new_v_ln_g', 'new_v_ln_b', 'new_v_a_w_in', 'new_v_a_b_in', 'new_v_a_vn_g', 'new_v_a_vn_b', 'new_v_a_w_s', 'new_v_a_b_s', 'new_v_a_w_out', 'new_v_b_w_qkv', 'new_v_b_w_out', 'new_v_mlp_w_up', 'new_v_mlp_w_down']
TWIN_LEAF_KINDS = {'loss': 'loss', 'grad_x': 'grad_x', 'grad_ada_w': 'grad_w', 'grad_ada_b': 'grad_w', 'grad_ln_g': 'grad_w', 'grad_ln_b': 'grad_w', 'grad_a_w_in': 'grad_w', 'grad_a_b_in': 'grad_w', 'grad_a_vn_g': 'grad_w', 'grad_a_vn_b': 'grad_w', 'grad_a_w_s': 'grad_w', 'grad_a_b_s': 'grad_w', 'grad_a_w_out': 'grad_w', 'grad_b_w_qkv': 'grad_w', 'grad_b_w_out': 'grad_w', 'grad_mlp_w_up': 'grad_w', 'grad_mlp_w_down': 'grad_w', 'delta_ada_w': 'delta_w', 'delta_ada_b': 'delta_w', 'delta_ln_g': 'delta_w', 'delta_ln_b': 'delta_w', 'delta_a_w_in': 'delta_w', 'delta_a_b_in': 'delta_w', 'delta_a_vn_g': 'delta_w', 'delta_a_vn_b': 'delta_w', 'delta_a_w_s': 'delta_w', 'delta_a_b_s': 'delta_w', 'delta_a_w_out': 'delta_w', 'delta_b_w_qkv': 'delta_w', 'delta_b_w_out': 'delta_w', 'delta_mlp_w_up': 'delta_w', 'delta_mlp_w_down': 'delta_w', 'new_m_ada_w': 'new_m', 'new_m_ada_b': 'new_m', 'new_m_ln_g': 'new_m', 'new_m_ln_b': 'new_m', 'new_m_a_w_in': 'new_m', 'new_m_a_b_in': 'new_m', 'new_m_a_vn_g': 'new_m', 'new_m_a_vn_b': 'new_m', 'new_m_a_w_s': 'new_m', 'new_m_a_b_s': 'new_m', 'new_m_a_w_out': 'new_m', 'new_m_b_w_qkv': 'new_m', 'new_m_b_w_out': 'new_m', 'new_m_mlp_w_up': 'new_m', 'new_m_mlp_w_down': 'new_m', 'new_v_ada_w': 'new_v', 'new_v_ada_b': 'new_v', 'new_v_ln_g': 'new_v', 'new_v_ln_b': 'new_v', 'new_v_a_w_in': 'new_v', 'new_v_a_b_in': 'new_v', 'new_v_a_vn_g': 'new_v', 'new_v_a_vn_b': 'new_v', 'new_v_a_w_s': 'new_v', 'new_v_a_b_s': 'new_v', 'new_v_a_w_out': 'new_v', 'new_v_b_w_qkv': 'new_v', 'new_v_b_w_out': 'new_v', 'new_v_mlp_w_up': 'new_v', 'new_v_mlp_w_down': 'new_v'}


def _forward(args):
    return _fwd_reference(*[args[k] for k in FWD_PARAMS])


def _output_shape():
    def fwd():
        inp = _fwd_setup_inputs(0)
        return _fwd_reference(*[inp[k] for k in FWD_PARAMS])
    out = _jax.eval_shape(fwd)
    return out.shape, out.dtype

N_MICROBATCH = 1
ADAM_LR = 0.001
ADAM_B1 = 0.9
ADAM_B2 = 0.999
ADAM_EPS = 1e-08
ADAM_WD = 0.01
ADAM_STEP = 10
PER_EXAMPLE_BATCH_AXIS = {'x': 0, 'c': 0, 'loss_target': 0}
SHARED_INPUTS = []
_WEIGHT_DTYPES = {'ada_w': _jnp.float32, 'ada_b': _jnp.float32, 'ln_g': _jnp.float32, 'ln_b': _jnp.float32, 'a_w_in': _jnp.float32, 'a_b_in': _jnp.float32, 'a_vn_g': _jnp.float32, 'a_vn_b': _jnp.float32, 'a_w_s': _jnp.float32, 'a_b_s': _jnp.float32, 'a_w_out': _jnp.float32, 'b_w_qkv': _jnp.float32, 'b_w_out': _jnp.float32, 'mlp_w_up': _jnp.float32, 'mlp_w_down': _jnp.float32}
MOMENT_SCALE = {'ada_w': 4.862874e-02, 'ada_b': 1.254709e-01, 'ln_g': 1.609588e+01, 'ln_b': 3.995375e+00, 'a_w_in': 4.463522e-02, 'a_b_in': 7.763670e-02, 'a_vn_g': 3.091484e-02, 'a_vn_b': 3.072189e-02, 'a_w_s': 2.085958e-02, 'a_b_s': 2.964704e-02, 'a_w_out': 1.399472e-01, 'b_w_qkv': 1.245381e-02, 'b_w_out': 6.292154e-02, 'mlp_w_up': 4.325845e-02, 'mlp_w_down': 2.327316e-01}


def _to_microbatches(a, axis):
    t = _jnp.moveaxis(a, axis, 0)
    t = t.reshape((N_MICROBATCH, t.shape[0] // N_MICROBATCH) + t.shape[1:])
    return _jnp.moveaxis(t, 1, axis + 1)


def setup_inputs(seed: int = 0) -> dict:
    inp = _fwd_setup_inputs(seed)
    key = _jax.random.fold_in(_jax.random.key(seed), 7919)
    shape, _ = _output_shape()
    out = dict(inp)
    out["loss_target"] = _jax.random.normal(_jax.random.fold_in(key, 0), shape, _jnp.float32)
    for i, name in enumerate(TWIN_WEIGHTS):
        w = inp[name].astype(_jnp.float32)
        if MOMENT_SCALE is None:
            s = _jnp.sqrt(_jnp.mean(_jnp.square(w)) + 1e-30)
        else:
            s = MOMENT_SCALE[name]
        km, kv = _jax.random.split(_jax.random.fold_in(key, i + 1))
        out[name] = w
        out["m_" + name] = s * _jax.random.normal(km, w.shape, _jnp.float32)
        out["v_" + name] = (s * s) * _jax.random.uniform(kv, w.shape, _jnp.float32, 0.5, 1.5)
    if N_MICROBATCH > 1:
        for name, axis in PER_EXAMPLE_BATCH_AXIS.items():
            out[name] = _to_microbatches(out[name], axis)
    return {'x': out['x'], 'c': out['c'], 'ada_w': out['ada_w'], 'ada_b': out['ada_b'], 'ln_g': out['ln_g'], 'ln_b': out['ln_b'], 'a_w_in': out['a_w_in'], 'a_b_in': out['a_b_in'], 'a_vn_g': out['a_vn_g'], 'a_vn_b': out['a_vn_b'], 'a_w_s': out['a_w_s'], 'a_b_s': out['a_b_s'], 'a_w_out': out['a_w_out'], 'b_w_qkv': out['b_w_qkv'], 'b_w_out': out['b_w_out'], 'mlp_w_up': out['mlp_w_up'], 'mlp_w_down': out['mlp_w_down'], 'loss_target': out['loss_target'], 'm_ada_w': out['m_ada_w'], 'm_ada_b': out['m_ada_b'], 'm_ln_g': out['m_ln_g'], 'm_ln_b': out['m_ln_b'], 'm_a_w_in': out['m_a_w_in'], 'm_a_b_in': out['m_a_b_in'], 'm_a_vn_g': out['m_a_vn_g'], 'm_a_vn_b': out['m_a_vn_b'], 'm_a_w_s': out['m_a_w_s'], 'm_a_b_s': out['m_a_b_s'], 'm_a_w_out': out['m_a_w_out'], 'm_b_w_qkv': out['m_b_w_qkv'], 'm_b_w_out': out['m_b_w_out'], 'm_mlp_w_up': out['m_mlp_w_up'], 'm_mlp_w_down': out['m_mlp_w_down'], 'v_ada_w': out['v_ada_w'], 'v_ada_b': out['v_ada_b'], 'v_ln_g': out['v_ln_g'], 'v_ln_b': out['v_ln_b'], 'v_a_w_in': out['v_a_w_in'], 'v_a_b_in': out['v_a_b_in'], 'v_a_vn_g': out['v_a_vn_g'], 'v_a_vn_b': out['v_a_vn_b'], 'v_a_w_s': out['v_a_w_s'], 'v_a_b_s': out['v_a_b_s'], 'v_a_w_out': out['v_a_w_out'], 'v_b_w_qkv': out['v_b_w_qkv'], 'v_b_w_out': out['v_b_w_out'], 'v_mlp_w_up': out['v_mlp_w_up'], 'v_mlp_w_down': out['v_mlp_w_down']}


def _loss(weights, diff, rest, loss_target):
    with _jax.named_scope("forward"):
        args = {**rest, TWIN_DIFF_INPUT: diff, **{k: w.astype(_WEIGHT_DTYPES[k]) for k, w in weights.items()}}
        y = _forward(args)
    with _jax.named_scope("loss_head"):
        err = _jnp.square(y.astype(_jnp.float32) - loss_target)
        return 0.5 * _jnp.sum(_jnp.mean(err, axis=-1)) if err.ndim else 0.5 * err


def _adamw(w, g, m, v):
    m = ADAM_B1 * m + (1.0 - ADAM_B1) * g
    v = ADAM_B2 * v + (1.0 - ADAM_B2) * _jnp.square(g)
    m_hat = m / (1.0 - ADAM_B1 ** ADAM_STEP)
    v_hat = v / (1.0 - ADAM_B2 ** ADAM_STEP)
    delta = -ADAM_LR * (m_hat / (_jnp.sqrt(v_hat) + ADAM_EPS) + ADAM_WD * w)
    return delta, m, v


def reference(x, c, ada_w, ada_b, ln_g, ln_b, a_w_in, a_b_in, a_vn_g, a_vn_b, a_w_s, a_b_s, a_w_out, b_w_qkv, b_w_out, mlp_w_up, mlp_w_down, loss_target, m_ada_w, m_ada_b, m_ln_g, m_ln_b, m_a_w_in, m_a_b_in, m_a_vn_g, m_a_vn_b, m_a_w_s, m_a_b_s, m_a_w_out, m_b_w_qkv, m_b_w_out, m_mlp_w_up, m_mlp_w_down, v_ada_w, v_ada_b, v_ln_g, v_ln_b, v_a_w_in, v_a_b_in, v_a_vn_g, v_a_vn_b, v_a_w_s, v_a_b_s, v_a_w_out, v_b_w_qkv, v_b_w_out, v_mlp_w_up, v_mlp_w_down):
    given = dict(x=x, c=c, ada_w=ada_w, ada_b=ada_b, ln_g=ln_g, ln_b=ln_b, a_w_in=a_w_in, a_b_in=a_b_in, a_vn_g=a_vn_g, a_vn_b=a_vn_b, a_w_s=a_w_s, a_b_s=a_b_s, a_w_out=a_w_out, b_w_qkv=b_w_qkv, b_w_out=b_w_out, mlp_w_up=mlp_w_up, mlp_w_down=mlp_w_down, loss_target=loss_target, m_ada_w=m_ada_w, m_ada_b=m_ada_b, m_ln_g=m_ln_g, m_ln_b=m_ln_b, m_a_w_in=m_a_w_in, m_a_b_in=m_a_b_in, m_a_vn_g=m_a_vn_g, m_a_vn_b=m_a_vn_b, m_a_w_s=m_a_w_s, m_a_b_s=m_a_b_s, m_a_w_out=m_a_w_out, m_b_w_qkv=m_b_w_qkv, m_b_w_out=m_b_w_out, m_mlp_w_up=m_mlp_w_up, m_mlp_w_down=m_mlp_w_down, v_ada_w=v_ada_w, v_ada_b=v_ada_b, v_ln_g=v_ln_g, v_ln_b=v_ln_b, v_a_w_in=v_a_w_in, v_a_b_in=v_a_b_in, v_a_vn_g=v_a_vn_g, v_a_vn_b=v_a_vn_b, v_a_w_s=v_a_w_s, v_a_b_s=v_a_b_s, v_a_w_out=v_a_w_out, v_b_w_qkv=v_b_w_qkv, v_b_w_out=v_b_w_out, v_mlp_w_up=v_mlp_w_up, v_mlp_w_down=v_mlp_w_down)
    weights = {n: given[n] for n in TWIN_WEIGHTS}
    shared = {n: given[n] for n in SHARED_INPUTS}
    per_example = {n: given[n] for n in ['x', 'c']}
    grad_fn = _jax.value_and_grad(_loss, argnums=(0, 1))

    def one_microbatch(ex, loss_target):
        ex = dict(ex)
        diff = ex.pop(TWIN_DIFF_INPUT)
        return grad_fn(weights, diff, {**shared, **ex}, loss_target)

    if N_MICROBATCH == 1:
        loss, (grad_w, grad_x) = one_microbatch(per_example, given["loss_target"])
    else:
        def body(carry, xs):
            loss_sum, grad_sum = carry
            l_k, (gw_k, gx_k) = one_microbatch(xs[0], xs[1])
            with _jax.named_scope("update"):
                return (loss_sum + l_k, _jax.tree.map(_jnp.add, grad_sum, gw_k)), gx_k

        init = (_jnp.zeros((), _jnp.float32), _jax.tree.map(_jnp.zeros_like, weights))
        (loss, grad_w), grad_x = _jax.lax.scan(body, init, (per_example, given["loss_target"]))
    with _jax.named_scope("update"):
        delta_w, new_m, new_v = {}, {}, {}
        for n in TWIN_WEIGHTS:
            delta_w[n], new_m[n], new_v[n] = _adamw(weights[n], grad_w[n], given["m_" + n], given["v_" + n])
    return (loss, grad_x, *[grad_w[n] for n in TWIN_WEIGHTS], *[delta_w[n] for n in TWIN_WEIGHTS],
            *[new_m[n] for n in TWIN_WEIGHTS], *[new_v[n] for n in TWIN_WEIGHTS])
```

```python
import functools
import math

import jax
import jax.numpy as jnp
from jax import lax
from jax.experimental import pallas as pl
from jax.experimental.pallas import tpu as pltpu

F32 = jnp.float32
MXU_DTYPE = jnp.bfloat16

DEPTH = 2
CHUNK = 128
A_GROUPS = 16
B_HEADS = 16
HEAD_DIM = 64
B_PATTERNS = ((128, 1), (512, 4), (2048, 16))
SPAN = 128
ALPHA = (2 * DEPTH) ** 0.25
LN_EPS = 1e-5
NEG = -1e30
ATT_SCALE = HEAD_DIM ** -0.5
ADAM_LR, ADAM_B1, ADAM_B2, ADAM_EPS, ADAM_WD, ADAM_STEP = 0.001, 0.9, 0.999, 1e-08, 0.01, 10

N_CHIPS = 4
N_DEV = 8
LANES = 128
SUBLANES = 8
VMEM_LIMIT = 52 * 1024 * 1024
ROW_TILE = 256
MESH = pl.DeviceIdType.MESH


def _cparams(sem):
    return pltpu.CompilerParams(dimension_semantics=sem, vmem_limit_bytes=VMEM_LIMIT)


def _fold8(v):
    r, c = v.shape
    return jnp.sum(v.reshape(r // SUBLANES, SUBLANES, c), axis=0)


def _gelu(x):
    c = math.sqrt(2.0 / math.pi)
    return 0.5 * x * (1.0 + jnp.tanh(c * (x + 0.044715 * (x * x * x))))


def _gelu_grad(x):
    c = math.sqrt(2.0 / math.pi)
    t = jnp.tanh(c * (x + 0.044715 * (x * x * x)))
    return 0.5 * (1.0 + t) + 0.5 * x * (1.0 - t * t) * c * (1.0 + 3.0 * 0.044715 * x * x)


def _dot(a, b, dims):
    return lax.dot_general(a.astype(MXU_DTYPE), b.astype(MXU_DTYPE), (dims, ((), ())), preferred_element_type=F32)


def _dot_nn(a, b):
    return _dot(a, b, ((1,), (0,)))


def _dot_nt(a, b):
    return _dot(a, b, ((1,), (1,)))


def _dot_tn(a, b):
    return _dot(a, b, ((0,), (0,)))


def _mm(a, b, *, mode, name, outs, tm, tn, tk, epi=None, extras=(), b_col0=0, n_out=None):
    if mode == "nn":
        m, kdim = a.shape
        p, kb, ns = b.shape
        assert kb == kdim and ns % tn == 0 and b_col0 % tn == 0
        n = n_out if n_out is not None else p * ns
        npt, j0 = ns // tn, b_col0 // tn
        a_spec = pl.BlockSpec((tm, tk), lambda i, j, k: (i, k))
        b_spec = pl.BlockSpec((None, tk, tn), lambda i, j, k: ((j + j0) // npt, k, (j + j0) % npt))
        dot = _dot_nn
    elif mode == "nt":
        m, kdim = a.shape
        p, n, ns = b.shape
        assert ns % tk == 0 and b_col0 % tk == 0
        npt, j0 = ns // tk, b_col0 // tk
        a_spec = pl.BlockSpec((tm, tk), lambda i, j, k: (i, k))
        b_spec = pl.BlockSpec((None, tn, tk), lambda i, j, k: ((k + j0) // npt, j, (k + j0) % npt))
        dot = _dot_nt
    else:
        kdim, m = a.shape
        kb, n = b.shape
        assert kb == kdim
        a_spec = pl.BlockSpec((tk, tm), lambda i, j, k: (k, i))
        b_spec = pl.BlockSpec((tk, tn), lambda i, j, k: (k, j))
        dot = _dot_tn
    assert m % tm == 0 and n % tn == 0 and kdim % tk == 0, (name, m, n, kdim, tm, tn, tk)
    nk = kdim // tk
    ex_specs, ex_arrays = [], []
    for kind, arr in extras:
        if kind == "row":
            ex_specs.append(pl.BlockSpec((1, tn), lambda i, j, k: (0, j)))
        else:
            ex_specs.append(pl.BlockSpec((tm, tn), lambda i, j, k: (i, j)))
        ex_arrays.append(arr)
    n_ex, n_o = len(ex_arrays), len(outs)

    def body(a_ref, b_ref, *rest):
        ex_refs, o_refs, acc = rest[:n_ex], rest[n_ex:n_ex + n_o], rest[n_ex + n_o]
        k = pl.program_id(2)

        @pl.when(k == 0)
        def _():
            acc[...] = jnp.zeros_like(acc)

        acc[...] += dot(a_ref[...], b_ref[...])

        @pl.when(k == nk - 1)
        def _():
            r = acc[...]
            vals = epi(r, *[e[...] for e in ex_refs]) if epi is not None else [r]
            for o, v in zip(o_refs, vals):
                o[...] = v.astype(o.dtype)

    res = pl.pallas_call(
        body,
        grid=(m // tm, n // tn, nk),
        in_specs=[a_spec, b_spec] + ex_specs,
        out_specs=[pl.BlockSpec((tm, tn), lambda i, j, k: (i, j)) for _ in outs],
        out_shape=[jax.ShapeDtypeStruct((m, n), dt) for dt in outs],
        scratch_shapes=[pltpu.VMEM((tm, tn), F32)],
        name=name,
        compiler_params=_cparams(("parallel", "parallel", "arbitrary")),
    )(a, b, *ex_arrays)
    return res if len(outs) > 1 else res[0]


def _rows(body, n_rows, tr, ins, outs, name, scratch=()):
    def spec(kind, shape):
        if kind == "blk":
            return pl.BlockSpec((tr,) + tuple(shape[1:]), lambda i: (i,) + (0,) * (len(shape) - 1))
        return pl.BlockSpec(tuple(shape), lambda i: (0,) * len(shape))

    return pl.pallas_call(
        body,
        grid=(n_rows // tr,),
        in_specs=[spec(k, a.shape) for k, a in ins],
        out_specs=[spec(k, s) for k, s, _ in outs],
        out_shape=[jax.ShapeDtypeStruct(tuple(s), d) for _, s, d in outs],
        scratch_shapes=list(scratch),
        name=name,
        compiler_params=_cparams(("arbitrary",)),
    )(*[a for _, a in ins])


def _ln_stats(z):
    mu = jnp.mean(z, axis=-1, keepdims=True)
    zc = z - mu
    var = jnp.mean(zc * zc, axis=-1, keepdims=True)
    rstd = lax.rsqrt(var + LN_EPS)
    return zc * rstd, rstd


def _mod(x, scale, shift, name):
    s, d = x.shape

    def body(x_ref, sc_ref, sh_ref, h_ref):
        h_ref[...] = (x_ref[...] * (1.0 + sc_ref[...]) + sh_ref[...]).astype(h_ref.dtype)

    return _rows(body, s, ROW_TILE, [("blk", x), ("all", scale), ("all", shift)], [("blk", (s, d), MXU_DTYPE)], name)[0]


def _resid_ln(x, y, gate, g, b, nxt, name):
    s, d = x.shape
    ins = [("blk", x), ("blk", y), ("all", gate), ("all", g), ("all", b)]
    outs = [("blk", (s, d), F32)]
    if nxt is not None:
        ins += [("all", nxt[0]), ("all", nxt[1])]
        outs += [("blk", (s, d), MXU_DTYPE)]

    def body(x_ref, y_ref, gate_ref, g_ref, b_ref, *rest):
        z = ALPHA * x_ref[...] + gate_ref[...] * y_ref[...]
        xhat, _ = _ln_stats(z)
        xn = xhat * g_ref[...] + b_ref[...]
        if nxt is None:
            rest[0][...] = xn
        else:
            sc_ref, sh_ref, xn_ref, h_ref = rest
            xn_ref[...] = xn
            h_ref[...] = (xn * (1.0 + sc_ref[...]) + sh_ref[...]).astype(h_ref.dtype)

    res = _rows(body, s, ROW_TILE, ins, outs, name)
    return (res[0], res[1]) if nxt is not None else (res[0], None)


def _loss_grad(xf, target, name):
    s, d = xf.shape

    def body(x_ref, t_ref, dy_ref, l_ref, acc):
        i = pl.program_id(0)

        @pl.when(i == 0)
        def _():
            acc[...] = jnp.zeros_like(acc)

        e = x_ref[...] - t_ref[...]
        dy_ref[...] = e * (1.0 / d)
        acc[...] += _fold8(e * e)

        @pl.when(i == pl.num_programs(0) - 1)
        def _():
            l_ref[...] = jnp.full(l_ref.shape, 0.5 / d, F32) * jnp.sum(acc[...])

    dy, l = _rows(body, s, ROW_TILE, [("blk", xf), ("blk", target)],
                  [("blk", (s, d), F32), ("all", (SUBLANES, LANES), F32)], name,
                  scratch=[pltpu.VMEM((SUBLANES, d), F32)])
    return dy, l[0, 0]


def _ln_bwd(dxo, x, y, gate, g, name):
    s, d = x.shape

    def body(dxo_ref, x_ref, y_ref, gate_ref, g_ref, dxr_ref, dyy_ref, red_ref, a_g, a_b, a_gate):
        i = pl.program_id(0)

        @pl.when(i == 0)
        def _():
            a_g[...] = jnp.zeros_like(a_g)
            a_b[...] = jnp.zeros_like(a_b)
            a_gate[...] = jnp.zeros_like(a_gate)

        yv = y_ref[...]
        z = ALPHA * x_ref[...] + gate_ref[...] * yv
        xhat, rstd = _ln_stats(z)
        dxo_v = dxo_ref[...]
        dxh = dxo_v * g_ref[...]
        dz = rstd * (dxh - jnp.mean(dxh, axis=-1, keepdims=True) - xhat * jnp.mean(dxh * xhat, axis=-1, keepdims=True))
        dxr_ref[...] = ALPHA * dz
        dyy_ref[...] = (gate_ref[...] * dz).astype(dyy_ref.dtype)
        a_g[...] += _fold8(dxo_v * xhat)
        a_b[...] += _fold8(dxo_v)
        a_gate[...] += _fold8(dz * yv)

        @pl.when(i == pl.num_programs(0) - 1)
        def _():
            red_ref[...] = jnp.zeros_like(red_ref)
            red_ref[0:1, :] = jnp.sum(a_g[...], axis=0, keepdims=True)
            red_ref[1:2, :] = jnp.sum(a_b[...], axis=0, keepdims=True)
            red_ref[2:3, :] = jnp.sum(a_gate[...], axis=0, keepdims=True)

    return _rows(body, s, ROW_TILE, [("blk", dxo), ("blk", x), ("blk", y), ("all", gate), ("all", g)],
                 [("blk", (s, d), F32), ("blk", (s, d), MXU_DTYPE), ("all", (SUBLANES, d), F32)], name,
                 scratch=[pltpu.VMEM((SUBLANES, d), F32)] * 3)


def _mod_bwd(dxr, dhs, x, scale, name):
    s, d = x.shape
    n_dh = len(dhs)

    def body(dxr_ref, *rest):
        dh_refs = rest[:n_dh]
        x_ref, sc_ref, dx_ref, red_ref, a_sh, a_sc = rest[n_dh:]
        i = pl.program_id(0)

        @pl.when(i == 0)
        def _():
            a_sh[...] = jnp.zeros_like(a_sh)
            a_sc[...] = jnp.zeros_like(a_sc)

        dh = dh_refs[0][...]
        for r in dh_refs[1:]:
            dh = dh + r[...]
        dx_ref[...] = dxr_ref[...] + dh * (1.0 + sc_ref[...])
        a_sh[...] += _fold8(dh)
        a_sc[...] += _fold8(dh * x_ref[...])

        @pl.when(i == pl.num_programs(0) - 1)
        def _():
            red_ref[...] = jnp.zeros_like(red_ref)
            red_ref[0:1, :] = jnp.sum(a_sh[...], axis=0, keepdims=True)
            red_ref[1:2, :] = jnp.sum(a_sc[...], axis=0, keepdims=True)

    return _rows(body, s, ROW_TILE, [("blk", dxr)] + [("blk", h) for h in dhs] + [("blk", x), ("all", scale)],
                 [("blk", (s, d), F32), ("all", (SUBLANES, d), F32)], name,
                 scratch=[pltpu.VMEM((SUBLANES, d), F32)] * 2)


def _left_half(shape):
    return lax.broadcasted_iota(jnp.int32, shape, 1) < (LANES // 2)


def _spatial_z(vn, wc_ref, bias_ref, j):
    vb = vn[:, j * LANES:(j + 1) * LANES]
    z0 = _dot_nn(wc_ref[2 * j], vb)
    z1 = _dot_nn(wc_ref[2 * j + 1], vb)
    return jnp.where(_left_half(z0.shape), z0, z1) + bias_ref[:, j * LANES:(j + 1) * LANES]


def _spatial_fwd(uvpre, vn_g, vn_b, wc, bias_full, name):
    s, d2 = uvpre.shape
    d = d2 // 2

    def body(uv_ref, g_ref, b_ref, wc_ref, bias_ref, out_ref):
        u = _gelu(uv_ref[:, :d])
        v = _gelu(uv_ref[:, d:])
        vh, _ = _ln_stats(v)
        vn = vh * g_ref[...] + b_ref[...]
        for j in range(d // LANES):
            z = _spatial_z(vn, wc_ref, bias_ref, j)
            out_ref[:, j * LANES:(j + 1) * LANES] = (u[:, j * LANES:(j + 1) * LANES] * z).astype(out_ref.dtype)

    return _rows(body, s, CHUNK, [("blk", uvpre), ("all", vn_g), ("all", vn_b), ("all", wc), ("all", bias_full)],
                 [("blk", (s, d), MXU_DTYPE)], name)[0]


def _spatial_bwd(uvpre, dgated, vn_g, vn_b, wc, wct, bias_full, name):
    s, d2 = uvpre.shape
    d = d2 // 2

    def body(uv_ref, dg_ref, g_ref, b_ref, wc_ref, wct_ref, bias_ref,
             duv_ref, dws_ref, dbias_ref, dbin_ref, dvg_ref, dvb_ref, dvn_buf, a_bin, a_vg, a_vb):
        i = pl.program_id(0)

        @pl.when(i == 0)
        def _():
            dws_ref[...] = jnp.zeros_like(dws_ref)
            dbias_ref[...] = jnp.zeros_like(dbias_ref)
            a_bin[...] = jnp.zeros_like(a_bin)
            a_vg[...] = jnp.zeros_like(a_vg)
            a_vb[...] = jnp.zeros_like(a_vb)

        up = uv_ref[:, :d]
        vp = uv_ref[:, d:]
        u = _gelu(up)
        v = _gelu(vp)
        vh, rstd = _ln_stats(v)
        vn = vh * g_ref[...] + b_ref[...]
        dg = dg_ref[...]
        dzz = dg * u
        dbias_ref[...] += dzz
        for j in range(d // LANES):
            cols = slice(j * LANES, (j + 1) * LANES)
            z = _spatial_z(vn, wc_ref, bias_ref, j)
            dup = dg[:, cols] * z * _gelu_grad(up[:, cols])
            duv_ref[:, cols] = dup.astype(duv_ref.dtype)
            a_bin[:, cols] += _fold8(dup)
            dzb = dzz[:, cols]
            left = _left_half(dzb.shape)
            dvn_buf[:, cols] = jnp.where(left, _dot_nn(wct_ref[2 * j], dzb), _dot_nn(wct_ref[2 * j + 1], dzb))
            vb = vn[:, cols]
            dws_ref[2 * j] += _dot_nt(jnp.where(left, dzb, 0.0), vb)
            dws_ref[2 * j + 1] += _dot_nt(jnp.where(left, 0.0, dzb), vb)
        dvn = dvn_buf[...]
        a_vg[...] += _fold8(dvn * vh)
        a_vb[...] += _fold8(dvn)
        dvh = dvn * g_ref[...]
        dv = rstd * (dvh - jnp.mean(dvh, axis=-1, keepdims=True) - vh * jnp.mean(dvh * vh, axis=-1, keepdims=True))
        dvp = dv * _gelu_grad(vp)
        duv_ref[:, d:] = dvp.astype(duv_ref.dtype)
        a_bin[:, d:] += _fold8(dvp)

        @pl.when(i == pl.num_programs(0) - 1)
        def _():
            dbin_ref[...] = jnp.sum(a_bin[...], axis=0, keepdims=True)
            dvg_ref[...] = jnp.sum(a_vg[...], axis=0, keepdims=True)
            dvb_ref[...] = jnp.sum(a_vb[...], axis=0, keepdims=True)

    return _rows(body, s, CHUNK,
                 [("blk", uvpre), ("blk", dgated), ("all", vn_g), ("all", vn_b), ("all", wc), ("all", wct), ("all", bias_full)],
                 [("blk", (s, d2), MXU_DTYPE), ("all", (A_GROUPS, CHUNK, CHUNK), F32), ("all", (CHUNK, d), F32),
                  ("all", (1, d2), F32), ("all", (1, d), F32), ("all", (1, d), F32)], name,
                 scratch=[pltpu.VMEM((CHUNK, d), F32), pltpu.VMEM((SUBLANES, d2), F32),
                          pltpu.VMEM((SUBLANES, d), F32), pltpu.VMEM((SUBLANES, d), F32)])


def _head_mask(v, h):
    lane = lax.broadcasted_iota(jnp.int32, v.shape, 1)
    return jnp.where((lane >= h * HEAD_DIM) & (lane < (h + 1) * HEAD_DIM), v, jnp.zeros_like(v))


def _att_scores(qm, k, slope, dil, kind, on):
    s = _dot_nt(qm, k) * ATT_SCALE
    qi = lax.broadcasted_iota(jnp.int32, s.shape, 0)
    ki = lax.broadcasted_iota(jnp.int32, s.shape, 1)
    if kind == "cur":
        diff, valid = qi - ki, ki <= qi
    else:
        diff, valid = SPAN + qi - ki, ki >= qi
    s = s - slope * (float(dil) * diff.astype(F32))
    if on is not None:
        valid = valid & on
    return jnp.where(valid, s, NEG)


def _att_specs(s, dil, kinds):
    nb = s // (dil * SPAN)

    def rowblk(which, b):
        if which == "prev":
            return jnp.where(b % nb == 0, b, b - 1)
        if which == "next":
            return jnp.where(b % nb == nb - 1, b, b + 1)
        return b

    return [pl.BlockSpec((SPAN, LANES), functools.partial(lambda b, hp, o, w: (rowblk(w, b), o + hp), o=off, w=which))
            for off, which in kinds]


def _attn_fwd(qkv, slopes, dil, name):
    s, d3 = qkv.shape
    d = d3 // 3
    cb = d // LANES
    nb = s // (dil * SPAN)

    def body(q_ref, kc_ref, kp_ref, vc_ref, vp_ref, sl_ref, o_ref, l_ref):
        b, hp = pl.program_id(0), pl.program_id(1)
        has_prev = (b % nb) != 0
        q = q_ref[...]
        o_h, l_h = [], []
        for h in range(2):
            slope = sl_ref[pl.ds(2 * hp + h, 1), :]
            qm = _head_mask(q, h)
            s_c = _att_scores(qm, kc_ref[...], slope, dil, "cur", None)
            s_p = _att_scores(qm, kp_ref[...], slope, dil, "prev", has_prev)
            m = jnp.maximum(jnp.max(s_c, axis=-1, keepdims=True), jnp.max(s_p, axis=-1, keepdims=True))
            p_c = jnp.exp(s_c - m)
            p_p = jnp.exp(s_p - m)
            l = jnp.sum(p_c, axis=-1, keepdims=True) + jnp.sum(p_p, axis=-1, keepdims=True)
            o_h.append((_dot_nn(p_c, vc_ref[...]) + _dot_nn(p_p, vp_ref[...])) / l)
            l_h.append(jnp.broadcast_to(m + jnp.log(l), (SPAN, LANES)))
        left = _left_half((SPAN, LANES))
        o_ref[...] = jnp.where(left, o_h[0], o_h[1])
        l_ref[...] = jnp.where(left, l_h[0], l_h[1])

    specs = _att_specs(s, dil, [(0, "cur"), (cb, "cur"), (cb, "prev"), (2 * cb, "cur"), (2 * cb, "prev")])
    out_spec = pl.BlockSpec((SPAN, LANES), lambda b, hp: (b, hp))
    return pl.pallas_call(
        body,
        grid=(s // SPAN, cb),
        in_specs=specs + [pl.BlockSpec(slopes.shape, lambda b, hp: (0, 0))],
        out_specs=[out_spec, out_spec],
        out_shape=[jax.ShapeDtypeStruct((s, d), F32)] * 2,
        name=name,
        compiler_params=_cparams(("parallel", "parallel")),
    )(qkv, qkv, qkv, qkv, qkv, slopes)


def _lane_col(v, h):
    return v[:, h * HEAD_DIM:h * HEAD_DIM + 1]


def _attn_dq(qkv, do, lse, dd, slopes, dil, name):
    s, d3 = qkv.shape
    d = d3 // 3
    cb = d // LANES
    nb = s // (dil * SPAN)

    def body(q_ref, kc_ref, kp_ref, vc_ref, vp_ref, do_ref, l_ref, dd_ref, sl_ref, dq_ref):
        b, hp = pl.program_id(0), pl.program_id(1)
        has_prev = (b % nb) != 0
        q, do_v = q_ref[...], do_ref[...]
        outs = []
        for h in range(2):
            slope = sl_ref[pl.ds(2 * hp + h, 1), :]
            qm = _head_mask(q, h)
            dom = _head_mask(do_v, h)
            lse_h = _lane_col(l_ref[...], h)
            dd_h = _lane_col(dd_ref[...], h)
            acc = jnp.zeros((SPAN, LANES), F32)
            for kind, k_ref, v_ref, on in (("cur", kc_ref, vc_ref, None), ("prev", kp_ref, vp_ref, has_prev)):
                sc = _att_scores(qm, k_ref[...], slope, dil, kind, on)
                p = jnp.exp(sc - lse_h)
                ds = p * (_dot_nt(dom, v_ref[...]) - dd_h)
                acc = acc + _dot_nn(ds, k_ref[...])
            outs.append(acc * ATT_SCALE)
        dq_ref[...] = jnp.where(_left_half((SPAN, LANES)), outs[0], outs[1]).astype(dq_ref.dtype)

    specs = _att_specs(s, dil, [(0, "cur"), (cb, "cur"), (cb, "prev"), (2 * cb, "cur"), (2 * cb, "prev")])
    blk = pl.BlockSpec((SPAN, LANES), lambda b, hp: (b, hp))
    return pl.pallas_call(
        body,
        grid=(s // SPAN, cb),
        in_specs=specs + [blk, blk, blk, pl.BlockSpec(slopes.shape, lambda b, hp: (0, 0))],
        out_specs=blk,
        out_shape=jax.ShapeDtypeStruct((s, d), MXU_DTYPE),
        name=name,
        compiler_params=_cparams(("parallel", "parallel")),
    )(qkv, qkv, qkv, qkv, qkv, do, lse, dd, slopes)


def _attn_dkv(qkv, do, lse, dd, slopes, dil, name):
    s, d3 = qkv.shape
    d = d3 // 3
    cb = d // LANES
    nb = s // (dil * SPAN)

    def body(k_ref, v_ref, qc_ref, qn_ref, doc_ref, don_ref, lc_ref, ln_ref, ddc_ref, ddn_ref, sl_ref, dk_ref, dv_ref):
        b, hp = pl.program_id(0), pl.program_id(1)
        has_next = (b % nb) != nb - 1
        dk_h, dv_h = [], []
        for h in range(2):
            slope = sl_ref[pl.ds(2 * hp + h, 1), :]
            km = _head_mask(k_ref[...], h)
            vm = _head_mask(v_ref[...], h)
            dk = jnp.zeros((SPAN, LANES), F32)
            dv = jnp.zeros((SPAN, LANES), F32)
            for kind, q_ref, do_ref, l_ref, dd_ref, on in (("cur", qc_ref, doc_ref, lc_ref, ddc_ref, None),
                                                            ("prev", qn_ref, don_ref, ln_ref, ddn_ref, has_next)):
                sc = _att_scores(q_ref[...], km, slope, dil, kind, on)
                p = jnp.exp(sc - _lane_col(l_ref[...], h))
                ds = p * (_dot_nt(do_ref[...], vm) - _lane_col(dd_ref[...], h))
                dv = dv + _dot_tn(p, do_ref[...])
                dk = dk + _dot_tn(ds, q_ref[...])
            dk_h.append(dk * ATT_SCALE)
            dv_h.append(dv)
        left = _left_half((SPAN, LANES))
        dk_ref[...] = jnp.where(left, dk_h[0], dk_h[1]).astype(dk_ref.dtype)
        dv_ref[...] = jnp.where(left, dv_h[0], dv_h[1]).astype(dv_ref.dtype)

    qkv_specs = _att_specs(s, dil, [(cb, "cur"), (2 * cb, "cur"), (0, "cur"), (0, "next")])
    pair = _att_specs(s, dil, [(0, "cur"), (0, "next")])
    blk = pl.BlockSpec((SPAN, LANES), lambda b, hp: (b, hp))
    return pl.pallas_call(
        body,
        grid=(s // SPAN, cb),
        in_specs=qkv_specs + pair + pair + pair + [pl.BlockSpec(slopes.shape, lambda b, hp: (0, 0))],
        out_specs=[blk, blk],
        out_shape=[jax.ShapeDtypeStruct((s, d), MXU_DTYPE)] * 2,
        name=name,
        compiler_params=_cparams(("parallel", "parallel")),
    )(qkv, qkv, qkv, qkv, do, do, lse, lse, dd, dd, slopes)


def _mix_weights(l_refs):
    ls = [r[...] for r in l_refs]
    m = functools.reduce(jnp.maximum, ls)
    es = [jnp.exp(l - m) for l in ls]
    tot = functools.reduce(lambda a, c: a + c, es)
    return [e / tot for e in es]


def _combine_fwd(os_, ls_, name):
    s, d = os_[0].shape
    n = len(os_)

    def body(*refs):
        o_refs, l_refs, out_ref = refs[:n], refs[n:2 * n], refs[2 * n]
        ws = _mix_weights(l_refs)
        acc = ws[0] * o_refs[0][...]
        for w, o in zip(ws[1:], o_refs[1:]):
            acc = acc + w * o[...]
        out_ref[...] = acc

    return _rows(body, s, ROW_TILE, [("blk", a) for a in os_ + ls_], [("blk", (s, d), F32)], name)[0]


def _combine_bwd(do, o, ls_, name):
    s, d = o.shape
    n = len(ls_)
    ri = lax.broadcasted_iota(jnp.int32, (LANES, LANES), 0) // HEAD_DIM
    ci = lax.broadcasted_iota(jnp.int32, (LANES, LANES), 1) // HEAD_DIM
    seg = (ri == ci).astype(F32)

    def body(do_ref, o_ref, *rest):
        l_refs, seg_ref, outs = rest[:n], rest[n], rest[n + 1:]
        ws = _mix_weights(l_refs)
        dov = do_ref[...]
        prod = dov * o_ref[...]
        for j in range(d // LANES):
            cols = slice(j * LANES, (j + 1) * LANES)
            r = jnp.dot(prod[:, cols], seg_ref[...], precision=lax.Precision.HIGHEST, preferred_element_type=F32)
            for g in range(n):
                outs[2 * g][:, cols] = (ws[g][:, cols] * dov[:, cols]).astype(outs[2 * g].dtype)
                outs[2 * g + 1][:, cols] = ws[g][:, cols] * r

    outs = []
    for _ in range(n):
        outs += [("blk", (s, d), MXU_DTYPE), ("blk", (s, d), F32)]
    res = _rows(body, s, ROW_TILE, [("blk", do), ("blk", o)] + [("blk", l) for l in ls_] + [("all", seg)], outs, name)
    return [(res[2 * g], res[2 * g + 1]) for g in range(n)]


def _ada_fwd(c_all, w, b, name):
    nsub, d, cs = w.shape

    def body(c_ref, w_ref, b_ref, o_ref):
        cv = c_ref[...]
        sc = cv * (1.0 / (1.0 + jnp.exp(-cv)))
        o_ref[...] = _dot_nn(sc, w_ref[...]) + b_ref[...]

    return pl.pallas_call(
        body,
        grid=(nsub,),
        in_specs=[pl.BlockSpec(c_all.shape, lambda i: (0, 0)), pl.BlockSpec((None, d, cs), lambda i: (i, 0, 0)),
                  pl.BlockSpec((None, 1, cs), lambda i: (i, 0, 0))],
        out_specs=pl.BlockSpec((None, N_DEV, cs), lambda i: (i, 0, 0)),
        out_shape=jax.ShapeDtypeStruct((nsub, N_DEV, cs), F32),
        name=name,
        compiler_params=_cparams(("parallel",)),
    )(c_all, w, b)


def _ada_bwd(c_all_t, dm, name):
    d, nb = c_all_t.shape
    nsub, _, cs = dm.shape

    def body(c_ref, dm_ref, o_ref):
        cv = c_ref[...]
        sc = cv * (1.0 / (1.0 + jnp.exp(-cv)))
        acc = sc[:, 0:1] * dm_ref[0:1, :]
        for bi in range(1, nb):
            acc = acc + sc[:, bi:bi + 1] * dm_ref[bi:bi + 1, :]
        o_ref[...] = acc

    return pl.pallas_call(
        body,
        grid=(nsub,),
        in_specs=[pl.BlockSpec(c_all_t.shape, lambda i: (0, 0)), pl.BlockSpec((None, nb, cs), lambda i: (i, 0, 0))],
        out_specs=pl.BlockSpec((None, d, cs), lambda i: (i, 0, 0)),
        out_shape=jax.ShapeDtypeStruct((nsub, d, cs), F32),
        name=name,
        compiler_params=_cparams(("parallel",)),
    )(c_all_t, dm)


def _row_tile(r, row_elems):
    t = 2 * SUBLANES
    if r % t:
        return r
    while t * 2 * row_elems <= 256 * 1024 and r % (t * 2) == 0:
        t *= 2
    return t


def _adamw(w, g, m, v, name):
    shape = w.shape
    c = shape[-1]
    r = w.size // c
    tr = _row_tile(r, c)
    w2, g2, m2, v2 = [a.reshape(r, c) for a in (w, g, m, v)]
    bc1 = 1.0 - ADAM_B1 ** ADAM_STEP
    bc2 = 1.0 - ADAM_B2 ** ADAM_STEP

    def body(w_ref, g_ref, m_ref, v_ref, d_ref, nm_ref, nv_ref):
        gv = g_ref[...]
        nm = ADAM_B1 * m_ref[...] + (1.0 - ADAM_B1) * gv
        nv = ADAM_B2 * v_ref[...] + (1.0 - ADAM_B2) * (gv * gv)
        d_ref[...] = -ADAM_LR * ((nm / bc1) / (jnp.sqrt(nv / bc2) + ADAM_EPS) + ADAM_WD * w_ref[...])
        nm_ref[...] = nm
        nv_ref[...] = nv

    res = _rows(body, r, tr, [("blk", a) for a in (w2, g2, m2, v2)], [("blk", (r, c), F32)] * 3, name)
    return [a.reshape(shape) for a in res]


def _sum_slots(buf, name):
    n, r, c = buf.shape
    tr = _row_tile(r, n * c)

    def body(b_ref, o_ref):
        acc = b_ref[0].astype(F32)
        for k in range(1, n):
            acc = acc + b_ref[k].astype(F32)
        o_ref[...] = acc

    return pl.pallas_call(
        body,
        grid=(r // tr,),
        in_specs=[pl.BlockSpec((n, tr, c), lambda i: (0, i, 0))],
        out_specs=pl.BlockSpec((tr, c), lambda i: (i, 0)),
        out_shape=jax.ShapeDtypeStruct((r, c), F32),
        name=name,
        compiler_params=_cparams(("parallel",)),
    )(buf)


def _me():
    return lax.axis_index("x"), lax.axis_index("y"), lax.axis_index("c")


def _all_gather_small(blk, name):
    m_per, n = blk.shape

    def body(x_ref, out_ref, send_sems, recv_sems, local_sem):
        x, y, c = _me()
        me, sibling = (x, y, c), (x, y, 1 - c)
        chips = [(1 - x, y), (x, 1 - y), (1 - x, 1 - y)]

        def rows(px, py, pc):
            return out_ref.at[pl.ds((4 * px + 2 * py + pc) * m_per, m_per), :]

        def copy(k, block, to, src=None):
            return pltpu.make_async_remote_copy(
                src_ref=rows(*block) if src is None else src, dst_ref=rows(*block),
                send_sem=send_sems.at[k], recv_sem=recv_sems.at[k], device_id=to, device_id_type=MESH)

        mine = pltpu.make_async_copy(x_ref, rows(*me), local_sem)
        mine.start()
        first = [copy(0, me, sibling, src=x_ref)]
        first += [copy(1 + j, me, (*chip, c), src=x_ref) for j, chip in enumerate(chips)]
        for cp in first:
            cp.start()
        passed = [copy(4 + j, (*chip, c), sibling) for j, chip in enumerate(chips)]
        for j, chip in enumerate(chips):
            copy(1 + j, (*chip, c), me).wait_recv()
            passed[j].start()
        copy(0, sibling, me).wait_recv()
        for j, chip in enumerate(chips):
            copy(4 + j, (*chip, 1 - c), me).wait_recv()
        for cp in first + passed:
            cp.wait_send()
        mine.wait()

    return pl.pallas_call(
        body,
        out_shape=jax.ShapeDtypeStruct((N_DEV * m_per, n), blk.dtype),
        in_specs=[pl.BlockSpec(memory_space=pltpu.VMEM)],
        out_specs=pl.BlockSpec(memory_space=pltpu.VMEM),
        scratch_shapes=[pltpu.SemaphoreType.DMA((7,)), pltpu.SemaphoreType.DMA((7,)), pltpu.SemaphoreType.DMA],
        name=name,
        compiler_params=pltpu.CompilerParams(vmem_limit_bytes=VMEM_LIMIT),
    )(blk)


def _gather_weights(shards, name):
    n = len(shards)

    def body(*refs):
        in_refs, out_refs = refs[:n], refs[n:2 * n]
        send_sems, recv_sems, local_sems = refs[2 * n:]
        x, y, c = _me()
        q = 2 * x + y
        chips = [(1 - x, y), (x, 1 - y), (1 - x, 1 - y)]
        local, remote = [], []
        for w in range(n):
            cp = pltpu.make_async_copy(in_refs[w], out_refs[w].at[q], local_sems.at[w])
            cp.start()
            local.append(cp)
        for w in range(n):
            for j, chip in enumerate(chips):
                cp = pltpu.make_async_remote_copy(
                    src_ref=in_refs[w], dst_ref=out_refs[w].at[q],
                    send_sem=send_sems.at[3 * w + j], recv_sem=recv_sems.at[3 * w + j],
                    device_id=(*chip, c), device_id_type=MESH)
                cp.start()
                remote.append(cp)
        for cp in remote:
            cp.wait_recv()
        for cp in remote:
            cp.wait_send()
        for cp in local:
            cp.wait()

    hbm = pl.BlockSpec(memory_space=pltpu.HBM)
    return pl.pallas_call(
        body,
        out_shape=[jax.ShapeDtypeStruct((N_CHIPS,) + s.shape, s.dtype) for s in shards],
        in_specs=[hbm] * n,
        out_specs=[hbm] * n,
        scratch_shapes=[pltpu.SemaphoreType.DMA((3 * n,)), pltpu.SemaphoreType.DMA((3 * n,)), pltpu.SemaphoreType.DMA((n,))],
        name=name,
    )(*shards)


def _scatter_partials(grads, kinds, name):
    n = len(grads)
    pieces = []
    for gr, kind in zip(grads, kinds):
        k, nn = gr.shape
        pieces.append((k // 2, nn // N_CHIPS) if kind == "col" else (k // N_CHIPS // 2, nn))

    def body(*refs):
        in_refs, out_refs = refs[:n], refs[n:2 * n]
        send_sems, recv_sems = refs[2 * n:]
        x, y, c = _me()
        slot = 4 * x + 2 * y + c
        started = []
        for w in range(n):
            pr, pc = pieces[w]
            for r in range(N_DEV):
                fx, fy, fc = (r >> 2) & 1, (r >> 1) & 1, r & 1
                tx, ty, tc = (x + fx) % 2, (y + fy) % 2, (c + fc) % 2
                tq = 2 * tx + ty
                if kinds[w] == "col":
                    src = in_refs[w].at[pl.ds(tc * pr, pr), pl.ds(tq * pc, pc)]
                else:
                    src = in_refs[w].at[pl.ds((2 * tq + tc) * pr, pr), :]
                dst = out_refs[w].at[slot]
                if r == 0:
                    cp = pltpu.make_async_copy(src, dst, recv_sems.at[N_DEV * w])
                else:
                    cp = pltpu.make_async_remote_copy(
                        src_ref=src, dst_ref=dst, send_sem=send_sems.at[N_DEV * w + r], recv_sem=recv_sems.at[N_DEV * w + r],
                        device_id=(tx, ty, tc), device_id_type=MESH)
                cp.start()
                started.append((r, cp))
        for r, cp in started:
            if r == 0:
                cp.wait()
            else:
                cp.wait_recv()
        for r, cp in started:
            if r != 0:
                cp.wait_send()

    hbm = pl.BlockSpec(memory_space=pltpu.HBM)
    return pl.pallas_call(
        body,
        out_shape=[jax.ShapeDtypeStruct((N_DEV,) + p, g.dtype) for p, g in zip(pieces, grads)],
        in_specs=[hbm] * n,
        out_specs=[hbm] * n,
        scratch_shapes=[pltpu.SemaphoreType.DMA((N_DEV * n,)), pltpu.SemaphoreType.DMA((N_DEV * n,))],
        name=name,
    )(*grads)


def _swap_halves(halves, name):
    n = len(halves)

    def body(*refs):
        in_refs, out_refs = refs[:n], refs[n:2 * n]
        send_sems, recv_sems, local_sems = refs[2 * n:]
        x, y, c = _me()
        cps = []
        for w in range(n):
            lc = pltpu.make_async_copy(in_refs[w], out_refs[w].at[c], local_sems.at[w])
            lc.start()
            rc = pltpu.make_async_remote_copy(
                src_ref=in_refs[w], dst_ref=out_refs[w].at[c], send_sem=send_sems.at[w], recv_sem=recv_sems.at[w],
                device_id=(x, y, 1 - c), device_id_type=MESH)
            rc.start()
            cps.append((lc, rc))
        for lc, rc in cps:
            rc.wait_recv()
        for lc, rc in cps:
            rc.wait_send()
            lc.wait()

    hbm = pl.BlockSpec(memory_space=pltpu.HBM)
    return pl.pallas_call(
        body,
        out_shape=[jax.ShapeDtypeStruct((2,) + h.shape, h.dtype) for h in halves],
        in_specs=[hbm] * n,
        out_specs=[hbm] * n,
        scratch_shapes=[pltpu.SemaphoreType.DMA((n,)), pltpu.SemaphoreType.DMA((n,)), pltpu.SemaphoreType.DMA((n,))],
        name=name,
    )(*halves)


def _to_streams(a, dil):
    if dil == 1:
        return a
    s, c = a.shape
    return a.reshape(s // dil, dil, c).transpose(1, 0, 2).reshape(s, c)


def _from_streams(a, dil):
    if dil == 1:
        return a
    s, c = a.shape
    return a.reshape(dil, s // dil, c).transpose(1, 0, 2).reshape(s, c)


def _mm_tiles(s):
    return min(s, 1024)


def _local_step(x0, target, mvec, ln_g, ln_b, small, big):
    s, d = x0.shape
    tm = _mm_tiles(s)
    row = lambda v: v.reshape(1, -1)
    shift = [row(mvec[i, :d]) for i in range(4)]
    scale = [row(mvec[i, d:2 * d]) for i in range(4)]
    gate = [row(1.0 + mvec[i, 2 * d:]) for i in range(4)]
    lg = [row(ln_g[i]) for i in range(4)]
    lb = [row(ln_b[i]) for i in range(4)]
    mm = functools.partial(_mm, tm=tm)
    mm_w = functools.partial(_mm, tm=1024, tk=min(s, 512), mode="tn")

    xs, ys = [x0], []
    h0 = _mod(x0, scale[0], shift[0], "mod0")
    uvpre = mm(h0, big["a_w_in"], mode="nn", name="a_in", outs=[F32], tn=512, tk=512,
               epi=lambda r, bias: [r + bias], extras=[("row", small["a_b_in"])])
    gated = _spatial_fwd(uvpre, small["a_vn_g"], small["a_vn_b"], small["wc"], small["bias_full"], "a_spatial")
    ys.append(mm(gated, big["a_w_out"], mode="nn", name="a_out", outs=[F32], tn=1024, tk=512))
    x1, h1 = _resid_ln(xs[0], ys[0], gate[0], lg[0], lb[0], (scale[1], shift[1]), "ln0")
    xs.append(x1)
    relu2 = lambda r: [r, jnp.square(jnp.maximum(r, 0.0))]
    a0, r0 = mm(h1, big["up0"], mode="nn", name="up0", outs=[F32, MXU_DTYPE], tn=1024, tk=512, epi=relu2)
    ys.append(mm(r0, big["down0"], mode="nn", name="down0", outs=[F32], tn=1024, tk=512))
    x2, h2 = _resid_ln(xs[1], ys[1], gate[1], lg[1], lb[1], (scale[2], shift[2]), "ln1")
    xs.append(x2)
    hg, qkvs, o_g, l_g = [], [], [], []
    for g, (_, dil) in enumerate(B_PATTERNS):
        hp = _to_streams(h2, dil)
        qkv = mm(hp, big["b_w_qkv"], mode="nn", name=f"qkv{g}", outs=[MXU_DTYPE], tn=768, tk=512, b_col0=g * 3 * d, n_out=3 * d)
        og, lgv = _attn_fwd(qkv, small["slopes"], dil, f"attn_fwd{g}")
        hg.append(hp)
        qkvs.append(qkv)
        o_g.append(_from_streams(og, dil))
        l_g.append(_from_streams(lgv, dil))
    o_mix = _combine_fwd(o_g, l_g, "combine")
    ys.append(mm(o_mix, big["b_w_out"], mode="nn", name="b_out", outs=[F32], tn=1024, tk=512))
    x3, h3 = _resid_ln(xs[2], ys[2], gate[2], lg[2], lb[2], (scale[3], shift[3]), "ln2")
    xs.append(x3)
    a1, r1 = mm(h3, big["up1"], mode="nn", name="up1", outs=[F32, MXU_DTYPE], tn=1024, tk=512, epi=relu2)
    ys.append(mm(r1, big["down1"], mode="nn", name="down1", outs=[F32], tn=1024, tk=512))
    x4, _ = _resid_ln(xs[3], ys[3], gate[3], lg[3], lb[3], None, "ln3")

    gb, dm, dlg, dlb = {}, [None] * 4, [None] * 4, [None] * 4
    dx, loss = _loss_grad(x4, target, "loss")

    def mlp_bwd(i, sub, dx, h, a, r):
        dxr, dyy, red = _ln_bwd(dx, xs[sub], ys[sub], gate[sub], lg[sub], f"ln_bwd{sub}")
        gb[f"down{i}"] = mm_w(r, dyy, name=f"g_down{i}", outs=[MXU_DTYPE], tn=1024)
        da = mm(dyy, big[f"down{i}"], mode="nt", name=f"d_down{i}", outs=[MXU_DTYPE], tn=1024, tk=512,
                epi=lambda acc, av: [acc * (2.0 * jnp.maximum(av, 0.0))], extras=[("full", a)])
        gb[f"up{i}"] = mm_w(h, da, name=f"g_up{i}", outs=[MXU_DTYPE], tn=1024)
        dh = mm(da, big[f"up{i}"], mode="nt", name=f"d_up{i}", outs=[F32], tn=1024, tk=512)
        dx, red2 = _mod_bwd(dxr, [dh], xs[sub], scale[sub], f"mod_bwd{sub}")
        dm[sub] = jnp.concatenate([red2[0], red2[1], red[2]])
        dlg[sub], dlb[sub] = red[0], red[1]
        return dx

    dx = mlp_bwd(1, 3, dx, h3, a1, r1)
    dxr, dyy, red = _ln_bwd(dx, xs[2], ys[2], gate[2], lg[2], "ln_bwd2")
    gb["b_w_out"] = mm_w(o_mix, dyy, name="g_b_out", outs=[MXU_DTYPE], tn=1024)
    do = mm(dyy, big["b_w_out"], mode="nt", name="d_b_out", outs=[F32], tn=1024, tk=512)
    parts = _combine_bwd(do, o_mix, l_g, "combine_bwd")
    dhs, gq = [], []
    for g, (_, dil) in enumerate(B_PATTERNS):
        do_g, dd_g = _to_streams(parts[g][0], dil), _to_streams(parts[g][1], dil)
        lse_g = _to_streams(l_g[g], dil)
        dq = _attn_dq(qkvs[g], do_g, lse_g, dd_g, small["slopes"], dil, f"attn_dq{g}")
        dk, dv = _attn_dkv(qkvs[g], do_g, lse_g, dd_g, small["slopes"], dil, f"attn_dkv{g}")
        dqkv = jnp.concatenate([dq, dk, dv], axis=1)
        gq.append(mm_w(hg[g], dqkv, name=f"g_qkv{g}", outs=[MXU_DTYPE], tn=1024))
        dh = mm(dqkv, big["b_w_qkv"], mode="nt", name=f"d_qkv{g}", outs=[F32], tn=1024, tk=768, b_col0=g * 3 * d)
        dhs.append(_from_streams(dh, dil))
    gb["b_w_qkv"] = jnp.concatenate(gq, axis=1)
    dx, red2 = _mod_bwd(dxr, dhs, xs[2], scale[2], "mod_bwd2")
    dm[2] = jnp.concatenate([red2[0], red2[1], red[2]])
    dlg[2], dlb[2] = red[0], red[1]
    dx = mlp_bwd(0, 1, dx, h1, a0, r0)
    dxr, dyy, red = _ln_bwd(dx, xs[0], ys[0], gate[0], lg[0], "ln_bwd0")
    gb["a_w_out"] = mm_w(gated, dyy, name="g_a_out", outs=[MXU_DTYPE], tn=1024)
    dgated = mm(dyy, big["a_w_out"], mode="nt", name="d_a_out", outs=[F32], tn=1024, tk=512)
    duv, dws, dbias, dbin, dvg, dvb = _spatial_bwd(uvpre, dgated, small["a_vn_g"], small["a_vn_b"], small["wc"],
                                                   small["wct"], small["bias_full"], "a_spatial_bwd")
    gb["a_w_in"] = mm_w(h0, duv, name="g_a_in", outs=[MXU_DTYPE], tn=1024)
    dh = mm(duv, big["a_w_in"], mode="nt", name="d_a_in", outs=[F32], tn=1024, tk=512)
    dx, red2 = _mod_bwd(dxr, [dh], xs[0], scale[0], "mod_bwd0")
    dm[0] = jnp.concatenate([red2[0], red2[1], red[2]])
    dlg[0], dlb[0] = red[0], red[1]

    tril = jnp.tril(jnp.ones((CHUNK, CHUNK), bool))
    gsmall = {
        "a_b_in": dbin.reshape(-1), "a_vn_g": dvg.reshape(-1), "a_vn_b": dvb.reshape(-1),
        "a_w_s": jnp.where(tril, dws, 0.0).reshape(-1),
        "a_b_s": dbias.reshape(CHUNK, A_GROUPS, d // A_GROUPS).sum(-1).T.reshape(-1),
    }
    return loss, dx, gb, jnp.stack(dm), jnp.stack(dlg), jnp.stack(dlb), gsmall


BIG = ("a_w_in", "a_w_out", "b_w_qkv", "b_w_out", "up0", "up1", "down0", "down1")
BIG_KIND = {"a_w_in": "col", "a_w_out": "row", "b_w_qkv": "col", "b_w_out": "row",
            "up0": "col", "up1": "col", "down0": "row", "down1": "row"}
SMALL = ("a_b_in", "a_vn_g", "a_vn_b", "a_b_s", "a_w_s")


def kernel(x, c, ada_w, ada_b, ln_g, ln_b, a_w_in, a_b_in, a_vn_g, a_vn_b, a_w_s, a_b_s, a_w_out, b_w_qkv, b_w_out, mlp_w_up, mlp_w_down, loss_target, m_ada_w, m_ada_b, m_ln_g, m_ln_b, m_a_w_in, m_a_b_in, m_a_vn_g, m_a_vn_b, m_a_w_s, m_a_b_s, m_a_w_out, m_b_w_qkv, m_b_w_out, m_mlp_w_up, m_mlp_w_down, v_ada_w, v_ada_b, v_ln_g, v_ln_b, v_a_w_in, v_a_b_in, v_a_vn_g, v_a_vn_b, v_a_w_s, v_a_b_s, v_a_w_out, v_b_w_qkv, v_b_w_out, v_mlp_w_up, v_mlp_w_down):
    s, d = x.shape[1], x.shape[2]
    xi, yi, ci = _me()
    q = 2 * xi + yi
    dev = 2 * q + ci
    nsub = 2 * DEPTH
    cs = ada_w.shape[-1]
    ls = ln_g.shape[-1]

    pack = jnp.concatenate([c.reshape(-1), ln_g.reshape(-1), ln_b.reshape(-1)]).reshape(-1, LANES)
    got = _all_gather_small(pack, "gather_small").reshape(N_DEV, -1)
    c_all = got[:, :d]
    per_chip = got[0::2]
    ln_g_full = per_chip[:, d:d + nsub * ls].reshape(N_CHIPS, nsub, ls).transpose(1, 0, 2).reshape(nsub, d)
    ln_b_full = per_chip[:, d + nsub * ls:].reshape(N_CHIPS, nsub, ls).transpose(1, 0, 2).reshape(nsub, d)
    m_part = _ada_fwd(c_all, ada_w.reshape(nsub, d, cs), ada_b.reshape(nsub, 1, cs), "ada_fwd")
    m_all = _all_gather_small(m_part.reshape(-1, LANES), "gather_mod").reshape(N_DEV, nsub, N_DEV, cs)
    m_mine = lax.dynamic_index_in_dim(m_all[0::2], dev, axis=2, keepdims=False)
    mvec = m_mine.transpose(1, 0, 2).reshape(nsub, 3 * d)

    shards = {
        "a_w_in": a_w_in[0], "a_w_out": a_w_out[0], "b_w_qkv": b_w_qkv[0], "b_w_out": b_w_out[0],
        "up0": mlp_w_up[0], "up1": mlp_w_up[1], "down0": mlp_w_down[0], "down1": mlp_w_down[1],
    }
    gathered = _gather_weights([shards[k].astype(MXU_DTYPE) for k in BIG], "gather_weights")
    big = {}
    for k, gw in zip(BIG, gathered):
        big[k] = gw if BIG_KIND[k] == "col" else gw.reshape(1, -1, gw.shape[-1])

    tril = jnp.tril(jnp.ones((CHUNK, CHUNK), bool))
    wc = jnp.where(tril, a_w_s[0], 0.0).astype(MXU_DTYPE)
    heads = jnp.arange(1, B_HEADS + 1, dtype=F32)
    small = {
        "a_b_in": a_b_in, "a_vn_g": a_vn_g, "a_vn_b": a_vn_b,
        "wc": wc, "wct": wc.transpose(0, 2, 1),
        "bias_full": jnp.repeat(a_b_s[0].T, d // A_GROUPS, axis=1),
        "slopes": jnp.broadcast_to(jnp.exp2(-8.0 * heads / B_HEADS)[:, None], (B_HEADS, LANES)),
    }

    loss_part, grad_x, gb, dm, dlg, dlb, gsmall = _local_step(x[0], loss_target[0], mvec, ln_g_full, ln_b_full, small, big)
    loss = lax.psum(loss_part, ("x", "y", "c"))

    bufs = _scatter_partials([gb[k] for k in BIG], [BIG_KIND[k] for k in BIG], "scatter_partials")
    halves = [_sum_slots(b, f"sum_{k}") for k, b in zip(BIG, bufs)]
    fulls = _swap_halves(halves, "swap_halves")
    gfull = {k: f.reshape(-1, f.shape[-1]) for k, f in zip(BIG, fulls)}

    pack_b = jnp.concatenate([dm.reshape(-1), dlg.reshape(-1), dlb.reshape(-1)] + [gsmall[k] for k in SMALL])
    n_small = pack_b.shape[0]
    got_b = _all_gather_small(pack_b.reshape(-1, LANES), "gather_small_grads").reshape(N_DEV, -1, LANES)
    tot = _sum_slots(got_b, "sum_small").reshape(-1)
    o = 0
    dm_tot = tot[o:o + nsub * 3 * d].reshape(nsub, 3 * d); o += nsub * 3 * d
    dlg_tot = tot[o:o + nsub * d].reshape(nsub, d); o += nsub * d
    dlb_tot = tot[o:o + nsub * d].reshape(nsub, d); o += nsub * d
    g_small = {}
    for k, ref in zip(SMALL, (a_b_in, a_vn_g, a_vn_b, a_b_s, a_w_s)):
        g_small[k] = tot[o:o + ref.size].reshape(ref.shape); o += ref.size
    assert o == n_small
    dm_all = got_b.reshape(N_DEV, -1)[:, :nsub * 3 * d].reshape(N_DEV, nsub, 3 * d)
    dm_cols = lax.dynamic_slice_in_dim(dm_all, q * cs, cs, axis=2).transpose(1, 0, 2)

    grads = {
        "ada_w": _ada_bwd(c_all.T, dm_cols, "ada_bwd").reshape(ada_w.shape),
        "ada_b": lax.dynamic_slice_in_dim(dm_tot, q * cs, cs, axis=1).reshape(ada_b.shape),
        "ln_g": lax.dynamic_slice_in_dim(dlg_tot, q * ls, ls, axis=1).reshape(ln_g.shape),
        "ln_b": lax.dynamic_slice_in_dim(dlb_tot, q * ls, ls, axis=1).reshape(ln_b.shape),
        "a_w_in": gfull["a_w_in"][None], "a_w_out": gfull["a_w_out"][None],
        "b_w_qkv": gfull["b_w_qkv"][None], "b_w_out": gfull["b_w_out"][None],
        "mlp_w_up": jnp.stack([gfull["up0"], gfull["up1"]]), "mlp_w_down": jnp.stack([gfull["down0"], gfull["down1"]]),
        **g_small,
    }
    weights = dict(ada_w=ada_w, ada_b=ada_b, ln_g=ln_g, ln_b=ln_b, a_w_in=a_w_in, a_b_in=a_b_in, a_vn_g=a_vn_g, a_vn_b=a_vn_b,
                   a_w_s=a_w_s, a_b_s=a_b_s, a_w_out=a_w_out, b_w_qkv=b_w_qkv, b_w_out=b_w_out, mlp_w_up=mlp_w_up, mlp_w_down=mlp_w_down)
    ms = dict(ada_w=m_ada_w, ada_b=m_ada_b, ln_g=m_ln_g, ln_b=m_ln_b, a_w_in=m_a_w_in, a_b_in=m_a_b_in, a_vn_g=m_a_vn_g, a_vn_b=m_a_vn_b,
              a_w_s=m_a_w_s, a_b_s=m_a_b_s, a_w_out=m_a_w_out, b_w_qkv=m_b_w_qkv, b_w_out=m_b_w_out, mlp_w_up=m_mlp_w_up, mlp_w_down=m_mlp_w_down)
    vs = dict(ada_w=v_ada_w, ada_b=v_ada_b, ln_g=v_ln_g, ln_b=v_ln_b, a_w_in=v_a_w_in, a_b_in=v_a_b_in, a_vn_g=v_a_vn_g, a_vn_b=v_a_vn_b,
              a_w_s=v_a_w_s, a_b_s=v_a_b_s, a_w_out=v_a_w_out, b_w_qkv=v_b_w_qkv, b_w_out=v_b_w_out, mlp_w_up=v_mlp_w_up, mlp_w_down=v_mlp_w_down)
    names = list(weights)
    deltas, new_m, new_v = [], [], []
    for k in names:
        dl, nm, nv = _adamw(weights[k], grads[k], ms[k], vs[k], f"adamw_{k}")
        deltas.append(dl)
        new_m.append(nm)
        new_v.append(nv)
    return (loss, grad_x[None], *[grads[k] for k in names], *deltas, *new_m, *new_v)
```

```python
import functools
import math

import jax
import jax.numpy as jnp
from jax import lax
from jax.experimental import pallas as pl
from jax.experimental.pallas import tpu as pltpu

F32 = jnp.float32
MXU_DTYPE = jnp.bfloat16

DEPTH = 2
CHUNK = 128
A_GROUPS = 16
B_HEADS = 16
HEAD_DIM = 64
B_PATTERNS = ((128, 1), (512, 4), (2048, 16))
SPAN = 128
ALPHA = (2 * DEPTH) ** 0.25
LN_EPS = 1e-5
NEG = -1e30
ATT_SCALE = HEAD_DIM ** -0.5
ADAM_LR, ADAM_B1, ADAM_B2, ADAM_EPS, ADAM_WD, ADAM_STEP = 0.001, 0.9, 0.999, 1e-08, 0.01, 10

N_CHIPS = 4
N_DEV = 8
LANES = 128
SUBLANES = 8
VMEM_LIMIT = 52 * 1024 * 1024
ROW_TILE = 256
MESH = pl.DeviceIdType.MESH


def _cparams(sem):
    return pltpu.CompilerParams(dimension_semantics=sem, vmem_limit_bytes=VMEM_LIMIT)


def _fold8(v):
    r, c = v.shape
    return jnp.sum(v.reshape(r // SUBLANES, SUBLANES, c), axis=0)


def _gelu(x):
    c = math.sqrt(2.0 / math.pi)
    return 0.5 * x * (1.0 + jnp.tanh(c * (x + 0.044715 * (x * x * x))))


def _gelu_grad(x):
    c = math.sqrt(2.0 / math.pi)
    t = jnp.tanh(c * (x + 0.044715 * (x * x * x)))
    return 0.5 * (1.0 + t) + 0.5 * x * (1.0 - t * t) * c * (1.0 + 3.0 * 0.044715 * x * x)


def _dot(a, b, dims):
    return lax.dot_general(a.astype(MXU_DTYPE), b.astype(MXU_DTYPE), (dims, ((), ())), preferred_element_type=F32)


def _dot_nn(a, b):
    return _dot(a, b, ((1,), (0,)))


def _dot_nt(a, b):
    return _dot(a, b, ((1,), (1,)))


def _dot_tn(a, b):
    return _dot(a, b, ((0,), (0,)))


def _mm(a, b, *, mode, name, outs, tm, tn, tk, epi=None, extras=(), b_col0=0, n_out=None):
    if mode == "nn":
        m, kdim = a.shape
        p, kb, ns = b.shape
        assert kb == kdim and ns % tn == 0 and b_col0 % tn == 0
        n = n_out if n_out is not None else p * ns
        npt, j0 = ns // tn, b_col0 // tn
        a_spec = pl.BlockSpec((tm, tk), lambda i, j, k: (i, k))
        b_spec = pl.BlockSpec((None, tk, tn), lambda i, j, k: ((j + j0) // npt, k, (j + j0) % npt))
        dot = _dot_nn
    elif mode == "nt":
        m, kdim = a.shape
        p, n, ns = b.shape
        assert ns % tk == 0 and b_col0 % tk == 0
        npt, j0 = ns // tk, b_col0 // tk
        a_spec = pl.BlockSpec((tm, tk), lambda i, j, k: (i, k))
        b_spec = pl.BlockSpec((None, tn, tk), lambda i, j, k: ((k + j0) // npt, j, (k + j0) % npt))
        dot = _dot_nt
    else:
        kdim, m = a.shape
        kb, n = b.shape
        assert kb == kdim
        a_spec = pl.BlockSpec((tk, tm), lambda i, j, k: (k, i))
        b_spec = pl.BlockSpec((tk, tn), lambda i, j, k: (k, j))
        dot = _dot_tn
    assert m % tm == 0 and n % tn == 0 and kdim % tk == 0, (name, m, n, kdim, tm, tn, tk)
    nk = kdim // tk
    ex_specs, ex_arrays = [], []
    for kind, arr in extras:
        if kind == "row":
            ex_specs.append(pl.BlockSpec((1, tn), lambda i, j, k: (0, j)))
        else:
            ex_specs.append(pl.BlockSpec((tm, tn), lambda i, j, k: (i, j)))
        ex_arrays.append(arr)
    n_ex, n_o = len(ex_arrays), len(outs)

    def body(a_ref, b_ref, *rest):
        ex_refs, o_refs, acc = rest[:n_ex], rest[n_ex:n_ex + n_o], rest[n_ex + n_o]
        k = pl.program_id(2)

        @pl.when(k == 0)
        def _():
            acc[...] = jnp.zeros_like(acc)

        acc[...] += dot(a_ref[...], b_ref[...])

        @pl.when(k == nk - 1)
        def _():
            r = acc[...]
            vals = epi(r, *[e[...] for e in ex_refs]) if epi is not None else [r]
            for o, v in zip(o_refs, vals):
                o[...] = v.astype(o.dtype)

    res = pl.pallas_call(
        body,
        grid=(m // tm, n // tn, nk),
        in_specs=[a_spec, b_spec] + ex_specs,
        out_specs=[pl.BlockSpec((tm, tn), lambda i, j, k: (i, j)) for _ in outs],
        out_shape=[jax.ShapeDtypeStruct((m, n), dt) for dt in outs],
        scratch_shapes=[pltpu.VMEM((tm, tn), F32)],
        name=name,
        compiler_params=_cparams(("parallel", "parallel", "arbitrary")),
    )(a, b, *ex_arrays)
    return res if len(outs) > 1 else res[0]


def _rows(body, n_rows, tr, ins, outs, name, scratch=()):
    def spec(kind, shape):
        if kind == "blk":
            return pl.BlockSpec((tr,) + tuple(shape[1:]), lambda i: (i,) + (0,) * (len(shape) - 1))
        return pl.BlockSpec(tuple(shape), lambda i: (0,) * len(shape))

    return pl.pallas_call(
        body,
        grid=(n_rows // tr,),
        in_specs=[spec(k, a.shape) for k, a in ins],
        out_specs=[spec(k, s) for k, s, _ in outs],
        out_shape=[jax.ShapeDtypeStruct(tuple(s), d) for _, s, d in outs],
        scratch_shapes=list(scratch),
        name=name,
        compiler_params=_cparams(("arbitrary",)),
    )(*[a for _, a in ins])


def _ln_stats(z):
    mu = jnp.mean(z, axis=-1, keepdims=True)
    zc = z - mu
    var = jnp.mean(zc * zc, axis=-1, keepdims=True)
    rstd = lax.rsqrt(var + LN_EPS)
    return zc * rstd, rstd


def _mod(x, scale, shift, name):
    s, d = x.shape

    def body(x_ref, sc_ref, sh_ref, h_ref):
        h_ref[...] = (x_ref[...] * (1.0 + sc_ref[...]) + sh_ref[...]).astype(h_ref.dtype)

    return _rows(body, s, ROW_TILE, [("blk", x), ("all", scale), ("all", shift)], [("blk", (s, d), MXU_DTYPE)], name)[0]


def _resid_ln(x, y, gate, g, b, nxt, name):
    s, d = x.shape
    ins = [("blk", x), ("blk", y), ("all", gate), ("all", g), ("all", b)]
    outs = [("blk", (s, d), F32)]
    if nxt is not None:
        ins += [("all", nxt[0]), ("all", nxt[1])]
        outs += [("blk", (s, d), MXU_DTYPE)]

    def body(x_ref, y_ref, gate_ref, g_ref, b_ref, *rest):
        z = ALPHA * x_ref[...] + gate_ref[...] * y_ref[...]
        xhat, _ = _ln_stats(z)
        xn = xhat * g_ref[...] + b_ref[...]
        if nxt is None:
            rest[0][...] = xn
        else:
            sc_ref, sh_ref, xn_ref, h_ref = rest
            xn_ref[...] = xn
            h_ref[...] = (xn * (1.0 + sc_ref[...]) + sh_ref[...]).astype(h_ref.dtype)

    res = _rows(body, s, ROW_TILE, ins, outs, name)
    return (res[0], res[1]) if nxt is not None else (res[0], None)


def _loss_grad(xf, target, name):
    s, d = xf.shape

    def body(x_ref, t_ref, dy_ref, l_ref, acc):
        i = pl.program_id(0)

        @pl.when(i == 0)
        def _():
            acc[...] = jnp.zeros_like(acc)

        e = x_ref[...] - t_ref[...]
        dy_ref[...] = e * (1.0 / d)
        acc[...] += _fold8(e * e)

        @pl.when(i == pl.num_programs(0) - 1)
        def _():
            l_ref[...] = jnp.full(l_ref.shape, 0.5 / d, F32) * jnp.sum(acc[...])

    dy, l = _rows(body, s, ROW_TILE, [("blk", xf), ("blk", target)],
                  [("blk", (s, d), F32), ("all", (SUBLANES, LANES), F32)], name,
                  scratch=[pltpu.VMEM((SUBLANES, d), F32)])
    return dy, l[0, 0]


def _ln_bwd(dxo, x, y, gate, g, name):
    s, d = x.shape

    def body(dxo_ref, x_ref, y_ref, gate_ref, g_ref, dxr_ref, dyy_ref, red_ref, a_g, a_b, a_gate):
        i = pl.program_id(0)

        @pl.when(i == 0)
        def _():
            a_g[...] = jnp.zeros_like(a_g)
            a_b[...] = jnp.zeros_like(a_b)
            a_gate[...] = jnp.zeros_like(a_gate)

        yv = y_ref[...]
        z = ALPHA * x_ref[...] + gate_ref[...] * yv
        xhat, rstd = _ln_stats(z)
        dxo_v = dxo_ref[...]
        dxh = dxo_v * g_ref[...]
        dz = rstd * (dxh - jnp.mean(dxh, axis=-1, keepdims=True) - xhat * jnp.mean(dxh * xhat, axis=-1, keepdims=True))
        dxr_ref[...] = ALPHA * dz
        dyy_ref[...] = (gate_ref[...] * dz).astype(dyy_ref.dtype)
        a_g[...] += _fold8(dxo_v * xhat)
        a_b[...] += _fold8(dxo_v)
        a_gate[...] += _fold8(dz * yv)

        @pl.when(i == pl.num_programs(0) - 1)
        def _():
            red_ref[...] = jnp.zeros_like(red_ref)
            red_ref[0:1, :] = jnp.sum(a_g[...], axis=0, keepdims=True)
            red_ref[1:2, :] = jnp.sum(a_b[...], axis=0, keepdims=True)
            red_ref[2:3, :] = jnp.sum(a_gate[...], axis=0, keepdims=True)

    return _rows(body, s, ROW_TILE, [("blk", dxo), ("blk", x), ("blk", y), ("all", gate), ("all", g)],
                 [("blk", (s, d), F32), ("blk", (s, d), MXU_DTYPE), ("all", (SUBLANES, d), F32)], name,
                 scratch=[pltpu.VMEM((SUBLANES, d), F32)] * 3)


def _mod_bwd(dxr, dhs, x, scale, name):
    s, d = x.shape
    n_dh = len(dhs)

    def body(dxr_ref, *rest):
        dh_refs = rest[:n_dh]
        x_ref, sc_ref, dx_ref, red_ref, a_sh, a_sc = rest[n_dh:]
        i = pl.program_id(0)

        @pl.when(i == 0)
        def _():
            a_sh[...] = jnp.zeros_like(a_sh)
            a_sc[...] = jnp.zeros_like(a_sc)

        dh = dh_refs[0][...]
        for r in dh_refs[1:]:
            dh = dh + r[...]
        dx_ref[...] = dxr_ref[...] + dh * (1.0 + sc_ref[...])
        a_sh[...] += _fold8(dh)
        a_sc[...] += _fold8(dh * x_ref[...])

        @pl.when(i == pl.num_programs(0) - 1)
        def _():
            red_ref[...] = jnp.zeros_like(red_ref)
            red_ref[0:1, :] = jnp.sum(a_sh[...], axis=0, keepdims=True)
            red_ref[1:2, :] = jnp.sum(a_sc[...], axis=0, keepdims=True)

    return _rows(body, s, ROW_TILE, [("blk", dxr)] + [("blk", h) for h in dhs] + [("blk", x), ("all", scale)],
                 [("blk", (s, d), F32), ("all", (SUBLANES, d), F32)], name,
                 scratch=[pltpu.VMEM((SUBLANES, d), F32)] * 2)


def _left_half(shape):
    return lax.broadcasted_iota(jnp.int32, shape, 1) < (LANES // 2)


def _spatial_z(vn, wc_ref, bias_ref, j):
    vb = vn[:, j * LANES:(j + 1) * LANES]
    z0 = _dot_nn(wc_ref[2 * j], vb)
    z1 = _dot_nn(wc_ref[2 * j + 1], vb)
    return jnp.where(_left_half(z0.shape), z0, z1) + bias_ref[:, j * LANES:(j + 1) * LANES]


def _spatial_fwd(uvpre, vn_g, vn_b, wc, bias_full, name):
    s, d2 = uvpre.shape
    d = d2 // 2

    def body(uv_ref, g_ref, b_ref, wc_ref, bias_ref, out_ref):
        u = _gelu(uv_ref[:, :d])
        v = _gelu(uv_ref[:, d:])
        vh, _ = _ln_stats(v)
        vn = vh * g_ref[...] + b_ref[...]
        for j in range(d // LANES):
            z = _spatial_z(vn, wc_ref, bias_ref, j)
            out_ref[:, j * LANES:(j + 1) * LANES] = (u[:, j * LANES:(j + 1) * LANES] * z).astype(out_ref.dtype)

    return _rows(body, s, CHUNK, [("blk", uvpre), ("all", vn_g), ("all", vn_b), ("all", wc), ("all", bias_full)],
                 [("blk", (s, d), MXU_DTYPE)], name)[0]


def _spatial_bwd(uvpre, dgated, vn_g, vn_b, wc, wct, bias_full, name):
    s, d2 = uvpre.shape
    d = d2 // 2

    def body(uv_ref, dg_ref, g_ref, b_ref, wc_ref, wct_ref, bias_ref,
             duv_ref, dws_ref, dbias_ref, dbin_ref, dvg_ref, dvb_ref, dvn_buf, a_bin, a_vg, a_vb):
        i = pl.program_id(0)

        @pl.when(i == 0)
        def _():
            dws_ref[...] = jnp.zeros_like(dws_ref)
            dbias_ref[...] = jnp.zeros_like(dbias_ref)
            a_bin[...] = jnp.zeros_like(a_bin)
            a_vg[...] = jnp.zeros_like(a_vg)
            a_vb[...] = jnp.zeros_like(a_vb)

        up = uv_ref[:, :d]
        vp = uv_ref[:, d:]
        u = _gelu(up)
        v = _gelu(vp)
        vh, rstd = _ln_stats(v)
        vn = vh * g_ref[...] + b_ref[...]
        dg = dg_ref[...]
        dzz = dg * u
        dbias_ref[...] += dzz
        for j in range(d // LANES):
            cols = slice(j * LANES, (j + 1) * LANES)
            z = _spatial_z(vn, wc_ref, bias_ref, j)
            dup = dg[:, cols] * z * _gelu_grad(up[:, cols])
            duv_ref[:, cols] = dup.astype(duv_ref.dtype)
            a_bin[:, cols] += _fold8(dup)
            dzb = dzz[:, cols]
            left = _left_half(dzb.shape)
            dvn_buf[:, cols] = jnp.where(left, _dot_nn(wct_ref[2 * j], dzb), _dot_nn(wct_ref[2 * j + 1], dzb))
            vb = vn[:, cols]
            dws_ref[2 * j] += _dot_nt(jnp.where(left, dzb, 0.0), vb)
            dws_ref[2 * j + 1] += _dot_nt(jnp.where(left, 0.0, dzb), vb)
        dvn = dvn_buf[...]
        a_vg[...] += _fold8(dvn * vh)
        a_vb[...] += _fold8(dvn)
        dvh = dvn * g_ref[...]
        dv = rstd * (dvh - jnp.mean(dvh, axis=-1, keepdims=True) - vh * jnp.mean(dvh * vh, axis=-1, keepdims=True))
        dvp = dv * _gelu_grad(vp)
        duv_ref[:, d:] = dvp.astype(duv_ref.dtype)
        a_bin[:, d:] += _fold8(dvp)

        @pl.when(i == pl.num_programs(0) - 1)
        def _():
            dbin_ref[...] = jnp.sum(a_bin[...], axis=0, keepdims=True)
            dvg_ref[...] = jnp.sum(a_vg[...], axis=0, keepdims=True)
            dvb_ref[...] = jnp.sum(a_vb[...], axis=0, keepdims=True)

    return _rows(body, s, CHUNK,
                 [("blk", uvpre), ("blk", dgated), ("all", vn_g), ("all", vn_b), ("all", wc), ("all", wct), ("all", bias_full)],
                 [("blk", (s, d2), MXU_DTYPE), ("all", (A_GROUPS, CHUNK, CHUNK), F32), ("all", (CHUNK, d), F32),
                  ("all", (1, d2), F32), ("all", (1, d), F32), ("all", (1, d), F32)], name,
                 scratch=[pltpu.VMEM((CHUNK, d), F32), pltpu.VMEM((SUBLANES, d2), F32),
                          pltpu.VMEM((SUBLANES, d), F32), pltpu.VMEM((SUBLANES, d), F32)])


def _head_mask(v, h):
    lane = lax.broadcasted_iota(jnp.int32, v.shape, 1)
    return jnp.where((lane >= h * HEAD_DIM) & (lane < (h + 1) * HEAD_DIM), v, jnp.zeros_like(v))


def _att_bias(slopes, dil):
    qi = lax.broadcasted_iota(jnp.int32, (SPAN, SPAN), 0)
    ki = lax.broadcasted_iota(jnp.int32, (SPAN, SPAN), 1)
    sl = slopes[:, None, None]
    cur = jnp.where(ki <= qi, -sl * (float(dil) * (qi - ki).astype(F32)), NEG)
    prev = jnp.where(ki >= qi, -sl * (float(dil) * (SPAN + qi - ki).astype(F32)), NEG)
    absent = jnp.full_like(prev, NEG)
    pairs = slopes.shape[0] // 2

    def fwd(pv):
        return jnp.concatenate([cur, pv], axis=2).reshape(pairs, 2 * SPAN, 2 * SPAN)

    def bwd(pv):
        return jnp.concatenate([cur.reshape(pairs, 2 * SPAN, SPAN), pv.reshape(pairs, 2 * SPAN, SPAN)], axis=1)

    return jnp.stack([fwd(absent), fwd(prev)]), jnp.stack([bwd(absent), bwd(prev)])


def _att_specs(s, d, dil, kinds):
    nb = s // (dil * SPAN)

    def rowblk(which, b):
        if which == "prev":
            return jnp.where(b % nb == 0, b, b - 1)
        if which == "next":
            return jnp.where(b % nb == nb - 1, b, b + 1)
        return b

    return [pl.BlockSpec((SPAN, d), functools.partial(lambda b, o, w: (rowblk(w, b), o), o=part, w=which))
            for part, which in kinds]


def _lane_col(v, h):
    return v[:, h * HEAD_DIM:h * HEAD_DIM + 1]


def _attn_fwd(qkv, slopes, dil, name):
    s, d3 = qkv.shape
    d = d3 // 3
    nb = s // (dil * SPAN)
    table, _ = _att_bias(slopes, dil)

    def body(q_ref, kc_ref, kp_ref, vc_ref, vp_ref, tb_ref, o_ref, l_ref):
        left = _left_half((SPAN, LANES))
        for hp in range(d // LANES):
            cols = slice(hp * LANES, (hp + 1) * LANES)
            q = q_ref[:, cols]
            q2 = jnp.concatenate([_head_mask(q, 0), _head_mask(q, 1)], axis=0) * ATT_SCALE
            k2 = jnp.concatenate([kc_ref[:, cols], kp_ref[:, cols]], axis=0)
            v2 = jnp.concatenate([vc_ref[:, cols], vp_ref[:, cols]], axis=0)
            sc = _dot_nt(q2, k2) + tb_ref[hp]
            m = jnp.max(sc, axis=-1, keepdims=True)
            p = jnp.exp(sc - m)
            l = jnp.sum(p, axis=-1, keepdims=True)
            r = _dot_nn(p, v2) * (1.0 / l)
            lse = jnp.broadcast_to(m + jnp.log(l), (2 * SPAN, LANES))
            o_ref[:, cols] = jnp.where(left, r[:SPAN], r[SPAN:])
            l_ref[:, cols] = jnp.where(left, lse[:SPAN], lse[SPAN:])

    specs = _att_specs(s, d, dil, [(0, "cur"), (1, "cur"), (1, "prev"), (2, "cur"), (2, "prev")])
    tbl = pl.BlockSpec((None,) + table.shape[1:], lambda b: (jnp.where(b % nb == 0, 0, 1), 0, 0, 0))
    out_spec = pl.BlockSpec((SPAN, d), lambda b: (b, 0))
    return pl.pallas_call(
        body,
        grid=(s // SPAN,),
        in_specs=specs + [tbl],
        out_specs=[out_spec, out_spec],
        out_shape=[jax.ShapeDtypeStruct((s, d), F32)] * 2,
        name=name,
        compiler_params=_cparams(("parallel",)),
    )(qkv, qkv, qkv, qkv, qkv, table)


def _attn_bwd(qkv, do, lse, dd, slopes, dil, name):
    s, d3 = qkv.shape
    d = d3 // 3
    nb = s // (dil * SPAN)
    _, table = _att_bias(slopes, dil)

    def heads_stacked(cur, nxt):
        return jnp.concatenate([_head_mask(cur, 0), _head_mask(cur, 1), _head_mask(nxt, 0), _head_mask(nxt, 1)], axis=0)

    def cols_stacked(cur, nxt):
        return jnp.concatenate([jnp.broadcast_to(_lane_col(a, h), (SPAN, LANES)) for a in (cur, nxt) for h in range(2)], axis=0)

    def body(k_ref, v_ref, qc_ref, qn_ref, doc_ref, don_ref, lc_ref, ln_ref, ddc_ref, ddn_ref, tb_ref, out_ref, carry):
        b = pl.program_id(0)

        @pl.when(b == 0)
        def _():
            carry[...] = jnp.zeros_like(carry)

        left = _left_half((SPAN, LANES))
        for hp in range(d // LANES):
            cols = slice(hp * LANES, (hp + 1) * LANES)
            k, v = k_ref[:, cols], v_ref[:, cols]
            q4 = heads_stacked(qc_ref[:, cols], qn_ref[:, cols])
            do4 = heads_stacked(doc_ref[:, cols], don_ref[:, cols])
            sc = _dot_nt(q4 * ATT_SCALE, k) + tb_ref[hp]
            p = jnp.exp(sc - cols_stacked(lc_ref[:, cols], ln_ref[:, cols]))
            ds = p * (_dot_nt(do4, v) - cols_stacked(ddc_ref[:, cols], ddn_ref[:, cols]))
            dq4 = _dot_nn(ds, k)
            dq_cur = jnp.where(left, dq4[:SPAN], dq4[SPAN:2 * SPAN]) + carry[:, cols]
            carry[:, cols] = jnp.where(left, dq4[2 * SPAN:3 * SPAN], dq4[3 * SPAN:])
            out_ref[:, cols] = (dq_cur * ATT_SCALE).astype(out_ref.dtype)
            out_ref[:, d + hp * LANES:d + (hp + 1) * LANES] = (_dot_tn(ds, q4) * ATT_SCALE).astype(out_ref.dtype)
            out_ref[:, 2 * d + hp * LANES:2 * d + (hp + 1) * LANES] = _dot_tn(p, do4).astype(out_ref.dtype)

    qkv_specs = _att_specs(s, d, dil, [(1, "cur"), (2, "cur"), (0, "cur"), (0, "next")])
    pair = _att_specs(s, d, dil, [(0, "cur"), (0, "next")])
    tbl = pl.BlockSpec((None,) + table.shape[1:], lambda b: (jnp.where(b % nb == nb - 1, 0, 1), 0, 0, 0))
    return pl.pallas_call(
        body,
        grid=(s // SPAN,),
        in_specs=qkv_specs + pair + pair + pair + [tbl],
        out_specs=pl.BlockSpec((SPAN, d3), lambda b: (b, 0)),
        out_shape=jax.ShapeDtypeStruct((s, d3), MXU_DTYPE),
        scratch_shapes=[pltpu.VMEM((SPAN, d), F32)],
        name=name,
        compiler_params=_cparams(("arbitrary",)),
    )(qkv, qkv, qkv, qkv, do, do, lse, lse, dd, dd, table)


def _mix_weights(l_refs):
    ls = [r[...] for r in l_refs]
    m = functools.reduce(jnp.maximum, ls)
    es = [jnp.exp(l - m) for l in ls]
    tot = functools.reduce(lambda a, c: a + c, es)
    return [e / tot for e in es]


def _combine_fwd(os_, ls_, name):
    s, d = os_[0].shape
    n = len(os_)

    def body(*refs):
        o_refs, l_refs, out_ref = refs[:n], refs[n:2 * n], refs[2 * n]
        ws = _mix_weights(l_refs)
        acc = ws[0] * o_refs[0][...]
        for w, o in zip(ws[1:], o_refs[1:]):
            acc = acc + w * o[...]
        out_ref[...] = acc

    return _rows(body, s, ROW_TILE, [("blk", a) for a in os_ + ls_], [("blk", (s, d), F32)], name)[0]


def _combine_bwd(do, o, ls_, name):
    s, d = o.shape
    n = len(ls_)
    ri = lax.broadcasted_iota(jnp.int32, (LANES, LANES), 0) // HEAD_DIM
    ci = lax.broadcasted_iota(jnp.int32, (LANES, LANES), 1) // HEAD_DIM
    seg = (ri == ci).astype(F32)

    def body(do_ref, o_ref, *rest):
        l_refs, seg_ref, outs = rest[:n], rest[n], rest[n + 1:]
        ws = _mix_weights(l_refs)
        dov = do_ref[...]
        prod = dov * o_ref[...]
        for j in range(d // LANES):
            cols = slice(j * LANES, (j + 1) * LANES)
            r = jnp.dot(prod[:, cols], seg_ref[...], precision=lax.Precision.HIGHEST, preferred_element_type=F32)
            for g in range(n):
                outs[2 * g][:, cols] = (ws[g][:, cols] * dov[:, cols]).astype(outs[2 * g].dtype)
                outs[2 * g + 1][:, cols] = ws[g][:, cols] * r

    outs = []
    for _ in range(n):
        outs += [("blk", (s, d), MXU_DTYPE), ("blk", (s, d), F32)]
    res = _rows(body, s, ROW_TILE, [("blk", do), ("blk", o)] + [("blk", l) for l in ls_] + [("all", seg)], outs, name)
    return [(res[2 * g], res[2 * g + 1]) for g in range(n)]


def _ada_fwd(c_all, w, b, name):
    nsub, d, cs = w.shape

    def body(c_ref, w_ref, b_ref, o_ref):
        cv = c_ref[...]
        sc = cv * (1.0 / (1.0 + jnp.exp(-cv)))
        o_ref[...] = _dot_nn(sc, w_ref[...]) + b_ref[...]

    return pl.pallas_call(
        body,
        grid=(nsub,),
        in_specs=[pl.BlockSpec(c_all.shape, lambda i: (0, 0)), pl.BlockSpec((None, d, cs), lambda i: (i, 0, 0)),
                  pl.BlockSpec((None, 1, cs), lambda i: (i, 0, 0))],
        out_specs=pl.BlockSpec((None, N_DEV, cs), lambda i: (i, 0, 0)),
        out_shape=jax.ShapeDtypeStruct((nsub, N_DEV, cs), F32),
        name=name,
        compiler_params=_cparams(("parallel",)),
    )(c_all, w, b)


def _ada_bwd(c_all_t, dm, name):
    d, nb = c_all_t.shape
    nsub, _, cs = dm.shape

    def body(c_ref, dm_ref, o_ref):
        cv = c_ref[...]
        sc = cv * (1.0 / (1.0 + jnp.exp(-cv)))
        acc = sc[:, 0:1] * dm_ref[0:1, :]
        for bi in range(1, nb):
            acc = acc + sc[:, bi:bi + 1] * dm_ref[bi:bi + 1, :]
        o_ref[...] = acc

    return pl.pallas_call(
        body,
        grid=(nsub,),
        in_specs=[pl.BlockSpec(c_all_t.shape, lambda i: (0, 0)), pl.BlockSpec((None, nb, cs), lambda i: (i, 0, 0))],
        out_specs=pl.BlockSpec((None, d, cs), lambda i: (i, 0, 0)),
        out_shape=jax.ShapeDtypeStruct((nsub, d, cs), F32),
        name=name,
        compiler_params=_cparams(("parallel",)),
    )(c_all_t, dm)


def _row_tile(r, row_elems):
    t = 2 * SUBLANES
    if r % t:
        return r
    while t * 2 * row_elems <= 256 * 1024 and r % (t * 2) == 0:
        t *= 2
    return t


def _adamw(w, g, m, v, name):
    shape = w.shape
    c = shape[-1]
    r = w.size // c
    tr = _row_tile(r, c)
    w2, g2, m2, v2 = [a.reshape(r, c) for a in (w, g, m, v)]
    bc1 = 1.0 - ADAM_B1 ** ADAM_STEP
    bc2 = 1.0 - ADAM_B2 ** ADAM_STEP

    def body(w_ref, g_ref, m_ref, v_ref, d_ref, nm_ref, nv_ref):
        gv = g_ref[...]
        nm = ADAM_B1 * m_ref[...] + (1.0 - ADAM_B1) * gv
        nv = ADAM_B2 * v_ref[...] + (1.0 - ADAM_B2) * (gv * gv)
        d_ref[...] = -ADAM_LR * ((nm / bc1) / (jnp.sqrt(nv / bc2) + ADAM_EPS) + ADAM_WD * w_ref[...])
        nm_ref[...] = nm
        nv_ref[...] = nv

    res = _rows(body, r, tr, [("blk", a) for a in (w2, g2, m2, v2)], [("blk", (r, c), F32)] * 3, name)
    return [a.reshape(shape) for a in res]


def _sum_slots(buf, name):
    n, r, c = buf.shape
    tr = _row_tile(r, n * c)

    def body(b_ref, o_ref):
        acc = b_ref[0].astype(F32)
        for k in range(1, n):
            acc = acc + b_ref[k].astype(F32)
        o_ref[...] = acc

    return pl.pallas_call(
        body,
        grid=(r // tr,),
        in_specs=[pl.BlockSpec((n, tr, c), lambda i: (0, i, 0))],
        out_specs=pl.BlockSpec((tr, c), lambda i: (i, 0)),
        out_shape=jax.ShapeDtypeStruct((r, c), F32),
        name=name,
        compiler_params=_cparams(("parallel",)),
    )(buf)


def _me():
    return lax.axis_index("x"), lax.axis_index("y"), lax.axis_index("c")


COPY_STREAMS = 16


def _row_chunks(rows):
    n = COPY_STREAMS
    while n > 1 and rows % (n * 2 * SUBLANES):
        n //= 2
    return [(i * (rows // n), rows // n) for i in range(n)]


def _all_gather_small(blk, name):
    m_per, n = blk.shape

    def body(x_ref, out_ref, send_sems, recv_sems, local_sem):
        x, y, c = _me()
        me, sibling = (x, y, c), (x, y, 1 - c)
        chips = [(1 - x, y), (x, 1 - y), (1 - x, 1 - y)]

        def rows(px, py, pc):
            return out_ref.at[pl.ds((4 * px + 2 * py + pc) * m_per, m_per), :]

        def copy(k, block, to, src=None):
            return pltpu.make_async_remote_copy(
                src_ref=rows(*block) if src is None else src, dst_ref=rows(*block),
                send_sem=send_sems.at[k], recv_sem=recv_sems.at[k], device_id=to, device_id_type=MESH)

        mine = pltpu.make_async_copy(x_ref, rows(*me), local_sem)
        mine.start()
        first = [copy(0, me, sibling, src=x_ref)]
        first += [copy(1 + j, me, (*chip, c), src=x_ref) for j, chip in enumerate(chips)]
        for cp in first:
            cp.start()
        passed = [copy(4 + j, (*chip, c), sibling) for j, chip in enumerate(chips)]
        for j, chip in enumerate(chips):
            copy(1 + j, (*chip, c), me).wait_recv()
            passed[j].start()
        copy(0, sibling, me).wait_recv()
        for j, chip in enumerate(chips):
            copy(4 + j, (*chip, 1 - c), me).wait_recv()
        for cp in first + passed:
            cp.wait_send()
        mine.wait()

    return pl.pallas_call(
        body,
        out_shape=jax.ShapeDtypeStruct((N_DEV * m_per, n), blk.dtype),
        in_specs=[pl.BlockSpec(memory_space=pltpu.VMEM)],
        out_specs=pl.BlockSpec(memory_space=pltpu.VMEM),
        scratch_shapes=[pltpu.SemaphoreType.DMA((7,)), pltpu.SemaphoreType.DMA((7,)), pltpu.SemaphoreType.DMA],
        name=name,
        compiler_params=pltpu.CompilerParams(vmem_limit_bytes=VMEM_LIMIT),
    )(blk)


def _gather_weights(shards, name):
    n = len(shards)

    def body(*refs):
        in_refs, out_refs = refs[:n], refs[n:2 * n]
        send_sems, recv_sems, local_sems = refs[2 * n:]
        x, y, c = _me()
        q = 2 * x + y
        chips = [(1 - x, y), (x, 1 - y), (1 - x, 1 - y)]
        local, remote = [], []
        for w in range(n):
            cp = pltpu.make_async_copy(in_refs[w], out_refs[w].at[q], local_sems.at[w])
            cp.start()
            local.append(cp)
        for w in range(n):
            for j, chip in enumerate(chips):
                for r0, nr in _row_chunks(shards[w].shape[0]):
                    pltpu.make_async_remote_copy(
                        src_ref=in_refs[w].at[pl.ds(r0, nr), :], dst_ref=out_refs[w].at[q, pl.ds(r0, nr), :],
                        send_sem=send_sems.at[3 * w + j], recv_sem=recv_sems.at[3 * w + j],
                        device_id=(*chip, c), device_id_type=MESH).start()
                remote.append(pltpu.make_async_remote_copy(
                    src_ref=in_refs[w], dst_ref=out_refs[w].at[q],
                    send_sem=send_sems.at[3 * w + j], recv_sem=recv_sems.at[3 * w + j],
                    device_id=(*chip, c), device_id_type=MESH))
        for cp in remote:
            cp.wait_recv()
        for cp in remote:
            cp.wait_send()
        for cp in local:
            cp.wait()

    hbm = pl.BlockSpec(memory_space=pltpu.HBM)
    return pl.pallas_call(
        body,
        out_shape=[jax.ShapeDtypeStruct((N_CHIPS,) + s.shape, s.dtype) for s in shards],
        in_specs=[hbm] * n,
        out_specs=[hbm] * n,
        scratch_shapes=[pltpu.SemaphoreType.DMA((3 * n,)), pltpu.SemaphoreType.DMA((3 * n,)), pltpu.SemaphoreType.DMA((n,))],
        name=name,
    )(*shards)


def _scatter_partials(grads, kinds, name):
    n = len(grads)
    pieces = []
    for gr, kind in zip(grads, kinds):
        k, nn = gr.shape
        pieces.append((k // 2, nn // N_CHIPS) if kind == "col" else (k // N_CHIPS // 2, nn))

    def body(*refs):
        in_refs, out_refs = refs[:n], refs[n:2 * n]
        send_sems, recv_sems = refs[2 * n:]
        x, y, c = _me()
        slot = 4 * x + 2 * y + c
        started = []
        for w in range(n):
            pr, pc = pieces[w]
            for r in range(N_DEV):
                fx, fy, fc = (r >> 2) & 1, (r >> 1) & 1, r & 1
                tx, ty, tc = (x + fx) % 2, (y + fy) % 2, (c + fc) % 2
                tq = 2 * tx + ty
                if kinds[w] == "col":
                    src = in_refs[w].at[pl.ds(tc * pr, pr), pl.ds(tq * pc, pc)]
                else:
                    src = in_refs[w].at[pl.ds((2 * tq + tc) * pr, pr), :]
                dst = out_refs[w].at[slot]
                if r == 0:
                    cp = pltpu.make_async_copy(src, dst, recv_sems.at[N_DEV * w])
                else:
                    cp = pltpu.make_async_remote_copy(
                        src_ref=src, dst_ref=dst, send_sem=send_sems.at[N_DEV * w + r], recv_sem=recv_sems.at[N_DEV * w + r],
                        device_id=(tx, ty, tc), device_id_type=MESH)
                cp.start()
                started.append((r, cp))
        for r, cp in started:
            if r == 0:
                cp.wait()
            else:
                cp.wait_recv()
        for r, cp in started:
            if r != 0:
                cp.wait_send()

    hbm = pl.BlockSpec(memory_space=pltpu.HBM)
    return pl.pallas_call(
        body,
        out_shape=[jax.ShapeDtypeStruct((N_DEV,) + p, g.dtype) for p, g in zip(pieces, grads)],
        in_specs=[hbm] * n,
        out_specs=[hbm] * n,
        scratch_shapes=[pltpu.SemaphoreType.DMA((N_DEV * n,)), pltpu.SemaphoreType.DMA((N_DEV * n,))],
        name=name,
    )(*grads)


def _swap_halves(halves, name):
    n = len(halves)

    def body(*refs):
        in_refs, out_refs = refs[:n], refs[n:2 * n]
        send_sems, recv_sems, local_sems = refs[2 * n:]
        x, y, c = _me()
        cps = []
        for w in range(n):
            lc = pltpu.make_async_copy(in_refs[w], out_refs[w].at[c], local_sems.at[w])
            lc.start()
            for r0, nr in _row_chunks(halves[w].shape[0]):
                pltpu.make_async_remote_copy(
                    src_ref=in_refs[w].at[pl.ds(r0, nr), :], dst_ref=out_refs[w].at[c, pl.ds(r0, nr), :],
                    send_sem=send_sems.at[w], recv_sem=recv_sems.at[w],
                    device_id=(x, y, 1 - c), device_id_type=MESH).start()
            rc = pltpu.make_async_remote_copy(
                src_ref=in_refs[w], dst_ref=out_refs[w].at[c], send_sem=send_sems.at[w], recv_sem=recv_sems.at[w],
                device_id=(x, y, 1 - c), device_id_type=MESH)
            cps.append((lc, rc))
        for lc, rc in cps:
            rc.wait_recv()
        for lc, rc in cps:
            rc.wait_send()
            lc.wait()

    hbm = pl.BlockSpec(memory_space=pltpu.HBM)
    return pl.pallas_call(
        body,
        out_shape=[jax.ShapeDtypeStruct((2,) + h.shape, h.dtype) for h in halves],
        in_specs=[hbm] * n,
        out_specs=[hbm] * n,
        scratch_shapes=[pltpu.SemaphoreType.DMA((n,)), pltpu.SemaphoreType.DMA((n,)), pltpu.SemaphoreType.DMA((n,))],
        name=name,
    )(*halves)


def _to_streams(a, dil):
    if dil == 1:
        return a
    s, c = a.shape
    return a.reshape(s // dil, dil, c).transpose(1, 0, 2).reshape(s, c)


def _from_streams(a, dil):
    if dil == 1:
        return a
    s, c = a.shape
    return a.reshape(dil, s // dil, c).transpose(1, 0, 2).reshape(s, c)


def _mm_tiles(s):
    return min(s, 1024)


def _local_step(x0, target, mvec, ln_g, ln_b, small, big):
    s, d = x0.shape
    tm = _mm_tiles(s)
    row = lambda v: v.reshape(1, -1)
    shift = [row(mvec[i, :d]) for i in range(4)]
    scale = [row(mvec[i, d:2 * d]) for i in range(4)]
    gate = [row(1.0 + mvec[i, 2 * d:]) for i in range(4)]
    lg = [row(ln_g[i]) for i in range(4)]
    lb = [row(ln_b[i]) for i in range(4)]
    mm = functools.partial(_mm, tm=tm)
    mm_w = functools.partial(_mm, tm=1024, tk=min(s, 512), mode="tn")

    xs, ys = [x0], []
    h0 = _mod(x0, scale[0], shift[0], "mod0")
    uvpre = mm(h0, big["a_w_in"], mode="nn", name="a_in", outs=[F32], tn=512, tk=512,
               epi=lambda r, bias: [r + bias], extras=[("row", small["a_b_in"])])
    gated = _spatial_fwd(uvpre, small["a_vn_g"], small["a_vn_b"], small["wc"], small["bias_full"], "a_spatial")
    ys.append(mm(gated, big["a_w_out"], mode="nn", name="a_out", outs=[F32], tn=1024, tk=512))
    x1, h1 = _resid_ln(xs[0], ys[0], gate[0], lg[0], lb[0], (scale[1], shift[1]), "ln0")
    xs.append(x1)
    relu2 = lambda r: [r, jnp.square(jnp.maximum(r, 0.0))]
    a0, r0 = mm(h1, big["up0"], mode="nn", name="up0", outs=[F32, MXU_DTYPE], tn=1024, tk=512, epi=relu2)
    ys.append(mm(r0, big["down0"], mode="nn", name="down0", outs=[F32], tn=1024, tk=512))
    x2, h2 = _resid_ln(xs[1], ys[1], gate[1], lg[1], lb[1], (scale[2], shift[2]), "ln1")
    xs.append(x2)
    hg, qkvs, o_g, l_g = [], [], [], []
    for g, (_, dil) in enumerate(B_PATTERNS):
        hp = _to_streams(h2, dil)
        qkv = mm(hp, big["b_w_qkv"], mode="nn", name=f"qkv{g}", outs=[MXU_DTYPE], tn=768, tk=512, b_col0=g * 3 * d, n_out=3 * d)
        og, lgv = _attn_fwd(qkv, small["slopes"], dil, f"attn_fwd{g}")
        hg.append(hp)
        qkvs.append(qkv)
        o_g.append(_from_streams(og, dil))
        l_g.append(_from_streams(lgv, dil))
    o_mix = _combine_fwd(o_g, l_g, "combine")
    ys.append(mm(o_mix, big["b_w_out"], mode="nn", name="b_out", outs=[F32], tn=1024, tk=512))
    x3, h3 = _resid_ln(xs[2], ys[2], gate[2], lg[2], lb[2], (scale[3], shift[3]), "ln2")
    xs.append(x3)
    a1, r1 = mm(h3, big["up1"], mode="nn", name="up1", outs=[F32, MXU_DTYPE], tn=1024, tk=512, epi=relu2)
    ys.append(mm(r1, big["down1"], mode="nn", name="down1", outs=[F32], tn=1024, tk=512))
    x4, _ = _resid_ln(xs[3], ys[3], gate[3], lg[3], lb[3], None, "ln3")

    gb, dm, dlg, dlb = {}, [None] * 4, [None] * 4, [None] * 4
    dx, loss = _loss_grad(x4, target, "loss")

    def mlp_bwd(i, sub, dx, h, a, r):
        dxr, dyy, red = _ln_bwd(dx, xs[sub], ys[sub], gate[sub], lg[sub], f"ln_bwd{sub}")
        gb[f"down{i}"] = mm_w(r, dyy, name=f"g_down{i}", outs=[MXU_DTYPE], tn=1024)
        da = mm(dyy, big[f"down{i}"], mode="nt", name=f"d_down{i}", outs=[MXU_DTYPE], tn=1024, tk=512,
                epi=lambda acc, av: [acc * (2.0 * jnp.maximum(av, 0.0))], extras=[("full", a)])
        gb[f"up{i}"] = mm_w(h, da, name=f"g_up{i}", outs=[MXU_DTYPE], tn=1024)
        dh = mm(da, big[f"up{i}"], mode="nt", name=f"d_up{i}", outs=[F32], tn=1024, tk=512)
        dx, red2 = _mod_bwd(dxr, [dh], xs[sub], scale[sub], f"mod_bwd{sub}")
        dm[sub] = jnp.concatenate([red2[0], red2[1], red[2]])
        dlg[sub], dlb[sub] = red[0], red[1]
        return dx

    dx = mlp_bwd(1, 3, dx, h3, a1, r1)
    dxr, dyy, red = _ln_bwd(dx, xs[2], ys[2], gate[2], lg[2], "ln_bwd2")
    gb["b_w_out"] = mm_w(o_mix, dyy, name="g_b_out", outs=[MXU_DTYPE], tn=1024)
    do = mm(dyy, big["b_w_out"], mode="nt", name="d_b_out", outs=[F32], tn=1024, tk=512)
    parts = _combine_bwd(do, o_mix, l_g, "combine_bwd")
    dhs, gq = [], []
    for g, (_, dil) in enumerate(B_PATTERNS):
        do_g, dd_g = _to_streams(parts[g][0], dil), _to_streams(parts[g][1], dil)
        lse_g = _to_streams(l_g[g], dil)
        dqkv = _attn_bwd(qkvs[g], do_g, lse_g, dd_g, small["slopes"], dil, f"attn_bwd{g}")
        gq.append(mm_w(hg[g], dqkv, name=f"g_qkv{g}", outs=[MXU_DTYPE], tn=1024))
        dh = mm(dqkv, big["b_w_qkv"], mode="nt", name=f"d_qkv{g}", outs=[F32], tn=1024, tk=768, b_col0=g * 3 * d)
        dhs.append(_from_streams(dh, dil))
    gb["b_w_qkv"] = jnp.concatenate(gq, axis=1)
    dx, red2 = _mod_bwd(dxr, dhs, xs[2], scale[2], "mod_bwd2")
    dm[2] = jnp.concatenate([red2[0], red2[1], red[2]])
    dlg[2], dlb[2] = red[0], red[1]
    dx = mlp_bwd(0, 1, dx, h1, a0, r0)
    dxr, dyy, red = _ln_bwd(dx, xs[0], ys[0], gate[0], lg[0], "ln_bwd0")
    gb["a_w_out"] = mm_w(gated, dyy, name="g_a_out", outs=[MXU_DTYPE], tn=1024)
    dgated = mm(dyy, big["a_w_out"], mode="nt", name="d_a_out", outs=[F32], tn=1024, tk=512)
    duv, dws, dbias, dbin, dvg, dvb = _spatial_bwd(uvpre, dgated, small["a_vn_g"], small["a_vn_b"], small["wc"],
                                                   small["wct"], small["bias_full"], "a_spatial_bwd")
    gb["a_w_in"] = mm_w(h0, duv, name="g_a_in", outs=[MXU_DTYPE], tn=1024)
    dh = mm(duv, big["a_w_in"], mode="nt", name="d_a_in", outs=[F32], tn=1024, tk=512)
    dx, red2 = _mod_bwd(dxr, [dh], xs[0], scale[0], "mod_bwd0")
    dm[0] = jnp.concatenate([red2[0], red2[1], red[2]])
    dlg[0], dlb[0] = red[0], red[1]

    tril = jnp.tril(jnp.ones((CHUNK, CHUNK), bool))
    gsmall = {
        "a_b_in": dbin.reshape(-1), "a_vn_g": dvg.reshape(-1), "a_vn_b": dvb.reshape(-1),
        "a_w_s": jnp.where(tril, dws, 0.0).reshape(-1),
        "a_b_s": dbias.reshape(CHUNK, A_GROUPS, d // A_GROUPS).sum(-1).T.reshape(-1),
    }
    return loss, dx, gb, jnp.stack(dm), jnp.stack(dlg), jnp.stack(dlb), gsmall


BIG = ("a_w_in", "a_w_out", "b_w_qkv", "b_w_out", "up0", "up1", "down0", "down1")
BIG_KIND = {"a_w_in": "col", "a_w_out": "row", "b_w_qkv": "col", "b_w_out": "row",
            "up0": "col", "up1": "col", "down0": "row", "down1": "row"}
SMALL = ("a_b_in", "a_vn_g", "a_vn_b", "a_b_s", "a_w_s")


def kernel(x, c, ada_w, ada_b, ln_g, ln_b, a_w_in, a_b_in, a_vn_g, a_vn_b, a_w_s, a_b_s, a_w_out, b_w_qkv, b_w_out, mlp_w_up, mlp_w_down, loss_target, m_ada_w, m_ada_b, m_ln_g, m_ln_b, m_a_w_in, m_a_b_in, m_a_vn_g, m_a_vn_b, m_a_w_s, m_a_b_s, m_a_w_out, m_b_w_qkv, m_b_w_out, m_mlp_w_up, m_mlp_w_down, v_ada_w, v_ada_b, v_ln_g, v_ln_b, v_a_w_in, v_a_b_in, v_a_vn_g, v_a_vn_b, v_a_w_s, v_a_b_s, v_a_w_out, v_b_w_qkv, v_b_w_out, v_mlp_w_up, v_mlp_w_down):
    s, d = x.shape[1], x.shape[2]
    xi, yi, ci = _me()
    q = 2 * xi + yi
    dev = 2 * q + ci
    nsub = 2 * DEPTH
    cs = ada_w.shape[-1]
    ls = ln_g.shape[-1]

    pack = jnp.concatenate([c.reshape(-1), ln_g.reshape(-1), ln_b.reshape(-1)]).reshape(-1, LANES)
    got = _all_gather_small(pack, "gather_small").reshape(N_DEV, -1)
    c_all = got[:, :d]
    per_chip = got[0::2]
    ln_g_full = per_chip[:, d:d + nsub * ls].reshape(N_CHIPS, nsub, ls).transpose(1, 0, 2).reshape(nsub, d)
    ln_b_full = per_chip[:, d + nsub * ls:].reshape(N_CHIPS, nsub, ls).transpose(1, 0, 2).reshape(nsub, d)
    m_part = _ada_fwd(c_all, ada_w.reshape(nsub, d, cs), ada_b.reshape(nsub, 1, cs), "ada_fwd")
    m_all = _all_gather_small(m_part.reshape(-1, LANES), "gather_mod").reshape(N_DEV, nsub, N_DEV, cs)
    m_mine = lax.dynamic_index_in_dim(m_all[0::2], dev, axis=2, keepdims=False)
    mvec = m_mine.transpose(1, 0, 2).reshape(nsub, 3 * d)

    shards = {
        "a_w_in": a_w_in[0], "a_w_out": a_w_out[0], "b_w_qkv": b_w_qkv[0], "b_w_out": b_w_out[0],
        "up0": mlp_w_up[0], "up1": mlp_w_up[1], "down0": mlp_w_down[0], "down1": mlp_w_down[1],
    }
    gathered = _gather_weights([shards[k].astype(MXU_DTYPE) for k in BIG], "gather_weights")
    big = {}
    for k, gw in zip(BIG, gathered):
        big[k] = gw if BIG_KIND[k] == "col" else gw.reshape(1, -1, gw.shape[-1])

    tril = jnp.tril(jnp.ones((CHUNK, CHUNK), bool))
    wc = jnp.where(tril, a_w_s[0], 0.0).astype(MXU_DTYPE)
    heads = jnp.arange(1, B_HEADS + 1, dtype=F32)
    small = {
        "a_b_in": a_b_in, "a_vn_g": a_vn_g, "a_vn_b": a_vn_b,
        "wc": wc, "wct": wc.transpose(0, 2, 1),
        "bias_full": jnp.repeat(a_b_s[0].T, d // A_GROUPS, axis=1),
        "slopes": jnp.exp2(-8.0 * heads / B_HEADS),
    }

    loss_part, grad_x, gb, dm, dlg, dlb, gsmall = _local_step(x[0], loss_target[0], mvec, ln_g_full, ln_b_full, small, big)
    loss = lax.psum(loss_part, ("x", "y", "c"))

    bufs = _scatter_partials([gb[k] for k in BIG], [BIG_KIND[k] for k in BIG], "scatter_partials")
    halves = [_sum_slots(b, f"sum_{k}") for k, b in zip(BIG, bufs)]
    fulls = _swap_halves(halves, "swap_halves")
    gfull = {k: f.reshape(-1, f.shape[-1]) for k, f in zip(BIG, fulls)}

    pack_b = jnp.concatenate([dm.reshape(-1), dlg.reshape(-1), dlb.reshape(-1)] + [gsmall[k] for k in SMALL])
    n_small = pack_b.shape[0]
    got_b = _all_gather_small(pack_b.reshape(-1, LANES), "gather_small_grads").reshape(N_DEV, -1, LANES)
    tot = _sum_slots(got_b, "sum_small").reshape(-1)
    o = 0
    dm_tot = tot[o:o + nsub * 3 * d].reshape(nsub, 3 * d); o += nsub * 3 * d
    dlg_tot = tot[o:o + nsub * d].reshape(nsub, d); o += nsub * d
    dlb_tot = tot[o:o + nsub * d].reshape(nsub, d); o += nsub * d
    g_small = {}
    for k, ref in zip(SMALL, (a_b_in, a_vn_g, a_vn_b, a_b_s, a_w_s)):
        g_small[k] = tot[o:o + ref.size].reshape(ref.shape); o += ref.size
    assert o == n_small
    dm_all = got_b.reshape(N_DEV, -1)[:, :nsub * 3 * d].reshape(N_DEV, nsub, 3 * d)
    dm_cols = lax.dynamic_slice_in_dim(dm_all, q * cs, cs, axis=2).transpose(1, 0, 2)

    grads = {
        "ada_w": _ada_bwd(c_all.T, dm_cols, "ada_bwd").reshape(ada_w.shape),
        "ada_b": lax.dynamic_slice_in_dim(dm_tot, q * cs, cs, axis=1).reshape(ada_b.shape),
        "ln_g": lax.dynamic_slice_in_dim(dlg_tot, q * ls, ls, axis=1).reshape(ln_g.shape),
        "ln_b": lax.dynamic_slice_in_dim(dlb_tot, q * ls, ls, axis=1).reshape(ln_b.shape),
        "a_w_in": gfull["a_w_in"][None], "a_w_out": gfull["a_w_out"][None],
        "b_w_qkv": gfull["b_w_qkv"][None], "b_w_out": gfull["b_w_out"][None],
        "mlp_w_up": jnp.stack([gfull["up0"], gfull["up1"]]), "mlp_w_down": jnp.stack([gfull["down0"], gfull["down1"]]),
        **g_small,
    }
    weights = dict(ada_w=ada_w, ada_b=ada_b, ln_g=ln_g, ln_b=ln_b, a_w_in=a_w_in, a_b_in=a_b_in, a_vn_g=a_vn_g, a_vn_b=a_vn_b,
                   a_w_s=a_w_s, a_b_s=a_b_s, a_w_out=a_w_out, b_w_qkv=b_w_qkv, b_w_out=b_w_out, mlp_w_up=mlp_w_up, mlp_w_down=mlp_w_down)
    ms = dict(ada_w=m_ada_w, ada_b=m_ada_b, ln_g=m_ln_g, ln_b=m_ln_b, a_w_in=m_a_w_in, a_b_in=m_a_b_in, a_vn_g=m_a_vn_g, a_vn_b=m_a_vn_b,
              a_w_s=m_a_w_s, a_b_s=m_a_b_s, a_w_out=m_a_w_out, b_w_qkv=m_b_w_qkv, b_w_out=m_b_w_out, mlp_w_up=m_mlp_w_up, mlp_w_down=m_mlp_w_down)
    vs = dict(ada_w=v_ada_w, ada_b=v_ada_b, ln_g=v_ln_g, ln_b=v_ln_b, a_w_in=v_a_w_in, a_b_in=v_a_b_in, a_vn_g=v_a_vn_g, a_vn_b=v_a_vn_b,
              a_w_s=v_a_w_s, a_b_s=v_a_b_s, a_w_out=v_a_w_out, b_w_qkv=v_b_w_qkv, b_w_out=v_b_w_out, mlp_w_up=v_mlp_w_up, mlp_w_down=v_mlp_w_down)
    names = list(weights)
    deltas, new_m, new_v = [], [], []
    for k in names:
        dl, nm, nv = _adamw(weights[k], grads[k], ms[k], vs[k], f"adamw_{k}")
        deltas.append(dl)
        new_m.append(nm)
        new_v.append(nv)
    return (loss, grad_x[None], *[grads[k] for k in names], *deltas, *new_m, *new_v)
```

```python
import functools
import math

import jax
import jax.numpy as jnp
from jax import lax
from jax.experimental import pallas as pl
from jax.experimental.pallas import tpu as pltpu

F32 = jnp.float32
MXU_DTYPE = jnp.bfloat16

DEPTH = 2
CHUNK = 128
A_GROUPS = 16
B_HEADS = 16
HEAD_DIM = 64
B_PATTERNS = ((128, 1), (512, 4), (2048, 16))
SPAN = 128
ALPHA = (2 * DEPTH) ** 0.25
LN_EPS = 1e-5
NEG = -1e30
ATT_SCALE = HEAD_DIM ** -0.5
ADAM_LR, ADAM_B1, ADAM_B2, ADAM_EPS, ADAM_WD, ADAM_STEP = 0.001, 0.9, 0.999, 1e-08, 0.01, 10

N_CHIPS = 4
N_DEV = 8
LANES = 128
SUBLANES = 8
VMEM_LIMIT = 52 * 1024 * 1024
ROW_TILE = 256
MESH = pl.DeviceIdType.MESH


def _cparams(sem):
    return pltpu.CompilerParams(dimension_semantics=sem, vmem_limit_bytes=VMEM_LIMIT)


def _fold8(v):
    r, c = v.shape
    return jnp.sum(v.reshape(r // SUBLANES, SUBLANES, c), axis=0)


def _gelu(x):
    c = math.sqrt(2.0 / math.pi)
    return 0.5 * x * (1.0 + jnp.tanh(c * (x + 0.044715 * (x * x * x))))


def _gelu_grad(x):
    c = math.sqrt(2.0 / math.pi)
    t = jnp.tanh(c * (x + 0.044715 * (x * x * x)))
    return 0.5 * (1.0 + t) + 0.5 * x * (1.0 - t * t) * c * (1.0 + 3.0 * 0.044715 * x * x)


def _dot(a, b, dims):
    return lax.dot_general(a.astype(MXU_DTYPE), b.astype(MXU_DTYPE), (dims, ((), ())), preferred_element_type=F32)


def _dot_nn(a, b):
    return _dot(a, b, ((1,), (0,)))


def _dot_nt(a, b):
    return _dot(a, b, ((1,), (1,)))


def _dot_tn(a, b):
    return _dot(a, b, ((0,), (0,)))


def _mm(a, b, *, mode, name, outs, tm, tn, tk, epi=None, extras=(), b_col0=0, n_out=None):
    if mode == "nn":
        m, kdim = a.shape
        p, kb, ns = b.shape
        assert kb == kdim and ns % tn == 0 and b_col0 % tn == 0
        n = n_out if n_out is not None else p * ns
        npt, j0 = ns // tn, b_col0 // tn
        a_spec = pl.BlockSpec((tm, tk), lambda i, j, k: (i, k))
        b_spec = pl.BlockSpec((None, tk, tn), lambda i, j, k: ((j + j0) // npt, k, (j + j0) % npt))
        dot = _dot_nn
    elif mode == "nt":
        m, kdim = a.shape
        p, n, ns = b.shape
        assert ns % tk == 0 and b_col0 % tk == 0
        npt, j0 = ns // tk, b_col0 // tk
        a_spec = pl.BlockSpec((tm, tk), lambda i, j, k: (i, k))
        b_spec = pl.BlockSpec((None, tn, tk), lambda i, j, k: ((k + j0) // npt, j, (k + j0) % npt))
        dot = _dot_nt
    else:
        kdim, m = a.shape
        kb, n = b.shape
        assert kb == kdim
        a_spec = pl.BlockSpec((tk, tm), lambda i, j, k: (k, i))
        b_spec = pl.BlockSpec((tk, tn), lambda i, j, k: (k, j))
        dot = _dot_tn
    assert m % tm == 0 and n % tn == 0 and kdim % tk == 0, (name, m, n, kdim, tm, tn, tk)
    nk = kdim // tk
    ex_specs, ex_arrays = [], []
    for kind, arr in extras:
        if kind == "row":
            ex_specs.append(pl.BlockSpec((1, tn), lambda i, j, k: (0, j)))
        else:
            ex_specs.append(pl.BlockSpec((tm, tn), lambda i, j, k: (i, j)))
        ex_arrays.append(arr)
    n_ex, n_o = len(ex_arrays), len(outs)

    def body(a_ref, b_ref, *rest):
        ex_refs, o_refs, acc = rest[:n_ex], rest[n_ex:n_ex + n_o], rest[n_ex + n_o]
        k = pl.program_id(2)

        @pl.when(k == 0)
        def _():
            acc[...] = jnp.zeros_like(acc)

        acc[...] += dot(a_ref[...], b_ref[...])

        @pl.when(k == nk - 1)
        def _():
            r = acc[...]
            vals = epi(r, *[e[...] for e in ex_refs]) if epi is not None else [r]
            for o, v in zip(o_refs, vals):
                o[...] = v.astype(o.dtype)

    res = pl.pallas_call(
        body,
        grid=(m // tm, n // tn, nk),
        in_specs=[a_spec, b_spec] + ex_specs,
        out_specs=[pl.BlockSpec((tm, tn), lambda i, j, k: (i, j)) for _ in outs],
        out_shape=[jax.ShapeDtypeStruct((m, n), dt) for dt in outs],
        scratch_shapes=[pltpu.VMEM((tm, tn), F32)],
        name=name,
        compiler_params=_cparams(("parallel", "parallel", "arbitrary")),
    )(a, b, *ex_arrays)
    return res if len(outs) > 1 else res[0]


def _rows(body, n_rows, tr, ins, outs, name, scratch=()):
    def spec(kind, shape):
        if kind == "blk":
            return pl.BlockSpec((tr,) + tuple(shape[1:]), lambda i: (i,) + (0,) * (len(shape) - 1))
        return pl.BlockSpec(tuple(shape), lambda i: (0,) * len(shape))

    return pl.pallas_call(
        body,
        grid=(n_rows // tr,),
        in_specs=[spec(k, a.shape) for k, a in ins],
        out_specs=[spec(k, s) for k, s, _ in outs],
        out_shape=[jax.ShapeDtypeStruct(tuple(s), d) for _, s, d in outs],
        scratch_shapes=list(scratch),
        name=name,
        compiler_params=_cparams(("arbitrary",)),
    )(*[a for _, a in ins])


def _ln_stats(z):
    mu = jnp.mean(z, axis=-1, keepdims=True)
    zc = z - mu
    var = jnp.mean(zc * zc, axis=-1, keepdims=True)
    rstd = lax.rsqrt(var + LN_EPS)
    return zc * rstd, rstd


def _mod(x, scale, shift, name):
    s, d = x.shape

    def body(x_ref, sc_ref, sh_ref, h_ref):
        h_ref[...] = (x_ref[...] * (1.0 + sc_ref[...]) + sh_ref[...]).astype(h_ref.dtype)

    return _rows(body, s, ROW_TILE, [("blk", x), ("all", scale), ("all", shift)], [("blk", (s, d), MXU_DTYPE)], name)[0]


def _resid_ln(x, y, gate, g, b, nxt, name):
    s, d = x.shape
    ins = [("blk", x), ("blk", y), ("all", gate), ("all", g), ("all", b)]
    outs = [("blk", (s, d), F32)]
    if nxt is not None:
        ins += [("all", nxt[0]), ("all", nxt[1])]
        outs += [("blk", (s, d), MXU_DTYPE)]

    def body(x_ref, y_ref, gate_ref, g_ref, b_ref, *rest):
        z = ALPHA * x_ref[...] + gate_ref[...] * y_ref[...]
        xhat, _ = _ln_stats(z)
        xn = xhat * g_ref[...] + b_ref[...]
        if nxt is None:
            rest[0][...] = xn
        else:
            sc_ref, sh_ref, xn_ref, h_ref = rest
            xn_ref[...] = xn
            h_ref[...] = (xn * (1.0 + sc_ref[...]) + sh_ref[...]).astype(h_ref.dtype)

    res = _rows(body, s, ROW_TILE, ins, outs, name)
    return (res[0], res[1]) if nxt is not None else (res[0], None)


def _loss_grad(xf, target, name):
    s, d = xf.shape

    def body(x_ref, t_ref, dy_ref, l_ref, acc):
        i = pl.program_id(0)

        @pl.when(i == 0)
        def _():
            acc[...] = jnp.zeros_like(acc)

        e = x_ref[...] - t_ref[...]
        dy_ref[...] = e * (1.0 / d)
        acc[...] += _fold8(e * e)

        @pl.when(i == pl.num_programs(0) - 1)
        def _():
            l_ref[...] = jnp.full(l_ref.shape, 0.5 / d, F32) * jnp.sum(acc[...])

    dy, l = _rows(body, s, ROW_TILE, [("blk", xf), ("blk", target)],
                  [("blk", (s, d), F32), ("all", (SUBLANES, LANES), F32)], name,
                  scratch=[pltpu.VMEM((SUBLANES, d), F32)])
    return dy, l[0, 0]


def _ln_bwd(dxo, x, y, gate, g, name):
    s, d = x.shape

    def body(dxo_ref, x_ref, y_ref, gate_ref, g_ref, dxr_ref, dyy_ref, red_ref, a_g, a_b, a_gate):
        i = pl.program_id(0)

        @pl.when(i == 0)
        def _():
            a_g[...] = jnp.zeros_like(a_g)
            a_b[...] = jnp.zeros_like(a_b)
            a_gate[...] = jnp.zeros_like(a_gate)

        yv = y_ref[...]
        z = ALPHA * x_ref[...] + gate_ref[...] * yv
        xhat, rstd = _ln_stats(z)
        dxo_v = dxo_ref[...]
        dxh = dxo_v * g_ref[...]
        dz = rstd * (dxh - jnp.mean(dxh, axis=-1, keepdims=True) - xhat * jnp.mean(dxh * xhat, axis=-1, keepdims=True))
        dxr_ref[...] = ALPHA * dz
        dyy_ref[...] = (gate_ref[...] * dz).astype(dyy_ref.dtype)
        a_g[...] += _fold8(dxo_v * xhat)
        a_b[...] += _fold8(dxo_v)
        a_gate[...] += _fold8(dz * yv)

        @pl.when(i == pl.num_programs(0) - 1)
        def _():
            red_ref[...] = jnp.zeros_like(red_ref)
            red_ref[0:1, :] = jnp.sum(a_g[...], axis=0, keepdims=True)
            red_ref[1:2, :] = jnp.sum(a_b[...], axis=0, keepdims=True)
            red_ref[2:3, :] = jnp.sum(a_gate[...], axis=0, keepdims=True)

    return _rows(body, s, ROW_TILE, [("blk", dxo), ("blk", x), ("blk", y), ("all", gate), ("all", g)],
                 [("blk", (s, d), F32), ("blk", (s, d), MXU_DTYPE), ("all", (SUBLANES, d), F32)], name,
                 scratch=[pltpu.VMEM((SUBLANES, d), F32)] * 3)


def _mod_bwd(dxr, dhs, x, scale, name):
    s, d = x.shape
    n_dh = len(dhs)

    def body(dxr_ref, *rest):
        dh_refs = rest[:n_dh]
        x_ref, sc_ref, dx_ref, red_ref, a_sh, a_sc = rest[n_dh:]
        i = pl.program_id(0)

        @pl.when(i == 0)
        def _():
            a_sh[...] = jnp.zeros_like(a_sh)
            a_sc[...] = jnp.zeros_like(a_sc)

        dh = dh_refs[0][...]
        for r in dh_refs[1:]:
            dh = dh + r[...]
        dx_ref[...] = dxr_ref[...] + dh * (1.0 + sc_ref[...])
        a_sh[...] += _fold8(dh)
        a_sc[...] += _fold8(dh * x_ref[...])

        @pl.when(i == pl.num_programs(0) - 1)
        def _():
            red_ref[...] = jnp.zeros_like(red_ref)
            red_ref[0:1, :] = jnp.sum(a_sh[...], axis=0, keepdims=True)
            red_ref[1:2, :] = jnp.sum(a_sc[...], axis=0, keepdims=True)

    return _rows(body, s, ROW_TILE, [("blk", dxr)] + [("blk", h) for h in dhs] + [("blk", x), ("all", scale)],
                 [("blk", (s, d), F32), ("all", (SUBLANES, d), F32)], name,
                 scratch=[pltpu.VMEM((SUBLANES, d), F32)] * 2)


def _left_half(shape):
    return lax.broadcasted_iota(jnp.int32, shape, 1) < (LANES // 2)


def _spatial_z(vn, wc_ref, bias_ref, j):
    vb = vn[:, j * LANES:(j + 1) * LANES]
    z0 = _dot_nn(wc_ref[2 * j], vb)
    z1 = _dot_nn(wc_ref[2 * j + 1], vb)
    return jnp.where(_left_half(z0.shape), z0, z1) + bias_ref[:, j * LANES:(j + 1) * LANES]


def _spatial_fwd(uvpre, vn_g, vn_b, wc, bias_full, name):
    s, d2 = uvpre.shape
    d = d2 // 2

    def body(uv_ref, g_ref, b_ref, wc_ref, bias_ref, out_ref):
        u = _gelu(uv_ref[:, :d])
        v = _gelu(uv_ref[:, d:])
        vh, _ = _ln_stats(v)
        vn = vh * g_ref[...] + b_ref[...]
        for j in range(d // LANES):
            z = _spatial_z(vn, wc_ref, bias_ref, j)
            out_ref[:, j * LANES:(j + 1) * LANES] = (u[:, j * LANES:(j + 1) * LANES] * z).astype(out_ref.dtype)

    return _rows(body, s, CHUNK, [("blk", uvpre), ("all", vn_g), ("all", vn_b), ("all", wc), ("all", bias_full)],
                 [("blk", (s, d), MXU_DTYPE)], name)[0]


def _spatial_bwd(uvpre, dgated, vn_g, vn_b, wc, wct, bias_full, name):
    s, d2 = uvpre.shape
    d = d2 // 2

    def body(uv_ref, dg_ref, g_ref, b_ref, wc_ref, wct_ref, bias_ref,
             duv_ref, dws_ref, dbias_ref, dbin_ref, dvg_ref, dvb_ref, dvn_buf, a_bin, a_vg, a_vb):
        i = pl.program_id(0)

        @pl.when(i == 0)
        def _():
            dws_ref[...] = jnp.zeros_like(dws_ref)
            dbias_ref[...] = jnp.zeros_like(dbias_ref)
            a_bin[...] = jnp.zeros_like(a_bin)
            a_vg[...] = jnp.zeros_like(a_vg)
            a_vb[...] = jnp.zeros_like(a_vb)

        up = uv_ref[:, :d]
        vp = uv_ref[:, d:]
        u = _gelu(up)
        v = _gelu(vp)
        vh, rstd = _ln_stats(v)
        vn = vh * g_ref[...] + b_ref[...]
        dg = dg_ref[...]
        dzz = dg * u
        dbias_ref[...] += dzz
        for j in range(d // LANES):
            cols = slice(j * LANES, (j + 1) * LANES)
            z = _spatial_z(vn, wc_ref, bias_ref, j)
            dup = dg[:, cols] * z * _gelu_grad(up[:, cols])
            duv_ref[:, cols] = dup.astype(duv_ref.dtype)
            a_bin[:, cols] += _fold8(dup)
            dzb = dzz[:, cols]
            left = _left_half(dzb.shape)
            dvn_buf[:, cols] = jnp.where(left, _dot_nn(wct_ref[2 * j], dzb), _dot_nn(wct_ref[2 * j + 1], dzb))
            vb = vn[:, cols]
            dws_ref[2 * j] += _dot_nt(jnp.where(left, dzb, 0.0), vb)
            dws_ref[2 * j + 1] += _dot_nt(jnp.where(left, 0.0, dzb), vb)
        dvn = dvn_buf[...]
        a_vg[...] += _fold8(dvn * vh)
        a_vb[...] += _fold8(dvn)
        dvh = dvn * g_ref[...]
        dv = rstd * (dvh - jnp.mean(dvh, axis=-1, keepdims=True) - vh * jnp.mean(dvh * vh, axis=-1, keepdims=True))
        dvp = dv * _gelu_grad(vp)
        duv_ref[:, d:] = dvp.astype(duv_ref.dtype)
        a_bin[:, d:] += _fold8(dvp)

        @pl.when(i == pl.num_programs(0) - 1)
        def _():
            dbin_ref[...] = jnp.sum(a_bin[...], axis=0, keepdims=True)
            dvg_ref[...] = jnp.sum(a_vg[...], axis=0, keepdims=True)
            dvb_ref[...] = jnp.sum(a_vb[...], axis=0, keepdims=True)

    return _rows(body, s, CHUNK,
                 [("blk", uvpre), ("blk", dgated), ("all", vn_g), ("all", vn_b), ("all", wc), ("all", wct), ("all", bias_full)],
                 [("blk", (s, d2), MXU_DTYPE), ("all", (A_GROUPS, CHUNK, CHUNK), F32), ("all", (CHUNK, d), F32),
                  ("all", (1, d2), F32), ("all", (1, d), F32), ("all", (1, d), F32)], name,
                 scratch=[pltpu.VMEM((CHUNK, d), F32), pltpu.VMEM((SUBLANES, d2), F32),
                          pltpu.VMEM((SUBLANES, d), F32), pltpu.VMEM((SUBLANES, d), F32)])


def _head_mask(v, h):
    lane = lax.broadcasted_iota(jnp.int32, v.shape, 1)
    return jnp.where((lane >= h * HEAD_DIM) & (lane < (h + 1) * HEAD_DIM), v, jnp.zeros_like(v))


def _att_bias(slopes, dil):
    qi = lax.broadcasted_iota(jnp.int32, (SPAN, SPAN), 0)
    ki = lax.broadcasted_iota(jnp.int32, (SPAN, SPAN), 1)
    sl = slopes[:, None, None]
    cur = jnp.where(ki <= qi, -sl * (float(dil) * (qi - ki).astype(F32)), NEG)
    prev = jnp.where(ki >= qi, -sl * (float(dil) * (SPAN + qi - ki).astype(F32)), NEG)
    absent = jnp.full_like(prev, NEG)
    pairs = slopes.shape[0] // 2

    def fwd(pv):
        return jnp.concatenate([cur, pv], axis=2).reshape(pairs, 2 * SPAN, 2 * SPAN)

    def bwd(pv):
        return jnp.concatenate([cur.reshape(pairs, 2 * SPAN, SPAN), pv.reshape(pairs, 2 * SPAN, SPAN)], axis=1)

    return jnp.stack([fwd(absent), fwd(prev)]), jnp.stack([bwd(absent), bwd(prev)])


def _att_specs(s, d, dil, kinds):
    nb = s // (dil * SPAN)

    def rowblk(which, b):
        if which == "prev":
            return jnp.where(b % nb == 0, b, b - 1)
        if which == "next":
            return jnp.where(b % nb == nb - 1, b, b + 1)
        return b

    return [pl.BlockSpec((SPAN, d), functools.partial(lambda b, o, w: (rowblk(w, b), o), o=part, w=which))
            for part, which in kinds]


def _lane_col(v, h):
    return v[:, h * HEAD_DIM:h * HEAD_DIM + 1]


def _attn_fwd(qkv, slopes, dil, name):
    s, d3 = qkv.shape
    d = d3 // 3
    nb = s // (dil * SPAN)
    table, _ = _att_bias(slopes, dil)

    def body(q_ref, kc_ref, kp_ref, vc_ref, vp_ref, tb_ref, o_ref, l_ref):
        left = _left_half((SPAN, LANES))
        for hp in range(d // LANES):
            cols = slice(hp * LANES, (hp + 1) * LANES)
            q = q_ref[:, cols]
            q2 = jnp.concatenate([_head_mask(q, 0), _head_mask(q, 1)], axis=0) * ATT_SCALE
            k2 = jnp.concatenate([kc_ref[:, cols], kp_ref[:, cols]], axis=0)
            v2 = jnp.concatenate([vc_ref[:, cols], vp_ref[:, cols]], axis=0)
            sc = _dot_nt(q2, k2) + tb_ref[hp]
            m = jnp.max(sc, axis=-1, keepdims=True)
            p = jnp.exp(sc - m)
            l = jnp.sum(p, axis=-1, keepdims=True)
            r = _dot_nn(p, v2) * (1.0 / l)
            lse = jnp.broadcast_to(m + jnp.log(l), (2 * SPAN, LANES))
            o_ref[:, cols] = jnp.where(left, r[:SPAN], r[SPAN:])
            l_ref[:, cols] = jnp.where(left, lse[:SPAN], lse[SPAN:])

    specs = _att_specs(s, d, dil, [(0, "cur"), (1, "cur"), (1, "prev"), (2, "cur"), (2, "prev")])
    tbl = pl.BlockSpec((None,) + table.shape[1:], lambda b: (jnp.where(b % nb == 0, 0, 1), 0, 0, 0))
    out_spec = pl.BlockSpec((SPAN, d), lambda b: (b, 0))
    return pl.pallas_call(
        body,
        grid=(s // SPAN,),
        in_specs=specs + [tbl],
        out_specs=[out_spec, out_spec],
        out_shape=[jax.ShapeDtypeStruct((s, d), F32)] * 2,
        name=name,
        compiler_params=_cparams(("parallel",)),
    )(qkv, qkv, qkv, qkv, qkv, table)


def _attn_bwd(qkv, do, lse, dd, slopes, dil, name):
    s, d3 = qkv.shape
    d = d3 // 3
    nb = s // (dil * SPAN)
    _, table = _att_bias(slopes, dil)

    def heads_stacked(cur, nxt):
        return jnp.concatenate([_head_mask(cur, 0), _head_mask(cur, 1), _head_mask(nxt, 0), _head_mask(nxt, 1)], axis=0)

    def cols_stacked(cur, nxt):
        return jnp.concatenate([jnp.broadcast_to(_lane_col(a, h), (SPAN, LANES)) for a in (cur, nxt) for h in range(2)], axis=0)

    def body(k_ref, v_ref, qc_ref, qn_ref, doc_ref, don_ref, lc_ref, ln_ref, ddc_ref, ddn_ref, tb_ref, out_ref, carry):
        b = pl.program_id(0)

        @pl.when(b == 0)
        def _():
            carry[...] = jnp.zeros_like(carry)

        left = _left_half((SPAN, LANES))
        for hp in range(d // LANES):
            cols = slice(hp * LANES, (hp + 1) * LANES)
            k, v = k_ref[:, cols], v_ref[:, cols]
            q4 = heads_stacked(qc_ref[:, cols], qn_ref[:, cols])
            do4 = heads_stacked(doc_ref[:, cols], don_ref[:, cols])
            sc = _dot_nt(q4 * ATT_SCALE, k) + tb_ref[hp]
            p = jnp.exp(sc - cols_stacked(lc_ref[:, cols], ln_ref[:, cols]))
            ds = p * (_dot_nt(do4, v) - cols_stacked(ddc_ref[:, cols], ddn_ref[:, cols]))
            dq4 = _dot_nn(ds, k)
            dq_cur = jnp.where(left, dq4[:SPAN], dq4[SPAN:2 * SPAN]) + carry[:, cols]
            carry[:, cols] = jnp.where(left, dq4[2 * SPAN:3 * SPAN], dq4[3 * SPAN:])
            out_ref[:, cols] = (dq_cur * ATT_SCALE).astype(out_ref.dtype)
            out_ref[:, d + hp * LANES:d + (hp + 1) * LANES] = (_dot_tn(ds, q4) * ATT_SCALE).astype(out_ref.dtype)
            out_ref[:, 2 * d + hp * LANES:2 * d + (hp + 1) * LANES] = _dot_tn(p, do4).astype(out_ref.dtype)

    qkv_specs = _att_specs(s, d, dil, [(1, "cur"), (2, "cur"), (0, "cur"), (0, "next")])
    pair = _att_specs(s, d, dil, [(0, "cur"), (0, "next")])
    tbl = pl.BlockSpec((None,) + table.shape[1:], lambda b: (jnp.where(b % nb == nb - 1, 0, 1), 0, 0, 0))
    return pl.pallas_call(
        body,
        grid=(s // SPAN,),
        in_specs=qkv_specs + pair + pair + pair + [tbl],
        out_specs=pl.BlockSpec((SPAN, d3), lambda b: (b, 0)),
        out_shape=jax.ShapeDtypeStruct((s, d3), MXU_DTYPE),
        scratch_shapes=[pltpu.VMEM((SPAN, d), F32)],
        name=name,
        compiler_params=_cparams(("arbitrary",)),
    )(qkv, qkv, qkv, qkv, do, do, lse, lse, dd, dd, table)


def _mix_weights(l_refs):
    ls = [r[...] for r in l_refs]
    m = functools.reduce(jnp.maximum, ls)
    es = [jnp.exp(l - m) for l in ls]
    tot = functools.reduce(lambda a, c: a + c, es)
    return [e / tot for e in es]


def _combine_fwd(os_, ls_, name):
    s, d = os_[0].shape
    n = len(os_)

    def body(*refs):
        o_refs, l_refs, out_ref = refs[:n], refs[n:2 * n], refs[2 * n]
        ws = _mix_weights(l_refs)
        acc = ws[0] * o_refs[0][...]
        for w, o in zip(ws[1:], o_refs[1:]):
            acc = acc + w * o[...]
        out_ref[...] = acc

    return _rows(body, s, ROW_TILE, [("blk", a) for a in os_ + ls_], [("blk", (s, d), F32)], name)[0]


def _combine_bwd(do, o, ls_, name):
    s, d = o.shape
    n = len(ls_)
    ri = lax.broadcasted_iota(jnp.int32, (LANES, LANES), 0) // HEAD_DIM
    ci = lax.broadcasted_iota(jnp.int32, (LANES, LANES), 1) // HEAD_DIM
    seg = (ri == ci).astype(F32)

    def body(do_ref, o_ref, *rest):
        l_refs, seg_ref, outs = rest[:n], rest[n], rest[n + 1:]
        ws = _mix_weights(l_refs)
        dov = do_ref[...]
        prod = dov * o_ref[...]
        for j in range(d // LANES):
            cols = slice(j * LANES, (j + 1) * LANES)
            r = jnp.dot(prod[:, cols], seg_ref[...], precision=lax.Precision.HIGHEST, preferred_element_type=F32)
            for g in range(n):
                outs[2 * g][:, cols] = (ws[g][:, cols] * dov[:, cols]).astype(outs[2 * g].dtype)
                outs[2 * g + 1][:, cols] = ws[g][:, cols] * r

    outs = []
    for _ in range(n):
        outs += [("blk", (s, d), MXU_DTYPE), ("blk", (s, d), F32)]
    res = _rows(body, s, ROW_TILE, [("blk", do), ("blk", o)] + [("blk", l) for l in ls_] + [("all", seg)], outs, name)
    return [(res[2 * g], res[2 * g + 1]) for g in range(n)]


def _ada_fwd(c_all, w, b, name):
    nsub, d, cs = w.shape

    def body(c_ref, w_ref, b_ref, o_ref):
        cv = c_ref[...]
        sc = cv * (1.0 / (1.0 + jnp.exp(-cv)))
        o_ref[...] = _dot_nn(sc, w_ref[...]) + b_ref[...]

    return pl.pallas_call(
        body,
        grid=(nsub,),
        in_specs=[pl.BlockSpec(c_all.shape, lambda i: (0, 0)), pl.BlockSpec((None, d, cs), lambda i: (i, 0, 0)),
                  pl.BlockSpec((None, 1, cs), lambda i: (i, 0, 0))],
        out_specs=pl.BlockSpec((None, N_DEV, cs), lambda i: (i, 0, 0)),
        out_shape=jax.ShapeDtypeStruct((nsub, N_DEV, cs), F32),
        name=name,
        compiler_params=_cparams(("parallel",)),
    )(c_all, w, b)


def _ada_bwd(c_all_t, dm, name):
    d, nb = c_all_t.shape
    nsub, _, cs = dm.shape

    def body(c_ref, dm_ref, o_ref):
        cv = c_ref[...]
        sc = cv * (1.0 / (1.0 + jnp.exp(-cv)))
        acc = sc[:, 0:1] * dm_ref[0:1, :]
        for bi in range(1, nb):
            acc = acc + sc[:, bi:bi + 1] * dm_ref[bi:bi + 1, :]
        o_ref[...] = acc

    return pl.pallas_call(
        body,
        grid=(nsub,),
        in_specs=[pl.BlockSpec(c_all_t.shape, lambda i: (0, 0)), pl.BlockSpec((None, nb, cs), lambda i: (i, 0, 0))],
        out_specs=pl.BlockSpec((None, d, cs), lambda i: (i, 0, 0)),
        out_shape=jax.ShapeDtypeStruct((nsub, d, cs), F32),
        name=name,
        compiler_params=_cparams(("parallel",)),
    )(c_all_t, dm)


def _row_tile(r, row_elems):
    t = 2 * SUBLANES
    if r % t:
        return r
    while t * 2 * row_elems <= 256 * 1024 and r % (t * 2) == 0:
        t *= 2
    return t


def _adamw(w, g, m, v, name):
    shape = w.shape
    c = shape[-1]
    r = w.size // c
    tr = _row_tile(r, c)
    w2, g2, m2, v2 = [a.reshape(r, c) for a in (w, g, m, v)]
    bc1 = 1.0 - ADAM_B1 ** ADAM_STEP
    bc2 = 1.0 - ADAM_B2 ** ADAM_STEP

    def body(w_ref, g_ref, m_ref, v_ref, d_ref, nm_ref, nv_ref):
        gv = g_ref[...]
        nm = ADAM_B1 * m_ref[...] + (1.0 - ADAM_B1) * gv
        nv = ADAM_B2 * v_ref[...] + (1.0 - ADAM_B2) * (gv * gv)
        d_ref[...] = -ADAM_LR * ((nm / bc1) / (jnp.sqrt(nv / bc2) + ADAM_EPS) + ADAM_WD * w_ref[...])
        nm_ref[...] = nm
        nv_ref[...] = nv

    res = _rows(body, r, tr, [("blk", a) for a in (w2, g2, m2, v2)], [("blk", (r, c), F32)] * 3, name)
    return [a.reshape(shape) for a in res]


def _sum_slots(buf, name):
    n, r, c = buf.shape
    tr = _row_tile(r, n * c)

    def body(b_ref, o_ref):
        acc = b_ref[0].astype(F32)
        for k in range(1, n):
            acc = acc + b_ref[k].astype(F32)
        o_ref[...] = acc

    return pl.pallas_call(
        body,
        grid=(r // tr,),
        in_specs=[pl.BlockSpec((n, tr, c), lambda i: (0, i, 0))],
        out_specs=pl.BlockSpec((tr, c), lambda i: (i, 0)),
        out_shape=jax.ShapeDtypeStruct((r, c), F32),
        name=name,
        compiler_params=_cparams(("parallel",)),
    )(buf)


def _me():
    return lax.axis_index("x"), lax.axis_index("y"), lax.axis_index("c")


def _all_gather_small(blk, name):
    m_per, n = blk.shape

    def body(x_ref, out_ref, send_sems, recv_sems, local_sem):
        x, y, c = _me()
        me, sibling = (x, y, c), (x, y, 1 - c)
        chips = [(1 - x, y), (x, 1 - y), (1 - x, 1 - y)]

        def rows(px, py, pc):
            return out_ref.at[pl.ds((4 * px + 2 * py + pc) * m_per, m_per), :]

        def copy(k, block, to, src=None):
            return pltpu.make_async_remote_copy(
                src_ref=rows(*block) if src is None else src, dst_ref=rows(*block),
                send_sem=send_sems.at[k], recv_sem=recv_sems.at[k], device_id=to, device_id_type=MESH)

        mine = pltpu.make_async_copy(x_ref, rows(*me), local_sem)
        mine.start()
        first = [copy(0, me, sibling, src=x_ref)]
        first += [copy(1 + j, me, (*chip, c), src=x_ref) for j, chip in enumerate(chips)]
        for cp in first:
            cp.start()
        passed = [copy(4 + j, (*chip, c), sibling) for j, chip in enumerate(chips)]
        for j, chip in enumerate(chips):
            copy(1 + j, (*chip, c), me).wait_recv()
            passed[j].start()
        copy(0, sibling, me).wait_recv()
        for j, chip in enumerate(chips):
            copy(4 + j, (*chip, 1 - c), me).wait_recv()
        for cp in first + passed:
            cp.wait_send()
        mine.wait()

    return pl.pallas_call(
        body,
        out_shape=jax.ShapeDtypeStruct((N_DEV * m_per, n), blk.dtype),
        in_specs=[pl.BlockSpec(memory_space=pltpu.VMEM)],
        out_specs=pl.BlockSpec(memory_space=pltpu.VMEM),
        scratch_shapes=[pltpu.SemaphoreType.DMA((7,)), pltpu.SemaphoreType.DMA((7,)), pltpu.SemaphoreType.DMA],
        name=name,
        compiler_params=pltpu.CompilerParams(vmem_limit_bytes=VMEM_LIMIT),
    )(blk)


_HBM = pl.BlockSpec(memory_space=pltpu.HBM)
_SEM = pl.BlockSpec(memory_space=pltpu.SEMAPHORE)
_EFFECT = pltpu.SideEffectType.DATAFLOW_SIDE_EFFECTING


def _other_chips(x, y):
    return [(1 - x, y), (x, 1 - y), (1 - x, 1 - y)]


def _gather_copy(w, j, src_ref, land_ref, send_sems, recv_sems):
    x, y, c = _me()
    return pltpu.make_async_remote_copy(
        src_ref=src_ref, dst_ref=land_ref.at[2 * x + y], send_sem=send_sems.at[3 * w + j], recv_sem=recv_sems.at[3 * w + j],
        device_id=(*_other_chips(x, y)[j], c), device_id_type=MESH)


def _gather_start(shards, name):
    n = len(shards)
    lands = [lax.empty((N_CHIPS,) + s.shape, s.dtype) for s in shards]

    def body(*refs):
        in_refs, land_refs = refs[:n], refs[n:2 * n]
        send_sems, recv_sems = refs[2 * n], refs[2 * n + 1]
        token = refs[-1]
        for w in range(n):
            for j in range(3):
                _gather_copy(w, j, in_refs[w], land_refs[w], send_sems, recv_sems).start()
        token[...] = jnp.zeros_like(token)

    res = pl.pallas_call(
        body,
        out_shape=(pltpu.SemaphoreType.DMA((3 * n,)), pltpu.SemaphoreType.DMA((3 * n,)),
                   *[pltpu.HBM(s.shape, s.dtype) for s in shards], *[pltpu.HBM(l.shape, l.dtype) for l in lands],
                   jax.ShapeDtypeStruct((SUBLANES, LANES), F32)),
        in_specs=[_HBM] * (2 * n),
        out_specs=(_SEM, _SEM, *[_HBM] * (2 * n), pl.BlockSpec(memory_space=pltpu.VMEM)),
        input_output_aliases={i: 2 + i for i in range(2 * n)},
        name=name,
        compiler_params=pltpu.CompilerParams(has_side_effects=_EFFECT),
    )(*[pltpu.with_memory_space_constraint(a, pltpu.HBM) for a in list(shards) + lands])
    return res[0], res[1], res[2:2 + n], res[2 + n:2 + 2 * n], res[-1]


def _gather_wait(w, shard, land, send_sems, recv_sems, after, name):
    def body(s_ref, land_ref, send_sems, recv_sems, after_ref, s_out, land_out, local_sem):
        x, y, _ = _me()
        for j in range(3):
            cp = _gather_copy(w, j, s_ref, land_ref, send_sems, recv_sems)
            cp.wait_send()
            cp.wait_recv()
        own = pltpu.make_async_copy(s_ref, land_out.at[2 * x + y], local_sem)
        own.start()
        own.wait()

    return pl.pallas_call(
        body,
        out_shape=(pltpu.HBM(shard.shape, shard.dtype), pltpu.HBM(land.shape, land.dtype)),
        in_specs=(_HBM, _HBM, _SEM, _SEM, pl.BlockSpec(memory_space=pl.ANY)),
        out_specs=(_HBM, _HBM),
        input_output_aliases={0: 0, 1: 1},
        scratch_shapes=[pltpu.SemaphoreType.DMA],
        name=name,
        compiler_params=pltpu.CompilerParams(has_side_effects=_EFFECT),
    )(shard, land, send_sems, recv_sems, after)[1]


def _scatter_partials(grads, kinds, name):
    n = len(grads)
    pieces = []
    for gr, kind in zip(grads, kinds):
        k, nn = gr.shape
        pieces.append((k // 2, nn // N_CHIPS) if kind == "col" else (k // N_CHIPS // 2, nn))

    def body(*refs):
        in_refs, out_refs = refs[:n], refs[n:2 * n]
        send_sems, recv_sems = refs[2 * n:]
        x, y, c = _me()
        slot = 4 * x + 2 * y + c
        started = []
        for w in range(n):
            pr, pc = pieces[w]
            for r in range(N_DEV):
                fx, fy, fc = (r >> 2) & 1, (r >> 1) & 1, r & 1
                tx, ty, tc = (x + fx) % 2, (y + fy) % 2, (c + fc) % 2
                tq = 2 * tx + ty
                if kinds[w] == "col":
                    src = in_refs[w].at[pl.ds(tc * pr, pr), pl.ds(tq * pc, pc)]
                else:
                    src = in_refs[w].at[pl.ds((2 * tq + tc) * pr, pr), :]
                dst = out_refs[w].at[slot]
                if r == 0:
                    cp = pltpu.make_async_copy(src, dst, recv_sems.at[N_DEV * w])
                else:
                    cp = pltpu.make_async_remote_copy(
                        src_ref=src, dst_ref=dst, send_sem=send_sems.at[N_DEV * w + r], recv_sem=recv_sems.at[N_DEV * w + r],
                        device_id=(tx, ty, tc), device_id_type=MESH)
                cp.start()
                started.append((r, cp))
        for r, cp in started:
            if r == 0:
                cp.wait()
            else:
                cp.wait_recv()
        for r, cp in started:
            if r != 0:
                cp.wait_send()

    hbm = pl.BlockSpec(memory_space=pltpu.HBM)
    return pl.pallas_call(
        body,
        out_shape=[jax.ShapeDtypeStruct((N_DEV,) + p, g.dtype) for p, g in zip(pieces, grads)],
        in_specs=[hbm] * n,
        out_specs=[hbm] * n,
        scratch_shapes=[pltpu.SemaphoreType.DMA((N_DEV * n,)), pltpu.SemaphoreType.DMA((N_DEV * n,))],
        name=name,
    )(*grads)


def _swap_halves(halves, name):
    n = len(halves)

    def body(*refs):
        in_refs, out_refs = refs[:n], refs[n:2 * n]
        send_sems, recv_sems, local_sems = refs[2 * n:]
        x, y, c = _me()
        cps = []
        for w in range(n):
            lc = pltpu.make_async_copy(in_refs[w], out_refs[w].at[c], local_sems.at[w])
            lc.start()
            rc = pltpu.make_async_remote_copy(
                src_ref=in_refs[w], dst_ref=out_refs[w].at[c], send_sem=send_sems.at[w], recv_sem=recv_sems.at[w],
                device_id=(x, y, 1 - c), device_id_type=MESH)
            rc.start()
            cps.append((lc, rc))
        for lc, rc in cps:
            rc.wait_recv()
        for lc, rc in cps:
            rc.wait_send()
            lc.wait()

    vmem = pl.BlockSpec(memory_space=pltpu.VMEM)
    return pl.pallas_call(
        body,
        out_shape=[jax.ShapeDtypeStruct((2,) + h.shape, h.dtype) for h in halves],
        in_specs=[vmem] * n,
        out_specs=[vmem] * n,
        scratch_shapes=[pltpu.SemaphoreType.DMA((n,)), pltpu.SemaphoreType.DMA((n,)), pltpu.SemaphoreType.DMA((n,))],
        name=name,
        compiler_params=pltpu.CompilerParams(vmem_limit_bytes=VMEM_LIMIT),
    )(*halves)


def _to_streams(a, dil):
    if dil == 1:
        return a
    s, c = a.shape
    return a.reshape(s // dil, dil, c).transpose(1, 0, 2).reshape(s, c)


def _from_streams(a, dil):
    if dil == 1:
        return a
    s, c = a.shape
    return a.reshape(dil, s // dil, c).transpose(1, 0, 2).reshape(s, c)


def _mm_tiles(s):
    return min(s, 1024)


def _local_step(x0, target, mvec, ln_g, ln_b, small, fetch):
    s, d = x0.shape
    tm = _mm_tiles(s)
    row = lambda v: v.reshape(1, -1)
    shift = [row(mvec[i, :d]) for i in range(4)]
    scale = [row(mvec[i, d:2 * d]) for i in range(4)]
    gate = [row(1.0 + mvec[i, 2 * d:]) for i in range(4)]
    lg = [row(ln_g[i]) for i in range(4)]
    lb = [row(ln_b[i]) for i in range(4)]
    mm = functools.partial(_mm, tm=tm)
    mm_w = functools.partial(_mm, tm=1024, tk=min(s, 512), mode="tn")

    xs, ys, big = [x0], [], {}
    h0 = _mod(x0, scale[0], shift[0], "mod0")
    big["a_w_in"] = fetch("a_w_in", h0)
    uvpre = mm(h0, big["a_w_in"], mode="nn", name="a_in", outs=[F32], tn=512, tk=512,
               epi=lambda r, bias: [r + bias], extras=[("row", small["a_b_in"])])
    gated = _spatial_fwd(uvpre, small["a_vn_g"], small["a_vn_b"], small["wc"], small["bias_full"], "a_spatial")
    big["a_w_out"] = fetch("a_w_out", gated)
    ys.append(mm(gated, big["a_w_out"], mode="nn", name="a_out", outs=[F32], tn=1024, tk=512))
    x1, h1 = _resid_ln(xs[0], ys[0], gate[0], lg[0], lb[0], (scale[1], shift[1]), "ln0")
    xs.append(x1)
    relu2 = lambda r: [r, jnp.square(jnp.maximum(r, 0.0))]
    big["up0"] = fetch("up0", h1)
    a0, r0 = mm(h1, big["up0"], mode="nn", name="up0", outs=[MXU_DTYPE, MXU_DTYPE], tn=1024, tk=512, epi=relu2)
    big["down0"] = fetch("down0", r0)
    ys.append(mm(r0, big["down0"], mode="nn", name="down0", outs=[F32], tn=1024, tk=512))
    x2, h2 = _resid_ln(xs[1], ys[1], gate[1], lg[1], lb[1], (scale[2], shift[2]), "ln1")
    xs.append(x2)
    hg, qkvs, o_g, l_g = [], [], [], []
    big["b_w_qkv"] = fetch("b_w_qkv", h2)
    for g, (_, dil) in enumerate(B_PATTERNS):
        hp = _to_streams(h2, dil)
        qkv = mm(hp, big["b_w_qkv"], mode="nn", name=f"qkv{g}", outs=[MXU_DTYPE], tn=768, tk=512, b_col0=g * 3 * d, n_out=3 * d)
        og, lgv = _attn_fwd(qkv, small["slopes"], dil, f"attn_fwd{g}")
        hg.append(hp)
        qkvs.append(qkv)
        o_g.append(_from_streams(og, dil))
        l_g.append(_from_streams(lgv, dil))
    o_mix = _combine_fwd(o_g, l_g, "combine")
    big["b_w_out"] = fetch("b_w_out", o_mix)
    ys.append(mm(o_mix, big["b_w_out"], mode="nn", name="b_out", outs=[F32], tn=1024, tk=512))
    x3, h3 = _resid_ln(xs[2], ys[2], gate[2], lg[2], lb[2], (scale[3], shift[3]), "ln2")
    xs.append(x3)
    big["up1"] = fetch("up1", h3)
    a1, r1 = mm(h3, big["up1"], mode="nn", name="up1", outs=[MXU_DTYPE, MXU_DTYPE], tn=1024, tk=512, epi=relu2)
    big["down1"] = fetch("down1", r1)
    ys.append(mm(r1, big["down1"], mode="nn", name="down1", outs=[F32], tn=1024, tk=512))
    x4, _ = _resid_ln(xs[3], ys[3], gate[3], lg[3], lb[3], None, "ln3")

    gb, dm, dlg, dlb = {}, [None] * 4, [None] * 4, [None] * 4
    dx, loss = _loss_grad(x4, target, "loss")

    def mlp_bwd(i, sub, dx, h, a, r):
        dxr, dyy, red = _ln_bwd(dx, xs[sub], ys[sub], gate[sub], lg[sub], f"ln_bwd{sub}")
        gb[f"down{i}"] = mm_w(r, dyy, name=f"g_down{i}", outs=[MXU_DTYPE], tn=1024)
        da = mm(dyy, big[f"down{i}"], mode="nt", name=f"d_down{i}", outs=[MXU_DTYPE], tn=1024, tk=512,
                epi=lambda acc, av: [acc * (2.0 * jnp.maximum(av.astype(F32), 0.0))], extras=[("full", a)])
        gb[f"up{i}"] = mm_w(h, da, name=f"g_up{i}", outs=[MXU_DTYPE], tn=1024)
        dh = mm(da, big[f"up{i}"], mode="nt", name=f"d_up{i}", outs=[F32], tn=1024, tk=512)
        dx, red2 = _mod_bwd(dxr, [dh], xs[sub], scale[sub], f"mod_bwd{sub}")
        dm[sub] = jnp.concatenate([red2[0], red2[1], red[2]])
        dlg[sub], dlb[sub] = red[0], red[1]
        return dx

    dx = mlp_bwd(1, 3, dx, h3, a1, r1)
    dxr, dyy, red = _ln_bwd(dx, xs[2], ys[2], gate[2], lg[2], "ln_bwd2")
    gb["b_w_out"] = mm_w(o_mix, dyy, name="g_b_out", outs=[MXU_DTYPE], tn=1024)
    do = mm(dyy, big["b_w_out"], mode="nt", name="d_b_out", outs=[F32], tn=1024, tk=512)
    parts = _combine_bwd(do, o_mix, l_g, "combine_bwd")
    dhs, gq = [], []
    for g, (_, dil) in enumerate(B_PATTERNS):
        do_g, dd_g = _to_streams(parts[g][0], dil), _to_streams(parts[g][1], dil)
        lse_g = _to_streams(l_g[g], dil)
        dqkv = _attn_bwd(qkvs[g], do_g, lse_g, dd_g, small["slopes"], dil, f"attn_bwd{g}")
        gq.append(mm_w(hg[g], dqkv, name=f"g_qkv{g}", outs=[MXU_DTYPE], tn=1024))
        dh = mm(dqkv, big["b_w_qkv"], mode="nt", name=f"d_qkv{g}", outs=[F32], tn=1024, tk=768, b_col0=g * 3 * d)
        dhs.append(_from_streams(dh, dil))
    gb["b_w_qkv"] = jnp.concatenate(gq, axis=1)
    dx, red2 = _mod_bwd(dxr, dhs, xs[2], scale[2], "mod_bwd2")
    dm[2] = jnp.concatenate([red2[0], red2[1], red[2]])
    dlg[2], dlb[2] = red[0], red[1]
    dx = mlp_bwd(0, 1, dx, h1, a0, r0)
    dxr, dyy, red = _ln_bwd(dx, xs[0], ys[0], gate[0], lg[0], "ln_bwd0")
    gb["a_w_out"] = mm_w(gated, dyy, name="g_a_out", outs=[MXU_DTYPE], tn=1024)
    dgated = mm(dyy, big["a_w_out"], mode="nt", name="d_a_out", outs=[F32], tn=1024, tk=512)
    duv, dws, dbias, dbin, dvg, dvb = _spatial_bwd(uvpre, dgated, small["a_vn_g"], small["a_vn_b"], small["wc"],
                                                   small["wct"], small["bias_full"], "a_spatial_bwd")
    gb["a_w_in"] = mm_w(h0, duv, name="g_a_in", outs=[MXU_DTYPE], tn=1024)
    dh = mm(duv, big["a_w_in"], mode="nt", name="d_a_in", outs=[F32], tn=1024, tk=512)
    dx, red2 = _mod_bwd(dxr, [dh], xs[0], scale[0], "mod_bwd0")
    dm[0] = jnp.concatenate([red2[0], red2[1], red[2]])
    dlg[0], dlb[0] = red[0], red[1]

    tril = jnp.tril(jnp.ones((CHUNK, CHUNK), bool))
    gsmall = {
        "a_b_in": dbin.reshape(-1), "a_vn_g": dvg.reshape(-1), "a_vn_b": dvb.reshape(-1),
        "a_w_s": jnp.where(tril, dws, 0.0).reshape(-1),
        "a_b_s": dbias.reshape(CHUNK, A_GROUPS, d // A_GROUPS).sum(-1).T.reshape(-1),
    }
    return loss, dx, gb, jnp.stack(dm), jnp.stack(dlg), jnp.stack(dlb), gsmall


BIG = ("a_w_in", "a_w_out", "up0", "down0", "b_w_qkv", "b_w_out", "up1", "down1")
BIG_KIND = {"a_w_in": "col", "a_w_out": "row", "b_w_qkv": "col", "b_w_out": "row",
            "up0": "col", "up1": "col", "down0": "row", "down1": "row"}
SMALL = ("a_b_in", "a_vn_g", "a_vn_b", "a_b_s", "a_w_s")


def kernel(x, c, ada_w, ada_b, ln_g, ln_b, a_w_in, a_b_in, a_vn_g, a_vn_b, a_w_s, a_b_s, a_w_out, b_w_qkv, b_w_out, mlp_w_up, mlp_w_down, loss_target, m_ada_w, m_ada_b, m_ln_g, m_ln_b, m_a_w_in, m_a_b_in, m_a_vn_g, m_a_vn_b, m_a_w_s, m_a_b_s, m_a_w_out, m_b_w_qkv, m_b_w_out, m_mlp_w_up, m_mlp_w_down, v_ada_w, v_ada_b, v_ln_g, v_ln_b, v_a_w_in, v_a_b_in, v_a_vn_g, v_a_vn_b, v_a_w_s, v_a_b_s, v_a_w_out, v_b_w_qkv, v_b_w_out, v_mlp_w_up, v_mlp_w_down):
    s, d = x.shape[1], x.shape[2]
    xi, yi, ci = _me()
    q = 2 * xi + yi
    dev = 2 * q + ci
    nsub = 2 * DEPTH
    cs = ada_w.shape[-1]
    ls = ln_g.shape[-1]

    pack = jnp.concatenate([c.reshape(-1), ln_g.reshape(-1), ln_b.reshape(-1)]).reshape(-1, LANES)
    got = _all_gather_small(pack, "gather_small").reshape(N_DEV, -1)
    c_all = got[:, :d]
    per_chip = got[0::2]
    ln_g_full = per_chip[:, d:d + nsub * ls].reshape(N_CHIPS, nsub, ls).transpose(1, 0, 2).reshape(nsub, d)
    ln_b_full = per_chip[:, d + nsub * ls:].reshape(N_CHIPS, nsub, ls).transpose(1, 0, 2).reshape(nsub, d)
    m_part = _ada_fwd(c_all, ada_w.reshape(nsub, d, cs), ada_b.reshape(nsub, 1, cs), "ada_fwd")
    m_all = _all_gather_small(m_part.reshape(-1, LANES), "gather_mod").reshape(N_DEV, nsub, N_DEV, cs)
    m_mine = lax.dynamic_index_in_dim(m_all[0::2], dev, axis=2, keepdims=False)
    mvec = m_mine.transpose(1, 0, 2).reshape(nsub, 3 * d)

    shards = {
        "a_w_in": a_w_in[0], "a_w_out": a_w_out[0], "b_w_qkv": b_w_qkv[0], "b_w_out": b_w_out[0],
        "up0": mlp_w_up[0], "up1": mlp_w_up[1], "down0": mlp_w_down[0], "down1": mlp_w_down[1],
    }
    send_sems, recv_sems, shard_thru, lands, token = _gather_start([shards[k].astype(MXU_DTYPE) for k in BIG], "gather_start")

    def fetch(k, after):
        w = BIG.index(k)
        gw = _gather_wait(w, shard_thru[w], lands[w], send_sems, recv_sems, after, f"gather_wait_{k}")
        return gw if BIG_KIND[k] == "col" else gw.reshape(1, -1, gw.shape[-1])

    tril = jnp.tril(jnp.ones((CHUNK, CHUNK), bool))
    wc = jnp.where(tril, a_w_s[0], 0.0).astype(MXU_DTYPE)
    heads = jnp.arange(1, B_HEADS + 1, dtype=F32)
    small = {
        "a_b_in": a_b_in, "a_vn_g": a_vn_g, "a_vn_b": a_vn_b,
        "wc": wc, "wct": wc.transpose(0, 2, 1),
        "bias_full": jnp.repeat(a_b_s[0].T, d // A_GROUPS, axis=1),
        "slopes": jnp.exp2(-8.0 * heads / B_HEADS),
    }

    loss_part, grad_x, gb, dm, dlg, dlb, gsmall = _local_step(x[0] + token[0, 0], loss_target[0], mvec, ln_g_full, ln_b_full, small, fetch)
    loss = lax.psum(loss_part, ("x", "y", "c"))

    bufs = _scatter_partials([gb[k] for k in BIG], [BIG_KIND[k] for k in BIG], "scatter_partials")
    halves = [_sum_slots(b, f"sum_{k}") for k, b in zip(BIG, bufs)]
    fulls = _swap_halves(halves[:4], "swap_halves_a") + _swap_halves(halves[4:], "swap_halves_b")
    gfull = {k: f.reshape(-1, f.shape[-1]) for k, f in zip(BIG, fulls)}

    pack_b = jnp.concatenate([dm.reshape(-1), dlg.reshape(-1), dlb.reshape(-1)] + [gsmall[k] for k in SMALL])
    n_small = pack_b.shape[0]
    pack_b = jnp.pad(pack_b, (0, -n_small % (ROW_TILE * LANES)))
    got_b = _all_gather_small(pack_b.reshape(-1, LANES), "gather_small_grads").reshape(N_DEV, -1, LANES)
    tot = _sum_slots(got_b, "sum_small").reshape(-1)
    o = 0
    dm_tot = tot[o:o + nsub * 3 * d].reshape(nsub, 3 * d); o += nsub * 3 * d
    dlg_tot = tot[o:o + nsub * d].reshape(nsub, d); o += nsub * d
    dlb_tot = tot[o:o + nsub * d].reshape(nsub, d); o += nsub * d
    g_small = {}
    for k, ref in zip(SMALL, (a_b_in, a_vn_g, a_vn_b, a_b_s, a_w_s)):
        g_small[k] = tot[o:o + ref.size].reshape(ref.shape); o += ref.size
    assert o == n_small
    dm_all = got_b.reshape(N_DEV, -1)[:, :nsub * 3 * d].reshape(N_DEV, nsub, 3 * d)
    dm_cols = lax.dynamic_slice_in_dim(dm_all, q * cs, cs, axis=2).transpose(1, 0, 2)

    grads = {
        "ada_w": _ada_bwd(c_all.T, dm_cols, "ada_bwd").reshape(ada_w.shape),
        "ada_b": lax.dynamic_slice_in_dim(dm_tot, q * cs, cs, axis=1).reshape(ada_b.shape),
        "ln_g": lax.dynamic_slice_in_dim(dlg_tot, q * ls, ls, axis=1).reshape(ln_g.shape),
        "ln_b": lax.dynamic_slice_in_dim(dlb_tot, q * ls, ls, axis=1).reshape(ln_b.shape),
        "a_w_in": gfull["a_w_in"][None], "a_w_out": gfull["a_w_out"][None],
        "b_w_qkv": gfull["b_w_qkv"][None], "b_w_out": gfull["b_w_out"][None],
        "mlp_w_up": jnp.stack([gfull["up0"], gfull["up1"]]), "mlp_w_down": jnp.stack([gfull["down0"], gfull["down1"]]),
        **g_small,
    }
    weights = dict(ada_w=ada_w, ada_b=ada_b, ln_g=ln_g, ln_b=ln_b, a_w_in=a_w_in, a_b_in=a_b_in, a_vn_g=a_vn_g, a_vn_b=a_vn_b,
                   a_w_s=a_w_s, a_b_s=a_b_s, a_w_out=a_w_out, b_w_qkv=b_w_qkv, b_w_out=b_w_out, mlp_w_up=mlp_w_up, mlp_w_down=mlp_w_down)
    ms = dict(ada_w=m_ada_w, ada_b=m_ada_b, ln_g=m_ln_g, ln_b=m_ln_b, a_w_in=m_a_w_in, a_b_in=m_a_b_in, a_vn_g=m_a_vn_g, a_vn_b=m_a_vn_b,
              a_w_s=m_a_w_s, a_b_s=m_a_b_s, a_w_out=m_a_w_out, b_w_qkv=m_b_w_qkv, b_w_out=m_b_w_out, mlp_w_up=m_mlp_w_up, mlp_w_down=m_mlp_w_down)
    vs = dict(ada_w=v_ada_w, ada_b=v_ada_b, ln_g=v_ln_g, ln_b=v_ln_b, a_w_in=v_a_w_in, a_b_in=v_a_b_in, a_vn_g=v_a_vn_g, a_vn_b=v_a_vn_b,
              a_w_s=v_a_w_s, a_b_s=v_a_b_s, a_w_out=v_a_w_out, b_w_qkv=v_b_w_qkv, b_w_out=v_b_w_out, mlp_w_up=v_mlp_w_up, mlp_w_down=v_mlp_w_down)
    names = list(weights)
    deltas, new_m, new_v = [], [], []
    for k in names:
        dl, nm, nv = _adamw(weights[k], grads[k], ms[k], vs[k], f"adamw_{k}")
        deltas.append(dl)
        new_m.append(nm)
        new_v.append(nv)
    return (loss, grad_x[None], *[grads[k] for k in names], *deltas, *new_m, *new_v)
```

```python
import functools
import math

import jax
import jax.numpy as jnp
from jax import lax
from jax.experimental import pallas as pl
from jax.experimental.pallas import tpu as pltpu

F32 = jnp.float32
MXU_DTYPE = jnp.bfloat16

DEPTH = 2
CHUNK = 128
A_GROUPS = 16
B_HEADS = 16
HEAD_DIM = 64
B_PATTERNS = ((128, 1), (512, 4), (2048, 16))
SPAN = 128
ALPHA = (2 * DEPTH) ** 0.25
LN_EPS = 1e-5
NEG = -1e30
ATT_SCALE = HEAD_DIM ** -0.5
ADAM_LR, ADAM_B1, ADAM_B2, ADAM_EPS, ADAM_WD, ADAM_STEP = 0.001, 0.9, 0.999, 1e-08, 0.01, 10

N_CHIPS = 4
N_DEV = 8
LANES = 128
SUBLANES = 8
VMEM_LIMIT = 52 * 1024 * 1024
ROW_TILE = 256
MESH = pl.DeviceIdType.MESH


def _cparams(sem):
    return pltpu.CompilerParams(dimension_semantics=sem, vmem_limit_bytes=VMEM_LIMIT)


def _fold8(v):
    r, c = v.shape
    return jnp.sum(v.reshape(r // SUBLANES, SUBLANES, c), axis=0)


def _gelu(x):
    c = math.sqrt(2.0 / math.pi)
    return 0.5 * x * (1.0 + jnp.tanh(c * (x + 0.044715 * (x * x * x))))


def _gelu_grad(x):
    c = math.sqrt(2.0 / math.pi)
    t = jnp.tanh(c * (x + 0.044715 * (x * x * x)))
    return 0.5 * (1.0 + t) + 0.5 * x * (1.0 - t * t) * c * (1.0 + 3.0 * 0.044715 * x * x)


def _dot(a, b, dims):
    return lax.dot_general(a.astype(MXU_DTYPE), b.astype(MXU_DTYPE), (dims, ((), ())), preferred_element_type=F32)


def _dot_nn(a, b):
    return _dot(a, b, ((1,), (0,)))


def _dot_nt(a, b):
    return _dot(a, b, ((1,), (1,)))


def _dot_tn(a, b):
    return _dot(a, b, ((0,), (0,)))


def _mm(a, b, *, mode, name, outs, tm, tn, tk, epi=None, extras=(), b_col0=0, n_out=None):
    if mode == "nn":
        m, kdim = a.shape
        p, kb, ns = b.shape
        assert kb == kdim and ns % tn == 0 and b_col0 % tn == 0
        n = n_out if n_out is not None else p * ns
        npt, j0 = ns // tn, b_col0 // tn
        a_spec = pl.BlockSpec((tm, tk), lambda i, j, k: (i, k))
        b_spec = pl.BlockSpec((None, tk, tn), lambda i, j, k: ((j + j0) // npt, k, (j + j0) % npt))
        dot = _dot_nn
    elif mode == "nt":
        m, kdim = a.shape
        p, n, ns = b.shape
        assert ns % tk == 0 and b_col0 % tk == 0
        npt, j0 = ns // tk, b_col0 // tk
        a_spec = pl.BlockSpec((tm, tk), lambda i, j, k: (i, k))
        b_spec = pl.BlockSpec((None, tn, tk), lambda i, j, k: ((k + j0) // npt, j, (k + j0) % npt))
        dot = _dot_nt
    else:
        kdim, m = a.shape
        kb, n = b.shape
        assert kb == kdim
        a_spec = pl.BlockSpec((tk, tm), lambda i, j, k: (k, i))
        b_spec = pl.BlockSpec((tk, tn), lambda i, j, k: (k, j))
        dot = _dot_tn
    assert m % tm == 0 and n % tn == 0 and kdim % tk == 0, (name, m, n, kdim, tm, tn, tk)
    nk = kdim // tk
    ex_specs, ex_arrays = [], []
    for kind, arr in extras:
        if kind == "row":
            ex_specs.append(pl.BlockSpec((1, tn), lambda i, j, k: (0, j)))
        else:
            ex_specs.append(pl.BlockSpec((tm, tn), lambda i, j, k: (i, j)))
        ex_arrays.append(arr)
    n_ex, n_o = len(ex_arrays), len(outs)

    def body(a_ref, b_ref, *rest):
        ex_refs, o_refs, acc = rest[:n_ex], rest[n_ex:n_ex + n_o], rest[n_ex + n_o]
        k = pl.program_id(2)

        @pl.when(k == 0)
        def _():
            acc[...] = jnp.zeros_like(acc)

        acc[...] += dot(a_ref[...], b_ref[...])

        @pl.when(k == nk - 1)
        def _():
            r = acc[...]
            vals = epi(r, *[e[...] for e in ex_refs]) if epi is not None else [r]
            for o, v in zip(o_refs, vals):
                o[...] = v.astype(o.dtype)

    res = pl.pallas_call(
        body,
        grid=(m // tm, n // tn, nk),
        in_specs=[a_spec, b_spec] + ex_specs,
        out_specs=[pl.BlockSpec((tm, tn), lambda i, j, k: (i, j)) for _ in outs],
        out_shape=[jax.ShapeDtypeStruct((m, n), dt) for dt in outs],
        scratch_shapes=[pltpu.VMEM((tm, tn), F32)],
        name=name,
        compiler_params=_cparams(("parallel", "parallel", "arbitrary")),
    )(a, b, *ex_arrays)
    return res if len(outs) > 1 else res[0]


def _rows(body, n_rows, tr, ins, outs, name, scratch=()):
    def spec(kind, shape):
        if kind == "blk":
            return pl.BlockSpec((tr,) + tuple(shape[1:]), lambda i: (i,) + (0,) * (len(shape) - 1))
        return pl.BlockSpec(tuple(shape), lambda i: (0,) * len(shape))

    return pl.pallas_call(
        body,
        grid=(n_rows // tr,),
        in_specs=[spec(k, a.shape) for k, a in ins],
        out_specs=[spec(k, s) for k, s, _ in outs],
        out_shape=[jax.ShapeDtypeStruct(tuple(s), d) for _, s, d in outs],
        scratch_shapes=list(scratch),
        name=name,
        compiler_params=_cparams(("arbitrary",)),
    )(*[a for _, a in ins])


def _ln_stats(z):
    mu = jnp.mean(z, axis=-1, keepdims=True)
    zc = z - mu
    var = jnp.mean(zc * zc, axis=-1, keepdims=True)
    rstd = lax.rsqrt(var + LN_EPS)
    return zc * rstd, rstd


def _mod(x, scale, shift, name):
    s, d = x.shape

    def body(x_ref, sc_ref, sh_ref, h_ref):
        h_ref[...] = (x_ref[...] * (1.0 + sc_ref[...]) + sh_ref[...]).astype(h_ref.dtype)

    return _rows(body, s, ROW_TILE, [("blk", x), ("all", scale), ("all", shift)], [("blk", (s, d), MXU_DTYPE)], name)[0]


def _resid_ln(x, y, gate, g, b, nxt, name):
    s, d = x.shape
    ins = [("blk", x), ("blk", y), ("all", gate), ("all", g), ("all", b)]
    outs = [("blk", (s, d), F32)]
    if nxt is not None:
        ins += [("all", nxt[0]), ("all", nxt[1])]
        outs += [("blk", (s, d), MXU_DTYPE)]

    def body(x_ref, y_ref, gate_ref, g_ref, b_ref, *rest):
        z = ALPHA * x_ref[...] + gate_ref[...] * y_ref[...]
        xhat, _ = _ln_stats(z)
        xn = xhat * g_ref[...] + b_ref[...]
        if nxt is None:
            rest[0][...] = xn
        else:
            sc_ref, sh_ref, xn_ref, h_ref = rest
            xn_ref[...] = xn
            h_ref[...] = (xn * (1.0 + sc_ref[...]) + sh_ref[...]).astype(h_ref.dtype)

    res = _rows(body, s, ROW_TILE, ins, outs, name)
    return (res[0], res[1]) if nxt is not None else (res[0], None)


def _loss_grad(xf, target, name):
    s, d = xf.shape

    def body(x_ref, t_ref, dy_ref, l_ref, acc):
        i = pl.program_id(0)

        @pl.when(i == 0)
        def _():
            acc[...] = jnp.zeros_like(acc)

        e = x_ref[...] - t_ref[...]
        dy_ref[...] = e * (1.0 / d)
        acc[...] += _fold8(e * e)

        @pl.when(i == pl.num_programs(0) - 1)
        def _():
            l_ref[...] = jnp.full(l_ref.shape, 0.5 / d, F32) * jnp.sum(acc[...])

    dy, l = _rows(body, s, ROW_TILE, [("blk", xf), ("blk", target)],
                  [("blk", (s, d), F32), ("all", (SUBLANES, LANES), F32)], name,
                  scratch=[pltpu.VMEM((SUBLANES, d), F32)])
    return dy, l[0, 0]


def _ln_bwd(dxo, x, y, gate, g, name):
    s, d = x.shape

    def body(dxo_ref, x_ref, y_ref, gate_ref, g_ref, dxr_ref, dyy_ref, red_ref, a_g, a_b, a_gate):
        i = pl.program_id(0)

        @pl.when(i == 0)
        def _():
            a_g[...] = jnp.zeros_like(a_g)
            a_b[...] = jnp.zeros_like(a_b)
            a_gate[...] = jnp.zeros_like(a_gate)

        yv = y_ref[...]
        z = ALPHA * x_ref[...] + gate_ref[...] * yv
        xhat, rstd = _ln_stats(z)
        dxo_v = dxo_ref[...]
        dxh = dxo_v * g_ref[...]
        dz = rstd * (dxh - jnp.mean(dxh, axis=-1, keepdims=True) - xhat * jnp.mean(dxh * xhat, axis=-1, keepdims=True))
        dxr_ref[...] = ALPHA * dz
        dyy_ref[...] = (gate_ref[...] * dz).astype(dyy_ref.dtype)
        a_g[...] += _fold8(dxo_v * xhat)
        a_b[...] += _fold8(dxo_v)
        a_gate[...] += _fold8(dz * yv)

        @pl.when(i == pl.num_programs(0) - 1)
        def _():
            red_ref[...] = jnp.zeros_like(red_ref)
            red_ref[0:1, :] = jnp.sum(a_g[...], axis=0, keepdims=True)
            red_ref[1:2, :] = jnp.sum(a_b[...], axis=0, keepdims=True)
            red_ref[2:3, :] = jnp.sum(a_gate[...], axis=0, keepdims=True)

    return _rows(body, s, ROW_TILE, [("blk", dxo), ("blk", x), ("blk", y), ("all", gate), ("all", g)],
                 [("blk", (s, d), F32), ("blk", (s, d), MXU_DTYPE), ("all", (SUBLANES, d), F32)], name,
                 scratch=[pltpu.VMEM((SUBLANES, d), F32)] * 3)


def _mod_bwd(dxr, dhs, x, scale, name):
    s, d = x.shape
    n_dh = len(dhs)

    def body(dxr_ref, *rest):
        dh_refs = rest[:n_dh]
        x_ref, sc_ref, dx_ref, red_ref, a_sh, a_sc = rest[n_dh:]
        i = pl.program_id(0)

        @pl.when(i == 0)
        def _():
            a_sh[...] = jnp.zeros_like(a_sh)
            a_sc[...] = jnp.zeros_like(a_sc)

        dh = dh_refs[0][...]
        for r in dh_refs[1:]:
            dh = dh + r[...]
        dx_ref[...] = dxr_ref[...] + dh * (1.0 + sc_ref[...])
        a_sh[...] += _fold8(dh)
        a_sc[...] += _fold8(dh * x_ref[...])

        @pl.when(i == pl.num_programs(0) - 1)
        def _():
            red_ref[...] = jnp.zeros_like(red_ref)
            red_ref[0:1, :] = jnp.sum(a_sh[...], axis=0, keepdims=True)
            red_ref[1:2, :] = jnp.sum(a_sc[...], axis=0, keepdims=True)

    return _rows(body, s, ROW_TILE, [("blk", dxr)] + [("blk", h) for h in dhs] + [("blk", x), ("all", scale)],
                 [("blk", (s, d), F32), ("all", (SUBLANES, d), F32)], name,
                 scratch=[pltpu.VMEM((SUBLANES, d), F32)] * 2)


def _left_half(shape):
    return lax.broadcasted_iota(jnp.int32, shape, 1) < (LANES // 2)


def _spatial_z(vn, wc_ref, bias_ref, j):
    vb = vn[:, j * LANES:(j + 1) * LANES]
    z0 = _dot_nn(wc_ref[2 * j], vb)
    z1 = _dot_nn(wc_ref[2 * j + 1], vb)
    return jnp.where(_left_half(z0.shape), z0, z1) + bias_ref[:, j * LANES:(j + 1) * LANES]


def _spatial_fwd(uvpre, vn_g, vn_b, wc, bias_full, name):
    s, d2 = uvpre.shape
    d = d2 // 2

    def body(uv_ref, g_ref, b_ref, wc_ref, bias_ref, out_ref):
        u = _gelu(uv_ref[:, :d])
        v = _gelu(uv_ref[:, d:])
        vh, _ = _ln_stats(v)
        vn = vh * g_ref[...] + b_ref[...]
        for j in range(d // LANES):
            z = _spatial_z(vn, wc_ref, bias_ref, j)
            out_ref[:, j * LANES:(j + 1) * LANES] = (u[:, j * LANES:(j + 1) * LANES] * z).astype(out_ref.dtype)

    return _rows(body, s, CHUNK, [("blk", uvpre), ("all", vn_g), ("all", vn_b), ("all", wc), ("all", bias_full)],
                 [("blk", (s, d), MXU_DTYPE)], name)[0]


def _spatial_bwd(uvpre, dgated, vn_g, vn_b, wc, wct, bias_full, name):
    s, d2 = uvpre.shape
    d = d2 // 2

    def body(uv_ref, dg_ref, g_ref, b_ref, wc_ref, wct_ref, bias_ref,
             duv_ref, dws_ref, dbias_ref, dbin_ref, dvg_ref, dvb_ref, dvn_buf, a_bin, a_vg, a_vb):
        i = pl.program_id(0)

        @pl.when(i == 0)
        def _():
            dws_ref[...] = jnp.zeros_like(dws_ref)
            dbias_ref[...] = jnp.zeros_like(dbias_ref)
            a_bin[...] = jnp.zeros_like(a_bin)
            a_vg[...] = jnp.zeros_like(a_vg)
            a_vb[...] = jnp.zeros_like(a_vb)

        up = uv_ref[:, :d]
        vp = uv_ref[:, d:]
        u = _gelu(up)
        v = _gelu(vp)
        vh, rstd = _ln_stats(v)
        vn = vh * g_ref[...] + b_ref[...]
        dg = dg_ref[...]
        dzz = dg * u
        dbias_ref[...] += dzz
        for j in range(d // LANES):
            cols = slice(j * LANES, (j + 1) * LANES)
            z = _spatial_z(vn, wc_ref, bias_ref, j)
            dup = dg[:, cols] * z * _gelu_grad(up[:, cols])
            duv_ref[:, cols] = dup.astype(duv_ref.dtype)
            a_bin[:, cols] += _fold8(dup)
            dzb = dzz[:, cols]
            left = _left_half(dzb.shape)
            dvn_buf[:, cols] = jnp.where(left, _dot_nn(wct_ref[2 * j], dzb), _dot_nn(wct_ref[2 * j + 1], dzb))
            vb = vn[:, cols]
            dws_ref[2 * j] += _dot_nt(jnp.where(left, dzb, 0.0), vb)
            dws_ref[2 * j + 1] += _dot_nt(jnp.where(left, 0.0, dzb), vb)
        dvn = dvn_buf[...]
        a_vg[...] += _fold8(dvn * vh)
        a_vb[...] += _fold8(dvn)
        dvh = dvn * g_ref[...]
        dv = rstd * (dvh - jnp.mean(dvh, axis=-1, keepdims=True) - vh * jnp.mean(dvh * vh, axis=-1, keepdims=True))
        dvp = dv * _gelu_grad(vp)
        duv_ref[:, d:] = dvp.astype(duv_ref.dtype)
        a_bin[:, d:] += _fold8(dvp)

        @pl.when(i == pl.num_programs(0) - 1)
        def _():
            dbin_ref[...] = jnp.sum(a_bin[...], axis=0, keepdims=True)
            dvg_ref[...] = jnp.sum(a_vg[...], axis=0, keepdims=True)
            dvb_ref[...] = jnp.sum(a_vb[...], axis=0, keepdims=True)

    return _rows(body, s, CHUNK,
                 [("blk", uvpre), ("blk", dgated), ("all", vn_g), ("all", vn_b), ("all", wc), ("all", wct), ("all", bias_full)],
                 [("blk", (s, d2), MXU_DTYPE), ("all", (A_GROUPS, CHUNK, CHUNK), F32), ("all", (CHUNK, d), F32),
                  ("all", (1, d2), F32), ("all", (1, d), F32), ("all", (1, d), F32)], name,
                 scratch=[pltpu.VMEM((CHUNK, d), F32), pltpu.VMEM((SUBLANES, d2), F32),
                          pltpu.VMEM((SUBLANES, d), F32), pltpu.VMEM((SUBLANES, d), F32)])


def _head_mask(v, h):
    lane = lax.broadcasted_iota(jnp.int32, v.shape, 1)
    return jnp.where((lane >= h * HEAD_DIM) & (lane < (h + 1) * HEAD_DIM), v, jnp.zeros_like(v))


def _att_bias(slopes, dil):
    qi = lax.broadcasted_iota(jnp.int32, (SPAN, SPAN), 0)
    ki = lax.broadcasted_iota(jnp.int32, (SPAN, SPAN), 1)
    sl = slopes[:, None, None]
    cur = jnp.where(ki <= qi, -sl * (float(dil) * (qi - ki).astype(F32)), NEG)
    prev = jnp.where(ki >= qi, -sl * (float(dil) * (SPAN + qi - ki).astype(F32)), NEG)
    absent = jnp.full_like(prev, NEG)
    pairs = slopes.shape[0] // 2

    def fwd(pv):
        return jnp.concatenate([cur, pv], axis=2).reshape(pairs, 2 * SPAN, 2 * SPAN)

    def bwd(pv):
        return jnp.concatenate([cur.reshape(pairs, 2 * SPAN, SPAN), pv.reshape(pairs, 2 * SPAN, SPAN)], axis=1)

    return jnp.stack([fwd(absent), fwd(prev)]), jnp.stack([bwd(absent), bwd(prev)])


def _att_specs(s, d, dil, kinds):
    nb = s // (dil * SPAN)

    def rowblk(which, b):
        if which == "prev":
            return jnp.where(b % nb == 0, b, b - 1)
        if which == "next":
            return jnp.where(b % nb == nb - 1, b, b + 1)
        return b

    return [pl.BlockSpec((SPAN, d), functools.partial(lambda b, o, w: (rowblk(w, b), o), o=part, w=which))
            for part, which in kinds]


def _lane_col(v, h):
    return v[:, h * HEAD_DIM:h * HEAD_DIM + 1]


def _attn_fwd(qkv, slopes, dil, name):
    s, d3 = qkv.shape
    d = d3 // 3
    nb = s // (dil * SPAN)
    table, _ = _att_bias(slopes, dil)

    def body(q_ref, kc_ref, kp_ref, vc_ref, vp_ref, tb_ref, o_ref, l_ref):
        left = _left_half((SPAN, LANES))
        for hp in range(d // LANES):
            cols = slice(hp * LANES, (hp + 1) * LANES)
            q = q_ref[:, cols]
            q2 = jnp.concatenate([_head_mask(q, 0), _head_mask(q, 1)], axis=0) * ATT_SCALE
            k2 = jnp.concatenate([kc_ref[:, cols], kp_ref[:, cols]], axis=0)
            v2 = jnp.concatenate([vc_ref[:, cols], vp_ref[:, cols]], axis=0)
            sc = _dot_nt(q2, k2) + tb_ref[hp]
            m = jnp.max(sc, axis=-1, keepdims=True)
            p = jnp.exp(sc - m)
            l = jnp.sum(p, axis=-1, keepdims=True)
            r = _dot_nn(p, v2) * (1.0 / l)
            lse = jnp.broadcast_to(m + jnp.log(l), (2 * SPAN, LANES))
            o_ref[:, cols] = jnp.where(left, r[:SPAN], r[SPAN:])
            l_ref[:, cols] = jnp.where(left, lse[:SPAN], lse[SPAN:])

    specs = _att_specs(s, d, dil, [(0, "cur"), (1, "cur"), (1, "prev"), (2, "cur"), (2, "prev")])
    tbl = pl.BlockSpec((None,) + table.shape[1:], lambda b: (jnp.where(b % nb == 0, 0, 1), 0, 0, 0))
    out_spec = pl.BlockSpec((SPAN, d), lambda b: (b, 0))
    return pl.pallas_call(
        body,
        grid=(s // SPAN,),
        in_specs=specs + [tbl],
        out_specs=[out_spec, out_spec],
        out_shape=[jax.ShapeDtypeStruct((s, d), F32)] * 2,
        name=name,
        compiler_params=_cparams(("parallel",)),
    )(qkv, qkv, qkv, qkv, qkv, table)


def _attn_bwd(qkv, do, lse, dd, slopes, dil, name):
    s, d3 = qkv.shape
    d = d3 // 3
    nb = s // (dil * SPAN)
    _, table = _att_bias(slopes, dil)

    def heads_stacked(cur, nxt):
        return jnp.concatenate([_head_mask(cur, 0), _head_mask(cur, 1), _head_mask(nxt, 0), _head_mask(nxt, 1)], axis=0)

    def cols_stacked(cur, nxt):
        return jnp.concatenate([jnp.broadcast_to(_lane_col(a, h), (SPAN, LANES)) for a in (cur, nxt) for h in range(2)], axis=0)

    def body(k_ref, v_ref, qc_ref, qn_ref, doc_ref, don_ref, lc_ref, ln_ref, ddc_ref, ddn_ref, tb_ref, out_ref, carry):
        b = pl.program_id(0)

        @pl.when(b == 0)
        def _():
            carry[...] = jnp.zeros_like(carry)

        left = _left_half((SPAN, LANES))
        for hp in range(d // LANES):
            cols = slice(hp * LANES, (hp + 1) * LANES)
            k, v = k_ref[:, cols], v_ref[:, cols]
            q4 = heads_stacked(qc_ref[:, cols], qn_ref[:, cols])
            do4 = heads_stacked(doc_ref[:, cols], don_ref[:, cols])
            sc = _dot_nt(q4 * ATT_SCALE, k) + tb_ref[hp]
            p = jnp.exp(sc - cols_stacked(lc_ref[:, cols], ln_ref[:, cols]))
            ds = p * (_dot_nt(do4, v) - cols_stacked(ddc_ref[:, cols], ddn_ref[:, cols]))
            dq4 = _dot_nn(ds, k)
            dq_cur = jnp.where(left, dq4[:SPAN], dq4[SPAN:2 * SPAN]) + carry[:, cols]
            carry[:, cols] = jnp.where(left, dq4[2 * SPAN:3 * SPAN], dq4[3 * SPAN:])
            out_ref[:, cols] = (dq_cur * ATT_SCALE).astype(out_ref.dtype)
            out_ref[:, d + hp * LANES:d + (hp + 1) * LANES] = (_dot_tn(ds, q4) * ATT_SCALE).astype(out_ref.dtype)
            out_ref[:, 2 * d + hp * LANES:2 * d + (hp + 1) * LANES] = _dot_tn(p, do4).astype(out_ref.dtype)

    qkv_specs = _att_specs(s, d, dil, [(1, "cur"), (2, "cur"), (0, "cur"), (0, "next")])
    pair = _att_specs(s, d, dil, [(0, "cur"), (0, "next")])
    tbl = pl.BlockSpec((None,) + table.shape[1:], lambda b: (jnp.where(b % nb == nb - 1, 0, 1), 0, 0, 0))
    return pl.pallas_call(
        body,
        grid=(s // SPAN,),
        in_specs=qkv_specs + pair + pair + pair + [tbl],
        out_specs=pl.BlockSpec((SPAN, d3), lambda b: (b, 0)),
        out_shape=jax.ShapeDtypeStruct((s, d3), MXU_DTYPE),
        scratch_shapes=[pltpu.VMEM((SPAN, d), F32)],
        name=name,
        compiler_params=_cparams(("arbitrary",)),
    )(qkv, qkv, qkv, qkv, do, do, lse, lse, dd, dd, table)


def _mix_weights(l_refs):
    ls = [r[...] for r in l_refs]
    m = functools.reduce(jnp.maximum, ls)
    es = [jnp.exp(l - m) for l in ls]
    tot = functools.reduce(lambda a, c: a + c, es)
    return [e / tot for e in es]


def _combine_fwd(os_, ls_, name):
    s, d = os_[0].shape
    n = len(os_)

    def body(*refs):
        o_refs, l_refs, out_ref = refs[:n], refs[n:2 * n], refs[2 * n]
        ws = _mix_weights(l_refs)
        acc = ws[0] * o_refs[0][...]
        for w, o in zip(ws[1:], o_refs[1:]):
            acc = acc + w * o[...]
        out_ref[...] = acc

    return _rows(body, s, ROW_TILE, [("blk", a) for a in os_ + ls_], [("blk", (s, d), F32)], name)[0]


def _combine_bwd(do, o, ls_, name):
    s, d = o.shape
    n = len(ls_)
    ri = lax.broadcasted_iota(jnp.int32, (LANES, LANES), 0) // HEAD_DIM
    ci = lax.broadcasted_iota(jnp.int32, (LANES, LANES), 1) // HEAD_DIM
    seg = (ri == ci).astype(F32)

    def body(do_ref, o_ref, *rest):
        l_refs, seg_ref, outs = rest[:n], rest[n], rest[n + 1:]
        ws = _mix_weights(l_refs)
        dov = do_ref[...]
        prod = dov * o_ref[...]
        for j in range(d // LANES):
            cols = slice(j * LANES, (j + 1) * LANES)
            r = jnp.dot(prod[:, cols], seg_ref[...], precision=lax.Precision.HIGHEST, preferred_element_type=F32)
            for g in range(n):
                outs[2 * g][:, cols] = (ws[g][:, cols] * dov[:, cols]).astype(outs[2 * g].dtype)
                outs[2 * g + 1][:, cols] = ws[g][:, cols] * r

    outs = []
    for _ in range(n):
        outs += [("blk", (s, d), MXU_DTYPE), ("blk", (s, d), F32)]
    res = _rows(body, s, ROW_TILE, [("blk", do), ("blk", o)] + [("blk", l) for l in ls_] + [("all", seg)], outs, name)
    return [(res[2 * g], res[2 * g + 1]) for g in range(n)]


def _ada_fwd(c_all, w, b, name):
    nsub, d, cs = w.shape

    def body(c_ref, w_ref, b_ref, o_ref):
        cv = c_ref[...]
        sc = cv * (1.0 / (1.0 + jnp.exp(-cv)))
        o_ref[...] = _dot_nn(sc, w_ref[...]) + b_ref[...]

    return pl.pallas_call(
        body,
        grid=(nsub,),
        in_specs=[pl.BlockSpec(c_all.shape, lambda i: (0, 0)), pl.BlockSpec((None, d, cs), lambda i: (i, 0, 0)),
                  pl.BlockSpec((None, 1, cs), lambda i: (i, 0, 0))],
        out_specs=pl.BlockSpec((None, N_DEV, cs), lambda i: (i, 0, 0)),
        out_shape=jax.ShapeDtypeStruct((nsub, N_DEV, cs), F32),
        name=name,
        compiler_params=_cparams(("parallel",)),
    )(c_all, w, b)


def _ada_bwd(c_all_t, dm, name):
    d, nb = c_all_t.shape
    nsub, _, cs = dm.shape

    def body(c_ref, dm_ref, o_ref):
        cv = c_ref[...]
        sc = cv * (1.0 / (1.0 + jnp.exp(-cv)))
        acc = sc[:, 0:1] * dm_ref[0:1, :]
        for bi in range(1, nb):
            acc = acc + sc[:, bi:bi + 1] * dm_ref[bi:bi + 1, :]
        o_ref[...] = acc

    return pl.pallas_call(
        body,
        grid=(nsub,),
        in_specs=[pl.BlockSpec(c_all_t.shape, lambda i: (0, 0)), pl.BlockSpec((None, nb, cs), lambda i: (i, 0, 0))],
        out_specs=pl.BlockSpec((None, d, cs), lambda i: (i, 0, 0)),
        out_shape=jax.ShapeDtypeStruct((nsub, d, cs), F32),
        name=name,
        compiler_params=_cparams(("parallel",)),
    )(c_all_t, dm)


def _row_tile(r, row_elems):
    t = 2 * SUBLANES
    if r % t:
        return r
    while t * 2 * row_elems <= 256 * 1024 and r % (t * 2) == 0:
        t *= 2
    return t


def _adamw(w, g, m, v, name):
    shape = w.shape
    c = shape[-1]
    r = w.size // c
    tr = _row_tile(r, c)
    w2, g2, m2, v2 = [a.reshape(r, c) for a in (w, g, m, v)]
    bc1 = 1.0 - ADAM_B1 ** ADAM_STEP
    bc2 = 1.0 - ADAM_B2 ** ADAM_STEP

    def body(w_ref, g_ref, m_ref, v_ref, d_ref, nm_ref, nv_ref):
        gv = g_ref[...]
        nm = ADAM_B1 * m_ref[...] + (1.0 - ADAM_B1) * gv
        nv = ADAM_B2 * v_ref[...] + (1.0 - ADAM_B2) * (gv * gv)
        d_ref[...] = -ADAM_LR * ((nm / bc1) / (jnp.sqrt(nv / bc2) + ADAM_EPS) + ADAM_WD * w_ref[...])
        nm_ref[...] = nm
        nv_ref[...] = nv

    res = _rows(body, r, tr, [("blk", a) for a in (w2, g2, m2, v2)], [("blk", (r, c), F32)] * 3, name)
    return [a.reshape(shape) for a in res]


def _sum_slots(buf, name):
    n, r, c = buf.shape
    tr = _row_tile(r, n * c)

    def body(b_ref, o_ref):
        acc = b_ref[0].astype(F32)
        for k in range(1, n):
            acc = acc + b_ref[k].astype(F32)
        o_ref[...] = acc

    return pl.pallas_call(
        body,
        grid=(r // tr,),
        in_specs=[pl.BlockSpec((n, tr, c), lambda i: (0, i, 0))],
        out_specs=pl.BlockSpec((tr, c), lambda i: (i, 0)),
        out_shape=jax.ShapeDtypeStruct((r, c), F32),
        name=name,
        compiler_params=_cparams(("parallel",)),
    )(buf)


def _me():
    return lax.axis_index("x"), lax.axis_index("y"), lax.axis_index("c")


def _all_gather_small(blk, name):
    m_per, n = blk.shape

    def body(x_ref, out_ref, send_sems, recv_sems, local_sem):
        x, y, c = _me()
        me, sibling = (x, y, c), (x, y, 1 - c)
        chips = [(1 - x, y), (x, 1 - y), (1 - x, 1 - y)]

        def rows(px, py, pc):
            return out_ref.at[pl.ds((4 * px + 2 * py + pc) * m_per, m_per), :]

        def copy(k, block, to, src=None):
            return pltpu.make_async_remote_copy(
                src_ref=rows(*block) if src is None else src, dst_ref=rows(*block),
                send_sem=send_sems.at[k], recv_sem=recv_sems.at[k], device_id=to, device_id_type=MESH)

        mine = pltpu.make_async_copy(x_ref, rows(*me), local_sem)
        mine.start()
        first = [copy(0, me, sibling, src=x_ref)]
        first += [copy(1 + j, me, (*chip, c), src=x_ref) for j, chip in enumerate(chips)]
        for cp in first:
            cp.start()
        passed = [copy(4 + j, (*chip, c), sibling) for j, chip in enumerate(chips)]
        for j, chip in enumerate(chips):
            copy(1 + j, (*chip, c), me).wait_recv()
            passed[j].start()
        copy(0, sibling, me).wait_recv()
        for j, chip in enumerate(chips):
            copy(4 + j, (*chip, 1 - c), me).wait_recv()
        for cp in first + passed:
            cp.wait_send()
        mine.wait()

    return pl.pallas_call(
        body,
        out_shape=jax.ShapeDtypeStruct((N_DEV * m_per, n), blk.dtype),
        in_specs=[pl.BlockSpec(memory_space=pltpu.VMEM)],
        out_specs=pl.BlockSpec(memory_space=pltpu.VMEM),
        scratch_shapes=[pltpu.SemaphoreType.DMA((7,)), pltpu.SemaphoreType.DMA((7,)), pltpu.SemaphoreType.DMA],
        name=name,
        compiler_params=pltpu.CompilerParams(vmem_limit_bytes=VMEM_LIMIT),
    )(blk)


_HBM = pl.BlockSpec(memory_space=pltpu.HBM)
_SEM = pl.BlockSpec(memory_space=pltpu.SEMAPHORE)
_EFFECT = pltpu.SideEffectType.DATAFLOW_SIDE_EFFECTING


def _other_chips(x, y):
    return [(1 - x, y), (x, 1 - y), (1 - x, 1 - y)]


def _gather_copy(w, j, src_ref, land_ref, send_sems, recv_sems):
    x, y, c = _me()
    return pltpu.make_async_remote_copy(
        src_ref=src_ref, dst_ref=land_ref.at[2 * x + y], send_sem=send_sems.at[3 * w + j], recv_sem=recv_sems.at[3 * w + j],
        device_id=(*_other_chips(x, y)[j], c), device_id_type=MESH)


def _gather_start(shards, name):
    n = len(shards)
    lands = [lax.empty((N_CHIPS,) + s.shape, s.dtype) for s in shards]

    def body(*refs):
        in_refs, land_refs = refs[:n], refs[n:2 * n]
        send_sems, recv_sems = refs[2 * n], refs[2 * n + 1]
        token = refs[-1]
        for w in range(n):
            for j in range(3):
                _gather_copy(w, j, in_refs[w], land_refs[w], send_sems, recv_sems).start()
        token[...] = jnp.zeros_like(token)

    res = pl.pallas_call(
        body,
        out_shape=(pltpu.SemaphoreType.DMA((3 * n,)), pltpu.SemaphoreType.DMA((3 * n,)),
                   *[pltpu.HBM(s.shape, s.dtype) for s in shards], *[pltpu.HBM(l.shape, l.dtype) for l in lands],
                   jax.ShapeDtypeStruct((SUBLANES, LANES), F32)),
        in_specs=[_HBM] * (2 * n),
        out_specs=(_SEM, _SEM, *[_HBM] * (2 * n), pl.BlockSpec(memory_space=pltpu.VMEM)),
        input_output_aliases={i: 2 + i for i in range(2 * n)},
        name=name,
        compiler_params=pltpu.CompilerParams(has_side_effects=_EFFECT),
    )(*[pltpu.with_memory_space_constraint(a, pltpu.HBM) for a in list(shards) + lands])
    return res[0], res[1], res[2:2 + n], res[2 + n:2 + 2 * n], res[-1]


def _gather_wait(w, shard, land, send_sems, recv_sems, after, name):
    def body(s_ref, land_ref, send_sems, recv_sems, after_ref, s_out, land_out, stage):
        x, y, _ = _me()
        pltpu.sync_copy(s_ref, stage)
        pltpu.sync_copy(stage, land_out.at[2 * x + y])
        for j in range(3):
            cp = _gather_copy(w, j, s_ref, land_ref, send_sems, recv_sems)
            cp.wait_send()
            cp.wait_recv()

    return pl.pallas_call(
        body,
        out_shape=(pltpu.HBM(shard.shape, shard.dtype), pltpu.HBM(land.shape, land.dtype)),
        in_specs=(_HBM, _HBM, _SEM, _SEM, pl.BlockSpec(memory_space=pl.ANY)),
        out_specs=(_HBM, _HBM),
        input_output_aliases={0: 0, 1: 1},
        scratch_shapes=[pltpu.VMEM(shard.shape, shard.dtype)],
        name=name,
        compiler_params=pltpu.CompilerParams(has_side_effects=_EFFECT, vmem_limit_bytes=VMEM_LIMIT),
    )(shard, land, send_sems, recv_sems, after)[1]


def _piece_shape(shape, kind):
    k, nn = shape
    return (k // 2, nn // N_CHIPS) if kind == "col" else (k // N_CHIPS // 2, nn)


def _piece_of(g_ref, kind, tq, tc):
    pr, pc = _piece_shape(g_ref.shape, kind)
    if kind == "col":
        return g_ref.at[pl.ds(tc * pr, pr), pl.ds(tq * pc, pc)]
    return g_ref.at[pl.ds((2 * tq + tc) * pr, pr), :]


def _scatter_copy(r, kind, g_ref, land_ref, send_sems, recv_sems):
    x, y, c = _me()
    tx, ty, tc = (x + ((r >> 2) & 1)) % 2, (y + ((r >> 1) & 1)) % 2, (c + (r & 1)) % 2
    return pltpu.make_async_remote_copy(
        src_ref=_piece_of(g_ref, kind, 2 * tx + ty, tc), dst_ref=land_ref.at[4 * x + 2 * y + c],
        send_sem=send_sems.at[r], recv_sem=recv_sems.at[r], device_id=(tx, ty, tc), device_id_type=MESH)


def _scatter_start(g, kind, name):
    piece = _piece_shape(g.shape, kind)
    land = lax.empty((N_DEV,) + piece, g.dtype)

    def body(g_ref, land_ref, send_sems, recv_sems, g_out, land_out, stage):
        x, y, c = _me()
        for r in range(1, N_DEV):
            _scatter_copy(r, kind, g_ref, land_ref, send_sems, recv_sems).start()
        pltpu.sync_copy(_piece_of(g_ref, kind, 2 * x + y, c), stage)
        pltpu.sync_copy(stage, land_out.at[4 * x + 2 * y + c])

    return pl.pallas_call(
        body,
        out_shape=(pltpu.SemaphoreType.DMA((N_DEV,)), pltpu.SemaphoreType.DMA((N_DEV,)),
                   pltpu.HBM(g.shape, g.dtype), pltpu.HBM(land.shape, land.dtype)),
        in_specs=[_HBM, _HBM],
        out_specs=(_SEM, _SEM, _HBM, _HBM),
        input_output_aliases={0: 2, 1: 3},
        scratch_shapes=[pltpu.VMEM(piece, g.dtype)],
        name=name,
        compiler_params=pltpu.CompilerParams(has_side_effects=_EFFECT, vmem_limit_bytes=VMEM_LIMIT),
    )(pltpu.with_memory_space_constraint(g, pltpu.HBM), pltpu.with_memory_space_constraint(land, pltpu.HBM))


def _scatter_wait(send_sems, recv_sems, g, land, kind, after, name):
    def body(g_ref, land_ref, send_sems, recv_sems, after_ref, g_out, land_out):
        for r in range(1, N_DEV):
            cp = _scatter_copy(r, kind, g_ref, land_ref, send_sems, recv_sems)
            cp.wait_send()
            cp.wait_recv()

    return pl.pallas_call(
        body,
        out_shape=(pltpu.HBM(g.shape, g.dtype), pltpu.HBM(land.shape, land.dtype)),
        in_specs=(_HBM, _HBM, _SEM, _SEM, pl.BlockSpec(memory_space=pl.ANY)),
        out_specs=(_HBM, _HBM),
        input_output_aliases={0: 0, 1: 1},
        name=name,
        compiler_params=pltpu.CompilerParams(has_side_effects=_EFFECT),
    )(g, land, send_sems, recv_sems, after)[1]


def _swap_halves(halves, name):
    n = len(halves)

    def body(*refs):
        in_refs, out_refs = refs[:n], refs[n:2 * n]
        send_sems, recv_sems, local_sems = refs[2 * n:]
        x, y, c = _me()
        cps = []
        for w in range(n):
            lc = pltpu.make_async_copy(in_refs[w], out_refs[w].at[c], local_sems.at[w])
            lc.start()
            rc = pltpu.make_async_remote_copy(
                src_ref=in_refs[w], dst_ref=out_refs[w].at[c], send_sem=send_sems.at[w], recv_sem=recv_sems.at[w],
                device_id=(x, y, 1 - c), device_id_type=MESH)
            rc.start()
            cps.append((lc, rc))
        for lc, rc in cps:
            rc.wait_recv()
        for lc, rc in cps:
            rc.wait_send()
            lc.wait()

    vmem = pl.BlockSpec(memory_space=pltpu.VMEM)
    return pl.pallas_call(
        body,
        out_shape=[jax.ShapeDtypeStruct((2,) + h.shape, h.dtype) for h in halves],
        in_specs=[vmem] * n,
        out_specs=[vmem] * n,
        scratch_shapes=[pltpu.SemaphoreType.DMA((n,)), pltpu.SemaphoreType.DMA((n,)), pltpu.SemaphoreType.DMA((n,))],
        name=name,
        compiler_params=pltpu.CompilerParams(vmem_limit_bytes=VMEM_LIMIT),
    )(*halves)


def _to_streams(a, dil):
    if dil == 1:
        return a
    s, c = a.shape
    return a.reshape(s // dil, dil, c).transpose(1, 0, 2).reshape(s, c)


def _from_streams(a, dil):
    if dil == 1:
        return a
    s, c = a.shape
    return a.reshape(dil, s // dil, c).transpose(1, 0, 2).reshape(s, c)


def _mm_tiles(s):
    return min(s, 1024)


def _local_step(x0, target, mvec, ln_g, ln_b, small, fetch, emit):
    s, d = x0.shape
    tm = _mm_tiles(s)
    row = lambda v: v.reshape(1, -1)
    shift = [row(mvec[i, :d]) for i in range(4)]
    scale = [row(mvec[i, d:2 * d]) for i in range(4)]
    gate = [row(1.0 + mvec[i, 2 * d:]) for i in range(4)]
    lg = [row(ln_g[i]) for i in range(4)]
    lb = [row(ln_b[i]) for i in range(4)]
    mm = functools.partial(_mm, tm=tm)
    mm_w = functools.partial(_mm, tm=1024, tk=min(s, 512), mode="tn")

    xs, ys, big = [x0], [], {}
    h0 = _mod(x0, scale[0], shift[0], "mod0")
    big["a_w_in"] = fetch("a_w_in", h0)
    uvpre = mm(h0, big["a_w_in"], mode="nn", name="a_in", outs=[F32], tn=512, tk=512,
               epi=lambda r, bias: [r + bias], extras=[("row", small["a_b_in"])])
    gated = _spatial_fwd(uvpre, small["a_vn_g"], small["a_vn_b"], small["wc"], small["bias_full"], "a_spatial")
    big["a_w_out"] = fetch("a_w_out", gated)
    ys.append(mm(gated, big["a_w_out"], mode="nn", name="a_out", outs=[F32], tn=1024, tk=512))
    x1, h1 = _resid_ln(xs[0], ys[0], gate[0], lg[0], lb[0], (scale[1], shift[1]), "ln0")
    xs.append(x1)
    relu2 = lambda r: [r, jnp.square(jnp.maximum(r, 0.0))]
    big["up0"] = fetch("up0", h1)
    a0, r0 = mm(h1, big["up0"], mode="nn", name="up0", outs=[MXU_DTYPE, MXU_DTYPE], tn=1024, tk=512, epi=relu2)
    big["down0"] = fetch("down0", r0)
    ys.append(mm(r0, big["down0"], mode="nn", name="down0", outs=[F32], tn=1024, tk=512))
    x2, h2 = _resid_ln(xs[1], ys[1], gate[1], lg[1], lb[1], (scale[2], shift[2]), "ln1")
    xs.append(x2)
    hg, qkvs, o_g, l_g = [], [], [], []
    big["b_w_qkv"] = fetch("b_w_qkv", h2)
    for g, (_, dil) in enumerate(B_PATTERNS):
        hp = _to_streams(h2, dil)
        qkv = mm(hp, big["b_w_qkv"], mode="nn", name=f"qkv{g}", outs=[MXU_DTYPE], tn=768, tk=512, b_col0=g * 3 * d, n_out=3 * d)
        og, lgv = _attn_fwd(qkv, small["slopes"], dil, f"attn_fwd{g}")
        hg.append(hp)
        qkvs.append(qkv)
        o_g.append(_from_streams(og, dil))
        l_g.append(_from_streams(lgv, dil))
    o_mix = _combine_fwd(o_g, l_g, "combine")
    big["b_w_out"] = fetch("b_w_out", o_mix)
    ys.append(mm(o_mix, big["b_w_out"], mode="nn", name="b_out", outs=[F32], tn=1024, tk=512))
    x3, h3 = _resid_ln(xs[2], ys[2], gate[2], lg[2], lb[2], (scale[3], shift[3]), "ln2")
    xs.append(x3)
    big["up1"] = fetch("up1", h3)
    a1, r1 = mm(h3, big["up1"], mode="nn", name="up1", outs=[MXU_DTYPE, MXU_DTYPE], tn=1024, tk=512, epi=relu2)
    big["down1"] = fetch("down1", r1)
    ys.append(mm(r1, big["down1"], mode="nn", name="down1", outs=[F32], tn=1024, tk=512))
    x4, _ = _resid_ln(xs[3], ys[3], gate[3], lg[3], lb[3], None, "ln3")

    gb, dm, dlg, dlb = {}, [None] * 4, [None] * 4, [None] * 4
    dx, loss = _loss_grad(x4, target, "loss")

    def mlp_bwd(i, sub, dx, h, a, r):
        dxr, dyy, red = _ln_bwd(dx, xs[sub], ys[sub], gate[sub], lg[sub], f"ln_bwd{sub}")
        gb[f"down{i}"] = emit(f"down{i}", mm_w(r, dyy, name=f"g_down{i}", outs=[MXU_DTYPE], tn=1024))
        da = mm(dyy, big[f"down{i}"], mode="nt", name=f"d_down{i}", outs=[MXU_DTYPE], tn=1024, tk=512,
                epi=lambda acc, av: [acc * (2.0 * jnp.maximum(av.astype(F32), 0.0))], extras=[("full", a)])
        gb[f"up{i}"] = emit(f"up{i}", mm_w(h, da, name=f"g_up{i}", outs=[MXU_DTYPE], tn=1024))
        dh = mm(da, big[f"up{i}"], mode="nt", name=f"d_up{i}", outs=[F32], tn=1024, tk=512)
        dx, red2 = _mod_bwd(dxr, [dh], xs[sub], scale[sub], f"mod_bwd{sub}")
        dm[sub] = jnp.concatenate([red2[0], red2[1], red[2]])
        dlg[sub], dlb[sub] = red[0], red[1]
        return dx

    dx = mlp_bwd(1, 3, dx, h3, a1, r1)
    dxr, dyy, red = _ln_bwd(dx, xs[2], ys[2], gate[2], lg[2], "ln_bwd2")
    gb["b_w_out"] = emit("b_w_out", mm_w(o_mix, dyy, name="g_b_out", outs=[MXU_DTYPE], tn=1024))
    do = mm(dyy, big["b_w_out"], mode="nt", name="d_b_out", outs=[F32], tn=1024, tk=512)
    parts = _combine_bwd(do, o_mix, l_g, "combine_bwd")
    dhs, gq = [], []
    for g, (_, dil) in enumerate(B_PATTERNS):
        do_g, dd_g = _to_streams(parts[g][0], dil), _to_streams(parts[g][1], dil)
        lse_g = _to_streams(l_g[g], dil)
        dqkv = _attn_bwd(qkvs[g], do_g, lse_g, dd_g, small["slopes"], dil, f"attn_bwd{g}")
        gq.append(mm_w(hg[g], dqkv, name=f"g_qkv{g}", outs=[MXU_DTYPE], tn=1024))
        dh = mm(dqkv, big["b_w_qkv"], mode="nt", name=f"d_qkv{g}", outs=[F32], tn=1024, tk=768, b_col0=g * 3 * d)
        dhs.append(_from_streams(dh, dil))
    gb["b_w_qkv"] = emit("b_w_qkv", jnp.concatenate(gq, axis=1))
    dx, red2 = _mod_bwd(dxr, dhs, xs[2], scale[2], "mod_bwd2")
    dm[2] = jnp.concatenate([red2[0], red2[1], red[2]])
    dlg[2], dlb[2] = red[0], red[1]
    dx = mlp_bwd(0, 1, dx, h1, a0, r0)
    dxr, dyy, red = _ln_bwd(dx, xs[0], ys[0], gate[0], lg[0], "ln_bwd0")
    gb["a_w_out"] = emit("a_w_out", mm_w(gated, dyy, name="g_a_out", outs=[MXU_DTYPE], tn=1024))
    dgated = mm(dyy, big["a_w_out"], mode="nt", name="d_a_out", outs=[F32], tn=1024, tk=512)
    duv, dws, dbias, dbin, dvg, dvb = _spatial_bwd(uvpre, dgated, small["a_vn_g"], small["a_vn_b"], small["wc"],
                                                   small["wct"], small["bias_full"], "a_spatial_bwd")
    gb["a_w_in"] = emit("a_w_in", mm_w(h0, duv, name="g_a_in", outs=[MXU_DTYPE], tn=1024))
    dh = mm(duv, big["a_w_in"], mode="nt", name="d_a_in", outs=[F32], tn=1024, tk=512)
    dx, red2 = _mod_bwd(dxr, [dh], xs[0], scale[0], "mod_bwd0")
    dm[0] = jnp.concatenate([red2[0], red2[1], red[2]])
    dlg[0], dlb[0] = red[0], red[1]

    tril = jnp.tril(jnp.ones((CHUNK, CHUNK), bool))
    gsmall = {
        "a_b_in": dbin.reshape(-1), "a_vn_g": dvg.reshape(-1), "a_vn_b": dvb.reshape(-1),
        "a_w_s": jnp.where(tril, dws, 0.0).reshape(-1),
        "a_b_s": dbias.reshape(CHUNK, A_GROUPS, d // A_GROUPS).sum(-1).T.reshape(-1),
    }
    return loss, dx, gb, jnp.stack(dm), jnp.stack(dlg), jnp.stack(dlb), gsmall


BIG = ("a_w_in", "a_w_out", "up0", "down0", "b_w_qkv", "b_w_out", "up1", "down1")
BIG_KIND = {"a_w_in": "col", "a_w_out": "row", "b_w_qkv": "col", "b_w_out": "row",
            "up0": "col", "up1": "col", "down0": "row", "down1": "row"}
SMALL = ("a_b_in", "a_vn_g", "a_vn_b", "a_b_s", "a_w_s")


def kernel(x, c, ada_w, ada_b, ln_g, ln_b, a_w_in, a_b_in, a_vn_g, a_vn_b, a_w_s, a_b_s, a_w_out, b_w_qkv, b_w_out, mlp_w_up, mlp_w_down, loss_target, m_ada_w, m_ada_b, m_ln_g, m_ln_b, m_a_w_in, m_a_b_in, m_a_vn_g, m_a_vn_b, m_a_w_s, m_a_b_s, m_a_w_out, m_b_w_qkv, m_b_w_out, m_mlp_w_up, m_mlp_w_down, v_ada_w, v_ada_b, v_ln_g, v_ln_b, v_a_w_in, v_a_b_in, v_a_vn_g, v_a_vn_b, v_a_w_s, v_a_b_s, v_a_w_out, v_b_w_qkv, v_b_w_out, v_mlp_w_up, v_mlp_w_down):
    s, d = x.shape[1], x.shape[2]
    xi, yi, ci = _me()
    q = 2 * xi + yi
    dev = 2 * q + ci
    nsub = 2 * DEPTH
    cs = ada_w.shape[-1]
    ls = ln_g.shape[-1]

    pack = jnp.concatenate([c.reshape(-1), ln_g.reshape(-1), ln_b.reshape(-1)]).reshape(-1, LANES)
    got = _all_gather_small(pack, "gather_small").reshape(N_DEV, -1)
    c_all = got[:, :d]
    per_chip = got[0::2]
    ln_g_full = per_chip[:, d:d + nsub * ls].reshape(N_CHIPS, nsub, ls).transpose(1, 0, 2).reshape(nsub, d)
    ln_b_full = per_chip[:, d + nsub * ls:].reshape(N_CHIPS, nsub, ls).transpose(1, 0, 2).reshape(nsub, d)
    m_part = _ada_fwd(c_all, ada_w.reshape(nsub, d, cs), ada_b.reshape(nsub, 1, cs), "ada_fwd")
    m_all = _all_gather_small(m_part.reshape(-1, LANES), "gather_mod").reshape(N_DEV, nsub, N_DEV, cs)
    m_mine = lax.dynamic_index_in_dim(m_all[0::2], dev, axis=2, keepdims=False)
    mvec = m_mine.transpose(1, 0, 2).reshape(nsub, 3 * d)

    shards = {
        "a_w_in": a_w_in[0], "a_w_out": a_w_out[0], "b_w_qkv": b_w_qkv[0], "b_w_out": b_w_out[0],
        "up0": mlp_w_up[0], "up1": mlp_w_up[1], "down0": mlp_w_down[0], "down1": mlp_w_down[1],
    }
    send_sems, recv_sems, shard_thru, lands, token = _gather_start([shards[k].astype(MXU_DTYPE) for k in BIG], "gather_start")

    def fetch(k, after):
        w = BIG.index(k)
        gw = _gather_wait(w, shard_thru[w], lands[w], send_sems, recv_sems, after, f"gather_wait_{k}")
        return gw if BIG_KIND[k] == "col" else gw.reshape(1, -1, gw.shape[-1])

    scattering = {}

    def emit(k, g):
        scattering[k] = _scatter_start(g, BIG_KIND[k], f"scatter_start_{k}")
        return g

    tril = jnp.tril(jnp.ones((CHUNK, CHUNK), bool))
    wc = jnp.where(tril, a_w_s[0], 0.0).astype(MXU_DTYPE)
    heads = jnp.arange(1, B_HEADS + 1, dtype=F32)
    small = {
        "a_b_in": a_b_in, "a_vn_g": a_vn_g, "a_vn_b": a_vn_b,
        "wc": wc, "wct": wc.transpose(0, 2, 1),
        "bias_full": jnp.repeat(a_b_s[0].T, d // A_GROUPS, axis=1),
        "slopes": jnp.exp2(-8.0 * heads / B_HEADS),
    }

    loss_part, grad_x, gb, dm, dlg, dlb, gsmall = _local_step(x[0] + token[0, 0], loss_target[0], mvec, ln_g_full, ln_b_full, small, fetch, emit)
    loss = lax.psum(loss_part, ("x", "y", "c"))

    bufs = {k: _scatter_wait(*scattering[k], BIG_KIND[k], grad_x, f"scatter_wait_{k}") for k in scattering}
    halves = [_sum_slots(bufs[k], f"sum_{k}") for k in BIG]
    fulls = _swap_halves(halves[:4], "swap_halves_a") + _swap_halves(halves[4:], "swap_halves_b")
    gfull = {k: f.reshape(-1, f.shape[-1]) for k, f in zip(BIG, fulls)}

    pack_b = jnp.concatenate([dm.reshape(-1), dlg.reshape(-1), dlb.reshape(-1)] + [gsmall[k] for k in SMALL])
    n_small = pack_b.shape[0]
    pack_b = jnp.pad(pack_b, (0, -n_small % (ROW_TILE * LANES)))
    got_b = _all_gather_small(pack_b.reshape(-1, LANES), "gather_small_grads").reshape(N_DEV, -1, LANES)
    tot = _sum_slots(got_b, "sum_small").reshape(-1)
    o = 0
    dm_tot = tot[o:o + nsub * 3 * d].reshape(nsub, 3 * d); o += nsub * 3 * d
    dlg_tot = tot[o:o + nsub * d].reshape(nsub, d); o += nsub * d
    dlb_tot = tot[o:o + nsub * d].reshape(nsub, d); o += nsub * d
    g_small = {}
    for k, ref in zip(SMALL, (a_b_in, a_vn_g, a_vn_b, a_b_s, a_w_s)):
        g_small[k] = tot[o:o + ref.size].reshape(ref.shape); o += ref.size
    assert o == n_small
    dm_all = got_b.reshape(N_DEV, -1)[:, :nsub * 3 * d].reshape(N_DEV, nsub, 3 * d)
    dm_cols = lax.dynamic_slice_in_dim(dm_all, q * cs, cs, axis=2).transpose(1, 0, 2)

    grads = {
        "ada_w": _ada_bwd(c_all.T, dm_cols, "ada_bwd").reshape(ada_w.shape),
        "ada_b": lax.dynamic_slice_in_dim(dm_tot, q * cs, cs, axis=1).reshape(ada_b.shape),
        "ln_g": lax.dynamic_slice_in_dim(dlg_tot, q * ls, ls, axis=1).reshape(ln_g.shape),
        "ln_b": lax.dynamic_slice_in_dim(dlb_tot, q * ls, ls, axis=1).reshape(ln_b.shape),
        "a_w_in": gfull["a_w_in"][None], "a_w_out": gfull["a_w_out"][None],
        "b_w_qkv": gfull["b_w_qkv"][None], "b_w_out": gfull["b_w_out"][None],
        "mlp_w_up": jnp.stack([gfull["up0"], gfull["up1"]]), "mlp_w_down": jnp.stack([gfull["down0"], gfull["down1"]]),
        **g_small,
    }
    weights = dict(ada_w=ada_w, ada_b=ada_b, ln_g=ln_g, ln_b=ln_b, a_w_in=a_w_in, a_b_in=a_b_in, a_vn_g=a_vn_g, a_vn_b=a_vn_b,
                   a_w_s=a_w_s, a_b_s=a_b_s, a_w_out=a_w_out, b_w_qkv=b_w_qkv, b_w_out=b_w_out, mlp_w_up=mlp_w_up, mlp_w_down=mlp_w_down)
    ms = dict(ada_w=m_ada_w, ada_b=m_ada_b, ln_g=m_ln_g, ln_b=m_ln_b, a_w_in=m_a_w_in, a_b_in=m_a_b_in, a_vn_g=m_a_vn_g, a_vn_b=m_a_vn_b,
              a_w_s=m_a_w_s, a_b_s=m_a_b_s, a_w_out=m_a_w_out, b_w_qkv=m_b_w_qkv, b_w_out=m_b_w_out, mlp_w_up=m_mlp_w_up, mlp_w_down=m_mlp_w_down)
    vs = dict(ada_w=v_ada_w, ada_b=v_ada_b, ln_g=v_ln_g, ln_b=v_ln_b, a_w_in=v_a_w_in, a_b_in=v_a_b_in, a_vn_g=v_a_vn_g, a_vn_b=v_a_vn_b,
              a_w_s=v_a_w_s, a_b_s=v_a_b_s, a_w_out=v_a_w_out, b_w_qkv=v_b_w_qkv, b_w_out=v_b_w_out, mlp_w_up=v_mlp_w_up, mlp_w_down=v_mlp_w_down)
    names = list(weights)
    deltas, new_m, new_v = [], [], []
    for k in names:
        dl, nm, nv = _adamw(weights[k], grads[k], ms[k], vs[k], f"adamw_{k}")
        deltas.append(dl)
        new_m.append(nm)
        new_v.append(nv)
    return (loss, grad_x[None], *[grads[k] for k in names], *deltas, *new_m, *new_v)
```

```python
import functools
import math

import jax
import jax.numpy as jnp
from jax import lax
from jax.experimental import pallas as pl
from jax.experimental.pallas import tpu as pltpu

F32 = jnp.float32
MXU_DTYPE = jnp.bfloat16

DEPTH = 2
CHUNK = 128
A_GROUPS = 16
B_HEADS = 16
HEAD_DIM = 64
B_PATTERNS = ((128, 1), (512, 4), (2048, 16))
SPAN = 128
ALPHA = (2 * DEPTH) ** 0.25
LN_EPS = 1e-5
NEG = -1e30
ATT_SCALE = HEAD_DIM ** -0.5
ADAM_LR, ADAM_B1, ADAM_B2, ADAM_EPS, ADAM_WD, ADAM_STEP = 0.001, 0.9, 0.999, 1e-08, 0.01, 10

N_CHIPS = 4
N_DEV = 8
LANES = 128
SUBLANES = 8
VMEM_LIMIT = 52 * 1024 * 1024
ROW_TILE = 256
MESH = pl.DeviceIdType.MESH


def _cparams(sem):
    return pltpu.CompilerParams(dimension_semantics=sem, vmem_limit_bytes=VMEM_LIMIT)


def _fold8(v):
    r, c = v.shape
    return jnp.sum(v.reshape(r // SUBLANES, SUBLANES, c), axis=0)


def _gelu(x):
    c = math.sqrt(2.0 / math.pi)
    return 0.5 * x * (1.0 + jnp.tanh(c * (x + 0.044715 * (x * x * x))))


def _gelu_grad(x):
    c = math.sqrt(2.0 / math.pi)
    t = jnp.tanh(c * (x + 0.044715 * (x * x * x)))
    return 0.5 * (1.0 + t) + 0.5 * x * (1.0 - t * t) * c * (1.0 + 3.0 * 0.044715 * x * x)


def _dot(a, b, dims):
    return lax.dot_general(a.astype(MXU_DTYPE), b.astype(MXU_DTYPE), (dims, ((), ())), preferred_element_type=F32)


def _dot_nn(a, b):
    return _dot(a, b, ((1,), (0,)))


def _dot_nt(a, b):
    return _dot(a, b, ((1,), (1,)))


def _dot_tn(a, b):
    return _dot(a, b, ((0,), (0,)))


def _mm(a, b, *, mode, name, outs, tm, tn, tk, epi=None, extras=(), b_col0=0, n_out=None):
    if mode == "nn":
        m, kdim = a.shape
        p, kb, ns = b.shape
        assert kb == kdim and ns % tn == 0 and b_col0 % tn == 0
        n = n_out if n_out is not None else p * ns
        npt, j0 = ns // tn, b_col0 // tn
        a_spec = pl.BlockSpec((tm, tk), lambda i, j, k: (i, k))
        b_spec = pl.BlockSpec((None, tk, tn), lambda i, j, k: ((j + j0) // npt, k, (j + j0) % npt))
        dot = _dot_nn
    elif mode == "nt":
        m, kdim = a.shape
        p, n, ns = b.shape
        assert ns % tk == 0 and b_col0 % tk == 0
        npt, j0 = ns // tk, b_col0 // tk
        a_spec = pl.BlockSpec((tm, tk), lambda i, j, k: (i, k))
        b_spec = pl.BlockSpec((None, tn, tk), lambda i, j, k: ((k + j0) // npt, j, (k + j0) % npt))
        dot = _dot_nt
    else:
        kdim, m = a.shape
        kb, n = b.shape
        assert kb == kdim
        a_spec = pl.BlockSpec((tk, tm), lambda i, j, k: (k, i))
        b_spec = pl.BlockSpec((tk, tn), lambda i, j, k: (k, j))
        dot = _dot_tn
    assert m % tm == 0 and n % tn == 0 and kdim % tk == 0, (name, m, n, kdim, tm, tn, tk)
    nk = kdim // tk
    ex_specs, ex_arrays = [], []
    for kind, arr in extras:
        if kind == "row":
            ex_specs.append(pl.BlockSpec((1, tn), lambda i, j, k: (0, j)))
        else:
            ex_specs.append(pl.BlockSpec((tm, tn), lambda i, j, k: (i, j)))
        ex_arrays.append(arr)
    n_ex, n_o = len(ex_arrays), len(outs)

    def body(a_ref, b_ref, *rest):
        ex_refs, o_refs, acc = rest[:n_ex], rest[n_ex:n_ex + n_o], rest[n_ex + n_o]
        k = pl.program_id(2)

        @pl.when(k == 0)
        def _():
            acc[...] = jnp.zeros_like(acc)

        acc[...] += dot(a_ref[...], b_ref[...])

        @pl.when(k == nk - 1)
        def _():
            r = acc[...]
            vals = epi(r, *[e[...] for e in ex_refs]) if epi is not None else [r]
            for o, v in zip(o_refs, vals):
                o[...] = v.astype(o.dtype)

    res = pl.pallas_call(
        body,
        grid=(m // tm, n // tn, nk),
        in_specs=[a_spec, b_spec] + ex_specs,
        out_specs=[pl.BlockSpec((tm, tn), lambda i, j, k: (i, j)) for _ in outs],
        out_shape=[jax.ShapeDtypeStruct((m, n), dt) for dt in outs],
        scratch_shapes=[pltpu.VMEM((tm, tn), F32)],
        name=name,
        compiler_params=_cparams(("parallel", "parallel", "arbitrary")),
    )(a, b, *ex_arrays)
    return res if len(outs) > 1 else res[0]


def _rows(body, n_rows, tr, ins, outs, name, scratch=()):
    def spec(kind, shape):
        if kind == "blk":
            return pl.BlockSpec((tr,) + tuple(shape[1:]), lambda i: (i,) + (0,) * (len(shape) - 1))
        return pl.BlockSpec(tuple(shape), lambda i: (0,) * len(shape))

    return pl.pallas_call(
        body,
        grid=(n_rows // tr,),
        in_specs=[spec(k, a.shape) for k, a in ins],
        out_specs=[spec(k, s) for k, s, _ in outs],
        out_shape=[jax.ShapeDtypeStruct(tuple(s), d) for _, s, d in outs],
        scratch_shapes=list(scratch),
        name=name,
        compiler_params=_cparams(("arbitrary",)),
    )(*[a for _, a in ins])


def _ln_stats(z):
    mu = jnp.mean(z, axis=-1, keepdims=True)
    zc = z - mu
    var = jnp.mean(zc * zc, axis=-1, keepdims=True)
    rstd = lax.rsqrt(var + LN_EPS)
    return zc * rstd, rstd


def _mod(x, scale, shift, name):
    s, d = x.shape

    def body(x_ref, sc_ref, sh_ref, h_ref):
        h_ref[...] = (x_ref[...] * (1.0 + sc_ref[...]) + sh_ref[...]).astype(h_ref.dtype)

    return _rows(body, s, ROW_TILE, [("blk", x), ("all", scale), ("all", shift)], [("blk", (s, d), MXU_DTYPE)], name)[0]


def _resid_ln(x, y, gate, g, b, nxt, name):
    s, d = x.shape
    ins = [("blk", x), ("blk", y), ("all", gate), ("all", g), ("all", b)]
    outs = [("blk", (s, d), F32)]
    if nxt is not None:
        ins += [("all", nxt[0]), ("all", nxt[1])]
        outs += [("blk", (s, d), MXU_DTYPE)]

    def body(x_ref, y_ref, gate_ref, g_ref, b_ref, *rest):
        z = ALPHA * x_ref[...] + gate_ref[...] * y_ref[...]
        xhat, _ = _ln_stats(z)
        xn = xhat * g_ref[...] + b_ref[...]
        if nxt is None:
            rest[0][...] = xn
        else:
            sc_ref, sh_ref, xn_ref, h_ref = rest
            xn_ref[...] = xn
            h_ref[...] = (xn * (1.0 + sc_ref[...]) + sh_ref[...]).astype(h_ref.dtype)

    res = _rows(body, s, ROW_TILE, ins, outs, name)
    return (res[0], res[1]) if nxt is not None else (res[0], None)


def _loss_grad(xf, target, name):
    s, d = xf.shape

    def body(x_ref, t_ref, dy_ref, l_ref, acc):
        i = pl.program_id(0)

        @pl.when(i == 0)
        def _():
            acc[...] = jnp.zeros_like(acc)

        e = x_ref[...] - t_ref[...]
        dy_ref[...] = e * (1.0 / d)
        acc[...] += _fold8(e * e)

        @pl.when(i == pl.num_programs(0) - 1)
        def _():
            l_ref[...] = jnp.full(l_ref.shape, 0.5 / d, F32) * jnp.sum(acc[...])

    dy, l = _rows(body, s, ROW_TILE, [("blk", xf), ("blk", target)],
                  [("blk", (s, d), F32), ("all", (SUBLANES, LANES), F32)], name,
                  scratch=[pltpu.VMEM((SUBLANES, d), F32)])
    return dy, l[0, 0]


def _ln_bwd(dxo, x, y, gate, g, name):
    s, d = x.shape

    def body(dxo_ref, x_ref, y_ref, gate_ref, g_ref, dxr_ref, dyy_ref, red_ref, a_g, a_b, a_gate):
        i = pl.program_id(0)

        @pl.when(i == 0)
        def _():
            a_g[...] = jnp.zeros_like(a_g)
            a_b[...] = jnp.zeros_like(a_b)
            a_gate[...] = jnp.zeros_like(a_gate)

        yv = y_ref[...]
        z = ALPHA * x_ref[...] + gate_ref[...] * yv
        xhat, rstd = _ln_stats(z)
        dxo_v = dxo_ref[...]
        dxh = dxo_v * g_ref[...]
        dz = rstd * (dxh - jnp.mean(dxh, axis=-1, keepdims=True) - xhat * jnp.mean(dxh * xhat, axis=-1, keepdims=True))
        dxr_ref[...] = ALPHA * dz
        dyy_ref[...] = (gate_ref[...] * dz).astype(dyy_ref.dtype)
        a_g[...] += _fold8(dxo_v * xhat)
        a_b[...] += _fold8(dxo_v)
        a_gate[...] += _fold8(dz * yv)

        @pl.when(i == pl.num_programs(0) - 1)
        def _():
            red_ref[...] = jnp.zeros_like(red_ref)
            red_ref[0:1, :] = jnp.sum(a_g[...], axis=0, keepdims=True)
            red_ref[1:2, :] = jnp.sum(a_b[...], axis=0, keepdims=True)
            red_ref[2:3, :] = jnp.sum(a_gate[...], axis=0, keepdims=True)

    return _rows(body, s, ROW_TILE, [("blk", dxo), ("blk", x), ("blk", y), ("all", gate), ("all", g)],
                 [("blk", (s, d), F32), ("blk", (s, d), MXU_DTYPE), ("all", (SUBLANES, d), F32)], name,
                 scratch=[pltpu.VMEM((SUBLANES, d), F32)] * 3)


def _mod_bwd(dxr, dhs, x, scale, name):
    s, d = x.shape
    n_dh = len(dhs)

    def body(dxr_ref, *rest):
        dh_refs = rest[:n_dh]
        x_ref, sc_ref, dx_ref, red_ref, a_sh, a_sc = rest[n_dh:]
        i = pl.program_id(0)

        @pl.when(i == 0)
        def _():
            a_sh[...] = jnp.zeros_like(a_sh)
            a_sc[...] = jnp.zeros_like(a_sc)

        dh = dh_refs[0][...]
        for r in dh_refs[1:]:
            dh = dh + r[...]
        dx_ref[...] = dxr_ref[...] + dh * (1.0 + sc_ref[...])
        a_sh[...] += _fold8(dh)
        a_sc[...] += _fold8(dh * x_ref[...])

        @pl.when(i == pl.num_programs(0) - 1)
        def _():
            red_ref[...] = jnp.zeros_like(red_ref)
            red_ref[0:1, :] = jnp.sum(a_sh[...], axis=0, keepdims=True)
            red_ref[1:2, :] = jnp.sum(a_sc[...], axis=0, keepdims=True)

    return _rows(body, s, ROW_TILE, [("blk", dxr)] + [("blk", h) for h in dhs] + [("blk", x), ("all", scale)],
                 [("blk", (s, d), F32), ("all", (SUBLANES, d), F32)], name,
                 scratch=[pltpu.VMEM((SUBLANES, d), F32)] * 2)


def _left_half(shape):
    return lax.broadcasted_iota(jnp.int32, shape, 1) < (LANES // 2)


def _spatial_z(vn, wc_ref, bias_ref, j):
    vb = vn[:, j * LANES:(j + 1) * LANES]
    z0 = _dot_nn(wc_ref[2 * j], vb)
    z1 = _dot_nn(wc_ref[2 * j + 1], vb)
    return jnp.where(_left_half(z0.shape), z0, z1) + bias_ref[:, j * LANES:(j + 1) * LANES]


def _spatial_fwd(uvpre, vn_g, vn_b, wc, bias_full, name):
    s, d2 = uvpre.shape
    d = d2 // 2

    def body(uv_ref, g_ref, b_ref, wc_ref, bias_ref, out_ref):
        u = _gelu(uv_ref[:, :d])
        v = _gelu(uv_ref[:, d:])
        vh, _ = _ln_stats(v)
        vn = vh * g_ref[...] + b_ref[...]
        for j in range(d // LANES):
            z = _spatial_z(vn, wc_ref, bias_ref, j)
            out_ref[:, j * LANES:(j + 1) * LANES] = (u[:, j * LANES:(j + 1) * LANES] * z).astype(out_ref.dtype)

    return _rows(body, s, CHUNK, [("blk", uvpre), ("all", vn_g), ("all", vn_b), ("all", wc), ("all", bias_full)],
                 [("blk", (s, d), MXU_DTYPE)], name)[0]


def _spatial_bwd(uvpre, dgated, vn_g, vn_b, wc, wct, bias_full, name):
    s, d2 = uvpre.shape
    d = d2 // 2

    def body(uv_ref, dg_ref, g_ref, b_ref, wc_ref, wct_ref, bias_ref,
             duv_ref, dws_ref, dbias_ref, dbin_ref, dvg_ref, dvb_ref, dvn_buf, a_bin, a_vg, a_vb):
        i = pl.program_id(0)

        @pl.when(i == 0)
        def _():
            dws_ref[...] = jnp.zeros_like(dws_ref)
            dbias_ref[...] = jnp.zeros_like(dbias_ref)
            a_bin[...] = jnp.zeros_like(a_bin)
            a_vg[...] = jnp.zeros_like(a_vg)
            a_vb[...] = jnp.zeros_like(a_vb)

        up = uv_ref[:, :d]
        vp = uv_ref[:, d:]
        u = _gelu(up)
        v = _gelu(vp)
        vh, rstd = _ln_stats(v)
        vn = vh * g_ref[...] + b_ref[...]
        dg = dg_ref[...]
        dzz = dg * u
        dbias_ref[...] += dzz
        for j in range(d // LANES):
            cols = slice(j * LANES, (j + 1) * LANES)
            z = _spatial_z(vn, wc_ref, bias_ref, j)
            dup = dg[:, cols] * z * _gelu_grad(up[:, cols])
            duv_ref[:, cols] = dup.astype(duv_ref.dtype)
            a_bin[:, cols] += _fold8(dup)
            dzb = dzz[:, cols]
            left = _left_half(dzb.shape)
            dvn_buf[:, cols] = jnp.where(left, _dot_nn(wct_ref[2 * j], dzb), _dot_nn(wct_ref[2 * j + 1], dzb))
            vb = vn[:, cols]
            dws_ref[2 * j] += _dot_nt(jnp.where(left, dzb, 0.0), vb)
            dws_ref[2 * j + 1] += _dot_nt(jnp.where(left, 0.0, dzb), vb)
        dvn = dvn_buf[...]
        a_vg[...] += _fold8(dvn * vh)
        a_vb[...] += _fold8(dvn)
        dvh = dvn * g_ref[...]
        dv = rstd * (dvh - jnp.mean(dvh, axis=-1, keepdims=True) - vh * jnp.mean(dvh * vh, axis=-1, keepdims=True))
        dvp = dv * _gelu_grad(vp)
        duv_ref[:, d:] = dvp.astype(duv_ref.dtype)
        a_bin[:, d:] += _fold8(dvp)

        @pl.when(i == pl.num_programs(0) - 1)
        def _():
            dbin_ref[...] = jnp.sum(a_bin[...], axis=0, keepdims=True)
            dvg_ref[...] = jnp.sum(a_vg[...], axis=0, keepdims=True)
            dvb_ref[...] = jnp.sum(a_vb[...], axis=0, keepdims=True)

    return _rows(body, s, CHUNK,
                 [("blk", uvpre), ("blk", dgated), ("all", vn_g), ("all", vn_b), ("all", wc), ("all", wct), ("all", bias_full)],
                 [("blk", (s, d2), MXU_DTYPE), ("all", (A_GROUPS, CHUNK, CHUNK), F32), ("all", (CHUNK, d), F32),
                  ("all", (1, d2), F32), ("all", (1, d), F32), ("all", (1, d), F32)], name,
                 scratch=[pltpu.VMEM((CHUNK, d), F32), pltpu.VMEM((SUBLANES, d2), F32),
                          pltpu.VMEM((SUBLANES, d), F32), pltpu.VMEM((SUBLANES, d), F32)])


def _head_mask(v, h):
    lane = lax.broadcasted_iota(jnp.int32, v.shape, 1)
    return jnp.where((lane >= h * HEAD_DIM) & (lane < (h + 1) * HEAD_DIM), v, jnp.zeros_like(v))


def _att_bias(slopes, dil):
    qi = lax.broadcasted_iota(jnp.int32, (SPAN, SPAN), 0)
    ki = lax.broadcasted_iota(jnp.int32, (SPAN, SPAN), 1)
    sl = slopes[:, None, None]
    cur = jnp.where(ki <= qi, -sl * (float(dil) * (qi - ki).astype(F32)), NEG)
    prev = jnp.where(ki >= qi, -sl * (float(dil) * (SPAN + qi - ki).astype(F32)), NEG)
    absent = jnp.full_like(prev, NEG)
    pairs = slopes.shape[0] // 2

    def fwd(pv):
        return jnp.concatenate([cur, pv], axis=2).reshape(pairs, 2 * SPAN, 2 * SPAN)

    def bwd(pv):
        return jnp.concatenate([cur.reshape(pairs, 2 * SPAN, SPAN), pv.reshape(pairs, 2 * SPAN, SPAN)], axis=1)

    return jnp.stack([fwd(absent), fwd(prev)]), jnp.stack([bwd(absent), bwd(prev)])


def _att_specs(s, d, dil, kinds):
    nb = s // (dil * SPAN)

    def rowblk(which, b):
        if which == "prev":
            return jnp.where(b % nb == 0, b, b - 1)
        if which == "next":
            return jnp.where(b % nb == nb - 1, b, b + 1)
        return b

    return [pl.BlockSpec((SPAN, d), functools.partial(lambda b, o, w: (rowblk(w, b), o), o=part, w=which))
            for part, which in kinds]


def _lane_col(v, h):
    return v[:, h * HEAD_DIM:h * HEAD_DIM + 1]


def _attn_fwd(qkv, slopes, dil, name):
    s, d3 = qkv.shape
    d = d3 // 3
    nb = s // (dil * SPAN)
    table, _ = _att_bias(slopes, dil)

    def body(q_ref, kc_ref, kp_ref, vc_ref, vp_ref, tb_ref, o_ref, l_ref):
        left = _left_half((SPAN, LANES))
        for hp in range(d // LANES):
            cols = slice(hp * LANES, (hp + 1) * LANES)
            q = q_ref[:, cols]
            q2 = jnp.concatenate([_head_mask(q, 0), _head_mask(q, 1)], axis=0) * ATT_SCALE
            k2 = jnp.concatenate([kc_ref[:, cols], kp_ref[:, cols]], axis=0)
            v2 = jnp.concatenate([vc_ref[:, cols], vp_ref[:, cols]], axis=0)
            sc = _dot_nt(q2, k2) + tb_ref[hp]
            m = jnp.max(sc, axis=-1, keepdims=True)
            p = jnp.exp(sc - m)
            l = jnp.sum(p, axis=-1, keepdims=True)
            r = _dot_nn(p, v2) * (1.0 / l)
            lse = jnp.broadcast_to(m + jnp.log(l), (2 * SPAN, LANES))
            o_ref[:, cols] = jnp.where(left, r[:SPAN], r[SPAN:])
            l_ref[:, cols] = jnp.where(left, lse[:SPAN], lse[SPAN:])

    specs = _att_specs(s, d, dil, [(0, "cur"), (1, "cur"), (1, "prev"), (2, "cur"), (2, "prev")])
    tbl = pl.BlockSpec((None,) + table.shape[1:], lambda b: (jnp.where(b % nb == 0, 0, 1), 0, 0, 0))
    out_spec = pl.BlockSpec((SPAN, d), lambda b: (b, 0))
    return pl.pallas_call(
        body,
        grid=(s // SPAN,),
        in_specs=specs + [tbl],
        out_specs=[out_spec, out_spec],
        out_shape=[jax.ShapeDtypeStruct((s, d), F32)] * 2,
        name=name,
        compiler_params=_cparams(("parallel",)),
    )(qkv, qkv, qkv, qkv, qkv, table)


def _attn_bwd(qkv, do, lse, dd, slopes, dil, name):
    s, d3 = qkv.shape
    d = d3 // 3
    nb = s // (dil * SPAN)
    _, table = _att_bias(slopes, dil)

    def heads_stacked(cur, nxt):
        return jnp.concatenate([_head_mask(cur, 0), _head_mask(cur, 1), _head_mask(nxt, 0), _head_mask(nxt, 1)], axis=0)

    def cols_stacked(cur, nxt):
        return jnp.concatenate([jnp.broadcast_to(_lane_col(a, h), (SPAN, LANES)) for a in (cur, nxt) for h in range(2)], axis=0)

    def body(k_ref, v_ref, qc_ref, qn_ref, doc_ref, don_ref, lc_ref, ln_ref, ddc_ref, ddn_ref, tb_ref, out_ref, carry):
        b = pl.program_id(0)

        @pl.when(b == 0)
        def _():
            carry[...] = jnp.zeros_like(carry)

        left = _left_half((SPAN, LANES))
        for hp in range(d // LANES):
            cols = slice(hp * LANES, (hp + 1) * LANES)
            k, v = k_ref[:, cols], v_ref[:, cols]
            q4 = heads_stacked(qc_ref[:, cols], qn_ref[:, cols])
            do4 = heads_stacked(doc_ref[:, cols], don_ref[:, cols])
            sc = _dot_nt(q4 * ATT_SCALE, k) + tb_ref[hp]
            p = jnp.exp(sc - cols_stacked(lc_ref[:, cols], ln_ref[:, cols]))
            ds = p * (_dot_nt(do4, v) - cols_stacked(ddc_ref[:, cols], ddn_ref[:, cols]))
            dq4 = _dot_nn(ds, k)
            dq_cur = jnp.where(left, dq4[:SPAN], dq4[SPAN:2 * SPAN]) + carry[:, cols]
            carry[:, cols] = jnp.where(left, dq4[2 * SPAN:3 * SPAN], dq4[3 * SPAN:])
            out_ref[:, cols] = (dq_cur * ATT_SCALE).astype(out_ref.dtype)
            out_ref[:, d + hp * LANES:d + (hp + 1) * LANES] = (_dot_tn(ds, q4) * ATT_SCALE).astype(out_ref.dtype)
            out_ref[:, 2 * d + hp * LANES:2 * d + (hp + 1) * LANES] = _dot_tn(p, do4).astype(out_ref.dtype)

    qkv_specs = _att_specs(s, d, dil, [(1, "cur"), (2, "cur"), (0, "cur"), (0, "next")])
    pair = _att_specs(s, d, dil, [(0, "cur"), (0, "next")])
    tbl = pl.BlockSpec((None,) + table.shape[1:], lambda b: (jnp.where(b % nb == nb - 1, 0, 1), 0, 0, 0))
    return pl.pallas_call(
        body,
        grid=(s // SPAN,),
        in_specs=qkv_specs + pair + pair + pair + [tbl],
        out_specs=pl.BlockSpec((SPAN, d3), lambda b: (b, 0)),
        out_shape=jax.ShapeDtypeStruct((s, d3), MXU_DTYPE),
        scratch_shapes=[pltpu.VMEM((SPAN, d), F32)],
        name=name,
        compiler_params=_cparams(("arbitrary",)),
    )(qkv, qkv, qkv, qkv, do, do, lse, lse, dd, dd, table)


def _mix_weights(l_refs):
    ls = [r[...] for r in l_refs]
    m = functools.reduce(jnp.maximum, ls)
    es = [jnp.exp(l - m) for l in ls]
    tot = functools.reduce(lambda a, c: a + c, es)
    return [e / tot for e in es]


def _combine_fwd(os_, ls_, name):
    s, d = os_[0].shape
    n = len(os_)

    def body(*refs):
        o_refs, l_refs, out_ref = refs[:n], refs[n:2 * n], refs[2 * n]
        ws = _mix_weights(l_refs)
        acc = ws[0] * o_refs[0][...]
        for w, o in zip(ws[1:], o_refs[1:]):
            acc = acc + w * o[...]
        out_ref[...] = acc

    return _rows(body, s, ROW_TILE, [("blk", a) for a in os_ + ls_], [("blk", (s, d), F32)], name)[0]


def _combine_bwd(do, o, ls_, name):
    s, d = o.shape
    n = len(ls_)
    ri = lax.broadcasted_iota(jnp.int32, (LANES, LANES), 0) // HEAD_DIM
    ci = lax.broadcasted_iota(jnp.int32, (LANES, LANES), 1) // HEAD_DIM
    seg = (ri == ci).astype(F32)

    def body(do_ref, o_ref, *rest):
        l_refs, seg_ref, outs = rest[:n], rest[n], rest[n + 1:]
        ws = _mix_weights(l_refs)
        dov = do_ref[...]
        prod = dov * o_ref[...]
        for j in range(d // LANES):
            cols = slice(j * LANES, (j + 1) * LANES)
            r = jnp.dot(prod[:, cols], seg_ref[...], precision=lax.Precision.HIGHEST, preferred_element_type=F32)
            for g in range(n):
                outs[2 * g][:, cols] = (ws[g][:, cols] * dov[:, cols]).astype(outs[2 * g].dtype)
                outs[2 * g + 1][:, cols] = ws[g][:, cols] * r

    outs = []
    for _ in range(n):
        outs += [("blk", (s, d), MXU_DTYPE), ("blk", (s, d), F32)]
    res = _rows(body, s, ROW_TILE, [("blk", do), ("blk", o)] + [("blk", l) for l in ls_] + [("all", seg)], outs, name)
    return [(res[2 * g], res[2 * g + 1]) for g in range(n)]


def _ada_fwd(c_all, w, b, name):
    nsub, d, cs = w.shape

    def body(c_ref, w_ref, b_ref, o_ref):
        cv = c_ref[...]
        sc = cv * (1.0 / (1.0 + jnp.exp(-cv)))
        o_ref[...] = _dot_nn(sc, w_ref[...]) + b_ref[...]

    return pl.pallas_call(
        body,
        grid=(nsub,),
        in_specs=[pl.BlockSpec(c_all.shape, lambda i: (0, 0)), pl.BlockSpec((None, d, cs), lambda i: (i, 0, 0)),
                  pl.BlockSpec((None, 1, cs), lambda i: (i, 0, 0))],
        out_specs=pl.BlockSpec((None, N_DEV, cs), lambda i: (i, 0, 0)),
        out_shape=jax.ShapeDtypeStruct((nsub, N_DEV, cs), F32),
        name=name,
        compiler_params=_cparams(("parallel",)),
    )(c_all, w, b)


def _ada_bwd(c_all_t, dm, name):
    d, nb = c_all_t.shape
    nsub, _, cs = dm.shape

    def body(c_ref, dm_ref, o_ref):
        cv = c_ref[...]
        sc = cv * (1.0 / (1.0 + jnp.exp(-cv)))
        acc = sc[:, 0:1] * dm_ref[0:1, :]
        for bi in range(1, nb):
            acc = acc + sc[:, bi:bi + 1] * dm_ref[bi:bi + 1, :]
        o_ref[...] = acc

    return pl.pallas_call(
        body,
        grid=(nsub,),
        in_specs=[pl.BlockSpec(c_all_t.shape, lambda i: (0, 0)), pl.BlockSpec((None, nb, cs), lambda i: (i, 0, 0))],
        out_specs=pl.BlockSpec((None, d, cs), lambda i: (i, 0, 0)),
        out_shape=jax.ShapeDtypeStruct((nsub, d, cs), F32),
        name=name,
        compiler_params=_cparams(("parallel",)),
    )(c_all_t, dm)


def _row_tile(r, row_elems):
    t = 2 * SUBLANES
    if r % t:
        return r
    while t * 2 * row_elems <= 256 * 1024 and r % (t * 2) == 0:
        t *= 2
    return t


def _adamw(w, g, m, v, name):
    shape = w.shape
    c = shape[-1]
    r = w.size // c
    tr = _row_tile(r, c)
    w2, g2, m2, v2 = [a.reshape(r, c) for a in (w, g, m, v)]
    bc1 = 1.0 - ADAM_B1 ** ADAM_STEP
    bc2 = 1.0 - ADAM_B2 ** ADAM_STEP

    def body(w_ref, g_ref, m_ref, v_ref, d_ref, nm_ref, nv_ref):
        gv = g_ref[...]
        nm = ADAM_B1 * m_ref[...] + (1.0 - ADAM_B1) * gv
        nv = ADAM_B2 * v_ref[...] + (1.0 - ADAM_B2) * (gv * gv)
        d_ref[...] = -ADAM_LR * ((nm / bc1) / (jnp.sqrt(nv / bc2) + ADAM_EPS) + ADAM_WD * w_ref[...])
        nm_ref[...] = nm
        nv_ref[...] = nv

    res = _rows(body, r, tr, [("blk", a) for a in (w2, g2, m2, v2)], [("blk", (r, c), F32)] * 3, name)
    return [a.reshape(shape) for a in res]


def _sum_slots(buf, name):
    n, r, c = buf.shape
    tr = _row_tile(r, n * c)

    def body(b_ref, o_ref):
        acc = b_ref[0].astype(F32)
        for k in range(1, n):
            acc = acc + b_ref[k].astype(F32)
        o_ref[...] = acc

    return pl.pallas_call(
        body,
        grid=(r // tr,),
        in_specs=[pl.BlockSpec((n, tr, c), lambda i: (0, i, 0))],
        out_specs=pl.BlockSpec((tr, c), lambda i: (i, 0)),
        out_shape=jax.ShapeDtypeStruct((r, c), F32),
        name=name,
        compiler_params=_cparams(("parallel",)),
    )(buf)


def _me():
    return lax.axis_index("x"), lax.axis_index("y"), lax.axis_index("c")


def _all_gather_small(blk, name):
    m_per, n = blk.shape

    def body(x_ref, out_ref, send_sems, recv_sems, local_sem):
        x, y, c = _me()
        me, sibling = (x, y, c), (x, y, 1 - c)
        chips = [(1 - x, y), (x, 1 - y), (1 - x, 1 - y)]

        def rows(px, py, pc):
            return out_ref.at[pl.ds((4 * px + 2 * py + pc) * m_per, m_per), :]

        def copy(k, block, to, src=None):
            return pltpu.make_async_remote_copy(
                src_ref=rows(*block) if src is None else src, dst_ref=rows(*block),
                send_sem=send_sems.at[k], recv_sem=recv_sems.at[k], device_id=to, device_id_type=MESH)

        mine = pltpu.make_async_copy(x_ref, rows(*me), local_sem)
        mine.start()
        first = [copy(0, me, sibling, src=x_ref)]
        first += [copy(1 + j, me, (*chip, c), src=x_ref) for j, chip in enumerate(chips)]
        for cp in first:
            cp.start()
        passed = [copy(4 + j, (*chip, c), sibling) for j, chip in enumerate(chips)]
        for j, chip in enumerate(chips):
            copy(1 + j, (*chip, c), me).wait_recv()
            passed[j].start()
        copy(0, sibling, me).wait_recv()
        for j, chip in enumerate(chips):
            copy(4 + j, (*chip, 1 - c), me).wait_recv()
        for cp in first + passed:
            cp.wait_send()
        mine.wait()

    return pl.pallas_call(
        body,
        out_shape=jax.ShapeDtypeStruct((N_DEV * m_per, n), blk.dtype),
        in_specs=[pl.BlockSpec(memory_space=pltpu.VMEM)],
        out_specs=pl.BlockSpec(memory_space=pltpu.VMEM),
        scratch_shapes=[pltpu.SemaphoreType.DMA((7,)), pltpu.SemaphoreType.DMA((7,)), pltpu.SemaphoreType.DMA],
        name=name,
        compiler_params=pltpu.CompilerParams(vmem_limit_bytes=VMEM_LIMIT),
    )(blk)


_HBM = pl.BlockSpec(memory_space=pltpu.HBM)
_SEM = pl.BlockSpec(memory_space=pltpu.SEMAPHORE)
_EFFECT = pltpu.SideEffectType.DATAFLOW_SIDE_EFFECTING


def _other_chips(x, y):
    return [(1 - x, y), (x, 1 - y), (1 - x, 1 - y)]


def _gather_copy(w, j, src_ref, land_ref, send_sems, recv_sems):
    x, y, c = _me()
    return pltpu.make_async_remote_copy(
        src_ref=src_ref, dst_ref=land_ref.at[2 * x + y], send_sem=send_sems.at[3 * w + j], recv_sem=recv_sems.at[3 * w + j],
        device_id=(*_other_chips(x, y)[j], c), device_id_type=MESH)


def _gather_start(shards, name):
    n = len(shards)
    lands = [lax.empty((N_CHIPS,) + s.shape, s.dtype) for s in shards]

    def body(*refs):
        in_refs, land_refs = refs[:n], refs[n:2 * n]
        send_sems, recv_sems = refs[2 * n], refs[2 * n + 1]
        token = refs[-1]
        for w in range(n):
            for j in range(3):
                _gather_copy(w, j, in_refs[w], land_refs[w], send_sems, recv_sems).start()
        token[...] = jnp.zeros_like(token)

    res = pl.pallas_call(
        body,
        out_shape=(pltpu.SemaphoreType.DMA((3 * n,)), pltpu.SemaphoreType.DMA((3 * n,)),
                   *[pltpu.HBM(s.shape, s.dtype) for s in shards], *[pltpu.HBM(l.shape, l.dtype) for l in lands],
                   jax.ShapeDtypeStruct((SUBLANES, LANES), F32)),
        in_specs=[_HBM] * (2 * n),
        out_specs=(_SEM, _SEM, *[_HBM] * (2 * n), pl.BlockSpec(memory_space=pltpu.VMEM)),
        input_output_aliases={i: 2 + i for i in range(2 * n)},
        name=name,
        compiler_params=pltpu.CompilerParams(has_side_effects=_EFFECT),
    )(*[pltpu.with_memory_space_constraint(a, pltpu.HBM) for a in list(shards) + lands])
    return res[0], res[1], res[2:2 + n], res[2 + n:2 + 2 * n], res[-1]


def _gather_wait(w, shard, land, send_sems, recv_sems, after, name):
    def body(s_ref, land_ref, send_sems, recv_sems, after_ref, s_out, land_out, stage):
        x, y, _ = _me()
        pltpu.sync_copy(s_ref, stage)
        pltpu.sync_copy(stage, land_out.at[2 * x + y])
        for j in range(3):
            cp = _gather_copy(w, j, s_ref, land_ref, send_sems, recv_sems)
            cp.wait_send()
            cp.wait_recv()

    return pl.pallas_call(
        body,
        out_shape=(pltpu.HBM(shard.shape, shard.dtype), pltpu.HBM(land.shape, land.dtype)),
        in_specs=(_HBM, _HBM, _SEM, _SEM, pl.BlockSpec(memory_space=pl.ANY)),
        out_specs=(_HBM, _HBM),
        input_output_aliases={0: 0, 1: 1},
        scratch_shapes=[pltpu.VMEM(shard.shape, shard.dtype)],
        name=name,
        compiler_params=pltpu.CompilerParams(has_side_effects=_EFFECT, vmem_limit_bytes=VMEM_LIMIT),
    )(shard, land, send_sems, recv_sems, after)[1]


def _piece_shape(shape, kind):
    k, nn = shape
    return (k // 2, nn // N_CHIPS) if kind == "col" else (k // N_CHIPS // 2, nn)


def _piece_of(g_ref, kind, tq, tc):
    pr, pc = _piece_shape(g_ref.shape, kind)
    if kind == "col":
        return g_ref.at[pl.ds(tc * pr, pr), pl.ds(tq * pc, pc)]
    return g_ref.at[pl.ds((2 * tq + tc) * pr, pr), :]


def _scatter_copy(r, kind, g_ref, land_ref, send_sems, recv_sems):
    x, y, c = _me()
    tx, ty, tc = (x + ((r >> 2) & 1)) % 2, (y + ((r >> 1) & 1)) % 2, (c + (r & 1)) % 2
    return pltpu.make_async_remote_copy(
        src_ref=_piece_of(g_ref, kind, 2 * tx + ty, tc), dst_ref=land_ref.at[4 * x + 2 * y + c],
        send_sem=send_sems.at[r], recv_sem=recv_sems.at[r], device_id=(tx, ty, tc), device_id_type=MESH)


def _scatter_start(g, kind, name):
    piece = _piece_shape(g.shape, kind)
    land = lax.empty((N_DEV,) + piece, g.dtype)

    def body(g_ref, land_ref, send_sems, recv_sems, g_out, land_out, stage):
        x, y, c = _me()
        for r in range(1, N_DEV):
            _scatter_copy(r, kind, g_ref, land_ref, send_sems, recv_sems).start()
        pltpu.sync_copy(_piece_of(g_ref, kind, 2 * x + y, c), stage)
        pltpu.sync_copy(stage, land_out.at[4 * x + 2 * y + c])

    return pl.pallas_call(
        body,
        out_shape=(pltpu.SemaphoreType.DMA((N_DEV,)), pltpu.SemaphoreType.DMA((N_DEV,)),
                   pltpu.HBM(g.shape, g.dtype), pltpu.HBM(land.shape, land.dtype)),
        in_specs=[_HBM, _HBM],
        out_specs=(_SEM, _SEM, _HBM, _HBM),
        input_output_aliases={0: 2, 1: 3},
        scratch_shapes=[pltpu.VMEM(piece, g.dtype)],
        name=name,
        compiler_params=pltpu.CompilerParams(has_side_effects=_EFFECT, vmem_limit_bytes=VMEM_LIMIT),
    )(pltpu.with_memory_space_constraint(g, pltpu.HBM), pltpu.with_memory_space_constraint(land, pltpu.HBM))


def _scatter_wait(send_sems, recv_sems, g, land, kind, after, name):
    def body(g_ref, land_ref, send_sems, recv_sems, after_ref, g_out, land_out):
        for r in range(1, N_DEV):
            cp = _scatter_copy(r, kind, g_ref, land_ref, send_sems, recv_sems)
            cp.wait_send()
            cp.wait_recv()

    return pl.pallas_call(
        body,
        out_shape=(pltpu.HBM(g.shape, g.dtype), pltpu.HBM(land.shape, land.dtype)),
        in_specs=(_HBM, _HBM, _SEM, _SEM, pl.BlockSpec(memory_space=pl.ANY)),
        out_specs=(_HBM, _HBM),
        input_output_aliases={0: 0, 1: 1},
        name=name,
        compiler_params=pltpu.CompilerParams(has_side_effects=_EFFECT),
    )(g, land, send_sems, recv_sems, after)[1]


def _swap_halves(halves, name):
    n = len(halves)

    def body(*refs):
        in_refs, out_refs = refs[:n], refs[n:2 * n]
        send_sems, recv_sems, local_sems = refs[2 * n:]
        x, y, c = _me()
        cps = []
        for w in range(n):
            lc = pltpu.make_async_copy(in_refs[w], out_refs[w].at[c], local_sems.at[w])
            lc.start()
            rc = pltpu.make_async_remote_copy(
                src_ref=in_refs[w], dst_ref=out_refs[w].at[c], send_sem=send_sems.at[w], recv_sem=recv_sems.at[w],
                device_id=(x, y, 1 - c), device_id_type=MESH)
            rc.start()
            cps.append((lc, rc))
        for lc, rc in cps:
            rc.wait_recv()
        for lc, rc in cps:
            rc.wait_send()
            lc.wait()

    vmem = pl.BlockSpec(memory_space=pltpu.VMEM)
    return pl.pallas_call(
        body,
        out_shape=[jax.ShapeDtypeStruct((2,) + h.shape, h.dtype) for h in halves],
        in_specs=[vmem] * n,
        out_specs=[vmem] * n,
        scratch_shapes=[pltpu.SemaphoreType.DMA((n,)), pltpu.SemaphoreType.DMA((n,)), pltpu.SemaphoreType.DMA((n,))],
        name=name,
        compiler_params=pltpu.CompilerParams(vmem_limit_bytes=VMEM_LIMIT),
    )(*halves)


def _to_streams(a, dil):
    if dil == 1:
        return a
    s, c = a.shape
    return a.reshape(s // dil, dil, c).transpose(1, 0, 2).reshape(s, c)


def _from_streams(a, dil):
    if dil == 1:
        return a
    s, c = a.shape
    return a.reshape(dil, s // dil, c).transpose(1, 0, 2).reshape(s, c)


def _mm_tiles(s):
    return min(s, 1024)


def _local_step(x0, target, mvec, ln_g, ln_b, small, fetch, emit):
    s, d = x0.shape
    tm = _mm_tiles(s)
    row = lambda v: v.reshape(1, -1)
    shift = [row(mvec[i, :d]) for i in range(4)]
    scale = [row(mvec[i, d:2 * d]) for i in range(4)]
    gate = [row(1.0 + mvec[i, 2 * d:]) for i in range(4)]
    lg = [row(ln_g[i]) for i in range(4)]
    lb = [row(ln_b[i]) for i in range(4)]
    mm = functools.partial(_mm, tm=tm)
    mm_w = functools.partial(_mm, tm=1024, tk=min(s, 512), mode="tn")

    xs, ys, big = [x0], [], {}
    h0 = _mod(x0, scale[0], shift[0], "mod0")
    big["a_w_in"] = fetch("a_w_in", h0)
    uvpre = mm(h0, big["a_w_in"], mode="nn", name="a_in", outs=[F32], tn=512, tk=512,
               epi=lambda r, bias: [r + bias], extras=[("row", small["a_b_in"])])
    gated = _spatial_fwd(uvpre, small["a_vn_g"], small["a_vn_b"], small["wc"], small["bias_full"], "a_spatial")
    big["a_w_out"] = fetch("a_w_out", gated)
    ys.append(mm(gated, big["a_w_out"], mode="nn", name="a_out", outs=[F32], tn=1024, tk=512))
    x1, h1 = _resid_ln(xs[0], ys[0], gate[0], lg[0], lb[0], (scale[1], shift[1]), "ln0")
    xs.append(x1)
    relu2 = lambda r: [r, jnp.square(jnp.maximum(r, 0.0))]
    big["up0"] = fetch("up0", h1)
    a0, r0 = mm(h1, big["up0"], mode="nn", name="up0", outs=[MXU_DTYPE, MXU_DTYPE], tn=1024, tk=512, epi=relu2)
    big["down0"] = fetch("down0", r0)
    ys.append(mm(r0, big["down0"], mode="nn", name="down0", outs=[F32], tn=1024, tk=512))
    x2, h2 = _resid_ln(xs[1], ys[1], gate[1], lg[1], lb[1], (scale[2], shift[2]), "ln1")
    xs.append(x2)
    hg, qkvs, o_g, l_g = [], [], [], []
    big["b_w_qkv"] = fetch("b_w_qkv", h2)
    for g, (_, dil) in enumerate(B_PATTERNS):
        hp = _to_streams(h2, dil)
        qkv = mm(hp, big["b_w_qkv"], mode="nn", name=f"qkv{g}", outs=[MXU_DTYPE], tn=768, tk=512, b_col0=g * 3 * d, n_out=3 * d)
        og, lgv = _attn_fwd(qkv, small["slopes"], dil, f"attn_fwd{g}")
        hg.append(hp)
        qkvs.append(qkv)
        o_g.append(_from_streams(og, dil))
        l_g.append(_from_streams(lgv, dil))
    o_mix = _combine_fwd(o_g, l_g, "combine")
    big["b_w_out"] = fetch("b_w_out", o_mix)
    ys.append(mm(o_mix, big["b_w_out"], mode="nn", name="b_out", outs=[F32], tn=1024, tk=512))
    x3, h3 = _resid_ln(xs[2], ys[2], gate[2], lg[2], lb[2], (scale[3], shift[3]), "ln2")
    xs.append(x3)
    big["up1"] = fetch("up1", h3)
    a1, r1 = mm(h3, big["up1"], mode="nn", name="up1", outs=[MXU_DTYPE, MXU_DTYPE], tn=1024, tk=512, epi=relu2)
    big["down1"] = fetch("down1", r1)
    ys.append(mm(r1, big["down1"], mode="nn", name="down1", outs=[F32], tn=1024, tk=512))
    x4, _ = _resid_ln(xs[3], ys[3], gate[3], lg[3], lb[3], None, "ln3")

    gb, dm, dlg, dlb = {}, [None] * 4, [None] * 4, [None] * 4
    dx, loss = _loss_grad(x4, target, "loss")

    def mlp_bwd(i, sub, dx, h, a, r):
        dxr, dyy, red = _ln_bwd(dx, xs[sub], ys[sub], gate[sub], lg[sub], f"ln_bwd{sub}")
        gb[f"down{i}"] = emit(f"down{i}", mm_w(r, dyy, name=f"g_down{i}", outs=[MXU_DTYPE], tn=1024))
        da = mm(dyy, big[f"down{i}"], mode="nt", name=f"d_down{i}", outs=[MXU_DTYPE], tn=1024, tk=512,
                epi=lambda acc, av: [acc * (2.0 * jnp.maximum(av.astype(F32), 0.0))], extras=[("full", a)])
        gb[f"up{i}"] = emit(f"up{i}", mm_w(h, da, name=f"g_up{i}", outs=[MXU_DTYPE], tn=1024))
        dh = mm(da, big[f"up{i}"], mode="nt", name=f"d_up{i}", outs=[F32], tn=1024, tk=512)
        dx, red2 = _mod_bwd(dxr, [dh], xs[sub], scale[sub], f"mod_bwd{sub}")
        dm[sub] = jnp.concatenate([red2[0], red2[1], red[2]])
        dlg[sub], dlb[sub] = red[0], red[1]
        return dx

    dx = mlp_bwd(1, 3, dx, h3, a1, r1)
    dxr, dyy, red = _ln_bwd(dx, xs[2], ys[2], gate[2], lg[2], "ln_bwd2")
    gb["b_w_out"] = emit("b_w_out", mm_w(o_mix, dyy, name="g_b_out", outs=[MXU_DTYPE], tn=1024))
    do = mm(dyy, big["b_w_out"], mode="nt", name="d_b_out", outs=[F32], tn=1024, tk=512)
    parts = _combine_bwd(do, o_mix, l_g, "combine_bwd")
    dhs, gq = [], []
    for g, (_, dil) in enumerate(B_PATTERNS):
        do_g, dd_g = _to_streams(parts[g][0], dil), _to_streams(parts[g][1], dil)
        lse_g = _to_streams(l_g[g], dil)
        dqkv = _attn_bwd(qkvs[g], do_g, lse_g, dd_g, small["slopes"], dil, f"attn_bwd{g}")
        gq.append(mm_w(hg[g], dqkv, name=f"g_qkv{g}", outs=[MXU_DTYPE], tn=1024))
        dh = mm(dqkv, big["b_w_qkv"], mode="nt", name=f"d_qkv{g}", outs=[F32], tn=1024, tk=768, b_col0=g * 3 * d)
        dhs.append(_from_streams(dh, dil))
    gb["b_w_qkv"] = emit("b_w_qkv", jnp.concatenate(gq, axis=1))
    dx, red2 = _mod_bwd(dxr, dhs, xs[2], scale[2], "mod_bwd2")
    dm[2] = jnp.concatenate([red2[0], red2[1], red[2]])
    dlg[2], dlb[2] = red[0], red[1]
    dx = mlp_bwd(0, 1, dx, h1, a0, r0)
    dxr, dyy, red = _ln_bwd(dx, xs[0], ys[0], gate[0], lg[0], "ln_bwd0")
    gb["a_w_out"] = emit("a_w_out", mm_w(gated, dyy, name="g_a_out", outs=[MXU_DTYPE], tn=1024))
    dgated = mm(dyy, big["a_w_out"], mode="nt", name="d_a_out", outs=[F32], tn=1024, tk=512)
    duv, dws, dbias, dbin, dvg, dvb = _spatial_bwd(uvpre, dgated, small["a_vn_g"], small["a_vn_b"], small["wc"],
                                                   small["wct"], small["bias_full"], "a_spatial_bwd")
    gb["a_w_in"] = emit("a_w_in", mm_w(h0, duv, name="g_a_in", outs=[MXU_DTYPE], tn=1024))
    dh = mm(duv, big["a_w_in"], mode="nt", name="d_a_in", outs=[F32], tn=1024, tk=512)
    dx, red2 = _mod_bwd(dxr, [dh], xs[0], scale[0], "mod_bwd0")
    dm[0] = jnp.concatenate([red2[0], red2[1], red[2]])
    dlg[0], dlb[0] = red[0], red[1]

    tril = jnp.tril(jnp.ones((CHUNK, CHUNK), bool))
    gsmall = {
        "a_b_in": dbin.reshape(-1), "a_vn_g": dvg.reshape(-1), "a_vn_b": dvb.reshape(-1),
        "a_w_s": jnp.where(tril, dws, 0.0).reshape(-1),
        "a_b_s": dbias.reshape(CHUNK, A_GROUPS, d // A_GROUPS).sum(-1).T.reshape(-1),
    }
    return loss, dx, gb, jnp.stack(dm), jnp.stack(dlg), jnp.stack(dlb), gsmall


BIG = ("a_w_in", "a_w_out", "up0", "down0", "b_w_qkv", "b_w_out", "up1", "down1")
BIG_KIND = {"a_w_in": "col", "a_w_out": "row", "b_w_qkv": "col", "b_w_out": "row",
            "up0": "col", "up1": "col", "down0": "row", "down1": "row"}
SMALL = ("a_b_in", "a_vn_g", "a_vn_b", "a_b_s", "a_w_s")


def kernel(x, c, ada_w, ada_b, ln_g, ln_b, a_w_in, a_b_in, a_vn_g, a_vn_b, a_w_s, a_b_s, a_w_out, b_w_qkv, b_w_out, mlp_w_up, mlp_w_down, loss_target, m_ada_w, m_ada_b, m_ln_g, m_ln_b, m_a_w_in, m_a_b_in, m_a_vn_g, m_a_vn_b, m_a_w_s, m_a_b_s, m_a_w_out, m_b_w_qkv, m_b_w_out, m_mlp_w_up, m_mlp_w_down, v_ada_w, v_ada_b, v_ln_g, v_ln_b, v_a_w_in, v_a_b_in, v_a_vn_g, v_a_vn_b, v_a_w_s, v_a_b_s, v_a_w_out, v_b_w_qkv, v_b_w_out, v_mlp_w_up, v_mlp_w_down):
    s, d = x.shape[1], x.shape[2]
    xi, yi, ci = _me()
    q = 2 * xi + yi
    dev = 2 * q + ci
    nsub = 2 * DEPTH
    cs = ada_w.shape[-1]
    ls = ln_g.shape[-1]

    pack = jnp.concatenate([c.reshape(-1), ln_g.reshape(-1), ln_b.reshape(-1)]).reshape(-1, LANES)
    got = _all_gather_small(pack, "gather_small").reshape(N_DEV, -1)
    c_all = got[:, :d]
    per_chip = got[0::2]
    ln_g_full = per_chip[:, d:d + nsub * ls].reshape(N_CHIPS, nsub, ls).transpose(1, 0, 2).reshape(nsub, d)
    ln_b_full = per_chip[:, d + nsub * ls:].reshape(N_CHIPS, nsub, ls).transpose(1, 0, 2).reshape(nsub, d)
    m_part = _ada_fwd(c_all, ada_w.reshape(nsub, d, cs), ada_b.reshape(nsub, 1, cs), "ada_fwd")
    m_all = _all_gather_small(m_part.reshape(-1, LANES), "gather_mod").reshape(N_DEV, nsub, N_DEV, cs)
    m_mine = lax.dynamic_index_in_dim(m_all[0::2], dev, axis=2, keepdims=False)
    mvec = m_mine.transpose(1, 0, 2).reshape(nsub, 3 * d)

    shards = {
        "a_w_in": a_w_in[0], "a_w_out": a_w_out[0], "b_w_qkv": b_w_qkv[0], "b_w_out": b_w_out[0],
        "up0": mlp_w_up[0], "up1": mlp_w_up[1], "down0": mlp_w_down[0], "down1": mlp_w_down[1],
    }
    send_sems, recv_sems, shard_thru, lands, token = _gather_start([shards[k].astype(MXU_DTYPE) for k in BIG], "gather_start")

    def fetch(k, after):
        w = BIG.index(k)
        gw = _gather_wait(w, shard_thru[w], lands[w], send_sems, recv_sems, after, f"gather_wait_{k}")
        return gw if BIG_KIND[k] == "col" else gw.reshape(1, -1, gw.shape[-1])

    scattering = {}

    def emit(k, g):
        scattering[k] = _scatter_start(g, BIG_KIND[k], f"scatter_start_{k}")
        return g

    tril = jnp.tril(jnp.ones((CHUNK, CHUNK), bool))
    wc = jnp.where(tril, a_w_s[0], 0.0).astype(MXU_DTYPE)
    heads = jnp.arange(1, B_HEADS + 1, dtype=F32)
    small = {
        "a_b_in": a_b_in, "a_vn_g": a_vn_g, "a_vn_b": a_vn_b,
        "wc": wc, "wct": wc.transpose(0, 2, 1),
        "bias_full": jnp.repeat(a_b_s[0].T, d // A_GROUPS, axis=1),
        "slopes": jnp.exp2(-8.0 * heads / B_HEADS),
    }

    loss_part, grad_x, gb, dm, dlg, dlb, gsmall = _local_step(x[0] + token[0, 0], loss_target[0], mvec, ln_g_full, ln_b_full, small, fetch, emit)
    loss = lax.psum(loss_part, ("x", "y", "c"))

    weights = dict(ada_w=ada_w, ada_b=ada_b, ln_g=ln_g, ln_b=ln_b, a_w_in=a_w_in, a_b_in=a_b_in, a_vn_g=a_vn_g, a_vn_b=a_vn_b,
                   a_w_s=a_w_s, a_b_s=a_b_s, a_w_out=a_w_out, b_w_qkv=b_w_qkv, b_w_out=b_w_out, mlp_w_up=mlp_w_up, mlp_w_down=mlp_w_down)
    ms = dict(ada_w=m_ada_w, ada_b=m_ada_b, ln_g=m_ln_g, ln_b=m_ln_b, a_w_in=m_a_w_in, a_b_in=m_a_b_in, a_vn_g=m_a_vn_g, a_vn_b=m_a_vn_b,
              a_w_s=m_a_w_s, a_b_s=m_a_b_s, a_w_out=m_a_w_out, b_w_qkv=m_b_w_qkv, b_w_out=m_b_w_out, mlp_w_up=m_mlp_w_up, mlp_w_down=m_mlp_w_down)
    vs = dict(ada_w=v_ada_w, ada_b=v_ada_b, ln_g=v_ln_g, ln_b=v_ln_b, a_w_in=v_a_w_in, a_b_in=v_a_b_in, a_vn_g=v_a_vn_g, a_vn_b=v_a_vn_b,
              a_w_s=v_a_w_s, a_b_s=v_a_b_s, a_w_out=v_a_w_out, b_w_qkv=v_b_w_qkv, b_w_out=v_b_w_out, mlp_w_up=v_mlp_w_up, mlp_w_down=v_mlp_w_down)
    grads, updates = {}, {}

    def update(k):
        updates[k] = _adamw(weights[k], grads[k], ms[k], vs[k], f"adamw_{k}")
        return updates[k][0]

    pack_b = jnp.concatenate([dm.reshape(-1), dlg.reshape(-1), dlb.reshape(-1)] + [gsmall[k] for k in SMALL])
    n_small = pack_b.shape[0]
    pack_b = jnp.pad(pack_b, (0, -n_small % (ROW_TILE * LANES)))
    got_b = _all_gather_small(pack_b.reshape(-1, LANES), "gather_small_grads").reshape(N_DEV, -1, LANES)
    tot = _sum_slots(got_b, "sum_small").reshape(-1)
    o = 0
    dm_tot = tot[o:o + nsub * 3 * d].reshape(nsub, 3 * d); o += nsub * 3 * d
    dlg_tot = tot[o:o + nsub * d].reshape(nsub, d); o += nsub * d
    dlb_tot = tot[o:o + nsub * d].reshape(nsub, d); o += nsub * d
    g_small = {}
    for k, ref in zip(SMALL, (a_b_in, a_vn_g, a_vn_b, a_b_s, a_w_s)):
        g_small[k] = tot[o:o + ref.size].reshape(ref.shape); o += ref.size
    assert o == n_small
    dm_all = got_b.reshape(N_DEV, -1)[:, :nsub * 3 * d].reshape(N_DEV, nsub, 3 * d)
    dm_cols = lax.dynamic_slice_in_dim(dm_all, q * cs, cs, axis=2).transpose(1, 0, 2)

    grads.update({
        "ada_w": _ada_bwd(c_all.T, dm_cols, "ada_bwd").reshape(ada_w.shape),
        "ada_b": lax.dynamic_slice_in_dim(dm_tot, q * cs, cs, axis=1).reshape(ada_b.shape),
        "ln_g": lax.dynamic_slice_in_dim(dlg_tot, q * ls, ls, axis=1).reshape(ln_g.shape),
        "ln_b": lax.dynamic_slice_in_dim(dlb_tot, q * ls, ls, axis=1).reshape(ln_b.shape),
        **g_small,
    })
    for k in ("ada_b", "ln_g", "ln_b") + SMALL:
        update(k)
    done = update("ada_w")

    gfull = {}
    for group in (("down1", "up1", "b_w_out", "b_w_qkv"), ("down0", "up0", "a_w_out", "a_w_in")):
        bufs = [_scatter_wait(*scattering[k], BIG_KIND[k], done, f"scatter_wait_{k}") for k in group]
        halves = [_sum_slots(b, f"sum_{k}") for k, b in zip(group, bufs)]
        fulls = _swap_halves(halves, f"swap_halves_{group[0]}")
        gfull.update({k: f.reshape(-1, f.shape[-1]) for k, f in zip(group, fulls)})
        if group[0] == "down1":
            grads["b_w_qkv"], grads["b_w_out"] = gfull["b_w_qkv"][None], gfull["b_w_out"][None]
            update("b_w_out")
            done = update("b_w_qkv")
    grads.update({
        "a_w_in": gfull["a_w_in"][None], "a_w_out": gfull["a_w_out"][None],
        "mlp_w_up": jnp.stack([gfull["up0"], gfull["up1"]]), "mlp_w_down": jnp.stack([gfull["down0"], gfull["down1"]]),
    })
    for k in ("a_w_in", "a_w_out", "mlp_w_up", "mlp_w_down"):
        update(k)
    names = list(weights)
    return (loss, grad_x[None], *[grads[k] for k in names], *[updates[k][0] for k in names],
            *[updates[k][1] for k in names], *[updates[k][2] for k in names])
```

```python
import functools
import math

import jax
import jax.numpy as jnp
from jax import lax
from jax.experimental import pallas as pl
from jax.experimental.pallas import tpu as pltpu

F32 = jnp.float32
MXU_DTYPE = jnp.bfloat16

DEPTH = 2
CHUNK = 128
A_GROUPS = 16
B_HEADS = 16
HEAD_DIM = 64
B_PATTERNS = ((128, 1), (512, 4), (2048, 16))
SPAN = 128
ALPHA = (2 * DEPTH) ** 0.25
LN_EPS = 1e-5
NEG = -1e30
ATT_SCALE = HEAD_DIM ** -0.5
ADAM_LR, ADAM_B1, ADAM_B2, ADAM_EPS, ADAM_WD, ADAM_STEP = 0.001, 0.9, 0.999, 1e-08, 0.01, 10

N_CHIPS = 4
N_DEV = 8
LANES = 128
SUBLANES = 8
VMEM_LIMIT = 52 * 1024 * 1024
ROW_TILE = 256
MESH = pl.DeviceIdType.MESH


def _cparams(sem):
    return pltpu.CompilerParams(dimension_semantics=sem, vmem_limit_bytes=VMEM_LIMIT)


def _fold8(v):
    r, c = v.shape
    return jnp.sum(v.reshape(r // SUBLANES, SUBLANES, c), axis=0)


def _gelu(x):
    c = math.sqrt(2.0 / math.pi)
    return 0.5 * x * (1.0 + jnp.tanh(c * (x + 0.044715 * (x * x * x))))


def _gelu_grad(x):
    c = math.sqrt(2.0 / math.pi)
    t = jnp.tanh(c * (x + 0.044715 * (x * x * x)))
    return 0.5 * (1.0 + t) + 0.5 * x * (1.0 - t * t) * c * (1.0 + 3.0 * 0.044715 * x * x)


def _dot(a, b, dims):
    return lax.dot_general(a.astype(MXU_DTYPE), b.astype(MXU_DTYPE), (dims, ((), ())), preferred_element_type=F32)


def _dot_nn(a, b):
    return _dot(a, b, ((1,), (0,)))


def _dot_nt(a, b):
    return _dot(a, b, ((1,), (1,)))


def _dot_tn(a, b):
    return _dot(a, b, ((0,), (0,)))


def _mm(a, b, *, mode, name, outs, tm, tn, tk, epi=None, extras=(), b_col0=0, n_out=None):
    if mode == "nn":
        m, kdim = a.shape
        p, kb, ns = b.shape
        assert kb == kdim and ns % tn == 0 and b_col0 % tn == 0
        n = n_out if n_out is not None else p * ns
        npt, j0 = ns // tn, b_col0 // tn
        a_spec = pl.BlockSpec((tm, tk), lambda i, j, k: (i, k))
        b_spec = pl.BlockSpec((None, tk, tn), lambda i, j, k: ((j + j0) // npt, k, (j + j0) % npt))
        dot = _dot_nn
    elif mode == "nt":
        m, kdim = a.shape
        p, n, ns = b.shape
        assert ns % tk == 0 and b_col0 % tk == 0
        npt, j0 = ns // tk, b_col0 // tk
        a_spec = pl.BlockSpec((tm, tk), lambda i, j, k: (i, k))
        b_spec = pl.BlockSpec((None, tn, tk), lambda i, j, k: ((k + j0) // npt, j, (k + j0) % npt))
        dot = _dot_nt
    else:
        kdim, m = a.shape
        kb, n = b.shape
        assert kb == kdim
        a_spec = pl.BlockSpec((tk, tm), lambda i, j, k: (k, i))
        b_spec = pl.BlockSpec((tk, tn), lambda i, j, k: (k, j))
        dot = _dot_tn
    assert m % tm == 0 and n % tn == 0 and kdim % tk == 0, (name, m, n, kdim, tm, tn, tk)
    nk = kdim // tk
    ex_specs, ex_arrays = [], []
    for kind, arr in extras:
        if kind == "row":
            ex_specs.append(pl.BlockSpec((1, tn), lambda i, j, k: (0, j)))
        else:
            ex_specs.append(pl.BlockSpec((tm, tn), lambda i, j, k: (i, j)))
        ex_arrays.append(arr)
    n_ex, n_o = len(ex_arrays), len(outs)

    def body(a_ref, b_ref, *rest):
        ex_refs, o_refs, acc = rest[:n_ex], rest[n_ex:n_ex + n_o], rest[n_ex + n_o]
        k = pl.program_id(2)

        @pl.when(k == 0)
        def _():
            acc[...] = jnp.zeros_like(acc)

        acc[...] += dot(a_ref[...], b_ref[...])

        @pl.when(k == nk - 1)
        def _():
            r = acc[...]
            vals = epi(r, *[e[...] for e in ex_refs]) if epi is not None else [r]
            for o, v in zip(o_refs, vals):
                o[...] = v.astype(o.dtype)

    res = pl.pallas_call(
        body,
        grid=(m // tm, n // tn, nk),
        in_specs=[a_spec, b_spec] + ex_specs,
        out_specs=[pl.BlockSpec((tm, tn), lambda i, j, k: (i, j)) for _ in outs],
        out_shape=[jax.ShapeDtypeStruct((m, n), dt) for dt in outs],
        scratch_shapes=[pltpu.VMEM((tm, tn), F32)],
        name=name,
        compiler_params=_cparams(("parallel", "parallel", "arbitrary")),
    )(a, b, *ex_arrays)
    return res if len(outs) > 1 else res[0]


def _rows(body, n_rows, tr, ins, outs, name, scratch=()):
    def spec(kind, shape):
        if kind == "blk":
            return pl.BlockSpec((tr,) + tuple(shape[1:]), lambda i: (i,) + (0,) * (len(shape) - 1))
        return pl.BlockSpec(tuple(shape), lambda i: (0,) * len(shape))

    return pl.pallas_call(
        body,
        grid=(n_rows // tr,),
        in_specs=[spec(k, a.shape) for k, a in ins],
        out_specs=[spec(k, s) for k, s, _ in outs],
        out_shape=[jax.ShapeDtypeStruct(tuple(s), d) for _, s, d in outs],
        scratch_shapes=list(scratch),
        name=name,
        compiler_params=_cparams(("arbitrary",)),
    )(*[a for _, a in ins])


def _ln_stats(z):
    mu = jnp.mean(z, axis=-1, keepdims=True)
    zc = z - mu
    var = jnp.mean(zc * zc, axis=-1, keepdims=True)
    rstd = lax.rsqrt(var + LN_EPS)
    return zc * rstd, rstd


def _mod(x, scale, shift, name):
    s, d = x.shape

    def body(x_ref, sc_ref, sh_ref, h_ref):
        h_ref[...] = (x_ref[...] * (1.0 + sc_ref[...]) + sh_ref[...]).astype(h_ref.dtype)

    return _rows(body, s, ROW_TILE, [("blk", x), ("all", scale), ("all", shift)], [("blk", (s, d), MXU_DTYPE)], name)[0]


def _resid_ln(x, y, gate, g, b, nxt, name):
    s, d = x.shape
    ins = [("blk", x), ("blk", y), ("all", gate), ("all", g), ("all", b)]
    outs = [("blk", (s, d), F32)]
    if nxt is not None:
        ins += [("all", nxt[0]), ("all", nxt[1])]
        outs += [("blk", (s, d), MXU_DTYPE)]

    def body(x_ref, y_ref, gate_ref, g_ref, b_ref, *rest):
        z = ALPHA * x_ref[...] + gate_ref[...] * y_ref[...]
        xhat, _ = _ln_stats(z)
        xn = xhat * g_ref[...] + b_ref[...]
        if nxt is None:
            rest[0][...] = xn
        else:
            sc_ref, sh_ref, xn_ref, h_ref = rest
            xn_ref[...] = xn
            h_ref[...] = (xn * (1.0 + sc_ref[...]) + sh_ref[...]).astype(h_ref.dtype)

    res = _rows(body, s, ROW_TILE, ins, outs, name)
    return (res[0], res[1]) if nxt is not None else (res[0], None)


def _loss_grad(xf, target, name):
    s, d = xf.shape

    def body(x_ref, t_ref, dy_ref, l_ref, acc):
        i = pl.program_id(0)

        @pl.when(i == 0)
        def _():
            acc[...] = jnp.zeros_like(acc)

        e = x_ref[...] - t_ref[...]
        dy_ref[...] = e * (1.0 / d)
        acc[...] += _fold8(e * e)

        @pl.when(i == pl.num_programs(0) - 1)
        def _():
            l_ref[...] = jnp.full(l_ref.shape, 0.5 / d, F32) * jnp.sum(acc[...])

    dy, l = _rows(body, s, ROW_TILE, [("blk", xf), ("blk", target)],
                  [("blk", (s, d), F32), ("all", (SUBLANES, LANES), F32)], name,
                  scratch=[pltpu.VMEM((SUBLANES, d), F32)])
    return dy, l[0, 0]


def _ln_bwd(dxo, x, y, gate, g, name):
    s, d = x.shape

    def body(dxo_ref, x_ref, y_ref, gate_ref, g_ref, dxr_ref, dyy_ref, red_ref, a_g, a_b, a_gate):
        i = pl.program_id(0)

        @pl.when(i == 0)
        def _():
            a_g[...] = jnp.zeros_like(a_g)
            a_b[...] = jnp.zeros_like(a_b)
            a_gate[...] = jnp.zeros_like(a_gate)

        yv = y_ref[...]
        z = ALPHA * x_ref[...] + gate_ref[...] * yv
        xhat, rstd = _ln_stats(z)
        dxo_v = dxo_ref[...]
        dxh = dxo_v * g_ref[...]
        dz = rstd * (dxh - jnp.mean(dxh, axis=-1, keepdims=True) - xhat * jnp.mean(dxh * xhat, axis=-1, keepdims=True))
        dxr_ref[...] = ALPHA * dz
        dyy_ref[...] = (gate_ref[...] * dz).astype(dyy_ref.dtype)
        a_g[...] += _fold8(dxo_v * xhat)
        a_b[...] += _fold8(dxo_v)
        a_gate[...] += _fold8(dz * yv)

        @pl.when(i == pl.num_programs(0) - 1)
        def _():
            red_ref[...] = jnp.zeros_like(red_ref)
            red_ref[0:1, :] = jnp.sum(a_g[...], axis=0, keepdims=True)
            red_ref[1:2, :] = jnp.sum(a_b[...], axis=0, keepdims=True)
            red_ref[2:3, :] = jnp.sum(a_gate[...], axis=0, keepdims=True)

    return _rows(body, s, ROW_TILE, [("blk", dxo), ("blk", x), ("blk", y), ("all", gate), ("all", g)],
                 [("blk", (s, d), F32), ("blk", (s, d), MXU_DTYPE), ("all", (SUBLANES, d), F32)], name,
                 scratch=[pltpu.VMEM((SUBLANES, d), F32)] * 3)


def _mod_bwd(dxr, dhs, x, scale, name):
    s, d = x.shape
    n_dh = len(dhs)

    def body(dxr_ref, *rest):
        dh_refs = rest[:n_dh]
        x_ref, sc_ref, dx_ref, red_ref, a_sh, a_sc = rest[n_dh:]
        i = pl.program_id(0)

        @pl.when(i == 0)
        def _():
            a_sh[...] = jnp.zeros_like(a_sh)
            a_sc[...] = jnp.zeros_like(a_sc)

        dh = dh_refs[0][...]
        for r in dh_refs[1:]:
            dh = dh + r[...]
        dx_ref[...] = dxr_ref[...] + dh * (1.0 + sc_ref[...])
        a_sh[...] += _fold8(dh)
        a_sc[...] += _fold8(dh * x_ref[...])

        @pl.when(i == pl.num_programs(0) - 1)
        def _():
            red_ref[...] = jnp.zeros_like(red_ref)
            red_ref[0:1, :] = jnp.sum(a_sh[...], axis=0, keepdims=True)
            red_ref[1:2, :] = jnp.sum(a_sc[...], axis=0, keepdims=True)

    return _rows(body, s, ROW_TILE, [("blk", dxr)] + [("blk", h) for h in dhs] + [("blk", x), ("all", scale)],
                 [("blk", (s, d), F32), ("all", (SUBLANES, d), F32)], name,
                 scratch=[pltpu.VMEM((SUBLANES, d), F32)] * 2)


def _left_half(shape):
    return lax.broadcasted_iota(jnp.int32, shape, 1) < (LANES // 2)


def _spatial_z(vn, wc_ref, bias_ref, j):
    vb = vn[:, j * LANES:(j + 1) * LANES]
    z0 = _dot_nn(wc_ref[2 * j], vb)
    z1 = _dot_nn(wc_ref[2 * j + 1], vb)
    return jnp.where(_left_half(z0.shape), z0, z1) + bias_ref[:, j * LANES:(j + 1) * LANES]


def _spatial_fwd(uvpre, vn_g, vn_b, wc, bias_full, name):
    s, d2 = uvpre.shape
    d = d2 // 2

    def body(uv_ref, g_ref, b_ref, wc_ref, bias_ref, out_ref):
        u = _gelu(uv_ref[:, :d])
        v = _gelu(uv_ref[:, d:])
        vh, _ = _ln_stats(v)
        vn = vh * g_ref[...] + b_ref[...]
        for j in range(d // LANES):
            z = _spatial_z(vn, wc_ref, bias_ref, j)
            out_ref[:, j * LANES:(j + 1) * LANES] = (u[:, j * LANES:(j + 1) * LANES] * z).astype(out_ref.dtype)

    return _rows(body, s, CHUNK, [("blk", uvpre), ("all", vn_g), ("all", vn_b), ("all", wc), ("all", bias_full)],
                 [("blk", (s, d), MXU_DTYPE)], name)[0]


def _spatial_bwd(uvpre, dgated, vn_g, vn_b, wc, wct, bias_full, name):
    s, d2 = uvpre.shape
    d = d2 // 2

    def body(uv_ref, dg_ref, g_ref, b_ref, wc_ref, wct_ref, bias_ref,
             duv_ref, dws_ref, dbias_ref, dbin_ref, dvg_ref, dvb_ref, dvn_buf, a_bin, a_vg, a_vb):
        i = pl.program_id(0)

        @pl.when(i == 0)
        def _():
            dws_ref[...] = jnp.zeros_like(dws_ref)
            dbias_ref[...] = jnp.zeros_like(dbias_ref)
            a_bin[...] = jnp.zeros_like(a_bin)
            a_vg[...] = jnp.zeros_like(a_vg)
            a_vb[...] = jnp.zeros_like(a_vb)

        up = uv_ref[:, :d]
        vp = uv_ref[:, d:]
        u = _gelu(up)
        v = _gelu(vp)
        vh, rstd = _ln_stats(v)
        vn = vh * g_ref[...] + b_ref[...]
        dg = dg_ref[...]
        dzz = dg * u
        dbias_ref[...] += dzz
        for j in range(d // LANES):
            cols = slice(j * LANES, (j + 1) * LANES)
            z = _spatial_z(vn, wc_ref, bias_ref, j)
            dup = dg[:, cols] * z * _gelu_grad(up[:, cols])
            duv_ref[:, cols] = dup.astype(duv_ref.dtype)
            a_bin[:, cols] += _fold8(dup)
            dzb = dzz[:, cols]
            left = _left_half(dzb.shape)
            dvn_buf[:, cols] = jnp.where(left, _dot_nn(wct_ref[2 * j], dzb), _dot_nn(wct_ref[2 * j + 1], dzb))
            vb = vn[:, cols]
            dws_ref[2 * j] += _dot_nt(jnp.where(left, dzb, 0.0), vb)
            dws_ref[2 * j + 1] += _dot_nt(jnp.where(left, 0.0, dzb), vb)
        dvn = dvn_buf[...]
        a_vg[...] += _fold8(dvn * vh)
        a_vb[...] += _fold8(dvn)
        dvh = dvn * g_ref[...]
        dv = rstd * (dvh - jnp.mean(dvh, axis=-1, keepdims=True) - vh * jnp.mean(dvh * vh, axis=-1, keepdims=True))
        dvp = dv * _gelu_grad(vp)
        duv_ref[:, d:] = dvp.astype(duv_ref.dtype)
        a_bin[:, d:] += _fold8(dvp)

        @pl.when(i == pl.num_programs(0) - 1)
        def _():
            dbin_ref[...] = jnp.sum(a_bin[...], axis=0, keepdims=True)
            dvg_ref[...] = jnp.sum(a_vg[...], axis=0, keepdims=True)
            dvb_ref[...] = jnp.sum(a_vb[...], axis=0, keepdims=True)

    return _rows(body, s, CHUNK,
                 [("blk", uvpre), ("blk", dgated), ("all", vn_g), ("all", vn_b), ("all", wc), ("all", wct), ("all", bias_full)],
                 [("blk", (s, d2), MXU_DTYPE), ("all", (A_GROUPS, CHUNK, CHUNK), F32), ("all", (CHUNK, d), F32),
                  ("all", (1, d2), F32), ("all", (1, d), F32), ("all", (1, d), F32)], name,
                 scratch=[pltpu.VMEM((CHUNK, d), F32), pltpu.VMEM((SUBLANES, d2), F32),
                          pltpu.VMEM((SUBLANES, d), F32), pltpu.VMEM((SUBLANES, d), F32)])


def _head_mask(v, h):
    lane = lax.broadcasted_iota(jnp.int32, v.shape, 1)
    return jnp.where((lane >= h * HEAD_DIM) & (lane < (h + 1) * HEAD_DIM), v, jnp.zeros_like(v))


def _att_bias(slopes, dil):
    qi = lax.broadcasted_iota(jnp.int32, (SPAN, SPAN), 0)
    ki = lax.broadcasted_iota(jnp.int32, (SPAN, SPAN), 1)
    sl = slopes[:, None, None]
    cur = jnp.where(ki <= qi, -sl * (float(dil) * (qi - ki).astype(F32)), NEG)
    prev = jnp.where(ki >= qi, -sl * (float(dil) * (SPAN + qi - ki).astype(F32)), NEG)
    absent = jnp.full_like(prev, NEG)
    pairs = slopes.shape[0] // 2

    def fwd(pv):
        return jnp.concatenate([cur, pv], axis=2).reshape(pairs, 2 * SPAN, 2 * SPAN)

    def bwd(pv):
        return jnp.concatenate([cur.reshape(pairs, 2 * SPAN, SPAN), pv.reshape(pairs, 2 * SPAN, SPAN)], axis=1)

    return jnp.stack([fwd(absent), fwd(prev)]), jnp.stack([bwd(absent), bwd(prev)])


def _att_specs(s, d, dil, kinds):
    nb = s // (dil * SPAN)

    def rowblk(which, b):
        if which == "prev":
            return jnp.where(b % nb == 0, b, b - 1)
        if which == "next":
            return jnp.where(b % nb == nb - 1, b, b + 1)
        return b

    return [pl.BlockSpec((SPAN, d), functools.partial(lambda b, o, w: (rowblk(w, b), o), o=part, w=which))
            for part, which in kinds]


def _lane_col(v, h):
    return v[:, h * HEAD_DIM:h * HEAD_DIM + 1]


def _attn_fwd(qkv, slopes, dil, name):
    s, d3 = qkv.shape
    d = d3 // 3
    nb = s // (dil * SPAN)
    table, _ = _att_bias(slopes, dil)

    def body(q_ref, kc_ref, kp_ref, vc_ref, vp_ref, tb_ref, o_ref, l_ref):
        left = _left_half((SPAN, LANES))
        for hp in range(d // LANES):
            cols = slice(hp * LANES, (hp + 1) * LANES)
            q = q_ref[:, cols]
            q2 = jnp.concatenate([_head_mask(q, 0), _head_mask(q, 1)], axis=0) * ATT_SCALE
            k2 = jnp.concatenate([kc_ref[:, cols], kp_ref[:, cols]], axis=0)
            v2 = jnp.concatenate([vc_ref[:, cols], vp_ref[:, cols]], axis=0)
            sc = _dot_nt(q2, k2) + tb_ref[hp]
            m = jnp.max(sc, axis=-1, keepdims=True)
            p = jnp.exp(sc - m)
            l = jnp.sum(p, axis=-1, keepdims=True)
            r = _dot_nn(p, v2) * (1.0 / l)
            lse = jnp.broadcast_to(m + jnp.log(l), (2 * SPAN, LANES))
            o_ref[:, cols] = jnp.where(left, r[:SPAN], r[SPAN:])
            l_ref[:, cols] = jnp.where(left, lse[:SPAN], lse[SPAN:])

    specs = _att_specs(s, d, dil, [(0, "cur"), (1, "cur"), (1, "prev"), (2, "cur"), (2, "prev")])
    tbl = pl.BlockSpec((None,) + table.shape[1:], lambda b: (jnp.where(b % nb == 0, 0, 1), 0, 0, 0))
    out_spec = pl.BlockSpec((SPAN, d), lambda b: (b, 0))
    return pl.pallas_call(
        body,
        grid=(s // SPAN,),
        in_specs=specs + [tbl],
        out_specs=[out_spec, out_spec],
        out_shape=[jax.ShapeDtypeStruct((s, d), F32)] * 2,
        name=name,
        compiler_params=_cparams(("parallel",)),
    )(qkv, qkv, qkv, qkv, qkv, table)


def _attn_bwd(qkv, do, lse, dd, slopes, dil, name):
    s, d3 = qkv.shape
    d = d3 // 3
    nb = s // (dil * SPAN)
    _, table = _att_bias(slopes, dil)

    def heads_stacked(cur, nxt):
        return jnp.concatenate([_head_mask(cur, 0), _head_mask(cur, 1), _head_mask(nxt, 0), _head_mask(nxt, 1)], axis=0)

    def cols_stacked(cur, nxt):
        return jnp.concatenate([jnp.broadcast_to(_lane_col(a, h), (SPAN, LANES)) for a in (cur, nxt) for h in range(2)], axis=0)

    def body(k_ref, v_ref, qc_ref, qn_ref, doc_ref, don_ref, lc_ref, ln_ref, ddc_ref, ddn_ref, tb_ref, out_ref, carry):
        b = pl.program_id(0)

        @pl.when(b == 0)
        def _():
            carry[...] = jnp.zeros_like(carry)

        left = _left_half((SPAN, LANES))
        for hp in range(d // LANES):
            cols = slice(hp * LANES, (hp + 1) * LANES)
            k, v = k_ref[:, cols], v_ref[:, cols]
            q4 = heads_stacked(qc_ref[:, cols], qn_ref[:, cols])
            do4 = heads_stacked(doc_ref[:, cols], don_ref[:, cols])
            sc = _dot_nt(q4 * ATT_SCALE, k) + tb_ref[hp]
            p = jnp.exp(sc - cols_stacked(lc_ref[:, cols], ln_ref[:, cols]))
            ds = p * (_dot_nt(do4, v) - cols_stacked(ddc_ref[:, cols], ddn_ref[:, cols]))
            dq4 = _dot_nn(ds, k)
            dq_cur = jnp.where(left, dq4[:SPAN], dq4[SPAN:2 * SPAN]) + carry[:, cols]
            carry[:, cols] = jnp.where(left, dq4[2 * SPAN:3 * SPAN], dq4[3 * SPAN:])
            out_ref[:, cols] = (dq_cur * ATT_SCALE).astype(out_ref.dtype)
            out_ref[:, d + hp * LANES:d + (hp + 1) * LANES] = (_dot_tn(ds, q4) * ATT_SCALE).astype(out_ref.dtype)
            out_ref[:, 2 * d + hp * LANES:2 * d + (hp + 1) * LANES] = _dot_tn(p, do4).astype(out_ref.dtype)

    qkv_specs = _att_specs(s, d, dil, [(1, "cur"), (2, "cur"), (0, "cur"), (0, "next")])
    pair = _att_specs(s, d, dil, [(0, "cur"), (0, "next")])
    tbl = pl.BlockSpec((None,) + table.shape[1:], lambda b: (jnp.where(b % nb == nb - 1, 0, 1), 0, 0, 0))
    return pl.pallas_call(
        body,
        grid=(s // SPAN,),
        in_specs=qkv_specs + pair + pair + pair + [tbl],
        out_specs=pl.BlockSpec((SPAN, d3), lambda b: (b, 0)),
        out_shape=jax.ShapeDtypeStruct((s, d3), MXU_DTYPE),
        scratch_shapes=[pltpu.VMEM((SPAN, d), F32)],
        name=name,
        compiler_params=_cparams(("arbitrary",)),
    )(qkv, qkv, qkv, qkv, do, do, lse, lse, dd, dd, table)


def _mix_weights(l_refs):
    ls = [r[...] for r in l_refs]
    m = functools.reduce(jnp.maximum, ls)
    es = [jnp.exp(l - m) for l in ls]
    tot = functools.reduce(lambda a, c: a + c, es)
    return [e / tot for e in es]


def _combine_fwd(os_, ls_, name):
    s, d = os_[0].shape
    n = len(os_)

    def body(*refs):
        o_refs, l_refs, out_ref = refs[:n], refs[n:2 * n], refs[2 * n]
        ws = _mix_weights(l_refs)
        acc = ws[0] * o_refs[0][...]
        for w, o in zip(ws[1:], o_refs[1:]):
            acc = acc + w * o[...]
        out_ref[...] = acc

    return _rows(body, s, ROW_TILE, [("blk", a) for a in os_ + ls_], [("blk", (s, d), F32)], name)[0]


def _combine_bwd(do, o, ls_, name):
    s, d = o.shape
    n = len(ls_)
    ri = lax.broadcasted_iota(jnp.int32, (LANES, LANES), 0) // HEAD_DIM
    ci = lax.broadcasted_iota(jnp.int32, (LANES, LANES), 1) // HEAD_DIM
    seg = (ri == ci).astype(F32)

    def body(do_ref, o_ref, *rest):
        l_refs, seg_ref, outs = rest[:n], rest[n], rest[n + 1:]
        ws = _mix_weights(l_refs)
        dov = do_ref[...]
        prod = dov * o_ref[...]
        for j in range(d // LANES):
            cols = slice(j * LANES, (j + 1) * LANES)
            r = jnp.dot(prod[:, cols], seg_ref[...], precision=lax.Precision.HIGHEST, preferred_element_type=F32)
            for g in range(n):
                outs[2 * g][:, cols] = (ws[g][:, cols] * dov[:, cols]).astype(outs[2 * g].dtype)
                outs[2 * g + 1][:, cols] = ws[g][:, cols] * r

    outs = []
    for _ in range(n):
        outs += [("blk", (s, d), MXU_DTYPE), ("blk", (s, d), F32)]
    res = _rows(body, s, ROW_TILE, [("blk", do), ("blk", o)] + [("blk", l) for l in ls_] + [("all", seg)], outs, name)
    return [(res[2 * g], res[2 * g + 1]) for g in range(n)]


def _ada_fwd(c_all, w, b, name):
    nsub, d, cs = w.shape

    def body(c_ref, w_ref, b_ref, o_ref):
        cv = c_ref[...]
        sc = cv * (1.0 / (1.0 + jnp.exp(-cv)))
        o_ref[...] = _dot_nn(sc, w_ref[...]) + b_ref[...]

    return pl.pallas_call(
        body,
        grid=(nsub,),
        in_specs=[pl.BlockSpec(c_all.shape, lambda i: (0, 0)), pl.BlockSpec((None, d, cs), lambda i: (i, 0, 0)),
                  pl.BlockSpec((None, 1, cs), lambda i: (i, 0, 0))],
        out_specs=pl.BlockSpec((None, N_DEV, cs), lambda i: (i, 0, 0)),
        out_shape=jax.ShapeDtypeStruct((nsub, N_DEV, cs), F32),
        name=name,
        compiler_params=_cparams(("parallel",)),
    )(c_all, w, b)


def _ada_bwd(c_all_t, dm, name):
    d, nb = c_all_t.shape
    nsub, _, cs = dm.shape

    def body(c_ref, dm_ref, o_ref):
        cv = c_ref[...]
        sc = cv * (1.0 / (1.0 + jnp.exp(-cv)))
        acc = sc[:, 0:1] * dm_ref[0:1, :]
        for bi in range(1, nb):
            acc = acc + sc[:, bi:bi + 1] * dm_ref[bi:bi + 1, :]
        o_ref[...] = acc

    return pl.pallas_call(
        body,
        grid=(nsub,),
        in_specs=[pl.BlockSpec(c_all_t.shape, lambda i: (0, 0)), pl.BlockSpec((None, nb, cs), lambda i: (i, 0, 0))],
        out_specs=pl.BlockSpec((None, d, cs), lambda i: (i, 0, 0)),
        out_shape=jax.ShapeDtypeStruct((nsub, d, cs), F32),
        name=name,
        compiler_params=_cparams(("parallel",)),
    )(c_all_t, dm)


def _row_tile(r, row_elems):
    t = 2 * SUBLANES
    if r % t:
        return r
    while t * 2 * row_elems <= 256 * 1024 and r % (t * 2) == 0:
        t *= 2
    return t


def _adamw(w, g, m, v, name):
    shape = w.shape
    c = shape[-1]
    r = w.size // c
    tr = _row_tile(r, c)
    w2, g2, m2, v2 = [a.reshape(r, c) for a in (w, g, m, v)]
    bc1 = 1.0 - ADAM_B1 ** ADAM_STEP
    bc2 = 1.0 - ADAM_B2 ** ADAM_STEP

    def body(w_ref, g_ref, m_ref, v_ref, d_ref, nm_ref, nv_ref):
        gv = g_ref[...]
        nm = ADAM_B1 * m_ref[...] + (1.0 - ADAM_B1) * gv
        nv = ADAM_B2 * v_ref[...] + (1.0 - ADAM_B2) * (gv * gv)
        d_ref[...] = -ADAM_LR * ((nm / bc1) / (jnp.sqrt(nv / bc2) + ADAM_EPS) + ADAM_WD * w_ref[...])
        nm_ref[...] = nm
        nv_ref[...] = nv

    res = _rows(body, r, tr, [("blk", a) for a in (w2, g2, m2, v2)], [("blk", (r, c), F32)] * 3, name)
    return [a.reshape(shape) for a in res]


def _sum_slots(buf, name):
    n, r, c = buf.shape
    tr = _row_tile(r, n * c)

    def body(b_ref, o_ref):
        acc = b_ref[0].astype(F32)
        for k in range(1, n):
            acc = acc + b_ref[k].astype(F32)
        o_ref[...] = acc

    return pl.pallas_call(
        body,
        grid=(r // tr,),
        in_specs=[pl.BlockSpec((n, tr, c), lambda i: (0, i, 0))],
        out_specs=pl.BlockSpec((tr, c), lambda i: (i, 0)),
        out_shape=jax.ShapeDtypeStruct((r, c), F32),
        name=name,
        compiler_params=_cparams(("parallel",)),
    )(buf)


def _me():
    return lax.axis_index("x"), lax.axis_index("y"), lax.axis_index("c")


def _all_gather_small(blk, name):
    m_per, n = blk.shape

    def body(x_ref, out_ref, send_sems, recv_sems, local_sem):
        x, y, c = _me()
        me, sibling = (x, y, c), (x, y, 1 - c)
        chips = [(1 - x, y), (x, 1 - y), (1 - x, 1 - y)]

        def rows(px, py, pc):
            return out_ref.at[pl.ds((4 * px + 2 * py + pc) * m_per, m_per), :]

        def copy(k, block, to, src=None):
            return pltpu.make_async_remote_copy(
                src_ref=rows(*block) if src is None else src, dst_ref=rows(*block),
                send_sem=send_sems.at[k], recv_sem=recv_sems.at[k], device_id=to, device_id_type=MESH)

        mine = pltpu.make_async_copy(x_ref, rows(*me), local_sem)
        mine.start()
        first = [copy(0, me, sibling, src=x_ref)]
        first += [copy(1 + j, me, (*chip, c), src=x_ref) for j, chip in enumerate(chips)]
        for cp in first:
            cp.start()
        passed = [copy(4 + j, (*chip, c), sibling) for j, chip in enumerate(chips)]
        for j, chip in enumerate(chips):
            copy(1 + j, (*chip, c), me).wait_recv()
            passed[j].start()
        copy(0, sibling, me).wait_recv()
        for j, chip in enumerate(chips):
            copy(4 + j, (*chip, 1 - c), me).wait_recv()
        for cp in first + passed:
            cp.wait_send()
        mine.wait()

    return pl.pallas_call(
        body,
        out_shape=jax.ShapeDtypeStruct((N_DEV * m_per, n), blk.dtype),
        in_specs=[pl.BlockSpec(memory_space=pltpu.VMEM)],
        out_specs=pl.BlockSpec(memory_space=pltpu.VMEM),
        scratch_shapes=[pltpu.SemaphoreType.DMA((7,)), pltpu.SemaphoreType.DMA((7,)), pltpu.SemaphoreType.DMA],
        name=name,
        compiler_params=pltpu.CompilerParams(vmem_limit_bytes=VMEM_LIMIT),
    )(blk)


_HBM = pl.BlockSpec(memory_space=pltpu.HBM)
_SEM = pl.BlockSpec(memory_space=pltpu.SEMAPHORE)
_EFFECT = pltpu.SideEffectType.DATAFLOW_SIDE_EFFECTING


def _other_chips(x, y):
    return [(1 - x, y), (x, 1 - y), (1 - x, 1 - y)]


def _gather_copy(w, j, src_ref, land_ref, send_sems, recv_sems):
    x, y, c = _me()
    return pltpu.make_async_remote_copy(
        src_ref=src_ref, dst_ref=land_ref.at[2 * x + y], send_sem=send_sems.at[3 * w + j], recv_sem=recv_sems.at[3 * w + j],
        device_id=(*_other_chips(x, y)[j], c), device_id_type=MESH)


def _gather_start(shards, after, name):
    n = len(shards)
    lands = [lax.empty((N_CHIPS,) + s.shape, s.dtype) for s in shards]

    def body(*refs):
        in_refs, land_refs = refs[:n], refs[n:2 * n]
        send_sems, recv_sems = refs[2 * n + 1], refs[2 * n + 2]
        token = refs[-1]
        for w in range(n):
            for j in range(3):
                _gather_copy(w, j, in_refs[w], land_refs[w], send_sems, recv_sems).start()
        token[...] = jnp.zeros_like(token)

    res = pl.pallas_call(
        body,
        out_shape=(pltpu.SemaphoreType.DMA((3 * n,)), pltpu.SemaphoreType.DMA((3 * n,)),
                   *[pltpu.HBM(s.shape, s.dtype) for s in shards], *[pltpu.HBM(l.shape, l.dtype) for l in lands],
                   jax.ShapeDtypeStruct((SUBLANES, LANES), F32)),
        in_specs=[_HBM] * (2 * n) + [pl.BlockSpec(memory_space=pl.ANY)],
        out_specs=(_SEM, _SEM, *[_HBM] * (2 * n), pl.BlockSpec(memory_space=pltpu.VMEM)),
        input_output_aliases={i: 2 + i for i in range(2 * n)},
        name=name,
        compiler_params=pltpu.CompilerParams(has_side_effects=_EFFECT),
    )(*[pltpu.with_memory_space_constraint(a, pltpu.HBM) for a in list(shards) + lands], after)
    return res[0], res[1], res[2:2 + n], res[2 + n:2 + 2 * n], res[-1]


def _gather_wait(w, shard, land, send_sems, recv_sems, after, name):
    def body(s_ref, land_ref, send_sems, recv_sems, after_ref, s_out, land_out, stage):
        x, y, _ = _me()
        pltpu.sync_copy(s_ref, stage)
        pltpu.sync_copy(stage, land_out.at[2 * x + y])
        for j in range(3):
            cp = _gather_copy(w, j, s_ref, land_ref, send_sems, recv_sems)
            cp.wait_send()
            cp.wait_recv()

    return pl.pallas_call(
        body,
        out_shape=(pltpu.HBM(shard.shape, shard.dtype), pltpu.HBM(land.shape, land.dtype)),
        in_specs=(_HBM, _HBM, _SEM, _SEM, pl.BlockSpec(memory_space=pl.ANY)),
        out_specs=(_HBM, _HBM),
        input_output_aliases={0: 0, 1: 1},
        scratch_shapes=[pltpu.VMEM(shard.shape, shard.dtype)],
        name=name,
        compiler_params=pltpu.CompilerParams(has_side_effects=_EFFECT, vmem_limit_bytes=VMEM_LIMIT),
    )(shard, land, send_sems, recv_sems, after)[1]


def _piece_shape(shape, kind):
    k, nn = shape
    return (k // 2, nn // N_CHIPS) if kind == "col" else (k // N_CHIPS // 2, nn)


def _piece_of(g_ref, kind, tq, tc):
    pr, pc = _piece_shape(g_ref.shape, kind)
    if kind == "col":
        return g_ref.at[pl.ds(tc * pr, pr), pl.ds(tq * pc, pc)]
    return g_ref.at[pl.ds((2 * tq + tc) * pr, pr), :]


def _scatter_copy(r, kind, g_ref, land_ref, send_sems, recv_sems):
    x, y, c = _me()
    tx, ty, tc = (x + ((r >> 2) & 1)) % 2, (y + ((r >> 1) & 1)) % 2, (c + (r & 1)) % 2
    return pltpu.make_async_remote_copy(
        src_ref=_piece_of(g_ref, kind, 2 * tx + ty, tc), dst_ref=land_ref.at[4 * x + 2 * y + c],
        send_sem=send_sems.at[r], recv_sem=recv_sems.at[r], device_id=(tx, ty, tc), device_id_type=MESH)


def _scatter_start(g, kind, name):
    piece = _piece_shape(g.shape, kind)
    land = lax.empty((N_DEV,) + piece, g.dtype)

    def body(g_ref, land_ref, send_sems, recv_sems, g_out, land_out, stage):
        x, y, c = _me()
        for r in range(1, N_DEV):
            _scatter_copy(r, kind, g_ref, land_ref, send_sems, recv_sems).start()
        pltpu.sync_copy(_piece_of(g_ref, kind, 2 * x + y, c), stage)
        pltpu.sync_copy(stage, land_out.at[4 * x + 2 * y + c])

    return pl.pallas_call(
        body,
        out_shape=(pltpu.SemaphoreType.DMA((N_DEV,)), pltpu.SemaphoreType.DMA((N_DEV,)),
                   pltpu.HBM(g.shape, g.dtype), pltpu.HBM(land.shape, land.dtype)),
        in_specs=[_HBM, _HBM],
        out_specs=(_SEM, _SEM, _HBM, _HBM),
        input_output_aliases={0: 2, 1: 3},
        scratch_shapes=[pltpu.VMEM(piece, g.dtype)],
        name=name,
        compiler_params=pltpu.CompilerParams(has_side_effects=_EFFECT, vmem_limit_bytes=VMEM_LIMIT),
    )(pltpu.with_memory_space_constraint(g, pltpu.HBM), pltpu.with_memory_space_constraint(land, pltpu.HBM))


def _scatter_wait(send_sems, recv_sems, g, land, kind, after, name):
    def body(g_ref, land_ref, send_sems, recv_sems, after_ref, g_out, land_out):
        for r in range(1, N_DEV):
            cp = _scatter_copy(r, kind, g_ref, land_ref, send_sems, recv_sems)
            cp.wait_send()
            cp.wait_recv()

    return pl.pallas_call(
        body,
        out_shape=(pltpu.HBM(g.shape, g.dtype), pltpu.HBM(land.shape, land.dtype)),
        in_specs=(_HBM, _HBM, _SEM, _SEM, pl.BlockSpec(memory_space=pl.ANY)),
        out_specs=(_HBM, _HBM),
        input_output_aliases={0: 0, 1: 1},
        name=name,
        compiler_params=pltpu.CompilerParams(has_side_effects=_EFFECT),
    )(g, land, send_sems, recv_sems, after)[1]


def _swap_halves(halves, name):
    n = len(halves)

    def body(*refs):
        in_refs, out_refs = refs[:n], refs[n:2 * n]
        send_sems, recv_sems, local_sems = refs[2 * n:]
        x, y, c = _me()
        cps = []
        for w in range(n):
            lc = pltpu.make_async_copy(in_refs[w], out_refs[w].at[c], local_sems.at[w])
            lc.start()
            rc = pltpu.make_async_remote_copy(
                src_ref=in_refs[w], dst_ref=out_refs[w].at[c], send_sem=send_sems.at[w], recv_sem=recv_sems.at[w],
                device_id=(x, y, 1 - c), device_id_type=MESH)
            rc.start()
            cps.append((lc, rc))
        for lc, rc in cps:
            rc.wait_recv()
        for lc, rc in cps:
            rc.wait_send()
            lc.wait()

    vmem = pl.BlockSpec(memory_space=pltpu.VMEM)
    return pl.pallas_call(
        body,
        out_shape=[jax.ShapeDtypeStruct((2,) + h.shape, h.dtype) for h in halves],
        in_specs=[vmem] * n,
        out_specs=[vmem] * n,
        scratch_shapes=[pltpu.SemaphoreType.DMA((n,)), pltpu.SemaphoreType.DMA((n,)), pltpu.SemaphoreType.DMA((n,))],
        name=name,
        compiler_params=pltpu.CompilerParams(vmem_limit_bytes=VMEM_LIMIT),
    )(*halves)


def _to_streams(a, dil):
    if dil == 1:
        return a
    s, c = a.shape
    return a.reshape(s // dil, dil, c).transpose(1, 0, 2).reshape(s, c)


def _from_streams(a, dil):
    if dil == 1:
        return a
    s, c = a.shape
    return a.reshape(dil, s // dil, c).transpose(1, 0, 2).reshape(s, c)


def _mm_tiles(s):
    return min(s, 1024)


def _local_step(x0, target, mvec, ln_g, ln_b, small, fetch, emit):
    s, d = x0.shape
    tm = _mm_tiles(s)
    row = lambda v: v.reshape(1, -1)
    shift = [row(mvec[i, :d]) for i in range(4)]
    scale = [row(mvec[i, d:2 * d]) for i in range(4)]
    gate = [row(1.0 + mvec[i, 2 * d:]) for i in range(4)]
    lg = [row(ln_g[i]) for i in range(4)]
    lb = [row(ln_b[i]) for i in range(4)]
    mm = functools.partial(_mm, tm=tm)
    mm_w = functools.partial(_mm, tm=1024, tk=min(s, 512), mode="tn")

    xs, ys, big = [x0], [], {}
    h0 = _mod(x0, scale[0], shift[0], "mod0")
    big["a_w_in"] = fetch("a_w_in", h0)
    uvpre = mm(h0, big["a_w_in"], mode="nn", name="a_in", outs=[F32], tn=512, tk=512,
               epi=lambda r, bias: [r + bias], extras=[("row", small["a_b_in"])])
    gated = _spatial_fwd(uvpre, small["a_vn_g"], small["a_vn_b"], small["wc"], small["bias_full"], "a_spatial")
    big["a_w_out"] = fetch("a_w_out", gated)
    ys.append(mm(gated, big["a_w_out"], mode="nn", name="a_out", outs=[F32], tn=1024, tk=512))
    x1, h1 = _resid_ln(xs[0], ys[0], gate[0], lg[0], lb[0], (scale[1], shift[1]), "ln0")
    xs.append(x1)
    relu2 = lambda r: [r, jnp.square(jnp.maximum(r, 0.0))]
    big["up0"] = fetch("up0", h1)
    a0, r0 = mm(h1, big["up0"], mode="nn", name="up0", outs=[MXU_DTYPE, MXU_DTYPE], tn=1024, tk=512, epi=relu2)
    big["down0"] = fetch("down0", r0)
    ys.append(mm(r0, big["down0"], mode="nn", name="down0", outs=[F32], tn=1024, tk=512))
    x2, h2 = _resid_ln(xs[1], ys[1], gate[1], lg[1], lb[1], (scale[2], shift[2]), "ln1")
    xs.append(x2)
    hg, qkvs, o_g, l_g = [], [], [], []
    big["b_w_qkv"] = fetch("b_w_qkv", h2)
    for g, (_, dil) in enumerate(B_PATTERNS):
        hp = _to_streams(h2, dil)
        qkv = mm(hp, big["b_w_qkv"], mode="nn", name=f"qkv{g}", outs=[MXU_DTYPE], tn=768, tk=512, b_col0=g * 3 * d, n_out=3 * d)
        og, lgv = _attn_fwd(qkv, small["slopes"], dil, f"attn_fwd{g}")
        hg.append(hp)
        qkvs.append(qkv)
        o_g.append(_from_streams(og, dil))
        l_g.append(_from_streams(lgv, dil))
    o_mix = _combine_fwd(o_g, l_g, "combine")
    big["b_w_out"] = fetch("b_w_out", o_mix)
    ys.append(mm(o_mix, big["b_w_out"], mode="nn", name="b_out", outs=[F32], tn=1024, tk=512))
    x3, h3 = _resid_ln(xs[2], ys[2], gate[2], lg[2], lb[2], (scale[3], shift[3]), "ln2")
    xs.append(x3)
    big["up1"] = fetch("up1", h3)
    a1, r1 = mm(h3, big["up1"], mode="nn", name="up1", outs=[MXU_DTYPE, MXU_DTYPE], tn=1024, tk=512, epi=relu2)
    big["down1"] = fetch("down1", r1)
    ys.append(mm(r1, big["down1"], mode="nn", name="down1", outs=[F32], tn=1024, tk=512))
    x4, _ = _resid_ln(xs[3], ys[3], gate[3], lg[3], lb[3], None, "ln3")

    gb, dm, dlg, dlb = {}, [None] * 4, [None] * 4, [None] * 4
    dx, loss = _loss_grad(x4, target, "loss")

    def mlp_bwd(i, sub, dx, h, a, r):
        dxr, dyy, red = _ln_bwd(dx, xs[sub], ys[sub], gate[sub], lg[sub], f"ln_bwd{sub}")
        gb[f"down{i}"] = emit(f"down{i}", mm_w(r, dyy, name=f"g_down{i}", outs=[MXU_DTYPE], tn=1024))
        da = mm(dyy, big[f"down{i}"], mode="nt", name=f"d_down{i}", outs=[MXU_DTYPE], tn=1024, tk=512,
                epi=lambda acc, av: [acc * (2.0 * jnp.maximum(av.astype(F32), 0.0))], extras=[("full", a)])
        gb[f"up{i}"] = emit(f"up{i}", mm_w(h, da, name=f"g_up{i}", outs=[MXU_DTYPE], tn=1024))
        dh = mm(da, big[f"up{i}"], mode="nt", name=f"d_up{i}", outs=[F32], tn=1024, tk=512)
        dx, red2 = _mod_bwd(dxr, [dh], xs[sub], scale[sub], f"mod_bwd{sub}")
        dm[sub] = jnp.concatenate([red2[0], red2[1], red[2]])
        dlg[sub], dlb[sub] = red[0], red[1]
        return dx

    dx = mlp_bwd(1, 3, dx, h3, a1, r1)
    dxr, dyy, red = _ln_bwd(dx, xs[2], ys[2], gate[2], lg[2], "ln_bwd2")
    gb["b_w_out"] = emit("b_w_out", mm_w(o_mix, dyy, name="g_b_out", outs=[MXU_DTYPE], tn=1024))
    do = mm(dyy, big["b_w_out"], mode="nt", name="d_b_out", outs=[F32], tn=1024, tk=512)
    parts = _combine_bwd(do, o_mix, l_g, "combine_bwd")
    dhs, gq = [], []
    for g, (_, dil) in enumerate(B_PATTERNS):
        do_g, dd_g = _to_streams(parts[g][0], dil), _to_streams(parts[g][1], dil)
        lse_g = _to_streams(l_g[g], dil)
        dqkv = _attn_bwd(qkvs[g], do_g, lse_g, dd_g, small["slopes"], dil, f"attn_bwd{g}")
        gq.append(mm_w(hg[g], dqkv, name=f"g_qkv{g}", outs=[MXU_DTYPE], tn=1024))
        dh = mm(dqkv, big["b_w_qkv"], mode="nt", name=f"d_qkv{g}", outs=[F32], tn=1024, tk=768, b_col0=g * 3 * d)
        dhs.append(_from_streams(dh, dil))
    gb["b_w_qkv"] = emit("b_w_qkv", jnp.concatenate(gq, axis=1))
    dx, red2 = _mod_bwd(dxr, dhs, xs[2], scale[2], "mod_bwd2")
    dm[2] = jnp.concatenate([red2[0], red2[1], red[2]])
    dlg[2], dlb[2] = red[0], red[1]
    dx = mlp_bwd(0, 1, dx, h1, a0, r0)
    dxr, dyy, red = _ln_bwd(dx, xs[0], ys[0], gate[0], lg[0], "ln_bwd0")
    gb["a_w_out"] = emit("a_w_out", mm_w(gated, dyy, name="g_a_out", outs=[MXU_DTYPE], tn=1024))
    dgated = mm(dyy, big["a_w_out"], mode="nt", name="d_a_out", outs=[F32], tn=1024, tk=512)
    duv, dws, dbias, dbin, dvg, dvb = _spatial_bwd(uvpre, dgated, small["a_vn_g"], small["a_vn_b"], small["wc"],
                                                   small["wct"], small["bias_full"], "a_spatial_bwd")
    gb["a_w_in"] = emit("a_w_in", mm_w(h0, duv, name="g_a_in", outs=[MXU_DTYPE], tn=1024))
    dh = mm(duv, big["a_w_in"], mode="nt", name="d_a_in", outs=[F32], tn=1024, tk=512)
    dx, red2 = _mod_bwd(dxr, [dh], xs[0], scale[0], "mod_bwd0")
    dm[0] = jnp.concatenate([red2[0], red2[1], red[2]])
    dlg[0], dlb[0] = red[0], red[1]

    tril = jnp.tril(jnp.ones((CHUNK, CHUNK), bool))
    gsmall = {
        "a_b_in": dbin.reshape(-1), "a_vn_g": dvg.reshape(-1), "a_vn_b": dvb.reshape(-1),
        "a_w_s": jnp.where(tril, dws, 0.0).reshape(-1),
        "a_b_s": dbias.reshape(CHUNK, A_GROUPS, d // A_GROUPS).sum(-1).T.reshape(-1),
    }
    return loss, dx, gb, jnp.stack(dm), jnp.stack(dlg), jnp.stack(dlb), gsmall


BIG = ("a_w_in", "a_w_out", "up0", "down0", "b_w_qkv", "b_w_out", "up1", "down1")
BIG_KIND = {"a_w_in": "col", "a_w_out": "row", "b_w_qkv": "col", "b_w_out": "row",
            "up0": "col", "up1": "col", "down0": "row", "down1": "row"}
SMALL = ("a_b_in", "a_vn_g", "a_vn_b", "a_b_s", "a_w_s")


def kernel(x, c, ada_w, ada_b, ln_g, ln_b, a_w_in, a_b_in, a_vn_g, a_vn_b, a_w_s, a_b_s, a_w_out, b_w_qkv, b_w_out, mlp_w_up, mlp_w_down, loss_target, m_ada_w, m_ada_b, m_ln_g, m_ln_b, m_a_w_in, m_a_b_in, m_a_vn_g, m_a_vn_b, m_a_w_s, m_a_b_s, m_a_w_out, m_b_w_qkv, m_b_w_out, m_mlp_w_up, m_mlp_w_down, v_ada_w, v_ada_b, v_ln_g, v_ln_b, v_a_w_in, v_a_b_in, v_a_vn_g, v_a_vn_b, v_a_w_s, v_a_b_s, v_a_w_out, v_b_w_qkv, v_b_w_out, v_mlp_w_up, v_mlp_w_down):
    s, d = x.shape[1], x.shape[2]
    xi, yi, ci = _me()
    q = 2 * xi + yi
    dev = 2 * q + ci
    nsub = 2 * DEPTH
    cs = ada_w.shape[-1]
    ls = ln_g.shape[-1]

    pack = jnp.concatenate([c.reshape(-1), ln_g.reshape(-1), ln_b.reshape(-1)]).reshape(-1, LANES)
    got = _all_gather_small(pack, "gather_small").reshape(N_DEV, -1)
    c_all = got[:, :d]
    per_chip = got[0::2]
    ln_g_full = per_chip[:, d:d + nsub * ls].reshape(N_CHIPS, nsub, ls).transpose(1, 0, 2).reshape(nsub, d)
    ln_b_full = per_chip[:, d + nsub * ls:].reshape(N_CHIPS, nsub, ls).transpose(1, 0, 2).reshape(nsub, d)
    m_part = _ada_fwd(c_all, ada_w.reshape(nsub, d, cs), ada_b.reshape(nsub, 1, cs), "ada_fwd")
    m_all = _all_gather_small(m_part.reshape(-1, LANES), "gather_mod").reshape(N_DEV, nsub, N_DEV, cs)
    m_mine = lax.dynamic_index_in_dim(m_all[0::2], dev, axis=2, keepdims=False)
    mvec = m_mine.transpose(1, 0, 2).reshape(nsub, 3 * d)

    shards = {
        "a_w_in": a_w_in[0], "a_w_out": a_w_out[0], "b_w_qkv": b_w_qkv[0], "b_w_out": b_w_out[0],
        "up0": mlp_w_up[0], "up1": mlp_w_up[1], "down0": mlp_w_down[0], "down1": mlp_w_down[1],
    }
    send_sems, recv_sems, shard_thru, lands, token = _gather_start([shards[k].astype(MXU_DTYPE) for k in BIG], mvec, "gather_start")

    def fetch(k, after):
        w = BIG.index(k)
        gw = _gather_wait(w, shard_thru[w], lands[w], send_sems, recv_sems, after, f"gather_wait_{k}")
        return gw if BIG_KIND[k] == "col" else gw.reshape(1, -1, gw.shape[-1])

    scattering = {}

    def emit(k, g):
        scattering[k] = _scatter_start(g, BIG_KIND[k], f"scatter_start_{k}")
        return g

    tril = jnp.tril(jnp.ones((CHUNK, CHUNK), bool))
    wc = jnp.where(tril, a_w_s[0], 0.0).astype(MXU_DTYPE)
    heads = jnp.arange(1, B_HEADS + 1, dtype=F32)
    small = {
        "a_b_in": a_b_in, "a_vn_g": a_vn_g, "a_vn_b": a_vn_b,
        "wc": wc, "wct": wc.transpose(0, 2, 1),
        "bias_full": jnp.repeat(a_b_s[0].T, d // A_GROUPS, axis=1),
        "slopes": jnp.exp2(-8.0 * heads / B_HEADS),
    }

    loss_part, grad_x, gb, dm, dlg, dlb, gsmall = _local_step(x[0] + token[0, 0], loss_target[0], mvec, ln_g_full, ln_b_full, small, fetch, emit)
    loss = lax.psum(loss_part, ("x", "y", "c"))

    weights = dict(ada_w=ada_w, ada_b=ada_b, ln_g=ln_g, ln_b=ln_b, a_w_in=a_w_in, a_b_in=a_b_in, a_vn_g=a_vn_g, a_vn_b=a_vn_b,
                   a_w_s=a_w_s, a_b_s=a_b_s, a_w_out=a_w_out, b_w_qkv=b_w_qkv, b_w_out=b_w_out, mlp_w_up=mlp_w_up, mlp_w_down=mlp_w_down)
    ms = dict(ada_w=m_ada_w, ada_b=m_ada_b, ln_g=m_ln_g, ln_b=m_ln_b, a_w_in=m_a_w_in, a_b_in=m_a_b_in, a_vn_g=m_a_vn_g, a_vn_b=m_a_vn_b,
              a_w_s=m_a_w_s, a_b_s=m_a_b_s, a_w_out=m_a_w_out, b_w_qkv=m_b_w_qkv, b_w_out=m_b_w_out, mlp_w_up=m_mlp_w_up, mlp_w_down=m_mlp_w_down)
    vs = dict(ada_w=v_ada_w, ada_b=v_ada_b, ln_g=v_ln_g, ln_b=v_ln_b, a_w_in=v_a_w_in, a_b_in=v_a_b_in, a_vn_g=v_a_vn_g, a_vn_b=v_a_vn_b,
              a_w_s=v_a_w_s, a_b_s=v_a_b_s, a_w_out=v_a_w_out, b_w_qkv=v_b_w_qkv, b_w_out=v_b_w_out, mlp_w_up=v_mlp_w_up, mlp_w_down=v_mlp_w_down)
    grads, updates = {}, {}

    def update(k):
        updates[k] = _adamw(weights[k], grads[k], ms[k], vs[k], f"adamw_{k}")
        return updates[k][0]

    pack_b = jnp.concatenate([dm.reshape(-1), dlg.reshape(-1), dlb.reshape(-1)] + [gsmall[k] for k in SMALL])
    n_small = pack_b.shape[0]
    pack_b = jnp.pad(pack_b, (0, -n_small % (ROW_TILE * LANES)))
    got_b = _all_gather_small(pack_b.reshape(-1, LANES), "gather_small_grads").reshape(N_DEV, -1, LANES)
    tot = _sum_slots(got_b, "sum_small").reshape(-1)
    o = 0
    dm_tot = tot[o:o + nsub * 3 * d].reshape(nsub, 3 * d); o += nsub * 3 * d
    dlg_tot = tot[o:o + nsub * d].reshape(nsub, d); o += nsub * d
    dlb_tot = tot[o:o + nsub * d].reshape(nsub, d); o += nsub * d
    g_small = {}
    for k, ref in zip(SMALL, (a_b_in, a_vn_g, a_vn_b, a_b_s, a_w_s)):
        g_small[k] = tot[o:o + ref.size].reshape(ref.shape); o += ref.size
    assert o == n_small
    dm_all = got_b.reshape(N_DEV, -1)[:, :nsub * 3 * d].reshape(N_DEV, nsub, 3 * d)
    dm_cols = lax.dynamic_slice_in_dim(dm_all, q * cs, cs, axis=2).transpose(1, 0, 2)

    grads.update({
        "ada_w": _ada_bwd(c_all.T, dm_cols, "ada_bwd").reshape(ada_w.shape),
        "ada_b": lax.dynamic_slice_in_dim(dm_tot, q * cs, cs, axis=1).reshape(ada_b.shape),
        "ln_g": lax.dynamic_slice_in_dim(dlg_tot, q * ls, ls, axis=1).reshape(ln_g.shape),
        "ln_b": lax.dynamic_slice_in_dim(dlb_tot, q * ls, ls, axis=1).reshape(ln_b.shape),
        **g_small,
    })
    for k in ("ada_b", "ln_g", "ln_b") + SMALL:
        update(k)
    done = update("ada_w")

    gfull = {}
    for group in (("down1", "up1", "b_w_out", "b_w_qkv"), ("down0", "up0", "a_w_out", "a_w_in")):
        bufs = [_scatter_wait(*scattering[k], BIG_KIND[k], done, f"scatter_wait_{k}") for k in group]
        halves = [_sum_slots(b, f"sum_{k}") for k, b in zip(group, bufs)]
        fulls = _swap_halves(halves, f"swap_halves_{group[0]}")
        gfull.update({k: f.reshape(-1, f.shape[-1]) for k, f in zip(group, fulls)})
        if group[0] == "down1":
            grads["b_w_qkv"], grads["b_w_out"] = gfull["b_w_qkv"][None], gfull["b_w_out"][None]
            update("b_w_out")
            done = update("b_w_qkv")
    grads.update({
        "a_w_in": gfull["a_w_in"][None], "a_w_out": gfull["a_w_out"][None],
        "mlp_w_up": jnp.stack([gfull["up0"], gfull["up1"]]), "mlp_w_down": jnp.stack([gfull["down0"], gfull["down1"]]),
    })
    for k in ("a_w_in", "a_w_out", "mlp_w_up", "mlp_w_down"):
        update(k)
    names = list(weights)
    return (loss, grad_x[None], *[grads[k] for k in names], *[updates[k][0] for k in names],
            *[updates[k][1] for k in names], *[updates[k][2] for k in names])
```

```python
import functools
import math

import jax
import jax.numpy as jnp
from jax import lax
from jax.experimental import pallas as pl
from jax.experimental.pallas import tpu as pltpu

F32 = jnp.float32
MXU_DTYPE = jnp.bfloat16

DEPTH = 2
CHUNK = 128
A_GROUPS = 16
B_HEADS = 16
HEAD_DIM = 64
B_PATTERNS = ((128, 1), (512, 4), (2048, 16))
SPAN = 128
ALPHA = (2 * DEPTH) ** 0.25
LN_EPS = 1e-5
NEG = -1e30
ATT_SCALE = HEAD_DIM ** -0.5
ADAM_LR, ADAM_B1, ADAM_B2, ADAM_EPS, ADAM_WD, ADAM_STEP = 0.001, 0.9, 0.999, 1e-08, 0.01, 10

N_CHIPS = 4
N_DEV = 8
LANES = 128
SUBLANES = 8
VMEM_LIMIT = 52 * 1024 * 1024
ROW_TILE = 256
MESH = pl.DeviceIdType.MESH


def _cparams(sem):
    return pltpu.CompilerParams(dimension_semantics=sem, vmem_limit_bytes=VMEM_LIMIT)


def _fold8(v):
    r, c = v.shape
    return jnp.sum(v.reshape(r // SUBLANES, SUBLANES, c), axis=0)


def _gelu(x):
    c = math.sqrt(2.0 / math.pi)
    return 0.5 * x * (1.0 + jnp.tanh(c * (x + 0.044715 * (x * x * x))))


def _gelu_grad(x):
    c = math.sqrt(2.0 / math.pi)
    t = jnp.tanh(c * (x + 0.044715 * (x * x * x)))
    return 0.5 * (1.0 + t) + 0.5 * x * (1.0 - t * t) * c * (1.0 + 3.0 * 0.044715 * x * x)


def _dot(a, b, dims):
    return lax.dot_general(a.astype(MXU_DTYPE), b.astype(MXU_DTYPE), (dims, ((), ())), preferred_element_type=F32)


def _dot_nn(a, b):
    return _dot(a, b, ((1,), (0,)))


def _dot_nt(a, b):
    return _dot(a, b, ((1,), (1,)))


def _dot_tn(a, b):
    return _dot(a, b, ((0,), (0,)))


def _mm(a, b, *, mode, name, outs, tm, tn, tk, epi=None, extras=(), b_col0=0, n_out=None, after=None):
    if mode == "nn":
        m, kdim = a.shape
        p, kb, ns = b.shape
        assert kb == kdim and ns % tn == 0 and b_col0 % tn == 0
        n = n_out if n_out is not None else p * ns
        npt, j0 = ns // tn, b_col0 // tn
        a_spec = pl.BlockSpec((tm, tk), lambda i, j, k: (i, k))
        b_spec = pl.BlockSpec((None, tk, tn), lambda i, j, k: ((j + j0) // npt, k, (j + j0) % npt))
        dot = _dot_nn
    elif mode == "nt":
        m, kdim = a.shape
        p, n, ns = b.shape
        assert ns % tk == 0 and b_col0 % tk == 0
        npt, j0 = ns // tk, b_col0 // tk
        a_spec = pl.BlockSpec((tm, tk), lambda i, j, k: (i, k))
        b_spec = pl.BlockSpec((None, tn, tk), lambda i, j, k: ((k + j0) // npt, j, (k + j0) % npt))
        dot = _dot_nt
    else:
        kdim, m = a.shape
        kb, n = b.shape
        assert kb == kdim
        a_spec = pl.BlockSpec((tk, tm), lambda i, j, k: (k, i))
        b_spec = pl.BlockSpec((tk, tn), lambda i, j, k: (k, j))
        dot = _dot_tn
    assert m % tm == 0 and n % tn == 0 and kdim % tk == 0, (name, m, n, kdim, tm, tn, tk)
    nk = kdim // tk
    ex_specs, ex_arrays = [], []
    for kind, arr in extras:
        if kind == "row":
            ex_specs.append(pl.BlockSpec((1, tn), lambda i, j, k: (0, j)))
        else:
            ex_specs.append(pl.BlockSpec((tm, tn), lambda i, j, k: (i, j)))
        ex_arrays.append(arr)
    n_ex, n_o = len(ex_arrays), len(outs)
    n_dep = 0 if after is None else 1
    deps = [] if after is None else [after]

    def body(a_ref, b_ref, *rest):
        ex_refs, o_refs, acc = rest[:n_ex], rest[n_ex + n_dep:n_ex + n_dep + n_o], rest[n_ex + n_dep + n_o]
        k = pl.program_id(2)

        @pl.when(k == 0)
        def _():
            acc[...] = jnp.zeros_like(acc)

        acc[...] += dot(a_ref[...], b_ref[...])

        @pl.when(k == nk - 1)
        def _():
            r = acc[...]
            vals = epi(r, *[e[...] for e in ex_refs]) if epi is not None else [r]
            for o, v in zip(o_refs, vals):
                o[...] = v.astype(o.dtype)

    res = pl.pallas_call(
        body,
        grid=(m // tm, n // tn, nk),
        in_specs=[a_spec, b_spec] + ex_specs + [pl.BlockSpec(memory_space=pl.ANY)] * n_dep,
        out_specs=[pl.BlockSpec((tm, tn), lambda i, j, k: (i, j)) for _ in outs],
        out_shape=[jax.ShapeDtypeStruct((m, n), dt) for dt in outs],
        scratch_shapes=[pltpu.VMEM((tm, tn), F32)],
        name=name,
        compiler_params=_cparams(("parallel", "parallel", "arbitrary")),
    )(a, b, *ex_arrays, *deps)
    return res if len(outs) > 1 else res[0]


def _rows(body, n_rows, tr, ins, outs, name, scratch=()):
    def spec(kind, shape):
        if kind == "blk":
            return pl.BlockSpec((tr,) + tuple(shape[1:]), lambda i: (i,) + (0,) * (len(shape) - 1))
        if kind == "dep":
            return pl.BlockSpec(memory_space=pl.ANY)
        return pl.BlockSpec(tuple(shape), lambda i: (0,) * len(shape))

    return pl.pallas_call(
        body,
        grid=(n_rows // tr,),
        in_specs=[spec(k, a.shape) for k, a in ins],
        out_specs=[spec(k, s) for k, s, _ in outs],
        out_shape=[jax.ShapeDtypeStruct(tuple(s), d) for _, s, d in outs],
        scratch_shapes=list(scratch),
        name=name,
        compiler_params=_cparams(("arbitrary",)),
    )(*[a for _, a in ins])


def _ln_stats(z):
    mu = jnp.mean(z, axis=-1, keepdims=True)
    zc = z - mu
    var = jnp.mean(zc * zc, axis=-1, keepdims=True)
    rstd = lax.rsqrt(var + LN_EPS)
    return zc * rstd, rstd


def _mod(x, scale, shift, after, name):
    s, d = x.shape

    def body(x_ref, sc_ref, sh_ref, dep_ref, h_ref):
        h_ref[...] = (x_ref[...] * (1.0 + sc_ref[...]) + sh_ref[...]).astype(h_ref.dtype)

    return _rows(body, s, ROW_TILE, [("blk", x), ("all", scale), ("all", shift), ("dep", after)], [("blk", (s, d), MXU_DTYPE)], name)[0]


def _resid_ln(x, y, gate, g, b, nxt, name):
    s, d = x.shape
    ins = [("blk", x), ("blk", y), ("all", gate), ("all", g), ("all", b)]
    outs = [("blk", (s, d), F32)]
    if nxt is not None:
        ins += [("all", nxt[0]), ("all", nxt[1])]
        outs += [("blk", (s, d), MXU_DTYPE)]

    def body(x_ref, y_ref, gate_ref, g_ref, b_ref, *rest):
        z = ALPHA * x_ref[...] + gate_ref[...] * y_ref[...]
        xhat, _ = _ln_stats(z)
        xn = xhat * g_ref[...] + b_ref[...]
        if nxt is None:
            rest[0][...] = xn
        else:
            sc_ref, sh_ref, xn_ref, h_ref = rest
            xn_ref[...] = xn
            h_ref[...] = (xn * (1.0 + sc_ref[...]) + sh_ref[...]).astype(h_ref.dtype)

    res = _rows(body, s, ROW_TILE, ins, outs, name)
    return (res[0], res[1]) if nxt is not None else (res[0], None)


def _loss_grad(xf, target, name):
    s, d = xf.shape

    def body(x_ref, t_ref, dy_ref, l_ref, acc):
        i = pl.program_id(0)

        @pl.when(i == 0)
        def _():
            acc[...] = jnp.zeros_like(acc)

        e = x_ref[...] - t_ref[...]
        dy_ref[...] = e * (1.0 / d)
        acc[...] += _fold8(e * e)

        @pl.when(i == pl.num_programs(0) - 1)
        def _():
            l_ref[...] = jnp.full(l_ref.shape, 0.5 / d, F32) * jnp.sum(acc[...])

    dy, l = _rows(body, s, ROW_TILE, [("blk", xf), ("blk", target)],
                  [("blk", (s, d), F32), ("all", (SUBLANES, LANES), F32)], name,
                  scratch=[pltpu.VMEM((SUBLANES, d), F32)])
    return dy, l[0, 0]


def _ln_bwd(dxo, x, y, gate, g, name):
    s, d = x.shape

    def body(dxo_ref, x_ref, y_ref, gate_ref, g_ref, dxr_ref, dyy_ref, red_ref, a_g, a_b, a_gate):
        i = pl.program_id(0)

        @pl.when(i == 0)
        def _():
            a_g[...] = jnp.zeros_like(a_g)
            a_b[...] = jnp.zeros_like(a_b)
            a_gate[...] = jnp.zeros_like(a_gate)

        yv = y_ref[...]
        z = ALPHA * x_ref[...] + gate_ref[...] * yv
        xhat, rstd = _ln_stats(z)
        dxo_v = dxo_ref[...]
        dxh = dxo_v * g_ref[...]
        dz = rstd * (dxh - jnp.mean(dxh, axis=-1, keepdims=True) - xhat * jnp.mean(dxh * xhat, axis=-1, keepdims=True))
        dxr_ref[...] = ALPHA * dz
        dyy_ref[...] = (gate_ref[...] * dz).astype(dyy_ref.dtype)
        a_g[...] += _fold8(dxo_v * xhat)
        a_b[...] += _fold8(dxo_v)
        a_gate[...] += _fold8(dz * yv)

        @pl.when(i == pl.num_programs(0) - 1)
        def _():
            red_ref[...] = jnp.zeros_like(red_ref)
            red_ref[0:1, :] = jnp.sum(a_g[...], axis=0, keepdims=True)
            red_ref[1:2, :] = jnp.sum(a_b[...], axis=0, keepdims=True)
            red_ref[2:3, :] = jnp.sum(a_gate[...], axis=0, keepdims=True)

    return _rows(body, s, ROW_TILE, [("blk", dxo), ("blk", x), ("blk", y), ("all", gate), ("all", g)],
                 [("blk", (s, d), F32), ("blk", (s, d), MXU_DTYPE), ("all", (SUBLANES, d), F32)], name,
                 scratch=[pltpu.VMEM((SUBLANES, d), F32)] * 3)


def _mod_bwd(dxr, dhs, x, scale, name, after=None):
    s, d = x.shape
    n_dh = len(dhs)
    n_dep = 0 if after is None else 1

    def body(dxr_ref, *rest):
        dh_refs = rest[:n_dh]
        x_ref, sc_ref, dx_ref, red_ref, a_sh, a_sc = rest[n_dh:n_dh + 2] + rest[n_dh + 2 + n_dep:]
        i = pl.program_id(0)

        @pl.when(i == 0)
        def _():
            a_sh[...] = jnp.zeros_like(a_sh)
            a_sc[...] = jnp.zeros_like(a_sc)

        dh = dh_refs[0][...]
        for r in dh_refs[1:]:
            dh = dh + r[...]
        dx_ref[...] = dxr_ref[...] + dh * (1.0 + sc_ref[...])
        a_sh[...] += _fold8(dh)
        a_sc[...] += _fold8(dh * x_ref[...])

        @pl.when(i == pl.num_programs(0) - 1)
        def _():
            red_ref[...] = jnp.zeros_like(red_ref)
            red_ref[0:1, :] = jnp.sum(a_sh[...], axis=0, keepdims=True)
            red_ref[1:2, :] = jnp.sum(a_sc[...], axis=0, keepdims=True)

    return _rows(body, s, ROW_TILE, [("blk", dxr)] + [("blk", h) for h in dhs] + [("blk", x), ("all", scale)] + [("dep", after)] * n_dep,
                 [("blk", (s, d), F32), ("all", (SUBLANES, d), F32)], name,
                 scratch=[pltpu.VMEM((SUBLANES, d), F32)] * 2)


def _left_half(shape):
    return lax.broadcasted_iota(jnp.int32, shape, 1) < (LANES // 2)


def _spatial_z(vn, wc_ref, bias_ref, j):
    vb = vn[:, j * LANES:(j + 1) * LANES]
    z0 = _dot_nn(wc_ref[2 * j], vb)
    z1 = _dot_nn(wc_ref[2 * j + 1], vb)
    return jnp.where(_left_half(z0.shape), z0, z1) + bias_ref[:, j * LANES:(j + 1) * LANES]


def _spatial_fwd(uvpre, vn_g, vn_b, wc, bias_full, name):
    s, d2 = uvpre.shape
    d = d2 // 2

    def body(uv_ref, g_ref, b_ref, wc_ref, bias_ref, out_ref):
        u = _gelu(uv_ref[:, :d])
        v = _gelu(uv_ref[:, d:])
        vh, _ = _ln_stats(v)
        vn = vh * g_ref[...] + b_ref[...]
        for j in range(d // LANES):
            z = _spatial_z(vn, wc_ref, bias_ref, j)
            out_ref[:, j * LANES:(j + 1) * LANES] = (u[:, j * LANES:(j + 1) * LANES] * z).astype(out_ref.dtype)

    return _rows(body, s, CHUNK, [("blk", uvpre), ("all", vn_g), ("all", vn_b), ("all", wc), ("all", bias_full)],
                 [("blk", (s, d), MXU_DTYPE)], name)[0]


def _spatial_bwd(uvpre, dgated, vn_g, vn_b, wc, wct, bias_full, name):
    s, d2 = uvpre.shape
    d = d2 // 2

    def body(uv_ref, dg_ref, g_ref, b_ref, wc_ref, wct_ref, bias_ref,
             duv_ref, dws_ref, dbias_ref, dbin_ref, dvg_ref, dvb_ref, dvn_buf, a_bin, a_vg, a_vb):
        i = pl.program_id(0)

        @pl.when(i == 0)
        def _():
            dws_ref[...] = jnp.zeros_like(dws_ref)
            dbias_ref[...] = jnp.zeros_like(dbias_ref)
            a_bin[...] = jnp.zeros_like(a_bin)
            a_vg[...] = jnp.zeros_like(a_vg)
            a_vb[...] = jnp.zeros_like(a_vb)

        up = uv_ref[:, :d]
        vp = uv_ref[:, d:]
        u = _gelu(up)
        v = _gelu(vp)
        vh, rstd = _ln_stats(v)
        vn = vh * g_ref[...] + b_ref[...]
        dg = dg_ref[...]
        dzz = dg * u
        dbias_ref[...] += dzz
        for j in range(d // LANES):
            cols = slice(j * LANES, (j + 1) * LANES)
            z = _spatial_z(vn, wc_ref, bias_ref, j)
            dup = dg[:, cols] * z * _gelu_grad(up[:, cols])
            duv_ref[:, cols] = dup.astype(duv_ref.dtype)
            a_bin[:, cols] += _fold8(dup)
            dzb = dzz[:, cols]
            left = _left_half(dzb.shape)
            dvn_buf[:, cols] = jnp.where(left, _dot_nn(wct_ref[2 * j], dzb), _dot_nn(wct_ref[2 * j + 1], dzb))
            vb = vn[:, cols]
            dws_ref[2 * j] += _dot_nt(jnp.where(left, dzb, 0.0), vb)
            dws_ref[2 * j + 1] += _dot_nt(jnp.where(left, 0.0, dzb), vb)
        dvn = dvn_buf[...]
        a_vg[...] += _fold8(dvn * vh)
        a_vb[...] += _fold8(dvn)
        dvh = dvn * g_ref[...]
        dv = rstd * (dvh - jnp.mean(dvh, axis=-1, keepdims=True) - vh * jnp.mean(dvh * vh, axis=-1, keepdims=True))
        dvp = dv * _gelu_grad(vp)
        duv_ref[:, d:] = dvp.astype(duv_ref.dtype)
        a_bin[:, d:] += _fold8(dvp)

        @pl.when(i == pl.num_programs(0) - 1)
        def _():
            dbin_ref[...] = jnp.sum(a_bin[...], axis=0, keepdims=True)
            dvg_ref[...] = jnp.sum(a_vg[...], axis=0, keepdims=True)
            dvb_ref[...] = jnp.sum(a_vb[...], axis=0, keepdims=True)

    return _rows(body, s, CHUNK,
                 [("blk", uvpre), ("blk", dgated), ("all", vn_g), ("all", vn_b), ("all", wc), ("all", wct), ("all", bias_full)],
                 [("blk", (s, d2), MXU_DTYPE), ("all", (A_GROUPS, CHUNK, CHUNK), F32), ("all", (CHUNK, d), F32),
                  ("all", (1, d2), F32), ("all", (1, d), F32), ("all", (1, d), F32)], name,
                 scratch=[pltpu.VMEM((CHUNK, d), F32), pltpu.VMEM((SUBLANES, d2), F32),
                          pltpu.VMEM((SUBLANES, d), F32), pltpu.VMEM((SUBLANES, d), F32)])


def _head_mask(v, h):
    lane = lax.broadcasted_iota(jnp.int32, v.shape, 1)
    return jnp.where((lane >= h * HEAD_DIM) & (lane < (h + 1) * HEAD_DIM), v, jnp.zeros_like(v))


def _att_bias(slopes, dil):
    qi = lax.broadcasted_iota(jnp.int32, (SPAN, SPAN), 0)
    ki = lax.broadcasted_iota(jnp.int32, (SPAN, SPAN), 1)
    sl = slopes[:, None, None]
    cur = jnp.where(ki <= qi, -sl * (float(dil) * (qi - ki).astype(F32)), NEG)
    prev = jnp.where(ki >= qi, -sl * (float(dil) * (SPAN + qi - ki).astype(F32)), NEG)
    absent = jnp.full_like(prev, NEG)
    pairs = slopes.shape[0] // 2

    def fwd(pv):
        return jnp.concatenate([cur, pv], axis=2).reshape(pairs, 2 * SPAN, 2 * SPAN)

    def bwd(pv):
        return jnp.concatenate([cur.reshape(pairs, 2 * SPAN, SPAN), pv.reshape(pairs, 2 * SPAN, SPAN)], axis=1)

    return jnp.stack([fwd(absent), fwd(prev)]), jnp.stack([bwd(absent), bwd(prev)])


def _att_specs(s, d, dil, kinds):
    nb = s // (dil * SPAN)

    def rowblk(which, b):
        if which == "prev":
            return jnp.where(b % nb == 0, b, b - 1)
        if which == "next":
            return jnp.where(b % nb == nb - 1, b, b + 1)
        return b

    return [pl.BlockSpec((SPAN, d), functools.partial(lambda b, o, w: (rowblk(w, b), o), o=part, w=which))
            for part, which in kinds]


def _lane_col(v, h):
    return v[:, h * HEAD_DIM:h * HEAD_DIM + 1]


def _attn_fwd(qkv, slopes, dil, name):
    s, d3 = qkv.shape
    d = d3 // 3
    nb = s // (dil * SPAN)
    table, _ = _att_bias(slopes, dil)

    def body(q_ref, kc_ref, kp_ref, vc_ref, vp_ref, tb_ref, o_ref, l_ref):
        left = _left_half((SPAN, LANES))
        for hp in range(d // LANES):
            cols = slice(hp * LANES, (hp + 1) * LANES)
            q = q_ref[:, cols]
            q2 = jnp.concatenate([_head_mask(q, 0), _head_mask(q, 1)], axis=0) * ATT_SCALE
            k2 = jnp.concatenate([kc_ref[:, cols], kp_ref[:, cols]], axis=0)
            v2 = jnp.concatenate([vc_ref[:, cols], vp_ref[:, cols]], axis=0)
            sc = _dot_nt(q2, k2) + tb_ref[hp]
            m = jnp.max(sc, axis=-1, keepdims=True)
            p = jnp.exp(sc - m)
            l = jnp.sum(p, axis=-1, keepdims=True)
            r = _dot_nn(p, v2) * (1.0 / l)
            lse = jnp.broadcast_to(m + jnp.log(l), (2 * SPAN, LANES))
            o_ref[:, cols] = jnp.where(left, r[:SPAN], r[SPAN:])
            l_ref[:, cols] = jnp.where(left, lse[:SPAN], lse[SPAN:])

    specs = _att_specs(s, d, dil, [(0, "cur"), (1, "cur"), (1, "prev"), (2, "cur"), (2, "prev")])
    tbl = pl.BlockSpec((None,) + table.shape[1:], lambda b: (jnp.where(b % nb == 0, 0, 1), 0, 0, 0))
    out_spec = pl.BlockSpec((SPAN, d), lambda b: (b, 0))
    return pl.pallas_call(
        body,
        grid=(s // SPAN,),
        in_specs=specs + [tbl],
        out_specs=[out_spec, out_spec],
        out_shape=[jax.ShapeDtypeStruct((s, d), F32)] * 2,
        name=name,
        compiler_params=_cparams(("parallel",)),
    )(qkv, qkv, qkv, qkv, qkv, table)


def _attn_bwd(qkv, do, lse, dd, slopes, dil, name):
    s, d3 = qkv.shape
    d = d3 // 3
    nb = s // (dil * SPAN)
    _, table = _att_bias(slopes, dil)

    def heads_stacked(cur, nxt):
        return jnp.concatenate([_head_mask(cur, 0), _head_mask(cur, 1), _head_mask(nxt, 0), _head_mask(nxt, 1)], axis=0)

    def cols_stacked(cur, nxt):
        return jnp.concatenate([jnp.broadcast_to(_lane_col(a, h), (SPAN, LANES)) for a in (cur, nxt) for h in range(2)], axis=0)

    def body(k_ref, v_ref, qc_ref, qn_ref, doc_ref, don_ref, lc_ref, ln_ref, ddc_ref, ddn_ref, tb_ref, out_ref, carry):
        b = pl.program_id(0)

        @pl.when(b == 0)
        def _():
            carry[...] = jnp.zeros_like(carry)

        left = _left_half((SPAN, LANES))
        for hp in range(d // LANES):
            cols = slice(hp * LANES, (hp + 1) * LANES)
            k, v = k_ref[:, cols], v_ref[:, cols]
            q4 = heads_stacked(qc_ref[:, cols], qn_ref[:, cols])
            do4 = heads_stacked(doc_ref[:, cols], don_ref[:, cols])
            sc = _dot_nt(q4 * ATT_SCALE, k) + tb_ref[hp]
            p = jnp.exp(sc - cols_stacked(lc_ref[:, cols], ln_ref[:, cols]))
            ds = p * (_dot_nt(do4, v) - cols_stacked(ddc_ref[:, cols], ddn_ref[:, cols]))
            dq4 = _dot_nn(ds, k)
            dq_cur = jnp.where(left, dq4[:SPAN], dq4[SPAN:2 * SPAN]) + carry[:, cols]
            carry[:, cols] = jnp.where(left, dq4[2 * SPAN:3 * SPAN], dq4[3 * SPAN:])
            out_ref[:, cols] = (dq_cur * ATT_SCALE).astype(out_ref.dtype)
            out_ref[:, d + hp * LANES:d + (hp + 1) * LANES] = (_dot_tn(ds, q4) * ATT_SCALE).astype(out_ref.dtype)
            out_ref[:, 2 * d + hp * LANES:2 * d + (hp + 1) * LANES] = _dot_tn(p, do4).astype(out_ref.dtype)

    qkv_specs = _att_specs(s, d, dil, [(1, "cur"), (2, "cur"), (0, "cur"), (0, "next")])
    pair = _att_specs(s, d, dil, [(0, "cur"), (0, "next")])
    tbl = pl.BlockSpec((None,) + table.shape[1:], lambda b: (jnp.where(b % nb == nb - 1, 0, 1), 0, 0, 0))
    return pl.pallas_call(
        body,
        grid=(s // SPAN,),
        in_specs=qkv_specs + pair + pair + pair + [tbl],
        out_specs=pl.BlockSpec((SPAN, d3), lambda b: (b, 0)),
        out_shape=jax.ShapeDtypeStruct((s, d3), MXU_DTYPE),
        scratch_shapes=[pltpu.VMEM((SPAN, d), F32)],
        name=name,
        compiler_params=_cparams(("arbitrary",)),
    )(qkv, qkv, qkv, qkv, do, do, lse, lse, dd, dd, table)


def _mix_weights(l_refs):
    ls = [r[...] for r in l_refs]
    m = functools.reduce(jnp.maximum, ls)
    es = [jnp.exp(l - m) for l in ls]
    tot = functools.reduce(lambda a, c: a + c, es)
    return [e / tot for e in es]


def _combine_fwd(os_, ls_, name):
    s, d = os_[0].shape
    n = len(os_)

    def body(*refs):
        o_refs, l_refs, out_ref = refs[:n], refs[n:2 * n], refs[2 * n]
        ws = _mix_weights(l_refs)
        acc = ws[0] * o_refs[0][...]
        for w, o in zip(ws[1:], o_refs[1:]):
            acc = acc + w * o[...]
        out_ref[...] = acc

    return _rows(body, s, ROW_TILE, [("blk", a) for a in os_ + ls_], [("blk", (s, d), F32)], name)[0]


def _combine_bwd(do, o, ls_, name):
    s, d = o.shape
    n = len(ls_)
    ri = lax.broadcasted_iota(jnp.int32, (LANES, LANES), 0) // HEAD_DIM
    ci = lax.broadcasted_iota(jnp.int32, (LANES, LANES), 1) // HEAD_DIM
    seg = (ri == ci).astype(F32)

    def body(do_ref, o_ref, *rest):
        l_refs, seg_ref, outs = rest[:n], rest[n], rest[n + 1:]
        ws = _mix_weights(l_refs)
        dov = do_ref[...]
        prod = dov * o_ref[...]
        for j in range(d // LANES):
            cols = slice(j * LANES, (j + 1) * LANES)
            r = jnp.dot(prod[:, cols], seg_ref[...], precision=lax.Precision.HIGHEST, preferred_element_type=F32)
            for g in range(n):
                outs[2 * g][:, cols] = (ws[g][:, cols] * dov[:, cols]).astype(outs[2 * g].dtype)
                outs[2 * g + 1][:, cols] = ws[g][:, cols] * r

    outs = []
    for _ in range(n):
        outs += [("blk", (s, d), MXU_DTYPE), ("blk", (s, d), F32)]
    res = _rows(body, s, ROW_TILE, [("blk", do), ("blk", o)] + [("blk", l) for l in ls_] + [("all", seg)], outs, name)
    return [(res[2 * g], res[2 * g + 1]) for g in range(n)]


def _ada_fwd(c_all, w, b, name):
    nsub, d, cs = w.shape

    def body(c_ref, w_ref, b_ref, o_ref):
        cv = c_ref[...]
        sc = cv * (1.0 / (1.0 + jnp.exp(-cv)))
        o_ref[...] = _dot_nn(sc, w_ref[...]) + b_ref[...]

    return pl.pallas_call(
        body,
        grid=(nsub,),
        in_specs=[pl.BlockSpec(c_all.shape, lambda i: (0, 0)), pl.BlockSpec((None, d, cs), lambda i: (i, 0, 0)),
                  pl.BlockSpec((None, 1, cs), lambda i: (i, 0, 0))],
        out_specs=pl.BlockSpec((None, N_DEV, cs), lambda i: (i, 0, 0)),
        out_shape=jax.ShapeDtypeStruct((nsub, N_DEV, cs), F32),
        name=name,
        compiler_params=_cparams(("parallel",)),
    )(c_all, w, b)


def _ada_bwd(c_all_t, dm, name):
    d, nb = c_all_t.shape
    nsub, _, cs = dm.shape

    def body(c_ref, dm_ref, o_ref):
        cv = c_ref[...]
        sc = cv * (1.0 / (1.0 + jnp.exp(-cv)))
        acc = sc[:, 0:1] * dm_ref[0:1, :]
        for bi in range(1, nb):
            acc = acc + sc[:, bi:bi + 1] * dm_ref[bi:bi + 1, :]
        o_ref[...] = acc

    return pl.pallas_call(
        body,
        grid=(nsub,),
        in_specs=[pl.BlockSpec(c_all_t.shape, lambda i: (0, 0)), pl.BlockSpec((None, nb, cs), lambda i: (i, 0, 0))],
        out_specs=pl.BlockSpec((None, d, cs), lambda i: (i, 0, 0)),
        out_shape=jax.ShapeDtypeStruct((nsub, d, cs), F32),
        name=name,
        compiler_params=_cparams(("parallel",)),
    )(c_all_t, dm)


def _row_tile(r, row_elems):
    t = 2 * SUBLANES
    if r % t:
        return r
    while t * 2 * row_elems <= 256 * 1024 and r % (t * 2) == 0:
        t *= 2
    return t


def _adamw(w, g, m, v, name):
    shape = w.shape
    c = shape[-1]
    r = w.size // c
    tr = _row_tile(r, c)
    w2, g2, m2, v2 = [a.reshape(r, c) for a in (w, g, m, v)]
    bc1 = 1.0 - ADAM_B1 ** ADAM_STEP
    bc2 = 1.0 - ADAM_B2 ** ADAM_STEP

    def body(w_ref, g_ref, m_ref, v_ref, d_ref, nm_ref, nv_ref):
        gv = g_ref[...]
        nm = ADAM_B1 * m_ref[...] + (1.0 - ADAM_B1) * gv
        nv = ADAM_B2 * v_ref[...] + (1.0 - ADAM_B2) * (gv * gv)
        d_ref[...] = -ADAM_LR * ((nm / bc1) / (jnp.sqrt(nv / bc2) + ADAM_EPS) + ADAM_WD * w_ref[...])
        nm_ref[...] = nm
        nv_ref[...] = nv

    res = _rows(body, r, tr, [("blk", a) for a in (w2, g2, m2, v2)], [("blk", (r, c), F32)] * 3, name)
    return [a.reshape(shape) for a in res]


def _sum_slots(buf, name):
    n, r, c = buf.shape
    tr = _row_tile(r, n * c)

    def body(b_ref, o_ref):
        acc = b_ref[0].astype(F32)
        for k in range(1, n):
            acc = acc + b_ref[k].astype(F32)
        o_ref[...] = acc

    return pl.pallas_call(
        body,
        grid=(r // tr,),
        in_specs=[pl.BlockSpec((n, tr, c), lambda i: (0, i, 0))],
        out_specs=pl.BlockSpec((tr, c), lambda i: (i, 0)),
        out_shape=jax.ShapeDtypeStruct((r, c), F32),
        name=name,
        compiler_params=_cparams(("parallel",)),
    )(buf)


def _me():
    return lax.axis_index("x"), lax.axis_index("y"), lax.axis_index("c")


def _all_gather_small(blk, name):
    m_per, n = blk.shape

    def body(x_ref, out_ref, send_sems, recv_sems, local_sem):
        x, y, c = _me()
        me, sibling = (x, y, c), (x, y, 1 - c)
        chips = [(1 - x, y), (x, 1 - y), (1 - x, 1 - y)]

        def rows(px, py, pc):
            return out_ref.at[pl.ds((4 * px + 2 * py + pc) * m_per, m_per), :]

        def copy(k, block, to, src=None):
            return pltpu.make_async_remote_copy(
                src_ref=rows(*block) if src is None else src, dst_ref=rows(*block),
                send_sem=send_sems.at[k], recv_sem=recv_sems.at[k], device_id=to, device_id_type=MESH)

        mine = pltpu.make_async_copy(x_ref, rows(*me), local_sem)
        mine.start()
        first = [copy(0, me, sibling, src=x_ref)]
        first += [copy(1 + j, me, (*chip, c), src=x_ref) for j, chip in enumerate(chips)]
        for cp in first:
            cp.start()
        passed = [copy(4 + j, (*chip, c), sibling) for j, chip in enumerate(chips)]
        for j, chip in enumerate(chips):
            copy(1 + j, (*chip, c), me).wait_recv()
            passed[j].start()
        copy(0, sibling, me).wait_recv()
        for j, chip in enumerate(chips):
            copy(4 + j, (*chip, 1 - c), me).wait_recv()
        for cp in first + passed:
            cp.wait_send()
        mine.wait()

    return pl.pallas_call(
        body,
        out_shape=jax.ShapeDtypeStruct((N_DEV * m_per, n), blk.dtype),
        in_specs=[pl.BlockSpec(memory_space=pltpu.VMEM)],
        out_specs=pl.BlockSpec(memory_space=pltpu.VMEM),
        scratch_shapes=[pltpu.SemaphoreType.DMA((7,)), pltpu.SemaphoreType.DMA((7,)), pltpu.SemaphoreType.DMA],
        name=name,
        compiler_params=pltpu.CompilerParams(vmem_limit_bytes=VMEM_LIMIT),
    )(blk)


_HBM = pl.BlockSpec(memory_space=pltpu.HBM)
_SEM = pl.BlockSpec(memory_space=pltpu.SEMAPHORE)
_EFFECT = pltpu.SideEffectType.DATAFLOW_SIDE_EFFECTING


def _other_chips(x, y):
    return [(1 - x, y), (x, 1 - y), (1 - x, 1 - y)]


def _gather_copy(w, j, src_ref, land_ref, send_sems, recv_sems):
    x, y, c = _me()
    return pltpu.make_async_remote_copy(
        src_ref=src_ref, dst_ref=land_ref.at[2 * x + y], send_sem=send_sems.at[3 * w + j], recv_sem=recv_sems.at[3 * w + j],
        device_id=(*_other_chips(x, y)[j], c), device_id_type=MESH)


def _gather_start(shards, after, name):
    n = len(shards)
    lands = [lax.empty((N_CHIPS,) + s.shape, s.dtype) for s in shards]

    def body(*refs):
        in_refs, land_refs = refs[:n], refs[n:2 * n]
        send_sems, recv_sems = refs[2 * n + 1], refs[2 * n + 2]
        token = refs[-1]
        for w in range(n):
            for j in range(3):
                _gather_copy(w, j, in_refs[w], land_refs[w], send_sems, recv_sems).start()
        token[...] = jnp.zeros_like(token)

    res = pl.pallas_call(
        body,
        out_shape=(pltpu.SemaphoreType.DMA((3 * n,)), pltpu.SemaphoreType.DMA((3 * n,)),
                   *[pltpu.HBM(s.shape, s.dtype) for s in shards], *[pltpu.HBM(l.shape, l.dtype) for l in lands],
                   jax.ShapeDtypeStruct((SUBLANES, LANES), F32)),
        in_specs=[_HBM] * (2 * n) + [pl.BlockSpec(memory_space=pl.ANY)],
        out_specs=(_SEM, _SEM, *[_HBM] * (2 * n), pl.BlockSpec(memory_space=pltpu.VMEM)),
        input_output_aliases={i: 2 + i for i in range(2 * n)},
        name=name,
        compiler_params=pltpu.CompilerParams(has_side_effects=_EFFECT),
    )(*[pltpu.with_memory_space_constraint(a, pltpu.HBM) for a in list(shards) + lands], after)
    return res[0], res[1], res[2:2 + n], res[2 + n:2 + 2 * n], res[-1]


def _gather_wait(w, shard, land, send_sems, recv_sems, after, name):
    def body(s_ref, land_ref, send_sems, recv_sems, after_ref, s_out, land_out, stage):
        x, y, _ = _me()
        pltpu.sync_copy(s_ref, stage)
        pltpu.sync_copy(stage, land_out.at[2 * x + y])
        for j in range(3):
            cp = _gather_copy(w, j, s_ref, land_ref, send_sems, recv_sems)
            cp.wait_send()
            cp.wait_recv()

    return pl.pallas_call(
        body,
        out_shape=(pltpu.HBM(shard.shape, shard.dtype), pltpu.HBM(land.shape, land.dtype)),
        in_specs=(_HBM, _HBM, _SEM, _SEM, pl.BlockSpec(memory_space=pl.ANY)),
        out_specs=(_HBM, _HBM),
        input_output_aliases={0: 0, 1: 1},
        scratch_shapes=[pltpu.VMEM(shard.shape, shard.dtype)],
        name=name,
        compiler_params=pltpu.CompilerParams(has_side_effects=_EFFECT, vmem_limit_bytes=VMEM_LIMIT),
    )(shard, land, send_sems, recv_sems, after)[1]


def _piece_shape(shape, kind):
    k, nn = shape
    return (k // 2, nn // N_CHIPS) if kind == "col" else (k // N_CHIPS // 2, nn)


def _piece_of(g_ref, kind, tq, tc):
    pr, pc = _piece_shape(g_ref.shape, kind)
    if kind == "col":
        return g_ref.at[pl.ds(tc * pr, pr), pl.ds(tq * pc, pc)]
    return g_ref.at[pl.ds((2 * tq + tc) * pr, pr), :]


def _scatter_copy(r, kind, g_ref, land_ref, send_sems, recv_sems):
    x, y, c = _me()
    tx, ty, tc = (x + ((r >> 2) & 1)) % 2, (y + ((r >> 1) & 1)) % 2, (c + (r & 1)) % 2
    return pltpu.make_async_remote_copy(
        src_ref=_piece_of(g_ref, kind, 2 * tx + ty, tc), dst_ref=land_ref.at[4 * x + 2 * y + c],
        send_sem=send_sems.at[r], recv_sem=recv_sems.at[r], device_id=(tx, ty, tc), device_id_type=MESH)


def _scatter_start(g, kind, name):
    piece = _piece_shape(g.shape, kind)
    land = lax.empty((N_DEV,) + piece, g.dtype)

    def body(g_ref, land_ref, send_sems, recv_sems, g_out, land_out, stage):
        x, y, c = _me()
        for r in range(1, N_DEV):
            _scatter_copy(r, kind, g_ref, land_ref, send_sems, recv_sems).start()
        pltpu.sync_copy(_piece_of(g_ref, kind, 2 * x + y, c), stage)
        pltpu.sync_copy(stage, land_out.at[4 * x + 2 * y + c])

    return pl.pallas_call(
        body,
        out_shape=(pltpu.SemaphoreType.DMA((N_DEV,)), pltpu.SemaphoreType.DMA((N_DEV,)),
                   pltpu.HBM(g.shape, g.dtype), pltpu.HBM(land.shape, land.dtype)),
        in_specs=[_HBM, _HBM],
        out_specs=(_SEM, _SEM, _HBM, _HBM),
        input_output_aliases={0: 2, 1: 3},
        scratch_shapes=[pltpu.VMEM(piece, g.dtype)],
        name=name,
        compiler_params=pltpu.CompilerParams(has_side_effects=_EFFECT, vmem_limit_bytes=VMEM_LIMIT),
    )(pltpu.with_memory_space_constraint(g, pltpu.HBM), pltpu.with_memory_space_constraint(land, pltpu.HBM))


def _scatter_wait(send_sems, recv_sems, g, land, kind, after, name):
    def body(g_ref, land_ref, send_sems, recv_sems, after_ref, g_out, land_out):
        for r in range(1, N_DEV):
            cp = _scatter_copy(r, kind, g_ref, land_ref, send_sems, recv_sems)
            cp.wait_send()
            cp.wait_recv()

    return pl.pallas_call(
        body,
        out_shape=(pltpu.HBM(g.shape, g.dtype), pltpu.HBM(land.shape, land.dtype)),
        in_specs=(_HBM, _HBM, _SEM, _SEM, pl.BlockSpec(memory_space=pl.ANY)),
        out_specs=(_HBM, _HBM),
        input_output_aliases={0: 0, 1: 1},
        name=name,
        compiler_params=pltpu.CompilerParams(has_side_effects=_EFFECT),
    )(g, land, send_sems, recv_sems, after)[1]


def _swap_halves(halves, name):
    n = len(halves)

    def body(*refs):
        in_refs, out_refs = refs[:n], refs[n:2 * n]
        send_sems, recv_sems, local_sems = refs[2 * n:]
        x, y, c = _me()
        cps = []
        for w in range(n):
            lc = pltpu.make_async_copy(in_refs[w], out_refs[w].at[c], local_sems.at[w])
            lc.start()
            rc = pltpu.make_async_remote_copy(
                src_ref=in_refs[w], dst_ref=out_refs[w].at[c], send_sem=send_sems.at[w], recv_sem=recv_sems.at[w],
                device_id=(x, y, 1 - c), device_id_type=MESH)
            rc.start()
            cps.append((lc, rc))
        for lc, rc in cps:
            rc.wait_recv()
        for lc, rc in cps:
            rc.wait_send()
            lc.wait()

    vmem = pl.BlockSpec(memory_space=pltpu.VMEM)
    return pl.pallas_call(
        body,
        out_shape=[jax.ShapeDtypeStruct((2,) + h.shape, h.dtype) for h in halves],
        in_specs=[vmem] * n,
        out_specs=[vmem] * n,
        scratch_shapes=[pltpu.SemaphoreType.DMA((n,)), pltpu.SemaphoreType.DMA((n,)), pltpu.SemaphoreType.DMA((n,))],
        name=name,
        compiler_params=pltpu.CompilerParams(vmem_limit_bytes=VMEM_LIMIT),
    )(*halves)


def _to_streams(a, dil):
    if dil == 1:
        return a
    s, c = a.shape
    return a.reshape(s // dil, dil, c).transpose(1, 0, 2).reshape(s, c)


def _from_streams(a, dil):
    if dil == 1:
        return a
    s, c = a.shape
    return a.reshape(dil, s // dil, c).transpose(1, 0, 2).reshape(s, c)


def _mm_tiles(s):
    return min(s, 1024)


def _local_step(x0, target, mvec, ln_g, ln_b, small, fetch, emit, start):
    s, d = x0.shape
    tm = _mm_tiles(s)
    row = lambda v: v.reshape(1, -1)
    shift = [row(mvec[i, :d]) for i in range(4)]
    scale = [row(mvec[i, d:2 * d]) for i in range(4)]
    gate = [row(1.0 + mvec[i, 2 * d:]) for i in range(4)]
    lg = [row(ln_g[i]) for i in range(4)]
    lb = [row(ln_b[i]) for i in range(4)]
    mm = functools.partial(_mm, tm=tm)
    mm_w = functools.partial(_mm, tm=1024, tk=min(s, 512), mode="tn")

    xs, ys, big = [x0], [], {}
    h0 = _mod(x0, scale[0], shift[0], start, "mod0")
    big["a_w_in"] = fetch("a_w_in", h0)
    uvpre = mm(h0, big["a_w_in"], mode="nn", name="a_in", outs=[F32], tn=512, tk=512,
               epi=lambda r, bias: [r + bias], extras=[("row", small["a_b_in"])])
    gated = _spatial_fwd(uvpre, small["a_vn_g"], small["a_vn_b"], small["wc"], small["bias_full"], "a_spatial")
    big["a_w_out"] = fetch("a_w_out", gated)
    ys.append(mm(gated, big["a_w_out"], mode="nn", name="a_out", outs=[F32], tn=1024, tk=512))
    x1, h1 = _resid_ln(xs[0], ys[0], gate[0], lg[0], lb[0], (scale[1], shift[1]), "ln0")
    xs.append(x1)
    relu2 = lambda r: [r, jnp.square(jnp.maximum(r, 0.0))]
    big["up0"] = fetch("up0", h1)
    a0, r0 = mm(h1, big["up0"], mode="nn", name="up0", outs=[MXU_DTYPE, MXU_DTYPE], tn=1024, tk=512, epi=relu2)
    big["down0"] = fetch("down0", r0)
    ys.append(mm(r0, big["down0"], mode="nn", name="down0", outs=[F32], tn=1024, tk=512))
    x2, h2 = _resid_ln(xs[1], ys[1], gate[1], lg[1], lb[1], (scale[2], shift[2]), "ln1")
    xs.append(x2)
    hg, qkvs, o_g, l_g = [], [], [], []
    big["b_w_qkv"] = fetch("b_w_qkv", h2)
    for g, (_, dil) in enumerate(B_PATTERNS):
        hp = _to_streams(h2, dil)
        qkv = mm(hp, big["b_w_qkv"], mode="nn", name=f"qkv{g}", outs=[MXU_DTYPE], tn=768, tk=512, b_col0=g * 3 * d, n_out=3 * d)
        og, lgv = _attn_fwd(qkv, small["slopes"], dil, f"attn_fwd{g}")
        hg.append(hp)
        qkvs.append(qkv)
        o_g.append(_from_streams(og, dil))
        l_g.append(_from_streams(lgv, dil))
    o_mix = _combine_fwd(o_g, l_g, "combine")
    big["b_w_out"] = fetch("b_w_out", o_mix)
    ys.append(mm(o_mix, big["b_w_out"], mode="nn", name="b_out", outs=[F32], tn=1024, tk=512))
    x3, h3 = _resid_ln(xs[2], ys[2], gate[2], lg[2], lb[2], (scale[3], shift[3]), "ln2")
    xs.append(x3)
    big["up1"] = fetch("up1", h3)
    a1, r1 = mm(h3, big["up1"], mode="nn", name="up1", outs=[MXU_DTYPE, MXU_DTYPE], tn=1024, tk=512, epi=relu2)
    big["down1"] = fetch("down1", r1)
    ys.append(mm(r1, big["down1"], mode="nn", name="down1", outs=[F32], tn=1024, tk=512))
    x4, _ = _resid_ln(xs[3], ys[3], gate[3], lg[3], lb[3], None, "ln3")

    gb, dm, dlg, dlb = {}, [None] * 4, [None] * 4, [None] * 4
    dx, loss = _loss_grad(x4, target, "loss")

    def mlp_bwd(i, sub, dx, h, a, r):
        dxr, dyy, red = _ln_bwd(dx, xs[sub], ys[sub], gate[sub], lg[sub], f"ln_bwd{sub}")
        gb[f"down{i}"] = mm_w(r, dyy, name=f"g_down{i}", outs=[MXU_DTYPE], tn=1024)
        da = mm(dyy, big[f"down{i}"], mode="nt", name=f"d_down{i}", outs=[MXU_DTYPE], tn=1024, tk=512,
                after=emit(f"down{i}", gb[f"down{i}"]),
                epi=lambda acc, av: [acc * (2.0 * jnp.maximum(av.astype(F32), 0.0))], extras=[("full", a)])
        gb[f"up{i}"] = mm_w(h, da, name=f"g_up{i}", outs=[MXU_DTYPE], tn=1024)
        dh = mm(da, big[f"up{i}"], mode="nt", name=f"d_up{i}", outs=[F32], tn=1024, tk=512, after=emit(f"up{i}", gb[f"up{i}"]))
        dx, red2 = _mod_bwd(dxr, [dh], xs[sub], scale[sub], f"mod_bwd{sub}")
        dm[sub] = jnp.concatenate([red2[0], red2[1], red[2]])
        dlg[sub], dlb[sub] = red[0], red[1]
        return dx

    dx = mlp_bwd(1, 3, dx, h3, a1, r1)
    dxr, dyy, red = _ln_bwd(dx, xs[2], ys[2], gate[2], lg[2], "ln_bwd2")
    gb["b_w_out"] = mm_w(o_mix, dyy, name="g_b_out", outs=[MXU_DTYPE], tn=1024)
    do = mm(dyy, big["b_w_out"], mode="nt", name="d_b_out", outs=[F32], tn=1024, tk=512, after=emit("b_w_out", gb["b_w_out"]))
    parts = _combine_bwd(do, o_mix, l_g, "combine_bwd")
    dhs, gq = [], []
    for g, (_, dil) in enumerate(B_PATTERNS):
        do_g, dd_g = _to_streams(parts[g][0], dil), _to_streams(parts[g][1], dil)
        lse_g = _to_streams(l_g[g], dil)
        dqkv = _attn_bwd(qkvs[g], do_g, lse_g, dd_g, small["slopes"], dil, f"attn_bwd{g}")
        gq.append(mm_w(hg[g], dqkv, name=f"g_qkv{g}", outs=[MXU_DTYPE], tn=1024))
        dh = mm(dqkv, big["b_w_qkv"], mode="nt", name=f"d_qkv{g}", outs=[F32], tn=1024, tk=768, b_col0=g * 3 * d)
        dhs.append(_from_streams(dh, dil))
    gb["b_w_qkv"] = jnp.concatenate(gq, axis=1)
    dx, red2 = _mod_bwd(dxr, dhs, xs[2], scale[2], "mod_bwd2", after=emit("b_w_qkv", gb["b_w_qkv"]))
    dm[2] = jnp.concatenate([red2[0], red2[1], red[2]])
    dlg[2], dlb[2] = red[0], red[1]
    dx = mlp_bwd(0, 1, dx, h1, a0, r0)
    dxr, dyy, red = _ln_bwd(dx, xs[0], ys[0], gate[0], lg[0], "ln_bwd0")
    gb["a_w_out"] = mm_w(gated, dyy, name="g_a_out", outs=[MXU_DTYPE], tn=1024)
    dgated = mm(dyy, big["a_w_out"], mode="nt", name="d_a_out", outs=[F32], tn=1024, tk=512, after=emit("a_w_out", gb["a_w_out"]))
    duv, dws, dbias, dbin, dvg, dvb = _spatial_bwd(uvpre, dgated, small["a_vn_g"], small["a_vn_b"], small["wc"],
                                                   small["wct"], small["bias_full"], "a_spatial_bwd")
    gb["a_w_in"] = mm_w(h0, duv, name="g_a_in", outs=[MXU_DTYPE], tn=1024)
    dh = mm(duv, big["a_w_in"], mode="nt", name="d_a_in", outs=[F32], tn=1024, tk=512, after=emit("a_w_in", gb["a_w_in"]))
    dx, red2 = _mod_bwd(dxr, [dh], xs[0], scale[0], "mod_bwd0")
    dm[0] = jnp.concatenate([red2[0], red2[1], red[2]])
    dlg[0], dlb[0] = red[0], red[1]

    tril = jnp.tril(jnp.ones((CHUNK, CHUNK), bool))
    gsmall = {
        "a_b_in": dbin.reshape(-1), "a_vn_g": dvg.reshape(-1), "a_vn_b": dvb.reshape(-1),
        "a_w_s": jnp.where(tril, dws, 0.0).reshape(-1),
        "a_b_s": dbias.reshape(CHUNK, A_GROUPS, d // A_GROUPS).sum(-1).T.reshape(-1),
    }
    return loss, dx, gb, jnp.stack(dm), jnp.stack(dlg), jnp.stack(dlb), gsmall


BIG = ("a_w_in", "a_w_out", "up0", "down0", "b_w_qkv", "b_w_out", "up1", "down1")
BIG_KIND = {"a_w_in": "col", "a_w_out": "row", "b_w_qkv": "col", "b_w_out": "row",
            "up0": "col", "up1": "col", "down0": "row", "down1": "row"}
SMALL = ("a_b_in", "a_vn_g", "a_vn_b", "a_b_s", "a_w_s")


def kernel(x, c, ada_w, ada_b, ln_g, ln_b, a_w_in, a_b_in, a_vn_g, a_vn_b, a_w_s, a_b_s, a_w_out, b_w_qkv, b_w_out, mlp_w_up, mlp_w_down, loss_target, m_ada_w, m_ada_b, m_ln_g, m_ln_b, m_a_w_in, m_a_b_in, m_a_vn_g, m_a_vn_b, m_a_w_s, m_a_b_s, m_a_w_out, m_b_w_qkv, m_b_w_out, m_mlp_w_up, m_mlp_w_down, v_ada_w, v_ada_b, v_ln_g, v_ln_b, v_a_w_in, v_a_b_in, v_a_vn_g, v_a_vn_b, v_a_w_s, v_a_b_s, v_a_w_out, v_b_w_qkv, v_b_w_out, v_mlp_w_up, v_mlp_w_down):
    s, d = x.shape[1], x.shape[2]
    xi, yi, ci = _me()
    q = 2 * xi + yi
    dev = 2 * q + ci
    nsub = 2 * DEPTH
    cs = ada_w.shape[-1]
    ls = ln_g.shape[-1]

    pack = jnp.concatenate([c.reshape(-1), ln_g.reshape(-1), ln_b.reshape(-1)]).reshape(-1, LANES)
    got = _all_gather_small(pack, "gather_small").reshape(N_DEV, -1)
    c_all = got[:, :d]
    per_chip = got[0::2]
    ln_g_full = per_chip[:, d:d + nsub * ls].reshape(N_CHIPS, nsub, ls).transpose(1, 0, 2).reshape(nsub, d)
    ln_b_full = per_chip[:, d + nsub * ls:].reshape(N_CHIPS, nsub, ls).transpose(1, 0, 2).reshape(nsub, d)
    m_part = _ada_fwd(c_all, ada_w.reshape(nsub, d, cs), ada_b.reshape(nsub, 1, cs), "ada_fwd")
    m_all = _all_gather_small(m_part.reshape(-1, LANES), "gather_mod").reshape(N_DEV, nsub, N_DEV, cs)
    m_mine = lax.dynamic_index_in_dim(m_all[0::2], dev, axis=2, keepdims=False)
    mvec = m_mine.transpose(1, 0, 2).reshape(nsub, 3 * d)

    shards = {
        "a_w_in": a_w_in[0], "a_w_out": a_w_out[0], "b_w_qkv": b_w_qkv[0], "b_w_out": b_w_out[0],
        "up0": mlp_w_up[0], "up1": mlp_w_up[1], "down0": mlp_w_down[0], "down1": mlp_w_down[1],
    }
    send_sems, recv_sems, shard_thru, lands, token = _gather_start([shards[k].astype(MXU_DTYPE) for k in BIG], mvec, "gather_start")

    def fetch(k, after):
        w = BIG.index(k)
        gw = _gather_wait(w, shard_thru[w], lands[w], send_sems, recv_sems, after, f"gather_wait_{k}")
        return gw if BIG_KIND[k] == "col" else gw.reshape(1, -1, gw.shape[-1])

    scattering = {}

    def emit(k, g):
        scattering[k] = _scatter_start(g, BIG_KIND[k], f"scatter_start_{k}")
        return scattering[k][2]

    tril = jnp.tril(jnp.ones((CHUNK, CHUNK), bool))
    wc = jnp.where(tril, a_w_s[0], 0.0).astype(MXU_DTYPE)
    heads = jnp.arange(1, B_HEADS + 1, dtype=F32)
    small = {
        "a_b_in": a_b_in, "a_vn_g": a_vn_g, "a_vn_b": a_vn_b,
        "wc": wc, "wct": wc.transpose(0, 2, 1),
        "bias_full": jnp.repeat(a_b_s[0].T, d // A_GROUPS, axis=1),
        "slopes": jnp.exp2(-8.0 * heads / B_HEADS),
    }

    loss_part, grad_x, gb, dm, dlg, dlb, gsmall = _local_step(x[0], loss_target[0], mvec, ln_g_full, ln_b_full, small, fetch, emit, token)
    loss = lax.psum(loss_part, ("x", "y", "c"))

    weights = dict(ada_w=ada_w, ada_b=ada_b, ln_g=ln_g, ln_b=ln_b, a_w_in=a_w_in, a_b_in=a_b_in, a_vn_g=a_vn_g, a_vn_b=a_vn_b,
                   a_w_s=a_w_s, a_b_s=a_b_s, a_w_out=a_w_out, b_w_qkv=b_w_qkv, b_w_out=b_w_out, mlp_w_up=mlp_w_up, mlp_w_down=mlp_w_down)
    ms = dict(ada_w=m_ada_w, ada_b=m_ada_b, ln_g=m_ln_g, ln_b=m_ln_b, a_w_in=m_a_w_in, a_b_in=m_a_b_in, a_vn_g=m_a_vn_g, a_vn_b=m_a_vn_b,
              a_w_s=m_a_w_s, a_b_s=m_a_b_s, a_w_out=m_a_w_out, b_w_qkv=m_b_w_qkv, b_w_out=m_b_w_out, mlp_w_up=m_mlp_w_up, mlp_w_down=m_mlp_w_down)
    vs = dict(ada_w=v_ada_w, ada_b=v_ada_b, ln_g=v_ln_g, ln_b=v_ln_b, a_w_in=v_a_w_in, a_b_in=v_a_b_in, a_vn_g=v_a_vn_g, a_vn_b=v_a_vn_b,
              a_w_s=v_a_w_s, a_b_s=v_a_b_s, a_w_out=v_a_w_out, b_w_qkv=v_b_w_qkv, b_w_out=v_b_w_out, mlp_w_up=v_mlp_w_up, mlp_w_down=v_mlp_w_down)
    grads, updates = {}, {}

    def update(k):
        updates[k] = _adamw(weights[k], grads[k], ms[k], vs[k], f"adamw_{k}")
        return updates[k][0]

    pack_b = jnp.concatenate([dm.reshape(-1), dlg.reshape(-1), dlb.reshape(-1)] + [gsmall[k] for k in SMALL])
    n_small = pack_b.shape[0]
    pack_b = jnp.pad(pack_b, (0, -n_small % (ROW_TILE * LANES)))
    got_b = _all_gather_small(pack_b.reshape(-1, LANES), "gather_small_grads").reshape(N_DEV, -1, LANES)
    tot = _sum_slots(got_b, "sum_small").reshape(-1)
    o = 0
    dm_tot = tot[o:o + nsub * 3 * d].reshape(nsub, 3 * d); o += nsub * 3 * d
    dlg_tot = tot[o:o + nsub * d].reshape(nsub, d); o += nsub * d
    dlb_tot = tot[o:o + nsub * d].reshape(nsub, d); o += nsub * d
    g_small = {}
    for k, ref in zip(SMALL, (a_b_in, a_vn_g, a_vn_b, a_b_s, a_w_s)):
        g_small[k] = tot[o:o + ref.size].reshape(ref.shape); o += ref.size
    assert o == n_small
    dm_all = got_b.reshape(N_DEV, -1)[:, :nsub * 3 * d].reshape(N_DEV, nsub, 3 * d)
    dm_cols = lax.dynamic_slice_in_dim(dm_all, q * cs, cs, axis=2).transpose(1, 0, 2)

    grads.update({
        "ada_w": _ada_bwd(c_all.T, dm_cols, "ada_bwd").reshape(ada_w.shape),
        "ada_b": lax.dynamic_slice_in_dim(dm_tot, q * cs, cs, axis=1).reshape(ada_b.shape),
        "ln_g": lax.dynamic_slice_in_dim(dlg_tot, q * ls, ls, axis=1).reshape(ln_g.shape),
        "ln_b": lax.dynamic_slice_in_dim(dlb_tot, q * ls, ls, axis=1).reshape(ln_b.shape),
        **g_small,
    })
    for k in ("ada_b", "ln_g", "ln_b") + SMALL:
        update(k)
    done = update("ada_w")

    gfull = {}
    for group in (("down1", "up1", "b_w_out", "b_w_qkv"), ("down0", "up0", "a_w_out", "a_w_in")):
        bufs = [_scatter_wait(*scattering[k], BIG_KIND[k], done, f"scatter_wait_{k}") for k in group]
        halves = [_sum_slots(b, f"sum_{k}") for k, b in zip(group, bufs)]
        fulls = _swap_halves(halves, f"swap_halves_{group[0]}")
        gfull.update({k: f.reshape(-1, f.shape[-1]) for k, f in zip(group, fulls)})
        if group[0] == "down1":
            grads["b_w_qkv"], grads["b_w_out"] = gfull["b_w_qkv"][None], gfull["b_w_out"][None]
            update("b_w_out")
            done = update("b_w_qkv")
    grads.update({
        "a_w_in": gfull["a_w_in"][None], "a_w_out": gfull["a_w_out"][None],
        "mlp_w_up": jnp.stack([gfull["up0"], gfull["up1"]]), "mlp_w_down": jnp.stack([gfull["down0"], gfull["down1"]]),
    })
    for k in ("a_w_in", "a_w_out", "mlp_w_up", "mlp_w_down"):
        update(k)
    names = list(weights)
    return (loss, grad_x[None], *[grads[k] for k in names], *[updates[k][0] for k in names],
            *[updates[k][1] for k in names], *[updates[k][2] for k in names])
```

```python
import functools
import math

import jax
import jax.numpy as jnp
from jax import lax
from jax.experimental import pallas as pl
from jax.experimental.pallas import tpu as pltpu

F32 = jnp.float32
MXU_DTYPE = jnp.bfloat16

DEPTH = 2
CHUNK = 128
A_GROUPS = 16
B_HEADS = 16
HEAD_DIM = 64
B_PATTERNS = ((128, 1), (512, 4), (2048, 16))
SPAN = 128
ALPHA = (2 * DEPTH) ** 0.25
LN_EPS = 1e-5
NEG = -1e30
ATT_SCALE = HEAD_DIM ** -0.5
ADAM_LR, ADAM_B1, ADAM_B2, ADAM_EPS, ADAM_WD, ADAM_STEP = 0.001, 0.9, 0.999, 1e-08, 0.01, 10

N_CHIPS = 4
N_DEV = 8
LANES = 128
SUBLANES = 8
VMEM_LIMIT = 52 * 1024 * 1024
ROW_TILE = 256
MESH = pl.DeviceIdType.MESH


def _cparams(sem):
    return pltpu.CompilerParams(dimension_semantics=sem, vmem_limit_bytes=VMEM_LIMIT)


def _fold8(v):
    r, c = v.shape
    return jnp.sum(v.reshape(r // SUBLANES, SUBLANES, c), axis=0)


def _gelu(x):
    c = math.sqrt(2.0 / math.pi)
    return 0.5 * x * (1.0 + jnp.tanh(c * (x + 0.044715 * (x * x * x))))


def _gelu_grad(x):
    c = math.sqrt(2.0 / math.pi)
    t = jnp.tanh(c * (x + 0.044715 * (x * x * x)))
    return 0.5 * (1.0 + t) + 0.5 * x * (1.0 - t * t) * c * (1.0 + 3.0 * 0.044715 * x * x)


def _dot(a, b, dims):
    return lax.dot_general(a.astype(MXU_DTYPE), b.astype(MXU_DTYPE), (dims, ((), ())), preferred_element_type=F32)


def _dot_nn(a, b):
    return _dot(a, b, ((1,), (0,)))


def _dot_nt(a, b):
    return _dot(a, b, ((1,), (1,)))


def _dot_tn(a, b):
    return _dot(a, b, ((0,), (0,)))


def _mm(a, b, *, mode, name, outs, tm, tn, tk, epi=None, extras=(), b_col0=0, n_out=None, after=None):
    if mode == "nn":
        m, kdim = a.shape
        p, kb, ns = b.shape
        assert kb == kdim and ns % tn == 0 and b_col0 % tn == 0
        n = n_out if n_out is not None else p * ns
        npt, j0 = ns // tn, b_col0 // tn
        a_spec = pl.BlockSpec((tm, tk), lambda i, j, k: (i, k))
        b_spec = pl.BlockSpec((None, tk, tn), lambda i, j, k: ((j + j0) // npt, k, (j + j0) % npt))
        dot = _dot_nn
    elif mode == "nt":
        m, kdim = a.shape
        p, n, ns = b.shape
        assert ns % tk == 0 and b_col0 % tk == 0
        npt, j0 = ns // tk, b_col0 // tk
        a_spec = pl.BlockSpec((tm, tk), lambda i, j, k: (i, k))
        b_spec = pl.BlockSpec((None, tn, tk), lambda i, j, k: ((k + j0) // npt, j, (k + j0) % npt))
        dot = _dot_nt
    else:
        kdim, m = a.shape
        kb, n = b.shape
        assert kb == kdim
        a_spec = pl.BlockSpec((tk, tm), lambda i, j, k: (k, i))
        b_spec = pl.BlockSpec((tk, tn), lambda i, j, k: (k, j))
        dot = _dot_tn
    assert m % tm == 0 and n % tn == 0 and kdim % tk == 0, (name, m, n, kdim, tm, tn, tk)
    nk = kdim // tk
    ex_specs, ex_arrays = [], []
    for kind, arr in extras:
        if kind == "row":
            ex_specs.append(pl.BlockSpec((1, tn), lambda i, j, k: (0, j)))
        else:
            ex_specs.append(pl.BlockSpec((tm, tn), lambda i, j, k: (i, j)))
        ex_arrays.append(arr)
    n_ex, n_o = len(ex_arrays), len(outs)
    n_dep = 0 if after is None else 1
    deps = [] if after is None else [after]

    def body(a_ref, b_ref, *rest):
        ex_refs, o_refs = rest[:n_ex], rest[n_ex + n_dep:n_ex + n_dep + n_o]
        k = pl.program_id(2)

        def finish(r):
            vals = epi(r, *[e[...] for e in ex_refs]) if epi is not None else [r]
            for o, v in zip(o_refs, vals):
                o[...] = v.astype(o.dtype)

        if nk == 1:
            finish(dot(a_ref[...], b_ref[...]))
            return
        acc = rest[n_ex + n_dep + n_o]

        @pl.when(k == 0)
        def _():
            acc[...] = dot(a_ref[...], b_ref[...])

        @pl.when((k > 0) & (k < nk - 1))
        def _():
            acc[...] += dot(a_ref[...], b_ref[...])

        @pl.when(k == nk - 1)
        def _():
            finish(acc[...] + dot(a_ref[...], b_ref[...]))

    res = pl.pallas_call(
        body,
        grid=(m // tm, n // tn, nk),
        in_specs=[a_spec, b_spec] + ex_specs + [pl.BlockSpec(memory_space=pl.ANY)] * n_dep,
        out_specs=[pl.BlockSpec((tm, tn), lambda i, j, k: (i, j)) for _ in outs],
        out_shape=[jax.ShapeDtypeStruct((m, n), dt) for dt in outs],
        scratch_shapes=[pltpu.VMEM((tm, tn), F32)] if nk > 1 else [],
        name=name,
        compiler_params=_cparams(("parallel", "parallel", "arbitrary")),
    )(a, b, *ex_arrays, *deps)
    return res if len(outs) > 1 else res[0]


def _rows(body, n_rows, tr, ins, outs, name, scratch=()):
    def spec(kind, shape):
        if kind == "blk":
            return pl.BlockSpec((tr,) + tuple(shape[1:]), lambda i: (i,) + (0,) * (len(shape) - 1))
        if kind == "dep":
            return pl.BlockSpec(memory_space=pl.ANY)
        return pl.BlockSpec(tuple(shape), lambda i: (0,) * len(shape))

    return pl.pallas_call(
        body,
        grid=(n_rows // tr,),
        in_specs=[spec(k, a.shape) for k, a in ins],
        out_specs=[spec(k, s) for k, s, _ in outs],
        out_shape=[jax.ShapeDtypeStruct(tuple(s), d) for _, s, d in outs],
        scratch_shapes=list(scratch),
        name=name,
        compiler_params=_cparams(("arbitrary",)),
    )(*[a for _, a in ins])


def _ln_stats(z):
    mu = jnp.mean(z, axis=-1, keepdims=True)
    zc = z - mu
    var = jnp.mean(zc * zc, axis=-1, keepdims=True)
    rstd = lax.rsqrt(var + LN_EPS)
    return zc * rstd, rstd


def _mod(x, scale, shift, after, name):
    s, d = x.shape

    def body(x_ref, sc_ref, sh_ref, dep_ref, h_ref):
        h_ref[...] = (x_ref[...] * (1.0 + sc_ref[...]) + sh_ref[...]).astype(h_ref.dtype)

    return _rows(body, s, ROW_TILE, [("blk", x), ("all", scale), ("all", shift), ("dep", after)], [("blk", (s, d), MXU_DTYPE)], name)[0]


def _resid_ln(x, y, gate, g, b, nxt, name):
    s, d = x.shape
    ins = [("blk", x), ("blk", y), ("all", gate), ("all", g), ("all", b)]
    outs = [("blk", (s, d), F32)]
    if nxt is not None:
        ins += [("all", nxt[0]), ("all", nxt[1])]
        outs += [("blk", (s, d), MXU_DTYPE)]

    def body(x_ref, y_ref, gate_ref, g_ref, b_ref, *rest):
        z = ALPHA * x_ref[...] + gate_ref[...] * y_ref[...]
        xhat, _ = _ln_stats(z)
        xn = xhat * g_ref[...] + b_ref[...]
        if nxt is None:
            rest[0][...] = xn
        else:
            sc_ref, sh_ref, xn_ref, h_ref = rest
            xn_ref[...] = xn
            h_ref[...] = (xn * (1.0 + sc_ref[...]) + sh_ref[...]).astype(h_ref.dtype)

    res = _rows(body, s, ROW_TILE, ins, outs, name)
    return (res[0], res[1]) if nxt is not None else (res[0], None)


def _loss_grad(xf, target, name):
    s, d = xf.shape

    def body(x_ref, t_ref, dy_ref, l_ref, acc):
        i = pl.program_id(0)

        @pl.when(i == 0)
        def _():
            acc[...] = jnp.zeros_like(acc)

        e = x_ref[...] - t_ref[...]
        dy_ref[...] = e * (1.0 / d)
        acc[...] += _fold8(e * e)

        @pl.when(i == pl.num_programs(0) - 1)
        def _():
            l_ref[...] = jnp.full(l_ref.shape, 0.5 / d, F32) * jnp.sum(acc[...])

    dy, l = _rows(body, s, ROW_TILE, [("blk", xf), ("blk", target)],
                  [("blk", (s, d), F32), ("all", (SUBLANES, LANES), F32)], name,
                  scratch=[pltpu.VMEM((SUBLANES, d), F32)])
    return dy, l[0, 0]


def _ln_bwd(dxo, x, y, gate, g, name):
    s, d = x.shape

    def body(dxo_ref, x_ref, y_ref, gate_ref, g_ref, dxr_ref, dyy_ref, red_ref, a_g, a_b, a_gate):
        i = pl.program_id(0)

        @pl.when(i == 0)
        def _():
            a_g[...] = jnp.zeros_like(a_g)
            a_b[...] = jnp.zeros_like(a_b)
            a_gate[...] = jnp.zeros_like(a_gate)

        yv = y_ref[...]
        z = ALPHA * x_ref[...] + gate_ref[...] * yv
        xhat, rstd = _ln_stats(z)
        dxo_v = dxo_ref[...]
        dxh = dxo_v * g_ref[...]
        dz = rstd * (dxh - jnp.mean(dxh, axis=-1, keepdims=True) - xhat * jnp.mean(dxh * xhat, axis=-1, keepdims=True))
        dxr_ref[...] = ALPHA * dz
        dyy_ref[...] = (gate_ref[...] * dz).astype(dyy_ref.dtype)
        a_g[...] += _fold8(dxo_v * xhat)
        a_b[...] += _fold8(dxo_v)
        a_gate[...] += _fold8(dz * yv)

        @pl.when(i == pl.num_programs(0) - 1)
        def _():
            red_ref[...] = jnp.zeros_like(red_ref)
            red_ref[0:1, :] = jnp.sum(a_g[...], axis=0, keepdims=True)
            red_ref[1:2, :] = jnp.sum(a_b[...], axis=0, keepdims=True)
            red_ref[2:3, :] = jnp.sum(a_gate[...], axis=0, keepdims=True)

    return _rows(body, s, ROW_TILE, [("blk", dxo), ("blk", x), ("blk", y), ("all", gate), ("all", g)],
                 [("blk", (s, d), F32), ("blk", (s, d), MXU_DTYPE), ("all", (SUBLANES, d), F32)], name,
                 scratch=[pltpu.VMEM((SUBLANES, d), F32)] * 3)


def _mod_bwd(dxr, dhs, x, scale, name, after=None):
    s, d = x.shape
    n_dh = len(dhs)
    n_dep = 0 if after is None else 1

    def body(dxr_ref, *rest):
        dh_refs = rest[:n_dh]
        x_ref, sc_ref, dx_ref, red_ref, a_sh, a_sc = rest[n_dh:n_dh + 2] + rest[n_dh + 2 + n_dep:]
        i = pl.program_id(0)

        @pl.when(i == 0)
        def _():
            a_sh[...] = jnp.zeros_like(a_sh)
            a_sc[...] = jnp.zeros_like(a_sc)

        dh = dh_refs[0][...]
        for r in dh_refs[1:]:
            dh = dh + r[...]
        dx_ref[...] = dxr_ref[...] + dh * (1.0 + sc_ref[...])
        a_sh[...] += _fold8(dh)
        a_sc[...] += _fold8(dh * x_ref[...])

        @pl.when(i == pl.num_programs(0) - 1)
        def _():
            red_ref[...] = jnp.zeros_like(red_ref)
            red_ref[0:1, :] = jnp.sum(a_sh[...], axis=0, keepdims=True)
            red_ref[1:2, :] = jnp.sum(a_sc[...], axis=0, keepdims=True)

    return _rows(body, s, ROW_TILE, [("blk", dxr)] + [("blk", h) for h in dhs] + [("blk", x), ("all", scale)] + [("dep", after)] * n_dep,
                 [("blk", (s, d), F32), ("all", (SUBLANES, d), F32)], name,
                 scratch=[pltpu.VMEM((SUBLANES, d), F32)] * 2)


def _left_half(shape):
    return lax.broadcasted_iota(jnp.int32, shape, 1) < (LANES // 2)


def _spatial_z(vn, wc_ref, bias_ref, j):
    vb = vn[:, j * LANES:(j + 1) * LANES]
    z0 = _dot_nn(wc_ref[2 * j], vb)
    z1 = _dot_nn(wc_ref[2 * j + 1], vb)
    return jnp.where(_left_half(z0.shape), z0, z1) + bias_ref[:, j * LANES:(j + 1) * LANES]


def _spatial_fwd(uvpre, vn_g, vn_b, wc, bias_full, name):
    s, d2 = uvpre.shape
    d = d2 // 2

    def body(uv_ref, g_ref, b_ref, wc_ref, bias_ref, out_ref):
        u = _gelu(uv_ref[:, :d])
        v = _gelu(uv_ref[:, d:])
        vh, _ = _ln_stats(v)
        vn = vh * g_ref[...] + b_ref[...]
        for j in range(d // LANES):
            z = _spatial_z(vn, wc_ref, bias_ref, j)
            out_ref[:, j * LANES:(j + 1) * LANES] = (u[:, j * LANES:(j + 1) * LANES] * z).astype(out_ref.dtype)

    return _rows(body, s, CHUNK, [("blk", uvpre), ("all", vn_g), ("all", vn_b), ("all", wc), ("all", bias_full)],
                 [("blk", (s, d), MXU_DTYPE)], name)[0]


def _spatial_bwd(uvpre, dgated, vn_g, vn_b, wc, wct, bias_full, name):
    s, d2 = uvpre.shape
    d = d2 // 2

    def body(uv_ref, dg_ref, g_ref, b_ref, wc_ref, wct_ref, bias_ref,
             duv_ref, dws_ref, dbias_ref, dbin_ref, dvg_ref, dvb_ref, dvn_buf, a_bin, a_vg, a_vb):
        i = pl.program_id(0)

        @pl.when(i == 0)
        def _():
            dws_ref[...] = jnp.zeros_like(dws_ref)
            dbias_ref[...] = jnp.zeros_like(dbias_ref)
            a_bin[...] = jnp.zeros_like(a_bin)
            a_vg[...] = jnp.zeros_like(a_vg)
            a_vb[...] = jnp.zeros_like(a_vb)

        up = uv_ref[:, :d]
        vp = uv_ref[:, d:]
        u = _gelu(up)
        v = _gelu(vp)
        vh, rstd = _ln_stats(v)
        vn = vh * g_ref[...] + b_ref[...]
        dg = dg_ref[...]
        dzz = dg * u
        dbias_ref[...] += dzz
        for j in range(d // LANES):
            cols = slice(j * LANES, (j + 1) * LANES)
            z = _spatial_z(vn, wc_ref, bias_ref, j)
            dup = dg[:, cols] * z * _gelu_grad(up[:, cols])
            duv_ref[:, cols] = dup.astype(duv_ref.dtype)
            a_bin[:, cols] += _fold8(dup)
            dzb = dzz[:, cols]
            left = _left_half(dzb.shape)
            dvn_buf[:, cols] = jnp.where(left, _dot_nn(wct_ref[2 * j], dzb), _dot_nn(wct_ref[2 * j + 1], dzb))
            vb = vn[:, cols]
            dws_ref[2 * j] += _dot_nt(jnp.where(left, dzb, 0.0), vb)
            dws_ref[2 * j + 1] += _dot_nt(jnp.where(left, 0.0, dzb), vb)
        dvn = dvn_buf[...]
        a_vg[...] += _fold8(dvn * vh)
        a_vb[...] += _fold8(dvn)
        dvh = dvn * g_ref[...]
        dv = rstd * (dvh - jnp.mean(dvh, axis=-1, keepdims=True) - vh * jnp.mean(dvh * vh, axis=-1, keepdims=True))
        dvp = dv * _gelu_grad(vp)
        duv_ref[:, d:] = dvp.astype(duv_ref.dtype)
        a_bin[:, d:] += _fold8(dvp)

        @pl.when(i == pl.num_programs(0) - 1)
        def _():
            dbin_ref[...] = jnp.sum(a_bin[...], axis=0, keepdims=True)
            dvg_ref[...] = jnp.sum(a_vg[...], axis=0, keepdims=True)
            dvb_ref[...] = jnp.sum(a_vb[...], axis=0, keepdims=True)

    return _rows(body, s, CHUNK,
                 [("blk", uvpre), ("blk", dgated), ("all", vn_g), ("all", vn_b), ("all", wc), ("all", wct), ("all", bias_full)],
                 [("blk", (s, d2), MXU_DTYPE), ("all", (A_GROUPS, CHUNK, CHUNK), F32), ("all", (CHUNK, d), F32),
                  ("all", (1, d2), F32), ("all", (1, d), F32), ("all", (1, d), F32)], name,
                 scratch=[pltpu.VMEM((CHUNK, d), F32), pltpu.VMEM((SUBLANES, d2), F32),
                          pltpu.VMEM((SUBLANES, d), F32), pltpu.VMEM((SUBLANES, d), F32)])


def _head_mask(v, h):
    lane = lax.broadcasted_iota(jnp.int32, v.shape, 1)
    return jnp.where((lane >= h * HEAD_DIM) & (lane < (h + 1) * HEAD_DIM), v, jnp.zeros_like(v))


def _att_bias(slopes, dil):
    qi = lax.broadcasted_iota(jnp.int32, (SPAN, SPAN), 0)
    ki = lax.broadcasted_iota(jnp.int32, (SPAN, SPAN), 1)
    sl = slopes[:, None, None]
    cur = jnp.where(ki <= qi, -sl * (float(dil) * (qi - ki).astype(F32)), NEG)
    prev = jnp.where(ki >= qi, -sl * (float(dil) * (SPAN + qi - ki).astype(F32)), NEG)
    absent = jnp.full_like(prev, NEG)
    pairs = slopes.shape[0] // 2

    def fwd(pv):
        return jnp.concatenate([cur, pv], axis=2).reshape(pairs, 2 * SPAN, 2 * SPAN)

    def bwd(pv):
        return jnp.concatenate([cur.reshape(pairs, 2 * SPAN, SPAN), pv.reshape(pairs, 2 * SPAN, SPAN)], axis=1)

    return jnp.stack([fwd(absent), fwd(prev)]), jnp.stack([bwd(absent), bwd(prev)])


def _att_specs(s, d, dil, kinds):
    nb = s // (dil * SPAN)

    def rowblk(which, b):
        if which == "prev":
            return jnp.where(b % nb == 0, b, b - 1)
        if which == "next":
            return jnp.where(b % nb == nb - 1, b, b + 1)
        return b

    return [pl.BlockSpec((SPAN, d), functools.partial(lambda b, o, w: (rowblk(w, b), o), o=part, w=which))
            for part, which in kinds]


def _lane_col(v, h):
    return v[:, h * HEAD_DIM:h * HEAD_DIM + 1]


def _attn_fwd(qkv, slopes, dil, name):
    s, d3 = qkv.shape
    d = d3 // 3
    nb = s // (dil * SPAN)
    table, _ = _att_bias(slopes, dil)

    def body(q_ref, kc_ref, kp_ref, vc_ref, vp_ref, tb_ref, o_ref, l_ref):
        left = _left_half((SPAN, LANES))
        for hp in range(d // LANES):
            cols = slice(hp * LANES, (hp + 1) * LANES)
            q = q_ref[:, cols]
            q2 = jnp.concatenate([_head_mask(q, 0), _head_mask(q, 1)], axis=0) * ATT_SCALE
            k2 = jnp.concatenate([kc_ref[:, cols], kp_ref[:, cols]], axis=0)
            v2 = jnp.concatenate([vc_ref[:, cols], vp_ref[:, cols]], axis=0)
            sc = _dot_nt(q2, k2) + tb_ref[hp]
            m = jnp.max(sc, axis=-1, keepdims=True)
            p = jnp.exp(sc - m)
            l = jnp.sum(p, axis=-1, keepdims=True)
            r = _dot_nn(p, v2) * (1.0 / l)
            lse = jnp.broadcast_to(m + jnp.log(l), (2 * SPAN, LANES))
            o_ref[:, cols] = jnp.where(left, r[:SPAN], r[SPAN:])
            l_ref[:, cols] = jnp.where(left, lse[:SPAN], lse[SPAN:])

    specs = _att_specs(s, d, dil, [(0, "cur"), (1, "cur"), (1, "prev"), (2, "cur"), (2, "prev")])
    tbl = pl.BlockSpec((None,) + table.shape[1:], lambda b: (jnp.where(b % nb == 0, 0, 1), 0, 0, 0))
    out_spec = pl.BlockSpec((SPAN, d), lambda b: (b, 0))
    return pl.pallas_call(
        body,
        grid=(s // SPAN,),
        in_specs=specs + [tbl],
        out_specs=[out_spec, out_spec],
        out_shape=[jax.ShapeDtypeStruct((s, d), F32)] * 2,
        name=name,
        compiler_params=_cparams(("parallel",)),
    )(qkv, qkv, qkv, qkv, qkv, table)


def _attn_bwd(qkv, do, lse, dd, slopes, dil, name):
    s, d3 = qkv.shape
    d = d3 // 3
    nb = s // (dil * SPAN)
    _, table = _att_bias(slopes, dil)

    def heads_stacked(cur, nxt):
        return jnp.concatenate([_head_mask(cur, 0), _head_mask(cur, 1), _head_mask(nxt, 0), _head_mask(nxt, 1)], axis=0)

    def cols_stacked(cur, nxt):
        return jnp.concatenate([jnp.broadcast_to(_lane_col(a, h), (SPAN, LANES)) for a in (cur, nxt) for h in range(2)], axis=0)

    def body(k_ref, v_ref, qc_ref, qn_ref, doc_ref, don_ref, lc_ref, ln_ref, ddc_ref, ddn_ref, tb_ref, out_ref, carry):
        b = pl.program_id(0)

        @pl.when(b == 0)
        def _():
            carry[...] = jnp.zeros_like(carry)

        left = _left_half((SPAN, LANES))
        for hp in range(d // LANES):
            cols = slice(hp * LANES, (hp + 1) * LANES)
            k, v = k_ref[:, cols], v_ref[:, cols]
            q4 = heads_stacked(qc_ref[:, cols], qn_ref[:, cols])
            do4 = heads_stacked(doc_ref[:, cols], don_ref[:, cols])
            sc = _dot_nt(q4 * ATT_SCALE, k) + tb_ref[hp]
            p = jnp.exp(sc - cols_stacked(lc_ref[:, cols], ln_ref[:, cols]))
            ds = p * (_dot_nt(do4, v) - cols_stacked(ddc_ref[:, cols], ddn_ref[:, cols]))
            dq4 = _dot_nn(ds, k)
            dq_cur = jnp.where(left, dq4[:SPAN], dq4[SPAN:2 * SPAN]) + carry[:, cols]
            carry[:, cols] = jnp.where(left, dq4[2 * SPAN:3 * SPAN], dq4[3 * SPAN:])
            out_ref[:, cols] = (dq_cur * ATT_SCALE).astype(out_ref.dtype)
            out_ref[:, d + hp * LANES:d + (hp + 1) * LANES] = (_dot_tn(ds, q4) * ATT_SCALE).astype(out_ref.dtype)
            out_ref[:, 2 * d + hp * LANES:2 * d + (hp + 1) * LANES] = _dot_tn(p, do4).astype(out_ref.dtype)

    qkv_specs = _att_specs(s, d, dil, [(1, "cur"), (2, "cur"), (0, "cur"), (0, "next")])
    pair = _att_specs(s, d, dil, [(0, "cur"), (0, "next")])
    tbl = pl.BlockSpec((None,) + table.shape[1:], lambda b: (jnp.where(b % nb == nb - 1, 0, 1), 0, 0, 0))
    return pl.pallas_call(
        body,
        grid=(s // SPAN,),
        in_specs=qkv_specs + pair + pair + pair + [tbl],
        out_specs=pl.BlockSpec((SPAN, d3), lambda b: (b, 0)),
        out_shape=jax.ShapeDtypeStruct((s, d3), MXU_DTYPE),
        scratch_shapes=[pltpu.VMEM((SPAN, d), F32)],
        name=name,
        compiler_params=_cparams(("arbitrary",)),
    )(qkv, qkv, qkv, qkv, do, do, lse, lse, dd, dd, table)


def _mix_weights(l_refs):
    ls = [r[...] for r in l_refs]
    m = functools.reduce(jnp.maximum, ls)
    es = [jnp.exp(l - m) for l in ls]
    tot = functools.reduce(lambda a, c: a + c, es)
    return [e / tot for e in es]


def _combine_fwd(os_, ls_, name):
    s, d = os_[0].shape
    n = len(os_)

    def body(*refs):
        o_refs, l_refs, out_ref = refs[:n], refs[n:2 * n], refs[2 * n]
        ws = _mix_weights(l_refs)
        acc = ws[0] * o_refs[0][...]
        for w, o in zip(ws[1:], o_refs[1:]):
            acc = acc + w * o[...]
        out_ref[...] = acc

    return _rows(body, s, ROW_TILE, [("blk", a) for a in os_ + ls_], [("blk", (s, d), F32)], name)[0]


def _combine_bwd(do, o, ls_, name):
    s, d = o.shape
    n = len(ls_)
    ri = lax.broadcasted_iota(jnp.int32, (LANES, LANES), 0) // HEAD_DIM
    ci = lax.broadcasted_iota(jnp.int32, (LANES, LANES), 1) // HEAD_DIM
    seg = (ri == ci).astype(F32)

    def body(do_ref, o_ref, *rest):
        l_refs, seg_ref, outs = rest[:n], rest[n], rest[n + 1:]
        ws = _mix_weights(l_refs)
        dov = do_ref[...]
        prod = dov * o_ref[...]
        for j in range(d // LANES):
            cols = slice(j * LANES, (j + 1) * LANES)
            r = jnp.dot(prod[:, cols], seg_ref[...], precision=lax.Precision.HIGHEST, preferred_element_type=F32)
            for g in range(n):
                outs[2 * g][:, cols] = (ws[g][:, cols] * dov[:, cols]).astype(outs[2 * g].dtype)
                outs[2 * g + 1][:, cols] = ws[g][:, cols] * r

    outs = []
    for _ in range(n):
        outs += [("blk", (s, d), MXU_DTYPE), ("blk", (s, d), F32)]
    res = _rows(body, s, ROW_TILE, [("blk", do), ("blk", o)] + [("blk", l) for l in ls_] + [("all", seg)], outs, name)
    return [(res[2 * g], res[2 * g + 1]) for g in range(n)]


def _ada_fwd(c_all, w, b, name):
    nsub, d, cs = w.shape

    def body(c_ref, w_ref, b_ref, o_ref):
        cv = c_ref[...]
        sc = cv * (1.0 / (1.0 + jnp.exp(-cv)))
        o_ref[...] = _dot_nn(sc, w_ref[...]) + b_ref[...]

    return pl.pallas_call(
        body,
        grid=(nsub,),
        in_specs=[pl.BlockSpec(c_all.shape, lambda i: (0, 0)), pl.BlockSpec((None, d, cs), lambda i: (i, 0, 0)),
                  pl.BlockSpec((None, 1, cs), lambda i: (i, 0, 0))],
        out_specs=pl.BlockSpec((None, N_DEV, cs), lambda i: (i, 0, 0)),
        out_shape=jax.ShapeDtypeStruct((nsub, N_DEV, cs), F32),
        name=name,
        compiler_params=_cparams(("parallel",)),
    )(c_all, w, b)


def _ada_bwd(c_all_t, dm, name):
    d, nb = c_all_t.shape
    nsub, _, cs = dm.shape

    def body(c_ref, dm_ref, o_ref):
        cv = c_ref[...]
        sc = cv * (1.0 / (1.0 + jnp.exp(-cv)))
        acc = sc[:, 0:1] * dm_ref[0:1, :]
        for bi in range(1, nb):
            acc = acc + sc[:, bi:bi + 1] * dm_ref[bi:bi + 1, :]
        o_ref[...] = acc

    return pl.pallas_call(
        body,
        grid=(nsub,),
        in_specs=[pl.BlockSpec(c_all_t.shape, lambda i: (0, 0)), pl.BlockSpec((None, nb, cs), lambda i: (i, 0, 0))],
        out_specs=pl.BlockSpec((None, d, cs), lambda i: (i, 0, 0)),
        out_shape=jax.ShapeDtypeStruct((nsub, d, cs), F32),
        name=name,
        compiler_params=_cparams(("parallel",)),
    )(c_all_t, dm)


def _row_tile(r, row_elems):
    t = 2 * SUBLANES
    if r % t:
        return r
    while t * 2 * row_elems <= 256 * 1024 and r % (t * 2) == 0:
        t *= 2
    return t


def _adamw(w, g, m, v, name):
    shape = w.shape
    c = shape[-1]
    r = w.size // c
    tr = _row_tile(r, c)
    w2, g2, m2, v2 = [a.reshape(r, c) for a in (w, g, m, v)]
    bc1 = 1.0 - ADAM_B1 ** ADAM_STEP
    bc2 = 1.0 - ADAM_B2 ** ADAM_STEP

    def body(w_ref, g_ref, m_ref, v_ref, d_ref, nm_ref, nv_ref):
        gv = g_ref[...]
        nm = ADAM_B1 * m_ref[...] + (1.0 - ADAM_B1) * gv
        nv = ADAM_B2 * v_ref[...] + (1.0 - ADAM_B2) * (gv * gv)
        d_ref[...] = -ADAM_LR * ((nm / bc1) / (jnp.sqrt(nv / bc2) + ADAM_EPS) + ADAM_WD * w_ref[...])
        nm_ref[...] = nm
        nv_ref[...] = nv

    res = _rows(body, r, tr, [("blk", a) for a in (w2, g2, m2, v2)], [("blk", (r, c), F32)] * 3, name)
    return [a.reshape(shape) for a in res]


def _sum_slots(buf, name):
    n, r, c = buf.shape
    tr = _row_tile(r, n * c)

    def body(b_ref, o_ref):
        acc = b_ref[0].astype(F32)
        for k in range(1, n):
            acc = acc + b_ref[k].astype(F32)
        o_ref[...] = acc

    return pl.pallas_call(
        body,
        grid=(r // tr,),
        in_specs=[pl.BlockSpec((n, tr, c), lambda i: (0, i, 0))],
        out_specs=pl.BlockSpec((tr, c), lambda i: (i, 0)),
        out_shape=jax.ShapeDtypeStruct((r, c), F32),
        name=name,
        compiler_params=_cparams(("parallel",)),
    )(buf)


def _me():
    return lax.axis_index("x"), lax.axis_index("y"), lax.axis_index("c")


def _all_gather_small(blk, name):
    m_per, n = blk.shape

    def body(x_ref, out_ref, send_sems, recv_sems, local_sem):
        x, y, c = _me()
        me, sibling = (x, y, c), (x, y, 1 - c)
        chips = [(1 - x, y), (x, 1 - y), (1 - x, 1 - y)]

        def rows(px, py, pc):
            return out_ref.at[pl.ds((4 * px + 2 * py + pc) * m_per, m_per), :]

        def copy(k, block, to, src=None):
            return pltpu.make_async_remote_copy(
                src_ref=rows(*block) if src is None else src, dst_ref=rows(*block),
                send_sem=send_sems.at[k], recv_sem=recv_sems.at[k], device_id=to, device_id_type=MESH)

        mine = pltpu.make_async_copy(x_ref, rows(*me), local_sem)
        mine.start()
        first = [copy(0, me, sibling, src=x_ref)]
        first += [copy(1 + j, me, (*chip, c), src=x_ref) for j, chip in enumerate(chips)]
        for cp in first:
            cp.start()
        passed = [copy(4 + j, (*chip, c), sibling) for j, chip in enumerate(chips)]
        for j, chip in enumerate(chips):
            copy(1 + j, (*chip, c), me).wait_recv()
            passed[j].start()
        copy(0, sibling, me).wait_recv()
        for j, chip in enumerate(chips):
            copy(4 + j, (*chip, 1 - c), me).wait_recv()
        for cp in first + passed:
            cp.wait_send()
        mine.wait()

    return pl.pallas_call(
        body,
        out_shape=jax.ShapeDtypeStruct((N_DEV * m_per, n), blk.dtype),
        in_specs=[pl.BlockSpec(memory_space=pltpu.VMEM)],
        out_specs=pl.BlockSpec(memory_space=pltpu.VMEM),
        scratch_shapes=[pltpu.SemaphoreType.DMA((7,)), pltpu.SemaphoreType.DMA((7,)), pltpu.SemaphoreType.DMA],
        name=name,
        compiler_params=pltpu.CompilerParams(vmem_limit_bytes=VMEM_LIMIT),
    )(blk)


_HBM = pl.BlockSpec(memory_space=pltpu.HBM)
_SEM = pl.BlockSpec(memory_space=pltpu.SEMAPHORE)
_EFFECT = pltpu.SideEffectType.DATAFLOW_SIDE_EFFECTING


def _other_chips(x, y):
    return [(1 - x, y), (x, 1 - y), (1 - x, 1 - y)]


def _gather_copy(w, j, src_ref, land_ref, send_sems, recv_sems):
    x, y, c = _me()
    return pltpu.make_async_remote_copy(
        src_ref=src_ref, dst_ref=land_ref.at[2 * x + y], send_sem=send_sems.at[3 * w + j], recv_sem=recv_sems.at[3 * w + j],
        device_id=(*_other_chips(x, y)[j], c), device_id_type=MESH)


def _gather_start(shards, after, name):
    n = len(shards)
    lands = [lax.empty((N_CHIPS,) + s.shape, s.dtype) for s in shards]

    def body(*refs):
        in_refs, land_refs = refs[:n], refs[n:2 * n]
        send_sems, recv_sems = refs[2 * n + 1], refs[2 * n + 2]
        token = refs[-1]
        for w in range(n):
            for j in range(3):
                _gather_copy(w, j, in_refs[w], land_refs[w], send_sems, recv_sems).start()
        token[...] = jnp.zeros_like(token)

    res = pl.pallas_call(
        body,
        out_shape=(pltpu.SemaphoreType.DMA((3 * n,)), pltpu.SemaphoreType.DMA((3 * n,)),
                   *[pltpu.HBM(s.shape, s.dtype) for s in shards], *[pltpu.HBM(l.shape, l.dtype) for l in lands],
                   jax.ShapeDtypeStruct((SUBLANES, LANES), F32)),
        in_specs=[_HBM] * (2 * n) + [pl.BlockSpec(memory_space=pl.ANY)],
        out_specs=(_SEM, _SEM, *[_HBM] * (2 * n), pl.BlockSpec(memory_space=pltpu.VMEM)),
        input_output_aliases={i: 2 + i for i in range(2 * n)},
        name=name,
        compiler_params=pltpu.CompilerParams(has_side_effects=_EFFECT),
    )(*[pltpu.with_memory_space_constraint(a, pltpu.HBM) for a in list(shards) + lands], after)
    return res[0], res[1], res[2:2 + n], res[2 + n:2 + 2 * n], res[-1]


def _gather_wait(w, shard, land, send_sems, recv_sems, after, name):
    def body(s_ref, land_ref, send_sems, recv_sems, after_ref, s_out, land_out, stage):
        x, y, _ = _me()
        pltpu.sync_copy(s_ref, stage)
        pltpu.sync_copy(stage, land_out.at[2 * x + y])
        for j in range(3):
            cp = _gather_copy(w, j, s_ref, land_ref, send_sems, recv_sems)
            cp.wait_send()
            cp.wait_recv()

    return pl.pallas_call(
        body,
        out_shape=(pltpu.HBM(shard.shape, shard.dtype), pltpu.HBM(land.shape, land.dtype)),
        in_specs=(_HBM, _HBM, _SEM, _SEM, pl.BlockSpec(memory_space=pl.ANY)),
        out_specs=(_HBM, _HBM),
        input_output_aliases={0: 0, 1: 1},
        scratch_shapes=[pltpu.VMEM(shard.shape, shard.dtype)],
        name=name,
        compiler_params=pltpu.CompilerParams(has_side_effects=_EFFECT, vmem_limit_bytes=VMEM_LIMIT),
    )(shard, land, send_sems, recv_sems, after)[1]


def _piece_shape(shape, kind):
    k, nn = shape
    return (k // 2, nn // N_CHIPS) if kind == "col" else (k // N_CHIPS // 2, nn)


def _piece_of(g_ref, kind, tq, tc):
    pr, pc = _piece_shape(g_ref.shape, kind)
    if kind == "col":
        return g_ref.at[pl.ds(tc * pr, pr), pl.ds(tq * pc, pc)]
    return g_ref.at[pl.ds((2 * tq + tc) * pr, pr), :]


def _scatter_copy(r, kind, g_ref, land_ref, send_sems, recv_sems):
    x, y, c = _me()
    tx, ty, tc = (x + ((r >> 2) & 1)) % 2, (y + ((r >> 1) & 1)) % 2, (c + (r & 1)) % 2
    return pltpu.make_async_remote_copy(
        src_ref=_piece_of(g_ref, kind, 2 * tx + ty, tc), dst_ref=land_ref.at[4 * x + 2 * y + c],
        send_sem=send_sems.at[r], recv_sem=recv_sems.at[r], device_id=(tx, ty, tc), device_id_type=MESH)


def _scatter_start(g, kind, name):
    piece = _piece_shape(g.shape, kind)
    land = lax.empty((N_DEV,) + piece, g.dtype)

    def body(g_ref, land_ref, send_sems, recv_sems, g_out, land_out, stage):
        x, y, c = _me()
        for r in range(1, N_DEV):
            _scatter_copy(r, kind, g_ref, land_ref, send_sems, recv_sems).start()
        pltpu.sync_copy(_piece_of(g_ref, kind, 2 * x + y, c), stage)
        pltpu.sync_copy(stage, land_out.at[4 * x + 2 * y + c])

    return pl.pallas_call(
        body,
        out_shape=(pltpu.SemaphoreType.DMA((N_DEV,)), pltpu.SemaphoreType.DMA((N_DEV,)),
                   pltpu.HBM(g.shape, g.dtype), pltpu.HBM(land.shape, land.dtype)),
        in_specs=[_HBM, _HBM],
        out_specs=(_SEM, _SEM, _HBM, _HBM),
        input_output_aliases={0: 2, 1: 3},
        scratch_shapes=[pltpu.VMEM(piece, g.dtype)],
        name=name,
        compiler_params=pltpu.CompilerParams(has_side_effects=_EFFECT, vmem_limit_bytes=VMEM_LIMIT),
    )(pltpu.with_memory_space_constraint(g, pltpu.HBM), pltpu.with_memory_space_constraint(land, pltpu.HBM))


def _scatter_wait(send_sems, recv_sems, g, land, kind, after, name):
    def body(g_ref, land_ref, send_sems, recv_sems, after_ref, g_out, land_out):
        for r in range(1, N_DEV):
            cp = _scatter_copy(r, kind, g_ref, land_ref, send_sems, recv_sems)
            cp.wait_send()
            cp.wait_recv()

    return pl.pallas_call(
        body,
        out_shape=(pltpu.HBM(g.shape, g.dtype), pltpu.HBM(land.shape, land.dtype)),
        in_specs=(_HBM, _HBM, _SEM, _SEM, pl.BlockSpec(memory_space=pl.ANY)),
        out_specs=(_HBM, _HBM),
        input_output_aliases={0: 0, 1: 1},
        name=name,
        compiler_params=pltpu.CompilerParams(has_side_effects=_EFFECT),
    )(g, land, send_sems, recv_sems, after)[1]


def _swap_halves(halves, name):
    n = len(halves)

    def body(*refs):
        in_refs, out_refs = refs[:n], refs[n:2 * n]
        send_sems, recv_sems, local_sems = refs[2 * n:]
        x, y, c = _me()
        cps = []
        for w in range(n):
            lc = pltpu.make_async_copy(in_refs[w], out_refs[w].at[c], local_sems.at[w])
            lc.start()
            rc = pltpu.make_async_remote_copy(
                src_ref=in_refs[w], dst_ref=out_refs[w].at[c], send_sem=send_sems.at[w], recv_sem=recv_sems.at[w],
                device_id=(x, y, 1 - c), device_id_type=MESH)
            rc.start()
            cps.append((lc, rc))
        for lc, rc in cps:
            rc.wait_recv()
        for lc, rc in cps:
            rc.wait_send()
            lc.wait()

    vmem = pl.BlockSpec(memory_space=pltpu.VMEM)
    return pl.pallas_call(
        body,
        out_shape=[jax.ShapeDtypeStruct((2,) + h.shape, h.dtype) for h in halves],
        in_specs=[vmem] * n,
        out_specs=[vmem] * n,
        scratch_shapes=[pltpu.SemaphoreType.DMA((n,)), pltpu.SemaphoreType.DMA((n,)), pltpu.SemaphoreType.DMA((n,))],
        name=name,
        compiler_params=pltpu.CompilerParams(vmem_limit_bytes=VMEM_LIMIT),
    )(*halves)


def _to_streams(a, dil):
    if dil == 1:
        return a
    s, c = a.shape
    return a.reshape(s // dil, dil, c).transpose(1, 0, 2).reshape(s, c)


def _from_streams(a, dil):
    if dil == 1:
        return a
    s, c = a.shape
    return a.reshape(dil, s // dil, c).transpose(1, 0, 2).reshape(s, c)


def _mm_tiles(s):
    return min(s, 1024)


def _local_step(x0, target, mvec, ln_g, ln_b, small, fetch, emit, start):
    s, d = x0.shape
    tm = _mm_tiles(s)
    row = lambda v: v.reshape(1, -1)
    shift = [row(mvec[i, :d]) for i in range(4)]
    scale = [row(mvec[i, d:2 * d]) for i in range(4)]
    gate = [row(1.0 + mvec[i, 2 * d:]) for i in range(4)]
    lg = [row(ln_g[i]) for i in range(4)]
    lb = [row(ln_b[i]) for i in range(4)]
    mm = functools.partial(_mm, tm=tm)
    mm_w = functools.partial(_mm, tm=1024, tk=min(s, 2048), mode="tn")

    xs, ys, big = [x0], [], {}
    h0 = _mod(x0, scale[0], shift[0], start, "mod0")
    big["a_w_in"] = fetch("a_w_in", h0)
    uvpre = mm(h0, big["a_w_in"], mode="nn", name="a_in", outs=[F32], tn=512, tk=1024,
               epi=lambda r, bias: [r + bias], extras=[("row", small["a_b_in"])])
    gated = _spatial_fwd(uvpre, small["a_vn_g"], small["a_vn_b"], small["wc"], small["bias_full"], "a_spatial")
    big["a_w_out"] = fetch("a_w_out", gated)
    ys.append(mm(gated, big["a_w_out"], mode="nn", name="a_out", outs=[F32], tn=1024, tk=1024))
    x1, h1 = _resid_ln(xs[0], ys[0], gate[0], lg[0], lb[0], (scale[1], shift[1]), "ln0")
    xs.append(x1)
    relu2 = lambda r: [r, jnp.square(jnp.maximum(r, 0.0))]
    big["up0"] = fetch("up0", h1)
    a0, r0 = mm(h1, big["up0"], mode="nn", name="up0", outs=[MXU_DTYPE, MXU_DTYPE], tn=1024, tk=1024, epi=relu2)
    big["down0"] = fetch("down0", r0)
    ys.append(mm(r0, big["down0"], mode="nn", name="down0", outs=[F32], tn=1024, tk=2048))
    x2, h2 = _resid_ln(xs[1], ys[1], gate[1], lg[1], lb[1], (scale[2], shift[2]), "ln1")
    xs.append(x2)
    hg, qkvs, o_g, l_g = [], [], [], []
    big["b_w_qkv"] = fetch("b_w_qkv", h2)
    for g, (_, dil) in enumerate(B_PATTERNS):
        hp = _to_streams(h2, dil)
        qkv = mm(hp, big["b_w_qkv"], mode="nn", name=f"qkv{g}", outs=[MXU_DTYPE], tn=768, tk=1024, b_col0=g * 3 * d, n_out=3 * d)
        og, lgv = _attn_fwd(qkv, small["slopes"], dil, f"attn_fwd{g}")
        hg.append(hp)
        qkvs.append(qkv)
        o_g.append(_from_streams(og, dil))
        l_g.append(_from_streams(lgv, dil))
    o_mix = _combine_fwd(o_g, l_g, "combine")
    big["b_w_out"] = fetch("b_w_out", o_mix)
    ys.append(mm(o_mix, big["b_w_out"], mode="nn", name="b_out", outs=[F32], tn=1024, tk=1024))
    x3, h3 = _resid_ln(xs[2], ys[2], gate[2], lg[2], lb[2], (scale[3], shift[3]), "ln2")
    xs.append(x3)
    big["up1"] = fetch("up1", h3)
    a1, r1 = mm(h3, big["up1"], mode="nn", name="up1", outs=[MXU_DTYPE, MXU_DTYPE], tn=1024, tk=1024, epi=relu2)
    big["down1"] = fetch("down1", r1)
    ys.append(mm(r1, big["down1"], mode="nn", name="down1", outs=[F32], tn=1024, tk=2048))
    x4, _ = _resid_ln(xs[3], ys[3], gate[3], lg[3], lb[3], None, "ln3")

    gb, dm, dlg, dlb = {}, [None] * 4, [None] * 4, [None] * 4
    dx, loss = _loss_grad(x4, target, "loss")

    def mlp_bwd(i, sub, dx, h, a, r):
        dxr, dyy, red = _ln_bwd(dx, xs[sub], ys[sub], gate[sub], lg[sub], f"ln_bwd{sub}")
        gb[f"down{i}"] = mm_w(r, dyy, name=f"g_down{i}", outs=[MXU_DTYPE], tn=1024)
        da = mm(dyy, big[f"down{i}"], mode="nt", name=f"d_down{i}", outs=[MXU_DTYPE], tn=1024, tk=1024,
                after=emit(f"down{i}", gb[f"down{i}"]),
                epi=lambda acc, av: [acc * (2.0 * jnp.maximum(av.astype(F32), 0.0))], extras=[("full", a)])
        gb[f"up{i}"] = mm_w(h, da, name=f"g_up{i}", outs=[MXU_DTYPE], tn=1024)
        dh = mm(da, big[f"up{i}"], mode="nt", name=f"d_up{i}", outs=[F32], tn=1024, tk=1024, after=emit(f"up{i}", gb[f"up{i}"]))
        dx, red2 = _mod_bwd(dxr, [dh], xs[sub], scale[sub], f"mod_bwd{sub}")
        dm[sub] = jnp.concatenate([red2[0], red2[1], red[2]])
        dlg[sub], dlb[sub] = red[0], red[1]
        return dx

    dx = mlp_bwd(1, 3, dx, h3, a1, r1)
    dxr, dyy, red = _ln_bwd(dx, xs[2], ys[2], gate[2], lg[2], "ln_bwd2")
    gb["b_w_out"] = mm_w(o_mix, dyy, name="g_b_out", outs=[MXU_DTYPE], tn=1024, tk=1024)
    do = mm(dyy, big["b_w_out"], mode="nt", name="d_b_out", outs=[F32], tn=1024, tk=1024, after=emit("b_w_out", gb["b_w_out"]))
    parts = _combine_bwd(do, o_mix, l_g, "combine_bwd")
    dhs, gq = [], []
    for g, (_, dil) in enumerate(B_PATTERNS):
        do_g, dd_g = _to_streams(parts[g][0], dil), _to_streams(parts[g][1], dil)
        lse_g = _to_streams(l_g[g], dil)
        dqkv = _attn_bwd(qkvs[g], do_g, lse_g, dd_g, small["slopes"], dil, f"attn_bwd{g}")
        gq.append(mm_w(hg[g], dqkv, name=f"g_qkv{g}", outs=[MXU_DTYPE], tn=1024))
        dh = mm(dqkv, big["b_w_qkv"], mode="nt", name=f"d_qkv{g}", outs=[F32], tn=1024, tk=768, b_col0=g * 3 * d)
        dhs.append(_from_streams(dh, dil))
    gb["b_w_qkv"] = jnp.concatenate(gq, axis=1)
    dx, red2 = _mod_bwd(dxr, dhs, xs[2], scale[2], "mod_bwd2", after=emit("b_w_qkv", gb["b_w_qkv"]))
    dm[2] = jnp.concatenate([red2[0], red2[1], red[2]])
    dlg[2], dlb[2] = red[0], red[1]
    dx = mlp_bwd(0, 1, dx, h1, a0, r0)
    dxr, dyy, red = _ln_bwd(dx, xs[0], ys[0], gate[0], lg[0], "ln_bwd0")
    gb["a_w_out"] = mm_w(gated, dyy, name="g_a_out", outs=[MXU_DTYPE], tn=1024)
    dgated = mm(dyy, big["a_w_out"], mode="nt", name="d_a_out", outs=[F32], tn=1024, tk=1024, after=emit("a_w_out", gb["a_w_out"]))
    duv, dws, dbias, dbin, dvg, dvb = _spatial_bwd(uvpre, dgated, small["a_vn_g"], small["a_vn_b"], small["wc"],
                                                   small["wct"], small["bias_full"], "a_spatial_bwd")
    gb["a_w_in"] = mm_w(h0, duv, name="g_a_in", outs=[MXU_DTYPE], tn=1024)
    dh = mm(duv, big["a_w_in"], mode="nt", name="d_a_in", outs=[F32], tn=1024, tk=512, after=emit("a_w_in", gb["a_w_in"]))
    dx, red2 = _mod_bwd(dxr, [dh], xs[0], scale[0], "mod_bwd0")
    dm[0] = jnp.concatenate([red2[0], red2[1], red[2]])
    dlg[0], dlb[0] = red[0], red[1]

    tril = jnp.tril(jnp.ones((CHUNK, CHUNK), bool))
    gsmall = {
        "a_b_in": dbin.reshape(-1), "a_vn_g": dvg.reshape(-1), "a_vn_b": dvb.reshape(-1),
        "a_w_s": jnp.where(tril, dws, 0.0).reshape(-1),
        "a_b_s": dbias.reshape(CHUNK, A_GROUPS, d // A_GROUPS).sum(-1).T.reshape(-1),
    }
    return loss, dx, gb, jnp.stack(dm), jnp.stack(dlg), jnp.stack(dlb), gsmall


BIG = ("a_w_in", "a_w_out", "up0", "down0", "b_w_qkv", "b_w_out", "up1", "down1")
BIG_KIND = {"a_w_in": "col", "a_w_out": "row", "b_w_qkv": "col", "b_w_out": "row",
            "up0": "col", "up1": "col", "down0": "row", "down1": "row"}
SMALL = ("a_b_in", "a_vn_g", "a_vn_b", "a_b_s", "a_w_s")


def kernel(x, c, ada_w, ada_b, ln_g, ln_b, a_w_in, a_b_in, a_vn_g, a_vn_b, a_w_s, a_b_s, a_w_out, b_w_qkv, b_w_out, mlp_w_up, mlp_w_down, loss_target, m_ada_w, m_ada_b, m_ln_g, m_ln_b, m_a_w_in, m_a_b_in, m_a_vn_g, m_a_vn_b, m_a_w_s, m_a_b_s, m_a_w_out, m_b_w_qkv, m_b_w_out, m_mlp_w_up, m_mlp_w_down, v_ada_w, v_ada_b, v_ln_g, v_ln_b, v_a_w_in, v_a_b_in, v_a_vn_g, v_a_vn_b, v_a_w_s, v_a_b_s, v_a_w_out, v_b_w_qkv, v_b_w_out, v_mlp_w_up, v_mlp_w_down):
    s, d = x.shape[1], x.shape[2]
    xi, yi, ci = _me()
    q = 2 * xi + yi
    dev = 2 * q + ci
    nsub = 2 * DEPTH
    cs = ada_w.shape[-1]
    ls = ln_g.shape[-1]

    pack = jnp.concatenate([c.reshape(-1), ln_g.reshape(-1), ln_b.reshape(-1)]).reshape(-1, LANES)
    got = _all_gather_small(pack, "gather_small").reshape(N_DEV, -1)
    c_all = got[:, :d]
    per_chip = got[0::2]
    ln_g_full = per_chip[:, d:d + nsub * ls].reshape(N_CHIPS, nsub, ls).transpose(1, 0, 2).reshape(nsub, d)
    ln_b_full = per_chip[:, d + nsub * ls:].reshape(N_CHIPS, nsub, ls).transpose(1, 0, 2).reshape(nsub, d)
    m_part = _ada_fwd(c_all, ada_w.reshape(nsub, d, cs), ada_b.reshape(nsub, 1, cs), "ada_fwd")
    m_all = _all_gather_small(m_part.reshape(-1, LANES), "gather_mod").reshape(N_DEV, nsub, N_DEV, cs)
    m_mine = lax.dynamic_index_in_dim(m_all[0::2], dev, axis=2, keepdims=False)
    mvec = m_mine.transpose(1, 0, 2).reshape(nsub, 3 * d)

    shards = {
        "a_w_in": a_w_in[0], "a_w_out": a_w_out[0], "b_w_qkv": b_w_qkv[0], "b_w_out": b_w_out[0],
        "up0": mlp_w_up[0], "up1": mlp_w_up[1], "down0": mlp_w_down[0], "down1": mlp_w_down[1],
    }
    send_sems, recv_sems, shard_thru, lands, token = _gather_start([shards[k].astype(MXU_DTYPE) for k in BIG], mvec, "gather_start")

    def fetch(k, after):
        w = BIG.index(k)
        gw = _gather_wait(w, shard_thru[w], lands[w], send_sems, recv_sems, after, f"gather_wait_{k}")
        return gw if BIG_KIND[k] == "col" else gw.reshape(1, -1, gw.shape[-1])

    scattering = {}

    def emit(k, g):
        scattering[k] = _scatter_start(g, BIG_KIND[k], f"scatter_start_{k}")
        return scattering[k][2]

    tril = jnp.tril(jnp.ones((CHUNK, CHUNK), bool))
    wc = jnp.where(tril, a_w_s[0], 0.0).astype(MXU_DTYPE)
    heads = jnp.arange(1, B_HEADS + 1, dtype=F32)
    small = {
        "a_b_in": a_b_in, "a_vn_g": a_vn_g, "a_vn_b": a_vn_b,
        "wc": wc, "wct": wc.transpose(0, 2, 1),
        "bias_full": jnp.repeat(a_b_s[0].T, d // A_GROUPS, axis=1),
        "slopes": jnp.exp2(-8.0 * heads / B_HEADS),
    }

    loss_part, grad_x, gb, dm, dlg, dlb, gsmall = _local_step(x[0], loss_target[0], mvec, ln_g_full, ln_b_full, small, fetch, emit, token)
    loss = lax.psum(loss_part, ("x", "y", "c"))

    weights = dict(ada_w=ada_w, ada_b=ada_b, ln_g=ln_g, ln_b=ln_b, a_w_in=a_w_in, a_b_in=a_b_in, a_vn_g=a_vn_g, a_vn_b=a_vn_b,
                   a_w_s=a_w_s, a_b_s=a_b_s, a_w_out=a_w_out, b_w_qkv=b_w_qkv, b_w_out=b_w_out, mlp_w_up=mlp_w_up, mlp_w_down=mlp_w_down)
    ms = dict(ada_w=m_ada_w, ada_b=m_ada_b, ln_g=m_ln_g, ln_b=m_ln_b, a_w_in=m_a_w_in, a_b_in=m_a_b_in, a_vn_g=m_a_vn_g, a_vn_b=m_a_vn_b,
              a_w_s=m_a_w_s, a_b_s=m_a_b_s, a_w_out=m_a_w_out, b_w_qkv=m_b_w_qkv, b_w_out=m_b_w_out, mlp_w_up=m_mlp_w_up, mlp_w_down=m_mlp_w_down)
    vs = dict(ada_w=v_ada_w, ada_b=v_ada_b, ln_g=v_ln_g, ln_b=v_ln_b, a_w_in=v_a_w_in, a_b_in=v_a_b_in, a_vn_g=v_a_vn_g, a_vn_b=v_a_vn_b,
              a_w_s=v_a_w_s, a_b_s=v_a_b_s, a_w_out=v_a_w_out, b_w_qkv=v_b_w_qkv, b_w_out=v_b_w_out, mlp_w_up=v_mlp_w_up, mlp_w_down=v_mlp_w_down)
    grads, updates = {}, {}

    def update(k):
        updates[k] = _adamw(weights[k], grads[k], ms[k], vs[k], f"adamw_{k}")
        return updates[k][0]

    pack_b = jnp.concatenate([dm.reshape(-1), dlg.reshape(-1), dlb.reshape(-1)] + [gsmall[k] for k in SMALL])
    n_small = pack_b.shape[0]
    pack_b = jnp.pad(pack_b, (0, -n_small % (ROW_TILE * LANES)))
    got_b = _all_gather_small(pack_b.reshape(-1, LANES), "gather_small_grads").reshape(N_DEV, -1, LANES)
    tot = _sum_slots(got_b, "sum_small").reshape(-1)
    o = 0
    dm_tot = tot[o:o + nsub * 3 * d].reshape(nsub, 3 * d); o += nsub * 3 * d
    dlg_tot = tot[o:o + nsub * d].reshape(nsub, d); o += nsub * d
    dlb_tot = tot[o:o + nsub * d].reshape(nsub, d); o += nsub * d
    g_small = {}
    for k, ref in zip(SMALL, (a_b_in, a_vn_g, a_vn_b, a_b_s, a_w_s)):
        g_small[k] = tot[o:o + ref.size].reshape(ref.shape); o += ref.size
    assert o == n_small
    dm_all = got_b.reshape(N_DEV, -1)[:, :nsub * 3 * d].reshape(N_DEV, nsub, 3 * d)
    dm_cols = lax.dynamic_slice_in_dim(dm_all, q * cs, cs, axis=2).transpose(1, 0, 2)

    grads.update({
        "ada_w": _ada_bwd(c_all.T, dm_cols, "ada_bwd").reshape(ada_w.shape),
        "ada_b": lax.dynamic_slice_in_dim(dm_tot, q * cs, cs, axis=1).reshape(ada_b.shape),
        "ln_g": lax.dynamic_slice_in_dim(dlg_tot, q * ls, ls, axis=1).reshape(ln_g.shape),
        "ln_b": lax.dynamic_slice_in_dim(dlb_tot, q * ls, ls, axis=1).reshape(ln_b.shape),
        **g_small,
    })
    for k in ("ada_b", "ln_g", "ln_b") + SMALL:
        update(k)
    done = update("ada_w")

    gfull = {}
    for group in (("down1", "up1", "b_w_out", "b_w_qkv"), ("down0", "up0", "a_w_out", "a_w_in")):
        bufs = [_scatter_wait(*scattering[k], BIG_KIND[k], done, f"scatter_wait_{k}") for k in group]
        halves = [_sum_slots(b, f"sum_{k}") for k, b in zip(group, bufs)]
        fulls = _swap_halves(halves, f"swap_halves_{group[0]}")
        gfull.update({k: f.reshape(-1, f.shape[-1]) for k, f in zip(group, fulls)})
        if group[0] == "down1":
            grads["b_w_qkv"], grads["b_w_out"] = gfull["b_w_qkv"][None], gfull["b_w_out"][None]
            update("b_w_out")
            done = update("b_w_qkv")
    grads.update({
        "a_w_in": gfull["a_w_in"][None], "a_w_out": gfull["a_w_out"][None],
        "mlp_w_up": jnp.stack([gfull["up0"], gfull["up1"]]), "mlp_w_down": jnp.stack([gfull["down0"], gfull["down1"]]),
    })
    for k in ("a_w_in", "a_w_out", "mlp_w_up", "mlp_w_down"):
        update(k)
    names = list(weights)
    return (loss, grad_x[None], *[grads[k] for k in names], *[updates[k][0] for k in names],
            *[updates[k][1] for k in names], *[updates[k][2] for k in names])
```

```python
import functools
import math

import jax
import jax.numpy as jnp
from jax import lax
from jax.experimental import pallas as pl
from jax.experimental.pallas import tpu as pltpu

F32 = jnp.float32
MXU_DTYPE = jnp.bfloat16

DEPTH = 2
CHUNK = 128
A_GROUPS = 16
B_HEADS = 16
HEAD_DIM = 64
B_PATTERNS = ((128, 1), (512, 4), (2048, 16))
SPAN = 128
ALPHA = (2 * DEPTH) ** 0.25
LN_EPS = 1e-5
NEG = -1e30
ATT_SCALE = HEAD_DIM ** -0.5
ADAM_LR, ADAM_B1, ADAM_B2, ADAM_EPS, ADAM_WD, ADAM_STEP = 0.001, 0.9, 0.999, 1e-08, 0.01, 10

N_CHIPS = 4
N_DEV = 8
LANES = 128
SUBLANES = 8
VMEM_LIMIT = 52 * 1024 * 1024
ROW_TILE = 256
MESH = pl.DeviceIdType.MESH


def _cparams(sem):
    return pltpu.CompilerParams(dimension_semantics=sem, vmem_limit_bytes=VMEM_LIMIT)


def _fold8(v):
    r, c = v.shape
    return jnp.sum(v.reshape(r // SUBLANES, SUBLANES, c), axis=0)


def _gelu(x):
    c = math.sqrt(2.0 / math.pi)
    return 0.5 * x * (1.0 + jnp.tanh(c * (x + 0.044715 * (x * x * x))))


def _gelu_grad(x):
    c = math.sqrt(2.0 / math.pi)
    t = jnp.tanh(c * (x + 0.044715 * (x * x * x)))
    return 0.5 * (1.0 + t) + 0.5 * x * (1.0 - t * t) * c * (1.0 + 3.0 * 0.044715 * x * x)


def _dot(a, b, dims):
    return lax.dot_general(a.astype(MXU_DTYPE), b.astype(MXU_DTYPE), (dims, ((), ())), preferred_element_type=F32)


def _dot_nn(a, b):
    return _dot(a, b, ((1,), (0,)))


def _dot_nt(a, b):
    return _dot(a, b, ((1,), (1,)))


def _dot_tn(a, b):
    return _dot(a, b, ((0,), (0,)))


def _mm(a, b, *, mode, name, outs, tm, tn, tk, epi=None, extras=(), b_col0=0, n_out=None, after=None):
    if mode == "nn":
        m, kdim = a.shape
        p, kb, ns = b.shape
        assert kb == kdim and ns % tn == 0 and b_col0 % tn == 0
        n = n_out if n_out is not None else p * ns
        npt, j0 = ns // tn, b_col0 // tn
        a_spec = pl.BlockSpec((tm, tk), lambda i, j, k: (i, k))
        b_spec = pl.BlockSpec((None, tk, tn), lambda i, j, k: ((j + j0) // npt, k, (j + j0) % npt))
        dot = _dot_nn
    elif mode == "nt":
        m, kdim = a.shape
        p, n, ns = b.shape
        assert ns % tk == 0 and b_col0 % tk == 0
        npt, j0 = ns // tk, b_col0 // tk
        a_spec = pl.BlockSpec((tm, tk), lambda i, j, k: (i, k))
        b_spec = pl.BlockSpec((None, tn, tk), lambda i, j, k: ((k + j0) // npt, j, (k + j0) % npt))
        dot = _dot_nt
    else:
        kdim, m = a.shape
        kb, n = b.shape
        assert kb == kdim
        a_spec = pl.BlockSpec((tk, tm), lambda i, j, k: (k, i))
        b_spec = pl.BlockSpec((tk, tn), lambda i, j, k: (k, j))
        dot = _dot_tn
    assert m % tm == 0 and n % tn == 0 and kdim % tk == 0, (name, m, n, kdim, tm, tn, tk)
    nk = kdim // tk
    ex_specs, ex_arrays = [], []
    for kind, arr in extras:
        if kind == "row":
            ex_specs.append(pl.BlockSpec((1, tn), lambda i, j, k: (0, j)))
        else:
            ex_specs.append(pl.BlockSpec((tm, tn), lambda i, j, k: (i, j)))
        ex_arrays.append(arr)
    n_ex, n_o = len(ex_arrays), len(outs)
    n_dep = 0 if after is None else 1
    deps = [] if after is None else [after]

    def body(a_ref, b_ref, *rest):
        ex_refs, o_refs = rest[:n_ex], rest[n_ex + n_dep:n_ex + n_dep + n_o]
        k = pl.program_id(2)

        def finish(r):
            vals = epi(r, *[e[...] for e in ex_refs]) if epi is not None else [r]
            for o, v in zip(o_refs, vals):
                o[...] = v.astype(o.dtype)

        if nk == 1:
            finish(dot(a_ref[...], b_ref[...]))
            return
        acc = rest[n_ex + n_dep + n_o]

        @pl.when(k == 0)
        def _():
            acc[...] = dot(a_ref[...], b_ref[...])

        @pl.when((k > 0) & (k < nk - 1))
        def _():
            acc[...] += dot(a_ref[...], b_ref[...])

        @pl.when(k == nk - 1)
        def _():
            finish(acc[...] + dot(a_ref[...], b_ref[...]))

    res = pl.pallas_call(
        body,
        grid=(m // tm, n // tn, nk),
        in_specs=[a_spec, b_spec] + ex_specs + [pl.BlockSpec(memory_space=pl.ANY)] * n_dep,
        out_specs=[pl.BlockSpec((tm, tn), lambda i, j, k: (i, j)) for _ in outs],
        out_shape=[jax.ShapeDtypeStruct((m, n), dt) for dt in outs],
        scratch_shapes=[pltpu.VMEM((tm, tn), F32)] if nk > 1 else [],
        name=name,
        compiler_params=_cparams(("parallel", "parallel", "arbitrary")),
    )(a, b, *ex_arrays, *deps)
    return res if len(outs) > 1 else res[0]


def _rows(body, n_rows, tr, ins, outs, name, scratch=()):
    def spec(kind, shape):
        if kind == "blk":
            return pl.BlockSpec((tr,) + tuple(shape[1:]), lambda i: (i,) + (0,) * (len(shape) - 1))
        if kind == "dep":
            return pl.BlockSpec(memory_space=pl.ANY)
        return pl.BlockSpec(tuple(shape), lambda i: (0,) * len(shape))

    return pl.pallas_call(
        body,
        grid=(n_rows // tr,),
        in_specs=[spec(k, a.shape) for k, a in ins],
        out_specs=[spec(k, s) for k, s, _ in outs],
        out_shape=[jax.ShapeDtypeStruct(tuple(s), d) for _, s, d in outs],
        scratch_shapes=list(scratch),
        name=name,
        compiler_params=_cparams(("arbitrary",)),
    )(*[a for _, a in ins])


def _ln_stats(z):
    mu = jnp.mean(z, axis=-1, keepdims=True)
    zc = z - mu
    var = jnp.mean(zc * zc, axis=-1, keepdims=True)
    rstd = lax.rsqrt(var + LN_EPS)
    return zc * rstd, rstd


def _mod(x, scale, shift, after, name):
    s, d = x.shape

    def body(x_ref, sc_ref, sh_ref, dep_ref, h_ref):
        h_ref[...] = (x_ref[...] * (1.0 + sc_ref[...]) + sh_ref[...]).astype(h_ref.dtype)

    return _rows(body, s, ROW_TILE, [("blk", x), ("all", scale), ("all", shift), ("dep", after)], [("blk", (s, d), MXU_DTYPE)], name)[0]


def _resid_ln(x, y, gate, g, b, nxt, name):
    s, d = x.shape
    ins = [("blk", x), ("blk", y), ("all", gate), ("all", g), ("all", b)]
    outs = [("blk", (s, d), F32)]
    if nxt is not None:
        ins += [("all", nxt[0]), ("all", nxt[1])]
        outs += [("blk", (s, d), MXU_DTYPE)]

    def body(x_ref, y_ref, gate_ref, g_ref, b_ref, *rest):
        z = ALPHA * x_ref[...] + gate_ref[...] * y_ref[...]
        xhat, _ = _ln_stats(z)
        xn = xhat * g_ref[...] + b_ref[...]
        if nxt is None:
            rest[0][...] = xn
        else:
            sc_ref, sh_ref, xn_ref, h_ref = rest
            xn_ref[...] = xn
            h_ref[...] = (xn * (1.0 + sc_ref[...]) + sh_ref[...]).astype(h_ref.dtype)

    res = _rows(body, s, ROW_TILE, ins, outs, name)
    return (res[0], res[1]) if nxt is not None else (res[0], None)


def _loss_grad(xf, target, name):
    s, d = xf.shape

    def body(x_ref, t_ref, dy_ref, l_ref, acc):
        i = pl.program_id(0)

        @pl.when(i == 0)
        def _():
            acc[...] = jnp.zeros_like(acc)

        e = x_ref[...] - t_ref[...]
        dy_ref[...] = e * (1.0 / d)
        acc[...] += _fold8(e * e)

        @pl.when(i == pl.num_programs(0) - 1)
        def _():
            l_ref[...] = jnp.full(l_ref.shape, 0.5 / d, F32) * jnp.sum(acc[...])

    dy, l = _rows(body, s, ROW_TILE, [("blk", xf), ("blk", target)],
                  [("blk", (s, d), F32), ("all", (SUBLANES, LANES), F32)], name,
                  scratch=[pltpu.VMEM((SUBLANES, d), F32)])
    return dy, l[0, 0]


def _ln_bwd(dxo, x, y, gate, g, name):
    s, d = x.shape

    def body(dxo_ref, x_ref, y_ref, gate_ref, g_ref, dxr_ref, dyy_ref, red_ref, a_g, a_b, a_gate):
        i = pl.program_id(0)

        @pl.when(i == 0)
        def _():
            a_g[...] = jnp.zeros_like(a_g)
            a_b[...] = jnp.zeros_like(a_b)
            a_gate[...] = jnp.zeros_like(a_gate)

        yv = y_ref[...]
        z = ALPHA * x_ref[...] + gate_ref[...] * yv
        xhat, rstd = _ln_stats(z)
        dxo_v = dxo_ref[...]
        dxh = dxo_v * g_ref[...]
        dz = rstd * (dxh - jnp.mean(dxh, axis=-1, keepdims=True) - xhat * jnp.mean(dxh * xhat, axis=-1, keepdims=True))
        dxr_ref[...] = ALPHA * dz
        dyy_ref[...] = (gate_ref[...] * dz).astype(dyy_ref.dtype)
        a_g[...] += _fold8(dxo_v * xhat)
        a_b[...] += _fold8(dxo_v)
        a_gate[...] += _fold8(dz * yv)

        @pl.when(i == pl.num_programs(0) - 1)
        def _():
            red_ref[...] = jnp.zeros_like(red_ref)
            red_ref[0:1, :] = jnp.sum(a_g[...], axis=0, keepdims=True)
            red_ref[1:2, :] = jnp.sum(a_b[...], axis=0, keepdims=True)
            red_ref[2:3, :] = jnp.sum(a_gate[...], axis=0, keepdims=True)

    return _rows(body, s, ROW_TILE, [("blk", dxo), ("blk", x), ("blk", y), ("all", gate), ("all", g)],
                 [("blk", (s, d), F32), ("blk", (s, d), MXU_DTYPE), ("all", (SUBLANES, d), F32)], name,
                 scratch=[pltpu.VMEM((SUBLANES, d), F32)] * 3)


def _mod_bwd(dxr, dhs, x, scale, name, after=None):
    s, d = x.shape
    n_dh = len(dhs)
    n_dep = 0 if after is None else 1

    def body(dxr_ref, *rest):
        dh_refs = rest[:n_dh]
        x_ref, sc_ref, dx_ref, red_ref, a_sh, a_sc = rest[n_dh:n_dh + 2] + rest[n_dh + 2 + n_dep:]
        i = pl.program_id(0)

        @pl.when(i == 0)
        def _():
            a_sh[...] = jnp.zeros_like(a_sh)
            a_sc[...] = jnp.zeros_like(a_sc)

        dh = dh_refs[0][...]
        for r in dh_refs[1:]:
            dh = dh + r[...]
        dx_ref[...] = dxr_ref[...] + dh * (1.0 + sc_ref[...])
        a_sh[...] += _fold8(dh)
        a_sc[...] += _fold8(dh * x_ref[...])

        @pl.when(i == pl.num_programs(0) - 1)
        def _():
            red_ref[...] = jnp.zeros_like(red_ref)
            red_ref[0:1, :] = jnp.sum(a_sh[...], axis=0, keepdims=True)
            red_ref[1:2, :] = jnp.sum(a_sc[...], axis=0, keepdims=True)

    return _rows(body, s, ROW_TILE, [("blk", dxr)] + [("blk", h) for h in dhs] + [("blk", x), ("all", scale)] + [("dep", after)] * n_dep,
                 [("blk", (s, d), F32), ("all", (SUBLANES, d), F32)], name,
                 scratch=[pltpu.VMEM((SUBLANES, d), F32)] * 2)


def _mod_ln_bwd(dxr, dhs, x, scale, x_in, y, gate, g, name, after=None):
    s, d = x.shape
    n_dh = len(dhs)
    n_dep = 0 if after is None else 1

    def body(dxr_ref, *rest):
        dh_refs = rest[:n_dh]
        x_ref, sc_ref, xin_ref, y_ref, gate_ref, g_ref = rest[n_dh:n_dh + 6]
        dxr_out, dyy_ref, red_mod, red_ln, a_sh, a_sc, a_g, a_b, a_gate = rest[n_dh + 6 + n_dep:]
        i = pl.program_id(0)

        @pl.when(i == 0)
        def _():
            for a in (a_sh, a_sc, a_g, a_b, a_gate):
                a[...] = jnp.zeros_like(a)

        dh = dh_refs[0][...]
        for r in dh_refs[1:]:
            dh = dh + r[...]
        xv = x_ref[...]
        dxo_v = dxr_ref[...] + dh * (1.0 + sc_ref[...])
        a_sh[...] += _fold8(dh)
        a_sc[...] += _fold8(dh * xv)
        yv = y_ref[...]
        z = ALPHA * xin_ref[...] + gate_ref[...] * yv
        xhat, rstd = _ln_stats(z)
        dxh = dxo_v * g_ref[...]
        dz = rstd * (dxh - jnp.mean(dxh, axis=-1, keepdims=True) - xhat * jnp.mean(dxh * xhat, axis=-1, keepdims=True))
        dxr_out[...] = ALPHA * dz
        dyy_ref[...] = (gate_ref[...] * dz).astype(dyy_ref.dtype)
        a_g[...] += _fold8(dxo_v * xhat)
        a_b[...] += _fold8(dxo_v)
        a_gate[...] += _fold8(dz * yv)

        @pl.when(i == pl.num_programs(0) - 1)
        def _():
            red_mod[...] = jnp.zeros_like(red_mod)
            red_mod[0:1, :] = jnp.sum(a_sh[...], axis=0, keepdims=True)
            red_mod[1:2, :] = jnp.sum(a_sc[...], axis=0, keepdims=True)
            red_ln[...] = jnp.zeros_like(red_ln)
            red_ln[0:1, :] = jnp.sum(a_g[...], axis=0, keepdims=True)
            red_ln[1:2, :] = jnp.sum(a_b[...], axis=0, keepdims=True)
            red_ln[2:3, :] = jnp.sum(a_gate[...], axis=0, keepdims=True)

    ins = ([("blk", dxr)] + [("blk", h) for h in dhs]
           + [("blk", x), ("all", scale), ("blk", x_in), ("blk", y), ("all", gate), ("all", g)] + [("dep", after)] * n_dep)
    return _rows(body, s, ROW_TILE, ins,
                 [("blk", (s, d), F32), ("blk", (s, d), MXU_DTYPE), ("all", (SUBLANES, d), F32), ("all", (SUBLANES, d), F32)], name,
                 scratch=[pltpu.VMEM((SUBLANES, d), F32)] * 5)


def _left_half(shape):
    return lax.broadcasted_iota(jnp.int32, shape, 1) < (LANES // 2)


def _spatial_z(vn, wc_ref, bias_ref, j):
    vb = vn[:, j * LANES:(j + 1) * LANES]
    z0 = _dot_nn(wc_ref[2 * j], vb)
    z1 = _dot_nn(wc_ref[2 * j + 1], vb)
    return jnp.where(_left_half(z0.shape), z0, z1) + bias_ref[:, j * LANES:(j + 1) * LANES]


def _spatial_fwd(uvpre, vn_g, vn_b, wc, bias_full, name):
    s, d2 = uvpre.shape
    d = d2 // 2

    def body(uv_ref, g_ref, b_ref, wc_ref, bias_ref, out_ref):
        u = _gelu(uv_ref[:, :d])
        v = _gelu(uv_ref[:, d:])
        vh, _ = _ln_stats(v)
        vn = vh * g_ref[...] + b_ref[...]
        for j in range(d // LANES):
            z = _spatial_z(vn, wc_ref, bias_ref, j)
            out_ref[:, j * LANES:(j + 1) * LANES] = (u[:, j * LANES:(j + 1) * LANES] * z).astype(out_ref.dtype)

    return _rows(body, s, CHUNK, [("blk", uvpre), ("all", vn_g), ("all", vn_b), ("all", wc), ("all", bias_full)],
                 [("blk", (s, d), MXU_DTYPE)], name)[0]


def _spatial_bwd(uvpre, dgated, vn_g, vn_b, wc, wct, bias_full, name):
    s, d2 = uvpre.shape
    d = d2 // 2

    def body(uv_ref, dg_ref, g_ref, b_ref, wc_ref, wct_ref, bias_ref,
             duv_ref, dws_ref, dbias_ref, dbin_ref, dvg_ref, dvb_ref, dvn_buf, a_bin, a_vg, a_vb):
        i = pl.program_id(0)

        @pl.when(i == 0)
        def _():
            dws_ref[...] = jnp.zeros_like(dws_ref)
            dbias_ref[...] = jnp.zeros_like(dbias_ref)
            a_bin[...] = jnp.zeros_like(a_bin)
            a_vg[...] = jnp.zeros_like(a_vg)
            a_vb[...] = jnp.zeros_like(a_vb)

        up = uv_ref[:, :d]
        vp = uv_ref[:, d:]
        u = _gelu(up)
        v = _gelu(vp)
        vh, rstd = _ln_stats(v)
        vn = vh * g_ref[...] + b_ref[...]
        dg = dg_ref[...]
        dzz = dg * u
        dbias_ref[...] += dzz
        for j in range(d // LANES):
            cols = slice(j * LANES, (j + 1) * LANES)
            z = _spatial_z(vn, wc_ref, bias_ref, j)
            dup = dg[:, cols] * z * _gelu_grad(up[:, cols])
            duv_ref[:, cols] = dup.astype(duv_ref.dtype)
            a_bin[:, cols] += _fold8(dup)
            dzb = dzz[:, cols]
            left = _left_half(dzb.shape)
            dvn_buf[:, cols] = jnp.where(left, _dot_nn(wct_ref[2 * j], dzb), _dot_nn(wct_ref[2 * j + 1], dzb))
            vb = vn[:, cols]
            dws_ref[2 * j] += _dot_nt(jnp.where(left, dzb, 0.0), vb)
            dws_ref[2 * j + 1] += _dot_nt(jnp.where(left, 0.0, dzb), vb)
        dvn = dvn_buf[...]
        a_vg[...] += _fold8(dvn * vh)
        a_vb[...] += _fold8(dvn)
        dvh = dvn * g_ref[...]
        dv = rstd * (dvh - jnp.mean(dvh, axis=-1, keepdims=True) - vh * jnp.mean(dvh * vh, axis=-1, keepdims=True))
        dvp = dv * _gelu_grad(vp)
        duv_ref[:, d:] = dvp.astype(duv_ref.dtype)
        a_bin[:, d:] += _fold8(dvp)

        @pl.when(i == pl.num_programs(0) - 1)
        def _():
            dbin_ref[...] = jnp.sum(a_bin[...], axis=0, keepdims=True)
            dvg_ref[...] = jnp.sum(a_vg[...], axis=0, keepdims=True)
            dvb_ref[...] = jnp.sum(a_vb[...], axis=0, keepdims=True)

    return _rows(body, s, CHUNK,
                 [("blk", uvpre), ("blk", dgated), ("all", vn_g), ("all", vn_b), ("all", wc), ("all", wct), ("all", bias_full)],
                 [("blk", (s, d2), MXU_DTYPE), ("all", (A_GROUPS, CHUNK, CHUNK), F32), ("all", (CHUNK, d), F32),
                  ("all", (1, d2), F32), ("all", (1, d), F32), ("all", (1, d), F32)], name,
                 scratch=[pltpu.VMEM((CHUNK, d), F32), pltpu.VMEM((SUBLANES, d2), F32),
                          pltpu.VMEM((SUBLANES, d), F32), pltpu.VMEM((SUBLANES, d), F32)])


def _head_mask(v, h):
    lane = lax.broadcasted_iota(jnp.int32, v.shape, 1)
    return jnp.where((lane >= h * HEAD_DIM) & (lane < (h + 1) * HEAD_DIM), v, jnp.zeros_like(v))


def _att_bias(slopes, dil):
    qi = lax.broadcasted_iota(jnp.int32, (SPAN, SPAN), 0)
    ki = lax.broadcasted_iota(jnp.int32, (SPAN, SPAN), 1)
    sl = slopes[:, None, None]
    cur = jnp.where(ki <= qi, -sl * (float(dil) * (qi - ki).astype(F32)), NEG)
    prev = jnp.where(ki >= qi, -sl * (float(dil) * (SPAN + qi - ki).astype(F32)), NEG)
    absent = jnp.full_like(prev, NEG)
    pairs = slopes.shape[0] // 2

    def fwd(pv):
        return jnp.concatenate([cur, pv], axis=2).reshape(pairs, 2 * SPAN, 2 * SPAN)

    def bwd(pv):
        return jnp.concatenate([cur.reshape(pairs, 2 * SPAN, SPAN), pv.reshape(pairs, 2 * SPAN, SPAN)], axis=1)

    return jnp.stack([fwd(absent), fwd(prev)]), jnp.stack([bwd(absent), bwd(prev)])


def _att_specs(s, d, dil, kinds):
    nb = s // (dil * SPAN)

    def rowblk(which, b):
        if which == "prev":
            return jnp.where(b % nb == 0, b, b - 1)
        if which == "next":
            return jnp.where(b % nb == nb - 1, b, b + 1)
        return b

    return [pl.BlockSpec((SPAN, d), functools.partial(lambda b, o, w: (rowblk(w, b), o), o=part, w=which))
            for part, which in kinds]


def _lane_col(v, h):
    return v[:, h * HEAD_DIM:h * HEAD_DIM + 1]


def _attn_fwd(qkv, slopes, dil, name):
    s, d3 = qkv.shape
    d = d3 // 3
    nb = s // (dil * SPAN)
    table, _ = _att_bias(slopes, dil)

    def body(q_ref, kc_ref, kp_ref, vc_ref, vp_ref, tb_ref, o_ref, l_ref):
        left = _left_half((SPAN, LANES))
        for hp in range(d // LANES):
            cols = slice(hp * LANES, (hp + 1) * LANES)
            q = q_ref[:, cols]
            q2 = jnp.concatenate([_head_mask(q, 0), _head_mask(q, 1)], axis=0) * ATT_SCALE
            k2 = jnp.concatenate([kc_ref[:, cols], kp_ref[:, cols]], axis=0)
            v2 = jnp.concatenate([vc_ref[:, cols], vp_ref[:, cols]], axis=0)
            sc = _dot_nt(q2, k2) + tb_ref[hp]
            m = jnp.max(sc, axis=-1, keepdims=True)
            p = jnp.exp(sc - m)
            l = jnp.sum(p, axis=-1, keepdims=True)
            r = _dot_nn(p, v2) * (1.0 / l)
            lse = jnp.broadcast_to(m + jnp.log(l), (2 * SPAN, LANES))
            o_ref[:, cols] = jnp.where(left, r[:SPAN], r[SPAN:])
            l_ref[:, cols] = jnp.where(left, lse[:SPAN], lse[SPAN:])

    specs = _att_specs(s, d, dil, [(0, "cur"), (1, "cur"), (1, "prev"), (2, "cur"), (2, "prev")])
    tbl = pl.BlockSpec((None,) + table.shape[1:], lambda b: (jnp.where(b % nb == 0, 0, 1), 0, 0, 0))
    out_spec = pl.BlockSpec((SPAN, d), lambda b: (b, 0))
    return pl.pallas_call(
        body,
        grid=(s // SPAN,),
        in_specs=specs + [tbl],
        out_specs=[out_spec, out_spec],
        out_shape=[jax.ShapeDtypeStruct((s, d), F32)] * 2,
        name=name,
        compiler_params=_cparams(("parallel",)),
    )(qkv, qkv, qkv, qkv, qkv, table)


def _attn_bwd(qkv, do, lse, dd, slopes, dil, name):
    s, d3 = qkv.shape
    d = d3 // 3
    nb = s // (dil * SPAN)
    _, table = _att_bias(slopes, dil)

    def heads_stacked(cur, nxt):
        return jnp.concatenate([_head_mask(cur, 0), _head_mask(cur, 1), _head_mask(nxt, 0), _head_mask(nxt, 1)], axis=0)

    def cols_stacked(cur, nxt):
        return jnp.concatenate([jnp.broadcast_to(_lane_col(a, h), (SPAN, LANES)) for a in (cur, nxt) for h in range(2)], axis=0)

    def body(k_ref, v_ref, qc_ref, qn_ref, doc_ref, don_ref, lc_ref, ln_ref, ddc_ref, ddn_ref, tb_ref, out_ref, carry):
        b = pl.program_id(0)

        @pl.when(b == 0)
        def _():
            carry[...] = jnp.zeros_like(carry)

        left = _left_half((SPAN, LANES))
        for hp in range(d // LANES):
            cols = slice(hp * LANES, (hp + 1) * LANES)
            k, v = k_ref[:, cols], v_ref[:, cols]
            q4 = heads_stacked(qc_ref[:, cols], qn_ref[:, cols])
            do4 = heads_stacked(doc_ref[:, cols], don_ref[:, cols])
            sc = _dot_nt(q4 * ATT_SCALE, k) + tb_ref[hp]
            p = jnp.exp(sc - cols_stacked(lc_ref[:, cols], ln_ref[:, cols]))
            ds = p * (_dot_nt(do4, v) - cols_stacked(ddc_ref[:, cols], ddn_ref[:, cols]))
            dq4 = _dot_nn(ds, k)
            dq_cur = jnp.where(left, dq4[:SPAN], dq4[SPAN:2 * SPAN]) + carry[:, cols]
            carry[:, cols] = jnp.where(left, dq4[2 * SPAN:3 * SPAN], dq4[3 * SPAN:])
            out_ref[:, cols] = (dq_cur * ATT_SCALE).astype(out_ref.dtype)
            out_ref[:, d + hp * LANES:d + (hp + 1) * LANES] = (_dot_tn(ds, q4) * ATT_SCALE).astype(out_ref.dtype)
            out_ref[:, 2 * d + hp * LANES:2 * d + (hp + 1) * LANES] = _dot_tn(p, do4).astype(out_ref.dtype)

    qkv_specs = _att_specs(s, d, dil, [(1, "cur"), (2, "cur"), (0, "cur"), (0, "next")])
    pair = _att_specs(s, d, dil, [(0, "cur"), (0, "next")])
    tbl = pl.BlockSpec((None,) + table.shape[1:], lambda b: (jnp.where(b % nb == nb - 1, 0, 1), 0, 0, 0))
    return pl.pallas_call(
        body,
        grid=(s // SPAN,),
        in_specs=qkv_specs + pair + pair + pair + [tbl],
        out_specs=pl.BlockSpec((SPAN, d3), lambda b: (b, 0)),
        out_shape=jax.ShapeDtypeStruct((s, d3), MXU_DTYPE),
        scratch_shapes=[pltpu.VMEM((SPAN, d), F32)],
        name=name,
        compiler_params=_cparams(("arbitrary",)),
    )(qkv, qkv, qkv, qkv, do, do, lse, lse, dd, dd, table)


def _mix_weights(l_refs):
    ls = [r[...] for r in l_refs]
    m = functools.reduce(jnp.maximum, ls)
    es = [jnp.exp(l - m) for l in ls]
    tot = functools.reduce(lambda a, c: a + c, es)
    return [e / tot for e in es]


def _combine_fwd(os_, ls_, name):
    s, d = os_[0].shape
    n = len(os_)

    def body(*refs):
        o_refs, l_refs, out_ref = refs[:n], refs[n:2 * n], refs[2 * n]
        ws = _mix_weights(l_refs)
        acc = ws[0] * o_refs[0][...]
        for w, o in zip(ws[1:], o_refs[1:]):
            acc = acc + w * o[...]
        out_ref[...] = acc

    return _rows(body, s, ROW_TILE, [("blk", a) for a in os_ + ls_], [("blk", (s, d), F32)], name)[0]


def _combine_bwd(do, o, ls_, name):
    s, d = o.shape
    n = len(ls_)
    ri = lax.broadcasted_iota(jnp.int32, (LANES, LANES), 0) // HEAD_DIM
    ci = lax.broadcasted_iota(jnp.int32, (LANES, LANES), 1) // HEAD_DIM
    seg = (ri == ci).astype(F32)

    def body(do_ref, o_ref, *rest):
        l_refs, seg_ref, outs = rest[:n], rest[n], rest[n + 1:]
        ws = _mix_weights(l_refs)
        dov = do_ref[...]
        prod = dov * o_ref[...]
        for j in range(d // LANES):
            cols = slice(j * LANES, (j + 1) * LANES)
            r = jnp.dot(prod[:, cols], seg_ref[...], precision=lax.Precision.HIGHEST, preferred_element_type=F32)
            for g in range(n):
                outs[2 * g][:, cols] = (ws[g][:, cols] * dov[:, cols]).astype(outs[2 * g].dtype)
                outs[2 * g + 1][:, cols] = ws[g][:, cols] * r

    outs = []
    for _ in range(n):
        outs += [("blk", (s, d), MXU_DTYPE), ("blk", (s, d), F32)]
    res = _rows(body, s, ROW_TILE, [("blk", do), ("blk", o)] + [("blk", l) for l in ls_] + [("all", seg)], outs, name)
    return [(res[2 * g], res[2 * g + 1]) for g in range(n)]


def _ada_fwd(c_all, w, b, name):
    nsub, d, cs = w.shape

    def body(c_ref, w_ref, b_ref, o_ref):
        cv = c_ref[...]
        sc = cv * (1.0 / (1.0 + jnp.exp(-cv)))
        o_ref[...] = _dot_nn(sc, w_ref[...]) + b_ref[...]

    return pl.pallas_call(
        body,
        grid=(nsub,),
        in_specs=[pl.BlockSpec(c_all.shape, lambda i: (0, 0)), pl.BlockSpec((None, d, cs), lambda i: (i, 0, 0)),
                  pl.BlockSpec((None, 1, cs), lambda i: (i, 0, 0))],
        out_specs=pl.BlockSpec((None, N_DEV, cs), lambda i: (i, 0, 0)),
        out_shape=jax.ShapeDtypeStruct((nsub, N_DEV, cs), F32),
        name=name,
        compiler_params=_cparams(("parallel",)),
    )(c_all, w, b)


def _ada_bwd(c_all_t, dm, name):
    d, nb = c_all_t.shape
    nsub, _, cs = dm.shape

    def body(c_ref, dm_ref, o_ref):
        cv = c_ref[...]
        sc = cv * (1.0 / (1.0 + jnp.exp(-cv)))
        acc = sc[:, 0:1] * dm_ref[0:1, :]
        for bi in range(1, nb):
            acc = acc + sc[:, bi:bi + 1] * dm_ref[bi:bi + 1, :]
        o_ref[...] = acc

    return pl.pallas_call(
        body,
        grid=(nsub,),
        in_specs=[pl.BlockSpec(c_all_t.shape, lambda i: (0, 0)), pl.BlockSpec((None, nb, cs), lambda i: (i, 0, 0))],
        out_specs=pl.BlockSpec((None, d, cs), lambda i: (i, 0, 0)),
        out_shape=jax.ShapeDtypeStruct((nsub, d, cs), F32),
        name=name,
        compiler_params=_cparams(("parallel",)),
    )(c_all_t, dm)


def _row_tile(r, row_elems):
    t = 2 * SUBLANES
    if r % t:
        return r
    while t * 2 * row_elems <= 256 * 1024 and r % (t * 2) == 0:
        t *= 2
    return t


def _adamw(w, g, m, v, name):
    shape = w.shape
    c = shape[-1]
    r = w.size // c
    tr = _row_tile(r, c)
    w2, g2, m2, v2 = [a.reshape(r, c) for a in (w, g, m, v)]
    bc1 = 1.0 - ADAM_B1 ** ADAM_STEP
    bc2 = 1.0 - ADAM_B2 ** ADAM_STEP

    def body(w_ref, g_ref, m_ref, v_ref, d_ref, nm_ref, nv_ref):
        gv = g_ref[...]
        nm = ADAM_B1 * m_ref[...] + (1.0 - ADAM_B1) * gv
        nv = ADAM_B2 * v_ref[...] + (1.0 - ADAM_B2) * (gv * gv)
        d_ref[...] = -ADAM_LR * ((nm / bc1) / (jnp.sqrt(nv / bc2) + ADAM_EPS) + ADAM_WD * w_ref[...])
        nm_ref[...] = nm
        nv_ref[...] = nv

    res = _rows(body, r, tr, [("blk", a) for a in (w2, g2, m2, v2)], [("blk", (r, c), F32)] * 3, name)
    return [a.reshape(shape) for a in res]


def _sum_slots(buf, name):
    n, r, c = buf.shape
    tr = _row_tile(r, n * c)

    def body(b_ref, o_ref):
        acc = b_ref[0].astype(F32)
        for k in range(1, n):
            acc = acc + b_ref[k].astype(F32)
        o_ref[...] = acc

    return pl.pallas_call(
        body,
        grid=(r // tr,),
        in_specs=[pl.BlockSpec((n, tr, c), lambda i: (0, i, 0))],
        out_specs=pl.BlockSpec((tr, c), lambda i: (i, 0)),
        out_shape=jax.ShapeDtypeStruct((r, c), F32),
        name=name,
        compiler_params=_cparams(("parallel",)),
    )(buf)


def _me():
    return lax.axis_index("x"), lax.axis_index("y"), lax.axis_index("c")


def _all_gather_small(blk, name):
    m_per, n = blk.shape

    def body(x_ref, out_ref, send_sems, recv_sems, local_sem):
        x, y, c = _me()
        me, sibling = (x, y, c), (x, y, 1 - c)
        chips = [(1 - x, y), (x, 1 - y), (1 - x, 1 - y)]

        def rows(px, py, pc):
            return out_ref.at[pl.ds((4 * px + 2 * py + pc) * m_per, m_per), :]

        def copy(k, block, to, src=None):
            return pltpu.make_async_remote_copy(
                src_ref=rows(*block) if src is None else src, dst_ref=rows(*block),
                send_sem=send_sems.at[k], recv_sem=recv_sems.at[k], device_id=to, device_id_type=MESH)

        mine = pltpu.make_async_copy(x_ref, rows(*me), local_sem)
        mine.start()
        first = [copy(0, me, sibling, src=x_ref)]
        first += [copy(1 + j, me, (*chip, c), src=x_ref) for j, chip in enumerate(chips)]
        for cp in first:
            cp.start()
        passed = [copy(4 + j, (*chip, c), sibling) for j, chip in enumerate(chips)]
        for j, chip in enumerate(chips):
            copy(1 + j, (*chip, c), me).wait_recv()
            passed[j].start()
        copy(0, sibling, me).wait_recv()
        for j, chip in enumerate(chips):
            copy(4 + j, (*chip, 1 - c), me).wait_recv()
        for cp in first + passed:
            cp.wait_send()
        mine.wait()

    return pl.pallas_call(
        body,
        out_shape=jax.ShapeDtypeStruct((N_DEV * m_per, n), blk.dtype),
        in_specs=[pl.BlockSpec(memory_space=pltpu.VMEM)],
        out_specs=pl.BlockSpec(memory_space=pltpu.VMEM),
        scratch_shapes=[pltpu.SemaphoreType.DMA((7,)), pltpu.SemaphoreType.DMA((7,)), pltpu.SemaphoreType.DMA],
        name=name,
        compiler_params=pltpu.CompilerParams(vmem_limit_bytes=VMEM_LIMIT),
    )(blk)


_HBM = pl.BlockSpec(memory_space=pltpu.HBM)
_SEM = pl.BlockSpec(memory_space=pltpu.SEMAPHORE)
_EFFECT = pltpu.SideEffectType.DATAFLOW_SIDE_EFFECTING


def _other_chips(x, y):
    return [(1 - x, y), (x, 1 - y), (1 - x, 1 - y)]


def _gather_copy(w, j, src_ref, land_ref, send_sems, recv_sems):
    x, y, c = _me()
    return pltpu.make_async_remote_copy(
        src_ref=src_ref, dst_ref=land_ref.at[2 * x + y], send_sem=send_sems.at[3 * w + j], recv_sem=recv_sems.at[3 * w + j],
        device_id=(*_other_chips(x, y)[j], c), device_id_type=MESH)


def _gather_start(shards, after, name):
    n = len(shards)
    lands = [lax.empty((N_CHIPS,) + s.shape, s.dtype) for s in shards]

    def body(*refs):
        in_refs, land_refs = refs[:n], refs[n:2 * n]
        send_sems, recv_sems = refs[2 * n + 1], refs[2 * n + 2]
        token = refs[-1]
        for w in range(n):
            for j in range(3):
                _gather_copy(w, j, in_refs[w], land_refs[w], send_sems, recv_sems).start()
        token[...] = jnp.zeros_like(token)

    res = pl.pallas_call(
        body,
        out_shape=(pltpu.SemaphoreType.DMA((3 * n,)), pltpu.SemaphoreType.DMA((3 * n,)),
                   *[pltpu.HBM(s.shape, s.dtype) for s in shards], *[pltpu.HBM(l.shape, l.dtype) for l in lands],
                   jax.ShapeDtypeStruct((SUBLANES, LANES), F32)),
        in_specs=[_HBM] * (2 * n) + [pl.BlockSpec(memory_space=pl.ANY)],
        out_specs=(_SEM, _SEM, *[_HBM] * (2 * n), pl.BlockSpec(memory_space=pltpu.VMEM)),
        input_output_aliases={i: 2 + i for i in range(2 * n)},
        name=name,
        compiler_params=pltpu.CompilerParams(has_side_effects=_EFFECT),
    )(*[pltpu.with_memory_space_constraint(a, pltpu.HBM) for a in list(shards) + lands], after)
    return res[0], res[1], res[2:2 + n], res[2 + n:2 + 2 * n], res[-1]


def _gather_wait(w, shard, land, send_sems, recv_sems, after, name):
    def body(s_ref, land_ref, send_sems, recv_sems, after_ref, s_out, land_out, stage):
        x, y, _ = _me()
        pltpu.sync_copy(s_ref, stage)
        pltpu.sync_copy(stage, land_out.at[2 * x + y])
        for j in range(3):
            cp = _gather_copy(w, j, s_ref, land_ref, send_sems, recv_sems)
            cp.wait_send()
            cp.wait_recv()

    return pl.pallas_call(
        body,
        out_shape=(pltpu.HBM(shard.shape, shard.dtype), pltpu.HBM(land.shape, land.dtype)),
        in_specs=(_HBM, _HBM, _SEM, _SEM, pl.BlockSpec(memory_space=pl.ANY)),
        out_specs=(_HBM, _HBM),
        input_output_aliases={0: 0, 1: 1},
        scratch_shapes=[pltpu.VMEM(shard.shape, shard.dtype)],
        name=name,
        compiler_params=pltpu.CompilerParams(has_side_effects=_EFFECT, vmem_limit_bytes=VMEM_LIMIT),
    )(shard, land, send_sems, recv_sems, after)[1]


def _piece_shape(shape, kind):
    k, nn = shape
    return (k // 2, nn // N_CHIPS) if kind == "col" else (k // N_CHIPS // 2, nn)


def _piece_of(g_ref, kind, tq, tc):
    pr, pc = _piece_shape(g_ref.shape, kind)
    if kind == "col":
        return g_ref.at[pl.ds(tc * pr, pr), pl.ds(tq * pc, pc)]
    return g_ref.at[pl.ds((2 * tq + tc) * pr, pr), :]


def _scatter_copy(r, kind, g_ref, land_ref, send_sems, recv_sems):
    x, y, c = _me()
    tx, ty, tc = (x + ((r >> 2) & 1)) % 2, (y + ((r >> 1) & 1)) % 2, (c + (r & 1)) % 2
    return pltpu.make_async_remote_copy(
        src_ref=_piece_of(g_ref, kind, 2 * tx + ty, tc), dst_ref=land_ref.at[4 * x + 2 * y + c],
        send_sem=send_sems.at[r], recv_sem=recv_sems.at[r], device_id=(tx, ty, tc), device_id_type=MESH)


def _scatter_start(g, kind, name):
    piece = _piece_shape(g.shape, kind)
    land = lax.empty((N_DEV,) + piece, g.dtype)

    def body(g_ref, land_ref, send_sems, recv_sems, g_out, land_out, stage):
        x, y, c = _me()
        for r in range(1, N_DEV):
            _scatter_copy(r, kind, g_ref, land_ref, send_sems, recv_sems).start()
        pltpu.sync_copy(_piece_of(g_ref, kind, 2 * x + y, c), stage)
        pltpu.sync_copy(stage, land_out.at[4 * x + 2 * y + c])

    return pl.pallas_call(
        body,
        out_shape=(pltpu.SemaphoreType.DMA((N_DEV,)), pltpu.SemaphoreType.DMA((N_DEV,)),
                   pltpu.HBM(g.shape, g.dtype), pltpu.HBM(land.shape, land.dtype)),
        in_specs=[_HBM, _HBM],
        out_specs=(_SEM, _SEM, _HBM, _HBM),
        input_output_aliases={0: 2, 1: 3},
        scratch_shapes=[pltpu.VMEM(piece, g.dtype)],
        name=name,
        compiler_params=pltpu.CompilerParams(has_side_effects=_EFFECT, vmem_limit_bytes=VMEM_LIMIT),
    )(pltpu.with_memory_space_constraint(g, pltpu.HBM), pltpu.with_memory_space_constraint(land, pltpu.HBM))


def _scatter_wait(send_sems, recv_sems, g, land, kind, after, name):
    def body(g_ref, land_ref, send_sems, recv_sems, after_ref, g_out, land_out):
        for r in range(1, N_DEV):
            cp = _scatter_copy(r, kind, g_ref, land_ref, send_sems, recv_sems)
            cp.wait_send()
            cp.wait_recv()

    return pl.pallas_call(
        body,
        out_shape=(pltpu.HBM(g.shape, g.dtype), pltpu.HBM(land.shape, land.dtype)),
        in_specs=(_HBM, _HBM, _SEM, _SEM, pl.BlockSpec(memory_space=pl.ANY)),
        out_specs=(_HBM, _HBM),
        input_output_aliases={0: 0, 1: 1},
        name=name,
        compiler_params=pltpu.CompilerParams(has_side_effects=_EFFECT),
    )(g, land, send_sems, recv_sems, after)[1]


def _swap_halves(halves, name):
    n = len(halves)

    def body(*refs):
        in_refs, out_refs = refs[:n], refs[n:2 * n]
        send_sems, recv_sems, local_sems = refs[2 * n:]
        x, y, c = _me()
        cps = []
        for w in range(n):
            lc = pltpu.make_async_copy(in_refs[w], out_refs[w].at[c], local_sems.at[w])
            lc.start()
            rc = pltpu.make_async_remote_copy(
                src_ref=in_refs[w], dst_ref=out_refs[w].at[c], send_sem=send_sems.at[w], recv_sem=recv_sems.at[w],
                device_id=(x, y, 1 - c), device_id_type=MESH)
            rc.start()
            cps.append((lc, rc))
        for lc, rc in cps:
            rc.wait_recv()
        for lc, rc in cps:
            rc.wait_send()
            lc.wait()

    vmem = pl.BlockSpec(memory_space=pltpu.VMEM)
    return pl.pallas_call(
        body,
        out_shape=[jax.ShapeDtypeStruct((2,) + h.shape, h.dtype) for h in halves],
        in_specs=[vmem] * n,
        out_specs=[vmem] * n,
        scratch_shapes=[pltpu.SemaphoreType.DMA((n,)), pltpu.SemaphoreType.DMA((n,)), pltpu.SemaphoreType.DMA((n,))],
        name=name,
        compiler_params=pltpu.CompilerParams(vmem_limit_bytes=VMEM_LIMIT),
    )(*halves)


def _to_streams(a, dil):
    if dil == 1:
        return a
    s, c = a.shape
    return a.reshape(s // dil, dil, c).transpose(1, 0, 2).reshape(s, c)


def _from_streams(a, dil):
    if dil == 1:
        return a
    s, c = a.shape
    return a.reshape(dil, s // dil, c).transpose(1, 0, 2).reshape(s, c)


def _mm_tiles(s):
    return min(s, 1024)


def _local_step(x0, target, mvec, ln_g, ln_b, small, fetch, emit, start):
    s, d = x0.shape
    tm = _mm_tiles(s)
    row = lambda v: v.reshape(1, -1)
    shift = [row(mvec[i, :d]) for i in range(4)]
    scale = [row(mvec[i, d:2 * d]) for i in range(4)]
    gate = [row(1.0 + mvec[i, 2 * d:]) for i in range(4)]
    lg = [row(ln_g[i]) for i in range(4)]
    lb = [row(ln_b[i]) for i in range(4)]
    mm = functools.partial(_mm, tm=tm)
    mm_w = functools.partial(_mm, tm=1024, tk=min(s, 2048), mode="tn")

    xs, ys, big = [x0], [], {}
    h0 = _mod(x0, scale[0], shift[0], start, "mod0")
    big["a_w_in"] = fetch("a_w_in", h0)
    uvpre = mm(h0, big["a_w_in"], mode="nn", name="a_in", outs=[F32], tn=512, tk=1024,
               epi=lambda r, bias: [r + bias], extras=[("row", small["a_b_in"])])
    gated = _spatial_fwd(uvpre, small["a_vn_g"], small["a_vn_b"], small["wc"], small["bias_full"], "a_spatial")
    big["a_w_out"] = fetch("a_w_out", gated)
    ys.append(mm(gated, big["a_w_out"], mode="nn", name="a_out", outs=[F32], tn=1024, tk=1024))
    x1, h1 = _resid_ln(xs[0], ys[0], gate[0], lg[0], lb[0], (scale[1], shift[1]), "ln0")
    xs.append(x1)
    relu2 = lambda r: [jnp.square(jnp.maximum(r, 0.0))]
    big["up0"] = fetch("up0", h1)
    r0 = mm(h1, big["up0"], mode="nn", name="up0", outs=[MXU_DTYPE], tn=1024, tk=1024, epi=relu2)
    big["down0"] = fetch("down0", r0)
    ys.append(mm(r0, big["down0"], mode="nn", name="down0", outs=[F32], tn=1024, tk=2048))
    x2, h2 = _resid_ln(xs[1], ys[1], gate[1], lg[1], lb[1], (scale[2], shift[2]), "ln1")
    xs.append(x2)
    hg, qkvs, o_g, l_g, l_streams = [], [], [], [], []
    big["b_w_qkv"] = fetch("b_w_qkv", h2)
    for g, (_, dil) in enumerate(B_PATTERNS):
        hp = _to_streams(h2, dil)
        qkv = mm(hp, big["b_w_qkv"], mode="nn", name=f"qkv{g}", outs=[MXU_DTYPE], tn=768, tk=1024, b_col0=g * 3 * d, n_out=3 * d)
        og, lgv = _attn_fwd(qkv, small["slopes"], dil, f"attn_fwd{g}")
        hg.append(hp)
        qkvs.append(qkv)
        o_g.append(_from_streams(og, dil))
        l_g.append(_from_streams(lgv, dil))
        l_streams.append(lgv)
    o_mix = _combine_fwd(o_g, l_g, "combine")
    big["b_w_out"] = fetch("b_w_out", o_mix)
    ys.append(mm(o_mix, big["b_w_out"], mode="nn", name="b_out", outs=[F32], tn=1024, tk=1024))
    x3, h3 = _resid_ln(xs[2], ys[2], gate[2], lg[2], lb[2], (scale[3], shift[3]), "ln2")
    xs.append(x3)
    big["up1"] = fetch("up1", h3)
    r1 = mm(h3, big["up1"], mode="nn", name="up1", outs=[MXU_DTYPE], tn=1024, tk=1024, epi=relu2)
    big["down1"] = fetch("down1", r1)
    ys.append(mm(r1, big["down1"], mode="nn", name="down1", outs=[F32], tn=1024, tk=2048))
    x4, _ = _resid_ln(xs[3], ys[3], gate[3], lg[3], lb[3], None, "ln3")

    gb, red_ln, red_mod = {}, [None] * 4, [None] * 4
    dx, loss = _loss_grad(x4, target, "loss")

    def mlp_bwd(i, h, r, dyy):
        gb[f"down{i}"] = mm_w(r, dyy, name=f"g_down{i}", outs=[MXU_DTYPE], tn=1024)
        da = mm(dyy, big[f"down{i}"], mode="nt", name=f"d_down{i}", outs=[MXU_DTYPE], tn=1024, tk=1024,
                after=emit(f"down{i}", gb[f"down{i}"]),
                epi=lambda acc, rv: [acc * (2.0 * jnp.sqrt(rv.astype(F32)))], extras=[("full", r)])
        gb[f"up{i}"] = mm_w(h, da, name=f"g_up{i}", outs=[MXU_DTYPE], tn=1024)
        return [mm(da, big[f"up{i}"], mode="nt", name=f"d_up{i}", outs=[F32], tn=1024, tk=1024, after=emit(f"up{i}", gb[f"up{i}"]))]

    def join(sub, dxr, dhs, after=None):
        res = _mod_ln_bwd(dxr, dhs, xs[sub], scale[sub], xs[sub - 1], ys[sub - 1], gate[sub - 1], lg[sub - 1],
                          f"mod_ln_bwd{sub}", after=after)
        red_mod[sub], red_ln[sub - 1] = res[2], res[3]
        return res[0], res[1]

    dxr, dyy, red_ln[3] = _ln_bwd(dx, xs[3], ys[3], gate[3], lg[3], "ln_bwd3")
    dxr, dyy = join(3, dxr, mlp_bwd(1, h3, r1, dyy))
    gb["b_w_out"] = mm_w(o_mix, dyy, name="g_b_out", outs=[MXU_DTYPE], tn=1024, tk=1024)
    do = mm(dyy, big["b_w_out"], mode="nt", name="d_b_out", outs=[F32], tn=1024, tk=1024, after=emit("b_w_out", gb["b_w_out"]))
    parts = _combine_bwd(do, o_mix, l_g, "combine_bwd")
    dhs, gq = [], []
    for g, (_, dil) in enumerate(B_PATTERNS):
        do_g, dd_g = _to_streams(parts[g][0], dil), _to_streams(parts[g][1], dil)
        dqkv = _attn_bwd(qkvs[g], do_g, l_streams[g], dd_g, small["slopes"], dil, f"attn_bwd{g}")
        gq.append(mm_w(hg[g], dqkv, name=f"g_qkv{g}", outs=[MXU_DTYPE], tn=1024))
        dh = mm(dqkv, big["b_w_qkv"], mode="nt", name=f"d_qkv{g}", outs=[F32], tn=1024, tk=768, b_col0=g * 3 * d)
        dhs.append(_from_streams(dh, dil))
    gb["b_w_qkv"] = jnp.concatenate(gq, axis=1)
    dxr, dyy = join(2, dxr, dhs, after=emit("b_w_qkv", gb["b_w_qkv"]))
    dxr, dyy = join(1, dxr, mlp_bwd(0, h1, r0, dyy))
    gb["a_w_out"] = mm_w(gated, dyy, name="g_a_out", outs=[MXU_DTYPE], tn=1024)
    dgated = mm(dyy, big["a_w_out"], mode="nt", name="d_a_out", outs=[F32], tn=1024, tk=1024, after=emit("a_w_out", gb["a_w_out"]))
    duv, dws, dbias, dbin, dvg, dvb = _spatial_bwd(uvpre, dgated, small["a_vn_g"], small["a_vn_b"], small["wc"],
                                                   small["wct"], small["bias_full"], "a_spatial_bwd")
    gb["a_w_in"] = mm_w(h0, duv, name="g_a_in", outs=[MXU_DTYPE], tn=1024)
    dh = mm(duv, big["a_w_in"], mode="nt", name="d_a_in", outs=[F32], tn=1024, tk=512, after=emit("a_w_in", gb["a_w_in"]))
    dx, red_mod[0] = _mod_bwd(dxr, [dh], xs[0], scale[0], "mod_bwd0")
    dm = [jnp.concatenate([red_mod[i][0], red_mod[i][1], red_ln[i][2]]) for i in range(4)]
    dlg, dlb = [red_ln[i][0] for i in range(4)], [red_ln[i][1] for i in range(4)]

    tril = jnp.tril(jnp.ones((CHUNK, CHUNK), bool))
    gsmall = {
        "a_b_in": dbin.reshape(-1), "a_vn_g": dvg.reshape(-1), "a_vn_b": dvb.reshape(-1),
        "a_w_s": jnp.where(tril, dws, 0.0).reshape(-1),
        "a_b_s": dbias.reshape(CHUNK, A_GROUPS, d // A_GROUPS).sum(-1).T.reshape(-1),
    }
    return loss, dx, gb, jnp.stack(dm), jnp.stack(dlg), jnp.stack(dlb), gsmall


BIG = ("a_w_in", "a_w_out", "up0", "down0", "b_w_qkv", "b_w_out", "up1", "down1")
BIG_KIND = {"a_w_in": "col", "a_w_out": "row", "b_w_qkv": "col", "b_w_out": "row",
            "up0": "col", "up1": "col", "down0": "row", "down1": "row"}
SMALL = ("a_b_in", "a_vn_g", "a_vn_b", "a_b_s", "a_w_s")


def kernel(x, c, ada_w, ada_b, ln_g, ln_b, a_w_in, a_b_in, a_vn_g, a_vn_b, a_w_s, a_b_s, a_w_out, b_w_qkv, b_w_out, mlp_w_up, mlp_w_down, loss_target, m_ada_w, m_ada_b, m_ln_g, m_ln_b, m_a_w_in, m_a_b_in, m_a_vn_g, m_a_vn_b, m_a_w_s, m_a_b_s, m_a_w_out, m_b_w_qkv, m_b_w_out, m_mlp_w_up, m_mlp_w_down, v_ada_w, v_ada_b, v_ln_g, v_ln_b, v_a_w_in, v_a_b_in, v_a_vn_g, v_a_vn_b, v_a_w_s, v_a_b_s, v_a_w_out, v_b_w_qkv, v_b_w_out, v_mlp_w_up, v_mlp_w_down):
    s, d = x.shape[1], x.shape[2]
    xi, yi, ci = _me()
    q = 2 * xi + yi
    dev = 2 * q + ci
    nsub = 2 * DEPTH
    cs = ada_w.shape[-1]
    ls = ln_g.shape[-1]

    pack = jnp.concatenate([c.reshape(-1), ln_g.reshape(-1), ln_b.reshape(-1)]).reshape(-1, LANES)
    got = _all_gather_small(pack, "gather_small").reshape(N_DEV, -1)
    c_all = got[:, :d]
    per_chip = got[0::2]
    ln_g_full = per_chip[:, d:d + nsub * ls].reshape(N_CHIPS, nsub, ls).transpose(1, 0, 2).reshape(nsub, d)
    ln_b_full = per_chip[:, d + nsub * ls:].reshape(N_CHIPS, nsub, ls).transpose(1, 0, 2).reshape(nsub, d)
    m_part = _ada_fwd(c_all, ada_w.reshape(nsub, d, cs), ada_b.reshape(nsub, 1, cs), "ada_fwd")
    m_all = _all_gather_small(m_part.reshape(-1, LANES), "gather_mod").reshape(N_DEV, nsub, N_DEV, cs)
    m_mine = lax.dynamic_index_in_dim(m_all[0::2], dev, axis=2, keepdims=False)
    mvec = m_mine.transpose(1, 0, 2).reshape(nsub, 3 * d)

    shards = {
        "a_w_in": a_w_in[0], "a_w_out": a_w_out[0], "b_w_qkv": b_w_qkv[0], "b_w_out": b_w_out[0],
        "up0": mlp_w_up[0], "up1": mlp_w_up[1], "down0": mlp_w_down[0], "down1": mlp_w_down[1],
    }
    send_sems, recv_sems, shard_thru, lands, token = _gather_start([shards[k].astype(MXU_DTYPE) for k in BIG], mvec, "gather_start")

    def fetch(k, after):
        w = BIG.index(k)
        gw = _gather_wait(w, shard_thru[w], lands[w], send_sems, recv_sems, after, f"gather_wait_{k}")
        return gw if BIG_KIND[k] == "col" else gw.reshape(1, -1, gw.shape[-1])

    scattering = {}

    def emit(k, g):
        scattering[k] = _scatter_start(g, BIG_KIND[k], f"scatter_start_{k}")
        return scattering[k][2]

    tril = jnp.tril(jnp.ones((CHUNK, CHUNK), bool))
    wc = jnp.where(tril, a_w_s[0], 0.0).astype(MXU_DTYPE)
    heads = jnp.arange(1, B_HEADS + 1, dtype=F32)
    small = {
        "a_b_in": a_b_in, "a_vn_g": a_vn_g, "a_vn_b": a_vn_b,
        "wc": wc, "wct": wc.transpose(0, 2, 1),
        "bias_full": jnp.repeat(a_b_s[0].T, d // A_GROUPS, axis=1),
        "slopes": jnp.exp2(-8.0 * heads / B_HEADS),
    }

    loss_part, grad_x, gb, dm, dlg, dlb, gsmall = _local_step(x[0], loss_target[0], mvec, ln_g_full, ln_b_full, small, fetch, emit, token)
    loss = lax.psum(loss_part, ("x", "y", "c"))

    weights = dict(ada_w=ada_w, ada_b=ada_b, ln_g=ln_g, ln_b=ln_b, a_w_in=a_w_in, a_b_in=a_b_in, a_vn_g=a_vn_g, a_vn_b=a_vn_b,
                   a_w_s=a_w_s, a_b_s=a_b_s, a_w_out=a_w_out, b_w_qkv=b_w_qkv, b_w_out=b_w_out, mlp_w_up=mlp_w_up, mlp_w_down=mlp_w_down)
    ms = dict(ada_w=m_ada_w, ada_b=m_ada_b, ln_g=m_ln_g, ln_b=m_ln_b, a_w_in=m_a_w_in, a_b_in=m_a_b_in, a_vn_g=m_a_vn_g, a_vn_b=m_a_vn_b,
              a_w_s=m_a_w_s, a_b_s=m_a_b_s, a_w_out=m_a_w_out, b_w_qkv=m_b_w_qkv, b_w_out=m_b_w_out, mlp_w_up=m_mlp_w_up, mlp_w_down=m_mlp_w_down)
    vs = dict(ada_w=v_ada_w, ada_b=v_ada_b, ln_g=v_ln_g, ln_b=v_ln_b, a_w_in=v_a_w_in, a_b_in=v_a_b_in, a_vn_g=v_a_vn_g, a_vn_b=v_a_vn_b,
              a_w_s=v_a_w_s, a_b_s=v_a_b_s, a_w_out=v_a_w_out, b_w_qkv=v_b_w_qkv, b_w_out=v_b_w_out, mlp_w_up=v_mlp_w_up, mlp_w_down=v_mlp_w_down)
    grads, updates = {}, {}

    def update(k):
        updates[k] = _adamw(weights[k], grads[k], ms[k], vs[k], f"adamw_{k}")
        return updates[k][0]

    pack_b = jnp.concatenate([dm.reshape(-1), dlg.reshape(-1), dlb.reshape(-1)] + [gsmall[k] for k in SMALL])
    n_small = pack_b.shape[0]
    pack_b = jnp.pad(pack_b, (0, -n_small % (ROW_TILE * LANES)))
    got_b = _all_gather_small(pack_b.reshape(-1, LANES), "gather_small_grads").reshape(N_DEV, -1, LANES)
    tot = _sum_slots(got_b, "sum_small").reshape(-1)
    o = 0
    dm_tot = tot[o:o + nsub * 3 * d].reshape(nsub, 3 * d); o += nsub * 3 * d
    dlg_tot = tot[o:o + nsub * d].reshape(nsub, d); o += nsub * d
    dlb_tot = tot[o:o + nsub * d].reshape(nsub, d); o += nsub * d
    g_small = {}
    for k, ref in zip(SMALL, (a_b_in, a_vn_g, a_vn_b, a_b_s, a_w_s)):
        g_small[k] = tot[o:o + ref.size].reshape(ref.shape); o += ref.size
    assert o == n_small
    dm_all = got_b.reshape(N_DEV, -1)[:, :nsub * 3 * d].reshape(N_DEV, nsub, 3 * d)
    dm_cols = lax.dynamic_slice_in_dim(dm_all, q * cs, cs, axis=2).transpose(1, 0, 2)

    grads.update({
        "ada_w": _ada_bwd(c_all.T, dm_cols, "ada_bwd").reshape(ada_w.shape),
        "ada_b": lax.dynamic_slice_in_dim(dm_tot, q * cs, cs, axis=1).reshape(ada_b.shape),
        "ln_g": lax.dynamic_slice_in_dim(dlg_tot, q * ls, ls, axis=1).reshape(ln_g.shape),
        "ln_b": lax.dynamic_slice_in_dim(dlb_tot, q * ls, ls, axis=1).reshape(ln_b.shape),
        **g_small,
    })
    for k in ("ada_b", "ln_g", "ln_b") + SMALL:
        update(k)
    done = update("ada_w")

    gfull = {}
    for group in (("down1", "up1", "b_w_out", "b_w_qkv"), ("down0", "up0", "a_w_out", "a_w_in")):
        bufs = [_scatter_wait(*scattering[k], BIG_KIND[k], done, f"scatter_wait_{k}") for k in group]
        halves = [_sum_slots(b, f"sum_{k}") for k, b in zip(group, bufs)]
        fulls = _swap_halves(halves, f"swap_halves_{group[0]}")
        gfull.update({k: f.reshape(-1, f.shape[-1]) for k, f in zip(group, fulls)})
        if group[0] == "down1":
            grads["b_w_qkv"], grads["b_w_out"] = gfull["b_w_qkv"][None], gfull["b_w_out"][None]
            update("b_w_out")
            done = update("b_w_qkv")
    grads.update({
        "a_w_in": gfull["a_w_in"][None], "a_w_out": gfull["a_w_out"][None],
        "mlp_w_up": jnp.stack([gfull["up0"], gfull["up1"]]), "mlp_w_down": jnp.stack([gfull["down0"], gfull["down1"]]),
    })
    for k in ("a_w_in", "a_w_out", "mlp_w_up", "mlp_w_down"):
        update(k)
    names = list(weights)
    return (loss, grad_x[None], *[grads[k] for k in names], *[updates[k][0] for k in names],
            *[updates[k][1] for k in names], *[updates[k][2] for k in names])
```

```python
import functools
import math

import jax
import jax.numpy as jnp
from jax import lax
from jax.experimental import pallas as pl
from jax.experimental.pallas import tpu as pltpu

F32 = jnp.float32
MXU_DTYPE = jnp.bfloat16

DEPTH = 2
CHUNK = 128
A_GROUPS = 16
B_HEADS = 16
HEAD_DIM = 64
B_PATTERNS = ((128, 1), (512, 4), (2048, 16))
SPAN = 128
ALPHA = (2 * DEPTH) ** 0.25
LN_EPS = 1e-5
NEG = -1e30
ATT_SCALE = HEAD_DIM ** -0.5
ADAM_LR, ADAM_B1, ADAM_B2, ADAM_EPS, ADAM_WD, ADAM_STEP = 0.001, 0.9, 0.999, 1e-08, 0.01, 10

N_CHIPS = 4
N_DEV = 8
LANES = 128
SUBLANES = 8
VMEM_LIMIT = 52 * 1024 * 1024
ROW_TILE = 256
MESH = pl.DeviceIdType.MESH


def _cparams(sem):
    return pltpu.CompilerParams(dimension_semantics=sem, vmem_limit_bytes=VMEM_LIMIT)


def _fold8(v):
    r, c = v.shape
    return jnp.sum(v.reshape(r // SUBLANES, SUBLANES, c), axis=0)


def _gelu(x):
    c = math.sqrt(2.0 / math.pi)
    return 0.5 * x * (1.0 + jnp.tanh(c * (x + 0.044715 * (x * x * x))))


def _gelu_grad(x):
    c = math.sqrt(2.0 / math.pi)
    t = jnp.tanh(c * (x + 0.044715 * (x * x * x)))
    return 0.5 * (1.0 + t) + 0.5 * x * (1.0 - t * t) * c * (1.0 + 3.0 * 0.044715 * x * x)


def _dot(a, b, dims):
    return lax.dot_general(a.astype(MXU_DTYPE), b.astype(MXU_DTYPE), (dims, ((), ())), preferred_element_type=F32)


def _dot_nn(a, b):
    return _dot(a, b, ((1,), (0,)))


def _dot_nt(a, b):
    return _dot(a, b, ((1,), (1,)))


def _dot_tn(a, b):
    return _dot(a, b, ((0,), (0,)))


def _mm(a, b, *, mode, name, outs, tm, tn, tk, epi=None, extras=(), b_col0=0, n_out=None, after=None):
    if mode == "nn":
        m, kdim = a.shape
        p, kb, ns = b.shape
        assert kb == kdim and ns % tn == 0 and b_col0 % tn == 0
        n = n_out if n_out is not None else p * ns
        npt, j0 = ns // tn, b_col0 // tn
        a_spec = pl.BlockSpec((tm, tk), lambda i, j, k: (i, k))
        b_spec = pl.BlockSpec((None, tk, tn), lambda i, j, k: ((j + j0) // npt, k, (j + j0) % npt))
        dot = _dot_nn
    elif mode == "nt":
        m, kdim = a.shape
        p, n, ns = b.shape
        assert ns % tk == 0 and b_col0 % tk == 0
        npt, j0 = ns // tk, b_col0 // tk
        a_spec = pl.BlockSpec((tm, tk), lambda i, j, k: (i, k))
        b_spec = pl.BlockSpec((None, tn, tk), lambda i, j, k: ((k + j0) // npt, j, (k + j0) % npt))
        dot = _dot_nt
    else:
        kdim, m = a.shape
        kb, n = b.shape
        assert kb == kdim
        a_spec = pl.BlockSpec((tk, tm), lambda i, j, k: (k, i))
        b_spec = pl.BlockSpec((tk, tn), lambda i, j, k: (k, j))
        dot = _dot_tn
    assert m % tm == 0 and n % tn == 0 and kdim % tk == 0, (name, m, n, kdim, tm, tn, tk)
    nk = kdim // tk
    ex_specs, ex_arrays = [], []
    for kind, arr in extras:
        if kind == "row":
            ex_specs.append(pl.BlockSpec((1, tn), lambda i, j, k: (0, j)))
        else:
            ex_specs.append(pl.BlockSpec((tm, tn), lambda i, j, k: (i, j)))
        ex_arrays.append(arr)
    n_ex, n_o = len(ex_arrays), len(outs)
    n_dep = 0 if after is None else 1
    deps = [] if after is None else [after]

    def body(a_ref, b_ref, *rest):
        ex_refs, o_refs = rest[:n_ex], rest[n_ex + n_dep:n_ex + n_dep + n_o]
        k = pl.program_id(2)

        def finish(r):
            vals = epi(r, *[e[...] for e in ex_refs]) if epi is not None else [r]
            for o, v in zip(o_refs, vals):
                o[...] = v.astype(o.dtype)

        if nk == 1:
            finish(dot(a_ref[...], b_ref[...]))
            return
        acc = rest[n_ex + n_dep + n_o]

        @pl.when(k == 0)
        def _():
            acc[...] = dot(a_ref[...], b_ref[...])

        @pl.when((k > 0) & (k < nk - 1))
        def _():
            acc[...] += dot(a_ref[...], b_ref[...])

        @pl.when(k == nk - 1)
        def _():
            finish(acc[...] + dot(a_ref[...], b_ref[...]))

    res = pl.pallas_call(
        body,
        grid=(m // tm, n // tn, nk),
        in_specs=[a_spec, b_spec] + ex_specs + [pl.BlockSpec(memory_space=pl.ANY)] * n_dep,
        out_specs=[pl.BlockSpec((tm, tn), lambda i, j, k: (i, j)) for _ in outs],
        out_shape=[jax.ShapeDtypeStruct((m, n), dt) for dt in outs],
        scratch_shapes=[pltpu.VMEM((tm, tn), F32)] if nk > 1 else [],
        name=name,
        compiler_params=_cparams(("parallel", "parallel", "arbitrary")),
    )(a, b, *ex_arrays, *deps)
    return res if len(outs) > 1 else res[0]


def _rows(body, n_rows, tr, ins, outs, name, scratch=()):
    def spec(kind, shape):
        if kind == "blk":
            return pl.BlockSpec((tr,) + tuple(shape[1:]), lambda i: (i,) + (0,) * (len(shape) - 1))
        if kind == "dep":
            return pl.BlockSpec(memory_space=pl.ANY)
        return pl.BlockSpec(tuple(shape), lambda i: (0,) * len(shape))

    return pl.pallas_call(
        body,
        grid=(n_rows // tr,),
        in_specs=[spec(k, a.shape) for k, a in ins],
        out_specs=[spec(k, s) for k, s, _ in outs],
        out_shape=[jax.ShapeDtypeStruct(tuple(s), d) for _, s, d in outs],
        scratch_shapes=list(scratch),
        name=name,
        compiler_params=_cparams(("arbitrary",)),
    )(*[a for _, a in ins])


def _ln_stats(z):
    mu = jnp.mean(z, axis=-1, keepdims=True)
    zc = z - mu
    var = jnp.mean(zc * zc, axis=-1, keepdims=True)
    rstd = lax.rsqrt(var + LN_EPS)
    return zc * rstd, rstd


def _mod(x, scale, shift, after, name):
    s, d = x.shape

    def body(x_ref, sc_ref, sh_ref, dep_ref, h_ref):
        h_ref[...] = (x_ref[...] * (1.0 + sc_ref[...]) + sh_ref[...]).astype(h_ref.dtype)

    return _rows(body, s, ROW_TILE, [("blk", x), ("all", scale), ("all", shift), ("dep", after)], [("blk", (s, d), MXU_DTYPE)], name)[0]


def _resid_ln(x, y, gate, g, b, nxt, name):
    s, d = x.shape

    def body(x_ref, y_ref, gate_ref, g_ref, b_ref, sc_ref, sh_ref, xn_ref, h_ref):
        z = ALPHA * x_ref[...] + gate_ref[...] * y_ref[...]
        xhat, _ = _ln_stats(z)
        xn = xhat * g_ref[...] + b_ref[...]
        xn_ref[...] = xn
        h_ref[...] = (xn * (1.0 + sc_ref[...]) + sh_ref[...]).astype(h_ref.dtype)

    return _rows(body, s, ROW_TILE,
                 [("blk", x), ("blk", y), ("all", gate), ("all", g), ("all", b), ("all", nxt[0]), ("all", nxt[1])],
                 [("blk", (s, d), F32), ("blk", (s, d), MXU_DTYPE)], name)


def _mod_bwd(dxr, dhs, x, scale, name, after=None):
    s, d = x.shape
    n_dh = len(dhs)
    n_dep = 0 if after is None else 1

    def body(dxr_ref, *rest):
        dh_refs = rest[:n_dh]
        x_ref, sc_ref, dx_ref, red_ref, a_sh, a_sc = rest[n_dh:n_dh + 2] + rest[n_dh + 2 + n_dep:]
        i = pl.program_id(0)

        @pl.when(i == 0)
        def _():
            a_sh[...] = jnp.zeros_like(a_sh)
            a_sc[...] = jnp.zeros_like(a_sc)

        dh = dh_refs[0][...]
        for r in dh_refs[1:]:
            dh = dh + r[...]
        dx_ref[...] = dxr_ref[...] + dh * (1.0 + sc_ref[...])
        a_sh[...] += _fold8(dh)
        a_sc[...] += _fold8(dh * x_ref[...])

        @pl.when(i == pl.num_programs(0) - 1)
        def _():
            red_ref[...] = jnp.zeros_like(red_ref)
            red_ref[0:1, :] = jnp.sum(a_sh[...], axis=0, keepdims=True)
            red_ref[1:2, :] = jnp.sum(a_sc[...], axis=0, keepdims=True)

    return _rows(body, s, ROW_TILE, [("blk", dxr)] + [("blk", h) for h in dhs] + [("blk", x), ("all", scale)] + [("dep", after)] * n_dep,
                 [("blk", (s, d), F32), ("all", (SUBLANES, d), F32)], name,
                 scratch=[pltpu.VMEM((SUBLANES, d), F32)] * 2)


def _last_ln_loss_bwd(x, y, gate, g, b, target, name):
    s, d = x.shape

    def body(x_ref, y_ref, gate_ref, g_ref, b_ref, t_ref, l_ref, dxr_ref, dyy_ref, red_ref, a_l, a_g, a_b, a_gate):
        i = pl.program_id(0)

        @pl.when(i == 0)
        def _():
            for a in (a_l, a_g, a_b, a_gate):
                a[...] = jnp.zeros_like(a)

        yv = y_ref[...]
        z = ALPHA * x_ref[...] + gate_ref[...] * yv
        xhat, rstd = _ln_stats(z)
        e = xhat * g_ref[...] + b_ref[...] - t_ref[...]
        a_l[...] += _fold8(e * e)
        dxo_v = e * (1.0 / d)
        dxh = dxo_v * g_ref[...]
        dz = rstd * (dxh - jnp.mean(dxh, axis=-1, keepdims=True) - xhat * jnp.mean(dxh * xhat, axis=-1, keepdims=True))
        dxr_ref[...] = ALPHA * dz
        dyy_ref[...] = (gate_ref[...] * dz).astype(dyy_ref.dtype)
        a_g[...] += _fold8(dxo_v * xhat)
        a_b[...] += _fold8(dxo_v)
        a_gate[...] += _fold8(dz * yv)

        @pl.when(i == pl.num_programs(0) - 1)
        def _():
            l_ref[...] = jnp.full(l_ref.shape, 0.5 / d, F32) * jnp.sum(a_l[...])
            red_ref[...] = jnp.zeros_like(red_ref)
            red_ref[0:1, :] = jnp.sum(a_g[...], axis=0, keepdims=True)
            red_ref[1:2, :] = jnp.sum(a_b[...], axis=0, keepdims=True)
            red_ref[2:3, :] = jnp.sum(a_gate[...], axis=0, keepdims=True)

    l, dxr, dyy, red = _rows(
        body, s, ROW_TILE, [("blk", x), ("blk", y), ("all", gate), ("all", g), ("all", b), ("blk", target)],
        [("all", (SUBLANES, LANES), F32), ("blk", (s, d), F32), ("blk", (s, d), MXU_DTYPE), ("all", (SUBLANES, d), F32)], name,
        scratch=[pltpu.VMEM((SUBLANES, d), F32)] * 4)
    return l[0, 0], dxr, dyy, red


def _mod_ln_bwd(dxr, dhs, x, scale, x_in, y, gate, g, name, after=None):
    s, d = x.shape
    n_dh = len(dhs)
    n_dep = 0 if after is None else 1

    def body(dxr_ref, *rest):
        dh_refs = rest[:n_dh]
        x_ref, sc_ref, xin_ref, y_ref, gate_ref, g_ref = rest[n_dh:n_dh + 6]
        dxr_out, dyy_ref, red_mod, red_ln, a_sh, a_sc, a_g, a_b, a_gate = rest[n_dh + 6 + n_dep:]
        i = pl.program_id(0)

        @pl.when(i == 0)
        def _():
            for a in (a_sh, a_sc, a_g, a_b, a_gate):
                a[...] = jnp.zeros_like(a)

        dh = dh_refs[0][...]
        for r in dh_refs[1:]:
            dh = dh + r[...]
        xv = x_ref[...]
        dxo_v = dxr_ref[...] + dh * (1.0 + sc_ref[...])
        a_sh[...] += _fold8(dh)
        a_sc[...] += _fold8(dh * xv)
        yv = y_ref[...]
        z = ALPHA * xin_ref[...] + gate_ref[...] * yv
        xhat, rstd = _ln_stats(z)
        dxh = dxo_v * g_ref[...]
        dz = rstd * (dxh - jnp.mean(dxh, axis=-1, keepdims=True) - xhat * jnp.mean(dxh * xhat, axis=-1, keepdims=True))
        dxr_out[...] = ALPHA * dz
        dyy_ref[...] = (gate_ref[...] * dz).astype(dyy_ref.dtype)
        a_g[...] += _fold8(dxo_v * xhat)
        a_b[...] += _fold8(dxo_v)
        a_gate[...] += _fold8(dz * yv)

        @pl.when(i == pl.num_programs(0) - 1)
        def _():
            red_mod[...] = jnp.zeros_like(red_mod)
            red_mod[0:1, :] = jnp.sum(a_sh[...], axis=0, keepdims=True)
            red_mod[1:2, :] = jnp.sum(a_sc[...], axis=0, keepdims=True)
            red_ln[...] = jnp.zeros_like(red_ln)
            red_ln[0:1, :] = jnp.sum(a_g[...], axis=0, keepdims=True)
            red_ln[1:2, :] = jnp.sum(a_b[...], axis=0, keepdims=True)
            red_ln[2:3, :] = jnp.sum(a_gate[...], axis=0, keepdims=True)

    ins = ([("blk", dxr)] + [("blk", h) for h in dhs]
           + [("blk", x), ("all", scale), ("blk", x_in), ("blk", y), ("all", gate), ("all", g)] + [("dep", after)] * n_dep)
    return _rows(body, s, ROW_TILE, ins,
                 [("blk", (s, d), F32), ("blk", (s, d), MXU_DTYPE), ("all", (SUBLANES, d), F32), ("all", (SUBLANES, d), F32)], name,
                 scratch=[pltpu.VMEM((SUBLANES, d), F32)] * 5)


def _left_half(shape):
    return lax.broadcasted_iota(jnp.int32, shape, 1) < (LANES // 2)


def _spatial_z(vn, wc_ref, bias_ref, j):
    vb = vn[:, j * LANES:(j + 1) * LANES]
    z0 = _dot_nn(wc_ref[2 * j], vb)
    z1 = _dot_nn(wc_ref[2 * j + 1], vb)
    return jnp.where(_left_half(z0.shape), z0, z1) + bias_ref[:, j * LANES:(j + 1) * LANES]


def _spatial_fwd(uvpre, vn_g, vn_b, wc, bias_full, name):
    s, d2 = uvpre.shape
    d = d2 // 2

    def body(uv_ref, g_ref, b_ref, wc_ref, bias_ref, out_ref):
        u = _gelu(uv_ref[:, :d])
        v = _gelu(uv_ref[:, d:])
        vh, _ = _ln_stats(v)
        vn = vh * g_ref[...] + b_ref[...]
        for j in range(d // LANES):
            z = _spatial_z(vn, wc_ref, bias_ref, j)
            out_ref[:, j * LANES:(j + 1) * LANES] = (u[:, j * LANES:(j + 1) * LANES] * z).astype(out_ref.dtype)

    return _rows(body, s, CHUNK, [("blk", uvpre), ("all", vn_g), ("all", vn_b), ("all", wc), ("all", bias_full)],
                 [("blk", (s, d), MXU_DTYPE)], name)[0]


def _spatial_bwd(uvpre, dgated, vn_g, vn_b, wc, wct, bias_full, name):
    s, d2 = uvpre.shape
    d = d2 // 2

    def body(uv_ref, dg_ref, g_ref, b_ref, wc_ref, wct_ref, bias_ref,
             duv_ref, dws_ref, dbias_ref, dbin_ref, dvg_ref, dvb_ref, dvn_buf, a_bin, a_vg, a_vb):
        i = pl.program_id(0)

        @pl.when(i == 0)
        def _():
            dws_ref[...] = jnp.zeros_like(dws_ref)
            dbias_ref[...] = jnp.zeros_like(dbias_ref)
            a_bin[...] = jnp.zeros_like(a_bin)
            a_vg[...] = jnp.zeros_like(a_vg)
            a_vb[...] = jnp.zeros_like(a_vb)

        up = uv_ref[:, :d]
        vp = uv_ref[:, d:]
        u = _gelu(up)
        v = _gelu(vp)
        vh, rstd = _ln_stats(v)
        vn = vh * g_ref[...] + b_ref[...]
        dg = dg_ref[...]
        dzz = dg * u
        dbias_ref[...] += dzz
        for j in range(d // LANES):
            cols = slice(j * LANES, (j + 1) * LANES)
            z = _spatial_z(vn, wc_ref, bias_ref, j)
            dup = dg[:, cols] * z * _gelu_grad(up[:, cols])
            duv_ref[:, cols] = dup.astype(duv_ref.dtype)
            a_bin[:, cols] += _fold8(dup)
            dzb = dzz[:, cols]
            left = _left_half(dzb.shape)
            dvn_buf[:, cols] = jnp.where(left, _dot_nn(wct_ref[2 * j], dzb), _dot_nn(wct_ref[2 * j + 1], dzb))
            vb = vn[:, cols]
            dws_ref[2 * j] += _dot_nt(jnp.where(left, dzb, 0.0), vb)
            dws_ref[2 * j + 1] += _dot_nt(jnp.where(left, 0.0, dzb), vb)
        dvn = dvn_buf[...]
        a_vg[...] += _fold8(dvn * vh)
        a_vb[...] += _fold8(dvn)
        dvh = dvn * g_ref[...]
        dv = rstd * (dvh - jnp.mean(dvh, axis=-1, keepdims=True) - vh * jnp.mean(dvh * vh, axis=-1, keepdims=True))
        dvp = dv * _gelu_grad(vp)
        duv_ref[:, d:] = dvp.astype(duv_ref.dtype)
        a_bin[:, d:] += _fold8(dvp)

        @pl.when(i == pl.num_programs(0) - 1)
        def _():
            dbin_ref[...] = jnp.sum(a_bin[...], axis=0, keepdims=True)
            dvg_ref[...] = jnp.sum(a_vg[...], axis=0, keepdims=True)
            dvb_ref[...] = jnp.sum(a_vb[...], axis=0, keepdims=True)

    return _rows(body, s, CHUNK,
                 [("blk", uvpre), ("blk", dgated), ("all", vn_g), ("all", vn_b), ("all", wc), ("all", wct), ("all", bias_full)],
                 [("blk", (s, d2), MXU_DTYPE), ("all", (A_GROUPS, CHUNK, CHUNK), F32), ("all", (CHUNK, d), F32),
                  ("all", (1, d2), F32), ("all", (1, d), F32), ("all", (1, d), F32)], name,
                 scratch=[pltpu.VMEM((CHUNK, d), F32), pltpu.VMEM((SUBLANES, d2), F32),
                          pltpu.VMEM((SUBLANES, d), F32), pltpu.VMEM((SUBLANES, d), F32)])


def _head_mask(v, h):
    lane = lax.broadcasted_iota(jnp.int32, v.shape, 1)
    return jnp.where((lane >= h * HEAD_DIM) & (lane < (h + 1) * HEAD_DIM), v, jnp.zeros_like(v))


def _att_bias(slopes, dil):
    qi = lax.broadcasted_iota(jnp.int32, (SPAN, SPAN), 0)
    ki = lax.broadcasted_iota(jnp.int32, (SPAN, SPAN), 1)
    sl = slopes[:, None, None]
    cur = jnp.where(ki <= qi, -sl * (float(dil) * (qi - ki).astype(F32)), NEG)
    prev = jnp.where(ki >= qi, -sl * (float(dil) * (SPAN + qi - ki).astype(F32)), NEG)
    absent = jnp.full_like(prev, NEG)
    pairs = slopes.shape[0] // 2

    def fwd(pv):
        return jnp.concatenate([cur, pv], axis=2).reshape(pairs, 2 * SPAN, 2 * SPAN)

    def bwd(pv):
        return jnp.concatenate([cur.reshape(pairs, 2 * SPAN, SPAN), pv.reshape(pairs, 2 * SPAN, SPAN)], axis=1)

    return jnp.stack([fwd(absent), fwd(prev)]), jnp.stack([bwd(absent), bwd(prev)])


def _att_specs(s, d, dil, kinds):
    nb = s // (dil * SPAN)

    def rowblk(which, b):
        if which == "prev":
            return jnp.where(b % nb == 0, b, b - 1)
        if which == "next":
            return jnp.where(b % nb == nb - 1, b, b + 1)
        return b

    return [pl.BlockSpec((SPAN, d), functools.partial(lambda b, o, w: (rowblk(w, b), o), o=part, w=which))
            for part, which in kinds]


def _lane_col(v, h):
    return v[:, h * HEAD_DIM:h * HEAD_DIM + 1]


def _attn_fwd(qkv, slopes, dil, name):
    s, d3 = qkv.shape
    d = d3 // 3
    nb = s // (dil * SPAN)
    table, _ = _att_bias(slopes, dil)

    def body(q_ref, kc_ref, kp_ref, vc_ref, vp_ref, tb_ref, o_ref, l_ref):
        left = _left_half((SPAN, LANES))
        for hp in range(d // LANES):
            cols = slice(hp * LANES, (hp + 1) * LANES)
            q = q_ref[:, cols]
            q2 = jnp.concatenate([_head_mask(q, 0), _head_mask(q, 1)], axis=0) * ATT_SCALE
            k2 = jnp.concatenate([kc_ref[:, cols], kp_ref[:, cols]], axis=0)
            v2 = jnp.concatenate([vc_ref[:, cols], vp_ref[:, cols]], axis=0)
            sc = _dot_nt(q2, k2) + tb_ref[hp]
            m = jnp.max(sc, axis=-1, keepdims=True)
            p = jnp.exp(sc - m)
            l = jnp.sum(p, axis=-1, keepdims=True)
            r = _dot_nn(p, v2) * (1.0 / l)
            lse = jnp.broadcast_to(m + jnp.log(l), (2 * SPAN, LANES))
            o_ref[:, cols] = jnp.where(left, r[:SPAN], r[SPAN:])
            l_ref[:, cols] = jnp.where(left, lse[:SPAN], lse[SPAN:])

    specs = _att_specs(s, d, dil, [(0, "cur"), (1, "cur"), (1, "prev"), (2, "cur"), (2, "prev")])
    tbl = pl.BlockSpec((None,) + table.shape[1:], lambda b: (jnp.where(b % nb == 0, 0, 1), 0, 0, 0))
    out_spec = pl.BlockSpec((SPAN, d), lambda b: (b, 0))
    return pl.pallas_call(
        body,
        grid=(s // SPAN,),
        in_specs=specs + [tbl],
        out_specs=[out_spec, out_spec],
        out_shape=[jax.ShapeDtypeStruct((s, d), F32)] * 2,
        name=name,
        compiler_params=_cparams(("parallel",)),
    )(qkv, qkv, qkv, qkv, qkv, table)


def _attn_bwd(qkv, do, lse, dd, slopes, dil, name):
    s, d3 = qkv.shape
    d = d3 // 3
    nb = s // (dil * SPAN)
    _, table = _att_bias(slopes, dil)

    def heads_stacked(cur, nxt):
        return jnp.concatenate([_head_mask(cur, 0), _head_mask(cur, 1), _head_mask(nxt, 0), _head_mask(nxt, 1)], axis=0)

    def cols_stacked(cur, nxt):
        return jnp.concatenate([jnp.broadcast_to(_lane_col(a, h), (SPAN, LANES)) for a in (cur, nxt) for h in range(2)], axis=0)

    def body(k_ref, v_ref, qc_ref, qn_ref, doc_ref, don_ref, lc_ref, ln_ref, ddc_ref, ddn_ref, tb_ref, out_ref, carry):
        b = pl.program_id(0)

        @pl.when(b == 0)
        def _():
            carry[...] = jnp.zeros_like(carry)

        left = _left_half((SPAN, LANES))
        for hp in range(d // LANES):
            cols = slice(hp * LANES, (hp + 1) * LANES)
            k, v = k_ref[:, cols], v_ref[:, cols]
            q4 = heads_stacked(qc_ref[:, cols], qn_ref[:, cols])
            do4 = heads_stacked(doc_ref[:, cols], don_ref[:, cols])
            sc = _dot_nt(q4 * ATT_SCALE, k) + tb_ref[hp]
            p = jnp.exp(sc - cols_stacked(lc_ref[:, cols], ln_ref[:, cols]))
            ds = p * (_dot_nt(do4, v) - cols_stacked(ddc_ref[:, cols], ddn_ref[:, cols]))
            dq4 = _dot_nn(ds, k)
            dq_cur = jnp.where(left, dq4[:SPAN], dq4[SPAN:2 * SPAN]) + carry[:, cols]
            carry[:, cols] = jnp.where(left, dq4[2 * SPAN:3 * SPAN], dq4[3 * SPAN:])
            out_ref[:, cols] = (dq_cur * ATT_SCALE).astype(out_ref.dtype)
            out_ref[:, d + hp * LANES:d + (hp + 1) * LANES] = (_dot_tn(ds, q4) * ATT_SCALE).astype(out_ref.dtype)
            out_ref[:, 2 * d + hp * LANES:2 * d + (hp + 1) * LANES] = _dot_tn(p, do4).astype(out_ref.dtype)

    qkv_specs = _att_specs(s, d, dil, [(1, "cur"), (2, "cur"), (0, "cur"), (0, "next")])
    pair = _att_specs(s, d, dil, [(0, "cur"), (0, "next")])
    tbl = pl.BlockSpec((None,) + table.shape[1:], lambda b: (jnp.where(b % nb == nb - 1, 0, 1), 0, 0, 0))
    return pl.pallas_call(
        body,
        grid=(s // SPAN,),
        in_specs=qkv_specs + pair + pair + pair + [tbl],
        out_specs=pl.BlockSpec((SPAN, d3), lambda b: (b, 0)),
        out_shape=jax.ShapeDtypeStruct((s, d3), MXU_DTYPE),
        scratch_shapes=[pltpu.VMEM((SPAN, d), F32)],
        name=name,
        compiler_params=_cparams(("arbitrary",)),
    )(qkv, qkv, qkv, qkv, do, do, lse, lse, dd, dd, table)


def _mix_weights(l_refs):
    ls = [r[...] for r in l_refs]
    m = functools.reduce(jnp.maximum, ls)
    es = [jnp.exp(l - m) for l in ls]
    tot = functools.reduce(lambda a, c: a + c, es)
    return [e / tot for e in es]


def _combine_fwd(os_, ls_, name):
    s, d = os_[0].shape
    n = len(os_)

    def body(*refs):
        o_refs, l_refs, out_ref = refs[:n], refs[n:2 * n], refs[2 * n]
        ws = _mix_weights(l_refs)
        acc = ws[0] * o_refs[0][...]
        for w, o in zip(ws[1:], o_refs[1:]):
            acc = acc + w * o[...]
        out_ref[...] = acc

    return _rows(body, s, ROW_TILE, [("blk", a) for a in os_ + ls_], [("blk", (s, d), F32)], name)[0]


def _combine_bwd(do, o, ls_, name):
    s, d = o.shape
    n = len(ls_)
    ri = lax.broadcasted_iota(jnp.int32, (LANES, LANES), 0) // HEAD_DIM
    ci = lax.broadcasted_iota(jnp.int32, (LANES, LANES), 1) // HEAD_DIM
    seg = (ri == ci).astype(F32)

    def body(do_ref, o_ref, *rest):
        l_refs, seg_ref, outs = rest[:n], rest[n], rest[n + 1:]
        ws = _mix_weights(l_refs)
        dov = do_ref[...]
        prod = dov * o_ref[...]
        for j in range(d // LANES):
            cols = slice(j * LANES, (j + 1) * LANES)
            r = jnp.dot(prod[:, cols], seg_ref[...], precision=lax.Precision.HIGHEST, preferred_element_type=F32)
            for g in range(n):
                outs[2 * g][:, cols] = (ws[g][:, cols] * dov[:, cols]).astype(outs[2 * g].dtype)
                outs[2 * g + 1][:, cols] = ws[g][:, cols] * r

    outs = []
    for _ in range(n):
        outs += [("blk", (s, d), MXU_DTYPE), ("blk", (s, d), F32)]
    res = _rows(body, s, ROW_TILE, [("blk", do), ("blk", o)] + [("blk", l) for l in ls_] + [("all", seg)], outs, name)
    return [(res[2 * g], res[2 * g + 1]) for g in range(n)]


def _ada_fwd(c_all, w, b, name):
    nsub, d, cs = w.shape

    def body(c_ref, w_ref, b_ref, o_ref):
        cv = c_ref[...]
        sc = cv * (1.0 / (1.0 + jnp.exp(-cv)))
        o_ref[...] = _dot_nn(sc, w_ref[...]) + b_ref[...]

    return pl.pallas_call(
        body,
        grid=(nsub,),
        in_specs=[pl.BlockSpec(c_all.shape, lambda i: (0, 0)), pl.BlockSpec((None, d, cs), lambda i: (i, 0, 0)),
                  pl.BlockSpec((None, 1, cs), lambda i: (i, 0, 0))],
        out_specs=pl.BlockSpec((None, N_DEV, cs), lambda i: (i, 0, 0)),
        out_shape=jax.ShapeDtypeStruct((nsub, N_DEV, cs), F32),
        name=name,
        compiler_params=_cparams(("parallel",)),
    )(c_all, w, b)


def _ada_bwd(c_all_t, dm, name):
    d, nb = c_all_t.shape
    nsub, _, cs = dm.shape

    def body(c_ref, dm_ref, o_ref):
        cv = c_ref[...]
        sc = cv * (1.0 / (1.0 + jnp.exp(-cv)))
        acc = sc[:, 0:1] * dm_ref[0:1, :]
        for bi in range(1, nb):
            acc = acc + sc[:, bi:bi + 1] * dm_ref[bi:bi + 1, :]
        o_ref[...] = acc

    return pl.pallas_call(
        body,
        grid=(nsub,),
        in_specs=[pl.BlockSpec(c_all_t.shape, lambda i: (0, 0)), pl.BlockSpec((None, nb, cs), lambda i: (i, 0, 0))],
        out_specs=pl.BlockSpec((None, d, cs), lambda i: (i, 0, 0)),
        out_shape=jax.ShapeDtypeStruct((nsub, d, cs), F32),
        name=name,
        compiler_params=_cparams(("parallel",)),
    )(c_all_t, dm)


def _row_tile(r, row_elems):
    t = 2 * SUBLANES
    if r % t:
        return r
    while t * 2 * row_elems <= 256 * 1024 and r % (t * 2) == 0:
        t *= 2
    return t


def _adamw(w, g, m, v, name):
    shape = w.shape
    c = shape[-1]
    r = w.size // c
    tr = _row_tile(r, c)
    w2, g2, m2, v2 = [a.reshape(r, c) for a in (w, g, m, v)]
    bc1 = 1.0 - ADAM_B1 ** ADAM_STEP
    bc2 = 1.0 - ADAM_B2 ** ADAM_STEP

    def body(w_ref, g_ref, m_ref, v_ref, d_ref, nm_ref, nv_ref):
        gv = g_ref[...]
        nm = ADAM_B1 * m_ref[...] + (1.0 - ADAM_B1) * gv
        nv = ADAM_B2 * v_ref[...] + (1.0 - ADAM_B2) * (gv * gv)
        d_ref[...] = -ADAM_LR * ((nm / bc1) / (jnp.sqrt(nv / bc2) + ADAM_EPS) + ADAM_WD * w_ref[...])
        nm_ref[...] = nm
        nv_ref[...] = nv

    res = _rows(body, r, tr, [("blk", a) for a in (w2, g2, m2, v2)], [("blk", (r, c), F32)] * 3, name)
    return [a.reshape(shape) for a in res]


def _sum_slots(buf, name):
    n, r, c = buf.shape
    tr = _row_tile(r, n * c)

    def body(b_ref, o_ref):
        acc = b_ref[0].astype(F32)
        for k in range(1, n):
            acc = acc + b_ref[k].astype(F32)
        o_ref[...] = acc

    return pl.pallas_call(
        body,
        grid=(r // tr,),
        in_specs=[pl.BlockSpec((n, tr, c), lambda i: (0, i, 0))],
        out_specs=pl.BlockSpec((tr, c), lambda i: (i, 0)),
        out_shape=jax.ShapeDtypeStruct((r, c), F32),
        name=name,
        compiler_params=_cparams(("parallel",)),
    )(buf)


def _me():
    return lax.axis_index("x"), lax.axis_index("y"), lax.axis_index("c")


def _all_gather_small(blk, name):
    m_per, n = blk.shape

    def body(x_ref, out_ref, send_sems, recv_sems, local_sem):
        x, y, c = _me()
        me, sibling = (x, y, c), (x, y, 1 - c)
        chips = [(1 - x, y), (x, 1 - y), (1 - x, 1 - y)]

        def rows(px, py, pc):
            return out_ref.at[pl.ds((4 * px + 2 * py + pc) * m_per, m_per), :]

        def copy(k, block, to, src=None):
            return pltpu.make_async_remote_copy(
                src_ref=rows(*block) if src is None else src, dst_ref=rows(*block),
                send_sem=send_sems.at[k], recv_sem=recv_sems.at[k], device_id=to, device_id_type=MESH)

        mine = pltpu.make_async_copy(x_ref, rows(*me), local_sem)
        mine.start()
        first = [copy(0, me, sibling, src=x_ref)]
        first += [copy(1 + j, me, (*chip, c), src=x_ref) for j, chip in enumerate(chips)]
        for cp in first:
            cp.start()
        passed = [copy(4 + j, (*chip, c), sibling) for j, chip in enumerate(chips)]
        for j, chip in enumerate(chips):
            copy(1 + j, (*chip, c), me).wait_recv()
            passed[j].start()
        copy(0, sibling, me).wait_recv()
        for j, chip in enumerate(chips):
            copy(4 + j, (*chip, 1 - c), me).wait_recv()
        for cp in first + passed:
            cp.wait_send()
        mine.wait()

    return pl.pallas_call(
        body,
        out_shape=jax.ShapeDtypeStruct((N_DEV * m_per, n), blk.dtype),
        in_specs=[pl.BlockSpec(memory_space=pltpu.VMEM)],
        out_specs=pl.BlockSpec(memory_space=pltpu.VMEM),
        scratch_shapes=[pltpu.SemaphoreType.DMA((7,)), pltpu.SemaphoreType.DMA((7,)), pltpu.SemaphoreType.DMA],
        name=name,
        compiler_params=pltpu.CompilerParams(vmem_limit_bytes=VMEM_LIMIT),
    )(blk)


_HBM = pl.BlockSpec(memory_space=pltpu.HBM)
_SEM = pl.BlockSpec(memory_space=pltpu.SEMAPHORE)
_EFFECT = pltpu.SideEffectType.DATAFLOW_SIDE_EFFECTING


def _other_chips(x, y):
    return [(1 - x, y), (x, 1 - y), (1 - x, 1 - y)]


def _gather_copy(w, j, src_ref, land_ref, send_sems, recv_sems):
    x, y, c = _me()
    return pltpu.make_async_remote_copy(
        src_ref=src_ref, dst_ref=land_ref.at[2 * x + y], send_sem=send_sems.at[3 * w + j], recv_sem=recv_sems.at[3 * w + j],
        device_id=(*_other_chips(x, y)[j], c), device_id_type=MESH)


def _gather_start(shards, after, name):
    n = len(shards)
    lands = [lax.empty((N_CHIPS,) + s.shape, s.dtype) for s in shards]

    def body(*refs):
        in_refs, land_refs = refs[:n], refs[n:2 * n]
        send_sems, recv_sems = refs[2 * n + 1], refs[2 * n + 2]
        token = refs[-1]
        for w in range(n):
            for j in range(3):
                _gather_copy(w, j, in_refs[w], land_refs[w], send_sems, recv_sems).start()
        token[...] = jnp.zeros_like(token)

    res = pl.pallas_call(
        body,
        out_shape=(pltpu.SemaphoreType.DMA((3 * n,)), pltpu.SemaphoreType.DMA((3 * n,)),
                   *[pltpu.HBM(s.shape, s.dtype) for s in shards], *[pltpu.HBM(l.shape, l.dtype) for l in lands],
                   jax.ShapeDtypeStruct((SUBLANES, LANES), F32)),
        in_specs=[_HBM] * (2 * n) + [pl.BlockSpec(memory_space=pl.ANY)],
        out_specs=(_SEM, _SEM, *[_HBM] * (2 * n), pl.BlockSpec(memory_space=pltpu.VMEM)),
        input_output_aliases={i: 2 + i for i in range(2 * n)},
        name=name,
        compiler_params=pltpu.CompilerParams(has_side_effects=_EFFECT),
    )(*[pltpu.with_memory_space_constraint(a, pltpu.HBM) for a in list(shards) + lands], after)
    return res[0], res[1], res[2:2 + n], res[2 + n:2 + 2 * n], res[-1]


def _gather_wait(w, shard, land, send_sems, recv_sems, after, name):
    def body(s_ref, land_ref, send_sems, recv_sems, after_ref, s_out, land_out, stage):
        x, y, _ = _me()
        pltpu.sync_copy(s_ref, stage)
        pltpu.sync_copy(stage, land_out.at[2 * x + y])
        for j in range(3):
            cp = _gather_copy(w, j, s_ref, land_ref, send_sems, recv_sems)
            cp.wait_send()
            cp.wait_recv()

    return pl.pallas_call(
        body,
        out_shape=(pltpu.HBM(shard.shape, shard.dtype), pltpu.HBM(land.shape, land.dtype)),
        in_specs=(_HBM, _HBM, _SEM, _SEM, pl.BlockSpec(memory_space=pl.ANY)),
        out_specs=(_HBM, _HBM),
        input_output_aliases={0: 0, 1: 1},
        scratch_shapes=[pltpu.VMEM(shard.shape, shard.dtype)],
        name=name,
        compiler_params=pltpu.CompilerParams(has_side_effects=_EFFECT, vmem_limit_bytes=VMEM_LIMIT),
    )(shard, land, send_sems, recv_sems, after)[1]


def _piece_shape(shape, kind):
    k, nn = shape
    return (k // 2, nn // N_CHIPS) if kind == "col" else (k // N_CHIPS // 2, nn)


def _piece_of(g_ref, kind, tq, tc):
    pr, pc = _piece_shape(g_ref.shape, kind)
    if kind == "col":
        return g_ref.at[pl.ds(tc * pr, pr), pl.ds(tq * pc, pc)]
    return g_ref.at[pl.ds((2 * tq + tc) * pr, pr), :]


def _scatter_copy(r, kind, g_ref, land_ref, send_sems, recv_sems):
    x, y, c = _me()
    tx, ty, tc = (x + ((r >> 2) & 1)) % 2, (y + ((r >> 1) & 1)) % 2, (c + (r & 1)) % 2
    return pltpu.make_async_remote_copy(
        src_ref=_piece_of(g_ref, kind, 2 * tx + ty, tc), dst_ref=land_ref.at[4 * x + 2 * y + c],
        send_sem=send_sems.at[r], recv_sem=recv_sems.at[r], device_id=(tx, ty, tc), device_id_type=MESH)


def _scatter_start(g, kind, name):
    piece = _piece_shape(g.shape, kind)
    land = lax.empty((N_DEV,) + piece, g.dtype)

    def body(g_ref, land_ref, send_sems, recv_sems, g_out, land_out, stage):
        x, y, c = _me()
        for r in range(1, N_DEV):
            _scatter_copy(r, kind, g_ref, land_ref, send_sems, recv_sems).start()
        pltpu.sync_copy(_piece_of(g_ref, kind, 2 * x + y, c), stage)
        pltpu.sync_copy(stage, land_out.at[4 * x + 2 * y + c])

    return pl.pallas_call(
        body,
        out_shape=(pltpu.SemaphoreType.DMA((N_DEV,)), pltpu.SemaphoreType.DMA((N_DEV,)),
                   pltpu.HBM(g.shape, g.dtype), pltpu.HBM(land.shape, land.dtype)),
        in_specs=[_HBM, _HBM],
        out_specs=(_SEM, _SEM, _HBM, _HBM),
        input_output_aliases={0: 2, 1: 3},
        scratch_shapes=[pltpu.VMEM(piece, g.dtype)],
        name=name,
        compiler_params=pltpu.CompilerParams(has_side_effects=_EFFECT, vmem_limit_bytes=VMEM_LIMIT),
    )(pltpu.with_memory_space_constraint(g, pltpu.HBM), pltpu.with_memory_space_constraint(land, pltpu.HBM))


def _scatter_wait(send_sems, recv_sems, g, land, kind, after, name):
    def body(g_ref, land_ref, send_sems, recv_sems, after_ref, g_out, land_out):
        for r in range(1, N_DEV):
            cp = _scatter_copy(r, kind, g_ref, land_ref, send_sems, recv_sems)
            cp.wait_send()
            cp.wait_recv()

    return pl.pallas_call(
        body,
        out_shape=(pltpu.HBM(g.shape, g.dtype), pltpu.HBM(land.shape, land.dtype)),
        in_specs=(_HBM, _HBM, _SEM, _SEM, pl.BlockSpec(memory_space=pl.ANY)),
        out_specs=(_HBM, _HBM),
        input_output_aliases={0: 0, 1: 1},
        name=name,
        compiler_params=pltpu.CompilerParams(has_side_effects=_EFFECT),
    )(g, land, send_sems, recv_sems, after)[1]


def _swap_halves(halves, name):
    n = len(halves)

    def body(*refs):
        in_refs, out_refs = refs[:n], refs[n:2 * n]
        send_sems, recv_sems, local_sems = refs[2 * n:]
        x, y, c = _me()
        cps = []
        for w in range(n):
            lc = pltpu.make_async_copy(in_refs[w], out_refs[w].at[c], local_sems.at[w])
            lc.start()
            rc = pltpu.make_async_remote_copy(
                src_ref=in_refs[w], dst_ref=out_refs[w].at[c], send_sem=send_sems.at[w], recv_sem=recv_sems.at[w],
                device_id=(x, y, 1 - c), device_id_type=MESH)
            rc.start()
            cps.append((lc, rc))
        for lc, rc in cps:
            rc.wait_recv()
        for lc, rc in cps:
            rc.wait_send()
            lc.wait()

    vmem = pl.BlockSpec(memory_space=pltpu.VMEM)
    return pl.pallas_call(
        body,
        out_shape=[jax.ShapeDtypeStruct((2,) + h.shape, h.dtype) for h in halves],
        in_specs=[vmem] * n,
        out_specs=[vmem] * n,
        scratch_shapes=[pltpu.SemaphoreType.DMA((n,)), pltpu.SemaphoreType.DMA((n,)), pltpu.SemaphoreType.DMA((n,))],
        name=name,
        compiler_params=pltpu.CompilerParams(vmem_limit_bytes=VMEM_LIMIT),
    )(*halves)


def _to_streams(a, dil):
    if dil == 1:
        return a
    s, c = a.shape
    return a.reshape(s // dil, dil, c).transpose(1, 0, 2).reshape(s, c)


def _from_streams(a, dil):
    if dil == 1:
        return a
    s, c = a.shape
    return a.reshape(dil, s // dil, c).transpose(1, 0, 2).reshape(s, c)


def _mm_tiles(s):
    return min(s, 1024)


def _local_step(x0, target, mvec, ln_g, ln_b, small, fetch, emit, start):
    s, d = x0.shape
    tm = _mm_tiles(s)
    row = lambda v: v.reshape(1, -1)
    shift = [row(mvec[i, :d]) for i in range(4)]
    scale = [row(mvec[i, d:2 * d]) for i in range(4)]
    gate = [row(1.0 + mvec[i, 2 * d:]) for i in range(4)]
    lg = [row(ln_g[i]) for i in range(4)]
    lb = [row(ln_b[i]) for i in range(4)]
    mm = functools.partial(_mm, tm=tm)
    mm_w = functools.partial(_mm, tm=1024, tk=min(s, 2048), mode="tn")

    xs, ys, big = [x0], [], {}
    h0 = _mod(x0, scale[0], shift[0], start, "mod0")
    big["a_w_in"] = fetch("a_w_in", h0)
    uvpre = mm(h0, big["a_w_in"], mode="nn", name="a_in", outs=[F32], tn=512, tk=1024,
               epi=lambda r, bias: [r + bias], extras=[("row", small["a_b_in"])])
    gated = _spatial_fwd(uvpre, small["a_vn_g"], small["a_vn_b"], small["wc"], small["bias_full"], "a_spatial")
    big["a_w_out"] = fetch("a_w_out", gated)
    ys.append(mm(gated, big["a_w_out"], mode="nn", name="a_out", outs=[F32], tn=1024, tk=1024))
    x1, h1 = _resid_ln(xs[0], ys[0], gate[0], lg[0], lb[0], (scale[1], shift[1]), "ln0")
    xs.append(x1)
    relu2 = lambda r: [jnp.square(jnp.maximum(r, 0.0))]
    big["up0"] = fetch("up0", h1)
    r0 = mm(h1, big["up0"], mode="nn", name="up0", outs=[MXU_DTYPE], tn=1024, tk=1024, epi=relu2)
    big["down0"] = fetch("down0", r0)
    ys.append(mm(r0, big["down0"], mode="nn", name="down0", outs=[F32], tn=1024, tk=2048))
    x2, h2 = _resid_ln(xs[1], ys[1], gate[1], lg[1], lb[1], (scale[2], shift[2]), "ln1")
    xs.append(x2)
    hg, qkvs, o_g, l_g, l_streams = [], [], [], [], []
    big["b_w_qkv"] = fetch("b_w_qkv", h2)
    for g, (_, dil) in enumerate(B_PATTERNS):
        hp = _to_streams(h2, dil)
        qkv = mm(hp, big["b_w_qkv"], mode="nn", name=f"qkv{g}", outs=[MXU_DTYPE], tn=768, tk=1024, b_col0=g * 3 * d, n_out=3 * d)
        og, lgv = _attn_fwd(qkv, small["slopes"], dil, f"attn_fwd{g}")
        hg.append(hp)
        qkvs.append(qkv)
        o_g.append(_from_streams(og, dil))
        l_g.append(_from_streams(lgv, dil))
        l_streams.append(lgv)
    o_mix = _combine_fwd(o_g, l_g, "combine")
    big["b_w_out"] = fetch("b_w_out", o_mix)
    ys.append(mm(o_mix, big["b_w_out"], mode="nn", name="b_out", outs=[F32], tn=1024, tk=1024))
    x3, h3 = _resid_ln(xs[2], ys[2], gate[2], lg[2], lb[2], (scale[3], shift[3]), "ln2")
    xs.append(x3)
    big["up1"] = fetch("up1", h3)
    r1 = mm(h3, big["up1"], mode="nn", name="up1", outs=[MXU_DTYPE], tn=1024, tk=1024, epi=relu2)
    big["down1"] = fetch("down1", r1)
    ys.append(mm(r1, big["down1"], mode="nn", name="down1", outs=[F32], tn=1024, tk=2048))

    gb, red_ln, red_mod = {}, [None] * 4, [None] * 4

    def mlp_bwd(i, h, r, dyy):
        gb[f"down{i}"] = mm_w(r, dyy, name=f"g_down{i}", outs=[MXU_DTYPE], tn=1024)
        da = mm(dyy, big[f"down{i}"], mode="nt", name=f"d_down{i}", outs=[MXU_DTYPE], tn=1024, tk=1024,
                after=emit(f"down{i}", gb[f"down{i}"]),
                epi=lambda acc, rv: [acc * (2.0 * jnp.sqrt(rv.astype(F32)))], extras=[("full", r)])
        gb[f"up{i}"] = mm_w(h, da, name=f"g_up{i}", outs=[MXU_DTYPE], tn=1024)
        return [mm(da, big[f"up{i}"], mode="nt", name=f"d_up{i}", outs=[F32], tn=1024, tk=1024, after=emit(f"up{i}", gb[f"up{i}"]))]

    def join(sub, dxr, dhs, after=None):
        res = _mod_ln_bwd(dxr, dhs, xs[sub], scale[sub], xs[sub - 1], ys[sub - 1], gate[sub - 1], lg[sub - 1],
                          f"mod_ln_bwd{sub}", after=after)
        red_mod[sub], red_ln[sub - 1] = res[2], res[3]
        return res[0], res[1]

    loss, dxr, dyy, red_ln[3] = _last_ln_loss_bwd(xs[3], ys[3], gate[3], lg[3], lb[3], target, "ln3_loss_bwd")
    dxr, dyy = join(3, dxr, mlp_bwd(1, h3, r1, dyy))
    gb["b_w_out"] = mm_w(o_mix, dyy, name="g_b_out", outs=[MXU_DTYPE], tn=1024, tk=1024)
    do = mm(dyy, big["b_w_out"], mode="nt", name="d_b_out", outs=[F32], tn=1024, tk=1024, after=emit("b_w_out", gb["b_w_out"]))
    parts = _combine_bwd(do, o_mix, l_g, "combine_bwd")
    dhs, gq = [], []
    for g, (_, dil) in enumerate(B_PATTERNS):
        do_g, dd_g = _to_streams(parts[g][0], dil), _to_streams(parts[g][1], dil)
        dqkv = _attn_bwd(qkvs[g], do_g, l_streams[g], dd_g, small["slopes"], dil, f"attn_bwd{g}")
        gq.append(mm_w(hg[g], dqkv, name=f"g_qkv{g}", outs=[MXU_DTYPE], tn=1024))
        dh = mm(dqkv, big["b_w_qkv"], mode="nt", name=f"d_qkv{g}", outs=[F32], tn=1024, tk=768, b_col0=g * 3 * d)
        dhs.append(_from_streams(dh, dil))
    gb["b_w_qkv"] = jnp.concatenate(gq, axis=1)
    dxr, dyy = join(2, dxr, dhs, after=emit("b_w_qkv", gb["b_w_qkv"]))
    dxr, dyy = join(1, dxr, mlp_bwd(0, h1, r0, dyy))
    gb["a_w_out"] = mm_w(gated, dyy, name="g_a_out", outs=[MXU_DTYPE], tn=1024)
    dgated = mm(dyy, big["a_w_out"], mode="nt", name="d_a_out", outs=[F32], tn=1024, tk=1024, after=emit("a_w_out", gb["a_w_out"]))
    duv, dws, dbias, dbin, dvg, dvb = _spatial_bwd(uvpre, dgated, small["a_vn_g"], small["a_vn_b"], small["wc"],
                                                   small["wct"], small["bias_full"], "a_spatial_bwd")
    gb["a_w_in"] = mm_w(h0, duv, name="g_a_in", outs=[MXU_DTYPE], tn=1024)
    dh = mm(duv, big["a_w_in"], mode="nt", name="d_a_in", outs=[F32], tn=1024, tk=512, after=emit("a_w_in", gb["a_w_in"]))
    dx, red_mod[0] = _mod_bwd(dxr, [dh], xs[0], scale[0], "mod_bwd0")
    dm = [jnp.concatenate([red_mod[i][0], red_mod[i][1], red_ln[i][2]]) for i in range(4)]
    dlg, dlb = [red_ln[i][0] for i in range(4)], [red_ln[i][1] for i in range(4)]

    tril = jnp.tril(jnp.ones((CHUNK, CHUNK), bool))
    gsmall = {
        "a_b_in": dbin.reshape(-1), "a_vn_g": dvg.reshape(-1), "a_vn_b": dvb.reshape(-1),
        "a_w_s": jnp.where(tril, dws, 0.0).reshape(-1),
        "a_b_s": dbias.reshape(CHUNK, A_GROUPS, d // A_GROUPS).sum(-1).T.reshape(-1),
    }
    return loss, dx, gb, jnp.stack(dm), jnp.stack(dlg), jnp.stack(dlb), gsmall


BIG = ("a_w_in", "a_w_out", "up0", "down0", "b_w_qkv", "b_w_out", "up1", "down1")
BIG_KIND = {"a_w_in": "col", "a_w_out": "row", "b_w_qkv": "col", "b_w_out": "row",
            "up0": "col", "up1": "col", "down0": "row", "down1": "row"}
SMALL = ("a_b_in", "a_vn_g", "a_vn_b", "a_b_s", "a_w_s")


def kernel(x, c, ada_w, ada_b, ln_g, ln_b, a_w_in, a_b_in, a_vn_g, a_vn_b, a_w_s, a_b_s, a_w_out, b_w_qkv, b_w_out, mlp_w_up, mlp_w_down, loss_target, m_ada_w, m_ada_b, m_ln_g, m_ln_b, m_a_w_in, m_a_b_in, m_a_vn_g, m_a_vn_b, m_a_w_s, m_a_b_s, m_a_w_out, m_b_w_qkv, m_b_w_out, m_mlp_w_up, m_mlp_w_down, v_ada_w, v_ada_b, v_ln_g, v_ln_b, v_a_w_in, v_a_b_in, v_a_vn_g, v_a_vn_b, v_a_w_s, v_a_b_s, v_a_w_out, v_b_w_qkv, v_b_w_out, v_mlp_w_up, v_mlp_w_down):
    s, d = x.shape[1], x.shape[2]
    xi, yi, ci = _me()
    q = 2 * xi + yi
    dev = 2 * q + ci
    nsub = 2 * DEPTH
    cs = ada_w.shape[-1]
    ls = ln_g.shape[-1]

    pack = jnp.concatenate([c.reshape(-1), ln_g.reshape(-1), ln_b.reshape(-1)]).reshape(-1, LANES)
    got = _all_gather_small(pack, "gather_small").reshape(N_DEV, -1)
    c_all = got[:, :d]
    per_chip = got[0::2]
    ln_g_full = per_chip[:, d:d + nsub * ls].reshape(N_CHIPS, nsub, ls).transpose(1, 0, 2).reshape(nsub, d)
    ln_b_full = per_chip[:, d + nsub * ls:].reshape(N_CHIPS, nsub, ls).transpose(1, 0, 2).reshape(nsub, d)
    m_part = _ada_fwd(c_all, ada_w.reshape(nsub, d, cs), ada_b.reshape(nsub, 1, cs), "ada_fwd")
    m_all = _all_gather_small(m_part.reshape(-1, LANES), "gather_mod").reshape(N_DEV, nsub, N_DEV, cs)
    m_mine = lax.dynamic_index_in_dim(m_all[0::2], dev, axis=2, keepdims=False)
    mvec = m_mine.transpose(1, 0, 2).reshape(nsub, 3 * d)

    shards = {
        "a_w_in": a_w_in[0], "a_w_out": a_w_out[0], "b_w_qkv": b_w_qkv[0], "b_w_out": b_w_out[0],
        "up0": mlp_w_up[0], "up1": mlp_w_up[1], "down0": mlp_w_down[0], "down1": mlp_w_down[1],
    }
    send_sems, recv_sems, shard_thru, lands, token = _gather_start([shards[k].astype(MXU_DTYPE) for k in BIG], mvec, "gather_start")

    def fetch(k, after):
        w = BIG.index(k)
        gw = _gather_wait(w, shard_thru[w], lands[w], send_sems, recv_sems, after, f"gather_wait_{k}")
        return gw if BIG_KIND[k] == "col" else gw.reshape(1, -1, gw.shape[-1])

    scattering = {}

    def emit(k, g):
        scattering[k] = _scatter_start(g, BIG_KIND[k], f"scatter_start_{k}")
        return scattering[k][2]

    tril = jnp.tril(jnp.ones((CHUNK, CHUNK), bool))
    wc = jnp.where(tril, a_w_s[0], 0.0).astype(MXU_DTYPE)
    heads = jnp.arange(1, B_HEADS + 1, dtype=F32)
    small = {
        "a_b_in": a_b_in, "a_vn_g": a_vn_g, "a_vn_b": a_vn_b,
        "wc": wc, "wct": wc.transpose(0, 2, 1),
        "bias_full": jnp.repeat(a_b_s[0].T, d // A_GROUPS, axis=1),
        "slopes": jnp.exp2(-8.0 * heads / B_HEADS),
    }

    loss_part, grad_x, gb, dm, dlg, dlb, gsmall = _local_step(x[0], loss_target[0], mvec, ln_g_full, ln_b_full, small, fetch, emit, token)
    loss = lax.psum(loss_part, ("x", "y", "c"))

    weights = dict(ada_w=ada_w, ada_b=ada_b, ln_g=ln_g, ln_b=ln_b, a_w_in=a_w_in, a_b_in=a_b_in, a_vn_g=a_vn_g, a_vn_b=a_vn_b,
                   a_w_s=a_w_s, a_b_s=a_b_s, a_w_out=a_w_out, b_w_qkv=b_w_qkv, b_w_out=b_w_out, mlp_w_up=mlp_w_up, mlp_w_down=mlp_w_down)
    ms = dict(ada_w=m_ada_w, ada_b=m_ada_b, ln_g=m_ln_g, ln_b=m_ln_b, a_w_in=m_a_w_in, a_b_in=m_a_b_in, a_vn_g=m_a_vn_g, a_vn_b=m_a_vn_b,
              a_w_s=m_a_w_s, a_b_s=m_a_b_s, a_w_out=m_a_w_out, b_w_qkv=m_b_w_qkv, b_w_out=m_b_w_out, mlp_w_up=m_mlp_w_up, mlp_w_down=m_mlp_w_down)
    vs = dict(ada_w=v_ada_w, ada_b=v_ada_b, ln_g=v_ln_g, ln_b=v_ln_b, a_w_in=v_a_w_in, a_b_in=v_a_b_in, a_vn_g=v_a_vn_g, a_vn_b=v_a_vn_b,
              a_w_s=v_a_w_s, a_b_s=v_a_b_s, a_w_out=v_a_w_out, b_w_qkv=v_b_w_qkv, b_w_out=v_b_w_out, mlp_w_up=v_mlp_w_up, mlp_w_down=v_mlp_w_down)
    grads, updates = {}, {}

    def update(k):
        updates[k] = _adamw(weights[k], grads[k], ms[k], vs[k], f"adamw_{k}")
        return updates[k][0]

    pack_b = jnp.concatenate([dm.reshape(-1), dlg.reshape(-1), dlb.reshape(-1)] + [gsmall[k] for k in SMALL])
    n_small = pack_b.shape[0]
    pack_b = jnp.pad(pack_b, (0, -n_small % (ROW_TILE * LANES)))
    got_b = _all_gather_small(pack_b.reshape(-1, LANES), "gather_small_grads").reshape(N_DEV, -1, LANES)
    tot = _sum_slots(got_b, "sum_small").reshape(-1)
    o = 0
    dm_tot = tot[o:o + nsub * 3 * d].reshape(nsub, 3 * d); o += nsub * 3 * d
    dlg_tot = tot[o:o + nsub * d].reshape(nsub, d); o += nsub * d
    dlb_tot = tot[o:o + nsub * d].reshape(nsub, d); o += nsub * d
    g_small = {}
    for k, ref in zip(SMALL, (a_b_in, a_vn_g, a_vn_b, a_b_s, a_w_s)):
        g_small[k] = tot[o:o + ref.size].reshape(ref.shape); o += ref.size
    assert o == n_small
    dm_all = got_b.reshape(N_DEV, -1)[:, :nsub * 3 * d].reshape(N_DEV, nsub, 3 * d)
    dm_cols = lax.dynamic_slice_in_dim(dm_all, q * cs, cs, axis=2).transpose(1, 0, 2)

    grads.update({
        "ada_w": _ada_bwd(c_all.T, dm_cols, "ada_bwd").reshape(ada_w.shape),
        "ada_b": lax.dynamic_slice_in_dim(dm_tot, q * cs, cs, axis=1).reshape(ada_b.shape),
        "ln_g": lax.dynamic_slice_in_dim(dlg_tot, q * ls, ls, axis=1).reshape(ln_g.shape),
        "ln_b": lax.dynamic_slice_in_dim(dlb_tot, q * ls, ls, axis=1).reshape(ln_b.shape),
        **g_small,
    })
    for k in ("ada_b", "ln_g", "ln_b") + SMALL:
        update(k)
    done = update("ada_w")

    gfull = {}
    for group in (("down1", "up1", "b_w_out", "b_w_qkv"), ("down0", "up0", "a_w_out", "a_w_in")):
        bufs = [_scatter_wait(*scattering[k], BIG_KIND[k], done, f"scatter_wait_{k}") for k in group]
        halves = [_sum_slots(b, f"sum_{k}") for k, b in zip(group, bufs)]
        fulls = _swap_halves(halves, f"swap_halves_{group[0]}")
        gfull.update({k: f.reshape(-1, f.shape[-1]) for k, f in zip(group, fulls)})
        if group[0] == "down1":
            grads["b_w_qkv"], grads["b_w_out"] = gfull["b_w_qkv"][None], gfull["b_w_out"][None]
            update("b_w_out")
            done = update("b_w_qkv")
    grads.update({
        "a_w_in": gfull["a_w_in"][None], "a_w_out": gfull["a_w_out"][None],
        "mlp_w_up": jnp.stack([gfull["up0"], gfull["up1"]]), "mlp_w_down": jnp.stack([gfull["down0"], gfull["down1"]]),
    })
    for k in ("a_w_in", "a_w_out", "mlp_w_up", "mlp_w_down"):
        update(k)
    names = list(weights)
    return (loss, grad_x[None], *[grads[k] for k in names], *[updates[k][0] for k in names],
            *[updates[k][1] for k in names], *[updates[k][2] for k in names])
```

```python
import functools
import math

import jax
import jax.numpy as jnp
from jax import lax
from jax.experimental import pallas as pl
from jax.experimental.pallas import tpu as pltpu

F32 = jnp.float32
MXU_DTYPE = jnp.bfloat16

DEPTH = 2
CHUNK = 128
A_GROUPS = 16
B_HEADS = 16
HEAD_DIM = 64
B_PATTERNS = ((128, 1), (512, 4), (2048, 16))
SPAN = 128
ALPHA = (2 * DEPTH) ** 0.25
LN_EPS = 1e-5
NEG = -1e30
ATT_SCALE = HEAD_DIM ** -0.5
ADAM_LR, ADAM_B1, ADAM_B2, ADAM_EPS, ADAM_WD, ADAM_STEP = 0.001, 0.9, 0.999, 1e-08, 0.01, 10

N_CHIPS = 4
N_DEV = 8
LANES = 128
SUBLANES = 8
VMEM_LIMIT = 52 * 1024 * 1024
ROW_TILE = 256
MESH = pl.DeviceIdType.MESH


def _cparams(sem):
    return pltpu.CompilerParams(dimension_semantics=sem, vmem_limit_bytes=VMEM_LIMIT)


def _fold8(v):
    r, c = v.shape
    return jnp.sum(v.reshape(r // SUBLANES, SUBLANES, c), axis=0)


def _gelu(x):
    c = math.sqrt(2.0 / math.pi)
    return 0.5 * x * (1.0 + jnp.tanh(c * (x + 0.044715 * (x * x * x))))


def _gelu_grad(x):
    c = math.sqrt(2.0 / math.pi)
    t = jnp.tanh(c * (x + 0.044715 * (x * x * x)))
    return 0.5 * (1.0 + t) + 0.5 * x * (1.0 - t * t) * c * (1.0 + 3.0 * 0.044715 * x * x)


def _dot(a, b, dims):
    return lax.dot_general(a.astype(MXU_DTYPE), b.astype(MXU_DTYPE), (dims, ((), ())), preferred_element_type=F32)


def _dot_nn(a, b):
    return _dot(a, b, ((1,), (0,)))


def _dot_nt(a, b):
    return _dot(a, b, ((1,), (1,)))


def _dot_tn(a, b):
    return _dot(a, b, ((0,), (0,)))


def _mm(a, b, *, mode, name, outs, tm, tn, tk, epi=None, extras=(), b_col0=0, n_out=None, after=None):
    if mode == "nn":
        m, kdim = a.shape
        p, kb, ns = b.shape
        assert kb == kdim and ns % tn == 0 and b_col0 % tn == 0
        n = n_out if n_out is not None else p * ns
        npt, j0 = ns // tn, b_col0 // tn
        a_spec = pl.BlockSpec((tm, tk), lambda i, j, k: (i, k))
        b_spec = pl.BlockSpec((None, tk, tn), lambda i, j, k: ((j + j0) // npt, k, (j + j0) % npt))
        dot = _dot_nn
    elif mode == "nt":
        m, kdim = a.shape
        p, n, ns = b.shape
        assert ns % tk == 0 and b_col0 % tk == 0
        npt, j0 = ns // tk, b_col0 // tk
        a_spec = pl.BlockSpec((tm, tk), lambda i, j, k: (i, k))
        b_spec = pl.BlockSpec((None, tn, tk), lambda i, j, k: ((k + j0) // npt, j, (k + j0) % npt))
        dot = _dot_nt
    else:
        kdim, m = a.shape
        kb, n = b.shape
        assert kb == kdim
        a_spec = pl.BlockSpec((tk, tm), lambda i, j, k: (k, i))
        b_spec = pl.BlockSpec((tk, tn), lambda i, j, k: (k, j))
        dot = _dot_tn
    assert m % tm == 0 and n % tn == 0 and kdim % tk == 0, (name, m, n, kdim, tm, tn, tk)
    nk = kdim // tk
    ex_specs, ex_arrays = [], []
    for kind, arr in extras:
        if kind == "row":
            ex_specs.append(pl.BlockSpec((1, tn), lambda i, j, k: (0, j)))
        else:
            ex_specs.append(pl.BlockSpec((tm, tn), lambda i, j, k: (i, j)))
        ex_arrays.append(arr)
    n_ex, n_o = len(ex_arrays), len(outs)
    n_dep = 0 if after is None else 1
    deps = [] if after is None else [after]

    def body(a_ref, b_ref, *rest):
        ex_refs, o_refs = rest[:n_ex], rest[n_ex + n_dep:n_ex + n_dep + n_o]
        k = pl.program_id(2)

        def finish(r):
            vals = epi(r, *[e[...] for e in ex_refs]) if epi is not None else [r]
            for o, v in zip(o_refs, vals):
                o[...] = v.astype(o.dtype)

        if nk == 1:
            finish(dot(a_ref[...], b_ref[...]))
            return
        acc = rest[n_ex + n_dep + n_o]

        @pl.when(k == 0)
        def _():
            acc[...] = dot(a_ref[...], b_ref[...])

        @pl.when((k > 0) & (k < nk - 1))
        def _():
            acc[...] += dot(a_ref[...], b_ref[...])

        @pl.when(k == nk - 1)
        def _():
            finish(acc[...] + dot(a_ref[...], b_ref[...]))

    res = pl.pallas_call(
        body,
        grid=(m // tm, n // tn, nk),
        in_specs=[a_spec, b_spec] + ex_specs + [pl.BlockSpec(memory_space=pl.ANY)] * n_dep,
        out_specs=[pl.BlockSpec((tm, tn), lambda i, j, k: (i, j)) for _ in outs],
        out_shape=[jax.ShapeDtypeStruct((m, n), dt) for dt in outs],
        scratch_shapes=[pltpu.VMEM((tm, tn), F32)] if nk > 1 else [],
        name=name,
        compiler_params=_cparams(("parallel", "parallel", "arbitrary")),
    )(a, b, *ex_arrays, *deps)
    return res if len(outs) > 1 else res[0]


def _rows(body, n_rows, tr, ins, outs, name, scratch=()):
    def spec(kind, shape):
        if kind == "blk":
            return pl.BlockSpec((tr,) + tuple(shape[1:]), lambda i: (i,) + (0,) * (len(shape) - 1))
        if kind == "dep":
            return pl.BlockSpec(memory_space=pl.ANY)
        return pl.BlockSpec(tuple(shape), lambda i: (0,) * len(shape))

    return pl.pallas_call(
        body,
        grid=(n_rows // tr,),
        in_specs=[spec(k, a.shape) for k, a in ins],
        out_specs=[spec(k, s) for k, s, _ in outs],
        out_shape=[jax.ShapeDtypeStruct(tuple(s), d) for _, s, d in outs],
        scratch_shapes=list(scratch),
        name=name,
        compiler_params=_cparams(("arbitrary",)),
    )(*[a for _, a in ins])


def _ln_stats(z):
    mu = jnp.mean(z, axis=-1, keepdims=True)
    zc = z - mu
    var = jnp.mean(zc * zc, axis=-1, keepdims=True)
    rstd = lax.rsqrt(var + LN_EPS)
    return zc * rstd, rstd


def _mod(x, scale, shift, after, name):
    s, d = x.shape

    def body(x_ref, sc_ref, sh_ref, dep_ref, h_ref):
        h_ref[...] = (x_ref[...] * (1.0 + sc_ref[...]) + sh_ref[...]).astype(h_ref.dtype)

    return _rows(body, s, ROW_TILE, [("blk", x), ("all", scale), ("all", shift), ("dep", after)], [("blk", (s, d), MXU_DTYPE)], name)[0]


def _resid_ln(x, y, gate, g, b, nxt, name):
    s, d = x.shape

    def body(x_ref, y_ref, gate_ref, g_ref, b_ref, sc_ref, sh_ref, xn_ref, h_ref):
        z = ALPHA * x_ref[...] + gate_ref[...] * y_ref[...]
        xhat, _ = _ln_stats(z)
        xn = xhat * g_ref[...] + b_ref[...]
        xn_ref[...] = xn
        h_ref[...] = (xn * (1.0 + sc_ref[...]) + sh_ref[...]).astype(h_ref.dtype)

    return _rows(body, s, ROW_TILE,
                 [("blk", x), ("blk", y), ("all", gate), ("all", g), ("all", b), ("all", nxt[0]), ("all", nxt[1])],
                 [("blk", (s, d), F32), ("blk", (s, d), MXU_DTYPE)], name)


def _mod_bwd(dxr, dhs, x, scale, name, after=None):
    s, d = x.shape
    n_dh = len(dhs)
    n_dep = 0 if after is None else 1

    def body(dxr_ref, *rest):
        dh_refs = rest[:n_dh]
        x_ref, sc_ref, dx_ref, red_ref, a_sh, a_sc = rest[n_dh:n_dh + 2] + rest[n_dh + 2 + n_dep:]
        i = pl.program_id(0)

        @pl.when(i == 0)
        def _():
            a_sh[...] = jnp.zeros_like(a_sh)
            a_sc[...] = jnp.zeros_like(a_sc)

        dh = dh_refs[0][...]
        for r in dh_refs[1:]:
            dh = dh + r[...]
        dx_ref[...] = dxr_ref[...] + dh * (1.0 + sc_ref[...])
        a_sh[...] += _fold8(dh)
        a_sc[...] += _fold8(dh * x_ref[...])

        @pl.when(i == pl.num_programs(0) - 1)
        def _():
            red_ref[...] = jnp.zeros_like(red_ref)
            red_ref[0:1, :] = jnp.sum(a_sh[...], axis=0, keepdims=True)
            red_ref[1:2, :] = jnp.sum(a_sc[...], axis=0, keepdims=True)

    return _rows(body, s, ROW_TILE, [("blk", dxr)] + [("blk", h) for h in dhs] + [("blk", x), ("all", scale)] + [("dep", after)] * n_dep,
                 [("blk", (s, d), F32), ("all", (SUBLANES, d), F32)], name,
                 scratch=[pltpu.VMEM((SUBLANES, d), F32)] * 2)


def _last_ln_loss_bwd(x, y, gate, g, b, target, name):
    s, d = x.shape

    def body(x_ref, y_ref, gate_ref, g_ref, b_ref, t_ref, l_ref, dxr_ref, dyy_ref, red_ref, a_l, a_g, a_b, a_gate):
        i = pl.program_id(0)

        @pl.when(i == 0)
        def _():
            for a in (a_l, a_g, a_b, a_gate):
                a[...] = jnp.zeros_like(a)

        yv = y_ref[...]
        z = ALPHA * x_ref[...] + gate_ref[...] * yv
        xhat, rstd = _ln_stats(z)
        e = xhat * g_ref[...] + b_ref[...] - t_ref[...]
        a_l[...] += _fold8(e * e)
        dxo_v = e * (1.0 / d)
        dxh = dxo_v * g_ref[...]
        dz = rstd * (dxh - jnp.mean(dxh, axis=-1, keepdims=True) - xhat * jnp.mean(dxh * xhat, axis=-1, keepdims=True))
        dxr_ref[...] = ALPHA * dz
        dyy_ref[...] = (gate_ref[...] * dz).astype(dyy_ref.dtype)
        a_g[...] += _fold8(dxo_v * xhat)
        a_b[...] += _fold8(dxo_v)
        a_gate[...] += _fold8(dz * yv)

        @pl.when(i == pl.num_programs(0) - 1)
        def _():
            l_ref[...] = jnp.full(l_ref.shape, 0.5 / d, F32) * jnp.sum(a_l[...])
            red_ref[...] = jnp.zeros_like(red_ref)
            red_ref[0:1, :] = jnp.sum(a_g[...], axis=0, keepdims=True)
            red_ref[1:2, :] = jnp.sum(a_b[...], axis=0, keepdims=True)
            red_ref[2:3, :] = jnp.sum(a_gate[...], axis=0, keepdims=True)

    l, dxr, dyy, red = _rows(
        body, s, ROW_TILE, [("blk", x), ("blk", y), ("all", gate), ("all", g), ("all", b), ("blk", target)],
        [("all", (SUBLANES, LANES), F32), ("blk", (s, d), F32), ("blk", (s, d), MXU_DTYPE), ("all", (SUBLANES, d), F32)], name,
        scratch=[pltpu.VMEM((SUBLANES, d), F32)] * 4)
    return l[0, 0], dxr, dyy, red


def _mod_ln_bwd(dxr, dhs, x, scale, x_in, y, gate, g, name, after=None):
    s, d = x.shape
    n_dh = len(dhs)
    n_dep = 0 if after is None else 1

    def body(dxr_ref, *rest):
        dh_refs = rest[:n_dh]
        x_ref, sc_ref, xin_ref, y_ref, gate_ref, g_ref = rest[n_dh:n_dh + 6]
        dxr_out, dyy_ref, red_mod, red_ln, a_sh, a_sc, a_g, a_b, a_gate = rest[n_dh + 6 + n_dep:]
        i = pl.program_id(0)

        @pl.when(i == 0)
        def _():
            for a in (a_sh, a_sc, a_g, a_b, a_gate):
                a[...] = jnp.zeros_like(a)

        dh = dh_refs[0][...]
        for r in dh_refs[1:]:
            dh = dh + r[...]
        xv = x_ref[...]
        dxo_v = dxr_ref[...] + dh * (1.0 + sc_ref[...])
        a_sh[...] += _fold8(dh)
        a_sc[...] += _fold8(dh * xv)
        yv = y_ref[...]
        z = ALPHA * xin_ref[...] + gate_ref[...] * yv
        xhat, rstd = _ln_stats(z)
        dxh = dxo_v * g_ref[...]
        dz = rstd * (dxh - jnp.mean(dxh, axis=-1, keepdims=True) - xhat * jnp.mean(dxh * xhat, axis=-1, keepdims=True))
        dxr_out[...] = ALPHA * dz
        dyy_ref[...] = (gate_ref[...] * dz).astype(dyy_ref.dtype)
        a_g[...] += _fold8(dxo_v * xhat)
        a_b[...] += _fold8(dxo_v)
        a_gate[...] += _fold8(dz * yv)

        @pl.when(i == pl.num_programs(0) - 1)
        def _():
            red_mod[...] = jnp.zeros_like(red_mod)
            red_mod[0:1, :] = jnp.sum(a_sh[...], axis=0, keepdims=True)
            red_mod[1:2, :] = jnp.sum(a_sc[...], axis=0, keepdims=True)
            red_ln[...] = jnp.zeros_like(red_ln)
            red_ln[0:1, :] = jnp.sum(a_g[...], axis=0, keepdims=True)
            red_ln[1:2, :] = jnp.sum(a_b[...], axis=0, keepdims=True)
            red_ln[2:3, :] = jnp.sum(a_gate[...], axis=0, keepdims=True)

    ins = ([("blk", dxr)] + [("blk", h) for h in dhs]
           + [("blk", x), ("all", scale), ("blk", x_in), ("blk", y), ("all", gate), ("all", g)] + [("dep", after)] * n_dep)
    return _rows(body, s, ROW_TILE, ins,
                 [("blk", (s, d), F32), ("blk", (s, d), MXU_DTYPE), ("all", (SUBLANES, d), F32), ("all", (SUBLANES, d), F32)], name,
                 scratch=[pltpu.VMEM((SUBLANES, d), F32)] * 5)


def _left_half(shape):
    return lax.broadcasted_iota(jnp.int32, shape, 1) < (LANES // 2)


def _spatial_z(vn, wc_ref, bias_ref, j):
    vb = vn[:, j * LANES:(j + 1) * LANES]
    z0 = _dot_nn(wc_ref[2 * j], vb)
    z1 = _dot_nn(wc_ref[2 * j + 1], vb)
    return jnp.where(_left_half(z0.shape), z0, z1) + bias_ref[:, j * LANES:(j + 1) * LANES]


def _spatial_fwd(uvpre, vn_g, vn_b, wc, bias_full, name):
    s, d2 = uvpre.shape
    d = d2 // 2

    def body(uv_ref, g_ref, b_ref, wc_ref, bias_ref, out_ref):
        u = _gelu(uv_ref[:, :d])
        v = _gelu(uv_ref[:, d:])
        vh, _ = _ln_stats(v)
        vn = vh * g_ref[...] + b_ref[...]
        for j in range(d // LANES):
            z = _spatial_z(vn, wc_ref, bias_ref, j)
            out_ref[:, j * LANES:(j + 1) * LANES] = (u[:, j * LANES:(j + 1) * LANES] * z).astype(out_ref.dtype)

    return _rows(body, s, CHUNK, [("blk", uvpre), ("all", vn_g), ("all", vn_b), ("all", wc), ("all", bias_full)],
                 [("blk", (s, d), MXU_DTYPE)], name)[0]


def _spatial_bwd(uvpre, dgated, vn_g, vn_b, wc, wct, bias_full, name):
    s, d2 = uvpre.shape
    d = d2 // 2

    def body(uv_ref, dg_ref, g_ref, b_ref, wc_ref, wct_ref, bias_ref,
             duv_ref, dws_ref, dbias_ref, dbin_ref, dvg_ref, dvb_ref, dvn_buf, a_bin, a_vg, a_vb):
        i = pl.program_id(0)

        @pl.when(i == 0)
        def _():
            dws_ref[...] = jnp.zeros_like(dws_ref)
            dbias_ref[...] = jnp.zeros_like(dbias_ref)
            a_bin[...] = jnp.zeros_like(a_bin)
            a_vg[...] = jnp.zeros_like(a_vg)
            a_vb[...] = jnp.zeros_like(a_vb)

        up = uv_ref[:, :d]
        vp = uv_ref[:, d:]
        u = _gelu(up)
        v = _gelu(vp)
        vh, rstd = _ln_stats(v)
        vn = vh * g_ref[...] + b_ref[...]
        dg = dg_ref[...]
        dzz = dg * u
        dbias_ref[...] += dzz
        for j in range(d // LANES):
            cols = slice(j * LANES, (j + 1) * LANES)
            z = _spatial_z(vn, wc_ref, bias_ref, j)
            dup = dg[:, cols] * z * _gelu_grad(up[:, cols])
            duv_ref[:, cols] = dup.astype(duv_ref.dtype)
            a_bin[:, cols] += _fold8(dup)
            dzb = dzz[:, cols]
            left = _left_half(dzb.shape)
            dvn_buf[:, cols] = jnp.where(left, _dot_nn(wct_ref[2 * j], dzb), _dot_nn(wct_ref[2 * j + 1], dzb))
            vb = vn[:, cols]
            dws_ref[2 * j] += _dot_nt(jnp.where(left, dzb, 0.0), vb)
            dws_ref[2 * j + 1] += _dot_nt(jnp.where(left, 0.0, dzb), vb)
        dvn = dvn_buf[...]
        a_vg[...] += _fold8(dvn * vh)
        a_vb[...] += _fold8(dvn)
        dvh = dvn * g_ref[...]
        dv = rstd * (dvh - jnp.mean(dvh, axis=-1, keepdims=True) - vh * jnp.mean(dvh * vh, axis=-1, keepdims=True))
        dvp = dv * _gelu_grad(vp)
        duv_ref[:, d:] = dvp.astype(duv_ref.dtype)
        a_bin[:, d:] += _fold8(dvp)

        @pl.when(i == pl.num_programs(0) - 1)
        def _():
            dbin_ref[...] = jnp.sum(a_bin[...], axis=0, keepdims=True)
            dvg_ref[...] = jnp.sum(a_vg[...], axis=0, keepdims=True)
            dvb_ref[...] = jnp.sum(a_vb[...], axis=0, keepdims=True)

    return _rows(body, s, CHUNK,
                 [("blk", uvpre), ("blk", dgated), ("all", vn_g), ("all", vn_b), ("all", wc), ("all", wct), ("all", bias_full)],
                 [("blk", (s, d2), MXU_DTYPE), ("all", (A_GROUPS, CHUNK, CHUNK), F32), ("all", (CHUNK, d), F32),
                  ("all", (1, d2), F32), ("all", (1, d), F32), ("all", (1, d), F32)], name,
                 scratch=[pltpu.VMEM((CHUNK, d), F32), pltpu.VMEM((SUBLANES, d2), F32),
                          pltpu.VMEM((SUBLANES, d), F32), pltpu.VMEM((SUBLANES, d), F32)])


def _head_mask(v, h):
    lane = lax.broadcasted_iota(jnp.int32, v.shape, 1)
    return jnp.where((lane >= h * HEAD_DIM) & (lane < (h + 1) * HEAD_DIM), v, jnp.zeros_like(v))


def _att_bias(slopes, dil):
    qi = lax.broadcasted_iota(jnp.int32, (SPAN, SPAN), 0)
    ki = lax.broadcasted_iota(jnp.int32, (SPAN, SPAN), 1)
    sl = slopes[:, None, None]
    cur = jnp.where(ki <= qi, -sl * (float(dil) * (qi - ki).astype(F32)), NEG)
    prev = jnp.where(ki >= qi, -sl * (float(dil) * (SPAN + qi - ki).astype(F32)), NEG)
    absent = jnp.full_like(prev, NEG)
    pairs = slopes.shape[0] // 2

    def fwd(pv):
        return jnp.concatenate([cur, pv], axis=2).reshape(pairs, 2 * SPAN, 2 * SPAN)

    def bwd(pv):
        return jnp.concatenate([cur.reshape(pairs, 2 * SPAN, SPAN), pv.reshape(pairs, 2 * SPAN, SPAN)], axis=1)

    return jnp.stack([fwd(absent), fwd(prev)]), jnp.stack([bwd(absent), bwd(prev)])


def _att_specs(s, d, dil, kinds):
    nb = s // (dil * SPAN)

    def rowblk(which, b):
        if which == "prev":
            return jnp.where(b % nb == 0, b, b - 1)
        if which == "next":
            return jnp.where(b % nb == nb - 1, b, b + 1)
        return b

    return [pl.BlockSpec((SPAN, d), functools.partial(lambda b, o, w: (rowblk(w, b), o), o=part, w=which))
            for part, which in kinds]


def _lane_col(v, h):
    return v[:, h * HEAD_DIM:h * HEAD_DIM + 1]


def _attn_fwd(qkv, slopes, dil, name):
    s, d3 = qkv.shape
    d = d3 // 3
    nb = s // (dil * SPAN)
    table, _ = _att_bias(slopes, dil)

    def body(q_ref, kc_ref, kp_ref, vc_ref, vp_ref, tb_ref, o_ref, l_ref):
        left = _left_half((SPAN, LANES))
        for hp in range(d // LANES):
            cols = slice(hp * LANES, (hp + 1) * LANES)
            q = q_ref[:, cols]
            q2 = jnp.concatenate([_head_mask(q, 0), _head_mask(q, 1)], axis=0) * ATT_SCALE
            k2 = jnp.concatenate([kc_ref[:, cols], kp_ref[:, cols]], axis=0)
            v2 = jnp.concatenate([vc_ref[:, cols], vp_ref[:, cols]], axis=0)
            sc = _dot_nt(q2, k2) + tb_ref[hp]
            m = jnp.max(sc, axis=-1, keepdims=True)
            p = jnp.exp(sc - m)
            l = jnp.sum(p, axis=-1, keepdims=True)
            r = _dot_nn(p, v2) * (1.0 / l)
            lse = jnp.broadcast_to(m + jnp.log(l), (2 * SPAN, LANES))
            o_ref[:, cols] = jnp.where(left, r[:SPAN], r[SPAN:])
            l_ref[:, cols] = jnp.where(left, lse[:SPAN], lse[SPAN:])

    specs = _att_specs(s, d, dil, [(0, "cur"), (1, "cur"), (1, "prev"), (2, "cur"), (2, "prev")])
    tbl = pl.BlockSpec((None,) + table.shape[1:], lambda b: (jnp.where(b % nb == 0, 0, 1), 0, 0, 0))
    out_spec = pl.BlockSpec((SPAN, d), lambda b: (b, 0))
    return pl.pallas_call(
        body,
        grid=(s // SPAN,),
        in_specs=specs + [tbl],
        out_specs=[out_spec, out_spec],
        out_shape=[jax.ShapeDtypeStruct((s, d), F32)] * 2,
        name=name,
        compiler_params=_cparams(("parallel",)),
    )(qkv, qkv, qkv, qkv, qkv, table)


def _attn_bwd(qkv, do, lse, dd, slopes, dil, name):
    s, d3 = qkv.shape
    d = d3 // 3
    nb = s // (dil * SPAN)
    _, table = _att_bias(slopes, dil)

    def heads_stacked(cur, nxt):
        return jnp.concatenate([_head_mask(cur, 0), _head_mask(cur, 1), _head_mask(nxt, 0), _head_mask(nxt, 1)], axis=0)

    def cols_stacked(cur, nxt):
        return jnp.concatenate([jnp.broadcast_to(_lane_col(a, h), (SPAN, LANES)) for a in (cur, nxt) for h in range(2)], axis=0)

    def body(k_ref, v_ref, qc_ref, qn_ref, doc_ref, don_ref, lc_ref, ln_ref, ddc_ref, ddn_ref, tb_ref, out_ref, carry):
        b = pl.program_id(0)

        @pl.when(b == 0)
        def _():
            carry[...] = jnp.zeros_like(carry)

        left = _left_half((SPAN, LANES))
        for hp in range(d // LANES):
            cols = slice(hp * LANES, (hp + 1) * LANES)
            k, v = k_ref[:, cols], v_ref[:, cols]
            q4 = heads_stacked(qc_ref[:, cols], qn_ref[:, cols])
            do4 = heads_stacked(doc_ref[:, cols], don_ref[:, cols])
            sc = _dot_nt(q4 * ATT_SCALE, k) + tb_ref[hp]
            p = jnp.exp(sc - cols_stacked(lc_ref[:, cols], ln_ref[:, cols]))
            ds = p * (_dot_nt(do4, v) - cols_stacked(ddc_ref[:, cols], ddn_ref[:, cols]))
            dq4 = _dot_nn(ds, k)
            dq_cur = jnp.where(left, dq4[:SPAN], dq4[SPAN:2 * SPAN]) + carry[:, cols]
            carry[:, cols] = jnp.where(left, dq4[2 * SPAN:3 * SPAN], dq4[3 * SPAN:])
            out_ref[:, cols] = (dq_cur * ATT_SCALE).astype(out_ref.dtype)
            out_ref[:, d + hp * LANES:d + (hp + 1) * LANES] = (_dot_tn(ds, q4) * ATT_SCALE).astype(out_ref.dtype)
            out_ref[:, 2 * d + hp * LANES:2 * d + (hp + 1) * LANES] = _dot_tn(p, do4).astype(out_ref.dtype)

    qkv_specs = _att_specs(s, d, dil, [(1, "cur"), (2, "cur"), (0, "cur"), (0, "next")])
    pair = _att_specs(s, d, dil, [(0, "cur"), (0, "next")])
    tbl = pl.BlockSpec((None,) + table.shape[1:], lambda b: (jnp.where(b % nb == nb - 1, 0, 1), 0, 0, 0))
    return pl.pallas_call(
        body,
        grid=(s // SPAN,),
        in_specs=qkv_specs + pair + pair + pair + [tbl],
        out_specs=pl.BlockSpec((SPAN, d3), lambda b: (b, 0)),
        out_shape=jax.ShapeDtypeStruct((s, d3), MXU_DTYPE),
        scratch_shapes=[pltpu.VMEM((SPAN, d), F32)],
        name=name,
        compiler_params=_cparams(("arbitrary",)),
    )(qkv, qkv, qkv, qkv, do, do, lse, lse, dd, dd, table)


def _mix_weights(l_refs):
    ls = [r[...] for r in l_refs]
    m = functools.reduce(jnp.maximum, ls)
    es = [jnp.exp(l - m) for l in ls]
    tot = functools.reduce(lambda a, c: a + c, es)
    return [e / tot for e in es]


def _combine_fwd(os_, ls_, name):
    s, d = os_[0].shape
    n = len(os_)

    def body(*refs):
        o_refs, l_refs, out_ref = refs[:n], refs[n:2 * n], refs[2 * n]
        ws = _mix_weights(l_refs)
        acc = ws[0] * o_refs[0][...]
        for w, o in zip(ws[1:], o_refs[1:]):
            acc = acc + w * o[...]
        out_ref[...] = acc

    return _rows(body, s, ROW_TILE, [("blk", a) for a in os_ + ls_], [("blk", (s, d), F32)], name)[0]


def _combine_bwd(do, o, ls_, name):
    s, d = o.shape
    n = len(ls_)
    ri = lax.broadcasted_iota(jnp.int32, (LANES, LANES), 0) // HEAD_DIM
    ci = lax.broadcasted_iota(jnp.int32, (LANES, LANES), 1) // HEAD_DIM
    seg = (ri == ci).astype(F32)

    def body(do_ref, o_ref, *rest):
        l_refs, seg_ref, outs = rest[:n], rest[n], rest[n + 1:]
        ws = _mix_weights(l_refs)
        dov = do_ref[...]
        prod = dov * o_ref[...]
        for j in range(d // LANES):
            cols = slice(j * LANES, (j + 1) * LANES)
            r = jnp.dot(prod[:, cols], seg_ref[...], precision=lax.Precision.HIGHEST, preferred_element_type=F32)
            for g in range(n):
                outs[2 * g][:, cols] = (ws[g][:, cols] * dov[:, cols]).astype(outs[2 * g].dtype)
                outs[2 * g + 1][:, cols] = ws[g][:, cols] * r

    outs = []
    for _ in range(n):
        outs += [("blk", (s, d), MXU_DTYPE), ("blk", (s, d), F32)]
    res = _rows(body, s, ROW_TILE, [("blk", do), ("blk", o)] + [("blk", l) for l in ls_] + [("all", seg)], outs, name)
    return [(res[2 * g], res[2 * g + 1]) for g in range(n)]


def _ada_fwd(c_all, w, b, name):
    nsub, d, cs = w.shape

    def body(c_ref, w_ref, b_ref, o_ref):
        cv = c_ref[...]
        sc = cv * (1.0 / (1.0 + jnp.exp(-cv)))
        o_ref[...] = _dot_nn(sc, w_ref[...]) + b_ref[...]

    return pl.pallas_call(
        body,
        grid=(nsub,),
        in_specs=[pl.BlockSpec(c_all.shape, lambda i: (0, 0)), pl.BlockSpec((None, d, cs), lambda i: (i, 0, 0)),
                  pl.BlockSpec((None, 1, cs), lambda i: (i, 0, 0))],
        out_specs=pl.BlockSpec((None, N_DEV, cs), lambda i: (i, 0, 0)),
        out_shape=jax.ShapeDtypeStruct((nsub, N_DEV, cs), F32),
        name=name,
        compiler_params=_cparams(("parallel",)),
    )(c_all, w, b)


def _ada_bwd(c_all_t, dm, name):
    d, nb = c_all_t.shape
    nsub, _, cs = dm.shape

    def body(c_ref, dm_ref, o_ref):
        cv = c_ref[...]
        sc = cv * (1.0 / (1.0 + jnp.exp(-cv)))
        acc = sc[:, 0:1] * dm_ref[0:1, :]
        for bi in range(1, nb):
            acc = acc + sc[:, bi:bi + 1] * dm_ref[bi:bi + 1, :]
        o_ref[...] = acc

    return pl.pallas_call(
        body,
        grid=(nsub,),
        in_specs=[pl.BlockSpec(c_all_t.shape, lambda i: (0, 0)), pl.BlockSpec((None, nb, cs), lambda i: (i, 0, 0))],
        out_specs=pl.BlockSpec((None, d, cs), lambda i: (i, 0, 0)),
        out_shape=jax.ShapeDtypeStruct((nsub, d, cs), F32),
        name=name,
        compiler_params=_cparams(("parallel",)),
    )(c_all_t, dm)


def _row_tile(r, row_elems):
    t = 2 * SUBLANES
    if r % t:
        return r
    while t * 2 * row_elems <= 256 * 1024 and r % (t * 2) == 0:
        t *= 2
    return t


def _adamw(w, g, m, v, name):
    shape = w.shape
    c = shape[-1]
    r = w.size // c
    tr = _row_tile(r, c)
    w2, g2, m2, v2 = [a.reshape(r, c) for a in (w, g, m, v)]
    bc1 = 1.0 - ADAM_B1 ** ADAM_STEP
    bc2 = 1.0 - ADAM_B2 ** ADAM_STEP

    def body(w_ref, g_ref, m_ref, v_ref, d_ref, nm_ref, nv_ref):
        gv = g_ref[...]
        nm = ADAM_B1 * m_ref[...] + (1.0 - ADAM_B1) * gv
        nv = ADAM_B2 * v_ref[...] + (1.0 - ADAM_B2) * (gv * gv)
        d_ref[...] = -ADAM_LR * ((nm / bc1) / (jnp.sqrt(nv / bc2) + ADAM_EPS) + ADAM_WD * w_ref[...])
        nm_ref[...] = nm
        nv_ref[...] = nv

    res = _rows(body, r, tr, [("blk", a) for a in (w2, g2, m2, v2)], [("blk", (r, c), F32)] * 3, name)
    return [a.reshape(shape) for a in res]


def _sum_slots(buf, name):
    n, r, c = buf.shape
    tr = _row_tile(r, n * c)

    def body(b_ref, o_ref):
        acc = b_ref[0].astype(F32)
        for k in range(1, n):
            acc = acc + b_ref[k].astype(F32)
        o_ref[...] = acc

    return pl.pallas_call(
        body,
        grid=(r // tr,),
        in_specs=[pl.BlockSpec((n, tr, c), lambda i: (0, i, 0))],
        out_specs=pl.BlockSpec((tr, c), lambda i: (i, 0)),
        out_shape=jax.ShapeDtypeStruct((r, c), F32),
        name=name,
        compiler_params=_cparams(("parallel",)),
    )(buf)


def _me():
    return lax.axis_index("x"), lax.axis_index("y"), lax.axis_index("c")


def _all_gather_small(blk, name):
    m_per, n = blk.shape

    def body(x_ref, out_ref, send_sems, recv_sems, local_sem):
        x, y, c = _me()
        me, sibling = (x, y, c), (x, y, 1 - c)
        chips = [(1 - x, y), (x, 1 - y), (1 - x, 1 - y)]

        def rows(px, py, pc):
            return out_ref.at[pl.ds((4 * px + 2 * py + pc) * m_per, m_per), :]

        def copy(k, block, to, src=None):
            return pltpu.make_async_remote_copy(
                src_ref=rows(*block) if src is None else src, dst_ref=rows(*block),
                send_sem=send_sems.at[k], recv_sem=recv_sems.at[k], device_id=to, device_id_type=MESH)

        mine = pltpu.make_async_copy(x_ref, rows(*me), local_sem)
        mine.start()
        first = [copy(0, me, sibling, src=x_ref)]
        first += [copy(1 + j, me, (*chip, c), src=x_ref) for j, chip in enumerate(chips)]
        for cp in first:
            cp.start()
        passed = [copy(4 + j, (*chip, c), sibling) for j, chip in enumerate(chips)]
        for j, chip in enumerate(chips):
            copy(1 + j, (*chip, c), me).wait_recv()
            passed[j].start()
        copy(0, sibling, me).wait_recv()
        for j, chip in enumerate(chips):
            copy(4 + j, (*chip, 1 - c), me).wait_recv()
        for cp in first + passed:
            cp.wait_send()
        mine.wait()

    return pl.pallas_call(
        body,
        out_shape=jax.ShapeDtypeStruct((N_DEV * m_per, n), blk.dtype),
        in_specs=[pl.BlockSpec(memory_space=pltpu.VMEM)],
        out_specs=pl.BlockSpec(memory_space=pltpu.VMEM),
        scratch_shapes=[pltpu.SemaphoreType.DMA((7,)), pltpu.SemaphoreType.DMA((7,)), pltpu.SemaphoreType.DMA],
        name=name,
        compiler_params=pltpu.CompilerParams(vmem_limit_bytes=VMEM_LIMIT),
    )(blk)


_HBM = pl.BlockSpec(memory_space=pltpu.HBM)
_SEM = pl.BlockSpec(memory_space=pltpu.SEMAPHORE)
_EFFECT = pltpu.SideEffectType.DATAFLOW_SIDE_EFFECTING


def _other_chips(x, y):
    return [(1 - x, y), (x, 1 - y), (1 - x, 1 - y)]


def _gather_copy(w, j, src_ref, dst_ref, send_sems, recv_sems):
    x, y, c = _me()
    return pltpu.make_async_remote_copy(
        src_ref=src_ref, dst_ref=dst_ref, send_sem=send_sems.at[3 * w + j], recv_sem=recv_sems.at[3 * w + j],
        device_id=(*_other_chips(x, y)[j], c), device_id_type=MESH)


QKV_TILE = 768
TILES_PER_CHIP, TILES_PER_PATTERN = 3, 4


def _gather_start(shards, tiles, tiles_at, after, name):
    n, nt, npat = len(shards), len(tiles), len(B_PATTERNS)
    lands = [lax.empty((N_CHIPS,) + s.shape, s.dtype) for s in shards]
    plands = [lax.empty((TILES_PER_PATTERN,) + tiles[0].shape, tiles[0].dtype) for _ in range(npat)]
    n_in = 2 * n + nt + npat

    def body(*refs):
        in_refs, land_refs = refs[:n], refs[n:2 * n]
        tile_refs, pland_refs = refs[2 * n:2 * n + nt], refs[2 * n + nt:n_in]
        send_sems, recv_sems = refs[n_in + 1], refs[n_in + 2]
        token = refs[-1]
        x, y, _ = _me()
        q = 2 * x + y

        def send_tiles():
            for lt in range(nt):
                for qq in range(N_CHIPS):
                    gt = TILES_PER_CHIP * qq + lt

                    @pl.when(q == qq)
                    def _():
                        for j in range(3):
                            _gather_copy(n + lt, j, tile_refs[lt], pland_refs[gt // TILES_PER_PATTERN].at[gt % TILES_PER_PATTERN],
                                         send_sems, recv_sems).start()

        for w in range(n):
            if w == tiles_at:
                send_tiles()
            for j in range(3):
                _gather_copy(w, j, in_refs[w], land_refs[w].at[q], send_sems, recv_sems).start()
        token[...] = jnp.zeros_like(token)

    n_sem = 3 * (n + nt)
    arrays = list(shards) + lands + list(tiles) + plands
    res = pl.pallas_call(
        body,
        out_shape=(pltpu.SemaphoreType.DMA((n_sem,)), pltpu.SemaphoreType.DMA((n_sem,)),
                   *[pltpu.HBM(a.shape, a.dtype) for a in arrays], jax.ShapeDtypeStruct((SUBLANES, LANES), F32)),
        in_specs=[_HBM] * n_in + [pl.BlockSpec(memory_space=pl.ANY)],
        out_specs=(_SEM, _SEM, *[_HBM] * n_in, pl.BlockSpec(memory_space=pltpu.VMEM)),
        input_output_aliases={i: 2 + i for i in range(n_in)},
        name=name,
        compiler_params=pltpu.CompilerParams(has_side_effects=_EFFECT),
    )(*[pltpu.with_memory_space_constraint(a, pltpu.HBM) for a in arrays], after)
    thru = res[2:2 + n_in]
    return res[0], res[1], thru[:n], thru[n:2 * n], thru[2 * n:2 * n + nt], thru[2 * n + nt:], res[-1]


def _gather_wait(w, shard, land, send_sems, recv_sems, after, name):
    def body(s_ref, land_ref, send_sems, recv_sems, after_ref, s_out, land_out, stage):
        x, y, _ = _me()
        pltpu.sync_copy(s_ref, stage)
        pltpu.sync_copy(stage, land_out.at[2 * x + y])
        for j in range(3):
            cp = _gather_copy(w, j, s_ref, land_ref.at[2 * x + y], send_sems, recv_sems)
            cp.wait_send()
            cp.wait_recv()

    return pl.pallas_call(
        body,
        out_shape=(pltpu.HBM(shard.shape, shard.dtype), pltpu.HBM(land.shape, land.dtype)),
        in_specs=(_HBM, _HBM, _SEM, _SEM, pl.BlockSpec(memory_space=pl.ANY)),
        out_specs=(_HBM, _HBM),
        input_output_aliases={0: 0, 1: 1},
        scratch_shapes=[pltpu.VMEM(shard.shape, shard.dtype)],
        name=name,
        compiler_params=pltpu.CompilerParams(has_side_effects=_EFFECT, vmem_limit_bytes=VMEM_LIMIT),
    )(shard, land, send_sems, recv_sems, after)[1]


def _gather_wait_pattern(g, n, tiles, pland, send_sems, recv_sems, after, name):
    nt = len(tiles)
    last = g == len(B_PATTERNS) - 1

    def body(*refs):
        tile_refs, land_ref, send_sems, recv_sems = refs[:nt], refs[nt], refs[nt + 1], refs[nt + 2]
        land_out, stage = refs[nt + 4], refs[nt + 5]
        x, y, c = _me()
        q = 2 * x + y
        for t in range(TILES_PER_PATTERN):
            gt = TILES_PER_PATTERN * g + t
            owner, lt = gt // TILES_PER_CHIP, gt % TILES_PER_CHIP

            @pl.when(q == owner)
            def _():
                pltpu.sync_copy(tile_refs[lt], stage)
                pltpu.sync_copy(stage, land_out.at[t])

            @pl.when(q != owner)
            def _():
                flips = q ^ owner
                j = jnp.where(flips == 2, 0, jnp.where(flips == 1, 1, 2))
                pltpu.make_async_remote_copy(
                    src_ref=tile_refs[lt], dst_ref=land_ref.at[t], send_sem=send_sems.at[3 * (n + lt) + j],
                    recv_sem=recv_sems.at[3 * (n + lt) + j], device_id=(owner // 2, owner % 2, c), device_id_type=MESH).wait_recv()

        if last:
            for lt in range(nt):
                for j in range(3):
                    _gather_copy(n + lt, j, tile_refs[lt], land_ref.at[0], send_sems, recv_sems).wait_send()

    return pl.pallas_call(
        body,
        out_shape=pltpu.HBM(pland.shape, pland.dtype),
        in_specs=(*[_HBM] * nt, _HBM, _SEM, _SEM, pl.BlockSpec(memory_space=pl.ANY)),
        out_specs=_HBM,
        input_output_aliases={nt: 0},
        scratch_shapes=[pltpu.VMEM(tiles[0].shape, tiles[0].dtype)],
        name=name,
        compiler_params=pltpu.CompilerParams(has_side_effects=_EFFECT, vmem_limit_bytes=VMEM_LIMIT),
    )(*tiles, pland, send_sems, recv_sems, after)


def _piece_shape(shape, kind):
    k, nn = shape
    return (k // 2, nn // N_CHIPS) if kind == "col" else (k // N_CHIPS // 2, nn)


def _piece_of(g_ref, kind, tq, tc):
    pr, pc = _piece_shape(g_ref.shape, kind)
    if kind == "col":
        return g_ref.at[pl.ds(tc * pr, pr), pl.ds(tq * pc, pc)]
    return g_ref.at[pl.ds((2 * tq + tc) * pr, pr), :]


def _scatter_copy(r, kind, g_ref, land_ref, send_sems, recv_sems):
    x, y, c = _me()
    tx, ty, tc = (x + ((r >> 2) & 1)) % 2, (y + ((r >> 1) & 1)) % 2, (c + (r & 1)) % 2
    return pltpu.make_async_remote_copy(
        src_ref=_piece_of(g_ref, kind, 2 * tx + ty, tc), dst_ref=land_ref.at[4 * x + 2 * y + c],
        send_sem=send_sems.at[r], recv_sem=recv_sems.at[r], device_id=(tx, ty, tc), device_id_type=MESH)


def _scatter_start(g, kind, name):
    piece = _piece_shape(g.shape, kind)
    land = lax.empty((N_DEV,) + piece, g.dtype)

    def body(g_ref, land_ref, send_sems, recv_sems, g_out, land_out, stage):
        x, y, c = _me()
        for r in range(1, N_DEV):
            _scatter_copy(r, kind, g_ref, land_ref, send_sems, recv_sems).start()
        pltpu.sync_copy(_piece_of(g_ref, kind, 2 * x + y, c), stage)
        pltpu.sync_copy(stage, land_out.at[4 * x + 2 * y + c])

    return pl.pallas_call(
        body,
        out_shape=(pltpu.SemaphoreType.DMA((N_DEV,)), pltpu.SemaphoreType.DMA((N_DEV,)),
                   pltpu.HBM(g.shape, g.dtype), pltpu.HBM(land.shape, land.dtype)),
        in_specs=[_HBM, _HBM],
        out_specs=(_SEM, _SEM, _HBM, _HBM),
        input_output_aliases={0: 2, 1: 3},
        scratch_shapes=[pltpu.VMEM(piece, g.dtype)],
        name=name,
        compiler_params=pltpu.CompilerParams(has_side_effects=_EFFECT, vmem_limit_bytes=VMEM_LIMIT),
    )(pltpu.with_memory_space_constraint(g, pltpu.HBM), pltpu.with_memory_space_constraint(land, pltpu.HBM))


def _scatter_wait(send_sems, recv_sems, g, land, kind, after, name):
    def body(g_ref, land_ref, send_sems, recv_sems, after_ref, g_out, land_out):
        for r in range(1, N_DEV):
            cp = _scatter_copy(r, kind, g_ref, land_ref, send_sems, recv_sems)
            cp.wait_send()
            cp.wait_recv()

    return pl.pallas_call(
        body,
        out_shape=(pltpu.HBM(g.shape, g.dtype), pltpu.HBM(land.shape, land.dtype)),
        in_specs=(_HBM, _HBM, _SEM, _SEM, pl.BlockSpec(memory_space=pl.ANY)),
        out_specs=(_HBM, _HBM),
        input_output_aliases={0: 0, 1: 1},
        name=name,
        compiler_params=pltpu.CompilerParams(has_side_effects=_EFFECT),
    )(g, land, send_sems, recv_sems, after)[1]


def _swap_halves(halves, name):
    n = len(halves)

    def body(*refs):
        in_refs, out_refs = refs[:n], refs[n:2 * n]
        send_sems, recv_sems, local_sems = refs[2 * n:]
        x, y, c = _me()
        cps = []
        for w in range(n):
            lc = pltpu.make_async_copy(in_refs[w], out_refs[w].at[c], local_sems.at[w])
            lc.start()
            rc = pltpu.make_async_remote_copy(
                src_ref=in_refs[w], dst_ref=out_refs[w].at[c], send_sem=send_sems.at[w], recv_sem=recv_sems.at[w],
                device_id=(x, y, 1 - c), device_id_type=MESH)
            rc.start()
            cps.append((lc, rc))
        for lc, rc in cps:
            rc.wait_recv()
        for lc, rc in cps:
            rc.wait_send()
            lc.wait()

    vmem = pl.BlockSpec(memory_space=pltpu.VMEM)
    return pl.pallas_call(
        body,
        out_shape=[jax.ShapeDtypeStruct((2,) + h.shape, h.dtype) for h in halves],
        in_specs=[vmem] * n,
        out_specs=[vmem] * n,
        scratch_shapes=[pltpu.SemaphoreType.DMA((n,)), pltpu.SemaphoreType.DMA((n,)), pltpu.SemaphoreType.DMA((n,))],
        name=name,
        compiler_params=pltpu.CompilerParams(vmem_limit_bytes=VMEM_LIMIT),
    )(*halves)


def _to_streams(a, dil):
    if dil == 1:
        return a
    s, c = a.shape
    return a.reshape(s // dil, dil, c).transpose(1, 0, 2).reshape(s, c)


def _from_streams(a, dil):
    if dil == 1:
        return a
    s, c = a.shape
    return a.reshape(dil, s // dil, c).transpose(1, 0, 2).reshape(s, c)


def _mm_tiles(s):
    return min(s, 1024)


def _local_step(x0, target, mvec, ln_g, ln_b, small, fetch, emit, start):
    s, d = x0.shape
    tm = _mm_tiles(s)
    row = lambda v: v.reshape(1, -1)
    shift = [row(mvec[i, :d]) for i in range(4)]
    scale = [row(mvec[i, d:2 * d]) for i in range(4)]
    gate = [row(1.0 + mvec[i, 2 * d:]) for i in range(4)]
    lg = [row(ln_g[i]) for i in range(4)]
    lb = [row(ln_b[i]) for i in range(4)]
    mm = functools.partial(_mm, tm=tm)
    mm_w = functools.partial(_mm, tm=1024, tk=min(s, 2048), mode="tn")

    xs, ys, big = [x0], [], {}
    h0 = _mod(x0, scale[0], shift[0], start, "mod0")
    big["a_w_in"] = fetch("a_w_in", h0)
    uvpre = mm(h0, big["a_w_in"], mode="nn", name="a_in", outs=[F32], tn=512, tk=1024,
               epi=lambda r, bias: [r + bias], extras=[("row", small["a_b_in"])])
    gated = _spatial_fwd(uvpre, small["a_vn_g"], small["a_vn_b"], small["wc"], small["bias_full"], "a_spatial")
    big["a_w_out"] = fetch("a_w_out", gated)
    ys.append(mm(gated, big["a_w_out"], mode="nn", name="a_out", outs=[F32], tn=1024, tk=1024))
    x1, h1 = _resid_ln(xs[0], ys[0], gate[0], lg[0], lb[0], (scale[1], shift[1]), "ln0")
    xs.append(x1)
    relu2 = lambda r: [jnp.square(jnp.maximum(r, 0.0))]
    big["up0"] = fetch("up0", h1)
    r0 = mm(h1, big["up0"], mode="nn", name="up0", outs=[MXU_DTYPE], tn=1024, tk=1024, epi=relu2)
    big["down0"] = fetch("down0", r0)
    ys.append(mm(r0, big["down0"], mode="nn", name="down0", outs=[F32], tn=1024, tk=2048))
    x2, h2 = _resid_ln(xs[1], ys[1], gate[1], lg[1], lb[1], (scale[2], shift[2]), "ln1")
    xs.append(x2)
    hg, qkvs, o_g, l_g, l_streams, wq, prev = [], [], [], [], [], [], h2
    for g, (_, dil) in enumerate(B_PATTERNS):
        hp = _to_streams(h2, dil)
        wq.append(fetch(f"qkv{g}", prev))
        qkv = mm(hp, wq[g], mode="nn", name=f"qkv{g}", outs=[MXU_DTYPE], tn=QKV_TILE, tk=1024)
        og, lgv = _attn_fwd(qkv, small["slopes"], dil, f"attn_fwd{g}")
        prev = og
        hg.append(hp)
        qkvs.append(qkv)
        o_g.append(_from_streams(og, dil))
        l_g.append(_from_streams(lgv, dil))
        l_streams.append(lgv)
    o_mix = _combine_fwd(o_g, l_g, "combine")
    big["b_w_out"] = fetch("b_w_out", o_mix)
    ys.append(mm(o_mix, big["b_w_out"], mode="nn", name="b_out", outs=[F32], tn=1024, tk=1024))
    x3, h3 = _resid_ln(xs[2], ys[2], gate[2], lg[2], lb[2], (scale[3], shift[3]), "ln2")
    xs.append(x3)
    big["up1"] = fetch("up1", h3)
    r1 = mm(h3, big["up1"], mode="nn", name="up1", outs=[MXU_DTYPE], tn=1024, tk=1024, epi=relu2)
    big["down1"] = fetch("down1", r1)
    ys.append(mm(r1, big["down1"], mode="nn", name="down1", outs=[F32], tn=1024, tk=2048))

    gb, red_ln, red_mod = {}, [None] * 4, [None] * 4

    def mlp_bwd(i, h, r, dyy):
        gb[f"down{i}"] = mm_w(r, dyy, name=f"g_down{i}", outs=[MXU_DTYPE], tn=1024)
        da = mm(dyy, big[f"down{i}"], mode="nt", name=f"d_down{i}", outs=[MXU_DTYPE], tn=1024, tk=1024,
                after=emit(f"down{i}", gb[f"down{i}"]),
                epi=lambda acc, rv: [acc * (2.0 * jnp.sqrt(rv.astype(F32)))], extras=[("full", r)])
        gb[f"up{i}"] = mm_w(h, da, name=f"g_up{i}", outs=[MXU_DTYPE], tn=1024)
        return [mm(da, big[f"up{i}"], mode="nt", name=f"d_up{i}", outs=[F32], tn=1024, tk=1024, after=emit(f"up{i}", gb[f"up{i}"]))]

    def join(sub, dxr, dhs, after=None):
        res = _mod_ln_bwd(dxr, dhs, xs[sub], scale[sub], xs[sub - 1], ys[sub - 1], gate[sub - 1], lg[sub - 1],
                          f"mod_ln_bwd{sub}", after=after)
        red_mod[sub], red_ln[sub - 1] = res[2], res[3]
        return res[0], res[1]

    loss, dxr, dyy, red_ln[3] = _last_ln_loss_bwd(xs[3], ys[3], gate[3], lg[3], lb[3], target, "ln3_loss_bwd")
    dxr, dyy = join(3, dxr, mlp_bwd(1, h3, r1, dyy))
    gb["b_w_out"] = mm_w(o_mix, dyy, name="g_b_out", outs=[MXU_DTYPE], tn=1024, tk=1024)
    do = mm(dyy, big["b_w_out"], mode="nt", name="d_b_out", outs=[F32], tn=1024, tk=1024, after=emit("b_w_out", gb["b_w_out"]))
    parts = _combine_bwd(do, o_mix, l_g, "combine_bwd")
    dhs, gq = [], []
    for g, (_, dil) in enumerate(B_PATTERNS):
        do_g, dd_g = _to_streams(parts[g][0], dil), _to_streams(parts[g][1], dil)
        dqkv = _attn_bwd(qkvs[g], do_g, l_streams[g], dd_g, small["slopes"], dil, f"attn_bwd{g}")
        gq.append(mm_w(hg[g], dqkv, name=f"g_qkv{g}", outs=[MXU_DTYPE], tn=1024))
        dh = mm(dqkv, wq[g], mode="nt", name=f"d_qkv{g}", outs=[F32], tn=1024, tk=QKV_TILE)
        dhs.append(_from_streams(dh, dil))
    gb["b_w_qkv"] = jnp.concatenate(gq, axis=1)
    dxr, dyy = join(2, dxr, dhs, after=emit("b_w_qkv", gb["b_w_qkv"]))
    dxr, dyy = join(1, dxr, mlp_bwd(0, h1, r0, dyy))
    gb["a_w_out"] = mm_w(gated, dyy, name="g_a_out", outs=[MXU_DTYPE], tn=1024)
    dgated = mm(dyy, big["a_w_out"], mode="nt", name="d_a_out", outs=[F32], tn=1024, tk=1024, after=emit("a_w_out", gb["a_w_out"]))
    duv, dws, dbias, dbin, dvg, dvb = _spatial_bwd(uvpre, dgated, small["a_vn_g"], small["a_vn_b"], small["wc"],
                                                   small["wct"], small["bias_full"], "a_spatial_bwd")
    gb["a_w_in"] = mm_w(h0, duv, name="g_a_in", outs=[MXU_DTYPE], tn=1024)
    dh = mm(duv, big["a_w_in"], mode="nt", name="d_a_in", outs=[F32], tn=1024, tk=512, after=emit("a_w_in", gb["a_w_in"]))
    dx, red_mod[0] = _mod_bwd(dxr, [dh], xs[0], scale[0], "mod_bwd0")
    dm = [jnp.concatenate([red_mod[i][0], red_mod[i][1], red_ln[i][2]]) for i in range(4)]
    dlg, dlb = [red_ln[i][0] for i in range(4)], [red_ln[i][1] for i in range(4)]

    tril = jnp.tril(jnp.ones((CHUNK, CHUNK), bool))
    gsmall = {
        "a_b_in": dbin.reshape(-1), "a_vn_g": dvg.reshape(-1), "a_vn_b": dvb.reshape(-1),
        "a_w_s": jnp.where(tril, dws, 0.0).reshape(-1),
        "a_b_s": dbias.reshape(CHUNK, A_GROUPS, d // A_GROUPS).sum(-1).T.reshape(-1),
    }
    return loss, dx, gb, jnp.stack(dm), jnp.stack(dlg), jnp.stack(dlb), gsmall


BIG = ("a_w_in", "a_w_out", "up0", "down0", "b_w_qkv", "b_w_out", "up1", "down1")
BIG_KIND = {"a_w_in": "col", "a_w_out": "row", "b_w_qkv": "col", "b_w_out": "row",
            "up0": "col", "up1": "col", "down0": "row", "down1": "row"}
SMALL = ("a_b_in", "a_vn_g", "a_vn_b", "a_b_s", "a_w_s")


def kernel(x, c, ada_w, ada_b, ln_g, ln_b, a_w_in, a_b_in, a_vn_g, a_vn_b, a_w_s, a_b_s, a_w_out, b_w_qkv, b_w_out, mlp_w_up, mlp_w_down, loss_target, m_ada_w, m_ada_b, m_ln_g, m_ln_b, m_a_w_in, m_a_b_in, m_a_vn_g, m_a_vn_b, m_a_w_s, m_a_b_s, m_a_w_out, m_b_w_qkv, m_b_w_out, m_mlp_w_up, m_mlp_w_down, v_ada_w, v_ada_b, v_ln_g, v_ln_b, v_a_w_in, v_a_b_in, v_a_vn_g, v_a_vn_b, v_a_w_s, v_a_b_s, v_a_w_out, v_b_w_qkv, v_b_w_out, v_mlp_w_up, v_mlp_w_down):
    s, d = x.shape[1], x.shape[2]
    xi, yi, ci = _me()
    q = 2 * xi + yi
    dev = 2 * q + ci
    nsub = 2 * DEPTH
    cs = ada_w.shape[-1]
    ls = ln_g.shape[-1]

    pack = jnp.concatenate([c.reshape(-1), ln_g.reshape(-1), ln_b.reshape(-1)]).reshape(-1, LANES)
    got = _all_gather_small(pack, "gather_small").reshape(N_DEV, -1)
    c_all = got[:, :d]
    per_chip = got[0::2]
    ln_g_full = per_chip[:, d:d + nsub * ls].reshape(N_CHIPS, nsub, ls).transpose(1, 0, 2).reshape(nsub, d)
    ln_b_full = per_chip[:, d + nsub * ls:].reshape(N_CHIPS, nsub, ls).transpose(1, 0, 2).reshape(nsub, d)
    m_part = _ada_fwd(c_all, ada_w.reshape(nsub, d, cs), ada_b.reshape(nsub, 1, cs), "ada_fwd")
    m_all = _all_gather_small(m_part.reshape(-1, LANES), "gather_mod").reshape(N_DEV, nsub, N_DEV, cs)
    m_mine = lax.dynamic_index_in_dim(m_all[0::2], dev, axis=2, keepdims=False)
    mvec = m_mine.transpose(1, 0, 2).reshape(nsub, 3 * d)

    shards = {
        "a_w_in": a_w_in[0], "a_w_out": a_w_out[0], "b_w_qkv": b_w_qkv[0], "b_w_out": b_w_out[0],
        "up0": mlp_w_up[0], "up1": mlp_w_up[1], "down0": mlp_w_down[0], "down1": mlp_w_down[1],
    }
    whole = [k for k in BIG if k != "b_w_qkv"]
    tiles = [shards["b_w_qkv"][:, lt * QKV_TILE:(lt + 1) * QKV_TILE].astype(MXU_DTYPE) for lt in range(TILES_PER_CHIP)]
    send_sems, recv_sems, shard_thru, lands, tile_thru, plands, token = _gather_start(
        [shards[k].astype(MXU_DTYPE) for k in whole], tiles, BIG.index("b_w_qkv"), mvec, "gather_start")

    def fetch(k, after):
        if k.startswith("qkv"):
            return _gather_wait_pattern(int(k[3:]), len(whole), tile_thru, plands[int(k[3:])], send_sems, recv_sems, after, f"gather_wait_{k}")
        w = whole.index(k)
        gw = _gather_wait(w, shard_thru[w], lands[w], send_sems, recv_sems, after, f"gather_wait_{k}")
        return gw if BIG_KIND[k] == "col" else gw.reshape(1, -1, gw.shape[-1])

    scattering = {}

    def emit(k, g):
        scattering[k] = _scatter_start(g, BIG_KIND[k], f"scatter_start_{k}")
        return scattering[k][2]

    tril = jnp.tril(jnp.ones((CHUNK, CHUNK), bool))
    wc = jnp.where(tril, a_w_s[0], 0.0).astype(MXU_DTYPE)
    heads = jnp.arange(1, B_HEADS + 1, dtype=F32)
    small = {
        "a_b_in": a_b_in, "a_vn_g": a_vn_g, "a_vn_b": a_vn_b,
        "wc": wc, "wct": wc.transpose(0, 2, 1),
        "bias_full": jnp.repeat(a_b_s[0].T, d // A_GROUPS, axis=1),
        "slopes": jnp.exp2(-8.0 * heads / B_HEADS),
    }

    loss_part, grad_x, gb, dm, dlg, dlb, gsmall = _local_step(x[0], loss_target[0], mvec, ln_g_full, ln_b_full, small, fetch, emit, token)
    loss = lax.psum(loss_part, ("x", "y", "c"))

    weights = dict(ada_w=ada_w, ada_b=ada_b, ln_g=ln_g, ln_b=ln_b, a_w_in=a_w_in, a_b_in=a_b_in, a_vn_g=a_vn_g, a_vn_b=a_vn_b,
                   a_w_s=a_w_s, a_b_s=a_b_s, a_w_out=a_w_out, b_w_qkv=b_w_qkv, b_w_out=b_w_out, mlp_w_up=mlp_w_up, mlp_w_down=mlp_w_down)
    ms = dict(ada_w=m_ada_w, ada_b=m_ada_b, ln_g=m_ln_g, ln_b=m_ln_b, a_w_in=m_a_w_in, a_b_in=m_a_b_in, a_vn_g=m_a_vn_g, a_vn_b=m_a_vn_b,
              a_w_s=m_a_w_s, a_b_s=m_a_b_s, a_w_out=m_a_w_out, b_w_qkv=m_b_w_qkv, b_w_out=m_b_w_out, mlp_w_up=m_mlp_w_up, mlp_w_down=m_mlp_w_down)
    vs = dict(ada_w=v_ada_w, ada_b=v_ada_b, ln_g=v_ln_g, ln_b=v_ln_b, a_w_in=v_a_w_in, a_b_in=v_a_b_in, a_vn_g=v_a_vn_g, a_vn_b=v_a_vn_b,
              a_w_s=v_a_w_s, a_b_s=v_a_b_s, a_w_out=v_a_w_out, b_w_qkv=v_b_w_qkv, b_w_out=v_b_w_out, mlp_w_up=v_mlp_w_up, mlp_w_down=v_mlp_w_down)
    grads, updates = {}, {}

    def update(k):
        updates[k] = _adamw(weights[k], grads[k], ms[k], vs[k], f"adamw_{k}")
        return updates[k][0]

    pack_b = jnp.concatenate([dm.reshape(-1), dlg.reshape(-1), dlb.reshape(-1)] + [gsmall[k] for k in SMALL])
    n_small = pack_b.shape[0]
    pack_b = jnp.pad(pack_b, (0, -n_small % (ROW_TILE * LANES)))
    got_b = _all_gather_small(pack_b.reshape(-1, LANES), "gather_small_grads").reshape(N_DEV, -1, LANES)
    tot = _sum_slots(got_b, "sum_small").reshape(-1)
    o = 0
    dm_tot = tot[o:o + nsub * 3 * d].reshape(nsub, 3 * d); o += nsub * 3 * d
    dlg_tot = tot[o:o + nsub * d].reshape(nsub, d); o += nsub * d
    dlb_tot = tot[o:o + nsub * d].reshape(nsub, d); o += nsub * d
    g_small = {}
    for k, ref in zip(SMALL, (a_b_in, a_vn_g, a_vn_b, a_b_s, a_w_s)):
        g_small[k] = tot[o:o + ref.size].reshape(ref.shape); o += ref.size
    assert o == n_small
    dm_all = got_b.reshape(N_DEV, -1)[:, :nsub * 3 * d].reshape(N_DEV, nsub, 3 * d)
    dm_cols = lax.dynamic_slice_in_dim(dm_all, q * cs, cs, axis=2).transpose(1, 0, 2)

    grads.update({
        "ada_w": _ada_bwd(c_all.T, dm_cols, "ada_bwd").reshape(ada_w.shape),
        "ada_b": lax.dynamic_slice_in_dim(dm_tot, q * cs, cs, axis=1).reshape(ada_b.shape),
        "ln_g": lax.dynamic_slice_in_dim(dlg_tot, q * ls, ls, axis=1).reshape(ln_g.shape),
        "ln_b": lax.dynamic_slice_in_dim(dlb_tot, q * ls, ls, axis=1).reshape(ln_b.shape),
        **g_small,
    })
    for k in ("ada_b", "ln_g", "ln_b") + SMALL:
        update(k)
    done = update("ada_w")

    gfull = {}
    for group in (("down1", "up1", "b_w_out", "b_w_qkv"), ("down0", "up0", "a_w_out", "a_w_in")):
        bufs = [_scatter_wait(*scattering[k], BIG_KIND[k], done, f"scatter_wait_{k}") for k in group]
        halves = [_sum_slots(b, f"sum_{k}") for k, b in zip(group, bufs)]
        fulls = _swap_halves(halves, f"swap_halves_{group[0]}")
        gfull.update({k: f.reshape(-1, f.shape[-1]) for k, f in zip(group, fulls)})
        if group[0] == "down1":
            grads["b_w_qkv"], grads["b_w_out"] = gfull["b_w_qkv"][None], gfull["b_w_out"][None]
            update("b_w_out")
            done = update("b_w_qkv")
    grads.update({
        "a_w_in": gfull["a_w_in"][None], "a_w_out": gfull["a_w_out"][None],
        "mlp_w_up": jnp.stack([gfull["up0"], gfull["up1"]]), "mlp_w_down": jnp.stack([gfull["down0"], gfull["down1"]]),
    })
    for k in ("a_w_in", "a_w_out", "mlp_w_up", "mlp_w_down"):
        update(k)
    names = list(weights)
    return (loss, grad_x[None], *[grads[k] for k in names], *[updates[k][0] for k in names],
            *[updates[k][1] for k in names], *[updates[k][2] for k in names])
```

```python
import functools
import math

import jax
import jax.numpy as jnp
from jax import lax
from jax.experimental import pallas as pl
from jax.experimental.pallas import tpu as pltpu

F32 = jnp.float32
MXU_DTYPE = jnp.bfloat16

DEPTH = 2
CHUNK = 128
A_GROUPS = 16
B_HEADS = 16
HEAD_DIM = 64
B_PATTERNS = ((128, 1), (512, 4), (2048, 16))
SPAN = 128
ALPHA = (2 * DEPTH) ** 0.25
LN_EPS = 1e-5
NEG = -1e30
ATT_SCALE = HEAD_DIM ** -0.5
ADAM_LR, ADAM_B1, ADAM_B2, ADAM_EPS, ADAM_WD, ADAM_STEP = 0.001, 0.9, 0.999, 1e-08, 0.01, 10

N_CHIPS = 4
N_DEV = 8
LANES = 128
SUBLANES = 8
VMEM_LIMIT = 52 * 1024 * 1024
ROW_TILE = 256
MESH = pl.DeviceIdType.MESH


def _cparams(sem):
    return pltpu.CompilerParams(dimension_semantics=sem, vmem_limit_bytes=VMEM_LIMIT)


def _fold8(v):
    r, c = v.shape
    return jnp.sum(v.reshape(r // SUBLANES, SUBLANES, c), axis=0)


def _gelu(x):
    c = math.sqrt(2.0 / math.pi)
    return 0.5 * x * (1.0 + jnp.tanh(c * (x + 0.044715 * (x * x * x))))


def _gelu_grad(x):
    c = math.sqrt(2.0 / math.pi)
    t = jnp.tanh(c * (x + 0.044715 * (x * x * x)))
    return 0.5 * (1.0 + t) + 0.5 * x * (1.0 - t * t) * c * (1.0 + 3.0 * 0.044715 * x * x)


def _dot(a, b, dims):
    return lax.dot_general(a.astype(MXU_DTYPE), b.astype(MXU_DTYPE), (dims, ((), ())), preferred_element_type=F32)


def _dot_nn(a, b):
    return _dot(a, b, ((1,), (0,)))


def _dot_nt(a, b):
    return _dot(a, b, ((1,), (1,)))


def _dot_tn(a, b):
    return _dot(a, b, ((0,), (0,)))


def _mm(a, b, *, mode, name, outs, tm, tn, tk, epi=None, extras=(), b_col0=0, n_out=None, after=None,
        out_col0=0, out_cols=None, into=None):
    if mode == "nn":
        m, kdim = a.shape
        p, kb, ns = b.shape
        assert kb == kdim and ns % tn == 0 and b_col0 % tn == 0
        n = n_out if n_out is not None else p * ns
        npt, j0 = ns // tn, b_col0 // tn
        a_spec = pl.BlockSpec((tm, tk), lambda i, j, k: (i, k))
        b_spec = pl.BlockSpec((None, tk, tn), lambda i, j, k: ((j + j0) // npt, k, (j + j0) % npt))
        dot = _dot_nn
    elif mode == "nt":
        m, kdim = a.shape
        p, n, ns = b.shape
        assert ns % tk == 0 and b_col0 % tk == 0
        npt, j0 = ns // tk, b_col0 // tk
        a_spec = pl.BlockSpec((tm, tk), lambda i, j, k: (i, k))
        b_spec = pl.BlockSpec((None, tn, tk), lambda i, j, k: ((k + j0) // npt, j, (k + j0) % npt))
        dot = _dot_nt
    else:
        kdim, m = a.shape
        kb, n = b.shape
        assert kb == kdim
        a_spec = pl.BlockSpec((tk, tm), lambda i, j, k: (k, i))
        b_spec = pl.BlockSpec((tk, tn), lambda i, j, k: (k, j))
        dot = _dot_tn
    assert m % tm == 0 and n % tn == 0 and kdim % tk == 0, (name, m, n, kdim, tm, tn, tk)
    nk = kdim // tk
    ex_specs, ex_arrays = [], []
    for kind, arr in extras:
        if kind == "row":
            ex_specs.append(pl.BlockSpec((1, tn), lambda i, j, k: (0, j)))
        else:
            ex_specs.append(pl.BlockSpec((tm, tn), lambda i, j, k: (i, j)))
        ex_arrays.append(arr)
    n_ex, n_o = len(ex_arrays), len(outs)
    deps = [d for d in (after, into) if d is not None]
    n_dep = len(deps)
    j_out = out_col0 // tn
    assert out_col0 % tn == 0 and (into is None or len(outs) == 1)

    def body(a_ref, b_ref, *rest):
        ex_refs, o_refs = rest[:n_ex], rest[n_ex + n_dep:n_ex + n_dep + n_o]
        k = pl.program_id(2)

        def finish(r):
            vals = epi(r, *[e[...] for e in ex_refs]) if epi is not None else [r]
            for o, v in zip(o_refs, vals):
                o[...] = v.astype(o.dtype)

        if nk == 1:
            finish(dot(a_ref[...], b_ref[...]))
            return
        acc = rest[n_ex + n_dep + n_o]

        @pl.when(k == 0)
        def _():
            acc[...] = dot(a_ref[...], b_ref[...])

        @pl.when((k > 0) & (k < nk - 1))
        def _():
            acc[...] += dot(a_ref[...], b_ref[...])

        @pl.when(k == nk - 1)
        def _():
            finish(acc[...] + dot(a_ref[...], b_ref[...]))

    res = pl.pallas_call(
        body,
        grid=(m // tm, n // tn, nk),
        in_specs=[a_spec, b_spec] + ex_specs + [pl.BlockSpec(memory_space=pl.ANY)] * n_dep,
        out_specs=[pl.BlockSpec((tm, tn), lambda i, j, k: (i, j + j_out)) for _ in outs],
        out_shape=[jax.ShapeDtypeStruct((m, out_cols or n), dt) for dt in outs],
        input_output_aliases={} if into is None else {2 + n_ex + n_dep - 1: 0},
        scratch_shapes=[pltpu.VMEM((tm, tn), F32)] if nk > 1 else [],
        name=name,
        compiler_params=_cparams(("parallel", "parallel", "arbitrary")),
    )(a, b, *ex_arrays, *deps)
    return res if len(outs) > 1 else res[0]


def _rows(body, n_rows, tr, ins, outs, name, scratch=()):
    def spec(kind, shape):
        if kind == "blk":
            return pl.BlockSpec((tr,) + tuple(shape[1:]), lambda i: (i,) + (0,) * (len(shape) - 1))
        if kind == "dep":
            return pl.BlockSpec(memory_space=pl.ANY)
        return pl.BlockSpec(tuple(shape), lambda i: (0,) * len(shape))

    return pl.pallas_call(
        body,
        grid=(n_rows // tr,),
        in_specs=[spec(k, a.shape) for k, a in ins],
        out_specs=[spec(k, s) for k, s, _ in outs],
        out_shape=[jax.ShapeDtypeStruct(tuple(s), d) for _, s, d in outs],
        scratch_shapes=list(scratch),
        name=name,
        compiler_params=_cparams(("arbitrary",)),
    )(*[a for _, a in ins])


def _ln_stats(z):
    mu = jnp.mean(z, axis=-1, keepdims=True)
    zc = z - mu
    var = jnp.mean(zc * zc, axis=-1, keepdims=True)
    rstd = lax.rsqrt(var + LN_EPS)
    return zc * rstd, rstd


def _mod(x, scale, shift, after, name):
    s, d = x.shape

    def body(x_ref, sc_ref, sh_ref, dep_ref, h_ref):
        h_ref[...] = (x_ref[...] * (1.0 + sc_ref[...]) + sh_ref[...]).astype(h_ref.dtype)

    return _rows(body, s, ROW_TILE, [("blk", x), ("all", scale), ("all", shift), ("dep", after)], [("blk", (s, d), MXU_DTYPE)], name)[0]


def _resid_ln(x, y, gate, g, b, nxt, name):
    s, d = x.shape

    def body(x_ref, y_ref, gate_ref, g_ref, b_ref, sc_ref, sh_ref, xn_ref, h_ref):
        z = ALPHA * x_ref[...] + gate_ref[...] * y_ref[...]
        xhat, _ = _ln_stats(z)
        xn = xhat * g_ref[...] + b_ref[...]
        xn_ref[...] = xn
        h_ref[...] = (xn * (1.0 + sc_ref[...]) + sh_ref[...]).astype(h_ref.dtype)

    return _rows(body, s, ROW_TILE,
                 [("blk", x), ("blk", y), ("all", gate), ("all", g), ("all", b), ("all", nxt[0]), ("all", nxt[1])],
                 [("blk", (s, d), F32), ("blk", (s, d), MXU_DTYPE)], name)


def _mod_bwd(dxr, dhs, x, scale, name, after=None):
    s, d = x.shape
    n_dh = len(dhs)
    n_dep = 0 if after is None else 1

    def body(dxr_ref, *rest):
        dh_refs = rest[:n_dh]
        x_ref, sc_ref, dx_ref, red_ref, a_sh, a_sc = rest[n_dh:n_dh + 2] + rest[n_dh + 2 + n_dep:]
        i = pl.program_id(0)

        @pl.when(i == 0)
        def _():
            a_sh[...] = jnp.zeros_like(a_sh)
            a_sc[...] = jnp.zeros_like(a_sc)

        dh = dh_refs[0][...]
        for r in dh_refs[1:]:
            dh = dh + r[...]
        dx_ref[...] = dxr_ref[...] + dh * (1.0 + sc_ref[...])
        a_sh[...] += _fold8(dh)
        a_sc[...] += _fold8(dh * x_ref[...])

        @pl.when(i == pl.num_programs(0) - 1)
        def _():
            red_ref[...] = jnp.zeros_like(red_ref)
            red_ref[0:1, :] = jnp.sum(a_sh[...], axis=0, keepdims=True)
            red_ref[1:2, :] = jnp.sum(a_sc[...], axis=0, keepdims=True)

    return _rows(body, s, ROW_TILE, [("blk", dxr)] + [("blk", h) for h in dhs] + [("blk", x), ("all", scale)] + [("dep", after)] * n_dep,
                 [("blk", (s, d), F32), ("all", (SUBLANES, d), F32)], name,
                 scratch=[pltpu.VMEM((SUBLANES, d), F32)] * 2)


def _last_ln_loss_bwd(x, y, gate, g, b, target, name):
    s, d = x.shape

    def body(x_ref, y_ref, gate_ref, g_ref, b_ref, t_ref, l_ref, dxr_ref, dyy_ref, red_ref, a_l, a_g, a_b, a_gate):
        i = pl.program_id(0)

        @pl.when(i == 0)
        def _():
            for a in (a_l, a_g, a_b, a_gate):
                a[...] = jnp.zeros_like(a)

        yv = y_ref[...]
        z = ALPHA * x_ref[...] + gate_ref[...] * yv
        xhat, rstd = _ln_stats(z)
        e = xhat * g_ref[...] + b_ref[...] - t_ref[...]
        a_l[...] += _fold8(e * e)
        dxo_v = e * (1.0 / d)
        dxh = dxo_v * g_ref[...]
        dz = rstd * (dxh - jnp.mean(dxh, axis=-1, keepdims=True) - xhat * jnp.mean(dxh * xhat, axis=-1, keepdims=True))
        dxr_ref[...] = ALPHA * dz
        dyy_ref[...] = (gate_ref[...] * dz).astype(dyy_ref.dtype)
        a_g[...] += _fold8(dxo_v * xhat)
        a_b[...] += _fold8(dxo_v)
        a_gate[...] += _fold8(dz * yv)

        @pl.when(i == pl.num_programs(0) - 1)
        def _():
            l_ref[...] = jnp.full(l_ref.shape, 0.5 / d, F32) * jnp.sum(a_l[...])
            red_ref[...] = jnp.zeros_like(red_ref)
            red_ref[0:1, :] = jnp.sum(a_g[...], axis=0, keepdims=True)
            red_ref[1:2, :] = jnp.sum(a_b[...], axis=0, keepdims=True)
            red_ref[2:3, :] = jnp.sum(a_gate[...], axis=0, keepdims=True)

    l, dxr, dyy, red = _rows(
        body, s, ROW_TILE, [("blk", x), ("blk", y), ("all", gate), ("all", g), ("all", b), ("blk", target)],
        [("all", (SUBLANES, LANES), F32), ("blk", (s, d), F32), ("blk", (s, d), MXU_DTYPE), ("all", (SUBLANES, d), F32)], name,
        scratch=[pltpu.VMEM((SUBLANES, d), F32)] * 4)
    return l[0, 0], dxr, dyy, red


def _mod_ln_bwd(dxr, dhs, x, scale, x_in, y, gate, g, name, after=None):
    s, d = x.shape
    n_dh = len(dhs)
    n_dep = 0 if after is None else 1

    def body(dxr_ref, *rest):
        dh_refs = rest[:n_dh]
        x_ref, sc_ref, xin_ref, y_ref, gate_ref, g_ref = rest[n_dh:n_dh + 6]
        dxr_out, dyy_ref, red_mod, red_ln, a_sh, a_sc, a_g, a_b, a_gate = rest[n_dh + 6 + n_dep:]
        i = pl.program_id(0)

        @pl.when(i == 0)
        def _():
            for a in (a_sh, a_sc, a_g, a_b, a_gate):
                a[...] = jnp.zeros_like(a)

        dh = dh_refs[0][...]
        for r in dh_refs[1:]:
            dh = dh + r[...]
        xv = x_ref[...]
        dxo_v = dxr_ref[...] + dh * (1.0 + sc_ref[...])
        a_sh[...] += _fold8(dh)
        a_sc[...] += _fold8(dh * xv)
        yv = y_ref[...]
        z = ALPHA * xin_ref[...] + gate_ref[...] * yv
        xhat, rstd = _ln_stats(z)
        dxh = dxo_v * g_ref[...]
        dz = rstd * (dxh - jnp.mean(dxh, axis=-1, keepdims=True) - xhat * jnp.mean(dxh * xhat, axis=-1, keepdims=True))
        dxr_out[...] = ALPHA * dz
        dyy_ref[...] = (gate_ref[...] * dz).astype(dyy_ref.dtype)
        a_g[...] += _fold8(dxo_v * xhat)
        a_b[...] += _fold8(dxo_v)
        a_gate[...] += _fold8(dz * yv)

        @pl.when(i == pl.num_programs(0) - 1)
        def _():
            red_mod[...] = jnp.zeros_like(red_mod)
            red_mod[0:1, :] = jnp.sum(a_sh[...], axis=0, keepdims=True)
            red_mod[1:2, :] = jnp.sum(a_sc[...], axis=0, keepdims=True)
            red_ln[...] = jnp.zeros_like(red_ln)
            red_ln[0:1, :] = jnp.sum(a_g[...], axis=0, keepdims=True)
            red_ln[1:2, :] = jnp.sum(a_b[...], axis=0, keepdims=True)
            red_ln[2:3, :] = jnp.sum(a_gate[...], axis=0, keepdims=True)

    ins = ([("blk", dxr)] + [("blk", h) for h in dhs]
           + [("blk", x), ("all", scale), ("blk", x_in), ("blk", y), ("all", gate), ("all", g)] + [("dep", after)] * n_dep)
    return _rows(body, s, ROW_TILE, ins,
                 [("blk", (s, d), F32), ("blk", (s, d), MXU_DTYPE), ("all", (SUBLANES, d), F32), ("all", (SUBLANES, d), F32)], name,
                 scratch=[pltpu.VMEM((SUBLANES, d), F32)] * 5)


def _left_half(shape):
    return lax.broadcasted_iota(jnp.int32, shape, 1) < (LANES // 2)


def _spatial_z(vn, wc_ref, bias_ref, j):
    vb = vn[:, j * LANES:(j + 1) * LANES]
    z0 = _dot_nn(wc_ref[2 * j], vb)
    z1 = _dot_nn(wc_ref[2 * j + 1], vb)
    return jnp.where(_left_half(z0.shape), z0, z1) + bias_ref[:, j * LANES:(j + 1) * LANES]


def _spatial_fwd(uvpre, vn_g, vn_b, wc, bias_full, name):
    s, d2 = uvpre.shape
    d = d2 // 2

    def body(uv_ref, g_ref, b_ref, wc_ref, bias_ref, out_ref):
        u = _gelu(uv_ref[:, :d])
        v = _gelu(uv_ref[:, d:])
        vh, _ = _ln_stats(v)
        vn = vh * g_ref[...] + b_ref[...]
        for j in range(d // LANES):
            z = _spatial_z(vn, wc_ref, bias_ref, j)
            out_ref[:, j * LANES:(j + 1) * LANES] = (u[:, j * LANES:(j + 1) * LANES] * z).astype(out_ref.dtype)

    return _rows(body, s, CHUNK, [("blk", uvpre), ("all", vn_g), ("all", vn_b), ("all", wc), ("all", bias_full)],
                 [("blk", (s, d), MXU_DTYPE)], name)[0]


def _spatial_bwd(uvpre, dgated, vn_g, vn_b, wc, wct, bias_full, name):
    s, d2 = uvpre.shape
    d = d2 // 2

    def body(uv_ref, dg_ref, g_ref, b_ref, wc_ref, wct_ref, bias_ref,
             duv_ref, dws_ref, dbias_ref, dbin_ref, dvg_ref, dvb_ref, dvn_buf, a_bin, a_vg, a_vb):
        i = pl.program_id(0)

        @pl.when(i == 0)
        def _():
            dws_ref[...] = jnp.zeros_like(dws_ref)
            dbias_ref[...] = jnp.zeros_like(dbias_ref)
            a_bin[...] = jnp.zeros_like(a_bin)
            a_vg[...] = jnp.zeros_like(a_vg)
            a_vb[...] = jnp.zeros_like(a_vb)

        up = uv_ref[:, :d]
        vp = uv_ref[:, d:]
        u = _gelu(up)
        v = _gelu(vp)
        vh, rstd = _ln_stats(v)
        vn = vh * g_ref[...] + b_ref[...]
        dg = dg_ref[...]
        dzz = dg * u
        dbias_ref[...] += dzz
        for j in range(d // LANES):
            cols = slice(j * LANES, (j + 1) * LANES)
            z = _spatial_z(vn, wc_ref, bias_ref, j)
            dup = dg[:, cols] * z * _gelu_grad(up[:, cols])
            duv_ref[:, cols] = dup.astype(duv_ref.dtype)
            a_bin[:, cols] += _fold8(dup)
            dzb = dzz[:, cols]
            left = _left_half(dzb.shape)
            dvn_buf[:, cols] = jnp.where(left, _dot_nn(wct_ref[2 * j], dzb), _dot_nn(wct_ref[2 * j + 1], dzb))
            vb = vn[:, cols]
            dws_ref[2 * j] += _dot_nt(jnp.where(left, dzb, 0.0), vb)
            dws_ref[2 * j + 1] += _dot_nt(jnp.where(left, 0.0, dzb), vb)
        dvn = dvn_buf[...]
        a_vg[...] += _fold8(dvn * vh)
        a_vb[...] += _fold8(dvn)
        dvh = dvn * g_ref[...]
        dv = rstd * (dvh - jnp.mean(dvh, axis=-1, keepdims=True) - vh * jnp.mean(dvh * vh, axis=-1, keepdims=True))
        dvp = dv * _gelu_grad(vp)
        duv_ref[:, d:] = dvp.astype(duv_ref.dtype)
        a_bin[:, d:] += _fold8(dvp)

        @pl.when(i == pl.num_programs(0) - 1)
        def _():
            dbin_ref[...] = jnp.sum(a_bin[...], axis=0, keepdims=True)
            dvg_ref[...] = jnp.sum(a_vg[...], axis=0, keepdims=True)
            dvb_ref[...] = jnp.sum(a_vb[...], axis=0, keepdims=True)

    return _rows(body, s, CHUNK,
                 [("blk", uvpre), ("blk", dgated), ("all", vn_g), ("all", vn_b), ("all", wc), ("all", wct), ("all", bias_full)],
                 [("blk", (s, d2), MXU_DTYPE), ("all", (A_GROUPS, CHUNK, CHUNK), F32), ("all", (CHUNK, d), F32),
                  ("all", (1, d2), F32), ("all", (1, d), F32), ("all", (1, d), F32)], name,
                 scratch=[pltpu.VMEM((CHUNK, d), F32), pltpu.VMEM((SUBLANES, d2), F32),
                          pltpu.VMEM((SUBLANES, d), F32), pltpu.VMEM((SUBLANES, d), F32)])


def _head_mask(v, h):
    lane = lax.broadcasted_iota(jnp.int32, v.shape, 1)
    return jnp.where((lane >= h * HEAD_DIM) & (lane < (h + 1) * HEAD_DIM), v, jnp.zeros_like(v))


def _att_bias(slopes, dil):
    qi = lax.broadcasted_iota(jnp.int32, (SPAN, SPAN), 0)
    ki = lax.broadcasted_iota(jnp.int32, (SPAN, SPAN), 1)
    sl = slopes[:, None, None]
    cur = jnp.where(ki <= qi, -sl * (float(dil) * (qi - ki).astype(F32)), NEG)
    prev = jnp.where(ki >= qi, -sl * (float(dil) * (SPAN + qi - ki).astype(F32)), NEG)
    absent = jnp.full_like(prev, NEG)
    pairs = slopes.shape[0] // 2

    def fwd(pv):
        return jnp.concatenate([cur, pv], axis=2).reshape(pairs, 2 * SPAN, 2 * SPAN)

    def bwd(pv):
        return jnp.concatenate([cur.reshape(pairs, 2 * SPAN, SPAN), pv.reshape(pairs, 2 * SPAN, SPAN)], axis=1)

    return jnp.stack([fwd(absent), fwd(prev)]), jnp.stack([bwd(absent), bwd(prev)])


def _att_specs(s, d, dil, kinds):
    nb = s // (dil * SPAN)

    def rowblk(which, b):
        if which == "prev":
            return jnp.where(b % nb == 0, b, b - 1)
        if which == "next":
            return jnp.where(b % nb == nb - 1, b, b + 1)
        return b

    return [pl.BlockSpec((SPAN, d), functools.partial(lambda b, o, w: (rowblk(w, b), o), o=part, w=which))
            for part, which in kinds]


def _lane_col(v, h):
    return v[:, h * HEAD_DIM:h * HEAD_DIM + 1]


def _attn_fwd(qkv, slopes, dil, name):
    s, d3 = qkv.shape
    d = d3 // 3
    nb = s // (dil * SPAN)
    table, _ = _att_bias(slopes, dil)

    def body(q_ref, kc_ref, kp_ref, vc_ref, vp_ref, tb_ref, o_ref, l_ref):
        left = _left_half((SPAN, LANES))
        for hp in range(d // LANES):
            cols = slice(hp * LANES, (hp + 1) * LANES)
            q = q_ref[:, cols]
            q2 = jnp.concatenate([_head_mask(q, 0), _head_mask(q, 1)], axis=0) * ATT_SCALE
            k2 = jnp.concatenate([kc_ref[:, cols], kp_ref[:, cols]], axis=0)
            v2 = jnp.concatenate([vc_ref[:, cols], vp_ref[:, cols]], axis=0)
            sc = _dot_nt(q2, k2) + tb_ref[hp]
            m = jnp.max(sc, axis=-1, keepdims=True)
            p = jnp.exp(sc - m)
            l = jnp.sum(p, axis=-1, keepdims=True)
            r = _dot_nn(p, v2) * (1.0 / l)
            lse = jnp.broadcast_to(m + jnp.log(l), (2 * SPAN, LANES))
            o_ref[:, cols] = jnp.where(left, r[:SPAN], r[SPAN:])
            l_ref[:, cols] = jnp.where(left, lse[:SPAN], lse[SPAN:])

    specs = _att_specs(s, d, dil, [(0, "cur"), (1, "cur"), (1, "prev"), (2, "cur"), (2, "prev")])
    tbl = pl.BlockSpec((None,) + table.shape[1:], lambda b: (jnp.where(b % nb == 0, 0, 1), 0, 0, 0))
    out_spec = pl.BlockSpec((SPAN, d), lambda b: (b, 0))
    return pl.pallas_call(
        body,
        grid=(s // SPAN,),
        in_specs=specs + [tbl],
        out_specs=[out_spec, out_spec],
        out_shape=[jax.ShapeDtypeStruct((s, d), F32)] * 2,
        name=name,
        compiler_params=_cparams(("parallel",)),
    )(qkv, qkv, qkv, qkv, qkv, table)


def _attn_bwd(qkv, do, lse, dd, slopes, dil, name):
    s, d3 = qkv.shape
    d = d3 // 3
    nb = s // (dil * SPAN)
    _, table = _att_bias(slopes, dil)

    def heads_stacked(cur, nxt):
        return jnp.concatenate([_head_mask(cur, 0), _head_mask(cur, 1), _head_mask(nxt, 0), _head_mask(nxt, 1)], axis=0)

    def cols_stacked(cur, nxt):
        return jnp.concatenate([jnp.broadcast_to(_lane_col(a, h), (SPAN, LANES)) for a in (cur, nxt) for h in range(2)], axis=0)

    def body(k_ref, v_ref, qc_ref, qn_ref, doc_ref, don_ref, lc_ref, ln_ref, ddc_ref, ddn_ref, tb_ref, out_ref, carry):
        b = pl.program_id(0)

        @pl.when(b == 0)
        def _():
            carry[...] = jnp.zeros_like(carry)

        left = _left_half((SPAN, LANES))
        for hp in range(d // LANES):
            cols = slice(hp * LANES, (hp + 1) * LANES)
            k, v = k_ref[:, cols], v_ref[:, cols]
            q4 = heads_stacked(qc_ref[:, cols], qn_ref[:, cols])
            do4 = heads_stacked(doc_ref[:, cols], don_ref[:, cols])
            sc = _dot_nt(q4 * ATT_SCALE, k) + tb_ref[hp]
            p = jnp.exp(sc - cols_stacked(lc_ref[:, cols], ln_ref[:, cols]))
            ds = p * (_dot_nt(do4, v) - cols_stacked(ddc_ref[:, cols], ddn_ref[:, cols]))
            dq4 = _dot_nn(ds, k)
            dq_cur = jnp.where(left, dq4[:SPAN], dq4[SPAN:2 * SPAN]) + carry[:, cols]
            carry[:, cols] = jnp.where(left, dq4[2 * SPAN:3 * SPAN], dq4[3 * SPAN:])
            out_ref[:, cols] = (dq_cur * ATT_SCALE).astype(out_ref.dtype)
            out_ref[:, d + hp * LANES:d + (hp + 1) * LANES] = (_dot_tn(ds, q4) * ATT_SCALE).astype(out_ref.dtype)
            out_ref[:, 2 * d + hp * LANES:2 * d + (hp + 1) * LANES] = _dot_tn(p, do4).astype(out_ref.dtype)

    qkv_specs = _att_specs(s, d, dil, [(1, "cur"), (2, "cur"), (0, "cur"), (0, "next")])
    pair = _att_specs(s, d, dil, [(0, "cur"), (0, "next")])
    tbl = pl.BlockSpec((None,) + table.shape[1:], lambda b: (jnp.where(b % nb == nb - 1, 0, 1), 0, 0, 0))
    return pl.pallas_call(
        body,
        grid=(s // SPAN,),
        in_specs=qkv_specs + pair + pair + pair + [tbl],
        out_specs=pl.BlockSpec((SPAN, d3), lambda b: (b, 0)),
        out_shape=jax.ShapeDtypeStruct((s, d3), MXU_DTYPE),
        scratch_shapes=[pltpu.VMEM((SPAN, d), F32)],
        name=name,
        compiler_params=_cparams(("arbitrary",)),
    )(qkv, qkv, qkv, qkv, do, do, lse, lse, dd, dd, table)


def _mix_weights(l_refs):
    ls = [r[...] for r in l_refs]
    m = functools.reduce(jnp.maximum, ls)
    es = [jnp.exp(l - m) for l in ls]
    tot = functools.reduce(lambda a, c: a + c, es)
    return [e / tot for e in es]


def _combine_fwd(os_, ls_, name):
    s, d = os_[0].shape
    n = len(os_)

    def body(*refs):
        o_refs, l_refs, out_ref = refs[:n], refs[n:2 * n], refs[2 * n]
        ws = _mix_weights(l_refs)
        acc = ws[0] * o_refs[0][...]
        for w, o in zip(ws[1:], o_refs[1:]):
            acc = acc + w * o[...]
        out_ref[...] = acc

    return _rows(body, s, ROW_TILE, [("blk", a) for a in os_ + ls_], [("blk", (s, d), F32)], name)[0]


def _combine_bwd(do, o, ls_, name):
    s, d = o.shape
    n = len(ls_)
    ri = lax.broadcasted_iota(jnp.int32, (LANES, LANES), 0) // HEAD_DIM
    ci = lax.broadcasted_iota(jnp.int32, (LANES, LANES), 1) // HEAD_DIM
    seg = (ri == ci).astype(F32)

    def body(do_ref, o_ref, *rest):
        l_refs, seg_ref, outs = rest[:n], rest[n], rest[n + 1:]
        ws = _mix_weights(l_refs)
        dov = do_ref[...]
        prod = dov * o_ref[...]
        for j in range(d // LANES):
            cols = slice(j * LANES, (j + 1) * LANES)
            r = jnp.dot(prod[:, cols], seg_ref[...], precision=lax.Precision.HIGHEST, preferred_element_type=F32)
            for g in range(n):
                outs[2 * g][:, cols] = (ws[g][:, cols] * dov[:, cols]).astype(outs[2 * g].dtype)
                outs[2 * g + 1][:, cols] = ws[g][:, cols] * r

    outs = []
    for _ in range(n):
        outs += [("blk", (s, d), MXU_DTYPE), ("blk", (s, d), F32)]
    res = _rows(body, s, ROW_TILE, [("blk", do), ("blk", o)] + [("blk", l) for l in ls_] + [("all", seg)], outs, name)
    return [(res[2 * g], res[2 * g + 1]) for g in range(n)]


def _ada_fwd(c_all, w, b, name):
    nsub, d, cs = w.shape

    def body(c_ref, w_ref, b_ref, o_ref):
        cv = c_ref[...]
        sc = cv * (1.0 / (1.0 + jnp.exp(-cv)))
        o_ref[...] = _dot_nn(sc, w_ref[...]) + b_ref[...]

    return pl.pallas_call(
        body,
        grid=(nsub,),
        in_specs=[pl.BlockSpec(c_all.shape, lambda i: (0, 0)), pl.BlockSpec((None, d, cs), lambda i: (i, 0, 0)),
                  pl.BlockSpec((None, 1, cs), lambda i: (i, 0, 0))],
        out_specs=pl.BlockSpec((None, N_DEV, cs), lambda i: (i, 0, 0)),
        out_shape=jax.ShapeDtypeStruct((nsub, N_DEV, cs), F32),
        name=name,
        compiler_params=_cparams(("parallel",)),
    )(c_all, w, b)


def _ada_bwd(c_all_t, dm, name):
    d, nb = c_all_t.shape
    nsub, _, cs = dm.shape

    def body(c_ref, dm_ref, o_ref):
        cv = c_ref[...]
        sc = cv * (1.0 / (1.0 + jnp.exp(-cv)))
        acc = sc[:, 0:1] * dm_ref[0:1, :]
        for bi in range(1, nb):
            acc = acc + sc[:, bi:bi + 1] * dm_ref[bi:bi + 1, :]
        o_ref[...] = acc

    return pl.pallas_call(
        body,
        grid=(nsub,),
        in_specs=[pl.BlockSpec(c_all_t.shape, lambda i: (0, 0)), pl.BlockSpec((None, nb, cs), lambda i: (i, 0, 0))],
        out_specs=pl.BlockSpec((None, d, cs), lambda i: (i, 0, 0)),
        out_shape=jax.ShapeDtypeStruct((nsub, d, cs), F32),
        name=name,
        compiler_params=_cparams(("parallel",)),
    )(c_all_t, dm)


def _row_tile(r, row_elems):
    t = 2 * SUBLANES
    if r % t:
        return r
    while t * 2 * row_elems <= 256 * 1024 and r % (t * 2) == 0:
        t *= 2
    return t


def _adamw(w, g, m, v, name):
    shape = w.shape
    c = shape[-1]
    r = w.size // c
    tr = _row_tile(r, c)
    w2, g2, m2, v2 = [a.reshape(r, c) for a in (w, g, m, v)]
    bc1 = 1.0 - ADAM_B1 ** ADAM_STEP
    bc2 = 1.0 - ADAM_B2 ** ADAM_STEP

    def body(w_ref, g_ref, m_ref, v_ref, d_ref, nm_ref, nv_ref):
        gv = g_ref[...]
        nm = ADAM_B1 * m_ref[...] + (1.0 - ADAM_B1) * gv
        nv = ADAM_B2 * v_ref[...] + (1.0 - ADAM_B2) * (gv * gv)
        d_ref[...] = -ADAM_LR * ((nm / bc1) / (jnp.sqrt(nv / bc2) + ADAM_EPS) + ADAM_WD * w_ref[...])
        nm_ref[...] = nm
        nv_ref[...] = nv

    res = _rows(body, r, tr, [("blk", a) for a in (w2, g2, m2, v2)], [("blk", (r, c), F32)] * 3, name)
    return [a.reshape(shape) for a in res]


def _sum_slots(buf, name):
    n, r, c = buf.shape
    tr = _row_tile(r, n * c)

    def body(b_ref, o_ref):
        acc = b_ref[0].astype(F32)
        for k in range(1, n):
            acc = acc + b_ref[k].astype(F32)
        o_ref[...] = acc

    return pl.pallas_call(
        body,
        grid=(r // tr,),
        in_specs=[pl.BlockSpec((n, tr, c), lambda i: (0, i, 0))],
        out_specs=pl.BlockSpec((tr, c), lambda i: (i, 0)),
        out_shape=jax.ShapeDtypeStruct((r, c), F32),
        name=name,
        compiler_params=_cparams(("parallel",)),
    )(buf)


def _me():
    return lax.axis_index("x"), lax.axis_index("y"), lax.axis_index("c")


def _all_gather_small(blk, name, after=()):
    m_per, n = blk.shape

    def body(x_ref, *rest):
        out_ref, send_sems, recv_sems, local_sem = rest[len(after):]
        x, y, c = _me()
        me, sibling = (x, y, c), (x, y, 1 - c)
        chips = [(1 - x, y), (x, 1 - y), (1 - x, 1 - y)]

        def rows(px, py, pc):
            return out_ref.at[pl.ds((4 * px + 2 * py + pc) * m_per, m_per), :]

        def copy(k, block, to, src=None):
            return pltpu.make_async_remote_copy(
                src_ref=rows(*block) if src is None else src, dst_ref=rows(*block),
                send_sem=send_sems.at[k], recv_sem=recv_sems.at[k], device_id=to, device_id_type=MESH)

        mine = pltpu.make_async_copy(x_ref, rows(*me), local_sem)
        mine.start()
        first = [copy(0, me, sibling, src=x_ref)]
        first += [copy(1 + j, me, (*chip, c), src=x_ref) for j, chip in enumerate(chips)]
        for cp in first:
            cp.start()
        passed = [copy(4 + j, (*chip, c), sibling) for j, chip in enumerate(chips)]
        for j, chip in enumerate(chips):
            copy(1 + j, (*chip, c), me).wait_recv()
            passed[j].start()
        copy(0, sibling, me).wait_recv()
        for j, chip in enumerate(chips):
            copy(4 + j, (*chip, 1 - c), me).wait_recv()
        for cp in first + passed:
            cp.wait_send()
        mine.wait()

    return pl.pallas_call(
        body,
        out_shape=jax.ShapeDtypeStruct((N_DEV * m_per, n), blk.dtype),
        in_specs=[pl.BlockSpec(memory_space=pltpu.VMEM)] + [pl.BlockSpec(memory_space=pl.ANY)] * len(after),
        out_specs=pl.BlockSpec(memory_space=pltpu.VMEM),
        scratch_shapes=[pltpu.SemaphoreType.DMA((7,)), pltpu.SemaphoreType.DMA((7,)), pltpu.SemaphoreType.DMA],
        name=name,
        compiler_params=pltpu.CompilerParams(vmem_limit_bytes=VMEM_LIMIT),
    )(blk, *after)


_HBM = pl.BlockSpec(memory_space=pltpu.HBM)
_SEM = pl.BlockSpec(memory_space=pltpu.SEMAPHORE)
_EFFECT = pltpu.SideEffectType.DATAFLOW_SIDE_EFFECTING


def _other_chips(x, y):
    return [(1 - x, y), (x, 1 - y), (1 - x, 1 - y)]


def _gather_copy(w, j, src_ref, land_ref, send_sems, recv_sems):
    x, y, c = _me()
    return pltpu.make_async_remote_copy(
        src_ref=src_ref, dst_ref=land_ref.at[2 * x + y], send_sem=send_sems.at[3 * w + j], recv_sem=recv_sems.at[3 * w + j],
        device_id=(*_other_chips(x, y)[j], c), device_id_type=MESH)


def _gather_start(shards, after, name):
    n = len(shards)
    lands = [lax.empty((N_CHIPS,) + s.shape, s.dtype) for s in shards]

    def body(*refs):
        in_refs, land_refs = refs[:n], refs[n:2 * n]
        send_sems, recv_sems = refs[2 * n + 1], refs[2 * n + 2]
        token = refs[-1]
        for w in range(n):
            for j in range(3):
                _gather_copy(w, j, in_refs[w], land_refs[w], send_sems, recv_sems).start()
        token[...] = jnp.zeros_like(token)

    res = pl.pallas_call(
        body,
        out_shape=(pltpu.SemaphoreType.DMA((3 * n,)), pltpu.SemaphoreType.DMA((3 * n,)),
                   *[pltpu.HBM(s.shape, s.dtype) for s in shards], *[pltpu.HBM(l.shape, l.dtype) for l in lands],
                   jax.ShapeDtypeStruct((SUBLANES, LANES), F32)),
        in_specs=[_HBM] * (2 * n) + [pl.BlockSpec(memory_space=pl.ANY)],
        out_specs=(_SEM, _SEM, *[_HBM] * (2 * n), pl.BlockSpec(memory_space=pltpu.VMEM)),
        input_output_aliases={i: 2 + i for i in range(2 * n)},
        name=name,
        compiler_params=pltpu.CompilerParams(has_side_effects=_EFFECT),
    )(*[pltpu.with_memory_space_constraint(a, pltpu.HBM) for a in list(shards) + lands], after)
    return res[0], res[1], res[2:2 + n], res[2 + n:2 + 2 * n], res[-1]


def _gather_wait(w, shard, land, send_sems, recv_sems, after, name):
    def body(s_ref, land_ref, send_sems, recv_sems, after_ref, s_out, land_out, stage):
        x, y, _ = _me()
        pltpu.sync_copy(s_ref, stage)
        pltpu.sync_copy(stage, land_out.at[2 * x + y])
        for j in range(3):
            cp = _gather_copy(w, j, s_ref, land_ref, send_sems, recv_sems)
            cp.wait_send()
            cp.wait_recv()

    return pl.pallas_call(
        body,
        out_shape=(pltpu.HBM(shard.shape, shard.dtype), pltpu.HBM(land.shape, land.dtype)),
        in_specs=(_HBM, _HBM, _SEM, _SEM, pl.BlockSpec(memory_space=pl.ANY)),
        out_specs=(_HBM, _HBM),
        input_output_aliases={0: 0, 1: 1},
        scratch_shapes=[pltpu.VMEM(shard.shape, shard.dtype)],
        name=name,
        compiler_params=pltpu.CompilerParams(has_side_effects=_EFFECT, vmem_limit_bytes=VMEM_LIMIT),
    )(shard, land, send_sems, recv_sems, after)[1]


def _piece_shape(shape, kind):
    k, nn = shape
    return (k // 2, nn // N_CHIPS) if kind == "col" else (k // N_CHIPS // 2, nn)


def _piece_of(g_ref, kind, tq, tc):
    pr, pc = _piece_shape(g_ref.shape, kind)
    if kind == "col":
        return g_ref.at[pl.ds(tc * pr, pr), pl.ds(tq * pc, pc)]
    return g_ref.at[pl.ds((2 * tq + tc) * pr, pr), :]


def _scatter_copy(w, r, kind, g_ref, land_ref, send_sems, recv_sems):
    x, y, c = _me()
    tx, ty, tc = (x + ((r >> 2) & 1)) % 2, (y + ((r >> 1) & 1)) % 2, (c + (r & 1)) % 2
    return pltpu.make_async_remote_copy(
        src_ref=_piece_of(g_ref, kind, 2 * tx + ty, tc), dst_ref=land_ref.at[4 * x + 2 * y + c],
        send_sem=send_sems.at[N_DEV * w + r], recv_sem=recv_sems.at[N_DEV * w + r], device_id=(tx, ty, tc), device_id_type=MESH)


def _scatter_start(gs, kinds, name):
    n = len(gs)
    pieces = [_piece_shape(g.shape, kind) for g, kind in zip(gs, kinds)]
    lands = [lax.empty((N_DEV,) + p, g.dtype) for p, g in zip(pieces, gs)]

    def body(*refs):
        g_refs, land_refs, send_sems, recv_sems = refs[:n], refs[n:2 * n], refs[2 * n], refs[2 * n + 1]
        land_outs, stages = refs[3 * n + 2:4 * n + 2], refs[4 * n + 2:]
        x, y, c = _me()
        for w in range(n):
            for r in range(1, N_DEV):
                _scatter_copy(w, r, kinds[w], g_refs[w], land_refs[w], send_sems, recv_sems).start()
        for w in range(n):
            pltpu.sync_copy(_piece_of(g_refs[w], kinds[w], 2 * x + y, c), stages[w])
            pltpu.sync_copy(stages[w], land_outs[w].at[4 * x + 2 * y + c])

    arrays = list(gs) + lands
    res = pl.pallas_call(
        body,
        out_shape=(pltpu.SemaphoreType.DMA((N_DEV * n,)), pltpu.SemaphoreType.DMA((N_DEV * n,)),
                   *[pltpu.HBM(a.shape, a.dtype) for a in arrays]),
        in_specs=[_HBM] * (2 * n),
        out_specs=(_SEM, _SEM, *[_HBM] * (2 * n)),
        input_output_aliases={i: 2 + i for i in range(2 * n)},
        scratch_shapes=[pltpu.VMEM(p, g.dtype) for p, g in zip(pieces, gs)],
        name=name,
        compiler_params=pltpu.CompilerParams(has_side_effects=_EFFECT, vmem_limit_bytes=VMEM_LIMIT),
    )(*[pltpu.with_memory_space_constraint(a, pltpu.HBM) for a in arrays])
    return res[0], res[1], res[2:2 + n], res[2 + n:]


def _scatter_wait(send_sems, recv_sems, gs, lands, kinds, after, name):
    n = len(gs)

    def body(*refs):
        g_refs, land_refs, send_sems, recv_sems = refs[:n], refs[n:2 * n], refs[2 * n], refs[2 * n + 1]
        for w in range(n):
            for r in range(1, N_DEV):
                cp = _scatter_copy(w, r, kinds[w], g_refs[w], land_refs[w], send_sems, recv_sems)
                cp.wait_send()
                cp.wait_recv()

    arrays = list(gs) + list(lands)
    return pl.pallas_call(
        body,
        out_shape=tuple(pltpu.HBM(a.shape, a.dtype) for a in arrays),
        in_specs=(*[_HBM] * (2 * n), _SEM, _SEM, pl.BlockSpec(memory_space=pl.ANY)),
        out_specs=tuple([_HBM] * (2 * n)),
        input_output_aliases={i: i for i in range(2 * n)},
        name=name,
        compiler_params=pltpu.CompilerParams(has_side_effects=_EFFECT),
    )(*arrays, send_sems, recv_sems, after)[n:]


def _swap_halves(halves, name):
    n = len(halves)

    def body(*refs):
        in_refs, out_refs = refs[:n], refs[n:2 * n]
        send_sems, recv_sems, local_sems = refs[2 * n:]
        x, y, c = _me()
        cps = []
        for w in range(n):
            lc = pltpu.make_async_copy(in_refs[w], out_refs[w].at[c], local_sems.at[w])
            lc.start()
            rc = pltpu.make_async_remote_copy(
                src_ref=in_refs[w], dst_ref=out_refs[w].at[c], send_sem=send_sems.at[w], recv_sem=recv_sems.at[w],
                device_id=(x, y, 1 - c), device_id_type=MESH)
            rc.start()
            cps.append((lc, rc))
        for lc, rc in cps:
            rc.wait_recv()
        for lc, rc in cps:
            rc.wait_send()
            lc.wait()

    vmem = pl.BlockSpec(memory_space=pltpu.VMEM)
    return pl.pallas_call(
        body,
        out_shape=[jax.ShapeDtypeStruct((2,) + h.shape, h.dtype) for h in halves],
        in_specs=[vmem] * n,
        out_specs=[vmem] * n,
        scratch_shapes=[pltpu.SemaphoreType.DMA((n,)), pltpu.SemaphoreType.DMA((n,)), pltpu.SemaphoreType.DMA((n,))],
        name=name,
        compiler_params=pltpu.CompilerParams(vmem_limit_bytes=VMEM_LIMIT),
    )(*halves)


def _to_streams(a, dil):
    if dil == 1:
        return a
    s, c = a.shape
    return a.reshape(s // dil, dil, c).transpose(1, 0, 2).reshape(s, c)


def _from_streams(a, dil):
    if dil == 1:
        return a
    s, c = a.shape
    return a.reshape(dil, s // dil, c).transpose(1, 0, 2).reshape(s, c)


def _mm_tiles(s):
    return min(s, 1024)


def _local_step(x0, target, mvec, ln_g, ln_b, small, fetch, emit, start):
    s, d = x0.shape
    tm = _mm_tiles(s)
    row = lambda v: v.reshape(1, -1)
    shift = [row(mvec[i, :d]) for i in range(4)]
    scale = [row(mvec[i, d:2 * d]) for i in range(4)]
    gate = [row(1.0 + mvec[i, 2 * d:]) for i in range(4)]
    lg = [row(ln_g[i]) for i in range(4)]
    lb = [row(ln_b[i]) for i in range(4)]
    mm = functools.partial(_mm, tm=tm)
    mm_w = functools.partial(_mm, tm=1024, tk=min(s, 2048), mode="tn")

    xs, ys, big = [x0], [], {}
    h0 = _mod(x0, scale[0], shift[0], start, "mod0")
    big["a_w_in"] = fetch("a_w_in", h0)
    uvpre = mm(h0, big["a_w_in"], mode="nn", name="a_in", outs=[F32], tn=512, tk=1024,
               epi=lambda r, bias: [r + bias], extras=[("row", small["a_b_in"])])
    gated = _spatial_fwd(uvpre, small["a_vn_g"], small["a_vn_b"], small["wc"], small["bias_full"], "a_spatial")
    big["a_w_out"] = fetch("a_w_out", gated)
    ys.append(mm(gated, big["a_w_out"], mode="nn", name="a_out", outs=[F32], tn=1024, tk=1024))
    x1, h1 = _resid_ln(xs[0], ys[0], gate[0], lg[0], lb[0], (scale[1], shift[1]), "ln0")
    xs.append(x1)
    relu2 = lambda r: [jnp.square(jnp.maximum(r, 0.0))]
    big["up0"] = fetch("up0", h1)
    r0 = mm(h1, big["up0"], mode="nn", name="up0", outs=[MXU_DTYPE], tn=1024, tk=1024, epi=relu2)
    big["down0"] = fetch("down0", r0)
    ys.append(mm(r0, big["down0"], mode="nn", name="down0", outs=[F32], tn=1024, tk=2048))
    x2, h2 = _resid_ln(xs[1], ys[1], gate[1], lg[1], lb[1], (scale[2], shift[2]), "ln1")
    xs.append(x2)
    hg, qkvs, o_g, l_g, l_streams = [], [], [], [], []
    big["b_w_qkv"] = fetch("b_w_qkv", h2)
    for g, (_, dil) in enumerate(B_PATTERNS):
        hp = _to_streams(h2, dil)
        qkv = mm(hp, big["b_w_qkv"], mode="nn", name=f"qkv{g}", outs=[MXU_DTYPE], tn=768, tk=1024, b_col0=g * 3 * d, n_out=3 * d)
        og, lgv = _attn_fwd(qkv, small["slopes"], dil, f"attn_fwd{g}")
        hg.append(hp)
        qkvs.append(qkv)
        o_g.append(_from_streams(og, dil))
        l_g.append(_from_streams(lgv, dil))
        l_streams.append(lgv)
    o_mix = _combine_fwd(o_g, l_g, "combine")
    big["b_w_out"] = fetch("b_w_out", o_mix)
    ys.append(mm(o_mix, big["b_w_out"], mode="nn", name="b_out", outs=[F32], tn=1024, tk=1024))
    x3, h3 = _resid_ln(xs[2], ys[2], gate[2], lg[2], lb[2], (scale[3], shift[3]), "ln2")
    xs.append(x3)
    big["up1"] = fetch("up1", h3)
    r1 = mm(h3, big["up1"], mode="nn", name="up1", outs=[MXU_DTYPE], tn=1024, tk=1024, epi=relu2)
    big["down1"] = fetch("down1", r1)
    ys.append(mm(r1, big["down1"], mode="nn", name="down1", outs=[F32], tn=1024, tk=2048))

    gb, red_ln, red_mod = {}, [None] * 4, [None] * 4

    def mlp_bwd(i, h, r, dyy):
        gb[f"down{i}"] = mm_w(r, dyy, name=f"g_down{i}", outs=[MXU_DTYPE], tn=1024)
        da = mm(dyy, big[f"down{i}"], mode="nt", name=f"d_down{i}", outs=[MXU_DTYPE], tn=1024, tk=1024,
                after=emit(f"down{i}", gb[f"down{i}"]),
                epi=lambda acc, rv: [acc * (2.0 * jnp.sqrt(rv.astype(F32)))], extras=[("full", r)])
        gb[f"up{i}"] = mm_w(h, da, name=f"g_up{i}", outs=[MXU_DTYPE], tn=1024)
        return [mm(da, big[f"up{i}"], mode="nt", name=f"d_up{i}", outs=[F32], tn=1024, tk=1024, after=emit(f"up{i}", gb[f"up{i}"]))]

    def join(sub, dxr, dhs, after=None):
        res = _mod_ln_bwd(dxr, dhs, xs[sub], scale[sub], xs[sub - 1], ys[sub - 1], gate[sub - 1], lg[sub - 1],
                          f"mod_ln_bwd{sub}", after=after)
        red_mod[sub], red_ln[sub - 1] = res[2], res[3]
        return res[0], res[1]

    loss, dxr, dyy, red_ln[3] = _last_ln_loss_bwd(xs[3], ys[3], gate[3], lg[3], lb[3], target, "ln3_loss_bwd")
    dxr, dyy = join(3, dxr, mlp_bwd(1, h3, r1, dyy))
    gb["b_w_out"] = mm_w(o_mix, dyy, name="g_b_out", outs=[MXU_DTYPE], tn=1024, tk=1024)
    do = mm(dyy, big["b_w_out"], mode="nt", name="d_b_out", outs=[F32], tn=1024, tk=1024, after=emit("b_w_out", gb["b_w_out"]))
    parts = _combine_bwd(do, o_mix, l_g, "combine_bwd")
    dhs, gq = [], None
    for g, (_, dil) in enumerate(B_PATTERNS):
        do_g, dd_g = _to_streams(parts[g][0], dil), _to_streams(parts[g][1], dil)
        dqkv = _attn_bwd(qkvs[g], do_g, l_streams[g], dd_g, small["slopes"], dil, f"attn_bwd{g}")
        gq = mm_w(hg[g], dqkv, name=f"g_qkv{g}", outs=[MXU_DTYPE], tn=1024, out_col0=g * 3 * d, out_cols=len(B_PATTERNS) * 3 * d, into=gq)
        dh = mm(dqkv, big["b_w_qkv"], mode="nt", name=f"d_qkv{g}", outs=[F32], tn=1024, tk=768, b_col0=g * 3 * d)
        dhs.append(_from_streams(dh, dil))
    gb["b_w_qkv"] = gq
    dxr, dyy = join(2, dxr, dhs, after=emit("b_w_qkv", gb["b_w_qkv"]))
    dxr, dyy = join(1, dxr, mlp_bwd(0, h1, r0, dyy))
    gb["a_w_out"] = mm_w(gated, dyy, name="g_a_out", outs=[MXU_DTYPE], tn=1024)
    dgated = mm(dyy, big["a_w_out"], mode="nt", name="d_a_out", outs=[F32], tn=1024, tk=1024, after=emit("a_w_out", gb["a_w_out"]))
    duv, dws, dbias, dbin, dvg, dvb = _spatial_bwd(uvpre, dgated, small["a_vn_g"], small["a_vn_b"], small["wc"],
                                                   small["wct"], small["bias_full"], "a_spatial_bwd")
    gb["a_w_in"] = mm_w(h0, duv, name="g_a_in", outs=[MXU_DTYPE], tn=1024)
    dh = mm(duv, big["a_w_in"], mode="nt", name="d_a_in", outs=[F32], tn=1024, tk=512, after=emit("a_w_in", gb["a_w_in"]))
    dx, red_mod[0] = _mod_bwd(dxr, [dh], xs[0], scale[0], "mod_bwd0")
    dm = [jnp.concatenate([red_mod[i][0], red_mod[i][1], red_ln[i][2]]) for i in range(4)]
    dlg, dlb = [red_ln[i][0] for i in range(4)], [red_ln[i][1] for i in range(4)]

    tril = jnp.tril(jnp.ones((CHUNK, CHUNK), bool))
    gsmall = {
        "a_b_in": dbin.reshape(-1), "a_vn_g": dvg.reshape(-1), "a_vn_b": dvb.reshape(-1),
        "a_w_s": jnp.where(tril, dws, 0.0).reshape(-1),
        "a_b_s": dbias.reshape(CHUNK, A_GROUPS, d // A_GROUPS).sum(-1).T.reshape(-1),
    }
    return loss, dx, gb, jnp.stack(dm), jnp.stack(dlg), jnp.stack(dlb), gsmall


BIG = ("a_w_in", "a_w_out", "up0", "down0", "b_w_qkv", "b_w_out", "up1", "down1")
BIG_KIND = {"a_w_in": "col", "a_w_out": "row", "b_w_qkv": "col", "b_w_out": "row",
            "up0": "col", "up1": "col", "down0": "row", "down1": "row"}
SCATTER_GROUPS = (("down1", "up1"), ("b_w_out", "b_w_qkv"), ("down0", "up0"), ("a_w_out", "a_w_in"))
SMALL = ("a_b_in", "a_vn_g", "a_vn_b", "a_b_s", "a_w_s")


def kernel(x, c, ada_w, ada_b, ln_g, ln_b, a_w_in, a_b_in, a_vn_g, a_vn_b, a_w_s, a_b_s, a_w_out, b_w_qkv, b_w_out, mlp_w_up, mlp_w_down, loss_target, m_ada_w, m_ada_b, m_ln_g, m_ln_b, m_a_w_in, m_a_b_in, m_a_vn_g, m_a_vn_b, m_a_w_s, m_a_b_s, m_a_w_out, m_b_w_qkv, m_b_w_out, m_mlp_w_up, m_mlp_w_down, v_ada_w, v_ada_b, v_ln_g, v_ln_b, v_a_w_in, v_a_b_in, v_a_vn_g, v_a_vn_b, v_a_w_s, v_a_b_s, v_a_w_out, v_b_w_qkv, v_b_w_out, v_mlp_w_up, v_mlp_w_down):
    s, d = x.shape[1], x.shape[2]
    xi, yi, ci = _me()
    q = 2 * xi + yi
    dev = 2 * q + ci
    nsub = 2 * DEPTH
    cs = ada_w.shape[-1]
    ls = ln_g.shape[-1]

    shards = {
        "a_w_in": a_w_in[0], "a_w_out": a_w_out[0], "b_w_qkv": b_w_qkv[0], "b_w_out": b_w_out[0],
        "up0": mlp_w_up[0], "up1": mlp_w_up[1], "down0": mlp_w_down[0], "down1": mlp_w_down[1],
    }
    cast = [shards[k].astype(MXU_DTYPE) for k in BIG]

    pack = jnp.concatenate([c.reshape(-1), ln_g.reshape(-1), ln_b.reshape(-1)]).reshape(-1, LANES)
    got = _all_gather_small(pack, "gather_small", after=cast).reshape(N_DEV, -1)
    c_all = got[:, :d]
    per_chip = got[0::2]
    ln_g_full = per_chip[:, d:d + nsub * ls].reshape(N_CHIPS, nsub, ls).transpose(1, 0, 2).reshape(nsub, d)
    ln_b_full = per_chip[:, d + nsub * ls:].reshape(N_CHIPS, nsub, ls).transpose(1, 0, 2).reshape(nsub, d)
    m_part = _ada_fwd(c_all, ada_w.reshape(nsub, d, cs), ada_b.reshape(nsub, 1, cs), "ada_fwd")
    m_all = _all_gather_small(m_part.reshape(-1, LANES), "gather_mod").reshape(N_DEV, nsub, N_DEV, cs)
    m_mine = lax.dynamic_index_in_dim(m_all[0::2], dev, axis=2, keepdims=False)
    mvec = m_mine.transpose(1, 0, 2).reshape(nsub, 3 * d)

    send_sems, recv_sems, shard_thru, lands, token = _gather_start(cast, mvec, "gather_start")

    def fetch(k, after):
        w = BIG.index(k)
        gw = _gather_wait(w, shard_thru[w], lands[w], send_sems, recv_sems, after, f"gather_wait_{k}")
        return gw if BIG_KIND[k] == "col" else gw.reshape(1, -1, gw.shape[-1])

    scattering, pending = {}, {}

    def emit(k, g):
        pending[k] = g
        group = next(gr for gr in SCATTER_GROUPS if k in gr)
        if k != group[-1]:
            return None
        scattering[group] = _scatter_start([pending[m] for m in group], [BIG_KIND[m] for m in group], f"scatter_start_{k}")
        return scattering[group][2][0]

    tril = jnp.tril(jnp.ones((CHUNK, CHUNK), bool))
    wc = jnp.where(tril, a_w_s[0], 0.0).astype(MXU_DTYPE)
    heads = jnp.arange(1, B_HEADS + 1, dtype=F32)
    small = {
        "a_b_in": a_b_in, "a_vn_g": a_vn_g, "a_vn_b": a_vn_b,
        "wc": wc, "wct": wc.transpose(0, 2, 1),
        "bias_full": jnp.repeat(a_b_s[0].T, d // A_GROUPS, axis=1),
        "slopes": jnp.exp2(-8.0 * heads / B_HEADS),
    }

    loss_part, grad_x, gb, dm, dlg, dlb, gsmall = _local_step(x[0], loss_target[0], mvec, ln_g_full, ln_b_full, small, fetch, emit, token)
    loss = lax.psum(loss_part, ("x", "y", "c"))

    weights = dict(ada_w=ada_w, ada_b=ada_b, ln_g=ln_g, ln_b=ln_b, a_w_in=a_w_in, a_b_in=a_b_in, a_vn_g=a_vn_g, a_vn_b=a_vn_b,
                   a_w_s=a_w_s, a_b_s=a_b_s, a_w_out=a_w_out, b_w_qkv=b_w_qkv, b_w_out=b_w_out, mlp_w_up=mlp_w_up, mlp_w_down=mlp_w_down)
    ms = dict(ada_w=m_ada_w, ada_b=m_ada_b, ln_g=m_ln_g, ln_b=m_ln_b, a_w_in=m_a_w_in, a_b_in=m_a_b_in, a_vn_g=m_a_vn_g, a_vn_b=m_a_vn_b,
              a_w_s=m_a_w_s, a_b_s=m_a_b_s, a_w_out=m_a_w_out, b_w_qkv=m_b_w_qkv, b_w_out=m_b_w_out, mlp_w_up=m_mlp_w_up, mlp_w_down=m_mlp_w_down)
    vs = dict(ada_w=v_ada_w, ada_b=v_ada_b, ln_g=v_ln_g, ln_b=v_ln_b, a_w_in=v_a_w_in, a_b_in=v_a_b_in, a_vn_g=v_a_vn_g, a_vn_b=v_a_vn_b,
              a_w_s=v_a_w_s, a_b_s=v_a_b_s, a_w_out=v_a_w_out, b_w_qkv=v_b_w_qkv, b_w_out=v_b_w_out, mlp_w_up=v_mlp_w_up, mlp_w_down=v_mlp_w_down)
    grads, updates = {}, {}

    def update(k):
        updates[k] = _adamw(weights[k], grads[k], ms[k], vs[k], f"adamw_{k}")
        return updates[k][0]

    pack_b = jnp.concatenate([dm.reshape(-1), dlg.reshape(-1), dlb.reshape(-1)] + [gsmall[k] for k in SMALL])
    n_small = pack_b.shape[0]
    pack_b = jnp.pad(pack_b, (0, -n_small % (ROW_TILE * LANES)))
    got_b = _all_gather_small(pack_b.reshape(-1, LANES), "gather_small_grads").reshape(N_DEV, -1, LANES)
    tot = _sum_slots(got_b, "sum_small").reshape(-1)
    o = 0
    dm_tot = tot[o:o + nsub * 3 * d].reshape(nsub, 3 * d); o += nsub * 3 * d
    dlg_tot = tot[o:o + nsub * d].reshape(nsub, d); o += nsub * d
    dlb_tot = tot[o:o + nsub * d].reshape(nsub, d); o += nsub * d
    g_small = {}
    for k, ref in zip(SMALL, (a_b_in, a_vn_g, a_vn_b, a_b_s, a_w_s)):
        g_small[k] = tot[o:o + ref.size].reshape(ref.shape); o += ref.size
    assert o == n_small
    dm_all = got_b.reshape(N_DEV, -1)[:, :nsub * 3 * d].reshape(N_DEV, nsub, 3 * d)
    dm_cols = lax.dynamic_slice_in_dim(dm_all, q * cs, cs, axis=2).transpose(1, 0, 2)

    grads.update({
        "ada_w": _ada_bwd(c_all.T, dm_cols, "ada_bwd").reshape(ada_w.shape),
        "ada_b": lax.dynamic_slice_in_dim(dm_tot, q * cs, cs, axis=1).reshape(ada_b.shape),
        "ln_g": lax.dynamic_slice_in_dim(dlg_tot, q * ls, ls, axis=1).reshape(ln_g.shape),
        "ln_b": lax.dynamic_slice_in_dim(dlb_tot, q * ls, ls, axis=1).reshape(ln_b.shape),
        **g_small,
    })
    for k in ("ada_b", "ln_g", "ln_b") + SMALL:
        update(k)
    done = update("ada_w")

    gfull = {}
    for group in (SCATTER_GROUPS[0] + SCATTER_GROUPS[1], SCATTER_GROUPS[2] + SCATTER_GROUPS[3]):
        bufs = []
        for pair in (group[:2], group[2:]):
            bufs += _scatter_wait(*scattering[pair], [BIG_KIND[m] for m in pair], done, f"scatter_wait_{pair[-1]}")
        halves = [_sum_slots(b, f"sum_{k}") for k, b in zip(group, bufs)]
        fulls = _swap_halves(halves, f"swap_halves_{group[0]}")
        gfull.update({k: f.reshape(-1, f.shape[-1]) for k, f in zip(group, fulls)})
        if group[0] == "down1":
            grads["b_w_qkv"], grads["b_w_out"] = gfull["b_w_qkv"][None], gfull["b_w_out"][None]
            update("b_w_out")
            done = update("b_w_qkv")
    grads.update({
        "a_w_in": gfull["a_w_in"][None], "a_w_out": gfull["a_w_out"][None],
        "mlp_w_up": jnp.stack([gfull["up0"], gfull["up1"]]), "mlp_w_down": jnp.stack([gfull["down0"], gfull["down1"]]),
    })
    for k in ("a_w_in", "a_w_out", "mlp_w_up", "mlp_w_down"):
        update(k)
    names = list(weights)
    return (loss, grad_x[None], *[grads[k] for k in names], *[updates[k][0] for k in names],
            *[updates[k][1] for k in names], *[updates[k][2] for k in names])
```

```python
import functools
import math

import jax
import jax.numpy as jnp
from jax import lax
from jax.experimental import pallas as pl
from jax.experimental.pallas import tpu as pltpu

F32 = jnp.float32
MXU_DTYPE = jnp.bfloat16

DEPTH = 2
CHUNK = 128
A_GROUPS = 16
B_HEADS = 16
HEAD_DIM = 64
B_PATTERNS = ((128, 1), (512, 4), (2048, 16))
SPAN = 128
ALPHA = (2 * DEPTH) ** 0.25
LN_EPS = 1e-5
NEG = -1e30
ATT_SCALE = HEAD_DIM ** -0.5
ADAM_LR, ADAM_B1, ADAM_B2, ADAM_EPS, ADAM_WD, ADAM_STEP = 0.001, 0.9, 0.999, 1e-08, 0.01, 10

N_CHIPS = 4
N_DEV = 8
LANES = 128
SUBLANES = 8
VMEM_LIMIT = 52 * 1024 * 1024
ROW_TILE = 256
MESH = pl.DeviceIdType.MESH


def _cparams(sem):
    return pltpu.CompilerParams(dimension_semantics=sem, vmem_limit_bytes=VMEM_LIMIT)


def _fold8(v):
    r, c = v.shape
    return jnp.sum(v.reshape(r // SUBLANES, SUBLANES, c), axis=0)


def _gelu(x):
    c = math.sqrt(2.0 / math.pi)
    return 0.5 * x * (1.0 + jnp.tanh(c * (x + 0.044715 * (x * x * x))))


def _gelu_grad(x):
    c = math.sqrt(2.0 / math.pi)
    t = jnp.tanh(c * (x + 0.044715 * (x * x * x)))
    return 0.5 * (1.0 + t) + 0.5 * x * (1.0 - t * t) * c * (1.0 + 3.0 * 0.044715 * x * x)


def _dot(a, b, dims):
    return lax.dot_general(a.astype(MXU_DTYPE), b.astype(MXU_DTYPE), (dims, ((), ())), preferred_element_type=F32)


def _dot_nn(a, b):
    return _dot(a, b, ((1,), (0,)))


def _dot_nt(a, b):
    return _dot(a, b, ((1,), (1,)))


def _dot_tn(a, b):
    return _dot(a, b, ((0,), (0,)))


def _mm(a, b, *, mode, name, outs, tm, tn, tk, epi=None, extras=(), b_col0=0, n_out=None, after=None,
        out_col0=0, out_cols=None, into=None):
    if mode == "nn":
        m, kdim = a.shape
        p, kb, ns = b.shape
        assert kb == kdim and ns % tn == 0 and b_col0 % tn == 0
        n = n_out if n_out is not None else p * ns
        npt, j0 = ns // tn, b_col0 // tn
        a_spec = pl.BlockSpec((tm, tk), lambda i, j, k: (i, k))
        b_spec = pl.BlockSpec((None, tk, tn), lambda i, j, k: ((j + j0) // npt, k, (j + j0) % npt))
        dot = _dot_nn
    elif mode == "nt":
        m, kdim = a.shape
        p, n, ns = b.shape
        assert ns % tk == 0 and b_col0 % tk == 0
        npt, j0 = ns // tk, b_col0 // tk
        a_spec = pl.BlockSpec((tm, tk), lambda i, j, k: (i, k))
        b_spec = pl.BlockSpec((None, tn, tk), lambda i, j, k: ((k + j0) // npt, j, (k + j0) % npt))
        dot = _dot_nt
    else:
        kdim, m = a.shape
        kb, n = b.shape
        assert kb == kdim
        a_spec = pl.BlockSpec((tk, tm), lambda i, j, k: (k, i))
        b_spec = pl.BlockSpec((tk, tn), lambda i, j, k: (k, j))
        dot = _dot_tn
    assert m % tm == 0 and n % tn == 0 and kdim % tk == 0, (name, m, n, kdim, tm, tn, tk)
    nk = kdim // tk
    ex_specs, ex_arrays = [], []
    for kind, arr in extras:
        if kind == "row":
            ex_specs.append(pl.BlockSpec((1, tn), lambda i, j, k: (0, j)))
        else:
            ex_specs.append(pl.BlockSpec((tm, tn), lambda i, j, k: (i, j)))
        ex_arrays.append(arr)
    n_ex, n_o = len(ex_arrays), len(outs)
    deps = [d for d in (after, into) if d is not None]
    n_dep = len(deps)
    j_out = out_col0 // tn
    assert out_col0 % tn == 0 and (into is None or len(outs) == 1)

    def body(a_ref, b_ref, *rest):
        ex_refs, o_refs = rest[:n_ex], rest[n_ex + n_dep:n_ex + n_dep + n_o]
        k = pl.program_id(2)

        def finish(r):
            vals = epi(r, *[e[...] for e in ex_refs]) if epi is not None else [r]
            for o, v in zip(o_refs, vals):
                o[...] = v.astype(o.dtype)

        if nk == 1:
            finish(dot(a_ref[...], b_ref[...]))
            return
        acc = rest[n_ex + n_dep + n_o]

        @pl.when(k == 0)
        def _():
            acc[...] = dot(a_ref[...], b_ref[...])

        @pl.when((k > 0) & (k < nk - 1))
        def _():
            acc[...] += dot(a_ref[...], b_ref[...])

        @pl.when(k == nk - 1)
        def _():
            finish(acc[...] + dot(a_ref[...], b_ref[...]))

    res = pl.pallas_call(
        body,
        grid=(m // tm, n // tn, nk),
        in_specs=[a_spec, b_spec] + ex_specs + [pl.BlockSpec(memory_space=pl.ANY)] * n_dep,
        out_specs=[pl.BlockSpec((tm, tn), lambda i, j, k: (i, j + j_out)) for _ in outs],
        out_shape=[jax.ShapeDtypeStruct((m, out_cols or n), dt) for dt in outs],
        input_output_aliases={} if into is None else {2 + n_ex + n_dep - 1: 0},
        scratch_shapes=[pltpu.VMEM((tm, tn), F32)] if nk > 1 else [],
        name=name,
        compiler_params=_cparams(("parallel", "parallel", "arbitrary")),
    )(a, b, *ex_arrays, *deps)
    return res if len(outs) > 1 else res[0]


def _rows(body, n_rows, tr, ins, outs, name, scratch=()):
    def spec(kind, shape):
        if kind == "blk":
            return pl.BlockSpec((tr,) + tuple(shape[1:]), lambda i: (i,) + (0,) * (len(shape) - 1))
        if kind == "dep":
            return pl.BlockSpec(memory_space=pl.ANY)
        return pl.BlockSpec(tuple(shape), lambda i: (0,) * len(shape))

    return pl.pallas_call(
        body,
        grid=(n_rows // tr,),
        in_specs=[spec(k, a.shape) for k, a in ins],
        out_specs=[spec(k, s) for k, s, _ in outs],
        out_shape=[jax.ShapeDtypeStruct(tuple(s), d) for _, s, d in outs],
        scratch_shapes=list(scratch),
        name=name,
        compiler_params=_cparams(("arbitrary",)),
    )(*[a for _, a in ins])


def _ln_stats(z):
    mu = jnp.mean(z, axis=-1, keepdims=True)
    zc = z - mu
    var = jnp.mean(zc * zc, axis=-1, keepdims=True)
    rstd = lax.rsqrt(var + LN_EPS)
    return zc * rstd, rstd


def _mod(x, scale, shift, after, name):
    s, d = x.shape

    def body(x_ref, sc_ref, sh_ref, dep_ref, h_ref):
        h_ref[...] = (x_ref[...] * (1.0 + sc_ref[...]) + sh_ref[...]).astype(h_ref.dtype)

    return _rows(body, s, ROW_TILE, [("blk", x), ("all", scale), ("all", shift), ("dep", after)], [("blk", (s, d), MXU_DTYPE)], name)[0]


def _resid_ln(x, y, gate, g, b, nxt, name):
    s, d = x.shape

    def body(x_ref, y_ref, gate_ref, g_ref, b_ref, sc_ref, sh_ref, xn_ref, h_ref):
        z = ALPHA * x_ref[...] + gate_ref[...] * y_ref[...]
        xhat, _ = _ln_stats(z)
        xn = xhat * g_ref[...] + b_ref[...]
        xn_ref[...] = xn
        h_ref[...] = (xn * (1.0 + sc_ref[...]) + sh_ref[...]).astype(h_ref.dtype)

    return _rows(body, s, ROW_TILE,
                 [("blk", x), ("blk", y), ("all", gate), ("all", g), ("all", b), ("all", nxt[0]), ("all", nxt[1])],
                 [("blk", (s, d), F32), ("blk", (s, d), MXU_DTYPE)], name)


def _mod_bwd(dxr, dhs, x, scale, name, after=None):
    s, d = x.shape
    n_dh = len(dhs)
    n_dep = 0 if after is None else 1

    def body(dxr_ref, *rest):
        dh_refs = rest[:n_dh]
        x_ref, sc_ref, dx_ref, red_ref, a_sh, a_sc = rest[n_dh:n_dh + 2] + rest[n_dh + 2 + n_dep:]
        i = pl.program_id(0)

        @pl.when(i == 0)
        def _():
            a_sh[...] = jnp.zeros_like(a_sh)
            a_sc[...] = jnp.zeros_like(a_sc)

        dh = dh_refs[0][...]
        for r in dh_refs[1:]:
            dh = dh + r[...]
        dx_ref[...] = dxr_ref[...] + dh * (1.0 + sc_ref[...])
        a_sh[...] += _fold8(dh)
        a_sc[...] += _fold8(dh * x_ref[...])

        @pl.when(i == pl.num_programs(0) - 1)
        def _():
            red_ref[...] = jnp.zeros_like(red_ref)
            red_ref[0:1, :] = jnp.sum(a_sh[...], axis=0, keepdims=True)
            red_ref[1:2, :] = jnp.sum(a_sc[...], axis=0, keepdims=True)

    return _rows(body, s, ROW_TILE, [("blk", dxr)] + [("blk", h) for h in dhs] + [("blk", x), ("all", scale)] + [("dep", after)] * n_dep,
                 [("blk", (s, d), F32), ("all", (SUBLANES, d), F32)], name,
                 scratch=[pltpu.VMEM((SUBLANES, d), F32)] * 2)


def _last_ln_loss_bwd(x, y, gate, g, b, target, name):
    s, d = x.shape

    def body(x_ref, y_ref, gate_ref, g_ref, b_ref, t_ref, l_ref, dxr_ref, dyy_ref, red_ref, a_l, a_g, a_b, a_gate):
        i = pl.program_id(0)

        @pl.when(i == 0)
        def _():
            for a in (a_l, a_g, a_b, a_gate):
                a[...] = jnp.zeros_like(a)

        yv = y_ref[...]
        z = ALPHA * x_ref[...] + gate_ref[...] * yv
        xhat, rstd = _ln_stats(z)
        e = xhat * g_ref[...] + b_ref[...] - t_ref[...]
        a_l[...] += _fold8(e * e)
        dxo_v = e * (1.0 / d)
        dxh = dxo_v * g_ref[...]
        dz = rstd * (dxh - jnp.mean(dxh, axis=-1, keepdims=True) - xhat * jnp.mean(dxh * xhat, axis=-1, keepdims=True))
        dxr_ref[...] = ALPHA * dz
        dyy_ref[...] = (gate_ref[...] * dz).astype(dyy_ref.dtype)
        a_g[...] += _fold8(dxo_v * xhat)
        a_b[...] += _fold8(dxo_v)
        a_gate[...] += _fold8(dz * yv)

        @pl.when(i == pl.num_programs(0) - 1)
        def _():
            l_ref[...] = jnp.full(l_ref.shape, 0.5 / d, F32) * jnp.sum(a_l[...])
            red_ref[...] = jnp.zeros_like(red_ref)
            red_ref[0:1, :] = jnp.sum(a_g[...], axis=0, keepdims=True)
            red_ref[1:2, :] = jnp.sum(a_b[...], axis=0, keepdims=True)
            red_ref[2:3, :] = jnp.sum(a_gate[...], axis=0, keepdims=True)

    l, dxr, dyy, red = _rows(
        body, s, ROW_TILE, [("blk", x), ("blk", y), ("all", gate), ("all", g), ("all", b), ("blk", target)],
        [("all", (SUBLANES, LANES), F32), ("blk", (s, d), F32), ("blk", (s, d), MXU_DTYPE), ("all", (SUBLANES, d), F32)], name,
        scratch=[pltpu.VMEM((SUBLANES, d), F32)] * 4)
    return l[0, 0], dxr, dyy, red


def _mod_ln_bwd(dxr, dhs, x, scale, x_in, y, gate, g, name, after=None):
    s, d = x.shape
    n_dh = len(dhs)
    n_dep = 0 if after is None else 1

    def body(dxr_ref, *rest):
        dh_refs = rest[:n_dh]
        x_ref, sc_ref, xin_ref, y_ref, gate_ref, g_ref = rest[n_dh:n_dh + 6]
        dxr_out, dyy_ref, red_mod, red_ln, a_sh, a_sc, a_g, a_b, a_gate = rest[n_dh + 6 + n_dep:]
        i = pl.program_id(0)

        @pl.when(i == 0)
        def _():
            for a in (a_sh, a_sc, a_g, a_b, a_gate):
                a[...] = jnp.zeros_like(a)

        dh = dh_refs[0][...]
        for r in dh_refs[1:]:
            dh = dh + r[...]
        xv = x_ref[...]
        dxo_v = dxr_ref[...] + dh * (1.0 + sc_ref[...])
        a_sh[...] += _fold8(dh)
        a_sc[...] += _fold8(dh * xv)
        yv = y_ref[...]
        z = ALPHA * xin_ref[...] + gate_ref[...] * yv
        xhat, rstd = _ln_stats(z)
        dxh = dxo_v * g_ref[...]
        dz = rstd * (dxh - jnp.mean(dxh, axis=-1, keepdims=True) - xhat * jnp.mean(dxh * xhat, axis=-1, keepdims=True))
        dxr_out[...] = ALPHA * dz
        dyy_ref[...] = (gate_ref[...] * dz).astype(dyy_ref.dtype)
        a_g[...] += _fold8(dxo_v * xhat)
        a_b[...] += _fold8(dxo_v)
        a_gate[...] += _fold8(dz * yv)

        @pl.when(i == pl.num_programs(0) - 1)
        def _():
            red_mod[...] = jnp.zeros_like(red_mod)
            red_mod[0:1, :] = jnp.sum(a_sh[...], axis=0, keepdims=True)
            red_mod[1:2, :] = jnp.sum(a_sc[...], axis=0, keepdims=True)
            red_ln[...] = jnp.zeros_like(red_ln)
            red_ln[0:1, :] = jnp.sum(a_g[...], axis=0, keepdims=True)
            red_ln[1:2, :] = jnp.sum(a_b[...], axis=0, keepdims=True)
            red_ln[2:3, :] = jnp.sum(a_gate[...], axis=0, keepdims=True)

    ins = ([("blk", dxr)] + [("blk", h) for h in dhs]
           + [("blk", x), ("all", scale), ("blk", x_in), ("blk", y), ("all", gate), ("all", g)] + [("dep", after)] * n_dep)
    return _rows(body, s, ROW_TILE, ins,
                 [("blk", (s, d), F32), ("blk", (s, d), MXU_DTYPE), ("all", (SUBLANES, d), F32), ("all", (SUBLANES, d), F32)], name,
                 scratch=[pltpu.VMEM((SUBLANES, d), F32)] * 5)


def _left_half(shape):
    return lax.broadcasted_iota(jnp.int32, shape, 1) < (LANES // 2)


def _spatial_z(vn, wc_ref, bias_ref, j):
    vb = vn[:, j * LANES:(j + 1) * LANES]
    z0 = _dot_nn(wc_ref[2 * j], vb)
    z1 = _dot_nn(wc_ref[2 * j + 1], vb)
    return jnp.where(_left_half(z0.shape), z0, z1) + bias_ref[:, j * LANES:(j + 1) * LANES]


def _spatial_fwd(uvpre, vn_g, vn_b, wc, bias_full, name):
    s, d2 = uvpre.shape
    d = d2 // 2

    def body(uv_ref, g_ref, b_ref, wc_ref, bias_ref, out_ref):
        u = _gelu(uv_ref[:, :d])
        v = _gelu(uv_ref[:, d:])
        vh, _ = _ln_stats(v)
        vn = vh * g_ref[...] + b_ref[...]
        for j in range(d // LANES):
            z = _spatial_z(vn, wc_ref, bias_ref, j)
            out_ref[:, j * LANES:(j + 1) * LANES] = (u[:, j * LANES:(j + 1) * LANES] * z).astype(out_ref.dtype)

    return _rows(body, s, CHUNK, [("blk", uvpre), ("all", vn_g), ("all", vn_b), ("all", wc), ("all", bias_full)],
                 [("blk", (s, d), MXU_DTYPE)], name)[0]


def _spatial_bwd(uvpre, dgated, vn_g, vn_b, wc, wct, bias_full, name):
    s, d2 = uvpre.shape
    d = d2 // 2

    def body(uv_ref, dg_ref, g_ref, b_ref, wc_ref, wct_ref, bias_ref,
             duv_ref, dws_ref, dbias_ref, dbin_ref, dvg_ref, dvb_ref, dvn_buf, a_bin, a_vg, a_vb):
        i = pl.program_id(0)

        @pl.when(i == 0)
        def _():
            dws_ref[...] = jnp.zeros_like(dws_ref)
            dbias_ref[...] = jnp.zeros_like(dbias_ref)
            a_bin[...] = jnp.zeros_like(a_bin)
            a_vg[...] = jnp.zeros_like(a_vg)
            a_vb[...] = jnp.zeros_like(a_vb)

        up = uv_ref[:, :d]
        vp = uv_ref[:, d:]
        u = _gelu(up)
        v = _gelu(vp)
        vh, rstd = _ln_stats(v)
        vn = vh * g_ref[...] + b_ref[...]
        dg = dg_ref[...]
        dzz = dg * u
        dbias_ref[...] += dzz
        for j in range(d // LANES):
            cols = slice(j * LANES, (j + 1) * LANES)
            z = _spatial_z(vn, wc_ref, bias_ref, j)
            dup = dg[:, cols] * z * _gelu_grad(up[:, cols])
            duv_ref[:, cols] = dup.astype(duv_ref.dtype)
            a_bin[:, cols] += _fold8(dup)
            dzb = dzz[:, cols]
            left = _left_half(dzb.shape)
            dvn_buf[:, cols] = jnp.where(left, _dot_nn(wct_ref[2 * j], dzb), _dot_nn(wct_ref[2 * j + 1], dzb))
            vb = vn[:, cols]
            dws_ref[2 * j] += _dot_nt(jnp.where(left, dzb, 0.0), vb)
            dws_ref[2 * j + 1] += _dot_nt(jnp.where(left, 0.0, dzb), vb)
        dvn = dvn_buf[...]
        a_vg[...] += _fold8(dvn * vh)
        a_vb[...] += _fold8(dvn)
        dvh = dvn * g_ref[...]
        dv = rstd * (dvh - jnp.mean(dvh, axis=-1, keepdims=True) - vh * jnp.mean(dvh * vh, axis=-1, keepdims=True))
        dvp = dv * _gelu_grad(vp)
        duv_ref[:, d:] = dvp.astype(duv_ref.dtype)
        a_bin[:, d:] += _fold8(dvp)

        @pl.when(i == pl.num_programs(0) - 1)
        def _():
            dbin_ref[...] = jnp.sum(a_bin[...], axis=0, keepdims=True)
            dvg_ref[...] = jnp.sum(a_vg[...], axis=0, keepdims=True)
            dvb_ref[...] = jnp.sum(a_vb[...], axis=0, keepdims=True)

    return _rows(body, s, CHUNK,
                 [("blk", uvpre), ("blk", dgated), ("all", vn_g), ("all", vn_b), ("all", wc), ("all", wct), ("all", bias_full)],
                 [("blk", (s, d2), MXU_DTYPE), ("all", (A_GROUPS, CHUNK, CHUNK), F32), ("all", (CHUNK, d), F32),
                  ("all", (1, d2), F32), ("all", (1, d), F32), ("all", (1, d), F32)], name,
                 scratch=[pltpu.VMEM((CHUNK, d), F32), pltpu.VMEM((SUBLANES, d2), F32),
                          pltpu.VMEM((SUBLANES, d), F32), pltpu.VMEM((SUBLANES, d), F32)])


def _head_mask(v, h):
    lane = lax.broadcasted_iota(jnp.int32, v.shape, 1)
    return jnp.where((lane >= h * HEAD_DIM) & (lane < (h + 1) * HEAD_DIM), v, jnp.zeros_like(v))


def _att_bias(slopes, dil):
    qi = lax.broadcasted_iota(jnp.int32, (SPAN, SPAN), 0)
    ki = lax.broadcasted_iota(jnp.int32, (SPAN, SPAN), 1)
    sl = slopes[:, None, None]
    cur = jnp.where(ki <= qi, -sl * (float(dil) * (qi - ki).astype(F32)), NEG)
    prev = jnp.where(ki >= qi, -sl * (float(dil) * (SPAN + qi - ki).astype(F32)), NEG)
    absent = jnp.full_like(prev, NEG)
    pairs = slopes.shape[0] // 2

    def fwd(pv):
        return jnp.concatenate([cur, pv], axis=2).reshape(pairs, 2 * SPAN, 2 * SPAN)

    def bwd(pv):
        return jnp.concatenate([cur.reshape(pairs, 2 * SPAN, SPAN), pv.reshape(pairs, 2 * SPAN, SPAN)], axis=1)

    return jnp.stack([fwd(absent), fwd(prev)]), jnp.stack([bwd(absent), bwd(prev)])


def _att_specs(s, d, dil, kinds):
    nb = s // (dil * SPAN)

    def rowblk(which, b):
        if which == "prev":
            return jnp.where(b % nb == 0, b, b - 1)
        if which == "next":
            return jnp.where(b % nb == nb - 1, b, b + 1)
        return b

    return [pl.BlockSpec((SPAN, d), functools.partial(lambda b, o, w: (rowblk(w, b), o), o=part, w=which))
            for part, which in kinds]


def _lane_col(v, h):
    return v[:, h * HEAD_DIM:h * HEAD_DIM + 1]


def _attn_fwd(qkv, slopes, dil, name):
    s, d3 = qkv.shape
    d = d3 // 3
    nb = s // (dil * SPAN)
    table, _ = _att_bias(slopes, dil)

    def body(q_ref, kc_ref, kp_ref, vc_ref, vp_ref, tb_ref, o_ref, l_ref):
        left = _left_half((SPAN, LANES))
        for hp in range(d // LANES):
            cols = slice(hp * LANES, (hp + 1) * LANES)
            q = q_ref[:, cols]
            q2 = jnp.concatenate([_head_mask(q, 0), _head_mask(q, 1)], axis=0) * ATT_SCALE
            k2 = jnp.concatenate([kc_ref[:, cols], kp_ref[:, cols]], axis=0)
            v2 = jnp.concatenate([vc_ref[:, cols], vp_ref[:, cols]], axis=0)
            sc = _dot_nt(q2, k2) + tb_ref[hp]
            m = jnp.max(sc, axis=-1, keepdims=True)
            p = jnp.exp(sc - m)
            l = jnp.sum(p, axis=-1, keepdims=True)
            r = _dot_nn(p, v2) * (1.0 / l)
            lse = jnp.broadcast_to(m + jnp.log(l), (2 * SPAN, LANES))
            o_ref[:, cols] = jnp.where(left, r[:SPAN], r[SPAN:])
            l_ref[:, cols] = jnp.where(left, lse[:SPAN], lse[SPAN:])

    specs = _att_specs(s, d, dil, [(0, "cur"), (1, "cur"), (1, "prev"), (2, "cur"), (2, "prev")])
    tbl = pl.BlockSpec((None,) + table.shape[1:], lambda b: (jnp.where(b % nb == 0, 0, 1), 0, 0, 0))
    out_spec = pl.BlockSpec((SPAN, d), lambda b: (b, 0))
    return pl.pallas_call(
        body,
        grid=(s // SPAN,),
        in_specs=specs + [tbl],
        out_specs=[out_spec, out_spec],
        out_shape=[jax.ShapeDtypeStruct((s, d), F32)] * 2,
        name=name,
        compiler_params=_cparams(("parallel",)),
    )(qkv, qkv, qkv, qkv, qkv, table)


def _attn_bwd(qkv, do, lse, dd, slopes, dil, name):
    s, d3 = qkv.shape
    d = d3 // 3
    nb = s // (dil * SPAN)
    _, table = _att_bias(slopes, dil)

    def heads_stacked(cur, nxt):
        return jnp.concatenate([_head_mask(cur, 0), _head_mask(cur, 1), _head_mask(nxt, 0), _head_mask(nxt, 1)], axis=0)

    def cols_stacked(cur, nxt):
        return jnp.concatenate([jnp.broadcast_to(_lane_col(a, h), (SPAN, LANES)) for a in (cur, nxt) for h in range(2)], axis=0)

    def body(k_ref, v_ref, qc_ref, qn_ref, doc_ref, don_ref, lc_ref, ln_ref, ddc_ref, ddn_ref, tb_ref, out_ref, carry):
        b = pl.program_id(0)

        @pl.when(b == 0)
        def _():
            carry[...] = jnp.zeros_like(carry)

        left = _left_half((SPAN, LANES))
        for hp in range(d // LANES):
            cols = slice(hp * LANES, (hp + 1) * LANES)
            k, v = k_ref[:, cols], v_ref[:, cols]
            q4 = heads_stacked(qc_ref[:, cols], qn_ref[:, cols])
            do4 = heads_stacked(doc_ref[:, cols], don_ref[:, cols])
            sc = _dot_nt(q4 * ATT_SCALE, k) + tb_ref[hp]
            p = jnp.exp(sc - cols_stacked(lc_ref[:, cols], ln_ref[:, cols]))
            ds = p * (_dot_nt(do4, v) - cols_stacked(ddc_ref[:, cols], ddn_ref[:, cols]))
            dq4 = _dot_nn(ds, k)
            dq_cur = jnp.where(left, dq4[:SPAN], dq4[SPAN:2 * SPAN]) + carry[:, cols]
            carry[:, cols] = jnp.where(left, dq4[2 * SPAN:3 * SPAN], dq4[3 * SPAN:])
            out_ref[:, cols] = (dq_cur * ATT_SCALE).astype(out_ref.dtype)
            out_ref[:, d + hp * LANES:d + (hp + 1) * LANES] = (_dot_tn(ds, q4) * ATT_SCALE).astype(out_ref.dtype)
            out_ref[:, 2 * d + hp * LANES:2 * d + (hp + 1) * LANES] = _dot_tn(p, do4).astype(out_ref.dtype)

    qkv_specs = _att_specs(s, d, dil, [(1, "cur"), (2, "cur"), (0, "cur"), (0, "next")])
    pair = _att_specs(s, d, dil, [(0, "cur"), (0, "next")])
    tbl = pl.BlockSpec((None,) + table.shape[1:], lambda b: (jnp.where(b % nb == nb - 1, 0, 1), 0, 0, 0))
    return pl.pallas_call(
        body,
        grid=(s // SPAN,),
        in_specs=qkv_specs + pair + pair + pair + [tbl],
        out_specs=pl.BlockSpec((SPAN, d3), lambda b: (b, 0)),
        out_shape=jax.ShapeDtypeStruct((s, d3), MXU_DTYPE),
        scratch_shapes=[pltpu.VMEM((SPAN, d), F32)],
        name=name,
        compiler_params=_cparams(("arbitrary",)),
    )(qkv, qkv, qkv, qkv, do, do, lse, lse, dd, dd, table)


def _mix_weights(l_refs):
    ls = [r[...] for r in l_refs]
    m = functools.reduce(jnp.maximum, ls)
    es = [jnp.exp(l - m) for l in ls]
    tot = functools.reduce(lambda a, c: a + c, es)
    return [e / tot for e in es]


def _combine_fwd(os_, ls_, name):
    s, d = os_[0].shape
    n = len(os_)

    def body(*refs):
        o_refs, l_refs, out_ref = refs[:n], refs[n:2 * n], refs[2 * n]
        ws = _mix_weights(l_refs)
        acc = ws[0] * o_refs[0][...]
        for w, o in zip(ws[1:], o_refs[1:]):
            acc = acc + w * o[...]
        out_ref[...] = acc

    return _rows(body, s, ROW_TILE, [("blk", a) for a in os_ + ls_], [("blk", (s, d), F32)], name)[0]


def _combine_bwd(do, o, ls_, name):
    s, d = o.shape
    n = len(ls_)
    ri = lax.broadcasted_iota(jnp.int32, (LANES, LANES), 0) // HEAD_DIM
    ci = lax.broadcasted_iota(jnp.int32, (LANES, LANES), 1) // HEAD_DIM
    seg = (ri == ci).astype(F32)

    def body(do_ref, o_ref, *rest):
        l_refs, seg_ref, outs = rest[:n], rest[n], rest[n + 1:]
        ws = _mix_weights(l_refs)
        dov = do_ref[...]
        prod = dov * o_ref[...]
        for j in range(d // LANES):
            cols = slice(j * LANES, (j + 1) * LANES)
            r = jnp.dot(prod[:, cols], seg_ref[...], precision=lax.Precision.HIGHEST, preferred_element_type=F32)
            for g in range(n):
                outs[2 * g][:, cols] = (ws[g][:, cols] * dov[:, cols]).astype(outs[2 * g].dtype)
                outs[2 * g + 1][:, cols] = ws[g][:, cols] * r

    outs = []
    for _ in range(n):
        outs += [("blk", (s, d), MXU_DTYPE), ("blk", (s, d), F32)]
    res = _rows(body, s, ROW_TILE, [("blk", do), ("blk", o)] + [("blk", l) for l in ls_] + [("all", seg)], outs, name)
    return [(res[2 * g], res[2 * g + 1]) for g in range(n)]


def _ada_fwd(c_all, w, b, name):
    nsub, d, cs = w.shape

    def body(c_ref, w_ref, b_ref, o_ref):
        cv = c_ref[...]
        sc = cv * (1.0 / (1.0 + jnp.exp(-cv)))
        o_ref[...] = _dot_nn(sc, w_ref[...]) + b_ref[...]

    return pl.pallas_call(
        body,
        grid=(nsub,),
        in_specs=[pl.BlockSpec(c_all.shape, lambda i: (0, 0)), pl.BlockSpec((None, d, cs), lambda i: (i, 0, 0)),
                  pl.BlockSpec((None, 1, cs), lambda i: (i, 0, 0))],
        out_specs=pl.BlockSpec((None, N_DEV, cs), lambda i: (i, 0, 0)),
        out_shape=jax.ShapeDtypeStruct((nsub, N_DEV, cs), F32),
        name=name,
        compiler_params=_cparams(("parallel",)),
    )(c_all, w, b)


def _ada_bwd(c_all_t, dm, name):
    d, nb = c_all_t.shape
    nsub, _, cs = dm.shape

    def body(c_ref, dm_ref, o_ref):
        cv = c_ref[...]
        sc = cv * (1.0 / (1.0 + jnp.exp(-cv)))
        acc = sc[:, 0:1] * dm_ref[0:1, :]
        for bi in range(1, nb):
            acc = acc + sc[:, bi:bi + 1] * dm_ref[bi:bi + 1, :]
        o_ref[...] = acc

    return pl.pallas_call(
        body,
        grid=(nsub,),
        in_specs=[pl.BlockSpec(c_all_t.shape, lambda i: (0, 0)), pl.BlockSpec((None, nb, cs), lambda i: (i, 0, 0))],
        out_specs=pl.BlockSpec((None, d, cs), lambda i: (i, 0, 0)),
        out_shape=jax.ShapeDtypeStruct((nsub, d, cs), F32),
        name=name,
        compiler_params=_cparams(("parallel",)),
    )(c_all_t, dm)


def _row_tile(r, row_elems):
    t = 2 * SUBLANES
    if r % t:
        return r
    while t * 2 * row_elems <= 256 * 1024 and r % (t * 2) == 0:
        t *= 2
    return t


def _adamw(w, g, m, v, name):
    shape = w.shape
    c = shape[-1]
    r = w.size // c
    tr = _row_tile(r, c)
    w2, g2, m2, v2 = [a.reshape(r, c) for a in (w, g, m, v)]
    bc1 = 1.0 - ADAM_B1 ** ADAM_STEP
    bc2 = 1.0 - ADAM_B2 ** ADAM_STEP

    def body(w_ref, g_ref, m_ref, v_ref, d_ref, nm_ref, nv_ref):
        gv = g_ref[...]
        nm = ADAM_B1 * m_ref[...] + (1.0 - ADAM_B1) * gv
        nv = ADAM_B2 * v_ref[...] + (1.0 - ADAM_B2) * (gv * gv)
        d_ref[...] = -ADAM_LR * ((nm / bc1) / (jnp.sqrt(nv / bc2) + ADAM_EPS) + ADAM_WD * w_ref[...])
        nm_ref[...] = nm
        nv_ref[...] = nv

    res = _rows(body, r, tr, [("blk", a) for a in (w2, g2, m2, v2)], [("blk", (r, c), F32)] * 3, name)
    return [a.reshape(shape) for a in res]


def _sum_slots(buf, name):
    n, r, c = buf.shape
    tr = _row_tile(r, n * c)

    def body(b_ref, o_ref):
        acc = b_ref[0].astype(F32)
        for k in range(1, n):
            acc = acc + b_ref[k].astype(F32)
        o_ref[...] = acc

    return pl.pallas_call(
        body,
        grid=(r // tr,),
        in_specs=[pl.BlockSpec((n, tr, c), lambda i: (0, i, 0))],
        out_specs=pl.BlockSpec((tr, c), lambda i: (i, 0)),
        out_shape=jax.ShapeDtypeStruct((r, c), F32),
        name=name,
        compiler_params=_cparams(("parallel",)),
    )(buf)


def _me():
    return lax.axis_index("x"), lax.axis_index("y"), lax.axis_index("c")


def _all_gather_small(blk, name, after=()):
    m_per, n = blk.shape

    def body(x_ref, *rest):
        out_ref, send_sems, recv_sems, local_sem = rest[len(after):]
        x, y, c = _me()
        me, sibling = (x, y, c), (x, y, 1 - c)
        chips = [(1 - x, y), (x, 1 - y), (1 - x, 1 - y)]

        def rows(px, py, pc):
            return out_ref.at[pl.ds((4 * px + 2 * py + pc) * m_per, m_per), :]

        def copy(k, block, to, src=None):
            return pltpu.make_async_remote_copy(
                src_ref=rows(*block) if src is None else src, dst_ref=rows(*block),
                send_sem=send_sems.at[k], recv_sem=recv_sems.at[k], device_id=to, device_id_type=MESH)

        mine = pltpu.make_async_copy(x_ref, rows(*me), local_sem)
        mine.start()
        first = [copy(0, me, sibling, src=x_ref)]
        first += [copy(1 + j, me, (*chip, c), src=x_ref) for j, chip in enumerate(chips)]
        for cp in first:
            cp.start()
        passed = [copy(4 + j, (*chip, c), sibling) for j, chip in enumerate(chips)]
        for j, chip in enumerate(chips):
            copy(1 + j, (*chip, c), me).wait_recv()
            passed[j].start()
        copy(0, sibling, me).wait_recv()
        for j, chip in enumerate(chips):
            copy(4 + j, (*chip, 1 - c), me).wait_recv()
        for cp in first + passed:
            cp.wait_send()
        mine.wait()

    return pl.pallas_call(
        body,
        out_shape=jax.ShapeDtypeStruct((N_DEV * m_per, n), blk.dtype),
        in_specs=[pl.BlockSpec(memory_space=pltpu.VMEM)] + [pl.BlockSpec(memory_space=pl.ANY)] * len(after),
        out_specs=pl.BlockSpec(memory_space=pltpu.VMEM),
        scratch_shapes=[pltpu.SemaphoreType.DMA((7,)), pltpu.SemaphoreType.DMA((7,)), pltpu.SemaphoreType.DMA],
        name=name,
        compiler_params=pltpu.CompilerParams(vmem_limit_bytes=VMEM_LIMIT),
    )(blk, *after)


_HBM = pl.BlockSpec(memory_space=pltpu.HBM)
_SEM = pl.BlockSpec(memory_space=pltpu.SEMAPHORE)
_EFFECT = pltpu.SideEffectType.DATAFLOW_SIDE_EFFECTING


def _other_chips(x, y):
    return [(1 - x, y), (x, 1 - y), (1 - x, 1 - y)]


def _gather_copy(w, j, src_ref, land_ref, send_sems, recv_sems, halved=False):
    x, y, c = _me()
    if halved:
        half = src_ref.shape[0] // 2
        src_ref = src_ref.at[pl.ds(c * half, half), :]
    return pltpu.make_async_remote_copy(
        src_ref=src_ref, dst_ref=land_ref.at[2 * x + y], send_sem=send_sems.at[3 * w + j], recv_sem=recv_sems.at[3 * w + j],
        device_id=(*_other_chips(x, y)[j], c), device_id_type=MESH)


def _gather_start(shards, halved, after, name):
    n = len(shards)
    lands = [lax.empty((N_CHIPS, s.shape[0] // 2 if w in halved else s.shape[0], s.shape[1]), s.dtype) for w, s in enumerate(shards)]

    def body(*refs):
        in_refs, land_refs = refs[:n], refs[n:2 * n]
        send_sems, recv_sems = refs[2 * n + 1], refs[2 * n + 2]
        token = refs[-1]
        for w in range(n):
            for j in range(3):
                _gather_copy(w, j, in_refs[w], land_refs[w], send_sems, recv_sems, w in halved).start()
        token[...] = jnp.zeros_like(token)

    res = pl.pallas_call(
        body,
        out_shape=(pltpu.SemaphoreType.DMA((3 * n,)), pltpu.SemaphoreType.DMA((3 * n,)),
                   *[pltpu.HBM(s.shape, s.dtype) for s in shards], *[pltpu.HBM(l.shape, l.dtype) for l in lands],
                   jax.ShapeDtypeStruct((SUBLANES, LANES), F32)),
        in_specs=[_HBM] * (2 * n) + [pl.BlockSpec(memory_space=pl.ANY)],
        out_specs=(_SEM, _SEM, *[_HBM] * (2 * n), pl.BlockSpec(memory_space=pltpu.VMEM)),
        input_output_aliases={i: 2 + i for i in range(2 * n)},
        name=name,
        compiler_params=pltpu.CompilerParams(has_side_effects=_EFFECT),
    )(*[pltpu.with_memory_space_constraint(a, pltpu.HBM) for a in list(shards) + lands], after)
    return res[0], res[1], res[2:2 + n], res[2 + n:2 + 2 * n], res[-1]


def _gather_wait(w, shard, land, send_sems, recv_sems, after, name, halved=False):
    def body(s_ref, land_ref, send_sems, recv_sems, after_ref, s_out, land_out, stage):
        x, y, _ = _me()
        if not halved:
            pltpu.sync_copy(s_ref, stage)
            pltpu.sync_copy(stage, land_out.at[2 * x + y])
        for j in range(3):
            cp = _gather_copy(w, j, s_ref, land_ref, send_sems, recv_sems, halved)
            cp.wait_send()
            cp.wait_recv()

    return pl.pallas_call(
        body,
        out_shape=(pltpu.HBM(shard.shape, shard.dtype), pltpu.HBM(land.shape, land.dtype)),
        in_specs=(_HBM, _HBM, _SEM, _SEM, pl.BlockSpec(memory_space=pl.ANY)),
        out_specs=(_HBM, _HBM),
        input_output_aliases={0: 0, 1: 1},
        scratch_shapes=[pltpu.VMEM((SUBLANES, LANES) if halved else shard.shape, shard.dtype)],
        name=name,
        compiler_params=pltpu.CompilerParams(has_side_effects=_EFFECT, vmem_limit_bytes=VMEM_LIMIT),
    )(shard, land, send_sems, recv_sems, after)


def _assemble_halves(shard, land, name):
    half = land.shape[1]

    def body(s_ref, land_ref, out_ref, send_sems, recv_sems, local_sems):
        x, y, c = _me()
        own = pltpu.make_async_copy(s_ref, out_ref.at[2 * x + y], local_sems.at[3])
        own.start()
        cps = []
        for j, (ox, oy) in enumerate(_other_chips(x, y)):
            qj = 2 * ox + oy
            mine = out_ref.at[qj, pl.ds(c * half, half), :]
            lc = pltpu.make_async_copy(land_ref.at[qj], mine, local_sems.at[j])
            lc.start()
            rc = pltpu.make_async_remote_copy(
                src_ref=land_ref.at[qj], dst_ref=mine, send_sem=send_sems.at[j], recv_sem=recv_sems.at[j],
                device_id=(x, y, 1 - c), device_id_type=MESH)
            rc.start()
            cps.append((lc, rc))
        for lc, rc in cps:
            rc.wait_recv()
        for lc, rc in cps:
            rc.wait_send()
            lc.wait()
        own.wait()

    vmem = pl.BlockSpec(memory_space=pltpu.VMEM)
    return pl.pallas_call(
        body,
        out_shape=jax.ShapeDtypeStruct((N_CHIPS,) + shard.shape, shard.dtype),
        in_specs=[vmem, vmem],
        out_specs=vmem,
        scratch_shapes=[pltpu.SemaphoreType.DMA((3,)), pltpu.SemaphoreType.DMA((3,)), pltpu.SemaphoreType.DMA((4,))],
        name=name,
        compiler_params=pltpu.CompilerParams(vmem_limit_bytes=VMEM_LIMIT),
    )(shard, land)


def _piece_shape(shape, kind):
    k, nn = shape
    return (k // 2, nn // N_CHIPS) if kind == "col" else (k // N_CHIPS // 2, nn)


def _piece_of(g_ref, kind, tq, tc):
    pr, pc = _piece_shape(g_ref.shape, kind)
    if kind == "col":
        return g_ref.at[pl.ds(tc * pr, pr), pl.ds(tq * pc, pc)]
    return g_ref.at[pl.ds((2 * tq + tc) * pr, pr), :]


def _scatter_copy(w, r, kind, g_ref, land_ref, send_sems, recv_sems):
    x, y, c = _me()
    tx, ty, tc = (x + ((r >> 2) & 1)) % 2, (y + ((r >> 1) & 1)) % 2, (c + (r & 1)) % 2
    return pltpu.make_async_remote_copy(
        src_ref=_piece_of(g_ref, kind, 2 * tx + ty, tc), dst_ref=land_ref.at[4 * x + 2 * y + c],
        send_sem=send_sems.at[N_DEV * w + r], recv_sem=recv_sems.at[N_DEV * w + r], device_id=(tx, ty, tc), device_id_type=MESH)


def _scatter_start(gs, kinds, name):
    n = len(gs)
    pieces = [_piece_shape(g.shape, kind) for g, kind in zip(gs, kinds)]
    lands = [lax.empty((N_DEV,) + p, g.dtype) for p, g in zip(pieces, gs)]

    def body(*refs):
        g_refs, land_refs, send_sems, recv_sems = refs[:n], refs[n:2 * n], refs[2 * n], refs[2 * n + 1]
        land_outs, stages = refs[3 * n + 2:4 * n + 2], refs[4 * n + 2:]
        x, y, c = _me()
        for w in range(n):
            for r in range(1, N_DEV):
                _scatter_copy(w, r, kinds[w], g_refs[w], land_refs[w], send_sems, recv_sems).start()
        for w in range(n):
            pltpu.sync_copy(_piece_of(g_refs[w], kinds[w], 2 * x + y, c), stages[w])
            pltpu.sync_copy(stages[w], land_outs[w].at[4 * x + 2 * y + c])

    arrays = list(gs) + lands
    res = pl.pallas_call(
        body,
        out_shape=(pltpu.SemaphoreType.DMA((N_DEV * n,)), pltpu.SemaphoreType.DMA((N_DEV * n,)),
                   *[pltpu.HBM(a.shape, a.dtype) for a in arrays]),
        in_specs=[_HBM] * (2 * n),
        out_specs=(_SEM, _SEM, *[_HBM] * (2 * n)),
        input_output_aliases={i: 2 + i for i in range(2 * n)},
        scratch_shapes=[pltpu.VMEM(p, g.dtype) for p, g in zip(pieces, gs)],
        name=name,
        compiler_params=pltpu.CompilerParams(has_side_effects=_EFFECT, vmem_limit_bytes=VMEM_LIMIT),
    )(*[pltpu.with_memory_space_constraint(a, pltpu.HBM) for a in arrays])
    return res[0], res[1], res[2:2 + n], res[2 + n:]


def _scatter_wait(send_sems, recv_sems, gs, lands, kinds, after, name):
    n = len(gs)

    def body(*refs):
        g_refs, land_refs, send_sems, recv_sems = refs[:n], refs[n:2 * n], refs[2 * n], refs[2 * n + 1]
        for w in range(n):
            for r in range(1, N_DEV):
                cp = _scatter_copy(w, r, kinds[w], g_refs[w], land_refs[w], send_sems, recv_sems)
                cp.wait_send()
                cp.wait_recv()

    arrays = list(gs) + list(lands)
    return pl.pallas_call(
        body,
        out_shape=tuple(pltpu.HBM(a.shape, a.dtype) for a in arrays),
        in_specs=(*[_HBM] * (2 * n), _SEM, _SEM, pl.BlockSpec(memory_space=pl.ANY)),
        out_specs=tuple([_HBM] * (2 * n)),
        input_output_aliases={i: i for i in range(2 * n)},
        name=name,
        compiler_params=pltpu.CompilerParams(has_side_effects=_EFFECT),
    )(*arrays, send_sems, recv_sems, after)[n:]


def _swap_halves(halves, name):
    n = len(halves)

    def body(*refs):
        in_refs, out_refs = refs[:n], refs[n:2 * n]
        send_sems, recv_sems, local_sems = refs[2 * n:]
        x, y, c = _me()
        cps = []
        for w in range(n):
            lc = pltpu.make_async_copy(in_refs[w], out_refs[w].at[c], local_sems.at[w])
            lc.start()
            rc = pltpu.make_async_remote_copy(
                src_ref=in_refs[w], dst_ref=out_refs[w].at[c], send_sem=send_sems.at[w], recv_sem=recv_sems.at[w],
                device_id=(x, y, 1 - c), device_id_type=MESH)
            rc.start()
            cps.append((lc, rc))
        for lc, rc in cps:
            rc.wait_recv()
        for lc, rc in cps:
            rc.wait_send()
            lc.wait()

    vmem = pl.BlockSpec(memory_space=pltpu.VMEM)
    return pl.pallas_call(
        body,
        out_shape=[jax.ShapeDtypeStruct((2,) + h.shape, h.dtype) for h in halves],
        in_specs=[vmem] * n,
        out_specs=[vmem] * n,
        scratch_shapes=[pltpu.SemaphoreType.DMA((n,)), pltpu.SemaphoreType.DMA((n,)), pltpu.SemaphoreType.DMA((n,))],
        name=name,
        compiler_params=pltpu.CompilerParams(vmem_limit_bytes=VMEM_LIMIT),
    )(*halves)


def _to_streams(a, dil):
    if dil == 1:
        return a
    s, c = a.shape
    return a.reshape(s // dil, dil, c).transpose(1, 0, 2).reshape(s, c)


def _from_streams(a, dil):
    if dil == 1:
        return a
    s, c = a.shape
    return a.reshape(dil, s // dil, c).transpose(1, 0, 2).reshape(s, c)


def _mm_tiles(s):
    return min(s, 1024)


def _local_step(x0, target, mvec, ln_g, ln_b, small, fetch, emit, start):
    s, d = x0.shape
    tm = _mm_tiles(s)
    row = lambda v: v.reshape(1, -1)
    shift = [row(mvec[i, :d]) for i in range(4)]
    scale = [row(mvec[i, d:2 * d]) for i in range(4)]
    gate = [row(1.0 + mvec[i, 2 * d:]) for i in range(4)]
    lg = [row(ln_g[i]) for i in range(4)]
    lb = [row(ln_b[i]) for i in range(4)]
    mm = functools.partial(_mm, tm=tm)
    mm_w = functools.partial(_mm, tm=1024, tk=min(s, 2048), mode="tn")

    xs, ys, big = [x0], [], {}
    h0 = _mod(x0, scale[0], shift[0], start, "mod0")
    big["a_w_in"] = fetch("a_w_in", h0)
    uvpre = mm(h0, big["a_w_in"], mode="nn", name="a_in", outs=[F32], tn=512, tk=1024,
               epi=lambda r, bias: [r + bias], extras=[("row", small["a_b_in"])])
    gated = _spatial_fwd(uvpre, small["a_vn_g"], small["a_vn_b"], small["wc"], small["bias_full"], "a_spatial")
    big["a_w_out"] = fetch("a_w_out", gated)
    ys.append(mm(gated, big["a_w_out"], mode="nn", name="a_out", outs=[F32], tn=1024, tk=1024))
    x1, h1 = _resid_ln(xs[0], ys[0], gate[0], lg[0], lb[0], (scale[1], shift[1]), "ln0")
    xs.append(x1)
    relu2 = lambda r: [jnp.square(jnp.maximum(r, 0.0))]
    big["up0"] = fetch("up0", h1)
    r0 = mm(h1, big["up0"], mode="nn", name="up0", outs=[MXU_DTYPE], tn=1024, tk=1024, epi=relu2)
    big["down0"] = fetch("down0", r0)
    ys.append(mm(r0, big["down0"], mode="nn", name="down0", outs=[F32], tn=1024, tk=2048))
    x2, h2 = _resid_ln(xs[1], ys[1], gate[1], lg[1], lb[1], (scale[2], shift[2]), "ln1")
    xs.append(x2)
    hg, qkvs, o_g, l_g, l_streams = [], [], [], [], []
    big["b_w_qkv"] = fetch("b_w_qkv", h2)
    for g, (_, dil) in enumerate(B_PATTERNS):
        hp = _to_streams(h2, dil)
        qkv = mm(hp, big["b_w_qkv"], mode="nn", name=f"qkv{g}", outs=[MXU_DTYPE], tn=768, tk=1024, b_col0=g * 3 * d, n_out=3 * d)
        og, lgv = _attn_fwd(qkv, small["slopes"], dil, f"attn_fwd{g}")
        hg.append(hp)
        qkvs.append(qkv)
        o_g.append(_from_streams(og, dil))
        l_g.append(_from_streams(lgv, dil))
        l_streams.append(lgv)
    o_mix = _combine_fwd(o_g, l_g, "combine")
    big["b_w_out"] = fetch("b_w_out", o_mix)
    ys.append(mm(o_mix, big["b_w_out"], mode="nn", name="b_out", outs=[F32], tn=1024, tk=1024))
    x3, h3 = _resid_ln(xs[2], ys[2], gate[2], lg[2], lb[2], (scale[3], shift[3]), "ln2")
    xs.append(x3)
    big["up1"] = fetch("up1", h3)
    r1 = mm(h3, big["up1"], mode="nn", name="up1", outs=[MXU_DTYPE], tn=1024, tk=1024, epi=relu2)
    big["down1"] = fetch("down1", r1)
    ys.append(mm(r1, big["down1"], mode="nn", name="down1", outs=[F32], tn=1024, tk=2048))

    gb, red_ln, red_mod = {}, [None] * 4, [None] * 4

    def mlp_bwd(i, h, r, dyy):
        gb[f"down{i}"] = mm_w(r, dyy, name=f"g_down{i}", outs=[MXU_DTYPE], tn=1024)
        da = mm(dyy, big[f"down{i}"], mode="nt", name=f"d_down{i}", outs=[MXU_DTYPE], tn=1024, tk=1024,
                after=emit(f"down{i}", gb[f"down{i}"]),
                epi=lambda acc, rv: [acc * (2.0 * jnp.sqrt(rv.astype(F32)))], extras=[("full", r)])
        gb[f"up{i}"] = mm_w(h, da, name=f"g_up{i}", outs=[MXU_DTYPE], tn=1024)
        return [mm(da, big[f"up{i}"], mode="nt", name=f"d_up{i}", outs=[F32], tn=1024, tk=1024, after=emit(f"up{i}", gb[f"up{i}"]))]

    def join(sub, dxr, dhs, after=None):
        res = _mod_ln_bwd(dxr, dhs, xs[sub], scale[sub], xs[sub - 1], ys[sub - 1], gate[sub - 1], lg[sub - 1],
                          f"mod_ln_bwd{sub}", after=after)
        red_mod[sub], red_ln[sub - 1] = res[2], res[3]
        return res[0], res[1]

    loss, dxr, dyy, red_ln[3] = _last_ln_loss_bwd(xs[3], ys[3], gate[3], lg[3], lb[3], target, "ln3_loss_bwd")
    dxr, dyy = join(3, dxr, mlp_bwd(1, h3, r1, dyy))
    gb["b_w_out"] = mm_w(o_mix, dyy, name="g_b_out", outs=[MXU_DTYPE], tn=1024, tk=1024)
    do = mm(dyy, big["b_w_out"], mode="nt", name="d_b_out", outs=[F32], tn=1024, tk=1024, after=emit("b_w_out", gb["b_w_out"]))
    parts = _combine_bwd(do, o_mix, l_g, "combine_bwd")
    dhs, gq = [], None
    for g, (_, dil) in enumerate(B_PATTERNS):
        do_g, dd_g = _to_streams(parts[g][0], dil), _to_streams(parts[g][1], dil)
        dqkv = _attn_bwd(qkvs[g], do_g, l_streams[g], dd_g, small["slopes"], dil, f"attn_bwd{g}")
        gq = mm_w(hg[g], dqkv, name=f"g_qkv{g}", outs=[MXU_DTYPE], tn=1024, out_col0=g * 3 * d, out_cols=len(B_PATTERNS) * 3 * d, into=gq)
        dh = mm(dqkv, big["b_w_qkv"], mode="nt", name=f"d_qkv{g}", outs=[F32], tn=1024, tk=768, b_col0=g * 3 * d)
        dhs.append(_from_streams(dh, dil))
    gb["b_w_qkv"] = gq
    dxr, dyy = join(2, dxr, dhs, after=emit("b_w_qkv", gb["b_w_qkv"]))
    dxr, dyy = join(1, dxr, mlp_bwd(0, h1, r0, dyy))
    gb["a_w_out"] = mm_w(gated, dyy, name="g_a_out", outs=[MXU_DTYPE], tn=1024)
    dgated = mm(dyy, big["a_w_out"], mode="nt", name="d_a_out", outs=[F32], tn=1024, tk=1024, after=emit("a_w_out", gb["a_w_out"]))
    duv, dws, dbias, dbin, dvg, dvb = _spatial_bwd(uvpre, dgated, small["a_vn_g"], small["a_vn_b"], small["wc"],
                                                   small["wct"], small["bias_full"], "a_spatial_bwd")
    gb["a_w_in"] = mm_w(h0, duv, name="g_a_in", outs=[MXU_DTYPE], tn=1024)
    dh = mm(duv, big["a_w_in"], mode="nt", name="d_a_in", outs=[F32], tn=1024, tk=512, after=emit("a_w_in", gb["a_w_in"]))
    dx, red_mod[0] = _mod_bwd(dxr, [dh], xs[0], scale[0], "mod_bwd0")
    dm = [jnp.concatenate([red_mod[i][0], red_mod[i][1], red_ln[i][2]]) for i in range(4)]
    dlg, dlb = [red_ln[i][0] for i in range(4)], [red_ln[i][1] for i in range(4)]

    tril = jnp.tril(jnp.ones((CHUNK, CHUNK), bool))
    gsmall = {
        "a_b_in": dbin.reshape(-1), "a_vn_g": dvg.reshape(-1), "a_vn_b": dvb.reshape(-1),
        "a_w_s": jnp.where(tril, dws, 0.0).reshape(-1),
        "a_b_s": dbias.reshape(CHUNK, A_GROUPS, d // A_GROUPS).sum(-1).T.reshape(-1),
    }
    return loss, dx, gb, jnp.stack(dm), jnp.stack(dlg), jnp.stack(dlb), gsmall


BIG = ("a_w_in", "a_w_out", "up0", "down0", "b_w_qkv", "b_w_out", "up1", "down1")
BIG_KIND = {"a_w_in": "col", "a_w_out": "row", "b_w_qkv": "col", "b_w_out": "row",
            "up0": "col", "up1": "col", "down0": "row", "down1": "row"}
HALVED = ("b_w_qkv",)
SCATTER_GROUPS = (("down1", "up1"), ("b_w_out", "b_w_qkv"), ("down0", "up0"), ("a_w_out", "a_w_in"))
SMALL = ("a_b_in", "a_vn_g", "a_vn_b", "a_b_s", "a_w_s")


def kernel(x, c, ada_w, ada_b, ln_g, ln_b, a_w_in, a_b_in, a_vn_g, a_vn_b, a_w_s, a_b_s, a_w_out, b_w_qkv, b_w_out, mlp_w_up, mlp_w_down, loss_target, m_ada_w, m_ada_b, m_ln_g, m_ln_b, m_a_w_in, m_a_b_in, m_a_vn_g, m_a_vn_b, m_a_w_s, m_a_b_s, m_a_w_out, m_b_w_qkv, m_b_w_out, m_mlp_w_up, m_mlp_w_down, v_ada_w, v_ada_b, v_ln_g, v_ln_b, v_a_w_in, v_a_b_in, v_a_vn_g, v_a_vn_b, v_a_w_s, v_a_b_s, v_a_w_out, v_b_w_qkv, v_b_w_out, v_mlp_w_up, v_mlp_w_down):
    s, d = x.shape[1], x.shape[2]
    xi, yi, ci = _me()
    q = 2 * xi + yi
    dev = 2 * q + ci
    nsub = 2 * DEPTH
    cs = ada_w.shape[-1]
    ls = ln_g.shape[-1]

    shards = {
        "a_w_in": a_w_in[0], "a_w_out": a_w_out[0], "b_w_qkv": b_w_qkv[0], "b_w_out": b_w_out[0],
        "up0": mlp_w_up[0], "up1": mlp_w_up[1], "down0": mlp_w_down[0], "down1": mlp_w_down[1],
    }
    cast = [shards[k].astype(MXU_DTYPE) for k in BIG]

    pack = jnp.concatenate([c.reshape(-1), ln_g.reshape(-1), ln_b.reshape(-1)]).reshape(-1, LANES)
    got = _all_gather_small(pack, "gather_small", after=cast).reshape(N_DEV, -1)
    c_all = got[:, :d]
    per_chip = got[0::2]
    ln_g_full = per_chip[:, d:d + nsub * ls].reshape(N_CHIPS, nsub, ls).transpose(1, 0, 2).reshape(nsub, d)
    ln_b_full = per_chip[:, d + nsub * ls:].reshape(N_CHIPS, nsub, ls).transpose(1, 0, 2).reshape(nsub, d)
    m_part = _ada_fwd(c_all, ada_w.reshape(nsub, d, cs), ada_b.reshape(nsub, 1, cs), "ada_fwd")
    m_all = _all_gather_small(m_part.reshape(-1, LANES), "gather_mod").reshape(N_DEV, nsub, N_DEV, cs)
    m_mine = lax.dynamic_index_in_dim(m_all[0::2], dev, axis=2, keepdims=False)
    mvec = m_mine.transpose(1, 0, 2).reshape(nsub, 3 * d)

    halved = {BIG.index(k) for k in HALVED}
    send_sems, recv_sems, shard_thru, lands, token = _gather_start(cast, halved, mvec, "gather_start")

    def fetch(k, after):
        w = BIG.index(k)
        shard, gw = _gather_wait(w, shard_thru[w], lands[w], send_sems, recv_sems, after, f"gather_wait_{k}", w in halved)
        if w in halved:
            gw = _assemble_halves(shard, gw, f"assemble_{k}")
        return gw if BIG_KIND[k] == "col" else gw.reshape(1, -1, gw.shape[-1])

    scattering, pending = {}, {}

    def emit(k, g):
        pending[k] = g
        group = next(gr for gr in SCATTER_GROUPS if k in gr)
        if k != group[-1]:
            return None
        scattering[group] = _scatter_start([pending[m] for m in group], [BIG_KIND[m] for m in group], f"scatter_start_{k}")
        return scattering[group][2][0]

    tril = jnp.tril(jnp.ones((CHUNK, CHUNK), bool))
    wc = jnp.where(tril, a_w_s[0], 0.0).astype(MXU_DTYPE)
    heads = jnp.arange(1, B_HEADS + 1, dtype=F32)
    small = {
        "a_b_in": a_b_in, "a_vn_g": a_vn_g, "a_vn_b": a_vn_b,
        "wc": wc, "wct": wc.transpose(0, 2, 1),
        "bias_full": jnp.repeat(a_b_s[0].T, d // A_GROUPS, axis=1),
        "slopes": jnp.exp2(-8.0 * heads / B_HEADS),
    }

    loss_part, grad_x, gb, dm, dlg, dlb, gsmall = _local_step(x[0], loss_target[0], mvec, ln_g_full, ln_b_full, small, fetch, emit, token)
    loss = lax.psum(loss_part, ("x", "y", "c"))

    weights = dict(ada_w=ada_w, ada_b=ada_b, ln_g=ln_g, ln_b=ln_b, a_w_in=a_w_in, a_b_in=a_b_in, a_vn_g=a_vn_g, a_vn_b=a_vn_b,
                   a_w_s=a_w_s, a_b_s=a_b_s, a_w_out=a_w_out, b_w_qkv=b_w_qkv, b_w_out=b_w_out, mlp_w_up=mlp_w_up, mlp_w_down=mlp_w_down)
    ms = dict(ada_w=m_ada_w, ada_b=m_ada_b, ln_g=m_ln_g, ln_b=m_ln_b, a_w_in=m_a_w_in, a_b_in=m_a_b_in, a_vn_g=m_a_vn_g, a_vn_b=m_a_vn_b,
              a_w_s=m_a_w_s, a_b_s=m_a_b_s, a_w_out=m_a_w_out, b_w_qkv=m_b_w_qkv, b_w_out=m_b_w_out, mlp_w_up=m_mlp_w_up, mlp_w_down=m_mlp_w_down)
    vs = dict(ada_w=v_ada_w, ada_b=v_ada_b, ln_g=v_ln_g, ln_b=v_ln_b, a_w_in=v_a_w_in, a_b_in=v_a_b_in, a_vn_g=v_a_vn_g, a_vn_b=v_a_vn_b,
              a_w_s=v_a_w_s, a_b_s=v_a_b_s, a_w_out=v_a_w_out, b_w_qkv=v_b_w_qkv, b_w_out=v_b_w_out, mlp_w_up=v_mlp_w_up, mlp_w_down=v_mlp_w_down)
    grads, updates = {}, {}

    def update(k):
        updates[k] = _adamw(weights[k], grads[k], ms[k], vs[k], f"adamw_{k}")
        return updates[k][0]

    pack_b = jnp.concatenate([dm.reshape(-1), dlg.reshape(-1), dlb.reshape(-1)] + [gsmall[k] for k in SMALL])
    n_small = pack_b.shape[0]
    pack_b = jnp.pad(pack_b, (0, -n_small % (ROW_TILE * LANES)))
    got_b = _all_gather_small(pack_b.reshape(-1, LANES), "gather_small_grads").reshape(N_DEV, -1, LANES)
    tot = _sum_slots(got_b, "sum_small").reshape(-1)
    o = 0
    dm_tot = tot[o:o + nsub * 3 * d].reshape(nsub, 3 * d); o += nsub * 3 * d
    dlg_tot = tot[o:o + nsub * d].reshape(nsub, d); o += nsub * d
    dlb_tot = tot[o:o + nsub * d].reshape(nsub, d); o += nsub * d
    g_small = {}
    for k, ref in zip(SMALL, (a_b_in, a_vn_g, a_vn_b, a_b_s, a_w_s)):
        g_small[k] = tot[o:o + ref.size].reshape(ref.shape); o += ref.size
    assert o == n_small
    dm_all = got_b.reshape(N_DEV, -1)[:, :nsub * 3 * d].reshape(N_DEV, nsub, 3 * d)
    dm_cols = lax.dynamic_slice_in_dim(dm_all, q * cs, cs, axis=2).transpose(1, 0, 2)

    grads.update({
        "ada_w": _ada_bwd(c_all.T, dm_cols, "ada_bwd").reshape(ada_w.shape),
        "ada_b": lax.dynamic_slice_in_dim(dm_tot, q * cs, cs, axis=1).reshape(ada_b.shape),
        "ln_g": lax.dynamic_slice_in_dim(dlg_tot, q * ls, ls, axis=1).reshape(ln_g.shape),
        "ln_b": lax.dynamic_slice_in_dim(dlb_tot, q * ls, ls, axis=1).reshape(ln_b.shape),
        **g_small,
    })
    for k in ("ada_b", "ln_g", "ln_b") + SMALL:
        update(k)
    done = update("ada_w")

    gfull = {}
    for group in (SCATTER_GROUPS[0] + SCATTER_GROUPS[1], SCATTER_GROUPS[2] + SCATTER_GROUPS[3]):
        bufs = []
        for pair in (group[:2], group[2:]):
            bufs += _scatter_wait(*scattering[pair], [BIG_KIND[m] for m in pair], done, f"scatter_wait_{pair[-1]}")
        halves = [_sum_slots(b, f"sum_{k}") for k, b in zip(group, bufs)]
        fulls = _swap_halves(halves, f"swap_halves_{group[0]}")
        gfull.update({k: f.reshape(-1, f.shape[-1]) for k, f in zip(group, fulls)})
        if group[0] == "down1":
            grads["b_w_qkv"], grads["b_w_out"] = gfull["b_w_qkv"][None], gfull["b_w_out"][None]
            update("b_w_out")
            done = update("b_w_qkv")
    grads.update({
        "a_w_in": gfull["a_w_in"][None], "a_w_out": gfull["a_w_out"][None],
        "mlp_w_up": jnp.stack([gfull["up0"], gfull["up1"]]), "mlp_w_down": jnp.stack([gfull["down0"], gfull["down1"]]),
    })
    for k in ("a_w_in", "a_w_out", "mlp_w_up", "mlp_w_down"):
        update(k)
    names = list(weights)
    return (loss, grad_x[None], *[grads[k] for k in names], *[updates[k][0] for k in names],
            *[updates[k][1] for k in names], *[updates[k][2] for k in names])
```

```python
import functools
import math

import jax
import jax.numpy as jnp
from jax import lax
from jax.experimental import pallas as pl
from jax.experimental.pallas import tpu as pltpu

F32 = jnp.float32
MXU_DTYPE = jnp.bfloat16

DEPTH = 2
CHUNK = 128
A_GROUPS = 16
B_HEADS = 16
HEAD_DIM = 64
B_PATTERNS = ((128, 1), (512, 4), (2048, 16))
SPAN = 128
ALPHA = (2 * DEPTH) ** 0.25
LN_EPS = 1e-5
NEG = -1e30
ATT_SCALE = HEAD_DIM ** -0.5
ADAM_LR, ADAM_B1, ADAM_B2, ADAM_EPS, ADAM_WD, ADAM_STEP = 0.001, 0.9, 0.999, 1e-08, 0.01, 10

N_CHIPS = 4
N_DEV = 8
LANES = 128
SUBLANES = 8
VMEM_LIMIT = 52 * 1024 * 1024
ROW_TILE = 256
MESH = pl.DeviceIdType.MESH


def _cparams(sem):
    return pltpu.CompilerParams(dimension_semantics=sem, vmem_limit_bytes=VMEM_LIMIT)


def _fold8(v):
    r, c = v.shape
    return jnp.sum(v.reshape(r // SUBLANES, SUBLANES, c), axis=0)


def _gelu(x):
    c = math.sqrt(2.0 / math.pi)
    return 0.5 * x * (1.0 + jnp.tanh(c * (x + 0.044715 * (x * x * x))))


def _gelu_grad(x):
    c = math.sqrt(2.0 / math.pi)
    t = jnp.tanh(c * (x + 0.044715 * (x * x * x)))
    return 0.5 * (1.0 + t) + 0.5 * x * (1.0 - t * t) * c * (1.0 + 3.0 * 0.044715 * x * x)


def _dot(a, b, dims):
    return lax.dot_general(a.astype(MXU_DTYPE), b.astype(MXU_DTYPE), (dims, ((), ())), preferred_element_type=F32)


def _dot_nn(a, b):
    return _dot(a, b, ((1,), (0,)))


def _dot_nt(a, b):
    return _dot(a, b, ((1,), (1,)))


def _dot_tn(a, b):
    return _dot(a, b, ((0,), (0,)))


def _mm(a, b, *, mode, name, outs, tm, tn, tk, epi=None, extras=(), b_col0=0, n_out=None, after=None,
        out_col0=0, out_cols=None, into=None):
    if mode == "nn":
        m, kdim = a.shape
        p, kb, ns = b.shape
        assert kb == kdim and ns % tn == 0 and b_col0 % tn == 0
        n = n_out if n_out is not None else p * ns
        npt, j0 = ns // tn, b_col0 // tn
        a_spec = pl.BlockSpec((tm, tk), lambda i, j, k: (i, k))
        b_spec = pl.BlockSpec((None, tk, tn), lambda i, j, k: ((j + j0) // npt, k, (j + j0) % npt))
        dot = _dot_nn
    elif mode == "nt":
        m, kdim = a.shape
        p, n, ns = b.shape
        assert ns % tk == 0 and b_col0 % tk == 0
        npt, j0 = ns // tk, b_col0 // tk
        a_spec = pl.BlockSpec((tm, tk), lambda i, j, k: (i, k))
        b_spec = pl.BlockSpec((None, tn, tk), lambda i, j, k: ((k + j0) // npt, j, (k + j0) % npt))
        dot = _dot_nt
    else:
        kdim, m = a.shape
        kb, n = b.shape
        assert kb == kdim
        a_spec = pl.BlockSpec((tk, tm), lambda i, j, k: (k, i))
        b_spec = pl.BlockSpec((tk, tn), lambda i, j, k: (k, j))
        dot = _dot_tn
    assert m % tm == 0 and n % tn == 0 and kdim % tk == 0, (name, m, n, kdim, tm, tn, tk)
    nk = kdim // tk
    ex_specs, ex_arrays = [], []
    for kind, arr in extras:
        if kind == "row":
            ex_specs.append(pl.BlockSpec((1, tn), lambda i, j, k: (0, j)))
        else:
            ex_specs.append(pl.BlockSpec((tm, tn), lambda i, j, k: (i, j)))
        ex_arrays.append(arr)
    n_ex, n_o = len(ex_arrays), len(outs)
    deps = [d for d in (after, into) if d is not None]
    n_dep = len(deps)
    j_out = out_col0 // tn
    assert out_col0 % tn == 0 and (into is None or len(outs) == 1)

    def body(a_ref, b_ref, *rest):
        ex_refs, o_refs = rest[:n_ex], rest[n_ex + n_dep:n_ex + n_dep + n_o]
        k = pl.program_id(2)

        def finish(r):
            vals = epi(r, *[e[...] for e in ex_refs]) if epi is not None else [r]
            for o, v in zip(o_refs, vals):
                o[...] = v.astype(o.dtype)

        if nk == 1:
            finish(dot(a_ref[...], b_ref[...]))
            return
        acc = rest[n_ex + n_dep + n_o]

        @pl.when(k == 0)
        def _():
            acc[...] = dot(a_ref[...], b_ref[...])

        @pl.when((k > 0) & (k < nk - 1))
        def _():
            acc[...] += dot(a_ref[...], b_ref[...])

        @pl.when(k == nk - 1)
        def _():
            finish(acc[...] + dot(a_ref[...], b_ref[...]))

    res = pl.pallas_call(
        body,
        grid=(m // tm, n // tn, nk),
        in_specs=[a_spec, b_spec] + ex_specs + [pl.BlockSpec(memory_space=pl.ANY)] * n_dep,
        out_specs=[pl.BlockSpec((tm, tn), lambda i, j, k: (i, j + j_out)) for _ in outs],
        out_shape=[jax.ShapeDtypeStruct((m, out_cols or n), dt) for dt in outs],
        input_output_aliases={} if into is None else {2 + n_ex + n_dep - 1: 0},
        scratch_shapes=[pltpu.VMEM((tm, tn), F32)] if nk > 1 else [],
        name=name,
        compiler_params=_cparams(("parallel", "parallel", "arbitrary")),
    )(a, b, *ex_arrays, *deps)
    return res if len(outs) > 1 else res[0]


def _rows(body, n_rows, tr, ins, outs, name, scratch=()):
    def spec(kind, shape):
        if kind == "blk":
            return pl.BlockSpec((tr,) + tuple(shape[1:]), lambda i: (i,) + (0,) * (len(shape) - 1))
        if kind == "dep":
            return pl.BlockSpec(memory_space=pl.ANY)
        return pl.BlockSpec(tuple(shape), lambda i: (0,) * len(shape))

    return pl.pallas_call(
        body,
        grid=(n_rows // tr,),
        in_specs=[spec(k, a.shape) for k, a in ins],
        out_specs=[spec(k, s) for k, s, _ in outs],
        out_shape=[jax.ShapeDtypeStruct(tuple(s), d) for _, s, d in outs],
        scratch_shapes=list(scratch),
        name=name,
        compiler_params=_cparams(("arbitrary",)),
    )(*[a for _, a in ins])


def _ln_stats(z):
    mu = jnp.mean(z, axis=-1, keepdims=True)
    zc = z - mu
    var = jnp.mean(zc * zc, axis=-1, keepdims=True)
    rstd = lax.rsqrt(var + LN_EPS)
    return zc * rstd, rstd


def _mod(x, scale, shift, after, name):
    s, d = x.shape

    def body(x_ref, sc_ref, sh_ref, dep_ref, h_ref):
        h_ref[...] = (x_ref[...] * (1.0 + sc_ref[...]) + sh_ref[...]).astype(h_ref.dtype)

    return _rows(body, s, ROW_TILE, [("blk", x), ("all", scale), ("all", shift), ("dep", after)], [("blk", (s, d), MXU_DTYPE)], name)[0]


def _resid_ln(x, y, gate, g, b, nxt, name):
    s, d = x.shape

    def body(x_ref, y_ref, gate_ref, g_ref, b_ref, sc_ref, sh_ref, xn_ref, h_ref):
        z = ALPHA * x_ref[...] + gate_ref[...] * y_ref[...]
        xhat, _ = _ln_stats(z)
        xn = xhat * g_ref[...] + b_ref[...]
        xn_ref[...] = xn
        h_ref[...] = (xn * (1.0 + sc_ref[...]) + sh_ref[...]).astype(h_ref.dtype)

    return _rows(body, s, ROW_TILE,
                 [("blk", x), ("blk", y), ("all", gate), ("all", g), ("all", b), ("all", nxt[0]), ("all", nxt[1])],
                 [("blk", (s, d), F32), ("blk", (s, d), MXU_DTYPE)], name)


def _mod_bwd(dxr, dhs, x, scale, name, after=None):
    s, d = x.shape
    n_dh = len(dhs)
    n_dep = 0 if after is None else 1

    def body(dxr_ref, *rest):
        dh_refs = rest[:n_dh]
        x_ref, sc_ref, dx_ref, red_ref, a_sh, a_sc = rest[n_dh:n_dh + 2] + rest[n_dh + 2 + n_dep:]
        i = pl.program_id(0)

        @pl.when(i == 0)
        def _():
            a_sh[...] = jnp.zeros_like(a_sh)
            a_sc[...] = jnp.zeros_like(a_sc)

        dh = dh_refs[0][...]
        for r in dh_refs[1:]:
            dh = dh + r[...]
        dx_ref[...] = dxr_ref[...] + dh * (1.0 + sc_ref[...])
        a_sh[...] += _fold8(dh)
        a_sc[...] += _fold8(dh * x_ref[...])

        @pl.when(i == pl.num_programs(0) - 1)
        def _():
            red_ref[...] = jnp.zeros_like(red_ref)
            red_ref[0:1, :] = jnp.sum(a_sh[...], axis=0, keepdims=True)
            red_ref[1:2, :] = jnp.sum(a_sc[...], axis=0, keepdims=True)

    return _rows(body, s, ROW_TILE, [("blk", dxr)] + [("blk", h) for h in dhs] + [("blk", x), ("all", scale)] + [("dep", after)] * n_dep,
                 [("blk", (s, d), F32), ("all", (SUBLANES, d), F32)], name,
                 scratch=[pltpu.VMEM((SUBLANES, d), F32)] * 2)


def _last_ln_loss_bwd(x, y, gate, g, b, target, name):
    s, d = x.shape

    def body(x_ref, y_ref, gate_ref, g_ref, b_ref, t_ref, l_ref, dxr_ref, dyy_ref, red_ref, a_l, a_g, a_b, a_gate):
        i = pl.program_id(0)

        @pl.when(i == 0)
        def _():
            for a in (a_l, a_g, a_b, a_gate):
                a[...] = jnp.zeros_like(a)

        yv = y_ref[...]
        z = ALPHA * x_ref[...] + gate_ref[...] * yv
        xhat, rstd = _ln_stats(z)
        e = xhat * g_ref[...] + b_ref[...] - t_ref[...]
        a_l[...] += _fold8(e * e)
        dxo_v = e * (1.0 / d)
        dxh = dxo_v * g_ref[...]
        dz = rstd * (dxh - jnp.mean(dxh, axis=-1, keepdims=True) - xhat * jnp.mean(dxh * xhat, axis=-1, keepdims=True))
        dxr_ref[...] = ALPHA * dz
        dyy_ref[...] = (gate_ref[...] * dz).astype(dyy_ref.dtype)
        a_g[...] += _fold8(dxo_v * xhat)
        a_b[...] += _fold8(dxo_v)
        a_gate[...] += _fold8(dz * yv)

        @pl.when(i == pl.num_programs(0) - 1)
        def _():
            l_ref[...] = jnp.full(l_ref.shape, 0.5 / d, F32) * jnp.sum(a_l[...])
            red_ref[...] = jnp.zeros_like(red_ref)
            red_ref[0:1, :] = jnp.sum(a_g[...], axis=0, keepdims=True)
            red_ref[1:2, :] = jnp.sum(a_b[...], axis=0, keepdims=True)
            red_ref[2:3, :] = jnp.sum(a_gate[...], axis=0, keepdims=True)

    l, dxr, dyy, red = _rows(
        body, s, ROW_TILE, [("blk", x), ("blk", y), ("all", gate), ("all", g), ("all", b), ("blk", target)],
        [("all", (SUBLANES, LANES), F32), ("blk", (s, d), F32), ("blk", (s, d), MXU_DTYPE), ("all", (SUBLANES, d), F32)], name,
        scratch=[pltpu.VMEM((SUBLANES, d), F32)] * 4)
    return l[0, 0], dxr, dyy, red


def _mod_ln_bwd(dxr, dhs, x, scale, x_in, y, gate, g, name, after=None):
    s, d = x.shape
    n_dh = len(dhs)
    n_dep = 0 if after is None else 1

    def body(dxr_ref, *rest):
        dh_refs = rest[:n_dh]
        x_ref, sc_ref, xin_ref, y_ref, gate_ref, g_ref = rest[n_dh:n_dh + 6]
        dxr_out, dyy_ref, red_mod, red_ln, a_sh, a_sc, a_g, a_b, a_gate = rest[n_dh + 6 + n_dep:]
        i = pl.program_id(0)

        @pl.when(i == 0)
        def _():
            for a in (a_sh, a_sc, a_g, a_b, a_gate):
                a[...] = jnp.zeros_like(a)

        dh = dh_refs[0][...]
        for r in dh_refs[1:]:
            dh = dh + r[...]
        xv = x_ref[...]
        dxo_v = dxr_ref[...] + dh * (1.0 + sc_ref[...])
        a_sh[...] += _fold8(dh)
        a_sc[...] += _fold8(dh * xv)
        yv = y_ref[...]
        z = ALPHA * xin_ref[...] + gate_ref[...] * yv
        xhat, rstd = _ln_stats(z)
        dxh = dxo_v * g_ref[...]
        dz = rstd * (dxh - jnp.mean(dxh, axis=-1, keepdims=True) - xhat * jnp.mean(dxh * xhat, axis=-1, keepdims=True))
        dxr_out[...] = ALPHA * dz
        dyy_ref[...] = (gate_ref[...] * dz).astype(dyy_ref.dtype)
        a_g[...] += _fold8(dxo_v * xhat)
        a_b[...] += _fold8(dxo_v)
        a_gate[...] += _fold8(dz * yv)

        @pl.when(i == pl.num_programs(0) - 1)
        def _():
            red_mod[...] = jnp.zeros_like(red_mod)
            red_mod[0:1, :] = jnp.sum(a_sh[...], axis=0, keepdims=True)
            red_mod[1:2, :] = jnp.sum(a_sc[...], axis=0, keepdims=True)
            red_ln[...] = jnp.zeros_like(red_ln)
            red_ln[0:1, :] = jnp.sum(a_g[...], axis=0, keepdims=True)
            red_ln[1:2, :] = jnp.sum(a_b[...], axis=0, keepdims=True)
            red_ln[2:3, :] = jnp.sum(a_gate[...], axis=0, keepdims=True)

    ins = ([("blk", dxr)] + [("blk", h) for h in dhs]
           + [("blk", x), ("all", scale), ("blk", x_in), ("blk", y), ("all", gate), ("all", g)] + [("dep", after)] * n_dep)
    return _rows(body, s, ROW_TILE, ins,
                 [("blk", (s, d), F32), ("blk", (s, d), MXU_DTYPE), ("all", (SUBLANES, d), F32), ("all", (SUBLANES, d), F32)], name,
                 scratch=[pltpu.VMEM((SUBLANES, d), F32)] * 5)


def _left_half(shape):
    return lax.broadcasted_iota(jnp.int32, shape, 1) < (LANES // 2)


def _spatial_z(vn, wc_ref, bias_ref, j):
    vb = vn[:, j * LANES:(j + 1) * LANES]
    z0 = _dot_nn(wc_ref[2 * j], vb)
    z1 = _dot_nn(wc_ref[2 * j + 1], vb)
    return jnp.where(_left_half(z0.shape), z0, z1) + bias_ref[:, j * LANES:(j + 1) * LANES]


def _spatial_fwd(uvpre, vn_g, vn_b, wc, bias_full, name):
    s, d2 = uvpre.shape
    d = d2 // 2

    def body(uv_ref, g_ref, b_ref, wc_ref, bias_ref, out_ref):
        u = _gelu(uv_ref[:, :d])
        v = _gelu(uv_ref[:, d:])
        vh, _ = _ln_stats(v)
        vn = vh * g_ref[...] + b_ref[...]
        for j in range(d // LANES):
            z = _spatial_z(vn, wc_ref, bias_ref, j)
            out_ref[:, j * LANES:(j + 1) * LANES] = (u[:, j * LANES:(j + 1) * LANES] * z).astype(out_ref.dtype)

    return _rows(body, s, CHUNK, [("blk", uvpre), ("all", vn_g), ("all", vn_b), ("all", wc), ("all", bias_full)],
                 [("blk", (s, d), MXU_DTYPE)], name)[0]


def _spatial_bwd(uvpre, dgated, vn_g, vn_b, wc, wct, bias_full, name):
    s, d2 = uvpre.shape
    d = d2 // 2

    def body(uv_ref, dg_ref, g_ref, b_ref, wc_ref, wct_ref, bias_ref,
             duv_ref, dws_ref, dbias_ref, dbin_ref, dvg_ref, dvb_ref, dvn_buf, a_bin, a_vg, a_vb):
        i = pl.program_id(0)

        @pl.when(i == 0)
        def _():
            dws_ref[...] = jnp.zeros_like(dws_ref)
            dbias_ref[...] = jnp.zeros_like(dbias_ref)
            a_bin[...] = jnp.zeros_like(a_bin)
            a_vg[...] = jnp.zeros_like(a_vg)
            a_vb[...] = jnp.zeros_like(a_vb)

        up = uv_ref[:, :d]
        vp = uv_ref[:, d:]
        u = _gelu(up)
        v = _gelu(vp)
        vh, rstd = _ln_stats(v)
        vn = vh * g_ref[...] + b_ref[...]
        dg = dg_ref[...]
        dzz = dg * u
        dbias_ref[...] += dzz
        for j in range(d // LANES):
            cols = slice(j * LANES, (j + 1) * LANES)
            z = _spatial_z(vn, wc_ref, bias_ref, j)
            dup = dg[:, cols] * z * _gelu_grad(up[:, cols])
            duv_ref[:, cols] = dup.astype(duv_ref.dtype)
            a_bin[:, cols] += _fold8(dup)
            dzb = dzz[:, cols]
            left = _left_half(dzb.shape)
            dvn_buf[:, cols] = jnp.where(left, _dot_nn(wct_ref[2 * j], dzb), _dot_nn(wct_ref[2 * j + 1], dzb))
            vb = vn[:, cols]
            dws_ref[2 * j] += _dot_nt(jnp.where(left, dzb, 0.0), vb)
            dws_ref[2 * j + 1] += _dot_nt(jnp.where(left, 0.0, dzb), vb)
        dvn = dvn_buf[...]
        a_vg[...] += _fold8(dvn * vh)
        a_vb[...] += _fold8(dvn)
        dvh = dvn * g_ref[...]
        dv = rstd * (dvh - jnp.mean(dvh, axis=-1, keepdims=True) - vh * jnp.mean(dvh * vh, axis=-1, keepdims=True))
        dvp = dv * _gelu_grad(vp)
        duv_ref[:, d:] = dvp.astype(duv_ref.dtype)
        a_bin[:, d:] += _fold8(dvp)

        @pl.when(i == pl.num_programs(0) - 1)
        def _():
            dbin_ref[...] = jnp.sum(a_bin[...], axis=0, keepdims=True)
            dvg_ref[...] = jnp.sum(a_vg[...], axis=0, keepdims=True)
            dvb_ref[...] = jnp.sum(a_vb[...], axis=0, keepdims=True)

    return _rows(body, s, CHUNK,
                 [("blk", uvpre), ("blk", dgated), ("all", vn_g), ("all", vn_b), ("all", wc), ("all", wct), ("all", bias_full)],
                 [("blk", (s, d2), MXU_DTYPE), ("all", (A_GROUPS, CHUNK, CHUNK), F32), ("all", (CHUNK, d), F32),
                  ("all", (1, d2), F32), ("all", (1, d), F32), ("all", (1, d), F32)], name,
                 scratch=[pltpu.VMEM((CHUNK, d), F32), pltpu.VMEM((SUBLANES, d2), F32),
                          pltpu.VMEM((SUBLANES, d), F32), pltpu.VMEM((SUBLANES, d), F32)])


def _head_mask(v, h):
    lane = lax.broadcasted_iota(jnp.int32, v.shape, 1)
    return jnp.where((lane >= h * HEAD_DIM) & (lane < (h + 1) * HEAD_DIM), v, jnp.zeros_like(v))


def _att_bias(slopes, dil):
    qi = lax.broadcasted_iota(jnp.int32, (SPAN, SPAN), 0)
    ki = lax.broadcasted_iota(jnp.int32, (SPAN, SPAN), 1)
    sl = slopes[:, None, None]
    cur = jnp.where(ki <= qi, -sl * (float(dil) * (qi - ki).astype(F32)), NEG)
    prev = jnp.where(ki >= qi, -sl * (float(dil) * (SPAN + qi - ki).astype(F32)), NEG)
    absent = jnp.full_like(prev, NEG)
    pairs = slopes.shape[0] // 2

    def fwd(pv):
        return jnp.concatenate([cur, pv], axis=2).reshape(pairs, 2 * SPAN, 2 * SPAN)

    def bwd(pv):
        return jnp.concatenate([cur.reshape(pairs, 2 * SPAN, SPAN), pv.reshape(pairs, 2 * SPAN, SPAN)], axis=1)

    return jnp.stack([fwd(absent), fwd(prev)]), jnp.stack([bwd(absent), bwd(prev)])


def _att_specs(s, d, dil, kinds):
    nb = s // (dil * SPAN)

    def rowblk(which, b):
        if which == "prev":
            return jnp.where(b % nb == 0, b, b - 1)
        if which == "next":
            return jnp.where(b % nb == nb - 1, b, b + 1)
        return b

    return [pl.BlockSpec((SPAN, d), functools.partial(lambda b, o, w: (rowblk(w, b), o), o=part, w=which))
            for part, which in kinds]


def _lane_col(v, h):
    return v[:, h * HEAD_DIM:h * HEAD_DIM + 1]


def _attn_fwd(qkv, slopes, dil, name):
    s, d3 = qkv.shape
    d = d3 // 3
    nb = s // (dil * SPAN)
    table, _ = _att_bias(slopes, dil)

    def body(q_ref, kc_ref, kp_ref, vc_ref, vp_ref, tb_ref, o_ref, l_ref):
        left = _left_half((SPAN, LANES))
        for hp in range(d // LANES):
            cols = slice(hp * LANES, (hp + 1) * LANES)
            q = q_ref[:, cols]
            q2 = jnp.concatenate([_head_mask(q, 0), _head_mask(q, 1)], axis=0) * ATT_SCALE
            k2 = jnp.concatenate([kc_ref[:, cols], kp_ref[:, cols]], axis=0)
            v2 = jnp.concatenate([vc_ref[:, cols], vp_ref[:, cols]], axis=0)
            sc = _dot_nt(q2, k2) + tb_ref[hp]
            m = jnp.max(sc, axis=-1, keepdims=True)
            p = jnp.exp(sc - m)
            l = jnp.sum(p, axis=-1, keepdims=True)
            r = _dot_nn(p, v2) * (1.0 / l)
            lse = jnp.broadcast_to(m + jnp.log(l), (2 * SPAN, LANES))
            o_ref[:, cols] = jnp.where(left, r[:SPAN], r[SPAN:])
            l_ref[:, cols] = jnp.where(left, lse[:SPAN], lse[SPAN:])

    specs = _att_specs(s, d, dil, [(0, "cur"), (1, "cur"), (1, "prev"), (2, "cur"), (2, "prev")])
    tbl = pl.BlockSpec((None,) + table.shape[1:], lambda b: (jnp.where(b % nb == 0, 0, 1), 0, 0, 0))
    out_spec = pl.BlockSpec((SPAN, d), lambda b: (b, 0))
    return pl.pallas_call(
        body,
        grid=(s // SPAN,),
        in_specs=specs + [tbl],
        out_specs=[out_spec, out_spec],
        out_shape=[jax.ShapeDtypeStruct((s, d), F32)] * 2,
        name=name,
        compiler_params=_cparams(("parallel",)),
    )(qkv, qkv, qkv, qkv, qkv, table)


def _attn_bwd(qkv, do, lse, dd, slopes, dil, name):
    s, d3 = qkv.shape
    d = d3 // 3
    nb = s // (dil * SPAN)
    _, table = _att_bias(slopes, dil)

    def heads_stacked(cur, nxt):
        return jnp.concatenate([_head_mask(cur, 0), _head_mask(cur, 1), _head_mask(nxt, 0), _head_mask(nxt, 1)], axis=0)

    def cols_stacked(cur, nxt):
        return jnp.concatenate([jnp.broadcast_to(_lane_col(a, h), (SPAN, LANES)) for a in (cur, nxt) for h in range(2)], axis=0)

    def body(k_ref, v_ref, qc_ref, qn_ref, doc_ref, don_ref, lc_ref, ln_ref, ddc_ref, ddn_ref, tb_ref, out_ref, carry):
        b = pl.program_id(0)

        @pl.when(b == 0)
        def _():
            carry[...] = jnp.zeros_like(carry)

        left = _left_half((SPAN, LANES))
        for hp in range(d // LANES):
            cols = slice(hp * LANES, (hp + 1) * LANES)
            k, v = k_ref[:, cols], v_ref[:, cols]
            q4 = heads_stacked(qc_ref[:, cols], qn_ref[:, cols])
            do4 = heads_stacked(doc_ref[:, cols], don_ref[:, cols])
            sc = _dot_nt(q4 * ATT_SCALE, k) + tb_ref[hp]
            p = jnp.exp(sc - cols_stacked(lc_ref[:, cols], ln_ref[:, cols]))
            ds = p * (_dot_nt(do4, v) - cols_stacked(ddc_ref[:, cols], ddn_ref[:, cols]))
            dq4 = _dot_nn(ds, k)
            dq_cur = jnp.where(left, dq4[:SPAN], dq4[SPAN:2 * SPAN]) + carry[:, cols]
            carry[:, cols] = jnp.where(left, dq4[2 * SPAN:3 * SPAN], dq4[3 * SPAN:])
            out_ref[:, cols] = (dq_cur * ATT_SCALE).astype(out_ref.dtype)
            out_ref[:, d + hp * LANES:d + (hp + 1) * LANES] = (_dot_tn(ds, q4) * ATT_SCALE).astype(out_ref.dtype)
            out_ref[:, 2 * d + hp * LANES:2 * d + (hp + 1) * LANES] = _dot_tn(p, do4).astype(out_ref.dtype)

    qkv_specs = _att_specs(s, d, dil, [(1, "cur"), (2, "cur"), (0, "cur"), (0, "next")])
    pair = _att_specs(s, d, dil, [(0, "cur"), (0, "next")])
    tbl = pl.BlockSpec((None,) + table.shape[1:], lambda b: (jnp.where(b % nb == nb - 1, 0, 1), 0, 0, 0))
    return pl.pallas_call(
        body,
        grid=(s // SPAN,),
        in_specs=qkv_specs + pair + pair + pair + [tbl],
        out_specs=pl.BlockSpec((SPAN, d3), lambda b: (b, 0)),
        out_shape=jax.ShapeDtypeStruct((s, d3), MXU_DTYPE),
        scratch_shapes=[pltpu.VMEM((SPAN, d), F32)],
        name=name,
        compiler_params=_cparams(("arbitrary",)),
    )(qkv, qkv, qkv, qkv, do, do, lse, lse, dd, dd, table)


def _mix_weights(l_refs):
    ls = [r[...] for r in l_refs]
    m = functools.reduce(jnp.maximum, ls)
    es = [jnp.exp(l - m) for l in ls]
    tot = functools.reduce(lambda a, c: a + c, es)
    return [e / tot for e in es]


def _combine_fwd(os_, ls_, name):
    s, d = os_[0].shape
    n = len(os_)

    def body(*refs):
        o_refs, l_refs, out_ref = refs[:n], refs[n:2 * n], refs[2 * n]
        ws = _mix_weights(l_refs)
        acc = ws[0] * o_refs[0][...]
        for w, o in zip(ws[1:], o_refs[1:]):
            acc = acc + w * o[...]
        out_ref[...] = acc

    return _rows(body, s, ROW_TILE, [("blk", a) for a in os_ + ls_], [("blk", (s, d), F32)], name)[0]


def _combine_bwd(do, o, ls_, name):
    s, d = o.shape
    n = len(ls_)
    ri = lax.broadcasted_iota(jnp.int32, (LANES, LANES), 0) // HEAD_DIM
    ci = lax.broadcasted_iota(jnp.int32, (LANES, LANES), 1) // HEAD_DIM
    seg = (ri == ci).astype(F32)

    def body(do_ref, o_ref, *rest):
        l_refs, seg_ref, outs = rest[:n], rest[n], rest[n + 1:]
        ws = _mix_weights(l_refs)
        dov = do_ref[...]
        prod = dov * o_ref[...]
        for j in range(d // LANES):
            cols = slice(j * LANES, (j + 1) * LANES)
            r = jnp.dot(prod[:, cols], seg_ref[...], precision=lax.Precision.HIGHEST, preferred_element_type=F32)
            for g in range(n):
                outs[2 * g][:, cols] = (ws[g][:, cols] * dov[:, cols]).astype(outs[2 * g].dtype)
                outs[2 * g + 1][:, cols] = ws[g][:, cols] * r

    outs = []
    for _ in range(n):
        outs += [("blk", (s, d), MXU_DTYPE), ("blk", (s, d), F32)]
    res = _rows(body, s, ROW_TILE, [("blk", do), ("blk", o)] + [("blk", l) for l in ls_] + [("all", seg)], outs, name)
    return [(res[2 * g], res[2 * g + 1]) for g in range(n)]


def _ada_fwd(c_all, w, b, name):
    nsub, d, cs = w.shape

    def body(c_ref, w_ref, b_ref, o_ref):
        cv = c_ref[...]
        sc = cv * (1.0 / (1.0 + jnp.exp(-cv)))
        o_ref[...] = _dot_nn(sc, w_ref[...]) + b_ref[...]

    return pl.pallas_call(
        body,
        grid=(nsub,),
        in_specs=[pl.BlockSpec(c_all.shape, lambda i: (0, 0)), pl.BlockSpec((None, d, cs), lambda i: (i, 0, 0)),
                  pl.BlockSpec((None, 1, cs), lambda i: (i, 0, 0))],
        out_specs=pl.BlockSpec((None, N_DEV, cs), lambda i: (i, 0, 0)),
        out_shape=jax.ShapeDtypeStruct((nsub, N_DEV, cs), F32),
        name=name,
        compiler_params=_cparams(("parallel",)),
    )(c_all, w, b)


def _ada_bwd(c_all_t, dm, name):
    d, nb = c_all_t.shape
    nsub, _, cs = dm.shape

    def body(c_ref, dm_ref, o_ref):
        cv = c_ref[...]
        sc = cv * (1.0 / (1.0 + jnp.exp(-cv)))
        acc = sc[:, 0:1] * dm_ref[0:1, :]
        for bi in range(1, nb):
            acc = acc + sc[:, bi:bi + 1] * dm_ref[bi:bi + 1, :]
        o_ref[...] = acc

    return pl.pallas_call(
        body,
        grid=(nsub,),
        in_specs=[pl.BlockSpec(c_all_t.shape, lambda i: (0, 0)), pl.BlockSpec((None, nb, cs), lambda i: (i, 0, 0))],
        out_specs=pl.BlockSpec((None, d, cs), lambda i: (i, 0, 0)),
        out_shape=jax.ShapeDtypeStruct((nsub, d, cs), F32),
        name=name,
        compiler_params=_cparams(("parallel",)),
    )(c_all_t, dm)


def _row_tile(r, row_elems):
    t = 2 * SUBLANES
    if r % t:
        return r
    while t * 2 * row_elems <= 256 * 1024 and r % (t * 2) == 0:
        t *= 2
    return t


def _adamw(w, g, m, v, name):
    shape = w.shape
    c = shape[-1]
    r = w.size // c
    tr = _row_tile(r, c)
    w2, g2, m2, v2 = [a.reshape(r, c) for a in (w, g, m, v)]
    bc1 = 1.0 - ADAM_B1 ** ADAM_STEP
    bc2 = 1.0 - ADAM_B2 ** ADAM_STEP

    def body(w_ref, g_ref, m_ref, v_ref, d_ref, nm_ref, nv_ref):
        gv = g_ref[...]
        nm = ADAM_B1 * m_ref[...] + (1.0 - ADAM_B1) * gv
        nv = ADAM_B2 * v_ref[...] + (1.0 - ADAM_B2) * (gv * gv)
        d_ref[...] = -ADAM_LR * ((nm / bc1) / (jnp.sqrt(nv / bc2) + ADAM_EPS) + ADAM_WD * w_ref[...])
        nm_ref[...] = nm
        nv_ref[...] = nv

    res = _rows(body, r, tr, [("blk", a) for a in (w2, g2, m2, v2)], [("blk", (r, c), F32)] * 3, name)
    return [a.reshape(shape) for a in res]


def _sum_slots(buf, name):
    n, r, c = buf.shape
    tr = _row_tile(r, n * c)

    def body(b_ref, o_ref):
        acc = b_ref[0].astype(F32)
        for k in range(1, n):
            acc = acc + b_ref[k].astype(F32)
        o_ref[...] = acc

    return pl.pallas_call(
        body,
        grid=(r // tr,),
        in_specs=[pl.BlockSpec((n, tr, c), lambda i: (0, i, 0))],
        out_specs=pl.BlockSpec((tr, c), lambda i: (i, 0)),
        out_shape=jax.ShapeDtypeStruct((r, c), F32),
        name=name,
        compiler_params=_cparams(("parallel",)),
    )(buf)


def _me():
    return lax.axis_index("x"), lax.axis_index("y"), lax.axis_index("c")


def _all_gather_small(blk, name, after=()):
    m_per, n = blk.shape

    def body(x_ref, *rest):
        out_ref, send_sems, recv_sems, local_sem = rest[len(after):]
        x, y, c = _me()
        me, sibling = (x, y, c), (x, y, 1 - c)
        chips = [(1 - x, y), (x, 1 - y), (1 - x, 1 - y)]

        def rows(px, py, pc):
            return out_ref.at[pl.ds((4 * px + 2 * py + pc) * m_per, m_per), :]

        def copy(k, block, to, src=None):
            return pltpu.make_async_remote_copy(
                src_ref=rows(*block) if src is None else src, dst_ref=rows(*block),
                send_sem=send_sems.at[k], recv_sem=recv_sems.at[k], device_id=to, device_id_type=MESH)

        mine = pltpu.make_async_copy(x_ref, rows(*me), local_sem)
        mine.start()
        first = [copy(0, me, sibling, src=x_ref)]
        first += [copy(1 + j, me, (*chip, c), src=x_ref) for j, chip in enumerate(chips)]
        for cp in first:
            cp.start()
        passed = [copy(4 + j, (*chip, c), sibling) for j, chip in enumerate(chips)]
        for j, chip in enumerate(chips):
            copy(1 + j, (*chip, c), me).wait_recv()
            passed[j].start()
        copy(0, sibling, me).wait_recv()
        for j, chip in enumerate(chips):
            copy(4 + j, (*chip, 1 - c), me).wait_recv()
        for cp in first + passed:
            cp.wait_send()
        mine.wait()

    return pl.pallas_call(
        body,
        out_shape=jax.ShapeDtypeStruct((N_DEV * m_per, n), blk.dtype),
        in_specs=[pl.BlockSpec(memory_space=pltpu.VMEM)] + [pl.BlockSpec(memory_space=pl.ANY)] * len(after),
        out_specs=pl.BlockSpec(memory_space=pltpu.VMEM),
        scratch_shapes=[pltpu.SemaphoreType.DMA((7,)), pltpu.SemaphoreType.DMA((7,)), pltpu.SemaphoreType.DMA],
        name=name,
        compiler_params=pltpu.CompilerParams(vmem_limit_bytes=VMEM_LIMIT),
    )(blk, *after)


_HBM = pl.BlockSpec(memory_space=pltpu.HBM)
_SEM = pl.BlockSpec(memory_space=pltpu.SEMAPHORE)
_EFFECT = pltpu.SideEffectType.DATAFLOW_SIDE_EFFECTING


def _other_chips(x, y):
    return [(1 - x, y), (x, 1 - y), (1 - x, 1 - y)]


def _gather_copy(w, j, src_ref, land_ref, send_sems, recv_sems, halved=False):
    x, y, c = _me()
    if halved:
        half = src_ref.shape[0] // 2
        src_ref = src_ref.at[pl.ds(c * half, half), :]
    return pltpu.make_async_remote_copy(
        src_ref=src_ref, dst_ref=land_ref.at[2 * x + y], send_sem=send_sems.at[3 * w + j], recv_sem=recv_sems.at[3 * w + j],
        device_id=(*_other_chips(x, y)[j], c), device_id_type=MESH)


def _gather_start(shards, halved, after, name):
    n = len(shards)
    lands = [lax.empty((N_CHIPS, s.shape[0] // 2 if w in halved else s.shape[0], s.shape[1]), s.dtype) for w, s in enumerate(shards)]

    def body(*refs):
        in_refs, land_refs = refs[:n], refs[n:2 * n]
        send_sems, recv_sems = refs[2 * n + 1], refs[2 * n + 2]
        token = refs[-1]
        for w in range(n):
            for j in range(3):
                _gather_copy(w, j, in_refs[w], land_refs[w], send_sems, recv_sems, w in halved).start()
        token[...] = jnp.zeros_like(token)

    res = pl.pallas_call(
        body,
        out_shape=(pltpu.SemaphoreType.DMA((3 * n,)), pltpu.SemaphoreType.DMA((3 * n,)),
                   *[pltpu.HBM(s.shape, s.dtype) for s in shards], *[pltpu.HBM(l.shape, l.dtype) for l in lands],
                   jax.ShapeDtypeStruct((SUBLANES, LANES), F32)),
        in_specs=[_HBM] * (2 * n) + [pl.BlockSpec(memory_space=pl.ANY)],
        out_specs=(_SEM, _SEM, *[_HBM] * (2 * n), pl.BlockSpec(memory_space=pltpu.VMEM)),
        input_output_aliases={i: 2 + i for i in range(2 * n)},
        name=name,
        compiler_params=pltpu.CompilerParams(has_side_effects=_EFFECT),
    )(*[pltpu.with_memory_space_constraint(a, pltpu.HBM) for a in list(shards) + lands], after)
    return res[0], res[1], res[2:2 + n], res[2 + n:2 + 2 * n], res[-1]


def _gather_wait(w, shard, land, send_sems, recv_sems, after, name, halved=False):
    def body(s_ref, land_ref, send_sems, recv_sems, after_ref, s_out, land_out, stage):
        x, y, _ = _me()
        if not halved:
            pltpu.sync_copy(s_ref, stage)
            pltpu.sync_copy(stage, land_out.at[2 * x + y])
        for j in range(3):
            cp = _gather_copy(w, j, s_ref, land_ref, send_sems, recv_sems, halved)
            cp.wait_send()
            cp.wait_recv()

    return pl.pallas_call(
        body,
        out_shape=(pltpu.HBM(shard.shape, shard.dtype), pltpu.HBM(land.shape, land.dtype)),
        in_specs=(_HBM, _HBM, _SEM, _SEM, pl.BlockSpec(memory_space=pl.ANY)),
        out_specs=(_HBM, _HBM),
        input_output_aliases={0: 0, 1: 1},
        scratch_shapes=[pltpu.VMEM((SUBLANES, LANES) if halved else shard.shape, shard.dtype)],
        name=name,
        compiler_params=pltpu.CompilerParams(has_side_effects=_EFFECT, vmem_limit_bytes=VMEM_LIMIT),
    )(shard, land, send_sems, recv_sems, after)


def _assemble_halves(shard, land, name):
    half = land.shape[1]

    def body(s_ref, land_ref, out_ref, send_sems, recv_sems, local_sems):
        x, y, c = _me()
        own = pltpu.make_async_copy(s_ref, out_ref.at[2 * x + y], local_sems.at[3])
        own.start()
        cps = []
        for j, (ox, oy) in enumerate(_other_chips(x, y)):
            qj = 2 * ox + oy
            mine = out_ref.at[qj, pl.ds(c * half, half), :]
            lc = pltpu.make_async_copy(land_ref.at[qj], mine, local_sems.at[j])
            lc.start()
            rc = pltpu.make_async_remote_copy(
                src_ref=land_ref.at[qj], dst_ref=mine, send_sem=send_sems.at[j], recv_sem=recv_sems.at[j],
                device_id=(x, y, 1 - c), device_id_type=MESH)
            rc.start()
            cps.append((lc, rc))
        for lc, rc in cps:
            rc.wait_recv()
        for lc, rc in cps:
            rc.wait_send()
            lc.wait()
        own.wait()

    vmem = pl.BlockSpec(memory_space=pltpu.VMEM)
    return pl.pallas_call(
        body,
        out_shape=jax.ShapeDtypeStruct((N_CHIPS,) + shard.shape, shard.dtype),
        in_specs=[vmem, vmem],
        out_specs=vmem,
        scratch_shapes=[pltpu.SemaphoreType.DMA((3,)), pltpu.SemaphoreType.DMA((3,)), pltpu.SemaphoreType.DMA((4,))],
        name=name,
        compiler_params=pltpu.CompilerParams(vmem_limit_bytes=VMEM_LIMIT),
    )(shard, land)


def _piece_shape(shape, kind):
    k, nn = shape
    return (k // 2, nn // N_CHIPS) if kind == "col" else (k // N_CHIPS // 2, nn)


def _piece_of(g_ref, kind, tq, tc):
    pr, pc = _piece_shape(g_ref.shape, kind)
    if kind == "col":
        return g_ref.at[pl.ds(tc * pr, pr), pl.ds(tq * pc, pc)]
    return g_ref.at[pl.ds((2 * tq + tc) * pr, pr), :]


def _scatter_copy(w, r, kind, g_ref, land_ref, send_sems, recv_sems):
    x, y, c = _me()
    tx, ty, tc = (x + ((r >> 2) & 1)) % 2, (y + ((r >> 1) & 1)) % 2, (c + (r & 1)) % 2
    return pltpu.make_async_remote_copy(
        src_ref=_piece_of(g_ref, kind, 2 * tx + ty, tc), dst_ref=land_ref.at[4 * x + 2 * y + c],
        send_sem=send_sems.at[N_DEV * w + r], recv_sem=recv_sems.at[N_DEV * w + r], device_id=(tx, ty, tc), device_id_type=MESH)


def _scatter_start(gs, kinds, name):
    n = len(gs)
    pieces = [_piece_shape(g.shape, kind) for g, kind in zip(gs, kinds)]
    lands = [lax.empty((N_DEV,) + p, g.dtype) for p, g in zip(pieces, gs)]

    def body(*refs):
        g_refs, land_refs, send_sems, recv_sems = refs[:n], refs[n:2 * n], refs[2 * n], refs[2 * n + 1]
        land_outs, stages = refs[3 * n + 2:4 * n + 2], refs[4 * n + 2:]
        x, y, c = _me()
        for w in range(n):
            for r in range(1, N_DEV):
                _scatter_copy(w, r, kinds[w], g_refs[w], land_refs[w], send_sems, recv_sems).start()
        for w in range(n):
            pltpu.sync_copy(_piece_of(g_refs[w], kinds[w], 2 * x + y, c), stages[w])
            pltpu.sync_copy(stages[w], land_outs[w].at[4 * x + 2 * y + c])

    arrays = list(gs) + lands
    res = pl.pallas_call(
        body,
        out_shape=(pltpu.SemaphoreType.DMA((N_DEV * n,)), pltpu.SemaphoreType.DMA((N_DEV * n,)),
                   *[pltpu.HBM(a.shape, a.dtype) for a in arrays]),
        in_specs=[_HBM] * (2 * n),
        out_specs=(_SEM, _SEM, *[_HBM] * (2 * n)),
        input_output_aliases={i: 2 + i for i in range(2 * n)},
        scratch_shapes=[pltpu.VMEM(p, g.dtype) for p, g in zip(pieces, gs)],
        name=name,
        compiler_params=pltpu.CompilerParams(has_side_effects=_EFFECT, vmem_limit_bytes=VMEM_LIMIT),
    )(*[pltpu.with_memory_space_constraint(a, pltpu.HBM) for a in arrays])
    return res[0], res[1], res[2:2 + n], res[2 + n:]


def _scatter_wait(send_sems, recv_sems, gs, lands, kinds, after, name):
    n = len(gs)

    def body(*refs):
        g_refs, land_refs, send_sems, recv_sems = refs[:n], refs[n:2 * n], refs[2 * n], refs[2 * n + 1]
        for w in range(n):
            for r in range(1, N_DEV):
                cp = _scatter_copy(w, r, kinds[w], g_refs[w], land_refs[w], send_sems, recv_sems)
                cp.wait_send()
                cp.wait_recv()

    arrays = list(gs) + list(lands)
    return pl.pallas_call(
        body,
        out_shape=tuple(pltpu.HBM(a.shape, a.dtype) for a in arrays),
        in_specs=(*[_HBM] * (2 * n), _SEM, _SEM, pl.BlockSpec(memory_space=pl.ANY)),
        out_specs=tuple([_HBM] * (2 * n)),
        input_output_aliases={i: i for i in range(2 * n)},
        name=name,
        compiler_params=pltpu.CompilerParams(has_side_effects=_EFFECT),
    )(*arrays, send_sems, recv_sems, after)[n:]


def _swap_halves(halves, name):
    n = len(halves)

    def body(*refs):
        in_refs, out_refs = refs[:n], refs[n:2 * n]
        send_sems, recv_sems, local_sems = refs[2 * n:]
        x, y, c = _me()
        cps = []
        for w in range(n):
            lc = pltpu.make_async_copy(in_refs[w], out_refs[w].at[c], local_sems.at[w])
            lc.start()
            rc = pltpu.make_async_remote_copy(
                src_ref=in_refs[w], dst_ref=out_refs[w].at[c], send_sem=send_sems.at[w], recv_sem=recv_sems.at[w],
                device_id=(x, y, 1 - c), device_id_type=MESH)
            rc.start()
            cps.append((lc, rc))
        for lc, rc in cps:
            rc.wait_recv()
        for lc, rc in cps:
            rc.wait_send()
            lc.wait()

    vmem = pl.BlockSpec(memory_space=pltpu.VMEM)
    return pl.pallas_call(
        body,
        out_shape=[jax.ShapeDtypeStruct((2,) + h.shape, h.dtype) for h in halves],
        in_specs=[vmem] * n,
        out_specs=[vmem] * n,
        scratch_shapes=[pltpu.SemaphoreType.DMA((n,)), pltpu.SemaphoreType.DMA((n,)), pltpu.SemaphoreType.DMA((n,))],
        name=name,
        compiler_params=pltpu.CompilerParams(vmem_limit_bytes=VMEM_LIMIT),
    )(*halves)


def _to_streams(a, dil):
    if dil == 1:
        return a
    s, c = a.shape
    return a.reshape(s // dil, dil, c).transpose(1, 0, 2).reshape(s, c)


def _from_streams(a, dil):
    if dil == 1:
        return a
    s, c = a.shape
    return a.reshape(dil, s // dil, c).transpose(1, 0, 2).reshape(s, c)


def _mm_tiles(s):
    return min(s, 1024)


def _local_step(x0, target, mvec, ln_g, ln_b, small, fetch, emit, start):
    s, d = x0.shape
    tm = _mm_tiles(s)
    row = lambda v: v.reshape(1, -1)
    shift = [row(mvec[i, :d]) for i in range(4)]
    scale = [row(mvec[i, d:2 * d]) for i in range(4)]
    gate = [row(1.0 + mvec[i, 2 * d:]) for i in range(4)]
    lg = [row(ln_g[i]) for i in range(4)]
    lb = [row(ln_b[i]) for i in range(4)]
    mm = functools.partial(_mm, tm=tm)
    mm_w = functools.partial(_mm, tm=1024, tk=min(s, 2048), mode="tn")

    xs, ys, big = [x0], [], {}
    h0 = _mod(x0, scale[0], shift[0], start, "mod0")
    big["a_w_in"] = fetch("a_w_in", h0)
    uvpre = mm(h0, big["a_w_in"], mode="nn", name="a_in", outs=[F32], tn=512, tk=1024,
               epi=lambda r, bias: [r + bias], extras=[("row", small["a_b_in"])])
    gated = _spatial_fwd(uvpre, small["a_vn_g"], small["a_vn_b"], small["wc"], small["bias_full"], "a_spatial")
    big["a_w_out"] = fetch("a_w_out", gated)
    ys.append(mm(gated, big["a_w_out"], mode="nn", name="a_out", outs=[F32], tn=1024, tk=1024))
    x1, h1 = _resid_ln(xs[0], ys[0], gate[0], lg[0], lb[0], (scale[1], shift[1]), "ln0")
    xs.append(x1)
    relu2 = lambda r: [jnp.square(jnp.maximum(r, 0.0))]
    big["up0"] = fetch("up0", h1)
    r0 = mm(h1, big["up0"], mode="nn", name="up0", outs=[MXU_DTYPE], tn=1024, tk=1024, epi=relu2)
    big["down0"] = fetch("down0", r0)
    ys.append(mm(r0, big["down0"], mode="nn", name="down0", outs=[F32], tn=1024, tk=2048))
    x2, h2 = _resid_ln(xs[1], ys[1], gate[1], lg[1], lb[1], (scale[2], shift[2]), "ln1")
    xs.append(x2)
    hg, qkvs, o_g, l_g, l_streams = [], [], [], [], []
    big["b_w_qkv"] = fetch("b_w_qkv", h2)
    for g, (_, dil) in enumerate(B_PATTERNS):
        hp = _to_streams(h2, dil)
        qkv = mm(hp, big["b_w_qkv"], mode="nn", name=f"qkv{g}", outs=[MXU_DTYPE], tn=768, tk=1024, b_col0=g * 3 * d, n_out=3 * d)
        og, lgv = _attn_fwd(qkv, small["slopes"], dil, f"attn_fwd{g}")
        hg.append(hp)
        qkvs.append(qkv)
        o_g.append(_from_streams(og, dil))
        l_g.append(_from_streams(lgv, dil))
        l_streams.append(lgv)
    o_mix = _combine_fwd(o_g, l_g, "combine")
    big["b_w_out"] = fetch("b_w_out", o_mix)
    ys.append(mm(o_mix, big["b_w_out"], mode="nn", name="b_out", outs=[F32], tn=1024, tk=1024))
    x3, h3 = _resid_ln(xs[2], ys[2], gate[2], lg[2], lb[2], (scale[3], shift[3]), "ln2")
    xs.append(x3)
    big["up1"] = fetch("up1", h3)
    r1 = mm(h3, big["up1"], mode="nn", name="up1", outs=[MXU_DTYPE], tn=1024, tk=1024, epi=relu2)
    big["down1"] = fetch("down1", r1)
    ys.append(mm(r1, big["down1"], mode="nn", name="down1", outs=[F32], tn=1024, tk=2048))

    gb, red_ln, red_mod = {}, [None] * 4, [None] * 4

    def mlp_bwd(i, h, r, dyy):
        gb[f"down{i}"] = mm_w(r, dyy, name=f"g_down{i}", outs=[MXU_DTYPE], tn=1024)
        da = mm(dyy, big[f"down{i}"], mode="nt", name=f"d_down{i}", outs=[MXU_DTYPE], tn=1024, tk=1024,
                after=emit(f"down{i}", gb[f"down{i}"]),
                epi=lambda acc, rv: [acc * (2.0 * jnp.sqrt(rv.astype(F32)))], extras=[("full", r)])
        gb[f"up{i}"] = mm_w(h, da, name=f"g_up{i}", outs=[MXU_DTYPE], tn=1024)
        return [mm(da, big[f"up{i}"], mode="nt", name=f"d_up{i}", outs=[F32], tn=1024, tk=1024, after=emit(f"up{i}", gb[f"up{i}"]))]

    def join(sub, dxr, dhs, after=None):
        res = _mod_ln_bwd(dxr, dhs, xs[sub], scale[sub], xs[sub - 1], ys[sub - 1], gate[sub - 1], lg[sub - 1],
                          f"mod_ln_bwd{sub}", after=after)
        red_mod[sub], red_ln[sub - 1] = res[2], res[3]
        return res[0], res[1]

    loss, dxr, dyy, red_ln[3] = _last_ln_loss_bwd(xs[3], ys[3], gate[3], lg[3], lb[3], target, "ln3_loss_bwd")
    dxr, dyy = join(3, dxr, mlp_bwd(1, h3, r1, dyy))
    gb["b_w_out"] = mm_w(o_mix, dyy, name="g_b_out", outs=[MXU_DTYPE], tn=1024, tk=1024)
    do = mm(dyy, big["b_w_out"], mode="nt", name="d_b_out", outs=[F32], tn=1024, tk=1024, after=emit("b_w_out", gb["b_w_out"]))
    parts = _combine_bwd(do, o_mix, l_g, "combine_bwd")
    dhs, gq = [], None
    for g, (_, dil) in enumerate(B_PATTERNS):
        do_g, dd_g = _to_streams(parts[g][0], dil), _to_streams(parts[g][1], dil)
        dqkv = _attn_bwd(qkvs[g], do_g, l_streams[g], dd_g, small["slopes"], dil, f"attn_bwd{g}")
        gq = mm_w(hg[g], dqkv, name=f"g_qkv{g}", outs=[MXU_DTYPE], tn=1024, out_col0=g * 3 * d, out_cols=len(B_PATTERNS) * 3 * d, into=gq)
        dh = mm(dqkv, big["b_w_qkv"], mode="nt", name=f"d_qkv{g}", outs=[F32], tn=1024, tk=768, b_col0=g * 3 * d)
        dhs.append(_from_streams(dh, dil))
    gb["b_w_qkv"] = gq
    dxr, dyy = join(2, dxr, dhs, after=emit("b_w_qkv", gb["b_w_qkv"]))
    dxr, dyy = join(1, dxr, mlp_bwd(0, h1, r0, dyy))
    gb["a_w_out"] = mm_w(gated, dyy, name="g_a_out", outs=[MXU_DTYPE], tn=1024)
    dgated = mm(dyy, big["a_w_out"], mode="nt", name="d_a_out", outs=[F32], tn=1024, tk=1024, after=emit("a_w_out", gb["a_w_out"]))
    duv, dws, dbias, dbin, dvg, dvb = _spatial_bwd(uvpre, dgated, small["a_vn_g"], small["a_vn_b"], small["wc"],
                                                   small["wct"], small["bias_full"], "a_spatial_bwd")
    gb["a_w_in"] = mm_w(h0, duv, name="g_a_in", outs=[MXU_DTYPE], tn=1024)
    dh = mm(duv, big["a_w_in"], mode="nt", name="d_a_in", outs=[F32], tn=1024, tk=512, after=emit("a_w_in", gb["a_w_in"]))
    dx, red_mod[0] = _mod_bwd(dxr, [dh], xs[0], scale[0], "mod_bwd0")
    dm = [jnp.concatenate([red_mod[i][0], red_mod[i][1], red_ln[i][2]]) for i in range(4)]
    dlg, dlb = [red_ln[i][0] for i in range(4)], [red_ln[i][1] for i in range(4)]

    tril = jnp.tril(jnp.ones((CHUNK, CHUNK), bool))
    gsmall = {
        "a_b_in": dbin.reshape(-1), "a_vn_g": dvg.reshape(-1), "a_vn_b": dvb.reshape(-1),
        "a_w_s": jnp.where(tril, dws, 0.0).reshape(-1),
        "a_b_s": dbias.reshape(CHUNK, A_GROUPS, d // A_GROUPS).sum(-1).T.reshape(-1),
    }
    return loss, dx, gb, jnp.stack(dm), jnp.stack(dlg), jnp.stack(dlb), gsmall


BIG = ("a_w_in", "a_w_out", "up0", "down0", "b_w_qkv", "b_w_out", "up1", "down1")
BIG_KIND = {"a_w_in": "col", "a_w_out": "row", "b_w_qkv": "col", "b_w_out": "row",
            "up0": "col", "up1": "col", "down0": "row", "down1": "row"}
HALVED = ("a_w_in", "down0", "b_w_qkv")
SCATTER_GROUPS = (("down1", "up1"), ("b_w_out", "b_w_qkv"), ("down0", "up0"), ("a_w_out", "a_w_in"))
SMALL = ("a_b_in", "a_vn_g", "a_vn_b", "a_b_s", "a_w_s")


def kernel(x, c, ada_w, ada_b, ln_g, ln_b, a_w_in, a_b_in, a_vn_g, a_vn_b, a_w_s, a_b_s, a_w_out, b_w_qkv, b_w_out, mlp_w_up, mlp_w_down, loss_target, m_ada_w, m_ada_b, m_ln_g, m_ln_b, m_a_w_in, m_a_b_in, m_a_vn_g, m_a_vn_b, m_a_w_s, m_a_b_s, m_a_w_out, m_b_w_qkv, m_b_w_out, m_mlp_w_up, m_mlp_w_down, v_ada_w, v_ada_b, v_ln_g, v_ln_b, v_a_w_in, v_a_b_in, v_a_vn_g, v_a_vn_b, v_a_w_s, v_a_b_s, v_a_w_out, v_b_w_qkv, v_b_w_out, v_mlp_w_up, v_mlp_w_down):
    s, d = x.shape[1], x.shape[2]
    xi, yi, ci = _me()
    q = 2 * xi + yi
    dev = 2 * q + ci
    nsub = 2 * DEPTH
    cs = ada_w.shape[-1]
    ls = ln_g.shape[-1]

    shards = {
        "a_w_in": a_w_in[0], "a_w_out": a_w_out[0], "b_w_qkv": b_w_qkv[0], "b_w_out": b_w_out[0],
        "up0": mlp_w_up[0], "up1": mlp_w_up[1], "down0": mlp_w_down[0], "down1": mlp_w_down[1],
    }
    cast = [shards[k].astype(MXU_DTYPE) for k in BIG]

    pack = jnp.concatenate([c.reshape(-1), ln_g.reshape(-1), ln_b.reshape(-1)]).reshape(-1, LANES)
    got = _all_gather_small(pack, "gather_small", after=cast).reshape(N_DEV, -1)
    c_all = got[:, :d]
    per_chip = got[0::2]
    ln_g_full = per_chip[:, d:d + nsub * ls].reshape(N_CHIPS, nsub, ls).transpose(1, 0, 2).reshape(nsub, d)
    ln_b_full = per_chip[:, d + nsub * ls:].reshape(N_CHIPS, nsub, ls).transpose(1, 0, 2).reshape(nsub, d)
    m_part = _ada_fwd(c_all, ada_w.reshape(nsub, d, cs), ada_b.reshape(nsub, 1, cs), "ada_fwd")
    m_all = _all_gather_small(m_part.reshape(-1, LANES), "gather_mod").reshape(N_DEV, nsub, N_DEV, cs)
    m_mine = lax.dynamic_index_in_dim(m_all[0::2], dev, axis=2, keepdims=False)
    mvec = m_mine.transpose(1, 0, 2).reshape(nsub, 3 * d)

    halved = {BIG.index(k) for k in HALVED}
    send_sems, recv_sems, shard_thru, lands, token = _gather_start(cast, halved, mvec, "gather_start")

    def fetch(k, after):
        w = BIG.index(k)
        shard, gw = _gather_wait(w, shard_thru[w], lands[w], send_sems, recv_sems, after, f"gather_wait_{k}", w in halved)
        if w in halved:
            gw = _assemble_halves(shard, gw, f"assemble_{k}")
        return gw if BIG_KIND[k] == "col" else gw.reshape(1, -1, gw.shape[-1])

    scattering, pending = {}, {}

    def emit(k, g):
        pending[k] = g
        group = next(gr for gr in SCATTER_GROUPS if k in gr)
        if k != group[-1]:
            return None
        scattering[group] = _scatter_start([pending[m] for m in group], [BIG_KIND[m] for m in group], f"scatter_start_{k}")
        return scattering[group][2][0]

    tril = jnp.tril(jnp.ones((CHUNK, CHUNK), bool))
    wc = jnp.where(tril, a_w_s[0], 0.0).astype(MXU_DTYPE)
    heads = jnp.arange(1, B_HEADS + 1, dtype=F32)
    small = {
        "a_b_in": a_b_in, "a_vn_g": a_vn_g, "a_vn_b": a_vn_b,
        "wc": wc, "wct": wc.transpose(0, 2, 1),
        "bias_full": jnp.repeat(a_b_s[0].T, d // A_GROUPS, axis=1),
        "slopes": jnp.exp2(-8.0 * heads / B_HEADS),
    }

    loss_part, grad_x, gb, dm, dlg, dlb, gsmall = _local_step(x[0], loss_target[0], mvec, ln_g_full, ln_b_full, small, fetch, emit, token)
    loss = lax.psum(loss_part, ("x", "y", "c"))

    weights = dict(ada_w=ada_w, ada_b=ada_b, ln_g=ln_g, ln_b=ln_b, a_w_in=a_w_in, a_b_in=a_b_in, a_vn_g=a_vn_g, a_vn_b=a_vn_b,
                   a_w_s=a_w_s, a_b_s=a_b_s, a_w_out=a_w_out, b_w_qkv=b_w_qkv, b_w_out=b_w_out, mlp_w_up=mlp_w_up, mlp_w_down=mlp_w_down)
    ms = dict(ada_w=m_ada_w, ada_b=m_ada_b, ln_g=m_ln_g, ln_b=m_ln_b, a_w_in=m_a_w_in, a_b_in=m_a_b_in, a_vn_g=m_a_vn_g, a_vn_b=m_a_vn_b,
              a_w_s=m_a_w_s, a_b_s=m_a_b_s, a_w_out=m_a_w_out, b_w_qkv=m_b_w_qkv, b_w_out=m_b_w_out, mlp_w_up=m_mlp_w_up, mlp_w_down=m_mlp_w_down)
    vs = dict(ada_w=v_ada_w, ada_b=v_ada_b, ln_g=v_ln_g, ln_b=v_ln_b, a_w_in=v_a_w_in, a_b_in=v_a_b_in, a_vn_g=v_a_vn_g, a_vn_b=v_a_vn_b,
              a_w_s=v_a_w_s, a_b_s=v_a_b_s, a_w_out=v_a_w_out, b_w_qkv=v_b_w_qkv, b_w_out=v_b_w_out, mlp_w_up=v_mlp_w_up, mlp_w_down=v_mlp_w_down)
    grads, updates = {}, {}

    def update(k):
        updates[k] = _adamw(weights[k], grads[k], ms[k], vs[k], f"adamw_{k}")
        return updates[k][0]

    pack_b = jnp.concatenate([dm.reshape(-1), dlg.reshape(-1), dlb.reshape(-1)] + [gsmall[k] for k in SMALL])
    n_small = pack_b.shape[0]
    pack_b = jnp.pad(pack_b, (0, -n_small % (ROW_TILE * LANES)))
    got_b = _all_gather_small(pack_b.reshape(-1, LANES), "gather_small_grads").reshape(N_DEV, -1, LANES)
    tot = _sum_slots(got_b, "sum_small").reshape(-1)
    o = 0
    dm_tot = tot[o:o + nsub * 3 * d].reshape(nsub, 3 * d); o += nsub * 3 * d
    dlg_tot = tot[o:o + nsub * d].reshape(nsub, d); o += nsub * d
    dlb_tot = tot[o:o + nsub * d].reshape(nsub, d); o += nsub * d
    g_small = {}
    for k, ref in zip(SMALL, (a_b_in, a_vn_g, a_vn_b, a_b_s, a_w_s)):
        g_small[k] = tot[o:o + ref.size].reshape(ref.shape); o += ref.size
    assert o == n_small
    dm_all = got_b.reshape(N_DEV, -1)[:, :nsub * 3 * d].reshape(N_DEV, nsub, 3 * d)
    dm_cols = lax.dynamic_slice_in_dim(dm_all, q * cs, cs, axis=2).transpose(1, 0, 2)

    grads.update({
        "ada_w": _ada_bwd(c_all.T, dm_cols, "ada_bwd").reshape(ada_w.shape),
        "ada_b": lax.dynamic_slice_in_dim(dm_tot, q * cs, cs, axis=1).reshape(ada_b.shape),
        "ln_g": lax.dynamic_slice_in_dim(dlg_tot, q * ls, ls, axis=1).reshape(ln_g.shape),
        "ln_b": lax.dynamic_slice_in_dim(dlb_tot, q * ls, ls, axis=1).reshape(ln_b.shape),
        **g_small,
    })
    for k in ("ada_b", "ln_g", "ln_b") + SMALL:
        update(k)
    done = update("ada_w")

    gfull = {}
    for group in (SCATTER_GROUPS[0] + SCATTER_GROUPS[1], SCATTER_GROUPS[2] + SCATTER_GROUPS[3]):
        bufs = []
        for pair in (group[:2], group[2:]):
            bufs += _scatter_wait(*scattering[pair], [BIG_KIND[m] for m in pair], done, f"scatter_wait_{pair[-1]}")
        halves = [_sum_slots(b, f"sum_{k}") for k, b in zip(group, bufs)]
        fulls = _swap_halves(halves, f"swap_halves_{group[0]}")
        gfull.update({k: f.reshape(-1, f.shape[-1]) for k, f in zip(group, fulls)})
        if group[0] == "down1":
            grads["b_w_qkv"], grads["b_w_out"] = gfull["b_w_qkv"][None], gfull["b_w_out"][None]
            update("b_w_out")
            done = update("b_w_qkv")
    grads.update({
        "a_w_in": gfull["a_w_in"][None], "a_w_out": gfull["a_w_out"][None],
        "mlp_w_up": jnp.stack([gfull["up0"], gfull["up1"]]), "mlp_w_down": jnp.stack([gfull["down0"], gfull["down1"]]),
    })
    for k in ("a_w_in", "a_w_out", "mlp_w_up", "mlp_w_down"):
        update(k)
    names = list(weights)
    return (loss, grad_x[None], *[grads[k] for k in names], *[updates[k][0] for k in names],
            *[updates[k][1] for k in names], *[updates[k][2] for k in names])
```

```python
import functools
import math

import jax
import jax.numpy as jnp
from jax import lax
from jax.experimental import pallas as pl
from jax.experimental.pallas import tpu as pltpu

F32 = jnp.float32
MXU_DTYPE = jnp.bfloat16

DEPTH = 2
CHUNK = 128
A_GROUPS = 16
B_HEADS = 16
HEAD_DIM = 64
B_PATTERNS = ((128, 1), (512, 4), (2048, 16))
SPAN = 128
ALPHA = (2 * DEPTH) ** 0.25
LN_EPS = 1e-5
NEG = -1e30
ATT_SCALE = HEAD_DIM ** -0.5
ADAM_LR, ADAM_B1, ADAM_B2, ADAM_EPS, ADAM_WD, ADAM_STEP = 0.001, 0.9, 0.999, 1e-08, 0.01, 10

N_CHIPS = 4
N_DEV = 8
LANES = 128
SUBLANES = 8
VMEM_LIMIT = 52 * 1024 * 1024
ROW_TILE = 256
MESH = pl.DeviceIdType.MESH


def _cparams(sem):
    return pltpu.CompilerParams(dimension_semantics=sem, vmem_limit_bytes=VMEM_LIMIT)


def _fold8(v):
    r, c = v.shape
    return jnp.sum(v.reshape(r // SUBLANES, SUBLANES, c), axis=0)


def _gelu(x):
    c = math.sqrt(2.0 / math.pi)
    return 0.5 * x * (1.0 + jnp.tanh(c * (x + 0.044715 * (x * x * x))))


def _gelu_grad(x):
    c = math.sqrt(2.0 / math.pi)
    t = jnp.tanh(c * (x + 0.044715 * (x * x * x)))
    return 0.5 * (1.0 + t) + 0.5 * x * (1.0 - t * t) * c * (1.0 + 3.0 * 0.044715 * x * x)


def _dot(a, b, dims):
    return lax.dot_general(a.astype(MXU_DTYPE), b.astype(MXU_DTYPE), (dims, ((), ())), preferred_element_type=F32)


def _dot_nn(a, b):
    return _dot(a, b, ((1,), (0,)))


def _dot_nt(a, b):
    return _dot(a, b, ((1,), (1,)))


def _dot_tn(a, b):
    return _dot(a, b, ((0,), (0,)))


def _mm(a, b, *, mode, name, outs, tm, tn, tk, epi=None, extras=(), b_col0=0, n_out=None, after=None,
        out_col0=0, out_cols=None, into=None):
    if mode == "nn":
        m, kdim = a.shape
        p, kb, ns = b.shape
        assert kb == kdim and ns % tn == 0 and b_col0 % tn == 0
        n = n_out if n_out is not None else p * ns
        npt, j0 = ns // tn, b_col0 // tn
        a_spec = pl.BlockSpec((tm, tk), lambda i, j, k: (i, k))
        b_spec = pl.BlockSpec((None, tk, tn), lambda i, j, k: ((j + j0) // npt, k, (j + j0) % npt))
        dot = _dot_nn
    elif mode == "nt":
        m, kdim = a.shape
        p, n, ns = b.shape
        assert ns % tk == 0 and b_col0 % tk == 0
        npt, j0 = ns // tk, b_col0 // tk
        a_spec = pl.BlockSpec((tm, tk), lambda i, j, k: (i, k))
        b_spec = pl.BlockSpec((None, tn, tk), lambda i, j, k: ((k + j0) // npt, j, (k + j0) % npt))
        dot = _dot_nt
    else:
        kdim, m = a.shape
        kb, n = b.shape
        assert kb == kdim
        a_spec = pl.BlockSpec((tk, tm), lambda i, j, k: (k, i))
        b_spec = pl.BlockSpec((tk, tn), lambda i, j, k: (k, j))
        dot = _dot_tn
    assert m % tm == 0 and n % tn == 0 and kdim % tk == 0, (name, m, n, kdim, tm, tn, tk)
    nk = kdim // tk
    ex_specs, ex_arrays = [], []
    for kind, arr in extras:
        if kind == "row":
            ex_specs.append(pl.BlockSpec((1, tn), lambda i, j, k: (0, j)))
        else:
            ex_specs.append(pl.BlockSpec((tm, tn), lambda i, j, k: (i, j)))
        ex_arrays.append(arr)
    n_ex, n_o = len(ex_arrays), len(outs)
    deps = [d for d in (after, into) if d is not None]
    n_dep = len(deps)
    j_out = out_col0 // tn
    assert out_col0 % tn == 0 and (into is None or len(outs) == 1)

    def body(a_ref, b_ref, *rest):
        ex_refs, o_refs = rest[:n_ex], rest[n_ex + n_dep:n_ex + n_dep + n_o]
        k = pl.program_id(2)

        def finish(r):
            vals = epi(r, *[e[...] for e in ex_refs]) if epi is not None else [r]
            for o, v in zip(o_refs, vals):
                o[...] = v.astype(o.dtype)

        if nk == 1:
            finish(dot(a_ref[...], b_ref[...]))
            return
        acc = rest[n_ex + n_dep + n_o]

        @pl.when(k == 0)
        def _():
            acc[...] = dot(a_ref[...], b_ref[...])

        @pl.when((k > 0) & (k < nk - 1))
        def _():
            acc[...] += dot(a_ref[...], b_ref[...])

        @pl.when(k == nk - 1)
        def _():
            finish(acc[...] + dot(a_ref[...], b_ref[...]))

    res = pl.pallas_call(
        body,
        grid=(m // tm, n // tn, nk),
        in_specs=[a_spec, b_spec] + ex_specs + [pl.BlockSpec(memory_space=pl.ANY)] * n_dep,
        out_specs=[pl.BlockSpec((tm, tn), lambda i, j, k: (i, j + j_out)) for _ in outs],
        out_shape=[jax.ShapeDtypeStruct((m, out_cols or n), dt) for dt in outs],
        input_output_aliases={} if into is None else {2 + n_ex + n_dep - 1: 0},
        scratch_shapes=[pltpu.VMEM((tm, tn), F32)] if nk > 1 else [],
        name=name,
        compiler_params=_cparams(("parallel", "parallel", "arbitrary")),
    )(a, b, *ex_arrays, *deps)
    return res if len(outs) > 1 else res[0]


def _rows(body, n_rows, tr, ins, outs, name, scratch=()):
    def spec(kind, shape):
        if kind == "blk":
            return pl.BlockSpec((tr,) + tuple(shape[1:]), lambda i: (i,) + (0,) * (len(shape) - 1))
        if kind == "dep":
            return pl.BlockSpec(memory_space=pl.ANY)
        return pl.BlockSpec(tuple(shape), lambda i: (0,) * len(shape))

    return pl.pallas_call(
        body,
        grid=(n_rows // tr,),
        in_specs=[spec(k, a.shape) for k, a in ins],
        out_specs=[spec(k, s) for k, s, _ in outs],
        out_shape=[jax.ShapeDtypeStruct(tuple(s), d) for _, s, d in outs],
        scratch_shapes=list(scratch),
        name=name,
        compiler_params=_cparams(("arbitrary",)),
    )(*[a for _, a in ins])


def _ln_stats(z):
    mu = jnp.mean(z, axis=-1, keepdims=True)
    zc = z - mu
    var = jnp.mean(zc * zc, axis=-1, keepdims=True)
    rstd = lax.rsqrt(var + LN_EPS)
    return zc * rstd, rstd


def _mod(x, scale, shift, after, name):
    s, d = x.shape

    def body(x_ref, sc_ref, sh_ref, dep_ref, h_ref):
        h_ref[...] = (x_ref[...] * (1.0 + sc_ref[...]) + sh_ref[...]).astype(h_ref.dtype)

    return _rows(body, s, ROW_TILE, [("blk", x), ("all", scale), ("all", shift), ("dep", after)], [("blk", (s, d), MXU_DTYPE)], name)[0]


def _resid_ln(x, y, gate, g, b, nxt, name):
    s, d = x.shape

    def body(x_ref, y_ref, gate_ref, g_ref, b_ref, sc_ref, sh_ref, xn_ref, h_ref):
        z = ALPHA * x_ref[...] + gate_ref[...] * y_ref[...]
        xhat, _ = _ln_stats(z)
        xn = xhat * g_ref[...] + b_ref[...]
        xn_ref[...] = xn
        h_ref[...] = (xn * (1.0 + sc_ref[...]) + sh_ref[...]).astype(h_ref.dtype)

    return _rows(body, s, ROW_TILE,
                 [("blk", x), ("blk", y), ("all", gate), ("all", g), ("all", b), ("all", nxt[0]), ("all", nxt[1])],
                 [("blk", (s, d), F32), ("blk", (s, d), MXU_DTYPE)], name)


def _mod_bwd(dxr, dhs, x, scale, name, after=None):
    s, d = x.shape
    n_dh = len(dhs)
    n_dep = 0 if after is None else 1

    def body(dxr_ref, *rest):
        dh_refs = rest[:n_dh]
        x_ref, sc_ref, dx_ref, red_ref, a_sh, a_sc = rest[n_dh:n_dh + 2] + rest[n_dh + 2 + n_dep:]
        i = pl.program_id(0)

        @pl.when(i == 0)
        def _():
            a_sh[...] = jnp.zeros_like(a_sh)
            a_sc[...] = jnp.zeros_like(a_sc)

        dh = dh_refs[0][...]
        for r in dh_refs[1:]:
            dh = dh + r[...]
        dx_ref[...] = dxr_ref[...] + dh * (1.0 + sc_ref[...])
        a_sh[...] += _fold8(dh)
        a_sc[...] += _fold8(dh * x_ref[...])

        @pl.when(i == pl.num_programs(0) - 1)
        def _():
            red_ref[...] = jnp.zeros_like(red_ref)
            red_ref[0:1, :] = jnp.sum(a_sh[...], axis=0, keepdims=True)
            red_ref[1:2, :] = jnp.sum(a_sc[...], axis=0, keepdims=True)

    return _rows(body, s, ROW_TILE, [("blk", dxr)] + [("blk", h) for h in dhs] + [("blk", x), ("all", scale)] + [("dep", after)] * n_dep,
                 [("blk", (s, d), F32), ("all", (SUBLANES, d), F32)], name,
                 scratch=[pltpu.VMEM((SUBLANES, d), F32)] * 2)


def _last_ln_loss_bwd(x, y, gate, g, b, target, name):
    s, d = x.shape

    def body(x_ref, y_ref, gate_ref, g_ref, b_ref, t_ref, l_ref, dxr_ref, dyy_ref, red_ref, a_l, a_g, a_b, a_gate):
        i = pl.program_id(0)

        @pl.when(i == 0)
        def _():
            for a in (a_l, a_g, a_b, a_gate):
                a[...] = jnp.zeros_like(a)

        yv = y_ref[...]
        z = ALPHA * x_ref[...] + gate_ref[...] * yv
        xhat, rstd = _ln_stats(z)
        e = xhat * g_ref[...] + b_ref[...] - t_ref[...]
        a_l[...] += _fold8(e * e)
        dxo_v = e * (1.0 / d)
        dxh = dxo_v * g_ref[...]
        dz = rstd * (dxh - jnp.mean(dxh, axis=-1, keepdims=True) - xhat * jnp.mean(dxh * xhat, axis=-1, keepdims=True))
        dxr_ref[...] = ALPHA * dz
        dyy_ref[...] = (gate_ref[...] * dz).astype(dyy_ref.dtype)
        a_g[...] += _fold8(dxo_v * xhat)
        a_b[...] += _fold8(dxo_v)
        a_gate[...] += _fold8(dz * yv)

        @pl.when(i == pl.num_programs(0) - 1)
        def _():
            l_ref[...] = jnp.full(l_ref.shape, 0.5 / d, F32) * jnp.sum(a_l[...])
            red_ref[...] = jnp.zeros_like(red_ref)
            red_ref[0:1, :] = jnp.sum(a_g[...], axis=0, keepdims=True)
            red_ref[1:2, :] = jnp.sum(a_b[...], axis=0, keepdims=True)
            red_ref[2:3, :] = jnp.sum(a_gate[...], axis=0, keepdims=True)

    l, dxr, dyy, red = _rows(
        body, s, ROW_TILE, [("blk", x), ("blk", y), ("all", gate), ("all", g), ("all", b), ("blk", target)],
        [("all", (SUBLANES, LANES), F32), ("blk", (s, d), F32), ("blk", (s, d), MXU_DTYPE), ("all", (SUBLANES, d), F32)], name,
        scratch=[pltpu.VMEM((SUBLANES, d), F32)] * 4)
    return l[0, 0], dxr, dyy, red


def _mod_ln_bwd(dxr, dhs, x, scale, x_in, y, gate, g, name, after=None):
    s, d = x.shape
    n_dh = len(dhs)
    n_dep = 0 if after is None else 1

    def body(dxr_ref, *rest):
        dh_refs = rest[:n_dh]
        x_ref, sc_ref, xin_ref, y_ref, gate_ref, g_ref = rest[n_dh:n_dh + 6]
        dxr_out, dyy_ref, red_mod, red_ln, a_sh, a_sc, a_g, a_b, a_gate = rest[n_dh + 6 + n_dep:]
        i = pl.program_id(0)

        @pl.when(i == 0)
        def _():
            for a in (a_sh, a_sc, a_g, a_b, a_gate):
                a[...] = jnp.zeros_like(a)

        dh = dh_refs[0][...]
        for r in dh_refs[1:]:
            dh = dh + r[...]
        xv = x_ref[...]
        dxo_v = dxr_ref[...] + dh * (1.0 + sc_ref[...])
        a_sh[...] += _fold8(dh)
        a_sc[...] += _fold8(dh * xv)
        yv = y_ref[...]
        z = ALPHA * xin_ref[...] + gate_ref[...] * yv
        xhat, rstd = _ln_stats(z)
        dxh = dxo_v * g_ref[...]
        dz = rstd * (dxh - jnp.mean(dxh, axis=-1, keepdims=True) - xhat * jnp.mean(dxh * xhat, axis=-1, keepdims=True))
        dxr_out[...] = ALPHA * dz
        dyy_ref[...] = (gate_ref[...] * dz).astype(dyy_ref.dtype)
        a_g[...] += _fold8(dxo_v * xhat)
        a_b[...] += _fold8(dxo_v)
        a_gate[...] += _fold8(dz * yv)

        @pl.when(i == pl.num_programs(0) - 1)
        def _():
            red_mod[...] = jnp.zeros_like(red_mod)
            red_mod[0:1, :] = jnp.sum(a_sh[...], axis=0, keepdims=True)
            red_mod[1:2, :] = jnp.sum(a_sc[...], axis=0, keepdims=True)
            red_ln[...] = jnp.zeros_like(red_ln)
            red_ln[0:1, :] = jnp.sum(a_g[...], axis=0, keepdims=True)
            red_ln[1:2, :] = jnp.sum(a_b[...], axis=0, keepdims=True)
            red_ln[2:3, :] = jnp.sum(a_gate[...], axis=0, keepdims=True)

    ins = ([("blk", dxr)] + [("blk", h) for h in dhs]
           + [("blk", x), ("all", scale), ("blk", x_in), ("blk", y), ("all", gate), ("all", g)] + [("dep", after)] * n_dep)
    return _rows(body, s, ROW_TILE, ins,
                 [("blk", (s, d), F32), ("blk", (s, d), MXU_DTYPE), ("all", (SUBLANES, d), F32), ("all", (SUBLANES, d), F32)], name,
                 scratch=[pltpu.VMEM((SUBLANES, d), F32)] * 5)


def _left_half(shape):
    return lax.broadcasted_iota(jnp.int32, shape, 1) < (LANES // 2)


def _spatial_z(vn, wc_ref, bias_ref, j):
    vb = vn[:, j * LANES:(j + 1) * LANES]
    z0 = _dot_nn(wc_ref[2 * j], vb)
    z1 = _dot_nn(wc_ref[2 * j + 1], vb)
    return jnp.where(_left_half(z0.shape), z0, z1) + bias_ref[:, j * LANES:(j + 1) * LANES]


def _spatial_fwd(uvpre, vn_g, vn_b, wc, bias_full, name):
    s, d2 = uvpre.shape
    d = d2 // 2

    def body(uv_ref, g_ref, b_ref, wc_ref, bias_ref, out_ref):
        u = _gelu(uv_ref[:, :d])
        v = _gelu(uv_ref[:, d:])
        vh, _ = _ln_stats(v)
        vn = vh * g_ref[...] + b_ref[...]
        for j in range(d // LANES):
            z = _spatial_z(vn, wc_ref, bias_ref, j)
            out_ref[:, j * LANES:(j + 1) * LANES] = (u[:, j * LANES:(j + 1) * LANES] * z).astype(out_ref.dtype)

    return _rows(body, s, CHUNK, [("blk", uvpre), ("all", vn_g), ("all", vn_b), ("all", wc), ("all", bias_full)],
                 [("blk", (s, d), MXU_DTYPE)], name)[0]


def _spatial_bwd(uvpre, dgated, vn_g, vn_b, wc, wct, bias_full, name):
    s, d2 = uvpre.shape
    d = d2 // 2

    def body(uv_ref, dg_ref, g_ref, b_ref, wc_ref, wct_ref, bias_ref,
             duv_ref, dws_ref, dbias_ref, dbin_ref, dvg_ref, dvb_ref, dvn_buf, a_bin, a_vg, a_vb):
        i = pl.program_id(0)

        @pl.when(i == 0)
        def _():
            dws_ref[...] = jnp.zeros_like(dws_ref)
            dbias_ref[...] = jnp.zeros_like(dbias_ref)
            a_bin[...] = jnp.zeros_like(a_bin)
            a_vg[...] = jnp.zeros_like(a_vg)
            a_vb[...] = jnp.zeros_like(a_vb)

        up = uv_ref[:, :d]
        vp = uv_ref[:, d:]
        u = _gelu(up)
        v = _gelu(vp)
        vh, rstd = _ln_stats(v)
        vn = vh * g_ref[...] + b_ref[...]
        dg = dg_ref[...]
        dzz = dg * u
        dbias_ref[...] += dzz
        for j in range(d // LANES):
            cols = slice(j * LANES, (j + 1) * LANES)
            z = _spatial_z(vn, wc_ref, bias_ref, j)
            dup = dg[:, cols] * z * _gelu_grad(up[:, cols])
            duv_ref[:, cols] = dup.astype(duv_ref.dtype)
            a_bin[:, cols] += _fold8(dup)
            dzb = dzz[:, cols]
            left = _left_half(dzb.shape)
            dvn_buf[:, cols] = jnp.where(left, _dot_nn(wct_ref[2 * j], dzb), _dot_nn(wct_ref[2 * j + 1], dzb))
            vb = vn[:, cols]
            dws_ref[2 * j] += _dot_nt(jnp.where(left, dzb, 0.0), vb)
            dws_ref[2 * j + 1] += _dot_nt(jnp.where(left, 0.0, dzb), vb)
        dvn = dvn_buf[...]
        a_vg[...] += _fold8(dvn * vh)
        a_vb[...] += _fold8(dvn)
        dvh = dvn * g_ref[...]
        dv = rstd * (dvh - jnp.mean(dvh, axis=-1, keepdims=True) - vh * jnp.mean(dvh * vh, axis=-1, keepdims=True))
        dvp = dv * _gelu_grad(vp)
        duv_ref[:, d:] = dvp.astype(duv_ref.dtype)
        a_bin[:, d:] += _fold8(dvp)

        @pl.when(i == pl.num_programs(0) - 1)
        def _():
            dbin_ref[...] = jnp.sum(a_bin[...], axis=0, keepdims=True)
            dvg_ref[...] = jnp.sum(a_vg[...], axis=0, keepdims=True)
            dvb_ref[...] = jnp.sum(a_vb[...], axis=0, keepdims=True)

    return _rows(body, s, CHUNK,
                 [("blk", uvpre), ("blk", dgated), ("all", vn_g), ("all", vn_b), ("all", wc), ("all", wct), ("all", bias_full)],
                 [("blk", (s, d2), MXU_DTYPE), ("all", (A_GROUPS, CHUNK, CHUNK), F32), ("all", (CHUNK, d), F32),
                  ("all", (1, d2), F32), ("all", (1, d), F32), ("all", (1, d), F32)], name,
                 scratch=[pltpu.VMEM((CHUNK, d), F32), pltpu.VMEM((SUBLANES, d2), F32),
                          pltpu.VMEM((SUBLANES, d), F32), pltpu.VMEM((SUBLANES, d), F32)])


def _head_mask(v, h):
    lane = lax.broadcasted_iota(jnp.int32, v.shape, 1)
    return jnp.where((lane >= h * HEAD_DIM) & (lane < (h + 1) * HEAD_DIM), v, jnp.zeros_like(v))


def _att_bias(slopes, dil):
    qi = lax.broadcasted_iota(jnp.int32, (SPAN, SPAN), 0)
    ki = lax.broadcasted_iota(jnp.int32, (SPAN, SPAN), 1)
    sl = slopes[:, None, None]
    cur = jnp.where(ki <= qi, -sl * (float(dil) * (qi - ki).astype(F32)), NEG)
    prev = jnp.where(ki >= qi, -sl * (float(dil) * (SPAN + qi - ki).astype(F32)), NEG)
    absent = jnp.full_like(prev, NEG)
    pairs = slopes.shape[0] // 2

    def fwd(pv):
        return jnp.concatenate([cur, pv], axis=2).reshape(pairs, 2 * SPAN, 2 * SPAN)

    def bwd(pv):
        return jnp.concatenate([cur.reshape(pairs, 2 * SPAN, SPAN), pv.reshape(pairs, 2 * SPAN, SPAN)], axis=1)

    return jnp.stack([fwd(absent), fwd(prev)]), jnp.stack([bwd(absent), bwd(prev)])


def _att_specs(s, d, dil, kinds):
    nb = s // (dil * SPAN)

    def rowblk(which, b):
        if which == "prev":
            return jnp.where(b % nb == 0, b, b - 1)
        if which == "next":
            return jnp.where(b % nb == nb - 1, b, b + 1)
        return b

    return [pl.BlockSpec((SPAN, d), functools.partial(lambda b, o, w: (rowblk(w, b), o), o=part, w=which))
            for part, which in kinds]


def _head_col(v, head):
    return v[:, head:head + 1]


def _expand_heads(w, j):
    shape = (w.shape[0], LANES)
    return jnp.where(_left_half(shape), jnp.broadcast_to(_head_col(w, 2 * j), shape), jnp.broadcast_to(_head_col(w, 2 * j + 1), shape))


def _attn_fwd(qkv, slopes, dil, name):
    s, d3 = qkv.shape
    d = d3 // 3
    nb = s // (dil * SPAN)
    table, _ = _att_bias(slopes, dil)

    def body(q_ref, kc_ref, kp_ref, vc_ref, vp_ref, tb_ref, o_ref, l_ref):
        left = _left_half((SPAN, LANES))
        lane = lax.broadcasted_iota(jnp.int32, (SPAN, LANES), 1)
        lses = jnp.zeros((SPAN, LANES), F32)
        for hp in range(d // LANES):
            cols = slice(hp * LANES, (hp + 1) * LANES)
            q = q_ref[:, cols]
            q2 = jnp.concatenate([_head_mask(q, 0), _head_mask(q, 1)], axis=0) * ATT_SCALE
            k2 = jnp.concatenate([kc_ref[:, cols], kp_ref[:, cols]], axis=0)
            v2 = jnp.concatenate([vc_ref[:, cols], vp_ref[:, cols]], axis=0)
            sc = _dot_nt(q2, k2) + tb_ref[hp]
            m = jnp.max(sc, axis=-1, keepdims=True)
            p = jnp.exp(sc - m)
            l = jnp.sum(p, axis=-1, keepdims=True)
            r = _dot_nn(p, v2) * (1.0 / l)
            lse = m + jnp.log(l)
            o_ref[:, cols] = jnp.where(left, r[:SPAN], r[SPAN:])
            lses = jnp.where(lane == 2 * hp, lse[:SPAN], jnp.where(lane == 2 * hp + 1, lse[SPAN:], lses))
        l_ref[...] = lses

    specs = _att_specs(s, d, dil, [(0, "cur"), (1, "cur"), (1, "prev"), (2, "cur"), (2, "prev")])
    tbl = pl.BlockSpec((None,) + table.shape[1:], lambda b: (jnp.where(b % nb == 0, 0, 1), 0, 0, 0))
    out_spec = pl.BlockSpec((SPAN, d), lambda b: (b, 0))
    return pl.pallas_call(
        body,
        grid=(s // SPAN,),
        in_specs=specs + [tbl],
        out_specs=[out_spec, pl.BlockSpec((SPAN, LANES), lambda b: (b, 0))],
        out_shape=[jax.ShapeDtypeStruct((s, d), F32), jax.ShapeDtypeStruct((s, LANES), F32)],
        name=name,
        compiler_params=_cparams(("parallel",)),
    )(qkv, qkv, qkv, qkv, qkv, table)


def _attn_bwd(qkv, do, lse, dd, slopes, dil, name):
    s, d3 = qkv.shape
    d = d3 // 3
    nb = s // (dil * SPAN)
    _, table = _att_bias(slopes, dil)

    def heads_stacked(cur, nxt):
        return jnp.concatenate([_head_mask(cur, 0), _head_mask(cur, 1), _head_mask(nxt, 0), _head_mask(nxt, 1)], axis=0)

    def cols_stacked(cur, nxt, hp):
        return jnp.concatenate([jnp.broadcast_to(_head_col(a, 2 * hp + h), (SPAN, LANES)) for a in (cur, nxt) for h in range(2)], axis=0)

    def body(k_ref, v_ref, qc_ref, qn_ref, doc_ref, don_ref, lc_ref, ln_ref, ddc_ref, ddn_ref, tb_ref, out_ref, carry):
        b = pl.program_id(0)

        @pl.when(b == 0)
        def _():
            carry[...] = jnp.zeros_like(carry)

        left = _left_half((SPAN, LANES))
        lse_c, lse_n, dd_c, dd_n = lc_ref[...], ln_ref[...], ddc_ref[...], ddn_ref[...]
        for hp in range(d // LANES):
            cols = slice(hp * LANES, (hp + 1) * LANES)
            k, v = k_ref[:, cols], v_ref[:, cols]
            q4 = heads_stacked(qc_ref[:, cols], qn_ref[:, cols])
            do4 = heads_stacked(doc_ref[:, cols], don_ref[:, cols])
            sc = _dot_nt(q4 * ATT_SCALE, k) + tb_ref[hp]
            p = jnp.exp(sc - cols_stacked(lse_c, lse_n, hp))
            ds = p * (_dot_nt(do4, v) - cols_stacked(dd_c, dd_n, hp))
            dq4 = _dot_nn(ds, k)
            dq_cur = jnp.where(left, dq4[:SPAN], dq4[SPAN:2 * SPAN]) + carry[:, cols]
            carry[:, cols] = jnp.where(left, dq4[2 * SPAN:3 * SPAN], dq4[3 * SPAN:])
            out_ref[:, cols] = (dq_cur * ATT_SCALE).astype(out_ref.dtype)
            out_ref[:, d + hp * LANES:d + (hp + 1) * LANES] = (_dot_tn(ds, q4) * ATT_SCALE).astype(out_ref.dtype)
            out_ref[:, 2 * d + hp * LANES:2 * d + (hp + 1) * LANES] = _dot_tn(p, do4).astype(out_ref.dtype)

    qkv_specs = _att_specs(s, d, dil, [(1, "cur"), (2, "cur"), (0, "cur"), (0, "next")])
    pair = _att_specs(s, d, dil, [(0, "cur"), (0, "next")])
    heads = _att_specs(s, LANES, dil, [(0, "cur"), (0, "next")])
    tbl = pl.BlockSpec((None,) + table.shape[1:], lambda b: (jnp.where(b % nb == nb - 1, 0, 1), 0, 0, 0))
    return pl.pallas_call(
        body,
        grid=(s // SPAN,),
        in_specs=qkv_specs + pair + heads + heads + [tbl],
        out_specs=pl.BlockSpec((SPAN, d3), lambda b: (b, 0)),
        out_shape=jax.ShapeDtypeStruct((s, d3), MXU_DTYPE),
        scratch_shapes=[pltpu.VMEM((SPAN, d), F32)],
        name=name,
        compiler_params=_cparams(("arbitrary",)),
    )(qkv, qkv, qkv, qkv, do, do, lse, lse, dd, dd, table)


def _mix_weights(l_refs):
    ls = [r[...] for r in l_refs]
    m = functools.reduce(jnp.maximum, ls)
    es = [jnp.exp(l - m) for l in ls]
    tot = functools.reduce(lambda a, c: a + c, es)
    return [e / tot for e in es]


def _combine_fwd(os_, ls_, name):
    s, d = os_[0].shape
    n = len(os_)

    def body(*refs):
        o_refs, l_refs, out_ref = refs[:n], refs[n:2 * n], refs[2 * n]
        ws = _mix_weights(l_refs)
        for j in range(d // LANES):
            cols = slice(j * LANES, (j + 1) * LANES)
            acc = _expand_heads(ws[0], j) * o_refs[0][:, cols]
            for w, o in zip(ws[1:], o_refs[1:]):
                acc = acc + _expand_heads(w, j) * o[:, cols]
            out_ref[:, cols] = acc

    return _rows(body, s, ROW_TILE, [("blk", a) for a in os_ + ls_], [("blk", (s, d), F32)], name)[0]


def _combine_bwd(do, o, ls_, name):
    s, d = o.shape
    n = len(ls_)
    sel = (lax.broadcasted_iota(jnp.int32, (d, LANES), 0) // HEAD_DIM == lax.broadcasted_iota(jnp.int32, (d, LANES), 1)).astype(F32)

    def body(do_ref, o_ref, *rest):
        l_refs, sel_ref, outs = rest[:n], rest[n], rest[n + 1:]
        ws = _mix_weights(l_refs)
        dov = do_ref[...]
        r = jnp.dot(dov * o_ref[...], sel_ref[...], precision=lax.Precision.HIGHEST, preferred_element_type=F32)
        for g in range(n):
            outs[2 * g + 1][...] = ws[g] * r
            for j in range(d // LANES):
                cols = slice(j * LANES, (j + 1) * LANES)
                outs[2 * g][:, cols] = (_expand_heads(ws[g], j) * dov[:, cols]).astype(outs[2 * g].dtype)

    outs = []
    for _ in range(n):
        outs += [("blk", (s, d), MXU_DTYPE), ("blk", (s, LANES), F32)]
    res = _rows(body, s, ROW_TILE, [("blk", do), ("blk", o)] + [("blk", l) for l in ls_] + [("all", sel)], outs, name)
    return [(res[2 * g], res[2 * g + 1]) for g in range(n)]


def _ada_fwd(c_all, w, b, name):
    nsub, d, cs = w.shape

    def body(c_ref, w_ref, b_ref, o_ref):
        cv = c_ref[...]
        sc = cv * (1.0 / (1.0 + jnp.exp(-cv)))
        o_ref[...] = _dot_nn(sc, w_ref[...]) + b_ref[...]

    return pl.pallas_call(
        body,
        grid=(nsub,),
        in_specs=[pl.BlockSpec(c_all.shape, lambda i: (0, 0)), pl.BlockSpec((None, d, cs), lambda i: (i, 0, 0)),
                  pl.BlockSpec((None, 1, cs), lambda i: (i, 0, 0))],
        out_specs=pl.BlockSpec((None, N_DEV, cs), lambda i: (i, 0, 0)),
        out_shape=jax.ShapeDtypeStruct((nsub, N_DEV, cs), F32),
        name=name,
        compiler_params=_cparams(("parallel",)),
    )(c_all, w, b)


def _ada_bwd(c_all_t, dm, name):
    d, nb = c_all_t.shape
    nsub, _, cs = dm.shape

    def body(c_ref, dm_ref, o_ref):
        cv = c_ref[...]
        sc = cv * (1.0 / (1.0 + jnp.exp(-cv)))
        acc = sc[:, 0:1] * dm_ref[0:1, :]
        for bi in range(1, nb):
            acc = acc + sc[:, bi:bi + 1] * dm_ref[bi:bi + 1, :]
        o_ref[...] = acc

    return pl.pallas_call(
        body,
        grid=(nsub,),
        in_specs=[pl.BlockSpec(c_all_t.shape, lambda i: (0, 0)), pl.BlockSpec((None, nb, cs), lambda i: (i, 0, 0))],
        out_specs=pl.BlockSpec((None, d, cs), lambda i: (i, 0, 0)),
        out_shape=jax.ShapeDtypeStruct((nsub, d, cs), F32),
        name=name,
        compiler_params=_cparams(("parallel",)),
    )(c_all_t, dm)


def _row_tile(r, row_elems):
    t = 2 * SUBLANES
    if r % t:
        return r
    while t * 2 * row_elems <= 256 * 1024 and r % (t * 2) == 0:
        t *= 2
    return t


def _adamw(w, g, m, v, name):
    shape = w.shape
    c = shape[-1]
    r = w.size // c
    tr = _row_tile(r, c)
    w2, g2, m2, v2 = [a.reshape(r, c) for a in (w, g, m, v)]
    bc1 = 1.0 - ADAM_B1 ** ADAM_STEP
    bc2 = 1.0 - ADAM_B2 ** ADAM_STEP

    def body(w_ref, g_ref, m_ref, v_ref, d_ref, nm_ref, nv_ref):
        gv = g_ref[...]
        nm = ADAM_B1 * m_ref[...] + (1.0 - ADAM_B1) * gv
        nv = ADAM_B2 * v_ref[...] + (1.0 - ADAM_B2) * (gv * gv)
        d_ref[...] = -ADAM_LR * ((nm / bc1) / (jnp.sqrt(nv / bc2) + ADAM_EPS) + ADAM_WD * w_ref[...])
        nm_ref[...] = nm
        nv_ref[...] = nv

    res = _rows(body, r, tr, [("blk", a) for a in (w2, g2, m2, v2)], [("blk", (r, c), F32)] * 3, name)
    return [a.reshape(shape) for a in res]


def _sum_slots(buf, name):
    n, r, c = buf.shape
    tr = _row_tile(r, n * c)

    def body(b_ref, o_ref):
        acc = b_ref[0].astype(F32)
        for k in range(1, n):
            acc = acc + b_ref[k].astype(F32)
        o_ref[...] = acc

    return pl.pallas_call(
        body,
        grid=(r // tr,),
        in_specs=[pl.BlockSpec((n, tr, c), lambda i: (0, i, 0))],
        out_specs=pl.BlockSpec((tr, c), lambda i: (i, 0)),
        out_shape=jax.ShapeDtypeStruct((r, c), F32),
        name=name,
        compiler_params=_cparams(("parallel",)),
    )(buf)


def _me():
    return lax.axis_index("x"), lax.axis_index("y"), lax.axis_index("c")


def _all_gather_small(blk, name, after=()):
    m_per, n = blk.shape

    def body(x_ref, *rest):
        out_ref, send_sems, recv_sems, local_sem = rest[len(after):]
        x, y, c = _me()
        me, sibling = (x, y, c), (x, y, 1 - c)
        chips = [(1 - x, y), (x, 1 - y), (1 - x, 1 - y)]

        def rows(px, py, pc):
            return out_ref.at[pl.ds((4 * px + 2 * py + pc) * m_per, m_per), :]

        def copy(k, block, to, src=None):
            return pltpu.make_async_remote_copy(
                src_ref=rows(*block) if src is None else src, dst_ref=rows(*block),
                send_sem=send_sems.at[k], recv_sem=recv_sems.at[k], device_id=to, device_id_type=MESH)

        mine = pltpu.make_async_copy(x_ref, rows(*me), local_sem)
        mine.start()
        first = [copy(0, me, sibling, src=x_ref)]
        first += [copy(1 + j, me, (*chip, c), src=x_ref) for j, chip in enumerate(chips)]
        for cp in first:
            cp.start()
        passed = [copy(4 + j, (*chip, c), sibling) for j, chip in enumerate(chips)]
        for j, chip in enumerate(chips):
            copy(1 + j, (*chip, c), me).wait_recv()
            passed[j].start()
        copy(0, sibling, me).wait_recv()
        for j, chip in enumerate(chips):
            copy(4 + j, (*chip, 1 - c), me).wait_recv()
        for cp in first + passed:
            cp.wait_send()
        mine.wait()

    return pl.pallas_call(
        body,
        out_shape=jax.ShapeDtypeStruct((N_DEV * m_per, n), blk.dtype),
        in_specs=[pl.BlockSpec(memory_space=pltpu.VMEM)] + [pl.BlockSpec(memory_space=pl.ANY)] * len(after),
        out_specs=pl.BlockSpec(memory_space=pltpu.VMEM),
        scratch_shapes=[pltpu.SemaphoreType.DMA((7,)), pltpu.SemaphoreType.DMA((7,)), pltpu.SemaphoreType.DMA],
        name=name,
        compiler_params=pltpu.CompilerParams(vmem_limit_bytes=VMEM_LIMIT),
    )(blk, *after)


_HBM = pl.BlockSpec(memory_space=pltpu.HBM)
_SEM = pl.BlockSpec(memory_space=pltpu.SEMAPHORE)
_EFFECT = pltpu.SideEffectType.DATAFLOW_SIDE_EFFECTING


def _other_chips(x, y):
    return [(1 - x, y), (x, 1 - y), (1 - x, 1 - y)]


def _gather_copy(w, j, src_ref, land_ref, send_sems, recv_sems, halved=False):
    x, y, c = _me()
    if halved:
        half = src_ref.shape[0] // 2
        src_ref = src_ref.at[pl.ds(c * half, half), :]
    return pltpu.make_async_remote_copy(
        src_ref=src_ref, dst_ref=land_ref.at[2 * x + y], send_sem=send_sems.at[3 * w + j], recv_sem=recv_sems.at[3 * w + j],
        device_id=(*_other_chips(x, y)[j], c), device_id_type=MESH)


def _gather_start(shards, halved, after, name):
    n = len(shards)
    lands = [lax.empty((N_CHIPS, s.shape[0] // 2 if w in halved else s.shape[0], s.shape[1]), s.dtype) for w, s in enumerate(shards)]

    def body(*refs):
        in_refs, land_refs = refs[:n], refs[n:2 * n]
        send_sems, recv_sems = refs[2 * n + 1], refs[2 * n + 2]
        token = refs[-1]
        for w in range(n):
            for j in range(3):
                _gather_copy(w, j, in_refs[w], land_refs[w], send_sems, recv_sems, w in halved).start()
        token[...] = jnp.zeros_like(token)

    res = pl.pallas_call(
        body,
        out_shape=(pltpu.SemaphoreType.DMA((3 * n,)), pltpu.SemaphoreType.DMA((3 * n,)),
                   *[pltpu.HBM(s.shape, s.dtype) for s in shards], *[pltpu.HBM(l.shape, l.dtype) for l in lands],
                   jax.ShapeDtypeStruct((SUBLANES, LANES), F32)),
        in_specs=[_HBM] * (2 * n) + [pl.BlockSpec(memory_space=pl.ANY)],
        out_specs=(_SEM, _SEM, *[_HBM] * (2 * n), pl.BlockSpec(memory_space=pltpu.VMEM)),
        input_output_aliases={i: 2 + i for i in range(2 * n)},
        name=name,
        compiler_params=pltpu.CompilerParams(has_side_effects=_EFFECT),
    )(*[pltpu.with_memory_space_constraint(a, pltpu.HBM) for a in list(shards) + lands], after)
    return res[0], res[1], res[2:2 + n], res[2 + n:2 + 2 * n], res[-1]


def _gather_wait(w, shard, land, send_sems, recv_sems, after, name, halved=False):
    def body(s_ref, land_ref, send_sems, recv_sems, after_ref, s_out, land_out, stage):
        x, y, _ = _me()
        if not halved:
            pltpu.sync_copy(s_ref, stage)
            pltpu.sync_copy(stage, land_out.at[2 * x + y])
        for j in range(3):
            cp = _gather_copy(w, j, s_ref, land_ref, send_sems, recv_sems, halved)
            cp.wait_send()
            cp.wait_recv()

    return pl.pallas_call(
        body,
        out_shape=(pltpu.HBM(shard.shape, shard.dtype), pltpu.HBM(land.shape, land.dtype)),
        in_specs=(_HBM, _HBM, _SEM, _SEM, pl.BlockSpec(memory_space=pl.ANY)),
        out_specs=(_HBM, _HBM),
        input_output_aliases={0: 0, 1: 1},
        scratch_shapes=[pltpu.VMEM((SUBLANES, LANES) if halved else shard.shape, shard.dtype)],
        name=name,
        compiler_params=pltpu.CompilerParams(has_side_effects=_EFFECT, vmem_limit_bytes=VMEM_LIMIT),
    )(shard, land, send_sems, recv_sems, after)


def _assemble_halves(shard, land, name):
    half = land.shape[1]

    def body(s_ref, land_ref, out_ref, send_sems, recv_sems, local_sems):
        x, y, c = _me()
        own = pltpu.make_async_copy(s_ref, out_ref.at[2 * x + y], local_sems.at[3])
        own.start()
        cps = []
        for j, (ox, oy) in enumerate(_other_chips(x, y)):
            qj = 2 * ox + oy
            mine = out_ref.at[qj, pl.ds(c * half, half), :]
            lc = pltpu.make_async_copy(land_ref.at[qj], mine, local_sems.at[j])
            lc.start()
            rc = pltpu.make_async_remote_copy(
                src_ref=land_ref.at[qj], dst_ref=mine, send_sem=send_sems.at[j], recv_sem=recv_sems.at[j],
                device_id=(x, y, 1 - c), device_id_type=MESH)
            rc.start()
            cps.append((lc, rc))
        for lc, rc in cps:
            rc.wait_recv()
        for lc, rc in cps:
            rc.wait_send()
            lc.wait()
        own.wait()

    vmem = pl.BlockSpec(memory_space=pltpu.VMEM)
    return pl.pallas_call(
        body,
        out_shape=jax.ShapeDtypeStruct((N_CHIPS,) + shard.shape, shard.dtype),
        in_specs=[vmem, vmem],
        out_specs=vmem,
        scratch_shapes=[pltpu.SemaphoreType.DMA((3,)), pltpu.SemaphoreType.DMA((3,)), pltpu.SemaphoreType.DMA((4,))],
        name=name,
        compiler_params=pltpu.CompilerParams(vmem_limit_bytes=VMEM_LIMIT),
    )(shard, land)


def _piece_shape(shape, kind):
    k, nn = shape
    return (k // 2, nn // N_CHIPS) if kind == "col" else (k // N_CHIPS // 2, nn)


def _piece_of(g_ref, kind, tq, tc):
    pr, pc = _piece_shape(g_ref.shape, kind)
    if kind == "col":
        return g_ref.at[pl.ds(tc * pr, pr), pl.ds(tq * pc, pc)]
    return g_ref.at[pl.ds((2 * tq + tc) * pr, pr), :]


def _scatter_copy(w, r, kind, g_ref, land_ref, send_sems, recv_sems):
    x, y, c = _me()
    tx, ty, tc = (x + ((r >> 2) & 1)) % 2, (y + ((r >> 1) & 1)) % 2, (c + (r & 1)) % 2
    return pltpu.make_async_remote_copy(
        src_ref=_piece_of(g_ref, kind, 2 * tx + ty, tc), dst_ref=land_ref.at[4 * x + 2 * y + c],
        send_sem=send_sems.at[N_DEV * w + r], recv_sem=recv_sems.at[N_DEV * w + r], device_id=(tx, ty, tc), device_id_type=MESH)


def _scatter_start(gs, kinds, name):
    n = len(gs)
    pieces = [_piece_shape(g.shape, kind) for g, kind in zip(gs, kinds)]
    lands = [lax.empty((N_DEV,) + p, g.dtype) for p, g in zip(pieces, gs)]

    def body(*refs):
        g_refs, land_refs, send_sems, recv_sems = refs[:n], refs[n:2 * n], refs[2 * n], refs[2 * n + 1]
        land_outs, stages = refs[3 * n + 2:4 * n + 2], refs[4 * n + 2:]
        x, y, c = _me()
        for w in range(n):
            for r in range(1, N_DEV):
                _scatter_copy(w, r, kinds[w], g_refs[w], land_refs[w], send_sems, recv_sems).start()
        for w in range(n):
            pltpu.sync_copy(_piece_of(g_refs[w], kinds[w], 2 * x + y, c), stages[w])
            pltpu.sync_copy(stages[w], land_outs[w].at[4 * x + 2 * y + c])

    arrays = list(gs) + lands
    res = pl.pallas_call(
        body,
        out_shape=(pltpu.SemaphoreType.DMA((N_DEV * n,)), pltpu.SemaphoreType.DMA((N_DEV * n,)),
                   *[pltpu.HBM(a.shape, a.dtype) for a in arrays]),
        in_specs=[_HBM] * (2 * n),
        out_specs=(_SEM, _SEM, *[_HBM] * (2 * n)),
        input_output_aliases={i: 2 + i for i in range(2 * n)},
        scratch_shapes=[pltpu.VMEM(p, g.dtype) for p, g in zip(pieces, gs)],
        name=name,
        compiler_params=pltpu.CompilerParams(has_side_effects=_EFFECT, vmem_limit_bytes=VMEM_LIMIT),
    )(*[pltpu.with_memory_space_constraint(a, pltpu.HBM) for a in arrays])
    return res[0], res[1], res[2:2 + n], res[2 + n:]


def _scatter_wait(send_sems, recv_sems, gs, lands, kinds, after, name):
    n = len(gs)

    def body(*refs):
        g_refs, land_refs, send_sems, recv_sems = refs[:n], refs[n:2 * n], refs[2 * n], refs[2 * n + 1]
        for w in range(n):
            for r in range(1, N_DEV):
                cp = _scatter_copy(w, r, kinds[w], g_refs[w], land_refs[w], send_sems, recv_sems)
                cp.wait_send()
                cp.wait_recv()

    arrays = list(gs) + list(lands)
    return pl.pallas_call(
        body,
        out_shape=tuple(pltpu.HBM(a.shape, a.dtype) for a in arrays),
        in_specs=(*[_HBM] * (2 * n), _SEM, _SEM, pl.BlockSpec(memory_space=pl.ANY)),
        out_specs=tuple([_HBM] * (2 * n)),
        input_output_aliases={i: i for i in range(2 * n)},
        name=name,
        compiler_params=pltpu.CompilerParams(has_side_effects=_EFFECT),
    )(*arrays, send_sems, recv_sems, after)[n:]


def _swap_halves(halves, name):
    n = len(halves)

    def body(*refs):
        in_refs, out_refs = refs[:n], refs[n:2 * n]
        send_sems, recv_sems, local_sems = refs[2 * n:]
        x, y, c = _me()
        cps = []
        for w in range(n):
            lc = pltpu.make_async_copy(in_refs[w], out_refs[w].at[c], local_sems.at[w])
            lc.start()
            rc = pltpu.make_async_remote_copy(
                src_ref=in_refs[w], dst_ref=out_refs[w].at[c], send_sem=send_sems.at[w], recv_sem=recv_sems.at[w],
                device_id=(x, y, 1 - c), device_id_type=MESH)
            rc.start()
            cps.append((lc, rc))
        for lc, rc in cps:
            rc.wait_recv()
        for lc, rc in cps:
            rc.wait_send()
            lc.wait()

    vmem = pl.BlockSpec(memory_space=pltpu.VMEM)
    return pl.pallas_call(
        body,
        out_shape=[jax.ShapeDtypeStruct((2,) + h.shape, h.dtype) for h in halves],
        in_specs=[vmem] * n,
        out_specs=[vmem] * n,
        scratch_shapes=[pltpu.SemaphoreType.DMA((n,)), pltpu.SemaphoreType.DMA((n,)), pltpu.SemaphoreType.DMA((n,))],
        name=name,
        compiler_params=pltpu.CompilerParams(vmem_limit_bytes=VMEM_LIMIT),
    )(*halves)


def _to_streams(a, dil):
    if dil == 1:
        return a
    s, c = a.shape
    return a.reshape(s // dil, dil, c).transpose(1, 0, 2).reshape(s, c)


def _from_streams(a, dil):
    if dil == 1:
        return a
    s, c = a.shape
    return a.reshape(dil, s // dil, c).transpose(1, 0, 2).reshape(s, c)


def _mm_tiles(s):
    return min(s, 1024)


def _local_step(x0, target, mvec, ln_g, ln_b, small, fetch, emit, start):
    s, d = x0.shape
    tm = _mm_tiles(s)
    row = lambda v: v.reshape(1, -1)
    shift = [row(mvec[i, :d]) for i in range(4)]
    scale = [row(mvec[i, d:2 * d]) for i in range(4)]
    gate = [row(1.0 + mvec[i, 2 * d:]) for i in range(4)]
    lg = [row(ln_g[i]) for i in range(4)]
    lb = [row(ln_b[i]) for i in range(4)]
    mm = functools.partial(_mm, tm=tm)
    mm_w = functools.partial(_mm, tm=1024, tk=min(s, 2048), mode="tn")

    xs, ys, big = [x0], [], {}
    h0 = _mod(x0, scale[0], shift[0], start, "mod0")
    big["a_w_in"] = fetch("a_w_in", h0)
    uvpre = mm(h0, big["a_w_in"], mode="nn", name="a_in", outs=[F32], tn=512, tk=1024,
               epi=lambda r, bias: [r + bias], extras=[("row", small["a_b_in"])])
    gated = _spatial_fwd(uvpre, small["a_vn_g"], small["a_vn_b"], small["wc"], small["bias_full"], "a_spatial")
    big["a_w_out"] = fetch("a_w_out", gated)
    ys.append(mm(gated, big["a_w_out"], mode="nn", name="a_out", outs=[F32], tn=1024, tk=1024))
    x1, h1 = _resid_ln(xs[0], ys[0], gate[0], lg[0], lb[0], (scale[1], shift[1]), "ln0")
    xs.append(x1)
    relu2 = lambda r: [jnp.square(jnp.maximum(r, 0.0))]
    big["up0"] = fetch("up0", h1)
    r0 = mm(h1, big["up0"], mode="nn", name="up0", outs=[MXU_DTYPE], tn=1024, tk=1024, epi=relu2)
    big["down0"] = fetch("down0", r0)
    ys.append(mm(r0, big["down0"], mode="nn", name="down0", outs=[F32], tn=1024, tk=2048))
    x2, h2 = _resid_ln(xs[1], ys[1], gate[1], lg[1], lb[1], (scale[2], shift[2]), "ln1")
    xs.append(x2)
    hg, qkvs, o_g, l_g, l_streams = [], [], [], [], []
    big["b_w_qkv"] = fetch("b_w_qkv", h2)
    for g, (_, dil) in enumerate(B_PATTERNS):
        hp = _to_streams(h2, dil)
        qkv = mm(hp, big["b_w_qkv"], mode="nn", name=f"qkv{g}", outs=[MXU_DTYPE], tn=768, tk=1024, b_col0=g * 3 * d, n_out=3 * d)
        og, lgv = _attn_fwd(qkv, small["slopes"], dil, f"attn_fwd{g}")
        hg.append(hp)
        qkvs.append(qkv)
        o_g.append(_from_streams(og, dil))
        l_g.append(_from_streams(lgv, dil))
        l_streams.append(lgv)
    o_mix = _combine_fwd(o_g, l_g, "combine")
    big["b_w_out"] = fetch("b_w_out", o_mix)
    ys.append(mm(o_mix, big["b_w_out"], mode="nn", name="b_out", outs=[F32], tn=1024, tk=1024))
    x3, h3 = _resid_ln(xs[2], ys[2], gate[2], lg[2], lb[2], (scale[3], shift[3]), "ln2")
    xs.append(x3)
    big["up1"] = fetch("up1", h3)
    r1 = mm(h3, big["up1"], mode="nn", name="up1", outs=[MXU_DTYPE], tn=1024, tk=1024, epi=relu2)
    big["down1"] = fetch("down1", r1)
    ys.append(mm(r1, big["down1"], mode="nn", name="down1", outs=[F32], tn=1024, tk=2048))

    gb, red_ln, red_mod = {}, [None] * 4, [None] * 4

    def mlp_bwd(i, h, r, dyy):
        gb[f"down{i}"] = mm_w(r, dyy, name=f"g_down{i}", outs=[MXU_DTYPE], tn=1024)
        da = mm(dyy, big[f"down{i}"], mode="nt", name=f"d_down{i}", outs=[MXU_DTYPE], tn=1024, tk=1024,
                after=emit(f"down{i}", gb[f"down{i}"]),
                epi=lambda acc, rv: [acc * (2.0 * jnp.sqrt(rv.astype(F32)))], extras=[("full", r)])
        gb[f"up{i}"] = mm_w(h, da, name=f"g_up{i}", outs=[MXU_DTYPE], tn=1024)
        return [mm(da, big[f"up{i}"], mode="nt", name=f"d_up{i}", outs=[F32], tn=1024, tk=1024, after=emit(f"up{i}", gb[f"up{i}"]))]

    def join(sub, dxr, dhs, after=None):
        res = _mod_ln_bwd(dxr, dhs, xs[sub], scale[sub], xs[sub - 1], ys[sub - 1], gate[sub - 1], lg[sub - 1],
                          f"mod_ln_bwd{sub}", after=after)
        red_mod[sub], red_ln[sub - 1] = res[2], res[3]
        return res[0], res[1]

    loss, dxr, dyy, red_ln[3] = _last_ln_loss_bwd(xs[3], ys[3], gate[3], lg[3], lb[3], target, "ln3_loss_bwd")
    dxr, dyy = join(3, dxr, mlp_bwd(1, h3, r1, dyy))
    gb["b_w_out"] = mm_w(o_mix, dyy, name="g_b_out", outs=[MXU_DTYPE], tn=1024, tk=1024)
    do = mm(dyy, big["b_w_out"], mode="nt", name="d_b_out", outs=[F32], tn=1024, tk=1024, after=emit("b_w_out", gb["b_w_out"]))
    parts = _combine_bwd(do, o_mix, l_g, "combine_bwd")
    dhs, gq = [], None
    for g, (_, dil) in enumerate(B_PATTERNS):
        do_g, dd_g = _to_streams(parts[g][0], dil), _to_streams(parts[g][1], dil)
        dqkv = _attn_bwd(qkvs[g], do_g, l_streams[g], dd_g, small["slopes"], dil, f"attn_bwd{g}")
        gq = mm_w(hg[g], dqkv, name=f"g_qkv{g}", outs=[MXU_DTYPE], tn=1024, out_col0=g * 3 * d, out_cols=len(B_PATTERNS) * 3 * d, into=gq)
        dh = mm(dqkv, big["b_w_qkv"], mode="nt", name=f"d_qkv{g}", outs=[F32], tn=1024, tk=768, b_col0=g * 3 * d)
        dhs.append(_from_streams(dh, dil))
    gb["b_w_qkv"] = gq
    dxr, dyy = join(2, dxr, dhs, after=emit("b_w_qkv", gb["b_w_qkv"]))
    dxr, dyy = join(1, dxr, mlp_bwd(0, h1, r0, dyy))
    gb["a_w_out"] = mm_w(gated, dyy, name="g_a_out", outs=[MXU_DTYPE], tn=1024)
    dgated = mm(dyy, big["a_w_out"], mode="nt", name="d_a_out", outs=[F32], tn=1024, tk=1024, after=emit("a_w_out", gb["a_w_out"]))
    duv, dws, dbias, dbin, dvg, dvb = _spatial_bwd(uvpre, dgated, small["a_vn_g"], small["a_vn_b"], small["wc"],
                                                   small["wct"], small["bias_full"], "a_spatial_bwd")
    gb["a_w_in"] = mm_w(h0, duv, name="g_a_in", outs=[MXU_DTYPE], tn=1024)
    dh = mm(duv, big["a_w_in"], mode="nt", name="d_a_in", outs=[F32], tn=1024, tk=512, after=emit("a_w_in", gb["a_w_in"]))
    dx, red_mod[0] = _mod_bwd(dxr, [dh], xs[0], scale[0], "mod_bwd0")
    dm = [jnp.concatenate([red_mod[i][0], red_mod[i][1], red_ln[i][2]]) for i in range(4)]
    dlg, dlb = [red_ln[i][0] for i in range(4)], [red_ln[i][1] for i in range(4)]

    tril = jnp.tril(jnp.ones((CHUNK, CHUNK), bool))
    gsmall = {
        "a_b_in": dbin.reshape(-1), "a_vn_g": dvg.reshape(-1), "a_vn_b": dvb.reshape(-1),
        "a_w_s": jnp.where(tril, dws, 0.0).reshape(-1),
        "a_b_s": dbias.reshape(CHUNK, A_GROUPS, d // A_GROUPS).sum(-1).T.reshape(-1),
    }
    return loss, dx, gb, jnp.stack(dm), jnp.stack(dlg), jnp.stack(dlb), gsmall


BIG = ("a_w_in", "a_w_out", "up0", "down0", "b_w_qkv", "b_w_out", "up1", "down1")
BIG_KIND = {"a_w_in": "col", "a_w_out": "row", "b_w_qkv": "col", "b_w_out": "row",
            "up0": "col", "up1": "col", "down0": "row", "down1": "row"}
HALVED = ("a_w_in", "down0", "b_w_qkv")
SCATTER_GROUPS = (("down1", "up1"), ("b_w_out", "b_w_qkv"), ("down0", "up0"), ("a_w_out", "a_w_in"))
SMALL = ("a_b_in", "a_vn_g", "a_vn_b", "a_b_s", "a_w_s")


def kernel(x, c, ada_w, ada_b, ln_g, ln_b, a_w_in, a_b_in, a_vn_g, a_vn_b, a_w_s, a_b_s, a_w_out, b_w_qkv, b_w_out, mlp_w_up, mlp_w_down, loss_target, m_ada_w, m_ada_b, m_ln_g, m_ln_b, m_a_w_in, m_a_b_in, m_a_vn_g, m_a_vn_b, m_a_w_s, m_a_b_s, m_a_w_out, m_b_w_qkv, m_b_w_out, m_mlp_w_up, m_mlp_w_down, v_ada_w, v_ada_b, v_ln_g, v_ln_b, v_a_w_in, v_a_b_in, v_a_vn_g, v_a_vn_b, v_a_w_s, v_a_b_s, v_a_w_out, v_b_w_qkv, v_b_w_out, v_mlp_w_up, v_mlp_w_down):
    s, d = x.shape[1], x.shape[2]
    xi, yi, ci = _me()
    q = 2 * xi + yi
    dev = 2 * q + ci
    nsub = 2 * DEPTH
    cs = ada_w.shape[-1]
    ls = ln_g.shape[-1]

    shards = {
        "a_w_in": a_w_in[0], "a_w_out": a_w_out[0], "b_w_qkv": b_w_qkv[0], "b_w_out": b_w_out[0],
        "up0": mlp_w_up[0], "up1": mlp_w_up[1], "down0": mlp_w_down[0], "down1": mlp_w_down[1],
    }
    cast = [shards[k].astype(MXU_DTYPE) for k in BIG]

    pack = jnp.concatenate([c.reshape(-1), ln_g.reshape(-1), ln_b.reshape(-1)]).reshape(-1, LANES)
    got = _all_gather_small(pack, "gather_small", after=cast).reshape(N_DEV, -1)
    c_all = got[:, :d]
    per_chip = got[0::2]
    ln_g_full = per_chip[:, d:d + nsub * ls].reshape(N_CHIPS, nsub, ls).transpose(1, 0, 2).reshape(nsub, d)
    ln_b_full = per_chip[:, d + nsub * ls:].reshape(N_CHIPS, nsub, ls).transpose(1, 0, 2).reshape(nsub, d)
    m_part = _ada_fwd(c_all, ada_w.reshape(nsub, d, cs), ada_b.reshape(nsub, 1, cs), "ada_fwd")
    m_all = _all_gather_small(m_part.reshape(-1, LANES), "gather_mod").reshape(N_DEV, nsub, N_DEV, cs)
    m_mine = lax.dynamic_index_in_dim(m_all[0::2], dev, axis=2, keepdims=False)
    mvec = m_mine.transpose(1, 0, 2).reshape(nsub, 3 * d)

    halved = {BIG.index(k) for k in HALVED}
    send_sems, recv_sems, shard_thru, lands, token = _gather_start(cast, halved, mvec, "gather_start")

    def fetch(k, after):
        w = BIG.index(k)
        shard, gw = _gather_wait(w, shard_thru[w], lands[w], send_sems, recv_sems, after, f"gather_wait_{k}", w in halved)
        if w in halved:
            gw = _assemble_halves(shard, gw, f"assemble_{k}")
        return gw if BIG_KIND[k] == "col" else gw.reshape(1, -1, gw.shape[-1])

    scattering, pending = {}, {}

    def emit(k, g):
        pending[k] = g
        group = next(gr for gr in SCATTER_GROUPS if k in gr)
        if k != group[-1]:
            return None
        scattering[group] = _scatter_start([pending[m] for m in group], [BIG_KIND[m] for m in group], f"scatter_start_{k}")
        return scattering[group][2][0]

    tril = jnp.tril(jnp.ones((CHUNK, CHUNK), bool))
    wc = jnp.where(tril, a_w_s[0], 0.0).astype(MXU_DTYPE)
    heads = jnp.arange(1, B_HEADS + 1, dtype=F32)
    small = {
        "a_b_in": a_b_in, "a_vn_g": a_vn_g, "a_vn_b": a_vn_b,
        "wc": wc, "wct": wc.transpose(0, 2, 1),
        "bias_full": jnp.repeat(a_b_s[0].T, d // A_GROUPS, axis=1),
        "slopes": jnp.exp2(-8.0 * heads / B_HEADS),
    }

    loss_part, grad_x, gb, dm, dlg, dlb, gsmall = _local_step(x[0], loss_target[0], mvec, ln_g_full, ln_b_full, small, fetch, emit, token)
    loss = lax.psum(loss_part, ("x", "y", "c"))

    weights = dict(ada_w=ada_w, ada_b=ada_b, ln_g=ln_g, ln_b=ln_b, a_w_in=a_w_in, a_b_in=a_b_in, a_vn_g=a_vn_g, a_vn_b=a_vn_b,
                   a_w_s=a_w_s, a_b_s=a_b_s, a_w_out=a_w_out, b_w_qkv=b_w_qkv, b_w_out=b_w_out, mlp_w_up=mlp_w_up, mlp_w_down=mlp_w_down)
    ms = dict(ada_w=m_ada_w, ada_b=m_ada_b, ln_g=m_ln_g, ln_b=m_ln_b, a_w_in=m_a_w_in, a_b_in=m_a_b_in, a_vn_g=m_a_vn_g, a_vn_b=m_a_vn_b,
              a_w_s=m_a_w_s, a_b_s=m_a_b_s, a_w_out=m_a_w_out, b_w_qkv=m_b_w_qkv, b_w_out=m_b_w_out, mlp_w_up=m_mlp_w_up, mlp_w_down=m_mlp_w_down)
    vs = dict(ada_w=v_ada_w, ada_b=v_ada_b, ln_g=v_ln_g, ln_b=v_ln_b, a_w_in=v_a_w_in, a_b_in=v_a_b_in, a_vn_g=v_a_vn_g, a_vn_b=v_a_vn_b,
              a_w_s=v_a_w_s, a_b_s=v_a_b_s, a_w_out=v_a_w_out, b_w_qkv=v_b_w_qkv, b_w_out=v_b_w_out, mlp_w_up=v_mlp_w_up, mlp_w_down=v_mlp_w_down)
    grads, updates = {}, {}

    def update(k):
        updates[k] = _adamw(weights[k], grads[k], ms[k], vs[k], f"adamw_{k}")
        return updates[k][0]

    pack_b = jnp.concatenate([dm.reshape(-1), dlg.reshape(-1), dlb.reshape(-1)] + [gsmall[k] for k in SMALL])
    n_small = pack_b.shape[0]
    pack_b = jnp.pad(pack_b, (0, -n_small % (ROW_TILE * LANES)))
    got_b = _all_gather_small(pack_b.reshape(-1, LANES), "gather_small_grads").reshape(N_DEV, -1, LANES)
    tot = _sum_slots(got_b, "sum_small").reshape(-1)
    o = 0
    dm_tot = tot[o:o + nsub * 3 * d].reshape(nsub, 3 * d); o += nsub * 3 * d
    dlg_tot = tot[o:o + nsub * d].reshape(nsub, d); o += nsub * d
    dlb_tot = tot[o:o + nsub * d].reshape(nsub, d); o += nsub * d
    g_small = {}
    for k, ref in zip(SMALL, (a_b_in, a_vn_g, a_vn_b, a_b_s, a_w_s)):
        g_small[k] = tot[o:o + ref.size].reshape(ref.shape); o += ref.size
    assert o == n_small
    dm_all = got_b.reshape(N_DEV, -1)[:, :nsub * 3 * d].reshape(N_DEV, nsub, 3 * d)
    dm_cols = lax.dynamic_slice_in_dim(dm_all, q * cs, cs, axis=2).transpose(1, 0, 2)

    grads.update({
        "ada_w": _ada_bwd(c_all.T, dm_cols, "ada_bwd").reshape(ada_w.shape),
        "ada_b": lax.dynamic_slice_in_dim(dm_tot, q * cs, cs, axis=1).reshape(ada_b.shape),
        "ln_g": lax.dynamic_slice_in_dim(dlg_tot, q * ls, ls, axis=1).reshape(ln_g.shape),
        "ln_b": lax.dynamic_slice_in_dim(dlb_tot, q * ls, ls, axis=1).reshape(ln_b.shape),
        **g_small,
    })
    for k in ("ada_b", "ln_g", "ln_b") + SMALL:
        update(k)
    done = update("ada_w")

    gfull = {}
    for group in (SCATTER_GROUPS[0] + SCATTER_GROUPS[1], SCATTER_GROUPS[2] + SCATTER_GROUPS[3]):
        bufs = []
        for pair in (group[:2], group[2:]):
            bufs += _scatter_wait(*scattering[pair], [BIG_KIND[m] for m in pair], done, f"scatter_wait_{pair[-1]}")
        halves = [_sum_slots(b, f"sum_{k}") for k, b in zip(group, bufs)]
        fulls = _swap_halves(halves, f"swap_halves_{group[0]}")
        gfull.update({k: f.reshape(-1, f.shape[-1]) for k, f in zip(group, fulls)})
        if group[0] == "down1":
            grads["b_w_qkv"], grads["b_w_out"] = gfull["b_w_qkv"][None], gfull["b_w_out"][None]
            update("b_w_out")
            done = update("b_w_qkv")
    grads.update({
        "a_w_in": gfull["a_w_in"][None], "a_w_out": gfull["a_w_out"][None],
        "mlp_w_up": jnp.stack([gfull["up0"], gfull["up1"]]), "mlp_w_down": jnp.stack([gfull["down0"], gfull["down1"]]),
    })
    for k in ("a_w_in", "a_w_out", "mlp_w_up", "mlp_w_down"):
        update(k)
    names = list(weights)
    return (loss, grad_x[None], *[grads[k] for k in names], *[updates[k][0] for k in names],
            *[updates[k][1] for k in names], *[updates[k][2] for k in names])
```

```python
import functools
import math

import jax
import jax.numpy as jnp
from jax import lax
from jax.experimental import pallas as pl
from jax.experimental.pallas import tpu as pltpu

F32 = jnp.float32
MXU_DTYPE = jnp.bfloat16

DEPTH = 2
CHUNK = 128
A_GROUPS = 16
B_HEADS = 16
HEAD_DIM = 64
B_PATTERNS = ((128, 1), (512, 4), (2048, 16))
SPAN = 128
ALPHA = (2 * DEPTH) ** 0.25
LN_EPS = 1e-5
NEG = -1e30
ATT_SCALE = HEAD_DIM ** -0.5
ADAM_LR, ADAM_B1, ADAM_B2, ADAM_EPS, ADAM_WD, ADAM_STEP = 0.001, 0.9, 0.999, 1e-08, 0.01, 10

N_CHIPS = 4
N_DEV = 8
LANES = 128
SUBLANES = 8
VMEM_LIMIT = 52 * 1024 * 1024
ROW_TILE = 256
MM_ROW_CHUNK = 256
MESH = pl.DeviceIdType.MESH


def _cparams(sem):
    return pltpu.CompilerParams(dimension_semantics=sem, vmem_limit_bytes=VMEM_LIMIT)


def _fold8(v):
    r, c = v.shape
    return jnp.sum(v.reshape(r // SUBLANES, SUBLANES, c), axis=0)


def _gelu(x):
    c = math.sqrt(2.0 / math.pi)
    return 0.5 * x * (1.0 + jnp.tanh(c * (x + 0.044715 * (x * x * x))))


def _gelu_grad(x):
    c = math.sqrt(2.0 / math.pi)
    t = jnp.tanh(c * (x + 0.044715 * (x * x * x)))
    return 0.5 * (1.0 + t) + 0.5 * x * (1.0 - t * t) * c * (1.0 + 3.0 * 0.044715 * x * x)


def _dot(a, b, dims):
    return lax.dot_general(a.astype(MXU_DTYPE), b.astype(MXU_DTYPE), (dims, ((), ())), preferred_element_type=F32)


def _dot_nn(a, b):
    return _dot(a, b, ((1,), (0,)))


def _dot_nt(a, b):
    return _dot(a, b, ((1,), (1,)))


def _dot_tn(a, b):
    return _dot(a, b, ((0,), (0,)))


def _mm(a, b, *, mode, name, outs, tm, tn, tk, epi=None, extras=(), b_col0=0, n_out=None, after=None,
        out_col0=0, out_cols=None, into=None):
    if mode == "nn":
        m, kdim = a.shape
        p, kb, ns = b.shape
        assert kb == kdim and ns % tn == 0 and b_col0 % tn == 0
        n = n_out if n_out is not None else p * ns
        npt, j0 = ns // tn, b_col0 // tn
        a_spec = pl.BlockSpec((tm, tk), lambda i, j, k: (i, k))
        b_spec = pl.BlockSpec((None, tk, tn), lambda i, j, k: ((j + j0) // npt, k, (j + j0) % npt))
        dot = _dot_nn
    elif mode == "nt":
        m, kdim = a.shape
        p, n, ns = b.shape
        assert ns % tk == 0 and b_col0 % tk == 0
        npt, j0 = ns // tk, b_col0 // tk
        a_spec = pl.BlockSpec((tm, tk), lambda i, j, k: (i, k))
        b_spec = pl.BlockSpec((None, tn, tk), lambda i, j, k: ((k + j0) // npt, j, (k + j0) % npt))
        dot = _dot_nt
    else:
        kdim, m = a.shape
        kb, n = b.shape
        assert kb == kdim
        a_spec = pl.BlockSpec((tk, tm), lambda i, j, k: (k, i))
        b_spec = pl.BlockSpec((tk, tn), lambda i, j, k: (k, j))
        dot = _dot_tn
    assert m % tm == 0 and n % tn == 0 and kdim % tk == 0, (name, m, n, kdim, tm, tn, tk)
    nk = kdim // tk
    ex_specs, ex_arrays = [], []
    for kind, arr in extras:
        if kind == "row":
            ex_specs.append(pl.BlockSpec((1, tn), lambda i, j, k: (0, j)))
        else:
            ex_specs.append(pl.BlockSpec((tm, tn), lambda i, j, k: (i, j)))
        ex_arrays.append(arr)
    n_ex, n_o = len(ex_arrays), len(outs)
    deps = [d for d in (after, into) if d is not None]
    n_dep = len(deps)
    j_out = out_col0 // tn
    assert out_col0 % tn == 0 and (into is None or len(outs) == 1)

    def body(a_ref, b_ref, *rest):
        ex_refs, o_refs = rest[:n_ex], rest[n_ex + n_dep:n_ex + n_dep + n_o]
        k = pl.program_id(2)

        chunks = [slice(r0, r0 + min(tm, MM_ROW_CHUNK)) for r0 in range(0, tm, min(tm, MM_ROW_CHUNK))]

        def part(rows):
            return dot(a_ref[:, rows] if mode == "tn" else a_ref[rows, :], b_ref[...])

        def finish(r, rows):
            exs = [e[...] if kind == "row" else e[rows, :] for (kind, _), e in zip(extras, ex_refs)]
            vals = epi(r, *exs) if epi is not None else [r]
            for o, v in zip(o_refs, vals):
                o[rows, :] = v.astype(o.dtype)

        if nk == 1:
            for rows in chunks:
                finish(part(rows), rows)
            return
        acc = rest[n_ex + n_dep + n_o]

        @pl.when(k == 0)
        def _():
            for rows in chunks:
                acc[rows, :] = part(rows)

        @pl.when((k > 0) & (k < nk - 1))
        def _():
            for rows in chunks:
                acc[rows, :] += part(rows)

        @pl.when(k == nk - 1)
        def _():
            for rows in chunks:
                finish(acc[rows, :] + part(rows), rows)

    res = pl.pallas_call(
        body,
        grid=(m // tm, n // tn, nk),
        in_specs=[a_spec, b_spec] + ex_specs + [pl.BlockSpec(memory_space=pl.ANY)] * n_dep,
        out_specs=[pl.BlockSpec((tm, tn), lambda i, j, k: (i, j + j_out)) for _ in outs],
        out_shape=[jax.ShapeDtypeStruct((m, out_cols or n), dt) for dt in outs],
        input_output_aliases={} if into is None else {2 + n_ex + n_dep - 1: 0},
        scratch_shapes=[pltpu.VMEM((tm, tn), F32)] if nk > 1 else [],
        name=name,
        compiler_params=_cparams(("parallel", "parallel", "arbitrary")),
    )(a, b, *ex_arrays, *deps)
    return res if len(outs) > 1 else res[0]


def _rows(body, n_rows, tr, ins, outs, name, scratch=()):
    def spec(kind, shape):
        if kind == "blk":
            return pl.BlockSpec((tr,) + tuple(shape[1:]), lambda i: (i,) + (0,) * (len(shape) - 1))
        if kind == "dep":
            return pl.BlockSpec(memory_space=pl.ANY)
        return pl.BlockSpec(tuple(shape), lambda i: (0,) * len(shape))

    return pl.pallas_call(
        body,
        grid=(n_rows // tr,),
        in_specs=[spec(k, a.shape) for k, a in ins],
        out_specs=[spec(k, s) for k, s, _ in outs],
        out_shape=[jax.ShapeDtypeStruct(tuple(s), d) for _, s, d in outs],
        scratch_shapes=list(scratch),
        name=name,
        compiler_params=_cparams(("arbitrary",)),
    )(*[a for _, a in ins])


def _ln_stats(z):
    mu = jnp.mean(z, axis=-1, keepdims=True)
    zc = z - mu
    var = jnp.mean(zc * zc, axis=-1, keepdims=True)
    rstd = lax.rsqrt(var + LN_EPS)
    return zc * rstd, rstd


def _mod(x, scale, shift, after, name):
    s, d = x.shape

    def body(x_ref, sc_ref, sh_ref, dep_ref, h_ref):
        h_ref[...] = (x_ref[...] * (1.0 + sc_ref[...]) + sh_ref[...]).astype(h_ref.dtype)

    return _rows(body, s, ROW_TILE, [("blk", x), ("all", scale), ("all", shift), ("dep", after)], [("blk", (s, d), MXU_DTYPE)], name)[0]


def _resid_ln(x, y, gate, g, b, nxt, name):
    s, d = x.shape

    def body(x_ref, y_ref, gate_ref, g_ref, b_ref, sc_ref, sh_ref, xn_ref, h_ref):
        z = ALPHA * x_ref[...] + gate_ref[...] * y_ref[...]
        xhat, _ = _ln_stats(z)
        xn = xhat * g_ref[...] + b_ref[...]
        xn_ref[...] = xn
        h_ref[...] = (xn * (1.0 + sc_ref[...]) + sh_ref[...]).astype(h_ref.dtype)

    return _rows(body, s, ROW_TILE,
                 [("blk", x), ("blk", y), ("all", gate), ("all", g), ("all", b), ("all", nxt[0]), ("all", nxt[1])],
                 [("blk", (s, d), F32), ("blk", (s, d), MXU_DTYPE)], name)


def _mod_bwd(dxr, dhs, x, scale, name, after=None):
    s, d = x.shape
    n_dh = len(dhs)
    n_dep = 0 if after is None else 1

    def body(dxr_ref, *rest):
        dh_refs = rest[:n_dh]
        x_ref, sc_ref, dx_ref, red_ref, a_sh, a_sc = rest[n_dh:n_dh + 2] + rest[n_dh + 2 + n_dep:]
        i = pl.program_id(0)

        @pl.when(i == 0)
        def _():
            a_sh[...] = jnp.zeros_like(a_sh)
            a_sc[...] = jnp.zeros_like(a_sc)

        dh = dh_refs[0][...]
        for r in dh_refs[1:]:
            dh = dh + r[...]
        dx_ref[...] = dxr_ref[...] + dh * (1.0 + sc_ref[...])
        a_sh[...] += _fold8(dh)
        a_sc[...] += _fold8(dh * x_ref[...])

        @pl.when(i == pl.num_programs(0) - 1)
        def _():
            red_ref[...] = jnp.zeros_like(red_ref)
            red_ref[0:1, :] = jnp.sum(a_sh[...], axis=0, keepdims=True)
            red_ref[1:2, :] = jnp.sum(a_sc[...], axis=0, keepdims=True)

    return _rows(body, s, ROW_TILE, [("blk", dxr)] + [("blk", h) for h in dhs] + [("blk", x), ("all", scale)] + [("dep", after)] * n_dep,
                 [("blk", (s, d), F32), ("all", (SUBLANES, d), F32)], name,
                 scratch=[pltpu.VMEM((SUBLANES, d), F32)] * 2)


def _last_ln_loss_bwd(x, y, gate, g, b, target, name):
    s, d = x.shape

    def body(x_ref, y_ref, gate_ref, g_ref, b_ref, t_ref, l_ref, dxr_ref, dyy_ref, red_ref, a_l, a_g, a_b, a_gate):
        i = pl.program_id(0)

        @pl.when(i == 0)
        def _():
            for a in (a_l, a_g, a_b, a_gate):
                a[...] = jnp.zeros_like(a)

        yv = y_ref[...]
        z = ALPHA * x_ref[...] + gate_ref[...] * yv
        xhat, rstd = _ln_stats(z)
        e = xhat * g_ref[...] + b_ref[...] - t_ref[...]
        a_l[...] += _fold8(e * e)
        dxo_v = e * (1.0 / d)
        dxh = dxo_v * g_ref[...]
        dz = rstd * (dxh - jnp.mean(dxh, axis=-1, keepdims=True) - xhat * jnp.mean(dxh * xhat, axis=-1, keepdims=True))
        dxr_ref[...] = ALPHA * dz
        dyy_ref[...] = (gate_ref[...] * dz).astype(dyy_ref.dtype)
        a_g[...] += _fold8(dxo_v * xhat)
        a_b[...] += _fold8(dxo_v)
        a_gate[...] += _fold8(dz * yv)

        @pl.when(i == pl.num_programs(0) - 1)
        def _():
            l_ref[...] = jnp.full(l_ref.shape, 0.5 / d, F32) * jnp.sum(a_l[...])
            red_ref[...] = jnp.zeros_like(red_ref)
            red_ref[0:1, :] = jnp.sum(a_g[...], axis=0, keepdims=True)
            red_ref[1:2, :] = jnp.sum(a_b[...], axis=0, keepdims=True)
            red_ref[2:3, :] = jnp.sum(a_gate[...], axis=0, keepdims=True)

    l, dxr, dyy, red = _rows(
        body, s, ROW_TILE, [("blk", x), ("blk", y), ("all", gate), ("all", g), ("all", b), ("blk", target)],
        [("all", (SUBLANES, LANES), F32), ("blk", (s, d), F32), ("blk", (s, d), MXU_DTYPE), ("all", (SUBLANES, d), F32)], name,
        scratch=[pltpu.VMEM((SUBLANES, d), F32)] * 4)
    return l[0, 0], dxr, dyy, red


def _mod_ln_bwd(dxr, dhs, x, scale, x_in, y, gate, g, name, after=None):
    s, d = x.shape
    n_dh = len(dhs)
    n_dep = 0 if after is None else 1

    def body(dxr_ref, *rest):
        dh_refs = rest[:n_dh]
        x_ref, sc_ref, xin_ref, y_ref, gate_ref, g_ref = rest[n_dh:n_dh + 6]
        dxr_out, dyy_ref, red_mod, red_ln, a_sh, a_sc, a_g, a_b, a_gate = rest[n_dh + 6 + n_dep:]
        i = pl.program_id(0)

        @pl.when(i == 0)
        def _():
            for a in (a_sh, a_sc, a_g, a_b, a_gate):
                a[...] = jnp.zeros_like(a)

        dh = dh_refs[0][...]
        for r in dh_refs[1:]:
            dh = dh + r[...]
        xv = x_ref[...]
        dxo_v = dxr_ref[...] + dh * (1.0 + sc_ref[...])
        a_sh[...] += _fold8(dh)
        a_sc[...] += _fold8(dh * xv)
        yv = y_ref[...]
        z = ALPHA * xin_ref[...] + gate_ref[...] * yv
        xhat, rstd = _ln_stats(z)
        dxh = dxo_v * g_ref[...]
        dz = rstd * (dxh - jnp.mean(dxh, axis=-1, keepdims=True) - xhat * jnp.mean(dxh * xhat, axis=-1, keepdims=True))
        dxr_out[...] = ALPHA * dz
        dyy_ref[...] = (gate_ref[...] * dz).astype(dyy_ref.dtype)
        a_g[...] += _fold8(dxo_v * xhat)
        a_b[...] += _fold8(dxo_v)
        a_gate[...] += _fold8(dz * yv)

        @pl.when(i == pl.num_programs(0) - 1)
        def _():
            red_mod[...] = jnp.zeros_like(red_mod)
            red_mod[0:1, :] = jnp.sum(a_sh[...], axis=0, keepdims=True)
            red_mod[1:2, :] = jnp.sum(a_sc[...], axis=0, keepdims=True)
            red_ln[...] = jnp.zeros_like(red_ln)
            red_ln[0:1, :] = jnp.sum(a_g[...], axis=0, keepdims=True)
            red_ln[1:2, :] = jnp.sum(a_b[...], axis=0, keepdims=True)
            red_ln[2:3, :] = jnp.sum(a_gate[...], axis=0, keepdims=True)

    ins = ([("blk", dxr)] + [("blk", h) for h in dhs]
           + [("blk", x), ("all", scale), ("blk", x_in), ("blk", y), ("all", gate), ("all", g)] + [("dep", after)] * n_dep)
    return _rows(body, s, ROW_TILE, ins,
                 [("blk", (s, d), F32), ("blk", (s, d), MXU_DTYPE), ("all", (SUBLANES, d), F32), ("all", (SUBLANES, d), F32)], name,
                 scratch=[pltpu.VMEM((SUBLANES, d), F32)] * 5)


def _left_half(shape):
    return lax.broadcasted_iota(jnp.int32, shape, 1) < (LANES // 2)


def _spatial_z(vn, wc_ref, bias_ref, j):
    vb = vn[:, j * LANES:(j + 1) * LANES]
    z0 = _dot_nn(wc_ref[2 * j], vb)
    z1 = _dot_nn(wc_ref[2 * j + 1], vb)
    return jnp.where(_left_half(z0.shape), z0, z1) + bias_ref[:, j * LANES:(j + 1) * LANES]


def _spatial_fwd(uvpre, vn_g, vn_b, wc, bias_full, name):
    s, d2 = uvpre.shape
    d = d2 // 2

    def body(uv_ref, g_ref, b_ref, wc_ref, bias_ref, out_ref):
        u = _gelu(uv_ref[:, :d])
        v = _gelu(uv_ref[:, d:])
        vh, _ = _ln_stats(v)
        vn = vh * g_ref[...] + b_ref[...]
        for j in range(d // LANES):
            z = _spatial_z(vn, wc_ref, bias_ref, j)
            out_ref[:, j * LANES:(j + 1) * LANES] = (u[:, j * LANES:(j + 1) * LANES] * z).astype(out_ref.dtype)

    return _rows(body, s, CHUNK, [("blk", uvpre), ("all", vn_g), ("all", vn_b), ("all", wc), ("all", bias_full)],
                 [("blk", (s, d), MXU_DTYPE)], name)[0]


def _spatial_bwd(uvpre, dgated, vn_g, vn_b, wc, wct, bias_full, name):
    s, d2 = uvpre.shape
    d = d2 // 2

    def body(uv_ref, dg_ref, g_ref, b_ref, wc_ref, wct_ref, bias_ref,
             duv_ref, dws_ref, dbias_ref, dbin_ref, dvg_ref, dvb_ref, dvn_buf, a_bin, a_vg, a_vb):
        i = pl.program_id(0)

        @pl.when(i == 0)
        def _():
            dws_ref[...] = jnp.zeros_like(dws_ref)
            dbias_ref[...] = jnp.zeros_like(dbias_ref)
            a_bin[...] = jnp.zeros_like(a_bin)
            a_vg[...] = jnp.zeros_like(a_vg)
            a_vb[...] = jnp.zeros_like(a_vb)

        up = uv_ref[:, :d]
        vp = uv_ref[:, d:]
        u = _gelu(up)
        v = _gelu(vp)
        vh, rstd = _ln_stats(v)
        vn = vh * g_ref[...] + b_ref[...]
        dg = dg_ref[...]
        dzz = dg * u
        dbias_ref[...] += dzz
        for j in range(d // LANES):
            cols = slice(j * LANES, (j + 1) * LANES)
            z = _spatial_z(vn, wc_ref, bias_ref, j)
            dup = dg[:, cols] * z * _gelu_grad(up[:, cols])
            duv_ref[:, cols] = dup.astype(duv_ref.dtype)
            a_bin[:, cols] += _fold8(dup)
            dzb = dzz[:, cols]
            left = _left_half(dzb.shape)
            dvn_buf[:, cols] = jnp.where(left, _dot_nn(wct_ref[2 * j], dzb), _dot_nn(wct_ref[2 * j + 1], dzb))
            vb = vn[:, cols]
            dws_ref[2 * j] += _dot_nt(jnp.where(left, dzb, 0.0), vb)
            dws_ref[2 * j + 1] += _dot_nt(jnp.where(left, 0.0, dzb), vb)
        dvn = dvn_buf[...]
        a_vg[...] += _fold8(dvn * vh)
        a_vb[...] += _fold8(dvn)
        dvh = dvn * g_ref[...]
        dv = rstd * (dvh - jnp.mean(dvh, axis=-1, keepdims=True) - vh * jnp.mean(dvh * vh, axis=-1, keepdims=True))
        dvp = dv * _gelu_grad(vp)
        duv_ref[:, d:] = dvp.astype(duv_ref.dtype)
        a_bin[:, d:] += _fold8(dvp)

        @pl.when(i == pl.num_programs(0) - 1)
        def _():
            dbin_ref[...] = jnp.sum(a_bin[...], axis=0, keepdims=True)
            dvg_ref[...] = jnp.sum(a_vg[...], axis=0, keepdims=True)
            dvb_ref[...] = jnp.sum(a_vb[...], axis=0, keepdims=True)

    return _rows(body, s, CHUNK,
                 [("blk", uvpre), ("blk", dgated), ("all", vn_g), ("all", vn_b), ("all", wc), ("all", wct), ("all", bias_full)],
                 [("blk", (s, d2), MXU_DTYPE), ("all", (A_GROUPS, CHUNK, CHUNK), F32), ("all", (CHUNK, d), F32),
                  ("all", (1, d2), F32), ("all", (1, d), F32), ("all", (1, d), F32)], name,
                 scratch=[pltpu.VMEM((CHUNK, d), F32), pltpu.VMEM((SUBLANES, d2), F32),
                          pltpu.VMEM((SUBLANES, d), F32), pltpu.VMEM((SUBLANES, d), F32)])


def _head_mask(v, h):
    lane = lax.broadcasted_iota(jnp.int32, v.shape, 1)
    return jnp.where((lane >= h * HEAD_DIM) & (lane < (h + 1) * HEAD_DIM), v, jnp.zeros_like(v))


def _att_bias(slopes, dil):
    qi = lax.broadcasted_iota(jnp.int32, (SPAN, SPAN), 0)
    ki = lax.broadcasted_iota(jnp.int32, (SPAN, SPAN), 1)
    sl = slopes[:, None, None]
    cur = jnp.where(ki <= qi, -sl * (float(dil) * (qi - ki).astype(F32)), NEG)
    prev = jnp.where(ki >= qi, -sl * (float(dil) * (SPAN + qi - ki).astype(F32)), NEG)
    absent = jnp.full_like(prev, NEG)
    pairs = slopes.shape[0] // 2

    def fwd(pv):
        return jnp.concatenate([cur, pv], axis=2).reshape(pairs, 2 * SPAN, 2 * SPAN)

    def bwd(pv):
        return jnp.concatenate([cur.reshape(pairs, 2 * SPAN, SPAN), pv.reshape(pairs, 2 * SPAN, SPAN)], axis=1)

    return jnp.stack([fwd(absent), fwd(prev)]), jnp.stack([bwd(absent), bwd(prev)])


def _att_specs(s, d, dil, kinds):
    nb = s // (dil * SPAN)

    def rowblk(which, b):
        if which == "prev":
            return jnp.where(b % nb == 0, b, b - 1)
        if which == "next":
            return jnp.where(b % nb == nb - 1, b, b + 1)
        return b

    return [pl.BlockSpec((SPAN, d), functools.partial(lambda b, o, w: (rowblk(w, b), o), o=part, w=which))
            for part, which in kinds]


def _head_col(v, head):
    return v[:, head:head + 1]


def _expand_heads(w, j):
    shape = (w.shape[0], LANES)
    return jnp.where(_left_half(shape), jnp.broadcast_to(_head_col(w, 2 * j), shape), jnp.broadcast_to(_head_col(w, 2 * j + 1), shape))


def _attn_fwd(qkv, slopes, dil, name):
    s, d3 = qkv.shape
    d = d3 // 3
    nb = s // (dil * SPAN)
    table, _ = _att_bias(slopes, dil)

    def body(q_ref, kc_ref, kp_ref, vc_ref, vp_ref, tb_ref, o_ref, l_ref):
        left = _left_half((SPAN, LANES))
        lane = lax.broadcasted_iota(jnp.int32, (SPAN, LANES), 1)
        lses = jnp.zeros((SPAN, LANES), F32)
        for hp in range(d // LANES):
            cols = slice(hp * LANES, (hp + 1) * LANES)
            q = q_ref[:, cols]
            q2 = jnp.concatenate([_head_mask(q, 0), _head_mask(q, 1)], axis=0) * ATT_SCALE
            k2 = jnp.concatenate([kc_ref[:, cols], kp_ref[:, cols]], axis=0)
            v2 = jnp.concatenate([vc_ref[:, cols], vp_ref[:, cols]], axis=0)
            sc = _dot_nt(q2, k2) + tb_ref[hp]
            m = jnp.max(sc, axis=-1, keepdims=True)
            p = jnp.exp(sc - m)
            l = jnp.sum(p, axis=-1, keepdims=True)
            r = _dot_nn(p, v2) * (1.0 / l)
            lse = m + jnp.log(l)
            o_ref[:, cols] = jnp.where(left, r[:SPAN], r[SPAN:])
            lses = jnp.where(lane == 2 * hp, lse[:SPAN], jnp.where(lane == 2 * hp + 1, lse[SPAN:], lses))
        l_ref[...] = lses

    specs = _att_specs(s, d, dil, [(0, "cur"), (1, "cur"), (1, "prev"), (2, "cur"), (2, "prev")])
    tbl = pl.BlockSpec((None,) + table.shape[1:], lambda b: (jnp.where(b % nb == 0, 0, 1), 0, 0, 0))
    out_spec = pl.BlockSpec((SPAN, d), lambda b: (b, 0))
    return pl.pallas_call(
        body,
        grid=(s // SPAN,),
        in_specs=specs + [tbl],
        out_specs=[out_spec, pl.BlockSpec((SPAN, LANES), lambda b: (b, 0))],
        out_shape=[jax.ShapeDtypeStruct((s, d), F32), jax.ShapeDtypeStruct((s, LANES), F32)],
        name=name,
        compiler_params=_cparams(("parallel",)),
    )(qkv, qkv, qkv, qkv, qkv, table)


def _attn_bwd(qkv, do, lse, dd, slopes, dil, name):
    s, d3 = qkv.shape
    d = d3 // 3
    nb = s // (dil * SPAN)
    _, table = _att_bias(slopes, dil)

    def heads_stacked(cur, nxt):
        return jnp.concatenate([_head_mask(cur, 0), _head_mask(cur, 1), _head_mask(nxt, 0), _head_mask(nxt, 1)], axis=0)

    def cols_stacked(cur, nxt, hp):
        return jnp.concatenate([jnp.broadcast_to(_head_col(a, 2 * hp + h), (SPAN, LANES)) for a in (cur, nxt) for h in range(2)], axis=0)

    def body(k_ref, v_ref, qc_ref, qn_ref, doc_ref, don_ref, lc_ref, ln_ref, ddc_ref, ddn_ref, tb_ref, out_ref, carry):
        b = pl.program_id(0)

        @pl.when(b == 0)
        def _():
            carry[...] = jnp.zeros_like(carry)

        left = _left_half((SPAN, LANES))
        lse_c, lse_n, dd_c, dd_n = lc_ref[...], ln_ref[...], ddc_ref[...], ddn_ref[...]
        for hp in range(d // LANES):
            cols = slice(hp * LANES, (hp + 1) * LANES)
            k, v = k_ref[:, cols], v_ref[:, cols]
            q4 = heads_stacked(qc_ref[:, cols], qn_ref[:, cols])
            do4 = heads_stacked(doc_ref[:, cols], don_ref[:, cols])
            sc = _dot_nt(q4 * ATT_SCALE, k) + tb_ref[hp]
            p = jnp.exp(sc - cols_stacked(lse_c, lse_n, hp))
            ds = p * (_dot_nt(do4, v) - cols_stacked(dd_c, dd_n, hp))
            dq4 = _dot_nn(ds, k)
            dq_cur = jnp.where(left, dq4[:SPAN], dq4[SPAN:2 * SPAN]) + carry[:, cols]
            carry[:, cols] = jnp.where(left, dq4[2 * SPAN:3 * SPAN], dq4[3 * SPAN:])
            out_ref[:, cols] = (dq_cur * ATT_SCALE).astype(out_ref.dtype)
            out_ref[:, d + hp * LANES:d + (hp + 1) * LANES] = (_dot_tn(ds, q4) * ATT_SCALE).astype(out_ref.dtype)
            out_ref[:, 2 * d + hp * LANES:2 * d + (hp + 1) * LANES] = _dot_tn(p, do4).astype(out_ref.dtype)

    qkv_specs = _att_specs(s, d, dil, [(1, "cur"), (2, "cur"), (0, "cur"), (0, "next")])
    pair = _att_specs(s, d, dil, [(0, "cur"), (0, "next")])
    heads = _att_specs(s, LANES, dil, [(0, "cur"), (0, "next")])
    tbl = pl.BlockSpec((None,) + table.shape[1:], lambda b: (jnp.where(b % nb == nb - 1, 0, 1), 0, 0, 0))
    return pl.pallas_call(
        body,
        grid=(s // SPAN,),
        in_specs=qkv_specs + pair + heads + heads + [tbl],
        out_specs=pl.BlockSpec((SPAN, d3), lambda b: (b, 0)),
        out_shape=jax.ShapeDtypeStruct((s, d3), MXU_DTYPE),
        scratch_shapes=[pltpu.VMEM((SPAN, d), F32)],
        name=name,
        compiler_params=_cparams(("arbitrary",)),
    )(qkv, qkv, qkv, qkv, do, do, lse, lse, dd, dd, table)


def _mix_weights(l_refs):
    ls = [r[...] for r in l_refs]
    m = functools.reduce(jnp.maximum, ls)
    es = [jnp.exp(l - m) for l in ls]
    tot = functools.reduce(lambda a, c: a + c, es)
    return [e / tot for e in es]


def _combine_fwd(os_, ls_, name):
    s, d = os_[0].shape
    n = len(os_)

    def body(*refs):
        o_refs, l_refs, out_ref = refs[:n], refs[n:2 * n], refs[2 * n]
        ws = _mix_weights(l_refs)
        for j in range(d // LANES):
            cols = slice(j * LANES, (j + 1) * LANES)
            acc = _expand_heads(ws[0], j) * o_refs[0][:, cols]
            for w, o in zip(ws[1:], o_refs[1:]):
                acc = acc + _expand_heads(w, j) * o[:, cols]
            out_ref[:, cols] = acc

    return _rows(body, s, ROW_TILE, [("blk", a) for a in os_ + ls_], [("blk", (s, d), F32)], name)[0]


def _combine_bwd(do, o, ls_, name):
    s, d = o.shape
    n = len(ls_)
    sel = (lax.broadcasted_iota(jnp.int32, (d, LANES), 0) // HEAD_DIM == lax.broadcasted_iota(jnp.int32, (d, LANES), 1)).astype(F32)

    def body(do_ref, o_ref, *rest):
        l_refs, sel_ref, outs = rest[:n], rest[n], rest[n + 1:]
        ws = _mix_weights(l_refs)
        dov = do_ref[...]
        r = jnp.dot(dov * o_ref[...], sel_ref[...], precision=lax.Precision.HIGHEST, preferred_element_type=F32)
        for g in range(n):
            outs[2 * g + 1][...] = ws[g] * r
            for j in range(d // LANES):
                cols = slice(j * LANES, (j + 1) * LANES)
                outs[2 * g][:, cols] = (_expand_heads(ws[g], j) * dov[:, cols]).astype(outs[2 * g].dtype)

    outs = []
    for _ in range(n):
        outs += [("blk", (s, d), MXU_DTYPE), ("blk", (s, LANES), F32)]
    res = _rows(body, s, ROW_TILE, [("blk", do), ("blk", o)] + [("blk", l) for l in ls_] + [("all", sel)], outs, name)
    return [(res[2 * g], res[2 * g + 1]) for g in range(n)]


def _ada_fwd(c_all, w, b, name):
    nsub, d, cs = w.shape

    def body(c_ref, w_ref, b_ref, o_ref):
        cv = c_ref[...]
        sc = cv * (1.0 / (1.0 + jnp.exp(-cv)))
        o_ref[...] = _dot_nn(sc, w_ref[...]) + b_ref[...]

    return pl.pallas_call(
        body,
        grid=(nsub,),
        in_specs=[pl.BlockSpec(c_all.shape, lambda i: (0, 0)), pl.BlockSpec((None, d, cs), lambda i: (i, 0, 0)),
                  pl.BlockSpec((None, 1, cs), lambda i: (i, 0, 0))],
        out_specs=pl.BlockSpec((None, N_DEV, cs), lambda i: (i, 0, 0)),
        out_shape=jax.ShapeDtypeStruct((nsub, N_DEV, cs), F32),
        name=name,
        compiler_params=_cparams(("parallel",)),
    )(c_all, w, b)


def _ada_bwd(c_all_t, dm, name):
    d, nb = c_all_t.shape
    nsub, _, cs = dm.shape

    def body(c_ref, dm_ref, o_ref):
        cv = c_ref[...]
        sc = cv * (1.0 / (1.0 + jnp.exp(-cv)))
        acc = sc[:, 0:1] * dm_ref[0:1, :]
        for bi in range(1, nb):
            acc = acc + sc[:, bi:bi + 1] * dm_ref[bi:bi + 1, :]
        o_ref[...] = acc

    return pl.pallas_call(
        body,
        grid=(nsub,),
        in_specs=[pl.BlockSpec(c_all_t.shape, lambda i: (0, 0)), pl.BlockSpec((None, nb, cs), lambda i: (i, 0, 0))],
        out_specs=pl.BlockSpec((None, d, cs), lambda i: (i, 0, 0)),
        out_shape=jax.ShapeDtypeStruct((nsub, d, cs), F32),
        name=name,
        compiler_params=_cparams(("parallel",)),
    )(c_all_t, dm)


def _row_tile(r, row_elems):
    t = 2 * SUBLANES
    if r % t:
        return r
    while t * 2 * row_elems <= 256 * 1024 and r % (t * 2) == 0:
        t *= 2
    return t


def _adamw(w, g, m, v, name):
    shape = w.shape
    c = shape[-1]
    r = w.size // c
    tr = _row_tile(r, c)
    w2, g2, m2, v2 = [a.reshape(r, c) for a in (w, g, m, v)]
    bc1 = 1.0 - ADAM_B1 ** ADAM_STEP
    bc2 = 1.0 - ADAM_B2 ** ADAM_STEP

    def body(w_ref, g_ref, m_ref, v_ref, d_ref, nm_ref, nv_ref):
        gv = g_ref[...]
        nm = ADAM_B1 * m_ref[...] + (1.0 - ADAM_B1) * gv
        nv = ADAM_B2 * v_ref[...] + (1.0 - ADAM_B2) * (gv * gv)
        d_ref[...] = -ADAM_LR * ((nm / bc1) / (jnp.sqrt(nv / bc2) + ADAM_EPS) + ADAM_WD * w_ref[...])
        nm_ref[...] = nm
        nv_ref[...] = nv

    res = _rows(body, r, tr, [("blk", a) for a in (w2, g2, m2, v2)], [("blk", (r, c), F32)] * 3, name)
    return [a.reshape(shape) for a in res]


def _sum_slots(buf, name):
    n, r, c = buf.shape
    tr = _row_tile(r, n * c)

    def body(b_ref, o_ref):
        acc = b_ref[0].astype(F32)
        for k in range(1, n):
            acc = acc + b_ref[k].astype(F32)
        o_ref[...] = acc

    return pl.pallas_call(
        body,
        grid=(r // tr,),
        in_specs=[pl.BlockSpec((n, tr, c), lambda i: (0, i, 0))],
        out_specs=pl.BlockSpec((tr, c), lambda i: (i, 0)),
        out_shape=jax.ShapeDtypeStruct((r, c), F32),
        name=name,
        compiler_params=_cparams(("parallel",)),
    )(buf)


def _me():
    return lax.axis_index("x"), lax.axis_index("y"), lax.axis_index("c")


def _all_gather_small(blk, name, after=()):
    m_per, n = blk.shape

    def body(x_ref, *rest):
        out_ref, send_sems, recv_sems, local_sem = rest[len(after):]
        x, y, c = _me()
        me, sibling = (x, y, c), (x, y, 1 - c)
        chips = [(1 - x, y), (x, 1 - y), (1 - x, 1 - y)]

        def rows(px, py, pc):
            return out_ref.at[pl.ds((4 * px + 2 * py + pc) * m_per, m_per), :]

        def copy(k, block, to, src=None):
            return pltpu.make_async_remote_copy(
                src_ref=rows(*block) if src is None else src, dst_ref=rows(*block),
                send_sem=send_sems.at[k], recv_sem=recv_sems.at[k], device_id=to, device_id_type=MESH)

        mine = pltpu.make_async_copy(x_ref, rows(*me), local_sem)
        mine.start()
        first = [copy(0, me, sibling, src=x_ref)]
        first += [copy(1 + j, me, (*chip, c), src=x_ref) for j, chip in enumerate(chips)]
        for cp in first:
            cp.start()
        passed = [copy(4 + j, (*chip, c), sibling) for j, chip in enumerate(chips)]
        for j, chip in enumerate(chips):
            copy(1 + j, (*chip, c), me).wait_recv()
            passed[j].start()
        copy(0, sibling, me).wait_recv()
        for j, chip in enumerate(chips):
            copy(4 + j, (*chip, 1 - c), me).wait_recv()
        for cp in first + passed:
            cp.wait_send()
        mine.wait()

    return pl.pallas_call(
        body,
        out_shape=jax.ShapeDtypeStruct((N_DEV * m_per, n), blk.dtype),
        in_specs=[pl.BlockSpec(memory_space=pltpu.VMEM)] + [pl.BlockSpec(memory_space=pl.ANY)] * len(after),
        out_specs=pl.BlockSpec(memory_space=pltpu.VMEM),
        scratch_shapes=[pltpu.SemaphoreType.DMA((7,)), pltpu.SemaphoreType.DMA((7,)), pltpu.SemaphoreType.DMA],
        name=name,
        compiler_params=pltpu.CompilerParams(vmem_limit_bytes=VMEM_LIMIT),
    )(blk, *after)


_HBM = pl.BlockSpec(memory_space=pltpu.HBM)
_SEM = pl.BlockSpec(memory_space=pltpu.SEMAPHORE)
_EFFECT = pltpu.SideEffectType.DATAFLOW_SIDE_EFFECTING


def _other_chips(x, y):
    return [(1 - x, y), (x, 1 - y), (1 - x, 1 - y)]


def _gather_copy(w, j, src_ref, land_ref, send_sems, recv_sems, halved=False):
    x, y, c = _me()
    if halved:
        half = src_ref.shape[0] // 2
        src_ref = src_ref.at[pl.ds(c * half, half), :]
    return pltpu.make_async_remote_copy(
        src_ref=src_ref, dst_ref=land_ref.at[2 * x + y], send_sem=send_sems.at[3 * w + j], recv_sem=recv_sems.at[3 * w + j],
        device_id=(*_other_chips(x, y)[j], c), device_id_type=MESH)


def _gather_start(shards, halved, after, name):
    n = len(shards)
    lands = [lax.empty((N_CHIPS, s.shape[0] // 2 if w in halved else s.shape[0], s.shape[1]), s.dtype) for w, s in enumerate(shards)]

    def body(*refs):
        in_refs, land_refs = refs[:n], refs[n:2 * n]
        send_sems, recv_sems = refs[2 * n + 1], refs[2 * n + 2]
        token = refs[-1]
        for w in range(n):
            for j in range(3):
                _gather_copy(w, j, in_refs[w], land_refs[w], send_sems, recv_sems, w in halved).start()
        token[...] = jnp.zeros_like(token)

    res = pl.pallas_call(
        body,
        out_shape=(pltpu.SemaphoreType.DMA((3 * n,)), pltpu.SemaphoreType.DMA((3 * n,)),
                   *[pltpu.HBM(s.shape, s.dtype) for s in shards], *[pltpu.HBM(l.shape, l.dtype) for l in lands],
                   jax.ShapeDtypeStruct((SUBLANES, LANES), F32)),
        in_specs=[_HBM] * (2 * n) + [pl.BlockSpec(memory_space=pl.ANY)],
        out_specs=(_SEM, _SEM, *[_HBM] * (2 * n), pl.BlockSpec(memory_space=pltpu.VMEM)),
        input_output_aliases={i: 2 + i for i in range(2 * n)},
        name=name,
        compiler_params=pltpu.CompilerParams(has_side_effects=_EFFECT),
    )(*[pltpu.with_memory_space_constraint(a, pltpu.HBM) for a in list(shards) + lands], after)
    return res[0], res[1], res[2:2 + n], res[2 + n:2 + 2 * n], res[-1]


def _gather_wait(w, shard, land, send_sems, recv_sems, after, name, halved=False):
    def body(s_ref, land_ref, send_sems, recv_sems, after_ref, s_out, land_out, stage):
        x, y, _ = _me()
        if not halved:
            pltpu.sync_copy(s_ref, stage)
            pltpu.sync_copy(stage, land_out.at[2 * x + y])
        for j in range(3):
            cp = _gather_copy(w, j, s_ref, land_ref, send_sems, recv_sems, halved)
            cp.wait_send()
            cp.wait_recv()

    return pl.pallas_call(
        body,
        out_shape=(pltpu.HBM(shard.shape, shard.dtype), pltpu.HBM(land.shape, land.dtype)),
        in_specs=(_HBM, _HBM, _SEM, _SEM, pl.BlockSpec(memory_space=pl.ANY)),
        out_specs=(_HBM, _HBM),
        input_output_aliases={0: 0, 1: 1},
        scratch_shapes=[pltpu.VMEM((SUBLANES, LANES) if halved else shard.shape, shard.dtype)],
        name=name,
        compiler_params=pltpu.CompilerParams(has_side_effects=_EFFECT, vmem_limit_bytes=VMEM_LIMIT),
    )(shard, land, send_sems, recv_sems, after)


def _assemble_halves(shard, land, name):
    half = land.shape[1]

    def body(s_ref, land_ref, out_ref, send_sems, recv_sems, local_sems):
        x, y, c = _me()
        own = pltpu.make_async_copy(s_ref, out_ref.at[2 * x + y], local_sems.at[3])
        own.start()
        cps = []
        for j, (ox, oy) in enumerate(_other_chips(x, y)):
            qj = 2 * ox + oy
            mine = out_ref.at[qj, pl.ds(c * half, half), :]
            lc = pltpu.make_async_copy(land_ref.at[qj], mine, local_sems.at[j])
            lc.start()
            rc = pltpu.make_async_remote_copy(
                src_ref=land_ref.at[qj], dst_ref=mine, send_sem=send_sems.at[j], recv_sem=recv_sems.at[j],
                device_id=(x, y, 1 - c), device_id_type=MESH)
            rc.start()
            cps.append((lc, rc))
        for lc, rc in cps:
            rc.wait_recv()
        for lc, rc in cps:
            rc.wait_send()
            lc.wait()
        own.wait()

    vmem = pl.BlockSpec(memory_space=pltpu.VMEM)
    return pl.pallas_call(
        body,
        out_shape=jax.ShapeDtypeStruct((N_CHIPS,) + shard.shape, shard.dtype),
        in_specs=[vmem, vmem],
        out_specs=vmem,
        scratch_shapes=[pltpu.SemaphoreType.DMA((3,)), pltpu.SemaphoreType.DMA((3,)), pltpu.SemaphoreType.DMA((4,))],
        name=name,
        compiler_params=pltpu.CompilerParams(vmem_limit_bytes=VMEM_LIMIT),
    )(shard, land)


def _piece_shape(shape, kind):
    k, nn = shape
    return (k // 2, nn // N_CHIPS) if kind == "col" else (k // N_CHIPS // 2, nn)


def _piece_of(g_ref, kind, tq, tc):
    pr, pc = _piece_shape(g_ref.shape, kind)
    if kind == "col":
        return g_ref.at[pl.ds(tc * pr, pr), pl.ds(tq * pc, pc)]
    return g_ref.at[pl.ds((2 * tq + tc) * pr, pr), :]


def _scatter_copy(w, r, kind, g_ref, land_ref, send_sems, recv_sems):
    x, y, c = _me()
    tx, ty, tc = (x + ((r >> 2) & 1)) % 2, (y + ((r >> 1) & 1)) % 2, (c + (r & 1)) % 2
    return pltpu.make_async_remote_copy(
        src_ref=_piece_of(g_ref, kind, 2 * tx + ty, tc), dst_ref=land_ref.at[4 * x + 2 * y + c],
        send_sem=send_sems.at[N_DEV * w + r], recv_sem=recv_sems.at[N_DEV * w + r], device_id=(tx, ty, tc), device_id_type=MESH)


def _scatter_start(gs, kinds, name):
    n = len(gs)
    pieces = [_piece_shape(g.shape, kind) for g, kind in zip(gs, kinds)]
    lands = [lax.empty((N_DEV,) + p, g.dtype) for p, g in zip(pieces, gs)]

    def body(*refs):
        g_refs, land_refs, send_sems, recv_sems = refs[:n], refs[n:2 * n], refs[2 * n], refs[2 * n + 1]
        land_outs, stages = refs[3 * n + 2:4 * n + 2], refs[4 * n + 2:]
        x, y, c = _me()
        for w in range(n):
            for r in range(1, N_DEV):
                _scatter_copy(w, r, kinds[w], g_refs[w], land_refs[w], send_sems, recv_sems).start()
        for w in range(n):
            pltpu.sync_copy(_piece_of(g_refs[w], kinds[w], 2 * x + y, c), stages[w])
            pltpu.sync_copy(stages[w], land_outs[w].at[4 * x + 2 * y + c])

    arrays = list(gs) + lands
    res = pl.pallas_call(
        body,
        out_shape=(pltpu.SemaphoreType.DMA((N_DEV * n,)), pltpu.SemaphoreType.DMA((N_DEV * n,)),
                   *[pltpu.HBM(a.shape, a.dtype) for a in arrays]),
        in_specs=[_HBM] * (2 * n),
        out_specs=(_SEM, _SEM, *[_HBM] * (2 * n)),
        input_output_aliases={i: 2 + i for i in range(2 * n)},
        scratch_shapes=[pltpu.VMEM(p, g.dtype) for p, g in zip(pieces, gs)],
        name=name,
        compiler_params=pltpu.CompilerParams(has_side_effects=_EFFECT, vmem_limit_bytes=VMEM_LIMIT),
    )(*[pltpu.with_memory_space_constraint(a, pltpu.HBM) for a in arrays])
    return res[0], res[1], res[2:2 + n], res[2 + n:]


def _scatter_wait(send_sems, recv_sems, gs, lands, kinds, after, name):
    n = len(gs)

    def body(*refs):
        g_refs, land_refs, send_sems, recv_sems = refs[:n], refs[n:2 * n], refs[2 * n], refs[2 * n + 1]
        for w in range(n):
            for r in range(1, N_DEV):
                cp = _scatter_copy(w, r, kinds[w], g_refs[w], land_refs[w], send_sems, recv_sems)
                cp.wait_send()
                cp.wait_recv()

    arrays = list(gs) + list(lands)
    return pl.pallas_call(
        body,
        out_shape=tuple(pltpu.HBM(a.shape, a.dtype) for a in arrays),
        in_specs=(*[_HBM] * (2 * n), _SEM, _SEM, pl.BlockSpec(memory_space=pl.ANY)),
        out_specs=tuple([_HBM] * (2 * n)),
        input_output_aliases={i: i for i in range(2 * n)},
        name=name,
        compiler_params=pltpu.CompilerParams(has_side_effects=_EFFECT),
    )(*arrays, send_sems, recv_sems, after)[n:]


def _swap_halves(halves, name):
    n = len(halves)

    def body(*refs):
        in_refs, out_refs = refs[:n], refs[n:2 * n]
        send_sems, recv_sems, local_sems = refs[2 * n:]
        x, y, c = _me()
        cps = []
        for w in range(n):
            lc = pltpu.make_async_copy(in_refs[w], out_refs[w].at[c], local_sems.at[w])
            lc.start()
            rc = pltpu.make_async_remote_copy(
                src_ref=in_refs[w], dst_ref=out_refs[w].at[c], send_sem=send_sems.at[w], recv_sem=recv_sems.at[w],
                device_id=(x, y, 1 - c), device_id_type=MESH)
            rc.start()
            cps.append((lc, rc))
        for lc, rc in cps:
            rc.wait_recv()
        for lc, rc in cps:
            rc.wait_send()
            lc.wait()

    vmem = pl.BlockSpec(memory_space=pltpu.VMEM)
    return pl.pallas_call(
        body,
        out_shape=[jax.ShapeDtypeStruct((2,) + h.shape, h.dtype) for h in halves],
        in_specs=[vmem] * n,
        out_specs=[vmem] * n,
        scratch_shapes=[pltpu.SemaphoreType.DMA((n,)), pltpu.SemaphoreType.DMA((n,)), pltpu.SemaphoreType.DMA((n,))],
        name=name,
        compiler_params=pltpu.CompilerParams(vmem_limit_bytes=VMEM_LIMIT),
    )(*halves)


def _to_streams(a, dil):
    if dil == 1:
        return a
    s, c = a.shape
    return a.reshape(s // dil, dil, c).transpose(1, 0, 2).reshape(s, c)


def _from_streams(a, dil):
    if dil == 1:
        return a
    s, c = a.shape
    return a.reshape(dil, s // dil, c).transpose(1, 0, 2).reshape(s, c)


def _mm_tiles(s):
    return min(s, 1024)


def _local_step(x0, target, mvec, ln_g, ln_b, small, fetch, emit, start):
    s, d = x0.shape
    tm = _mm_tiles(s)
    row = lambda v: v.reshape(1, -1)
    shift = [row(mvec[i, :d]) for i in range(4)]
    scale = [row(mvec[i, d:2 * d]) for i in range(4)]
    gate = [row(1.0 + mvec[i, 2 * d:]) for i in range(4)]
    lg = [row(ln_g[i]) for i in range(4)]
    lb = [row(ln_b[i]) for i in range(4)]
    mm = functools.partial(_mm, tm=tm)
    mm_w = functools.partial(_mm, tm=1024, tk=min(s, 2048), mode="tn")

    xs, ys, big = [x0], [], {}
    h0 = _mod(x0, scale[0], shift[0], start, "mod0")
    big["a_w_in"] = fetch("a_w_in", h0)
    uvpre = mm(h0, big["a_w_in"], mode="nn", name="a_in", outs=[F32], tn=512, tk=1024,
               epi=lambda r, bias: [r + bias], extras=[("row", small["a_b_in"])])
    gated = _spatial_fwd(uvpre, small["a_vn_g"], small["a_vn_b"], small["wc"], small["bias_full"], "a_spatial")
    big["a_w_out"] = fetch("a_w_out", gated)
    ys.append(mm(gated, big["a_w_out"], mode="nn", name="a_out", outs=[F32], tn=1024, tk=1024))
    x1, h1 = _resid_ln(xs[0], ys[0], gate[0], lg[0], lb[0], (scale[1], shift[1]), "ln0")
    xs.append(x1)
    relu2 = lambda r: [jnp.square(jnp.maximum(r, 0.0))]
    big["up0"] = fetch("up0", h1)
    r0 = mm(h1, big["up0"], mode="nn", name="up0", outs=[MXU_DTYPE], tn=1024, tk=1024, epi=relu2)
    big["down0"] = fetch("down0", r0)
    ys.append(mm(r0, big["down0"], mode="nn", name="down0", outs=[F32], tn=1024, tk=2048))
    x2, h2 = _resid_ln(xs[1], ys[1], gate[1], lg[1], lb[1], (scale[2], shift[2]), "ln1")
    xs.append(x2)
    hg, qkvs, o_g, l_g, l_streams = [], [], [], [], []
    big["b_w_qkv"] = fetch("b_w_qkv", h2)
    for g, (_, dil) in enumerate(B_PATTERNS):
        hp = _to_streams(h2, dil)
        qkv = mm(hp, big["b_w_qkv"], mode="nn", name=f"qkv{g}", outs=[MXU_DTYPE], tn=768, tk=1024, b_col0=g * 3 * d, n_out=3 * d)
        og, lgv = _attn_fwd(qkv, small["slopes"], dil, f"attn_fwd{g}")
        hg.append(hp)
        qkvs.append(qkv)
        o_g.append(_from_streams(og, dil))
        l_g.append(_from_streams(lgv, dil))
        l_streams.append(lgv)
    o_mix = _combine_fwd(o_g, l_g, "combine")
    big["b_w_out"] = fetch("b_w_out", o_mix)
    ys.append(mm(o_mix, big["b_w_out"], mode="nn", name="b_out", outs=[F32], tn=1024, tk=1024))
    x3, h3 = _resid_ln(xs[2], ys[2], gate[2], lg[2], lb[2], (scale[3], shift[3]), "ln2")
    xs.append(x3)
    big["up1"] = fetch("up1", h3)
    r1 = mm(h3, big["up1"], mode="nn", name="up1", outs=[MXU_DTYPE], tn=1024, tk=1024, epi=relu2)
    big["down1"] = fetch("down1", r1)
    ys.append(mm(r1, big["down1"], mode="nn", name="down1", outs=[F32], tn=1024, tk=2048))

    gb, red_ln, red_mod = {}, [None] * 4, [None] * 4

    def mlp_bwd(i, h, r, dyy):
        gb[f"down{i}"] = mm_w(r, dyy, name=f"g_down{i}", outs=[MXU_DTYPE], tn=1024)
        da = mm(dyy, big[f"down{i}"], mode="nt", name=f"d_down{i}", outs=[MXU_DTYPE], tn=1024, tk=1024,
                after=emit(f"down{i}", gb[f"down{i}"]),
                epi=lambda acc, rv: [acc * (2.0 * jnp.sqrt(rv.astype(F32)))], extras=[("full", r)])
        gb[f"up{i}"] = mm_w(h, da, name=f"g_up{i}", outs=[MXU_DTYPE], tn=1024)
        return [mm(da, big[f"up{i}"], mode="nt", name=f"d_up{i}", outs=[F32], tn=1024, tk=1024, after=emit(f"up{i}", gb[f"up{i}"]))]

    def join(sub, dxr, dhs, after=None):
        res = _mod_ln_bwd(dxr, dhs, xs[sub], scale[sub], xs[sub - 1], ys[sub - 1], gate[sub - 1], lg[sub - 1],
                          f"mod_ln_bwd{sub}", after=after)
        red_mod[sub], red_ln[sub - 1] = res[2], res[3]
        return res[0], res[1]

    loss, dxr, dyy, red_ln[3] = _last_ln_loss_bwd(xs[3], ys[3], gate[3], lg[3], lb[3], target, "ln3_loss_bwd")
    dxr, dyy = join(3, dxr, mlp_bwd(1, h3, r1, dyy))
    gb["b_w_out"] = mm_w(o_mix, dyy, name="g_b_out", outs=[MXU_DTYPE], tn=1024, tk=1024)
    do = mm(dyy, big["b_w_out"], mode="nt", name="d_b_out", outs=[F32], tn=1024, tk=1024, after=emit("b_w_out", gb["b_w_out"]))
    parts = _combine_bwd(do, o_mix, l_g, "combine_bwd")
    dhs, gq = [], None
    for g, (_, dil) in enumerate(B_PATTERNS):
        do_g, dd_g = _to_streams(parts[g][0], dil), _to_streams(parts[g][1], dil)
        dqkv = _attn_bwd(qkvs[g], do_g, l_streams[g], dd_g, small["slopes"], dil, f"attn_bwd{g}")
        gq = mm_w(hg[g], dqkv, name=f"g_qkv{g}", outs=[MXU_DTYPE], tn=1024, out_col0=g * 3 * d, out_cols=len(B_PATTERNS) * 3 * d, into=gq)
        dh = mm(dqkv, big["b_w_qkv"], mode="nt", name=f"d_qkv{g}", outs=[F32], tn=1024, tk=768, b_col0=g * 3 * d)
        dhs.append(_from_streams(dh, dil))
    gb["b_w_qkv"] = gq
    dxr, dyy = join(2, dxr, dhs, after=emit("b_w_qkv", gb["b_w_qkv"]))
    dxr, dyy = join(1, dxr, mlp_bwd(0, h1, r0, dyy))
    gb["a_w_out"] = mm_w(gated, dyy, name="g_a_out", outs=[MXU_DTYPE], tn=1024)
    dgated = mm(dyy, big["a_w_out"], mode="nt", name="d_a_out", outs=[F32], tn=1024, tk=1024, after=emit("a_w_out", gb["a_w_out"]))
    duv, dws, dbias, dbin, dvg, dvb = _spatial_bwd(uvpre, dgated, small["a_vn_g"], small["a_vn_b"], small["wc"],
                                                   small["wct"], small["bias_full"], "a_spatial_bwd")
    gb["a_w_in"] = mm_w(h0, duv, name="g_a_in", outs=[MXU_DTYPE], tn=1024)
    dh = mm(duv, big["a_w_in"], mode="nt", name="d_a_in", outs=[F32], tn=1024, tk=512, after=emit("a_w_in", gb["a_w_in"]))
    dx, red_mod[0] = _mod_bwd(dxr, [dh], xs[0], scale[0], "mod_bwd0")
    dm = [jnp.concatenate([red_mod[i][0], red_mod[i][1], red_ln[i][2]]) for i in range(4)]
    dlg, dlb = [red_ln[i][0] for i in range(4)], [red_ln[i][1] for i in range(4)]

    tril = jnp.tril(jnp.ones((CHUNK, CHUNK), bool))
    gsmall = {
        "a_b_in": dbin.reshape(-1), "a_vn_g": dvg.reshape(-1), "a_vn_b": dvb.reshape(-1),
        "a_w_s": jnp.where(tril, dws, 0.0).reshape(-1),
        "a_b_s": dbias.reshape(CHUNK, A_GROUPS, d // A_GROUPS).sum(-1).T.reshape(-1),
    }
    return loss, dx, gb, jnp.stack(dm), jnp.stack(dlg), jnp.stack(dlb), gsmall


BIG = ("a_w_in", "a_w_out", "up0", "down0", "b_w_qkv", "b_w_out", "up1", "down1")
BIG_KIND = {"a_w_in": "col", "a_w_out": "row", "b_w_qkv": "col", "b_w_out": "row",
            "up0": "col", "up1": "col", "down0": "row", "down1": "row"}
HALVED = ("a_w_in", "down0", "b_w_qkv")
SCATTER_GROUPS = (("down1", "up1"), ("b_w_out", "b_w_qkv"), ("down0", "up0"), ("a_w_out", "a_w_in"))
SMALL = ("a_b_in", "a_vn_g", "a_vn_b", "a_b_s", "a_w_s")


def kernel(x, c, ada_w, ada_b, ln_g, ln_b, a_w_in, a_b_in, a_vn_g, a_vn_b, a_w_s, a_b_s, a_w_out, b_w_qkv, b_w_out, mlp_w_up, mlp_w_down, loss_target, m_ada_w, m_ada_b, m_ln_g, m_ln_b, m_a_w_in, m_a_b_in, m_a_vn_g, m_a_vn_b, m_a_w_s, m_a_b_s, m_a_w_out, m_b_w_qkv, m_b_w_out, m_mlp_w_up, m_mlp_w_down, v_ada_w, v_ada_b, v_ln_g, v_ln_b, v_a_w_in, v_a_b_in, v_a_vn_g, v_a_vn_b, v_a_w_s, v_a_b_s, v_a_w_out, v_b_w_qkv, v_b_w_out, v_mlp_w_up, v_mlp_w_down):
    s, d = x.shape[1], x.shape[2]
    xi, yi, ci = _me()
    q = 2 * xi + yi
    dev = 2 * q + ci
    nsub = 2 * DEPTH
    cs = ada_w.shape[-1]
    ls = ln_g.shape[-1]

    shards = {
        "a_w_in": a_w_in[0], "a_w_out": a_w_out[0], "b_w_qkv": b_w_qkv[0], "b_w_out": b_w_out[0],
        "up0": mlp_w_up[0], "up1": mlp_w_up[1], "down0": mlp_w_down[0], "down1": mlp_w_down[1],
    }
    cast = [shards[k].astype(MXU_DTYPE) for k in BIG]

    pack = jnp.concatenate([c.reshape(-1), ln_g.reshape(-1), ln_b.reshape(-1)]).reshape(-1, LANES)
    got = _all_gather_small(pack, "gather_small", after=cast).reshape(N_DEV, -1)
    c_all = got[:, :d]
    per_chip = got[0::2]
    ln_g_full = per_chip[:, d:d + nsub * ls].reshape(N_CHIPS, nsub, ls).transpose(1, 0, 2).reshape(nsub, d)
    ln_b_full = per_chip[:, d + nsub * ls:].reshape(N_CHIPS, nsub, ls).transpose(1, 0, 2).reshape(nsub, d)
    m_part = _ada_fwd(c_all, ada_w.reshape(nsub, d, cs), ada_b.reshape(nsub, 1, cs), "ada_fwd")
    m_all = _all_gather_small(m_part.reshape(-1, LANES), "gather_mod").reshape(N_DEV, nsub, N_DEV, cs)
    m_mine = lax.dynamic_index_in_dim(m_all[0::2], dev, axis=2, keepdims=False)
    mvec = m_mine.transpose(1, 0, 2).reshape(nsub, 3 * d)

    halved = {BIG.index(k) for k in HALVED}
    send_sems, recv_sems, shard_thru, lands, token = _gather_start(cast, halved, mvec, "gather_start")

    def fetch(k, after):
        w = BIG.index(k)
        shard, gw = _gather_wait(w, shard_thru[w], lands[w], send_sems, recv_sems, after, f"gather_wait_{k}", w in halved)
        if w in halved:
            gw = _assemble_halves(shard, gw, f"assemble_{k}")
        return gw if BIG_KIND[k] == "col" else gw.reshape(1, -1, gw.shape[-1])

    scattering, pending = {}, {}

    def emit(k, g):
        pending[k] = g
        group = next(gr for gr in SCATTER_GROUPS if k in gr)
        if k != group[-1]:
            return None
        scattering[group] = _scatter_start([pending[m] for m in group], [BIG_KIND[m] for m in group], f"scatter_start_{k}")
        return scattering[group][2][0]

    tril = jnp.tril(jnp.ones((CHUNK, CHUNK), bool))
    wc = jnp.where(tril, a_w_s[0], 0.0).astype(MXU_DTYPE)
    heads = jnp.arange(1, B_HEADS + 1, dtype=F32)
    small = {
        "a_b_in": a_b_in, "a_vn_g": a_vn_g, "a_vn_b": a_vn_b,
        "wc": wc, "wct": wc.transpose(0, 2, 1),
        "bias_full": jnp.repeat(a_b_s[0].T, d // A_GROUPS, axis=1),
        "slopes": jnp.exp2(-8.0 * heads / B_HEADS),
    }

    loss_part, grad_x, gb, dm, dlg, dlb, gsmall = _local_step(x[0], loss_target[0], mvec, ln_g_full, ln_b_full, small, fetch, emit, token)
    loss = lax.psum(loss_part, ("x", "y", "c"))

    weights = dict(ada_w=ada_w, ada_b=ada_b, ln_g=ln_g, ln_b=ln_b, a_w_in=a_w_in, a_b_in=a_b_in, a_vn_g=a_vn_g, a_vn_b=a_vn_b,
                   a_w_s=a_w_s, a_b_s=a_b_s, a_w_out=a_w_out, b_w_qkv=b_w_qkv, b_w_out=b_w_out, mlp_w_up=mlp_w_up, mlp_w_down=mlp_w_down)
    ms = dict(ada_w=m_ada_w, ada_b=m_ada_b, ln_g=m_ln_g, ln_b=m_ln_b, a_w_in=m_a_w_in, a_b_in=m_a_b_in, a_vn_g=m_a_vn_g, a_vn_b=m_a_vn_b,
              a_w_s=m_a_w_s, a_b_s=m_a_b_s, a_w_out=m_a_w_out, b_w_qkv=m_b_w_qkv, b_w_out=m_b_w_out, mlp_w_up=m_mlp_w_up, mlp_w_down=m_mlp_w_down)
    vs = dict(ada_w=v_ada_w, ada_b=v_ada_b, ln_g=v_ln_g, ln_b=v_ln_b, a_w_in=v_a_w_in, a_b_in=v_a_b_in, a_vn_g=v_a_vn_g, a_vn_b=v_a_vn_b,
              a_w_s=v_a_w_s, a_b_s=v_a_b_s, a_w_out=v_a_w_out, b_w_qkv=v_b_w_qkv, b_w_out=v_b_w_out, mlp_w_up=v_mlp_w_up, mlp_w_down=v_mlp_w_down)
    grads, updates = {}, {}

    def update(k):
        updates[k] = _adamw(weights[k], grads[k], ms[k], vs[k], f"adamw_{k}")
        return updates[k][0]

    pack_b = jnp.concatenate([dm.reshape(-1), dlg.reshape(-1), dlb.reshape(-1)] + [gsmall[k] for k in SMALL])
    n_small = pack_b.shape[0]
    pack_b = jnp.pad(pack_b, (0, -n_small % (ROW_TILE * LANES)))
    got_b = _all_gather_small(pack_b.reshape(-1, LANES), "gather_small_grads").reshape(N_DEV, -1, LANES)
    tot = _sum_slots(got_b, "sum_small").reshape(-1)
    o = 0
    dm_tot = tot[o:o + nsub * 3 * d].reshape(nsub, 3 * d); o += nsub * 3 * d
    dlg_tot = tot[o:o + nsub * d].reshape(nsub, d); o += nsub * d
    dlb_tot = tot[o:o + nsub * d].reshape(nsub, d); o += nsub * d
    g_small = {}
    for k, ref in zip(SMALL, (a_b_in, a_vn_g, a_vn_b, a_b_s, a_w_s)):
        g_small[k] = tot[o:o + ref.size].reshape(ref.shape); o += ref.size
    assert o == n_small
    dm_all = got_b.reshape(N_DEV, -1)[:, :nsub * 3 * d].reshape(N_DEV, nsub, 3 * d)
    dm_cols = lax.dynamic_slice_in_dim(dm_all, q * cs, cs, axis=2).transpose(1, 0, 2)

    grads.update({
        "ada_w": _ada_bwd(c_all.T, dm_cols, "ada_bwd").reshape(ada_w.shape),
        "ada_b": lax.dynamic_slice_in_dim(dm_tot, q * cs, cs, axis=1).reshape(ada_b.shape),
        "ln_g": lax.dynamic_slice_in_dim(dlg_tot, q * ls, ls, axis=1).reshape(ln_g.shape),
        "ln_b": lax.dynamic_slice_in_dim(dlb_tot, q * ls, ls, axis=1).reshape(ln_b.shape),
        **g_small,
    })
    for k in ("ada_b", "ln_g", "ln_b") + SMALL:
        update(k)
    done = update("ada_w")

    gfull = {}
    for group in (SCATTER_GROUPS[0] + SCATTER_GROUPS[1], SCATTER_GROUPS[2] + SCATTER_GROUPS[3]):
        bufs = []
        for pair in (group[:2], group[2:]):
            bufs += _scatter_wait(*scattering[pair], [BIG_KIND[m] for m in pair], done, f"scatter_wait_{pair[-1]}")
        halves = [_sum_slots(b, f"sum_{k}") for k, b in zip(group, bufs)]
        fulls = _swap_halves(halves, f"swap_halves_{group[0]}")
        gfull.update({k: f.reshape(-1, f.shape[-1]) for k, f in zip(group, fulls)})
        if group[0] == "down1":
            grads["b_w_qkv"], grads["b_w_out"] = gfull["b_w_qkv"][None], gfull["b_w_out"][None]
            update("b_w_out")
            done = update("b_w_qkv")
    grads.update({
        "a_w_in": gfull["a_w_in"][None], "a_w_out": gfull["a_w_out"][None],
        "mlp_w_up": jnp.stack([gfull["up0"], gfull["up1"]]), "mlp_w_down": jnp.stack([gfull["down0"], gfull["down1"]]),
    })
    for k in ("a_w_in", "a_w_out", "mlp_w_up", "mlp_w_down"):
        update(k)
    names = list(weights)
    return (loss, grad_x[None], *[grads[k] for k in names], *[updates[k][0] for k in names],
            *[updates[k][1] for k in names], *[updates[k][2] for k in names])
```

```python
import functools
import math

import jax
import jax.numpy as jnp
from jax import lax
from jax.experimental import pallas as pl
from jax.experimental.pallas import tpu as pltpu

F32 = jnp.float32
MXU_DTYPE = jnp.bfloat16

DEPTH = 2
CHUNK = 128
A_GROUPS = 16
B_HEADS = 16
HEAD_DIM = 64
B_PATTERNS = ((128, 1), (512, 4), (2048, 16))
SPAN = 128
ALPHA = (2 * DEPTH) ** 0.25
LN_EPS = 1e-5
NEG = -1e30
ATT_SCALE = HEAD_DIM ** -0.5
ADAM_LR, ADAM_B1, ADAM_B2, ADAM_EPS, ADAM_WD, ADAM_STEP = 0.001, 0.9, 0.999, 1e-08, 0.01, 10

N_CHIPS = 4
N_DEV = 8
LANES = 128
SUBLANES = 8
VMEM_LIMIT = 52 * 1024 * 1024
ROW_TILE = 512
MM_ROW_CHUNK = 256
MESH = pl.DeviceIdType.MESH


def _cparams(sem):
    return pltpu.CompilerParams(dimension_semantics=sem, vmem_limit_bytes=VMEM_LIMIT)


def _fold8(v):
    r, c = v.shape
    return jnp.sum(v.reshape(r // SUBLANES, SUBLANES, c), axis=0)


def _gelu(x):
    c = math.sqrt(2.0 / math.pi)
    return 0.5 * x * (1.0 + jnp.tanh(c * (x + 0.044715 * (x * x * x))))


def _gelu_grad(x):
    c = math.sqrt(2.0 / math.pi)
    t = jnp.tanh(c * (x + 0.044715 * (x * x * x)))
    return 0.5 * (1.0 + t) + 0.5 * x * (1.0 - t * t) * c * (1.0 + 3.0 * 0.044715 * x * x)


def _dot(a, b, dims):
    return lax.dot_general(a.astype(MXU_DTYPE), b.astype(MXU_DTYPE), (dims, ((), ())), preferred_element_type=F32)


def _dot_nn(a, b):
    return _dot(a, b, ((1,), (0,)))


def _dot_nt(a, b):
    return _dot(a, b, ((1,), (1,)))


def _dot_tn(a, b):
    return _dot(a, b, ((0,), (0,)))


def _mm(a, b, *, mode, name, outs, tm, tn, tk, epi=None, extras=(), b_col0=0, n_out=None, after=None,
        out_col0=0, out_cols=None, into=None):
    if mode == "nn":
        m, kdim = a.shape
        p, kb, ns = b.shape
        assert kb == kdim and ns % tn == 0 and b_col0 % tn == 0
        n = n_out if n_out is not None else p * ns
        npt, j0 = ns // tn, b_col0 // tn
        a_spec = pl.BlockSpec((tm, tk), lambda i, j, k: (i, k))
        b_spec = pl.BlockSpec((None, tk, tn), lambda i, j, k: ((j + j0) // npt, k, (j + j0) % npt))
        dot = _dot_nn
    elif mode == "nt":
        m, kdim = a.shape
        p, n, ns = b.shape
        assert ns % tk == 0 and b_col0 % tk == 0
        npt, j0 = ns // tk, b_col0 // tk
        a_spec = pl.BlockSpec((tm, tk), lambda i, j, k: (i, k))
        b_spec = pl.BlockSpec((None, tn, tk), lambda i, j, k: ((k + j0) // npt, j, (k + j0) % npt))
        dot = _dot_nt
    else:
        kdim, m = a.shape
        kb, n = b.shape
        assert kb == kdim
        a_spec = pl.BlockSpec((tk, tm), lambda i, j, k: (k, i))
        b_spec = pl.BlockSpec((tk, tn), lambda i, j, k: (k, j))
        dot = _dot_tn
    assert m % tm == 0 and n % tn == 0 and kdim % tk == 0, (name, m, n, kdim, tm, tn, tk)
    nk = kdim // tk
    ex_specs, ex_arrays = [], []
    for kind, arr in extras:
        if kind == "row":
            ex_specs.append(pl.BlockSpec((1, tn), lambda i, j, k: (0, j)))
        else:
            ex_specs.append(pl.BlockSpec((tm, tn), lambda i, j, k: (i, j)))
        ex_arrays.append(arr)
    n_ex, n_o = len(ex_arrays), len(outs)
    deps = [d for d in (after, into) if d is not None]
    n_dep = len(deps)
    j_out = out_col0 // tn
    assert out_col0 % tn == 0 and (into is None or len(outs) == 1)

    def body(a_ref, b_ref, *rest):
        ex_refs, o_refs = rest[:n_ex], rest[n_ex + n_dep:n_ex + n_dep + n_o]
        k = pl.program_id(2)

        chunks = [slice(r0, r0 + min(tm, MM_ROW_CHUNK)) for r0 in range(0, tm, min(tm, MM_ROW_CHUNK))]

        def part(rows):
            return dot(a_ref[:, rows] if mode == "tn" else a_ref[rows, :], b_ref[...])

        def finish(r, rows):
            exs = [e[...] if kind == "row" else e[rows, :] for (kind, _), e in zip(extras, ex_refs)]
            vals = epi(r, *exs) if epi is not None else [r]
            for o, v in zip(o_refs, vals):
                o[rows, :] = v.astype(o.dtype)

        if nk == 1:
            for rows in chunks:
                finish(part(rows), rows)
            return
        acc = rest[n_ex + n_dep + n_o]

        @pl.when(k == 0)
        def _():
            for rows in chunks:
                acc[rows, :] = part(rows)

        @pl.when((k > 0) & (k < nk - 1))
        def _():
            for rows in chunks:
                acc[rows, :] += part(rows)

        @pl.when(k == nk - 1)
        def _():
            for rows in chunks:
                finish(acc[rows, :] + part(rows), rows)

    res = pl.pallas_call(
        body,
        grid=(m // tm, n // tn, nk),
        in_specs=[a_spec, b_spec] + ex_specs + [pl.BlockSpec(memory_space=pl.ANY)] * n_dep,
        out_specs=[pl.BlockSpec((tm, tn), lambda i, j, k: (i, j + j_out)) for _ in outs],
        out_shape=[jax.ShapeDtypeStruct((m, out_cols or n), dt) for dt in outs],
        input_output_aliases={} if into is None else {2 + n_ex + n_dep - 1: 0},
        scratch_shapes=[pltpu.VMEM((tm, tn), F32)] if nk > 1 else [],
        name=name,
        compiler_params=_cparams(("parallel", "parallel", "arbitrary")),
    )(a, b, *ex_arrays, *deps)
    return res if len(outs) > 1 else res[0]


def _rows(body, n_rows, tr, ins, outs, name, scratch=()):
    def spec(kind, shape):
        if kind == "blk":
            return pl.BlockSpec((tr,) + tuple(shape[1:]), lambda i: (i,) + (0,) * (len(shape) - 1))
        if kind == "dep":
            return pl.BlockSpec(memory_space=pl.ANY)
        return pl.BlockSpec(tuple(shape), lambda i: (0,) * len(shape))

    return pl.pallas_call(
        body,
        grid=(n_rows // tr,),
        in_specs=[spec(k, a.shape) for k, a in ins],
        out_specs=[spec(k, s) for k, s, _ in outs],
        out_shape=[jax.ShapeDtypeStruct(tuple(s), d) for _, s, d in outs],
        scratch_shapes=list(scratch),
        name=name,
        compiler_params=_cparams(("arbitrary",)),
    )(*[a for _, a in ins])


def _ln_stats(z):
    mu = jnp.mean(z, axis=-1, keepdims=True)
    zc = z - mu
    var = jnp.mean(zc * zc, axis=-1, keepdims=True)
    rstd = lax.rsqrt(var + LN_EPS)
    return zc * rstd, rstd


def _mod(x, scale, shift, after, name):
    s, d = x.shape

    def body(x_ref, sc_ref, sh_ref, dep_ref, h_ref):
        h_ref[...] = (x_ref[...] * (1.0 + sc_ref[...]) + sh_ref[...]).astype(h_ref.dtype)

    return _rows(body, s, ROW_TILE, [("blk", x), ("all", scale), ("all", shift), ("dep", after)], [("blk", (s, d), MXU_DTYPE)], name)[0]


def _resid_ln(x, y, gate, g, b, nxt, name):
    s, d = x.shape

    def body(x_ref, y_ref, gate_ref, g_ref, b_ref, sc_ref, sh_ref, xn_ref, h_ref):
        z = ALPHA * x_ref[...] + gate_ref[...] * y_ref[...]
        xhat, _ = _ln_stats(z)
        xn = xhat * g_ref[...] + b_ref[...]
        xn_ref[...] = xn
        h_ref[...] = (xn * (1.0 + sc_ref[...]) + sh_ref[...]).astype(h_ref.dtype)

    return _rows(body, s, ROW_TILE,
                 [("blk", x), ("blk", y), ("all", gate), ("all", g), ("all", b), ("all", nxt[0]), ("all", nxt[1])],
                 [("blk", (s, d), F32), ("blk", (s, d), MXU_DTYPE)], name)


def _mod_bwd(dxr, dhs, x, scale, name, after=None):
    s, d = x.shape
    n_dh = len(dhs)
    n_dep = 0 if after is None else 1

    def body(dxr_ref, *rest):
        dh_refs = rest[:n_dh]
        x_ref, sc_ref, dx_ref, red_ref, a_sh, a_sc = rest[n_dh:n_dh + 2] + rest[n_dh + 2 + n_dep:]
        i = pl.program_id(0)

        @pl.when(i == 0)
        def _():
            a_sh[...] = jnp.zeros_like(a_sh)
            a_sc[...] = jnp.zeros_like(a_sc)

        dh = dh_refs[0][...]
        for r in dh_refs[1:]:
            dh = dh + r[...]
        dx_ref[...] = dxr_ref[...] + dh * (1.0 + sc_ref[...])
        a_sh[...] += _fold8(dh)
        a_sc[...] += _fold8(dh * x_ref[...])

        @pl.when(i == pl.num_programs(0) - 1)
        def _():
            red_ref[...] = jnp.zeros_like(red_ref)
            red_ref[0:1, :] = jnp.sum(a_sh[...], axis=0, keepdims=True)
            red_ref[1:2, :] = jnp.sum(a_sc[...], axis=0, keepdims=True)

    return _rows(body, s, ROW_TILE, [("blk", dxr)] + [("blk", h) for h in dhs] + [("blk", x), ("all", scale)] + [("dep", after)] * n_dep,
                 [("blk", (s, d), F32), ("all", (SUBLANES, d), F32)], name,
                 scratch=[pltpu.VMEM((SUBLANES, d), F32)] * 2)


def _last_ln_loss_bwd(x, y, gate, g, b, target, name):
    s, d = x.shape

    def body(x_ref, y_ref, gate_ref, g_ref, b_ref, t_ref, l_ref, dxr_ref, dyy_ref, red_ref, a_l, a_g, a_b, a_gate):
        i = pl.program_id(0)

        @pl.when(i == 0)
        def _():
            for a in (a_l, a_g, a_b, a_gate):
                a[...] = jnp.zeros_like(a)

        yv = y_ref[...]
        z = ALPHA * x_ref[...] + gate_ref[...] * yv
        xhat, rstd = _ln_stats(z)
        e = xhat * g_ref[...] + b_ref[...] - t_ref[...]
        a_l[...] += _fold8(e * e)
        dxo_v = e * (1.0 / d)
        dxh = dxo_v * g_ref[...]
        dz = rstd * (dxh - jnp.mean(dxh, axis=-1, keepdims=True) - xhat * jnp.mean(dxh * xhat, axis=-1, keepdims=True))
        dxr_ref[...] = ALPHA * dz
        dyy_ref[...] = (gate_ref[...] * dz).astype(dyy_ref.dtype)
        a_g[...] += _fold8(dxo_v * xhat)
        a_b[...] += _fold8(dxo_v)
        a_gate[...] += _fold8(dz * yv)

        @pl.when(i == pl.num_programs(0) - 1)
        def _():
            l_ref[...] = jnp.full(l_ref.shape, 0.5 / d, F32) * jnp.sum(a_l[...])
            red_ref[...] = jnp.zeros_like(red_ref)
            red_ref[0:1, :] = jnp.sum(a_g[...], axis=0, keepdims=True)
            red_ref[1:2, :] = jnp.sum(a_b[...], axis=0, keepdims=True)
            red_ref[2:3, :] = jnp.sum(a_gate[...], axis=0, keepdims=True)

    l, dxr, dyy, red = _rows(
        body, s, ROW_TILE, [("blk", x), ("blk", y), ("all", gate), ("all", g), ("all", b), ("blk", target)],
        [("all", (SUBLANES, LANES), F32), ("blk", (s, d), F32), ("blk", (s, d), MXU_DTYPE), ("all", (SUBLANES, d), F32)], name,
        scratch=[pltpu.VMEM((SUBLANES, d), F32)] * 4)
    return l[0, 0], dxr, dyy, red


def _mod_ln_bwd(dxr, dhs, x, scale, x_in, y, gate, g, name, after=None):
    s, d = x.shape
    n_dh = len(dhs)
    n_dep = 0 if after is None else 1

    def body(dxr_ref, *rest):
        dh_refs = rest[:n_dh]
        x_ref, sc_ref, xin_ref, y_ref, gate_ref, g_ref = rest[n_dh:n_dh + 6]
        dxr_out, dyy_ref, red_mod, red_ln, a_sh, a_sc, a_g, a_b, a_gate = rest[n_dh + 6 + n_dep:]
        i = pl.program_id(0)

        @pl.when(i == 0)
        def _():
            for a in (a_sh, a_sc, a_g, a_b, a_gate):
                a[...] = jnp.zeros_like(a)

        dh = dh_refs[0][...]
        for r in dh_refs[1:]:
            dh = dh + r[...]
        xv = x_ref[...]
        dxo_v = dxr_ref[...] + dh * (1.0 + sc_ref[...])
        a_sh[...] += _fold8(dh)
        a_sc[...] += _fold8(dh * xv)
        yv = y_ref[...]
        z = ALPHA * xin_ref[...] + gate_ref[...] * yv
        xhat, rstd = _ln_stats(z)
        dxh = dxo_v * g_ref[...]
        dz = rstd * (dxh - jnp.mean(dxh, axis=-1, keepdims=True) - xhat * jnp.mean(dxh * xhat, axis=-1, keepdims=True))
        dxr_out[...] = ALPHA * dz
        dyy_ref[...] = (gate_ref[...] * dz).astype(dyy_ref.dtype)
        a_g[...] += _fold8(dxo_v * xhat)
        a_b[...] += _fold8(dxo_v)
        a_gate[...] += _fold8(dz * yv)

        @pl.when(i == pl.num_programs(0) - 1)
        def _():
            red_mod[...] = jnp.zeros_like(red_mod)
            red_mod[0:1, :] = jnp.sum(a_sh[...], axis=0, keepdims=True)
            red_mod[1:2, :] = jnp.sum(a_sc[...], axis=0, keepdims=True)
            red_ln[...] = jnp.zeros_like(red_ln)
            red_ln[0:1, :] = jnp.sum(a_g[...], axis=0, keepdims=True)
            red_ln[1:2, :] = jnp.sum(a_b[...], axis=0, keepdims=True)
            red_ln[2:3, :] = jnp.sum(a_gate[...], axis=0, keepdims=True)

    ins = ([("blk", dxr)] + [("blk", h) for h in dhs]
           + [("blk", x), ("all", scale), ("blk", x_in), ("blk", y), ("all", gate), ("all", g)] + [("dep", after)] * n_dep)
    return _rows(body, s, ROW_TILE, ins,
                 [("blk", (s, d), F32), ("blk", (s, d), MXU_DTYPE), ("all", (SUBLANES, d), F32), ("all", (SUBLANES, d), F32)], name,
                 scratch=[pltpu.VMEM((SUBLANES, d), F32)] * 5)


def _left_half(shape):
    return lax.broadcasted_iota(jnp.int32, shape, 1) < (LANES // 2)


CHUNKS_PER_STEP = 2


def _chunks_of_step():
    return [slice(i * CHUNK, (i + 1) * CHUNK) for i in range(CHUNKS_PER_STEP)]


def _spatial_z(vn, wc_ref, bias_ref, j):
    vb = vn[:, j * LANES:(j + 1) * LANES]
    z0 = _dot_nn(wc_ref[2 * j], vb)
    z1 = _dot_nn(wc_ref[2 * j + 1], vb)
    return jnp.where(_left_half(z0.shape), z0, z1) + bias_ref[:, j * LANES:(j + 1) * LANES]


def _spatial_fwd(uvpre, vn_g, vn_b, wc, bias_full, name):
    s, d2 = uvpre.shape
    d = d2 // 2

    def body(uv_ref, g_ref, b_ref, wc_ref, bias_ref, out_ref):
        for rows in _chunks_of_step():
            u = _gelu(uv_ref[rows, :d])
            v = _gelu(uv_ref[rows, d:])
            vh, _ = _ln_stats(v)
            vn = vh * g_ref[...] + b_ref[...]
            for j in range(d // LANES):
                z = _spatial_z(vn, wc_ref, bias_ref, j)
                out_ref[rows, j * LANES:(j + 1) * LANES] = (u[:, j * LANES:(j + 1) * LANES] * z).astype(out_ref.dtype)

    return _rows(body, s, CHUNKS_PER_STEP * CHUNK, [("blk", uvpre), ("all", vn_g), ("all", vn_b), ("all", wc), ("all", bias_full)],
                 [("blk", (s, d), MXU_DTYPE)], name)[0]


def _spatial_bwd(uvpre, dgated, vn_g, vn_b, wc, wct, bias_full, name):
    s, d2 = uvpre.shape
    d = d2 // 2

    def body(uv_ref, dg_ref, g_ref, b_ref, wc_ref, wct_ref, bias_ref,
             duv_ref, dws_ref, dbias_ref, dbin_ref, dvg_ref, dvb_ref, dvn_buf, a_bin, a_vg, a_vb):
        i = pl.program_id(0)

        @pl.when(i == 0)
        def _():
            dws_ref[...] = jnp.zeros_like(dws_ref)
            dbias_ref[...] = jnp.zeros_like(dbias_ref)
            a_bin[...] = jnp.zeros_like(a_bin)
            a_vg[...] = jnp.zeros_like(a_vg)
            a_vb[...] = jnp.zeros_like(a_vb)

        for rows in _chunks_of_step():
            up = uv_ref[rows, :d]
            vp = uv_ref[rows, d:]
            u = _gelu(up)
            v = _gelu(vp)
            vh, rstd = _ln_stats(v)
            vn = vh * g_ref[...] + b_ref[...]
            dg = dg_ref[rows, :]
            dzz = dg * u
            dbias_ref[...] += dzz
            for j in range(d // LANES):
                cols = slice(j * LANES, (j + 1) * LANES)
                z = _spatial_z(vn, wc_ref, bias_ref, j)
                dup = dg[:, cols] * z * _gelu_grad(up[:, cols])
                duv_ref[rows, cols] = dup.astype(duv_ref.dtype)
                a_bin[:, cols] += _fold8(dup)
                dzb = dzz[:, cols]
                left = _left_half(dzb.shape)
                dvn_buf[:, cols] = jnp.where(left, _dot_nn(wct_ref[2 * j], dzb), _dot_nn(wct_ref[2 * j + 1], dzb))
                vb = vn[:, cols]
                dws_ref[2 * j] += _dot_nt(jnp.where(left, dzb, 0.0), vb)
                dws_ref[2 * j + 1] += _dot_nt(jnp.where(left, 0.0, dzb), vb)
            dvn = dvn_buf[...]
            a_vg[...] += _fold8(dvn * vh)
            a_vb[...] += _fold8(dvn)
            dvh = dvn * g_ref[...]
            dv = rstd * (dvh - jnp.mean(dvh, axis=-1, keepdims=True) - vh * jnp.mean(dvh * vh, axis=-1, keepdims=True))
            dvp = dv * _gelu_grad(vp)
            duv_ref[rows, d:] = dvp.astype(duv_ref.dtype)
            a_bin[:, d:] += _fold8(dvp)

        @pl.when(i == pl.num_programs(0) - 1)
        def _():
            dbin_ref[...] = jnp.sum(a_bin[...], axis=0, keepdims=True)
            dvg_ref[...] = jnp.sum(a_vg[...], axis=0, keepdims=True)
            dvb_ref[...] = jnp.sum(a_vb[...], axis=0, keepdims=True)

    return _rows(body, s, CHUNKS_PER_STEP * CHUNK,
                 [("blk", uvpre), ("blk", dgated), ("all", vn_g), ("all", vn_b), ("all", wc), ("all", wct), ("all", bias_full)],
                 [("blk", (s, d2), MXU_DTYPE), ("all", (A_GROUPS, CHUNK, CHUNK), F32), ("all", (CHUNK, d), F32),
                  ("all", (1, d2), F32), ("all", (1, d), F32), ("all", (1, d), F32)], name,
                 scratch=[pltpu.VMEM((CHUNK, d), F32), pltpu.VMEM((SUBLANES, d2), F32),
                          pltpu.VMEM((SUBLANES, d), F32), pltpu.VMEM((SUBLANES, d), F32)])


def _head_mask(v, h):
    lane = lax.broadcasted_iota(jnp.int32, v.shape, 1)
    return jnp.where((lane >= h * HEAD_DIM) & (lane < (h + 1) * HEAD_DIM), v, jnp.zeros_like(v))


def _att_bias(slopes, dil):
    qi = lax.broadcasted_iota(jnp.int32, (SPAN, SPAN), 0)
    ki = lax.broadcasted_iota(jnp.int32, (SPAN, SPAN), 1)
    sl = slopes[:, None, None]
    cur = jnp.where(ki <= qi, -sl * (float(dil) * (qi - ki).astype(F32)), NEG)
    prev = jnp.where(ki >= qi, -sl * (float(dil) * (SPAN + qi - ki).astype(F32)), NEG)
    absent = jnp.full_like(prev, NEG)
    pairs = slopes.shape[0] // 2

    def fwd(pv):
        return jnp.concatenate([cur, pv], axis=2).reshape(pairs, 2 * SPAN, 2 * SPAN)

    def bwd(pv):
        return jnp.concatenate([cur.reshape(pairs, 2 * SPAN, SPAN), pv.reshape(pairs, 2 * SPAN, SPAN)], axis=1)

    return jnp.stack([fwd(absent), fwd(prev)]), jnp.stack([bwd(absent), bwd(prev)])


def _att_specs(s, d, dil, kinds):
    nb = s // (dil * SPAN)

    def rowblk(which, b):
        if which == "prev":
            return jnp.where(b % nb == 0, b, b - 1)
        if which == "next":
            return jnp.where(b % nb == nb - 1, b, b + 1)
        return b

    return [pl.BlockSpec((SPAN, d), functools.partial(lambda b, o, w: (rowblk(w, b), o), o=part, w=which))
            for part, which in kinds]


def _head_col(v, head):
    return v[:, head:head + 1]


def _expand_heads(w, j):
    shape = (w.shape[0], LANES)
    return jnp.where(_left_half(shape), jnp.broadcast_to(_head_col(w, 2 * j), shape), jnp.broadcast_to(_head_col(w, 2 * j + 1), shape))


def _attn_fwd(qkv, slopes, dil, name):
    s, d3 = qkv.shape
    d = d3 // 3
    nb = s // (dil * SPAN)
    table, _ = _att_bias(slopes, dil)

    def body(q_ref, kc_ref, kp_ref, vc_ref, vp_ref, tb_ref, o_ref, l_ref):
        left = _left_half((SPAN, LANES))
        lane = lax.broadcasted_iota(jnp.int32, (SPAN, LANES), 1)
        lses = jnp.zeros((SPAN, LANES), F32)
        for hp in range(d // LANES):
            cols = slice(hp * LANES, (hp + 1) * LANES)
            q = q_ref[:, cols]
            q2 = jnp.concatenate([_head_mask(q, 0), _head_mask(q, 1)], axis=0) * ATT_SCALE
            k2 = jnp.concatenate([kc_ref[:, cols], kp_ref[:, cols]], axis=0)
            v2 = jnp.concatenate([vc_ref[:, cols], vp_ref[:, cols]], axis=0)
            sc = _dot_nt(q2, k2) + tb_ref[hp]
            m = jnp.max(sc, axis=-1, keepdims=True)
            p = jnp.exp(sc - m)
            l = jnp.sum(p, axis=-1, keepdims=True)
            r = _dot_nn(p, v2) * (1.0 / l)
            lse = m + jnp.log(l)
            o_ref[:, cols] = jnp.where(left, r[:SPAN], r[SPAN:])
            lses = jnp.where(lane == 2 * hp, lse[:SPAN], jnp.where(lane == 2 * hp + 1, lse[SPAN:], lses))
        l_ref[...] = lses

    specs = _att_specs(s, d, dil, [(0, "cur"), (1, "cur"), (1, "prev"), (2, "cur"), (2, "prev")])
    tbl = pl.BlockSpec((None,) + table.shape[1:], lambda b: (jnp.where(b % nb == 0, 0, 1), 0, 0, 0))
    out_spec = pl.BlockSpec((SPAN, d), lambda b: (b, 0))
    return pl.pallas_call(
        body,
        grid=(s // SPAN,),
        in_specs=specs + [tbl],
        out_specs=[out_spec, pl.BlockSpec((SPAN, LANES), lambda b: (b, 0))],
        out_shape=[jax.ShapeDtypeStruct((s, d), F32), jax.ShapeDtypeStruct((s, LANES), F32)],
        name=name,
        compiler_params=_cparams(("parallel",)),
    )(qkv, qkv, qkv, qkv, qkv, table)


def _attn_bwd(qkv, do, lse, dd, slopes, dil, name):
    s, d3 = qkv.shape
    d = d3 // 3
    nb = s // (dil * SPAN)
    _, table = _att_bias(slopes, dil)

    def heads_stacked(cur, nxt):
        return jnp.concatenate([_head_mask(cur, 0), _head_mask(cur, 1), _head_mask(nxt, 0), _head_mask(nxt, 1)], axis=0)

    def cols_stacked(cur, nxt, hp):
        return jnp.concatenate([jnp.broadcast_to(_head_col(a, 2 * hp + h), (SPAN, LANES)) for a in (cur, nxt) for h in range(2)], axis=0)

    def body(k_ref, v_ref, qc_ref, qn_ref, doc_ref, don_ref, lc_ref, ln_ref, ddc_ref, ddn_ref, tb_ref, out_ref, carry):
        b = pl.program_id(0)

        @pl.when(b == 0)
        def _():
            carry[...] = jnp.zeros_like(carry)

        left = _left_half((SPAN, LANES))
        lse_c, lse_n, dd_c, dd_n = lc_ref[...], ln_ref[...], ddc_ref[...], ddn_ref[...]
        for hp in range(d // LANES):
            cols = slice(hp * LANES, (hp + 1) * LANES)
            k, v = k_ref[:, cols], v_ref[:, cols]
            q4 = heads_stacked(qc_ref[:, cols], qn_ref[:, cols])
            do4 = heads_stacked(doc_ref[:, cols], don_ref[:, cols])
            sc = _dot_nt(q4 * ATT_SCALE, k) + tb_ref[hp]
            p = jnp.exp(sc - cols_stacked(lse_c, lse_n, hp))
            ds = p * (_dot_nt(do4, v) - cols_stacked(dd_c, dd_n, hp))
            dq4 = _dot_nn(ds, k)
            dq_cur = jnp.where(left, dq4[:SPAN], dq4[SPAN:2 * SPAN]) + carry[:, cols]
            carry[:, cols] = jnp.where(left, dq4[2 * SPAN:3 * SPAN], dq4[3 * SPAN:])
            out_ref[:, cols] = (dq_cur * ATT_SCALE).astype(out_ref.dtype)
            out_ref[:, d + hp * LANES:d + (hp + 1) * LANES] = (_dot_tn(ds, q4) * ATT_SCALE).astype(out_ref.dtype)
            out_ref[:, 2 * d + hp * LANES:2 * d + (hp + 1) * LANES] = _dot_tn(p, do4).astype(out_ref.dtype)

    qkv_specs = _att_specs(s, d, dil, [(1, "cur"), (2, "cur"), (0, "cur"), (0, "next")])
    pair = _att_specs(s, d, dil, [(0, "cur"), (0, "next")])
    heads = _att_specs(s, LANES, dil, [(0, "cur"), (0, "next")])
    tbl = pl.BlockSpec((None,) + table.shape[1:], lambda b: (jnp.where(b % nb == nb - 1, 0, 1), 0, 0, 0))
    return pl.pallas_call(
        body,
        grid=(s // SPAN,),
        in_specs=qkv_specs + pair + heads + heads + [tbl],
        out_specs=pl.BlockSpec((SPAN, d3), lambda b: (b, 0)),
        out_shape=jax.ShapeDtypeStruct((s, d3), MXU_DTYPE),
        scratch_shapes=[pltpu.VMEM((SPAN, d), F32)],
        name=name,
        compiler_params=_cparams(("arbitrary",)),
    )(qkv, qkv, qkv, qkv, do, do, lse, lse, dd, dd, table)


def _mix_weights(l_refs):
    ls = [r[...] for r in l_refs]
    m = functools.reduce(jnp.maximum, ls)
    es = [jnp.exp(l - m) for l in ls]
    tot = functools.reduce(lambda a, c: a + c, es)
    return [e / tot for e in es]


def _combine_fwd(os_, ls_, name):
    s, d = os_[0].shape
    n = len(os_)

    def body(*refs):
        o_refs, l_refs, out_ref = refs[:n], refs[n:2 * n], refs[2 * n]
        ws = _mix_weights(l_refs)
        for j in range(d // LANES):
            cols = slice(j * LANES, (j + 1) * LANES)
            acc = _expand_heads(ws[0], j) * o_refs[0][:, cols]
            for w, o in zip(ws[1:], o_refs[1:]):
                acc = acc + _expand_heads(w, j) * o[:, cols]
            out_ref[:, cols] = acc

    return _rows(body, s, ROW_TILE, [("blk", a) for a in os_ + ls_], [("blk", (s, d), F32)], name)[0]


def _combine_bwd(do, o, ls_, name):
    s, d = o.shape
    n = len(ls_)
    sel = (lax.broadcasted_iota(jnp.int32, (d, LANES), 0) // HEAD_DIM == lax.broadcasted_iota(jnp.int32, (d, LANES), 1)).astype(F32)

    def body(do_ref, o_ref, *rest):
        l_refs, sel_ref, outs = rest[:n], rest[n], rest[n + 1:]
        ws = _mix_weights(l_refs)
        dov = do_ref[...]
        r = jnp.dot(dov * o_ref[...], sel_ref[...], precision=lax.Precision.HIGHEST, preferred_element_type=F32)
        for g in range(n):
            outs[2 * g + 1][...] = ws[g] * r
            for j in range(d // LANES):
                cols = slice(j * LANES, (j + 1) * LANES)
                outs[2 * g][:, cols] = (_expand_heads(ws[g], j) * dov[:, cols]).astype(outs[2 * g].dtype)

    outs = []
    for _ in range(n):
        outs += [("blk", (s, d), MXU_DTYPE), ("blk", (s, LANES), F32)]
    res = _rows(body, s, ROW_TILE, [("blk", do), ("blk", o)] + [("blk", l) for l in ls_] + [("all", sel)], outs, name)
    return [(res[2 * g], res[2 * g + 1]) for g in range(n)]


def _ada_fwd(c_all, w, b, name):
    nsub, d, cs = w.shape

    def body(c_ref, w_ref, b_ref, o_ref):
        cv = c_ref[...]
        sc = cv * (1.0 / (1.0 + jnp.exp(-cv)))
        o_ref[...] = _dot_nn(sc, w_ref[...]) + b_ref[...]

    return pl.pallas_call(
        body,
        grid=(nsub,),
        in_specs=[pl.BlockSpec(c_all.shape, lambda i: (0, 0)), pl.BlockSpec((None, d, cs), lambda i: (i, 0, 0)),
                  pl.BlockSpec((None, 1, cs), lambda i: (i, 0, 0))],
        out_specs=pl.BlockSpec((None, N_DEV, cs), lambda i: (i, 0, 0)),
        out_shape=jax.ShapeDtypeStruct((nsub, N_DEV, cs), F32),
        name=name,
        compiler_params=_cparams(("parallel",)),
    )(c_all, w, b)


def _ada_bwd(c_all_t, dm, name):
    d, nb = c_all_t.shape
    nsub, _, cs = dm.shape

    def body(c_ref, dm_ref, o_ref):
        cv = c_ref[...]
        sc = cv * (1.0 / (1.0 + jnp.exp(-cv)))
        acc = sc[:, 0:1] * dm_ref[0:1, :]
        for bi in range(1, nb):
            acc = acc + sc[:, bi:bi + 1] * dm_ref[bi:bi + 1, :]
        o_ref[...] = acc

    return pl.pallas_call(
        body,
        grid=(nsub,),
        in_specs=[pl.BlockSpec(c_all_t.shape, lambda i: (0, 0)), pl.BlockSpec((None, nb, cs), lambda i: (i, 0, 0))],
        out_specs=pl.BlockSpec((None, d, cs), lambda i: (i, 0, 0)),
        out_shape=jax.ShapeDtypeStruct((nsub, d, cs), F32),
        name=name,
        compiler_params=_cparams(("parallel",)),
    )(c_all_t, dm)


def _row_tile(r, row_elems, block_elems=256 * 1024):
    t = 2 * SUBLANES
    if r % t:
        return r
    while t * 2 * row_elems <= block_elems and r % (t * 2) == 0:
        t *= 2
    return t


def _adamw(w, g, m, v, name):
    shape = w.shape
    c = shape[-1]
    r = w.size // c
    tr = _row_tile(r, c)
    w2, g2, m2, v2 = [a.reshape(r, c) for a in (w, g, m, v)]
    bc1 = 1.0 - ADAM_B1 ** ADAM_STEP
    bc2 = 1.0 - ADAM_B2 ** ADAM_STEP

    def body(w_ref, g_ref, m_ref, v_ref, d_ref, nm_ref, nv_ref):
        gv = g_ref[...]
        nm = ADAM_B1 * m_ref[...] + (1.0 - ADAM_B1) * gv
        nv = ADAM_B2 * v_ref[...] + (1.0 - ADAM_B2) * (gv * gv)
        d_ref[...] = -ADAM_LR * ((nm / bc1) / (jnp.sqrt(nv / bc2) + ADAM_EPS) + ADAM_WD * w_ref[...])
        nm_ref[...] = nm
        nv_ref[...] = nv

    res = _rows(body, r, tr, [("blk", a) for a in (w2, g2, m2, v2)], [("blk", (r, c), F32)] * 3, name)
    return [a.reshape(shape) for a in res]


def _sum_slots(buf, name):
    n, r, c = buf.shape
    tr = _row_tile(r, n * c, 2 * 1024 * 1024)

    def body(b_ref, o_ref):
        acc = b_ref[0].astype(F32)
        for k in range(1, n):
            acc = acc + b_ref[k].astype(F32)
        o_ref[...] = acc

    return pl.pallas_call(
        body,
        grid=(r // tr,),
        in_specs=[pl.BlockSpec((n, tr, c), lambda i: (0, i, 0))],
        out_specs=pl.BlockSpec((tr, c), lambda i: (i, 0)),
        out_shape=jax.ShapeDtypeStruct((r, c), F32),
        name=name,
        compiler_params=_cparams(("parallel",)),
    )(buf)


def _me():
    return lax.axis_index("x"), lax.axis_index("y"), lax.axis_index("c")


def _all_gather_small(blk, name, after=()):
    m_per, n = blk.shape

    def body(x_ref, *rest):
        out_ref, send_sems, recv_sems, local_sem = rest[len(after):]
        x, y, c = _me()
        me, sibling = (x, y, c), (x, y, 1 - c)
        chips = [(1 - x, y), (x, 1 - y), (1 - x, 1 - y)]

        def rows(px, py, pc):
            return out_ref.at[pl.ds((4 * px + 2 * py + pc) * m_per, m_per), :]

        def copy(k, block, to, src=None):
            return pltpu.make_async_remote_copy(
                src_ref=rows(*block) if src is None else src, dst_ref=rows(*block),
                send_sem=send_sems.at[k], recv_sem=recv_sems.at[k], device_id=to, device_id_type=MESH)

        mine = pltpu.make_async_copy(x_ref, rows(*me), local_sem)
        mine.start()
        first = [copy(0, me, sibling, src=x_ref)]
        first += [copy(1 + j, me, (*chip, c), src=x_ref) for j, chip in enumerate(chips)]
        for cp in first:
            cp.start()
        passed = [copy(4 + j, (*chip, c), sibling) for j, chip in enumerate(chips)]
        for j, chip in enumerate(chips):
            copy(1 + j, (*chip, c), me).wait_recv()
            passed[j].start()
        copy(0, sibling, me).wait_recv()
        for j, chip in enumerate(chips):
            copy(4 + j, (*chip, 1 - c), me).wait_recv()
        for cp in first + passed:
            cp.wait_send()
        mine.wait()

    return pl.pallas_call(
        body,
        out_shape=jax.ShapeDtypeStruct((N_DEV * m_per, n), blk.dtype),
        in_specs=[pl.BlockSpec(memory_space=pltpu.VMEM)] + [pl.BlockSpec(memory_space=pl.ANY)] * len(after),
        out_specs=pl.BlockSpec(memory_space=pltpu.VMEM),
        scratch_shapes=[pltpu.SemaphoreType.DMA((7,)), pltpu.SemaphoreType.DMA((7,)), pltpu.SemaphoreType.DMA],
        name=name,
        compiler_params=pltpu.CompilerParams(vmem_limit_bytes=VMEM_LIMIT),
    )(blk, *after)


_HBM = pl.BlockSpec(memory_space=pltpu.HBM)
_SEM = pl.BlockSpec(memory_space=pltpu.SEMAPHORE)
_EFFECT = pltpu.SideEffectType.DATAFLOW_SIDE_EFFECTING


def _other_chips(x, y):
    return [(1 - x, y), (x, 1 - y), (1 - x, 1 - y)]


def _gather_copy(w, j, src_ref, land_ref, send_sems, recv_sems, halved=False):
    x, y, c = _me()
    if halved:
        half = src_ref.shape[0] // 2
        src_ref = src_ref.at[pl.ds(c * half, half), :]
    return pltpu.make_async_remote_copy(
        src_ref=src_ref, dst_ref=land_ref.at[2 * x + y], send_sem=send_sems.at[3 * w + j], recv_sem=recv_sems.at[3 * w + j],
        device_id=(*_other_chips(x, y)[j], c), device_id_type=MESH)


def _gather_start(shards, halved, after, name):
    n = len(shards)
    lands = [lax.empty((N_CHIPS, s.shape[0] // 2 if w in halved else s.shape[0], s.shape[1]), s.dtype) for w, s in enumerate(shards)]

    def body(*refs):
        in_refs, land_refs = refs[:n], refs[n:2 * n]
        send_sems, recv_sems = refs[2 * n + 1], refs[2 * n + 2]
        token = refs[-1]
        for w in range(n):
            for j in range(3):
                _gather_copy(w, j, in_refs[w], land_refs[w], send_sems, recv_sems, w in halved).start()
        token[...] = jnp.zeros_like(token)

    res = pl.pallas_call(
        body,
        out_shape=(pltpu.SemaphoreType.DMA((3 * n,)), pltpu.SemaphoreType.DMA((3 * n,)),
                   *[pltpu.HBM(s.shape, s.dtype) for s in shards], *[pltpu.HBM(l.shape, l.dtype) for l in lands],
                   jax.ShapeDtypeStruct((SUBLANES, LANES), F32)),
        in_specs=[_HBM] * (2 * n) + [pl.BlockSpec(memory_space=pl.ANY)],
        out_specs=(_SEM, _SEM, *[_HBM] * (2 * n), pl.BlockSpec(memory_space=pltpu.VMEM)),
        input_output_aliases={i: 2 + i for i in range(2 * n)},
        name=name,
        compiler_params=pltpu.CompilerParams(has_side_effects=_EFFECT),
    )(*[pltpu.with_memory_space_constraint(a, pltpu.HBM) for a in list(shards) + lands], after)
    return res[0], res[1], res[2:2 + n], res[2 + n:2 + 2 * n], res[-1]


def _gather_wait(w, shard, land, send_sems, recv_sems, after, name, halved=False):
    def body(s_ref, land_ref, send_sems, recv_sems, after_ref, s_out, land_out, stage):
        x, y, _ = _me()
        if not halved:
            pltpu.sync_copy(s_ref, stage)
            pltpu.sync_copy(stage, land_out.at[2 * x + y])
        for j in range(3):
            cp = _gather_copy(w, j, s_ref, land_ref, send_sems, recv_sems, halved)
            cp.wait_send()
            cp.wait_recv()

    return pl.pallas_call(
        body,
        out_shape=(pltpu.HBM(shard.shape, shard.dtype), pltpu.HBM(land.shape, land.dtype)),
        in_specs=(_HBM, _HBM, _SEM, _SEM, pl.BlockSpec(memory_space=pl.ANY)),
        out_specs=(_HBM, _HBM),
        input_output_aliases={0: 0, 1: 1},
        scratch_shapes=[pltpu.VMEM((SUBLANES, LANES) if halved else shard.shape, shard.dtype)],
        name=name,
        compiler_params=pltpu.CompilerParams(has_side_effects=_EFFECT, vmem_limit_bytes=VMEM_LIMIT),
    )(shard, land, send_sems, recv_sems, after)


def _assemble_halves(shard, land, name):
    half = land.shape[1]

    def body(s_ref, land_ref, out_ref, send_sems, recv_sems, local_sems):
        x, y, c = _me()
        own = pltpu.make_async_copy(s_ref, out_ref.at[2 * x + y], local_sems.at[3])
        own.start()
        cps = []
        for j, (ox, oy) in enumerate(_other_chips(x, y)):
            qj = 2 * ox + oy
            mine = out_ref.at[qj, pl.ds(c * half, half), :]
            lc = pltpu.make_async_copy(land_ref.at[qj], mine, local_sems.at[j])
            lc.start()
            rc = pltpu.make_async_remote_copy(
                src_ref=land_ref.at[qj], dst_ref=mine, send_sem=send_sems.at[j], recv_sem=recv_sems.at[j],
                device_id=(x, y, 1 - c), device_id_type=MESH)
            rc.start()
            cps.append((lc, rc))
        for lc, rc in cps:
            rc.wait_recv()
        for lc, rc in cps:
            rc.wait_send()
            lc.wait()
        own.wait()

    vmem = pl.BlockSpec(memory_space=pltpu.VMEM)
    return pl.pallas_call(
        body,
        out_shape=jax.ShapeDtypeStruct((N_CHIPS,) + shard.shape, shard.dtype),
        in_specs=[vmem, vmem],
        out_specs=vmem,
        scratch_shapes=[pltpu.SemaphoreType.DMA((3,)), pltpu.SemaphoreType.DMA((3,)), pltpu.SemaphoreType.DMA((4,))],
        name=name,
        compiler_params=pltpu.CompilerParams(vmem_limit_bytes=VMEM_LIMIT),
    )(shard, land)


def _piece_shape(shape, kind):
    k, nn = shape
    return (k // 2, nn // N_CHIPS) if kind == "col" else (k // N_CHIPS // 2, nn)


def _piece_of(g_ref, kind, tq, tc):
    pr, pc = _piece_shape(g_ref.shape, kind)
    if kind == "col":
        return g_ref.at[pl.ds(tc * pr, pr), pl.ds(tq * pc, pc)]
    return g_ref.at[pl.ds((2 * tq + tc) * pr, pr), :]


def _scatter_copy(w, r, kind, g_ref, land_ref, send_sems, recv_sems):
    x, y, c = _me()
    tx, ty, tc = (x + ((r >> 2) & 1)) % 2, (y + ((r >> 1) & 1)) % 2, (c + (r & 1)) % 2
    return pltpu.make_async_remote_copy(
        src_ref=_piece_of(g_ref, kind, 2 * tx + ty, tc), dst_ref=land_ref.at[4 * x + 2 * y + c],
        send_sem=send_sems.at[N_DEV * w + r], recv_sem=recv_sems.at[N_DEV * w + r], device_id=(tx, ty, tc), device_id_type=MESH)


def _scatter_start(gs, kinds, name):
    n = len(gs)
    pieces = [_piece_shape(g.shape, kind) for g, kind in zip(gs, kinds)]
    lands = [lax.empty((N_DEV,) + p, g.dtype) for p, g in zip(pieces, gs)]

    def body(*refs):
        g_refs, land_refs, send_sems, recv_sems = refs[:n], refs[n:2 * n], refs[2 * n], refs[2 * n + 1]
        land_outs, stages = refs[3 * n + 2:4 * n + 2], refs[4 * n + 2:]
        x, y, c = _me()
        for w in range(n):
            for r in range(1, N_DEV):
                _scatter_copy(w, r, kinds[w], g_refs[w], land_refs[w], send_sems, recv_sems).start()
        for w in range(n):
            pltpu.sync_copy(_piece_of(g_refs[w], kinds[w], 2 * x + y, c), stages[w])
            pltpu.sync_copy(stages[w], land_outs[w].at[4 * x + 2 * y + c])

    arrays = list(gs) + lands
    res = pl.pallas_call(
        body,
        out_shape=(pltpu.SemaphoreType.DMA((N_DEV * n,)), pltpu.SemaphoreType.DMA((N_DEV * n,)),
                   *[pltpu.HBM(a.shape, a.dtype) for a in arrays]),
        in_specs=[_HBM] * (2 * n),
        out_specs=(_SEM, _SEM, *[_HBM] * (2 * n)),
        input_output_aliases={i: 2 + i for i in range(2 * n)},
        scratch_shapes=[pltpu.VMEM(p, g.dtype) for p, g in zip(pieces, gs)],
        name=name,
        compiler_params=pltpu.CompilerParams(has_side_effects=_EFFECT, vmem_limit_bytes=VMEM_LIMIT),
    )(*[pltpu.with_memory_space_constraint(a, pltpu.HBM) for a in arrays])
    return res[0], res[1], res[2:2 + n], res[2 + n:]


def _scatter_wait(send_sems, recv_sems, gs, lands, kinds, after, name):
    n = len(gs)

    def body(*refs):
        g_refs, land_refs, send_sems, recv_sems = refs[:n], refs[n:2 * n], refs[2 * n], refs[2 * n + 1]
        for w in range(n):
            for r in range(1, N_DEV):
                cp = _scatter_copy(w, r, kinds[w], g_refs[w], land_refs[w], send_sems, recv_sems)
                cp.wait_send()
                cp.wait_recv()

    arrays = list(gs) + list(lands)
    return pl.pallas_call(
        body,
        out_shape=tuple(pltpu.HBM(a.shape, a.dtype) for a in arrays),
        in_specs=(*[_HBM] * (2 * n), _SEM, _SEM, pl.BlockSpec(memory_space=pl.ANY)),
        out_specs=tuple([_HBM] * (2 * n)),
        input_output_aliases={i: i for i in range(2 * n)},
        name=name,
        compiler_params=pltpu.CompilerParams(has_side_effects=_EFFECT),
    )(*arrays, send_sems, recv_sems, after)[n:]


def _swap_halves(halves, name):
    n = len(halves)

    def body(*refs):
        in_refs, out_refs = refs[:n], refs[n:2 * n]
        send_sems, recv_sems, local_sems = refs[2 * n:]
        x, y, c = _me()
        cps = []
        for w in range(n):
            lc = pltpu.make_async_copy(in_refs[w], out_refs[w].at[c], local_sems.at[w])
            lc.start()
            rc = pltpu.make_async_remote_copy(
                src_ref=in_refs[w], dst_ref=out_refs[w].at[c], send_sem=send_sems.at[w], recv_sem=recv_sems.at[w],
                device_id=(x, y, 1 - c), device_id_type=MESH)
            rc.start()
            cps.append((lc, rc))
        for lc, rc in cps:
            rc.wait_recv()
        for lc, rc in cps:
            rc.wait_send()
            lc.wait()

    vmem = pl.BlockSpec(memory_space=pltpu.VMEM)
    return pl.pallas_call(
        body,
        out_shape=[jax.ShapeDtypeStruct((2,) + h.shape, h.dtype) for h in halves],
        in_specs=[vmem] * n,
        out_specs=[vmem] * n,
        scratch_shapes=[pltpu.SemaphoreType.DMA((n,)), pltpu.SemaphoreType.DMA((n,)), pltpu.SemaphoreType.DMA((n,))],
        name=name,
        compiler_params=pltpu.CompilerParams(vmem_limit_bytes=VMEM_LIMIT),
    )(*halves)


def _to_streams(a, dil):
    if dil == 1:
        return a
    s, c = a.shape
    return a.reshape(s // dil, dil, c).transpose(1, 0, 2).reshape(s, c)


def _from_streams(a, dil):
    if dil == 1:
        return a
    s, c = a.shape
    return a.reshape(dil, s // dil, c).transpose(1, 0, 2).reshape(s, c)


def _mm_tiles(s):
    return min(s, 1024)


def _local_step(x0, target, mvec, ln_g, ln_b, small, fetch, emit, start):
    s, d = x0.shape
    tm = _mm_tiles(s)
    row = lambda v: v.reshape(1, -1)
    shift = [row(mvec[i, :d]) for i in range(4)]
    scale = [row(mvec[i, d:2 * d]) for i in range(4)]
    gate = [row(1.0 + mvec[i, 2 * d:]) for i in range(4)]
    lg = [row(ln_g[i]) for i in range(4)]
    lb = [row(ln_b[i]) for i in range(4)]
    mm = functools.partial(_mm, tm=tm)
    mm_w = functools.partial(_mm, tm=1024, tk=min(s, 2048), mode="tn")

    xs, ys, big = [x0], [], {}
    h0 = _mod(x0, scale[0], shift[0], start, "mod0")
    big["a_w_in"] = fetch("a_w_in", h0)
    uvpre = mm(h0, big["a_w_in"], mode="nn", name="a_in", outs=[F32], tn=512, tk=1024,
               epi=lambda r, bias: [r + bias], extras=[("row", small["a_b_in"])])
    gated = _spatial_fwd(uvpre, small["a_vn_g"], small["a_vn_b"], small["wc"], small["bias_full"], "a_spatial")
    big["a_w_out"] = fetch("a_w_out", gated)
    ys.append(mm(gated, big["a_w_out"], mode="nn", name="a_out", outs=[F32], tn=1024, tk=1024))
    x1, h1 = _resid_ln(xs[0], ys[0], gate[0], lg[0], lb[0], (scale[1], shift[1]), "ln0")
    xs.append(x1)
    relu2 = lambda r: [jnp.square(jnp.maximum(r, 0.0))]
    big["up0"] = fetch("up0", h1)
    r0 = mm(h1, big["up0"], mode="nn", name="up0", outs=[MXU_DTYPE], tn=1024, tk=1024, epi=relu2)
    big["down0"] = fetch("down0", r0)
    ys.append(mm(r0, big["down0"], mode="nn", name="down0", outs=[F32], tn=1024, tk=2048))
    x2, h2 = _resid_ln(xs[1], ys[1], gate[1], lg[1], lb[1], (scale[2], shift[2]), "ln1")
    xs.append(x2)
    hg, qkvs, o_g, l_g, l_streams = [], [], [], [], []
    big["b_w_qkv"] = fetch("b_w_qkv", h2)
    for g, (_, dil) in enumerate(B_PATTERNS):
        hp = _to_streams(h2, dil)
        qkv = mm(hp, big["b_w_qkv"], mode="nn", name=f"qkv{g}", outs=[MXU_DTYPE], tn=768, tk=1024, b_col0=g * 3 * d, n_out=3 * d)
        og, lgv = _attn_fwd(qkv, small["slopes"], dil, f"attn_fwd{g}")
        hg.append(hp)
        qkvs.append(qkv)
        o_g.append(_from_streams(og, dil))
        l_g.append(_from_streams(lgv, dil))
        l_streams.append(lgv)
    o_mix = _combine_fwd(o_g, l_g, "combine")
    big["b_w_out"] = fetch("b_w_out", o_mix)
    ys.append(mm(o_mix, big["b_w_out"], mode="nn", name="b_out", outs=[F32], tn=1024, tk=1024))
    x3, h3 = _resid_ln(xs[2], ys[2], gate[2], lg[2], lb[2], (scale[3], shift[3]), "ln2")
    xs.append(x3)
    big["up1"] = fetch("up1", h3)
    r1 = mm(h3, big["up1"], mode="nn", name="up1", outs=[MXU_DTYPE], tn=1024, tk=1024, epi=relu2)
    big["down1"] = fetch("down1", r1)
    ys.append(mm(r1, big["down1"], mode="nn", name="down1", outs=[F32], tn=1024, tk=2048))

    gb, red_ln, red_mod = {}, [None] * 4, [None] * 4

    def mlp_bwd(i, h, r, dyy):
        gb[f"down{i}"] = mm_w(r, dyy, name=f"g_down{i}", outs=[MXU_DTYPE], tn=1024)
        da = mm(dyy, big[f"down{i}"], mode="nt", name=f"d_down{i}", outs=[MXU_DTYPE], tn=1024, tk=1024,
                after=emit(f"down{i}", gb[f"down{i}"]),
                epi=lambda acc, rv: [acc * (2.0 * jnp.sqrt(rv.astype(F32)))], extras=[("full", r)])
        gb[f"up{i}"] = mm_w(h, da, name=f"g_up{i}", outs=[MXU_DTYPE], tn=1024)
        return [mm(da, big[f"up{i}"], mode="nt", name=f"d_up{i}", outs=[F32], tn=1024, tk=1024, after=emit(f"up{i}", gb[f"up{i}"]))]

    def join(sub, dxr, dhs, after=None):
        res = _mod_ln_bwd(dxr, dhs, xs[sub], scale[sub], xs[sub - 1], ys[sub - 1], gate[sub - 1], lg[sub - 1],
                          f"mod_ln_bwd{sub}", after=after)
        red_mod[sub], red_ln[sub - 1] = res[2], res[3]
        return res[0], res[1]

    loss, dxr, dyy, red_ln[3] = _last_ln_loss_bwd(xs[3], ys[3], gate[3], lg[3], lb[3], target, "ln3_loss_bwd")
    dxr, dyy = join(3, dxr, mlp_bwd(1, h3, r1, dyy))
    gb["b_w_out"] = mm_w(o_mix, dyy, name="g_b_out", outs=[MXU_DTYPE], tn=1024, tk=1024)
    do = mm(dyy, big["b_w_out"], mode="nt", name="d_b_out", outs=[F32], tn=1024, tk=1024, after=emit("b_w_out", gb["b_w_out"]))
    parts = _combine_bwd(do, o_mix, l_g, "combine_bwd")
    dhs, gq = [], None
    for g, (_, dil) in enumerate(B_PATTERNS):
        do_g, dd_g = _to_streams(parts[g][0], dil), _to_streams(parts[g][1], dil)
        dqkv = _attn_bwd(qkvs[g], do_g, l_streams[g], dd_g, small["slopes"], dil, f"attn_bwd{g}")
        gq = mm_w(hg[g], dqkv, name=f"g_qkv{g}", outs=[MXU_DTYPE], tn=1024, out_col0=g * 3 * d, out_cols=len(B_PATTERNS) * 3 * d, into=gq)
        dh = mm(dqkv, big["b_w_qkv"], mode="nt", name=f"d_qkv{g}", outs=[F32], tn=1024, tk=768, b_col0=g * 3 * d)
        dhs.append(_from_streams(dh, dil))
    gb["b_w_qkv"] = gq
    dxr, dyy = join(2, dxr, dhs, after=emit("b_w_qkv", gb["b_w_qkv"]))
    dxr, dyy = join(1, dxr, mlp_bwd(0, h1, r0, dyy))
    gb["a_w_out"] = mm_w(gated, dyy, name="g_a_out", outs=[MXU_DTYPE], tn=1024)
    dgated = mm(dyy, big["a_w_out"], mode="nt", name="d_a_out", outs=[F32], tn=1024, tk=1024, after=emit("a_w_out", gb["a_w_out"]))
    duv, dws, dbias, dbin, dvg, dvb = _spatial_bwd(uvpre, dgated, small["a_vn_g"], small["a_vn_b"], small["wc"],
                                                   small["wct"], small["bias_full"], "a_spatial_bwd")
    gb["a_w_in"] = mm_w(h0, duv, name="g_a_in", outs=[MXU_DTYPE], tn=1024)
    dh = mm(duv, big["a_w_in"], mode="nt", name="d_a_in", outs=[F32], tn=1024, tk=512, after=emit("a_w_in", gb["a_w_in"]))
    dx, red_mod[0] = _mod_bwd(dxr, [dh], xs[0], scale[0], "mod_bwd0")
    dm = [jnp.concatenate([red_mod[i][0], red_mod[i][1], red_ln[i][2]]) for i in range(4)]
    dlg, dlb = [red_ln[i][0] for i in range(4)], [red_ln[i][1] for i in range(4)]

    tril = jnp.tril(jnp.ones((CHUNK, CHUNK), bool))
    gsmall = {
        "a_b_in": dbin.reshape(-1), "a_vn_g": dvg.reshape(-1), "a_vn_b": dvb.reshape(-1),
        "a_w_s": jnp.where(tril, dws, 0.0).reshape(-1),
        "a_b_s": dbias.reshape(CHUNK, A_GROUPS, d // A_GROUPS).sum(-1).T.reshape(-1),
    }
    return loss, dx, gb, jnp.stack(dm), jnp.stack(dlg), jnp.stack(dlb), gsmall


BIG = ("a_w_in", "a_w_out", "up0", "down0", "b_w_qkv", "b_w_out", "up1", "down1")
BIG_KIND = {"a_w_in": "col", "a_w_out": "row", "b_w_qkv": "col", "b_w_out": "row",
            "up0": "col", "up1": "col", "down0": "row", "down1": "row"}
HALVED = ("a_w_in", "down0", "b_w_qkv")
SCATTER_GROUPS = (("down1", "up1"), ("b_w_out", "b_w_qkv"), ("down0", "up0"), ("a_w_out", "a_w_in"))
SMALL = ("a_b_in", "a_vn_g", "a_vn_b", "a_b_s", "a_w_s")


def kernel(x, c, ada_w, ada_b, ln_g, ln_b, a_w_in, a_b_in, a_vn_g, a_vn_b, a_w_s, a_b_s, a_w_out, b_w_qkv, b_w_out, mlp_w_up, mlp_w_down, loss_target, m_ada_w, m_ada_b, m_ln_g, m_ln_b, m_a_w_in, m_a_b_in, m_a_vn_g, m_a_vn_b, m_a_w_s, m_a_b_s, m_a_w_out, m_b_w_qkv, m_b_w_out, m_mlp_w_up, m_mlp_w_down, v_ada_w, v_ada_b, v_ln_g, v_ln_b, v_a_w_in, v_a_b_in, v_a_vn_g, v_a_vn_b, v_a_w_s, v_a_b_s, v_a_w_out, v_b_w_qkv, v_b_w_out, v_mlp_w_up, v_mlp_w_down):
    s, d = x.shape[1], x.shape[2]
    xi, yi, ci = _me()
    q = 2 * xi + yi
    dev = 2 * q + ci
    nsub = 2 * DEPTH
    cs = ada_w.shape[-1]
    ls = ln_g.shape[-1]

    shards = {
        "a_w_in": a_w_in[0], "a_w_out": a_w_out[0], "b_w_qkv": b_w_qkv[0], "b_w_out": b_w_out[0],
        "up0": mlp_w_up[0], "up1": mlp_w_up[1], "down0": mlp_w_down[0], "down1": mlp_w_down[1],
    }
    cast = [shards[k].astype(MXU_DTYPE) for k in BIG]

    pack = jnp.concatenate([c.reshape(-1), ln_g.reshape(-1), ln_b.reshape(-1)]).reshape(-1, LANES)
    got = _all_gather_small(pack, "gather_small", after=cast).reshape(N_DEV, -1)
    c_all = got[:, :d]
    per_chip = got[0::2]
    ln_g_full = per_chip[:, d:d + nsub * ls].reshape(N_CHIPS, nsub, ls).transpose(1, 0, 2).reshape(nsub, d)
    ln_b_full = per_chip[:, d + nsub * ls:].reshape(N_CHIPS, nsub, ls).transpose(1, 0, 2).reshape(nsub, d)
    m_part = _ada_fwd(c_all, ada_w.reshape(nsub, d, cs), ada_b.reshape(nsub, 1, cs), "ada_fwd")
    m_all = _all_gather_small(m_part.reshape(-1, LANES), "gather_mod").reshape(N_DEV, nsub, N_DEV, cs)
    m_mine = lax.dynamic_index_in_dim(m_all[0::2], dev, axis=2, keepdims=False)
    mvec = m_mine.transpose(1, 0, 2).reshape(nsub, 3 * d)

    halved = {BIG.index(k) for k in HALVED}
    send_sems, recv_sems, shard_thru, lands, token = _gather_start(cast, halved, mvec, "gather_start")

    def fetch(k, after):
        w = BIG.index(k)
        shard, gw = _gather_wait(w, shard_thru[w], lands[w], send_sems, recv_sems, after, f"gather_wait_{k}", w in halved)
        if w in halved:
            gw = _assemble_halves(shard, gw, f"assemble_{k}")
        return gw if BIG_KIND[k] == "col" else gw.reshape(1, -1, gw.shape[-1])

    scattering, pending = {}, {}

    def emit(k, g):
        pending[k] = g
        group = next(gr for gr in SCATTER_GROUPS if k in gr)
        if k != group[-1]:
            return None
        scattering[group] = _scatter_start([pending[m] for m in group], [BIG_KIND[m] for m in group], f"scatter_start_{k}")
        return scattering[group][2][0]

    tril = jnp.tril(jnp.ones((CHUNK, CHUNK), bool))
    wc = jnp.where(tril, a_w_s[0], 0.0).astype(MXU_DTYPE)
    heads = jnp.arange(1, B_HEADS + 1, dtype=F32)
    small = {
        "a_b_in": a_b_in, "a_vn_g": a_vn_g, "a_vn_b": a_vn_b,
        "wc": wc, "wct": wc.transpose(0, 2, 1),
        "bias_full": jnp.repeat(a_b_s[0].T, d // A_GROUPS, axis=1),
        "slopes": jnp.exp2(-8.0 * heads / B_HEADS),
    }

    loss_part, grad_x, gb, dm, dlg, dlb, gsmall = _local_step(x[0], loss_target[0], mvec, ln_g_full, ln_b_full, small, fetch, emit, token)
    loss = lax.psum(loss_part, ("x", "y", "c"))

    weights = dict(ada_w=ada_w, ada_b=ada_b, ln_g=ln_g, ln_b=ln_b, a_w_in=a_w_in, a_b_in=a_b_in, a_vn_g=a_vn_g, a_vn_b=a_vn_b,
                   a_w_s=a_w_s, a_b_s=a_b_s, a_w_out=a_w_out, b_w_qkv=b_w_qkv, b_w_out=b_w_out, mlp_w_up=mlp_w_up, mlp_w_down=mlp_w_down)
    ms = dict(ada_w=m_ada_w, ada_b=m_ada_b, ln_g=m_ln_g, ln_b=m_ln_b, a_w_in=m_a_w_in, a_b_in=m_a_b_in, a_vn_g=m_a_vn_g, a_vn_b=m_a_vn_b,
              a_w_s=m_a_w_s, a_b_s=m_a_b_s, a_w_out=m_a_w_out, b_w_qkv=m_b_w_qkv, b_w_out=m_b_w_out, mlp_w_up=m_mlp_w_up, mlp_w_down=m_mlp_w_down)
    vs = dict(ada_w=v_ada_w, ada_b=v_ada_b, ln_g=v_ln_g, ln_b=v_ln_b, a_w_in=v_a_w_in, a_b_in=v_a_b_in, a_vn_g=v_a_vn_g, a_vn_b=v_a_vn_b,
              a_w_s=v_a_w_s, a_b_s=v_a_b_s, a_w_out=v_a_w_out, b_w_qkv=v_b_w_qkv, b_w_out=v_b_w_out, mlp_w_up=v_mlp_w_up, mlp_w_down=v_mlp_w_down)
    grads, updates = {}, {}

    def update(k):
        updates[k] = _adamw(weights[k], grads[k], ms[k], vs[k], f"adamw_{k}")
        return updates[k][0]

    pack_b = jnp.concatenate([dm.reshape(-1), dlg.reshape(-1), dlb.reshape(-1)] + [gsmall[k] for k in SMALL])
    n_small = pack_b.shape[0]
    pack_b = jnp.pad(pack_b, (0, -n_small % (256 * LANES)))
    got_b = _all_gather_small(pack_b.reshape(-1, LANES), "gather_small_grads").reshape(N_DEV, -1, LANES)
    tot = _sum_slots(got_b, "sum_small").reshape(-1)
    o = 0
    dm_tot = tot[o:o + nsub * 3 * d].reshape(nsub, 3 * d); o += nsub * 3 * d
    dlg_tot = tot[o:o + nsub * d].reshape(nsub, d); o += nsub * d
    dlb_tot = tot[o:o + nsub * d].reshape(nsub, d); o += nsub * d
    g_small = {}
    for k, ref in zip(SMALL, (a_b_in, a_vn_g, a_vn_b, a_b_s, a_w_s)):
        g_small[k] = tot[o:o + ref.size].reshape(ref.shape); o += ref.size
    assert o == n_small
    dm_all = got_b.reshape(N_DEV, -1)[:, :nsub * 3 * d].reshape(N_DEV, nsub, 3 * d)
    dm_cols = lax.dynamic_slice_in_dim(dm_all, q * cs, cs, axis=2).transpose(1, 0, 2)

    grads.update({
        "ada_w": _ada_bwd(c_all.T, dm_cols, "ada_bwd").reshape(ada_w.shape),
        "ada_b": lax.dynamic_slice_in_dim(dm_tot, q * cs, cs, axis=1).reshape(ada_b.shape),
        "ln_g": lax.dynamic_slice_in_dim(dlg_tot, q * ls, ls, axis=1).reshape(ln_g.shape),
        "ln_b": lax.dynamic_slice_in_dim(dlb_tot, q * ls, ls, axis=1).reshape(ln_b.shape),
        **g_small,
    })
    for k in ("ada_b", "ln_g", "ln_b") + SMALL:
        update(k)
    done = update("ada_w")

    gfull = {}
    for group in (SCATTER_GROUPS[0] + SCATTER_GROUPS[1], SCATTER_GROUPS[2] + SCATTER_GROUPS[3]):
        bufs = []
        for pair in (group[:2], group[2:]):
            bufs += _scatter_wait(*scattering[pair], [BIG_KIND[m] for m in pair], done, f"scatter_wait_{pair[-1]}")
        halves = [_sum_slots(b, f"sum_{k}") for k, b in zip(group, bufs)]
        fulls = _swap_halves(halves, f"swap_halves_{group[0]}")
        gfull.update({k: f.reshape(-1, f.shape[-1]) for k, f in zip(group, fulls)})
        if group[0] == "down1":
            grads["b_w_qkv"], grads["b_w_out"] = gfull["b_w_qkv"][None], gfull["b_w_out"][None]
            update("b_w_out")
            done = update("b_w_qkv")
    grads.update({
        "a_w_in": gfull["a_w_in"][None], "a_w_out": gfull["a_w_out"][None],
        "mlp_w_up": jnp.stack([gfull["up0"], gfull["up1"]]), "mlp_w_down": jnp.stack([gfull["down0"], gfull["down1"]]),
    })
    for k in ("a_w_in", "a_w_out", "mlp_w_up", "mlp_w_down"):
        update(k)
    names = list(weights)
    return (loss, grad_x[None], *[grads[k] for k in names], *[updates[k][0] for k in names],
            *[updates[k][1] for k in names], *[updates[k][2] for k in names])
```

```python
import functools
import math

import jax
import jax.numpy as jnp
from jax import lax
from jax.experimental import pallas as pl
from jax.experimental.pallas import tpu as pltpu

F32 = jnp.float32
MXU_DTYPE = jnp.bfloat16

DEPTH = 2
CHUNK = 128
A_GROUPS = 16
B_HEADS = 16
HEAD_DIM = 64
B_PATTERNS = ((128, 1), (512, 4), (2048, 16))
SPAN = 128
ALPHA = (2 * DEPTH) ** 0.25
LN_EPS = 1e-5
NEG = -1e30
ATT_SCALE = HEAD_DIM ** -0.5
ADAM_LR, ADAM_B1, ADAM_B2, ADAM_EPS, ADAM_WD, ADAM_STEP = 0.001, 0.9, 0.999, 1e-08, 0.01, 10

N_CHIPS = 4
N_DEV = 8
LANES = 128
SUBLANES = 8
VMEM_LIMIT = 52 * 1024 * 1024
ROW_TILE = 512
MM_ROW_CHUNK = 256
MESH = pl.DeviceIdType.MESH


def _cparams(sem):
    return pltpu.CompilerParams(dimension_semantics=sem, vmem_limit_bytes=VMEM_LIMIT)


def _fold8(v):
    r, c = v.shape
    return jnp.sum(v.reshape(r // SUBLANES, SUBLANES, c), axis=0)


def _gelu(x):
    c = math.sqrt(2.0 / math.pi)
    return 0.5 * x * (1.0 + jnp.tanh(c * (x + 0.044715 * (x * x * x))))


def _gelu_grad(x):
    c = math.sqrt(2.0 / math.pi)
    t = jnp.tanh(c * (x + 0.044715 * (x * x * x)))
    return 0.5 * (1.0 + t) + 0.5 * x * (1.0 - t * t) * c * (1.0 + 3.0 * 0.044715 * x * x)


def _dot(a, b, dims):
    return lax.dot_general(a.astype(MXU_DTYPE), b.astype(MXU_DTYPE), (dims, ((), ())), preferred_element_type=F32)


def _dot_nn(a, b):
    return _dot(a, b, ((1,), (0,)))


def _dot_nt(a, b):
    return _dot(a, b, ((1,), (1,)))


def _dot_tn(a, b):
    return _dot(a, b, ((0,), (0,)))


def _mm(a, b, *, mode, name, outs, tm, tn, tk, epi=None, extras=(), b_col0=0, n_out=None, after=None,
        out_col0=0, out_cols=None, into=None):
    if mode == "nn":
        m, kdim = a.shape
        p, kb, ns = b.shape
        assert kb == kdim and ns % tn == 0 and b_col0 % tn == 0
        n = n_out if n_out is not None else p * ns
        npt, j0 = ns // tn, b_col0 // tn
        a_spec = pl.BlockSpec((tm, tk), lambda i, j, k: (i, k))
        b_spec = pl.BlockSpec((None, tk, tn), lambda i, j, k: ((j + j0) // npt, k, (j + j0) % npt))
        dot = _dot_nn
    elif mode == "nt":
        m, kdim = a.shape
        p, n, ns = b.shape
        assert ns % tk == 0 and b_col0 % tk == 0
        npt, j0 = ns // tk, b_col0 // tk
        a_spec = pl.BlockSpec((tm, tk), lambda i, j, k: (i, k))
        b_spec = pl.BlockSpec((None, tn, tk), lambda i, j, k: ((k + j0) // npt, j, (k + j0) % npt))
        dot = _dot_nt
    else:
        kdim, m = a.shape
        kb, n = b.shape
        assert kb == kdim
        a_spec = pl.BlockSpec((tk, tm), lambda i, j, k: (k, i))
        b_spec = pl.BlockSpec((tk, tn), lambda i, j, k: (k, j))
        dot = _dot_tn
    assert m % tm == 0 and n % tn == 0 and kdim % tk == 0, (name, m, n, kdim, tm, tn, tk)
    nk = kdim // tk
    ex_specs, ex_arrays = [], []
    for kind, arr in extras:
        if kind == "row":
            ex_specs.append(pl.BlockSpec((1, tn), lambda i, j, k: (0, j)))
        else:
            ex_specs.append(pl.BlockSpec((tm, tn), lambda i, j, k: (i, j)))
        ex_arrays.append(arr)
    n_ex, n_o = len(ex_arrays), len(outs)
    deps = [d for d in (after, into) if d is not None]
    n_dep = len(deps)
    j_out = out_col0 // tn
    assert out_col0 % tn == 0 and (into is None or len(outs) == 1)

    def body(a_ref, b_ref, *rest):
        ex_refs, o_refs = rest[:n_ex], rest[n_ex + n_dep:n_ex + n_dep + n_o]
        k = pl.program_id(2)

        chunks = [slice(r0, r0 + min(tm, MM_ROW_CHUNK)) for r0 in range(0, tm, min(tm, MM_ROW_CHUNK))]

        def part(rows):
            return dot(a_ref[:, rows] if mode == "tn" else a_ref[rows, :], b_ref[...])

        def finish(r, rows):
            exs = [e[...] if kind == "row" else e[rows, :] for (kind, _), e in zip(extras, ex_refs)]
            vals = epi(r, *exs) if epi is not None else [r]
            for o, v in zip(o_refs, vals):
                o[rows, :] = v.astype(o.dtype)

        if nk == 1:
            for rows in chunks:
                finish(part(rows), rows)
            return
        acc = rest[n_ex + n_dep + n_o]

        @pl.when(k == 0)
        def _():
            for rows in chunks:
                acc[rows, :] = part(rows)

        @pl.when((k > 0) & (k < nk - 1))
        def _():
            for rows in chunks:
                acc[rows, :] += part(rows)

        @pl.when(k == nk - 1)
        def _():
            for rows in chunks:
                finish(acc[rows, :] + part(rows), rows)

    res = pl.pallas_call(
        body,
        grid=(m // tm, n // tn, nk),
        in_specs=[a_spec, b_spec] + ex_specs + [pl.BlockSpec(memory_space=pl.ANY)] * n_dep,
        out_specs=[pl.BlockSpec((tm, tn), lambda i, j, k: (i, j + j_out)) for _ in outs],
        out_shape=[jax.ShapeDtypeStruct((m, out_cols or n), dt) for dt in outs],
        input_output_aliases={} if into is None else {2 + n_ex + n_dep - 1: 0},
        scratch_shapes=[pltpu.VMEM((tm, tn), F32)] if nk > 1 else [],
        name=name,
        compiler_params=_cparams(("parallel", "parallel", "arbitrary")),
    )(a, b, *ex_arrays, *deps)
    return res if len(outs) > 1 else res[0]


def _rows(body, n_rows, tr, ins, outs, name, scratch=()):
    def spec(kind, shape):
        if kind == "blk":
            return pl.BlockSpec((tr,) + tuple(shape[1:]), lambda i: (i,) + (0,) * (len(shape) - 1))
        if kind == "dep":
            return pl.BlockSpec(memory_space=pl.ANY)
        return pl.BlockSpec(tuple(shape), lambda i: (0,) * len(shape))

    return pl.pallas_call(
        body,
        grid=(n_rows // tr,),
        in_specs=[spec(k, a.shape) for k, a in ins],
        out_specs=[spec(k, s) for k, s, _ in outs],
        out_shape=[jax.ShapeDtypeStruct(tuple(s), d) for _, s, d in outs],
        scratch_shapes=list(scratch),
        name=name,
        compiler_params=_cparams(("arbitrary",)),
    )(*[a for _, a in ins])


def _ln_stats(z):
    mu = jnp.mean(z, axis=-1, keepdims=True)
    zc = z - mu
    var = jnp.mean(zc * zc, axis=-1, keepdims=True)
    rstd = lax.rsqrt(var + LN_EPS)
    return zc * rstd, rstd


def _mod(x, scale, shift, after, name):
    s, d = x.shape

    def body(x_ref, sc_ref, sh_ref, dep_ref, h_ref):
        h_ref[...] = (x_ref[...] * (1.0 + sc_ref[...]) + sh_ref[...]).astype(h_ref.dtype)

    return _rows(body, s, ROW_TILE, [("blk", x), ("all", scale), ("all", shift), ("dep", after)], [("blk", (s, d), MXU_DTYPE)], name)[0]


def _resid_ln(x, y, gate, g, b, nxt, name):
    s, d = x.shape

    def body(x_ref, y_ref, gate_ref, g_ref, b_ref, sc_ref, sh_ref, xn_ref, h_ref):
        z = ALPHA * x_ref[...] + gate_ref[...] * y_ref[...]
        xhat, _ = _ln_stats(z)
        xn = xhat * g_ref[...] + b_ref[...]
        xn_ref[...] = xn
        h_ref[...] = (xn * (1.0 + sc_ref[...]) + sh_ref[...]).astype(h_ref.dtype)

    return _rows(body, s, ROW_TILE,
                 [("blk", x), ("blk", y), ("all", gate), ("all", g), ("all", b), ("all", nxt[0]), ("all", nxt[1])],
                 [("blk", (s, d), F32), ("blk", (s, d), MXU_DTYPE)], name)


def _mod_bwd(dxr, dhs, x, scale, name, after=None):
    s, d = x.shape
    n_dh = len(dhs)
    n_dep = 0 if after is None else 1

    def body(dxr_ref, *rest):
        dh_refs = rest[:n_dh]
        x_ref, sc_ref, dx_ref, red_ref, a_sh, a_sc = rest[n_dh:n_dh + 2] + rest[n_dh + 2 + n_dep:]
        i = pl.program_id(0)

        @pl.when(i == 0)
        def _():
            a_sh[...] = jnp.zeros_like(a_sh)
            a_sc[...] = jnp.zeros_like(a_sc)

        dh = dh_refs[0][...]
        for r in dh_refs[1:]:
            dh = dh + r[...]
        dx_ref[...] = dxr_ref[...] + dh * (1.0 + sc_ref[...])
        a_sh[...] += _fold8(dh)
        a_sc[...] += _fold8(dh * x_ref[...])

        @pl.when(i == pl.num_programs(0) - 1)
        def _():
            red_ref[...] = jnp.zeros_like(red_ref)
            red_ref[0:1, :] = jnp.sum(a_sh[...], axis=0, keepdims=True)
            red_ref[1:2, :] = jnp.sum(a_sc[...], axis=0, keepdims=True)

    return _rows(body, s, ROW_TILE, [("blk", dxr)] + [("blk", h) for h in dhs] + [("blk", x), ("all", scale)] + [("dep", after)] * n_dep,
                 [("blk", (s, d), F32), ("all", (SUBLANES, d), F32)], name,
                 scratch=[pltpu.VMEM((SUBLANES, d), F32)] * 2)


def _last_ln_loss_bwd(x, y, gate, g, b, target, name):
    s, d = x.shape

    def body(x_ref, y_ref, gate_ref, g_ref, b_ref, t_ref, l_ref, dxr_ref, dyy_ref, red_ref, a_l, a_g, a_b, a_gate):
        i = pl.program_id(0)

        @pl.when(i == 0)
        def _():
            for a in (a_l, a_g, a_b, a_gate):
                a[...] = jnp.zeros_like(a)

        yv = y_ref[...]
        z = ALPHA * x_ref[...] + gate_ref[...] * yv
        xhat, rstd = _ln_stats(z)
        e = xhat * g_ref[...] + b_ref[...] - t_ref[...]
        a_l[...] += _fold8(e * e)
        dxo_v = e * (1.0 / d)
        dxh = dxo_v * g_ref[...]
        dz = rstd * (dxh - jnp.mean(dxh, axis=-1, keepdims=True) - xhat * jnp.mean(dxh * xhat, axis=-1, keepdims=True))
        dxr_ref[...] = ALPHA * dz
        dyy_ref[...] = (gate_ref[...] * dz).astype(dyy_ref.dtype)
        a_g[...] += _fold8(dxo_v * xhat)
        a_b[...] += _fold8(dxo_v)
        a_gate[...] += _fold8(dz * yv)

        @pl.when(i == pl.num_programs(0) - 1)
        def _():
            l_ref[...] = jnp.full(l_ref.shape, 0.5 / d, F32) * jnp.sum(a_l[...])
            red_ref[...] = jnp.zeros_like(red_ref)
            red_ref[0:1, :] = jnp.sum(a_g[...], axis=0, keepdims=True)
            red_ref[1:2, :] = jnp.sum(a_b[...], axis=0, keepdims=True)
            red_ref[2:3, :] = jnp.sum(a_gate[...], axis=0, keepdims=True)

    l, dxr, dyy, red = _rows(
        body, s, ROW_TILE, [("blk", x), ("blk", y), ("all", gate), ("all", g), ("all", b), ("blk", target)],
        [("all", (SUBLANES, LANES), F32), ("blk", (s, d), F32), ("blk", (s, d), MXU_DTYPE), ("all", (SUBLANES, d), F32)], name,
        scratch=[pltpu.VMEM((SUBLANES, d), F32)] * 4)
    return l[0, 0], dxr, dyy, red


def _mod_ln_bwd(dxr, dhs, x, scale, x_in, y, gate, g, name, after=None):
    s, d = x.shape
    n_dh = len(dhs)
    n_dep = 0 if after is None else 1

    def body(dxr_ref, *rest):
        dh_refs = rest[:n_dh]
        x_ref, sc_ref, xin_ref, y_ref, gate_ref, g_ref = rest[n_dh:n_dh + 6]
        dxr_out, dyy_ref, red_mod, red_ln, a_sh, a_sc, a_g, a_b, a_gate = rest[n_dh + 6 + n_dep:]
        i = pl.program_id(0)

        @pl.when(i == 0)
        def _():
            for a in (a_sh, a_sc, a_g, a_b, a_gate):
                a[...] = jnp.zeros_like(a)

        dh = dh_refs[0][...]
        for r in dh_refs[1:]:
            dh = dh + r[...]
        xv = x_ref[...]
        dxo_v = dxr_ref[...] + dh * (1.0 + sc_ref[...])
        a_sh[...] += _fold8(dh)
        a_sc[...] += _fold8(dh * xv)
        yv = y_ref[...]
        z = ALPHA * xin_ref[...] + gate_ref[...] * yv
        xhat, rstd = _ln_stats(z)
        dxh = dxo_v * g_ref[...]
        dz = rstd * (dxh - jnp.mean(dxh, axis=-1, keepdims=True) - xhat * jnp.mean(dxh * xhat, axis=-1, keepdims=True))
        dxr_out[...] = ALPHA * dz
        dyy_ref[...] = (gate_ref[...] * dz).astype(dyy_ref.dtype)
        a_g[...] += _fold8(dxo_v * xhat)
        a_b[...] += _fold8(dxo_v)
        a_gate[...] += _fold8(dz * yv)

        @pl.when(i == pl.num_programs(0) - 1)
        def _():
            red_mod[...] = jnp.zeros_like(red_mod)
            red_mod[0:1, :] = jnp.sum(a_sh[...], axis=0, keepdims=True)
            red_mod[1:2, :] = jnp.sum(a_sc[...], axis=0, keepdims=True)
            red_ln[...] = jnp.zeros_like(red_ln)
            red_ln[0:1, :] = jnp.sum(a_g[...], axis=0, keepdims=True)
            red_ln[1:2, :] = jnp.sum(a_b[...], axis=0, keepdims=True)
            red_ln[2:3, :] = jnp.sum(a_gate[...], axis=0, keepdims=True)

    ins = ([("blk", dxr)] + [("blk", h) for h in dhs]
           + [("blk", x), ("all", scale), ("blk", x_in), ("blk", y), ("all", gate), ("all", g)] + [("dep", after)] * n_dep)
    return _rows(body, s, ROW_TILE, ins,
                 [("blk", (s, d), F32), ("blk", (s, d), MXU_DTYPE), ("all", (SUBLANES, d), F32), ("all", (SUBLANES, d), F32)], name,
                 scratch=[pltpu.VMEM((SUBLANES, d), F32)] * 5)


def _left_half(shape):
    return lax.broadcasted_iota(jnp.int32, shape, 1) < (LANES // 2)


CHUNKS_PER_STEP = 2


def _chunks_of_step():
    return [slice(i * CHUNK, (i + 1) * CHUNK) for i in range(CHUNKS_PER_STEP)]


def _spatial_z(vn, wc_ref, bias_ref, j):
    vb = vn[:, j * LANES:(j + 1) * LANES]
    z0 = _dot_nn(wc_ref[2 * j], vb)
    z1 = _dot_nn(wc_ref[2 * j + 1], vb)
    return jnp.where(_left_half(z0.shape), z0, z1) + bias_ref[:, j * LANES:(j + 1) * LANES]


def _spatial_fwd(uvpre, vn_g, vn_b, wc, bias_full, name):
    s, d2 = uvpre.shape
    d = d2 // 2

    def body(uv_ref, g_ref, b_ref, wc_ref, bias_ref, out_ref):
        for rows in _chunks_of_step():
            u = _gelu(uv_ref[rows, :d])
            v = _gelu(uv_ref[rows, d:])
            vh, _ = _ln_stats(v)
            vn = vh * g_ref[...] + b_ref[...]
            for j in range(d // LANES):
                z = _spatial_z(vn, wc_ref, bias_ref, j)
                out_ref[rows, j * LANES:(j + 1) * LANES] = (u[:, j * LANES:(j + 1) * LANES] * z).astype(out_ref.dtype)

    return _rows(body, s, CHUNKS_PER_STEP * CHUNK, [("blk", uvpre), ("all", vn_g), ("all", vn_b), ("all", wc), ("all", bias_full)],
                 [("blk", (s, d), MXU_DTYPE)], name)[0]


def _spatial_bwd(uvpre, dgated, vn_g, vn_b, wc, wct, bias_full, name):
    s, d2 = uvpre.shape
    d = d2 // 2

    def body(uv_ref, dg_ref, g_ref, b_ref, wc_ref, wct_ref, bias_ref,
             duv_ref, dws_ref, dbias_ref, dbin_ref, dvg_ref, dvb_ref, dvn_buf, a_bin, a_vg, a_vb):
        i = pl.program_id(0)

        @pl.when(i == 0)
        def _():
            dws_ref[...] = jnp.zeros_like(dws_ref)
            dbias_ref[...] = jnp.zeros_like(dbias_ref)
            a_bin[...] = jnp.zeros_like(a_bin)
            a_vg[...] = jnp.zeros_like(a_vg)
            a_vb[...] = jnp.zeros_like(a_vb)

        for rows in _chunks_of_step():
            up = uv_ref[rows, :d]
            vp = uv_ref[rows, d:]
            u = _gelu(up)
            v = _gelu(vp)
            vh, rstd = _ln_stats(v)
            vn = vh * g_ref[...] + b_ref[...]
            dg = dg_ref[rows, :]
            dzz = dg * u
            dbias_ref[...] += dzz
            for j in range(d // LANES):
                cols = slice(j * LANES, (j + 1) * LANES)
                z = _spatial_z(vn, wc_ref, bias_ref, j)
                dup = dg[:, cols] * z * _gelu_grad(up[:, cols])
                duv_ref[rows, cols] = dup.astype(duv_ref.dtype)
                a_bin[:, cols] += _fold8(dup)
                dzb = dzz[:, cols]
                left = _left_half(dzb.shape)
                dvn_buf[:, cols] = jnp.where(left, _dot_nn(wct_ref[2 * j], dzb), _dot_nn(wct_ref[2 * j + 1], dzb))
                vb = vn[:, cols]
                dws_ref[2 * j] += _dot_nt(jnp.where(left, dzb, 0.0), vb)
                dws_ref[2 * j + 1] += _dot_nt(jnp.where(left, 0.0, dzb), vb)
            dvn = dvn_buf[...]
            a_vg[...] += _fold8(dvn * vh)
            a_vb[...] += _fold8(dvn)
            dvh = dvn * g_ref[...]
            dv = rstd * (dvh - jnp.mean(dvh, axis=-1, keepdims=True) - vh * jnp.mean(dvh * vh, axis=-1, keepdims=True))
            dvp = dv * _gelu_grad(vp)
            duv_ref[rows, d:] = dvp.astype(duv_ref.dtype)
            a_bin[:, d:] += _fold8(dvp)

        @pl.when(i == pl.num_programs(0) - 1)
        def _():
            dbin_ref[...] = jnp.sum(a_bin[...], axis=0, keepdims=True)
            dvg_ref[...] = jnp.sum(a_vg[...], axis=0, keepdims=True)
            dvb_ref[...] = jnp.sum(a_vb[...], axis=0, keepdims=True)

    return _rows(body, s, CHUNKS_PER_STEP * CHUNK,
                 [("blk", uvpre), ("blk", dgated), ("all", vn_g), ("all", vn_b), ("all", wc), ("all", wct), ("all", bias_full)],
                 [("blk", (s, d2), MXU_DTYPE), ("all", (A_GROUPS, CHUNK, CHUNK), F32), ("all", (CHUNK, d), F32),
                  ("all", (1, d2), F32), ("all", (1, d), F32), ("all", (1, d), F32)], name,
                 scratch=[pltpu.VMEM((CHUNK, d), F32), pltpu.VMEM((SUBLANES, d2), F32),
                          pltpu.VMEM((SUBLANES, d), F32), pltpu.VMEM((SUBLANES, d), F32)])


def _head_mask(v, h):
    lane = lax.broadcasted_iota(jnp.int32, v.shape, 1)
    return jnp.where((lane >= h * HEAD_DIM) & (lane < (h + 1) * HEAD_DIM), v, jnp.zeros_like(v))


def _att_bias(slopes, dil):
    qi = lax.broadcasted_iota(jnp.int32, (SPAN, SPAN), 0)
    ki = lax.broadcasted_iota(jnp.int32, (SPAN, SPAN), 1)
    sl = slopes[:, None, None]
    cur = jnp.where(ki <= qi, -sl * (float(dil) * (qi - ki).astype(F32)), NEG)
    prev = jnp.where(ki >= qi, -sl * (float(dil) * (SPAN + qi - ki).astype(F32)), NEG)
    absent = jnp.full_like(prev, NEG)
    pairs = slopes.shape[0] // 2

    def fwd(pv):
        return jnp.concatenate([cur, pv], axis=2).reshape(pairs, 2 * SPAN, 2 * SPAN)

    def bwd(pv):
        return jnp.concatenate([cur.reshape(pairs, 2 * SPAN, SPAN), pv.reshape(pairs, 2 * SPAN, SPAN)], axis=1)

    return jnp.stack([fwd(absent), fwd(prev)]), jnp.stack([bwd(absent), bwd(prev)])


def _att_specs(s, d, dil, kinds):
    nb = s // (dil * SPAN)

    def rowblk(which, b):
        if which == "prev":
            return jnp.where(b % nb == 0, b, b - 1)
        if which == "next":
            return jnp.where(b % nb == nb - 1, b, b + 1)
        return b

    return [pl.BlockSpec((SPAN, d), functools.partial(lambda b, o, w: (rowblk(w, b), o), o=part, w=which))
            for part, which in kinds]


def _head_col(v, head):
    return v[:, head:head + 1]


def _expand_heads(w, j):
    shape = (w.shape[0], LANES)
    return jnp.where(_left_half(shape), jnp.broadcast_to(_head_col(w, 2 * j), shape), jnp.broadcast_to(_head_col(w, 2 * j + 1), shape))


def _attn_fwd(qkv, slopes, dil, name):
    s, d3 = qkv.shape
    d = d3 // 3
    nb = s // (dil * SPAN)
    table, _ = _att_bias(slopes, dil)

    def body(q_ref, kc_ref, kp_ref, vc_ref, vp_ref, tb_ref, o_ref, l_ref):
        left = _left_half((SPAN, LANES))
        lane = lax.broadcasted_iota(jnp.int32, (SPAN, LANES), 1)
        lses = jnp.zeros((SPAN, LANES), F32)
        for hp in range(d // LANES):
            cols = slice(hp * LANES, (hp + 1) * LANES)
            q = q_ref[:, cols]
            q2 = jnp.concatenate([_head_mask(q, 0), _head_mask(q, 1)], axis=0) * ATT_SCALE
            k2 = jnp.concatenate([kc_ref[:, cols], kp_ref[:, cols]], axis=0)
            v2 = jnp.concatenate([vc_ref[:, cols], vp_ref[:, cols]], axis=0)
            sc = _dot_nt(q2, k2) + tb_ref[hp]
            m = jnp.max(sc, axis=-1, keepdims=True)
            p = jnp.exp(sc - m)
            l = jnp.sum(p, axis=-1, keepdims=True)
            r = _dot_nn(p, v2) * (1.0 / l)
            lse = m + jnp.log(l)
            o_ref[:, cols] = jnp.where(left, r[:SPAN], r[SPAN:])
            lses = jnp.where(lane == 2 * hp, lse[:SPAN], jnp.where(lane == 2 * hp + 1, lse[SPAN:], lses))
        l_ref[...] = lses

    specs = _att_specs(s, d, dil, [(0, "cur"), (1, "cur"), (1, "prev"), (2, "cur"), (2, "prev")])
    tbl = pl.BlockSpec((None,) + table.shape[1:], lambda b: (jnp.where(b % nb == 0, 0, 1), 0, 0, 0))
    out_spec = pl.BlockSpec((SPAN, d), lambda b: (b, 0))
    return pl.pallas_call(
        body,
        grid=(s // SPAN,),
        in_specs=specs + [tbl],
        out_specs=[out_spec, pl.BlockSpec((SPAN, LANES), lambda b: (b, 0))],
        out_shape=[jax.ShapeDtypeStruct((s, d), F32), jax.ShapeDtypeStruct((s, LANES), F32)],
        name=name,
        compiler_params=_cparams(("parallel",)),
    )(qkv, qkv, qkv, qkv, qkv, table)


def _attn_bwd(qkv, do, lse, dd, slopes, dil, name):
    s, d3 = qkv.shape
    d = d3 // 3
    nb = s // (dil * SPAN)
    _, table = _att_bias(slopes, dil)

    def heads_stacked(cur, nxt):
        return jnp.concatenate([_head_mask(cur, 0), _head_mask(cur, 1), _head_mask(nxt, 0), _head_mask(nxt, 1)], axis=0)

    def cols_stacked(cur, nxt, hp):
        return jnp.concatenate([jnp.broadcast_to(_head_col(a, 2 * hp + h), (SPAN, LANES)) for a in (cur, nxt) for h in range(2)], axis=0)

    def body(k_ref, v_ref, qc_ref, qn_ref, doc_ref, don_ref, lc_ref, ln_ref, ddc_ref, ddn_ref, tb_ref, out_ref, carry):
        b = pl.program_id(0)

        @pl.when(b == 0)
        def _():
            carry[...] = jnp.zeros_like(carry)

        left = _left_half((SPAN, LANES))
        lse_c, lse_n, dd_c, dd_n = lc_ref[...], ln_ref[...], ddc_ref[...], ddn_ref[...]
        for hp in range(d // LANES):
            cols = slice(hp * LANES, (hp + 1) * LANES)
            k, v = k_ref[:, cols], v_ref[:, cols]
            q4 = heads_stacked(qc_ref[:, cols], qn_ref[:, cols])
            do4 = heads_stacked(doc_ref[:, cols], don_ref[:, cols])
            sc = _dot_nt(q4 * ATT_SCALE, k) + tb_ref[hp]
            p = jnp.exp(sc - cols_stacked(lse_c, lse_n, hp))
            ds = p * (_dot_nt(do4, v) - cols_stacked(dd_c, dd_n, hp))
            dq4 = _dot_nn(ds, k)
            dq_cur = jnp.where(left, dq4[:SPAN], dq4[SPAN:2 * SPAN]) + carry[:, cols]
            carry[:, cols] = jnp.where(left, dq4[2 * SPAN:3 * SPAN], dq4[3 * SPAN:])
            out_ref[:, cols] = (dq_cur * ATT_SCALE).astype(out_ref.dtype)
            out_ref[:, d + hp * LANES:d + (hp + 1) * LANES] = (_dot_tn(ds, q4) * ATT_SCALE).astype(out_ref.dtype)
            out_ref[:, 2 * d + hp * LANES:2 * d + (hp + 1) * LANES] = _dot_tn(p, do4).astype(out_ref.dtype)

    qkv_specs = _att_specs(s, d, dil, [(1, "cur"), (2, "cur"), (0, "cur"), (0, "next")])
    pair = _att_specs(s, d, dil, [(0, "cur"), (0, "next")])
    heads = _att_specs(s, LANES, dil, [(0, "cur"), (0, "next")])
    tbl = pl.BlockSpec((None,) + table.shape[1:], lambda b: (jnp.where(b % nb == nb - 1, 0, 1), 0, 0, 0))
    return pl.pallas_call(
        body,
        grid=(s // SPAN,),
        in_specs=qkv_specs + pair + heads + heads + [tbl],
        out_specs=pl.BlockSpec((SPAN, d3), lambda b: (b, 0)),
        out_shape=jax.ShapeDtypeStruct((s, d3), MXU_DTYPE),
        scratch_shapes=[pltpu.VMEM((SPAN, d), F32)],
        name=name,
        compiler_params=_cparams(("arbitrary",)),
    )(qkv, qkv, qkv, qkv, do, do, lse, lse, dd, dd, table)


def _mix_weights(l_refs):
    ls = [r[...] for r in l_refs]
    m = functools.reduce(jnp.maximum, ls)
    es = [jnp.exp(l - m) for l in ls]
    tot = functools.reduce(lambda a, c: a + c, es)
    return [e / tot for e in es]


def _combine_fwd(os_, ls_, name):
    s, d = os_[0].shape
    n = len(os_)

    def body(*refs):
        o_refs, l_refs, out_ref = refs[:n], refs[n:2 * n], refs[2 * n]
        ws = _mix_weights(l_refs)
        for j in range(d // LANES):
            cols = slice(j * LANES, (j + 1) * LANES)
            acc = _expand_heads(ws[0], j) * o_refs[0][:, cols]
            for w, o in zip(ws[1:], o_refs[1:]):
                acc = acc + _expand_heads(w, j) * o[:, cols]
            out_ref[:, cols] = acc

    return _rows(body, s, ROW_TILE, [("blk", a) for a in os_ + ls_], [("blk", (s, d), F32)], name)[0]


def _combine_bwd(do, o, ls_, name):
    s, d = o.shape
    n = len(ls_)
    sel = (lax.broadcasted_iota(jnp.int32, (d, LANES), 0) // HEAD_DIM == lax.broadcasted_iota(jnp.int32, (d, LANES), 1)).astype(F32)

    def body(do_ref, o_ref, *rest):
        l_refs, sel_ref, outs = rest[:n], rest[n], rest[n + 1:]
        ws = _mix_weights(l_refs)
        dov = do_ref[...]
        r = jnp.dot(dov * o_ref[...], sel_ref[...], precision=lax.Precision.HIGHEST, preferred_element_type=F32)
        for g in range(n):
            outs[2 * g + 1][...] = ws[g] * r
            for j in range(d // LANES):
                cols = slice(j * LANES, (j + 1) * LANES)
                outs[2 * g][:, cols] = (_expand_heads(ws[g], j) * dov[:, cols]).astype(outs[2 * g].dtype)

    outs = []
    for _ in range(n):
        outs += [("blk", (s, d), MXU_DTYPE), ("blk", (s, LANES), F32)]
    res = _rows(body, s, ROW_TILE, [("blk", do), ("blk", o)] + [("blk", l) for l in ls_] + [("all", sel)], outs, name)
    return [(res[2 * g], res[2 * g + 1]) for g in range(n)]


def _ada_fwd(c_all, w, b, name):
    nsub, d, cs = w.shape

    def body(c_ref, w_ref, b_ref, o_ref):
        cv = c_ref[...]
        sc = cv * (1.0 / (1.0 + jnp.exp(-cv)))
        o_ref[...] = _dot_nn(sc, w_ref[...]) + b_ref[...]

    return pl.pallas_call(
        body,
        grid=(nsub,),
        in_specs=[pl.BlockSpec(c_all.shape, lambda i: (0, 0)), pl.BlockSpec((None, d, cs), lambda i: (i, 0, 0)),
                  pl.BlockSpec((None, 1, cs), lambda i: (i, 0, 0))],
        out_specs=pl.BlockSpec((None, N_DEV, cs), lambda i: (i, 0, 0)),
        out_shape=jax.ShapeDtypeStruct((nsub, N_DEV, cs), F32),
        name=name,
        compiler_params=_cparams(("parallel",)),
    )(c_all, w, b)


def _ada_bwd(c_all_t, dm, name):
    d, nb = c_all_t.shape
    nsub, _, cs = dm.shape

    def body(c_ref, dm_ref, o_ref):
        cv = c_ref[...]
        sc = cv * (1.0 / (1.0 + jnp.exp(-cv)))
        acc = sc[:, 0:1] * dm_ref[0:1, :]
        for bi in range(1, nb):
            acc = acc + sc[:, bi:bi + 1] * dm_ref[bi:bi + 1, :]
        o_ref[...] = acc

    return pl.pallas_call(
        body,
        grid=(nsub,),
        in_specs=[pl.BlockSpec(c_all_t.shape, lambda i: (0, 0)), pl.BlockSpec((None, nb, cs), lambda i: (i, 0, 0))],
        out_specs=pl.BlockSpec((None, d, cs), lambda i: (i, 0, 0)),
        out_shape=jax.ShapeDtypeStruct((nsub, d, cs), F32),
        name=name,
        compiler_params=_cparams(("parallel",)),
    )(c_all_t, dm)


def _row_tile(r, row_elems, block_elems=256 * 1024):
    t = 2 * SUBLANES
    if r % t:
        return r
    while t * 2 * row_elems <= block_elems and r % (t * 2) == 0:
        t *= 2
    return t


def _adamw(w, g, m, v, name):
    shape = w.shape
    c = shape[-1]
    r = w.size // c
    tr = _row_tile(r, c)
    w2, g2, m2, v2 = [a.reshape(r, c) for a in (w, g, m, v)]
    bc1 = 1.0 - ADAM_B1 ** ADAM_STEP
    bc2 = 1.0 - ADAM_B2 ** ADAM_STEP

    def body(w_ref, g_ref, m_ref, v_ref, d_ref, nm_ref, nv_ref):
        gv = g_ref[...]
        nm = ADAM_B1 * m_ref[...] + (1.0 - ADAM_B1) * gv
        nv = ADAM_B2 * v_ref[...] + (1.0 - ADAM_B2) * (gv * gv)
        d_ref[...] = -ADAM_LR * ((nm / bc1) / (jnp.sqrt(nv / bc2) + ADAM_EPS) + ADAM_WD * w_ref[...])
        nm_ref[...] = nm
        nv_ref[...] = nv

    res = _rows(body, r, tr, [("blk", a) for a in (w2, g2, m2, v2)], [("blk", (r, c), F32)] * 3, name)
    return [a.reshape(shape) for a in res]


def _sum_slots(buf, name):
    n, r, c = buf.shape
    tr = _row_tile(r, n * c, 2 * 1024 * 1024)

    def body(b_ref, o_ref):
        acc = b_ref[0].astype(F32)
        for k in range(1, n):
            acc = acc + b_ref[k].astype(F32)
        o_ref[...] = acc

    return pl.pallas_call(
        body,
        grid=(r // tr,),
        in_specs=[pl.BlockSpec((n, tr, c), lambda i: (0, i, 0))],
        out_specs=pl.BlockSpec((tr, c), lambda i: (i, 0)),
        out_shape=jax.ShapeDtypeStruct((r, c), F32),
        name=name,
        compiler_params=_cparams(("parallel",)),
    )(buf)


def _me():
    return lax.axis_index("x"), lax.axis_index("y"), lax.axis_index("c")


def _all_gather_small(blk, name, after=()):
    m_per, n = blk.shape

    def body(x_ref, *rest):
        out_ref, send_sems, recv_sems, local_sem = rest[len(after):]
        x, y, c = _me()
        me, sibling = (x, y, c), (x, y, 1 - c)
        chips = [(1 - x, y), (x, 1 - y), (1 - x, 1 - y)]

        def rows(px, py, pc):
            return out_ref.at[pl.ds((4 * px + 2 * py + pc) * m_per, m_per), :]

        def copy(k, block, to, src=None):
            return pltpu.make_async_remote_copy(
                src_ref=rows(*block) if src is None else src, dst_ref=rows(*block),
                send_sem=send_sems.at[k], recv_sem=recv_sems.at[k], device_id=to, device_id_type=MESH)

        mine = pltpu.make_async_copy(x_ref, rows(*me), local_sem)
        mine.start()
        first = [copy(0, me, sibling, src=x_ref)]
        first += [copy(1 + j, me, (*chip, c), src=x_ref) for j, chip in enumerate(chips)]
        for cp in first:
            cp.start()
        passed = [copy(4 + j, (*chip, c), sibling) for j, chip in enumerate(chips)]
        for j, chip in enumerate(chips):
            copy(1 + j, (*chip, c), me).wait_recv()
            passed[j].start()
        copy(0, sibling, me).wait_recv()
        for j, chip in enumerate(chips):
            copy(4 + j, (*chip, 1 - c), me).wait_recv()
        for cp in first + passed:
            cp.wait_send()
        mine.wait()

    return pl.pallas_call(
        body,
        out_shape=jax.ShapeDtypeStruct((N_DEV * m_per, n), blk.dtype),
        in_specs=[pl.BlockSpec(memory_space=pltpu.VMEM)] + [pl.BlockSpec(memory_space=pl.ANY)] * len(after),
        out_specs=pl.BlockSpec(memory_space=pltpu.VMEM),
        scratch_shapes=[pltpu.SemaphoreType.DMA((7,)), pltpu.SemaphoreType.DMA((7,)), pltpu.SemaphoreType.DMA],
        name=name,
        compiler_params=pltpu.CompilerParams(vmem_limit_bytes=VMEM_LIMIT),
    )(blk, *after)


_HBM = pl.BlockSpec(memory_space=pltpu.HBM)
_SEM = pl.BlockSpec(memory_space=pltpu.SEMAPHORE)
_EFFECT = pltpu.SideEffectType.DATAFLOW_SIDE_EFFECTING


def _other_chips(x, y):
    return [(1 - x, y), (x, 1 - y), (1 - x, 1 - y)]


def _gather_copy(w, j, src_ref, land_ref, send_sems, recv_sems, halved=False):
    x, y, c = _me()
    if halved:
        half = src_ref.shape[0] // 2
        src_ref = src_ref.at[pl.ds(c * half, half), :]
    return pltpu.make_async_remote_copy(
        src_ref=src_ref, dst_ref=land_ref.at[2 * x + y], send_sem=send_sems.at[3 * w + j], recv_sem=recv_sems.at[3 * w + j],
        device_id=(*_other_chips(x, y)[j], c), device_id_type=MESH)


def _gather_start(shards, halved, after, name):
    n = len(shards)
    lands = [lax.empty((N_CHIPS, s.shape[0] // 2 if w in halved else s.shape[0], s.shape[1]), s.dtype) for w, s in enumerate(shards)]

    def body(*refs):
        in_refs, land_refs = refs[:n], refs[n:2 * n]
        send_sems, recv_sems = refs[2 * n + 1], refs[2 * n + 2]
        token = refs[-1]
        for w in range(n):
            for j in range(3):
                _gather_copy(w, j, in_refs[w], land_refs[w], send_sems, recv_sems, w in halved).start()
        token[...] = jnp.zeros_like(token)

    res = pl.pallas_call(
        body,
        out_shape=(pltpu.SemaphoreType.DMA((3 * n,)), pltpu.SemaphoreType.DMA((3 * n,)),
                   *[pltpu.HBM(s.shape, s.dtype) for s in shards], *[pltpu.HBM(l.shape, l.dtype) for l in lands],
                   jax.ShapeDtypeStruct((SUBLANES, LANES), F32)),
        in_specs=[_HBM] * (2 * n) + [pl.BlockSpec(memory_space=pl.ANY)],
        out_specs=(_SEM, _SEM, *[_HBM] * (2 * n), pl.BlockSpec(memory_space=pltpu.VMEM)),
        input_output_aliases={i: 2 + i for i in range(2 * n)},
        name=name,
        compiler_params=pltpu.CompilerParams(has_side_effects=_EFFECT),
    )(*[pltpu.with_memory_space_constraint(a, pltpu.HBM) for a in list(shards) + lands], after)
    return res[0], res[1], res[2:2 + n], res[2 + n:2 + 2 * n], res[-1]


def _gather_wait(w, shard, land, send_sems, recv_sems, after, name, halved=False):
    def body(s_ref, land_ref, send_sems, recv_sems, after_ref, s_out, land_out, stage):
        x, y, _ = _me()
        if not halved:
            pltpu.sync_copy(s_ref, stage)
            pltpu.sync_copy(stage, land_out.at[2 * x + y])
        for j in range(3):
            cp = _gather_copy(w, j, s_ref, land_ref, send_sems, recv_sems, halved)
            cp.wait_send()
            cp.wait_recv()

    return pl.pallas_call(
        body,
        out_shape=(pltpu.HBM(shard.shape, shard.dtype), pltpu.HBM(land.shape, land.dtype)),
        in_specs=(_HBM, _HBM, _SEM, _SEM, pl.BlockSpec(memory_space=pl.ANY)),
        out_specs=(_HBM, _HBM),
        input_output_aliases={0: 0, 1: 1},
        scratch_shapes=[pltpu.VMEM((SUBLANES, LANES) if halved else shard.shape, shard.dtype)],
        name=name,
        compiler_params=pltpu.CompilerParams(has_side_effects=_EFFECT, vmem_limit_bytes=VMEM_LIMIT),
    )(shard, land, send_sems, recv_sems, after)


def _assemble_halves(shard, land, name):
    half = land.shape[1]

    def body(s_ref, land_ref, out_ref, send_sems, recv_sems, local_sems):
        x, y, c = _me()
        own = pltpu.make_async_copy(s_ref, out_ref.at[2 * x + y], local_sems.at[3])
        own.start()
        cps = []
        for j, (ox, oy) in enumerate(_other_chips(x, y)):
            qj = 2 * ox + oy
            mine = out_ref.at[qj, pl.ds(c * half, half), :]
            lc = pltpu.make_async_copy(land_ref.at[qj], mine, local_sems.at[j])
            lc.start()
            rc = pltpu.make_async_remote_copy(
                src_ref=land_ref.at[qj], dst_ref=mine, send_sem=send_sems.at[j], recv_sem=recv_sems.at[j],
                device_id=(x, y, 1 - c), device_id_type=MESH)
            rc.start()
            cps.append((lc, rc))
        for lc, rc in cps:
            rc.wait_recv()
        for lc, rc in cps:
            rc.wait_send()
            lc.wait()
        own.wait()

    vmem = pl.BlockSpec(memory_space=pltpu.VMEM)
    return pl.pallas_call(
        body,
        out_shape=jax.ShapeDtypeStruct((N_CHIPS,) + shard.shape, shard.dtype),
        in_specs=[vmem, vmem],
        out_specs=vmem,
        scratch_shapes=[pltpu.SemaphoreType.DMA((3,)), pltpu.SemaphoreType.DMA((3,)), pltpu.SemaphoreType.DMA((4,))],
        name=name,
        compiler_params=pltpu.CompilerParams(vmem_limit_bytes=VMEM_LIMIT),
    )(shard, land)


def _piece_shape(shape, kind):
    k, nn = shape
    return (k // 2, nn // N_CHIPS) if kind == "col" else (k // N_CHIPS // 2, nn)


def _piece_of(g_ref, kind, tq, tc):
    pr, pc = _piece_shape(g_ref.shape, kind)
    if kind == "col":
        return g_ref.at[pl.ds(tc * pr, pr), pl.ds(tq * pc, pc)]
    return g_ref.at[pl.ds((2 * tq + tc) * pr, pr), :]


def _scatter_copy(w, r, kind, g_ref, land_ref, send_sems, recv_sems):
    x, y, c = _me()
    tx, ty, tc = (x + ((r >> 2) & 1)) % 2, (y + ((r >> 1) & 1)) % 2, (c + (r & 1)) % 2
    return pltpu.make_async_remote_copy(
        src_ref=_piece_of(g_ref, kind, 2 * tx + ty, tc), dst_ref=land_ref.at[4 * x + 2 * y + c],
        send_sem=send_sems.at[N_DEV * w + r], recv_sem=recv_sems.at[N_DEV * w + r], device_id=(tx, ty, tc), device_id_type=MESH)


def _scatter_start(gs, kinds, name):
    n = len(gs)
    pieces = [_piece_shape(g.shape, kind) for g, kind in zip(gs, kinds)]
    lands = [lax.empty((N_DEV,) + p, g.dtype) for p, g in zip(pieces, gs)]

    def body(*refs):
        g_refs, land_refs, send_sems, recv_sems = refs[:n], refs[n:2 * n], refs[2 * n], refs[2 * n + 1]
        land_outs, stages = refs[3 * n + 2:4 * n + 2], refs[4 * n + 2:]
        x, y, c = _me()
        for w in range(n):
            for r in range(1, N_DEV):
                _scatter_copy(w, r, kinds[w], g_refs[w], land_refs[w], send_sems, recv_sems).start()
        for w in range(n):
            pltpu.sync_copy(_piece_of(g_refs[w], kinds[w], 2 * x + y, c), stages[w])
            pltpu.sync_copy(stages[w], land_outs[w].at[4 * x + 2 * y + c])

    arrays = list(gs) + lands
    res = pl.pallas_call(
        body,
        out_shape=(pltpu.SemaphoreType.DMA((N_DEV * n,)), pltpu.SemaphoreType.DMA((N_DEV * n,)),
                   *[pltpu.HBM(a.shape, a.dtype) for a in arrays]),
        in_specs=[_HBM] * (2 * n),
        out_specs=(_SEM, _SEM, *[_HBM] * (2 * n)),
        input_output_aliases={i: 2 + i for i in range(2 * n)},
        scratch_shapes=[pltpu.VMEM(p, g.dtype) for p, g in zip(pieces, gs)],
        name=name,
        compiler_params=pltpu.CompilerParams(has_side_effects=_EFFECT, vmem_limit_bytes=VMEM_LIMIT),
    )(*[pltpu.with_memory_space_constraint(a, pltpu.HBM) for a in arrays])
    return res[0], res[1], res[2:2 + n], res[2 + n:]


def _scatter_wait(send_sems, recv_sems, gs, lands, kinds, after, name):
    n = len(gs)

    def body(*refs):
        g_refs, land_refs, send_sems, recv_sems = refs[:n], refs[n:2 * n], refs[2 * n], refs[2 * n + 1]
        for w in range(n):
            for r in range(1, N_DEV):
                cp = _scatter_copy(w, r, kinds[w], g_refs[w], land_refs[w], send_sems, recv_sems)
                cp.wait_send()
                cp.wait_recv()

    arrays = list(gs) + list(lands)
    return pl.pallas_call(
        body,
        out_shape=tuple(pltpu.HBM(a.shape, a.dtype) for a in arrays),
        in_specs=(*[_HBM] * (2 * n), _SEM, _SEM, pl.BlockSpec(memory_space=pl.ANY)),
        out_specs=tuple([_HBM] * (2 * n)),
        input_output_aliases={i: i for i in range(2 * n)},
        name=name,
        compiler_params=pltpu.CompilerParams(has_side_effects=_EFFECT),
    )(*arrays, send_sems, recv_sems, after)[n:]


def _swap_halves(halves, name):
    n = len(halves)

    def body(*refs):
        in_refs, out_refs = refs[:n], refs[n:2 * n]
        send_sems, recv_sems, local_sems = refs[2 * n:]
        x, y, c = _me()
        cps = []
        for w in range(n):
            lc = pltpu.make_async_copy(in_refs[w], out_refs[w].at[c], local_sems.at[w])
            lc.start()
            rc = pltpu.make_async_remote_copy(
                src_ref=in_refs[w], dst_ref=out_refs[w].at[c], send_sem=send_sems.at[w], recv_sem=recv_sems.at[w],
                device_id=(x, y, 1 - c), device_id_type=MESH)
            rc.start()
            cps.append((lc, rc))
        for lc, rc in cps:
            rc.wait_recv()
        for lc, rc in cps:
            rc.wait_send()
            lc.wait()

    vmem = pl.BlockSpec(memory_space=pltpu.VMEM)
    return pl.pallas_call(
        body,
        out_shape=[jax.ShapeDtypeStruct((2,) + h.shape, h.dtype) for h in halves],
        in_specs=[vmem] * n,
        out_specs=[vmem] * n,
        scratch_shapes=[pltpu.SemaphoreType.DMA((n,)), pltpu.SemaphoreType.DMA((n,)), pltpu.SemaphoreType.DMA((n,))],
        name=name,
        compiler_params=pltpu.CompilerParams(vmem_limit_bytes=VMEM_LIMIT),
    )(*halves)


def _to_streams(a, dil):
    if dil == 1:
        return a
    s, c = a.shape
    return a.reshape(s // dil, dil, c).transpose(1, 0, 2).reshape(s, c)


def _from_streams(a, dil):
    if dil == 1:
        return a
    s, c = a.shape
    return a.reshape(dil, s // dil, c).transpose(1, 0, 2).reshape(s, c)


def _mm_tiles(s):
    return min(s, 2048)


def _local_step(x0, target, mvec, ln_g, ln_b, small, fetch, emit, start):
    s, d = x0.shape
    tm = _mm_tiles(s)
    row = lambda v: v.reshape(1, -1)
    shift = [row(mvec[i, :d]) for i in range(4)]
    scale = [row(mvec[i, d:2 * d]) for i in range(4)]
    gate = [row(1.0 + mvec[i, 2 * d:]) for i in range(4)]
    lg = [row(ln_g[i]) for i in range(4)]
    lb = [row(ln_b[i]) for i in range(4)]
    mm = functools.partial(_mm, tm=tm)
    mm_w = functools.partial(_mm, tm=1024, tk=min(s, 2048), mode="tn")

    xs, ys, big = [x0], [], {}
    h0 = _mod(x0, scale[0], shift[0], start, "mod0")
    big["a_w_in"] = fetch("a_w_in", h0)
    uvpre = mm(h0, big["a_w_in"], mode="nn", name="a_in", outs=[F32], tn=512, tk=1024,
               epi=lambda r, bias: [r + bias], extras=[("row", small["a_b_in"])])
    gated = _spatial_fwd(uvpre, small["a_vn_g"], small["a_vn_b"], small["wc"], small["bias_full"], "a_spatial")
    big["a_w_out"] = fetch("a_w_out", gated)
    ys.append(mm(gated, big["a_w_out"], mode="nn", name="a_out", outs=[F32], tn=1024, tk=1024))
    x1, h1 = _resid_ln(xs[0], ys[0], gate[0], lg[0], lb[0], (scale[1], shift[1]), "ln0")
    xs.append(x1)
    relu2 = lambda r: [jnp.square(jnp.maximum(r, 0.0))]
    big["up0"] = fetch("up0", h1)
    r0 = mm(h1, big["up0"], mode="nn", name="up0", outs=[MXU_DTYPE], tn=1024, tk=1024, epi=relu2)
    big["down0"] = fetch("down0", r0)
    ys.append(mm(r0, big["down0"], mode="nn", name="down0", outs=[F32], tm=min(s, 1024), tn=1024, tk=2048))
    x2, h2 = _resid_ln(xs[1], ys[1], gate[1], lg[1], lb[1], (scale[2], shift[2]), "ln1")
    xs.append(x2)
    hg, qkvs, o_g, l_g, l_streams = [], [], [], [], []
    big["b_w_qkv"] = fetch("b_w_qkv", h2)
    for g, (_, dil) in enumerate(B_PATTERNS):
        hp = _to_streams(h2, dil)
        qkv = mm(hp, big["b_w_qkv"], mode="nn", name=f"qkv{g}", outs=[MXU_DTYPE], tn=768, tk=1024, b_col0=g * 3 * d, n_out=3 * d)
        og, lgv = _attn_fwd(qkv, small["slopes"], dil, f"attn_fwd{g}")
        hg.append(hp)
        qkvs.append(qkv)
        o_g.append(_from_streams(og, dil))
        l_g.append(_from_streams(lgv, dil))
        l_streams.append(lgv)
    o_mix = _combine_fwd(o_g, l_g, "combine")
    big["b_w_out"] = fetch("b_w_out", o_mix)
    ys.append(mm(o_mix, big["b_w_out"], mode="nn", name="b_out", outs=[F32], tn=1024, tk=1024))
    x3, h3 = _resid_ln(xs[2], ys[2], gate[2], lg[2], lb[2], (scale[3], shift[3]), "ln2")
    xs.append(x3)
    big["up1"] = fetch("up1", h3)
    r1 = mm(h3, big["up1"], mode="nn", name="up1", outs=[MXU_DTYPE], tn=1024, tk=1024, epi=relu2)
    big["down1"] = fetch("down1", r1)
    ys.append(mm(r1, big["down1"], mode="nn", name="down1", outs=[F32], tm=min(s, 1024), tn=1024, tk=2048))

    gb, red_ln, red_mod = {}, [None] * 4, [None] * 4

    def mlp_bwd(i, h, r, dyy):
        gb[f"down{i}"] = mm_w(r, dyy, name=f"g_down{i}", outs=[MXU_DTYPE], tn=1024)
        da = mm(dyy, big[f"down{i}"], mode="nt", name=f"d_down{i}", outs=[MXU_DTYPE], tn=1024, tk=1024,
                after=emit(f"down{i}", gb[f"down{i}"]),
                epi=lambda acc, rv: [acc * (2.0 * jnp.sqrt(rv.astype(F32)))], extras=[("full", r)])
        gb[f"up{i}"] = mm_w(h, da, name=f"g_up{i}", outs=[MXU_DTYPE], tn=1024)
        return [mm(da, big[f"up{i}"], mode="nt", name=f"d_up{i}", outs=[F32], tn=1024, tk=1024, after=emit(f"up{i}", gb[f"up{i}"]))]

    def join(sub, dxr, dhs, after=None):
        res = _mod_ln_bwd(dxr, dhs, xs[sub], scale[sub], xs[sub - 1], ys[sub - 1], gate[sub - 1], lg[sub - 1],
                          f"mod_ln_bwd{sub}", after=after)
        red_mod[sub], red_ln[sub - 1] = res[2], res[3]
        return res[0], res[1]

    loss, dxr, dyy, red_ln[3] = _last_ln_loss_bwd(xs[3], ys[3], gate[3], lg[3], lb[3], target, "ln3_loss_bwd")
    dxr, dyy = join(3, dxr, mlp_bwd(1, h3, r1, dyy))
    gb["b_w_out"] = mm_w(o_mix, dyy, name="g_b_out", outs=[MXU_DTYPE], tn=1024, tk=1024)
    do = mm(dyy, big["b_w_out"], mode="nt", name="d_b_out", outs=[F32], tn=1024, tk=1024, after=emit("b_w_out", gb["b_w_out"]))
    parts = _combine_bwd(do, o_mix, l_g, "combine_bwd")
    dhs, gq = [], None
    for g, (_, dil) in enumerate(B_PATTERNS):
        do_g, dd_g = _to_streams(parts[g][0], dil), _to_streams(parts[g][1], dil)
        dqkv = _attn_bwd(qkvs[g], do_g, l_streams[g], dd_g, small["slopes"], dil, f"attn_bwd{g}")
        gq = mm_w(hg[g], dqkv, name=f"g_qkv{g}", outs=[MXU_DTYPE], tn=1024, out_col0=g * 3 * d, out_cols=len(B_PATTERNS) * 3 * d, into=gq)
        dh = mm(dqkv, big["b_w_qkv"], mode="nt", name=f"d_qkv{g}", outs=[F32], tn=1024, tk=768, b_col0=g * 3 * d)
        dhs.append(_from_streams(dh, dil))
    gb["b_w_qkv"] = gq
    dxr, dyy = join(2, dxr, dhs, after=emit("b_w_qkv", gb["b_w_qkv"]))
    dxr, dyy = join(1, dxr, mlp_bwd(0, h1, r0, dyy))
    gb["a_w_out"] = mm_w(gated, dyy, name="g_a_out", outs=[MXU_DTYPE], tn=1024)
    dgated = mm(dyy, big["a_w_out"], mode="nt", name="d_a_out", outs=[F32], tn=1024, tk=1024, after=emit("a_w_out", gb["a_w_out"]))
    duv, dws, dbias, dbin, dvg, dvb = _spatial_bwd(uvpre, dgated, small["a_vn_g"], small["a_vn_b"], small["wc"],
                                                   small["wct"], small["bias_full"], "a_spatial_bwd")
    gb["a_w_in"] = mm_w(h0, duv, name="g_a_in", outs=[MXU_DTYPE], tn=1024)
    dh = mm(duv, big["a_w_in"], mode="nt", name="d_a_in", outs=[F32], tn=1024, tk=512, after=emit("a_w_in", gb["a_w_in"]))
    dx, red_mod[0] = _mod_bwd(dxr, [dh], xs[0], scale[0], "mod_bwd0")
    dm = [jnp.concatenate([red_mod[i][0], red_mod[i][1], red_ln[i][2]]) for i in range(4)]
    dlg, dlb = [red_ln[i][0] for i in range(4)], [red_ln[i][1] for i in range(4)]

    tril = jnp.tril(jnp.ones((CHUNK, CHUNK), bool))
    gsmall = {
        "a_b_in": dbin.reshape(-1), "a_vn_g": dvg.reshape(-1), "a_vn_b": dvb.reshape(-1),
        "a_w_s": jnp.where(tril, dws, 0.0).reshape(-1),
        "a_b_s": dbias.reshape(CHUNK, A_GROUPS, d // A_GROUPS).sum(-1).T.reshape(-1),
    }
    return loss, dx, gb, jnp.stack(dm), jnp.stack(dlg), jnp.stack(dlb), gsmall


BIG = ("a_w_in", "a_w_out", "up0", "down0", "b_w_qkv", "b_w_out", "up1", "down1")
BIG_KIND = {"a_w_in": "col", "a_w_out": "row", "b_w_qkv": "col", "b_w_out": "row",
            "up0": "col", "up1": "col", "down0": "row", "down1": "row"}
HALVED = ("a_w_in", "down0", "b_w_qkv")
SCATTER_GROUPS = (("down1", "up1"), ("b_w_out", "b_w_qkv"), ("down0", "up0"), ("a_w_out", "a_w_in"))
SMALL = ("a_b_in", "a_vn_g", "a_vn_b", "a_b_s", "a_w_s")


def kernel(x, c, ada_w, ada_b, ln_g, ln_b, a_w_in, a_b_in, a_vn_g, a_vn_b, a_w_s, a_b_s, a_w_out, b_w_qkv, b_w_out, mlp_w_up, mlp_w_down, loss_target, m_ada_w, m_ada_b, m_ln_g, m_ln_b, m_a_w_in, m_a_b_in, m_a_vn_g, m_a_vn_b, m_a_w_s, m_a_b_s, m_a_w_out, m_b_w_qkv, m_b_w_out, m_mlp_w_up, m_mlp_w_down, v_ada_w, v_ada_b, v_ln_g, v_ln_b, v_a_w_in, v_a_b_in, v_a_vn_g, v_a_vn_b, v_a_w_s, v_a_b_s, v_a_w_out, v_b_w_qkv, v_b_w_out, v_mlp_w_up, v_mlp_w_down):
    s, d = x.shape[1], x.shape[2]
    xi, yi, ci = _me()
    q = 2 * xi + yi
    dev = 2 * q + ci
    nsub = 2 * DEPTH
    cs = ada_w.shape[-1]
    ls = ln_g.shape[-1]

    shards = {
        "a_w_in": a_w_in[0], "a_w_out": a_w_out[0], "b_w_qkv": b_w_qkv[0], "b_w_out": b_w_out[0],
        "up0": mlp_w_up[0], "up1": mlp_w_up[1], "down0": mlp_w_down[0], "down1": mlp_w_down[1],
    }
    cast = [shards[k].astype(MXU_DTYPE) for k in BIG]

    pack = jnp.concatenate([c.reshape(-1), ln_g.reshape(-1), ln_b.reshape(-1)]).reshape(-1, LANES)
    got = _all_gather_small(pack, "gather_small", after=cast).reshape(N_DEV, -1)
    c_all = got[:, :d]
    per_chip = got[0::2]
    ln_g_full = per_chip[:, d:d + nsub * ls].reshape(N_CHIPS, nsub, ls).transpose(1, 0, 2).reshape(nsub, d)
    ln_b_full = per_chip[:, d + nsub * ls:].reshape(N_CHIPS, nsub, ls).transpose(1, 0, 2).reshape(nsub, d)
    m_part = _ada_fwd(c_all, ada_w.reshape(nsub, d, cs), ada_b.reshape(nsub, 1, cs), "ada_fwd")
    m_all = _all_gather_small(m_part.reshape(-1, LANES), "gather_mod").reshape(N_DEV, nsub, N_DEV, cs)
    m_mine = lax.dynamic_index_in_dim(m_all[0::2], dev, axis=2, keepdims=False)
    mvec = m_mine.transpose(1, 0, 2).reshape(nsub, 3 * d)

    halved = {BIG.index(k) for k in HALVED}
    send_sems, recv_sems, shard_thru, lands, token = _gather_start(cast, halved, mvec, "gather_start")

    def fetch(k, after):
        w = BIG.index(k)
        shard, gw = _gather_wait(w, shard_thru[w], lands[w], send_sems, recv_sems, after, f"gather_wait_{k}", w in halved)
        if w in halved:
            gw = _assemble_halves(shard, gw, f"assemble_{k}")
        return gw if BIG_KIND[k] == "col" else gw.reshape(1, -1, gw.shape[-1])

    scattering, pending = {}, {}

    def emit(k, g):
        pending[k] = g
        group = next(gr for gr in SCATTER_GROUPS if k in gr)
        if k != group[-1]:
            return None
        scattering[group] = _scatter_start([pending[m] for m in group], [BIG_KIND[m] for m in group], f"scatter_start_{k}")
        return scattering[group][2][0]

    tril = jnp.tril(jnp.ones((CHUNK, CHUNK), bool))
    wc = jnp.where(tril, a_w_s[0], 0.0).astype(MXU_DTYPE)
    heads = jnp.arange(1, B_HEADS + 1, dtype=F32)
    small = {
        "a_b_in": a_b_in, "a_vn_g": a_vn_g, "a_vn_b": a_vn_b,
        "wc": wc, "wct": wc.transpose(0, 2, 1),
        "bias_full": jnp.repeat(a_b_s[0].T, d // A_GROUPS, axis=1),
        "slopes": jnp.exp2(-8.0 * heads / B_HEADS),
    }

    loss_part, grad_x, gb, dm, dlg, dlb, gsmall = _local_step(x[0], loss_target[0], mvec, ln_g_full, ln_b_full, small, fetch, emit, token)
    loss = lax.psum(loss_part, ("x", "y", "c"))

    weights = dict(ada_w=ada_w, ada_b=ada_b, ln_g=ln_g, ln_b=ln_b, a_w_in=a_w_in, a_b_in=a_b_in, a_vn_g=a_vn_g, a_vn_b=a_vn_b,
                   a_w_s=a_w_s, a_b_s=a_b_s, a_w_out=a_w_out, b_w_qkv=b_w_qkv, b_w_out=b_w_out, mlp_w_up=mlp_w_up, mlp_w_down=mlp_w_down)
    ms = dict(ada_w=m_ada_w, ada_b=m_ada_b, ln_g=m_ln_g, ln_b=m_ln_b, a_w_in=m_a_w_in, a_b_in=m_a_b_in, a_vn_g=m_a_vn_g, a_vn_b=m_a_vn_b,
              a_w_s=m_a_w_s, a_b_s=m_a_b_s, a_w_out=m_a_w_out, b_w_qkv=m_b_w_qkv, b_w_out=m_b_w_out, mlp_w_up=m_mlp_w_up, mlp_w_down=m_mlp_w_down)
    vs = dict(ada_w=v_ada_w, ada_b=v_ada_b, ln_g=v_ln_g, ln_b=v_ln_b, a_w_in=v_a_w_in, a_b_in=v_a_b_in, a_vn_g=v_a_vn_g, a_vn_b=v_a_vn_b,
              a_w_s=v_a_w_s, a_b_s=v_a_b_s, a_w_out=v_a_w_out, b_w_qkv=v_b_w_qkv, b_w_out=v_b_w_out, mlp_w_up=v_mlp_w_up, mlp_w_down=v_mlp_w_down)
    grads, updates = {}, {}

    def update(k):
        updates[k] = _adamw(weights[k], grads[k], ms[k], vs[k], f"adamw_{k}")
        return updates[k][0]

    pack_b = jnp.concatenate([dm.reshape(-1), dlg.reshape(-1), dlb.reshape(-1)] + [gsmall[k] for k in SMALL])
    n_small = pack_b.shape[0]
    pack_b = jnp.pad(pack_b, (0, -n_small % (256 * LANES)))
    got_b = _all_gather_small(pack_b.reshape(-1, LANES), "gather_small_grads").reshape(N_DEV, -1, LANES)
    tot = _sum_slots(got_b, "sum_small").reshape(-1)
    o = 0
    dm_tot = tot[o:o + nsub * 3 * d].reshape(nsub, 3 * d); o += nsub * 3 * d
    dlg_tot = tot[o:o + nsub * d].reshape(nsub, d); o += nsub * d
    dlb_tot = tot[o:o + nsub * d].reshape(nsub, d); o += nsub * d
    g_small = {}
    for k, ref in zip(SMALL, (a_b_in, a_vn_g, a_vn_b, a_b_s, a_w_s)):
        g_small[k] = tot[o:o + ref.size].reshape(ref.shape); o += ref.size
    assert o == n_small
    dm_all = got_b.reshape(N_DEV, -1)[:, :nsub * 3 * d].reshape(N_DEV, nsub, 3 * d)
    dm_cols = lax.dynamic_slice_in_dim(dm_all, q * cs, cs, axis=2).transpose(1, 0, 2)

    grads.update({
        "ada_w": _ada_bwd(c_all.T, dm_cols, "ada_bwd").reshape(ada_w.shape),
        "ada_b": lax.dynamic_slice_in_dim(dm_tot, q * cs, cs, axis=1).reshape(ada_b.shape),
        "ln_g": lax.dynamic_slice_in_dim(dlg_tot, q * ls, ls, axis=1).reshape(ln_g.shape),
        "ln_b": lax.dynamic_slice_in_dim(dlb_tot, q * ls, ls, axis=1).reshape(ln_b.shape),
        **g_small,
    })
    for k in ("ada_b", "ln_g", "ln_b") + SMALL:
        update(k)
    done = update("ada_w")

    gfull = {}
    for group in (SCATTER_GROUPS[0] + SCATTER_GROUPS[1], SCATTER_GROUPS[2] + SCATTER_GROUPS[3]):
        bufs = []
        for pair in (group[:2], group[2:]):
            bufs += _scatter_wait(*scattering[pair], [BIG_KIND[m] for m in pair], done, f"scatter_wait_{pair[-1]}")
        halves = [_sum_slots(b, f"sum_{k}") for k, b in zip(group, bufs)]
        fulls = _swap_halves(halves, f"swap_halves_{group[0]}")
        gfull.update({k: f.reshape(-1, f.shape[-1]) for k, f in zip(group, fulls)})
        if group[0] == "down1":
            grads["b_w_qkv"], grads["b_w_out"] = gfull["b_w_qkv"][None], gfull["b_w_out"][None]
            update("b_w_out")
            done = update("b_w_qkv")
    grads.update({
        "a_w_in": gfull["a_w_in"][None], "a_w_out": gfull["a_w_out"][None],
        "mlp_w_up": jnp.stack([gfull["up0"], gfull["up1"]]), "mlp_w_down": jnp.stack([gfull["down0"], gfull["down1"]]),
    })
    for k in ("a_w_in", "a_w_out", "mlp_w_up", "mlp_w_down"):
        update(k)
    names = list(weights)
    return (loss, grad_x[None], *[grads[k] for k in names], *[updates[k][0] for k in names],
            *[updates[k][1] for k in names], *[updates[k][2] for k in names])
```

```python
import functools
import math

import jax
import jax.numpy as jnp
from jax import lax
from jax.experimental import pallas as pl
from jax.experimental.pallas import tpu as pltpu

F32 = jnp.float32
MXU_DTYPE = jnp.bfloat16

DEPTH = 2
CHUNK = 128
A_GROUPS = 16
B_HEADS = 16
HEAD_DIM = 64
B_PATTERNS = ((128, 1), (512, 4), (2048, 16))
SPAN = 128
ALPHA = (2 * DEPTH) ** 0.25
LN_EPS = 1e-5
NEG = -1e30
ATT_SCALE = HEAD_DIM ** -0.5
ADAM_LR, ADAM_B1, ADAM_B2, ADAM_EPS, ADAM_WD, ADAM_STEP = 0.001, 0.9, 0.999, 1e-08, 0.01, 10

N_CHIPS = 4
N_DEV = 8
LANES = 128
SUBLANES = 8
VMEM_LIMIT = 52 * 1024 * 1024
ROW_TILE = 512
MM_ROW_CHUNK = 256
MESH = pl.DeviceIdType.MESH


def _cparams(sem):
    return pltpu.CompilerParams(dimension_semantics=sem, vmem_limit_bytes=VMEM_LIMIT)


def _fold8(v):
    r, c = v.shape
    return jnp.sum(v.reshape(r // SUBLANES, SUBLANES, c), axis=0)


def _gelu(x):
    c = math.sqrt(2.0 / math.pi)
    return 0.5 * x * (1.0 + jnp.tanh(c * (x + 0.044715 * (x * x * x))))


def _gelu_grad(x):
    c = math.sqrt(2.0 / math.pi)
    t = jnp.tanh(c * (x + 0.044715 * (x * x * x)))
    return 0.5 * (1.0 + t) + 0.5 * x * (1.0 - t * t) * c * (1.0 + 3.0 * 0.044715 * x * x)


def _dot(a, b, dims):
    return lax.dot_general(a.astype(MXU_DTYPE), b.astype(MXU_DTYPE), (dims, ((), ())), preferred_element_type=F32)


def _dot_nn(a, b):
    return _dot(a, b, ((1,), (0,)))


def _dot_nt(a, b):
    return _dot(a, b, ((1,), (1,)))


def _dot_tn(a, b):
    return _dot(a, b, ((0,), (0,)))


def _mm(a, b, *, mode, name, outs, tm, tn, tk, epi=None, extras=(), b_col0=0, n_out=None, after=None,
        out_col0=0, out_cols=None, into=None):
    if mode == "nn":
        m, kdim = a.shape
        p, kb, ns = b.shape
        assert kb == kdim and ns % tn == 0 and b_col0 % tn == 0
        n = n_out if n_out is not None else p * ns
        npt, j0 = ns // tn, b_col0 // tn
        a_spec = pl.BlockSpec((tm, tk), lambda i, j, k: (i, k))
        b_spec = pl.BlockSpec((None, tk, tn), lambda i, j, k: ((j + j0) // npt, k, (j + j0) % npt))
        dot = _dot_nn
    elif mode == "nt":
        m, kdim = a.shape
        p, n, ns = b.shape
        assert ns % tk == 0 and b_col0 % tk == 0
        npt, j0 = ns // tk, b_col0 // tk
        a_spec = pl.BlockSpec((tm, tk), lambda i, j, k: (i, k))
        b_spec = pl.BlockSpec((None, tn, tk), lambda i, j, k: ((k + j0) // npt, j, (k + j0) % npt))
        dot = _dot_nt
    else:
        kdim, m = a.shape
        kb, n = b.shape
        assert kb == kdim
        a_spec = pl.BlockSpec((tk, tm), lambda i, j, k: (k, i))
        b_spec = pl.BlockSpec((tk, tn), lambda i, j, k: (k, j))
        dot = _dot_tn
    assert m % tm == 0 and n % tn == 0 and kdim % tk == 0, (name, m, n, kdim, tm, tn, tk)
    nk = kdim // tk
    ex_specs, ex_arrays = [], []
    for kind, arr in extras:
        if kind == "row":
            ex_specs.append(pl.BlockSpec((1, tn), lambda i, j, k: (0, j)))
        else:
            ex_specs.append(pl.BlockSpec((tm, tn), lambda i, j, k: (i, j)))
        ex_arrays.append(arr)
    n_ex, n_o = len(ex_arrays), len(outs)
    deps = [d for d in (after, into) if d is not None]
    n_dep = len(deps)
    j_out = out_col0 // tn
    assert out_col0 % tn == 0 and (into is None or len(outs) == 1)

    def body(a_ref, b_ref, *rest):
        ex_refs, o_refs = rest[:n_ex], rest[n_ex + n_dep:n_ex + n_dep + n_o]
        k = pl.program_id(2)

        chunks = [slice(r0, r0 + min(tm, MM_ROW_CHUNK)) for r0 in range(0, tm, min(tm, MM_ROW_CHUNK))]

        def part(rows):
            return dot(a_ref[:, rows] if mode == "tn" else a_ref[rows, :], b_ref[...])

        def finish(r, rows):
            exs = [e[...] if kind == "row" else e[rows, :] for (kind, _), e in zip(extras, ex_refs)]
            vals = epi(r, *exs) if epi is not None else [r]
            for o, v in zip(o_refs, vals):
                o[rows, :] = v.astype(o.dtype)

        if nk == 1:
            for rows in chunks:
                finish(part(rows), rows)
            return
        acc = rest[n_ex + n_dep + n_o]

        @pl.when(k == 0)
        def _():
            for rows in chunks:
                acc[rows, :] = part(rows)

        @pl.when((k > 0) & (k < nk - 1))
        def _():
            for rows in chunks:
                acc[rows, :] += part(rows)

        @pl.when(k == nk - 1)
        def _():
            for rows in chunks:
                finish(acc[rows, :] + part(rows), rows)

    res = pl.pallas_call(
        body,
        grid=(m // tm, n // tn, nk),
        in_specs=[a_spec, b_spec] + ex_specs + [pl.BlockSpec(memory_space=pl.ANY)] * n_dep,
        out_specs=[pl.BlockSpec((tm, tn), lambda i, j, k: (i, j + j_out)) for _ in outs],
        out_shape=[jax.ShapeDtypeStruct((m, out_cols or n), dt) for dt in outs],
        input_output_aliases={} if into is None else {2 + n_ex + n_dep - 1: 0},
        scratch_shapes=[pltpu.VMEM((tm, tn), F32)] if nk > 1 else [],
        name=name,
        compiler_params=_cparams(("parallel", "parallel", "arbitrary")),
    )(a, b, *ex_arrays, *deps)
    return res if len(outs) > 1 else res[0]


def _rows(body, n_rows, tr, ins, outs, name, scratch=()):
    def spec(kind, shape):
        if kind == "blk":
            return pl.BlockSpec((tr,) + tuple(shape[1:]), lambda i: (i,) + (0,) * (len(shape) - 1))
        if kind == "dep":
            return pl.BlockSpec(memory_space=pl.ANY)
        return pl.BlockSpec(tuple(shape), lambda i: (0,) * len(shape))

    return pl.pallas_call(
        body,
        grid=(n_rows // tr,),
        in_specs=[spec(k, a.shape) for k, a in ins],
        out_specs=[spec(k, s) for k, s, _ in outs],
        out_shape=[jax.ShapeDtypeStruct(tuple(s), d) for _, s, d in outs],
        scratch_shapes=list(scratch),
        name=name,
        compiler_params=_cparams(("arbitrary",)),
    )(*[a for _, a in ins])


def _ln_stats(z):
    mu = jnp.mean(z, axis=-1, keepdims=True)
    zc = z - mu
    var = jnp.mean(zc * zc, axis=-1, keepdims=True)
    rstd = lax.rsqrt(var + LN_EPS)
    return zc * rstd, rstd


def _mod(x, scale, shift, after, name):
    s, d = x.shape

    def body(x_ref, sc_ref, sh_ref, dep_ref, h_ref):
        h_ref[...] = (x_ref[...] * (1.0 + sc_ref[...]) + sh_ref[...]).astype(h_ref.dtype)

    return _rows(body, s, ROW_TILE, [("blk", x), ("all", scale), ("all", shift), ("dep", after)], [("blk", (s, d), MXU_DTYPE)], name)[0]


def _resid_ln(x, y, gate, g, b, nxt, name):
    s, d = x.shape

    def body(x_ref, y_ref, gate_ref, g_ref, b_ref, sc_ref, sh_ref, xn_ref, h_ref):
        z = ALPHA * x_ref[...] + gate_ref[...] * y_ref[...]
        xhat, _ = _ln_stats(z)
        xn = xhat * g_ref[...] + b_ref[...]
        xn_ref[...] = xn
        h_ref[...] = (xn * (1.0 + sc_ref[...]) + sh_ref[...]).astype(h_ref.dtype)

    return _rows(body, s, ROW_TILE,
                 [("blk", x), ("blk", y), ("all", gate), ("all", g), ("all", b), ("all", nxt[0]), ("all", nxt[1])],
                 [("blk", (s, d), F32), ("blk", (s, d), MXU_DTYPE)], name)


def _mod_bwd(dxr, dhs, x, scale, name, after=None):
    s, d = x.shape
    n_dh = len(dhs)
    n_dep = 0 if after is None else 1

    def body(dxr_ref, *rest):
        dh_refs = rest[:n_dh]
        x_ref, sc_ref, dx_ref, red_ref, a_sh, a_sc = rest[n_dh:n_dh + 2] + rest[n_dh + 2 + n_dep:]
        i = pl.program_id(0)

        @pl.when(i == 0)
        def _():
            a_sh[...] = jnp.zeros_like(a_sh)
            a_sc[...] = jnp.zeros_like(a_sc)

        dh = dh_refs[0][...]
        for r in dh_refs[1:]:
            dh = dh + r[...]
        dx_ref[...] = dxr_ref[...] + dh * (1.0 + sc_ref[...])
        a_sh[...] += _fold8(dh)
        a_sc[...] += _fold8(dh * x_ref[...])

        @pl.when(i == pl.num_programs(0) - 1)
        def _():
            red_ref[...] = jnp.zeros_like(red_ref)
            red_ref[0:1, :] = jnp.sum(a_sh[...], axis=0, keepdims=True)
            red_ref[1:2, :] = jnp.sum(a_sc[...], axis=0, keepdims=True)

    return _rows(body, s, ROW_TILE, [("blk", dxr)] + [("blk", h) for h in dhs] + [("blk", x), ("all", scale)] + [("dep", after)] * n_dep,
                 [("blk", (s, d), F32), ("all", (SUBLANES, d), F32)], name,
                 scratch=[pltpu.VMEM((SUBLANES, d), F32)] * 2)


def _last_ln_loss_bwd(x, y, gate, g, b, target, name):
    s, d = x.shape

    def body(x_ref, y_ref, gate_ref, g_ref, b_ref, t_ref, l_ref, dxr_ref, dyy_ref, red_ref, a_l, a_g, a_b, a_gate):
        i = pl.program_id(0)

        @pl.when(i == 0)
        def _():
            for a in (a_l, a_g, a_b, a_gate):
                a[...] = jnp.zeros_like(a)

        yv = y_ref[...]
        z = ALPHA * x_ref[...] + gate_ref[...] * yv
        xhat, rstd = _ln_stats(z)
        e = xhat * g_ref[...] + b_ref[...] - t_ref[...]
        a_l[...] += _fold8(e * e)
        dxo_v = e * (1.0 / d)
        dxh = dxo_v * g_ref[...]
        dz = rstd * (dxh - jnp.mean(dxh, axis=-1, keepdims=True) - xhat * jnp.mean(dxh * xhat, axis=-1, keepdims=True))
        dxr_ref[...] = ALPHA * dz
        dyy_ref[...] = (gate_ref[...] * dz).astype(dyy_ref.dtype)
        a_g[...] += _fold8(dxo_v * xhat)
        a_b[...] += _fold8(dxo_v)
        a_gate[...] += _fold8(dz * yv)

        @pl.when(i == pl.num_programs(0) - 1)
        def _():
            l_ref[...] = jnp.full(l_ref.shape, 0.5 / d, F32) * jnp.sum(a_l[...])
            red_ref[...] = jnp.zeros_like(red_ref)
            red_ref[0:1, :] = jnp.sum(a_g[...], axis=0, keepdims=True)
            red_ref[1:2, :] = jnp.sum(a_b[...], axis=0, keepdims=True)
            red_ref[2:3, :] = jnp.sum(a_gate[...], axis=0, keepdims=True)

    l, dxr, dyy, red = _rows(
        body, s, ROW_TILE, [("blk", x), ("blk", y), ("all", gate), ("all", g), ("all", b), ("blk", target)],
        [("all", (SUBLANES, LANES), F32), ("blk", (s, d), F32), ("blk", (s, d), MXU_DTYPE), ("all", (SUBLANES, d), F32)], name,
        scratch=[pltpu.VMEM((SUBLANES, d), F32)] * 4)
    return l[0, 0], dxr, dyy, red


def _mod_ln_bwd(dxr, dhs, x, scale, x_in, y, gate, g, name, after=None):
    s, d = x.shape
    n_dh = len(dhs)
    n_dep = 0 if after is None else 1

    def body(dxr_ref, *rest):
        dh_refs = rest[:n_dh]
        x_ref, sc_ref, xin_ref, y_ref, gate_ref, g_ref = rest[n_dh:n_dh + 6]
        dxr_out, dyy_ref, red_mod, red_ln, a_sh, a_sc, a_g, a_b, a_gate = rest[n_dh + 6 + n_dep:]
        i = pl.program_id(0)

        @pl.when(i == 0)
        def _():
            for a in (a_sh, a_sc, a_g, a_b, a_gate):
                a[...] = jnp.zeros_like(a)

        dh = dh_refs[0][...]
        for r in dh_refs[1:]:
            dh = dh + r[...]
        xv = x_ref[...]
        dxo_v = dxr_ref[...] + dh * (1.0 + sc_ref[...])
        a_sh[...] += _fold8(dh)
        a_sc[...] += _fold8(dh * xv)
        yv = y_ref[...]
        z = ALPHA * xin_ref[...] + gate_ref[...] * yv
        xhat, rstd = _ln_stats(z)
        dxh = dxo_v * g_ref[...]
        dz = rstd * (dxh - jnp.mean(dxh, axis=-1, keepdims=True) - xhat * jnp.mean(dxh * xhat, axis=-1, keepdims=True))
        dxr_out[...] = ALPHA * dz
        dyy_ref[...] = (gate_ref[...] * dz).astype(dyy_ref.dtype)
        a_g[...] += _fold8(dxo_v * xhat)
        a_b[...] += _fold8(dxo_v)
        a_gate[...] += _fold8(dz * yv)

        @pl.when(i == pl.num_programs(0) - 1)
        def _():
            red_mod[...] = jnp.zeros_like(red_mod)
            red_mod[0:1, :] = jnp.sum(a_sh[...], axis=0, keepdims=True)
            red_mod[1:2, :] = jnp.sum(a_sc[...], axis=0, keepdims=True)
            red_ln[...] = jnp.zeros_like(red_ln)
            red_ln[0:1, :] = jnp.sum(a_g[...], axis=0, keepdims=True)
            red_ln[1:2, :] = jnp.sum(a_b[...], axis=0, keepdims=True)
            red_ln[2:3, :] = jnp.sum(a_gate[...], axis=0, keepdims=True)

    ins = ([("blk", dxr)] + [("blk", h) for h in dhs]
           + [("blk", x), ("all", scale), ("blk", x_in), ("blk", y), ("all", gate), ("all", g)] + [("dep", after)] * n_dep)
    return _rows(body, s, ROW_TILE, ins,
                 [("blk", (s, d), F32), ("blk", (s, d), MXU_DTYPE), ("all", (SUBLANES, d), F32), ("all", (SUBLANES, d), F32)], name,
                 scratch=[pltpu.VMEM((SUBLANES, d), F32)] * 5)


def _left_half(shape):
    return lax.broadcasted_iota(jnp.int32, shape, 1) < (LANES // 2)


CHUNKS_PER_STEP = 2


def _chunks_of_step():
    return [slice(i * CHUNK, (i + 1) * CHUNK) for i in range(CHUNKS_PER_STEP)]


def _spatial_z(vn, wc_ref, bias_ref, j):
    vb = vn[:, j * LANES:(j + 1) * LANES]
    z0 = _dot_nn(wc_ref[2 * j], vb)
    z1 = _dot_nn(wc_ref[2 * j + 1], vb)
    return jnp.where(_left_half(z0.shape), z0, z1) + bias_ref[:, j * LANES:(j + 1) * LANES]


def _spatial_fwd(uvpre, vn_g, vn_b, wc, bias_full, name):
    s, d2 = uvpre.shape
    d = d2 // 2

    def body(uv_ref, g_ref, b_ref, wc_ref, bias_ref, out_ref):
        for rows in _chunks_of_step():
            u = _gelu(uv_ref[rows, :d])
            v = _gelu(uv_ref[rows, d:])
            vh, _ = _ln_stats(v)
            vn = vh * g_ref[...] + b_ref[...]
            for j in range(d // LANES):
                z = _spatial_z(vn, wc_ref, bias_ref, j)
                out_ref[rows, j * LANES:(j + 1) * LANES] = (u[:, j * LANES:(j + 1) * LANES] * z).astype(out_ref.dtype)

    return _rows(body, s, CHUNKS_PER_STEP * CHUNK, [("blk", uvpre), ("all", vn_g), ("all", vn_b), ("all", wc), ("all", bias_full)],
                 [("blk", (s, d), MXU_DTYPE)], name)[0]


def _spatial_bwd(uvpre, dgated, vn_g, vn_b, wc, wct, bias_full, name):
    s, d2 = uvpre.shape
    d = d2 // 2

    def body(uv_ref, dg_ref, g_ref, b_ref, wc_ref, wct_ref, bias_ref,
             duv_ref, dws_ref, dbias_ref, dbin_ref, dvg_ref, dvb_ref, dvn_buf, a_bin, a_vg, a_vb):
        i = pl.program_id(0)

        @pl.when(i == 0)
        def _():
            dws_ref[...] = jnp.zeros_like(dws_ref)
            dbias_ref[...] = jnp.zeros_like(dbias_ref)
            a_bin[...] = jnp.zeros_like(a_bin)
            a_vg[...] = jnp.zeros_like(a_vg)
            a_vb[...] = jnp.zeros_like(a_vb)

        for rows in _chunks_of_step():
            up = uv_ref[rows, :d]
            vp = uv_ref[rows, d:]
            u = _gelu(up)
            v = _gelu(vp)
            vh, rstd = _ln_stats(v)
            vn = vh * g_ref[...] + b_ref[...]
            dg = dg_ref[rows, :]
            dzz = dg * u
            dbias_ref[...] += dzz
            for j in range(d // LANES):
                cols = slice(j * LANES, (j + 1) * LANES)
                z = _spatial_z(vn, wc_ref, bias_ref, j)
                dup = dg[:, cols] * z * _gelu_grad(up[:, cols])
                duv_ref[rows, cols] = dup.astype(duv_ref.dtype)
                a_bin[:, cols] += _fold8(dup)
                dzb = dzz[:, cols]
                left = _left_half(dzb.shape)
                dvn_buf[:, cols] = jnp.where(left, _dot_nn(wct_ref[2 * j], dzb), _dot_nn(wct_ref[2 * j + 1], dzb))
                vb = vn[:, cols]
                dws_ref[2 * j] += _dot_nt(jnp.where(left, dzb, 0.0), vb)
                dws_ref[2 * j + 1] += _dot_nt(jnp.where(left, 0.0, dzb), vb)
            dvn = dvn_buf[...]
            a_vg[...] += _fold8(dvn * vh)
            a_vb[...] += _fold8(dvn)
            dvh = dvn * g_ref[...]
            dv = rstd * (dvh - jnp.mean(dvh, axis=-1, keepdims=True) - vh * jnp.mean(dvh * vh, axis=-1, keepdims=True))
            dvp = dv * _gelu_grad(vp)
            duv_ref[rows, d:] = dvp.astype(duv_ref.dtype)
            a_bin[:, d:] += _fold8(dvp)

        @pl.when(i == pl.num_programs(0) - 1)
        def _():
            dbin_ref[...] = jnp.sum(a_bin[...], axis=0, keepdims=True)
            dvg_ref[...] = jnp.sum(a_vg[...], axis=0, keepdims=True)
            dvb_ref[...] = jnp.sum(a_vb[...], axis=0, keepdims=True)

    return _rows(body, s, CHUNKS_PER_STEP * CHUNK,
                 [("blk", uvpre), ("blk", dgated), ("all", vn_g), ("all", vn_b), ("all", wc), ("all", wct), ("all", bias_full)],
                 [("blk", (s, d2), MXU_DTYPE), ("all", (A_GROUPS, CHUNK, CHUNK), F32), ("all", (CHUNK, d), F32),
                  ("all", (1, d2), F32), ("all", (1, d), F32), ("all", (1, d), F32)], name,
                 scratch=[pltpu.VMEM((CHUNK, d), F32), pltpu.VMEM((SUBLANES, d2), F32),
                          pltpu.VMEM((SUBLANES, d), F32), pltpu.VMEM((SUBLANES, d), F32)])


def _head_mask(v, h):
    lane = lax.broadcasted_iota(jnp.int32, v.shape, 1)
    return jnp.where((lane >= h * HEAD_DIM) & (lane < (h + 1) * HEAD_DIM), v, jnp.zeros_like(v))


def _att_bias(slopes, dil):
    qi = lax.broadcasted_iota(jnp.int32, (SPAN, SPAN), 0)
    ki = lax.broadcasted_iota(jnp.int32, (SPAN, SPAN), 1)
    sl = slopes[:, None, None]
    cur = jnp.where(ki <= qi, -sl * (float(dil) * (qi - ki).astype(F32)), NEG)
    prev = jnp.where(ki >= qi, -sl * (float(dil) * (SPAN + qi - ki).astype(F32)), NEG)
    absent = jnp.full_like(prev, NEG)
    pairs = slopes.shape[0] // 2

    def fwd(pv):
        return jnp.concatenate([cur, pv], axis=2).reshape(pairs, 2 * SPAN, 2 * SPAN)

    def bwd(pv):
        return jnp.concatenate([cur.reshape(pairs, 2 * SPAN, SPAN), pv.reshape(pairs, 2 * SPAN, SPAN)], axis=1)

    return jnp.stack([fwd(absent), fwd(prev)]), jnp.stack([bwd(absent), bwd(prev)])


def _att_specs(s, d, dil, kinds):
    nb = s // (dil * SPAN)

    def rowblk(which, b):
        if which == "prev":
            return jnp.where(b % nb == 0, b, b - 1)
        if which == "next":
            return jnp.where(b % nb == nb - 1, b, b + 1)
        return b

    return [pl.BlockSpec((SPAN, d), functools.partial(lambda b, o, w: (rowblk(w, b), o), o=part, w=which))
            for part, which in kinds]


def _head_col(v, head):
    return v[:, head:head + 1]


def _expand_heads(w, j):
    shape = (w.shape[0], LANES)
    return jnp.where(_left_half(shape), jnp.broadcast_to(_head_col(w, 2 * j), shape), jnp.broadcast_to(_head_col(w, 2 * j + 1), shape))


def _attn_fwd(qkv, slopes, dil, name):
    s, d3 = qkv.shape
    d = d3 // 3
    nb = s // (dil * SPAN)
    table, _ = _att_bias(slopes, dil)

    def body(q_ref, kc_ref, kp_ref, vc_ref, vp_ref, tb_ref, o_ref, l_ref):
        left = _left_half((SPAN, LANES))
        lane = lax.broadcasted_iota(jnp.int32, (SPAN, LANES), 1)
        lses = jnp.zeros((SPAN, LANES), F32)
        for hp in range(d // LANES):
            cols = slice(hp * LANES, (hp + 1) * LANES)
            q = q_ref[:, cols]
            q2 = jnp.concatenate([_head_mask(q, 0), _head_mask(q, 1)], axis=0) * ATT_SCALE
            k2 = jnp.concatenate([kc_ref[:, cols], kp_ref[:, cols]], axis=0)
            v2 = jnp.concatenate([vc_ref[:, cols], vp_ref[:, cols]], axis=0)
            sc = _dot_nt(q2, k2) + tb_ref[hp]
            m = jnp.max(sc, axis=-1, keepdims=True)
            p = jnp.exp(sc - m)
            l = jnp.sum(p, axis=-1, keepdims=True)
            r = _dot_nn(p, v2) * (1.0 / l)
            lse = m + jnp.log(l)
            o_ref[:, cols] = jnp.where(left, r[:SPAN], r[SPAN:])
            lses = jnp.where(lane == 2 * hp, lse[:SPAN], jnp.where(lane == 2 * hp + 1, lse[SPAN:], lses))
        l_ref[...] = lses

    specs = _att_specs(s, d, dil, [(0, "cur"), (1, "cur"), (1, "prev"), (2, "cur"), (2, "prev")])
    tbl = pl.BlockSpec((None,) + table.shape[1:], lambda b: (jnp.where(b % nb == 0, 0, 1), 0, 0, 0))
    out_spec = pl.BlockSpec((SPAN, d), lambda b: (b, 0))
    return pl.pallas_call(
        body,
        grid=(s // SPAN,),
        in_specs=specs + [tbl],
        out_specs=[out_spec, pl.BlockSpec((SPAN, LANES), lambda b: (b, 0))],
        out_shape=[jax.ShapeDtypeStruct((s, d), F32), jax.ShapeDtypeStruct((s, LANES), F32)],
        name=name,
        compiler_params=_cparams(("parallel",)),
    )(qkv, qkv, qkv, qkv, qkv, table)


def _attn_bwd(qkv, do, lse, dd, slopes, dil, name):
    s, d3 = qkv.shape
    d = d3 // 3
    nb = s // (dil * SPAN)
    _, table = _att_bias(slopes, dil)

    def heads_stacked(cur, nxt):
        return jnp.concatenate([_head_mask(cur, 0), _head_mask(cur, 1), _head_mask(nxt, 0), _head_mask(nxt, 1)], axis=0)

    def cols_stacked(cur, nxt, hp):
        return jnp.concatenate([jnp.broadcast_to(_head_col(a, 2 * hp + h), (SPAN, LANES)) for a in (cur, nxt) for h in range(2)], axis=0)

    def body(k_ref, v_ref, qc_ref, qn_ref, doc_ref, don_ref, lc_ref, ln_ref, ddc_ref, ddn_ref, tb_ref, out_ref, carry):
        b = pl.program_id(0)

        @pl.when(b == 0)
        def _():
            carry[...] = jnp.zeros_like(carry)

        left = _left_half((SPAN, LANES))
        lse_c, lse_n, dd_c, dd_n = lc_ref[...], ln_ref[...], ddc_ref[...], ddn_ref[...]
        for hp in range(d // LANES):
            cols = slice(hp * LANES, (hp + 1) * LANES)
            k, v = k_ref[:, cols], v_ref[:, cols]
            q4 = heads_stacked(qc_ref[:, cols], qn_ref[:, cols])
            do4 = heads_stacked(doc_ref[:, cols], don_ref[:, cols])
            sc = _dot_nt(q4 * ATT_SCALE, k) + tb_ref[hp]
            p = jnp.exp(sc - cols_stacked(lse_c, lse_n, hp))
            ds = p * (_dot_nt(do4, v) - cols_stacked(dd_c, dd_n, hp))
            dq4 = _dot_nn(ds, k)
            dq_cur = jnp.where(left, dq4[:SPAN], dq4[SPAN:2 * SPAN]) + carry[:, cols]
            carry[:, cols] = jnp.where(left, dq4[2 * SPAN:3 * SPAN], dq4[3 * SPAN:])
            out_ref[:, cols] = (dq_cur * ATT_SCALE).astype(out_ref.dtype)
            out_ref[:, d + hp * LANES:d + (hp + 1) * LANES] = (_dot_tn(ds, q4) * ATT_SCALE).astype(out_ref.dtype)
            out_ref[:, 2 * d + hp * LANES:2 * d + (hp + 1) * LANES] = _dot_tn(p, do4).astype(out_ref.dtype)

    qkv_specs = _att_specs(s, d, dil, [(1, "cur"), (2, "cur"), (0, "cur"), (0, "next")])
    pair = _att_specs(s, d, dil, [(0, "cur"), (0, "next")])
    heads = _att_specs(s, LANES, dil, [(0, "cur"), (0, "next")])
    tbl = pl.BlockSpec((None,) + table.shape[1:], lambda b: (jnp.where(b % nb == nb - 1, 0, 1), 0, 0, 0))
    return pl.pallas_call(
        body,
        grid=(s // SPAN,),
        in_specs=qkv_specs + pair + heads + heads + [tbl],
        out_specs=pl.BlockSpec((SPAN, d3), lambda b: (b, 0)),
        out_shape=jax.ShapeDtypeStruct((s, d3), MXU_DTYPE),
        scratch_shapes=[pltpu.VMEM((SPAN, d), F32)],
        name=name,
        compiler_params=_cparams(("arbitrary",)),
    )(qkv, qkv, qkv, qkv, do, do, lse, lse, dd, dd, table)


def _mix_weights(l_refs):
    ls = [r[...] for r in l_refs]
    m = functools.reduce(jnp.maximum, ls)
    es = [jnp.exp(l - m) for l in ls]
    tot = functools.reduce(lambda a, c: a + c, es)
    return [e / tot for e in es]


def _combine_fwd(os_, ls_, name):
    s, d = os_[0].shape
    n = len(os_)

    def body(*refs):
        o_refs, l_refs, out_ref = refs[:n], refs[n:2 * n], refs[2 * n]
        ws = _mix_weights(l_refs)
        for j in range(d // LANES):
            cols = slice(j * LANES, (j + 1) * LANES)
            acc = _expand_heads(ws[0], j) * o_refs[0][:, cols]
            for w, o in zip(ws[1:], o_refs[1:]):
                acc = acc + _expand_heads(w, j) * o[:, cols]
            out_ref[:, cols] = acc

    return _rows(body, s, ROW_TILE, [("blk", a) for a in os_ + ls_], [("blk", (s, d), F32)], name)[0]


def _combine_bwd(do, o, ls_, name):
    s, d = o.shape
    n = len(ls_)
    sel = (lax.broadcasted_iota(jnp.int32, (d, LANES), 0) // HEAD_DIM == lax.broadcasted_iota(jnp.int32, (d, LANES), 1)).astype(F32)

    def body(do_ref, o_ref, *rest):
        l_refs, sel_ref, outs = rest[:n], rest[n], rest[n + 1:]
        ws = _mix_weights(l_refs)
        dov = do_ref[...]
        r = jnp.dot(dov * o_ref[...], sel_ref[...], precision=lax.Precision.HIGHEST, preferred_element_type=F32)
        for g in range(n):
            outs[2 * g + 1][...] = ws[g] * r
            for j in range(d // LANES):
                cols = slice(j * LANES, (j + 1) * LANES)
                outs[2 * g][:, cols] = (_expand_heads(ws[g], j) * dov[:, cols]).astype(outs[2 * g].dtype)

    outs = []
    for _ in range(n):
        outs += [("blk", (s, d), MXU_DTYPE), ("blk", (s, LANES), F32)]
    res = _rows(body, s, ROW_TILE, [("blk", do), ("blk", o)] + [("blk", l) for l in ls_] + [("all", sel)], outs, name)
    return [(res[2 * g], res[2 * g + 1]) for g in range(n)]


def _ada_fwd(c_all, w, b, name):
    nsub, d, cs = w.shape

    def body(c_ref, w_ref, b_ref, o_ref):
        cv = c_ref[...]
        sc = cv * (1.0 / (1.0 + jnp.exp(-cv)))
        o_ref[...] = _dot_nn(sc, w_ref[...]) + b_ref[...]

    return pl.pallas_call(
        body,
        grid=(nsub,),
        in_specs=[pl.BlockSpec(c_all.shape, lambda i: (0, 0)), pl.BlockSpec((None, d, cs), lambda i: (i, 0, 0)),
                  pl.BlockSpec((None, 1, cs), lambda i: (i, 0, 0))],
        out_specs=pl.BlockSpec((None, N_DEV, cs), lambda i: (i, 0, 0)),
        out_shape=jax.ShapeDtypeStruct((nsub, N_DEV, cs), F32),
        name=name,
        compiler_params=_cparams(("parallel",)),
    )(c_all, w, b)


def _ada_bwd(c_all_t, dm, name):
    d, nb = c_all_t.shape
    nsub, _, cs = dm.shape

    def body(c_ref, dm_ref, o_ref):
        cv = c_ref[...]
        sc = cv * (1.0 / (1.0 + jnp.exp(-cv)))
        acc = sc[:, 0:1] * dm_ref[0:1, :]
        for bi in range(1, nb):
            acc = acc + sc[:, bi:bi + 1] * dm_ref[bi:bi + 1, :]
        o_ref[...] = acc

    return pl.pallas_call(
        body,
        grid=(nsub,),
        in_specs=[pl.BlockSpec(c_all_t.shape, lambda i: (0, 0)), pl.BlockSpec((None, nb, cs), lambda i: (i, 0, 0))],
        out_specs=pl.BlockSpec((None, d, cs), lambda i: (i, 0, 0)),
        out_shape=jax.ShapeDtypeStruct((nsub, d, cs), F32),
        name=name,
        compiler_params=_cparams(("parallel",)),
    )(c_all_t, dm)


def _row_tile(r, row_elems, block_elems=256 * 1024):
    t = 2 * SUBLANES
    if r % t:
        return r
    while t * 2 * row_elems <= block_elems and r % (t * 2) == 0:
        t *= 2
    return t


def _adamw(w, g, m, v, name):
    shape = w.shape
    c = shape[-1]
    r = w.size // c
    tr = _row_tile(r, c)
    w2, g2, m2, v2 = [a.reshape(r, c) for a in (w, g, m, v)]
    bc1 = 1.0 - ADAM_B1 ** ADAM_STEP
    bc2 = 1.0 - ADAM_B2 ** ADAM_STEP

    def body(w_ref, g_ref, m_ref, v_ref, d_ref, nm_ref, nv_ref):
        gv = g_ref[...]
        nm = ADAM_B1 * m_ref[...] + (1.0 - ADAM_B1) * gv
        nv = ADAM_B2 * v_ref[...] + (1.0 - ADAM_B2) * (gv * gv)
        d_ref[...] = -ADAM_LR * ((nm / bc1) / (jnp.sqrt(nv / bc2) + ADAM_EPS) + ADAM_WD * w_ref[...])
        nm_ref[...] = nm
        nv_ref[...] = nv

    res = _rows(body, r, tr, [("blk", a) for a in (w2, g2, m2, v2)], [("blk", (r, c), F32)] * 3, name)
    return [a.reshape(shape) for a in res]


def _sum_slots(buf, name):
    n, r, c = buf.shape
    tr = _row_tile(r, n * c, 2 * 1024 * 1024)

    def body(b_ref, o_ref):
        acc = b_ref[0].astype(F32)
        for k in range(1, n):
            acc = acc + b_ref[k].astype(F32)
        o_ref[...] = acc

    return pl.pallas_call(
        body,
        grid=(r // tr,),
        in_specs=[pl.BlockSpec((n, tr, c), lambda i: (0, i, 0))],
        out_specs=pl.BlockSpec((tr, c), lambda i: (i, 0)),
        out_shape=jax.ShapeDtypeStruct((r, c), F32),
        name=name,
        compiler_params=_cparams(("parallel",)),
    )(buf)


def _me():
    return lax.axis_index("x"), lax.axis_index("y"), lax.axis_index("c")


def _all_gather_small(blk, name, after=()):
    m_per, n = blk.shape

    def body(x_ref, *rest):
        out_ref, send_sems, recv_sems, local_sem = rest[len(after):]
        x, y, c = _me()
        me, sibling = (x, y, c), (x, y, 1 - c)
        chips = [(1 - x, y), (x, 1 - y), (1 - x, 1 - y)]

        def rows(px, py, pc):
            return out_ref.at[pl.ds((4 * px + 2 * py + pc) * m_per, m_per), :]

        def copy(k, block, to, src=None):
            return pltpu.make_async_remote_copy(
                src_ref=rows(*block) if src is None else src, dst_ref=rows(*block),
                send_sem=send_sems.at[k], recv_sem=recv_sems.at[k], device_id=to, device_id_type=MESH)

        mine = pltpu.make_async_copy(x_ref, rows(*me), local_sem)
        mine.start()
        first = [copy(0, me, sibling, src=x_ref)]
        first += [copy(1 + j, me, (*chip, c), src=x_ref) for j, chip in enumerate(chips)]
        for cp in first:
            cp.start()
        passed = [copy(4 + j, (*chip, c), sibling) for j, chip in enumerate(chips)]
        for j, chip in enumerate(chips):
            copy(1 + j, (*chip, c), me).wait_recv()
            passed[j].start()
        copy(0, sibling, me).wait_recv()
        for j, chip in enumerate(chips):
            copy(4 + j, (*chip, 1 - c), me).wait_recv()
        for cp in first + passed:
            cp.wait_send()
        mine.wait()

    return pl.pallas_call(
        body,
        out_shape=jax.ShapeDtypeStruct((N_DEV * m_per, n), blk.dtype),
        in_specs=[pl.BlockSpec(memory_space=pltpu.VMEM)] + [pl.BlockSpec(memory_space=pl.ANY)] * len(after),
        out_specs=pl.BlockSpec(memory_space=pltpu.VMEM),
        scratch_shapes=[pltpu.SemaphoreType.DMA((7,)), pltpu.SemaphoreType.DMA((7,)), pltpu.SemaphoreType.DMA],
        name=name,
        compiler_params=pltpu.CompilerParams(vmem_limit_bytes=VMEM_LIMIT),
    )(blk, *after)


_HBM = pl.BlockSpec(memory_space=pltpu.HBM)
_SEM = pl.BlockSpec(memory_space=pltpu.SEMAPHORE)
_EFFECT = pltpu.SideEffectType.DATAFLOW_SIDE_EFFECTING


def _other_chips(x, y):
    return [(1 - x, y), (x, 1 - y), (1 - x, 1 - y)]


def _gather_copy(w, j, src_ref, land_ref, send_sems, recv_sems, halved=False):
    x, y, c = _me()
    if halved:
        half = src_ref.shape[0] // 2
        src_ref = src_ref.at[pl.ds(c * half, half), :]
    return pltpu.make_async_remote_copy(
        src_ref=src_ref, dst_ref=land_ref.at[2 * x + y], send_sem=send_sems.at[3 * w + j], recv_sem=recv_sems.at[3 * w + j],
        device_id=(*_other_chips(x, y)[j], c), device_id_type=MESH)


def _gather_start(shards, halved, after, name):
    n = len(shards)
    lands = [lax.empty((N_CHIPS, s.shape[0] // 2 if w in halved else s.shape[0], s.shape[1]), s.dtype) for w, s in enumerate(shards)]

    def body(*refs):
        in_refs, land_refs = refs[:n], refs[n:2 * n]
        send_sems, recv_sems = refs[2 * n + 1], refs[2 * n + 2]
        token = refs[-1]
        for w in range(n):
            for j in range(3):
                _gather_copy(w, j, in_refs[w], land_refs[w], send_sems, recv_sems, w in halved).start()
        token[...] = jnp.zeros_like(token)

    res = pl.pallas_call(
        body,
        out_shape=(pltpu.SemaphoreType.DMA((3 * n,)), pltpu.SemaphoreType.DMA((3 * n,)),
                   *[pltpu.HBM(s.shape, s.dtype) for s in shards], *[pltpu.HBM(l.shape, l.dtype) for l in lands],
                   jax.ShapeDtypeStruct((SUBLANES, LANES), F32)),
        in_specs=[_HBM] * (2 * n) + [pl.BlockSpec(memory_space=pl.ANY)],
        out_specs=(_SEM, _SEM, *[_HBM] * (2 * n), pl.BlockSpec(memory_space=pltpu.VMEM)),
        input_output_aliases={i: 2 + i for i in range(2 * n)},
        name=name,
        compiler_params=pltpu.CompilerParams(has_side_effects=_EFFECT),
    )(*[pltpu.with_memory_space_constraint(a, pltpu.HBM) for a in list(shards) + lands], after)
    return res[0], res[1], res[2:2 + n], res[2 + n:2 + 2 * n], res[-1]


def _gather_wait(w, shard, land, send_sems, recv_sems, after, name, halved=False):
    def body(s_ref, land_ref, send_sems, recv_sems, after_ref, s_out, land_out, stage):
        x, y, _ = _me()
        if not halved:
            pltpu.sync_copy(s_ref, stage)
            pltpu.sync_copy(stage, land_out.at[2 * x + y])
        for j in range(3):
            cp = _gather_copy(w, j, s_ref, land_ref, send_sems, recv_sems, halved)
            cp.wait_send()
            cp.wait_recv()

    return pl.pallas_call(
        body,
        out_shape=(pltpu.HBM(shard.shape, shard.dtype), pltpu.HBM(land.shape, land.dtype)),
        in_specs=(_HBM, _HBM, _SEM, _SEM, pl.BlockSpec(memory_space=pl.ANY)),
        out_specs=(_HBM, _HBM),
        input_output_aliases={0: 0, 1: 1},
        scratch_shapes=[pltpu.VMEM((SUBLANES, LANES) if halved else shard.shape, shard.dtype)],
        name=name,
        compiler_params=pltpu.CompilerParams(has_side_effects=_EFFECT, vmem_limit_bytes=VMEM_LIMIT),
    )(shard, land, send_sems, recv_sems, after)


def _assemble_halves(shard, land, name):
    half = land.shape[1]

    def body(s_ref, land_ref, out_ref, send_sems, recv_sems, local_sems):
        x, y, c = _me()
        own = pltpu.make_async_copy(s_ref, out_ref.at[2 * x + y], local_sems.at[3])
        own.start()
        cps = []
        for j, (ox, oy) in enumerate(_other_chips(x, y)):
            qj = 2 * ox + oy
            mine = out_ref.at[qj, pl.ds(c * half, half), :]
            lc = pltpu.make_async_copy(land_ref.at[qj], mine, local_sems.at[j])
            lc.start()
            rc = pltpu.make_async_remote_copy(
                src_ref=land_ref.at[qj], dst_ref=mine, send_sem=send_sems.at[j], recv_sem=recv_sems.at[j],
                device_id=(x, y, 1 - c), device_id_type=MESH)
            rc.start()
            cps.append((lc, rc))
        for lc, rc in cps:
            rc.wait_recv()
        for lc, rc in cps:
            rc.wait_send()
            lc.wait()
        own.wait()

    vmem = pl.BlockSpec(memory_space=pltpu.VMEM)
    return pl.pallas_call(
        body,
        out_shape=jax.ShapeDtypeStruct((N_CHIPS,) + shard.shape, shard.dtype),
        in_specs=[vmem, vmem],
        out_specs=vmem,
        scratch_shapes=[pltpu.SemaphoreType.DMA((3,)), pltpu.SemaphoreType.DMA((3,)), pltpu.SemaphoreType.DMA((4,))],
        name=name,
        compiler_params=pltpu.CompilerParams(vmem_limit_bytes=VMEM_LIMIT),
    )(shard, land)


def _piece_shape(shape, kind):
    k, nn = shape
    if kind == "all":
        return (k, nn)
    return (k // 2, nn // N_CHIPS) if kind == "col" else (k // N_CHIPS // 2, nn)


def _piece_of(g_ref, kind, tq, tc):
    pr, pc = _piece_shape(g_ref.shape, kind)
    if kind == "all":
        return g_ref
    if kind == "col":
        return g_ref.at[pl.ds(tc * pr, pr), pl.ds(tq * pc, pc)]
    return g_ref.at[pl.ds((2 * tq + tc) * pr, pr), :]


def _scatter_copy(w, r, kind, g_ref, land_ref, send_sems, recv_sems):
    x, y, c = _me()
    tx, ty, tc = (x + ((r >> 2) & 1)) % 2, (y + ((r >> 1) & 1)) % 2, (c + (r & 1)) % 2
    return pltpu.make_async_remote_copy(
        src_ref=_piece_of(g_ref, kind, 2 * tx + ty, tc), dst_ref=land_ref.at[4 * x + 2 * y + c],
        send_sem=send_sems.at[N_DEV * w + r], recv_sem=recv_sems.at[N_DEV * w + r], device_id=(tx, ty, tc), device_id_type=MESH)


def _scatter_start(gs, kinds, name):
    n = len(gs)
    pieces = [_piece_shape(g.shape, kind) for g, kind in zip(gs, kinds)]
    lands = [lax.empty((N_DEV,) + p, g.dtype) for p, g in zip(pieces, gs)]

    def body(*refs):
        g_refs, land_refs, send_sems, recv_sems = refs[:n], refs[n:2 * n], refs[2 * n], refs[2 * n + 1]
        land_outs, stages = refs[3 * n + 2:4 * n + 2], refs[4 * n + 2:]
        x, y, c = _me()
        for w in range(n):
            for r in range(1, N_DEV):
                _scatter_copy(w, r, kinds[w], g_refs[w], land_refs[w], send_sems, recv_sems).start()
        for w in range(n):
            pltpu.sync_copy(_piece_of(g_refs[w], kinds[w], 2 * x + y, c), stages[w])
            pltpu.sync_copy(stages[w], land_outs[w].at[4 * x + 2 * y + c])

    arrays = list(gs) + lands
    res = pl.pallas_call(
        body,
        out_shape=(pltpu.SemaphoreType.DMA((N_DEV * n,)), pltpu.SemaphoreType.DMA((N_DEV * n,)),
                   *[pltpu.HBM(a.shape, a.dtype) for a in arrays]),
        in_specs=[_HBM] * (2 * n),
        out_specs=(_SEM, _SEM, *[_HBM] * (2 * n)),
        input_output_aliases={i: 2 + i for i in range(2 * n)},
        scratch_shapes=[pltpu.VMEM(p, g.dtype) for p, g in zip(pieces, gs)],
        name=name,
        compiler_params=pltpu.CompilerParams(has_side_effects=_EFFECT, vmem_limit_bytes=VMEM_LIMIT),
    )(*[pltpu.with_memory_space_constraint(a, pltpu.HBM) for a in arrays])
    return res[0], res[1], res[2:2 + n], res[2 + n:]


def _scatter_wait(send_sems, recv_sems, gs, lands, kinds, after, name):
    n = len(gs)

    def body(*refs):
        g_refs, land_refs, send_sems, recv_sems = refs[:n], refs[n:2 * n], refs[2 * n], refs[2 * n + 1]
        for w in range(n):
            for r in range(1, N_DEV):
                cp = _scatter_copy(w, r, kinds[w], g_refs[w], land_refs[w], send_sems, recv_sems)
                cp.wait_send()
                cp.wait_recv()

    arrays = list(gs) + list(lands)
    return pl.pallas_call(
        body,
        out_shape=tuple(pltpu.HBM(a.shape, a.dtype) for a in arrays),
        in_specs=(*[_HBM] * (2 * n), _SEM, _SEM, pl.BlockSpec(memory_space=pl.ANY)),
        out_specs=tuple([_HBM] * (2 * n)),
        input_output_aliases={i: i for i in range(2 * n)},
        name=name,
        compiler_params=pltpu.CompilerParams(has_side_effects=_EFFECT),
    )(*arrays, send_sems, recv_sems, after)[n:]


def _swap_halves(halves, name):
    n = len(halves)

    def body(*refs):
        in_refs, out_refs = refs[:n], refs[n:2 * n]
        send_sems, recv_sems, local_sems = refs[2 * n:]
        x, y, c = _me()
        cps = []
        for w in range(n):
            lc = pltpu.make_async_copy(in_refs[w], out_refs[w].at[c], local_sems.at[w])
            lc.start()
            rc = pltpu.make_async_remote_copy(
                src_ref=in_refs[w], dst_ref=out_refs[w].at[c], send_sem=send_sems.at[w], recv_sem=recv_sems.at[w],
                device_id=(x, y, 1 - c), device_id_type=MESH)
            rc.start()
            cps.append((lc, rc))
        for lc, rc in cps:
            rc.wait_recv()
        for lc, rc in cps:
            rc.wait_send()
            lc.wait()

    vmem = pl.BlockSpec(memory_space=pltpu.VMEM)
    return pl.pallas_call(
        body,
        out_shape=[jax.ShapeDtypeStruct((2,) + h.shape, h.dtype) for h in halves],
        in_specs=[vmem] * n,
        out_specs=[vmem] * n,
        scratch_shapes=[pltpu.SemaphoreType.DMA((n,)), pltpu.SemaphoreType.DMA((n,)), pltpu.SemaphoreType.DMA((n,))],
        name=name,
        compiler_params=pltpu.CompilerParams(vmem_limit_bytes=VMEM_LIMIT),
    )(*halves)


def _to_streams(a, dil):
    if dil == 1:
        return a
    s, c = a.shape
    return a.reshape(s // dil, dil, c).transpose(1, 0, 2).reshape(s, c)


def _from_streams(a, dil):
    if dil == 1:
        return a
    s, c = a.shape
    return a.reshape(dil, s // dil, c).transpose(1, 0, 2).reshape(s, c)


def _mm_tiles(s):
    return min(s, 2048)


def _local_step(x0, target, mvec, ln_g, ln_b, small, fetch, emit, start):
    s, d = x0.shape
    tm = _mm_tiles(s)
    row = lambda v: v.reshape(1, -1)
    shift = [row(mvec[i, :d]) for i in range(4)]
    scale = [row(mvec[i, d:2 * d]) for i in range(4)]
    gate = [row(1.0 + mvec[i, 2 * d:]) for i in range(4)]
    lg = [row(ln_g[i]) for i in range(4)]
    lb = [row(ln_b[i]) for i in range(4)]
    mm = functools.partial(_mm, tm=tm)
    mm_w = functools.partial(_mm, tm=1024, tk=min(s, 2048), mode="tn")

    xs, ys, big = [x0], [], {}
    h0 = _mod(x0, scale[0], shift[0], start, "mod0")
    big["a_w_in"] = fetch("a_w_in", h0)
    uvpre = mm(h0, big["a_w_in"], mode="nn", name="a_in", outs=[F32], tn=512, tk=1024,
               epi=lambda r, bias: [r + bias], extras=[("row", small["a_b_in"])])
    gated = _spatial_fwd(uvpre, small["a_vn_g"], small["a_vn_b"], small["wc"], small["bias_full"], "a_spatial")
    big["a_w_out"] = fetch("a_w_out", gated)
    ys.append(mm(gated, big["a_w_out"], mode="nn", name="a_out", outs=[F32], tn=1024, tk=1024))
    x1, h1 = _resid_ln(xs[0], ys[0], gate[0], lg[0], lb[0], (scale[1], shift[1]), "ln0")
    xs.append(x1)
    relu2 = lambda r: [jnp.square(jnp.maximum(r, 0.0))]
    big["up0"] = fetch("up0", h1)
    r0 = mm(h1, big["up0"], mode="nn", name="up0", outs=[MXU_DTYPE], tn=1024, tk=1024, epi=relu2)
    big["down0"] = fetch("down0", r0)
    ys.append(mm(r0, big["down0"], mode="nn", name="down0", outs=[F32], tm=min(s, 1024), tn=1024, tk=2048))
    x2, h2 = _resid_ln(xs[1], ys[1], gate[1], lg[1], lb[1], (scale[2], shift[2]), "ln1")
    xs.append(x2)
    hg, qkvs, o_g, l_g, l_streams = [], [], [], [], []
    big["b_w_qkv"] = fetch("b_w_qkv", h2)
    for g, (_, dil) in enumerate(B_PATTERNS):
        hp = _to_streams(h2, dil)
        qkv = mm(hp, big["b_w_qkv"], mode="nn", name=f"qkv{g}", outs=[MXU_DTYPE], tn=768, tk=1024, b_col0=g * 3 * d, n_out=3 * d)
        og, lgv = _attn_fwd(qkv, small["slopes"], dil, f"attn_fwd{g}")
        hg.append(hp)
        qkvs.append(qkv)
        o_g.append(_from_streams(og, dil))
        l_g.append(_from_streams(lgv, dil))
        l_streams.append(lgv)
    o_mix = _combine_fwd(o_g, l_g, "combine")
    big["b_w_out"] = fetch("b_w_out", o_mix)
    ys.append(mm(o_mix, big["b_w_out"], mode="nn", name="b_out", outs=[F32], tn=1024, tk=1024))
    x3, h3 = _resid_ln(xs[2], ys[2], gate[2], lg[2], lb[2], (scale[3], shift[3]), "ln2")
    xs.append(x3)
    big["up1"] = fetch("up1", h3)
    r1 = mm(h3, big["up1"], mode="nn", name="up1", outs=[MXU_DTYPE], tn=1024, tk=1024, epi=relu2)
    big["down1"] = fetch("down1", r1)
    ys.append(mm(r1, big["down1"], mode="nn", name="down1", outs=[F32], tm=min(s, 1024), tn=1024, tk=2048))

    gb, red_ln, red_mod = {}, [None] * 4, [None] * 4

    def mlp_bwd(i, h, r, dyy):
        gb[f"down{i}"] = mm_w(r, dyy, name=f"g_down{i}", outs=[MXU_DTYPE], tn=1024)
        da = mm(dyy, big[f"down{i}"], mode="nt", name=f"d_down{i}", outs=[MXU_DTYPE], tn=1024, tk=1024,
                after=emit(f"down{i}", gb[f"down{i}"]),
                epi=lambda acc, rv: [acc * (2.0 * jnp.sqrt(rv.astype(F32)))], extras=[("full", r)])
        gb[f"up{i}"] = mm_w(h, da, name=f"g_up{i}", outs=[MXU_DTYPE], tn=1024)
        return [mm(da, big[f"up{i}"], mode="nt", name=f"d_up{i}", outs=[F32], tn=1024, tk=1024, after=emit(f"up{i}", gb[f"up{i}"]))]

    def join(sub, dxr, dhs, after=None):
        res = _mod_ln_bwd(dxr, dhs, xs[sub], scale[sub], xs[sub - 1], ys[sub - 1], gate[sub - 1], lg[sub - 1],
                          f"mod_ln_bwd{sub}", after=after)
        red_mod[sub], red_ln[sub - 1] = res[2], res[3]
        return res[0], res[1]

    loss, dxr, dyy, red_ln[3] = _last_ln_loss_bwd(xs[3], ys[3], gate[3], lg[3], lb[3], target, "ln3_loss_bwd")
    dxr, dyy = join(3, dxr, mlp_bwd(1, h3, r1, dyy))
    gb["b_w_out"] = mm_w(o_mix, dyy, name="g_b_out", outs=[MXU_DTYPE], tn=1024, tk=1024)
    do = mm(dyy, big["b_w_out"], mode="nt", name="d_b_out", outs=[F32], tn=1024, tk=1024, after=emit("b_w_out", gb["b_w_out"]))
    parts = _combine_bwd(do, o_mix, l_g, "combine_bwd")
    dhs, gq = [], None
    for g, (_, dil) in enumerate(B_PATTERNS):
        do_g, dd_g = _to_streams(parts[g][0], dil), _to_streams(parts[g][1], dil)
        dqkv = _attn_bwd(qkvs[g], do_g, l_streams[g], dd_g, small["slopes"], dil, f"attn_bwd{g}")
        gq = mm_w(hg[g], dqkv, name=f"g_qkv{g}", outs=[MXU_DTYPE], tn=1024, out_col0=g * 3 * d, out_cols=len(B_PATTERNS) * 3 * d, into=gq)
        dh = mm(dqkv, big["b_w_qkv"], mode="nt", name=f"d_qkv{g}", outs=[F32], tn=1024, tk=768, b_col0=g * 3 * d)
        dhs.append(_from_streams(dh, dil))
    gb["b_w_qkv"] = gq
    dxr, dyy = join(2, dxr, dhs, after=emit("b_w_qkv", gb["b_w_qkv"]))
    dxr, dyy = join(1, dxr, mlp_bwd(0, h1, r0, dyy))
    gb["a_w_out"] = mm_w(gated, dyy, name="g_a_out", outs=[MXU_DTYPE], tn=1024)
    dgated = mm(dyy, big["a_w_out"], mode="nt", name="d_a_out", outs=[F32], tn=1024, tk=1024, after=emit("a_w_out", gb["a_w_out"]))
    duv, dws, dbias, dbin, dvg, dvb = _spatial_bwd(uvpre, dgated, small["a_vn_g"], small["a_vn_b"], small["wc"],
                                                   small["wct"], small["bias_full"], "a_spatial_bwd")
    tril = jnp.tril(jnp.ones((CHUNK, CHUNK), bool))
    dws = jnp.where(tril, dws, 0.0).reshape(-1, LANES)
    gb["a_w_in"] = mm_w(h0, duv, name="g_a_in", outs=[MXU_DTYPE], tn=1024, after=emit("a_w_s", dws.astype(MXU_DTYPE)))
    dh = mm(duv, big["a_w_in"], mode="nt", name="d_a_in", outs=[F32], tn=1024, tk=512, after=emit("a_w_in", gb["a_w_in"]))
    dx, red_mod[0] = _mod_bwd(dxr, [dh], xs[0], scale[0], "mod_bwd0")
    dm = [jnp.concatenate([red_mod[i][0], red_mod[i][1], red_ln[i][2]]) for i in range(4)]
    dlg, dlb = [red_ln[i][0] for i in range(4)], [red_ln[i][1] for i in range(4)]

    gsmall = {
        "a_b_in": dbin.reshape(-1), "a_vn_g": dvg.reshape(-1), "a_vn_b": dvb.reshape(-1),
        "a_w_s": dws.reshape(-1),
        "a_b_s": dbias.reshape(CHUNK, A_GROUPS, d // A_GROUPS).sum(-1).T.reshape(-1),
    }
    return loss, dx, gb, jnp.stack(dm), jnp.stack(dlg), jnp.stack(dlb), gsmall


BIG = ("a_w_in", "a_w_out", "up0", "down0", "b_w_qkv", "b_w_out", "up1", "down1")
BIG_KIND = {"a_w_in": "col", "a_w_out": "row", "b_w_qkv": "col", "b_w_out": "row",
            "up0": "col", "up1": "col", "down0": "row", "down1": "row", "a_w_s": "all"}
HALVED = ("a_w_in", "down0", "b_w_qkv")
SCATTER_GROUPS = (("down1", "up1"), ("b_w_out", "b_w_qkv"), ("down0", "up0"), ("a_w_out", "a_w_in"), ("a_w_s",))
SMALL = ("a_b_in", "a_vn_g", "a_vn_b", "a_b_s")


def kernel(x, c, ada_w, ada_b, ln_g, ln_b, a_w_in, a_b_in, a_vn_g, a_vn_b, a_w_s, a_b_s, a_w_out, b_w_qkv, b_w_out, mlp_w_up, mlp_w_down, loss_target, m_ada_w, m_ada_b, m_ln_g, m_ln_b, m_a_w_in, m_a_b_in, m_a_vn_g, m_a_vn_b, m_a_w_s, m_a_b_s, m_a_w_out, m_b_w_qkv, m_b_w_out, m_mlp_w_up, m_mlp_w_down, v_ada_w, v_ada_b, v_ln_g, v_ln_b, v_a_w_in, v_a_b_in, v_a_vn_g, v_a_vn_b, v_a_w_s, v_a_b_s, v_a_w_out, v_b_w_qkv, v_b_w_out, v_mlp_w_up, v_mlp_w_down):
    s, d = x.shape[1], x.shape[2]
    xi, yi, ci = _me()
    q = 2 * xi + yi
    dev = 2 * q + ci
    nsub = 2 * DEPTH
    cs = ada_w.shape[-1]
    ls = ln_g.shape[-1]

    shards = {
        "a_w_in": a_w_in[0], "a_w_out": a_w_out[0], "b_w_qkv": b_w_qkv[0], "b_w_out": b_w_out[0],
        "up0": mlp_w_up[0], "up1": mlp_w_up[1], "down0": mlp_w_down[0], "down1": mlp_w_down[1],
    }
    cast = [shards[k].astype(MXU_DTYPE) for k in BIG]

    pack = jnp.concatenate([c.reshape(-1), ln_g.reshape(-1), ln_b.reshape(-1)]).reshape(-1, LANES)
    got = _all_gather_small(pack, "gather_small", after=cast).reshape(N_DEV, -1)
    c_all = got[:, :d]
    per_chip = got[0::2]
    ln_g_full = per_chip[:, d:d + nsub * ls].reshape(N_CHIPS, nsub, ls).transpose(1, 0, 2).reshape(nsub, d)
    ln_b_full = per_chip[:, d + nsub * ls:].reshape(N_CHIPS, nsub, ls).transpose(1, 0, 2).reshape(nsub, d)
    m_part = _ada_fwd(c_all, ada_w.reshape(nsub, d, cs), ada_b.reshape(nsub, 1, cs), "ada_fwd")
    m_all = _all_gather_small(m_part.reshape(-1, LANES), "gather_mod").reshape(N_DEV, nsub, N_DEV, cs)
    m_mine = lax.dynamic_index_in_dim(m_all[0::2], dev, axis=2, keepdims=False)
    mvec = m_mine.transpose(1, 0, 2).reshape(nsub, 3 * d)

    halved = {BIG.index(k) for k in HALVED}
    send_sems, recv_sems, shard_thru, lands, token = _gather_start(cast, halved, mvec, "gather_start")

    def fetch(k, after):
        w = BIG.index(k)
        shard, gw = _gather_wait(w, shard_thru[w], lands[w], send_sems, recv_sems, after, f"gather_wait_{k}", w in halved)
        if w in halved:
            gw = _assemble_halves(shard, gw, f"assemble_{k}")
        return gw if BIG_KIND[k] == "col" else gw.reshape(1, -1, gw.shape[-1])

    scattering, pending = {}, {}

    def emit(k, g):
        pending[k] = g
        group = next(gr for gr in SCATTER_GROUPS if k in gr)
        if k != group[-1]:
            return None
        scattering[group] = _scatter_start([pending[m] for m in group], [BIG_KIND[m] for m in group], f"scatter_start_{k}")
        return scattering[group][2][0]

    tril = jnp.tril(jnp.ones((CHUNK, CHUNK), bool))
    wc = jnp.where(tril, a_w_s[0], 0.0).astype(MXU_DTYPE)
    heads = jnp.arange(1, B_HEADS + 1, dtype=F32)
    small = {
        "a_b_in": a_b_in, "a_vn_g": a_vn_g, "a_vn_b": a_vn_b,
        "wc": wc, "wct": wc.transpose(0, 2, 1),
        "bias_full": jnp.repeat(a_b_s[0].T, d // A_GROUPS, axis=1),
        "slopes": jnp.exp2(-8.0 * heads / B_HEADS),
    }

    loss_part, grad_x, gb, dm, dlg, dlb, gsmall = _local_step(x[0], loss_target[0], mvec, ln_g_full, ln_b_full, small, fetch, emit, token)
    loss = lax.psum(loss_part, ("x", "y", "c"))

    weights = dict(ada_w=ada_w, ada_b=ada_b, ln_g=ln_g, ln_b=ln_b, a_w_in=a_w_in, a_b_in=a_b_in, a_vn_g=a_vn_g, a_vn_b=a_vn_b,
                   a_w_s=a_w_s, a_b_s=a_b_s, a_w_out=a_w_out, b_w_qkv=b_w_qkv, b_w_out=b_w_out, mlp_w_up=mlp_w_up, mlp_w_down=mlp_w_down)
    ms = dict(ada_w=m_ada_w, ada_b=m_ada_b, ln_g=m_ln_g, ln_b=m_ln_b, a_w_in=m_a_w_in, a_b_in=m_a_b_in, a_vn_g=m_a_vn_g, a_vn_b=m_a_vn_b,
              a_w_s=m_a_w_s, a_b_s=m_a_b_s, a_w_out=m_a_w_out, b_w_qkv=m_b_w_qkv, b_w_out=m_b_w_out, mlp_w_up=m_mlp_w_up, mlp_w_down=m_mlp_w_down)
    vs = dict(ada_w=v_ada_w, ada_b=v_ada_b, ln_g=v_ln_g, ln_b=v_ln_b, a_w_in=v_a_w_in, a_b_in=v_a_b_in, a_vn_g=v_a_vn_g, a_vn_b=v_a_vn_b,
              a_w_s=v_a_w_s, a_b_s=v_a_b_s, a_w_out=v_a_w_out, b_w_qkv=v_b_w_qkv, b_w_out=v_b_w_out, mlp_w_up=v_mlp_w_up, mlp_w_down=v_mlp_w_down)
    grads, updates = {}, {}

    def update(k):
        updates[k] = _adamw(weights[k], grads[k], ms[k], vs[k], f"adamw_{k}")
        return updates[k][0]

    pack_b = jnp.concatenate([dm.reshape(-1), dlg.reshape(-1), dlb.reshape(-1)] + [gsmall[k] for k in SMALL])
    n_small = pack_b.shape[0]
    pack_b = jnp.pad(pack_b, (0, -n_small % (256 * LANES)))
    got_b = _all_gather_small(pack_b.reshape(-1, LANES), "gather_small_grads").reshape(N_DEV, -1, LANES)
    tot = _sum_slots(got_b, "sum_small").reshape(-1)
    o = 0
    dm_tot = tot[o:o + nsub * 3 * d].reshape(nsub, 3 * d); o += nsub * 3 * d
    dlg_tot = tot[o:o + nsub * d].reshape(nsub, d); o += nsub * d
    dlb_tot = tot[o:o + nsub * d].reshape(nsub, d); o += nsub * d
    g_small = {}
    for k, ref in zip(SMALL, (a_b_in, a_vn_g, a_vn_b, a_b_s)):
        g_small[k] = tot[o:o + ref.size].reshape(ref.shape); o += ref.size
    assert o == n_small
    aws = _scatter_wait(*scattering[("a_w_s",)], ["all"], tot, "scatter_wait_a_w_s")[0]
    g_small["a_w_s"] = _sum_slots(aws, "sum_a_w_s").reshape(a_w_s.shape)
    dm_all = got_b.reshape(N_DEV, -1)[:, :nsub * 3 * d].reshape(N_DEV, nsub, 3 * d)
    dm_cols = lax.dynamic_slice_in_dim(dm_all, q * cs, cs, axis=2).transpose(1, 0, 2)

    grads.update({
        "ada_w": _ada_bwd(c_all.T, dm_cols, "ada_bwd").reshape(ada_w.shape),
        "ada_b": lax.dynamic_slice_in_dim(dm_tot, q * cs, cs, axis=1).reshape(ada_b.shape),
        "ln_g": lax.dynamic_slice_in_dim(dlg_tot, q * ls, ls, axis=1).reshape(ln_g.shape),
        "ln_b": lax.dynamic_slice_in_dim(dlb_tot, q * ls, ls, axis=1).reshape(ln_b.shape),
        **g_small,
    })
    for k in ("ada_b", "ln_g", "ln_b", "a_w_s") + SMALL:
        update(k)
    done = update("ada_w")

    gfull = {}
    for group in (SCATTER_GROUPS[0] + SCATTER_GROUPS[1], SCATTER_GROUPS[2] + SCATTER_GROUPS[3]):
        bufs = []
        for pair in (group[:2], group[2:]):
            bufs += _scatter_wait(*scattering[pair], [BIG_KIND[m] for m in pair], done, f"scatter_wait_{pair[-1]}")
        halves = [_sum_slots(b, f"sum_{k}") for k, b in zip(group, bufs)]
        fulls = _swap_halves(halves, f"swap_halves_{group[0]}")
        gfull.update({k: f.reshape(-1, f.shape[-1]) for k, f in zip(group, fulls)})
        if group[0] == "down1":
            grads["b_w_qkv"], grads["b_w_out"] = gfull["b_w_qkv"][None], gfull["b_w_out"][None]
            update("b_w_out")
            done = update("b_w_qkv")
    grads.update({
        "a_w_in": gfull["a_w_in"][None], "a_w_out": gfull["a_w_out"][None],
        "mlp_w_up": jnp.stack([gfull["up0"], gfull["up1"]]), "mlp_w_down": jnp.stack([gfull["down0"], gfull["down1"]]),
    })
    for k in ("a_w_in", "a_w_out", "mlp_w_up", "mlp_w_down"):
        update(k)
    names = list(weights)
    return (loss, grad_x[None], *[grads[k] for k in names], *[updates[k][0] for k in names],
            *[updates[k][1] for k in names], *[updates[k][2] for k in names])
```

```python
import functools
import math

import jax
import jax.numpy as jnp
from jax import lax
from jax.experimental import pallas as pl
from jax.experimental.pallas import tpu as pltpu

F32 = jnp.float32
MXU_DTYPE = jnp.bfloat16

DEPTH = 2
CHUNK = 128
A_GROUPS = 16
B_HEADS = 16
HEAD_DIM = 64
B_PATTERNS = ((128, 1), (512, 4), (2048, 16))
SPAN = 128
ALPHA = (2 * DEPTH) ** 0.25
LN_EPS = 1e-5
NEG = -1e30
ATT_SCALE = HEAD_DIM ** -0.5
ADAM_LR, ADAM_B1, ADAM_B2, ADAM_EPS, ADAM_WD, ADAM_STEP = 0.001, 0.9, 0.999, 1e-08, 0.01, 10

N_CHIPS = 4
N_DEV = 8
LANES = 128
SUBLANES = 8
VMEM_LIMIT = 52 * 1024 * 1024
ROW_TILE = 512
MM_ROW_CHUNK = 256
MESH = pl.DeviceIdType.MESH


def _cparams(sem):
    return pltpu.CompilerParams(dimension_semantics=sem, vmem_limit_bytes=VMEM_LIMIT)


def _fold8(v):
    r, c = v.shape
    return jnp.sum(v.reshape(r // SUBLANES, SUBLANES, c), axis=0)


def _gelu(x):
    c = math.sqrt(2.0 / math.pi)
    return 0.5 * x * (1.0 + jnp.tanh(c * (x + 0.044715 * (x * x * x))))


def _gelu_grad(x):
    c = math.sqrt(2.0 / math.pi)
    t = jnp.tanh(c * (x + 0.044715 * (x * x * x)))
    return 0.5 * (1.0 + t) + 0.5 * x * (1.0 - t * t) * c * (1.0 + 3.0 * 0.044715 * x * x)


def _dot(a, b, dims):
    return lax.dot_general(a.astype(MXU_DTYPE), b.astype(MXU_DTYPE), (dims, ((), ())), preferred_element_type=F32)


def _dot_nn(a, b):
    return _dot(a, b, ((1,), (0,)))


def _dot_nt(a, b):
    return _dot(a, b, ((1,), (1,)))


def _dot_tn(a, b):
    return _dot(a, b, ((0,), (0,)))


def _mm(a, b, *, mode, name, outs, tm, tn, tk, epi=None, extras=(), b_col0=0, n_out=None, after=None,
        out_col0=0, out_cols=None, into=None):
    if mode == "nn":
        m, kdim = a.shape
        p, kb, ns = b.shape
        assert kb == kdim and ns % tn == 0 and b_col0 % tn == 0
        n = n_out if n_out is not None else p * ns
        npt, j0 = ns // tn, b_col0 // tn
        a_spec = pl.BlockSpec((tm, tk), lambda i, j, k: (i, k))
        b_spec = pl.BlockSpec((None, tk, tn), lambda i, j, k: ((j + j0) // npt, k, (j + j0) % npt))
        dot = _dot_nn
    elif mode == "nt":
        m, kdim = a.shape
        p, n, ns = b.shape
        assert ns % tk == 0 and b_col0 % tk == 0
        npt, j0 = ns // tk, b_col0 // tk
        a_spec = pl.BlockSpec((tm, tk), lambda i, j, k: (i, k))
        b_spec = pl.BlockSpec((None, tn, tk), lambda i, j, k: ((k + j0) // npt, j, (k + j0) % npt))
        dot = _dot_nt
    else:
        kdim, m = a.shape
        kb, n = b.shape
        assert kb == kdim
        a_spec = pl.BlockSpec((tk, tm), lambda i, j, k: (k, i))
        b_spec = pl.BlockSpec((tk, tn), lambda i, j, k: (k, j))
        dot = _dot_tn
    assert m % tm == 0 and n % tn == 0 and kdim % tk == 0, (name, m, n, kdim, tm, tn, tk)
    nk = kdim // tk
    ex_specs, ex_arrays = [], []
    for kind, arr in extras:
        if kind == "row":
            ex_specs.append(pl.BlockSpec((1, tn), lambda i, j, k: (0, j)))
        else:
            ex_specs.append(pl.BlockSpec((tm, tn), lambda i, j, k: (i, j)))
        ex_arrays.append(arr)
    n_ex, n_o = len(ex_arrays), len(outs)
    deps = [d for d in (after, into) if d is not None]
    n_dep = len(deps)
    j_out = out_col0 // tn
    assert out_col0 % tn == 0 and (into is None or len(outs) == 1)

    def body(a_ref, b_ref, *rest):
        ex_refs, o_refs = rest[:n_ex], rest[n_ex + n_dep:n_ex + n_dep + n_o]
        k = pl.program_id(2)

        chunks = [slice(r0, r0 + min(tm, MM_ROW_CHUNK)) for r0 in range(0, tm, min(tm, MM_ROW_CHUNK))]

        def part(rows):
            return dot(a_ref[:, rows] if mode == "tn" else a_ref[rows, :], b_ref[...])

        def finish(r, rows):
            exs = [e[...] if kind == "row" else e[rows, :] for (kind, _), e in zip(extras, ex_refs)]
            vals = epi(r, *exs) if epi is not None else [r]
            for o, v in zip(o_refs, vals):
                o[rows, :] = v.astype(o.dtype)

        if nk == 1:
            for rows in chunks:
                finish(part(rows), rows)
            return
        acc = rest[n_ex + n_dep + n_o]

        @pl.when(k == 0)
        def _():
            for rows in chunks:
                acc[rows, :] = part(rows)

        @pl.when((k > 0) & (k < nk - 1))
        def _():
            for rows in chunks:
                acc[rows, :] += part(rows)

        @pl.when(k == nk - 1)
        def _():
            for rows in chunks:
                finish(acc[rows, :] + part(rows), rows)

    res = pl.pallas_call(
        body,
        grid=(m // tm, n // tn, nk),
        in_specs=[a_spec, b_spec] + ex_specs + [pl.BlockSpec(memory_space=pl.ANY)] * n_dep,
        out_specs=[pl.BlockSpec((tm, tn), lambda i, j, k: (i, j + j_out)) for _ in outs],
        out_shape=[jax.ShapeDtypeStruct((m, out_cols or n), dt) for dt in outs],
        input_output_aliases={} if into is None else {2 + n_ex + n_dep - 1: 0},
        scratch_shapes=[pltpu.VMEM((tm, tn), F32)] if nk > 1 else [],
        name=name,
        compiler_params=_cparams(("parallel", "parallel", "arbitrary")),
    )(a, b, *ex_arrays, *deps)
    return res if len(outs) > 1 else res[0]


def _rows(body, n_rows, tr, ins, outs, name, scratch=()):
    def spec(kind, shape):
        if kind == "blk":
            return pl.BlockSpec((tr,) + tuple(shape[1:]), lambda i: (i,) + (0,) * (len(shape) - 1))
        if kind == "dep":
            return pl.BlockSpec(memory_space=pl.ANY)
        return pl.BlockSpec(tuple(shape), lambda i: (0,) * len(shape))

    return pl.pallas_call(
        body,
        grid=(n_rows // tr,),
        in_specs=[spec(k, a.shape) for k, a in ins],
        out_specs=[spec(k, s) for k, s, _ in outs],
        out_shape=[jax.ShapeDtypeStruct(tuple(s), d) for _, s, d in outs],
        scratch_shapes=list(scratch),
        name=name,
        compiler_params=_cparams(("arbitrary",)),
    )(*[a for _, a in ins])


def _ln_stats(z):
    mu = jnp.mean(z, axis=-1, keepdims=True)
    zc = z - mu
    var = jnp.mean(zc * zc, axis=-1, keepdims=True)
    rstd = lax.rsqrt(var + LN_EPS)
    return zc * rstd, rstd


def _mod(x, scale, shift, after, name):
    s, d = x.shape

    def body(x_ref, sc_ref, sh_ref, dep_ref, h_ref):
        h_ref[...] = (x_ref[...] * (1.0 + sc_ref[...]) + sh_ref[...]).astype(h_ref.dtype)

    return _rows(body, s, ROW_TILE, [("blk", x), ("all", scale), ("all", shift), ("dep", after)], [("blk", (s, d), MXU_DTYPE)], name)[0]


def _resid_ln(x, y, gate, g, b, nxt, name):
    s, d = x.shape

    def body(x_ref, y_ref, gate_ref, g_ref, b_ref, sc_ref, sh_ref, xn_ref, h_ref):
        z = ALPHA * x_ref[...] + gate_ref[...] * y_ref[...]
        xhat, _ = _ln_stats(z)
        xn = xhat * g_ref[...] + b_ref[...]
        xn_ref[...] = xn
        h_ref[...] = (xn * (1.0 + sc_ref[...]) + sh_ref[...]).astype(h_ref.dtype)

    return _rows(body, s, ROW_TILE,
                 [("blk", x), ("blk", y), ("all", gate), ("all", g), ("all", b), ("all", nxt[0]), ("all", nxt[1])],
                 [("blk", (s, d), F32), ("blk", (s, d), MXU_DTYPE)], name)


def _mod_bwd(dxr, dhs, x, scale, name, after=None):
    s, d = x.shape
    n_dh = len(dhs)
    n_dep = 0 if after is None else 1

    def body(dxr_ref, *rest):
        dh_refs = rest[:n_dh]
        x_ref, sc_ref, dx_ref, red_ref, a_sh, a_sc = rest[n_dh:n_dh + 2] + rest[n_dh + 2 + n_dep:]
        i = pl.program_id(0)

        @pl.when(i == 0)
        def _():
            a_sh[...] = jnp.zeros_like(a_sh)
            a_sc[...] = jnp.zeros_like(a_sc)

        dh = dh_refs[0][...]
        for r in dh_refs[1:]:
            dh = dh + r[...]
        dx_ref[...] = dxr_ref[...] + dh * (1.0 + sc_ref[...])
        a_sh[...] += _fold8(dh)
        a_sc[...] += _fold8(dh * x_ref[...])

        @pl.when(i == pl.num_programs(0) - 1)
        def _():
            red_ref[...] = jnp.zeros_like(red_ref)
            red_ref[0:1, :] = jnp.sum(a_sh[...], axis=0, keepdims=True)
            red_ref[1:2, :] = jnp.sum(a_sc[...], axis=0, keepdims=True)

    return _rows(body, s, ROW_TILE, [("blk", dxr)] + [("blk", h) for h in dhs] + [("blk", x), ("all", scale)] + [("dep", after)] * n_dep,
                 [("blk", (s, d), F32), ("all", (SUBLANES, d), F32)], name,
                 scratch=[pltpu.VMEM((SUBLANES, d), F32)] * 2)


def _last_ln_loss_bwd(x, y, gate, g, b, target, name):
    s, d = x.shape

    def body(x_ref, y_ref, gate_ref, g_ref, b_ref, t_ref, l_ref, dxr_ref, dyy_ref, red_ref, a_l, a_g, a_b, a_gate):
        i = pl.program_id(0)

        @pl.when(i == 0)
        def _():
            for a in (a_l, a_g, a_b, a_gate):
                a[...] = jnp.zeros_like(a)

        yv = y_ref[...]
        z = ALPHA * x_ref[...] + gate_ref[...] * yv
        xhat, rstd = _ln_stats(z)
        e = xhat * g_ref[...] + b_ref[...] - t_ref[...]
        a_l[...] += _fold8(e * e)
        dxo_v = e * (1.0 / d)
        dxh = dxo_v * g_ref[...]
        dz = rstd * (dxh - jnp.mean(dxh, axis=-1, keepdims=True) - xhat * jnp.mean(dxh * xhat, axis=-1, keepdims=True))
        dxr_ref[...] = ALPHA * dz
        dyy_ref[...] = (gate_ref[...] * dz).astype(dyy_ref.dtype)
        a_g[...] += _fold8(dxo_v * xhat)
        a_b[...] += _fold8(dxo_v)
        a_gate[...] += _fold8(dz * yv)

        @pl.when(i == pl.num_programs(0) - 1)
        def _():
            l_ref[...] = jnp.full(l_ref.shape, 0.5 / d, F32) * jnp.sum(a_l[...])
            red_ref[...] = jnp.zeros_like(red_ref)
            red_ref[0:1, :] = jnp.sum(a_g[...], axis=0, keepdims=True)
            red_ref[1:2, :] = jnp.sum(a_b[...], axis=0, keepdims=True)
            red_ref[2:3, :] = jnp.sum(a_gate[...], axis=0, keepdims=True)

    l, dxr, dyy, red = _rows(
        body, s, ROW_TILE, [("blk", x), ("blk", y), ("all", gate), ("all", g), ("all", b), ("blk", target)],
        [("all", (SUBLANES, LANES), F32), ("blk", (s, d), F32), ("blk", (s, d), MXU_DTYPE), ("all", (SUBLANES, d), F32)], name,
        scratch=[pltpu.VMEM((SUBLANES, d), F32)] * 4)
    return l[0, 0], dxr, dyy, red


def _mod_ln_bwd(dxr, dhs, x, scale, x_in, y, gate, g, name, after=None):
    s, d = x.shape
    n_dh = len(dhs)
    n_dep = 0 if after is None else 1

    def body(dxr_ref, *rest):
        dh_refs = rest[:n_dh]
        x_ref, sc_ref, xin_ref, y_ref, gate_ref, g_ref = rest[n_dh:n_dh + 6]
        dxr_out, dyy_ref, red_mod, red_ln, a_sh, a_sc, a_g, a_b, a_gate = rest[n_dh + 6 + n_dep:]
        i = pl.program_id(0)

        @pl.when(i == 0)
        def _():
            for a in (a_sh, a_sc, a_g, a_b, a_gate):
                a[...] = jnp.zeros_like(a)

        dh = dh_refs[0][...]
        for r in dh_refs[1:]:
            dh = dh + r[...]
        xv = x_ref[...]
        dxo_v = dxr_ref[...] + dh * (1.0 + sc_ref[...])
        a_sh[...] += _fold8(dh)
        a_sc[...] += _fold8(dh * xv)
        yv = y_ref[...]
        z = ALPHA * xin_ref[...] + gate_ref[...] * yv
        xhat, rstd = _ln_stats(z)
        dxh = dxo_v * g_ref[...]
        dz = rstd * (dxh - jnp.mean(dxh, axis=-1, keepdims=True) - xhat * jnp.mean(dxh * xhat, axis=-1, keepdims=True))
        dxr_out[...] = ALPHA * dz
        dyy_ref[...] = (gate_ref[...] * dz).astype(dyy_ref.dtype)
        a_g[...] += _fold8(dxo_v * xhat)
        a_b[...] += _fold8(dxo_v)
        a_gate[...] += _fold8(dz * yv)

        @pl.when(i == pl.num_programs(0) - 1)
        def _():
            red_mod[...] = jnp.zeros_like(red_mod)
            red_mod[0:1, :] = jnp.sum(a_sh[...], axis=0, keepdims=True)
            red_mod[1:2, :] = jnp.sum(a_sc[...], axis=0, keepdims=True)
            red_ln[...] = jnp.zeros_like(red_ln)
            red_ln[0:1, :] = jnp.sum(a_g[...], axis=0, keepdims=True)
            red_ln[1:2, :] = jnp.sum(a_b[...], axis=0, keepdims=True)
            red_ln[2:3, :] = jnp.sum(a_gate[...], axis=0, keepdims=True)

    ins = ([("blk", dxr)] + [("blk", h) for h in dhs]
           + [("blk", x), ("all", scale), ("blk", x_in), ("blk", y), ("all", gate), ("all", g)] + [("dep", after)] * n_dep)
    return _rows(body, s, ROW_TILE, ins,
                 [("blk", (s, d), F32), ("blk", (s, d), MXU_DTYPE), ("all", (SUBLANES, d), F32), ("all", (SUBLANES, d), F32)], name,
                 scratch=[pltpu.VMEM((SUBLANES, d), F32)] * 5)


def _left_half(shape):
    return lax.broadcasted_iota(jnp.int32, shape, 1) < (LANES // 2)


CHUNKS_PER_STEP = 2


def _chunks_of_step():
    return [slice(i * CHUNK, (i + 1) * CHUNK) for i in range(CHUNKS_PER_STEP)]


def _spatial_z(vn, wc_ref, bias_ref, j):
    vb = vn[:, j * LANES:(j + 1) * LANES]
    z0 = _dot_nn(wc_ref[2 * j], vb)
    z1 = _dot_nn(wc_ref[2 * j + 1], vb)
    return jnp.where(_left_half(z0.shape), z0, z1) + bias_ref[:, j * LANES:(j + 1) * LANES]


def _spatial_fwd(uvpre, vn_g, vn_b, wc, bias_full, name):
    s, d2 = uvpre.shape
    d = d2 // 2

    def body(uv_ref, g_ref, b_ref, wc_ref, bias_ref, out_ref):
        for rows in _chunks_of_step():
            u = _gelu(uv_ref[rows, :d])
            v = _gelu(uv_ref[rows, d:])
            vh, _ = _ln_stats(v)
            vn = vh * g_ref[...] + b_ref[...]
            for j in range(d // LANES):
                z = _spatial_z(vn, wc_ref, bias_ref, j)
                out_ref[rows, j * LANES:(j + 1) * LANES] = (u[:, j * LANES:(j + 1) * LANES] * z).astype(out_ref.dtype)

    return _rows(body, s, CHUNKS_PER_STEP * CHUNK, [("blk", uvpre), ("all", vn_g), ("all", vn_b), ("all", wc), ("all", bias_full)],
                 [("blk", (s, d), MXU_DTYPE)], name)[0]


def _spatial_bwd(uvpre, dgated, vn_g, vn_b, wc, wct, bias_full, name):
    s, d2 = uvpre.shape
    d = d2 // 2

    def body(uv_ref, dg_ref, g_ref, b_ref, wc_ref, wct_ref, bias_ref,
             duv_ref, dws_ref, dbias_ref, dbin_ref, dvg_ref, dvb_ref, dvn_buf, a_bin, a_vg, a_vb):
        i = pl.program_id(0)

        @pl.when(i == 0)
        def _():
            dws_ref[...] = jnp.zeros_like(dws_ref)
            dbias_ref[...] = jnp.zeros_like(dbias_ref)
            a_bin[...] = jnp.zeros_like(a_bin)
            a_vg[...] = jnp.zeros_like(a_vg)
            a_vb[...] = jnp.zeros_like(a_vb)

        for rows in _chunks_of_step():
            up = uv_ref[rows, :d]
            vp = uv_ref[rows, d:]
            u = _gelu(up)
            v = _gelu(vp)
            vh, rstd = _ln_stats(v)
            vn = vh * g_ref[...] + b_ref[...]
            dg = dg_ref[rows, :]
            dzz = dg * u
            dbias_ref[...] += dzz
            for j in range(d // LANES):
                cols = slice(j * LANES, (j + 1) * LANES)
                z = _spatial_z(vn, wc_ref, bias_ref, j)
                dup = dg[:, cols] * z * _gelu_grad(up[:, cols])
                duv_ref[rows, cols] = dup.astype(duv_ref.dtype)
                a_bin[:, cols] += _fold8(dup)
                dzb = dzz[:, cols]
                left = _left_half(dzb.shape)
                dvn_buf[:, cols] = jnp.where(left, _dot_nn(wct_ref[2 * j], dzb), _dot_nn(wct_ref[2 * j + 1], dzb))
                vb = vn[:, cols]
                dws_ref[2 * j] += _dot_nt(jnp.where(left, dzb, 0.0), vb)
                dws_ref[2 * j + 1] += _dot_nt(jnp.where(left, 0.0, dzb), vb)
            dvn = dvn_buf[...]
            a_vg[...] += _fold8(dvn * vh)
            a_vb[...] += _fold8(dvn)
            dvh = dvn * g_ref[...]
            dv = rstd * (dvh - jnp.mean(dvh, axis=-1, keepdims=True) - vh * jnp.mean(dvh * vh, axis=-1, keepdims=True))
            dvp = dv * _gelu_grad(vp)
            duv_ref[rows, d:] = dvp.astype(duv_ref.dtype)
            a_bin[:, d:] += _fold8(dvp)

        @pl.when(i == pl.num_programs(0) - 1)
        def _():
            dbin_ref[...] = jnp.sum(a_bin[...], axis=0, keepdims=True)
            dvg_ref[...] = jnp.sum(a_vg[...], axis=0, keepdims=True)
            dvb_ref[...] = jnp.sum(a_vb[...], axis=0, keepdims=True)

    return _rows(body, s, CHUNKS_PER_STEP * CHUNK,
                 [("blk", uvpre), ("blk", dgated), ("all", vn_g), ("all", vn_b), ("all", wc), ("all", wct), ("all", bias_full)],
                 [("blk", (s, d2), MXU_DTYPE), ("all", (A_GROUPS, CHUNK, CHUNK), F32), ("all", (CHUNK, d), F32),
                  ("all", (1, d2), F32), ("all", (1, d), F32), ("all", (1, d), F32)], name,
                 scratch=[pltpu.VMEM((CHUNK, d), F32), pltpu.VMEM((SUBLANES, d2), F32),
                          pltpu.VMEM((SUBLANES, d), F32), pltpu.VMEM((SUBLANES, d), F32)])


def _head_mask(v, h):
    lane = lax.broadcasted_iota(jnp.int32, v.shape, 1)
    return jnp.where((lane >= h * HEAD_DIM) & (lane < (h + 1) * HEAD_DIM), v, jnp.zeros_like(v))


def _att_bias(slopes, dil):
    qi = lax.broadcasted_iota(jnp.int32, (SPAN, SPAN), 0)
    ki = lax.broadcasted_iota(jnp.int32, (SPAN, SPAN), 1)
    sl = slopes[:, None, None]
    cur = jnp.where(ki <= qi, -sl * (float(dil) * (qi - ki).astype(F32)), NEG)
    prev = jnp.where(ki >= qi, -sl * (float(dil) * (SPAN + qi - ki).astype(F32)), NEG)
    absent = jnp.full_like(prev, NEG)
    pairs = slopes.shape[0] // 2

    def fwd(pv):
        return jnp.concatenate([cur, pv], axis=2).reshape(pairs, 2 * SPAN, 2 * SPAN)

    def bwd(pv):
        return jnp.concatenate([cur.reshape(pairs, 2 * SPAN, SPAN), pv.reshape(pairs, 2 * SPAN, SPAN)], axis=1)

    return jnp.stack([fwd(absent), fwd(prev)]), jnp.stack([bwd(absent), bwd(prev)])


def _att_specs(s, d, dil, kinds):
    nb = s // (dil * SPAN)

    def rowblk(which, b):
        if which == "prev":
            return jnp.where(b % nb == 0, b, b - 1)
        if which == "next":
            return jnp.where(b % nb == nb - 1, b, b + 1)
        return b

    return [pl.BlockSpec((SPAN, d), functools.partial(lambda b, o, w: (rowblk(w, b), o), o=part, w=which))
            for part, which in kinds]


def _head_col(v, head):
    return v[:, head:head + 1]


def _expand_heads(w, j):
    shape = (w.shape[0], LANES)
    return jnp.where(_left_half(shape), jnp.broadcast_to(_head_col(w, 2 * j), shape), jnp.broadcast_to(_head_col(w, 2 * j + 1), shape))


def _attn_fwd(qkv, slopes, dil, name):
    s, d3 = qkv.shape
    d = d3 // 3
    nb = s // (dil * SPAN)
    table, _ = _att_bias(slopes, dil)

    def body(q_ref, kc_ref, kp_ref, vc_ref, vp_ref, tb_ref, o_ref, l_ref):
        left = _left_half((SPAN, LANES))
        lane = lax.broadcasted_iota(jnp.int32, (SPAN, LANES), 1)
        lses = jnp.zeros((SPAN, LANES), F32)
        for hp in range(d // LANES):
            cols = slice(hp * LANES, (hp + 1) * LANES)
            q = q_ref[:, cols]
            q2 = jnp.concatenate([_head_mask(q, 0), _head_mask(q, 1)], axis=0) * ATT_SCALE
            k2 = jnp.concatenate([kc_ref[:, cols], kp_ref[:, cols]], axis=0)
            v2 = jnp.concatenate([vc_ref[:, cols], vp_ref[:, cols]], axis=0)
            sc = _dot_nt(q2, k2) + tb_ref[hp]
            m = jnp.max(sc, axis=-1, keepdims=True)
            p = jnp.exp(sc - m)
            l = jnp.sum(p, axis=-1, keepdims=True)
            r = _dot_nn(p, v2) * (1.0 / l)
            lse = m + jnp.log(l)
            o_ref[:, cols] = jnp.where(left, r[:SPAN], r[SPAN:])
            lses = jnp.where(lane == 2 * hp, lse[:SPAN], jnp.where(lane == 2 * hp + 1, lse[SPAN:], lses))
        l_ref[...] = lses

    specs = _att_specs(s, d, dil, [(0, "cur"), (1, "cur"), (1, "prev"), (2, "cur"), (2, "prev")])
    tbl = pl.BlockSpec((None,) + table.shape[1:], lambda b: (jnp.where(b % nb == 0, 0, 1), 0, 0, 0))
    out_spec = pl.BlockSpec((SPAN, d), lambda b: (b, 0))
    return pl.pallas_call(
        body,
        grid=(s // SPAN,),
        in_specs=specs + [tbl],
        out_specs=[out_spec, pl.BlockSpec((SPAN, LANES), lambda b: (b, 0))],
        out_shape=[jax.ShapeDtypeStruct((s, d), F32), jax.ShapeDtypeStruct((s, LANES), F32)],
        name=name,
        compiler_params=_cparams(("parallel",)),
    )(qkv, qkv, qkv, qkv, qkv, table)


def _attn_bwd(qkv, do, lse, dd, slopes, dil, name):
    s, d3 = qkv.shape
    d = d3 // 3
    nb = s // (dil * SPAN)
    _, table = _att_bias(slopes, dil)

    def heads_stacked(cur, nxt):
        return jnp.concatenate([_head_mask(cur, 0), _head_mask(cur, 1), _head_mask(nxt, 0), _head_mask(nxt, 1)], axis=0)

    def cols_stacked(cur, nxt, hp):
        return jnp.concatenate([jnp.broadcast_to(_head_col(a, 2 * hp + h), (SPAN, LANES)) for a in (cur, nxt) for h in range(2)], axis=0)

    def body(k_ref, v_ref, qc_ref, qn_ref, doc_ref, don_ref, lc_ref, ln_ref, ddc_ref, ddn_ref, tb_ref, out_ref, carry):
        b = pl.program_id(0)

        @pl.when(b == 0)
        def _():
            carry[...] = jnp.zeros_like(carry)

        left = _left_half((SPAN, LANES))
        lse_c, lse_n, dd_c, dd_n = lc_ref[...], ln_ref[...], ddc_ref[...], ddn_ref[...]
        for hp in range(d // LANES):
            cols = slice(hp * LANES, (hp + 1) * LANES)
            k, v = k_ref[:, cols], v_ref[:, cols]
            q4 = heads_stacked(qc_ref[:, cols], qn_ref[:, cols])
            do4 = heads_stacked(doc_ref[:, cols], don_ref[:, cols])
            sc = _dot_nt(q4 * ATT_SCALE, k) + tb_ref[hp]
            p = jnp.exp(sc - cols_stacked(lse_c, lse_n, hp))
            ds = p * (_dot_nt(do4, v) - cols_stacked(dd_c, dd_n, hp))
            dq4 = _dot_nn(ds, k)
            dq_cur = jnp.where(left, dq4[:SPAN], dq4[SPAN:2 * SPAN]) + carry[:, cols]
            carry[:, cols] = jnp.where(left, dq4[2 * SPAN:3 * SPAN], dq4[3 * SPAN:])
            out_ref[:, cols] = (dq_cur * ATT_SCALE).astype(out_ref.dtype)
            out_ref[:, d + hp * LANES:d + (hp + 1) * LANES] = (_dot_tn(ds, q4) * ATT_SCALE).astype(out_ref.dtype)
            out_ref[:, 2 * d + hp * LANES:2 * d + (hp + 1) * LANES] = _dot_tn(p, do4).astype(out_ref.dtype)

    qkv_specs = _att_specs(s, d, dil, [(1, "cur"), (2, "cur"), (0, "cur"), (0, "next")])
    pair = _att_specs(s, d, dil, [(0, "cur"), (0, "next")])
    heads = _att_specs(s, LANES, dil, [(0, "cur"), (0, "next")])
    tbl = pl.BlockSpec((None,) + table.shape[1:], lambda b: (jnp.where(b % nb == nb - 1, 0, 1), 0, 0, 0))
    return pl.pallas_call(
        body,
        grid=(s // SPAN,),
        in_specs=qkv_specs + pair + heads + heads + [tbl],
        out_specs=pl.BlockSpec((SPAN, d3), lambda b: (b, 0)),
        out_shape=jax.ShapeDtypeStruct((s, d3), MXU_DTYPE),
        scratch_shapes=[pltpu.VMEM((SPAN, d), F32)],
        name=name,
        compiler_params=_cparams(("arbitrary",)),
    )(qkv, qkv, qkv, qkv, do, do, lse, lse, dd, dd, table)


def _mix_weights(l_refs):
    ls = [r[...] for r in l_refs]
    m = functools.reduce(jnp.maximum, ls)
    es = [jnp.exp(l - m) for l in ls]
    tot = functools.reduce(lambda a, c: a + c, es)
    return [e / tot for e in es]


def _combine_fwd(os_, ls_, name):
    s, d = os_[0].shape
    n = len(os_)

    def body(*refs):
        o_refs, l_refs, out_ref = refs[:n], refs[n:2 * n], refs[2 * n]
        ws = _mix_weights(l_refs)
        for j in range(d // LANES):
            cols = slice(j * LANES, (j + 1) * LANES)
            acc = _expand_heads(ws[0], j) * o_refs[0][:, cols]
            for w, o in zip(ws[1:], o_refs[1:]):
                acc = acc + _expand_heads(w, j) * o[:, cols]
            out_ref[:, cols] = acc

    return _rows(body, s, ROW_TILE, [("blk", a) for a in os_ + ls_], [("blk", (s, d), F32)], name)[0]


def _combine_bwd(do, o, ls_, name):
    s, d = o.shape
    n = len(ls_)
    sel = (lax.broadcasted_iota(jnp.int32, (d, LANES), 0) // HEAD_DIM == lax.broadcasted_iota(jnp.int32, (d, LANES), 1)).astype(F32)

    def body(do_ref, o_ref, *rest):
        l_refs, sel_ref, outs = rest[:n], rest[n], rest[n + 1:]
        ws = _mix_weights(l_refs)
        dov = do_ref[...]
        r = jnp.dot(dov * o_ref[...], sel_ref[...], precision=lax.Precision.HIGHEST, preferred_element_type=F32)
        for g in range(n):
            outs[2 * g + 1][...] = ws[g] * r
            for j in range(d // LANES):
                cols = slice(j * LANES, (j + 1) * LANES)
                outs[2 * g][:, cols] = (_expand_heads(ws[g], j) * dov[:, cols]).astype(outs[2 * g].dtype)

    outs = []
    for _ in range(n):
        outs += [("blk", (s, d), MXU_DTYPE), ("blk", (s, LANES), F32)]
    res = _rows(body, s, ROW_TILE, [("blk", do), ("blk", o)] + [("blk", l) for l in ls_] + [("all", sel)], outs, name)
    return [(res[2 * g], res[2 * g + 1]) for g in range(n)]


def _ada_fwd(c_all, w, b, name):
    nsub, d, cs = w.shape

    def body(c_ref, w_ref, b_ref, o_ref):
        cv = c_ref[...]
        sc = cv * (1.0 / (1.0 + jnp.exp(-cv)))
        o_ref[...] = _dot_nn(sc, w_ref[...]) + b_ref[...]

    return pl.pallas_call(
        body,
        grid=(nsub,),
        in_specs=[pl.BlockSpec(c_all.shape, lambda i: (0, 0)), pl.BlockSpec((None, d, cs), lambda i: (i, 0, 0)),
                  pl.BlockSpec((None, 1, cs), lambda i: (i, 0, 0))],
        out_specs=pl.BlockSpec((None, N_DEV, cs), lambda i: (i, 0, 0)),
        out_shape=jax.ShapeDtypeStruct((nsub, N_DEV, cs), F32),
        name=name,
        compiler_params=_cparams(("parallel",)),
    )(c_all, w, b)


def _ada_bwd(c_all_t, dm, name):
    d, nb = c_all_t.shape
    nsub, _, cs = dm.shape

    def body(c_ref, dm_ref, o_ref):
        cv = c_ref[...]
        sc = cv * (1.0 / (1.0 + jnp.exp(-cv)))
        acc = sc[:, 0:1] * dm_ref[0:1, :]
        for bi in range(1, nb):
            acc = acc + sc[:, bi:bi + 1] * dm_ref[bi:bi + 1, :]
        o_ref[...] = acc

    return pl.pallas_call(
        body,
        grid=(nsub,),
        in_specs=[pl.BlockSpec(c_all_t.shape, lambda i: (0, 0)), pl.BlockSpec((None, nb, cs), lambda i: (i, 0, 0))],
        out_specs=pl.BlockSpec((None, d, cs), lambda i: (i, 0, 0)),
        out_shape=jax.ShapeDtypeStruct((nsub, d, cs), F32),
        name=name,
        compiler_params=_cparams(("parallel",)),
    )(c_all_t, dm)


def _row_tile(r, row_elems, block_elems=256 * 1024):
    t = 2 * SUBLANES
    if r % t:
        return r
    while t * 2 * row_elems <= block_elems and r % (t * 2) == 0:
        t *= 2
    return t


def _adamw(w, g, m, v, name):
    shape = w.shape
    c = shape[-1]
    r = w.size // c
    tr = _row_tile(r, c)
    w2, g2, m2, v2 = [a.reshape(r, c) for a in (w, g, m, v)]
    bc1 = 1.0 - ADAM_B1 ** ADAM_STEP
    bc2 = 1.0 - ADAM_B2 ** ADAM_STEP

    def body(w_ref, g_ref, m_ref, v_ref, d_ref, nm_ref, nv_ref):
        gv = g_ref[...]
        nm = ADAM_B1 * m_ref[...] + (1.0 - ADAM_B1) * gv
        nv = ADAM_B2 * v_ref[...] + (1.0 - ADAM_B2) * (gv * gv)
        d_ref[...] = -ADAM_LR * ((nm / bc1) / (jnp.sqrt(nv / bc2) + ADAM_EPS) + ADAM_WD * w_ref[...])
        nm_ref[...] = nm
        nv_ref[...] = nv

    res = _rows(body, r, tr, [("blk", a) for a in (w2, g2, m2, v2)], [("blk", (r, c), F32)] * 3, name)
    return [a.reshape(shape) for a in res]


def _sum_slots(buf, name):
    n, r, c = buf.shape
    tr = _row_tile(r, n * c, 2 * 1024 * 1024)

    def body(b_ref, o_ref):
        acc = b_ref[0].astype(F32)
        for k in range(1, n):
            acc = acc + b_ref[k].astype(F32)
        o_ref[...] = acc

    return pl.pallas_call(
        body,
        grid=(r // tr,),
        in_specs=[pl.BlockSpec((n, tr, c), lambda i: (0, i, 0))],
        out_specs=pl.BlockSpec((tr, c), lambda i: (i, 0)),
        out_shape=jax.ShapeDtypeStruct((r, c), F32),
        name=name,
        compiler_params=_cparams(("parallel",)),
    )(buf)


def _me():
    return lax.axis_index("x"), lax.axis_index("y"), lax.axis_index("c")


def _all_gather_small(blk, name, after=()):
    m_per, n = blk.shape

    def body(x_ref, *rest):
        out_ref, send_sems, recv_sems, local_sem = rest[len(after):]
        x, y, c = _me()
        me, sibling = (x, y, c), (x, y, 1 - c)
        chips = [(1 - x, y), (x, 1 - y), (1 - x, 1 - y)]

        def rows(px, py, pc):
            return out_ref.at[pl.ds((4 * px + 2 * py + pc) * m_per, m_per), :]

        def copy(k, block, to, src=None):
            return pltpu.make_async_remote_copy(
                src_ref=rows(*block) if src is None else src, dst_ref=rows(*block),
                send_sem=send_sems.at[k], recv_sem=recv_sems.at[k], device_id=to, device_id_type=MESH)

        mine = pltpu.make_async_copy(x_ref, rows(*me), local_sem)
        mine.start()
        first = [copy(0, me, sibling, src=x_ref)]
        first += [copy(1 + j, me, (*chip, c), src=x_ref) for j, chip in enumerate(chips)]
        for cp in first:
            cp.start()
        passed = [copy(4 + j, (*chip, c), sibling) for j, chip in enumerate(chips)]
        for j, chip in enumerate(chips):
            copy(1 + j, (*chip, c), me).wait_recv()
            passed[j].start()
        copy(0, sibling, me).wait_recv()
        for j, chip in enumerate(chips):
            copy(4 + j, (*chip, 1 - c), me).wait_recv()
        for cp in first + passed:
            cp.wait_send()
        mine.wait()

    return pl.pallas_call(
        body,
        out_shape=jax.ShapeDtypeStruct((N_DEV * m_per, n), blk.dtype),
        in_specs=[pl.BlockSpec(memory_space=pltpu.VMEM)] + [pl.BlockSpec(memory_space=pl.ANY)] * len(after),
        out_specs=pl.BlockSpec(memory_space=pltpu.VMEM),
        scratch_shapes=[pltpu.SemaphoreType.DMA((7,)), pltpu.SemaphoreType.DMA((7,)), pltpu.SemaphoreType.DMA],
        name=name,
        compiler_params=pltpu.CompilerParams(vmem_limit_bytes=VMEM_LIMIT),
    )(blk, *after)


_HBM = pl.BlockSpec(memory_space=pltpu.HBM)
_SEM = pl.BlockSpec(memory_space=pltpu.SEMAPHORE)
_EFFECT = pltpu.SideEffectType.DATAFLOW_SIDE_EFFECTING


def _other_chips(x, y):
    return [(1 - x, y), (x, 1 - y), (1 - x, 1 - y)]


def _gather_copy(w, j, src_ref, land_ref, send_sems, recv_sems, halved=False):
    x, y, c = _me()
    if halved:
        half = src_ref.shape[0] // 2
        src_ref = src_ref.at[pl.ds(c * half, half), :]
    return pltpu.make_async_remote_copy(
        src_ref=src_ref, dst_ref=land_ref.at[2 * x + y], send_sem=send_sems.at[3 * w + j], recv_sem=recv_sems.at[3 * w + j],
        device_id=(*_other_chips(x, y)[j], c), device_id_type=MESH)


def _gather_start(shards, halved, after, name):
    n = len(shards)
    lands = [lax.empty((N_CHIPS, s.shape[0] // 2 if w in halved else s.shape[0], s.shape[1]), s.dtype) for w, s in enumerate(shards)]

    def body(*refs):
        in_refs, land_refs = refs[:n], refs[n:2 * n]
        send_sems, recv_sems = refs[2 * n + 1], refs[2 * n + 2]
        token = refs[-1]
        for w in range(n):
            for j in range(3):
                _gather_copy(w, j, in_refs[w], land_refs[w], send_sems, recv_sems, w in halved).start()
        token[...] = jnp.zeros_like(token)

    res = pl.pallas_call(
        body,
        out_shape=(pltpu.SemaphoreType.DMA((3 * n,)), pltpu.SemaphoreType.DMA((3 * n,)),
                   *[pltpu.HBM(s.shape, s.dtype) for s in shards], *[pltpu.HBM(l.shape, l.dtype) for l in lands],
                   jax.ShapeDtypeStruct((SUBLANES, LANES), F32)),
        in_specs=[_HBM] * (2 * n) + [pl.BlockSpec(memory_space=pl.ANY)],
        out_specs=(_SEM, _SEM, *[_HBM] * (2 * n), pl.BlockSpec(memory_space=pltpu.VMEM)),
        input_output_aliases={i: 2 + i for i in range(2 * n)},
        name=name,
        compiler_params=pltpu.CompilerParams(has_side_effects=_EFFECT),
    )(*[pltpu.with_memory_space_constraint(a, pltpu.HBM) for a in list(shards) + lands], after)
    return res[0], res[1], res[2:2 + n], res[2 + n:2 + 2 * n], res[-1]


def _gather_wait(w, shard, land, send_sems, recv_sems, after, name, halved=False):
    def body(s_ref, land_ref, send_sems, recv_sems, after_ref, s_out, land_out, stage):
        x, y, _ = _me()
        if not halved:
            pltpu.sync_copy(s_ref, stage)
            pltpu.sync_copy(stage, land_out.at[2 * x + y])
        for j in range(3):
            cp = _gather_copy(w, j, s_ref, land_ref, send_sems, recv_sems, halved)
            cp.wait_send()
            cp.wait_recv()

    return pl.pallas_call(
        body,
        out_shape=(pltpu.HBM(shard.shape, shard.dtype), pltpu.HBM(land.shape, land.dtype)),
        in_specs=(_HBM, _HBM, _SEM, _SEM, pl.BlockSpec(memory_space=pl.ANY)),
        out_specs=(_HBM, _HBM),
        input_output_aliases={0: 0, 1: 1},
        scratch_shapes=[pltpu.VMEM((SUBLANES, LANES) if halved else shard.shape, shard.dtype)],
        name=name,
        compiler_params=pltpu.CompilerParams(has_side_effects=_EFFECT, vmem_limit_bytes=VMEM_LIMIT),
    )(shard, land, send_sems, recv_sems, after)


def _assemble_halves(shard, land, name):
    half = land.shape[1]

    def body(s_ref, land_ref, out_ref, send_sems, recv_sems, local_sems):
        x, y, c = _me()
        own = pltpu.make_async_copy(s_ref, out_ref.at[2 * x + y], local_sems.at[3])
        own.start()
        cps = []
        for j, (ox, oy) in enumerate(_other_chips(x, y)):
            qj = 2 * ox + oy
            mine = out_ref.at[qj, pl.ds(c * half, half), :]
            lc = pltpu.make_async_copy(land_ref.at[qj], mine, local_sems.at[j])
            lc.start()
            rc = pltpu.make_async_remote_copy(
                src_ref=land_ref.at[qj], dst_ref=mine, send_sem=send_sems.at[j], recv_sem=recv_sems.at[j],
                device_id=(x, y, 1 - c), device_id_type=MESH)
            rc.start()
            cps.append((lc, rc))
        for lc, rc in cps:
            rc.wait_recv()
        for lc, rc in cps:
            rc.wait_send()
            lc.wait()
        own.wait()

    vmem = pl.BlockSpec(memory_space=pltpu.VMEM)
    return pl.pallas_call(
        body,
        out_shape=jax.ShapeDtypeStruct((N_CHIPS,) + shard.shape, shard.dtype),
        in_specs=[vmem, vmem],
        out_specs=vmem,
        scratch_shapes=[pltpu.SemaphoreType.DMA((3,)), pltpu.SemaphoreType.DMA((3,)), pltpu.SemaphoreType.DMA((4,))],
        name=name,
        compiler_params=pltpu.CompilerParams(vmem_limit_bytes=VMEM_LIMIT),
    )(shard, land)


def _piece_shape(shape, kind):
    k, nn = shape
    if kind == "all":
        return (k, nn)
    return (k // 2, nn // N_CHIPS) if kind == "col" else (k // N_CHIPS // 2, nn)


def _piece_of(g_ref, kind, tq, tc):
    pr, pc = _piece_shape(g_ref.shape, kind)
    if kind == "all":
        return g_ref
    if kind == "col":
        return g_ref.at[pl.ds(tc * pr, pr), pl.ds(tq * pc, pc)]
    return g_ref.at[pl.ds((2 * tq + tc) * pr, pr), :]


def _scatter_copy(w, r, kind, g_ref, land_ref, send_sems, recv_sems):
    x, y, c = _me()
    tx, ty, tc = (x + ((r >> 2) & 1)) % 2, (y + ((r >> 1) & 1)) % 2, (c + (r & 1)) % 2
    return pltpu.make_async_remote_copy(
        src_ref=_piece_of(g_ref, kind, 2 * tx + ty, tc), dst_ref=land_ref.at[4 * x + 2 * y + c],
        send_sem=send_sems.at[N_DEV * w + r], recv_sem=recv_sems.at[N_DEV * w + r], device_id=(tx, ty, tc), device_id_type=MESH)


def _scatter_start(gs, kinds, name):
    n = len(gs)
    pieces = [_piece_shape(g.shape, kind) for g, kind in zip(gs, kinds)]
    lands = [lax.empty((N_DEV,) + p, g.dtype) for p, g in zip(pieces, gs)]

    def body(*refs):
        g_refs, land_refs, send_sems, recv_sems = refs[:n], refs[n:2 * n], refs[2 * n], refs[2 * n + 1]
        land_outs, stages = refs[3 * n + 2:4 * n + 2], refs[4 * n + 2:]
        x, y, c = _me()
        for w in range(n):
            for r in range(1, N_DEV):
                _scatter_copy(w, r, kinds[w], g_refs[w], land_refs[w], send_sems, recv_sems).start()
        for w in range(n):
            pltpu.sync_copy(_piece_of(g_refs[w], kinds[w], 2 * x + y, c), stages[w])
            pltpu.sync_copy(stages[w], land_outs[w].at[4 * x + 2 * y + c])

    arrays = list(gs) + lands
    res = pl.pallas_call(
        body,
        out_shape=(pltpu.SemaphoreType.DMA((N_DEV * n,)), pltpu.SemaphoreType.DMA((N_DEV * n,)),
                   *[pltpu.HBM(a.shape, a.dtype) for a in arrays]),
        in_specs=[_HBM] * (2 * n),
        out_specs=(_SEM, _SEM, *[_HBM] * (2 * n)),
        input_output_aliases={i: 2 + i for i in range(2 * n)},
        scratch_shapes=[pltpu.VMEM(p, g.dtype) for p, g in zip(pieces, gs)],
        name=name,
        compiler_params=pltpu.CompilerParams(has_side_effects=_EFFECT, vmem_limit_bytes=VMEM_LIMIT),
    )(*[pltpu.with_memory_space_constraint(a, pltpu.HBM) for a in arrays])
    return res[0], res[1], res[2:2 + n], res[2 + n:]


def _scatter_wait(send_sems, recv_sems, gs, lands, kinds, after, name):
    n = len(gs)

    def body(*refs):
        g_refs, land_refs, send_sems, recv_sems = refs[:n], refs[n:2 * n], refs[2 * n], refs[2 * n + 1]
        for w in range(n):
            for r in range(1, N_DEV):
                cp = _scatter_copy(w, r, kinds[w], g_refs[w], land_refs[w], send_sems, recv_sems)
                cp.wait_send()
                cp.wait_recv()

    arrays = list(gs) + list(lands)
    return pl.pallas_call(
        body,
        out_shape=tuple(pltpu.HBM(a.shape, a.dtype) for a in arrays),
        in_specs=(*[_HBM] * (2 * n), _SEM, _SEM, pl.BlockSpec(memory_space=pl.ANY)),
        out_specs=tuple([_HBM] * (2 * n)),
        input_output_aliases={i: i for i in range(2 * n)},
        name=name,
        compiler_params=pltpu.CompilerParams(has_side_effects=_EFFECT),
    )(*arrays, send_sems, recv_sems, after)[n:]


def _swap_halves(halves, name):
    n = len(halves)

    def body(*refs):
        in_refs, out_refs = refs[:n], refs[n:2 * n]
        send_sems, recv_sems, local_sems = refs[2 * n:]
        x, y, c = _me()
        cps = []
        for w in range(n):
            lc = pltpu.make_async_copy(in_refs[w], out_refs[w].at[c], local_sems.at[w])
            lc.start()
            rc = pltpu.make_async_remote_copy(
                src_ref=in_refs[w], dst_ref=out_refs[w].at[c], send_sem=send_sems.at[w], recv_sem=recv_sems.at[w],
                device_id=(x, y, 1 - c), device_id_type=MESH)
            rc.start()
            cps.append((lc, rc))
        for lc, rc in cps:
            rc.wait_recv()
        for lc, rc in cps:
            rc.wait_send()
            lc.wait()

    vmem = pl.BlockSpec(memory_space=pltpu.VMEM)
    return pl.pallas_call(
        body,
        out_shape=[jax.ShapeDtypeStruct((2,) + h.shape, h.dtype) for h in halves],
        in_specs=[vmem] * n,
        out_specs=[vmem] * n,
        scratch_shapes=[pltpu.SemaphoreType.DMA((n,)), pltpu.SemaphoreType.DMA((n,)), pltpu.SemaphoreType.DMA((n,))],
        name=name,
        compiler_params=pltpu.CompilerParams(vmem_limit_bytes=VMEM_LIMIT),
    )(*halves)


def _to_streams(a, dil):
    if dil == 1:
        return a
    s, c = a.shape
    return a.reshape(s // dil, dil, c).transpose(1, 0, 2).reshape(s, c)


def _from_streams(a, dil):
    if dil == 1:
        return a
    s, c = a.shape
    return a.reshape(dil, s // dil, c).transpose(1, 0, 2).reshape(s, c)


def _mm_tiles(s):
    return min(s, 2048)


def _local_step(x0, target, mvec, ln_g, ln_b, small, fetch, emit, start):
    s, d = x0.shape
    tm = _mm_tiles(s)
    row = lambda v: v.reshape(1, -1)
    shift = [row(mvec[i, :d]) for i in range(4)]
    scale = [row(mvec[i, d:2 * d]) for i in range(4)]
    gate = [row(1.0 + mvec[i, 2 * d:]) for i in range(4)]
    lg = [row(ln_g[i]) for i in range(4)]
    lb = [row(ln_b[i]) for i in range(4)]
    mm = functools.partial(_mm, tm=tm)
    mm_w = functools.partial(_mm, tm=1024, tk=min(s, 2048), mode="tn")

    xs, ys, big = [x0], [], {}
    h0 = _mod(x0, scale[0], shift[0], start, "mod0")
    big["a_w_in"] = fetch("a_w_in", h0)
    uvpre = mm(h0, big["a_w_in"], mode="nn", name="a_in", outs=[F32], tn=512, tk=1024,
               epi=lambda r, bias: [r + bias], extras=[("row", small["a_b_in"])])
    gated = _spatial_fwd(uvpre, small["a_vn_g"], small["a_vn_b"], small["wc"], small["bias_full"], "a_spatial")
    big["a_w_out"] = fetch("a_w_out", gated)
    ys.append(mm(gated, big["a_w_out"], mode="nn", name="a_out", outs=[F32], tn=1024, tk=1024))
    x1, h1 = _resid_ln(xs[0], ys[0], gate[0], lg[0], lb[0], (scale[1], shift[1]), "ln0")
    xs.append(x1)
    relu2 = lambda r: [jnp.square(jnp.maximum(r, 0.0))]
    big["up0"] = fetch("up0", h1)
    r0 = mm(h1, big["up0"], mode="nn", name="up0", outs=[MXU_DTYPE], tn=1024, tk=1024, epi=relu2)
    big["down0"] = fetch("down0", r0)
    ys.append(mm(r0, big["down0"], mode="nn", name="down0", outs=[F32], tm=min(s, 1024), tn=1024, tk=2048))
    x2, h2 = _resid_ln(xs[1], ys[1], gate[1], lg[1], lb[1], (scale[2], shift[2]), "ln1")
    xs.append(x2)
    hg, qkvs, o_g, l_g, l_streams = [], [], [], [], []
    big["b_w_qkv"] = fetch("b_w_qkv", h2)
    for g, (_, dil) in enumerate(B_PATTERNS):
        hp = _to_streams(h2, dil)
        qkv = mm(hp, big["b_w_qkv"], mode="nn", name=f"qkv{g}", outs=[MXU_DTYPE], tn=768, tk=1024, b_col0=g * 3 * d, n_out=3 * d)
        og, lgv = _attn_fwd(qkv, small["slopes"], dil, f"attn_fwd{g}")
        hg.append(hp)
        qkvs.append(qkv)
        o_g.append(_from_streams(og, dil))
        l_g.append(_from_streams(lgv, dil))
        l_streams.append(lgv)
    o_mix = _combine_fwd(o_g, l_g, "combine")
    big["b_w_out"] = fetch("b_w_out", o_mix)
    ys.append(mm(o_mix, big["b_w_out"], mode="nn", name="b_out", outs=[F32], tn=1024, tk=1024))
    x3, h3 = _resid_ln(xs[2], ys[2], gate[2], lg[2], lb[2], (scale[3], shift[3]), "ln2")
    xs.append(x3)
    big["up1"] = fetch("up1", h3)
    r1 = mm(h3, big["up1"], mode="nn", name="up1", outs=[MXU_DTYPE], tn=1024, tk=1024, epi=relu2)
    big["down1"] = fetch("down1", r1)
    ys.append(mm(r1, big["down1"], mode="nn", name="down1", outs=[F32], tm=min(s, 1024), tn=1024, tk=2048))

    gb, red_ln, red_mod = {}, [None] * 4, [None] * 4

    def mlp_bwd(i, h, r, dyy):
        gb[f"down{i}"] = mm_w(r, dyy, name=f"g_down{i}", outs=[MXU_DTYPE], tm=2048, tn=1024)
        da = mm(dyy, big[f"down{i}"], mode="nt", name=f"d_down{i}", outs=[MXU_DTYPE], tn=1024, tk=1024,
                after=emit(f"down{i}", gb[f"down{i}"]),
                epi=lambda acc, rv: [acc * (2.0 * jnp.sqrt(rv.astype(F32)))], extras=[("full", r)])
        gb[f"up{i}"] = mm_w(h, da, name=f"g_up{i}", outs=[MXU_DTYPE], tn=2048)
        return [mm(da, big[f"up{i}"], mode="nt", name=f"d_up{i}", outs=[F32], tn=1024, tk=1024, after=emit(f"up{i}", gb[f"up{i}"]))]

    def join(sub, dxr, dhs, after=None):
        res = _mod_ln_bwd(dxr, dhs, xs[sub], scale[sub], xs[sub - 1], ys[sub - 1], gate[sub - 1], lg[sub - 1],
                          f"mod_ln_bwd{sub}", after=after)
        red_mod[sub], red_ln[sub - 1] = res[2], res[3]
        return res[0], res[1]

    loss, dxr, dyy, red_ln[3] = _last_ln_loss_bwd(xs[3], ys[3], gate[3], lg[3], lb[3], target, "ln3_loss_bwd")
    dxr, dyy = join(3, dxr, mlp_bwd(1, h3, r1, dyy))
    gb["b_w_out"] = mm_w(o_mix, dyy, name="g_b_out", outs=[MXU_DTYPE], tn=1024, tk=1024)
    do = mm(dyy, big["b_w_out"], mode="nt", name="d_b_out", outs=[F32], tn=1024, tk=1024, after=emit("b_w_out", gb["b_w_out"]))
    parts = _combine_bwd(do, o_mix, l_g, "combine_bwd")
    dhs, gq = [], None
    for g, (_, dil) in enumerate(B_PATTERNS):
        do_g, dd_g = _to_streams(parts[g][0], dil), _to_streams(parts[g][1], dil)
        dqkv = _attn_bwd(qkvs[g], do_g, l_streams[g], dd_g, small["slopes"], dil, f"attn_bwd{g}")
        gq = mm_w(hg[g], dqkv, name=f"g_qkv{g}", outs=[MXU_DTYPE], tn=1024, out_col0=g * 3 * d, out_cols=len(B_PATTERNS) * 3 * d, into=gq)
        dh = mm(dqkv, big["b_w_qkv"], mode="nt", name=f"d_qkv{g}", outs=[F32], tn=1024, tk=768, b_col0=g * 3 * d)
        dhs.append(_from_streams(dh, dil))
    gb["b_w_qkv"] = gq
    dxr, dyy = join(2, dxr, dhs, after=emit("b_w_qkv", gb["b_w_qkv"]))
    dxr, dyy = join(1, dxr, mlp_bwd(0, h1, r0, dyy))
    gb["a_w_out"] = mm_w(gated, dyy, name="g_a_out", outs=[MXU_DTYPE], tn=1024)
    dgated = mm(dyy, big["a_w_out"], mode="nt", name="d_a_out", outs=[F32], tn=1024, tk=1024, after=emit("a_w_out", gb["a_w_out"]))
    duv, dws, dbias, dbin, dvg, dvb = _spatial_bwd(uvpre, dgated, small["a_vn_g"], small["a_vn_b"], small["wc"],
                                                   small["wct"], small["bias_full"], "a_spatial_bwd")
    tril = jnp.tril(jnp.ones((CHUNK, CHUNK), bool))
    dws = jnp.where(tril, dws, 0.0).reshape(-1, LANES)
    gb["a_w_in"] = mm_w(h0, duv, name="g_a_in", outs=[MXU_DTYPE], tn=1024, after=emit("a_w_s", dws.astype(MXU_DTYPE)))
    dh = mm(duv, big["a_w_in"], mode="nt", name="d_a_in", outs=[F32], tn=1024, tk=512, after=emit("a_w_in", gb["a_w_in"]))
    dx, red_mod[0] = _mod_bwd(dxr, [dh], xs[0], scale[0], "mod_bwd0")
    dm = [jnp.concatenate([red_mod[i][0], red_mod[i][1], red_ln[i][2]]) for i in range(4)]
    dlg, dlb = [red_ln[i][0] for i in range(4)], [red_ln[i][1] for i in range(4)]

    gsmall = {
        "a_b_in": dbin.reshape(-1), "a_vn_g": dvg.reshape(-1), "a_vn_b": dvb.reshape(-1),
        "a_w_s": dws.reshape(-1),
        "a_b_s": dbias.reshape(CHUNK, A_GROUPS, d // A_GROUPS).sum(-1).T.reshape(-1),
    }
    return loss, dx, gb, jnp.stack(dm), jnp.stack(dlg), jnp.stack(dlb), gsmall


BIG = ("a_w_in", "a_w_out", "up0", "down0", "b_w_qkv", "b_w_out", "up1", "down1")
BIG_KIND = {"a_w_in": "col", "a_w_out": "row", "b_w_qkv": "col", "b_w_out": "row",
            "up0": "col", "up1": "col", "down0": "row", "down1": "row", "a_w_s": "all"}
HALVED = ("a_w_in", "down0", "b_w_qkv")
SCATTER_GROUPS = (("down1", "up1"), ("b_w_out", "b_w_qkv"), ("down0", "up0"), ("a_w_out", "a_w_in"), ("a_w_s",))
SMALL = ("a_b_in", "a_vn_g", "a_vn_b", "a_b_s")


def kernel(x, c, ada_w, ada_b, ln_g, ln_b, a_w_in, a_b_in, a_vn_g, a_vn_b, a_w_s, a_b_s, a_w_out, b_w_qkv, b_w_out, mlp_w_up, mlp_w_down, loss_target, m_ada_w, m_ada_b, m_ln_g, m_ln_b, m_a_w_in, m_a_b_in, m_a_vn_g, m_a_vn_b, m_a_w_s, m_a_b_s, m_a_w_out, m_b_w_qkv, m_b_w_out, m_mlp_w_up, m_mlp_w_down, v_ada_w, v_ada_b, v_ln_g, v_ln_b, v_a_w_in, v_a_b_in, v_a_vn_g, v_a_vn_b, v_a_w_s, v_a_b_s, v_a_w_out, v_b_w_qkv, v_b_w_out, v_mlp_w_up, v_mlp_w_down):
    s, d = x.shape[1], x.shape[2]
    xi, yi, ci = _me()
    q = 2 * xi + yi
    dev = 2 * q + ci
    nsub = 2 * DEPTH
    cs = ada_w.shape[-1]
    ls = ln_g.shape[-1]

    shards = {
        "a_w_in": a_w_in[0], "a_w_out": a_w_out[0], "b_w_qkv": b_w_qkv[0], "b_w_out": b_w_out[0],
        "up0": mlp_w_up[0], "up1": mlp_w_up[1], "down0": mlp_w_down[0], "down1": mlp_w_down[1],
    }
    cast = [shards[k].astype(MXU_DTYPE) for k in BIG]

    pack = jnp.concatenate([c.reshape(-1), ln_g.reshape(-1), ln_b.reshape(-1)]).reshape(-1, LANES)
    got = _all_gather_small(pack, "gather_small", after=cast).reshape(N_DEV, -1)
    c_all = got[:, :d]
    per_chip = got[0::2]
    ln_g_full = per_chip[:, d:d + nsub * ls].reshape(N_CHIPS, nsub, ls).transpose(1, 0, 2).reshape(nsub, d)
    ln_b_full = per_chip[:, d + nsub * ls:].reshape(N_CHIPS, nsub, ls).transpose(1, 0, 2).reshape(nsub, d)
    m_part = _ada_fwd(c_all, ada_w.reshape(nsub, d, cs), ada_b.reshape(nsub, 1, cs), "ada_fwd")
    m_all = _all_gather_small(m_part.reshape(-1, LANES), "gather_mod").reshape(N_DEV, nsub, N_DEV, cs)
    m_mine = lax.dynamic_index_in_dim(m_all[0::2], dev, axis=2, keepdims=False)
    mvec = m_mine.transpose(1, 0, 2).reshape(nsub, 3 * d)

    halved = {BIG.index(k) for k in HALVED}
    send_sems, recv_sems, shard_thru, lands, token = _gather_start(cast, halved, mvec, "gather_start")

    def fetch(k, after):
        w = BIG.index(k)
        shard, gw = _gather_wait(w, shard_thru[w], lands[w], send_sems, recv_sems, after, f"gather_wait_{k}", w in halved)
        if w in halved:
            gw = _assemble_halves(shard, gw, f"assemble_{k}")
        return gw if BIG_KIND[k] == "col" else gw.reshape(1, -1, gw.shape[-1])

    scattering, pending = {}, {}

    def emit(k, g):
        pending[k] = g
        group = next(gr for gr in SCATTER_GROUPS if k in gr)
        if k != group[-1]:
            return None
        scattering[group] = _scatter_start([pending[m] for m in group], [BIG_KIND[m] for m in group], f"scatter_start_{k}")
        return scattering[group][2][0]

    tril = jnp.tril(jnp.ones((CHUNK, CHUNK), bool))
    wc = jnp.where(tril, a_w_s[0], 0.0).astype(MXU_DTYPE)
    heads = jnp.arange(1, B_HEADS + 1, dtype=F32)
    small = {
        "a_b_in": a_b_in, "a_vn_g": a_vn_g, "a_vn_b": a_vn_b,
        "wc": wc, "wct": wc.transpose(0, 2, 1),
        "bias_full": jnp.repeat(a_b_s[0].T, d // A_GROUPS, axis=1),
        "slopes": jnp.exp2(-8.0 * heads / B_HEADS),
    }

    loss_part, grad_x, gb, dm, dlg, dlb, gsmall = _local_step(x[0], loss_target[0], mvec, ln_g_full, ln_b_full, small, fetch, emit, token)
    loss = lax.psum(loss_part, ("x", "y", "c"))

    weights = dict(ada_w=ada_w, ada_b=ada_b, ln_g=ln_g, ln_b=ln_b, a_w_in=a_w_in, a_b_in=a_b_in, a_vn_g=a_vn_g, a_vn_b=a_vn_b,
                   a_w_s=a_w_s, a_b_s=a_b_s, a_w_out=a_w_out, b_w_qkv=b_w_qkv, b_w_out=b_w_out, mlp_w_up=mlp_w_up, mlp_w_down=mlp_w_down)
    ms = dict(ada_w=m_ada_w, ada_b=m_ada_b, ln_g=m_ln_g, ln_b=m_ln_b, a_w_in=m_a_w_in, a_b_in=m_a_b_in, a_vn_g=m_a_vn_g, a_vn_b=m_a_vn_b,
              a_w_s=m_a_w_s, a_b_s=m_a_b_s, a_w_out=m_a_w_out, b_w_qkv=m_b_w_qkv, b_w_out=m_b_w_out, mlp_w_up=m_mlp_w_up, mlp_w_down=m_mlp_w_down)
    vs = dict(ada_w=v_ada_w, ada_b=v_ada_b, ln_g=v_ln_g, ln_b=v_ln_b, a_w_in=v_a_w_in, a_b_in=v_a_b_in, a_vn_g=v_a_vn_g, a_vn_b=v_a_vn_b,
              a_w_s=v_a_w_s, a_b_s=v_a_b_s, a_w_out=v_a_w_out, b_w_qkv=v_b_w_qkv, b_w_out=v_b_w_out, mlp_w_up=v_mlp_w_up, mlp_w_down=v_mlp_w_down)
    grads, updates = {}, {}

    def update(k):
        updates[k] = _adamw(weights[k], grads[k], ms[k], vs[k], f"adamw_{k}")
        return updates[k][0]

    gfull = {}

    def big_group(group, after):
        bufs = []
        for pair in (group[:2], group[2:]):
            bufs += _scatter_wait(*scattering[pair], [BIG_KIND[m] for m in pair], after, f"scatter_wait_{pair[-1]}")
        halves = [_sum_slots(b, f"sum_{k}") for k, b in zip(group, bufs)]
        fulls = _swap_halves(halves, f"swap_halves_{group[0]}")
        gfull.update({k: f.reshape(-1, f.shape[-1]) for k, f in zip(group, fulls)})

    big_group(SCATTER_GROUPS[0] + SCATTER_GROUPS[1], grad_x)
    grads["b_w_qkv"], grads["b_w_out"] = gfull["b_w_qkv"][None], gfull["b_w_out"][None]
    update("b_w_out")
    done = update("b_w_qkv")

    pack_b = jnp.concatenate([dm.reshape(-1), dlg.reshape(-1), dlb.reshape(-1)] + [gsmall[k] for k in SMALL])
    n_small = pack_b.shape[0]
    pack_b = jnp.pad(pack_b, (0, -n_small % (256 * LANES)))
    got_b = _all_gather_small(pack_b.reshape(-1, LANES), "gather_small_grads", after=[done]).reshape(N_DEV, -1, LANES)
    tot = _sum_slots(got_b, "sum_small").reshape(-1)
    o = 0
    dm_tot = tot[o:o + nsub * 3 * d].reshape(nsub, 3 * d); o += nsub * 3 * d
    dlg_tot = tot[o:o + nsub * d].reshape(nsub, d); o += nsub * d
    dlb_tot = tot[o:o + nsub * d].reshape(nsub, d); o += nsub * d
    g_small = {}
    for k, ref in zip(SMALL, (a_b_in, a_vn_g, a_vn_b, a_b_s)):
        g_small[k] = tot[o:o + ref.size].reshape(ref.shape); o += ref.size
    assert o == n_small
    aws = _scatter_wait(*scattering[("a_w_s",)], ["all"], tot, "scatter_wait_a_w_s")[0]
    g_small["a_w_s"] = _sum_slots(aws, "sum_a_w_s").reshape(a_w_s.shape)
    dm_all = got_b.reshape(N_DEV, -1)[:, :nsub * 3 * d].reshape(N_DEV, nsub, 3 * d)
    dm_cols = lax.dynamic_slice_in_dim(dm_all, q * cs, cs, axis=2).transpose(1, 0, 2)
    grads.update({
        "ada_w": _ada_bwd(c_all.T, dm_cols, "ada_bwd").reshape(ada_w.shape),
        "ada_b": lax.dynamic_slice_in_dim(dm_tot, q * cs, cs, axis=1).reshape(ada_b.shape),
        "ln_g": lax.dynamic_slice_in_dim(dlg_tot, q * ls, ls, axis=1).reshape(ln_g.shape),
        "ln_b": lax.dynamic_slice_in_dim(dlb_tot, q * ls, ls, axis=1).reshape(ln_b.shape),
        **g_small,
    })
    for k in ("ada_b", "ln_g", "ln_b", "a_w_s") + SMALL:
        update(k)
    done = update("ada_w")

    big_group(SCATTER_GROUPS[2] + SCATTER_GROUPS[3], done)
    grads.update({
        "a_w_in": gfull["a_w_in"][None], "a_w_out": gfull["a_w_out"][None],
        "mlp_w_up": jnp.stack([gfull["up0"], gfull["up1"]]), "mlp_w_down": jnp.stack([gfull["down0"], gfull["down1"]]),
    })
    for k in ("a_w_in", "a_w_out", "mlp_w_up", "mlp_w_down"):
        update(k)
    names = list(weights)
    return (loss, grad_x[None], *[grads[k] for k in names], *[updates[k][0] for k in names],
            *[updates[k][1] for k in names], *[updates[k][2] for k in names])
```

```python
import functools
import math

import jax
import jax.numpy as jnp
from jax import lax
from jax.experimental import pallas as pl
from jax.experimental.pallas import tpu as pltpu

F32 = jnp.float32
MXU_DTYPE = jnp.bfloat16

DEPTH = 2
CHUNK = 128
A_GROUPS = 16
B_HEADS = 16
HEAD_DIM = 64
B_PATTERNS = ((128, 1), (512, 4), (2048, 16))
SPAN = 128
ALPHA = (2 * DEPTH) ** 0.25
LN_EPS = 1e-5
NEG = -1e30
ATT_SCALE = HEAD_DIM ** -0.5
ADAM_LR, ADAM_B1, ADAM_B2, ADAM_EPS, ADAM_WD, ADAM_STEP = 0.001, 0.9, 0.999, 1e-08, 0.01, 10

N_CHIPS = 4
N_DEV = 8
LANES = 128
SUBLANES = 8
VMEM_LIMIT = 52 * 1024 * 1024
ROW_TILE = 512
MM_ROW_CHUNK = 256
MESH = pl.DeviceIdType.MESH


def _cparams(sem):
    return pltpu.CompilerParams(dimension_semantics=sem, vmem_limit_bytes=VMEM_LIMIT)


def _fold8(v):
    r, c = v.shape
    return jnp.sum(v.reshape(r // SUBLANES, SUBLANES, c), axis=0)


def _gelu(x):
    c = math.sqrt(2.0 / math.pi)
    return 0.5 * x * (1.0 + jnp.tanh(c * (x + 0.044715 * (x * x * x))))


def _gelu_grad(x):
    c = math.sqrt(2.0 / math.pi)
    t = jnp.tanh(c * (x + 0.044715 * (x * x * x)))
    return 0.5 * (1.0 + t) + 0.5 * x * (1.0 - t * t) * c * (1.0 + 3.0 * 0.044715 * x * x)


def _dot(a, b, dims):
    return lax.dot_general(a.astype(MXU_DTYPE), b.astype(MXU_DTYPE), (dims, ((), ())), preferred_element_type=F32)


def _dot_nn(a, b):
    return _dot(a, b, ((1,), (0,)))


def _dot_nt(a, b):
    return _dot(a, b, ((1,), (1,)))


def _dot_tn(a, b):
    return _dot(a, b, ((0,), (0,)))


def _mm(a, b, *, mode, name, outs, tm, tn, tk, epi=None, extras=(), b_col0=0, n_out=None, after=None,
        out_col0=0, out_cols=None, into=None):
    if mode == "nn":
        m, kdim = a.shape
        p, kb, ns = b.shape
        assert kb == kdim and ns % tn == 0 and b_col0 % tn == 0
        n = n_out if n_out is not None else p * ns
        npt, j0 = ns // tn, b_col0 // tn
        a_spec = pl.BlockSpec((tm, tk), lambda i, j, k: (i, k))
        b_spec = pl.BlockSpec((None, tk, tn), lambda i, j, k: ((j + j0) // npt, k, (j + j0) % npt))
        dot = _dot_nn
    elif mode == "nt":
        m, kdim = a.shape
        p, n, ns = b.shape
        assert ns % tk == 0 and b_col0 % tk == 0
        npt, j0 = ns // tk, b_col0 // tk
        a_spec = pl.BlockSpec((tm, tk), lambda i, j, k: (i, k))
        b_spec = pl.BlockSpec((None, tn, tk), lambda i, j, k: ((k + j0) // npt, j, (k + j0) % npt))
        dot = _dot_nt
    else:
        kdim, m = a.shape
        kb, n = b.shape
        assert kb == kdim
        a_spec = pl.BlockSpec((tk, tm), lambda i, j, k: (k, i))
        b_spec = pl.BlockSpec((tk, tn), lambda i, j, k: (k, j))
        dot = _dot_tn
    assert m % tm == 0 and n % tn == 0 and kdim % tk == 0, (name, m, n, kdim, tm, tn, tk)
    nk = kdim // tk
    ex_specs, ex_arrays = [], []
    for kind, arr in extras:
        if kind == "row":
            ex_specs.append(pl.BlockSpec((1, tn), lambda i, j, k: (0, j)))
        else:
            ex_specs.append(pl.BlockSpec((tm, tn), lambda i, j, k: (i, j)))
        ex_arrays.append(arr)
    n_ex, n_o = len(ex_arrays), len(outs)
    deps = [d for d in (after, into) if d is not None]
    n_dep = len(deps)
    j_out = out_col0 // tn
    assert out_col0 % tn == 0 and (into is None or len(outs) == 1)

    def body(a_ref, b_ref, *rest):
        ex_refs, o_refs = rest[:n_ex], rest[n_ex + n_dep:n_ex + n_dep + n_o]
        k = pl.program_id(2)

        chunks = [slice(r0, r0 + min(tm, MM_ROW_CHUNK)) for r0 in range(0, tm, min(tm, MM_ROW_CHUNK))]

        def part(rows):
            return dot(a_ref[:, rows] if mode == "tn" else a_ref[rows, :], b_ref[...])

        def finish(r, rows):
            exs = [e[...] if kind == "row" else e[rows, :] for (kind, _), e in zip(extras, ex_refs)]
            vals = epi(r, *exs) if epi is not None else [r]
            for o, v in zip(o_refs, vals):
                o[rows, :] = v.astype(o.dtype)

        if nk == 1:
            for rows in chunks:
                finish(part(rows), rows)
            return
        acc = rest[n_ex + n_dep + n_o]

        @pl.when(k == 0)
        def _():
            for rows in chunks:
                acc[rows, :] = part(rows)

        @pl.when((k > 0) & (k < nk - 1))
        def _():
            for rows in chunks:
                acc[rows, :] += part(rows)

        @pl.when(k == nk - 1)
        def _():
            for rows in chunks:
                finish(acc[rows, :] + part(rows), rows)

    res = pl.pallas_call(
        body,
        grid=(m // tm, n // tn, nk),
        in_specs=[a_spec, b_spec] + ex_specs + [pl.BlockSpec(memory_space=pl.ANY)] * n_dep,
        out_specs=[pl.BlockSpec((tm, tn), lambda i, j, k: (i, j + j_out)) for _ in outs],
        out_shape=[jax.ShapeDtypeStruct((m, out_cols or n), dt) for dt in outs],
        input_output_aliases={} if into is None else {2 + n_ex + n_dep - 1: 0},
        scratch_shapes=[pltpu.VMEM((tm, tn), F32)] if nk > 1 else [],
        name=name,
        compiler_params=_cparams(("parallel", "parallel", "arbitrary")),
    )(a, b, *ex_arrays, *deps)
    return res if len(outs) > 1 else res[0]


def _rows(body, n_rows, tr, ins, outs, name, scratch=()):
    def spec(kind, shape):
        if kind == "blk":
            return pl.BlockSpec((tr,) + tuple(shape[1:]), lambda i: (i,) + (0,) * (len(shape) - 1))
        if kind == "dep":
            return pl.BlockSpec(memory_space=pl.ANY)
        return pl.BlockSpec(tuple(shape), lambda i: (0,) * len(shape))

    return pl.pallas_call(
        body,
        grid=(n_rows // tr,),
        in_specs=[spec(k, a.shape) for k, a in ins],
        out_specs=[spec(k, s) for k, s, _ in outs],
        out_shape=[jax.ShapeDtypeStruct(tuple(s), d) for _, s, d in outs],
        scratch_shapes=list(scratch),
        name=name,
        compiler_params=_cparams(("arbitrary",)),
    )(*[a for _, a in ins])


def _ln_stats(z):
    mu = jnp.mean(z, axis=-1, keepdims=True)
    zc = z - mu
    var = jnp.mean(zc * zc, axis=-1, keepdims=True)
    rstd = lax.rsqrt(var + LN_EPS)
    return zc * rstd, rstd


def _mod(x, scale, shift, after, name):
    s, d = x.shape

    def body(x_ref, sc_ref, sh_ref, dep_ref, h_ref):
        h_ref[...] = (x_ref[...] * (1.0 + sc_ref[...]) + sh_ref[...]).astype(h_ref.dtype)

    return _rows(body, s, ROW_TILE, [("blk", x), ("all", scale), ("all", shift), ("dep", after)], [("blk", (s, d), MXU_DTYPE)], name)[0]


def _resid_ln(x, y, gate, g, b, nxt, name):
    s, d = x.shape

    def body(x_ref, y_ref, gate_ref, g_ref, b_ref, sc_ref, sh_ref, xn_ref, h_ref):
        z = ALPHA * x_ref[...] + gate_ref[...] * y_ref[...]
        xhat, _ = _ln_stats(z)
        xn = xhat * g_ref[...] + b_ref[...]
        xn_ref[...] = xn
        h_ref[...] = (xn * (1.0 + sc_ref[...]) + sh_ref[...]).astype(h_ref.dtype)

    return _rows(body, s, ROW_TILE,
                 [("blk", x), ("blk", y), ("all", gate), ("all", g), ("all", b), ("all", nxt[0]), ("all", nxt[1])],
                 [("blk", (s, d), F32), ("blk", (s, d), MXU_DTYPE)], name)


def _mod_bwd(dxr, dhs, x, scale, name, after=None):
    s, d = x.shape
    n_dh = len(dhs)
    n_dep = 0 if after is None else 1

    def body(dxr_ref, *rest):
        dh_refs = rest[:n_dh]
        x_ref, sc_ref, dx_ref, red_ref, a_sh, a_sc = rest[n_dh:n_dh + 2] + rest[n_dh + 2 + n_dep:]
        i = pl.program_id(0)

        @pl.when(i == 0)
        def _():
            a_sh[...] = jnp.zeros_like(a_sh)
            a_sc[...] = jnp.zeros_like(a_sc)

        dh = dh_refs[0][...]
        for r in dh_refs[1:]:
            dh = dh + r[...]
        dx_ref[...] = dxr_ref[...] + dh * (1.0 + sc_ref[...])
        a_sh[...] += _fold8(dh)
        a_sc[...] += _fold8(dh * x_ref[...])

        @pl.when(i == pl.num_programs(0) - 1)
        def _():
            red_ref[...] = jnp.zeros_like(red_ref)
            red_ref[0:1, :] = jnp.sum(a_sh[...], axis=0, keepdims=True)
            red_ref[1:2, :] = jnp.sum(a_sc[...], axis=0, keepdims=True)

    return _rows(body, s, ROW_TILE, [("blk", dxr)] + [("blk", h) for h in dhs] + [("blk", x), ("all", scale)] + [("dep", after)] * n_dep,
                 [("blk", (s, d), F32), ("all", (SUBLANES, d), F32)], name,
                 scratch=[pltpu.VMEM((SUBLANES, d), F32)] * 2)


def _last_ln_loss_bwd(x, y, gate, g, b, target, name):
    s, d = x.shape

    def body(x_ref, y_ref, gate_ref, g_ref, b_ref, t_ref, l_ref, dxr_ref, dyy_ref, red_ref, a_l, a_g, a_b, a_gate):
        i = pl.program_id(0)

        @pl.when(i == 0)
        def _():
            for a in (a_l, a_g, a_b, a_gate):
                a[...] = jnp.zeros_like(a)

        yv = y_ref[...]
        z = ALPHA * x_ref[...] + gate_ref[...] * yv
        xhat, rstd = _ln_stats(z)
        e = xhat * g_ref[...] + b_ref[...] - t_ref[...]
        a_l[...] += _fold8(e * e)
        dxo_v = e * (1.0 / d)
        dxh = dxo_v * g_ref[...]
        dz = rstd * (dxh - jnp.mean(dxh, axis=-1, keepdims=True) - xhat * jnp.mean(dxh * xhat, axis=-1, keepdims=True))
        dxr_ref[...] = ALPHA * dz
        dyy_ref[...] = (gate_ref[...] * dz).astype(dyy_ref.dtype)
        a_g[...] += _fold8(dxo_v * xhat)
        a_b[...] += _fold8(dxo_v)
        a_gate[...] += _fold8(dz * yv)

        @pl.when(i == pl.num_programs(0) - 1)
        def _():
            l_ref[...] = jnp.full(l_ref.shape, 0.5 / d, F32) * jnp.sum(a_l[...])
            red_ref[...] = jnp.zeros_like(red_ref)
            red_ref[0:1, :] = jnp.sum(a_g[...], axis=0, keepdims=True)
            red_ref[1:2, :] = jnp.sum(a_b[...], axis=0, keepdims=True)
            red_ref[2:3, :] = jnp.sum(a_gate[...], axis=0, keepdims=True)

    l, dxr, dyy, red = _rows(
        body, s, ROW_TILE, [("blk", x), ("blk", y), ("all", gate), ("all", g), ("all", b), ("blk", target)],
        [("all", (SUBLANES, LANES), F32), ("blk", (s, d), F32), ("blk", (s, d), MXU_DTYPE), ("all", (SUBLANES, d), F32)], name,
        scratch=[pltpu.VMEM((SUBLANES, d), F32)] * 4)
    return l[0, 0], dxr, dyy, red


def _mod_ln_bwd(dxr, dhs, x, scale, x_in, y, gate, g, name, after=None):
    s, d = x.shape
    n_dh = len(dhs)
    n_dep = 0 if after is None else 1

    def body(dxr_ref, *rest):
        dh_refs = rest[:n_dh]
        x_ref, sc_ref, xin_ref, y_ref, gate_ref, g_ref = rest[n_dh:n_dh + 6]
        dxr_out, dyy_ref, red_mod, red_ln, a_sh, a_sc, a_g, a_b, a_gate = rest[n_dh + 6 + n_dep:]
        i = pl.program_id(0)

        @pl.when(i == 0)
        def _():
            for a in (a_sh, a_sc, a_g, a_b, a_gate):
                a[...] = jnp.zeros_like(a)

        dh = dh_refs[0][...]
        for r in dh_refs[1:]:
            dh = dh + r[...]
        xv = x_ref[...]
        dxo_v = dxr_ref[...] + dh * (1.0 + sc_ref[...])
        a_sh[...] += _fold8(dh)
        a_sc[...] += _fold8(dh * xv)
        yv = y_ref[...]
        z = ALPHA * xin_ref[...] + gate_ref[...] * yv
        xhat, rstd = _ln_stats(z)
        dxh = dxo_v * g_ref[...]
        dz = rstd * (dxh - jnp.mean(dxh, axis=-1, keepdims=True) - xhat * jnp.mean(dxh * xhat, axis=-1, keepdims=True))
        dxr_out[...] = ALPHA * dz
        dyy_ref[...] = (gate_ref[...] * dz).astype(dyy_ref.dtype)
        a_g[...] += _fold8(dxo_v * xhat)
        a_b[...] += _fold8(dxo_v)
        a_gate[...] += _fold8(dz * yv)

        @pl.when(i == pl.num_programs(0) - 1)
        def _():
            red_mod[...] = jnp.zeros_like(red_mod)
            red_mod[0:1, :] = jnp.sum(a_sh[...], axis=0, keepdims=True)
            red_mod[1:2, :] = jnp.sum(a_sc[...], axis=0, keepdims=True)
            red_ln[...] = jnp.zeros_like(red_ln)
            red_ln[0:1, :] = jnp.sum(a_g[...], axis=0, keepdims=True)
            red_ln[1:2, :] = jnp.sum(a_b[...], axis=0, keepdims=True)
            red_ln[2:3, :] = jnp.sum(a_gate[...], axis=0, keepdims=True)

    ins = ([("blk", dxr)] + [("blk", h) for h in dhs]
           + [("blk", x), ("all", scale), ("blk", x_in), ("blk", y), ("all", gate), ("all", g)] + [("dep", after)] * n_dep)
    return _rows(body, s, ROW_TILE, ins,
                 [("blk", (s, d), F32), ("blk", (s, d), MXU_DTYPE), ("all", (SUBLANES, d), F32), ("all", (SUBLANES, d), F32)], name,
                 scratch=[pltpu.VMEM((SUBLANES, d), F32)] * 5)


def _left_half(shape):
    return lax.broadcasted_iota(jnp.int32, shape, 1) < (LANES // 2)


CHUNKS_PER_STEP = 2


def _chunks_of_step():
    return [slice(i * CHUNK, (i + 1) * CHUNK) for i in range(CHUNKS_PER_STEP)]


def _spatial_z(vn, wc_ref, bias_ref, j):
    vb = vn[:, j * LANES:(j + 1) * LANES]
    z0 = _dot_nn(wc_ref[2 * j], vb)
    z1 = _dot_nn(wc_ref[2 * j + 1], vb)
    return jnp.where(_left_half(z0.shape), z0, z1) + bias_ref[:, j * LANES:(j + 1) * LANES]


def _spatial_fwd(uvpre, vn_g, vn_b, wc, bias_full, name):
    s, d2 = uvpre.shape
    d = d2 // 2

    def body(uv_ref, g_ref, b_ref, wc_ref, bias_ref, out_ref):
        for rows in _chunks_of_step():
            u = _gelu(uv_ref[rows, :d])
            v = _gelu(uv_ref[rows, d:])
            vh, _ = _ln_stats(v)
            vn = vh * g_ref[...] + b_ref[...]
            for j in range(d // LANES):
                z = _spatial_z(vn, wc_ref, bias_ref, j)
                out_ref[rows, j * LANES:(j + 1) * LANES] = (u[:, j * LANES:(j + 1) * LANES] * z).astype(out_ref.dtype)

    return _rows(body, s, CHUNKS_PER_STEP * CHUNK, [("blk", uvpre), ("all", vn_g), ("all", vn_b), ("all", wc), ("all", bias_full)],
                 [("blk", (s, d), MXU_DTYPE)], name)[0]


def _spatial_bwd(uvpre, dgated, vn_g, vn_b, wc, wct, bias_full, name):
    s, d2 = uvpre.shape
    d = d2 // 2

    def body(uv_ref, dg_ref, g_ref, b_ref, wc_ref, wct_ref, bias_ref,
             duv_ref, dws_ref, dbias_ref, dbin_ref, dvg_ref, dvb_ref, dvn_buf, a_bin, a_vg, a_vb):
        i = pl.program_id(0)

        @pl.when(i == 0)
        def _():
            dws_ref[...] = jnp.zeros_like(dws_ref)
            dbias_ref[...] = jnp.zeros_like(dbias_ref)
            a_bin[...] = jnp.zeros_like(a_bin)
            a_vg[...] = jnp.zeros_like(a_vg)
            a_vb[...] = jnp.zeros_like(a_vb)

        for rows in _chunks_of_step():
            up = uv_ref[rows, :d]
            vp = uv_ref[rows, d:]
            u = _gelu(up)
            v = _gelu(vp)
            vh, rstd = _ln_stats(v)
            vn = vh * g_ref[...] + b_ref[...]
            dg = dg_ref[rows, :]
            dzz = dg * u
            dbias_ref[...] += dzz
            for j in range(d // LANES):
                cols = slice(j * LANES, (j + 1) * LANES)
                z = _spatial_z(vn, wc_ref, bias_ref, j)
                dup = dg[:, cols] * z * _gelu_grad(up[:, cols])
                duv_ref[rows, cols] = dup.astype(duv_ref.dtype)
                a_bin[:, cols] += _fold8(dup)
                dzb = dzz[:, cols]
                left = _left_half(dzb.shape)
                dvn_buf[:, cols] = jnp.where(left, _dot_nn(wct_ref[2 * j], dzb), _dot_nn(wct_ref[2 * j + 1], dzb))
                vb = vn[:, cols]
                dws_ref[2 * j] += _dot_nt(jnp.where(left, dzb, 0.0), vb)
                dws_ref[2 * j + 1] += _dot_nt(jnp.where(left, 0.0, dzb), vb)
            dvn = dvn_buf[...]
            a_vg[...] += _fold8(dvn * vh)
            a_vb[...] += _fold8(dvn)
            dvh = dvn * g_ref[...]
            dv = rstd * (dvh - jnp.mean(dvh, axis=-1, keepdims=True) - vh * jnp.mean(dvh * vh, axis=-1, keepdims=True))
            dvp = dv * _gelu_grad(vp)
            duv_ref[rows, d:] = dvp.astype(duv_ref.dtype)
            a_bin[:, d:] += _fold8(dvp)

        @pl.when(i == pl.num_programs(0) - 1)
        def _():
            dbin_ref[...] = jnp.sum(a_bin[...], axis=0, keepdims=True)
            dvg_ref[...] = jnp.sum(a_vg[...], axis=0, keepdims=True)
            dvb_ref[...] = jnp.sum(a_vb[...], axis=0, keepdims=True)

    return _rows(body, s, CHUNKS_PER_STEP * CHUNK,
                 [("blk", uvpre), ("blk", dgated), ("all", vn_g), ("all", vn_b), ("all", wc), ("all", wct), ("all", bias_full)],
                 [("blk", (s, d2), MXU_DTYPE), ("all", (A_GROUPS, CHUNK, CHUNK), F32), ("all", (CHUNK, d), F32),
                  ("all", (1, d2), F32), ("all", (1, d), F32), ("all", (1, d), F32)], name,
                 scratch=[pltpu.VMEM((CHUNK, d), F32), pltpu.VMEM((SUBLANES, d2), F32),
                          pltpu.VMEM((SUBLANES, d), F32), pltpu.VMEM((SUBLANES, d), F32)])


def _head_mask(v, h):
    lane = lax.broadcasted_iota(jnp.int32, v.shape, 1)
    return jnp.where((lane >= h * HEAD_DIM) & (lane < (h + 1) * HEAD_DIM), v, jnp.zeros_like(v))


def _att_bias(slopes, dil):
    qi = lax.broadcasted_iota(jnp.int32, (SPAN, SPAN), 0)
    ki = lax.broadcasted_iota(jnp.int32, (SPAN, SPAN), 1)
    sl = slopes[:, None, None]
    cur = jnp.where(ki <= qi, -sl * (float(dil) * (qi - ki).astype(F32)), NEG)
    prev = jnp.where(ki >= qi, -sl * (float(dil) * (SPAN + qi - ki).astype(F32)), NEG)
    absent = jnp.full_like(prev, NEG)
    pairs = slopes.shape[0] // 2

    def fwd(pv):
        return jnp.concatenate([cur, pv], axis=2).reshape(pairs, 2 * SPAN, 2 * SPAN)

    def bwd(pv):
        return jnp.concatenate([cur.reshape(pairs, 2 * SPAN, SPAN), pv.reshape(pairs, 2 * SPAN, SPAN)], axis=1)

    return jnp.stack([fwd(absent), fwd(prev)]), jnp.stack([bwd(absent), bwd(prev)])


def _att_specs(s, d, dil, kinds):
    nb = s // (dil * SPAN)

    def rowblk(which, b):
        if which == "prev":
            return jnp.where(b % nb == 0, b, b - 1)
        if which == "next":
            return jnp.where(b % nb == nb - 1, b, b + 1)
        return b

    return [pl.BlockSpec((SPAN, d), functools.partial(lambda b, o, w: (rowblk(w, b), o), o=part, w=which))
            for part, which in kinds]


def _head_col(v, head):
    return v[:, head:head + 1]


def _expand_heads(w, j):
    shape = (w.shape[0], LANES)
    return jnp.where(_left_half(shape), jnp.broadcast_to(_head_col(w, 2 * j), shape), jnp.broadcast_to(_head_col(w, 2 * j + 1), shape))


def _attn_fwd(qkv, slopes, dil, name):
    s, d3 = qkv.shape
    d = d3 // 3
    nb = s // (dil * SPAN)
    table, _ = _att_bias(slopes, dil)

    def body(q_ref, kc_ref, kp_ref, vc_ref, vp_ref, tb_ref, o_ref, l_ref):
        left = _left_half((SPAN, LANES))
        lane = lax.broadcasted_iota(jnp.int32, (SPAN, LANES), 1)
        lses = jnp.zeros((SPAN, LANES), F32)
        for hp in range(d // LANES):
            cols = slice(hp * LANES, (hp + 1) * LANES)
            q = q_ref[:, cols]
            q2 = jnp.concatenate([_head_mask(q, 0), _head_mask(q, 1)], axis=0) * ATT_SCALE
            k2 = jnp.concatenate([kc_ref[:, cols], kp_ref[:, cols]], axis=0)
            v2 = jnp.concatenate([vc_ref[:, cols], vp_ref[:, cols]], axis=0)
            sc = _dot_nt(q2, k2) + tb_ref[hp]
            m = jnp.max(sc, axis=-1, keepdims=True)
            p = jnp.exp(sc - m)
            l = jnp.sum(p, axis=-1, keepdims=True)
            r = _dot_nn(p, v2) * (1.0 / l)
            lse = m + jnp.log(l)
            o_ref[:, cols] = jnp.where(left, r[:SPAN], r[SPAN:])
            lses = jnp.where(lane == 2 * hp, lse[:SPAN], jnp.where(lane == 2 * hp + 1, lse[SPAN:], lses))
        l_ref[...] = lses

    specs = _att_specs(s, d, dil, [(0, "cur"), (1, "cur"), (1, "prev"), (2, "cur"), (2, "prev")])
    tbl = pl.BlockSpec((None,) + table.shape[1:], lambda b: (jnp.where(b % nb == 0, 0, 1), 0, 0, 0))
    out_spec = pl.BlockSpec((SPAN, d), lambda b: (b, 0))
    return pl.pallas_call(
        body,
        grid=(s // SPAN,),
        in_specs=specs + [tbl],
        out_specs=[out_spec, pl.BlockSpec((SPAN, LANES), lambda b: (b, 0))],
        out_shape=[jax.ShapeDtypeStruct((s, d), F32), jax.ShapeDtypeStruct((s, LANES), F32)],
        name=name,
        compiler_params=_cparams(("parallel",)),
    )(qkv, qkv, qkv, qkv, qkv, table)


def _attn_bwd(qkv, do, lse, dd, slopes, dil, name):
    s, d3 = qkv.shape
    d = d3 // 3
    nb = s // (dil * SPAN)
    _, table = _att_bias(slopes, dil)

    def heads_stacked(cur, nxt):
        return jnp.concatenate([_head_mask(cur, 0), _head_mask(cur, 1), _head_mask(nxt, 0), _head_mask(nxt, 1)], axis=0)

    def cols_stacked(cur, nxt, hp):
        return jnp.concatenate([jnp.broadcast_to(_head_col(a, 2 * hp + h), (SPAN, LANES)) for a in (cur, nxt) for h in range(2)], axis=0)

    def body(k_ref, v_ref, qc_ref, qn_ref, doc_ref, don_ref, lc_ref, ln_ref, ddc_ref, ddn_ref, tb_ref, out_ref, carry):
        b = pl.program_id(0)

        @pl.when(b == 0)
        def _():
            carry[...] = jnp.zeros_like(carry)

        left = _left_half((SPAN, LANES))
        lse_c, lse_n, dd_c, dd_n = lc_ref[...], ln_ref[...], ddc_ref[...], ddn_ref[...]
        for hp in range(d // LANES):
            cols = slice(hp * LANES, (hp + 1) * LANES)
            k, v = k_ref[:, cols], v_ref[:, cols]
            q4 = heads_stacked(qc_ref[:, cols], qn_ref[:, cols])
            do4 = heads_stacked(doc_ref[:, cols], don_ref[:, cols])
            sc = _dot_nt(q4 * ATT_SCALE, k) + tb_ref[hp]
            p = jnp.exp(sc - cols_stacked(lse_c, lse_n, hp))
            ds = p * (_dot_nt(do4, v) - cols_stacked(dd_c, dd_n, hp))
            dq4 = _dot_nn(ds, k)
            dq_cur = jnp.where(left, dq4[:SPAN], dq4[SPAN:2 * SPAN]) + carry[:, cols]
            carry[:, cols] = jnp.where(left, dq4[2 * SPAN:3 * SPAN], dq4[3 * SPAN:])
            out_ref[:, cols] = (dq_cur * ATT_SCALE).astype(out_ref.dtype)
            out_ref[:, d + hp * LANES:d + (hp + 1) * LANES] = (_dot_tn(ds, q4) * ATT_SCALE).astype(out_ref.dtype)
            out_ref[:, 2 * d + hp * LANES:2 * d + (hp + 1) * LANES] = _dot_tn(p, do4).astype(out_ref.dtype)

    qkv_specs = _att_specs(s, d, dil, [(1, "cur"), (2, "cur"), (0, "cur"), (0, "next")])
    pair = _att_specs(s, d, dil, [(0, "cur"), (0, "next")])
    heads = _att_specs(s, LANES, dil, [(0, "cur"), (0, "next")])
    tbl = pl.BlockSpec((None,) + table.shape[1:], lambda b: (jnp.where(b % nb == nb - 1, 0, 1), 0, 0, 0))
    return pl.pallas_call(
        body,
        grid=(s // SPAN,),
        in_specs=qkv_specs + pair + heads + heads + [tbl],
        out_specs=pl.BlockSpec((SPAN, d3), lambda b: (b, 0)),
        out_shape=jax.ShapeDtypeStruct((s, d3), MXU_DTYPE),
        scratch_shapes=[pltpu.VMEM((SPAN, d), F32)],
        name=name,
        compiler_params=_cparams(("arbitrary",)),
    )(qkv, qkv, qkv, qkv, do, do, lse, lse, dd, dd, table)


def _mix_weights(l_refs):
    ls = [r[...] for r in l_refs]
    m = functools.reduce(jnp.maximum, ls)
    es = [jnp.exp(l - m) for l in ls]
    tot = functools.reduce(lambda a, c: a + c, es)
    return [e / tot for e in es]


def _combine_fwd(os_, ls_, name):
    s, d = os_[0].shape
    n = len(os_)

    def body(*refs):
        o_refs, l_refs, out_ref = refs[:n], refs[n:2 * n], refs[2 * n]
        ws = _mix_weights(l_refs)
        for j in range(d // LANES):
            cols = slice(j * LANES, (j + 1) * LANES)
            acc = _expand_heads(ws[0], j) * o_refs[0][:, cols]
            for w, o in zip(ws[1:], o_refs[1:]):
                acc = acc + _expand_heads(w, j) * o[:, cols]
            out_ref[:, cols] = acc

    return _rows(body, s, ROW_TILE, [("blk", a) for a in os_ + ls_], [("blk", (s, d), F32)], name)[0]


def _combine_bwd(do, o, ls_, name):
    s, d = o.shape
    n = len(ls_)
    sel = (lax.broadcasted_iota(jnp.int32, (d, LANES), 0) // HEAD_DIM == lax.broadcasted_iota(jnp.int32, (d, LANES), 1)).astype(F32)

    def body(do_ref, o_ref, *rest):
        l_refs, sel_ref, outs = rest[:n], rest[n], rest[n + 1:]
        ws = _mix_weights(l_refs)
        dov = do_ref[...]
        r = jnp.dot(dov * o_ref[...], sel_ref[...], precision=lax.Precision.HIGHEST, preferred_element_type=F32)
        for g in range(n):
            outs[2 * g + 1][...] = ws[g] * r
            for j in range(d // LANES):
                cols = slice(j * LANES, (j + 1) * LANES)
                outs[2 * g][:, cols] = (_expand_heads(ws[g], j) * dov[:, cols]).astype(outs[2 * g].dtype)

    outs = []
    for _ in range(n):
        outs += [("blk", (s, d), MXU_DTYPE), ("blk", (s, LANES), F32)]
    res = _rows(body, s, ROW_TILE, [("blk", do), ("blk", o)] + [("blk", l) for l in ls_] + [("all", sel)], outs, name)
    return [(res[2 * g], res[2 * g + 1]) for g in range(n)]


def _ada_fwd(c_all, w, b, name):
    nsub, d, cs = w.shape

    def body(c_ref, w_ref, b_ref, o_ref):
        cv = c_ref[...]
        sc = cv * (1.0 / (1.0 + jnp.exp(-cv)))
        o_ref[...] = _dot_nn(sc, w_ref[...]) + b_ref[...]

    return pl.pallas_call(
        body,
        grid=(nsub,),
        in_specs=[pl.BlockSpec(c_all.shape, lambda i: (0, 0)), pl.BlockSpec((None, d, cs), lambda i: (i, 0, 0)),
                  pl.BlockSpec((None, 1, cs), lambda i: (i, 0, 0))],
        out_specs=pl.BlockSpec((None, N_DEV, cs), lambda i: (i, 0, 0)),
        out_shape=jax.ShapeDtypeStruct((nsub, N_DEV, cs), F32),
        name=name,
        compiler_params=_cparams(("parallel",)),
    )(c_all, w, b)


def _ada_bwd(c_all_t, dm, name):
    d, nb = c_all_t.shape
    nsub, _, cs = dm.shape

    def body(c_ref, dm_ref, o_ref):
        cv = c_ref[...]
        sc = cv * (1.0 / (1.0 + jnp.exp(-cv)))
        acc = sc[:, 0:1] * dm_ref[0:1, :]
        for bi in range(1, nb):
            acc = acc + sc[:, bi:bi + 1] * dm_ref[bi:bi + 1, :]
        o_ref[...] = acc

    return pl.pallas_call(
        body,
        grid=(nsub,),
        in_specs=[pl.BlockSpec(c_all_t.shape, lambda i: (0, 0)), pl.BlockSpec((None, nb, cs), lambda i: (i, 0, 0))],
        out_specs=pl.BlockSpec((None, d, cs), lambda i: (i, 0, 0)),
        out_shape=jax.ShapeDtypeStruct((nsub, d, cs), F32),
        name=name,
        compiler_params=_cparams(("parallel",)),
    )(c_all_t, dm)


def _row_tile(r, row_elems, block_elems=256 * 1024):
    t = 2 * SUBLANES
    if r % t:
        return r
    while t * 2 * row_elems <= block_elems and r % (t * 2) == 0:
        t *= 2
    return t


def _adamw(w, g, m, v, name):
    shape = w.shape
    c = shape[-1]
    r = w.size // c
    tr = _row_tile(r, c, 512 * 1024)
    w2, g2, m2, v2 = [a.reshape(r, c) for a in (w, g, m, v)]
    bc1 = 1.0 - ADAM_B1 ** ADAM_STEP
    bc2 = 1.0 - ADAM_B2 ** ADAM_STEP

    def body(w_ref, g_ref, m_ref, v_ref, d_ref, nm_ref, nv_ref):
        gv = g_ref[...]
        nm = ADAM_B1 * m_ref[...] + (1.0 - ADAM_B1) * gv
        nv = ADAM_B2 * v_ref[...] + (1.0 - ADAM_B2) * (gv * gv)
        d_ref[...] = -ADAM_LR * ((nm / bc1) / (jnp.sqrt(nv / bc2) + ADAM_EPS) + ADAM_WD * w_ref[...])
        nm_ref[...] = nm
        nv_ref[...] = nv

    res = _rows(body, r, tr, [("blk", a) for a in (w2, g2, m2, v2)], [("blk", (r, c), F32)] * 3, name)
    return [a.reshape(shape) for a in res]


def _sum_slots(buf, name):
    n, r, c = buf.shape
    tr = _row_tile(r, n * c, 2 * 1024 * 1024)

    def body(b_ref, o_ref):
        acc = b_ref[0].astype(F32)
        for k in range(1, n):
            acc = acc + b_ref[k].astype(F32)
        o_ref[...] = acc

    return pl.pallas_call(
        body,
        grid=(r // tr,),
        in_specs=[pl.BlockSpec((n, tr, c), lambda i: (0, i, 0))],
        out_specs=pl.BlockSpec((tr, c), lambda i: (i, 0)),
        out_shape=jax.ShapeDtypeStruct((r, c), F32),
        name=name,
        compiler_params=_cparams(("parallel",)),
    )(buf)


def _me():
    return lax.axis_index("x"), lax.axis_index("y"), lax.axis_index("c")


def _all_gather_small(blk, name, after=()):
    m_per, n = blk.shape

    def body(x_ref, *rest):
        out_ref, send_sems, recv_sems, local_sem = rest[len(after):]
        x, y, c = _me()
        me, sibling = (x, y, c), (x, y, 1 - c)
        chips = [(1 - x, y), (x, 1 - y), (1 - x, 1 - y)]

        def rows(px, py, pc):
            return out_ref.at[pl.ds((4 * px + 2 * py + pc) * m_per, m_per), :]

        def copy(k, block, to, src=None):
            return pltpu.make_async_remote_copy(
                src_ref=rows(*block) if src is None else src, dst_ref=rows(*block),
                send_sem=send_sems.at[k], recv_sem=recv_sems.at[k], device_id=to, device_id_type=MESH)

        mine = pltpu.make_async_copy(x_ref, rows(*me), local_sem)
        mine.start()
        first = [copy(0, me, sibling, src=x_ref)]
        first += [copy(1 + j, me, (*chip, c), src=x_ref) for j, chip in enumerate(chips)]
        for cp in first:
            cp.start()
        passed = [copy(4 + j, (*chip, c), sibling) for j, chip in enumerate(chips)]
        for j, chip in enumerate(chips):
            copy(1 + j, (*chip, c), me).wait_recv()
            passed[j].start()
        copy(0, sibling, me).wait_recv()
        for j, chip in enumerate(chips):
            copy(4 + j, (*chip, 1 - c), me).wait_recv()
        for cp in first + passed:
            cp.wait_send()
        mine.wait()

    return pl.pallas_call(
        body,
        out_shape=jax.ShapeDtypeStruct((N_DEV * m_per, n), blk.dtype),
        in_specs=[pl.BlockSpec(memory_space=pltpu.VMEM)] + [pl.BlockSpec(memory_space=pl.ANY)] * len(after),
        out_specs=pl.BlockSpec(memory_space=pltpu.VMEM),
        scratch_shapes=[pltpu.SemaphoreType.DMA((7,)), pltpu.SemaphoreType.DMA((7,)), pltpu.SemaphoreType.DMA],
        name=name,
        compiler_params=pltpu.CompilerParams(vmem_limit_bytes=VMEM_LIMIT),
    )(blk, *after)


_HBM = pl.BlockSpec(memory_space=pltpu.HBM)
_SEM = pl.BlockSpec(memory_space=pltpu.SEMAPHORE)
_EFFECT = pltpu.SideEffectType.DATAFLOW_SIDE_EFFECTING


def _other_chips(x, y):
    return [(1 - x, y), (x, 1 - y), (1 - x, 1 - y)]


def _gather_copy(w, j, src_ref, land_ref, send_sems, recv_sems, halved=False):
    x, y, c = _me()
    if halved:
        half = src_ref.shape[0] // 2
        src_ref = src_ref.at[pl.ds(c * half, half), :]
    return pltpu.make_async_remote_copy(
        src_ref=src_ref, dst_ref=land_ref.at[2 * x + y], send_sem=send_sems.at[3 * w + j], recv_sem=recv_sems.at[3 * w + j],
        device_id=(*_other_chips(x, y)[j], c), device_id_type=MESH)


def _gather_start(shards, halved, after, name):
    n = len(shards)
    lands = [lax.empty((N_CHIPS, s.shape[0] // 2 if w in halved else s.shape[0], s.shape[1]), s.dtype) for w, s in enumerate(shards)]

    def body(*refs):
        in_refs, land_refs = refs[:n], refs[n:2 * n]
        send_sems, recv_sems = refs[2 * n + 1], refs[2 * n + 2]
        token = refs[-1]
        for w in range(n):
            for j in range(3):
                _gather_copy(w, j, in_refs[w], land_refs[w], send_sems, recv_sems, w in halved).start()
        token[...] = jnp.zeros_like(token)

    res = pl.pallas_call(
        body,
        out_shape=(pltpu.SemaphoreType.DMA((3 * n,)), pltpu.SemaphoreType.DMA((3 * n,)),
                   *[pltpu.HBM(s.shape, s.dtype) for s in shards], *[pltpu.HBM(l.shape, l.dtype) for l in lands],
                   jax.ShapeDtypeStruct((SUBLANES, LANES), F32)),
        in_specs=[_HBM] * (2 * n) + [pl.BlockSpec(memory_space=pl.ANY)],
        out_specs=(_SEM, _SEM, *[_HBM] * (2 * n), pl.BlockSpec(memory_space=pltpu.VMEM)),
        input_output_aliases={i: 2 + i for i in range(2 * n)},
        name=name,
        compiler_params=pltpu.CompilerParams(has_side_effects=_EFFECT),
    )(*[pltpu.with_memory_space_constraint(a, pltpu.HBM) for a in list(shards) + lands], after)
    return res[0], res[1], res[2:2 + n], res[2 + n:2 + 2 * n], res[-1]


def _gather_wait(w, shard, land, send_sems, recv_sems, after, name, halved=False):
    def body(s_ref, land_ref, send_sems, recv_sems, after_ref, s_out, land_out, stage):
        x, y, _ = _me()
        if not halved:
            pltpu.sync_copy(s_ref, stage)
            pltpu.sync_copy(stage, land_out.at[2 * x + y])
        for j in range(3):
            cp = _gather_copy(w, j, s_ref, land_ref, send_sems, recv_sems, halved)
            cp.wait_send()
            cp.wait_recv()

    return pl.pallas_call(
        body,
        out_shape=(pltpu.HBM(shard.shape, shard.dtype), pltpu.HBM(land.shape, land.dtype)),
        in_specs=(_HBM, _HBM, _SEM, _SEM, pl.BlockSpec(memory_space=pl.ANY)),
        out_specs=(_HBM, _HBM),
        input_output_aliases={0: 0, 1: 1},
        scratch_shapes=[pltpu.VMEM((SUBLANES, LANES) if halved else shard.shape, shard.dtype)],
        name=name,
        compiler_params=pltpu.CompilerParams(has_side_effects=_EFFECT, vmem_limit_bytes=VMEM_LIMIT),
    )(shard, land, send_sems, recv_sems, after)


def _assemble_halves(shard, land, name):
    half = land.shape[1]

    def body(s_ref, land_ref, out_ref, send_sems, recv_sems, local_sems):
        x, y, c = _me()
        own = pltpu.make_async_copy(s_ref, out_ref.at[2 * x + y], local_sems.at[3])
        own.start()
        cps = []
        for j, (ox, oy) in enumerate(_other_chips(x, y)):
            qj = 2 * ox + oy
            mine = out_ref.at[qj, pl.ds(c * half, half), :]
            lc = pltpu.make_async_copy(land_ref.at[qj], mine, local_sems.at[j])
            lc.start()
            rc = pltpu.make_async_remote_copy(
                src_ref=land_ref.at[qj], dst_ref=mine, send_sem=send_sems.at[j], recv_sem=recv_sems.at[j],
                device_id=(x, y, 1 - c), device_id_type=MESH)
            rc.start()
            cps.append((lc, rc))
        for lc, rc in cps:
            rc.wait_recv()
        for lc, rc in cps:
            rc.wait_send()
            lc.wait()
        own.wait()

    vmem = pl.BlockSpec(memory_space=pltpu.VMEM)
    return pl.pallas_call(
        body,
        out_shape=jax.ShapeDtypeStruct((N_CHIPS,) + shard.shape, shard.dtype),
        in_specs=[vmem, vmem],
        out_specs=vmem,
        scratch_shapes=[pltpu.SemaphoreType.DMA((3,)), pltpu.SemaphoreType.DMA((3,)), pltpu.SemaphoreType.DMA((4,))],
        name=name,
        compiler_params=pltpu.CompilerParams(vmem_limit_bytes=VMEM_LIMIT),
    )(shard, land)


def _piece_shape(shape, kind):
    k, nn = shape
    if kind == "all":
        return (k, nn)
    return (k // 2, nn // N_CHIPS) if kind == "col" else (k // N_CHIPS // 2, nn)


def _piece_of(g_ref, kind, tq, tc):
    pr, pc = _piece_shape(g_ref.shape, kind)
    if kind == "all":
        return g_ref
    if kind == "col":
        return g_ref.at[pl.ds(tc * pr, pr), pl.ds(tq * pc, pc)]
    return g_ref.at[pl.ds((2 * tq + tc) * pr, pr), :]


def _scatter_copy(w, r, kind, g_ref, land_ref, send_sems, recv_sems):
    x, y, c = _me()
    tx, ty, tc = (x + ((r >> 2) & 1)) % 2, (y + ((r >> 1) & 1)) % 2, (c + (r & 1)) % 2
    return pltpu.make_async_remote_copy(
        src_ref=_piece_of(g_ref, kind, 2 * tx + ty, tc), dst_ref=land_ref.at[4 * x + 2 * y + c],
        send_sem=send_sems.at[N_DEV * w + r], recv_sem=recv_sems.at[N_DEV * w + r], device_id=(tx, ty, tc), device_id_type=MESH)


def _scatter_start(gs, kinds, name):
    n = len(gs)
    pieces = [_piece_shape(g.shape, kind) for g, kind in zip(gs, kinds)]
    lands = [lax.empty((N_DEV,) + p, g.dtype) for p, g in zip(pieces, gs)]

    def body(*refs):
        g_refs, land_refs, send_sems, recv_sems = refs[:n], refs[n:2 * n], refs[2 * n], refs[2 * n + 1]
        land_outs, stages = refs[3 * n + 2:4 * n + 2], refs[4 * n + 2:]
        x, y, c = _me()
        for w in range(n):
            for r in range(1, N_DEV):
                _scatter_copy(w, r, kinds[w], g_refs[w], land_refs[w], send_sems, recv_sems).start()
        for w in range(n):
            pltpu.sync_copy(_piece_of(g_refs[w], kinds[w], 2 * x + y, c), stages[w])
            pltpu.sync_copy(stages[w], land_outs[w].at[4 * x + 2 * y + c])

    arrays = list(gs) + lands
    res = pl.pallas_call(
        body,
        out_shape=(pltpu.SemaphoreType.DMA((N_DEV * n,)), pltpu.SemaphoreType.DMA((N_DEV * n,)),
                   *[pltpu.HBM(a.shape, a.dtype) for a in arrays]),
        in_specs=[_HBM] * (2 * n),
        out_specs=(_SEM, _SEM, *[_HBM] * (2 * n)),
        input_output_aliases={i: 2 + i for i in range(2 * n)},
        scratch_shapes=[pltpu.VMEM(p, g.dtype) for p, g in zip(pieces, gs)],
        name=name,
        compiler_params=pltpu.CompilerParams(has_side_effects=_EFFECT, vmem_limit_bytes=VMEM_LIMIT),
    )(*[pltpu.with_memory_space_constraint(a, pltpu.HBM) for a in arrays])
    return res[0], res[1], res[2:2 + n], res[2 + n:]


def _scatter_wait(send_sems, recv_sems, gs, lands, kinds, after, name):
    n = len(gs)

    def body(*refs):
        g_refs, land_refs, send_sems, recv_sems = refs[:n], refs[n:2 * n], refs[2 * n], refs[2 * n + 1]
        for w in range(n):
            for r in range(1, N_DEV):
                cp = _scatter_copy(w, r, kinds[w], g_refs[w], land_refs[w], send_sems, recv_sems)
                cp.wait_send()
                cp.wait_recv()

    arrays = list(gs) + list(lands)
    return pl.pallas_call(
        body,
        out_shape=tuple(pltpu.HBM(a.shape, a.dtype) for a in arrays),
        in_specs=(*[_HBM] * (2 * n), _SEM, _SEM, pl.BlockSpec(memory_space=pl.ANY)),
        out_specs=tuple([_HBM] * (2 * n)),
        input_output_aliases={i: i for i in range(2 * n)},
        name=name,
        compiler_params=pltpu.CompilerParams(has_side_effects=_EFFECT),
    )(*arrays, send_sems, recv_sems, after)[n:]


def _swap_halves(halves, name):
    n = len(halves)

    def body(*refs):
        in_refs, out_refs = refs[:n], refs[n:2 * n]
        send_sems, recv_sems, local_sems = refs[2 * n:]
        x, y, c = _me()
        cps = []
        for w in range(n):
            lc = pltpu.make_async_copy(in_refs[w], out_refs[w].at[c], local_sems.at[w])
            lc.start()
            rc = pltpu.make_async_remote_copy(
                src_ref=in_refs[w], dst_ref=out_refs[w].at[c], send_sem=send_sems.at[w], recv_sem=recv_sems.at[w],
                device_id=(x, y, 1 - c), device_id_type=MESH)
            rc.start()
            cps.append((lc, rc))
        for lc, rc in cps:
            rc.wait_recv()
        for lc, rc in cps:
            rc.wait_send()
            lc.wait()

    vmem = pl.BlockSpec(memory_space=pltpu.VMEM)
    return pl.pallas_call(
        body,
        out_shape=[jax.ShapeDtypeStruct((2,) + h.shape, h.dtype) for h in halves],
        in_specs=[vmem] * n,
        out_specs=[vmem] * n,
        scratch_shapes=[pltpu.SemaphoreType.DMA((n,)), pltpu.SemaphoreType.DMA((n,)), pltpu.SemaphoreType.DMA((n,))],
        name=name,
        compiler_params=pltpu.CompilerParams(vmem_limit_bytes=VMEM_LIMIT),
    )(*halves)


def _to_streams(a, dil):
    if dil == 1:
        return a
    s, c = a.shape
    return a.reshape(s // dil, dil, c).transpose(1, 0, 2).reshape(s, c)


def _from_streams(a, dil):
    if dil == 1:
        return a
    s, c = a.shape
    return a.reshape(dil, s // dil, c).transpose(1, 0, 2).reshape(s, c)


def _mm_tiles(s):
    return min(s, 2048)


def _local_step(x0, target, mvec, ln_g, ln_b, small, fetch, emit, start):
    s, d = x0.shape
    tm = _mm_tiles(s)
    row = lambda v: v.reshape(1, -1)
    shift = [row(mvec[i, :d]) for i in range(4)]
    scale = [row(mvec[i, d:2 * d]) for i in range(4)]
    gate = [row(1.0 + mvec[i, 2 * d:]) for i in range(4)]
    lg = [row(ln_g[i]) for i in range(4)]
    lb = [row(ln_b[i]) for i in range(4)]
    mm = functools.partial(_mm, tm=tm)
    mm_w = functools.partial(_mm, tm=1024, tk=min(s, 2048), mode="tn")

    xs, ys, big = [x0], [], {}
    h0 = _mod(x0, scale[0], shift[0], start, "mod0")
    big["a_w_in"] = fetch("a_w_in", h0)
    uvpre = mm(h0, big["a_w_in"], mode="nn", name="a_in", outs=[F32], tn=512, tk=1024,
               epi=lambda r, bias: [r + bias], extras=[("row", small["a_b_in"])])
    gated = _spatial_fwd(uvpre, small["a_vn_g"], small["a_vn_b"], small["wc"], small["bias_full"], "a_spatial")
    big["a_w_out"] = fetch("a_w_out", gated)
    ys.append(mm(gated, big["a_w_out"], mode="nn", name="a_out", outs=[F32], tn=1024, tk=1024))
    x1, h1 = _resid_ln(xs[0], ys[0], gate[0], lg[0], lb[0], (scale[1], shift[1]), "ln0")
    xs.append(x1)
    relu2 = lambda r: [jnp.square(jnp.maximum(r, 0.0))]
    big["up0"] = fetch("up0", h1)
    r0 = mm(h1, big["up0"], mode="nn", name="up0", outs=[MXU_DTYPE], tn=1024, tk=1024, epi=relu2)
    big["down0"] = fetch("down0", r0)
    ys.append(mm(r0, big["down0"], mode="nn", name="down0", outs=[F32], tm=min(s, 1024), tn=1024, tk=2048))
    x2, h2 = _resid_ln(xs[1], ys[1], gate[1], lg[1], lb[1], (scale[2], shift[2]), "ln1")
    xs.append(x2)
    hg, qkvs, o_g, l_g, l_streams = [], [], [], [], []
    big["b_w_qkv"] = fetch("b_w_qkv", h2)
    for g, (_, dil) in enumerate(B_PATTERNS):
        hp = _to_streams(h2, dil)
        qkv = mm(hp, big["b_w_qkv"], mode="nn", name=f"qkv{g}", outs=[MXU_DTYPE], tn=768, tk=1024, b_col0=g * 3 * d, n_out=3 * d)
        og, lgv = _attn_fwd(qkv, small["slopes"], dil, f"attn_fwd{g}")
        hg.append(hp)
        qkvs.append(qkv)
        o_g.append(_from_streams(og, dil))
        l_g.append(_from_streams(lgv, dil))
        l_streams.append(lgv)
    o_mix = _combine_fwd(o_g, l_g, "combine")
    big["b_w_out"] = fetch("b_w_out", o_mix)
    ys.append(mm(o_mix, big["b_w_out"], mode="nn", name="b_out", outs=[F32], tn=1024, tk=1024))
    x3, h3 = _resid_ln(xs[2], ys[2], gate[2], lg[2], lb[2], (scale[3], shift[3]), "ln2")
    xs.append(x3)
    big["up1"] = fetch("up1", h3)
    r1 = mm(h3, big["up1"], mode="nn", name="up1", outs=[MXU_DTYPE], tn=1024, tk=1024, epi=relu2)
    big["down1"] = fetch("down1", r1)
    ys.append(mm(r1, big["down1"], mode="nn", name="down1", outs=[F32], tm=min(s, 1024), tn=1024, tk=2048))

    gb, red_ln, red_mod = {}, [None] * 4, [None] * 4

    def mlp_bwd(i, h, r, dyy):
        gb[f"down{i}"] = mm_w(r, dyy, name=f"g_down{i}", outs=[MXU_DTYPE], tn=1024)
        da = mm(dyy, big[f"down{i}"], mode="nt", name=f"d_down{i}", outs=[MXU_DTYPE], tn=1024, tk=1024,
                after=emit(f"down{i}", gb[f"down{i}"]),
                epi=lambda acc, rv: [acc * (2.0 * jnp.sqrt(rv.astype(F32)))], extras=[("full", r)])
        gb[f"up{i}"] = mm_w(h, da, name=f"g_up{i}", outs=[MXU_DTYPE], tn=1024)
        return [mm(da, big[f"up{i}"], mode="nt", name=f"d_up{i}", outs=[F32], tn=1024, tk=1024, after=emit(f"up{i}", gb[f"up{i}"]))]

    def join(sub, dxr, dhs, after=None):
        res = _mod_ln_bwd(dxr, dhs, xs[sub], scale[sub], xs[sub - 1], ys[sub - 1], gate[sub - 1], lg[sub - 1],
                          f"mod_ln_bwd{sub}", after=after)
        red_mod[sub], red_ln[sub - 1] = res[2], res[3]
        return res[0], res[1]

    loss, dxr, dyy, red_ln[3] = _last_ln_loss_bwd(xs[3], ys[3], gate[3], lg[3], lb[3], target, "ln3_loss_bwd")
    dxr, dyy = join(3, dxr, mlp_bwd(1, h3, r1, dyy))
    gb["b_w_out"] = mm_w(o_mix, dyy, name="g_b_out", outs=[MXU_DTYPE], tn=1024, tk=1024)
    do = mm(dyy, big["b_w_out"], mode="nt", name="d_b_out", outs=[F32], tn=1024, tk=1024, after=emit("b_w_out", gb["b_w_out"]))
    parts = _combine_bwd(do, o_mix, l_g, "combine_bwd")
    dhs, gq = [], None
    for g, (_, dil) in enumerate(B_PATTERNS):
        do_g, dd_g = _to_streams(parts[g][0], dil), _to_streams(parts[g][1], dil)
        dqkv = _attn_bwd(qkvs[g], do_g, l_streams[g], dd_g, small["slopes"], dil, f"attn_bwd{g}")
        gq = mm_w(hg[g], dqkv, name=f"g_qkv{g}", outs=[MXU_DTYPE], tn=1024, out_col0=g * 3 * d, out_cols=len(B_PATTERNS) * 3 * d, into=gq)
        dh = mm(dqkv, big["b_w_qkv"], mode="nt", name=f"d_qkv{g}", outs=[F32], tn=1024, tk=768, b_col0=g * 3 * d)
        dhs.append(_from_streams(dh, dil))
    gb["b_w_qkv"] = gq
    dxr, dyy = join(2, dxr, dhs, after=emit("b_w_qkv", gb["b_w_qkv"]))
    dxr, dyy = join(1, dxr, mlp_bwd(0, h1, r0, dyy))
    gb["a_w_out"] = mm_w(gated, dyy, name="g_a_out", outs=[MXU_DTYPE], tn=1024)
    dgated = mm(dyy, big["a_w_out"], mode="nt", name="d_a_out", outs=[F32], tn=1024, tk=1024, after=emit("a_w_out", gb["a_w_out"]))
    duv, dws, dbias, dbin, dvg, dvb = _spatial_bwd(uvpre, dgated, small["a_vn_g"], small["a_vn_b"], small["wc"],
                                                   small["wct"], small["bias_full"], "a_spatial_bwd")
    tril = jnp.tril(jnp.ones((CHUNK, CHUNK), bool))
    dws = jnp.where(tril, dws, 0.0).reshape(-1, LANES)
    gb["a_w_in"] = mm_w(h0, duv, name="g_a_in", outs=[MXU_DTYPE], tn=1024, after=emit("a_w_s", dws.astype(MXU_DTYPE)))
    dh = mm(duv, big["a_w_in"], mode="nt", name="d_a_in", outs=[F32], tn=1024, tk=512, after=emit("a_w_in", gb["a_w_in"]))
    dx, red_mod[0] = _mod_bwd(dxr, [dh], xs[0], scale[0], "mod_bwd0")
    dm = [jnp.concatenate([red_mod[i][0], red_mod[i][1], red_ln[i][2]]) for i in range(4)]
    dlg, dlb = [red_ln[i][0] for i in range(4)], [red_ln[i][1] for i in range(4)]

    gsmall = {
        "a_b_in": dbin.reshape(-1), "a_vn_g": dvg.reshape(-1), "a_vn_b": dvb.reshape(-1),
        "a_w_s": dws.reshape(-1),
        "a_b_s": dbias.reshape(CHUNK, A_GROUPS, d // A_GROUPS).sum(-1).T.reshape(-1),
    }
    return loss, dx, gb, jnp.stack(dm), jnp.stack(dlg), jnp.stack(dlb), gsmall


BIG = ("a_w_in", "a_w_out", "up0", "down0", "b_w_qkv", "b_w_out", "up1", "down1")
BIG_KIND = {"a_w_in": "col", "a_w_out": "row", "b_w_qkv": "col", "b_w_out": "row",
            "up0": "col", "up1": "col", "down0": "row", "down1": "row", "a_w_s": "all"}
HALVED = ("a_w_in", "down0", "b_w_qkv")
SCATTER_GROUPS = (("down1", "up1"), ("b_w_out", "b_w_qkv"), ("down0", "up0"), ("a_w_out", "a_w_in"), ("a_w_s",))
SMALL = ("a_b_in", "a_vn_g", "a_vn_b", "a_b_s")


def kernel(x, c, ada_w, ada_b, ln_g, ln_b, a_w_in, a_b_in, a_vn_g, a_vn_b, a_w_s, a_b_s, a_w_out, b_w_qkv, b_w_out, mlp_w_up, mlp_w_down, loss_target, m_ada_w, m_ada_b, m_ln_g, m_ln_b, m_a_w_in, m_a_b_in, m_a_vn_g, m_a_vn_b, m_a_w_s, m_a_b_s, m_a_w_out, m_b_w_qkv, m_b_w_out, m_mlp_w_up, m_mlp_w_down, v_ada_w, v_ada_b, v_ln_g, v_ln_b, v_a_w_in, v_a_b_in, v_a_vn_g, v_a_vn_b, v_a_w_s, v_a_b_s, v_a_w_out, v_b_w_qkv, v_b_w_out, v_mlp_w_up, v_mlp_w_down):
    s, d = x.shape[1], x.shape[2]
    xi, yi, ci = _me()
    q = 2 * xi + yi
    dev = 2 * q + ci
    nsub = 2 * DEPTH
    cs = ada_w.shape[-1]
    ls = ln_g.shape[-1]

    shards = {
        "a_w_in": a_w_in[0], "a_w_out": a_w_out[0], "b_w_qkv": b_w_qkv[0], "b_w_out": b_w_out[0],
        "up0": mlp_w_up[0], "up1": mlp_w_up[1], "down0": mlp_w_down[0], "down1": mlp_w_down[1],
    }
    cast = [shards[k].astype(MXU_DTYPE) for k in BIG]

    pack = jnp.concatenate([c.reshape(-1), ln_g.reshape(-1), ln_b.reshape(-1)]).reshape(-1, LANES)
    got = _all_gather_small(pack, "gather_small", after=cast).reshape(N_DEV, -1)
    c_all = got[:, :d]
    per_chip = got[0::2]
    ln_g_full = per_chip[:, d:d + nsub * ls].reshape(N_CHIPS, nsub, ls).transpose(1, 0, 2).reshape(nsub, d)
    ln_b_full = per_chip[:, d + nsub * ls:].reshape(N_CHIPS, nsub, ls).transpose(1, 0, 2).reshape(nsub, d)
    m_part = _ada_fwd(c_all, ada_w.reshape(nsub, d, cs), ada_b.reshape(nsub, 1, cs), "ada_fwd")
    m_all = _all_gather_small(m_part.reshape(-1, LANES), "gather_mod").reshape(N_DEV, nsub, N_DEV, cs)
    m_mine = lax.dynamic_index_in_dim(m_all[0::2], dev, axis=2, keepdims=False)
    mvec = m_mine.transpose(1, 0, 2).reshape(nsub, 3 * d)

    halved = {BIG.index(k) for k in HALVED}
    send_sems, recv_sems, shard_thru, lands, token = _gather_start(cast, halved, mvec, "gather_start")

    def fetch(k, after):
        w = BIG.index(k)
        shard, gw = _gather_wait(w, shard_thru[w], lands[w], send_sems, recv_sems, after, f"gather_wait_{k}", w in halved)
        if w in halved:
            gw = _assemble_halves(shard, gw, f"assemble_{k}")
        return gw if BIG_KIND[k] == "col" else gw.reshape(1, -1, gw.shape[-1])

    scattering, pending = {}, {}

    def emit(k, g):
        pending[k] = g
        group = next(gr for gr in SCATTER_GROUPS if k in gr)
        if k != group[-1]:
            return None
        scattering[group] = _scatter_start([pending[m] for m in group], [BIG_KIND[m] for m in group], f"scatter_start_{k}")
        return scattering[group][2][0]

    tril = jnp.tril(jnp.ones((CHUNK, CHUNK), bool))
    wc = jnp.where(tril, a_w_s[0], 0.0).astype(MXU_DTYPE)
    heads = jnp.arange(1, B_HEADS + 1, dtype=F32)
    small = {
        "a_b_in": a_b_in, "a_vn_g": a_vn_g, "a_vn_b": a_vn_b,
        "wc": wc, "wct": wc.transpose(0, 2, 1),
        "bias_full": jnp.repeat(a_b_s[0].T, d // A_GROUPS, axis=1),
        "slopes": jnp.exp2(-8.0 * heads / B_HEADS),
    }

    loss_part, grad_x, gb, dm, dlg, dlb, gsmall = _local_step(x[0], loss_target[0], mvec, ln_g_full, ln_b_full, small, fetch, emit, token)
    loss = lax.psum(loss_part, ("x", "y", "c"))

    weights = dict(ada_w=ada_w, ada_b=ada_b, ln_g=ln_g, ln_b=ln_b, a_w_in=a_w_in, a_b_in=a_b_in, a_vn_g=a_vn_g, a_vn_b=a_vn_b,
                   a_w_s=a_w_s, a_b_s=a_b_s, a_w_out=a_w_out, b_w_qkv=b_w_qkv, b_w_out=b_w_out, mlp_w_up=mlp_w_up, mlp_w_down=mlp_w_down)
    ms = dict(ada_w=m_ada_w, ada_b=m_ada_b, ln_g=m_ln_g, ln_b=m_ln_b, a_w_in=m_a_w_in, a_b_in=m_a_b_in, a_vn_g=m_a_vn_g, a_vn_b=m_a_vn_b,
              a_w_s=m_a_w_s, a_b_s=m_a_b_s, a_w_out=m_a_w_out, b_w_qkv=m_b_w_qkv, b_w_out=m_b_w_out, mlp_w_up=m_mlp_w_up, mlp_w_down=m_mlp_w_down)
    vs = dict(ada_w=v_ada_w, ada_b=v_ada_b, ln_g=v_ln_g, ln_b=v_ln_b, a_w_in=v_a_w_in, a_b_in=v_a_b_in, a_vn_g=v_a_vn_g, a_vn_b=v_a_vn_b,
              a_w_s=v_a_w_s, a_b_s=v_a_b_s, a_w_out=v_a_w_out, b_w_qkv=v_b_w_qkv, b_w_out=v_b_w_out, mlp_w_up=v_mlp_w_up, mlp_w_down=v_mlp_w_down)
    grads, updates = {}, {}

    def update(k):
        updates[k] = _adamw(weights[k], grads[k], ms[k], vs[k], f"adamw_{k}")
        return updates[k][0]

    gfull = {}

    def big_group(group, after):
        bufs = []
        for pair in (group[:2], group[2:]):
            bufs += _scatter_wait(*scattering[pair], [BIG_KIND[m] for m in pair], after, f"scatter_wait_{pair[-1]}")
        halves = [_sum_slots(b, f"sum_{k}") for k, b in zip(group, bufs)]
        fulls = _swap_halves(halves, f"swap_halves_{group[0]}")
        gfull.update({k: f.reshape(-1, f.shape[-1]) for k, f in zip(group, fulls)})

    big_group(SCATTER_GROUPS[0] + SCATTER_GROUPS[1], grad_x)
    grads["b_w_qkv"], grads["b_w_out"] = gfull["b_w_qkv"][None], gfull["b_w_out"][None]
    update("b_w_out")
    done = update("b_w_qkv")

    pack_b = jnp.concatenate([dm.reshape(-1), dlg.reshape(-1), dlb.reshape(-1)] + [gsmall[k] for k in SMALL])
    n_small = pack_b.shape[0]
    pack_b = jnp.pad(pack_b, (0, -n_small % (256 * LANES)))
    got_b = _all_gather_small(pack_b.reshape(-1, LANES), "gather_small_grads", after=[done]).reshape(N_DEV, -1, LANES)
    tot = _sum_slots(got_b, "sum_small").reshape(-1)
    o = 0
    dm_tot = tot[o:o + nsub * 3 * d].reshape(nsub, 3 * d); o += nsub * 3 * d
    dlg_tot = tot[o:o + nsub * d].reshape(nsub, d); o += nsub * d
    dlb_tot = tot[o:o + nsub * d].reshape(nsub, d); o += nsub * d
    g_small = {}
    for k, ref in zip(SMALL, (a_b_in, a_vn_g, a_vn_b, a_b_s)):
        g_small[k] = tot[o:o + ref.size].reshape(ref.shape); o += ref.size
    assert o == n_small
    aws = _scatter_wait(*scattering[("a_w_s",)], ["all"], tot, "scatter_wait_a_w_s")[0]
    g_small["a_w_s"] = _sum_slots(aws, "sum_a_w_s").reshape(a_w_s.shape)
    dm_all = got_b.reshape(N_DEV, -1)[:, :nsub * 3 * d].reshape(N_DEV, nsub, 3 * d)
    dm_cols = lax.dynamic_slice_in_dim(dm_all, q * cs, cs, axis=2).transpose(1, 0, 2)
    grads.update({
        "ada_w": _ada_bwd(c_all.T, dm_cols, "ada_bwd").reshape(ada_w.shape),
        "ada_b": lax.dynamic_slice_in_dim(dm_tot, q * cs, cs, axis=1).reshape(ada_b.shape),
        "ln_g": lax.dynamic_slice_in_dim(dlg_tot, q * ls, ls, axis=1).reshape(ln_g.shape),
        "ln_b": lax.dynamic_slice_in_dim(dlb_tot, q * ls, ls, axis=1).reshape(ln_b.shape),
        **g_small,
    })
    for k in ("ada_b", "ln_g", "ln_b", "a_w_s") + SMALL:
        update(k)
    done = update("ada_w")

    big_group(SCATTER_GROUPS[2] + SCATTER_GROUPS[3], done)
    grads.update({
        "a_w_in": gfull["a_w_in"][None], "a_w_out": gfull["a_w_out"][None],
        "mlp_w_up": jnp.stack([gfull["up0"], gfull["up1"]]), "mlp_w_down": jnp.stack([gfull["down0"], gfull["down1"]]),
    })
    for k in ("a_w_in", "a_w_out", "mlp_w_up", "mlp_w_down"):
        update(k)
    names = list(weights)
    return (loss, grad_x[None], *[grads[k] for k in names], *[updates[k][0] for k in names],
            *[updates[k][1] for k in names], *[updates[k][2] for k in names])
```

```python
import functools
import math

import jax
import jax.numpy as jnp
from jax import lax
from jax.experimental import pallas as pl
from jax.experimental.pallas import tpu as pltpu

F32 = jnp.float32
MXU_DTYPE = jnp.bfloat16

DEPTH = 2
CHUNK = 128
A_GROUPS = 16
B_HEADS = 16
HEAD_DIM = 64
B_PATTERNS = ((128, 1), (512, 4), (2048, 16))
SPAN = 128
ALPHA = (2 * DEPTH) ** 0.25
LN_EPS = 1e-5
NEG = -1e30
ATT_SCALE = HEAD_DIM ** -0.5
ADAM_LR, ADAM_B1, ADAM_B2, ADAM_EPS, ADAM_WD, ADAM_STEP = 0.001, 0.9, 0.999, 1e-08, 0.01, 10

N_CHIPS = 4
N_DEV = 8
LANES = 128
SUBLANES = 8
VMEM_LIMIT = 52 * 1024 * 1024
ROW_TILE = 512
MM_ROW_CHUNK = 256
MESH = pl.DeviceIdType.MESH


def _cparams(sem):
    return pltpu.CompilerParams(dimension_semantics=sem, vmem_limit_bytes=VMEM_LIMIT)


def _fold8(v):
    r, c = v.shape
    return jnp.sum(v.reshape(r // SUBLANES, SUBLANES, c), axis=0)


def _gelu(x):
    c = math.sqrt(2.0 / math.pi)
    return 0.5 * x * (1.0 + jnp.tanh(c * (x + 0.044715 * (x * x * x))))


def _gelu_grad(x):
    c = math.sqrt(2.0 / math.pi)
    t = jnp.tanh(c * (x + 0.044715 * (x * x * x)))
    return 0.5 * (1.0 + t) + 0.5 * x * (1.0 - t * t) * c * (1.0 + 3.0 * 0.044715 * x * x)


def _dot(a, b, dims):
    return lax.dot_general(a.astype(MXU_DTYPE), b.astype(MXU_DTYPE), (dims, ((), ())), preferred_element_type=F32)


def _dot_nn(a, b):
    return _dot(a, b, ((1,), (0,)))


def _dot_nt(a, b):
    return _dot(a, b, ((1,), (1,)))


def _dot_tn(a, b):
    return _dot(a, b, ((0,), (0,)))


def _mm(a, b, *, mode, name, outs, tm, tn, tk, epi=None, extras=(), b_col0=0, n_out=None, after=None,
        out_col0=0, out_cols=None, into=None):
    if mode == "nn":
        m, kdim = a.shape
        p, kb, ns = b.shape
        assert kb == kdim and ns % tn == 0 and b_col0 % tn == 0
        n = n_out if n_out is not None else p * ns
        npt, j0 = ns // tn, b_col0 // tn
        a_spec = pl.BlockSpec((tm, tk), lambda i, j, k: (i, k))
        b_spec = pl.BlockSpec((None, tk, tn), lambda i, j, k: ((j + j0) // npt, k, (j + j0) % npt))
        dot = _dot_nn
    elif mode == "nt":
        m, kdim = a.shape
        p, n, ns = b.shape
        assert ns % tk == 0 and b_col0 % tk == 0
        npt, j0 = ns // tk, b_col0 // tk
        a_spec = pl.BlockSpec((tm, tk), lambda i, j, k: (i, k))
        b_spec = pl.BlockSpec((None, tn, tk), lambda i, j, k: ((k + j0) // npt, j, (k + j0) % npt))
        dot = _dot_nt
    else:
        kdim, m = a.shape
        kb, n = b.shape
        assert kb == kdim
        a_spec = pl.BlockSpec((tk, tm), lambda i, j, k: (k, i))
        b_spec = pl.BlockSpec((tk, tn), lambda i, j, k: (k, j))
        dot = _dot_tn
    assert m % tm == 0 and n % tn == 0 and kdim % tk == 0, (name, m, n, kdim, tm, tn, tk)
    nk = kdim // tk
    ex_specs, ex_arrays = [], []
    for kind, arr in extras:
        if kind == "row":
            ex_specs.append(pl.BlockSpec((1, tn), lambda i, j, k: (0, j)))
        else:
            ex_specs.append(pl.BlockSpec((tm, tn), lambda i, j, k: (i, j)))
        ex_arrays.append(arr)
    n_ex, n_o = len(ex_arrays), len(outs)
    deps = [d for d in (after, into) if d is not None]
    n_dep = len(deps)
    j_out = out_col0 // tn
    assert out_col0 % tn == 0 and (into is None or len(outs) == 1)

    def body(a_ref, b_ref, *rest):
        ex_refs, o_refs = rest[:n_ex], rest[n_ex + n_dep:n_ex + n_dep + n_o]
        k = pl.program_id(2)

        chunks = [slice(r0, r0 + min(tm, MM_ROW_CHUNK)) for r0 in range(0, tm, min(tm, MM_ROW_CHUNK))]

        def part(rows):
            return dot(a_ref[:, rows] if mode == "tn" else a_ref[rows, :], b_ref[...])

        def finish(r, rows):
            exs = [e[...] if kind == "row" else e[rows, :] for (kind, _), e in zip(extras, ex_refs)]
            vals = epi(r, *exs) if epi is not None else [r]
            for o, v in zip(o_refs, vals):
                o[rows, :] = v.astype(o.dtype)

        if nk == 1:
            for rows in chunks:
                finish(part(rows), rows)
            return
        acc = rest[n_ex + n_dep + n_o]

        @pl.when(k == 0)
        def _():
            for rows in chunks:
                acc[rows, :] = part(rows)

        @pl.when((k > 0) & (k < nk - 1))
        def _():
            for rows in chunks:
                acc[rows, :] += part(rows)

        @pl.when(k == nk - 1)
        def _():
            for rows in chunks:
                finish(acc[rows, :] + part(rows), rows)

    res = pl.pallas_call(
        body,
        grid=(m // tm, n // tn, nk),
        in_specs=[a_spec, b_spec] + ex_specs + [pl.BlockSpec(memory_space=pl.ANY)] * n_dep,
        out_specs=[pl.BlockSpec((tm, tn), lambda i, j, k: (i, j + j_out)) for _ in outs],
        out_shape=[jax.ShapeDtypeStruct((m, out_cols or n), dt) for dt in outs],
        input_output_aliases={} if into is None else {2 + n_ex + n_dep - 1: 0},
        scratch_shapes=[pltpu.VMEM((tm, tn), F32)] if nk > 1 else [],
        name=name,
        compiler_params=_cparams(("parallel", "parallel", "arbitrary")),
    )(a, b, *ex_arrays, *deps)
    return res if len(outs) > 1 else res[0]


def _rows(body, n_rows, tr, ins, outs, name, scratch=()):
    def spec(kind, shape):
        if kind == "blk":
            return pl.BlockSpec((tr,) + tuple(shape[1:]), lambda i: (i,) + (0,) * (len(shape) - 1))
        if kind == "dep":
            return pl.BlockSpec(memory_space=pl.ANY)
        return pl.BlockSpec(tuple(shape), lambda i: (0,) * len(shape))

    return pl.pallas_call(
        body,
        grid=(n_rows // tr,),
        in_specs=[spec(k, a.shape) for k, a in ins],
        out_specs=[spec(k, s) for k, s, _ in outs],
        out_shape=[jax.ShapeDtypeStruct(tuple(s), d) for _, s, d in outs],
        scratch_shapes=list(scratch),
        name=name,
        compiler_params=_cparams(("arbitrary",)),
    )(*[a for _, a in ins])


def _ln_stats(z):
    mu = jnp.mean(z, axis=-1, keepdims=True)
    zc = z - mu
    var = jnp.mean(zc * zc, axis=-1, keepdims=True)
    rstd = lax.rsqrt(var + LN_EPS)
    return zc * rstd, rstd


def _mod(x, scale, shift, after, name):
    s, d = x.shape

    def body(x_ref, sc_ref, sh_ref, dep_ref, h_ref):
        h_ref[...] = (x_ref[...] * (1.0 + sc_ref[...]) + sh_ref[...]).astype(h_ref.dtype)

    return _rows(body, s, ROW_TILE, [("blk", x), ("all", scale), ("all", shift), ("dep", after)], [("blk", (s, d), MXU_DTYPE)], name)[0]


def _resid_ln(x, y, gate, g, b, nxt, name):
    s, d = x.shape

    def body(x_ref, y_ref, gate_ref, g_ref, b_ref, sc_ref, sh_ref, xn_ref, h_ref):
        z = ALPHA * x_ref[...] + gate_ref[...] * y_ref[...]
        xhat, _ = _ln_stats(z)
        xn = xhat * g_ref[...] + b_ref[...]
        xn_ref[...] = xn
        h_ref[...] = (xn * (1.0 + sc_ref[...]) + sh_ref[...]).astype(h_ref.dtype)

    return _rows(body, s, ROW_TILE,
                 [("blk", x), ("blk", y), ("all", gate), ("all", g), ("all", b), ("all", nxt[0]), ("all", nxt[1])],
                 [("blk", (s, d), F32), ("blk", (s, d), MXU_DTYPE)], name)


def _mod_bwd(dxr, dhs, x, scale, name, after=None):
    s, d = x.shape
    n_dh = len(dhs)
    n_dep = 0 if after is None else 1

    def body(dxr_ref, *rest):
        dh_refs = rest[:n_dh]
        x_ref, sc_ref, dx_ref, red_ref, a_sh, a_sc = rest[n_dh:n_dh + 2] + rest[n_dh + 2 + n_dep:]
        i = pl.program_id(0)

        @pl.when(i == 0)
        def _():
            a_sh[...] = jnp.zeros_like(a_sh)
            a_sc[...] = jnp.zeros_like(a_sc)

        dh = dh_refs[0][...]
        for r in dh_refs[1:]:
            dh = dh + r[...]
        dx_ref[...] = dxr_ref[...] + dh * (1.0 + sc_ref[...])
        a_sh[...] += _fold8(dh)
        a_sc[...] += _fold8(dh * x_ref[...])

        @pl.when(i == pl.num_programs(0) - 1)
        def _():
            red_ref[...] = jnp.zeros_like(red_ref)
            red_ref[0:1, :] = jnp.sum(a_sh[...], axis=0, keepdims=True)
            red_ref[1:2, :] = jnp.sum(a_sc[...], axis=0, keepdims=True)

    return _rows(body, s, ROW_TILE, [("blk", dxr)] + [("blk", h) for h in dhs] + [("blk", x), ("all", scale)] + [("dep", after)] * n_dep,
                 [("blk", (s, d), F32), ("all", (SUBLANES, d), F32)], name,
                 scratch=[pltpu.VMEM((SUBLANES, d), F32)] * 2)


def _last_ln_loss_bwd(x, y, gate, g, b, target, name):
    s, d = x.shape

    def body(x_ref, y_ref, gate_ref, g_ref, b_ref, t_ref, l_ref, dxr_ref, dyy_ref, red_ref, a_l, a_g, a_b, a_gate):
        i = pl.program_id(0)

        @pl.when(i == 0)
        def _():
            for a in (a_l, a_g, a_b, a_gate):
                a[...] = jnp.zeros_like(a)

        yv = y_ref[...]
        z = ALPHA * x_ref[...] + gate_ref[...] * yv
        xhat, rstd = _ln_stats(z)
        e = xhat * g_ref[...] + b_ref[...] - t_ref[...]
        a_l[...] += _fold8(e * e)
        dxo_v = e * (1.0 / d)
        dxh = dxo_v * g_ref[...]
        dz = rstd * (dxh - jnp.mean(dxh, axis=-1, keepdims=True) - xhat * jnp.mean(dxh * xhat, axis=-1, keepdims=True))
        dxr_ref[...] = ALPHA * dz
        dyy_ref[...] = (gate_ref[...] * dz).astype(dyy_ref.dtype)
        a_g[...] += _fold8(dxo_v * xhat)
        a_b[...] += _fold8(dxo_v)
        a_gate[...] += _fold8(dz * yv)

        @pl.when(i == pl.num_programs(0) - 1)
        def _():
            l_ref[...] = jnp.full(l_ref.shape, 0.5 / d, F32) * jnp.sum(a_l[...])
            red_ref[...] = jnp.zeros_like(red_ref)
            red_ref[0:1, :] = jnp.sum(a_g[...], axis=0, keepdims=True)
            red_ref[1:2, :] = jnp.sum(a_b[...], axis=0, keepdims=True)
            red_ref[2:3, :] = jnp.sum(a_gate[...], axis=0, keepdims=True)

    l, dxr, dyy, red = _rows(
        body, s, ROW_TILE, [("blk", x), ("blk", y), ("all", gate), ("all", g), ("all", b), ("blk", target)],
        [("all", (SUBLANES, LANES), F32), ("blk", (s, d), F32), ("blk", (s, d), MXU_DTYPE), ("all", (SUBLANES, d), F32)], name,
        scratch=[pltpu.VMEM((SUBLANES, d), F32)] * 4)
    return l[0, 0], dxr, dyy, red


def _mod_ln_bwd(dxr, dhs, x, scale, x_in, y, gate, g, name, after=None):
    s, d = x.shape
    n_dh = len(dhs)
    n_dep = 0 if after is None else 1

    def body(dxr_ref, *rest):
        dh_refs = rest[:n_dh]
        x_ref, sc_ref, xin_ref, y_ref, gate_ref, g_ref = rest[n_dh:n_dh + 6]
        dxr_out, dyy_ref, red_mod, red_ln, a_sh, a_sc, a_g, a_b, a_gate = rest[n_dh + 6 + n_dep:]
        i = pl.program_id(0)

        @pl.when(i == 0)
        def _():
            for a in (a_sh, a_sc, a_g, a_b, a_gate):
                a[...] = jnp.zeros_like(a)

        dh = dh_refs[0][...]
        for r in dh_refs[1:]:
            dh = dh + r[...]
        xv = x_ref[...]
        dxo_v = dxr_ref[...] + dh * (1.0 + sc_ref[...])
        a_sh[...] += _fold8(dh)
        a_sc[...] += _fold8(dh * xv)
        yv = y_ref[...]
        z = ALPHA * xin_ref[...] + gate_ref[...] * yv
        xhat, rstd = _ln_stats(z)
        dxh = dxo_v * g_ref[...]
        dz = rstd * (dxh - jnp.mean(dxh, axis=-1, keepdims=True) - xhat * jnp.mean(dxh * xhat, axis=-1, keepdims=True))
        dxr_out[...] = ALPHA * dz
        dyy_ref[...] = (gate_ref[...] * dz).astype(dyy_ref.dtype)
        a_g[...] += _fold8(dxo_v * xhat)
        a_b[...] += _fold8(dxo_v)
        a_gate[...] += _fold8(dz * yv)

        @pl.when(i == pl.num_programs(0) - 1)
        def _():
            red_mod[...] = jnp.zeros_like(red_mod)
            red_mod[0:1, :] = jnp.sum(a_sh[...], axis=0, keepdims=True)
            red_mod[1:2, :] = jnp.sum(a_sc[...], axis=0, keepdims=True)
            red_ln[...] = jnp.zeros_like(red_ln)
            red_ln[0:1, :] = jnp.sum(a_g[...], axis=0, keepdims=True)
            red_ln[1:2, :] = jnp.sum(a_b[...], axis=0, keepdims=True)
            red_ln[2:3, :] = jnp.sum(a_gate[...], axis=0, keepdims=True)

    ins = ([("blk", dxr)] + [("blk", h) for h in dhs]
           + [("blk", x), ("all", scale), ("blk", x_in), ("blk", y), ("all", gate), ("all", g)] + [("dep", after)] * n_dep)
    return _rows(body, s, ROW_TILE, ins,
                 [("blk", (s, d), F32), ("blk", (s, d), MXU_DTYPE), ("all", (SUBLANES, d), F32), ("all", (SUBLANES, d), F32)], name,
                 scratch=[pltpu.VMEM((SUBLANES, d), F32)] * 5)


def _left_half(shape):
    return lax.broadcasted_iota(jnp.int32, shape, 1) < (LANES // 2)


CHUNKS_PER_STEP = 2


def _chunks_of_step():
    return [slice(i * CHUNK, (i + 1) * CHUNK) for i in range(CHUNKS_PER_STEP)]


def _spatial_z(vn, wc_ref, bias_ref, j):
    vb = vn[:, j * LANES:(j + 1) * LANES]
    z0 = _dot_nn(wc_ref[2 * j], vb)
    z1 = _dot_nn(wc_ref[2 * j + 1], vb)
    return jnp.where(_left_half(z0.shape), z0, z1) + bias_ref[:, j * LANES:(j + 1) * LANES]


def _spatial_fwd(uvpre, vn_g, vn_b, wc, bias_full, name):
    s, d2 = uvpre.shape
    d = d2 // 2

    def body(uv_ref, g_ref, b_ref, wc_ref, bias_ref, out_ref):
        for rows in _chunks_of_step():
            u = _gelu(uv_ref[rows, :d])
            v = _gelu(uv_ref[rows, d:])
            vh, _ = _ln_stats(v)
            vn = vh * g_ref[...] + b_ref[...]
            for j in range(d // LANES):
                z = _spatial_z(vn, wc_ref, bias_ref, j)
                out_ref[rows, j * LANES:(j + 1) * LANES] = (u[:, j * LANES:(j + 1) * LANES] * z).astype(out_ref.dtype)

    return _rows(body, s, CHUNKS_PER_STEP * CHUNK, [("blk", uvpre), ("all", vn_g), ("all", vn_b), ("all", wc), ("all", bias_full)],
                 [("blk", (s, d), MXU_DTYPE)], name)[0]


def _spatial_bwd(uvpre, dgated, vn_g, vn_b, wc, wct, bias_full, name):
    s, d2 = uvpre.shape
    d = d2 // 2

    def body(uv_ref, dg_ref, g_ref, b_ref, wc_ref, wct_ref, bias_ref,
             duv_ref, dws_ref, dbias_ref, dbin_ref, dvg_ref, dvb_ref, dvn_buf, a_bin, a_vg, a_vb):
        i = pl.program_id(0)

        @pl.when(i == 0)
        def _():
            dws_ref[...] = jnp.zeros_like(dws_ref)
            dbias_ref[...] = jnp.zeros_like(dbias_ref)
            a_bin[...] = jnp.zeros_like(a_bin)
            a_vg[...] = jnp.zeros_like(a_vg)
            a_vb[...] = jnp.zeros_like(a_vb)

        for rows in _chunks_of_step():
            up = uv_ref[rows, :d]
            vp = uv_ref[rows, d:]
            u = _gelu(up)
            v = _gelu(vp)
            vh, rstd = _ln_stats(v)
            vn = vh * g_ref[...] + b_ref[...]
            dg = dg_ref[rows, :]
            dzz = dg * u
            dbias_ref[...] += dzz
            for j in range(d // LANES):
                cols = slice(j * LANES, (j + 1) * LANES)
                z = _spatial_z(vn, wc_ref, bias_ref, j)
                dup = dg[:, cols] * z * _gelu_grad(up[:, cols])
                duv_ref[rows, cols] = dup.astype(duv_ref.dtype)
                a_bin[:, cols] += _fold8(dup)
                dzb = dzz[:, cols]
                left = _left_half(dzb.shape)
                dvn_buf[:, cols] = jnp.where(left, _dot_nn(wct_ref[2 * j], dzb), _dot_nn(wct_ref[2 * j + 1], dzb))
                vb = vn[:, cols]
                dws_ref[2 * j] += _dot_nt(jnp.where(left, dzb, 0.0), vb)
                dws_ref[2 * j + 1] += _dot_nt(jnp.where(left, 0.0, dzb), vb)
            dvn = dvn_buf[...]
            a_vg[...] += _fold8(dvn * vh)
            a_vb[...] += _fold8(dvn)
            dvh = dvn * g_ref[...]
            dv = rstd * (dvh - jnp.mean(dvh, axis=-1, keepdims=True) - vh * jnp.mean(dvh * vh, axis=-1, keepdims=True))
            dvp = dv * _gelu_grad(vp)
            duv_ref[rows, d:] = dvp.astype(duv_ref.dtype)
            a_bin[:, d:] += _fold8(dvp)

        @pl.when(i == pl.num_programs(0) - 1)
        def _():
            dbin_ref[...] = jnp.sum(a_bin[...], axis=0, keepdims=True)
            dvg_ref[...] = jnp.sum(a_vg[...], axis=0, keepdims=True)
            dvb_ref[...] = jnp.sum(a_vb[...], axis=0, keepdims=True)

    return _rows(body, s, CHUNKS_PER_STEP * CHUNK,
                 [("blk", uvpre), ("blk", dgated), ("all", vn_g), ("all", vn_b), ("all", wc), ("all", wct), ("all", bias_full)],
                 [("blk", (s, d2), MXU_DTYPE), ("all", (A_GROUPS, CHUNK, CHUNK), F32), ("all", (CHUNK, d), F32),
                  ("all", (1, d2), F32), ("all", (1, d), F32), ("all", (1, d), F32)], name,
                 scratch=[pltpu.VMEM((CHUNK, d), F32), pltpu.VMEM((SUBLANES, d2), F32),
                          pltpu.VMEM((SUBLANES, d), F32), pltpu.VMEM((SUBLANES, d), F32)])


def _head_mask(v, h):
    lane = lax.broadcasted_iota(jnp.int32, v.shape, 1)
    return jnp.where((lane >= h * HEAD_DIM) & (lane < (h + 1) * HEAD_DIM), v, jnp.zeros_like(v))


def _att_bias(slopes, dil):
    qi = lax.broadcasted_iota(jnp.int32, (SPAN, SPAN), 0)
    ki = lax.broadcasted_iota(jnp.int32, (SPAN, SPAN), 1)
    sl = slopes[:, None, None]
    cur = jnp.where(ki <= qi, -sl * (float(dil) * (qi - ki).astype(F32)), NEG)
    prev = jnp.where(ki >= qi, -sl * (float(dil) * (SPAN + qi - ki).astype(F32)), NEG)
    absent = jnp.full_like(prev, NEG)
    pairs = slopes.shape[0] // 2

    def fwd(pv):
        return jnp.concatenate([cur, pv], axis=2).reshape(pairs, 2 * SPAN, 2 * SPAN)

    def bwd(pv):
        return jnp.concatenate([cur.reshape(pairs, 2 * SPAN, SPAN), pv.reshape(pairs, 2 * SPAN, SPAN)], axis=1)

    return jnp.stack([fwd(absent), fwd(prev)]), jnp.stack([bwd(absent), bwd(prev)])


ATT_PAIR = 2


def _att_specs(s, d, dil, kinds):
    nb = s // (dil * SPAN)
    assert nb % ATT_PAIR == 0

    def spec(part, which):
        if which == "pair":
            return pl.BlockSpec((ATT_PAIR * SPAN, d), lambda b: (b, part))
        if which == "prev":
            return pl.BlockSpec((SPAN, d), lambda b: (jnp.where((ATT_PAIR * b) % nb == 0, ATT_PAIR * b, ATT_PAIR * b - 1), part))
        return pl.BlockSpec((SPAN, d), lambda b: (jnp.where((ATT_PAIR * b + 1) % nb == nb - 1, ATT_PAIR * b + 1, ATT_PAIR * b + 2), part))

    return [spec(part, which) for part, which in kinds]


def _head_col(v, head):
    return v[:, head:head + 1]


def _expand_heads(w, j):
    shape = (w.shape[0], LANES)
    return jnp.where(_left_half(shape), jnp.broadcast_to(_head_col(w, 2 * j), shape), jnp.broadcast_to(_head_col(w, 2 * j + 1), shape))


def _attn_fwd(qkv, slopes, dil, name):
    s, d3 = qkv.shape
    d = d3 // 3
    nb = s // (dil * SPAN)
    table, _ = _att_bias(slopes, dil)

    def body(q_ref, k_ref, kp_ref, v_ref, vp_ref, tb_ref, o_ref, l_ref):
        b = pl.program_id(0)
        left = _left_half((SPAN, LANES))
        lane = lax.broadcasted_iota(jnp.int32, (SPAN, LANES), 1)
        for sub in range(ATT_PAIR):
            rows, before = slice(sub * SPAN, (sub + 1) * SPAN), slice((sub - 1) * SPAN, sub * SPAN)
            variant = jnp.where((ATT_PAIR * b) % nb == 0, 0, 1) if sub == 0 else 1
            lses = jnp.zeros((SPAN, LANES), F32)
            for hp in range(d // LANES):
                cols = slice(hp * LANES, (hp + 1) * LANES)
                q = q_ref[rows, cols]
                q2 = jnp.concatenate([_head_mask(q, 0), _head_mask(q, 1)], axis=0) * ATT_SCALE
                k2 = jnp.concatenate([k_ref[rows, cols], kp_ref[:, cols] if sub == 0 else k_ref[before, cols]], axis=0)
                v2 = jnp.concatenate([v_ref[rows, cols], vp_ref[:, cols] if sub == 0 else v_ref[before, cols]], axis=0)
                sc = _dot_nt(q2, k2) + tb_ref[variant, hp]
                m = jnp.max(sc, axis=-1, keepdims=True)
                p = jnp.exp(sc - m)
                l = jnp.sum(p, axis=-1, keepdims=True)
                r = _dot_nn(p, v2) * (1.0 / l)
                lse = m + jnp.log(l)
                o_ref[rows, cols] = jnp.where(left, r[:SPAN], r[SPAN:])
                lses = jnp.where(lane == 2 * hp, lse[:SPAN], jnp.where(lane == 2 * hp + 1, lse[SPAN:], lses))
            l_ref[rows, :] = lses

    specs = _att_specs(s, d, dil, [(0, "pair"), (1, "pair"), (1, "prev"), (2, "pair"), (2, "prev")])
    return pl.pallas_call(
        body,
        grid=(s // (ATT_PAIR * SPAN),),
        in_specs=specs + [pl.BlockSpec(table.shape, lambda b: (0, 0, 0, 0))],
        out_specs=[pl.BlockSpec((ATT_PAIR * SPAN, d), lambda b: (b, 0)), pl.BlockSpec((ATT_PAIR * SPAN, LANES), lambda b: (b, 0))],
        out_shape=[jax.ShapeDtypeStruct((s, d), F32), jax.ShapeDtypeStruct((s, LANES), F32)],
        name=name,
        compiler_params=_cparams(("parallel",)),
    )(qkv, qkv, qkv, qkv, qkv, table)


def _attn_bwd(qkv, do, lse, dd, slopes, dil, name):
    s, d3 = qkv.shape
    d = d3 // 3
    nb = s // (dil * SPAN)
    _, table = _att_bias(slopes, dil)

    def heads_stacked(cur, nxt):
        return jnp.concatenate([_head_mask(cur, 0), _head_mask(cur, 1), _head_mask(nxt, 0), _head_mask(nxt, 1)], axis=0)

    def cols_stacked(cur, nxt, hp):
        return jnp.concatenate([jnp.broadcast_to(_head_col(a, 2 * hp + h), (SPAN, LANES)) for a in (cur, nxt) for h in range(2)], axis=0)

    def body(k_ref, v_ref, q_ref, qn_ref, do_ref, don_ref, l_ref, ln_ref, dd_ref, ddn_ref, tb_ref, out_ref, carry):
        b = pl.program_id(0)

        @pl.when(b == 0)
        def _():
            carry[...] = jnp.zeros_like(carry)

        left = _left_half((SPAN, LANES))
        for sub in range(ATT_PAIR):
            rows, after = slice(sub * SPAN, (sub + 1) * SPAN), slice((sub + 1) * SPAN, (sub + 2) * SPAN)
            last = sub == ATT_PAIR - 1
            variant = jnp.where((ATT_PAIR * b + sub) % nb == nb - 1, 0, 1) if last else 1
            lse_c, dd_c = l_ref[rows, :], dd_ref[rows, :]
            lse_n, dd_n = (ln_ref[...], ddn_ref[...]) if last else (l_ref[after, :], dd_ref[after, :])
            for hp in range(d // LANES):
                cols = slice(hp * LANES, (hp + 1) * LANES)
                k, v = k_ref[rows, cols], v_ref[rows, cols]
                q4 = heads_stacked(q_ref[rows, cols], qn_ref[:, cols] if last else q_ref[after, cols])
                do4 = heads_stacked(do_ref[rows, cols], don_ref[:, cols] if last else do_ref[after, cols])
                sc = _dot_nt(q4 * ATT_SCALE, k) + tb_ref[variant, hp]
                p = jnp.exp(sc - cols_stacked(lse_c, lse_n, hp))
                ds = p * (_dot_nt(do4, v) - cols_stacked(dd_c, dd_n, hp))
                dq4 = _dot_nn(ds, k)
                dq_cur = jnp.where(left, dq4[:SPAN], dq4[SPAN:2 * SPAN]) + carry[:, cols]
                carry[:, cols] = jnp.where(left, dq4[2 * SPAN:3 * SPAN], dq4[3 * SPAN:])
                out_ref[rows, cols] = (dq_cur * ATT_SCALE).astype(out_ref.dtype)
                out_ref[rows, d + hp * LANES:d + (hp + 1) * LANES] = (_dot_tn(ds, q4) * ATT_SCALE).astype(out_ref.dtype)
                out_ref[rows, 2 * d + hp * LANES:2 * d + (hp + 1) * LANES] = _dot_tn(p, do4).astype(out_ref.dtype)

    qkv_specs = _att_specs(s, d, dil, [(1, "pair"), (2, "pair"), (0, "pair"), (0, "next")])
    wide = _att_specs(s, d, dil, [(0, "pair"), (0, "next")])
    heads = _att_specs(s, LANES, dil, [(0, "pair"), (0, "next")])
    return pl.pallas_call(
        body,
        grid=(s // (ATT_PAIR * SPAN),),
        in_specs=qkv_specs + wide + heads + heads + [pl.BlockSpec(table.shape, lambda b: (0, 0, 0, 0))],
        out_specs=pl.BlockSpec((ATT_PAIR * SPAN, d3), lambda b: (b, 0)),
        out_shape=jax.ShapeDtypeStruct((s, d3), MXU_DTYPE),
        scratch_shapes=[pltpu.VMEM((SPAN, d), F32)],
        name=name,
        compiler_params=_cparams(("arbitrary",)),
    )(qkv, qkv, qkv, qkv, do, do, lse, lse, dd, dd, table)


def _mix_weights(l_refs):
    ls = [r[...] for r in l_refs]
    m = functools.reduce(jnp.maximum, ls)
    es = [jnp.exp(l - m) for l in ls]
    tot = functools.reduce(lambda a, c: a + c, es)
    return [e / tot for e in es]


def _combine_fwd(os_, ls_, name):
    s, d = os_[0].shape
    n = len(os_)

    def body(*refs):
        o_refs, l_refs, out_ref = refs[:n], refs[n:2 * n], refs[2 * n]
        ws = _mix_weights(l_refs)
        for j in range(d // LANES):
            cols = slice(j * LANES, (j + 1) * LANES)
            acc = _expand_heads(ws[0], j) * o_refs[0][:, cols]
            for w, o in zip(ws[1:], o_refs[1:]):
                acc = acc + _expand_heads(w, j) * o[:, cols]
            out_ref[:, cols] = acc

    return _rows(body, s, ROW_TILE, [("blk", a) for a in os_ + ls_], [("blk", (s, d), F32)], name)[0]


def _combine_bwd(do, o, ls_, name):
    s, d = o.shape
    n = len(ls_)
    sel = (lax.broadcasted_iota(jnp.int32, (d, LANES), 0) // HEAD_DIM == lax.broadcasted_iota(jnp.int32, (d, LANES), 1)).astype(F32)

    def body(do_ref, o_ref, *rest):
        l_refs, sel_ref, outs = rest[:n], rest[n], rest[n + 1:]
        ws = _mix_weights(l_refs)
        dov = do_ref[...]
        r = jnp.dot(dov * o_ref[...], sel_ref[...], precision=lax.Precision.HIGHEST, preferred_element_type=F32)
        for g in range(n):
            outs[2 * g + 1][...] = ws[g] * r
            for j in range(d // LANES):
                cols = slice(j * LANES, (j + 1) * LANES)
                outs[2 * g][:, cols] = (_expand_heads(ws[g], j) * dov[:, cols]).astype(outs[2 * g].dtype)

    outs = []
    for _ in range(n):
        outs += [("blk", (s, d), MXU_DTYPE), ("blk", (s, LANES), F32)]
    res = _rows(body, s, ROW_TILE, [("blk", do), ("blk", o)] + [("blk", l) for l in ls_] + [("all", sel)], outs, name)
    return [(res[2 * g], res[2 * g + 1]) for g in range(n)]


def _ada_fwd(c_all, w, b, name):
    nsub, d, cs = w.shape

    def body(c_ref, w_ref, b_ref, o_ref):
        cv = c_ref[...]
        sc = cv * (1.0 / (1.0 + jnp.exp(-cv)))
        o_ref[...] = _dot_nn(sc, w_ref[...]) + b_ref[...]

    return pl.pallas_call(
        body,
        grid=(nsub,),
        in_specs=[pl.BlockSpec(c_all.shape, lambda i: (0, 0)), pl.BlockSpec((None, d, cs), lambda i: (i, 0, 0)),
                  pl.BlockSpec((None, 1, cs), lambda i: (i, 0, 0))],
        out_specs=pl.BlockSpec((None, N_DEV, cs), lambda i: (i, 0, 0)),
        out_shape=jax.ShapeDtypeStruct((nsub, N_DEV, cs), F32),
        name=name,
        compiler_params=_cparams(("parallel",)),
    )(c_all, w, b)


def _ada_bwd(c_all_t, dm, name):
    d, nb = c_all_t.shape
    nsub, _, cs = dm.shape

    def body(c_ref, dm_ref, o_ref):
        cv = c_ref[...]
        sc = cv * (1.0 / (1.0 + jnp.exp(-cv)))
        acc = sc[:, 0:1] * dm_ref[0:1, :]
        for bi in range(1, nb):
            acc = acc + sc[:, bi:bi + 1] * dm_ref[bi:bi + 1, :]
        o_ref[...] = acc

    return pl.pallas_call(
        body,
        grid=(nsub,),
        in_specs=[pl.BlockSpec(c_all_t.shape, lambda i: (0, 0)), pl.BlockSpec((None, nb, cs), lambda i: (i, 0, 0))],
        out_specs=pl.BlockSpec((None, d, cs), lambda i: (i, 0, 0)),
        out_shape=jax.ShapeDtypeStruct((nsub, d, cs), F32),
        name=name,
        compiler_params=_cparams(("parallel",)),
    )(c_all_t, dm)


def _row_tile(r, row_elems, block_elems=256 * 1024):
    t = 2 * SUBLANES
    if r % t:
        return r
    while t * 2 * row_elems <= block_elems and r % (t * 2) == 0:
        t *= 2
    return t


def _adamw(w, g, m, v, name):
    shape = w.shape
    c = shape[-1]
    r = w.size // c
    tr = _row_tile(r, c, 512 * 1024)
    w2, g2, m2, v2 = [a.reshape(r, c) for a in (w, g, m, v)]
    bc1 = 1.0 - ADAM_B1 ** ADAM_STEP
    bc2 = 1.0 - ADAM_B2 ** ADAM_STEP

    def body(w_ref, g_ref, m_ref, v_ref, d_ref, nm_ref, nv_ref):
        gv = g_ref[...]
        nm = ADAM_B1 * m_ref[...] + (1.0 - ADAM_B1) * gv
        nv = ADAM_B2 * v_ref[...] + (1.0 - ADAM_B2) * (gv * gv)
        d_ref[...] = -ADAM_LR * ((nm / bc1) / (jnp.sqrt(nv / bc2) + ADAM_EPS) + ADAM_WD * w_ref[...])
        nm_ref[...] = nm
        nv_ref[...] = nv

    res = _rows(body, r, tr, [("blk", a) for a in (w2, g2, m2, v2)], [("blk", (r, c), F32)] * 3, name)
    return [a.reshape(shape) for a in res]


def _sum_slots(buf, name):
    n, r, c = buf.shape
    tr = _row_tile(r, n * c, 2 * 1024 * 1024)

    def body(b_ref, o_ref):
        acc = b_ref[0].astype(F32)
        for k in range(1, n):
            acc = acc + b_ref[k].astype(F32)
        o_ref[...] = acc

    return pl.pallas_call(
        body,
        grid=(r // tr,),
        in_specs=[pl.BlockSpec((n, tr, c), lambda i: (0, i, 0))],
        out_specs=pl.BlockSpec((tr, c), lambda i: (i, 0)),
        out_shape=jax.ShapeDtypeStruct((r, c), F32),
        name=name,
        compiler_params=_cparams(("parallel",)),
    )(buf)


def _me():
    return lax.axis_index("x"), lax.axis_index("y"), lax.axis_index("c")


def _all_gather_small(blk, name, after=()):
    m_per, n = blk.shape

    def body(x_ref, *rest):
        out_ref, send_sems, recv_sems, local_sem = rest[len(after):]
        x, y, c = _me()
        me, sibling = (x, y, c), (x, y, 1 - c)
        chips = [(1 - x, y), (x, 1 - y), (1 - x, 1 - y)]

        def rows(px, py, pc):
            return out_ref.at[pl.ds((4 * px + 2 * py + pc) * m_per, m_per), :]

        def copy(k, block, to, src=None):
            return pltpu.make_async_remote_copy(
                src_ref=rows(*block) if src is None else src, dst_ref=rows(*block),
                send_sem=send_sems.at[k], recv_sem=recv_sems.at[k], device_id=to, device_id_type=MESH)

        mine = pltpu.make_async_copy(x_ref, rows(*me), local_sem)
        mine.start()
        first = [copy(0, me, sibling, src=x_ref)]
        first += [copy(1 + j, me, (*chip, c), src=x_ref) for j, chip in enumerate(chips)]
        for cp in first:
            cp.start()
        passed = [copy(4 + j, (*chip, c), sibling) for j, chip in enumerate(chips)]
        for j, chip in enumerate(chips):
            copy(1 + j, (*chip, c), me).wait_recv()
            passed[j].start()
        copy(0, sibling, me).wait_recv()
        for j, chip in enumerate(chips):
            copy(4 + j, (*chip, 1 - c), me).wait_recv()
        for cp in first + passed:
            cp.wait_send()
        mine.wait()

    return pl.pallas_call(
        body,
        out_shape=jax.ShapeDtypeStruct((N_DEV * m_per, n), blk.dtype),
        in_specs=[pl.BlockSpec(memory_space=pltpu.VMEM)] + [pl.BlockSpec(memory_space=pl.ANY)] * len(after),
        out_specs=pl.BlockSpec(memory_space=pltpu.VMEM),
        scratch_shapes=[pltpu.SemaphoreType.DMA((7,)), pltpu.SemaphoreType.DMA((7,)), pltpu.SemaphoreType.DMA],
        name=name,
        compiler_params=pltpu.CompilerParams(vmem_limit_bytes=VMEM_LIMIT),
    )(blk, *after)


_HBM = pl.BlockSpec(memory_space=pltpu.HBM)
_SEM = pl.BlockSpec(memory_space=pltpu.SEMAPHORE)
_EFFECT = pltpu.SideEffectType.DATAFLOW_SIDE_EFFECTING


def _other_chips(x, y):
    return [(1 - x, y), (x, 1 - y), (1 - x, 1 - y)]


def _gather_copy(w, j, src_ref, land_ref, send_sems, recv_sems, halved=False):
    x, y, c = _me()
    if halved:
        half = src_ref.shape[0] // 2
        src_ref = src_ref.at[pl.ds(c * half, half), :]
    return pltpu.make_async_remote_copy(
        src_ref=src_ref, dst_ref=land_ref.at[2 * x + y], send_sem=send_sems.at[3 * w + j], recv_sem=recv_sems.at[3 * w + j],
        device_id=(*_other_chips(x, y)[j], c), device_id_type=MESH)


def _gather_start(shards, halved, after, name):
    n = len(shards)
    lands = [lax.empty((N_CHIPS, s.shape[0] // 2 if w in halved else s.shape[0], s.shape[1]), s.dtype) for w, s in enumerate(shards)]

    def body(*refs):
        in_refs, land_refs = refs[:n], refs[n:2 * n]
        send_sems, recv_sems = refs[2 * n + 1], refs[2 * n + 2]
        token = refs[-1]
        for w in range(n):
            for j in range(3):
                _gather_copy(w, j, in_refs[w], land_refs[w], send_sems, recv_sems, w in halved).start()
        token[...] = jnp.zeros_like(token)

    res = pl.pallas_call(
        body,
        out_shape=(pltpu.SemaphoreType.DMA((3 * n,)), pltpu.SemaphoreType.DMA((3 * n,)),
                   *[pltpu.HBM(s.shape, s.dtype) for s in shards], *[pltpu.HBM(l.shape, l.dtype) for l in lands],
                   jax.ShapeDtypeStruct((SUBLANES, LANES), F32)),
        in_specs=[_HBM] * (2 * n) + [pl.BlockSpec(memory_space=pl.ANY)],
        out_specs=(_SEM, _SEM, *[_HBM] * (2 * n), pl.BlockSpec(memory_space=pltpu.VMEM)),
        input_output_aliases={i: 2 + i for i in range(2 * n)},
        name=name,
        compiler_params=pltpu.CompilerParams(has_side_effects=_EFFECT),
    )(*[pltpu.with_memory_space_constraint(a, pltpu.HBM) for a in list(shards) + lands], after)
    return res[0], res[1], res[2:2 + n], res[2 + n:2 + 2 * n], res[-1]


def _gather_wait(w, shard, land, send_sems, recv_sems, after, name, halved=False):
    def body(s_ref, land_ref, send_sems, recv_sems, after_ref, s_out, land_out, stage):
        x, y, _ = _me()
        if not halved:
            pltpu.sync_copy(s_ref, stage)
            pltpu.sync_copy(stage, land_out.at[2 * x + y])
        for j in range(3):
            cp = _gather_copy(w, j, s_ref, land_ref, send_sems, recv_sems, halved)
            cp.wait_send()
            cp.wait_recv()

    return pl.pallas_call(
        body,
        out_shape=(pltpu.HBM(shard.shape, shard.dtype), pltpu.HBM(land.shape, land.dtype)),
        in_specs=(_HBM, _HBM, _SEM, _SEM, pl.BlockSpec(memory_space=pl.ANY)),
        out_specs=(_HBM, _HBM),
        input_output_aliases={0: 0, 1: 1},
        scratch_shapes=[pltpu.VMEM((SUBLANES, LANES) if halved else shard.shape, shard.dtype)],
        name=name,
        compiler_params=pltpu.CompilerParams(has_side_effects=_EFFECT, vmem_limit_bytes=VMEM_LIMIT),
    )(shard, land, send_sems, recv_sems, after)


def _assemble_halves(shard, land, name):
    half = land.shape[1]

    def body(s_ref, land_ref, out_ref, send_sems, recv_sems, local_sems):
        x, y, c = _me()
        own = pltpu.make_async_copy(s_ref, out_ref.at[2 * x + y], local_sems.at[3])
        own.start()
        cps = []
        for j, (ox, oy) in enumerate(_other_chips(x, y)):
            qj = 2 * ox + oy
            mine = out_ref.at[qj, pl.ds(c * half, half), :]
            lc = pltpu.make_async_copy(land_ref.at[qj], mine, local_sems.at[j])
            lc.start()
            rc = pltpu.make_async_remote_copy(
                src_ref=land_ref.at[qj], dst_ref=mine, send_sem=send_sems.at[j], recv_sem=recv_sems.at[j],
                device_id=(x, y, 1 - c), device_id_type=MESH)
            rc.start()
            cps.append((lc, rc))
        for lc, rc in cps:
            rc.wait_recv()
        for lc, rc in cps:
            rc.wait_send()
            lc.wait()
        own.wait()

    vmem = pl.BlockSpec(memory_space=pltpu.VMEM)
    return pl.pallas_call(
        body,
        out_shape=jax.ShapeDtypeStruct((N_CHIPS,) + shard.shape, shard.dtype),
        in_specs=[vmem, vmem],
        out_specs=vmem,
        scratch_shapes=[pltpu.SemaphoreType.DMA((3,)), pltpu.SemaphoreType.DMA((3,)), pltpu.SemaphoreType.DMA((4,))],
        name=name,
        compiler_params=pltpu.CompilerParams(vmem_limit_bytes=VMEM_LIMIT),
    )(shard, land)


def _piece_shape(shape, kind):
    k, nn = shape
    if kind == "all":
        return (k, nn)
    return (k // 2, nn // N_CHIPS) if kind == "col" else (k // N_CHIPS // 2, nn)


def _piece_of(g_ref, kind, tq, tc):
    pr, pc = _piece_shape(g_ref.shape, kind)
    if kind == "all":
        return g_ref
    if kind == "col":
        return g_ref.at[pl.ds(tc * pr, pr), pl.ds(tq * pc, pc)]
    return g_ref.at[pl.ds((2 * tq + tc) * pr, pr), :]


def _scatter_copy(w, r, kind, g_ref, land_ref, send_sems, recv_sems):
    x, y, c = _me()
    tx, ty, tc = (x + ((r >> 2) & 1)) % 2, (y + ((r >> 1) & 1)) % 2, (c + (r & 1)) % 2
    return pltpu.make_async_remote_copy(
        src_ref=_piece_of(g_ref, kind, 2 * tx + ty, tc), dst_ref=land_ref.at[4 * x + 2 * y + c],
        send_sem=send_sems.at[N_DEV * w + r], recv_sem=recv_sems.at[N_DEV * w + r], device_id=(tx, ty, tc), device_id_type=MESH)


def _scatter_start(gs, kinds, name):
    n = len(gs)
    pieces = [_piece_shape(g.shape, kind) for g, kind in zip(gs, kinds)]
    lands = [lax.empty((N_DEV,) + p, g.dtype) for p, g in zip(pieces, gs)]

    def body(*refs):
        g_refs, land_refs, send_sems, recv_sems = refs[:n], refs[n:2 * n], refs[2 * n], refs[2 * n + 1]
        land_outs, stages = refs[3 * n + 2:4 * n + 2], refs[4 * n + 2:]
        x, y, c = _me()
        for w in range(n):
            for r in range(1, N_DEV):
                _scatter_copy(w, r, kinds[w], g_refs[w], land_refs[w], send_sems, recv_sems).start()
        for w in range(n):
            pltpu.sync_copy(_piece_of(g_refs[w], kinds[w], 2 * x + y, c), stages[w])
            pltpu.sync_copy(stages[w], land_outs[w].at[4 * x + 2 * y + c])

    arrays = list(gs) + lands
    res = pl.pallas_call(
        body,
        out_shape=(pltpu.SemaphoreType.DMA((N_DEV * n,)), pltpu.SemaphoreType.DMA((N_DEV * n,)),
                   *[pltpu.HBM(a.shape, a.dtype) for a in arrays]),
        in_specs=[_HBM] * (2 * n),
        out_specs=(_SEM, _SEM, *[_HBM] * (2 * n)),
        input_output_aliases={i: 2 + i for i in range(2 * n)},
        scratch_shapes=[pltpu.VMEM(p, g.dtype) for p, g in zip(pieces, gs)],
        name=name,
        compiler_params=pltpu.CompilerParams(has_side_effects=_EFFECT, vmem_limit_bytes=VMEM_LIMIT),
    )(*[pltpu.with_memory_space_constraint(a, pltpu.HBM) for a in arrays])
    return res[0], res[1], res[2:2 + n], res[2 + n:]


def _scatter_wait(send_sems, recv_sems, gs, lands, kinds, after, name):
    n = len(gs)

    def body(*refs):
        g_refs, land_refs, send_sems, recv_sems = refs[:n], refs[n:2 * n], refs[2 * n], refs[2 * n + 1]
        for w in range(n):
            for r in range(1, N_DEV):
                cp = _scatter_copy(w, r, kinds[w], g_refs[w], land_refs[w], send_sems, recv_sems)
                cp.wait_send()
                cp.wait_recv()

    arrays = list(gs) + list(lands)
    return pl.pallas_call(
        body,
        out_shape=tuple(pltpu.HBM(a.shape, a.dtype) for a in arrays),
        in_specs=(*[_HBM] * (2 * n), _SEM, _SEM, pl.BlockSpec(memory_space=pl.ANY)),
        out_specs=tuple([_HBM] * (2 * n)),
        input_output_aliases={i: i for i in range(2 * n)},
        name=name,
        compiler_params=pltpu.CompilerParams(has_side_effects=_EFFECT),
    )(*arrays, send_sems, recv_sems, after)[n:]


def _swap_halves(halves, name):
    n = len(halves)

    def body(*refs):
        in_refs, out_refs = refs[:n], refs[n:2 * n]
        send_sems, recv_sems, local_sems = refs[2 * n:]
        x, y, c = _me()
        cps = []
        for w in range(n):
            lc = pltpu.make_async_copy(in_refs[w], out_refs[w].at[c], local_sems.at[w])
            lc.start()
            rc = pltpu.make_async_remote_copy(
                src_ref=in_refs[w], dst_ref=out_refs[w].at[c], send_sem=send_sems.at[w], recv_sem=recv_sems.at[w],
                device_id=(x, y, 1 - c), device_id_type=MESH)
            rc.start()
            cps.append((lc, rc))
        for lc, rc in cps:
            rc.wait_recv()
        for lc, rc in cps:
            rc.wait_send()
            lc.wait()

    vmem = pl.BlockSpec(memory_space=pltpu.VMEM)
    return pl.pallas_call(
        body,
        out_shape=[jax.ShapeDtypeStruct((2,) + h.shape, h.dtype) for h in halves],
        in_specs=[vmem] * n,
        out_specs=[vmem] * n,
        scratch_shapes=[pltpu.SemaphoreType.DMA((n,)), pltpu.SemaphoreType.DMA((n,)), pltpu.SemaphoreType.DMA((n,))],
        name=name,
        compiler_params=pltpu.CompilerParams(vmem_limit_bytes=VMEM_LIMIT),
    )(*halves)


def _to_streams(a, dil):
    if dil == 1:
        return a
    s, c = a.shape
    return a.reshape(s // dil, dil, c).transpose(1, 0, 2).reshape(s, c)


def _from_streams(a, dil):
    if dil == 1:
        return a
    s, c = a.shape
    return a.reshape(dil, s // dil, c).transpose(1, 0, 2).reshape(s, c)


def _mm_tiles(s):
    return min(s, 2048)


def _local_step(x0, target, mvec, ln_g, ln_b, small, fetch, emit, start):
    s, d = x0.shape
    tm = _mm_tiles(s)
    row = lambda v: v.reshape(1, -1)
    shift = [row(mvec[i, :d]) for i in range(4)]
    scale = [row(mvec[i, d:2 * d]) for i in range(4)]
    gate = [row(1.0 + mvec[i, 2 * d:]) for i in range(4)]
    lg = [row(ln_g[i]) for i in range(4)]
    lb = [row(ln_b[i]) for i in range(4)]
    mm = functools.partial(_mm, tm=tm)
    mm_w = functools.partial(_mm, tm=1024, tk=min(s, 2048), mode="tn")

    xs, ys, big = [x0], [], {}
    h0 = _mod(x0, scale[0], shift[0], start, "mod0")
    big["a_w_in"] = fetch("a_w_in", h0)
    uvpre = mm(h0, big["a_w_in"], mode="nn", name="a_in", outs=[F32], tn=512, tk=1024,
               epi=lambda r, bias: [r + bias], extras=[("row", small["a_b_in"])])
    gated = _spatial_fwd(uvpre, small["a_vn_g"], small["a_vn_b"], small["wc"], small["bias_full"], "a_spatial")
    big["a_w_out"] = fetch("a_w_out", gated)
    ys.append(mm(gated, big["a_w_out"], mode="nn", name="a_out", outs=[F32], tn=1024, tk=1024))
    x1, h1 = _resid_ln(xs[0], ys[0], gate[0], lg[0], lb[0], (scale[1], shift[1]), "ln0")
    xs.append(x1)
    relu2 = lambda r: [jnp.square(jnp.maximum(r, 0.0))]
    big["up0"] = fetch("up0", h1)
    r0 = mm(h1, big["up0"], mode="nn", name="up0", outs=[MXU_DTYPE], tn=1024, tk=1024, epi=relu2)
    big["down0"] = fetch("down0", r0)
    ys.append(mm(r0, big["down0"], mode="nn", name="down0", outs=[F32], tm=min(s, 1024), tn=1024, tk=2048))
    x2, h2 = _resid_ln(xs[1], ys[1], gate[1], lg[1], lb[1], (scale[2], shift[2]), "ln1")
    xs.append(x2)
    hg, qkvs, o_g, l_g, l_streams = [], [], [], [], []
    big["b_w_qkv"] = fetch("b_w_qkv", h2)
    for g, (_, dil) in enumerate(B_PATTERNS):
        hp = _to_streams(h2, dil)
        qkv = mm(hp, big["b_w_qkv"], mode="nn", name=f"qkv{g}", outs=[MXU_DTYPE], tn=768, tk=1024, b_col0=g * 3 * d, n_out=3 * d)
        og, lgv = _attn_fwd(qkv, small["slopes"], dil, f"attn_fwd{g}")
        hg.append(hp)
        qkvs.append(qkv)
        o_g.append(_from_streams(og, dil))
        l_g.append(_from_streams(lgv, dil))
        l_streams.append(lgv)
    o_mix = _combine_fwd(o_g, l_g, "combine")
    big["b_w_out"] = fetch("b_w_out", o_mix)
    ys.append(mm(o_mix, big["b_w_out"], mode="nn", name="b_out", outs=[F32], tn=1024, tk=1024))
    x3, h3 = _resid_ln(xs[2], ys[2], gate[2], lg[2], lb[2], (scale[3], shift[3]), "ln2")
    xs.append(x3)
    big["up1"] = fetch("up1", h3)
    r1 = mm(h3, big["up1"], mode="nn", name="up1", outs=[MXU_DTYPE], tn=1024, tk=1024, epi=relu2)
    big["down1"] = fetch("down1", r1)
    ys.append(mm(r1, big["down1"], mode="nn", name="down1", outs=[F32], tm=min(s, 1024), tn=1024, tk=2048))

    gb, red_ln, red_mod = {}, [None] * 4, [None] * 4

    def mlp_bwd(i, h, r, dyy):
        gb[f"down{i}"] = mm_w(r, dyy, name=f"g_down{i}", outs=[MXU_DTYPE], tn=1024)
        da = mm(dyy, big[f"down{i}"], mode="nt", name=f"d_down{i}", outs=[MXU_DTYPE], tn=1024, tk=1024,
                after=emit(f"down{i}", gb[f"down{i}"]),
                epi=lambda acc, rv: [acc * (2.0 * jnp.sqrt(rv.astype(F32)))], extras=[("full", r)])
        gb[f"up{i}"] = mm_w(h, da, name=f"g_up{i}", outs=[MXU_DTYPE], tn=1024)
        return [mm(da, big[f"up{i}"], mode="nt", name=f"d_up{i}", outs=[F32], tn=1024, tk=1024, after=emit(f"up{i}", gb[f"up{i}"]))]

    def join(sub, dxr, dhs, after=None):
        res = _mod_ln_bwd(dxr, dhs, xs[sub], scale[sub], xs[sub - 1], ys[sub - 1], gate[sub - 1], lg[sub - 1],
                          f"mod_ln_bwd{sub}", after=after)
        red_mod[sub], red_ln[sub - 1] = res[2], res[3]
        return res[0], res[1]

    loss, dxr, dyy, red_ln[3] = _last_ln_loss_bwd(xs[3], ys[3], gate[3], lg[3], lb[3], target, "ln3_loss_bwd")
    dxr, dyy = join(3, dxr, mlp_bwd(1, h3, r1, dyy))
    gb["b_w_out"] = mm_w(o_mix, dyy, name="g_b_out", outs=[MXU_DTYPE], tn=1024, tk=1024)
    do = mm(dyy, big["b_w_out"], mode="nt", name="d_b_out", outs=[F32], tn=1024, tk=1024, after=emit("b_w_out", gb["b_w_out"]))
    parts = _combine_bwd(do, o_mix, l_g, "combine_bwd")
    dhs, gq = [], None
    for g, (_, dil) in enumerate(B_PATTERNS):
        do_g, dd_g = _to_streams(parts[g][0], dil), _to_streams(parts[g][1], dil)
        dqkv = _attn_bwd(qkvs[g], do_g, l_streams[g], dd_g, small["slopes"], dil, f"attn_bwd{g}")
        gq = mm_w(hg[g], dqkv, name=f"g_qkv{g}", outs=[MXU_DTYPE], tn=1024, out_col0=g * 3 * d, out_cols=len(B_PATTERNS) * 3 * d, into=gq)
        dh = mm(dqkv, big["b_w_qkv"], mode="nt", name=f"d_qkv{g}", outs=[F32], tn=1024, tk=768, b_col0=g * 3 * d)
        dhs.append(_from_streams(dh, dil))
    gb["b_w_qkv"] = gq
    dxr, dyy = join(2, dxr, dhs, after=emit("b_w_qkv", gb["b_w_qkv"]))
    dxr, dyy = join(1, dxr, mlp_bwd(0, h1, r0, dyy))
    gb["a_w_out"] = mm_w(gated, dyy, name="g_a_out", outs=[MXU_DTYPE], tn=1024)
    dgated = mm(dyy, big["a_w_out"], mode="nt", name="d_a_out", outs=[F32], tn=1024, tk=1024, after=emit("a_w_out", gb["a_w_out"]))
    duv, dws, dbias, dbin, dvg, dvb = _spatial_bwd(uvpre, dgated, small["a_vn_g"], small["a_vn_b"], small["wc"],
                                                   small["wct"], small["bias_full"], "a_spatial_bwd")
    tril = jnp.tril(jnp.ones((CHUNK, CHUNK), bool))
    dws = jnp.where(tril, dws, 0.0).reshape(-1, LANES)
    gb["a_w_in"] = mm_w(h0, duv, name="g_a_in", outs=[MXU_DTYPE], tn=1024, after=emit("a_w_s", dws.astype(MXU_DTYPE)))
    dh = mm(duv, big["a_w_in"], mode="nt", name="d_a_in", outs=[F32], tn=1024, tk=512, after=emit("a_w_in", gb["a_w_in"]))
    dx, red_mod[0] = _mod_bwd(dxr, [dh], xs[0], scale[0], "mod_bwd0")
    dm = [jnp.concatenate([red_mod[i][0], red_mod[i][1], red_ln[i][2]]) for i in range(4)]
    dlg, dlb = [red_ln[i][0] for i in range(4)], [red_ln[i][1] for i in range(4)]

    gsmall = {
        "a_b_in": dbin.reshape(-1), "a_vn_g": dvg.reshape(-1), "a_vn_b": dvb.reshape(-1),
        "a_w_s": dws.reshape(-1),
        "a_b_s": dbias.reshape(CHUNK, A_GROUPS, d // A_GROUPS).sum(-1).T.reshape(-1),
    }
    return loss, dx, gb, jnp.stack(dm), jnp.stack(dlg), jnp.stack(dlb), gsmall


BIG = ("a_w_in", "a_w_out", "up0", "down0", "b_w_qkv", "b_w_out", "up1", "down1")
BIG_KIND = {"a_w_in": "col", "a_w_out": "row", "b_w_qkv": "col", "b_w_out": "row",
            "up0": "col", "up1": "col", "down0": "row", "down1": "row", "a_w_s": "all"}
HALVED = ("a_w_in", "down0", "b_w_qkv")
SCATTER_GROUPS = (("down1", "up1"), ("b_w_out", "b_w_qkv"), ("down0", "up0"), ("a_w_out", "a_w_in"), ("a_w_s",))
SMALL = ("a_b_in", "a_vn_g", "a_vn_b", "a_b_s")


def kernel(x, c, ada_w, ada_b, ln_g, ln_b, a_w_in, a_b_in, a_vn_g, a_vn_b, a_w_s, a_b_s, a_w_out, b_w_qkv, b_w_out, mlp_w_up, mlp_w_down, loss_target, m_ada_w, m_ada_b, m_ln_g, m_ln_b, m_a_w_in, m_a_b_in, m_a_vn_g, m_a_vn_b, m_a_w_s, m_a_b_s, m_a_w_out, m_b_w_qkv, m_b_w_out, m_mlp_w_up, m_mlp_w_down, v_ada_w, v_ada_b, v_ln_g, v_ln_b, v_a_w_in, v_a_b_in, v_a_vn_g, v_a_vn_b, v_a_w_s, v_a_b_s, v_a_w_out, v_b_w_qkv, v_b_w_out, v_mlp_w_up, v_mlp_w_down):
    s, d = x.shape[1], x.shape[2]
    xi, yi, ci = _me()
    q = 2 * xi + yi
    dev = 2 * q + ci
    nsub = 2 * DEPTH
    cs = ada_w.shape[-1]
    ls = ln_g.shape[-1]

    shards = {
        "a_w_in": a_w_in[0], "a_w_out": a_w_out[0], "b_w_qkv": b_w_qkv[0], "b_w_out": b_w_out[0],
        "up0": mlp_w_up[0], "up1": mlp_w_up[1], "down0": mlp_w_down[0], "down1": mlp_w_down[1],
    }
    cast = [shards[k].astype(MXU_DTYPE) for k in BIG]

    pack = jnp.concatenate([c.reshape(-1), ln_g.reshape(-1), ln_b.reshape(-1)]).reshape(-1, LANES)
    got = _all_gather_small(pack, "gather_small", after=cast).reshape(N_DEV, -1)
    c_all = got[:, :d]
    per_chip = got[0::2]
    ln_g_full = per_chip[:, d:d + nsub * ls].reshape(N_CHIPS, nsub, ls).transpose(1, 0, 2).reshape(nsub, d)
    ln_b_full = per_chip[:, d + nsub * ls:].reshape(N_CHIPS, nsub, ls).transpose(1, 0, 2).reshape(nsub, d)
    m_part = _ada_fwd(c_all, ada_w.reshape(nsub, d, cs), ada_b.reshape(nsub, 1, cs), "ada_fwd")
    m_all = _all_gather_small(m_part.reshape(-1, LANES), "gather_mod").reshape(N_DEV, nsub, N_DEV, cs)
    m_mine = lax.dynamic_index_in_dim(m_all[0::2], dev, axis=2, keepdims=False)
    mvec = m_mine.transpose(1, 0, 2).reshape(nsub, 3 * d)

    halved = {BIG.index(k) for k in HALVED}
    send_sems, recv_sems, shard_thru, lands, token = _gather_start(cast, halved, mvec, "gather_start")

    def fetch(k, after):
        w = BIG.index(k)
        shard, gw = _gather_wait(w, shard_thru[w], lands[w], send_sems, recv_sems, after, f"gather_wait_{k}", w in halved)
        if w in halved:
            gw = _assemble_halves(shard, gw, f"assemble_{k}")
        return gw if BIG_KIND[k] == "col" else gw.reshape(1, -1, gw.shape[-1])

    scattering, pending = {}, {}

    def emit(k, g):
        pending[k] = g
        group = next(gr for gr in SCATTER_GROUPS if k in gr)
        if k != group[-1]:
            return None
        scattering[group] = _scatter_start([pending[m] for m in group], [BIG_KIND[m] for m in group], f"scatter_start_{k}")
        return scattering[group][2][0]

    tril = jnp.tril(jnp.ones((CHUNK, CHUNK), bool))
    wc = jnp.where(tril, a_w_s[0], 0.0).astype(MXU_DTYPE)
    heads = jnp.arange(1, B_HEADS + 1, dtype=F32)
    small = {
        "a_b_in": a_b_in, "a_vn_g": a_vn_g, "a_vn_b": a_vn_b,
        "wc": wc, "wct": wc.transpose(0, 2, 1),
        "bias_full": jnp.repeat(a_b_s[0].T, d // A_GROUPS, axis=1),
        "slopes": jnp.exp2(-8.0 * heads / B_HEADS),
    }

    loss_part, grad_x, gb, dm, dlg, dlb, gsmall = _local_step(x[0], loss_target[0], mvec, ln_g_full, ln_b_full, small, fetch, emit, token)
    loss = lax.psum(loss_part, ("x", "y", "c"))

    weights = dict(ada_w=ada_w, ada_b=ada_b, ln_g=ln_g, ln_b=ln_b, a_w_in=a_w_in, a_b_in=a_b_in, a_vn_g=a_vn_g, a_vn_b=a_vn_b,
                   a_w_s=a_w_s, a_b_s=a_b_s, a_w_out=a_w_out, b_w_qkv=b_w_qkv, b_w_out=b_w_out, mlp_w_up=mlp_w_up, mlp_w_down=mlp_w_down)
    ms = dict(ada_w=m_ada_w, ada_b=m_ada_b, ln_g=m_ln_g, ln_b=m_ln_b, a_w_in=m_a_w_in, a_b_in=m_a_b_in, a_vn_g=m_a_vn_g, a_vn_b=m_a_vn_b,
              a_w_s=m_a_w_s, a_b_s=m_a_b_s, a_w_out=m_a_w_out, b_w_qkv=m_b_w_qkv, b_w_out=m_b_w_out, mlp_w_up=m_mlp_w_up, mlp_w_down=m_mlp_w_down)
    vs = dict(ada_w=v_ada_w, ada_b=v_ada_b, ln_g=v_ln_g, ln_b=v_ln_b, a_w_in=v_a_w_in, a_b_in=v_a_b_in, a_vn_g=v_a_vn_g, a_vn_b=v_a_vn_b,
              a_w_s=v_a_w_s, a_b_s=v_a_b_s, a_w_out=v_a_w_out, b_w_qkv=v_b_w_qkv, b_w_out=v_b_w_out, mlp_w_up=v_mlp_w_up, mlp_w_down=v_mlp_w_down)
    grads, updates = {}, {}

    def update(k):
        updates[k] = _adamw(weights[k], grads[k], ms[k], vs[k], f"adamw_{k}")
        return updates[k][0]

    gfull = {}

    def big_group(group, after):
        bufs = []
        for pair in (group[:2], group[2:]):
            bufs += _scatter_wait(*scattering[pair], [BIG_KIND[m] for m in pair], after, f"scatter_wait_{pair[-1]}")
        halves = [_sum_slots(b, f"sum_{k}") for k, b in zip(group, bufs)]
        fulls = _swap_halves(halves, f"swap_halves_{group[0]}")
        gfull.update({k: f.reshape(-1, f.shape[-1]) for k, f in zip(group, fulls)})

    big_group(SCATTER_GROUPS[0] + SCATTER_GROUPS[1], grad_x)
    grads["b_w_qkv"], grads["b_w_out"] = gfull["b_w_qkv"][None], gfull["b_w_out"][None]
    update("b_w_out")
    done = update("b_w_qkv")

    pack_b = jnp.concatenate([dm.reshape(-1), dlg.reshape(-1), dlb.reshape(-1)] + [gsmall[k] for k in SMALL])
    n_small = pack_b.shape[0]
    pack_b = jnp.pad(pack_b, (0, -n_small % (256 * LANES)))
    got_b = _all_gather_small(pack_b.reshape(-1, LANES), "gather_small_grads", after=[done]).reshape(N_DEV, -1, LANES)
    tot = _sum_slots(got_b, "sum_small").reshape(-1)
    o = 0
    dm_tot = tot[o:o + nsub * 3 * d].reshape(nsub, 3 * d); o += nsub * 3 * d
    dlg_tot = tot[o:o + nsub * d].reshape(nsub, d); o += nsub * d
    dlb_tot = tot[o:o + nsub * d].reshape(nsub, d); o += nsub * d
    g_small = {}
    for k, ref in zip(SMALL, (a_b_in, a_vn_g, a_vn_b, a_b_s)):
        g_small[k] = tot[o:o + ref.size].reshape(ref.shape); o += ref.size
    assert o == n_small
    aws = _scatter_wait(*scattering[("a_w_s",)], ["all"], tot, "scatter_wait_a_w_s")[0]
    g_small["a_w_s"] = _sum_slots(aws, "sum_a_w_s").reshape(a_w_s.shape)
    dm_all = got_b.reshape(N_DEV, -1)[:, :nsub * 3 * d].reshape(N_DEV, nsub, 3 * d)
    dm_cols = lax.dynamic_slice_in_dim(dm_all, q * cs, cs, axis=2).transpose(1, 0, 2)
    grads.update({
        "ada_w": _ada_bwd(c_all.T, dm_cols, "ada_bwd").reshape(ada_w.shape),
        "ada_b": lax.dynamic_slice_in_dim(dm_tot, q * cs, cs, axis=1).reshape(ada_b.shape),
        "ln_g": lax.dynamic_slice_in_dim(dlg_tot, q * ls, ls, axis=1).reshape(ln_g.shape),
        "ln_b": lax.dynamic_slice_in_dim(dlb_tot, q * ls, ls, axis=1).reshape(ln_b.shape),
        **g_small,
    })
    for k in ("ada_b", "ln_g", "ln_b", "a_w_s") + SMALL:
        update(k)
    done = update("ada_w")

    big_group(SCATTER_GROUPS[2] + SCATTER_GROUPS[3], done)
    grads.update({
        "a_w_in": gfull["a_w_in"][None], "a_w_out": gfull["a_w_out"][None],
        "mlp_w_up": jnp.stack([gfull["up0"], gfull["up1"]]), "mlp_w_down": jnp.stack([gfull["down0"], gfull["down1"]]),
    })
    for k in ("a_w_in", "a_w_out", "mlp_w_up", "mlp_w_down"):
        update(k)
    names = list(weights)
    return (loss, grad_x[None], *[grads[k] for k in names], *[updates[k][0] for k in names],
            *[updates[k][1] for k in names], *[updates[k][2] for k in names])
```

```python
import functools
import math

import jax
import jax.numpy as jnp
from jax import lax
from jax.experimental import pallas as pl
from jax.experimental.pallas import tpu as pltpu

F32 = jnp.float32
MXU_DTYPE = jnp.bfloat16

DEPTH = 2
CHUNK = 128
A_GROUPS = 16
B_HEADS = 16
HEAD_DIM = 64
B_PATTERNS = ((128, 1), (512, 4), (2048, 16))
SPAN = 128
ALPHA = (2 * DEPTH) ** 0.25
LN_EPS = 1e-5
NEG = -1e30
ATT_SCALE = HEAD_DIM ** -0.5
ADAM_LR, ADAM_B1, ADAM_B2, ADAM_EPS, ADAM_WD, ADAM_STEP = 0.001, 0.9, 0.999, 1e-08, 0.01, 10

N_CHIPS = 4
N_DEV = 8
LANES = 128
SUBLANES = 8
VMEM_LIMIT = 52 * 1024 * 1024
ROW_TILE = 512
MM_ROW_CHUNK = 256
MESH = pl.DeviceIdType.MESH


def _cparams(sem):
    return pltpu.CompilerParams(dimension_semantics=sem, vmem_limit_bytes=VMEM_LIMIT)


def _fold8(v):
    r, c = v.shape
    return jnp.sum(v.reshape(r // SUBLANES, SUBLANES, c), axis=0)


def _gelu(x):
    c = math.sqrt(2.0 / math.pi)
    return 0.5 * x * (1.0 + jnp.tanh(c * (x + 0.044715 * (x * x * x))))


def _gelu_grad(x):
    c = math.sqrt(2.0 / math.pi)
    t = jnp.tanh(c * (x + 0.044715 * (x * x * x)))
    return 0.5 * (1.0 + t) + 0.5 * x * (1.0 - t * t) * c * (1.0 + 3.0 * 0.044715 * x * x)


def _dot(a, b, dims):
    return lax.dot_general(a.astype(MXU_DTYPE), b.astype(MXU_DTYPE), (dims, ((), ())), preferred_element_type=F32)


def _dot_nn(a, b):
    return _dot(a, b, ((1,), (0,)))


def _dot_nt(a, b):
    return _dot(a, b, ((1,), (1,)))


def _dot_tn(a, b):
    return _dot(a, b, ((0,), (0,)))


def _mm(a, b, *, mode, name, outs, tm, tn, tk, epi=None, extras=(), b_col0=0, n_out=None, after=None,
        out_col0=0, out_cols=None, into=None):
    if mode == "nn":
        m, kdim = a.shape
        p, kb, ns = b.shape
        assert kb == kdim and ns % tn == 0 and b_col0 % tn == 0
        n = n_out if n_out is not None else p * ns
        npt, j0 = ns // tn, b_col0 // tn
        a_spec = pl.BlockSpec((tm, tk), lambda i, j, k: (i, k))
        b_spec = pl.BlockSpec((None, tk, tn), lambda i, j, k: ((j + j0) // npt, k, (j + j0) % npt))
        dot = _dot_nn
    elif mode == "nt":
        m, kdim = a.shape
        p, n, ns = b.shape
        assert ns % tk == 0 and b_col0 % tk == 0
        npt, j0 = ns // tk, b_col0 // tk
        a_spec = pl.BlockSpec((tm, tk), lambda i, j, k: (i, k))
        b_spec = pl.BlockSpec((None, tn, tk), lambda i, j, k: ((k + j0) // npt, j, (k + j0) % npt))
        dot = _dot_nt
    else:
        kdim, m = a.shape
        kb, n = b.shape
        assert kb == kdim
        a_spec = pl.BlockSpec((tk, tm), lambda i, j, k: (k, i))
        b_spec = pl.BlockSpec((tk, tn), lambda i, j, k: (k, j))
        dot = _dot_tn
    assert m % tm == 0 and n % tn == 0 and kdim % tk == 0, (name, m, n, kdim, tm, tn, tk)
    nk = kdim // tk
    ex_specs, ex_arrays = [], []
    for kind, arr in extras:
        if kind == "row":
            ex_specs.append(pl.BlockSpec((1, tn), lambda i, j, k: (0, j)))
        else:
            ex_specs.append(pl.BlockSpec((tm, tn), lambda i, j, k: (i, j)))
        ex_arrays.append(arr)
    n_ex, n_o = len(ex_arrays), len(outs)
    deps = [d for d in (after, into) if d is not None]
    n_dep = len(deps)
    j_out = out_col0 // tn
    assert out_col0 % tn == 0 and (into is None or len(outs) == 1)

    def body(a_ref, b_ref, *rest):
        ex_refs, o_refs = rest[:n_ex], rest[n_ex + n_dep:n_ex + n_dep + n_o]
        k = pl.program_id(2)

        chunks = [slice(r0, r0 + min(tm, MM_ROW_CHUNK)) for r0 in range(0, tm, min(tm, MM_ROW_CHUNK))]

        def part(rows):
            return dot(a_ref[:, rows] if mode == "tn" else a_ref[rows, :], b_ref[...])

        def finish(r, rows):
            exs = [e[...] if kind == "row" else e[rows, :] for (kind, _), e in zip(extras, ex_refs)]
            vals = epi(r, *exs) if epi is not None else [r]
            for o, v in zip(o_refs, vals):
                o[rows, :] = v.astype(o.dtype)

        if nk == 1:
            for rows in chunks:
                finish(part(rows), rows)
            return
        acc = rest[n_ex + n_dep + n_o]

        @pl.when(k == 0)
        def _():
            for rows in chunks:
                acc[rows, :] = part(rows)

        @pl.when((k > 0) & (k < nk - 1))
        def _():
            for rows in chunks:
                acc[rows, :] += part(rows)

        @pl.when(k == nk - 1)
        def _():
            for rows in chunks:
                finish(acc[rows, :] + part(rows), rows)

    res = pl.pallas_call(
        body,
        grid=(m // tm, n // tn, nk),
        in_specs=[a_spec, b_spec] + ex_specs + [pl.BlockSpec(memory_space=pl.ANY)] * n_dep,
        out_specs=[pl.BlockSpec((tm, tn), lambda i, j, k: (i, j + j_out)) for _ in outs],
        out_shape=[jax.ShapeDtypeStruct((m, out_cols or n), dt) for dt in outs],
        input_output_aliases={} if into is None else {2 + n_ex + n_dep - 1: 0},
        scratch_shapes=[pltpu.VMEM((tm, tn), F32)] if nk > 1 else [],
        name=name,
        compiler_params=_cparams(("parallel", "parallel", "arbitrary")),
    )(a, b, *ex_arrays, *deps)
    return res if len(outs) > 1 else res[0]


def _rows(body, n_rows, tr, ins, outs, name, scratch=()):
    def spec(kind, shape):
        if kind == "blk":
            return pl.BlockSpec((tr,) + tuple(shape[1:]), lambda i: (i,) + (0,) * (len(shape) - 1))
        if kind == "dep":
            return pl.BlockSpec(memory_space=pl.ANY)
        if kind == "str":
            return pl.BlockSpec((shape[0], tr // shape[0], shape[2]), lambda i: (0, i, 0))
        return pl.BlockSpec(tuple(shape), lambda i: (0,) * len(shape))

    return pl.pallas_call(
        body,
        grid=(n_rows // tr,),
        in_specs=[spec(k, a.shape) for k, a in ins],
        out_specs=[spec(k, s) for k, s, _ in outs],
        out_shape=[jax.ShapeDtypeStruct(tuple(s), d) for _, s, d in outs],
        scratch_shapes=list(scratch),
        name=name,
        compiler_params=_cparams(("arbitrary",)),
    )(*[a for _, a in ins])


def _ln_stats(z):
    mu = jnp.mean(z, axis=-1, keepdims=True)
    zc = z - mu
    var = jnp.mean(zc * zc, axis=-1, keepdims=True)
    rstd = lax.rsqrt(var + LN_EPS)
    return zc * rstd, rstd


def _stream_scratch(c):
    return pltpu.VMEM((c // LANES, ROW_TILE, LANES), F32)


def _streams_in(ref3, scr):
    dil, n, c = ref3.shape
    for r in range(dil):
        for j in range(c // LANES):
            scr.at[j][pl.ds(r, n, stride=dil), :] = ref3[r, :, j * LANES:(j + 1) * LANES].astype(F32)
    return jnp.concatenate([scr[j] for j in range(c // LANES)], axis=1)


def _streams_out(val, ref3, scr):
    dil, n, c = ref3.shape
    for j in range(c // LANES):
        scr[j] = val[:, j * LANES:(j + 1) * LANES].astype(F32)
    for r in range(dil):
        for j in range(c // LANES):
            ref3[r, :, j * LANES:(j + 1) * LANES] = scr.at[j][pl.ds(r, n, stride=dil), :].astype(ref3.dtype)


def _mod(x, scale, shift, after, name):
    s, d = x.shape

    def body(x_ref, sc_ref, sh_ref, dep_ref, h_ref):
        h_ref[...] = (x_ref[...] * (1.0 + sc_ref[...]) + sh_ref[...]).astype(h_ref.dtype)

    return _rows(body, s, ROW_TILE, [("blk", x), ("all", scale), ("all", shift), ("dep", after)], [("blk", (s, d), MXU_DTYPE)], name)[0]


def _resid_ln(x, y, gate, g, b, nxt, name, dils=()):
    s, d = x.shape

    def body(x_ref, y_ref, gate_ref, g_ref, b_ref, sc_ref, sh_ref, xn_ref, h_ref, *rest):
        z = ALPHA * x_ref[...] + gate_ref[...] * y_ref[...]
        xhat, _ = _ln_stats(z)
        xn = xhat * g_ref[...] + b_ref[...]
        xn_ref[...] = xn
        h = xn * (1.0 + sc_ref[...]) + sh_ref[...]
        h_ref[...] = h.astype(h_ref.dtype)
        for hs_ref in rest[:len(dils)]:
            _streams_out(h, hs_ref, rest[-1])

    return _rows(body, s, ROW_TILE,
                 [("blk", x), ("blk", y), ("all", gate), ("all", g), ("all", b), ("all", nxt[0]), ("all", nxt[1])],
                 [("blk", (s, d), F32), ("blk", (s, d), MXU_DTYPE)] + [("str", (dil, s // dil, d), MXU_DTYPE) for dil in dils], name,
                 scratch=[_stream_scratch(d)] if dils else [])


def _mod_bwd(dxr, dhs, x, scale, name, after=None):
    s, d = x.shape
    n_dh = len(dhs)
    n_dep = 0 if after is None else 1

    def body(dxr_ref, *rest):
        dh_refs = rest[:n_dh]
        x_ref, sc_ref, dx_ref, red_ref, a_sh, a_sc = rest[n_dh:n_dh + 2] + rest[n_dh + 2 + n_dep:]
        i = pl.program_id(0)

        @pl.when(i == 0)
        def _():
            a_sh[...] = jnp.zeros_like(a_sh)
            a_sc[...] = jnp.zeros_like(a_sc)

        dh = dh_refs[0][...]
        for r in dh_refs[1:]:
            dh = dh + r[...]
        dx_ref[...] = dxr_ref[...] + dh * (1.0 + sc_ref[...])
        a_sh[...] += _fold8(dh)
        a_sc[...] += _fold8(dh * x_ref[...])

        @pl.when(i == pl.num_programs(0) - 1)
        def _():
            red_ref[...] = jnp.zeros_like(red_ref)
            red_ref[0:1, :] = jnp.sum(a_sh[...], axis=0, keepdims=True)
            red_ref[1:2, :] = jnp.sum(a_sc[...], axis=0, keepdims=True)

    return _rows(body, s, ROW_TILE, [("blk", dxr)] + [("blk", h) for h in dhs] + [("blk", x), ("all", scale)] + [("dep", after)] * n_dep,
                 [("blk", (s, d), F32), ("all", (SUBLANES, d), F32)], name,
                 scratch=[pltpu.VMEM((SUBLANES, d), F32)] * 2)


def _last_ln_loss_bwd(x, y, gate, g, b, target, name):
    s, d = x.shape

    def body(x_ref, y_ref, gate_ref, g_ref, b_ref, t_ref, l_ref, dxr_ref, dyy_ref, red_ref, a_l, a_g, a_b, a_gate):
        i = pl.program_id(0)

        @pl.when(i == 0)
        def _():
            for a in (a_l, a_g, a_b, a_gate):
                a[...] = jnp.zeros_like(a)

        yv = y_ref[...]
        z = ALPHA * x_ref[...] + gate_ref[...] * yv
        xhat, rstd = _ln_stats(z)
        e = xhat * g_ref[...] + b_ref[...] - t_ref[...]
        a_l[...] += _fold8(e * e)
        dxo_v = e * (1.0 / d)
        dxh = dxo_v * g_ref[...]
        dz = rstd * (dxh - jnp.mean(dxh, axis=-1, keepdims=True) - xhat * jnp.mean(dxh * xhat, axis=-1, keepdims=True))
        dxr_ref[...] = ALPHA * dz
        dyy_ref[...] = (gate_ref[...] * dz).astype(dyy_ref.dtype)
        a_g[...] += _fold8(dxo_v * xhat)
        a_b[...] += _fold8(dxo_v)
        a_gate[...] += _fold8(dz * yv)

        @pl.when(i == pl.num_programs(0) - 1)
        def _():
            l_ref[...] = jnp.full(l_ref.shape, 0.5 / d, F32) * jnp.sum(a_l[...])
            red_ref[...] = jnp.zeros_like(red_ref)
            red_ref[0:1, :] = jnp.sum(a_g[...], axis=0, keepdims=True)
            red_ref[1:2, :] = jnp.sum(a_b[...], axis=0, keepdims=True)
            red_ref[2:3, :] = jnp.sum(a_gate[...], axis=0, keepdims=True)

    l, dxr, dyy, red = _rows(
        body, s, ROW_TILE, [("blk", x), ("blk", y), ("all", gate), ("all", g), ("all", b), ("blk", target)],
        [("all", (SUBLANES, LANES), F32), ("blk", (s, d), F32), ("blk", (s, d), MXU_DTYPE), ("all", (SUBLANES, d), F32)], name,
        scratch=[pltpu.VMEM((SUBLANES, d), F32)] * 4)
    return l[0, 0], dxr, dyy, red


def _mod_ln_bwd(dxr, dhs, x, scale, x_in, y, gate, g, name, after=None):
    s, d = x.shape
    n_dh = len(dhs)
    n_dep = 0 if after is None else 1

    def body(dxr_ref, *rest):
        dh_refs = rest[:n_dh]
        x_ref, sc_ref, xin_ref, y_ref, gate_ref, g_ref = rest[n_dh:n_dh + 6]
        dxr_out, dyy_ref, red_mod, red_ln, a_sh, a_sc, a_g, a_b, a_gate = rest[n_dh + 6 + n_dep:n_dh + 15 + n_dep]
        i = pl.program_id(0)

        @pl.when(i == 0)
        def _():
            for a in (a_sh, a_sc, a_g, a_b, a_gate):
                a[...] = jnp.zeros_like(a)

        dh = dh_refs[0][...]
        for r in dh_refs[1:]:
            dh = dh + (r[...] if len(r.shape) == 2 else _streams_in(r, rest[-1]))
        xv = x_ref[...]
        dxo_v = dxr_ref[...] + dh * (1.0 + sc_ref[...])
        a_sh[...] += _fold8(dh)
        a_sc[...] += _fold8(dh * xv)
        yv = y_ref[...]
        z = ALPHA * xin_ref[...] + gate_ref[...] * yv
        xhat, rstd = _ln_stats(z)
        dxh = dxo_v * g_ref[...]
        dz = rstd * (dxh - jnp.mean(dxh, axis=-1, keepdims=True) - xhat * jnp.mean(dxh * xhat, axis=-1, keepdims=True))
        dxr_out[...] = ALPHA * dz
        dyy_ref[...] = (gate_ref[...] * dz).astype(dyy_ref.dtype)
        a_g[...] += _fold8(dxo_v * xhat)
        a_b[...] += _fold8(dxo_v)
        a_gate[...] += _fold8(dz * yv)

        @pl.when(i == pl.num_programs(0) - 1)
        def _():
            red_mod[...] = jnp.zeros_like(red_mod)
            red_mod[0:1, :] = jnp.sum(a_sh[...], axis=0, keepdims=True)
            red_mod[1:2, :] = jnp.sum(a_sc[...], axis=0, keepdims=True)
            red_ln[...] = jnp.zeros_like(red_ln)
            red_ln[0:1, :] = jnp.sum(a_g[...], axis=0, keepdims=True)
            red_ln[1:2, :] = jnp.sum(a_b[...], axis=0, keepdims=True)
            red_ln[2:3, :] = jnp.sum(a_gate[...], axis=0, keepdims=True)

    ins = ([("blk", dxr)] + [("blk" if h.ndim == 2 else "str", h) for h in dhs]
           + [("blk", x), ("all", scale), ("blk", x_in), ("blk", y), ("all", gate), ("all", g)] + [("dep", after)] * n_dep)
    return _rows(body, s, ROW_TILE, ins,
                 [("blk", (s, d), F32), ("blk", (s, d), MXU_DTYPE), ("all", (SUBLANES, d), F32), ("all", (SUBLANES, d), F32)], name,
                 scratch=[pltpu.VMEM((SUBLANES, d), F32)] * 5 + [_stream_scratch(d)] * any(h.ndim == 3 for h in dhs))


def _left_half(shape):
    return lax.broadcasted_iota(jnp.int32, shape, 1) < (LANES // 2)


CHUNKS_PER_STEP = 2


def _chunks_of_step():
    return [slice(i * CHUNK, (i + 1) * CHUNK) for i in range(CHUNKS_PER_STEP)]


def _spatial_z(vn, wc_ref, bias_ref, j):
    vb = vn[:, j * LANES:(j + 1) * LANES]
    z0 = _dot_nn(wc_ref[2 * j], vb)
    z1 = _dot_nn(wc_ref[2 * j + 1], vb)
    return jnp.where(_left_half(z0.shape), z0, z1) + bias_ref[:, j * LANES:(j + 1) * LANES]


def _spatial_fwd(uvpre, vn_g, vn_b, wc, bias_full, name):
    s, d2 = uvpre.shape
    d = d2 // 2

    def body(uv_ref, g_ref, b_ref, wc_ref, bias_ref, out_ref):
        for rows in _chunks_of_step():
            u = _gelu(uv_ref[rows, :d])
            v = _gelu(uv_ref[rows, d:])
            vh, _ = _ln_stats(v)
            vn = vh * g_ref[...] + b_ref[...]
            for j in range(d // LANES):
                z = _spatial_z(vn, wc_ref, bias_ref, j)
                out_ref[rows, j * LANES:(j + 1) * LANES] = (u[:, j * LANES:(j + 1) * LANES] * z).astype(out_ref.dtype)

    return _rows(body, s, CHUNKS_PER_STEP * CHUNK, [("blk", uvpre), ("all", vn_g), ("all", vn_b), ("all", wc), ("all", bias_full)],
                 [("blk", (s, d), MXU_DTYPE)], name)[0]


def _spatial_bwd(uvpre, dgated, vn_g, vn_b, wc, wct, bias_full, name):
    s, d2 = uvpre.shape
    d = d2 // 2

    def body(uv_ref, dg_ref, g_ref, b_ref, wc_ref, wct_ref, bias_ref,
             duv_ref, dws_ref, dbias_ref, dbin_ref, dvg_ref, dvb_ref, dvn_buf, a_bin, a_vg, a_vb):
        i = pl.program_id(0)

        @pl.when(i == 0)
        def _():
            dws_ref[...] = jnp.zeros_like(dws_ref)
            dbias_ref[...] = jnp.zeros_like(dbias_ref)
            a_bin[...] = jnp.zeros_like(a_bin)
            a_vg[...] = jnp.zeros_like(a_vg)
            a_vb[...] = jnp.zeros_like(a_vb)

        for rows in _chunks_of_step():
            up = uv_ref[rows, :d]
            vp = uv_ref[rows, d:]
            u = _gelu(up)
            v = _gelu(vp)
            vh, rstd = _ln_stats(v)
            vn = vh * g_ref[...] + b_ref[...]
            dg = dg_ref[rows, :]
            dzz = dg * u
            dbias_ref[...] += dzz
            for j in range(d // LANES):
                cols = slice(j * LANES, (j + 1) * LANES)
                z = _spatial_z(vn, wc_ref, bias_ref, j)
                dup = dg[:, cols] * z * _gelu_grad(up[:, cols])
                duv_ref[rows, cols] = dup.astype(duv_ref.dtype)
                a_bin[:, cols] += _fold8(dup)
                dzb = dzz[:, cols]
                left = _left_half(dzb.shape)
                dvn_buf[:, cols] = jnp.where(left, _dot_nn(wct_ref[2 * j], dzb), _dot_nn(wct_ref[2 * j + 1], dzb))
                vb = vn[:, cols]
                dws_ref[2 * j] += _dot_nt(jnp.where(left, dzb, 0.0), vb)
                dws_ref[2 * j + 1] += _dot_nt(jnp.where(left, 0.0, dzb), vb)
            dvn = dvn_buf[...]
            a_vg[...] += _fold8(dvn * vh)
            a_vb[...] += _fold8(dvn)
            dvh = dvn * g_ref[...]
            dv = rstd * (dvh - jnp.mean(dvh, axis=-1, keepdims=True) - vh * jnp.mean(dvh * vh, axis=-1, keepdims=True))
            dvp = dv * _gelu_grad(vp)
            duv_ref[rows, d:] = dvp.astype(duv_ref.dtype)
            a_bin[:, d:] += _fold8(dvp)

        @pl.when(i == pl.num_programs(0) - 1)
        def _():
            dbin_ref[...] = jnp.sum(a_bin[...], axis=0, keepdims=True)
            dvg_ref[...] = jnp.sum(a_vg[...], axis=0, keepdims=True)
            dvb_ref[...] = jnp.sum(a_vb[...], axis=0, keepdims=True)

    return _rows(body, s, CHUNKS_PER_STEP * CHUNK,
                 [("blk", uvpre), ("blk", dgated), ("all", vn_g), ("all", vn_b), ("all", wc), ("all", wct), ("all", bias_full)],
                 [("blk", (s, d2), MXU_DTYPE), ("all", (A_GROUPS, CHUNK, CHUNK), F32), ("all", (CHUNK, d), F32),
                  ("all", (1, d2), F32), ("all", (1, d), F32), ("all", (1, d), F32)], name,
                 scratch=[pltpu.VMEM((CHUNK, d), F32), pltpu.VMEM((SUBLANES, d2), F32),
                          pltpu.VMEM((SUBLANES, d), F32), pltpu.VMEM((SUBLANES, d), F32)])


def _head_mask(v, h):
    lane = lax.broadcasted_iota(jnp.int32, v.shape, 1)
    return jnp.where((lane >= h * HEAD_DIM) & (lane < (h + 1) * HEAD_DIM), v, jnp.zeros_like(v))


def _att_bias(slopes, dil):
    qi = lax.broadcasted_iota(jnp.int32, (SPAN, SPAN), 0)
    ki = lax.broadcasted_iota(jnp.int32, (SPAN, SPAN), 1)
    sl = slopes[:, None, None]
    cur = jnp.where(ki <= qi, -sl * (float(dil) * (qi - ki).astype(F32)), NEG)
    prev = jnp.where(ki >= qi, -sl * (float(dil) * (SPAN + qi - ki).astype(F32)), NEG)
    absent = jnp.full_like(prev, NEG)
    pairs = slopes.shape[0] // 2

    def fwd(pv):
        return jnp.concatenate([cur, pv], axis=2).reshape(pairs, 2 * SPAN, 2 * SPAN)

    def bwd(pv):
        return jnp.concatenate([cur.reshape(pairs, 2 * SPAN, SPAN), pv.reshape(pairs, 2 * SPAN, SPAN)], axis=1)

    return jnp.stack([fwd(absent), fwd(prev)]), jnp.stack([bwd(absent), bwd(prev)])


ATT_PAIR = 2


def _att_specs(s, d, dil, kinds):
    nb = s // (dil * SPAN)
    assert nb % ATT_PAIR == 0

    def spec(part, which):
        if which == "pair":
            return pl.BlockSpec((ATT_PAIR * SPAN, d), lambda b: (b, part))
        if which == "prev":
            return pl.BlockSpec((SPAN, d), lambda b: (jnp.where((ATT_PAIR * b) % nb == 0, ATT_PAIR * b, ATT_PAIR * b - 1), part))
        return pl.BlockSpec((SPAN, d), lambda b: (jnp.where((ATT_PAIR * b + 1) % nb == nb - 1, ATT_PAIR * b + 1, ATT_PAIR * b + 2), part))

    return [spec(part, which) for part, which in kinds]


def _head_col(v, head):
    return v[:, head:head + 1]


def _expand_heads(w, j):
    shape = (w.shape[0], LANES)
    return jnp.where(_left_half(shape), jnp.broadcast_to(_head_col(w, 2 * j), shape), jnp.broadcast_to(_head_col(w, 2 * j + 1), shape))


def _attn_fwd(qkv, slopes, dil, name):
    s, d3 = qkv.shape
    d = d3 // 3
    nb = s // (dil * SPAN)
    table, _ = _att_bias(slopes, dil)

    def body(q_ref, k_ref, kp_ref, v_ref, vp_ref, tb_ref, o_ref, l_ref):
        b = pl.program_id(0)
        left = _left_half((SPAN, LANES))
        lane = lax.broadcasted_iota(jnp.int32, (SPAN, LANES), 1)
        for sub in range(ATT_PAIR):
            rows, before = slice(sub * SPAN, (sub + 1) * SPAN), slice((sub - 1) * SPAN, sub * SPAN)
            variant = jnp.where((ATT_PAIR * b) % nb == 0, 0, 1) if sub == 0 else 1
            lses = jnp.zeros((SPAN, LANES), F32)
            for hp in range(d // LANES):
                cols = slice(hp * LANES, (hp + 1) * LANES)
                q = q_ref[rows, cols]
                q2 = jnp.concatenate([_head_mask(q, 0), _head_mask(q, 1)], axis=0) * ATT_SCALE
                k2 = jnp.concatenate([k_ref[rows, cols], kp_ref[:, cols] if sub == 0 else k_ref[before, cols]], axis=0)
                v2 = jnp.concatenate([v_ref[rows, cols], vp_ref[:, cols] if sub == 0 else v_ref[before, cols]], axis=0)
                sc = _dot_nt(q2, k2) + tb_ref[variant, hp]
                m = jnp.max(sc, axis=-1, keepdims=True)
                p = jnp.exp(sc - m)
                l = jnp.sum(p, axis=-1, keepdims=True)
                r = _dot_nn(p, v2) * (1.0 / l)
                lse = m + jnp.log(l)
                o_ref[rows, cols] = jnp.where(left, r[:SPAN], r[SPAN:])
                lses = jnp.where(lane == 2 * hp, lse[:SPAN], jnp.where(lane == 2 * hp + 1, lse[SPAN:], lses))
            l_ref[rows, :] = lses

    specs = _att_specs(s, d, dil, [(0, "pair"), (1, "pair"), (1, "prev"), (2, "pair"), (2, "prev")])
    return pl.pallas_call(
        body,
        grid=(s // (ATT_PAIR * SPAN),),
        in_specs=specs + [pl.BlockSpec(table.shape, lambda b: (0, 0, 0, 0))],
        out_specs=[pl.BlockSpec((ATT_PAIR * SPAN, d), lambda b: (b, 0)), pl.BlockSpec((ATT_PAIR * SPAN, LANES), lambda b: (b, 0))],
        out_shape=[jax.ShapeDtypeStruct((s, d), F32), jax.ShapeDtypeStruct((s, LANES), F32)],
        name=name,
        compiler_params=_cparams(("parallel",)),
    )(qkv, qkv, qkv, qkv, qkv, table)


def _attn_bwd(qkv, do, lse, dd, slopes, dil, name):
    s, d3 = qkv.shape
    d = d3 // 3
    nb = s // (dil * SPAN)
    _, table = _att_bias(slopes, dil)

    def heads_stacked(cur, nxt):
        return jnp.concatenate([_head_mask(cur, 0), _head_mask(cur, 1), _head_mask(nxt, 0), _head_mask(nxt, 1)], axis=0)

    def cols_stacked(cur, nxt, hp):
        return jnp.concatenate([jnp.broadcast_to(_head_col(a, 2 * hp + h), (SPAN, LANES)) for a in (cur, nxt) for h in range(2)], axis=0)

    def body(k_ref, v_ref, q_ref, qn_ref, do_ref, don_ref, l_ref, ln_ref, dd_ref, ddn_ref, tb_ref, out_ref, carry):
        b = pl.program_id(0)

        @pl.when(b == 0)
        def _():
            carry[...] = jnp.zeros_like(carry)

        left = _left_half((SPAN, LANES))
        for sub in range(ATT_PAIR):
            rows, after = slice(sub * SPAN, (sub + 1) * SPAN), slice((sub + 1) * SPAN, (sub + 2) * SPAN)
            last = sub == ATT_PAIR - 1
            variant = jnp.where((ATT_PAIR * b + sub) % nb == nb - 1, 0, 1) if last else 1
            lse_c, dd_c = l_ref[rows, :], dd_ref[rows, :]
            lse_n, dd_n = (ln_ref[...], ddn_ref[...]) if last else (l_ref[after, :], dd_ref[after, :])
            for hp in range(d // LANES):
                cols = slice(hp * LANES, (hp + 1) * LANES)
                k, v = k_ref[rows, cols], v_ref[rows, cols]
                q4 = heads_stacked(q_ref[rows, cols], qn_ref[:, cols] if last else q_ref[after, cols])
                do4 = heads_stacked(do_ref[rows, cols], don_ref[:, cols] if last else do_ref[after, cols])
                sc = _dot_nt(q4 * ATT_SCALE, k) + tb_ref[variant, hp]
                p = jnp.exp(sc - cols_stacked(lse_c, lse_n, hp))
                ds = p * (_dot_nt(do4, v) - cols_stacked(dd_c, dd_n, hp))
                dq4 = _dot_nn(ds, k)
                dq_cur = jnp.where(left, dq4[:SPAN], dq4[SPAN:2 * SPAN]) + carry[:, cols]
                carry[:, cols] = jnp.where(left, dq4[2 * SPAN:3 * SPAN], dq4[3 * SPAN:])
                out_ref[rows, cols] = (dq_cur * ATT_SCALE).astype(out_ref.dtype)
                out_ref[rows, d + hp * LANES:d + (hp + 1) * LANES] = (_dot_tn(ds, q4) * ATT_SCALE).astype(out_ref.dtype)
                out_ref[rows, 2 * d + hp * LANES:2 * d + (hp + 1) * LANES] = _dot_tn(p, do4).astype(out_ref.dtype)

    qkv_specs = _att_specs(s, d, dil, [(1, "pair"), (2, "pair"), (0, "pair"), (0, "next")])
    wide = _att_specs(s, d, dil, [(0, "pair"), (0, "next")])
    heads = _att_specs(s, LANES, dil, [(0, "pair"), (0, "next")])
    return pl.pallas_call(
        body,
        grid=(s // (ATT_PAIR * SPAN),),
        in_specs=qkv_specs + wide + heads + heads + [pl.BlockSpec(table.shape, lambda b: (0, 0, 0, 0))],
        out_specs=pl.BlockSpec((ATT_PAIR * SPAN, d3), lambda b: (b, 0)),
        out_shape=jax.ShapeDtypeStruct((s, d3), MXU_DTYPE),
        scratch_shapes=[pltpu.VMEM((SPAN, d), F32)],
        name=name,
        compiler_params=_cparams(("arbitrary",)),
    )(qkv, qkv, qkv, qkv, do, do, lse, lse, dd, dd, table)


def _mix_weights(l_refs):
    ls = [r[...] for r in l_refs]
    m = functools.reduce(jnp.maximum, ls)
    es = [jnp.exp(l - m) for l in ls]
    tot = functools.reduce(lambda a, c: a + c, es)
    return [e / tot for e in es]


def _combine_fwd(os_, ls_, name):
    s, d = ls_[0].shape[0], os_[0].shape[-1]
    n = len(os_)
    n_str = sum(o.ndim == 3 for o in os_)

    def body(*refs):
        o_refs, l_refs, out_ref, scrs = refs[:n], refs[n:2 * n], refs[2 * n], list(refs[2 * n + 1:])
        ws = _mix_weights(l_refs)
        os_v = [o if len(o.shape) == 2 else _streams_in(o, scrs.pop()) for o in o_refs]
        for j in range(d // LANES):
            cols = slice(j * LANES, (j + 1) * LANES)
            acc = _expand_heads(ws[0], j) * os_v[0][:, cols]
            for w, o in zip(ws[1:], os_v[1:]):
                acc = acc + _expand_heads(w, j) * o[:, cols]
            out_ref[:, cols] = acc

    return _rows(body, s, ROW_TILE, [("blk" if a.ndim == 2 else "str", a) for a in os_] + [("blk", a) for a in ls_],
                 [("blk", (s, d), F32)], name, scratch=[_stream_scratch(d)] * n_str)[0]


def _combine_bwd(do, o, ls_, dils, name):
    s, d = o.shape
    n = len(ls_)
    sel = (lax.broadcasted_iota(jnp.int32, (d, LANES), 0) // HEAD_DIM == lax.broadcasted_iota(jnp.int32, (d, LANES), 1)).astype(F32)

    def body(do_ref, o_ref, *rest):
        l_refs, sel_ref, outs = rest[:n], rest[n], rest[n + 1:n + 1 + 2 * n]
        ws = _mix_weights(l_refs)
        dov = do_ref[...]
        r = jnp.dot(dov * o_ref[...], sel_ref[...], precision=lax.Precision.HIGHEST, preferred_element_type=F32)
        for g in range(n):
            outs[2 * g + 1][...] = ws[g] * r
            parts = [_expand_heads(ws[g], j) * dov[:, j * LANES:(j + 1) * LANES] for j in range(d // LANES)]
            if dils[g] == 1:
                for j, part in enumerate(parts):
                    outs[2 * g][:, j * LANES:(j + 1) * LANES] = part.astype(outs[2 * g].dtype)
            else:
                _streams_out(jnp.concatenate(parts, axis=1), outs[2 * g], rest[-1])

    outs = []
    for dil in dils:
        outs += [("blk", (s, d), MXU_DTYPE) if dil == 1 else ("str", (dil, s // dil, d), MXU_DTYPE), ("blk", (s, LANES), F32)]
    res = _rows(body, s, ROW_TILE, [("blk", do), ("blk", o)] + [("blk", l) for l in ls_] + [("all", sel)], outs, name,
                scratch=[_stream_scratch(d)])
    return [(res[2 * g], res[2 * g + 1]) for g in range(n)]


def _ada_fwd(c_all, w, b, name):
    nsub, d, cs = w.shape

    def body(c_ref, w_ref, b_ref, o_ref):
        cv = c_ref[...]
        sc = cv * (1.0 / (1.0 + jnp.exp(-cv)))
        o_ref[...] = _dot_nn(sc, w_ref[...]) + b_ref[...]

    return pl.pallas_call(
        body,
        grid=(nsub,),
        in_specs=[pl.BlockSpec(c_all.shape, lambda i: (0, 0)), pl.BlockSpec((None, d, cs), lambda i: (i, 0, 0)),
                  pl.BlockSpec((None, 1, cs), lambda i: (i, 0, 0))],
        out_specs=pl.BlockSpec((None, N_DEV, cs), lambda i: (i, 0, 0)),
        out_shape=jax.ShapeDtypeStruct((nsub, N_DEV, cs), F32),
        name=name,
        compiler_params=_cparams(("parallel",)),
    )(c_all, w, b)


def _ada_bwd(c_all_t, dm, name):
    d, nb = c_all_t.shape
    nsub, _, cs = dm.shape

    def body(c_ref, dm_ref, o_ref):
        cv = c_ref[...]
        sc = cv * (1.0 / (1.0 + jnp.exp(-cv)))
        acc = sc[:, 0:1] * dm_ref[0:1, :]
        for bi in range(1, nb):
            acc = acc + sc[:, bi:bi + 1] * dm_ref[bi:bi + 1, :]
        o_ref[...] = acc

    return pl.pallas_call(
        body,
        grid=(nsub,),
        in_specs=[pl.BlockSpec(c_all_t.shape, lambda i: (0, 0)), pl.BlockSpec((None, nb, cs), lambda i: (i, 0, 0))],
        out_specs=pl.BlockSpec((None, d, cs), lambda i: (i, 0, 0)),
        out_shape=jax.ShapeDtypeStruct((nsub, d, cs), F32),
        name=name,
        compiler_params=_cparams(("parallel",)),
    )(c_all_t, dm)


def _row_tile(r, row_elems, block_elems=256 * 1024):
    t = 2 * SUBLANES
    if r % t:
        return r
    while t * 2 * row_elems <= block_elems and r % (t * 2) == 0:
        t *= 2
    return t


def _adamw(w, g, m, v, name):
    shape = w.shape
    c = shape[-1]
    r = w.size // c
    tr = _row_tile(r, c, 512 * 1024)
    w2, g2, m2, v2 = [a.reshape(r, c) for a in (w, g, m, v)]
    bc1 = 1.0 - ADAM_B1 ** ADAM_STEP
    bc2 = 1.0 - ADAM_B2 ** ADAM_STEP

    def body(w_ref, g_ref, m_ref, v_ref, d_ref, nm_ref, nv_ref):
        gv = g_ref[...]
        nm = ADAM_B1 * m_ref[...] + (1.0 - ADAM_B1) * gv
        nv = ADAM_B2 * v_ref[...] + (1.0 - ADAM_B2) * (gv * gv)
        d_ref[...] = -ADAM_LR * ((nm / bc1) / (jnp.sqrt(nv / bc2) + ADAM_EPS) + ADAM_WD * w_ref[...])
        nm_ref[...] = nm
        nv_ref[...] = nv

    res = _rows(body, r, tr, [("blk", a) for a in (w2, g2, m2, v2)], [("blk", (r, c), F32)] * 3, name)
    return [a.reshape(shape) for a in res]


def _sum_slots(buf, name):
    n, r, c = buf.shape
    tr = _row_tile(r, n * c, 2 * 1024 * 1024)

    def body(b_ref, o_ref):
        acc = b_ref[0].astype(F32)
        for k in range(1, n):
            acc = acc + b_ref[k].astype(F32)
        o_ref[...] = acc

    return pl.pallas_call(
        body,
        grid=(r // tr,),
        in_specs=[pl.BlockSpec((n, tr, c), lambda i: (0, i, 0))],
        out_specs=pl.BlockSpec((tr, c), lambda i: (i, 0)),
        out_shape=jax.ShapeDtypeStruct((r, c), F32),
        name=name,
        compiler_params=_cparams(("parallel",)),
    )(buf)


def _me():
    return lax.axis_index("x"), lax.axis_index("y"), lax.axis_index("c")


def _all_gather_small(blk, name, after=()):
    m_per, n = blk.shape

    def body(x_ref, *rest):
        out_ref, send_sems, recv_sems, local_sem = rest[len(after):]
        x, y, c = _me()
        me, sibling = (x, y, c), (x, y, 1 - c)
        chips = [(1 - x, y), (x, 1 - y), (1 - x, 1 - y)]

        def rows(px, py, pc):
            return out_ref.at[pl.ds((4 * px + 2 * py + pc) * m_per, m_per), :]

        def copy(k, block, to, src=None):
            return pltpu.make_async_remote_copy(
                src_ref=rows(*block) if src is None else src, dst_ref=rows(*block),
                send_sem=send_sems.at[k], recv_sem=recv_sems.at[k], device_id=to, device_id_type=MESH)

        mine = pltpu.make_async_copy(x_ref, rows(*me), local_sem)
        mine.start()
        first = [copy(0, me, sibling, src=x_ref)]
        first += [copy(1 + j, me, (*chip, c), src=x_ref) for j, chip in enumerate(chips)]
        for cp in first:
            cp.start()
        passed = [copy(4 + j, (*chip, c), sibling) for j, chip in enumerate(chips)]
        for j, chip in enumerate(chips):
            copy(1 + j, (*chip, c), me).wait_recv()
            passed[j].start()
        copy(0, sibling, me).wait_recv()
        for j, chip in enumerate(chips):
            copy(4 + j, (*chip, 1 - c), me).wait_recv()
        for cp in first + passed:
            cp.wait_send()
        mine.wait()

    return pl.pallas_call(
        body,
        out_shape=jax.ShapeDtypeStruct((N_DEV * m_per, n), blk.dtype),
        in_specs=[pl.BlockSpec(memory_space=pltpu.VMEM)] + [pl.BlockSpec(memory_space=pl.ANY)] * len(after),
        out_specs=pl.BlockSpec(memory_space=pltpu.VMEM),
        scratch_shapes=[pltpu.SemaphoreType.DMA((7,)), pltpu.SemaphoreType.DMA((7,)), pltpu.SemaphoreType.DMA],
        name=name,
        compiler_params=pltpu.CompilerParams(vmem_limit_bytes=VMEM_LIMIT),
    )(blk, *after)


_HBM = pl.BlockSpec(memory_space=pltpu.HBM)
_SEM = pl.BlockSpec(memory_space=pltpu.SEMAPHORE)
_EFFECT = pltpu.SideEffectType.DATAFLOW_SIDE_EFFECTING


def _other_chips(x, y):
    return [(1 - x, y), (x, 1 - y), (1 - x, 1 - y)]


def _gather_copy(w, j, src_ref, land_ref, send_sems, recv_sems, halved=False):
    x, y, c = _me()
    if halved:
        half = src_ref.shape[0] // 2
        src_ref = src_ref.at[pl.ds(c * half, half), :]
    return pltpu.make_async_remote_copy(
        src_ref=src_ref, dst_ref=land_ref.at[2 * x + y], send_sem=send_sems.at[3 * w + j], recv_sem=recv_sems.at[3 * w + j],
        device_id=(*_other_chips(x, y)[j], c), device_id_type=MESH)


def _gather_start(shards, halved, after, name):
    n = len(shards)
    lands = [lax.empty((N_CHIPS, s.shape[0] // 2 if w in halved else s.shape[0], s.shape[1]), s.dtype) for w, s in enumerate(shards)]

    def body(*refs):
        in_refs, land_refs = refs[:n], refs[n:2 * n]
        send_sems, recv_sems = refs[2 * n + 1], refs[2 * n + 2]
        token = refs[-1]
        for w in range(n):
            for j in range(3):
                _gather_copy(w, j, in_refs[w], land_refs[w], send_sems, recv_sems, w in halved).start()
        token[...] = jnp.zeros_like(token)

    res = pl.pallas_call(
        body,
        out_shape=(pltpu.SemaphoreType.DMA((3 * n,)), pltpu.SemaphoreType.DMA((3 * n,)),
                   *[pltpu.HBM(s.shape, s.dtype) for s in shards], *[pltpu.HBM(l.shape, l.dtype) for l in lands],
                   jax.ShapeDtypeStruct((SUBLANES, LANES), F32)),
        in_specs=[_HBM] * (2 * n) + [pl.BlockSpec(memory_space=pl.ANY)],
        out_specs=(_SEM, _SEM, *[_HBM] * (2 * n), pl.BlockSpec(memory_space=pltpu.VMEM)),
        input_output_aliases={i: 2 + i for i in range(2 * n)},
        name=name,
        compiler_params=pltpu.CompilerParams(has_side_effects=_EFFECT),
    )(*[pltpu.with_memory_space_constraint(a, pltpu.HBM) for a in list(shards) + lands], after)
    return res[0], res[1], res[2:2 + n], res[2 + n:2 + 2 * n], res[-1]


def _gather_wait(w, shard, land, send_sems, recv_sems, after, name, halved=False):
    def body(s_ref, land_ref, send_sems, recv_sems, after_ref, s_out, land_out, stage):
        x, y, _ = _me()
        if not halved:
            pltpu.sync_copy(s_ref, stage)
            pltpu.sync_copy(stage, land_out.at[2 * x + y])
        for j in range(3):
            cp = _gather_copy(w, j, s_ref, land_ref, send_sems, recv_sems, halved)
            cp.wait_send()
            cp.wait_recv()

    return pl.pallas_call(
        body,
        out_shape=(pltpu.HBM(shard.shape, shard.dtype), pltpu.HBM(land.shape, land.dtype)),
        in_specs=(_HBM, _HBM, _SEM, _SEM, pl.BlockSpec(memory_space=pl.ANY)),
        out_specs=(_HBM, _HBM),
        input_output_aliases={0: 0, 1: 1},
        scratch_shapes=[pltpu.VMEM((SUBLANES, LANES) if halved else shard.shape, shard.dtype)],
        name=name,
        compiler_params=pltpu.CompilerParams(has_side_effects=_EFFECT, vmem_limit_bytes=VMEM_LIMIT),
    )(shard, land, send_sems, recv_sems, after)


def _assemble_halves(shard, land, name):
    half = land.shape[1]

    def body(s_ref, land_ref, out_ref, send_sems, recv_sems, local_sems):
        x, y, c = _me()
        own = pltpu.make_async_copy(s_ref, out_ref.at[2 * x + y], local_sems.at[3])
        own.start()
        cps = []
        for j, (ox, oy) in enumerate(_other_chips(x, y)):
            qj = 2 * ox + oy
            mine = out_ref.at[qj, pl.ds(c * half, half), :]
            lc = pltpu.make_async_copy(land_ref.at[qj], mine, local_sems.at[j])
            lc.start()
            rc = pltpu.make_async_remote_copy(
                src_ref=land_ref.at[qj], dst_ref=mine, send_sem=send_sems.at[j], recv_sem=recv_sems.at[j],
                device_id=(x, y, 1 - c), device_id_type=MESH)
            rc.start()
            cps.append((lc, rc))
        for lc, rc in cps:
            rc.wait_recv()
        for lc, rc in cps:
            rc.wait_send()
            lc.wait()
        own.wait()

    vmem = pl.BlockSpec(memory_space=pltpu.VMEM)
    return pl.pallas_call(
        body,
        out_shape=jax.ShapeDtypeStruct((N_CHIPS,) + shard.shape, shard.dtype),
        in_specs=[vmem, vmem],
        out_specs=vmem,
        scratch_shapes=[pltpu.SemaphoreType.DMA((3,)), pltpu.SemaphoreType.DMA((3,)), pltpu.SemaphoreType.DMA((4,))],
        name=name,
        compiler_params=pltpu.CompilerParams(vmem_limit_bytes=VMEM_LIMIT),
    )(shard, land)


def _piece_shape(shape, kind):
    k, nn = shape
    if kind == "all":
        return (k, nn)
    return (k // 2, nn // N_CHIPS) if kind == "col" else (k // N_CHIPS // 2, nn)


def _piece_of(g_ref, kind, tq, tc):
    pr, pc = _piece_shape(g_ref.shape, kind)
    if kind == "all":
        return g_ref
    if kind == "col":
        return g_ref.at[pl.ds(tc * pr, pr), pl.ds(tq * pc, pc)]
    return g_ref.at[pl.ds((2 * tq + tc) * pr, pr), :]


def _scatter_copy(w, r, kind, g_ref, land_ref, send_sems, recv_sems):
    x, y, c = _me()
    tx, ty, tc = (x + ((r >> 2) & 1)) % 2, (y + ((r >> 1) & 1)) % 2, (c + (r & 1)) % 2
    return pltpu.make_async_remote_copy(
        src_ref=_piece_of(g_ref, kind, 2 * tx + ty, tc), dst_ref=land_ref.at[4 * x + 2 * y + c],
        send_sem=send_sems.at[N_DEV * w + r], recv_sem=recv_sems.at[N_DEV * w + r], device_id=(tx, ty, tc), device_id_type=MESH)


def _scatter_start(gs, kinds, name):
    n = len(gs)
    pieces = [_piece_shape(g.shape, kind) for g, kind in zip(gs, kinds)]
    lands = [lax.empty((N_DEV,) + p, g.dtype) for p, g in zip(pieces, gs)]

    def body(*refs):
        g_refs, land_refs, send_sems, recv_sems = refs[:n], refs[n:2 * n], refs[2 * n], refs[2 * n + 1]
        land_outs, stages = refs[3 * n + 2:4 * n + 2], refs[4 * n + 2:]
        x, y, c = _me()
        for w in range(n):
            for r in range(1, N_DEV):
                _scatter_copy(w, r, kinds[w], g_refs[w], land_refs[w], send_sems, recv_sems).start()
        for w in range(n):
            pltpu.sync_copy(_piece_of(g_refs[w], kinds[w], 2 * x + y, c), stages[w])
            pltpu.sync_copy(stages[w], land_outs[w].at[4 * x + 2 * y + c])

    arrays = list(gs) + lands
    res = pl.pallas_call(
        body,
        out_shape=(pltpu.SemaphoreType.DMA((N_DEV * n,)), pltpu.SemaphoreType.DMA((N_DEV * n,)),
                   *[pltpu.HBM(a.shape, a.dtype) for a in arrays]),
        in_specs=[_HBM] * (2 * n),
        out_specs=(_SEM, _SEM, *[_HBM] * (2 * n)),
        input_output_aliases={i: 2 + i for i in range(2 * n)},
        scratch_shapes=[pltpu.VMEM(p, g.dtype) for p, g in zip(pieces, gs)],
        name=name,
        compiler_params=pltpu.CompilerParams(has_side_effects=_EFFECT, vmem_limit_bytes=VMEM_LIMIT),
    )(*[pltpu.with_memory_space_constraint(a, pltpu.HBM) for a in arrays])
    return res[0], res[1], res[2:2 + n], res[2 + n:]


def _scatter_wait(send_sems, recv_sems, gs, lands, kinds, after, name):
    n = len(gs)

    def body(*refs):
        g_refs, land_refs, send_sems, recv_sems = refs[:n], refs[n:2 * n], refs[2 * n], refs[2 * n + 1]
        for w in range(n):
            for r in range(1, N_DEV):
                cp = _scatter_copy(w, r, kinds[w], g_refs[w], land_refs[w], send_sems, recv_sems)
                cp.wait_send()
                cp.wait_recv()

    arrays = list(gs) + list(lands)
    return pl.pallas_call(
        body,
        out_shape=tuple(pltpu.HBM(a.shape, a.dtype) for a in arrays),
        in_specs=(*[_HBM] * (2 * n), _SEM, _SEM, pl.BlockSpec(memory_space=pl.ANY)),
        out_specs=tuple([_HBM] * (2 * n)),
        input_output_aliases={i: i for i in range(2 * n)},
        name=name,
        compiler_params=pltpu.CompilerParams(has_side_effects=_EFFECT),
    )(*arrays, send_sems, recv_sems, after)[n:]


def _swap_halves(halves, name):
    n = len(halves)

    def body(*refs):
        in_refs, out_refs = refs[:n], refs[n:2 * n]
        send_sems, recv_sems, local_sems = refs[2 * n:]
        x, y, c = _me()
        cps = []
        for w in range(n):
            lc = pltpu.make_async_copy(in_refs[w], out_refs[w].at[c], local_sems.at[w])
            lc.start()
            rc = pltpu.make_async_remote_copy(
                src_ref=in_refs[w], dst_ref=out_refs[w].at[c], send_sem=send_sems.at[w], recv_sem=recv_sems.at[w],
                device_id=(x, y, 1 - c), device_id_type=MESH)
            rc.start()
            cps.append((lc, rc))
        for lc, rc in cps:
            rc.wait_recv()
        for lc, rc in cps:
            rc.wait_send()
            lc.wait()

    vmem = pl.BlockSpec(memory_space=pltpu.VMEM)
    return pl.pallas_call(
        body,
        out_shape=[jax.ShapeDtypeStruct((2,) + h.shape, h.dtype) for h in halves],
        in_specs=[vmem] * n,
        out_specs=[vmem] * n,
        scratch_shapes=[pltpu.SemaphoreType.DMA((n,)), pltpu.SemaphoreType.DMA((n,)), pltpu.SemaphoreType.DMA((n,))],
        name=name,
        compiler_params=pltpu.CompilerParams(vmem_limit_bytes=VMEM_LIMIT),
    )(*halves)


def _to_streams(a, dil):
    if dil == 1:
        return a
    s, c = a.shape
    return a.reshape(s // dil, dil, c).transpose(1, 0, 2).reshape(s, c)


def _from_streams(a, dil):
    if dil == 1:
        return a
    s, c = a.shape
    return a.reshape(dil, s // dil, c).transpose(1, 0, 2).reshape(s, c)


def _mm_tiles(s):
    return min(s, 2048)


def _local_step(x0, target, mvec, ln_g, ln_b, small, fetch, emit, start):
    s, d = x0.shape
    tm = _mm_tiles(s)
    row = lambda v: v.reshape(1, -1)
    shift = [row(mvec[i, :d]) for i in range(4)]
    scale = [row(mvec[i, d:2 * d]) for i in range(4)]
    gate = [row(1.0 + mvec[i, 2 * d:]) for i in range(4)]
    lg = [row(ln_g[i]) for i in range(4)]
    lb = [row(ln_b[i]) for i in range(4)]
    mm = functools.partial(_mm, tm=tm)
    mm_w = functools.partial(_mm, tm=1024, tk=min(s, 2048), mode="tn")

    xs, ys, big = [x0], [], {}
    h0 = _mod(x0, scale[0], shift[0], start, "mod0")
    big["a_w_in"] = fetch("a_w_in", h0)
    uvpre = mm(h0, big["a_w_in"], mode="nn", name="a_in", outs=[F32], tn=512, tk=1024,
               epi=lambda r, bias: [r + bias], extras=[("row", small["a_b_in"])])
    gated = _spatial_fwd(uvpre, small["a_vn_g"], small["a_vn_b"], small["wc"], small["bias_full"], "a_spatial")
    big["a_w_out"] = fetch("a_w_out", gated)
    ys.append(mm(gated, big["a_w_out"], mode="nn", name="a_out", outs=[F32], tn=1024, tk=1024))
    x1, h1 = _resid_ln(xs[0], ys[0], gate[0], lg[0], lb[0], (scale[1], shift[1]), "ln0")
    xs.append(x1)
    relu2 = lambda r: [jnp.square(jnp.maximum(r, 0.0))]
    big["up0"] = fetch("up0", h1)
    r0 = mm(h1, big["up0"], mode="nn", name="up0", outs=[MXU_DTYPE], tn=1024, tk=1024, epi=relu2)
    big["down0"] = fetch("down0", r0)
    ys.append(mm(r0, big["down0"], mode="nn", name="down0", outs=[F32], tm=min(s, 1024), tn=1024, tk=2048))
    dils = [dil for _, dil in B_PATTERNS]
    x2, h2, *h2_streams = _resid_ln(xs[1], ys[1], gate[1], lg[1], lb[1], (scale[2], shift[2]), "ln1", [dil for dil in dils if dil > 1])
    h2_streams = [h2] + [a.reshape(s, d) for a in h2_streams]
    xs.append(x2)
    hg, qkvs, o_g, l_g, l_streams = [], [], [], [], []
    big["b_w_qkv"] = fetch("b_w_qkv", h2)
    for g, (_, dil) in enumerate(B_PATTERNS):
        hp = h2_streams[g]
        qkv = mm(hp, big["b_w_qkv"], mode="nn", name=f"qkv{g}", outs=[MXU_DTYPE], tn=768, tk=1024, b_col0=g * 3 * d, n_out=3 * d)
        og, lgv = _attn_fwd(qkv, small["slopes"], dil, f"attn_fwd{g}")
        hg.append(hp)
        qkvs.append(qkv)
        o_g.append(og if dil == 1 else og.reshape(dil, s // dil, d))
        l_g.append(_from_streams(lgv, dil))
        l_streams.append(lgv)
    o_mix = _combine_fwd(o_g, l_g, "combine")
    big["b_w_out"] = fetch("b_w_out", o_mix)
    ys.append(mm(o_mix, big["b_w_out"], mode="nn", name="b_out", outs=[F32], tn=1024, tk=1024))
    x3, h3 = _resid_ln(xs[2], ys[2], gate[2], lg[2], lb[2], (scale[3], shift[3]), "ln2")
    xs.append(x3)
    big["up1"] = fetch("up1", h3)
    r1 = mm(h3, big["up1"], mode="nn", name="up1", outs=[MXU_DTYPE], tn=1024, tk=1024, epi=relu2)
    big["down1"] = fetch("down1", r1)
    ys.append(mm(r1, big["down1"], mode="nn", name="down1", outs=[F32], tm=min(s, 1024), tn=1024, tk=2048))

    gb, red_ln, red_mod = {}, [None] * 4, [None] * 4

    def mlp_bwd(i, h, r, dyy):
        gb[f"down{i}"] = mm_w(r, dyy, name=f"g_down{i}", outs=[MXU_DTYPE], tn=1024)
        da = mm(dyy, big[f"down{i}"], mode="nt", name=f"d_down{i}", outs=[MXU_DTYPE], tn=1024, tk=1024,
                after=emit(f"down{i}", gb[f"down{i}"]),
                epi=lambda acc, rv: [acc * (2.0 * jnp.sqrt(rv.astype(F32)))], extras=[("full", r)])
        gb[f"up{i}"] = mm_w(h, da, name=f"g_up{i}", outs=[MXU_DTYPE], tn=1024)
        return [mm(da, big[f"up{i}"], mode="nt", name=f"d_up{i}", outs=[F32], tn=1024, tk=1024, after=emit(f"up{i}", gb[f"up{i}"]))]

    def join(sub, dxr, dhs, after=None):
        res = _mod_ln_bwd(dxr, dhs, xs[sub], scale[sub], xs[sub - 1], ys[sub - 1], gate[sub - 1], lg[sub - 1],
                          f"mod_ln_bwd{sub}", after=after)
        red_mod[sub], red_ln[sub - 1] = res[2], res[3]
        return res[0], res[1]

    loss, dxr, dyy, red_ln[3] = _last_ln_loss_bwd(xs[3], ys[3], gate[3], lg[3], lb[3], target, "ln3_loss_bwd")
    dxr, dyy = join(3, dxr, mlp_bwd(1, h3, r1, dyy))
    gb["b_w_out"] = mm_w(o_mix, dyy, name="g_b_out", outs=[MXU_DTYPE], tn=1024, tk=1024)
    do = mm(dyy, big["b_w_out"], mode="nt", name="d_b_out", outs=[F32], tn=1024, tk=1024, after=emit("b_w_out", gb["b_w_out"]))
    parts = _combine_bwd(do, o_mix, l_g, dils, "combine_bwd")
    dhs, gq = [], None
    for g, (_, dil) in enumerate(B_PATTERNS):
        do_g, dd_g = parts[g][0].reshape(s, d), _to_streams(parts[g][1], dil)
        dqkv = _attn_bwd(qkvs[g], do_g, l_streams[g], dd_g, small["slopes"], dil, f"attn_bwd{g}")
        gq = mm_w(hg[g], dqkv, name=f"g_qkv{g}", outs=[MXU_DTYPE], tn=1024, out_col0=g * 3 * d, out_cols=len(B_PATTERNS) * 3 * d, into=gq)
        dh = mm(dqkv, big["b_w_qkv"], mode="nt", name=f"d_qkv{g}", outs=[F32], tn=1024, tk=768, b_col0=g * 3 * d)
        dhs.append(dh if dil == 1 else dh.reshape(dil, s // dil, d))
    gb["b_w_qkv"] = gq
    dxr, dyy = join(2, dxr, dhs, after=emit("b_w_qkv", gb["b_w_qkv"]))
    dxr, dyy = join(1, dxr, mlp_bwd(0, h1, r0, dyy))
    gb["a_w_out"] = mm_w(gated, dyy, name="g_a_out", outs=[MXU_DTYPE], tn=1024)
    dgated = mm(dyy, big["a_w_out"], mode="nt", name="d_a_out", outs=[F32], tn=1024, tk=1024, after=emit("a_w_out", gb["a_w_out"]))
    duv, dws, dbias, dbin, dvg, dvb = _spatial_bwd(uvpre, dgated, small["a_vn_g"], small["a_vn_b"], small["wc"],
                                                   small["wct"], small["bias_full"], "a_spatial_bwd")
    tril = jnp.tril(jnp.ones((CHUNK, CHUNK), bool))
    dws = jnp.where(tril, dws, 0.0).reshape(-1, LANES)
    gb["a_w_in"] = mm_w(h0, duv, name="g_a_in", outs=[MXU_DTYPE], tn=1024, after=emit("a_w_s", dws.astype(MXU_DTYPE)))
    dh = mm(duv, big["a_w_in"], mode="nt", name="d_a_in", outs=[F32], tn=1024, tk=512, after=emit("a_w_in", gb["a_w_in"]))
    dx, red_mod[0] = _mod_bwd(dxr, [dh], xs[0], scale[0], "mod_bwd0")
    dm = [jnp.concatenate([red_mod[i][0], red_mod[i][1], red_ln[i][2]]) for i in range(4)]
    dlg, dlb = [red_ln[i][0] for i in range(4)], [red_ln[i][1] for i in range(4)]

    gsmall = {
        "a_b_in": dbin.reshape(-1), "a_vn_g": dvg.reshape(-1), "a_vn_b": dvb.reshape(-1),
        "a_w_s": dws.reshape(-1),
        "a_b_s": dbias.reshape(CHUNK, A_GROUPS, d // A_GROUPS).sum(-1).T.reshape(-1),
    }
    return loss, dx, gb, jnp.stack(dm), jnp.stack(dlg), jnp.stack(dlb), gsmall


BIG = ("a_w_in", "a_w_out", "up0", "down0", "b_w_qkv", "b_w_out", "up1", "down1")
BIG_KIND = {"a_w_in": "col", "a_w_out": "row", "b_w_qkv": "col", "b_w_out": "row",
            "up0": "col", "up1": "col", "down0": "row", "down1": "row", "a_w_s": "all"}
HALVED = ("a_w_in", "down0", "b_w_qkv")
SCATTER_GROUPS = (("down1", "up1"), ("b_w_out", "b_w_qkv"), ("down0", "up0"), ("a_w_out", "a_w_in"), ("a_w_s",))
SMALL = ("a_b_in", "a_vn_g", "a_vn_b", "a_b_s")


def kernel(x, c, ada_w, ada_b, ln_g, ln_b, a_w_in, a_b_in, a_vn_g, a_vn_b, a_w_s, a_b_s, a_w_out, b_w_qkv, b_w_out, mlp_w_up, mlp_w_down, loss_target, m_ada_w, m_ada_b, m_ln_g, m_ln_b, m_a_w_in, m_a_b_in, m_a_vn_g, m_a_vn_b, m_a_w_s, m_a_b_s, m_a_w_out, m_b_w_qkv, m_b_w_out, m_mlp_w_up, m_mlp_w_down, v_ada_w, v_ada_b, v_ln_g, v_ln_b, v_a_w_in, v_a_b_in, v_a_vn_g, v_a_vn_b, v_a_w_s, v_a_b_s, v_a_w_out, v_b_w_qkv, v_b_w_out, v_mlp_w_up, v_mlp_w_down):
    s, d = x.shape[1], x.shape[2]
    xi, yi, ci = _me()
    q = 2 * xi + yi
    dev = 2 * q + ci
    nsub = 2 * DEPTH
    cs = ada_w.shape[-1]
    ls = ln_g.shape[-1]

    shards = {
        "a_w_in": a_w_in[0], "a_w_out": a_w_out[0], "b_w_qkv": b_w_qkv[0], "b_w_out": b_w_out[0],
        "up0": mlp_w_up[0], "up1": mlp_w_up[1], "down0": mlp_w_down[0], "down1": mlp_w_down[1],
    }
    cast = [shards[k].astype(MXU_DTYPE) for k in BIG]

    pack = jnp.concatenate([c.reshape(-1), ln_g.reshape(-1), ln_b.reshape(-1)]).reshape(-1, LANES)
    got = _all_gather_small(pack, "gather_small", after=cast).reshape(N_DEV, -1)
    c_all = got[:, :d]
    per_chip = got[0::2]
    ln_g_full = per_chip[:, d:d + nsub * ls].reshape(N_CHIPS, nsub, ls).transpose(1, 0, 2).reshape(nsub, d)
    ln_b_full = per_chip[:, d + nsub * ls:].reshape(N_CHIPS, nsub, ls).transpose(1, 0, 2).reshape(nsub, d)
    m_part = _ada_fwd(c_all, ada_w.reshape(nsub, d, cs), ada_b.reshape(nsub, 1, cs), "ada_fwd")
    m_all = _all_gather_small(m_part.reshape(-1, LANES), "gather_mod").reshape(N_DEV, nsub, N_DEV, cs)
    m_mine = lax.dynamic_index_in_dim(m_all[0::2], dev, axis=2, keepdims=False)
    mvec = m_mine.transpose(1, 0, 2).reshape(nsub, 3 * d)

    halved = {BIG.index(k) for k in HALVED}
    send_sems, recv_sems, shard_thru, lands, token = _gather_start(cast, halved, mvec, "gather_start")

    def fetch(k, after):
        w = BIG.index(k)
        shard, gw = _gather_wait(w, shard_thru[w], lands[w], send_sems, recv_sems, after, f"gather_wait_{k}", w in halved)
        if w in halved:
            gw = _assemble_halves(shard, gw, f"assemble_{k}")
        return gw if BIG_KIND[k] == "col" else gw.reshape(1, -1, gw.shape[-1])

    scattering, pending = {}, {}

    def emit(k, g):
        pending[k] = g
        group = next(gr for gr in SCATTER_GROUPS if k in gr)
        if k != group[-1]:
            return None
        scattering[group] = _scatter_start([pending[m] for m in group], [BIG_KIND[m] for m in group], f"scatter_start_{k}")
        return scattering[group][2][0]

    tril = jnp.tril(jnp.ones((CHUNK, CHUNK), bool))
    wc = jnp.where(tril, a_w_s[0], 0.0).astype(MXU_DTYPE)
    heads = jnp.arange(1, B_HEADS + 1, dtype=F32)
    small = {
        "a_b_in": a_b_in, "a_vn_g": a_vn_g, "a_vn_b": a_vn_b,
        "wc": wc, "wct": wc.transpose(0, 2, 1),
        "bias_full": jnp.repeat(a_b_s[0].T, d // A_GROUPS, axis=1),
        "slopes": jnp.exp2(-8.0 * heads / B_HEADS),
    }

    loss_part, grad_x, gb, dm, dlg, dlb, gsmall = _local_step(x[0], loss_target[0], mvec, ln_g_full, ln_b_full, small, fetch, emit, token)
    loss = lax.psum(loss_part, ("x", "y", "c"))

    weights = dict(ada_w=ada_w, ada_b=ada_b, ln_g=ln_g, ln_b=ln_b, a_w_in=a_w_in, a_b_in=a_b_in, a_vn_g=a_vn_g, a_vn_b=a_vn_b,
                   a_w_s=a_w_s, a_b_s=a_b_s, a_w_out=a_w_out, b_w_qkv=b_w_qkv, b_w_out=b_w_out, mlp_w_up=mlp_w_up, mlp_w_down=mlp_w_down)
    ms = dict(ada_w=m_ada_w, ada_b=m_ada_b, ln_g=m_ln_g, ln_b=m_ln_b, a_w_in=m_a_w_in, a_b_in=m_a_b_in, a_vn_g=m_a_vn_g, a_vn_b=m_a_vn_b,
              a_w_s=m_a_w_s, a_b_s=m_a_b_s, a_w_out=m_a_w_out, b_w_qkv=m_b_w_qkv, b_w_out=m_b_w_out, mlp_w_up=m_mlp_w_up, mlp_w_down=m_mlp_w_down)
    vs = dict(ada_w=v_ada_w, ada_b=v_ada_b, ln_g=v_ln_g, ln_b=v_ln_b, a_w_in=v_a_w_in, a_b_in=v_a_b_in, a_vn_g=v_a_vn_g, a_vn_b=v_a_vn_b,
              a_w_s=v_a_w_s, a_b_s=v_a_b_s, a_w_out=v_a_w_out, b_w_qkv=v_b_w_qkv, b_w_out=v_b_w_out, mlp_w_up=v_mlp_w_up, mlp_w_down=v_mlp_w_down)
    grads, updates = {}, {}

    def update(k):
        updates[k] = _adamw(weights[k], grads[k], ms[k], vs[k], f"adamw_{k}")
        return updates[k][0]

    gfull = {}

    def big_group(group, after):
        bufs = []
        for pair in (group[:2], group[2:]):
            bufs += _scatter_wait(*scattering[pair], [BIG_KIND[m] for m in pair], after, f"scatter_wait_{pair[-1]}")
        halves = [_sum_slots(b, f"sum_{k}") for k, b in zip(group, bufs)]
        fulls = _swap_halves(halves, f"swap_halves_{group[0]}")
        gfull.update({k: f.reshape(-1, f.shape[-1]) for k, f in zip(group, fulls)})

    big_group(SCATTER_GROUPS[0] + SCATTER_GROUPS[1], grad_x)
    grads["b_w_qkv"], grads["b_w_out"] = gfull["b_w_qkv"][None], gfull["b_w_out"][None]
    update("b_w_out")
    done = update("b_w_qkv")

    pack_b = jnp.concatenate([dm.reshape(-1), dlg.reshape(-1), dlb.reshape(-1)] + [gsmall[k] for k in SMALL])
    n_small = pack_b.shape[0]
    pack_b = jnp.pad(pack_b, (0, -n_small % (256 * LANES)))
    got_b = _all_gather_small(pack_b.reshape(-1, LANES), "gather_small_grads", after=[done]).reshape(N_DEV, -1, LANES)
    tot = _sum_slots(got_b, "sum_small").reshape(-1)
    o = 0
    dm_tot = tot[o:o + nsub * 3 * d].reshape(nsub, 3 * d); o += nsub * 3 * d
    dlg_tot = tot[o:o + nsub * d].reshape(nsub, d); o += nsub * d
    dlb_tot = tot[o:o + nsub * d].reshape(nsub, d); o += nsub * d
    g_small = {}
    for k, ref in zip(SMALL, (a_b_in, a_vn_g, a_vn_b, a_b_s)):
        g_small[k] = tot[o:o + ref.size].reshape(ref.shape); o += ref.size
    assert o == n_small
    aws = _scatter_wait(*scattering[("a_w_s",)], ["all"], tot, "scatter_wait_a_w_s")[0]
    g_small["a_w_s"] = _sum_slots(aws, "sum_a_w_s").reshape(a_w_s.shape)
    dm_all = got_b.reshape(N_DEV, -1)[:, :nsub * 3 * d].reshape(N_DEV, nsub, 3 * d)
    dm_cols = lax.dynamic_slice_in_dim(dm_all, q * cs, cs, axis=2).transpose(1, 0, 2)
    grads.update({
        "ada_w": _ada_bwd(c_all.T, dm_cols, "ada_bwd").reshape(ada_w.shape),
        "ada_b": lax.dynamic_slice_in_dim(dm_tot, q * cs, cs, axis=1).reshape(ada_b.shape),
        "ln_g": lax.dynamic_slice_in_dim(dlg_tot, q * ls, ls, axis=1).reshape(ln_g.shape),
        "ln_b": lax.dynamic_slice_in_dim(dlb_tot, q * ls, ls, axis=1).reshape(ln_b.shape),
        **g_small,
    })
    for k in ("ada_b", "ln_g", "ln_b", "a_w_s") + SMALL:
        update(k)
    done = update("ada_w")

    big_group(SCATTER_GROUPS[2] + SCATTER_GROUPS[3], done)
    grads.update({
        "a_w_in": gfull["a_w_in"][None], "a_w_out": gfull["a_w_out"][None],
        "mlp_w_up": jnp.stack([gfull["up0"], gfull["up1"]]), "mlp_w_down": jnp.stack([gfull["down0"], gfull["down1"]]),
    })
    for k in ("a_w_in", "a_w_out", "mlp_w_up", "mlp_w_down"):
        update(k)
    names = list(weights)
    return (loss, grad_x[None], *[grads[k] for k in names], *[updates[k][0] for k in names],
            *[updates[k][1] for k in names], *[updates[k][2] for k in names])
```

```python
import functools
import math

import jax
import jax.numpy as jnp
from jax import lax
from jax.experimental import pallas as pl
from jax.experimental.pallas import tpu as pltpu

F32 = jnp.float32
MXU_DTYPE = jnp.bfloat16

DEPTH = 2
CHUNK = 128
A_GROUPS = 16
B_HEADS = 16
HEAD_DIM = 64
B_PATTERNS = ((128, 1), (512, 4), (2048, 16))
SPAN = 128
ALPHA = (2 * DEPTH) ** 0.25
LN_EPS = 1e-5
NEG = -1e30
ATT_SCALE = HEAD_DIM ** -0.5
ADAM_LR, ADAM_B1, ADAM_B2, ADAM_EPS, ADAM_WD, ADAM_STEP = 0.001, 0.9, 0.999, 1e-08, 0.01, 10

N_CHIPS = 4
N_DEV = 8
LANES = 128
SUBLANES = 8
VMEM_LIMIT = 52 * 1024 * 1024
ROW_TILE = 512
MM_ROW_CHUNK = 256
MESH = pl.DeviceIdType.MESH


def _cparams(sem):
    return pltpu.CompilerParams(dimension_semantics=sem, vmem_limit_bytes=VMEM_LIMIT)


def _fold8(v):
    r, c = v.shape
    return jnp.sum(v.reshape(r // SUBLANES, SUBLANES, c), axis=0)


def _gelu(x):
    c = math.sqrt(2.0 / math.pi)
    return 0.5 * x * (1.0 + jnp.tanh(c * (x + 0.044715 * (x * x * x))))


def _gelu_and_grad(x):
    c = math.sqrt(2.0 / math.pi)
    t = jnp.tanh(c * (x + 0.044715 * (x * x * x)))
    return 0.5 * x * (1.0 + t), 0.5 * (1.0 + t) + 0.5 * x * (1.0 - t * t) * c * (1.0 + 3.0 * 0.044715 * x * x)


def _dot(a, b, dims):
    return lax.dot_general(a.astype(MXU_DTYPE), b.astype(MXU_DTYPE), (dims, ((), ())), preferred_element_type=F32)


def _dot_nn(a, b):
    return _dot(a, b, ((1,), (0,)))


def _dot_nt(a, b):
    return _dot(a, b, ((1,), (1,)))


def _dot_tn(a, b):
    return _dot(a, b, ((0,), (0,)))


def _mm(a, b, *, mode, name, outs, tm, tn, tk, epi=None, extras=(), b_col0=0, n_out=None, after=None,
        out_col0=0, out_cols=None, into=None):
    if mode == "nn":
        m, kdim = a.shape
        p, kb, ns = b.shape
        assert kb == kdim and ns % tn == 0 and b_col0 % tn == 0
        n = n_out if n_out is not None else p * ns
        npt, j0 = ns // tn, b_col0 // tn
        a_spec = pl.BlockSpec((tm, tk), lambda i, j, k: (i, k))
        b_spec = pl.BlockSpec((None, tk, tn), lambda i, j, k: ((j + j0) // npt, k, (j + j0) % npt))
        dot = _dot_nn
    elif mode == "nt":
        m, kdim = a.shape
        p, n, ns = b.shape
        assert ns % tk == 0 and b_col0 % tk == 0
        npt, j0 = ns // tk, b_col0 // tk
        a_spec = pl.BlockSpec((tm, tk), lambda i, j, k: (i, k))
        b_spec = pl.BlockSpec((None, tn, tk), lambda i, j, k: ((k + j0) // npt, j, (k + j0) % npt))
        dot = _dot_nt
    else:
        kdim, m = a.shape
        kb, n = b.shape
        assert kb == kdim
        a_spec = pl.BlockSpec((tk, tm), lambda i, j, k: (k, i))
        b_spec = pl.BlockSpec((tk, tn), lambda i, j, k: (k, j))
        dot = _dot_tn
    assert m % tm == 0 and n % tn == 0 and kdim % tk == 0, (name, m, n, kdim, tm, tn, tk)
    nk = kdim // tk
    ex_specs, ex_arrays = [], []
    for kind, arr in extras:
        if kind == "row":
            ex_specs.append(pl.BlockSpec((1, tn), lambda i, j, k: (0, j)))
        else:
            ex_specs.append(pl.BlockSpec((tm, tn), lambda i, j, k: (i, j)))
        ex_arrays.append(arr)
    n_ex, n_o = len(ex_arrays), len(outs)
    deps = [d for d in (after, into) if d is not None]
    n_dep = len(deps)
    j_out = out_col0 // tn
    assert out_col0 % tn == 0 and (into is None or len(outs) == 1)

    def body(a_ref, b_ref, *rest):
        ex_refs, o_refs = rest[:n_ex], rest[n_ex + n_dep:n_ex + n_dep + n_o]
        k = pl.program_id(2)

        chunks = [slice(r0, r0 + min(tm, MM_ROW_CHUNK)) for r0 in range(0, tm, min(tm, MM_ROW_CHUNK))]

        def part(rows):
            return dot(a_ref[:, rows] if mode == "tn" else a_ref[rows, :], b_ref[...])

        def finish(r, rows):
            exs = [e[...] if kind == "row" else e[rows, :] for (kind, _), e in zip(extras, ex_refs)]
            vals = epi(r, *exs) if epi is not None else [r]
            for o, v in zip(o_refs, vals):
                o[rows, :] = v.astype(o.dtype)

        if nk == 1:
            for rows in chunks:
                finish(part(rows), rows)
            return
        acc = rest[n_ex + n_dep + n_o]

        @pl.when(k == 0)
        def _():
            for rows in chunks:
                acc[rows, :] = part(rows)

        @pl.when((k > 0) & (k < nk - 1))
        def _():
            for rows in chunks:
                acc[rows, :] += part(rows)

        @pl.when(k == nk - 1)
        def _():
            for rows in chunks:
                finish(acc[rows, :] + part(rows), rows)

    res = pl.pallas_call(
        body,
        grid=(m // tm, n // tn, nk),
        in_specs=[a_spec, b_spec] + ex_specs + [pl.BlockSpec(memory_space=pl.ANY)] * n_dep,
        out_specs=[pl.BlockSpec((tm, tn), lambda i, j, k: (i, j + j_out)) for _ in outs],
        out_shape=[jax.ShapeDtypeStruct((m, out_cols or n), dt) for dt in outs],
        input_output_aliases={} if into is None else {2 + n_ex + n_dep - 1: 0},
        scratch_shapes=[pltpu.VMEM((tm, tn), F32)] if nk > 1 else [],
        name=name,
        compiler_params=_cparams(("parallel", "parallel", "arbitrary")),
    )(a, b, *ex_arrays, *deps)
    return res if len(outs) > 1 else res[0]


def _rows(body, n_rows, tr, ins, outs, name, scratch=()):
    def spec(kind, shape):
        if kind == "blk":
            return pl.BlockSpec((tr,) + tuple(shape[1:]), lambda i: (i,) + (0,) * (len(shape) - 1))
        if kind == "dep":
            return pl.BlockSpec(memory_space=pl.ANY)
        if kind == "str":
            return pl.BlockSpec((shape[0], tr // shape[0], shape[2]), lambda i: (0, i, 0))
        return pl.BlockSpec(tuple(shape), lambda i: (0,) * len(shape))

    return pl.pallas_call(
        body,
        grid=(n_rows // tr,),
        in_specs=[spec(k, a.shape) for k, a in ins],
        out_specs=[spec(k, s) for k, s, _ in outs],
        out_shape=[jax.ShapeDtypeStruct(tuple(s), d) for _, s, d in outs],
        scratch_shapes=list(scratch),
        name=name,
        compiler_params=_cparams(("arbitrary",)),
    )(*[a for _, a in ins])


def _ln_stats(z):
    mu = jnp.mean(z, axis=-1, keepdims=True)
    zc = z - mu
    var = jnp.mean(zc * zc, axis=-1, keepdims=True)
    rstd = lax.rsqrt(var + LN_EPS)
    return zc * rstd, rstd


def _stream_scratch(c):
    return pltpu.VMEM((c // LANES, ROW_TILE, LANES), F32)


def _streams_in(ref3, scr):
    dil, n, c = ref3.shape
    for r in range(dil):
        for j in range(c // LANES):
            scr.at[j][pl.ds(r, n, stride=dil), :] = ref3[r, :, j * LANES:(j + 1) * LANES].astype(F32)
    return jnp.concatenate([scr[j] for j in range(c // LANES)], axis=1)


def _streams_out(val, ref3, scr):
    dil, n, c = ref3.shape
    for j in range(c // LANES):
        scr[j] = val[:, j * LANES:(j + 1) * LANES].astype(F32)
    for r in range(dil):
        for j in range(c // LANES):
            ref3[r, :, j * LANES:(j + 1) * LANES] = scr.at[j][pl.ds(r, n, stride=dil), :].astype(ref3.dtype)


def _mod(x, scale, shift, after, name):
    s, d = x.shape

    def body(x_ref, sc_ref, sh_ref, dep_ref, h_ref):
        h_ref[...] = (x_ref[...] * (1.0 + sc_ref[...]) + sh_ref[...]).astype(h_ref.dtype)

    return _rows(body, s, ROW_TILE, [("blk", x), ("all", scale), ("all", shift), ("dep", after)], [("blk", (s, d), MXU_DTYPE)], name)[0]


def _resid_ln(x, y, gate, g, b, nxt, name, dils=()):
    s, d = x.shape

    def body(x_ref, y_ref, gate_ref, g_ref, b_ref, sc_ref, sh_ref, xn_ref, h_ref, *rest):
        z = ALPHA * x_ref[...] + gate_ref[...] * y_ref[...]
        xhat, _ = _ln_stats(z)
        xn = xhat * g_ref[...] + b_ref[...]
        xn_ref[...] = xn
        h = xn * (1.0 + sc_ref[...]) + sh_ref[...]
        h_ref[...] = h.astype(h_ref.dtype)
        for hs_ref in rest[:len(dils)]:
            _streams_out(h, hs_ref, rest[-1])

    return _rows(body, s, ROW_TILE,
                 [("blk", x), ("blk", y), ("all", gate), ("all", g), ("all", b), ("all", nxt[0]), ("all", nxt[1])],
                 [("blk", (s, d), F32), ("blk", (s, d), MXU_DTYPE)] + [("str", (dil, s // dil, d), MXU_DTYPE) for dil in dils], name,
                 scratch=[_stream_scratch(d)] if dils else [])


def _mod_bwd(dxr, dhs, x, scale, name, after=None):
    s, d = x.shape
    n_dh = len(dhs)
    n_dep = 0 if after is None else 1

    def body(dxr_ref, *rest):
        dh_refs = rest[:n_dh]
        x_ref, sc_ref, dx_ref, red_ref, a_sh, a_sc = rest[n_dh:n_dh + 2] + rest[n_dh + 2 + n_dep:]
        i = pl.program_id(0)

        @pl.when(i == 0)
        def _():
            a_sh[...] = jnp.zeros_like(a_sh)
            a_sc[...] = jnp.zeros_like(a_sc)

        dh = dh_refs[0][...]
        for r in dh_refs[1:]:
            dh = dh + r[...]
        dx_ref[...] = dxr_ref[...] + dh * (1.0 + sc_ref[...])
        a_sh[...] += _fold8(dh)
        a_sc[...] += _fold8(dh * x_ref[...])

        @pl.when(i == pl.num_programs(0) - 1)
        def _():
            red_ref[...] = jnp.zeros_like(red_ref)
            red_ref[0:1, :] = jnp.sum(a_sh[...], axis=0, keepdims=True)
            red_ref[1:2, :] = jnp.sum(a_sc[...], axis=0, keepdims=True)

    return _rows(body, s, ROW_TILE, [("blk", dxr)] + [("blk", h) for h in dhs] + [("blk", x), ("all", scale)] + [("dep", after)] * n_dep,
                 [("blk", (s, d), F32), ("all", (SUBLANES, d), F32)], name,
                 scratch=[pltpu.VMEM((SUBLANES, d), F32)] * 2)


def _last_ln_loss_bwd(x, y, gate, g, b, target, name):
    s, d = x.shape

    def body(x_ref, y_ref, gate_ref, g_ref, b_ref, t_ref, l_ref, dxr_ref, dyy_ref, red_ref, a_l, a_g, a_b, a_gate):
        i = pl.program_id(0)

        @pl.when(i == 0)
        def _():
            for a in (a_l, a_g, a_b, a_gate):
                a[...] = jnp.zeros_like(a)

        yv = y_ref[...]
        z = ALPHA * x_ref[...] + gate_ref[...] * yv
        xhat, rstd = _ln_stats(z)
        e = xhat * g_ref[...] + b_ref[...] - t_ref[...]
        a_l[...] += _fold8(e * e)
        dxo_v = e * (1.0 / d)
        dxh = dxo_v * g_ref[...]
        dz = rstd * (dxh - jnp.mean(dxh, axis=-1, keepdims=True) - xhat * jnp.mean(dxh * xhat, axis=-1, keepdims=True))
        dxr_ref[...] = ALPHA * dz
        dyy_ref[...] = (gate_ref[...] * dz).astype(dyy_ref.dtype)
        a_g[...] += _fold8(dxo_v * xhat)
        a_b[...] += _fold8(dxo_v)
        a_gate[...] += _fold8(dz * yv)

        @pl.when(i == pl.num_programs(0) - 1)
        def _():
            l_ref[...] = jnp.full(l_ref.shape, 0.5 / d, F32) * jnp.sum(a_l[...])
            red_ref[...] = jnp.zeros_like(red_ref)
            red_ref[0:1, :] = jnp.sum(a_g[...], axis=0, keepdims=True)
            red_ref[1:2, :] = jnp.sum(a_b[...], axis=0, keepdims=True)
            red_ref[2:3, :] = jnp.sum(a_gate[...], axis=0, keepdims=True)

    l, dxr, dyy, red = _rows(
        body, s, ROW_TILE, [("blk", x), ("blk", y), ("all", gate), ("all", g), ("all", b), ("blk", target)],
        [("all", (SUBLANES, LANES), F32), ("blk", (s, d), F32), ("blk", (s, d), MXU_DTYPE), ("all", (SUBLANES, d), F32)], name,
        scratch=[pltpu.VMEM((SUBLANES, d), F32)] * 4)
    return l[0, 0], dxr, dyy, red


def _mod_ln_bwd(dxr, dhs, x, scale, x_in, y, gate, g, name, after=None):
    s, d = x.shape
    n_dh = len(dhs)
    n_dep = 0 if after is None else 1

    def body(dxr_ref, *rest):
        dh_refs = rest[:n_dh]
        x_ref, sc_ref, xin_ref, y_ref, gate_ref, g_ref = rest[n_dh:n_dh + 6]
        dxr_out, dyy_ref, red_mod, red_ln, a_sh, a_sc, a_g, a_b, a_gate = rest[n_dh + 6 + n_dep:n_dh + 15 + n_dep]
        i = pl.program_id(0)

        @pl.when(i == 0)
        def _():
            for a in (a_sh, a_sc, a_g, a_b, a_gate):
                a[...] = jnp.zeros_like(a)

        dh = dh_refs[0][...]
        for r in dh_refs[1:]:
            dh = dh + (r[...] if len(r.shape) == 2 else _streams_in(r, rest[-1]))
        xv = x_ref[...]
        dxo_v = dxr_ref[...] + dh * (1.0 + sc_ref[...])
        a_sh[...] += _fold8(dh)
        a_sc[...] += _fold8(dh * xv)
        yv = y_ref[...]
        z = ALPHA * xin_ref[...] + gate_ref[...] * yv
        xhat, rstd = _ln_stats(z)
        dxh = dxo_v * g_ref[...]
        dz = rstd * (dxh - jnp.mean(dxh, axis=-1, keepdims=True) - xhat * jnp.mean(dxh * xhat, axis=-1, keepdims=True))
        dxr_out[...] = ALPHA * dz
        dyy_ref[...] = (gate_ref[...] * dz).astype(dyy_ref.dtype)
        a_g[...] += _fold8(dxo_v * xhat)
        a_b[...] += _fold8(dxo_v)
        a_gate[...] += _fold8(dz * yv)

        @pl.when(i == pl.num_programs(0) - 1)
        def _():
            red_mod[...] = jnp.zeros_like(red_mod)
            red_mod[0:1, :] = jnp.sum(a_sh[...], axis=0, keepdims=True)
            red_mod[1:2, :] = jnp.sum(a_sc[...], axis=0, keepdims=True)
            red_ln[...] = jnp.zeros_like(red_ln)
            red_ln[0:1, :] = jnp.sum(a_g[...], axis=0, keepdims=True)
            red_ln[1:2, :] = jnp.sum(a_b[...], axis=0, keepdims=True)
            red_ln[2:3, :] = jnp.sum(a_gate[...], axis=0, keepdims=True)

    ins = ([("blk", dxr)] + [("blk" if h.ndim == 2 else "str", h) for h in dhs]
           + [("blk", x), ("all", scale), ("blk", x_in), ("blk", y), ("all", gate), ("all", g)] + [("dep", after)] * n_dep)
    return _rows(body, s, ROW_TILE, ins,
                 [("blk", (s, d), F32), ("blk", (s, d), MXU_DTYPE), ("all", (SUBLANES, d), F32), ("all", (SUBLANES, d), F32)], name,
                 scratch=[pltpu.VMEM((SUBLANES, d), F32)] * 5 + [_stream_scratch(d)] * any(h.ndim == 3 for h in dhs))


def _left_half(shape):
    return lax.broadcasted_iota(jnp.int32, shape, 1) < (LANES // 2)


CHUNKS_PER_STEP = 2


def _chunks_of_step():
    return [slice(i * CHUNK, (i + 1) * CHUNK) for i in range(CHUNKS_PER_STEP)]


def _spatial_z(vn, wc_ref, bias_ref, j):
    vb = vn[:, j * LANES:(j + 1) * LANES]
    z0 = _dot_nn(wc_ref[2 * j], vb)
    z1 = _dot_nn(wc_ref[2 * j + 1], vb)
    return jnp.where(_left_half(z0.shape), z0, z1) + bias_ref[:, j * LANES:(j + 1) * LANES]


def _spatial_fwd(uvpre, vn_g, vn_b, wc, bias_full, name):
    s, d2 = uvpre.shape
    d = d2 // 2

    def body(uv_ref, g_ref, b_ref, wc_ref, bias_ref, out_ref):
        for rows in _chunks_of_step():
            u = _gelu(uv_ref[rows, :d])
            v = _gelu(uv_ref[rows, d:])
            vh, _ = _ln_stats(v)
            vn = vh * g_ref[...] + b_ref[...]
            for j in range(d // LANES):
                z = _spatial_z(vn, wc_ref, bias_ref, j)
                out_ref[rows, j * LANES:(j + 1) * LANES] = (u[:, j * LANES:(j + 1) * LANES] * z).astype(out_ref.dtype)

    return _rows(body, s, CHUNKS_PER_STEP * CHUNK, [("blk", uvpre), ("all", vn_g), ("all", vn_b), ("all", wc), ("all", bias_full)],
                 [("blk", (s, d), MXU_DTYPE)], name)[0]


def _spatial_bwd(uvpre, dgated, vn_g, vn_b, wc, wct, bias_full, name):
    s, d2 = uvpre.shape
    d = d2 // 2

    def body(uv_ref, dg_ref, g_ref, b_ref, wc_ref, wct_ref, bias_ref,
             duv_ref, dws_ref, dbias_ref, dbin_ref, dvg_ref, dvb_ref, dvn_buf, a_bin, a_vg, a_vb):
        i = pl.program_id(0)

        @pl.when(i == 0)
        def _():
            dws_ref[...] = jnp.zeros_like(dws_ref)
            dbias_ref[...] = jnp.zeros_like(dbias_ref)
            a_bin[...] = jnp.zeros_like(a_bin)
            a_vg[...] = jnp.zeros_like(a_vg)
            a_vb[...] = jnp.zeros_like(a_vb)

        for rows in _chunks_of_step():
            u, u_grad = _gelu_and_grad(uv_ref[rows, :d])
            v, v_grad = _gelu_and_grad(uv_ref[rows, d:])
            vh, rstd = _ln_stats(v)
            vn = vh * g_ref[...] + b_ref[...]
            dg = dg_ref[rows, :]
            dzz = dg * u
            dbias_ref[...] += dzz
            for j in range(d // LANES):
                cols = slice(j * LANES, (j + 1) * LANES)
                z = _spatial_z(vn, wc_ref, bias_ref, j)
                dup = dg[:, cols] * z * u_grad[:, cols]
                duv_ref[rows, cols] = dup.astype(duv_ref.dtype)
                a_bin[:, cols] += _fold8(dup)
                dzb = dzz[:, cols]
                left = _left_half(dzb.shape)
                dvn_buf[:, cols] = jnp.where(left, _dot_nn(wct_ref[2 * j], dzb), _dot_nn(wct_ref[2 * j + 1], dzb))
                vb = vn[:, cols]
                dws_ref[2 * j] += _dot_nt(jnp.where(left, dzb, 0.0), vb)
                dws_ref[2 * j + 1] += _dot_nt(jnp.where(left, 0.0, dzb), vb)
            dvn = dvn_buf[...]
            a_vg[...] += _fold8(dvn * vh)
            a_vb[...] += _fold8(dvn)
            dvh = dvn * g_ref[...]
            dv = rstd * (dvh - jnp.mean(dvh, axis=-1, keepdims=True) - vh * jnp.mean(dvh * vh, axis=-1, keepdims=True))
            dvp = dv * v_grad
            duv_ref[rows, d:] = dvp.astype(duv_ref.dtype)
            a_bin[:, d:] += _fold8(dvp)

        @pl.when(i == pl.num_programs(0) - 1)
        def _():
            dbin_ref[...] = jnp.sum(a_bin[...], axis=0, keepdims=True)
            dvg_ref[...] = jnp.sum(a_vg[...], axis=0, keepdims=True)
            dvb_ref[...] = jnp.sum(a_vb[...], axis=0, keepdims=True)

    return _rows(body, s, CHUNKS_PER_STEP * CHUNK,
                 [("blk", uvpre), ("blk", dgated), ("all", vn_g), ("all", vn_b), ("all", wc), ("all", wct), ("all", bias_full)],
                 [("blk", (s, d2), MXU_DTYPE), ("all", (A_GROUPS, CHUNK, CHUNK), F32), ("all", (CHUNK, d), F32),
                  ("all", (1, d2), F32), ("all", (1, d), F32), ("all", (1, d), F32)], name,
                 scratch=[pltpu.VMEM((CHUNK, d), F32), pltpu.VMEM((SUBLANES, d2), F32),
                          pltpu.VMEM((SUBLANES, d), F32), pltpu.VMEM((SUBLANES, d), F32)])


def _head_mask(v, h):
    lane = lax.broadcasted_iota(jnp.int32, v.shape, 1)
    return jnp.where((lane >= h * HEAD_DIM) & (lane < (h + 1) * HEAD_DIM), v, jnp.zeros_like(v))


def _att_bias(slopes, dil):
    qi = lax.broadcasted_iota(jnp.int32, (SPAN, SPAN), 0)
    ki = lax.broadcasted_iota(jnp.int32, (SPAN, SPAN), 1)
    sl = slopes[:, None, None]
    cur = jnp.where(ki <= qi, -sl * (float(dil) * (qi - ki).astype(F32)), NEG)
    prev = jnp.where(ki >= qi, -sl * (float(dil) * (SPAN + qi - ki).astype(F32)), NEG)
    absent = jnp.full_like(prev, NEG)
    pairs = slopes.shape[0] // 2

    def fwd(pv):
        return jnp.concatenate([cur, pv], axis=2).reshape(pairs, 2 * SPAN, 2 * SPAN)

    def bwd(pv):
        return jnp.concatenate([cur.reshape(pairs, 2 * SPAN, SPAN), pv.reshape(pairs, 2 * SPAN, SPAN)], axis=1)

    return jnp.stack([fwd(absent), fwd(prev)]), jnp.stack([bwd(absent), bwd(prev)])


ATT_GROUP = 4


def _att_group(s, dil):
    nb = s // (dil * SPAN)
    grp = min(ATT_GROUP, nb)
    assert nb % grp == 0
    return nb, grp


def _att_specs(s, d, dil, kinds):
    nb, grp = _att_group(s, dil)

    def spec(part, which):
        if which == "group":
            return pl.BlockSpec((grp * SPAN, d), lambda b: (b, part))
        if which == "prev":
            return pl.BlockSpec((SPAN, d), lambda b: (jnp.where((grp * b) % nb == 0, grp * b, grp * b - 1), part))
        return pl.BlockSpec((SPAN, d), lambda b: (jnp.where((grp * b + grp - 1) % nb == nb - 1, grp * b + grp - 1, grp * b + grp), part))

    return [spec(part, which) for part, which in kinds]


def _head_col(v, head):
    return v[:, head:head + 1]


def _expand_heads(w, j):
    shape = (w.shape[0], LANES)
    return jnp.where(_left_half(shape), jnp.broadcast_to(_head_col(w, 2 * j), shape), jnp.broadcast_to(_head_col(w, 2 * j + 1), shape))


def _attn_fwd(qkv, slopes, dil, name):
    s, d3 = qkv.shape
    d = d3 // 3
    nb, grp = _att_group(s, dil)
    table, _ = _att_bias(slopes, dil)

    def body(q_ref, k_ref, kp_ref, v_ref, vp_ref, tb_ref, o_ref, l_ref):
        b = pl.program_id(0)
        left = _left_half((SPAN, LANES))
        lane = lax.broadcasted_iota(jnp.int32, (SPAN, LANES), 1)
        for sub in range(grp):
            rows, before = slice(sub * SPAN, (sub + 1) * SPAN), slice((sub - 1) * SPAN, sub * SPAN)
            variant = jnp.where((grp * b) % nb == 0, 0, 1) if sub == 0 else 1
            lses = jnp.zeros((SPAN, LANES), F32)
            for hp in range(d // LANES):
                cols = slice(hp * LANES, (hp + 1) * LANES)
                q = q_ref[rows, cols]
                q2 = jnp.concatenate([_head_mask(q, 0), _head_mask(q, 1)], axis=0) * ATT_SCALE
                k2 = jnp.concatenate([k_ref[rows, cols], kp_ref[:, cols] if sub == 0 else k_ref[before, cols]], axis=0)
                v2 = jnp.concatenate([v_ref[rows, cols], vp_ref[:, cols] if sub == 0 else v_ref[before, cols]], axis=0)
                sc = _dot_nt(q2, k2) + tb_ref[variant, hp]
                m = jnp.max(sc, axis=-1, keepdims=True)
                p = jnp.exp(sc - m)
                l = jnp.sum(p, axis=-1, keepdims=True)
                r = _dot_nn(p, v2) * (1.0 / l)
                lse = m + jnp.log(l)
                o_ref[rows, cols] = jnp.where(left, r[:SPAN], r[SPAN:])
                lses = jnp.where(lane == 2 * hp, lse[:SPAN], jnp.where(lane == 2 * hp + 1, lse[SPAN:], lses))
            l_ref[rows, :] = lses

    specs = _att_specs(s, d, dil, [(0, "group"), (1, "group"), (1, "prev"), (2, "group"), (2, "prev")])
    return pl.pallas_call(
        body,
        grid=(s // (grp * SPAN),),
        in_specs=specs + [pl.BlockSpec(table.shape, lambda b: (0, 0, 0, 0))],
        out_specs=[pl.BlockSpec((grp * SPAN, d), lambda b: (b, 0)), pl.BlockSpec((grp * SPAN, LANES), lambda b: (b, 0))],
        out_shape=[jax.ShapeDtypeStruct((s, d), F32), jax.ShapeDtypeStruct((s, LANES), F32)],
        name=name,
        compiler_params=_cparams(("parallel",)),
    )(qkv, qkv, qkv, qkv, qkv, table)


def _attn_bwd(qkv, do, lse, dd, slopes, dil, name):
    s, d3 = qkv.shape
    d = d3 // 3
    nb, grp = _att_group(s, dil)
    _, table = _att_bias(slopes, dil)

    def heads_stacked(cur, nxt):
        return jnp.concatenate([_head_mask(cur, 0), _head_mask(cur, 1), _head_mask(nxt, 0), _head_mask(nxt, 1)], axis=0)

    def cols_stacked(cur, nxt, hp):
        return jnp.concatenate([jnp.broadcast_to(_head_col(a, 2 * hp + h), (SPAN, LANES)) for a in (cur, nxt) for h in range(2)], axis=0)

    def body(k_ref, v_ref, q_ref, qn_ref, do_ref, don_ref, l_ref, ln_ref, dd_ref, ddn_ref, tb_ref, out_ref, carry):
        b = pl.program_id(0)

        @pl.when(b == 0)
        def _():
            carry[...] = jnp.zeros_like(carry)

        left = _left_half((SPAN, LANES))
        for sub in range(grp):
            rows, after = slice(sub * SPAN, (sub + 1) * SPAN), slice((sub + 1) * SPAN, (sub + 2) * SPAN)
            last = sub == grp - 1
            variant = jnp.where((grp * b + sub) % nb == nb - 1, 0, 1) if last else 1
            lse_c, dd_c = l_ref[rows, :], dd_ref[rows, :]
            lse_n, dd_n = (ln_ref[...], ddn_ref[...]) if last else (l_ref[after, :], dd_ref[after, :])
            for hp in range(d // LANES):
                cols = slice(hp * LANES, (hp + 1) * LANES)
                k, v = k_ref[rows, cols], v_ref[rows, cols]
                q4 = heads_stacked(q_ref[rows, cols], qn_ref[:, cols] if last else q_ref[after, cols])
                do4 = heads_stacked(do_ref[rows, cols], don_ref[:, cols] if last else do_ref[after, cols])
                sc = _dot_nt(q4 * ATT_SCALE, k) + tb_ref[variant, hp]
                p = jnp.exp(sc - cols_stacked(lse_c, lse_n, hp))
                ds = p * (_dot_nt(do4, v) - cols_stacked(dd_c, dd_n, hp))
                dq4 = _dot_nn(ds, k)
                dq_cur = jnp.where(left, dq4[:SPAN], dq4[SPAN:2 * SPAN]) + carry[:, cols]
                carry[:, cols] = jnp.where(left, dq4[2 * SPAN:3 * SPAN], dq4[3 * SPAN:])
                out_ref[rows, cols] = (dq_cur * ATT_SCALE).astype(out_ref.dtype)
                out_ref[rows, d + hp * LANES:d + (hp + 1) * LANES] = (_dot_tn(ds, q4) * ATT_SCALE).astype(out_ref.dtype)
                out_ref[rows, 2 * d + hp * LANES:2 * d + (hp + 1) * LANES] = _dot_tn(p, do4).astype(out_ref.dtype)

    qkv_specs = _att_specs(s, d, dil, [(1, "group"), (2, "group"), (0, "group"), (0, "next")])
    wide = _att_specs(s, d, dil, [(0, "group"), (0, "next")])
    heads = _att_specs(s, LANES, dil, [(0, "group"), (0, "next")])
    return pl.pallas_call(
        body,
        grid=(s // (grp * SPAN),),
        in_specs=qkv_specs + wide + heads + heads + [pl.BlockSpec(table.shape, lambda b: (0, 0, 0, 0))],
        out_specs=pl.BlockSpec((grp * SPAN, d3), lambda b: (b, 0)),
        out_shape=jax.ShapeDtypeStruct((s, d3), MXU_DTYPE),
        scratch_shapes=[pltpu.VMEM((SPAN, d), F32)],
        name=name,
        compiler_params=_cparams(("arbitrary",)),
    )(qkv, qkv, qkv, qkv, do, do, lse, lse, dd, dd, table)


def _mix_weights(l_refs):
    ls = [r[...] for r in l_refs]
    m = functools.reduce(jnp.maximum, ls)
    es = [jnp.exp(l - m) for l in ls]
    tot = functools.reduce(lambda a, c: a + c, es)
    return [e / tot for e in es]


def _combine_fwd(os_, ls_, name):
    s, d = ls_[0].shape[0], os_[0].shape[-1]
    n = len(os_)
    n_str = sum(o.ndim == 3 for o in os_)

    def body(*refs):
        o_refs, l_refs, out_ref, scrs = refs[:n], refs[n:2 * n], refs[2 * n], list(refs[2 * n + 1:])
        ws = _mix_weights(l_refs)
        os_v = [o if len(o.shape) == 2 else _streams_in(o, scrs.pop()) for o in o_refs]
        for j in range(d // LANES):
            cols = slice(j * LANES, (j + 1) * LANES)
            acc = _expand_heads(ws[0], j) * os_v[0][:, cols]
            for w, o in zip(ws[1:], os_v[1:]):
                acc = acc + _expand_heads(w, j) * o[:, cols]
            out_ref[:, cols] = acc

    return _rows(body, s, ROW_TILE, [("blk" if a.ndim == 2 else "str", a) for a in os_] + [("blk", a) for a in ls_],
                 [("blk", (s, d), F32)], name, scratch=[_stream_scratch(d)] * n_str)[0]


def _combine_bwd(do, o, ls_, dils, name):
    s, d = o.shape
    n = len(ls_)
    sel = (lax.broadcasted_iota(jnp.int32, (d, LANES), 0) // HEAD_DIM == lax.broadcasted_iota(jnp.int32, (d, LANES), 1)).astype(F32)

    def body(do_ref, o_ref, *rest):
        l_refs, sel_ref, outs = rest[:n], rest[n], rest[n + 1:n + 1 + 2 * n]
        ws = _mix_weights(l_refs)
        dov = do_ref[...]
        r = jnp.dot(dov * o_ref[...], sel_ref[...], precision=lax.Precision.HIGHEST, preferred_element_type=F32)
        for g in range(n):
            outs[2 * g + 1][...] = ws[g] * r
            parts = [_expand_heads(ws[g], j) * dov[:, j * LANES:(j + 1) * LANES] for j in range(d // LANES)]
            if dils[g] == 1:
                for j, part in enumerate(parts):
                    outs[2 * g][:, j * LANES:(j + 1) * LANES] = part.astype(outs[2 * g].dtype)
            else:
                _streams_out(jnp.concatenate(parts, axis=1), outs[2 * g], rest[-1])

    outs = []
    for dil in dils:
        outs += [("blk", (s, d), MXU_DTYPE) if dil == 1 else ("str", (dil, s // dil, d), MXU_DTYPE), ("blk", (s, LANES), F32)]
    res = _rows(body, s, ROW_TILE, [("blk", do), ("blk", o)] + [("blk", l) for l in ls_] + [("all", sel)], outs, name,
                scratch=[_stream_scratch(d)])
    return [(res[2 * g], res[2 * g + 1]) for g in range(n)]


def _ada_fwd(c_all, w, b, name):
    nsub, d, cs = w.shape

    def body(c_ref, w_ref, b_ref, o_ref):
        cv = c_ref[...]
        sc = cv * (1.0 / (1.0 + jnp.exp(-cv)))
        o_ref[...] = _dot_nn(sc, w_ref[...]) + b_ref[...]

    return pl.pallas_call(
        body,
        grid=(nsub,),
        in_specs=[pl.BlockSpec(c_all.shape, lambda i: (0, 0)), pl.BlockSpec((None, d, cs), lambda i: (i, 0, 0)),
                  pl.BlockSpec((None, 1, cs), lambda i: (i, 0, 0))],
        out_specs=pl.BlockSpec((None, N_DEV, cs), lambda i: (i, 0, 0)),
        out_shape=jax.ShapeDtypeStruct((nsub, N_DEV, cs), F32),
        name=name,
        compiler_params=_cparams(("parallel",)),
    )(c_all, w, b)


def _ada_bwd(c_all_t, dm, name):
    d, nb = c_all_t.shape
    nsub, _, cs = dm.shape

    def body(c_ref, dm_ref, o_ref):
        cv = c_ref[...]
        sc = cv * (1.0 / (1.0 + jnp.exp(-cv)))
        acc = sc[:, 0:1] * dm_ref[0:1, :]
        for bi in range(1, nb):
            acc = acc + sc[:, bi:bi + 1] * dm_ref[bi:bi + 1, :]
        o_ref[...] = acc

    return pl.pallas_call(
        body,
        grid=(nsub,),
        in_specs=[pl.BlockSpec(c_all_t.shape, lambda i: (0, 0)), pl.BlockSpec((None, nb, cs), lambda i: (i, 0, 0))],
        out_specs=pl.BlockSpec((None, d, cs), lambda i: (i, 0, 0)),
        out_shape=jax.ShapeDtypeStruct((nsub, d, cs), F32),
        name=name,
        compiler_params=_cparams(("parallel",)),
    )(c_all_t, dm)


def _row_tile(r, row_elems, block_elems=256 * 1024):
    t = 2 * SUBLANES
    if r % t:
        return r
    while t * 2 * row_elems <= block_elems and r % (t * 2) == 0:
        t *= 2
    return t


def _adamw(w, g, m, v, name):
    shape = w.shape
    c = shape[-1]
    r = w.size // c
    tr = _row_tile(r, c, 512 * 1024)
    w2, g2, m2, v2 = [a.reshape(r, c) for a in (w, g, m, v)]
    bc1 = 1.0 - ADAM_B1 ** ADAM_STEP
    bc2 = 1.0 - ADAM_B2 ** ADAM_STEP

    def body(w_ref, g_ref, m_ref, v_ref, d_ref, nm_ref, nv_ref):
        gv = g_ref[...]
        nm = ADAM_B1 * m_ref[...] + (1.0 - ADAM_B1) * gv
        nv = ADAM_B2 * v_ref[...] + (1.0 - ADAM_B2) * (gv * gv)
        d_ref[...] = -ADAM_LR * ((nm / bc1) / (jnp.sqrt(nv / bc2) + ADAM_EPS) + ADAM_WD * w_ref[...])
        nm_ref[...] = nm
        nv_ref[...] = nv

    res = _rows(body, r, tr, [("blk", a) for a in (w2, g2, m2, v2)], [("blk", (r, c), F32)] * 3, name)
    return [a.reshape(shape) for a in res]


def _sum_slots(buf, name):
    n, r, c = buf.shape
    tr = _row_tile(r, n * c, 2 * 1024 * 1024)

    def body(b_ref, o_ref):
        acc = b_ref[0].astype(F32)
        for k in range(1, n):
            acc = acc + b_ref[k].astype(F32)
        o_ref[...] = acc

    return pl.pallas_call(
        body,
        grid=(r // tr,),
        in_specs=[pl.BlockSpec((n, tr, c), lambda i: (0, i, 0))],
        out_specs=pl.BlockSpec((tr, c), lambda i: (i, 0)),
        out_shape=jax.ShapeDtypeStruct((r, c), F32),
        name=name,
        compiler_params=_cparams(("parallel",)),
    )(buf)


def _me():
    return lax.axis_index("x"), lax.axis_index("y"), lax.axis_index("c")


def _all_gather_small(blk, name, after=()):
    m_per, n = blk.shape

    def body(x_ref, *rest):
        out_ref, send_sems, recv_sems, local_sem = rest[len(after):]
        x, y, c = _me()
        me, sibling = (x, y, c), (x, y, 1 - c)
        chips = [(1 - x, y), (x, 1 - y), (1 - x, 1 - y)]

        def rows(px, py, pc):
            return out_ref.at[pl.ds((4 * px + 2 * py + pc) * m_per, m_per), :]

        def copy(k, block, to, src=None):
            return pltpu.make_async_remote_copy(
                src_ref=rows(*block) if src is None else src, dst_ref=rows(*block),
                send_sem=send_sems.at[k], recv_sem=recv_sems.at[k], device_id=to, device_id_type=MESH)

        mine = pltpu.make_async_copy(x_ref, rows(*me), local_sem)
        mine.start()
        first = [copy(0, me, sibling, src=x_ref)]
        first += [copy(1 + j, me, (*chip, c), src=x_ref) for j, chip in enumerate(chips)]
        for cp in first:
            cp.start()
        passed = [copy(4 + j, (*chip, c), sibling) for j, chip in enumerate(chips)]
        for j, chip in enumerate(chips):
            copy(1 + j, (*chip, c), me).wait_recv()
            passed[j].start()
        copy(0, sibling, me).wait_recv()
        for j, chip in enumerate(chips):
            copy(4 + j, (*chip, 1 - c), me).wait_recv()
        for cp in first + passed:
            cp.wait_send()
        mine.wait()

    return pl.pallas_call(
        body,
        out_shape=jax.ShapeDtypeStruct((N_DEV * m_per, n), blk.dtype),
        in_specs=[pl.BlockSpec(memory_space=pltpu.VMEM)] + [pl.BlockSpec(memory_space=pl.ANY)] * len(after),
        out_specs=pl.BlockSpec(memory_space=pltpu.VMEM),
        scratch_shapes=[pltpu.SemaphoreType.DMA((7,)), pltpu.SemaphoreType.DMA((7,)), pltpu.SemaphoreType.DMA],
        name=name,
        compiler_params=pltpu.CompilerParams(vmem_limit_bytes=VMEM_LIMIT),
    )(blk, *after)


_HBM = pl.BlockSpec(memory_space=pltpu.HBM)
_SEM = pl.BlockSpec(memory_space=pltpu.SEMAPHORE)
_EFFECT = pltpu.SideEffectType.DATAFLOW_SIDE_EFFECTING


def _other_chips(x, y):
    return [(1 - x, y), (x, 1 - y), (1 - x, 1 - y)]


def _gather_copy(w, j, src_ref, land_ref, send_sems, recv_sems, halved=False):
    x, y, c = _me()
    if halved:
        half = src_ref.shape[0] // 2
        src_ref = src_ref.at[pl.ds(c * half, half), :]
    return pltpu.make_async_remote_copy(
        src_ref=src_ref, dst_ref=land_ref.at[2 * x + y], send_sem=send_sems.at[3 * w + j], recv_sem=recv_sems.at[3 * w + j],
        device_id=(*_other_chips(x, y)[j], c), device_id_type=MESH)


def _gather_start(shards, halved, after, name):
    n = len(shards)
    lands = [lax.empty((N_CHIPS, s.shape[0] // 2 if w in halved else s.shape[0], s.shape[1]), s.dtype) for w, s in enumerate(shards)]

    def body(*refs):
        in_refs, land_refs = refs[:n], refs[n:2 * n]
        send_sems, recv_sems = refs[2 * n + 1], refs[2 * n + 2]
        token = refs[-1]
        for w in range(n):
            for j in range(3):
                _gather_copy(w, j, in_refs[w], land_refs[w], send_sems, recv_sems, w in halved).start()
        token[...] = jnp.zeros_like(token)

    res = pl.pallas_call(
        body,
        out_shape=(pltpu.SemaphoreType.DMA((3 * n,)), pltpu.SemaphoreType.DMA((3 * n,)),
                   *[pltpu.HBM(s.shape, s.dtype) for s in shards], *[pltpu.HBM(l.shape, l.dtype) for l in lands],
                   jax.ShapeDtypeStruct((SUBLANES, LANES), F32)),
        in_specs=[_HBM] * (2 * n) + [pl.BlockSpec(memory_space=pl.ANY)],
        out_specs=(_SEM, _SEM, *[_HBM] * (2 * n), pl.BlockSpec(memory_space=pltpu.VMEM)),
        input_output_aliases={i: 2 + i for i in range(2 * n)},
        name=name,
        compiler_params=pltpu.CompilerParams(has_side_effects=_EFFECT),
    )(*[pltpu.with_memory_space_constraint(a, pltpu.HBM) for a in list(shards) + lands], after)
    return res[0], res[1], res[2:2 + n], res[2 + n:2 + 2 * n], res[-1]


def _gather_wait(w, shard, land, send_sems, recv_sems, after, name, halved=False):
    def body(s_ref, land_ref, send_sems, recv_sems, after_ref, s_out, land_out, stage):
        x, y, _ = _me()
        if not halved:
            pltpu.sync_copy(s_ref, stage)
            pltpu.sync_copy(stage, land_out.at[2 * x + y])
        for j in range(3):
            cp = _gather_copy(w, j, s_ref, land_ref, send_sems, recv_sems, halved)
            cp.wait_send()
            cp.wait_recv()

    return pl.pallas_call(
        body,
        out_shape=(pltpu.HBM(shard.shape, shard.dtype), pltpu.HBM(land.shape, land.dtype)),
        in_specs=(_HBM, _HBM, _SEM, _SEM, pl.BlockSpec(memory_space=pl.ANY)),
        out_specs=(_HBM, _HBM),
        input_output_aliases={0: 0, 1: 1},
        scratch_shapes=[pltpu.VMEM((SUBLANES, LANES) if halved else shard.shape, shard.dtype)],
        name=name,
        compiler_params=pltpu.CompilerParams(has_side_effects=_EFFECT, vmem_limit_bytes=VMEM_LIMIT),
    )(shard, land, send_sems, recv_sems, after)


def _assemble_halves(shard, land, name):
    half = land.shape[1]

    def body(s_ref, land_ref, out_ref, send_sems, recv_sems, local_sems):
        x, y, c = _me()
        own = pltpu.make_async_copy(s_ref, out_ref.at[2 * x + y], local_sems.at[3])
        own.start()
        cps = []
        for j, (ox, oy) in enumerate(_other_chips(x, y)):
            qj = 2 * ox + oy
            mine = out_ref.at[qj, pl.ds(c * half, half), :]
            lc = pltpu.make_async_copy(land_ref.at[qj], mine, local_sems.at[j])
            lc.start()
            rc = pltpu.make_async_remote_copy(
                src_ref=land_ref.at[qj], dst_ref=mine, send_sem=send_sems.at[j], recv_sem=recv_sems.at[j],
                device_id=(x, y, 1 - c), device_id_type=MESH)
            rc.start()
            cps.append((lc, rc))
        for lc, rc in cps:
            rc.wait_recv()
        for lc, rc in cps:
            rc.wait_send()
            lc.wait()
        own.wait()

    vmem = pl.BlockSpec(memory_space=pltpu.VMEM)
    return pl.pallas_call(
        body,
        out_shape=jax.ShapeDtypeStruct((N_CHIPS,) + shard.shape, shard.dtype),
        in_specs=[vmem, vmem],
        out_specs=vmem,
        scratch_shapes=[pltpu.SemaphoreType.DMA((3,)), pltpu.SemaphoreType.DMA((3,)), pltpu.SemaphoreType.DMA((4,))],
        name=name,
        compiler_params=pltpu.CompilerParams(vmem_limit_bytes=VMEM_LIMIT),
    )(shard, land)


def _piece_shape(shape, kind):
    k, nn = shape
    if kind == "all":
        return (k, nn)
    return (k // 2, nn // N_CHIPS) if kind == "col" else (k // N_CHIPS // 2, nn)


def _piece_of(g_ref, kind, tq, tc):
    pr, pc = _piece_shape(g_ref.shape, kind)
    if kind == "all":
        return g_ref
    if kind == "col":
        return g_ref.at[pl.ds(tc * pr, pr), pl.ds(tq * pc, pc)]
    return g_ref.at[pl.ds((2 * tq + tc) * pr, pr), :]


def _scatter_copy(w, r, kind, g_ref, land_ref, send_sems, recv_sems):
    x, y, c = _me()
    tx, ty, tc = (x + ((r >> 2) & 1)) % 2, (y + ((r >> 1) & 1)) % 2, (c + (r & 1)) % 2
    return pltpu.make_async_remote_copy(
        src_ref=_piece_of(g_ref, kind, 2 * tx + ty, tc), dst_ref=land_ref.at[4 * x + 2 * y + c],
        send_sem=send_sems.at[N_DEV * w + r], recv_sem=recv_sems.at[N_DEV * w + r], device_id=(tx, ty, tc), device_id_type=MESH)


def _scatter_start(gs, kinds, name):
    n = len(gs)
    pieces = [_piece_shape(g.shape, kind) for g, kind in zip(gs, kinds)]
    lands = [lax.empty((N_DEV,) + p, g.dtype) for p, g in zip(pieces, gs)]

    def body(*refs):
        g_refs, land_refs, send_sems, recv_sems = refs[:n], refs[n:2 * n], refs[2 * n], refs[2 * n + 1]
        land_outs, stages = refs[3 * n + 2:4 * n + 2], refs[4 * n + 2:]
        x, y, c = _me()
        for w in range(n):
            for r in range(1, N_DEV):
                _scatter_copy(w, r, kinds[w], g_refs[w], land_refs[w], send_sems, recv_sems).start()
        for w in range(n):
            pltpu.sync_copy(_piece_of(g_refs[w], kinds[w], 2 * x + y, c), stages[w])
            pltpu.sync_copy(stages[w], land_outs[w].at[4 * x + 2 * y + c])

    arrays = list(gs) + lands
    res = pl.pallas_call(
        body,
        out_shape=(pltpu.SemaphoreType.DMA((N_DEV * n,)), pltpu.SemaphoreType.DMA((N_DEV * n,)),
                   *[pltpu.HBM(a.shape, a.dtype) for a in arrays]),
        in_specs=[_HBM] * (2 * n),
        out_specs=(_SEM, _SEM, *[_HBM] * (2 * n)),
        input_output_aliases={i: 2 + i for i in range(2 * n)},
        scratch_shapes=[pltpu.VMEM(p, g.dtype) for p, g in zip(pieces, gs)],
        name=name,
        compiler_params=pltpu.CompilerParams(has_side_effects=_EFFECT, vmem_limit_bytes=VMEM_LIMIT),
    )(*[pltpu.with_memory_space_constraint(a, pltpu.HBM) for a in arrays])
    return res[0], res[1], res[2:2 + n], res[2 + n:]


def _scatter_wait(send_sems, recv_sems, gs, lands, kinds, after, name):
    n = len(gs)

    def body(*refs):
        g_refs, land_refs, send_sems, recv_sems = refs[:n], refs[n:2 * n], refs[2 * n], refs[2 * n + 1]
        for w in range(n):
            for r in range(1, N_DEV):
                cp = _scatter_copy(w, r, kinds[w], g_refs[w], land_refs[w], send_sems, recv_sems)
                cp.wait_send()
                cp.wait_recv()

    arrays = list(gs) + list(lands)
    return pl.pallas_call(
        body,
        out_shape=tuple(pltpu.HBM(a.shape, a.dtype) for a in arrays),
        in_specs=(*[_HBM] * (2 * n), _SEM, _SEM, pl.BlockSpec(memory_space=pl.ANY)),
        out_specs=tuple([_HBM] * (2 * n)),
        input_output_aliases={i: i for i in range(2 * n)},
        name=name,
        compiler_params=pltpu.CompilerParams(has_side_effects=_EFFECT),
    )(*arrays, send_sems, recv_sems, after)[n:]


def _swap_halves(halves, name):
    n = len(halves)

    def body(*refs):
        in_refs, out_refs = refs[:n], refs[n:2 * n]
        send_sems, recv_sems, local_sems = refs[2 * n:]
        x, y, c = _me()
        cps = []
        for w in range(n):
            lc = pltpu.make_async_copy(in_refs[w], out_refs[w].at[c], local_sems.at[w])
            lc.start()
            rc = pltpu.make_async_remote_copy(
                src_ref=in_refs[w], dst_ref=out_refs[w].at[c], send_sem=send_sems.at[w], recv_sem=recv_sems.at[w],
                device_id=(x, y, 1 - c), device_id_type=MESH)
            rc.start()
            cps.append((lc, rc))
        for lc, rc in cps:
            rc.wait_recv()
        for lc, rc in cps:
            rc.wait_send()
            lc.wait()

    vmem = pl.BlockSpec(memory_space=pltpu.VMEM)
    return pl.pallas_call(
        body,
        out_shape=[jax.ShapeDtypeStruct((2,) + h.shape, h.dtype) for h in halves],
        in_specs=[vmem] * n,
        out_specs=[vmem] * n,
        scratch_shapes=[pltpu.SemaphoreType.DMA((n,)), pltpu.SemaphoreType.DMA((n,)), pltpu.SemaphoreType.DMA((n,))],
        name=name,
        compiler_params=pltpu.CompilerParams(vmem_limit_bytes=VMEM_LIMIT),
    )(*halves)


def _to_streams(a, dil):
    if dil == 1:
        return a
    s, c = a.shape
    return a.reshape(s // dil, dil, c).transpose(1, 0, 2).reshape(s, c)


def _from_streams(a, dil):
    if dil == 1:
        return a
    s, c = a.shape
    return a.reshape(dil, s // dil, c).transpose(1, 0, 2).reshape(s, c)


def _mm_tiles(s):
    return min(s, 2048)


def _local_step(x0, target, mvec, ln_g, ln_b, small, fetch, emit, start):
    s, d = x0.shape
    tm = _mm_tiles(s)
    row = lambda v: v.reshape(1, -1)
    shift = [row(mvec[i, :d]) for i in range(4)]
    scale = [row(mvec[i, d:2 * d]) for i in range(4)]
    gate = [row(1.0 + mvec[i, 2 * d:]) for i in range(4)]
    lg = [row(ln_g[i]) for i in range(4)]
    lb = [row(ln_b[i]) for i in range(4)]
    mm = functools.partial(_mm, tm=tm)
    mm_w = functools.partial(_mm, tm=1024, tk=min(s, 2048), mode="tn")

    xs, ys, big = [x0], [], {}
    h0 = _mod(x0, scale[0], shift[0], start, "mod0")
    big["a_w_in"] = fetch("a_w_in", h0)
    uvpre = mm(h0, big["a_w_in"], mode="nn", name="a_in", outs=[F32], tn=512, tk=1024,
               epi=lambda r, bias: [r + bias], extras=[("row", small["a_b_in"])])
    gated = _spatial_fwd(uvpre, small["a_vn_g"], small["a_vn_b"], small["wc"], small["bias_full"], "a_spatial")
    big["a_w_out"] = fetch("a_w_out", gated)
    ys.append(mm(gated, big["a_w_out"], mode="nn", name="a_out", outs=[F32], tn=1024, tk=1024))
    x1, h1 = _resid_ln(xs[0], ys[0], gate[0], lg[0], lb[0], (scale[1], shift[1]), "ln0")
    xs.append(x1)
    relu2 = lambda r: [jnp.square(jnp.maximum(r, 0.0))]
    big["up0"] = fetch("up0", h1)
    r0 = mm(h1, big["up0"], mode="nn", name="up0", outs=[MXU_DTYPE], tn=1024, tk=1024, epi=relu2)
    big["down0"] = fetch("down0", r0)
    ys.append(mm(r0, big["down0"], mode="nn", name="down0", outs=[F32], tm=min(s, 1024), tn=1024, tk=2048))
    dils = [dil for _, dil in B_PATTERNS]
    x2, h2, *h2_streams = _resid_ln(xs[1], ys[1], gate[1], lg[1], lb[1], (scale[2], shift[2]), "ln1", [dil for dil in dils if dil > 1])
    h2_streams = [h2] + [a.reshape(s, d) for a in h2_streams]
    xs.append(x2)
    hg, qkvs, o_g, l_g, l_streams = [], [], [], [], []
    big["b_w_qkv"] = fetch("b_w_qkv", h2)
    for g, (_, dil) in enumerate(B_PATTERNS):
        hp = h2_streams[g]
        qkv = mm(hp, big["b_w_qkv"], mode="nn", name=f"qkv{g}", outs=[MXU_DTYPE], tn=768, tk=1024, b_col0=g * 3 * d, n_out=3 * d)
        og, lgv = _attn_fwd(qkv, small["slopes"], dil, f"attn_fwd{g}")
        hg.append(hp)
        qkvs.append(qkv)
        o_g.append(og if dil == 1 else og.reshape(dil, s // dil, d))
        l_g.append(_from_streams(lgv, dil))
        l_streams.append(lgv)
    o_mix = _combine_fwd(o_g, l_g, "combine")
    big["b_w_out"] = fetch("b_w_out", o_mix)
    ys.append(mm(o_mix, big["b_w_out"], mode="nn", name="b_out", outs=[F32], tn=1024, tk=1024))
    x3, h3 = _resid_ln(xs[2], ys[2], gate[2], lg[2], lb[2], (scale[3], shift[3]), "ln2")
    xs.append(x3)
    big["up1"] = fetch("up1", h3)
    r1 = mm(h3, big["up1"], mode="nn", name="up1", outs=[MXU_DTYPE], tn=1024, tk=1024, epi=relu2)
    big["down1"] = fetch("down1", r1)
    ys.append(mm(r1, big["down1"], mode="nn", name="down1", outs=[F32], tm=min(s, 1024), tn=1024, tk=2048))

    gb, red_ln, red_mod = {}, [None] * 4, [None] * 4

    def mlp_bwd(i, h, r, dyy):
        gb[f"down{i}"] = mm_w(r, dyy, name=f"g_down{i}", outs=[MXU_DTYPE], tn=1024)
        da = mm(dyy, big[f"down{i}"], mode="nt", name=f"d_down{i}", outs=[MXU_DTYPE], tn=1024, tk=1024,
                after=emit(f"down{i}", gb[f"down{i}"]),
                epi=lambda acc, rv: [acc * (2.0 * jnp.sqrt(rv.astype(F32)))], extras=[("full", r)])
        gb[f"up{i}"] = mm_w(h, da, name=f"g_up{i}", outs=[MXU_DTYPE], tn=1024)
        return [mm(da, big[f"up{i}"], mode="nt", name=f"d_up{i}", outs=[F32], tn=1024, tk=1024, after=emit(f"up{i}", gb[f"up{i}"]))]

    def join(sub, dxr, dhs, after=None):
        res = _mod_ln_bwd(dxr, dhs, xs[sub], scale[sub], xs[sub - 1], ys[sub - 1], gate[sub - 1], lg[sub - 1],
                          f"mod_ln_bwd{sub}", after=after)
        red_mod[sub], red_ln[sub - 1] = res[2], res[3]
        return res[0], res[1]

    loss, dxr, dyy, red_ln[3] = _last_ln_loss_bwd(xs[3], ys[3], gate[3], lg[3], lb[3], target, "ln3_loss_bwd")
    dxr, dyy = join(3, dxr, mlp_bwd(1, h3, r1, dyy))
    gb["b_w_out"] = mm_w(o_mix, dyy, name="g_b_out", outs=[MXU_DTYPE], tn=1024, tk=1024)
    do = mm(dyy, big["b_w_out"], mode="nt", name="d_b_out", outs=[F32], tn=1024, tk=1024, after=emit("b_w_out", gb["b_w_out"]))
    parts = _combine_bwd(do, o_mix, l_g, dils, "combine_bwd")
    dhs, gq = [], None
    for g, (_, dil) in enumerate(B_PATTERNS):
        do_g, dd_g = parts[g][0].reshape(s, d), _to_streams(parts[g][1], dil)
        dqkv = _attn_bwd(qkvs[g], do_g, l_streams[g], dd_g, small["slopes"], dil, f"attn_bwd{g}")
        gq = mm_w(hg[g], dqkv, name=f"g_qkv{g}", outs=[MXU_DTYPE], tn=1024, out_col0=g * 3 * d, out_cols=len(B_PATTERNS) * 3 * d, into=gq)
        dh = mm(dqkv, big["b_w_qkv"], mode="nt", name=f"d_qkv{g}", outs=[F32], tn=1024, tk=768, b_col0=g * 3 * d)
        dhs.append(dh if dil == 1 else dh.reshape(dil, s // dil, d))
    gb["b_w_qkv"] = gq
    dxr, dyy = join(2, dxr, dhs, after=emit("b_w_qkv", gb["b_w_qkv"]))
    dxr, dyy = join(1, dxr, mlp_bwd(0, h1, r0, dyy))
    gb["a_w_out"] = mm_w(gated, dyy, name="g_a_out", outs=[MXU_DTYPE], tn=1024)
    dgated = mm(dyy, big["a_w_out"], mode="nt", name="d_a_out", outs=[F32], tn=1024, tk=1024, after=emit("a_w_out", gb["a_w_out"]))
    duv, dws, dbias, dbin, dvg, dvb = _spatial_bwd(uvpre, dgated, small["a_vn_g"], small["a_vn_b"], small["wc"],
                                                   small["wct"], small["bias_full"], "a_spatial_bwd")
    tril = jnp.tril(jnp.ones((CHUNK, CHUNK), bool))
    dws = jnp.where(tril, dws, 0.0).reshape(-1, LANES)
    gb["a_w_in"] = mm_w(h0, duv, name="g_a_in", outs=[MXU_DTYPE], tn=1024, after=emit("a_w_s", dws.astype(MXU_DTYPE)))
    dh = mm(duv, big["a_w_in"], mode="nt", name="d_a_in", outs=[F32], tn=1024, tk=512, after=emit("a_w_in", gb["a_w_in"]))
    dx, red_mod[0] = _mod_bwd(dxr, [dh], xs[0], scale[0], "mod_bwd0")
    dm = [jnp.concatenate([red_mod[i][0], red_mod[i][1], red_ln[i][2]]) for i in range(4)]
    dlg, dlb = [red_ln[i][0] for i in range(4)], [red_ln[i][1] for i in range(4)]

    gsmall = {
        "a_b_in": dbin.reshape(-1), "a_vn_g": dvg.reshape(-1), "a_vn_b": dvb.reshape(-1),
        "a_w_s": dws.reshape(-1),
        "a_b_s": dbias.reshape(CHUNK, A_GROUPS, d // A_GROUPS).sum(-1).T.reshape(-1),
    }
    return loss, dx, gb, jnp.stack(dm), jnp.stack(dlg), jnp.stack(dlb), gsmall


BIG = ("a_w_in", "a_w_out", "up0", "down0", "b_w_qkv", "b_w_out", "up1", "down1")
BIG_KIND = {"a_w_in": "col", "a_w_out": "row", "b_w_qkv": "col", "b_w_out": "row",
            "up0": "col", "up1": "col", "down0": "row", "down1": "row", "a_w_s": "all"}
HALVED = ("a_w_in", "down0", "b_w_qkv")
SCATTER_GROUPS = (("down1", "up1"), ("b_w_out", "b_w_qkv"), ("down0", "up0"), ("a_w_out", "a_w_in"), ("a_w_s",))
SMALL = ("a_b_in", "a_vn_g", "a_vn_b", "a_b_s")


def kernel(x, c, ada_w, ada_b, ln_g, ln_b, a_w_in, a_b_in, a_vn_g, a_vn_b, a_w_s, a_b_s, a_w_out, b_w_qkv, b_w_out, mlp_w_up, mlp_w_down, loss_target, m_ada_w, m_ada_b, m_ln_g, m_ln_b, m_a_w_in, m_a_b_in, m_a_vn_g, m_a_vn_b, m_a_w_s, m_a_b_s, m_a_w_out, m_b_w_qkv, m_b_w_out, m_mlp_w_up, m_mlp_w_down, v_ada_w, v_ada_b, v_ln_g, v_ln_b, v_a_w_in, v_a_b_in, v_a_vn_g, v_a_vn_b, v_a_w_s, v_a_b_s, v_a_w_out, v_b_w_qkv, v_b_w_out, v_mlp_w_up, v_mlp_w_down):
    s, d = x.shape[1], x.shape[2]
    xi, yi, ci = _me()
    q = 2 * xi + yi
    dev = 2 * q + ci
    nsub = 2 * DEPTH
    cs = ada_w.shape[-1]
    ls = ln_g.shape[-1]

    shards = {
        "a_w_in": a_w_in[0], "a_w_out": a_w_out[0], "b_w_qkv": b_w_qkv[0], "b_w_out": b_w_out[0],
        "up0": mlp_w_up[0], "up1": mlp_w_up[1], "down0": mlp_w_down[0], "down1": mlp_w_down[1],
    }
    cast = [shards[k].astype(MXU_DTYPE) for k in BIG]

    pack = jnp.concatenate([c.reshape(-1), ln_g.reshape(-1), ln_b.reshape(-1)]).reshape(-1, LANES)
    got = _all_gather_small(pack, "gather_small", after=cast).reshape(N_DEV, -1)
    c_all = got[:, :d]
    per_chip = got[0::2]
    ln_g_full = per_chip[:, d:d + nsub * ls].reshape(N_CHIPS, nsub, ls).transpose(1, 0, 2).reshape(nsub, d)
    ln_b_full = per_chip[:, d + nsub * ls:].reshape(N_CHIPS, nsub, ls).transpose(1, 0, 2).reshape(nsub, d)
    m_part = _ada_fwd(c_all, ada_w.reshape(nsub, d, cs), ada_b.reshape(nsub, 1, cs), "ada_fwd")
    m_all = _all_gather_small(m_part.reshape(-1, LANES), "gather_mod").reshape(N_DEV, nsub, N_DEV, cs)
    m_mine = lax.dynamic_index_in_dim(m_all[0::2], dev, axis=2, keepdims=False)
    mvec = m_mine.transpose(1, 0, 2).reshape(nsub, 3 * d)

    halved = {BIG.index(k) for k in HALVED}
    send_sems, recv_sems, shard_thru, lands, token = _gather_start(cast, halved, mvec, "gather_start")

    def fetch(k, after):
        w = BIG.index(k)
        shard, gw = _gather_wait(w, shard_thru[w], lands[w], send_sems, recv_sems, after, f"gather_wait_{k}", w in halved)
        if w in halved:
            gw = _assemble_halves(shard, gw, f"assemble_{k}")
        return gw if BIG_KIND[k] == "col" else gw.reshape(1, -1, gw.shape[-1])

    scattering, pending = {}, {}

    def emit(k, g):
        pending[k] = g
        group = next(gr for gr in SCATTER_GROUPS if k in gr)
        if k != group[-1]:
            return None
        scattering[group] = _scatter_start([pending[m] for m in group], [BIG_KIND[m] for m in group], f"scatter_start_{k}")
        return scattering[group][2][0]

    tril = jnp.tril(jnp.ones((CHUNK, CHUNK), bool))
    wc = jnp.where(tril, a_w_s[0], 0.0).astype(MXU_DTYPE)
    heads = jnp.arange(1, B_HEADS + 1, dtype=F32)
    small = {
        "a_b_in": a_b_in, "a_vn_g": a_vn_g, "a_vn_b": a_vn_b,
        "wc": wc, "wct": wc.transpose(0, 2, 1),
        "bias_full": jnp.repeat(a_b_s[0].T, d // A_GROUPS, axis=1),
        "slopes": jnp.exp2(-8.0 * heads / B_HEADS),
    }

    loss_part, grad_x, gb, dm, dlg, dlb, gsmall = _local_step(x[0], loss_target[0], mvec, ln_g_full, ln_b_full, small, fetch, emit, token)
    loss = lax.psum(loss_part, ("x", "y", "c"))

    weights = dict(ada_w=ada_w, ada_b=ada_b, ln_g=ln_g, ln_b=ln_b, a_w_in=a_w_in, a_b_in=a_b_in, a_vn_g=a_vn_g, a_vn_b=a_vn_b,
                   a_w_s=a_w_s, a_b_s=a_b_s, a_w_out=a_w_out, b_w_qkv=b_w_qkv, b_w_out=b_w_out, mlp_w_up=mlp_w_up, mlp_w_down=mlp_w_down)
    ms = dict(ada_w=m_ada_w, ada_b=m_ada_b, ln_g=m_ln_g, ln_b=m_ln_b, a_w_in=m_a_w_in, a_b_in=m_a_b_in, a_vn_g=m_a_vn_g, a_vn_b=m_a_vn_b,
              a_w_s=m_a_w_s, a_b_s=m_a_b_s, a_w_out=m_a_w_out, b_w_qkv=m_b_w_qkv, b_w_out=m_b_w_out, mlp_w_up=m_mlp_w_up, mlp_w_down=m_mlp_w_down)
    vs = dict(ada_w=v_ada_w, ada_b=v_ada_b, ln_g=v_ln_g, ln_b=v_ln_b, a_w_in=v_a_w_in, a_b_in=v_a_b_in, a_vn_g=v_a_vn_g, a_vn_b=v_a_vn_b,
              a_w_s=v_a_w_s, a_b_s=v_a_b_s, a_w_out=v_a_w_out, b_w_qkv=v_b_w_qkv, b_w_out=v_b_w_out, mlp_w_up=v_mlp_w_up, mlp_w_down=v_mlp_w_down)
    grads, updates = {}, {}

    def update(k):
        updates[k] = _adamw(weights[k], grads[k], ms[k], vs[k], f"adamw_{k}")
        return updates[k][0]

    gfull = {}

    def big_group(group, after):
        bufs = []
        for pair in (group[:2], group[2:]):
            bufs += _scatter_wait(*scattering[pair], [BIG_KIND[m] for m in pair], after, f"scatter_wait_{pair[-1]}")
        halves = [_sum_slots(b, f"sum_{k}") for k, b in zip(group, bufs)]
        fulls = _swap_halves(halves, f"swap_halves_{group[0]}")
        gfull.update({k: f.reshape(-1, f.shape[-1]) for k, f in zip(group, fulls)})

    big_group(SCATTER_GROUPS[0] + SCATTER_GROUPS[1], grad_x)
    grads["b_w_qkv"], grads["b_w_out"] = gfull["b_w_qkv"][None], gfull["b_w_out"][None]
    update("b_w_out")
    done = update("b_w_qkv")

    pack_b = jnp.concatenate([dm.reshape(-1), dlg.reshape(-1), dlb.reshape(-1)] + [gsmall[k] for k in SMALL])
    n_small = pack_b.shape[0]
    pack_b = jnp.pad(pack_b, (0, -n_small % (256 * LANES)))
    got_b = _all_gather_small(pack_b.reshape(-1, LANES), "gather_small_grads", after=[done]).reshape(N_DEV, -1, LANES)
    tot = _sum_slots(got_b, "sum_small").reshape(-1)
    o = 0
    dm_tot = tot[o:o + nsub * 3 * d].reshape(nsub, 3 * d); o += nsub * 3 * d
    dlg_tot = tot[o:o + nsub * d].reshape(nsub, d); o += nsub * d
    dlb_tot = tot[o:o + nsub * d].reshape(nsub, d); o += nsub * d
    g_small = {}
    for k, ref in zip(SMALL, (a_b_in, a_vn_g, a_vn_b, a_b_s)):
        g_small[k] = tot[o:o + ref.size].reshape(ref.shape); o += ref.size
    assert o == n_small
    aws = _scatter_wait(*scattering[("a_w_s",)], ["all"], tot, "scatter_wait_a_w_s")[0]
    g_small["a_w_s"] = _sum_slots(aws, "sum_a_w_s").reshape(a_w_s.shape)
    dm_all = got_b.reshape(N_DEV, -1)[:, :nsub * 3 * d].reshape(N_DEV, nsub, 3 * d)
    dm_cols = lax.dynamic_slice_in_dim(dm_all, q * cs, cs, axis=2).transpose(1, 0, 2)
    grads.update({
        "ada_w": _ada_bwd(c_all.T, dm_cols, "ada_bwd").reshape(ada_w.shape),
        "ada_b": lax.dynamic_slice_in_dim(dm_tot, q * cs, cs, axis=1).reshape(ada_b.shape),
        "ln_g": lax.dynamic_slice_in_dim(dlg_tot, q * ls, ls, axis=1).reshape(ln_g.shape),
        "ln_b": lax.dynamic_slice_in_dim(dlb_tot, q * ls, ls, axis=1).reshape(ln_b.shape),
        **g_small,
    })
    for k in ("ada_b", "ln_g", "ln_b", "a_w_s") + SMALL:
        update(k)
    done = update("ada_w")

    big_group(SCATTER_GROUPS[2] + SCATTER_GROUPS[3], done)
    grads.update({
        "a_w_in": gfull["a_w_in"][None], "a_w_out": gfull["a_w_out"][None],
        "mlp_w_up": jnp.stack([gfull["up0"], gfull["up1"]]), "mlp_w_down": jnp.stack([gfull["down0"], gfull["down1"]]),
    })
    for k in ("a_w_in", "a_w_out", "mlp_w_up", "mlp_w_down"):
        update(k)
    names = list(weights)
    return (loss, grad_x[None], *[grads[k] for k in names], *[updates[k][0] for k in names],
            *[updates[k][1] for k in names], *[updates[k][2] for k in names])
```

```python
import functools
import math

import jax
import jax.numpy as jnp
from jax import lax
from jax.experimental import pallas as pl
from jax.experimental.pallas import tpu as pltpu

F32 = jnp.float32
MXU_DTYPE = jnp.bfloat16

DEPTH = 2
CHUNK = 128
A_GROUPS = 16
B_HEADS = 16
HEAD_DIM = 64
B_PATTERNS = ((128, 1), (512, 4), (2048, 16))
SPAN = 128
ALPHA = (2 * DEPTH) ** 0.25
LN_EPS = 1e-5
NEG = -1e30
ATT_SCALE = HEAD_DIM ** -0.5
ADAM_LR, ADAM_B1, ADAM_B2, ADAM_EPS, ADAM_WD, ADAM_STEP = 0.001, 0.9, 0.999, 1e-08, 0.01, 10

N_CHIPS = 4
N_DEV = 8
LANES = 128
SUBLANES = 8
VMEM_LIMIT = 52 * 1024 * 1024
ROW_TILE = 512
MM_ROW_CHUNK = 256
MESH = pl.DeviceIdType.MESH


def _cparams(sem):
    return pltpu.CompilerParams(dimension_semantics=sem, vmem_limit_bytes=VMEM_LIMIT)


def _fold8(v):
    r, c = v.shape
    return jnp.sum(v.reshape(r // SUBLANES, SUBLANES, c), axis=0)


def _gelu(x):
    c = math.sqrt(2.0 / math.pi)
    return 0.5 * x * (1.0 + jnp.tanh(c * (x + 0.044715 * (x * x * x))))


def _gelu_and_grad(x):
    c = math.sqrt(2.0 / math.pi)
    t = jnp.tanh(c * (x + 0.044715 * (x * x * x)))
    return 0.5 * x * (1.0 + t), 0.5 * (1.0 + t) + 0.5 * x * (1.0 - t * t) * c * (1.0 + 3.0 * 0.044715 * x * x)


def _dot(a, b, dims):
    return lax.dot_general(a.astype(MXU_DTYPE), b.astype(MXU_DTYPE), (dims, ((), ())), preferred_element_type=F32)


def _dot_nn(a, b):
    return _dot(a, b, ((1,), (0,)))


def _dot_nt(a, b):
    return _dot(a, b, ((1,), (1,)))


def _dot_tn(a, b):
    return _dot(a, b, ((0,), (0,)))


def _mm(a, b, *, mode, name, outs, tm, tn, tk, epi=None, extras=(), b_col0=0, n_out=None, after=None,
        out_col0=0, out_cols=None, into=None):
    if mode == "nn":
        m, kdim = a.shape
        p, kb, ns = b.shape
        assert kb == kdim and ns % tn == 0 and b_col0 % tn == 0
        n = n_out if n_out is not None else p * ns
        npt, j0 = ns // tn, b_col0 // tn
        a_spec = pl.BlockSpec((tm, tk), lambda i, j, k: (i, k))
        b_spec = pl.BlockSpec((None, tk, tn), lambda i, j, k: ((j + j0) // npt, k, (j + j0) % npt))
        dot = _dot_nn
    elif mode == "nt":
        m, kdim = a.shape
        p, n, ns = b.shape
        assert ns % tk == 0 and b_col0 % tk == 0
        npt, j0 = ns // tk, b_col0 // tk
        a_spec = pl.BlockSpec((tm, tk), lambda i, j, k: (i, k))
        b_spec = pl.BlockSpec((None, tn, tk), lambda i, j, k: ((k + j0) // npt, j, (k + j0) % npt))
        dot = _dot_nt
    else:
        kdim, m = a.shape
        kb, n = b.shape
        assert kb == kdim
        a_spec = pl.BlockSpec((tk, tm), lambda i, j, k: (k, i))
        b_spec = pl.BlockSpec((tk, tn), lambda i, j, k: (k, j))
        dot = _dot_tn
    assert m % tm == 0 and n % tn == 0 and kdim % tk == 0, (name, m, n, kdim, tm, tn, tk)
    nk = kdim // tk
    ex_specs, ex_arrays = [], []
    for kind, arr in extras:
        if kind == "row":
            ex_specs.append(pl.BlockSpec((1, tn), lambda i, j, k: (0, j)))
        else:
            ex_specs.append(pl.BlockSpec((tm, tn), lambda i, j, k: (i, j)))
        ex_arrays.append(arr)
    n_ex, n_o = len(ex_arrays), len(outs)
    deps = [d for d in (after, into) if d is not None]
    n_dep = len(deps)
    j_out = out_col0 // tn
    assert out_col0 % tn == 0 and (into is None or len(outs) == 1)

    def body(a_ref, b_ref, *rest):
        ex_refs, o_refs = rest[:n_ex], rest[n_ex + n_dep:n_ex + n_dep + n_o]
        k = pl.program_id(2)

        chunks = [slice(r0, r0 + min(tm, MM_ROW_CHUNK)) for r0 in range(0, tm, min(tm, MM_ROW_CHUNK))]

        def part(rows):
            return dot(a_ref[:, rows] if mode == "tn" else a_ref[rows, :], b_ref[...])

        def finish(r, rows):
            exs = [e[...] if kind == "row" else e[rows, :] for (kind, _), e in zip(extras, ex_refs)]
            vals = epi(r, *exs) if epi is not None else [r]
            for o, v in zip(o_refs, vals):
                o[rows, :] = v.astype(o.dtype)

        if nk == 1:
            for rows in chunks:
                finish(part(rows), rows)
            return
        acc = rest[n_ex + n_dep + n_o]

        @pl.when(k == 0)
        def _():
            for rows in chunks:
                acc[rows, :] = part(rows)

        @pl.when((k > 0) & (k < nk - 1))
        def _():
            for rows in chunks:
                acc[rows, :] += part(rows)

        @pl.when(k == nk - 1)
        def _():
            for rows in chunks:
                finish(acc[rows, :] + part(rows), rows)

    res = pl.pallas_call(
        body,
        grid=(m // tm, n // tn, nk),
        in_specs=[a_spec, b_spec] + ex_specs + [pl.BlockSpec(memory_space=pl.ANY)] * n_dep,
        out_specs=[pl.BlockSpec((tm, tn), lambda i, j, k: (i, j + j_out)) for _ in outs],
        out_shape=[jax.ShapeDtypeStruct((m, out_cols or n), dt) for dt in outs],
        input_output_aliases={} if into is None else {2 + n_ex + n_dep - 1: 0},
        scratch_shapes=[pltpu.VMEM((tm, tn), F32)] if nk > 1 else [],
        name=name,
        compiler_params=_cparams(("parallel", "parallel", "arbitrary")),
    )(a, b, *ex_arrays, *deps)
    return res if len(outs) > 1 else res[0]


def _rows(body, n_rows, tr, ins, outs, name, scratch=()):
    def spec(kind, shape):
        if kind == "blk":
            return pl.BlockSpec((tr,) + tuple(shape[1:]), lambda i: (i,) + (0,) * (len(shape) - 1))
        if kind == "dep":
            return pl.BlockSpec(memory_space=pl.ANY)
        if kind == "str":
            return pl.BlockSpec((shape[0], tr // shape[0], shape[2]), lambda i: (0, i, 0))
        return pl.BlockSpec(tuple(shape), lambda i: (0,) * len(shape))

    return pl.pallas_call(
        body,
        grid=(n_rows // tr,),
        in_specs=[spec(k, a.shape) for k, a in ins],
        out_specs=[spec(k, s) for k, s, _ in outs],
        out_shape=[jax.ShapeDtypeStruct(tuple(s), d) for _, s, d in outs],
        scratch_shapes=list(scratch),
        name=name,
        compiler_params=_cparams(("arbitrary",)),
    )(*[a for _, a in ins])


def _ln_stats(z):
    mu = jnp.mean(z, axis=-1, keepdims=True)
    zc = z - mu
    var = jnp.mean(zc * zc, axis=-1, keepdims=True)
    rstd = lax.rsqrt(var + LN_EPS)
    return zc * rstd, rstd


def _stream_scratch(c):
    return pltpu.VMEM((c // LANES, ROW_TILE, LANES), F32)


def _streams_in(ref3, scr):
    dil, n, c = ref3.shape
    for r in range(dil):
        for j in range(c // LANES):
            scr.at[j][pl.ds(r, n, stride=dil), :] = ref3[r, :, j * LANES:(j + 1) * LANES].astype(F32)
    return jnp.concatenate([scr[j] for j in range(c // LANES)], axis=1)


def _streams_out(val, ref3, scr):
    dil, n, c = ref3.shape
    for j in range(c // LANES):
        scr[j] = val[:, j * LANES:(j + 1) * LANES].astype(F32)
    for r in range(dil):
        for j in range(c // LANES):
            ref3[r, :, j * LANES:(j + 1) * LANES] = scr.at[j][pl.ds(r, n, stride=dil), :].astype(ref3.dtype)


def _mod(x, scale, shift, after, name):
    s, d = x.shape

    def body(x_ref, sc_ref, sh_ref, dep_ref, h_ref):
        h_ref[...] = (x_ref[...] * (1.0 + sc_ref[...]) + sh_ref[...]).astype(h_ref.dtype)

    return _rows(body, s, ROW_TILE, [("blk", x), ("all", scale), ("all", shift), ("dep", after)], [("blk", (s, d), MXU_DTYPE)], name)[0]


def _resid_ln(x, y, gate, g, b, nxt, name, dils=()):
    s, d = x.shape

    def body(x_ref, y_ref, gate_ref, g_ref, b_ref, sc_ref, sh_ref, xn_ref, h_ref, *rest):
        z = ALPHA * x_ref[...] + gate_ref[...] * y_ref[...]
        xhat, _ = _ln_stats(z)
        xn = xhat * g_ref[...] + b_ref[...]
        xn_ref[...] = xn
        h = xn * (1.0 + sc_ref[...]) + sh_ref[...]
        h_ref[...] = h.astype(h_ref.dtype)
        for hs_ref in rest[:len(dils)]:
            _streams_out(h, hs_ref, rest[-1])

    return _rows(body, s, ROW_TILE,
                 [("blk", x), ("blk", y), ("all", gate), ("all", g), ("all", b), ("all", nxt[0]), ("all", nxt[1])],
                 [("blk", (s, d), F32), ("blk", (s, d), MXU_DTYPE)] + [("str", (dil, s // dil, d), MXU_DTYPE) for dil in dils], name,
                 scratch=[_stream_scratch(d)] if dils else [])


def _mod_bwd(dxr, dhs, x, scale, name, after=None):
    s, d = x.shape
    n_dh = len(dhs)
    n_dep = 0 if after is None else 1

    def body(dxr_ref, *rest):
        dh_refs = rest[:n_dh]
        x_ref, sc_ref, dx_ref, red_ref, a_sh, a_sc = rest[n_dh:n_dh + 2] + rest[n_dh + 2 + n_dep:]
        i = pl.program_id(0)

        @pl.when(i == 0)
        def _():
            a_sh[...] = jnp.zeros_like(a_sh)
            a_sc[...] = jnp.zeros_like(a_sc)

        dh = dh_refs[0][...]
        for r in dh_refs[1:]:
            dh = dh + r[...]
        dx_ref[...] = dxr_ref[...] + dh * (1.0 + sc_ref[...])
        a_sh[...] += _fold8(dh)
        a_sc[...] += _fold8(dh * x_ref[...])

        @pl.when(i == pl.num_programs(0) - 1)
        def _():
            red_ref[...] = jnp.zeros_like(red_ref)
            red_ref[0:1, :] = jnp.sum(a_sh[...], axis=0, keepdims=True)
            red_ref[1:2, :] = jnp.sum(a_sc[...], axis=0, keepdims=True)

    return _rows(body, s, ROW_TILE, [("blk", dxr)] + [("blk", h) for h in dhs] + [("blk", x), ("all", scale)] + [("dep", after)] * n_dep,
                 [("blk", (s, d), F32), ("all", (SUBLANES, d), F32)], name,
                 scratch=[pltpu.VMEM((SUBLANES, d), F32)] * 2)


def _last_ln_loss_bwd(x, y, gate, g, b, target, name):
    s, d = x.shape

    def body(x_ref, y_ref, gate_ref, g_ref, b_ref, t_ref, l_ref, dxr_ref, dyy_ref, red_ref, a_l, a_g, a_b, a_gate):
        i = pl.program_id(0)

        @pl.when(i == 0)
        def _():
            for a in (a_l, a_g, a_b, a_gate):
                a[...] = jnp.zeros_like(a)

        yv = y_ref[...]
        z = ALPHA * x_ref[...] + gate_ref[...] * yv
        xhat, rstd = _ln_stats(z)
        e = xhat * g_ref[...] + b_ref[...] - t_ref[...]
        a_l[...] += _fold8(e * e)
        dxo_v = e * (1.0 / d)
        dxh = dxo_v * g_ref[...]
        dz = rstd * (dxh - jnp.mean(dxh, axis=-1, keepdims=True) - xhat * jnp.mean(dxh * xhat, axis=-1, keepdims=True))
        dxr_ref[...] = ALPHA * dz
        dyy_ref[...] = (gate_ref[...] * dz).astype(dyy_ref.dtype)
        a_g[...] += _fold8(dxo_v * xhat)
        a_b[...] += _fold8(dxo_v)
        a_gate[...] += _fold8(dz * yv)

        @pl.when(i == pl.num_programs(0) - 1)
        def _():
            l_ref[...] = jnp.full(l_ref.shape, 0.5 / d, F32) * jnp.sum(a_l[...])
            red_ref[...] = jnp.zeros_like(red_ref)
            red_ref[0:1, :] = jnp.sum(a_g[...], axis=0, keepdims=True)
            red_ref[1:2, :] = jnp.sum(a_b[...], axis=0, keepdims=True)
            red_ref[2:3, :] = jnp.sum(a_gate[...], axis=0, keepdims=True)

    l, dxr, dyy, red = _rows(
        body, s, ROW_TILE, [("blk", x), ("blk", y), ("all", gate), ("all", g), ("all", b), ("blk", target)],
        [("all", (SUBLANES, LANES), F32), ("blk", (s, d), F32), ("blk", (s, d), MXU_DTYPE), ("all", (SUBLANES, d), F32)], name,
        scratch=[pltpu.VMEM((SUBLANES, d), F32)] * 4)
    return l[0, 0], dxr, dyy, red


def _mod_ln_bwd(dxr, dhs, x, scale, x_in, y, gate, g, name, after=None):
    s, d = x.shape
    n_dh = len(dhs)
    n_dep = 0 if after is None else 1

    def body(dxr_ref, *rest):
        dh_refs = rest[:n_dh]
        x_ref, sc_ref, xin_ref, y_ref, gate_ref, g_ref = rest[n_dh:n_dh + 6]
        dxr_out, dyy_ref, red_mod, red_ln, a_sh, a_sc, a_g, a_b, a_gate = rest[n_dh + 6 + n_dep:n_dh + 15 + n_dep]
        i = pl.program_id(0)

        @pl.when(i == 0)
        def _():
            for a in (a_sh, a_sc, a_g, a_b, a_gate):
                a[...] = jnp.zeros_like(a)

        dh = dh_refs[0][...]
        for r in dh_refs[1:]:
            dh = dh + (r[...] if len(r.shape) == 2 else _streams_in(r, rest[-1]))
        xv = x_ref[...]
        dxo_v = dxr_ref[...] + dh * (1.0 + sc_ref[...])
        a_sh[...] += _fold8(dh)
        a_sc[...] += _fold8(dh * xv)
        yv = y_ref[...]
        z = ALPHA * xin_ref[...] + gate_ref[...] * yv
        xhat, rstd = _ln_stats(z)
        dxh = dxo_v * g_ref[...]
        dz = rstd * (dxh - jnp.mean(dxh, axis=-1, keepdims=True) - xhat * jnp.mean(dxh * xhat, axis=-1, keepdims=True))
        dxr_out[...] = ALPHA * dz
        dyy_ref[...] = (gate_ref[...] * dz).astype(dyy_ref.dtype)
        a_g[...] += _fold8(dxo_v * xhat)
        a_b[...] += _fold8(dxo_v)
        a_gate[...] += _fold8(dz * yv)

        @pl.when(i == pl.num_programs(0) - 1)
        def _():
            red_mod[...] = jnp.zeros_like(red_mod)
            red_mod[0:1, :] = jnp.sum(a_sh[...], axis=0, keepdims=True)
            red_mod[1:2, :] = jnp.sum(a_sc[...], axis=0, keepdims=True)
            red_ln[...] = jnp.zeros_like(red_ln)
            red_ln[0:1, :] = jnp.sum(a_g[...], axis=0, keepdims=True)
            red_ln[1:2, :] = jnp.sum(a_b[...], axis=0, keepdims=True)
            red_ln[2:3, :] = jnp.sum(a_gate[...], axis=0, keepdims=True)

    ins = ([("blk", dxr)] + [("blk" if h.ndim == 2 else "str", h) for h in dhs]
           + [("blk", x), ("all", scale), ("blk", x_in), ("blk", y), ("all", gate), ("all", g)] + [("dep", after)] * n_dep)
    return _rows(body, s, ROW_TILE, ins,
                 [("blk", (s, d), F32), ("blk", (s, d), MXU_DTYPE), ("all", (SUBLANES, d), F32), ("all", (SUBLANES, d), F32)], name,
                 scratch=[pltpu.VMEM((SUBLANES, d), F32)] * 5 + [_stream_scratch(d)] * any(h.ndim == 3 for h in dhs))


def _left_half(shape):
    return lax.broadcasted_iota(jnp.int32, shape, 1) < (LANES // 2)


CHUNKS_PER_STEP = 2


def _chunks_of_step():
    return [slice(i * CHUNK, (i + 1) * CHUNK) for i in range(CHUNKS_PER_STEP)]


def _spatial_z(vn, wc_ref, bias_ref, j):
    vb = vn[:, j * LANES:(j + 1) * LANES]
    z0 = _dot_nn(wc_ref[2 * j], vb)
    z1 = _dot_nn(wc_ref[2 * j + 1], vb)
    return jnp.where(_left_half(z0.shape), z0, z1) + bias_ref[:, j * LANES:(j + 1) * LANES]


def _spatial_fwd(uvpre, vn_g, vn_b, wc, bias_full, name):
    s, d2 = uvpre.shape
    d = d2 // 2

    def body(uv_ref, g_ref, b_ref, wc_ref, bias_ref, out_ref):
        for rows in _chunks_of_step():
            u = _gelu(uv_ref[rows, :d])
            v = _gelu(uv_ref[rows, d:])
            vh, _ = _ln_stats(v)
            vn = vh * g_ref[...] + b_ref[...]
            for j in range(d // LANES):
                z = _spatial_z(vn, wc_ref, bias_ref, j)
                out_ref[rows, j * LANES:(j + 1) * LANES] = (u[:, j * LANES:(j + 1) * LANES] * z).astype(out_ref.dtype)

    return _rows(body, s, CHUNKS_PER_STEP * CHUNK, [("blk", uvpre), ("all", vn_g), ("all", vn_b), ("all", wc), ("all", bias_full)],
                 [("blk", (s, d), MXU_DTYPE)], name)[0]


def _spatial_bwd(uvpre, dgated, vn_g, vn_b, wc, wct, bias_full, name):
    s, d2 = uvpre.shape
    d = d2 // 2

    def body(uv_ref, dg_ref, g_ref, b_ref, wc_ref, wct_ref, bias_ref,
             duv_ref, dws_ref, dbias_ref, dbin_ref, dvg_ref, dvb_ref, dvn_buf, a_bin, a_vg, a_vb):
        i = pl.program_id(0)

        @pl.when(i == 0)
        def _():
            dws_ref[...] = jnp.zeros_like(dws_ref)
            dbias_ref[...] = jnp.zeros_like(dbias_ref)
            a_bin[...] = jnp.zeros_like(a_bin)
            a_vg[...] = jnp.zeros_like(a_vg)
            a_vb[...] = jnp.zeros_like(a_vb)

        for rows in _chunks_of_step():
            u, u_grad = _gelu_and_grad(uv_ref[rows, :d])
            v, v_grad = _gelu_and_grad(uv_ref[rows, d:])
            vh, rstd = _ln_stats(v)
            vn = vh * g_ref[...] + b_ref[...]
            dg = dg_ref[rows, :]
            dzz = dg * u
            dbias_ref[...] += dzz
            for j in range(d // LANES):
                cols = slice(j * LANES, (j + 1) * LANES)
                z = _spatial_z(vn, wc_ref, bias_ref, j)
                dup = dg[:, cols] * z * u_grad[:, cols]
                duv_ref[rows, cols] = dup.astype(duv_ref.dtype)
                a_bin[:, cols] += _fold8(dup)
                dzb = dzz[:, cols]
                left = _left_half(dzb.shape)
                dvn_buf[:, cols] = jnp.where(left, _dot_nn(wct_ref[2 * j], dzb), _dot_nn(wct_ref[2 * j + 1], dzb))
                vb = vn[:, cols]
                dws_ref[2 * j] += _dot_nt(jnp.where(left, dzb, 0.0), vb)
                dws_ref[2 * j + 1] += _dot_nt(jnp.where(left, 0.0, dzb), vb)
            dvn = dvn_buf[...]
            a_vg[...] += _fold8(dvn * vh)
            a_vb[...] += _fold8(dvn)
            dvh = dvn * g_ref[...]
            dv = rstd * (dvh - jnp.mean(dvh, axis=-1, keepdims=True) - vh * jnp.mean(dvh * vh, axis=-1, keepdims=True))
            dvp = dv * v_grad
            duv_ref[rows, d:] = dvp.astype(duv_ref.dtype)
            a_bin[:, d:] += _fold8(dvp)

        @pl.when(i == pl.num_programs(0) - 1)
        def _():
            dbin_ref[...] = jnp.sum(a_bin[...], axis=0, keepdims=True)
            dvg_ref[...] = jnp.sum(a_vg[...], axis=0, keepdims=True)
            dvb_ref[...] = jnp.sum(a_vb[...], axis=0, keepdims=True)

    return _rows(body, s, CHUNKS_PER_STEP * CHUNK,
                 [("blk", uvpre), ("blk", dgated), ("all", vn_g), ("all", vn_b), ("all", wc), ("all", wct), ("all", bias_full)],
                 [("blk", (s, d2), MXU_DTYPE), ("all", (A_GROUPS, CHUNK, CHUNK), F32), ("all", (CHUNK, d), F32),
                  ("all", (1, d2), F32), ("all", (1, d), F32), ("all", (1, d), F32)], name,
                 scratch=[pltpu.VMEM((CHUNK, d), F32), pltpu.VMEM((SUBLANES, d2), F32),
                          pltpu.VMEM((SUBLANES, d), F32), pltpu.VMEM((SUBLANES, d), F32)])


def _head_mask(v, h):
    lane = lax.broadcasted_iota(jnp.int32, v.shape, 1)
    return jnp.where((lane >= h * HEAD_DIM) & (lane < (h + 1) * HEAD_DIM), v, jnp.zeros_like(v))


def _att_bias(slopes, dil):
    qi = lax.broadcasted_iota(jnp.int32, (SPAN, SPAN), 0)
    ki = lax.broadcasted_iota(jnp.int32, (SPAN, SPAN), 1)
    sl = slopes[:, None, None]
    cur = jnp.where(ki <= qi, -sl * (float(dil) * (qi - ki).astype(F32)), NEG)
    prev = jnp.where(ki >= qi, -sl * (float(dil) * (SPAN + qi - ki).astype(F32)), NEG)
    absent = jnp.full_like(prev, NEG)
    pairs = slopes.shape[0] // 2

    def fwd(pv):
        return jnp.concatenate([cur, pv], axis=2).reshape(pairs, 2 * SPAN, 2 * SPAN)

    def bwd(pv):
        return jnp.concatenate([cur.reshape(pairs, 2 * SPAN, SPAN), pv.reshape(pairs, 2 * SPAN, SPAN)], axis=1)

    return jnp.stack([fwd(absent), fwd(prev)]), jnp.stack([bwd(absent), bwd(prev)])


ATT_GROUP = 4


def _att_group(s, dil):
    nb = s // (dil * SPAN)
    grp = min(ATT_GROUP, nb)
    assert nb % grp == 0
    return nb, grp


def _att_specs(s, d, dil, kinds):
    nb, grp = _att_group(s, dil)

    def spec(part, which):
        if which == "group":
            return pl.BlockSpec((grp * SPAN, d), lambda b: (b, part))
        if which == "prev":
            return pl.BlockSpec((SPAN, d), lambda b: (jnp.where((grp * b) % nb == 0, grp * b, grp * b - 1), part))
        return pl.BlockSpec((SPAN, d), lambda b: (jnp.where((grp * b + grp - 1) % nb == nb - 1, grp * b + grp - 1, grp * b + grp), part))

    return [spec(part, which) for part, which in kinds]


def _head_col(v, head):
    return v[:, head:head + 1]


def _expand_heads(w, j):
    shape = (w.shape[0], LANES)
    return jnp.where(_left_half(shape), jnp.broadcast_to(_head_col(w, 2 * j), shape), jnp.broadcast_to(_head_col(w, 2 * j + 1), shape))


def _attn_fwd(qkv, slopes, dil, name):
    s, d3 = qkv.shape
    d = d3 // 3
    nb, grp = _att_group(s, dil)
    table, _ = _att_bias(slopes, dil)

    def body(q_ref, k_ref, kp_ref, v_ref, vp_ref, tb_ref, o_ref, l_ref):
        b = pl.program_id(0)
        left = _left_half((SPAN, LANES))
        lane = lax.broadcasted_iota(jnp.int32, (SPAN, LANES), 1)
        for sub in range(grp):
            rows, before = slice(sub * SPAN, (sub + 1) * SPAN), slice((sub - 1) * SPAN, sub * SPAN)
            variant = jnp.where((grp * b) % nb == 0, 0, 1) if sub == 0 else 1
            lses = jnp.zeros((SPAN, LANES), F32)
            for hp in range(d // LANES):
                cols = slice(hp * LANES, (hp + 1) * LANES)
                q = q_ref[rows, cols]
                q2 = jnp.concatenate([_head_mask(q, 0), _head_mask(q, 1)], axis=0) * ATT_SCALE
                k2 = jnp.concatenate([k_ref[rows, cols], kp_ref[:, cols] if sub == 0 else k_ref[before, cols]], axis=0)
                v2 = jnp.concatenate([v_ref[rows, cols], vp_ref[:, cols] if sub == 0 else v_ref[before, cols]], axis=0)
                sc = _dot_nt(q2, k2) + tb_ref[variant, hp]
                m = jnp.max(sc, axis=-1, keepdims=True)
                p = jnp.exp(sc - m)
                l = jnp.sum(p, axis=-1, keepdims=True)
                r = _dot_nn(p, v2) * (1.0 / l)
                lse = m + jnp.log(l)
                o_ref[rows, cols] = jnp.where(left, r[:SPAN], r[SPAN:])
                lses = jnp.where(lane == 2 * hp, lse[:SPAN], jnp.where(lane == 2 * hp + 1, lse[SPAN:], lses))
            l_ref[rows, :] = lses

    specs = _att_specs(s, d, dil, [(0, "group"), (1, "group"), (1, "prev"), (2, "group"), (2, "prev")])
    return pl.pallas_call(
        body,
        grid=(s // (grp * SPAN),),
        in_specs=specs + [pl.BlockSpec(table.shape, lambda b: (0, 0, 0, 0))],
        out_specs=[pl.BlockSpec((grp * SPAN, d), lambda b: (b, 0)), pl.BlockSpec((grp * SPAN, LANES), lambda b: (b, 0))],
        out_shape=[jax.ShapeDtypeStruct((s, d), F32), jax.ShapeDtypeStruct((s, LANES), F32)],
        name=name,
        compiler_params=_cparams(("parallel",)),
    )(qkv, qkv, qkv, qkv, qkv, table)


def _attn_bwd(qkv, do, lse, dd, slopes, dil, name):
    s, d3 = qkv.shape
    d = d3 // 3
    nb, grp = _att_group(s, dil)
    _, table = _att_bias(slopes, dil)

    def heads_stacked(cur, nxt):
        return jnp.concatenate([_head_mask(cur, 0), _head_mask(cur, 1), _head_mask(nxt, 0), _head_mask(nxt, 1)], axis=0)

    def cols_stacked(cur, nxt, hp):
        return jnp.concatenate([jnp.broadcast_to(_head_col(a, 2 * hp + h), (SPAN, LANES)) for a in (cur, nxt) for h in range(2)], axis=0)

    def body(k_ref, v_ref, q_ref, qn_ref, do_ref, don_ref, l_ref, ln_ref, dd_ref, ddn_ref, tb_ref, out_ref, carry):
        b = pl.program_id(0)

        @pl.when(b == 0)
        def _():
            carry[...] = jnp.zeros_like(carry)

        left = _left_half((SPAN, LANES))
        for sub in range(grp):
            rows, after = slice(sub * SPAN, (sub + 1) * SPAN), slice((sub + 1) * SPAN, (sub + 2) * SPAN)
            last = sub == grp - 1
            variant = jnp.where((grp * b + sub) % nb == nb - 1, 0, 1) if last else 1
            lse_c, dd_c = l_ref[rows, :], dd_ref[rows, :]
            lse_n, dd_n = (ln_ref[...], ddn_ref[...]) if last else (l_ref[after, :], dd_ref[after, :])
            for hp in range(d // LANES):
                cols = slice(hp * LANES, (hp + 1) * LANES)
                k, v = k_ref[rows, cols], v_ref[rows, cols]
                q4 = heads_stacked(q_ref[rows, cols], qn_ref[:, cols] if last else q_ref[after, cols])
                do4 = heads_stacked(do_ref[rows, cols], don_ref[:, cols] if last else do_ref[after, cols])
                sc = _dot_nt(q4 * ATT_SCALE, k) + tb_ref[variant, hp]
                p = jnp.exp(sc - cols_stacked(lse_c, lse_n, hp))
                ds = p * (_dot_nt(do4, v) - cols_stacked(dd_c, dd_n, hp))
                dq4 = _dot_nn(ds, k)
                dq_cur = jnp.where(left, dq4[:SPAN], dq4[SPAN:2 * SPAN]) + carry[:, cols]
                carry[:, cols] = jnp.where(left, dq4[2 * SPAN:3 * SPAN], dq4[3 * SPAN:])
                out_ref[rows, cols] = (dq_cur * ATT_SCALE).astype(out_ref.dtype)
                out_ref[rows, d + hp * LANES:d + (hp + 1) * LANES] = (_dot_tn(ds, q4) * ATT_SCALE).astype(out_ref.dtype)
                out_ref[rows, 2 * d + hp * LANES:2 * d + (hp + 1) * LANES] = _dot_tn(p, do4).astype(out_ref.dtype)

    qkv_specs = _att_specs(s, d, dil, [(1, "group"), (2, "group"), (0, "group"), (0, "next")])
    wide = _att_specs(s, d, dil, [(0, "group"), (0, "next")])
    heads = _att_specs(s, LANES, dil, [(0, "group"), (0, "next")])
    return pl.pallas_call(
        body,
        grid=(s // (grp * SPAN),),
        in_specs=qkv_specs + wide + heads + heads + [pl.BlockSpec(table.shape, lambda b: (0, 0, 0, 0))],
        out_specs=pl.BlockSpec((grp * SPAN, d3), lambda b: (b, 0)),
        out_shape=jax.ShapeDtypeStruct((s, d3), MXU_DTYPE),
        scratch_shapes=[pltpu.VMEM((SPAN, d), F32)],
        name=name,
        compiler_params=_cparams(("arbitrary",)),
    )(qkv, qkv, qkv, qkv, do, do, lse, lse, dd, dd, table)


def _mix_weights(l_refs):
    ls = [r[...] for r in l_refs]
    m = functools.reduce(jnp.maximum, ls)
    es = [jnp.exp(l - m) for l in ls]
    tot = functools.reduce(lambda a, c: a + c, es)
    return [e / tot for e in es]


def _combine_fwd(os_, ls_, name):
    s, d = ls_[0].shape[0], os_[0].shape[-1]
    n = len(os_)
    n_str = sum(o.ndim == 3 for o in os_)

    def body(*refs):
        o_refs, l_refs, out_ref, scrs = refs[:n], refs[n:2 * n], refs[2 * n], list(refs[2 * n + 1:])
        ws = _mix_weights(l_refs)
        os_v = [o if len(o.shape) == 2 else _streams_in(o, scrs.pop()) for o in o_refs]
        for j in range(d // LANES):
            cols = slice(j * LANES, (j + 1) * LANES)
            acc = _expand_heads(ws[0], j) * os_v[0][:, cols]
            for w, o in zip(ws[1:], os_v[1:]):
                acc = acc + _expand_heads(w, j) * o[:, cols]
            out_ref[:, cols] = acc

    return _rows(body, s, ROW_TILE, [("blk" if a.ndim == 2 else "str", a) for a in os_] + [("blk", a) for a in ls_],
                 [("blk", (s, d), F32)], name, scratch=[_stream_scratch(d)] * n_str)[0]


def _combine_bwd(do, o, ls_, dils, name):
    s, d = o.shape
    n = len(ls_)
    sel = (lax.broadcasted_iota(jnp.int32, (d, LANES), 0) // HEAD_DIM == lax.broadcasted_iota(jnp.int32, (d, LANES), 1)).astype(F32)

    def body(do_ref, o_ref, *rest):
        l_refs, sel_ref, outs = rest[:n], rest[n], rest[n + 1:n + 1 + 2 * n]
        ws = _mix_weights(l_refs)
        dov = do_ref[...]
        r = jnp.dot(dov * o_ref[...], sel_ref[...], precision=lax.Precision.HIGHEST, preferred_element_type=F32)
        for g in range(n):
            outs[2 * g + 1][...] = ws[g] * r
            parts = [_expand_heads(ws[g], j) * dov[:, j * LANES:(j + 1) * LANES] for j in range(d // LANES)]
            if dils[g] == 1:
                for j, part in enumerate(parts):
                    outs[2 * g][:, j * LANES:(j + 1) * LANES] = part.astype(outs[2 * g].dtype)
            else:
                _streams_out(jnp.concatenate(parts, axis=1), outs[2 * g], rest[-1])

    outs = []
    for dil in dils:
        outs += [("blk", (s, d), MXU_DTYPE) if dil == 1 else ("str", (dil, s // dil, d), MXU_DTYPE), ("blk", (s, LANES), F32)]
    res = _rows(body, s, ROW_TILE, [("blk", do), ("blk", o)] + [("blk", l) for l in ls_] + [("all", sel)], outs, name,
                scratch=[_stream_scratch(d)])
    return [(res[2 * g], res[2 * g + 1]) for g in range(n)]


def _ada_fwd(c_all, w, b, name):
    nsub, d, cs = w.shape

    def body(c_ref, w_ref, b_ref, o_ref):
        cv = c_ref[...]
        sc = cv * (1.0 / (1.0 + jnp.exp(-cv)))
        o_ref[...] = _dot_nn(sc, w_ref[...]) + b_ref[...]

    return pl.pallas_call(
        body,
        grid=(nsub,),
        in_specs=[pl.BlockSpec(c_all.shape, lambda i: (0, 0)), pl.BlockSpec((None, d, cs), lambda i: (i, 0, 0)),
                  pl.BlockSpec((None, 1, cs), lambda i: (i, 0, 0))],
        out_specs=pl.BlockSpec((None, N_DEV, cs), lambda i: (i, 0, 0)),
        out_shape=jax.ShapeDtypeStruct((nsub, N_DEV, cs), F32),
        name=name,
        compiler_params=_cparams(("parallel",)),
    )(c_all, w, b)


def _ada_bwd(c_all_t, dm, name):
    d, nb = c_all_t.shape
    nsub, _, cs = dm.shape

    def body(c_ref, dm_ref, o_ref):
        cv = c_ref[...]
        sc = cv * (1.0 / (1.0 + jnp.exp(-cv)))
        acc = sc[:, 0:1] * dm_ref[0:1, :]
        for bi in range(1, nb):
            acc = acc + sc[:, bi:bi + 1] * dm_ref[bi:bi + 1, :]
        o_ref[...] = acc

    return pl.pallas_call(
        body,
        grid=(nsub,),
        in_specs=[pl.BlockSpec(c_all_t.shape, lambda i: (0, 0)), pl.BlockSpec((None, nb, cs), lambda i: (i, 0, 0))],
        out_specs=pl.BlockSpec((None, d, cs), lambda i: (i, 0, 0)),
        out_shape=jax.ShapeDtypeStruct((nsub, d, cs), F32),
        name=name,
        compiler_params=_cparams(("parallel",)),
    )(c_all_t, dm)


def _row_tile(r, row_elems, block_elems=256 * 1024):
    t = 2 * SUBLANES
    if r % t:
        return r
    while t * 2 * row_elems <= block_elems and r % (t * 2) == 0:
        t *= 2
    return t


def _adamw(w, g, m, v, name):
    shape = w.shape
    c = shape[-1]
    r = w.size // c
    tr = _row_tile(r, c, 512 * 1024)
    w2, g2, m2, v2 = [a.reshape(r, c) for a in (w, g, m, v)]
    bc1 = 1.0 - ADAM_B1 ** ADAM_STEP
    bc2 = 1.0 - ADAM_B2 ** ADAM_STEP

    def body(w_ref, g_ref, m_ref, v_ref, d_ref, nm_ref, nv_ref):
        gv = g_ref[...]
        nm = ADAM_B1 * m_ref[...] + (1.0 - ADAM_B1) * gv
        nv = ADAM_B2 * v_ref[...] + (1.0 - ADAM_B2) * (gv * gv)
        d_ref[...] = -ADAM_LR * ((nm / bc1) / (jnp.sqrt(nv / bc2) + ADAM_EPS) + ADAM_WD * w_ref[...])
        nm_ref[...] = nm
        nv_ref[...] = nv

    res = _rows(body, r, tr, [("blk", a) for a in (w2, g2, m2, v2)], [("blk", (r, c), F32)] * 3, name)
    return [a.reshape(shape) for a in res]


def _sum_slots(buf, name):
    n, r, c = buf.shape
    tr = _row_tile(r, n * c, 2 * 1024 * 1024)

    def body(b_ref, o_ref):
        acc = b_ref[0].astype(F32)
        for k in range(1, n):
            acc = acc + b_ref[k].astype(F32)
        o_ref[...] = acc

    return pl.pallas_call(
        body,
        grid=(r // tr,),
        in_specs=[pl.BlockSpec((n, tr, c), lambda i: (0, i, 0))],
        out_specs=pl.BlockSpec((tr, c), lambda i: (i, 0)),
        out_shape=jax.ShapeDtypeStruct((r, c), F32),
        name=name,
        compiler_params=_cparams(("parallel",)),
    )(buf)


def _me():
    return lax.axis_index("x"), lax.axis_index("y"), lax.axis_index("c")


def _all_gather_small(blk, name, after=()):
    m_per, n = blk.shape

    def body(x_ref, *rest):
        out_ref, send_sems, recv_sems, local_sem = rest[len(after):]
        x, y, c = _me()
        me, sibling = (x, y, c), (x, y, 1 - c)
        chips = [(1 - x, y), (x, 1 - y), (1 - x, 1 - y)]

        def rows(px, py, pc):
            return out_ref.at[pl.ds((4 * px + 2 * py + pc) * m_per, m_per), :]

        def copy(k, block, to, src=None):
            return pltpu.make_async_remote_copy(
                src_ref=rows(*block) if src is None else src, dst_ref=rows(*block),
                send_sem=send_sems.at[k], recv_sem=recv_sems.at[k], device_id=to, device_id_type=MESH)

        mine = pltpu.make_async_copy(x_ref, rows(*me), local_sem)
        mine.start()
        first = [copy(0, me, sibling, src=x_ref)]
        first += [copy(1 + j, me, (*chip, c), src=x_ref) for j, chip in enumerate(chips)]
        for cp in first:
            cp.start()
        passed = [copy(4 + j, (*chip, c), sibling) for j, chip in enumerate(chips)]
        for j, chip in enumerate(chips):
            copy(1 + j, (*chip, c), me).wait_recv()
            passed[j].start()
        copy(0, sibling, me).wait_recv()
        for j, chip in enumerate(chips):
            copy(4 + j, (*chip, 1 - c), me).wait_recv()
        for cp in first + passed:
            cp.wait_send()
        mine.wait()

    return pl.pallas_call(
        body,
        out_shape=jax.ShapeDtypeStruct((N_DEV * m_per, n), blk.dtype),
        in_specs=[pl.BlockSpec(memory_space=pltpu.VMEM)] + [pl.BlockSpec(memory_space=pl.ANY)] * len(after),
        out_specs=pl.BlockSpec(memory_space=pltpu.VMEM),
        scratch_shapes=[pltpu.SemaphoreType.DMA((7,)), pltpu.SemaphoreType.DMA((7,)), pltpu.SemaphoreType.DMA],
        name=name,
        compiler_params=pltpu.CompilerParams(vmem_limit_bytes=VMEM_LIMIT),
    )(blk, *after)


_HBM = pl.BlockSpec(memory_space=pltpu.HBM)
_SEM = pl.BlockSpec(memory_space=pltpu.SEMAPHORE)
_EFFECT = pltpu.SideEffectType.DATAFLOW_SIDE_EFFECTING


def _other_chips(x, y):
    return [(1 - x, y), (x, 1 - y), (1 - x, 1 - y)]


def _gather_copy(w, j, src_ref, land_ref, send_sems, recv_sems, halved=False):
    x, y, c = _me()
    if halved:
        half = src_ref.shape[0] // 2
        src_ref = src_ref.at[pl.ds(c * half, half), :]
    return pltpu.make_async_remote_copy(
        src_ref=src_ref, dst_ref=land_ref.at[2 * x + y], send_sem=send_sems.at[3 * w + j], recv_sem=recv_sems.at[3 * w + j],
        device_id=(*_other_chips(x, y)[j], c), device_id_type=MESH)


def _gather_start(shards, halved, after, name):
    n = len(shards)
    lands = [lax.empty((N_CHIPS, s.shape[0] // 2 if w in halved else s.shape[0], s.shape[1]), s.dtype) for w, s in enumerate(shards)]

    def body(*refs):
        in_refs, land_refs = refs[:n], refs[n:2 * n]
        send_sems, recv_sems = refs[2 * n + 1], refs[2 * n + 2]
        token = refs[-1]
        for w in range(n):
            for j in range(3):
                _gather_copy(w, j, in_refs[w], land_refs[w], send_sems, recv_sems, w in halved).start()
        token[...] = jnp.zeros_like(token)

    res = pl.pallas_call(
        body,
        out_shape=(pltpu.SemaphoreType.DMA((3 * n,)), pltpu.SemaphoreType.DMA((3 * n,)),
                   *[pltpu.HBM(s.shape, s.dtype) for s in shards], *[pltpu.HBM(l.shape, l.dtype) for l in lands],
                   jax.ShapeDtypeStruct((SUBLANES, LANES), F32)),
        in_specs=[_HBM] * (2 * n) + [pl.BlockSpec(memory_space=pl.ANY)],
        out_specs=(_SEM, _SEM, *[_HBM] * (2 * n), pl.BlockSpec(memory_space=pltpu.VMEM)),
        input_output_aliases={i: 2 + i for i in range(2 * n)},
        name=name,
        compiler_params=pltpu.CompilerParams(has_side_effects=_EFFECT),
    )(*[pltpu.with_memory_space_constraint(a, pltpu.HBM) for a in list(shards) + lands], after)
    return res[0], res[1], res[2:2 + n], res[2 + n:2 + 2 * n], res[-1]


def _gather_wait(w, shard, land, send_sems, recv_sems, after, name, halved=False):
    def body(s_ref, land_ref, send_sems, recv_sems, after_ref, s_out, land_out, stage):
        x, y, _ = _me()
        if not halved:
            pltpu.sync_copy(s_ref, stage)
            pltpu.sync_copy(stage, land_out.at[2 * x + y])
        for j in range(3):
            cp = _gather_copy(w, j, s_ref, land_ref, send_sems, recv_sems, halved)
            cp.wait_send()
            cp.wait_recv()

    return pl.pallas_call(
        body,
        out_shape=(pltpu.HBM(shard.shape, shard.dtype), pltpu.HBM(land.shape, land.dtype)),
        in_specs=(_HBM, _HBM, _SEM, _SEM, pl.BlockSpec(memory_space=pl.ANY)),
        out_specs=(_HBM, _HBM),
        input_output_aliases={0: 0, 1: 1},
        scratch_shapes=[pltpu.VMEM((SUBLANES, LANES) if halved else shard.shape, shard.dtype)],
        name=name,
        compiler_params=pltpu.CompilerParams(has_side_effects=_EFFECT, vmem_limit_bytes=VMEM_LIMIT),
    )(shard, land, send_sems, recv_sems, after)


def _assemble_halves(shard, land, name):
    half = land.shape[1]

    def body(s_ref, land_ref, out_ref, send_sems, recv_sems, local_sems):
        x, y, c = _me()
        own = pltpu.make_async_copy(s_ref, out_ref.at[2 * x + y], local_sems.at[3])
        own.start()
        cps = []
        for j, (ox, oy) in enumerate(_other_chips(x, y)):
            qj = 2 * ox + oy
            mine = out_ref.at[qj, pl.ds(c * half, half), :]
            lc = pltpu.make_async_copy(land_ref.at[qj], mine, local_sems.at[j])
            lc.start()
            rc = pltpu.make_async_remote_copy(
                src_ref=land_ref.at[qj], dst_ref=mine, send_sem=send_sems.at[j], recv_sem=recv_sems.at[j],
                device_id=(x, y, 1 - c), device_id_type=MESH)
            rc.start()
            cps.append((lc, rc))
        for lc, rc in cps:
            rc.wait_recv()
        for lc, rc in cps:
            rc.wait_send()
            lc.wait()
        own.wait()

    vmem = pl.BlockSpec(memory_space=pltpu.VMEM)
    return pl.pallas_call(
        body,
        out_shape=jax.ShapeDtypeStruct((N_CHIPS,) + shard.shape, shard.dtype),
        in_specs=[vmem, vmem],
        out_specs=vmem,
        scratch_shapes=[pltpu.SemaphoreType.DMA((3,)), pltpu.SemaphoreType.DMA((3,)), pltpu.SemaphoreType.DMA((4,))],
        name=name,
        compiler_params=pltpu.CompilerParams(vmem_limit_bytes=VMEM_LIMIT),
    )(shard, land)


def _piece_shape(shape, kind):
    k, nn = shape
    if kind == "all":
        return (k, nn)
    return (k // 2, nn // N_CHIPS) if kind == "col" else (k // N_CHIPS // 2, nn)


def _piece_of(g_ref, kind, tq, tc):
    pr, pc = _piece_shape(g_ref.shape, kind)
    if kind == "all":
        return g_ref
    if kind == "col":
        return g_ref.at[pl.ds(tc * pr, pr), pl.ds(tq * pc, pc)]
    return g_ref.at[pl.ds((2 * tq + tc) * pr, pr), :]


def _scatter_copy(w, r, kind, g_ref, land_ref, send_sems, recv_sems):
    x, y, c = _me()
    tx, ty, tc = (x + ((r >> 2) & 1)) % 2, (y + ((r >> 1) & 1)) % 2, (c + (r & 1)) % 2
    return pltpu.make_async_remote_copy(
        src_ref=_piece_of(g_ref, kind, 2 * tx + ty, tc), dst_ref=land_ref.at[4 * x + 2 * y + c],
        send_sem=send_sems.at[N_DEV * w + r], recv_sem=recv_sems.at[N_DEV * w + r], device_id=(tx, ty, tc), device_id_type=MESH)


def _scatter_start(gs, kinds, name):
    n = len(gs)
    pieces = [_piece_shape(g.shape, kind) for g, kind in zip(gs, kinds)]
    lands = [lax.empty((N_DEV,) + p, g.dtype) for p, g in zip(pieces, gs)]

    def body(*refs):
        g_refs, land_refs, send_sems, recv_sems = refs[:n], refs[n:2 * n], refs[2 * n], refs[2 * n + 1]
        land_outs, stages = refs[3 * n + 2:4 * n + 2], refs[4 * n + 2:]
        x, y, c = _me()
        for w in range(n):
            for r in range(1, N_DEV):
                _scatter_copy(w, r, kinds[w], g_refs[w], land_refs[w], send_sems, recv_sems).start()
        for w in range(n):
            pltpu.sync_copy(_piece_of(g_refs[w], kinds[w], 2 * x + y, c), stages[w])
            pltpu.sync_copy(stages[w], land_outs[w].at[4 * x + 2 * y + c])

    arrays = list(gs) + lands
    res = pl.pallas_call(
        body,
        out_shape=(pltpu.SemaphoreType.DMA((N_DEV * n,)), pltpu.SemaphoreType.DMA((N_DEV * n,)),
                   *[pltpu.HBM(a.shape, a.dtype) for a in arrays]),
        in_specs=[_HBM] * (2 * n),
        out_specs=(_SEM, _SEM, *[_HBM] * (2 * n)),
        input_output_aliases={i: 2 + i for i in range(2 * n)},
        scratch_shapes=[pltpu.VMEM(p, g.dtype) for p, g in zip(pieces, gs)],
        name=name,
        compiler_params=pltpu.CompilerParams(has_side_effects=_EFFECT, vmem_limit_bytes=VMEM_LIMIT),
    )(*[pltpu.with_memory_space_constraint(a, pltpu.HBM) for a in arrays])
    return res[0], res[1], res[2:2 + n], res[2 + n:]


def _scatter_wait(send_sems, recv_sems, gs, lands, kinds, after, name):
    n = len(gs)

    def body(*refs):
        g_refs, land_refs, send_sems, recv_sems = refs[:n], refs[n:2 * n], refs[2 * n], refs[2 * n + 1]
        for w in range(n):
            for r in range(1, N_DEV):
                cp = _scatter_copy(w, r, kinds[w], g_refs[w], land_refs[w], send_sems, recv_sems)
                cp.wait_send()
                cp.wait_recv()

    arrays = list(gs) + list(lands)
    return pl.pallas_call(
        body,
        out_shape=tuple(pltpu.HBM(a.shape, a.dtype) for a in arrays),
        in_specs=(*[_HBM] * (2 * n), _SEM, _SEM, pl.BlockSpec(memory_space=pl.ANY)),
        out_specs=tuple([_HBM] * (2 * n)),
        input_output_aliases={i: i for i in range(2 * n)},
        name=name,
        compiler_params=pltpu.CompilerParams(has_side_effects=_EFFECT),
    )(*arrays, send_sems, recv_sems, after)[n:]


def _swap_halves(halves, name):
    n = len(halves)

    def body(*refs):
        in_refs, out_refs = refs[:n], refs[n:2 * n]
        send_sems, recv_sems, local_sems = refs[2 * n:]
        x, y, c = _me()
        cps = []
        for w in range(n):
            lc = pltpu.make_async_copy(in_refs[w], out_refs[w].at[c], local_sems.at[w])
            lc.start()
            rc = pltpu.make_async_remote_copy(
                src_ref=in_refs[w], dst_ref=out_refs[w].at[c], send_sem=send_sems.at[w], recv_sem=recv_sems.at[w],
                device_id=(x, y, 1 - c), device_id_type=MESH)
            rc.start()
            cps.append((lc, rc))
        for lc, rc in cps:
            rc.wait_recv()
        for lc, rc in cps:
            rc.wait_send()
            lc.wait()

    vmem = pl.BlockSpec(memory_space=pltpu.VMEM)
    return pl.pallas_call(
        body,
        out_shape=[jax.ShapeDtypeStruct((2,) + h.shape, h.dtype) for h in halves],
        in_specs=[vmem] * n,
        out_specs=[vmem] * n,
        scratch_shapes=[pltpu.SemaphoreType.DMA((n,)), pltpu.SemaphoreType.DMA((n,)), pltpu.SemaphoreType.DMA((n,))],
        name=name,
        compiler_params=pltpu.CompilerParams(vmem_limit_bytes=VMEM_LIMIT),
    )(*halves)


def _to_streams(a, dil):
    if dil == 1:
        return a
    s, c = a.shape
    return a.reshape(s // dil, dil, c).transpose(1, 0, 2).reshape(s, c)


def _from_streams(a, dil):
    if dil == 1:
        return a
    s, c = a.shape
    return a.reshape(dil, s // dil, c).transpose(1, 0, 2).reshape(s, c)


def _mm_tiles(s):
    return min(s, 2048)


def _local_step(x0, target, mvec, ln_g, ln_b, small, fetch, emit, start):
    s, d = x0.shape
    tm = _mm_tiles(s)
    row = lambda v: v.reshape(1, -1)
    shift = [row(mvec[i, :d]) for i in range(4)]
    scale = [row(mvec[i, d:2 * d]) for i in range(4)]
    gate = [row(1.0 + mvec[i, 2 * d:]) for i in range(4)]
    lg = [row(ln_g[i]) for i in range(4)]
    lb = [row(ln_b[i]) for i in range(4)]
    mm = functools.partial(_mm, tm=tm)
    mm_w = functools.partial(_mm, tm=1024, tk=min(s, 2048), mode="tn")

    def resid_ln_epilogue(sub):
        def epi(y, xv, gate_v, g_v, b_v, sc_v, sh_v):
            xhat, _ = _ln_stats(ALPHA * xv + gate_v * y)
            xn = xhat * g_v + b_v
            return [y, xn, xn * (1.0 + sc_v) + sh_v]

        rows = [gate[sub], lg[sub], lb[sub], scale[sub + 1], shift[sub + 1]]
        return dict(outs=[F32, F32, MXU_DTYPE], epi=epi, extras=[("full", xs[sub])] + [("row", r) for r in rows])

    xs, ys, big = [x0], [], {}
    h0 = _mod(x0, scale[0], shift[0], start, "mod0")
    big["a_w_in"] = fetch("a_w_in", h0)
    uvpre = mm(h0, big["a_w_in"], mode="nn", name="a_in", outs=[F32], tn=512, tk=1024,
               epi=lambda r, bias: [r + bias], extras=[("row", small["a_b_in"])])
    gated = _spatial_fwd(uvpre, small["a_vn_g"], small["a_vn_b"], small["wc"], small["bias_full"], "a_spatial")
    big["a_w_out"] = fetch("a_w_out", gated)
    y0, x1, h1 = mm(gated, big["a_w_out"], mode="nn", name="a_out", tm=min(s, 1024), tn=d, tk=1024, **resid_ln_epilogue(0))
    ys.append(y0)
    xs.append(x1)
    relu2 = lambda r: [jnp.square(jnp.maximum(r, 0.0))]
    big["up0"] = fetch("up0", h1)
    r0 = mm(h1, big["up0"], mode="nn", name="up0", outs=[MXU_DTYPE], tn=1024, tk=1024, epi=relu2)
    big["down0"] = fetch("down0", r0)
    ys.append(mm(r0, big["down0"], mode="nn", name="down0", outs=[F32], tm=min(s, 1024), tn=1024, tk=2048))
    dils = [dil for _, dil in B_PATTERNS]
    x2, h2, *h2_streams = _resid_ln(xs[1], ys[1], gate[1], lg[1], lb[1], (scale[2], shift[2]), "ln1", [dil for dil in dils if dil > 1])
    h2_streams = [h2] + [a.reshape(s, d) for a in h2_streams]
    xs.append(x2)
    hg, qkvs, o_g, l_g, l_streams = [], [], [], [], []
    big["b_w_qkv"] = fetch("b_w_qkv", h2)
    for g, (_, dil) in enumerate(B_PATTERNS):
        hp = h2_streams[g]
        qkv = mm(hp, big["b_w_qkv"], mode="nn", name=f"qkv{g}", outs=[MXU_DTYPE], tn=768, tk=1024, b_col0=g * 3 * d, n_out=3 * d)
        og, lgv = _attn_fwd(qkv, small["slopes"], dil, f"attn_fwd{g}")
        hg.append(hp)
        qkvs.append(qkv)
        o_g.append(og if dil == 1 else og.reshape(dil, s // dil, d))
        l_g.append(_from_streams(lgv, dil))
        l_streams.append(lgv)
    o_mix = _combine_fwd(o_g, l_g, "combine")
    big["b_w_out"] = fetch("b_w_out", o_mix)
    y2, x3, h3 = mm(o_mix, big["b_w_out"], mode="nn", name="b_out", tm=min(s, 1024), tn=d, tk=1024, **resid_ln_epilogue(2))
    ys.append(y2)
    xs.append(x3)
    big["up1"] = fetch("up1", h3)
    r1 = mm(h3, big["up1"], mode="nn", name="up1", outs=[MXU_DTYPE], tn=1024, tk=1024, epi=relu2)
    big["down1"] = fetch("down1", r1)
    ys.append(mm(r1, big["down1"], mode="nn", name="down1", outs=[F32], tm=min(s, 1024), tn=1024, tk=2048))

    gb, red_ln, red_mod = {}, [None] * 4, [None] * 4

    def mlp_bwd(i, h, r, dyy):
        gb[f"down{i}"] = mm_w(r, dyy, name=f"g_down{i}", outs=[MXU_DTYPE], tn=1024)
        da = mm(dyy, big[f"down{i}"], mode="nt", name=f"d_down{i}", outs=[MXU_DTYPE], tn=1024, tk=1024,
                after=emit(f"down{i}", gb[f"down{i}"]),
                epi=lambda acc, rv: [acc * (2.0 * jnp.sqrt(rv.astype(F32)))], extras=[("full", r)])
        gb[f"up{i}"] = mm_w(h, da, name=f"g_up{i}", outs=[MXU_DTYPE], tn=1024)
        return [mm(da, big[f"up{i}"], mode="nt", name=f"d_up{i}", outs=[F32], tn=1024, tk=1024, after=emit(f"up{i}", gb[f"up{i}"]))]

    def join(sub, dxr, dhs, after=None):
        res = _mod_ln_bwd(dxr, dhs, xs[sub], scale[sub], xs[sub - 1], ys[sub - 1], gate[sub - 1], lg[sub - 1],
                          f"mod_ln_bwd{sub}", after=after)
        red_mod[sub], red_ln[sub - 1] = res[2], res[3]
        return res[0], res[1]

    loss, dxr, dyy, red_ln[3] = _last_ln_loss_bwd(xs[3], ys[3], gate[3], lg[3], lb[3], target, "ln3_loss_bwd")
    dxr, dyy = join(3, dxr, mlp_bwd(1, h3, r1, dyy))
    gb["b_w_out"] = mm_w(o_mix, dyy, name="g_b_out", outs=[MXU_DTYPE], tn=1024, tk=1024)
    do = mm(dyy, big["b_w_out"], mode="nt", name="d_b_out", outs=[F32], tn=1024, tk=1024, after=emit("b_w_out", gb["b_w_out"]))
    parts = _combine_bwd(do, o_mix, l_g, dils, "combine_bwd")
    dhs, gq = [], None
    for g, (_, dil) in enumerate(B_PATTERNS):
        do_g, dd_g = parts[g][0].reshape(s, d), _to_streams(parts[g][1], dil)
        dqkv = _attn_bwd(qkvs[g], do_g, l_streams[g], dd_g, small["slopes"], dil, f"attn_bwd{g}")
        gq = mm_w(hg[g], dqkv, name=f"g_qkv{g}", outs=[MXU_DTYPE], tn=1024, out_col0=g * 3 * d, out_cols=len(B_PATTERNS) * 3 * d, into=gq)
        dh = mm(dqkv, big["b_w_qkv"], mode="nt", name=f"d_qkv{g}", outs=[F32], tn=1024, tk=768, b_col0=g * 3 * d)
        dhs.append(dh if dil == 1 else dh.reshape(dil, s // dil, d))
    gb["b_w_qkv"] = gq
    dxr, dyy = join(2, dxr, dhs, after=emit("b_w_qkv", gb["b_w_qkv"]))
    dxr, dyy = join(1, dxr, mlp_bwd(0, h1, r0, dyy))
    gb["a_w_out"] = mm_w(gated, dyy, name="g_a_out", outs=[MXU_DTYPE], tn=1024)
    dgated = mm(dyy, big["a_w_out"], mode="nt", name="d_a_out", outs=[F32], tn=1024, tk=1024, after=emit("a_w_out", gb["a_w_out"]))
    duv, dws, dbias, dbin, dvg, dvb = _spatial_bwd(uvpre, dgated, small["a_vn_g"], small["a_vn_b"], small["wc"],
                                                   small["wct"], small["bias_full"], "a_spatial_bwd")
    tril = jnp.tril(jnp.ones((CHUNK, CHUNK), bool))
    dws = jnp.where(tril, dws, 0.0).reshape(-1, LANES)
    gb["a_w_in"] = mm_w(h0, duv, name="g_a_in", outs=[MXU_DTYPE], tn=1024, after=emit("a_w_s", dws.astype(MXU_DTYPE)))
    dh = mm(duv, big["a_w_in"], mode="nt", name="d_a_in", outs=[F32], tn=1024, tk=512, after=emit("a_w_in", gb["a_w_in"]))
    dx, red_mod[0] = _mod_bwd(dxr, [dh], xs[0], scale[0], "mod_bwd0")
    dm = [jnp.concatenate([red_mod[i][0], red_mod[i][1], red_ln[i][2]]) for i in range(4)]
    dlg, dlb = [red_ln[i][0] for i in range(4)], [red_ln[i][1] for i in range(4)]

    gsmall = {
        "a_b_in": dbin.reshape(-1), "a_vn_g": dvg.reshape(-1), "a_vn_b": dvb.reshape(-1),
        "a_w_s": dws.reshape(-1),
        "a_b_s": dbias.reshape(CHUNK, A_GROUPS, d // A_GROUPS).sum(-1).T.reshape(-1),
    }
    return loss, dx, gb, jnp.stack(dm), jnp.stack(dlg), jnp.stack(dlb), gsmall


BIG = ("a_w_in", "a_w_out", "up0", "down0", "b_w_qkv", "b_w_out", "up1", "down1")
BIG_KIND = {"a_w_in": "col", "a_w_out": "row", "b_w_qkv": "col", "b_w_out": "row",
            "up0": "col", "up1": "col", "down0": "row", "down1": "row", "a_w_s": "all"}
HALVED = ("a_w_in", "down0", "b_w_qkv")
SCATTER_GROUPS = (("down1", "up1"), ("b_w_out", "b_w_qkv"), ("down0", "up0"), ("a_w_out", "a_w_in"), ("a_w_s",))
SMALL = ("a_b_in", "a_vn_g", "a_vn_b", "a_b_s")


def kernel(x, c, ada_w, ada_b, ln_g, ln_b, a_w_in, a_b_in, a_vn_g, a_vn_b, a_w_s, a_b_s, a_w_out, b_w_qkv, b_w_out, mlp_w_up, mlp_w_down, loss_target, m_ada_w, m_ada_b, m_ln_g, m_ln_b, m_a_w_in, m_a_b_in, m_a_vn_g, m_a_vn_b, m_a_w_s, m_a_b_s, m_a_w_out, m_b_w_qkv, m_b_w_out, m_mlp_w_up, m_mlp_w_down, v_ada_w, v_ada_b, v_ln_g, v_ln_b, v_a_w_in, v_a_b_in, v_a_vn_g, v_a_vn_b, v_a_w_s, v_a_b_s, v_a_w_out, v_b_w_qkv, v_b_w_out, v_mlp_w_up, v_mlp_w_down):
    s, d = x.shape[1], x.shape[2]
    xi, yi, ci = _me()
    q = 2 * xi + yi
    dev = 2 * q + ci
    nsub = 2 * DEPTH
    cs = ada_w.shape[-1]
    ls = ln_g.shape[-1]

    shards = {
        "a_w_in": a_w_in[0], "a_w_out": a_w_out[0], "b_w_qkv": b_w_qkv[0], "b_w_out": b_w_out[0],
        "up0": mlp_w_up[0], "up1": mlp_w_up[1], "down0": mlp_w_down[0], "down1": mlp_w_down[1],
    }
    cast = [shards[k].astype(MXU_DTYPE) for k in BIG]

    pack = jnp.concatenate([c.reshape(-1), ln_g.reshape(-1), ln_b.reshape(-1)]).reshape(-1, LANES)
    got = _all_gather_small(pack, "gather_small", after=cast).reshape(N_DEV, -1)
    c_all = got[:, :d]
    per_chip = got[0::2]
    ln_g_full = per_chip[:, d:d + nsub * ls].reshape(N_CHIPS, nsub, ls).transpose(1, 0, 2).reshape(nsub, d)
    ln_b_full = per_chip[:, d + nsub * ls:].reshape(N_CHIPS, nsub, ls).transpose(1, 0, 2).reshape(nsub, d)
    m_part = _ada_fwd(c_all, ada_w.reshape(nsub, d, cs), ada_b.reshape(nsub, 1, cs), "ada_fwd")
    m_all = _all_gather_small(m_part.reshape(-1, LANES), "gather_mod").reshape(N_DEV, nsub, N_DEV, cs)
    m_mine = lax.dynamic_index_in_dim(m_all[0::2], dev, axis=2, keepdims=False)
    mvec = m_mine.transpose(1, 0, 2).reshape(nsub, 3 * d)

    halved = {BIG.index(k) for k in HALVED}
    send_sems, recv_sems, shard_thru, lands, token = _gather_start(cast, halved, mvec, "gather_start")

    def fetch(k, after):
        w = BIG.index(k)
        shard, gw = _gather_wait(w, shard_thru[w], lands[w], send_sems, recv_sems, after, f"gather_wait_{k}", w in halved)
        if w in halved:
            gw = _assemble_halves(shard, gw, f"assemble_{k}")
        return gw if BIG_KIND[k] == "col" else gw.reshape(1, -1, gw.shape[-1])

    scattering, pending = {}, {}

    def emit(k, g):
        pending[k] = g
        group = next(gr for gr in SCATTER_GROUPS if k in gr)
        if k != group[-1]:
            return None
        scattering[group] = _scatter_start([pending[m] for m in group], [BIG_KIND[m] for m in group], f"scatter_start_{k}")
        return scattering[group][2][0]

    tril = jnp.tril(jnp.ones((CHUNK, CHUNK), bool))
    wc = jnp.where(tril, a_w_s[0], 0.0).astype(MXU_DTYPE)
    heads = jnp.arange(1, B_HEADS + 1, dtype=F32)
    small = {
        "a_b_in": a_b_in, "a_vn_g": a_vn_g, "a_vn_b": a_vn_b,
        "wc": wc, "wct": wc.transpose(0, 2, 1),
        "bias_full": jnp.repeat(a_b_s[0].T, d // A_GROUPS, axis=1),
        "slopes": jnp.exp2(-8.0 * heads / B_HEADS),
    }

    loss_part, grad_x, gb, dm, dlg, dlb, gsmall = _local_step(x[0], loss_target[0], mvec, ln_g_full, ln_b_full, small, fetch, emit, token)
    loss = lax.psum(loss_part, ("x", "y", "c"))

    weights = dict(ada_w=ada_w, ada_b=ada_b, ln_g=ln_g, ln_b=ln_b, a_w_in=a_w_in, a_b_in=a_b_in, a_vn_g=a_vn_g, a_vn_b=a_vn_b,
                   a_w_s=a_w_s, a_b_s=a_b_s, a_w_out=a_w_out, b_w_qkv=b_w_qkv, b_w_out=b_w_out, mlp_w_up=mlp_w_up, mlp_w_down=mlp_w_down)
    ms = dict(ada_w=m_ada_w, ada_b=m_ada_b, ln_g=m_ln_g, ln_b=m_ln_b, a_w_in=m_a_w_in, a_b_in=m_a_b_in, a_vn_g=m_a_vn_g, a_vn_b=m_a_vn_b,
              a_w_s=m_a_w_s, a_b_s=m_a_b_s, a_w_out=m_a_w_out, b_w_qkv=m_b_w_qkv, b_w_out=m_b_w_out, mlp_w_up=m_mlp_w_up, mlp_w_down=m_mlp_w_down)
    vs = dict(ada_w=v_ada_w, ada_b=v_ada_b, ln_g=v_ln_g, ln_b=v_ln_b, a_w_in=v_a_w_in, a_b_in=v_a_b_in, a_vn_g=v_a_vn_g, a_vn_b=v_a_vn_b,
              a_w_s=v_a_w_s, a_b_s=v_a_b_s, a_w_out=v_a_w_out, b_w_qkv=v_b_w_qkv, b_w_out=v_b_w_out, mlp_w_up=v_mlp_w_up, mlp_w_down=v_mlp_w_down)
    grads, updates = {}, {}

    def update(k):
        updates[k] = _adamw(weights[k], grads[k], ms[k], vs[k], f"adamw_{k}")
        return updates[k][0]

    gfull = {}

    def big_group(group, after):
        bufs = []
        for pair in (group[:2], group[2:]):
            bufs += _scatter_wait(*scattering[pair], [BIG_KIND[m] for m in pair], after, f"scatter_wait_{pair[-1]}")
        halves = [_sum_slots(b, f"sum_{k}") for k, b in zip(group, bufs)]
        fulls = _swap_halves(halves, f"swap_halves_{group[0]}")
        gfull.update({k: f.reshape(-1, f.shape[-1]) for k, f in zip(group, fulls)})

    big_group(SCATTER_GROUPS[0] + SCATTER_GROUPS[1], grad_x)
    grads["b_w_qkv"], grads["b_w_out"] = gfull["b_w_qkv"][None], gfull["b_w_out"][None]
    update("b_w_out")
    done = update("b_w_qkv")

    pack_b = jnp.concatenate([dm.reshape(-1), dlg.reshape(-1), dlb.reshape(-1)] + [gsmall[k] for k in SMALL])
    n_small = pack_b.shape[0]
    pack_b = jnp.pad(pack_b, (0, -n_small % (256 * LANES)))
    got_b = _all_gather_small(pack_b.reshape(-1, LANES), "gather_small_grads", after=[done]).reshape(N_DEV, -1, LANES)
    tot = _sum_slots(got_b, "sum_small").reshape(-1)
    o = 0
    dm_tot = tot[o:o + nsub * 3 * d].reshape(nsub, 3 * d); o += nsub * 3 * d
    dlg_tot = tot[o:o + nsub * d].reshape(nsub, d); o += nsub * d
    dlb_tot = tot[o:o + nsub * d].reshape(nsub, d); o += nsub * d
    g_small = {}
    for k, ref in zip(SMALL, (a_b_in, a_vn_g, a_vn_b, a_b_s)):
        g_small[k] = tot[o:o + ref.size].reshape(ref.shape); o += ref.size
    assert o == n_small
    aws = _scatter_wait(*scattering[("a_w_s",)], ["all"], tot, "scatter_wait_a_w_s")[0]
    g_small["a_w_s"] = _sum_slots(aws, "sum_a_w_s").reshape(a_w_s.shape)
    dm_all = got_b.reshape(N_DEV, -1)[:, :nsub * 3 * d].reshape(N_DEV, nsub, 3 * d)
    dm_cols = lax.dynamic_slice_in_dim(dm_all, q * cs, cs, axis=2).transpose(1, 0, 2)
    grads.update({
        "ada_w": _ada_bwd(c_all.T, dm_cols, "ada_bwd").reshape(ada_w.shape),
        "ada_b": lax.dynamic_slice_in_dim(dm_tot, q * cs, cs, axis=1).reshape(ada_b.shape),
        "ln_g": lax.dynamic_slice_in_dim(dlg_tot, q * ls, ls, axis=1).reshape(ln_g.shape),
        "ln_b": lax.dynamic_slice_in_dim(dlb_tot, q * ls, ls, axis=1).reshape(ln_b.shape),
        **g_small,
    })
    for k in ("ada_b", "ln_g", "ln_b", "a_w_s") + SMALL:
        update(k)
    done = update("ada_w")

    big_group(SCATTER_GROUPS[2] + SCATTER_GROUPS[3], done)
    grads.update({
        "a_w_in": gfull["a_w_in"][None], "a_w_out": gfull["a_w_out"][None],
        "mlp_w_up": jnp.stack([gfull["up0"], gfull["up1"]]), "mlp_w_down": jnp.stack([gfull["down0"], gfull["down1"]]),
    })
    for k in ("a_w_in", "a_w_out", "mlp_w_up", "mlp_w_down"):
        update(k)
    names = list(weights)
    return (loss, grad_x[None], *[grads[k] for k in names], *[updates[k][0] for k in names],
            *[updates[k][1] for k in names], *[updates[k][2] for k in names])
```

```python
import functools
import math

import jax
import jax.numpy as jnp
from jax import lax
from jax.experimental import pallas as pl
from jax.experimental.pallas import tpu as pltpu

F32 = jnp.float32
MXU_DTYPE = jnp.bfloat16

DEPTH = 2
CHUNK = 128
A_GROUPS = 16
B_HEADS = 16
HEAD_DIM = 64
B_PATTERNS = ((128, 1), (512, 4), (2048, 16))
SPAN = 128
ALPHA = (2 * DEPTH) ** 0.25
LN_EPS = 1e-5
NEG = -1e30
ATT_SCALE = HEAD_DIM ** -0.5
ADAM_LR, ADAM_B1, ADAM_B2, ADAM_EPS, ADAM_WD, ADAM_STEP = 0.001, 0.9, 0.999, 1e-08, 0.01, 10

N_CHIPS = 4
N_DEV = 8
LANES = 128
SUBLANES = 8
VMEM_LIMIT = 52 * 1024 * 1024
ROW_TILE = 512
MM_ROW_CHUNK = 256
MESH = pl.DeviceIdType.MESH


def _cparams(sem):
    return pltpu.CompilerParams(dimension_semantics=sem, vmem_limit_bytes=VMEM_LIMIT)


def _fold8(v):
    r, c = v.shape
    return jnp.sum(v.reshape(r // SUBLANES, SUBLANES, c), axis=0)


def _gelu(x):
    c = math.sqrt(2.0 / math.pi)
    return 0.5 * x * (1.0 + jnp.tanh(c * (x + 0.044715 * (x * x * x))))


def _gelu_and_grad(x):
    c = math.sqrt(2.0 / math.pi)
    t = jnp.tanh(c * (x + 0.044715 * (x * x * x)))
    return 0.5 * x * (1.0 + t), 0.5 * (1.0 + t) + 0.5 * x * (1.0 - t * t) * c * (1.0 + 3.0 * 0.044715 * x * x)


def _dot(a, b, dims):
    return lax.dot_general(a.astype(MXU_DTYPE), b.astype(MXU_DTYPE), (dims, ((), ())), preferred_element_type=F32)


def _dot_nn(a, b):
    return _dot(a, b, ((1,), (0,)))


def _dot_nt(a, b):
    return _dot(a, b, ((1,), (1,)))


def _dot_tn(a, b):
    return _dot(a, b, ((0,), (0,)))


def _mm(a, b, *, mode, name, outs, tm, tn, tk, epi=None, extras=(), b_col0=0, n_out=None, after=None,
        out_col0=0, out_cols=None, into=None):
    if mode == "nn":
        m, kdim = a.shape
        p, kb, ns = b.shape
        assert kb == kdim and ns % tn == 0 and b_col0 % tn == 0
        n = n_out if n_out is not None else p * ns
        npt, j0 = ns // tn, b_col0 // tn
        a_spec = pl.BlockSpec((tm, tk), lambda i, j, k: (i, k))
        b_spec = pl.BlockSpec((None, tk, tn), lambda i, j, k: ((j + j0) // npt, k, (j + j0) % npt))
        dot = _dot_nn
    elif mode == "nt":
        m, kdim = a.shape
        p, n, ns = b.shape
        assert ns % tk == 0 and b_col0 % tk == 0
        npt, j0 = ns // tk, b_col0 // tk
        a_spec = pl.BlockSpec((tm, tk), lambda i, j, k: (i, k))
        b_spec = pl.BlockSpec((None, tn, tk), lambda i, j, k: ((k + j0) // npt, j, (k + j0) % npt))
        dot = _dot_nt
    else:
        kdim, m = a.shape
        kb, n = b.shape
        assert kb == kdim
        a_spec = pl.BlockSpec((tk, tm), lambda i, j, k: (k, i))
        b_spec = pl.BlockSpec((tk, tn), lambda i, j, k: (k, j))
        dot = _dot_tn
    assert m % tm == 0 and n % tn == 0 and kdim % tk == 0, (name, m, n, kdim, tm, tn, tk)
    nk = kdim // tk
    ex_specs, ex_arrays = [], []
    for kind, arr in extras:
        if kind == "row":
            ex_specs.append(pl.BlockSpec((1, tn), lambda i, j, k: (0, j)))
        else:
            ex_specs.append(pl.BlockSpec((tm, tn), lambda i, j, k: (i, j)))
        ex_arrays.append(arr)
    n_ex, n_o = len(ex_arrays), len(outs)
    deps = [d for d in (after, into) if d is not None]
    n_dep = len(deps)
    j_out = out_col0 // tn
    assert out_col0 % tn == 0 and (into is None or len(outs) == 1)

    def body(a_ref, b_ref, *rest):
        ex_refs, o_refs = rest[:n_ex], rest[n_ex + n_dep:n_ex + n_dep + n_o]
        k = pl.program_id(2)

        chunks = [slice(r0, r0 + min(tm, MM_ROW_CHUNK)) for r0 in range(0, tm, min(tm, MM_ROW_CHUNK))]

        def part(rows):
            return dot(a_ref[:, rows] if mode == "tn" else a_ref[rows, :], b_ref[...])

        def finish(r, rows):
            exs = [e[...] if kind == "row" else e[rows, :] for (kind, _), e in zip(extras, ex_refs)]
            vals = epi(r, *exs) if epi is not None else [r]
            for o, v in zip(o_refs, vals):
                o[rows, :] = v.astype(o.dtype)

        if nk == 1:
            for rows in chunks:
                finish(part(rows), rows)
            return
        acc = rest[n_ex + n_dep + n_o]

        @pl.when(k == 0)
        def _():
            for rows in chunks:
                acc[rows, :] = part(rows)

        @pl.when((k > 0) & (k < nk - 1))
        def _():
            for rows in chunks:
                acc[rows, :] += part(rows)

        @pl.when(k == nk - 1)
        def _():
            for rows in chunks:
                finish(acc[rows, :] + part(rows), rows)

    res = pl.pallas_call(
        body,
        grid=(m // tm, n // tn, nk),
        in_specs=[a_spec, b_spec] + ex_specs + [pl.BlockSpec(memory_space=pl.ANY)] * n_dep,
        out_specs=[pl.BlockSpec((tm, tn), lambda i, j, k: (i, j + j_out)) for _ in outs],
        out_shape=[jax.ShapeDtypeStruct((m, out_cols or n), dt) for dt in outs],
        input_output_aliases={} if into is None else {2 + n_ex + n_dep - 1: 0},
        scratch_shapes=[pltpu.VMEM((tm, tn), F32)] if nk > 1 else [],
        name=name,
        compiler_params=_cparams(("parallel", "parallel", "arbitrary")),
    )(a, b, *ex_arrays, *deps)
    return res if len(outs) > 1 else res[0]


def _rows(body, n_rows, tr, ins, outs, name, scratch=()):
    def spec(kind, shape):
        if kind == "blk":
            return pl.BlockSpec((tr,) + tuple(shape[1:]), lambda i: (i,) + (0,) * (len(shape) - 1))
        if kind == "dep":
            return pl.BlockSpec(memory_space=pl.ANY)
        if kind == "str":
            return pl.BlockSpec((shape[0], tr // shape[0], shape[2]), lambda i: (0, i, 0))
        return pl.BlockSpec(tuple(shape), lambda i: (0,) * len(shape))

    return pl.pallas_call(
        body,
        grid=(n_rows // tr,),
        in_specs=[spec(k, a.shape) for k, a in ins],
        out_specs=[spec(k, s) for k, s, _ in outs],
        out_shape=[jax.ShapeDtypeStruct(tuple(s), d) for _, s, d in outs],
        scratch_shapes=list(scratch),
        name=name,
        compiler_params=_cparams(("arbitrary",)),
    )(*[a for _, a in ins])


def _ln_stats(z):
    mu = jnp.mean(z, axis=-1, keepdims=True)
    zc = z - mu
    var = jnp.mean(zc * zc, axis=-1, keepdims=True)
    rstd = lax.rsqrt(var + LN_EPS)
    return zc * rstd, rstd


def _stream_scratch(c):
    return pltpu.VMEM((c // LANES, ROW_TILE, LANES), F32)


def _streams_in(ref3, scr):
    dil, n, c = ref3.shape
    for r in range(dil):
        for j in range(c // LANES):
            scr.at[j][pl.ds(r, n, stride=dil), :] = ref3[r, :, j * LANES:(j + 1) * LANES].astype(F32)
    return jnp.concatenate([scr[j] for j in range(c // LANES)], axis=1)


def _streams_out(val, ref3, scr):
    dil, n, c = ref3.shape
    for j in range(c // LANES):
        scr[j] = val[:, j * LANES:(j + 1) * LANES].astype(F32)
    for r in range(dil):
        for j in range(c // LANES):
            ref3[r, :, j * LANES:(j + 1) * LANES] = scr.at[j][pl.ds(r, n, stride=dil), :].astype(ref3.dtype)


def _mod(x, scale, shift, after, name):
    s, d = x.shape

    def body(x_ref, sc_ref, sh_ref, dep_ref, h_ref):
        h_ref[...] = (x_ref[...] * (1.0 + sc_ref[...]) + sh_ref[...]).astype(h_ref.dtype)

    return _rows(body, s, ROW_TILE, [("blk", x), ("all", scale), ("all", shift), ("dep", after)], [("blk", (s, d), MXU_DTYPE)], name)[0]


def _resid_ln(x, y, gate, g, b, nxt, name, dils=()):
    s, d = x.shape

    def body(x_ref, y_ref, gate_ref, g_ref, b_ref, sc_ref, sh_ref, xn_ref, h_ref, *rest):
        z = ALPHA * x_ref[...] + gate_ref[...] * y_ref[...]
        xhat, _ = _ln_stats(z)
        xn = xhat * g_ref[...] + b_ref[...]
        xn_ref[...] = xn
        h = xn * (1.0 + sc_ref[...]) + sh_ref[...]
        h_ref[...] = h.astype(h_ref.dtype)
        for hs_ref in rest[:len(dils)]:
            _streams_out(h, hs_ref, rest[-1])

    return _rows(body, s, ROW_TILE,
                 [("blk", x), ("blk", y), ("all", gate), ("all", g), ("all", b), ("all", nxt[0]), ("all", nxt[1])],
                 [("blk", (s, d), F32), ("blk", (s, d), MXU_DTYPE)] + [("str", (dil, s // dil, d), MXU_DTYPE) for dil in dils], name,
                 scratch=[_stream_scratch(d)] if dils else [])


def _mod_bwd(dxr, dhs, x, scale, name, after=None):
    s, d = x.shape
    n_dh = len(dhs)
    n_dep = 0 if after is None else 1

    def body(dxr_ref, *rest):
        dh_refs = rest[:n_dh]
        x_ref, sc_ref, dx_ref, red_ref, a_sh, a_sc = rest[n_dh:n_dh + 2] + rest[n_dh + 2 + n_dep:]
        i = pl.program_id(0)

        @pl.when(i == 0)
        def _():
            a_sh[...] = jnp.zeros_like(a_sh)
            a_sc[...] = jnp.zeros_like(a_sc)

        dh = dh_refs[0][...]
        for r in dh_refs[1:]:
            dh = dh + r[...]
        dx_ref[...] = dxr_ref[...] + dh * (1.0 + sc_ref[...])
        a_sh[...] += _fold8(dh)
        a_sc[...] += _fold8(dh * x_ref[...])

        @pl.when(i == pl.num_programs(0) - 1)
        def _():
            red_ref[...] = jnp.zeros_like(red_ref)
            red_ref[0:1, :] = jnp.sum(a_sh[...], axis=0, keepdims=True)
            red_ref[1:2, :] = jnp.sum(a_sc[...], axis=0, keepdims=True)

    return _rows(body, s, ROW_TILE, [("blk", dxr)] + [("blk", h) for h in dhs] + [("blk", x), ("all", scale)] + [("dep", after)] * n_dep,
                 [("blk", (s, d), F32), ("all", (SUBLANES, d), F32)], name,
                 scratch=[pltpu.VMEM((SUBLANES, d), F32)] * 2)


def _last_ln_loss_bwd(x, y, gate, g, b, target, name):
    s, d = x.shape

    def body(x_ref, y_ref, gate_ref, g_ref, b_ref, t_ref, l_ref, dxr_ref, dyy_ref, red_ref, a_l, a_g, a_b, a_gate):
        i = pl.program_id(0)

        @pl.when(i == 0)
        def _():
            for a in (a_l, a_g, a_b, a_gate):
                a[...] = jnp.zeros_like(a)

        yv = y_ref[...]
        z = ALPHA * x_ref[...] + gate_ref[...] * yv
        xhat, rstd = _ln_stats(z)
        e = xhat * g_ref[...] + b_ref[...] - t_ref[...]
        a_l[...] += _fold8(e * e)
        dxo_v = e * (1.0 / d)
        dxh = dxo_v * g_ref[...]
        dz = rstd * (dxh - jnp.mean(dxh, axis=-1, keepdims=True) - xhat * jnp.mean(dxh * xhat, axis=-1, keepdims=True))
        dxr_ref[...] = ALPHA * dz
        dyy_ref[...] = (gate_ref[...] * dz).astype(dyy_ref.dtype)
        a_g[...] += _fold8(dxo_v * xhat)
        a_b[...] += _fold8(dxo_v)
        a_gate[...] += _fold8(dz * yv)

        @pl.when(i == pl.num_programs(0) - 1)
        def _():
            l_ref[...] = jnp.full(l_ref.shape, 0.5 / d, F32) * jnp.sum(a_l[...])
            red_ref[...] = jnp.zeros_like(red_ref)
            red_ref[0:1, :] = jnp.sum(a_g[...], axis=0, keepdims=True)
            red_ref[1:2, :] = jnp.sum(a_b[...], axis=0, keepdims=True)
            red_ref[2:3, :] = jnp.sum(a_gate[...], axis=0, keepdims=True)

    l, dxr, dyy, red = _rows(
        body, s, ROW_TILE, [("blk", x), ("blk", y), ("all", gate), ("all", g), ("all", b), ("blk", target)],
        [("all", (SUBLANES, LANES), F32), ("blk", (s, d), F32), ("blk", (s, d), MXU_DTYPE), ("all", (SUBLANES, d), F32)], name,
        scratch=[pltpu.VMEM((SUBLANES, d), F32)] * 4)
    return l[0, 0], dxr, dyy, red


def _mod_ln_bwd(dxr, dhs, x, scale, x_in, y, gate, g, name, after=None):
    s, d = x.shape
    n_dh = len(dhs)
    n_dep = 0 if after is None else 1

    def body(dxr_ref, *rest):
        dh_refs = rest[:n_dh]
        x_ref, sc_ref, xin_ref, y_ref, gate_ref, g_ref = rest[n_dh:n_dh + 6]
        dxr_out, dyy_ref, red_mod, red_ln, a_sh, a_sc, a_g, a_b, a_gate = rest[n_dh + 6 + n_dep:n_dh + 15 + n_dep]
        i = pl.program_id(0)

        @pl.when(i == 0)
        def _():
            for a in (a_sh, a_sc, a_g, a_b, a_gate):
                a[...] = jnp.zeros_like(a)

        dh = dh_refs[0][...]
        for r in dh_refs[1:]:
            dh = dh + (r[...] if len(r.shape) == 2 else _streams_in(r, rest[-1]))
        xv = x_ref[...]
        dxo_v = dxr_ref[...] + dh * (1.0 + sc_ref[...])
        a_sh[...] += _fold8(dh)
        a_sc[...] += _fold8(dh * xv)
        yv = y_ref[...]
        z = ALPHA * xin_ref[...] + gate_ref[...] * yv
        xhat, rstd = _ln_stats(z)
        dxh = dxo_v * g_ref[...]
        dz = rstd * (dxh - jnp.mean(dxh, axis=-1, keepdims=True) - xhat * jnp.mean(dxh * xhat, axis=-1, keepdims=True))
        dxr_out[...] = ALPHA * dz
        dyy_ref[...] = (gate_ref[...] * dz).astype(dyy_ref.dtype)
        a_g[...] += _fold8(dxo_v * xhat)
        a_b[...] += _fold8(dxo_v)
        a_gate[...] += _fold8(dz * yv)

        @pl.when(i == pl.num_programs(0) - 1)
        def _():
            red_mod[...] = jnp.zeros_like(red_mod)
            red_mod[0:1, :] = jnp.sum(a_sh[...], axis=0, keepdims=True)
            red_mod[1:2, :] = jnp.sum(a_sc[...], axis=0, keepdims=True)
            red_ln[...] = jnp.zeros_like(red_ln)
            red_ln[0:1, :] = jnp.sum(a_g[...], axis=0, keepdims=True)
            red_ln[1:2, :] = jnp.sum(a_b[...], axis=0, keepdims=True)
            red_ln[2:3, :] = jnp.sum(a_gate[...], axis=0, keepdims=True)

    ins = ([("blk", dxr)] + [("blk" if h.ndim == 2 else "str", h) for h in dhs]
           + [("blk", x), ("all", scale), ("blk", x_in), ("blk", y), ("all", gate), ("all", g)] + [("dep", after)] * n_dep)
    return _rows(body, s, ROW_TILE, ins,
                 [("blk", (s, d), F32), ("blk", (s, d), MXU_DTYPE), ("all", (SUBLANES, d), F32), ("all", (SUBLANES, d), F32)], name,
                 scratch=[pltpu.VMEM((SUBLANES, d), F32)] * 5 + [_stream_scratch(d)] * any(h.ndim == 3 for h in dhs))


def _left_half(shape):
    return lax.broadcasted_iota(jnp.int32, shape, 1) < (LANES // 2)


CHUNKS_PER_STEP = 2


def _chunks_of_step():
    return [slice(i * CHUNK, (i + 1) * CHUNK) for i in range(CHUNKS_PER_STEP)]


def _spatial_z(vn, wc_ref, bias_ref, j):
    vb = vn[:, j * LANES:(j + 1) * LANES]
    z0 = _dot_nn(wc_ref[2 * j], vb)
    z1 = _dot_nn(wc_ref[2 * j + 1], vb)
    return jnp.where(_left_half(z0.shape), z0, z1) + bias_ref[:, j * LANES:(j + 1) * LANES]


def _spatial_fwd(uvpre, vn_g, vn_b, wc, bias_full, name):
    s, d2 = uvpre.shape
    d = d2 // 2

    def body(uv_ref, g_ref, b_ref, wc_ref, bias_ref, out_ref):
        for rows in _chunks_of_step():
            u = _gelu(uv_ref[rows, :d])
            v = _gelu(uv_ref[rows, d:])
            vh, _ = _ln_stats(v)
            vn = vh * g_ref[...] + b_ref[...]
            for j in range(d // LANES):
                z = _spatial_z(vn, wc_ref, bias_ref, j)
                out_ref[rows, j * LANES:(j + 1) * LANES] = (u[:, j * LANES:(j + 1) * LANES] * z).astype(out_ref.dtype)

    return _rows(body, s, CHUNKS_PER_STEP * CHUNK, [("blk", uvpre), ("all", vn_g), ("all", vn_b), ("all", wc), ("all", bias_full)],
                 [("blk", (s, d), MXU_DTYPE)], name)[0]


def _spatial_bwd(uvpre, dgated, vn_g, vn_b, wc, wct, bias_full, name):
    s, d2 = uvpre.shape
    d = d2 // 2

    def body(uv_ref, dg_ref, g_ref, b_ref, wc_ref, wct_ref, bias_ref,
             duv_ref, dws_ref, dbias_ref, dbin_ref, dvg_ref, dvb_ref, dvn_buf, a_bin, a_vg, a_vb):
        i = pl.program_id(0)

        @pl.when(i == 0)
        def _():
            dws_ref[...] = jnp.zeros_like(dws_ref)
            dbias_ref[...] = jnp.zeros_like(dbias_ref)
            a_bin[...] = jnp.zeros_like(a_bin)
            a_vg[...] = jnp.zeros_like(a_vg)
            a_vb[...] = jnp.zeros_like(a_vb)

        for rows in _chunks_of_step():
            u, u_grad = _gelu_and_grad(uv_ref[rows, :d])
            v, v_grad = _gelu_and_grad(uv_ref[rows, d:])
            vh, rstd = _ln_stats(v)
            vn = vh * g_ref[...] + b_ref[...]
            dg = dg_ref[rows, :]
            dzz = dg * u
            dbias_ref[...] += dzz
            for j in range(d // LANES):
                cols = slice(j * LANES, (j + 1) * LANES)
                z = _spatial_z(vn, wc_ref, bias_ref, j)
                dup = dg[:, cols] * z * u_grad[:, cols]
                duv_ref[rows, cols] = dup.astype(duv_ref.dtype)
                a_bin[:, cols] += _fold8(dup)
                dzb = dzz[:, cols]
                left = _left_half(dzb.shape)
                dvn_buf[:, cols] = jnp.where(left, _dot_nn(wct_ref[2 * j], dzb), _dot_nn(wct_ref[2 * j + 1], dzb))
                vb = vn[:, cols]
                dws_ref[2 * j] += _dot_nt(jnp.where(left, dzb, 0.0), vb)
                dws_ref[2 * j + 1] += _dot_nt(jnp.where(left, 0.0, dzb), vb)
            dvn = dvn_buf[...]
            a_vg[...] += _fold8(dvn * vh)
            a_vb[...] += _fold8(dvn)
            dvh = dvn * g_ref[...]
            dv = rstd * (dvh - jnp.mean(dvh, axis=-1, keepdims=True) - vh * jnp.mean(dvh * vh, axis=-1, keepdims=True))
            dvp = dv * v_grad
            duv_ref[rows, d:] = dvp.astype(duv_ref.dtype)
            a_bin[:, d:] += _fold8(dvp)

        @pl.when(i == pl.num_programs(0) - 1)
        def _():
            dbin_ref[...] = jnp.sum(a_bin[...], axis=0, keepdims=True)
            dvg_ref[...] = jnp.sum(a_vg[...], axis=0, keepdims=True)
            dvb_ref[...] = jnp.sum(a_vb[...], axis=0, keepdims=True)

    return _rows(body, s, CHUNKS_PER_STEP * CHUNK,
                 [("blk", uvpre), ("blk", dgated), ("all", vn_g), ("all", vn_b), ("all", wc), ("all", wct), ("all", bias_full)],
                 [("blk", (s, d2), MXU_DTYPE), ("all", (A_GROUPS, CHUNK, CHUNK), F32), ("all", (CHUNK, d), F32),
                  ("all", (1, d2), F32), ("all", (1, d), F32), ("all", (1, d), F32)], name,
                 scratch=[pltpu.VMEM((CHUNK, d), F32), pltpu.VMEM((SUBLANES, d2), F32),
                          pltpu.VMEM((SUBLANES, d), F32), pltpu.VMEM((SUBLANES, d), F32)])


def _head_mask(v, h):
    lane = lax.broadcasted_iota(jnp.int32, v.shape, 1)
    return jnp.where((lane >= h * HEAD_DIM) & (lane < (h + 1) * HEAD_DIM), v, jnp.zeros_like(v))


def _att_bias(slopes, dil):
    qi = lax.broadcasted_iota(jnp.int32, (SPAN, SPAN), 0)
    ki = lax.broadcasted_iota(jnp.int32, (SPAN, SPAN), 1)
    sl = slopes[:, None, None]
    cur = jnp.where(ki <= qi, -sl * (float(dil) * (qi - ki).astype(F32)), NEG)
    prev = jnp.where(ki >= qi, -sl * (float(dil) * (SPAN + qi - ki).astype(F32)), NEG)
    absent = jnp.full_like(prev, NEG)
    pairs = slopes.shape[0] // 2

    def fwd(pv):
        return jnp.concatenate([cur, pv], axis=2).reshape(pairs, 2 * SPAN, 2 * SPAN)

    def bwd(pv):
        return jnp.concatenate([cur.reshape(pairs, 2 * SPAN, SPAN), pv.reshape(pairs, 2 * SPAN, SPAN)], axis=1)

    return jnp.stack([fwd(absent), fwd(prev)]), jnp.stack([bwd(absent), bwd(prev)])


ATT_GROUP = 4


def _att_group(s, dil):
    nb = s // (dil * SPAN)
    grp = min(ATT_GROUP, nb)
    assert nb % grp == 0
    return nb, grp


def _att_specs(s, d, dil, kinds):
    nb, grp = _att_group(s, dil)

    def spec(part, which):
        if which == "group":
            return pl.BlockSpec((grp * SPAN, d), lambda b: (b, part))
        if which == "prev":
            return pl.BlockSpec((SPAN, d), lambda b: (jnp.where((grp * b) % nb == 0, grp * b, grp * b - 1), part))
        return pl.BlockSpec((SPAN, d), lambda b: (jnp.where((grp * b + grp - 1) % nb == nb - 1, grp * b + grp - 1, grp * b + grp), part))

    return [spec(part, which) for part, which in kinds]


def _head_col(v, head):
    return v[:, head:head + 1]


def _expand_heads(w, j):
    shape = (w.shape[0], LANES)
    return jnp.where(_left_half(shape), jnp.broadcast_to(_head_col(w, 2 * j), shape), jnp.broadcast_to(_head_col(w, 2 * j + 1), shape))


def _attn_fwd(qkv, slopes, dil, name):
    s, d3 = qkv.shape
    d = d3 // 3
    nb, grp = _att_group(s, dil)
    table, _ = _att_bias(slopes, dil)

    def body(q_ref, k_ref, kp_ref, v_ref, vp_ref, tb_ref, o_ref, l_ref):
        b = pl.program_id(0)
        left = _left_half((SPAN, LANES))
        lane = lax.broadcasted_iota(jnp.int32, (SPAN, LANES), 1)
        for sub in range(grp):
            rows, before = slice(sub * SPAN, (sub + 1) * SPAN), slice((sub - 1) * SPAN, sub * SPAN)
            variant = jnp.where((grp * b) % nb == 0, 0, 1) if sub == 0 else 1
            lses = jnp.zeros((SPAN, LANES), F32)
            for hp in range(d // LANES):
                cols = slice(hp * LANES, (hp + 1) * LANES)
                q = q_ref[rows, cols]
                q2 = jnp.concatenate([_head_mask(q, 0), _head_mask(q, 1)], axis=0) * ATT_SCALE
                k2 = jnp.concatenate([k_ref[rows, cols], kp_ref[:, cols] if sub == 0 else k_ref[before, cols]], axis=0)
                v2 = jnp.concatenate([v_ref[rows, cols], vp_ref[:, cols] if sub == 0 else v_ref[before, cols]], axis=0)
                sc = _dot_nt(q2, k2) + tb_ref[variant, hp]
                m = jnp.max(sc, axis=-1, keepdims=True)
                p = jnp.exp(sc - m)
                l = jnp.sum(p, axis=-1, keepdims=True)
                r = _dot_nn(p, v2) * (1.0 / l)
                lse = m + jnp.log(l)
                o_ref[rows, cols] = jnp.where(left, r[:SPAN], r[SPAN:])
                lses = jnp.where(lane == 2 * hp, lse[:SPAN], jnp.where(lane == 2 * hp + 1, lse[SPAN:], lses))
            l_ref[rows, :] = lses

    specs = _att_specs(s, d, dil, [(0, "group"), (1, "group"), (1, "prev"), (2, "group"), (2, "prev")])
    return pl.pallas_call(
        body,
        grid=(s // (grp * SPAN),),
        in_specs=specs + [pl.BlockSpec(table.shape, lambda b: (0, 0, 0, 0))],
        out_specs=[pl.BlockSpec((grp * SPAN, d), lambda b: (b, 0)), pl.BlockSpec((grp * SPAN, LANES), lambda b: (b, 0))],
        out_shape=[jax.ShapeDtypeStruct((s, d), F32), jax.ShapeDtypeStruct((s, LANES), F32)],
        name=name,
        compiler_params=_cparams(("parallel",)),
    )(qkv, qkv, qkv, qkv, qkv, table)


def _attn_bwd(qkv, do, lse, dd, slopes, dil, name):
    s, d3 = qkv.shape
    d = d3 // 3
    nb, grp = _att_group(s, dil)
    _, table = _att_bias(slopes, dil)

    def heads_stacked(cur, nxt):
        return jnp.concatenate([_head_mask(cur, 0), _head_mask(cur, 1), _head_mask(nxt, 0), _head_mask(nxt, 1)], axis=0)

    def cols_stacked(cur, nxt, hp):
        return jnp.concatenate([jnp.broadcast_to(_head_col(a, 2 * hp + h), (SPAN, LANES)) for a in (cur, nxt) for h in range(2)], axis=0)

    def body(k_ref, v_ref, q_ref, qn_ref, do_ref, don_ref, l_ref, ln_ref, dd_ref, ddn_ref, tb_ref, out_ref, carry):
        b = pl.program_id(0)

        @pl.when(b == 0)
        def _():
            carry[...] = jnp.zeros_like(carry)

        left = _left_half((SPAN, LANES))
        for sub in range(grp):
            rows, after = slice(sub * SPAN, (sub + 1) * SPAN), slice((sub + 1) * SPAN, (sub + 2) * SPAN)
            last = sub == grp - 1
            variant = jnp.where((grp * b + sub) % nb == nb - 1, 0, 1) if last else 1
            lse_c, dd_c = l_ref[rows, :], dd_ref[rows, :]
            lse_n, dd_n = (ln_ref[...], ddn_ref[...]) if last else (l_ref[after, :], dd_ref[after, :])
            for hp in range(d // LANES):
                cols = slice(hp * LANES, (hp + 1) * LANES)
                k, v = k_ref[rows, cols], v_ref[rows, cols]
                q4 = heads_stacked(q_ref[rows, cols], qn_ref[:, cols] if last else q_ref[after, cols])
                do4 = heads_stacked(do_ref[rows, cols], don_ref[:, cols] if last else do_ref[after, cols])
                sc = _dot_nt(q4 * ATT_SCALE, k) + tb_ref[variant, hp]
                p = jnp.exp(sc - cols_stacked(lse_c, lse_n, hp))
                ds = p * (_dot_nt(do4, v) - cols_stacked(dd_c, dd_n, hp))
                dq4 = _dot_nn(ds, k)
                dq_cur = jnp.where(left, dq4[:SPAN], dq4[SPAN:2 * SPAN]) + carry[:, cols]
                carry[:, cols] = jnp.where(left, dq4[2 * SPAN:3 * SPAN], dq4[3 * SPAN:])
                out_ref[rows, cols] = (dq_cur * ATT_SCALE).astype(out_ref.dtype)
                out_ref[rows, d + hp * LANES:d + (hp + 1) * LANES] = (_dot_tn(ds, q4) * ATT_SCALE).astype(out_ref.dtype)
                out_ref[rows, 2 * d + hp * LANES:2 * d + (hp + 1) * LANES] = _dot_tn(p, do4).astype(out_ref.dtype)

    qkv_specs = _att_specs(s, d, dil, [(1, "group"), (2, "group"), (0, "group"), (0, "next")])
    wide = _att_specs(s, d, dil, [(0, "group"), (0, "next")])
    heads = _att_specs(s, LANES, dil, [(0, "group"), (0, "next")])
    return pl.pallas_call(
        body,
        grid=(s // (grp * SPAN),),
        in_specs=qkv_specs + wide + heads + heads + [pl.BlockSpec(table.shape, lambda b: (0, 0, 0, 0))],
        out_specs=pl.BlockSpec((grp * SPAN, d3), lambda b: (b, 0)),
        out_shape=jax.ShapeDtypeStruct((s, d3), MXU_DTYPE),
        scratch_shapes=[pltpu.VMEM((SPAN, d), F32)],
        name=name,
        compiler_params=_cparams(("arbitrary",)),
    )(qkv, qkv, qkv, qkv, do, do, lse, lse, dd, dd, table)


def _mix_weights(l_refs):
    ls = [r[...] for r in l_refs]
    m = functools.reduce(jnp.maximum, ls)
    es = [jnp.exp(l - m) for l in ls]
    tot = functools.reduce(lambda a, c: a + c, es)
    return [e / tot for e in es]


def _combine_fwd(os_, ls_, name):
    s, d = ls_[0].shape[0], os_[0].shape[-1]
    n = len(os_)
    n_str = sum(o.ndim == 3 for o in os_)

    def body(*refs):
        o_refs, l_refs, out_ref, scrs = refs[:n], refs[n:2 * n], refs[2 * n], list(refs[2 * n + 1:])
        ws = _mix_weights(l_refs)
        os_v = [o if len(o.shape) == 2 else _streams_in(o, scrs.pop()) for o in o_refs]
        for j in range(d // LANES):
            cols = slice(j * LANES, (j + 1) * LANES)
            acc = _expand_heads(ws[0], j) * os_v[0][:, cols]
            for w, o in zip(ws[1:], os_v[1:]):
                acc = acc + _expand_heads(w, j) * o[:, cols]
            out_ref[:, cols] = acc

    return _rows(body, s, ROW_TILE, [("blk" if a.ndim == 2 else "str", a) for a in os_] + [("blk", a) for a in ls_],
                 [("blk", (s, d), F32)], name, scratch=[_stream_scratch(d)] * n_str)[0]


def _combine_bwd(do, o, ls_, dils, name):
    s, d = o.shape
    n = len(ls_)
    sel = (lax.broadcasted_iota(jnp.int32, (d, LANES), 0) // HEAD_DIM == lax.broadcasted_iota(jnp.int32, (d, LANES), 1)).astype(F32)

    def body(do_ref, o_ref, *rest):
        l_refs, sel_ref, outs = rest[:n], rest[n], rest[n + 1:n + 1 + 2 * n]
        ws = _mix_weights(l_refs)
        dov = do_ref[...]
        r = jnp.dot(dov * o_ref[...], sel_ref[...], precision=lax.Precision.HIGHEST, preferred_element_type=F32)
        for g in range(n):
            outs[2 * g + 1][...] = ws[g] * r
            parts = [_expand_heads(ws[g], j) * dov[:, j * LANES:(j + 1) * LANES] for j in range(d // LANES)]
            if dils[g] == 1:
                for j, part in enumerate(parts):
                    outs[2 * g][:, j * LANES:(j + 1) * LANES] = part.astype(outs[2 * g].dtype)
            else:
                _streams_out(jnp.concatenate(parts, axis=1), outs[2 * g], rest[-1])

    outs = []
    for dil in dils:
        outs += [("blk", (s, d), MXU_DTYPE) if dil == 1 else ("str", (dil, s // dil, d), MXU_DTYPE), ("blk", (s, LANES), F32)]
    res = _rows(body, s, ROW_TILE, [("blk", do), ("blk", o)] + [("blk", l) for l in ls_] + [("all", sel)], outs, name,
                scratch=[_stream_scratch(d)])
    return [(res[2 * g], res[2 * g + 1]) for g in range(n)]


def _ada_fwd(c_all, w, b, name):
    nsub, d, cs = w.shape

    def body(c_ref, w_ref, b_ref, o_ref):
        cv = c_ref[...]
        sc = cv * (1.0 / (1.0 + jnp.exp(-cv)))
        o_ref[...] = _dot_nn(sc, w_ref[...]) + b_ref[...]

    return pl.pallas_call(
        body,
        grid=(nsub,),
        in_specs=[pl.BlockSpec(c_all.shape, lambda i: (0, 0)), pl.BlockSpec((None, d, cs), lambda i: (i, 0, 0)),
                  pl.BlockSpec((None, 1, cs), lambda i: (i, 0, 0))],
        out_specs=pl.BlockSpec((None, N_DEV, cs), lambda i: (i, 0, 0)),
        out_shape=jax.ShapeDtypeStruct((nsub, N_DEV, cs), F32),
        name=name,
        compiler_params=_cparams(("parallel",)),
    )(c_all, w, b)


def _ada_bwd(c_all_t, dm, name):
    d, nb = c_all_t.shape
    nsub, _, cs = dm.shape

    def body(c_ref, dm_ref, o_ref):
        cv = c_ref[...]
        sc = cv * (1.0 / (1.0 + jnp.exp(-cv)))
        acc = sc[:, 0:1] * dm_ref[0:1, :]
        for bi in range(1, nb):
            acc = acc + sc[:, bi:bi + 1] * dm_ref[bi:bi + 1, :]
        o_ref[...] = acc

    return pl.pallas_call(
        body,
        grid=(nsub,),
        in_specs=[pl.BlockSpec(c_all_t.shape, lambda i: (0, 0)), pl.BlockSpec((None, nb, cs), lambda i: (i, 0, 0))],
        out_specs=pl.BlockSpec((None, d, cs), lambda i: (i, 0, 0)),
        out_shape=jax.ShapeDtypeStruct((nsub, d, cs), F32),
        name=name,
        compiler_params=_cparams(("parallel",)),
    )(c_all_t, dm)


def _row_tile(r, row_elems, block_elems=256 * 1024):
    t = 2 * SUBLANES
    if r % t:
        return r
    while t * 2 * row_elems <= block_elems and r % (t * 2) == 0:
        t *= 2
    return t


def _adamw(w, g, m, v, name):
    shape = w.shape
    c = shape[-1]
    r = w.size // c
    tr = _row_tile(r, c, 512 * 1024)
    w2, g2, m2, v2 = [a.reshape(r, c) for a in (w, g, m, v)]
    bc1 = 1.0 - ADAM_B1 ** ADAM_STEP
    bc2 = 1.0 - ADAM_B2 ** ADAM_STEP

    def body(w_ref, g_ref, m_ref, v_ref, d_ref, nm_ref, nv_ref):
        gv = g_ref[...]
        nm = ADAM_B1 * m_ref[...] + (1.0 - ADAM_B1) * gv
        nv = ADAM_B2 * v_ref[...] + (1.0 - ADAM_B2) * (gv * gv)
        d_ref[...] = -ADAM_LR * ((nm / bc1) / (jnp.sqrt(nv / bc2) + ADAM_EPS) + ADAM_WD * w_ref[...])
        nm_ref[...] = nm
        nv_ref[...] = nv

    res = _rows(body, r, tr, [("blk", a) for a in (w2, g2, m2, v2)], [("blk", (r, c), F32)] * 3, name)
    return [a.reshape(shape) for a in res]


def _sum_slots(buf, name):
    n, r, c = buf.shape
    tr = _row_tile(r, n * c, 2 * 1024 * 1024)

    def body(b_ref, o_ref):
        acc = b_ref[0].astype(F32)
        for k in range(1, n):
            acc = acc + b_ref[k].astype(F32)
        o_ref[...] = acc

    return pl.pallas_call(
        body,
        grid=(r // tr,),
        in_specs=[pl.BlockSpec((n, tr, c), lambda i: (0, i, 0))],
        out_specs=pl.BlockSpec((tr, c), lambda i: (i, 0)),
        out_shape=jax.ShapeDtypeStruct((r, c), F32),
        name=name,
        compiler_params=_cparams(("parallel",)),
    )(buf)


def _me():
    return lax.axis_index("x"), lax.axis_index("y"), lax.axis_index("c")


def _all_gather_small(blk, name, after=()):
    m_per, n = blk.shape

    def body(x_ref, *rest):
        out_ref, send_sems, recv_sems, local_sem = rest[len(after):]
        x, y, c = _me()
        me, sibling = (x, y, c), (x, y, 1 - c)
        chips = [(1 - x, y), (x, 1 - y), (1 - x, 1 - y)]

        def rows(px, py, pc):
            return out_ref.at[pl.ds((4 * px + 2 * py + pc) * m_per, m_per), :]

        def copy(k, block, to, src=None):
            return pltpu.make_async_remote_copy(
                src_ref=rows(*block) if src is None else src, dst_ref=rows(*block),
                send_sem=send_sems.at[k], recv_sem=recv_sems.at[k], device_id=to, device_id_type=MESH)

        mine = pltpu.make_async_copy(x_ref, rows(*me), local_sem)
        mine.start()
        first = [copy(0, me, sibling, src=x_ref)]
        first += [copy(1 + j, me, (*chip, c), src=x_ref) for j, chip in enumerate(chips)]
        for cp in first:
            cp.start()
        passed = [copy(4 + j, (*chip, c), sibling) for j, chip in enumerate(chips)]
        for j, chip in enumerate(chips):
            copy(1 + j, (*chip, c), me).wait_recv()
            passed[j].start()
        copy(0, sibling, me).wait_recv()
        for j, chip in enumerate(chips):
            copy(4 + j, (*chip, 1 - c), me).wait_recv()
        for cp in first + passed:
            cp.wait_send()
        mine.wait()

    return pl.pallas_call(
        body,
        out_shape=jax.ShapeDtypeStruct((N_DEV * m_per, n), blk.dtype),
        in_specs=[pl.BlockSpec(memory_space=pltpu.VMEM)] + [pl.BlockSpec(memory_space=pl.ANY)] * len(after),
        out_specs=pl.BlockSpec(memory_space=pltpu.VMEM),
        scratch_shapes=[pltpu.SemaphoreType.DMA((7,)), pltpu.SemaphoreType.DMA((7,)), pltpu.SemaphoreType.DMA],
        name=name,
        compiler_params=pltpu.CompilerParams(vmem_limit_bytes=VMEM_LIMIT),
    )(blk, *after)


_HBM = pl.BlockSpec(memory_space=pltpu.HBM)
_SEM = pl.BlockSpec(memory_space=pltpu.SEMAPHORE)
_EFFECT = pltpu.SideEffectType.DATAFLOW_SIDE_EFFECTING


def _other_chips(x, y):
    return [(1 - x, y), (x, 1 - y), (1 - x, 1 - y)]


def _gather_copy(w, j, src_ref, land_ref, send_sems, recv_sems, halved=False):
    x, y, c = _me()
    if halved:
        half = src_ref.shape[0] // 2
        src_ref = src_ref.at[pl.ds(c * half, half), :]
    return pltpu.make_async_remote_copy(
        src_ref=src_ref, dst_ref=land_ref.at[2 * x + y], send_sem=send_sems.at[3 * w + j], recv_sem=recv_sems.at[3 * w + j],
        device_id=(*_other_chips(x, y)[j], c), device_id_type=MESH)


def _gather_start(shards, halved, after, name):
    n = len(shards)
    lands = [lax.empty((N_CHIPS, s.shape[0] // 2 if w in halved else s.shape[0], s.shape[1]), s.dtype) for w, s in enumerate(shards)]

    def body(*refs):
        in_refs, land_refs = refs[:n], refs[n:2 * n]
        send_sems, recv_sems = refs[2 * n + 1], refs[2 * n + 2]
        token = refs[-1]
        for w in range(n):
            for j in range(3):
                _gather_copy(w, j, in_refs[w], land_refs[w], send_sems, recv_sems, w in halved).start()
        token[...] = jnp.zeros_like(token)

    res = pl.pallas_call(
        body,
        out_shape=(pltpu.SemaphoreType.DMA((3 * n,)), pltpu.SemaphoreType.DMA((3 * n,)),
                   *[pltpu.HBM(s.shape, s.dtype) for s in shards], *[pltpu.HBM(l.shape, l.dtype) for l in lands],
                   jax.ShapeDtypeStruct((SUBLANES, LANES), F32)),
        in_specs=[_HBM] * (2 * n) + [pl.BlockSpec(memory_space=pl.ANY)],
        out_specs=(_SEM, _SEM, *[_HBM] * (2 * n), pl.BlockSpec(memory_space=pltpu.VMEM)),
        input_output_aliases={i: 2 + i for i in range(2 * n)},
        name=name,
        compiler_params=pltpu.CompilerParams(has_side_effects=_EFFECT),
    )(*[pltpu.with_memory_space_constraint(a, pltpu.HBM) for a in list(shards) + lands], after)
    return res[0], res[1], res[2:2 + n], res[2 + n:2 + 2 * n], res[-1]


def _gather_wait(w, shard, land, send_sems, recv_sems, after, name, halved=False):
    def body(s_ref, land_ref, send_sems, recv_sems, after_ref, s_out, land_out, stage):
        x, y, _ = _me()
        if not halved:
            pltpu.sync_copy(s_ref, stage)
            pltpu.sync_copy(stage, land_out.at[2 * x + y])
        for j in range(3):
            cp = _gather_copy(w, j, s_ref, land_ref, send_sems, recv_sems, halved)
            cp.wait_send()
            cp.wait_recv()

    return pl.pallas_call(
        body,
        out_shape=(pltpu.HBM(shard.shape, shard.dtype), pltpu.HBM(land.shape, land.dtype)),
        in_specs=(_HBM, _HBM, _SEM, _SEM, pl.BlockSpec(memory_space=pl.ANY)),
        out_specs=(_HBM, _HBM),
        input_output_aliases={0: 0, 1: 1},
        scratch_shapes=[pltpu.VMEM((SUBLANES, LANES) if halved else shard.shape, shard.dtype)],
        name=name,
        compiler_params=pltpu.CompilerParams(has_side_effects=_EFFECT, vmem_limit_bytes=VMEM_LIMIT),
    )(shard, land, send_sems, recv_sems, after)


def _assemble_halves(shard, land, name):
    half = land.shape[1]

    def body(s_ref, land_ref, out_ref, send_sems, recv_sems, local_sems):
        x, y, c = _me()
        own = pltpu.make_async_copy(s_ref, out_ref.at[2 * x + y], local_sems.at[3])
        own.start()
        cps = []
        for j, (ox, oy) in enumerate(_other_chips(x, y)):
            qj = 2 * ox + oy
            mine = out_ref.at[qj, pl.ds(c * half, half), :]
            lc = pltpu.make_async_copy(land_ref.at[qj], mine, local_sems.at[j])
            lc.start()
            rc = pltpu.make_async_remote_copy(
                src_ref=land_ref.at[qj], dst_ref=mine, send_sem=send_sems.at[j], recv_sem=recv_sems.at[j],
                device_id=(x, y, 1 - c), device_id_type=MESH)
            rc.start()
            cps.append((lc, rc))
        for lc, rc in cps:
            rc.wait_recv()
        for lc, rc in cps:
            rc.wait_send()
            lc.wait()
        own.wait()

    vmem = pl.BlockSpec(memory_space=pltpu.VMEM)
    return pl.pallas_call(
        body,
        out_shape=jax.ShapeDtypeStruct((N_CHIPS,) + shard.shape, shard.dtype),
        in_specs=[vmem, vmem],
        out_specs=vmem,
        scratch_shapes=[pltpu.SemaphoreType.DMA((3,)), pltpu.SemaphoreType.DMA((3,)), pltpu.SemaphoreType.DMA((4,))],
        name=name,
        compiler_params=pltpu.CompilerParams(vmem_limit_bytes=VMEM_LIMIT),
    )(shard, land)


def _piece_shape(shape, kind):
    k, nn = shape
    if kind == "all":
        return (k, nn)
    return (k // 2, nn // N_CHIPS) if kind == "col" else (k // N_CHIPS // 2, nn)


def _piece_of(g_ref, kind, tq, tc):
    pr, pc = _piece_shape(g_ref.shape, kind)
    if kind == "all":
        return g_ref
    if kind == "col":
        return g_ref.at[pl.ds(tc * pr, pr), pl.ds(tq * pc, pc)]
    return g_ref.at[pl.ds((2 * tq + tc) * pr, pr), :]


def _scatter_copy(w, r, kind, g_ref, land_ref, send_sems, recv_sems):
    x, y, c = _me()
    tx, ty, tc = (x + ((r >> 2) & 1)) % 2, (y + ((r >> 1) & 1)) % 2, (c + (r & 1)) % 2
    return pltpu.make_async_remote_copy(
        src_ref=_piece_of(g_ref, kind, 2 * tx + ty, tc), dst_ref=land_ref.at[4 * x + 2 * y + c],
        send_sem=send_sems.at[N_DEV * w + r], recv_sem=recv_sems.at[N_DEV * w + r], device_id=(tx, ty, tc), device_id_type=MESH)


def _scatter_start(gs, kinds, name):
    n = len(gs)
    pieces = [_piece_shape(g.shape, kind) for g, kind in zip(gs, kinds)]
    lands = [lax.empty((N_DEV,) + p, g.dtype) for p, g in zip(pieces, gs)]

    def body(*refs):
        g_refs, land_refs, send_sems, recv_sems = refs[:n], refs[n:2 * n], refs[2 * n], refs[2 * n + 1]
        land_outs, stages = refs[3 * n + 2:4 * n + 2], refs[4 * n + 2:]
        x, y, c = _me()
        for w in range(n):
            for r in range(1, N_DEV):
                _scatter_copy(w, r, kinds[w], g_refs[w], land_refs[w], send_sems, recv_sems).start()
        for w in range(n):
            pltpu.sync_copy(_piece_of(g_refs[w], kinds[w], 2 * x + y, c), stages[w])
            pltpu.sync_copy(stages[w], land_outs[w].at[4 * x + 2 * y + c])

    arrays = list(gs) + lands
    res = pl.pallas_call(
        body,
        out_shape=(pltpu.SemaphoreType.DMA((N_DEV * n,)), pltpu.SemaphoreType.DMA((N_DEV * n,)),
                   *[pltpu.HBM(a.shape, a.dtype) for a in arrays]),
        in_specs=[_HBM] * (2 * n),
        out_specs=(_SEM, _SEM, *[_HBM] * (2 * n)),
        input_output_aliases={i: 2 + i for i in range(2 * n)},
        scratch_shapes=[pltpu.VMEM(p, g.dtype) for p, g in zip(pieces, gs)],
        name=name,
        compiler_params=pltpu.CompilerParams(has_side_effects=_EFFECT, vmem_limit_bytes=VMEM_LIMIT),
    )(*[pltpu.with_memory_space_constraint(a, pltpu.HBM) for a in arrays])
    return res[0], res[1], res[2:2 + n], res[2 + n:]


def _scatter_wait(send_sems, recv_sems, gs, lands, kinds, after, name):
    n = len(gs)

    def body(*refs):
        g_refs, land_refs, send_sems, recv_sems = refs[:n], refs[n:2 * n], refs[2 * n], refs[2 * n + 1]
        for w in range(n):
            for r in range(1, N_DEV):
                cp = _scatter_copy(w, r, kinds[w], g_refs[w], land_refs[w], send_sems, recv_sems)
                cp.wait_send()
                cp.wait_recv()

    arrays = list(gs) + list(lands)
    return pl.pallas_call(
        body,
        out_shape=tuple(pltpu.HBM(a.shape, a.dtype) for a in arrays),
        in_specs=(*[_HBM] * (2 * n), _SEM, _SEM, pl.BlockSpec(memory_space=pl.ANY)),
        out_specs=tuple([_HBM] * (2 * n)),
        input_output_aliases={i: i for i in range(2 * n)},
        name=name,
        compiler_params=pltpu.CompilerParams(has_side_effects=_EFFECT),
    )(*arrays, send_sems, recv_sems, after)[n:]


def _swap_halves(halves, name):
    n = len(halves)

    def body(*refs):
        in_refs, out_refs = refs[:n], refs[n:2 * n]
        send_sems, recv_sems, local_sems = refs[2 * n:]
        x, y, c = _me()
        cps = []
        for w in range(n):
            lc = pltpu.make_async_copy(in_refs[w], out_refs[w].at[c], local_sems.at[w])
            lc.start()
            rc = pltpu.make_async_remote_copy(
                src_ref=in_refs[w], dst_ref=out_refs[w].at[c], send_sem=send_sems.at[w], recv_sem=recv_sems.at[w],
                device_id=(x, y, 1 - c), device_id_type=MESH)
            rc.start()
            cps.append((lc, rc))
        for lc, rc in cps:
            rc.wait_recv()
        for lc, rc in cps:
            rc.wait_send()
            lc.wait()

    vmem = pl.BlockSpec(memory_space=pltpu.VMEM)
    return pl.pallas_call(
        body,
        out_shape=[jax.ShapeDtypeStruct((2,) + h.shape, h.dtype) for h in halves],
        in_specs=[vmem] * n,
        out_specs=[vmem] * n,
        scratch_shapes=[pltpu.SemaphoreType.DMA((n,)), pltpu.SemaphoreType.DMA((n,)), pltpu.SemaphoreType.DMA((n,))],
        name=name,
        compiler_params=pltpu.CompilerParams(vmem_limit_bytes=VMEM_LIMIT),
    )(*halves)


def _to_streams(a, dil):
    if dil == 1:
        return a
    s, c = a.shape
    return a.reshape(s // dil, dil, c).transpose(1, 0, 2).reshape(s, c)


def _from_streams(a, dil):
    if dil == 1:
        return a
    s, c = a.shape
    return a.reshape(dil, s // dil, c).transpose(1, 0, 2).reshape(s, c)


def _mm_tiles(s):
    return min(s, 2048)


def _local_step(x0, target, mvec, ln_g, ln_b, small, fetch, emit, start):
    s, d = x0.shape
    tm = _mm_tiles(s)
    row = lambda v: v.reshape(1, -1)
    shift = [row(mvec[i, :d]) for i in range(4)]
    scale = [row(mvec[i, d:2 * d]) for i in range(4)]
    gate = [row(1.0 + mvec[i, 2 * d:]) for i in range(4)]
    lg = [row(ln_g[i]) for i in range(4)]
    lb = [row(ln_b[i]) for i in range(4)]
    mm = functools.partial(_mm, tm=tm)
    mm_w = functools.partial(_mm, tm=1024, tk=min(s, 2048), mode="tn")

    def resid_ln_epilogue(sub):
        def epi(y, xv, gate_v, g_v, b_v, sc_v, sh_v):
            xhat, _ = _ln_stats(ALPHA * xv + gate_v * y)
            xn = xhat * g_v + b_v
            return [y, xn, xn * (1.0 + sc_v) + sh_v]

        rows = [gate[sub], lg[sub], lb[sub], scale[sub + 1], shift[sub + 1]]
        return dict(outs=[F32, F32, MXU_DTYPE], epi=epi, extras=[("full", xs[sub])] + [("row", r) for r in rows])

    xs, ys, big = [x0], [], {}
    h0 = _mod(x0, scale[0], shift[0], start, "mod0")
    big["a_w_in"] = fetch("a_w_in", h0)
    uvpre = mm(h0, big["a_w_in"], mode="nn", name="a_in", outs=[F32], tn=512, tk=1024,
               epi=lambda r, bias: [r + bias], extras=[("row", small["a_b_in"])])
    gated = _spatial_fwd(uvpre, small["a_vn_g"], small["a_vn_b"], small["wc"], small["bias_full"], "a_spatial")
    big["a_w_out"] = fetch("a_w_out", gated)
    y0, x1, h1 = mm(gated, big["a_w_out"], mode="nn", name="a_out", tm=min(s, 1024), tn=d, tk=1024, **resid_ln_epilogue(0))
    ys.append(y0)
    xs.append(x1)
    relu2 = lambda r: [jnp.square(jnp.maximum(r, 0.0))]
    big["up0"] = fetch("up0", h1)
    r0 = mm(h1, big["up0"], mode="nn", name="up0", outs=[MXU_DTYPE], tn=1024, tk=1024, epi=relu2)
    big["down0"] = fetch("down0", r0)
    ys.append(mm(r0, big["down0"], mode="nn", name="down0", outs=[F32], tm=min(s, 1024), tn=1024, tk=2048))
    dils = [dil for _, dil in B_PATTERNS]
    x2, h2, *h2_streams = _resid_ln(xs[1], ys[1], gate[1], lg[1], lb[1], (scale[2], shift[2]), "ln1", [dil for dil in dils if dil > 1])
    h2_streams = [h2] + [a.reshape(s, d) for a in h2_streams]
    xs.append(x2)
    hg, qkvs, o_g, l_g, l_streams = [], [], [], [], []
    big["b_w_qkv"] = fetch("b_w_qkv", h2)
    for g, (_, dil) in enumerate(B_PATTERNS):
        hp = h2_streams[g]
        qkv = mm(hp, big["b_w_qkv"], mode="nn", name=f"qkv{g}", outs=[MXU_DTYPE], tn=768, tk=1024, b_col0=g * 3 * d, n_out=3 * d)
        og, lgv = _attn_fwd(qkv, small["slopes"], dil, f"attn_fwd{g}")
        hg.append(hp)
        qkvs.append(qkv)
        o_g.append(og if dil == 1 else og.reshape(dil, s // dil, d))
        l_g.append(_from_streams(lgv, dil))
        l_streams.append(lgv)
    o_mix = _combine_fwd(o_g, l_g, "combine")
    big["b_w_out"] = fetch("b_w_out", o_mix)
    y2, x3, h3 = mm(o_mix, big["b_w_out"], mode="nn", name="b_out", tm=min(s, 1024), tn=d, tk=1024, **resid_ln_epilogue(2))
    ys.append(y2)
    xs.append(x3)
    big["up1"] = fetch("up1", h3)
    r1 = mm(h3, big["up1"], mode="nn", name="up1", outs=[MXU_DTYPE], tn=1024, tk=1024, epi=relu2)
    big["down1"] = fetch("down1", r1)
    ys.append(mm(r1, big["down1"], mode="nn", name="down1", outs=[F32], tm=min(s, 1024), tn=1024, tk=2048))

    gb, red_ln, red_mod = {}, [None] * 4, [None] * 4

    def mlp_bwd(i, h, r, dyy):
        gb[f"down{i}"] = mm_w(r, dyy, name=f"g_down{i}", outs=[MXU_DTYPE], tn=1024)
        da = mm(dyy, big[f"down{i}"], mode="nt", name=f"d_down{i}", outs=[MXU_DTYPE], tn=1024, tk=1024,
                after=emit(f"down{i}", gb[f"down{i}"]),
                epi=lambda acc, rv: [acc * (2.0 * jnp.sqrt(rv.astype(F32)))], extras=[("full", r)])
        gb[f"up{i}"] = mm_w(h, da, name=f"g_up{i}", outs=[MXU_DTYPE], tn=1024)
        return [mm(da, big[f"up{i}"], mode="nt", name=f"d_up{i}", outs=[F32], tn=1024, tk=1024, after=emit(f"up{i}", gb[f"up{i}"]))]

    def join(sub, dxr, dhs, after=None):
        res = _mod_ln_bwd(dxr, dhs, xs[sub], scale[sub], xs[sub - 1], ys[sub - 1], gate[sub - 1], lg[sub - 1],
                          f"mod_ln_bwd{sub}", after=after)
        red_mod[sub], red_ln[sub - 1] = res[2], res[3]
        return res[0], res[1]

    loss, dxr, dyy, red_ln[3] = _last_ln_loss_bwd(xs[3], ys[3], gate[3], lg[3], lb[3], target, "ln3_loss_bwd")
    dxr, dyy = join(3, dxr, mlp_bwd(1, h3, r1, dyy))
    gb["b_w_out"] = mm_w(o_mix, dyy, name="g_b_out", outs=[MXU_DTYPE], tn=1024, tk=1024)
    do = mm(dyy, big["b_w_out"], mode="nt", name="d_b_out", outs=[F32], tn=1024, tk=1024, after=emit("b_w_out", gb["b_w_out"]))
    parts = _combine_bwd(do, o_mix, l_g, dils, "combine_bwd")
    dhs, gq = [], None
    for g, (_, dil) in enumerate(B_PATTERNS):
        do_g, dd_g = parts[g][0].reshape(s, d), _to_streams(parts[g][1], dil)
        dqkv = _attn_bwd(qkvs[g], do_g, l_streams[g], dd_g, small["slopes"], dil, f"attn_bwd{g}")
        gq = mm_w(hg[g], dqkv, name=f"g_qkv{g}", outs=[MXU_DTYPE], tn=1024, out_col0=g * 3 * d, out_cols=len(B_PATTERNS) * 3 * d, into=gq)
        dh = mm(dqkv, big["b_w_qkv"], mode="nt", name=f"d_qkv{g}", outs=[F32], tn=1024, tk=768, b_col0=g * 3 * d)
        dhs.append(dh if dil == 1 else dh.reshape(dil, s // dil, d))
    gb["b_w_qkv"] = gq
    dxr, dyy = join(2, dxr, dhs, after=emit("b_w_qkv", gb["b_w_qkv"]))
    dxr, dyy = join(1, dxr, mlp_bwd(0, h1, r0, dyy))
    gb["a_w_out"] = mm_w(gated, dyy, name="g_a_out", outs=[MXU_DTYPE], tn=1024)
    dgated = mm(dyy, big["a_w_out"], mode="nt", name="d_a_out", outs=[F32], tn=1024, tk=1024, after=emit("a_w_out", gb["a_w_out"]))
    duv, dws, dbias, dbin, dvg, dvb = _spatial_bwd(uvpre, dgated, small["a_vn_g"], small["a_vn_b"], small["wc"],
                                                   small["wct"], small["bias_full"], "a_spatial_bwd")
    tril = jnp.tril(jnp.ones((CHUNK, CHUNK), bool))
    dws = jnp.where(tril, dws, 0.0).reshape(-1, LANES)
    gb["a_w_in"] = mm_w(h0, duv, name="g_a_in", outs=[MXU_DTYPE], tn=1024, after=emit("a_w_s", dws.astype(MXU_DTYPE)))
    dh = mm(duv, big["a_w_in"], mode="nt", name="d_a_in", outs=[F32], tn=1024, tk=512, after=emit("a_w_in", gb["a_w_in"]))
    dx, red_mod[0] = _mod_bwd(dxr, [dh], xs[0], scale[0], "mod_bwd0")
    dm = [jnp.concatenate([red_mod[i][0], red_mod[i][1], red_ln[i][2]]) for i in range(4)]
    dlg, dlb = [red_ln[i][0] for i in range(4)], [red_ln[i][1] for i in range(4)]

    gsmall = {
        "a_b_in": dbin.reshape(-1), "a_vn_g": dvg.reshape(-1), "a_vn_b": dvb.reshape(-1),
        "a_w_s": dws.reshape(-1),
        "a_b_s": dbias.reshape(CHUNK, A_GROUPS, d // A_GROUPS).sum(-1).T.reshape(-1),
    }
    return loss, dx, gb, jnp.stack(dm), jnp.stack(dlg), jnp.stack(dlb), gsmall


BIG = ("a_w_in", "a_w_out", "up0", "down0", "b_w_qkv", "b_w_out", "up1", "down1")
BIG_KIND = {"a_w_in": "col", "a_w_out": "row", "b_w_qkv": "col", "b_w_out": "row",
            "up0": "col", "up1": "col", "down0": "row", "down1": "row", "a_w_s": "all"}
HALVED = ("a_w_in", "down0", "b_w_qkv")
SCATTER_GROUPS = (("down1", "up1"), ("b_w_out", "b_w_qkv"), ("down0", "up0"), ("a_w_out", "a_w_in"), ("a_w_s",))
SMALL = ("a_b_in", "a_vn_g", "a_vn_b", "a_b_s")


def kernel(x, c, ada_w, ada_b, ln_g, ln_b, a_w_in, a_b_in, a_vn_g, a_vn_b, a_w_s, a_b_s, a_w_out, b_w_qkv, b_w_out, mlp_w_up, mlp_w_down, loss_target, m_ada_w, m_ada_b, m_ln_g, m_ln_b, m_a_w_in, m_a_b_in, m_a_vn_g, m_a_vn_b, m_a_w_s, m_a_b_s, m_a_w_out, m_b_w_qkv, m_b_w_out, m_mlp_w_up, m_mlp_w_down, v_ada_w, v_ada_b, v_ln_g, v_ln_b, v_a_w_in, v_a_b_in, v_a_vn_g, v_a_vn_b, v_a_w_s, v_a_b_s, v_a_w_out, v_b_w_qkv, v_b_w_out, v_mlp_w_up, v_mlp_w_down):
    s, d = x.shape[1], x.shape[2]
    xi, yi, ci = _me()
    q = 2 * xi + yi
    dev = 2 * q + ci
    nsub = 2 * DEPTH
    cs = ada_w.shape[-1]
    ls = ln_g.shape[-1]

    shards = {
        "a_w_in": a_w_in[0], "a_w_out": a_w_out[0], "b_w_qkv": b_w_qkv[0], "b_w_out": b_w_out[0],
        "up0": mlp_w_up[0], "up1": mlp_w_up[1], "down0": mlp_w_down[0], "down1": mlp_w_down[1],
    }
    cast = [shards[k].astype(MXU_DTYPE) for k in BIG]

    pack = jnp.concatenate([c.reshape(-1), ln_g.reshape(-1), ln_b.reshape(-1)]).reshape(-1, LANES)
    got = _all_gather_small(pack, "gather_small", after=cast).reshape(N_DEV, -1)
    c_all = got[:, :d]
    per_chip = got[0::2]
    ln_g_full = per_chip[:, d:d + nsub * ls].reshape(N_CHIPS, nsub, ls).transpose(1, 0, 2).reshape(nsub, d)
    ln_b_full = per_chip[:, d + nsub * ls:].reshape(N_CHIPS, nsub, ls).transpose(1, 0, 2).reshape(nsub, d)
    m_part = _ada_fwd(c_all, ada_w.reshape(nsub, d, cs), ada_b.reshape(nsub, 1, cs), "ada_fwd")
    m_all = _all_gather_small(m_part.reshape(-1, LANES), "gather_mod").reshape(N_DEV, nsub, N_DEV, cs)
    m_mine = lax.dynamic_index_in_dim(m_all[0::2], dev, axis=2, keepdims=False)
    mvec = m_mine.transpose(1, 0, 2).reshape(nsub, 3 * d)

    halved = {BIG.index(k) for k in HALVED}
    send_sems, recv_sems, shard_thru, lands, token = _gather_start(cast, halved, mvec, "gather_start")

    def fetch(k, after):
        w = BIG.index(k)
        shard, gw = _gather_wait(w, shard_thru[w], lands[w], send_sems, recv_sems, after, f"gather_wait_{k}", w in halved)
        if w in halved:
            gw = _assemble_halves(shard, gw, f"assemble_{k}")
        return gw if BIG_KIND[k] == "col" else gw.reshape(1, -1, gw.shape[-1])

    scattering, pending = {}, {}

    def emit(k, g):
        pending[k] = g
        group = next(gr for gr in SCATTER_GROUPS if k in gr)
        if k != group[-1]:
            return None
        scattering[group] = _scatter_start([pending[m] for m in group], [BIG_KIND[m] for m in group], f"scatter_start_{k}")
        return scattering[group][2][0]

    tril = jnp.tril(jnp.ones((CHUNK, CHUNK), bool))
    wc = jnp.where(tril, a_w_s[0], 0.0).astype(MXU_DTYPE)
    heads = jnp.arange(1, B_HEADS + 1, dtype=F32)
    small = {
        "a_b_in": a_b_in, "a_vn_g": a_vn_g, "a_vn_b": a_vn_b,
        "wc": wc, "wct": wc.transpose(0, 2, 1),
        "bias_full": jnp.repeat(a_b_s[0].T, d // A_GROUPS, axis=1),
        "slopes": jnp.exp2(-8.0 * heads / B_HEADS),
    }

    loss_part, grad_x, gb, dm, dlg, dlb, gsmall = _local_step(x[0], loss_target[0], mvec, ln_g_full, ln_b_full, small, fetch, emit, token)

    weights = dict(ada_w=ada_w, ada_b=ada_b, ln_g=ln_g, ln_b=ln_b, a_w_in=a_w_in, a_b_in=a_b_in, a_vn_g=a_vn_g, a_vn_b=a_vn_b,
                   a_w_s=a_w_s, a_b_s=a_b_s, a_w_out=a_w_out, b_w_qkv=b_w_qkv, b_w_out=b_w_out, mlp_w_up=mlp_w_up, mlp_w_down=mlp_w_down)
    ms = dict(ada_w=m_ada_w, ada_b=m_ada_b, ln_g=m_ln_g, ln_b=m_ln_b, a_w_in=m_a_w_in, a_b_in=m_a_b_in, a_vn_g=m_a_vn_g, a_vn_b=m_a_vn_b,
              a_w_s=m_a_w_s, a_b_s=m_a_b_s, a_w_out=m_a_w_out, b_w_qkv=m_b_w_qkv, b_w_out=m_b_w_out, mlp_w_up=m_mlp_w_up, mlp_w_down=m_mlp_w_down)
    vs = dict(ada_w=v_ada_w, ada_b=v_ada_b, ln_g=v_ln_g, ln_b=v_ln_b, a_w_in=v_a_w_in, a_b_in=v_a_b_in, a_vn_g=v_a_vn_g, a_vn_b=v_a_vn_b,
              a_w_s=v_a_w_s, a_b_s=v_a_b_s, a_w_out=v_a_w_out, b_w_qkv=v_b_w_qkv, b_w_out=v_b_w_out, mlp_w_up=v_mlp_w_up, mlp_w_down=v_mlp_w_down)
    grads, updates = {}, {}

    def update(k):
        updates[k] = _adamw(weights[k], grads[k], ms[k], vs[k], f"adamw_{k}")
        return updates[k][0]

    gfull = {}

    def big_group(group, after):
        bufs = []
        for pair in (group[:2], group[2:]):
            bufs += _scatter_wait(*scattering[pair], [BIG_KIND[m] for m in pair], after, f"scatter_wait_{pair[-1]}")
        halves = [_sum_slots(b, f"sum_{k}") for k, b in zip(group, bufs)]
        fulls = _swap_halves(halves, f"swap_halves_{group[0]}")
        gfull.update({k: f.reshape(-1, f.shape[-1]) for k, f in zip(group, fulls)})

    big_group(SCATTER_GROUPS[0] + SCATTER_GROUPS[1], grad_x)
    grads["b_w_qkv"], grads["b_w_out"] = gfull["b_w_qkv"][None], gfull["b_w_out"][None]
    update("b_w_out")
    done = update("b_w_qkv")

    pack_b = jnp.concatenate([dm.reshape(-1), dlg.reshape(-1), dlb.reshape(-1)] + [gsmall[k] for k in SMALL] + [loss_part.reshape(1)])
    n_small = pack_b.shape[0]
    pack_b = jnp.pad(pack_b, (0, -n_small % (256 * LANES)))
    got_b = _all_gather_small(pack_b.reshape(-1, LANES), "gather_small_grads", after=[done]).reshape(N_DEV, -1, LANES)
    tot = _sum_slots(got_b, "sum_small").reshape(-1)
    o = 0
    dm_tot = tot[o:o + nsub * 3 * d].reshape(nsub, 3 * d); o += nsub * 3 * d
    dlg_tot = tot[o:o + nsub * d].reshape(nsub, d); o += nsub * d
    dlb_tot = tot[o:o + nsub * d].reshape(nsub, d); o += nsub * d
    g_small = {}
    for k, ref in zip(SMALL, (a_b_in, a_vn_g, a_vn_b, a_b_s)):
        g_small[k] = tot[o:o + ref.size].reshape(ref.shape); o += ref.size
    loss = tot[o]
    assert o + 1 == n_small
    aws = _scatter_wait(*scattering[("a_w_s",)], ["all"], tot, "scatter_wait_a_w_s")[0]
    g_small["a_w_s"] = _sum_slots(aws, "sum_a_w_s").reshape(a_w_s.shape)
    dm_all = got_b.reshape(N_DEV, -1)[:, :nsub * 3 * d].reshape(N_DEV, nsub, 3 * d)
    dm_cols = lax.dynamic_slice_in_dim(dm_all, q * cs, cs, axis=2).transpose(1, 0, 2)
    grads.update({
        "ada_w": _ada_bwd(c_all.T, dm_cols, "ada_bwd").reshape(ada_w.shape),
        "ada_b": lax.dynamic_slice_in_dim(dm_tot, q * cs, cs, axis=1).reshape(ada_b.shape),
        "ln_g": lax.dynamic_slice_in_dim(dlg_tot, q * ls, ls, axis=1).reshape(ln_g.shape),
        "ln_b": lax.dynamic_slice_in_dim(dlb_tot, q * ls, ls, axis=1).reshape(ln_b.shape),
        **g_small,
    })
    for k in ("ada_b", "ln_g", "ln_b", "a_w_s") + SMALL:
        update(k)
    done = update("ada_w")

    big_group(SCATTER_GROUPS[2] + SCATTER_GROUPS[3], done)
    grads.update({
        "a_w_in": gfull["a_w_in"][None], "a_w_out": gfull["a_w_out"][None],
        "mlp_w_up": jnp.stack([gfull["up0"], gfull["up1"]]), "mlp_w_down": jnp.stack([gfull["down0"], gfull["down1"]]),
    })
    for k in ("a_w_in", "a_w_out", "mlp_w_up", "mlp_w_down"):
        update(k)
    names = list(weights)
    return (loss, grad_x[None], *[grads[k] for k in names], *[updates[k][0] for k in names],
            *[updates[k][1] for k in names], *[updates[k][2] for k in names])
```

```python
import functools
import math

import jax
import jax.numpy as jnp
from jax import lax
from jax.experimental import pallas as pl
from jax.experimental.pallas import tpu as pltpu

F32 = jnp.float32
MXU_DTYPE = jnp.bfloat16

DEPTH = 2
CHUNK = 128
A_GROUPS = 16
B_HEADS = 16
HEAD_DIM = 64
B_PATTERNS = ((128, 1), (512, 4), (2048, 16))
SPAN = 128
ALPHA = (2 * DEPTH) ** 0.25
LN_EPS = 1e-5
NEG = -1e30
ATT_SCALE = HEAD_DIM ** -0.5
ADAM_LR, ADAM_B1, ADAM_B2, ADAM_EPS, ADAM_WD, ADAM_STEP = 0.001, 0.9, 0.999, 1e-08, 0.01, 10

N_CHIPS = 4
N_DEV = 8
LANES = 128
SUBLANES = 8
VMEM_LIMIT = 52 * 1024 * 1024
ROW_TILE = 512
MM_ROW_CHUNK = 256
MESH = pl.DeviceIdType.MESH


def _cparams(sem):
    return pltpu.CompilerParams(dimension_semantics=sem, vmem_limit_bytes=VMEM_LIMIT)


def _fold8(v):
    r, c = v.shape
    return jnp.sum(v.reshape(r // SUBLANES, SUBLANES, c), axis=0)


def _gelu(x):
    c = math.sqrt(2.0 / math.pi)
    return 0.5 * x * (1.0 + jnp.tanh(c * (x + 0.044715 * (x * x * x))))


def _gelu_and_grad(x):
    c = math.sqrt(2.0 / math.pi)
    t = jnp.tanh(c * (x + 0.044715 * (x * x * x)))
    return 0.5 * x * (1.0 + t), 0.5 * (1.0 + t) + 0.5 * x * (1.0 - t * t) * c * (1.0 + 3.0 * 0.044715 * x * x)


def _dot(a, b, dims):
    return lax.dot_general(a.astype(MXU_DTYPE), b.astype(MXU_DTYPE), (dims, ((), ())), preferred_element_type=F32)


def _dot_nn(a, b):
    return _dot(a, b, ((1,), (0,)))


def _dot_nt(a, b):
    return _dot(a, b, ((1,), (1,)))


def _dot_tn(a, b):
    return _dot(a, b, ((0,), (0,)))


def _mm(a, b, *, mode, name, outs, tm, tn, tk, epi=None, extras=(), b_col0=0, n_out=None, after=None,
        out_col0=0, out_cols=None, into=None):
    if mode == "nn":
        m, kdim = a.shape
        p, kb, ns = b.shape
        assert kb == kdim and ns % tn == 0 and b_col0 % tn == 0
        n = n_out if n_out is not None else p * ns
        npt, j0 = ns // tn, b_col0 // tn
        a_spec = pl.BlockSpec((tm, tk), lambda i, j, k: (i, k))
        b_spec = pl.BlockSpec((None, tk, tn), lambda i, j, k: ((j + j0) // npt, k, (j + j0) % npt))
        dot = _dot_nn
    elif mode == "nt":
        m, kdim = a.shape
        p, n, ns = b.shape
        assert ns % tk == 0 and b_col0 % tk == 0
        npt, j0 = ns // tk, b_col0 // tk
        a_spec = pl.BlockSpec((tm, tk), lambda i, j, k: (i, k))
        b_spec = pl.BlockSpec((None, tn, tk), lambda i, j, k: ((k + j0) // npt, j, (k + j0) % npt))
        dot = _dot_nt
    else:
        kdim, m = a.shape
        kb, n = b.shape
        assert kb == kdim
        a_spec = pl.BlockSpec((tk, tm), lambda i, j, k: (k, i))
        b_spec = pl.BlockSpec((tk, tn), lambda i, j, k: (k, j))
        dot = _dot_tn
    assert m % tm == 0 and n % tn == 0 and kdim % tk == 0, (name, m, n, kdim, tm, tn, tk)
    nk = kdim // tk
    ex_specs, ex_arrays = [], []
    for kind, arr in extras:
        if kind == "row":
            ex_specs.append(pl.BlockSpec((1, tn), lambda i, j, k: (0, j)))
        else:
            ex_specs.append(pl.BlockSpec((tm, tn), lambda i, j, k: (i, j)))
        ex_arrays.append(arr)
    n_ex, n_o = len(ex_arrays), len(outs)
    deps = [d for d in (after, into) if d is not None]
    n_dep = len(deps)
    j_out = out_col0 // tn
    assert out_col0 % tn == 0 and (into is None or len(outs) == 1)

    def body(a_ref, b_ref, *rest):
        ex_refs, o_refs = rest[:n_ex], rest[n_ex + n_dep:n_ex + n_dep + n_o]
        k = pl.program_id(2)

        chunks = [slice(r0, r0 + min(tm, MM_ROW_CHUNK)) for r0 in range(0, tm, min(tm, MM_ROW_CHUNK))]

        def part(rows):
            return dot(a_ref[:, rows] if mode == "tn" else a_ref[rows, :], b_ref[...])

        def finish(r, rows):
            exs = [e[...] if kind == "row" else e[rows, :] for (kind, _), e in zip(extras, ex_refs)]
            vals = epi(r, *exs) if epi is not None else [r]
            for o, v in zip(o_refs, vals):
                o[rows, :] = v.astype(o.dtype)

        if nk == 1:
            for rows in chunks:
                finish(part(rows), rows)
            return
        acc = rest[n_ex + n_dep + n_o]

        @pl.when(k == 0)
        def _():
            for rows in chunks:
                acc[rows, :] = part(rows)

        @pl.when((k > 0) & (k < nk - 1))
        def _():
            for rows in chunks:
                acc[rows, :] += part(rows)

        @pl.when(k == nk - 1)
        def _():
            for rows in chunks:
                finish(acc[rows, :] + part(rows), rows)

    res = pl.pallas_call(
        body,
        grid=(m // tm, n // tn, nk),
        in_specs=[a_spec, b_spec] + ex_specs + [pl.BlockSpec(memory_space=pl.ANY)] * n_dep,
        out_specs=[pl.BlockSpec((tm, tn), lambda i, j, k: (i, j + j_out)) for _ in outs],
        out_shape=[jax.ShapeDtypeStruct((m, out_cols or n), dt) for dt in outs],
        input_output_aliases={} if into is None else {2 + n_ex + n_dep - 1: 0},
        scratch_shapes=[pltpu.VMEM((tm, tn), F32)] if nk > 1 else [],
        name=name,
        compiler_params=_cparams(("parallel", "parallel", "arbitrary")),
    )(a, b, *ex_arrays, *deps)
    return res if len(outs) > 1 else res[0]


def _rows(body, n_rows, tr, ins, outs, name, scratch=()):
    def spec(kind, shape):
        if kind == "blk":
            return pl.BlockSpec((tr,) + tuple(shape[1:]), lambda i: (i,) + (0,) * (len(shape) - 1))
        if kind == "dep":
            return pl.BlockSpec(memory_space=pl.ANY)
        if kind == "str":
            return pl.BlockSpec((shape[0], tr // shape[0], shape[2]), lambda i: (0, i, 0))
        return pl.BlockSpec(tuple(shape), lambda i: (0,) * len(shape))

    return pl.pallas_call(
        body,
        grid=(n_rows // tr,),
        in_specs=[spec(k, a.shape) for k, a in ins],
        out_specs=[spec(k, s) for k, s, _ in outs],
        out_shape=[jax.ShapeDtypeStruct(tuple(s), d) for _, s, d in outs],
        scratch_shapes=list(scratch),
        name=name,
        compiler_params=_cparams(("arbitrary",)),
    )(*[a for _, a in ins])


def _ln_stats(z):
    mu = jnp.mean(z, axis=-1, keepdims=True)
    zc = z - mu
    var = jnp.mean(zc * zc, axis=-1, keepdims=True)
    rstd = lax.rsqrt(var + LN_EPS)
    return zc * rstd, rstd


def _stream_scratch(c):
    return pltpu.VMEM((c // LANES, ROW_TILE, LANES), F32)


def _streams_in(ref3, scr):
    dil, n, c = ref3.shape
    for r in range(dil):
        for j in range(c // LANES):
            scr.at[j][pl.ds(r, n, stride=dil), :] = ref3[r, :, j * LANES:(j + 1) * LANES].astype(F32)
    return jnp.concatenate([scr[j] for j in range(c // LANES)], axis=1)


def _streams_out(val, ref3, scr):
    dil, n, c = ref3.shape
    for j in range(c // LANES):
        scr[j] = val[:, j * LANES:(j + 1) * LANES].astype(F32)
    for r in range(dil):
        for j in range(c // LANES):
            ref3[r, :, j * LANES:(j + 1) * LANES] = scr.at[j][pl.ds(r, n, stride=dil), :].astype(ref3.dtype)


def _mod(x, scale, shift, after, name):
    s, d = x.shape

    def body(x_ref, sc_ref, sh_ref, dep_ref, h_ref):
        h_ref[...] = (x_ref[...] * (1.0 + sc_ref[...]) + sh_ref[...]).astype(h_ref.dtype)

    return _rows(body, s, ROW_TILE, [("blk", x), ("all", scale), ("all", shift), ("dep", after)], [("blk", (s, d), MXU_DTYPE)], name)[0]


def _resid_ln(x, y, gate, g, b, nxt, name, dils=()):
    s, d = x.shape

    def body(x_ref, y_ref, gate_ref, g_ref, b_ref, sc_ref, sh_ref, xn_ref, h_ref, *rest):
        z = ALPHA * x_ref[...] + gate_ref[...] * y_ref[...]
        xhat, _ = _ln_stats(z)
        xn = xhat * g_ref[...] + b_ref[...]
        xn_ref[...] = xn
        h = xn * (1.0 + sc_ref[...]) + sh_ref[...]
        h_ref[...] = h.astype(h_ref.dtype)
        for hs_ref in rest[:len(dils)]:
            _streams_out(h, hs_ref, rest[-1])

    return _rows(body, s, ROW_TILE,
                 [("blk", x), ("blk", y), ("all", gate), ("all", g), ("all", b), ("all", nxt[0]), ("all", nxt[1])],
                 [("blk", (s, d), F32), ("blk", (s, d), MXU_DTYPE)] + [("str", (dil, s // dil, d), MXU_DTYPE) for dil in dils], name,
                 scratch=[_stream_scratch(d)] if dils else [])


def _mod_bwd(dxr, dhs, x, scale, name, after=None):
    s, d = x.shape
    n_dh = len(dhs)
    n_dep = 0 if after is None else 1

    def body(dxr_ref, *rest):
        dh_refs = rest[:n_dh]
        x_ref, sc_ref, dx_ref, red_ref, a_sh, a_sc = rest[n_dh:n_dh + 2] + rest[n_dh + 2 + n_dep:]
        i = pl.program_id(0)

        @pl.when(i == 0)
        def _():
            a_sh[...] = jnp.zeros_like(a_sh)
            a_sc[...] = jnp.zeros_like(a_sc)

        dh = dh_refs[0][...]
        for r in dh_refs[1:]:
            dh = dh + r[...]
        dx_ref[...] = dxr_ref[...] + dh * (1.0 + sc_ref[...])
        a_sh[...] += _fold8(dh)
        a_sc[...] += _fold8(dh * x_ref[...])

        @pl.when(i == pl.num_programs(0) - 1)
        def _():
            red_ref[...] = jnp.zeros_like(red_ref)
            red_ref[0:1, :] = jnp.sum(a_sh[...], axis=0, keepdims=True)
            red_ref[1:2, :] = jnp.sum(a_sc[...], axis=0, keepdims=True)

    return _rows(body, s, ROW_TILE, [("blk", dxr)] + [("blk", h) for h in dhs] + [("blk", x), ("all", scale)] + [("dep", after)] * n_dep,
                 [("blk", (s, d), F32), ("all", (SUBLANES, d), F32)], name,
                 scratch=[pltpu.VMEM((SUBLANES, d), F32)] * 2)


def _last_ln_loss_bwd(x, y, gate, g, b, target, name):
    s, d = x.shape

    def body(x_ref, y_ref, gate_ref, g_ref, b_ref, t_ref, l_ref, dxr_ref, dyy_ref, red_ref, a_l, a_g, a_b, a_gate):
        i = pl.program_id(0)

        @pl.when(i == 0)
        def _():
            for a in (a_l, a_g, a_b, a_gate):
                a[...] = jnp.zeros_like(a)

        yv = y_ref[...]
        z = ALPHA * x_ref[...] + gate_ref[...] * yv
        xhat, rstd = _ln_stats(z)
        e = xhat * g_ref[...] + b_ref[...] - t_ref[...]
        a_l[...] += _fold8(e * e)
        dxo_v = e * (1.0 / d)
        dxh = dxo_v * g_ref[...]
        dz = rstd * (dxh - jnp.mean(dxh, axis=-1, keepdims=True) - xhat * jnp.mean(dxh * xhat, axis=-1, keepdims=True))
        dxr_ref[...] = ALPHA * dz
        dyy_ref[...] = (gate_ref[...] * dz).astype(dyy_ref.dtype)
        a_g[...] += _fold8(dxo_v * xhat)
        a_b[...] += _fold8(dxo_v)
        a_gate[...] += _fold8(dz * yv)

        @pl.when(i == pl.num_programs(0) - 1)
        def _():
            l_ref[...] = jnp.full(l_ref.shape, 0.5 / d, F32) * jnp.sum(a_l[...])
            red_ref[...] = jnp.zeros_like(red_ref)
            red_ref[0:1, :] = jnp.sum(a_g[...], axis=0, keepdims=True)
            red_ref[1:2, :] = jnp.sum(a_b[...], axis=0, keepdims=True)
            red_ref[2:3, :] = jnp.sum(a_gate[...], axis=0, keepdims=True)

    l, dxr, dyy, red = _rows(
        body, s, ROW_TILE, [("blk", x), ("blk", y), ("all", gate), ("all", g), ("all", b), ("blk", target)],
        [("all", (SUBLANES, LANES), F32), ("blk", (s, d), F32), ("blk", (s, d), MXU_DTYPE), ("all", (SUBLANES, d), F32)], name,
        scratch=[pltpu.VMEM((SUBLANES, d), F32)] * 4)
    return l[0, 0], dxr, dyy, red


def _mod_ln_bwd(dxr, dhs, x, scale, x_in, y, gate, g, name, after=None):
    s, d = x.shape
    n_dh = len(dhs)
    n_dep = 0 if after is None else 1

    def body(dxr_ref, *rest):
        dh_refs = rest[:n_dh]
        x_ref, sc_ref, xin_ref, y_ref, gate_ref, g_ref = rest[n_dh:n_dh + 6]
        dxr_out, dyy_ref, red_mod, red_ln, a_sh, a_sc, a_g, a_b, a_gate = rest[n_dh + 6 + n_dep:n_dh + 15 + n_dep]
        i = pl.program_id(0)

        @pl.when(i == 0)
        def _():
            for a in (a_sh, a_sc, a_g, a_b, a_gate):
                a[...] = jnp.zeros_like(a)

        dh = dh_refs[0][...]
        for r in dh_refs[1:]:
            dh = dh + (r[...] if len(r.shape) == 2 else _streams_in(r, rest[-1]))
        xv = x_ref[...]
        dxo_v = dxr_ref[...] + dh * (1.0 + sc_ref[...])
        a_sh[...] += _fold8(dh)
        a_sc[...] += _fold8(dh * xv)
        yv = y_ref[...]
        z = ALPHA * xin_ref[...] + gate_ref[...] * yv
        xhat, rstd = _ln_stats(z)
        dxh = dxo_v * g_ref[...]
        dz = rstd * (dxh - jnp.mean(dxh, axis=-1, keepdims=True) - xhat * jnp.mean(dxh * xhat, axis=-1, keepdims=True))
        dxr_out[...] = ALPHA * dz
        dyy_ref[...] = (gate_ref[...] * dz).astype(dyy_ref.dtype)
        a_g[...] += _fold8(dxo_v * xhat)
        a_b[...] += _fold8(dxo_v)
        a_gate[...] += _fold8(dz * yv)

        @pl.when(i == pl.num_programs(0) - 1)
        def _():
            red_mod[...] = jnp.zeros_like(red_mod)
            red_mod[0:1, :] = jnp.sum(a_sh[...], axis=0, keepdims=True)
            red_mod[1:2, :] = jnp.sum(a_sc[...], axis=0, keepdims=True)
            red_ln[...] = jnp.zeros_like(red_ln)
            red_ln[0:1, :] = jnp.sum(a_g[...], axis=0, keepdims=True)
            red_ln[1:2, :] = jnp.sum(a_b[...], axis=0, keepdims=True)
            red_ln[2:3, :] = jnp.sum(a_gate[...], axis=0, keepdims=True)

    ins = ([("blk", dxr)] + [("blk" if h.ndim == 2 else "str", h) for h in dhs]
           + [("blk", x), ("all", scale), ("blk", x_in), ("blk", y), ("all", gate), ("all", g)] + [("dep", after)] * n_dep)
    return _rows(body, s, ROW_TILE, ins,
                 [("blk", (s, d), F32), ("blk", (s, d), MXU_DTYPE), ("all", (SUBLANES, d), F32), ("all", (SUBLANES, d), F32)], name,
                 scratch=[pltpu.VMEM((SUBLANES, d), F32)] * 5 + [_stream_scratch(d)] * any(h.ndim == 3 for h in dhs))


def _left_half(shape):
    return lax.broadcasted_iota(jnp.int32, shape, 1) < (LANES // 2)


CHUNKS_PER_STEP = 2


def _chunks_of_step():
    return [slice(i * CHUNK, (i + 1) * CHUNK) for i in range(CHUNKS_PER_STEP)]


def _spatial_z(vn, wc_ref, bias_ref, j):
    vb = vn[:, j * LANES:(j + 1) * LANES]
    z0 = _dot_nn(wc_ref[2 * j], vb)
    z1 = _dot_nn(wc_ref[2 * j + 1], vb)
    return jnp.where(_left_half(z0.shape), z0, z1) + bias_ref[:, j * LANES:(j + 1) * LANES]


def _spatial_fwd(uvpre, vn_g, vn_b, wc, bias_full, name):
    s, d2 = uvpre.shape
    d = d2 // 2

    def body(uv_ref, g_ref, b_ref, wc_ref, bias_ref, out_ref):
        for rows in _chunks_of_step():
            u = _gelu(uv_ref[rows, :d])
            v = _gelu(uv_ref[rows, d:])
            vh, _ = _ln_stats(v)
            vn = vh * g_ref[...] + b_ref[...]
            for j in range(d // LANES):
                z = _spatial_z(vn, wc_ref, bias_ref, j)
                out_ref[rows, j * LANES:(j + 1) * LANES] = (u[:, j * LANES:(j + 1) * LANES] * z).astype(out_ref.dtype)

    return _rows(body, s, CHUNKS_PER_STEP * CHUNK, [("blk", uvpre), ("all", vn_g), ("all", vn_b), ("all", wc), ("all", bias_full)],
                 [("blk", (s, d), MXU_DTYPE)], name)[0]


def _spatial_bwd(uvpre, dgated, vn_g, vn_b, wc, wct, bias_full, name):
    s, d2 = uvpre.shape
    d = d2 // 2

    def body(uv_ref, dg_ref, g_ref, b_ref, wc_ref, wct_ref, bias_ref,
             duv_ref, dws_ref, dbias_ref, dbin_ref, dvg_ref, dvb_ref, dvn_buf, a_bin, a_vg, a_vb):
        i = pl.program_id(0)

        @pl.when(i == 0)
        def _():
            dws_ref[...] = jnp.zeros_like(dws_ref)
            dbias_ref[...] = jnp.zeros_like(dbias_ref)
            a_bin[...] = jnp.zeros_like(a_bin)
            a_vg[...] = jnp.zeros_like(a_vg)
            a_vb[...] = jnp.zeros_like(a_vb)

        for rows in _chunks_of_step():
            u, u_grad = _gelu_and_grad(uv_ref[rows, :d])
            v, v_grad = _gelu_and_grad(uv_ref[rows, d:])
            vh, rstd = _ln_stats(v)
            vn = vh * g_ref[...] + b_ref[...]
            dg = dg_ref[rows, :]
            dzz = dg * u
            dbias_ref[...] += dzz
            for j in range(d // LANES):
                cols = slice(j * LANES, (j + 1) * LANES)
                z = _spatial_z(vn, wc_ref, bias_ref, j)
                dup = dg[:, cols] * z * u_grad[:, cols]
                duv_ref[rows, cols] = dup.astype(duv_ref.dtype)
                a_bin[:, cols] += _fold8(dup)
                dzb = dzz[:, cols]
                left = _left_half(dzb.shape)
                dvn_buf[:, cols] = jnp.where(left, _dot_nn(wct_ref[2 * j], dzb), _dot_nn(wct_ref[2 * j + 1], dzb))
                vb = vn[:, cols]
                dws_ref[2 * j] += _dot_nt(jnp.where(left, dzb, 0.0), vb)
                dws_ref[2 * j + 1] += _dot_nt(jnp.where(left, 0.0, dzb), vb)
            dvn = dvn_buf[...]
            a_vg[...] += _fold8(dvn * vh)
            a_vb[...] += _fold8(dvn)
            dvh = dvn * g_ref[...]
            dv = rstd * (dvh - jnp.mean(dvh, axis=-1, keepdims=True) - vh * jnp.mean(dvh * vh, axis=-1, keepdims=True))
            dvp = dv * v_grad
            duv_ref[rows, d:] = dvp.astype(duv_ref.dtype)
            a_bin[:, d:] += _fold8(dvp)

        @pl.when(i == pl.num_programs(0) - 1)
        def _():
            dbin_ref[...] = jnp.sum(a_bin[...], axis=0, keepdims=True)
            dvg_ref[...] = jnp.sum(a_vg[...], axis=0, keepdims=True)
            dvb_ref[...] = jnp.sum(a_vb[...], axis=0, keepdims=True)

    return _rows(body, s, CHUNKS_PER_STEP * CHUNK,
                 [("blk", uvpre), ("blk", dgated), ("all", vn_g), ("all", vn_b), ("all", wc), ("all", wct), ("all", bias_full)],
                 [("blk", (s, d2), MXU_DTYPE), ("all", (A_GROUPS, CHUNK, CHUNK), F32), ("all", (CHUNK, d), F32),
                  ("all", (1, d2), F32), ("all", (1, d), F32), ("all", (1, d), F32)], name,
                 scratch=[pltpu.VMEM((CHUNK, d), F32), pltpu.VMEM((SUBLANES, d2), F32),
                          pltpu.VMEM((SUBLANES, d), F32), pltpu.VMEM((SUBLANES, d), F32)])


def _head_mask(v, h):
    lane = lax.broadcasted_iota(jnp.int32, v.shape, 1)
    return jnp.where((lane >= h * HEAD_DIM) & (lane < (h + 1) * HEAD_DIM), v, jnp.zeros_like(v))


def _att_bias(slopes, dil):
    qi = lax.broadcasted_iota(jnp.int32, (SPAN, SPAN), 0)
    ki = lax.broadcasted_iota(jnp.int32, (SPAN, SPAN), 1)
    sl = slopes[:, None, None]
    cur = jnp.where(ki <= qi, -sl * (float(dil) * (qi - ki).astype(F32)), NEG)
    prev = jnp.where(ki >= qi, -sl * (float(dil) * (SPAN + qi - ki).astype(F32)), NEG)
    absent = jnp.full_like(prev, NEG)
    pairs = slopes.shape[0] // 2

    def fwd(pv):
        return jnp.concatenate([cur, pv], axis=2).reshape(pairs, 2 * SPAN, 2 * SPAN)

    def bwd(pv):
        return jnp.concatenate([cur.reshape(pairs, 2 * SPAN, SPAN), pv.reshape(pairs, 2 * SPAN, SPAN)], axis=1)

    return jnp.stack([fwd(absent), fwd(prev)]), jnp.stack([bwd(absent), bwd(prev)])


ATT_GROUP = 4


def _att_group(s, dil):
    nb = s // (dil * SPAN)
    grp = min(ATT_GROUP, nb)
    assert nb % grp == 0
    return nb, grp


def _att_specs(s, d, dil, kinds):
    nb, grp = _att_group(s, dil)

    def spec(part, which):
        if which == "group":
            return pl.BlockSpec((grp * SPAN, d), lambda b: (b, part))
        if which == "prev":
            return pl.BlockSpec((SPAN, d), lambda b: (jnp.where((grp * b) % nb == 0, grp * b, grp * b - 1), part))
        return pl.BlockSpec((SPAN, d), lambda b: (jnp.where((grp * b + grp - 1) % nb == nb - 1, grp * b + grp - 1, grp * b + grp), part))

    return [spec(part, which) for part, which in kinds]


def _head_col(v, head):
    return v[:, head:head + 1]


def _expand_heads(w, j):
    shape = (w.shape[0], LANES)
    return jnp.where(_left_half(shape), jnp.broadcast_to(_head_col(w, 2 * j), shape), jnp.broadcast_to(_head_col(w, 2 * j + 1), shape))


def _attn_fwd(qkv, slopes, dil, name):
    s, d3 = qkv.shape
    d = d3 // 3
    nb, grp = _att_group(s, dil)
    table, _ = _att_bias(slopes, dil)

    def body(q_ref, k_ref, kp_ref, v_ref, vp_ref, tb_ref, o_ref, l_ref):
        b = pl.program_id(0)
        left = _left_half((SPAN, LANES))
        lane = lax.broadcasted_iota(jnp.int32, (SPAN, LANES), 1)
        for sub in range(grp):
            rows, before = slice(sub * SPAN, (sub + 1) * SPAN), slice((sub - 1) * SPAN, sub * SPAN)
            variant = jnp.where((grp * b) % nb == 0, 0, 1) if sub == 0 else 1
            lses = jnp.zeros((SPAN, LANES), F32)
            for hp in range(d // LANES):
                cols = slice(hp * LANES, (hp + 1) * LANES)
                q = q_ref[rows, cols]
                q2 = jnp.concatenate([_head_mask(q, 0), _head_mask(q, 1)], axis=0) * ATT_SCALE
                k2 = jnp.concatenate([k_ref[rows, cols], kp_ref[:, cols] if sub == 0 else k_ref[before, cols]], axis=0)
                v2 = jnp.concatenate([v_ref[rows, cols], vp_ref[:, cols] if sub == 0 else v_ref[before, cols]], axis=0)
                sc = _dot_nt(q2, k2) + tb_ref[variant, hp]
                m = jnp.max(sc, axis=-1, keepdims=True)
                p = jnp.exp(sc - m)
                l = jnp.sum(p, axis=-1, keepdims=True)
                r = _dot_nn(p, v2) * (1.0 / l)
                lse = m + jnp.log(l)
                o_ref[rows, cols] = jnp.where(left, r[:SPAN], r[SPAN:])
                lses = jnp.where(lane == 2 * hp, lse[:SPAN], jnp.where(lane == 2 * hp + 1, lse[SPAN:], lses))
            l_ref[rows, :] = lses

    specs = _att_specs(s, d, dil, [(0, "group"), (1, "group"), (1, "prev"), (2, "group"), (2, "prev")])
    return pl.pallas_call(
        body,
        grid=(s // (grp * SPAN),),
        in_specs=specs + [pl.BlockSpec(table.shape, lambda b: (0, 0, 0, 0))],
        out_specs=[pl.BlockSpec((grp * SPAN, d), lambda b: (b, 0)), pl.BlockSpec((grp * SPAN, LANES), lambda b: (b, 0))],
        out_shape=[jax.ShapeDtypeStruct((s, d), F32), jax.ShapeDtypeStruct((s, LANES), F32)],
        name=name,
        compiler_params=_cparams(("parallel",)),
    )(qkv, qkv, qkv, qkv, qkv, table)


def _attn_bwd(qkv, do, lse, dd, slopes, dil, name):
    s, d3 = qkv.shape
    d = d3 // 3
    nb, grp = _att_group(s, dil)
    _, table = _att_bias(slopes, dil)

    def heads_stacked(cur, nxt):
        return jnp.concatenate([_head_mask(cur, 0), _head_mask(cur, 1), _head_mask(nxt, 0), _head_mask(nxt, 1)], axis=0)

    def cols_stacked(cur, nxt, hp):
        return jnp.concatenate([jnp.broadcast_to(_head_col(a, 2 * hp + h), (SPAN, LANES)) for a in (cur, nxt) for h in range(2)], axis=0)

    def body(k_ref, v_ref, q_ref, qn_ref, do_ref, don_ref, l_ref, ln_ref, dd_ref, ddn_ref, tb_ref, out_ref, carry):
        b = pl.program_id(0)

        @pl.when(b == 0)
        def _():
            carry[...] = jnp.zeros_like(carry)

        left = _left_half((SPAN, LANES))
        for sub in range(grp):
            rows, after = slice(sub * SPAN, (sub + 1) * SPAN), slice((sub + 1) * SPAN, (sub + 2) * SPAN)
            last = sub == grp - 1
            variant = jnp.where((grp * b + sub) % nb == nb - 1, 0, 1) if last else 1
            lse_c, dd_c = l_ref[rows, :], dd_ref[rows, :]
            lse_n, dd_n = (ln_ref[...], ddn_ref[...]) if last else (l_ref[after, :], dd_ref[after, :])
            for hp in range(d // LANES):
                cols = slice(hp * LANES, (hp + 1) * LANES)
                k, v = k_ref[rows, cols], v_ref[rows, cols]
                q4 = heads_stacked(q_ref[rows, cols], qn_ref[:, cols] if last else q_ref[after, cols])
                do4 = heads_stacked(do_ref[rows, cols], don_ref[:, cols] if last else do_ref[after, cols])
                sc = _dot_nt(q4 * ATT_SCALE, k) + tb_ref[variant, hp]
                p = jnp.exp(sc - cols_stacked(lse_c, lse_n, hp))
                ds = p * (_dot_nt(do4, v) - cols_stacked(dd_c, dd_n, hp))
                dq4 = _dot_nn(ds, k)
                dq_cur = jnp.where(left, dq4[:SPAN], dq4[SPAN:2 * SPAN]) + carry[:, cols]
                carry[:, cols] = jnp.where(left, dq4[2 * SPAN:3 * SPAN], dq4[3 * SPAN:])
                out_ref[rows, cols] = (dq_cur * ATT_SCALE).astype(out_ref.dtype)
                out_ref[rows, d + hp * LANES:d + (hp + 1) * LANES] = (_dot_tn(ds, q4) * ATT_SCALE).astype(out_ref.dtype)
                out_ref[rows, 2 * d + hp * LANES:2 * d + (hp + 1) * LANES] = _dot_tn(p, do4).astype(out_ref.dtype)

    qkv_specs = _att_specs(s, d, dil, [(1, "group"), (2, "group"), (0, "group"), (0, "next")])
    wide = _att_specs(s, d, dil, [(0, "group"), (0, "next")])
    heads = _att_specs(s, LANES, dil, [(0, "group"), (0, "next")])
    return pl.pallas_call(
        body,
        grid=(s // (grp * SPAN),),
        in_specs=qkv_specs + wide + heads + heads + [pl.BlockSpec(table.shape, lambda b: (0, 0, 0, 0))],
        out_specs=pl.BlockSpec((grp * SPAN, d3), lambda b: (b, 0)),
        out_shape=jax.ShapeDtypeStruct((s, d3), MXU_DTYPE),
        scratch_shapes=[pltpu.VMEM((SPAN, d), F32)],
        name=name,
        compiler_params=_cparams(("arbitrary",)),
    )(qkv, qkv, qkv, qkv, do, do, lse, lse, dd, dd, table)


def _mix_weights(l_refs):
    ls = [r[...] for r in l_refs]
    m = functools.reduce(jnp.maximum, ls)
    es = [jnp.exp(l - m) for l in ls]
    tot = functools.reduce(lambda a, c: a + c, es)
    return [e / tot for e in es]


def _combine_fwd(os_, ls_, name):
    s, d = ls_[0].shape[0], os_[0].shape[-1]
    n = len(os_)
    n_str = sum(o.ndim == 3 for o in os_)

    def body(*refs):
        o_refs, l_refs, out_ref, scrs = refs[:n], refs[n:2 * n], refs[2 * n], list(refs[2 * n + 1:])
        ws = _mix_weights(l_refs)
        os_v = [o if len(o.shape) == 2 else _streams_in(o, scrs.pop()) for o in o_refs]
        for j in range(d // LANES):
            cols = slice(j * LANES, (j + 1) * LANES)
            acc = _expand_heads(ws[0], j) * os_v[0][:, cols]
            for w, o in zip(ws[1:], os_v[1:]):
                acc = acc + _expand_heads(w, j) * o[:, cols]
            out_ref[:, cols] = acc

    return _rows(body, s, ROW_TILE, [("blk" if a.ndim == 2 else "str", a) for a in os_] + [("blk", a) for a in ls_],
                 [("blk", (s, d), F32)], name, scratch=[_stream_scratch(d)] * n_str)[0]


def _combine_bwd(do, o, ls_, dils, name):
    s, d = o.shape
    n = len(ls_)
    sel = (lax.broadcasted_iota(jnp.int32, (d, LANES), 0) // HEAD_DIM == lax.broadcasted_iota(jnp.int32, (d, LANES), 1)).astype(F32)

    def body(do_ref, o_ref, *rest):
        l_refs, sel_ref, outs = rest[:n], rest[n], rest[n + 1:n + 1 + 2 * n]
        ws = _mix_weights(l_refs)
        dov = do_ref[...]
        r = jnp.dot(dov * o_ref[...], sel_ref[...], precision=lax.Precision.HIGHEST, preferred_element_type=F32)
        for g in range(n):
            outs[2 * g + 1][...] = ws[g] * r
            parts = [_expand_heads(ws[g], j) * dov[:, j * LANES:(j + 1) * LANES] for j in range(d // LANES)]
            if dils[g] == 1:
                for j, part in enumerate(parts):
                    outs[2 * g][:, j * LANES:(j + 1) * LANES] = part.astype(outs[2 * g].dtype)
            else:
                _streams_out(jnp.concatenate(parts, axis=1), outs[2 * g], rest[-1])

    outs = []
    for dil in dils:
        outs += [("blk", (s, d), MXU_DTYPE) if dil == 1 else ("str", (dil, s // dil, d), MXU_DTYPE), ("blk", (s, LANES), F32)]
    res = _rows(body, s, ROW_TILE, [("blk", do), ("blk", o)] + [("blk", l) for l in ls_] + [("all", sel)], outs, name,
                scratch=[_stream_scratch(d)])
    return [(res[2 * g], res[2 * g + 1]) for g in range(n)]


def _ada_fwd(c_all, w, b, name):
    nsub, d, cs = w.shape

    def body(c_ref, w_ref, b_ref, o_ref):
        cv = c_ref[...]
        sc = cv * (1.0 / (1.0 + jnp.exp(-cv)))
        o_ref[...] = _dot_nn(sc, w_ref[...]) + b_ref[...]

    return pl.pallas_call(
        body,
        grid=(nsub,),
        in_specs=[pl.BlockSpec(c_all.shape, lambda i: (0, 0)), pl.BlockSpec((None, d, cs), lambda i: (i, 0, 0)),
                  pl.BlockSpec((None, 1, cs), lambda i: (i, 0, 0))],
        out_specs=pl.BlockSpec((None, N_DEV, cs), lambda i: (i, 0, 0)),
        out_shape=jax.ShapeDtypeStruct((nsub, N_DEV, cs), F32),
        name=name,
        compiler_params=_cparams(("parallel",)),
    )(c_all, w, b)


def _ada_bwd(c_all_t, dm, name):
    d, nb = c_all_t.shape
    nsub, _, cs = dm.shape

    def body(c_ref, dm_ref, o_ref):
        cv = c_ref[...]
        sc = cv * (1.0 / (1.0 + jnp.exp(-cv)))
        acc = sc[:, 0:1] * dm_ref[0:1, :]
        for bi in range(1, nb):
            acc = acc + sc[:, bi:bi + 1] * dm_ref[bi:bi + 1, :]
        o_ref[...] = acc

    return pl.pallas_call(
        body,
        grid=(nsub,),
        in_specs=[pl.BlockSpec(c_all_t.shape, lambda i: (0, 0)), pl.BlockSpec((None, nb, cs), lambda i: (i, 0, 0))],
        out_specs=pl.BlockSpec((None, d, cs), lambda i: (i, 0, 0)),
        out_shape=jax.ShapeDtypeStruct((nsub, d, cs), F32),
        name=name,
        compiler_params=_cparams(("parallel",)),
    )(c_all_t, dm)


def _row_tile(r, row_elems, block_elems=256 * 1024):
    t = 2 * SUBLANES
    if r % t:
        return r
    while t * 2 * row_elems <= block_elems and r % (t * 2) == 0:
        t *= 2
    return t


def _adamw(w, g, m, v, name):
    shape = w.shape
    c = shape[-1]
    r = w.size // c
    tr = _row_tile(r, c, 512 * 1024)
    w2, g2, m2, v2 = [a.reshape(r, c) for a in (w, g, m, v)]
    bc1 = 1.0 - ADAM_B1 ** ADAM_STEP
    bc2 = 1.0 - ADAM_B2 ** ADAM_STEP

    def body(w_ref, g_ref, m_ref, v_ref, d_ref, nm_ref, nv_ref):
        gv = g_ref[...]
        nm = ADAM_B1 * m_ref[...] + (1.0 - ADAM_B1) * gv
        nv = ADAM_B2 * v_ref[...] + (1.0 - ADAM_B2) * (gv * gv)
        d_ref[...] = -ADAM_LR * ((nm / bc1) / (jnp.sqrt(nv / bc2) + ADAM_EPS) + ADAM_WD * w_ref[...])
        nm_ref[...] = nm
        nv_ref[...] = nv

    res = _rows(body, r, tr, [("blk", a) for a in (w2, g2, m2, v2)], [("blk", (r, c), F32)] * 3, name)
    return [a.reshape(shape) for a in res]


def _sum_slots(buf, name):
    n, r, c = buf.shape
    tr = _row_tile(r, n * c, 2 * 1024 * 1024)

    def body(b_ref, o_ref):
        acc = b_ref[0].astype(F32)
        for k in range(1, n):
            acc = acc + b_ref[k].astype(F32)
        o_ref[...] = acc

    return pl.pallas_call(
        body,
        grid=(r // tr,),
        in_specs=[pl.BlockSpec((n, tr, c), lambda i: (0, i, 0))],
        out_specs=pl.BlockSpec((tr, c), lambda i: (i, 0)),
        out_shape=jax.ShapeDtypeStruct((r, c), F32),
        name=name,
        compiler_params=_cparams(("parallel",)),
    )(buf)


def _me():
    return lax.axis_index("x"), lax.axis_index("y"), lax.axis_index("c")


def _all_gather_small(blk, name, after=()):
    m_per, n = blk.shape

    def body(x_ref, *rest):
        out_ref, send_sems, recv_sems, local_sem = rest[len(after):]
        x, y, c = _me()
        me, sibling = (x, y, c), (x, y, 1 - c)
        chips = [(1 - x, y), (x, 1 - y), (1 - x, 1 - y)]

        def rows(px, py, pc):
            return out_ref.at[pl.ds((4 * px + 2 * py + pc) * m_per, m_per), :]

        def copy(k, block, to, src=None):
            return pltpu.make_async_remote_copy(
                src_ref=rows(*block) if src is None else src, dst_ref=rows(*block),
                send_sem=send_sems.at[k], recv_sem=recv_sems.at[k], device_id=to, device_id_type=MESH)

        mine = pltpu.make_async_copy(x_ref, rows(*me), local_sem)
        mine.start()
        first = [copy(0, me, sibling, src=x_ref)]
        first += [copy(1 + j, me, (*chip, c), src=x_ref) for j, chip in enumerate(chips)]
        for cp in first:
            cp.start()
        passed = [copy(4 + j, (*chip, c), sibling) for j, chip in enumerate(chips)]
        for j, chip in enumerate(chips):
            copy(1 + j, (*chip, c), me).wait_recv()
            passed[j].start()
        copy(0, sibling, me).wait_recv()
        for j, chip in enumerate(chips):
            copy(4 + j, (*chip, 1 - c), me).wait_recv()
        for cp in first + passed:
            cp.wait_send()
        mine.wait()

    return pl.pallas_call(
        body,
        out_shape=jax.ShapeDtypeStruct((N_DEV * m_per, n), blk.dtype),
        in_specs=[pl.BlockSpec(memory_space=pltpu.VMEM)] + [pl.BlockSpec(memory_space=pl.ANY)] * len(after),
        out_specs=pl.BlockSpec(memory_space=pltpu.VMEM),
        scratch_shapes=[pltpu.SemaphoreType.DMA((7,)), pltpu.SemaphoreType.DMA((7,)), pltpu.SemaphoreType.DMA],
        name=name,
        compiler_params=pltpu.CompilerParams(vmem_limit_bytes=VMEM_LIMIT),
    )(blk, *after)


_HBM = pl.BlockSpec(memory_space=pltpu.HBM)
_SEM = pl.BlockSpec(memory_space=pltpu.SEMAPHORE)
_EFFECT = pltpu.SideEffectType.DATAFLOW_SIDE_EFFECTING


def _other_chips(x, y):
    return [(1 - x, y), (x, 1 - y), (1 - x, 1 - y)]


def _gather_copy(w, j, src_ref, land_ref, send_sems, recv_sems, halved=False):
    x, y, c = _me()
    if halved:
        half = src_ref.shape[0] // 2
        src_ref = src_ref.at[pl.ds(c * half, half), :]
    return pltpu.make_async_remote_copy(
        src_ref=src_ref, dst_ref=land_ref.at[2 * x + y], send_sem=send_sems.at[3 * w + j], recv_sem=recv_sems.at[3 * w + j],
        device_id=(*_other_chips(x, y)[j], c), device_id_type=MESH)


def _gather_start(shards, halved, after, name):
    n = len(shards)
    lands = [lax.empty((N_CHIPS, s.shape[0] // 2 if w in halved else s.shape[0], s.shape[1]), s.dtype) for w, s in enumerate(shards)]

    def body(*refs):
        in_refs, land_refs = refs[:n], refs[n:2 * n]
        send_sems, recv_sems = refs[2 * n + 1], refs[2 * n + 2]
        token = refs[-1]
        for w in range(n):
            for j in range(3):
                _gather_copy(w, j, in_refs[w], land_refs[w], send_sems, recv_sems, w in halved).start()
        token[...] = jnp.zeros_like(token)

    res = pl.pallas_call(
        body,
        out_shape=(pltpu.SemaphoreType.DMA((3 * n,)), pltpu.SemaphoreType.DMA((3 * n,)),
                   *[pltpu.HBM(s.shape, s.dtype) for s in shards], *[pltpu.HBM(l.shape, l.dtype) for l in lands],
                   jax.ShapeDtypeStruct((SUBLANES, LANES), F32)),
        in_specs=[_HBM] * (2 * n) + [pl.BlockSpec(memory_space=pl.ANY)],
        out_specs=(_SEM, _SEM, *[_HBM] * (2 * n), pl.BlockSpec(memory_space=pltpu.VMEM)),
        input_output_aliases={i: 2 + i for i in range(2 * n)},
        name=name,
        compiler_params=pltpu.CompilerParams(has_side_effects=_EFFECT),
    )(*[pltpu.with_memory_space_constraint(a, pltpu.HBM) for a in list(shards) + lands], after)
    return res[0], res[1], res[2:2 + n], res[2 + n:2 + 2 * n], res[-1]


def _gather_wait(w, shard, land, send_sems, recv_sems, after, name, halved=False):
    def body(s_ref, land_ref, send_sems, recv_sems, after_ref, s_out, land_out, stage):
        x, y, _ = _me()
        if not halved:
            pltpu.sync_copy(s_ref, stage)
            pltpu.sync_copy(stage, land_out.at[2 * x + y])
        for j in range(3):
            cp = _gather_copy(w, j, s_ref, land_ref, send_sems, recv_sems, halved)
            cp.wait_send()
            cp.wait_recv()

    return pl.pallas_call(
        body,
        out_shape=(pltpu.HBM(shard.shape, shard.dtype), pltpu.HBM(land.shape, land.dtype)),
        in_specs=(_HBM, _HBM, _SEM, _SEM, pl.BlockSpec(memory_space=pl.ANY)),
        out_specs=(_HBM, _HBM),
        input_output_aliases={0: 0, 1: 1},
        scratch_shapes=[pltpu.VMEM((SUBLANES, LANES) if halved else shard.shape, shard.dtype)],
        name=name,
        compiler_params=pltpu.CompilerParams(has_side_effects=_EFFECT, vmem_limit_bytes=VMEM_LIMIT),
    )(shard, land, send_sems, recv_sems, after)


def _assemble_halves(shard, land, name):
    half = land.shape[1]

    def body(s_ref, land_ref, out_ref, send_sems, recv_sems, local_sems):
        x, y, c = _me()
        own = pltpu.make_async_copy(s_ref, out_ref.at[2 * x + y], local_sems.at[3])
        own.start()
        cps = []
        for j, (ox, oy) in enumerate(_other_chips(x, y)):
            qj = 2 * ox + oy
            mine = out_ref.at[qj, pl.ds(c * half, half), :]
            lc = pltpu.make_async_copy(land_ref.at[qj], mine, local_sems.at[j])
            lc.start()
            rc = pltpu.make_async_remote_copy(
                src_ref=land_ref.at[qj], dst_ref=mine, send_sem=send_sems.at[j], recv_sem=recv_sems.at[j],
                device_id=(x, y, 1 - c), device_id_type=MESH)
            rc.start()
            cps.append((lc, rc))
        for lc, rc in cps:
            rc.wait_recv()
        for lc, rc in cps:
            rc.wait_send()
            lc.wait()
        own.wait()

    vmem = pl.BlockSpec(memory_space=pltpu.VMEM)
    return pl.pallas_call(
        body,
        out_shape=jax.ShapeDtypeStruct((N_CHIPS,) + shard.shape, shard.dtype),
        in_specs=[vmem, vmem],
        out_specs=vmem,
        scratch_shapes=[pltpu.SemaphoreType.DMA((3,)), pltpu.SemaphoreType.DMA((3,)), pltpu.SemaphoreType.DMA((4,))],
        name=name,
        compiler_params=pltpu.CompilerParams(vmem_limit_bytes=VMEM_LIMIT),
    )(shard, land)


def _piece_shape(shape, kind):
    k, nn = shape
    if kind == "all":
        return (k, nn)
    return (k // 2, nn // N_CHIPS) if kind == "col" else (k // N_CHIPS // 2, nn)


def _piece_of(g_ref, kind, tq, tc):
    pr, pc = _piece_shape(g_ref.shape, kind)
    if kind == "all":
        return g_ref
    if kind == "col":
        return g_ref.at[pl.ds(tc * pr, pr), pl.ds(tq * pc, pc)]
    return g_ref.at[pl.ds((2 * tq + tc) * pr, pr), :]


def _scatter_copy(w, r, kind, g_ref, land_ref, send_sems, recv_sems):
    x, y, c = _me()
    tx, ty, tc = (x + ((r >> 2) & 1)) % 2, (y + ((r >> 1) & 1)) % 2, (c + (r & 1)) % 2
    return pltpu.make_async_remote_copy(
        src_ref=_piece_of(g_ref, kind, 2 * tx + ty, tc), dst_ref=land_ref.at[4 * x + 2 * y + c],
        send_sem=send_sems.at[N_DEV * w + r], recv_sem=recv_sems.at[N_DEV * w + r], device_id=(tx, ty, tc), device_id_type=MESH)


def _scatter_start(gs, kinds, name):
    n = len(gs)
    pieces = [_piece_shape(g.shape, kind) for g, kind in zip(gs, kinds)]
    lands = [lax.empty((N_DEV,) + p, g.dtype) for p, g in zip(pieces, gs)]

    def body(*refs):
        g_refs, land_refs, send_sems, recv_sems = refs[:n], refs[n:2 * n], refs[2 * n], refs[2 * n + 1]
        land_outs, stages = refs[3 * n + 2:4 * n + 2], refs[4 * n + 2:]
        x, y, c = _me()
        for w in range(n):
            for r in range(1, N_DEV):
                _scatter_copy(w, r, kinds[w], g_refs[w], land_refs[w], send_sems, recv_sems).start()
        for w in range(n):
            pltpu.sync_copy(_piece_of(g_refs[w], kinds[w], 2 * x + y, c), stages[w])
            pltpu.sync_copy(stages[w], land_outs[w].at[4 * x + 2 * y + c])

    arrays = list(gs) + lands
    res = pl.pallas_call(
        body,
        out_shape=(pltpu.SemaphoreType.DMA((N_DEV * n,)), pltpu.SemaphoreType.DMA((N_DEV * n,)),
                   *[pltpu.HBM(a.shape, a.dtype) for a in arrays]),
        in_specs=[_HBM] * (2 * n),
        out_specs=(_SEM, _SEM, *[_HBM] * (2 * n)),
        input_output_aliases={i: 2 + i for i in range(2 * n)},
        scratch_shapes=[pltpu.VMEM(p, g.dtype) for p, g in zip(pieces, gs)],
        name=name,
        compiler_params=pltpu.CompilerParams(has_side_effects=_EFFECT, vmem_limit_bytes=VMEM_LIMIT),
    )(*[pltpu.with_memory_space_constraint(a, pltpu.HBM) for a in arrays])
    return res[0], res[1], res[2:2 + n], res[2 + n:]


def _scatter_wait(send_sems, recv_sems, gs, lands, kinds, after, name):
    n = len(gs)

    def body(*refs):
        g_refs, land_refs, send_sems, recv_sems = refs[:n], refs[n:2 * n], refs[2 * n], refs[2 * n + 1]
        for w in range(n):
            for r in range(1, N_DEV):
                cp = _scatter_copy(w, r, kinds[w], g_refs[w], land_refs[w], send_sems, recv_sems)
                cp.wait_send()
                cp.wait_recv()

    arrays = list(gs) + list(lands)
    return pl.pallas_call(
        body,
        out_shape=tuple(pltpu.HBM(a.shape, a.dtype) for a in arrays),
        in_specs=(*[_HBM] * (2 * n), _SEM, _SEM, pl.BlockSpec(memory_space=pl.ANY)),
        out_specs=tuple([_HBM] * (2 * n)),
        input_output_aliases={i: i for i in range(2 * n)},
        name=name,
        compiler_params=pltpu.CompilerParams(has_side_effects=_EFFECT),
    )(*arrays, send_sems, recv_sems, after)[n:]


def _sum_swap(bufs, name):
    n = len(bufs)

    def body(*refs):
        in_refs, out_refs = refs[:n], refs[n:2 * n]
        send_sems, recv_sems = refs[2 * n:]
        x, y, c = _me()
        cps = []
        for w in range(n):
            slots, r, _ = bufs[w].shape
            mine = out_refs[w].at[c]
            for r0 in range(0, r, min(r, ROW_TILE)):
                rows = slice(r0, r0 + min(r, ROW_TILE))
                acc = in_refs[w][0, rows, :].astype(F32)
                for k in range(1, slots):
                    acc = acc + in_refs[w][k, rows, :].astype(F32)
                mine[rows, :] = acc
            rc = pltpu.make_async_remote_copy(
                src_ref=mine, dst_ref=mine, send_sem=send_sems.at[w], recv_sem=recv_sems.at[w],
                device_id=(x, y, 1 - c), device_id_type=MESH)
            rc.start()
            cps.append(rc)
        for rc in cps:
            rc.wait_recv()
        for rc in cps:
            rc.wait_send()

    vmem = pl.BlockSpec(memory_space=pltpu.VMEM)
    return pl.pallas_call(
        body,
        out_shape=[jax.ShapeDtypeStruct((2,) + b.shape[1:], F32) for b in bufs],
        in_specs=[vmem] * n,
        out_specs=[vmem] * n,
        scratch_shapes=[pltpu.SemaphoreType.DMA((n,)), pltpu.SemaphoreType.DMA((n,))],
        name=name,
        compiler_params=pltpu.CompilerParams(vmem_limit_bytes=VMEM_LIMIT),
    )(*bufs)


def _to_streams(a, dil):
    if dil == 1:
        return a
    s, c = a.shape
    return a.reshape(s // dil, dil, c).transpose(1, 0, 2).reshape(s, c)


def _from_streams(a, dil):
    if dil == 1:
        return a
    s, c = a.shape
    return a.reshape(dil, s // dil, c).transpose(1, 0, 2).reshape(s, c)


def _mm_tiles(s):
    return min(s, 2048)


def _local_step(x0, target, mvec, ln_g, ln_b, small, fetch, emit, start):
    s, d = x0.shape
    tm = _mm_tiles(s)
    row = lambda v: v.reshape(1, -1)
    shift = [row(mvec[i, :d]) for i in range(4)]
    scale = [row(mvec[i, d:2 * d]) for i in range(4)]
    gate = [row(1.0 + mvec[i, 2 * d:]) for i in range(4)]
    lg = [row(ln_g[i]) for i in range(4)]
    lb = [row(ln_b[i]) for i in range(4)]
    mm = functools.partial(_mm, tm=tm)
    mm_w = functools.partial(_mm, tm=1024, tk=min(s, 2048), mode="tn")

    def resid_ln_epilogue(sub):
        def epi(y, xv, gate_v, g_v, b_v, sc_v, sh_v):
            xhat, _ = _ln_stats(ALPHA * xv + gate_v * y)
            xn = xhat * g_v + b_v
            return [y, xn, xn * (1.0 + sc_v) + sh_v]

        rows = [gate[sub], lg[sub], lb[sub], scale[sub + 1], shift[sub + 1]]
        return dict(outs=[F32, F32, MXU_DTYPE], epi=epi, extras=[("full", xs[sub])] + [("row", r) for r in rows])

    xs, ys, big = [x0], [], {}
    h0 = _mod(x0, scale[0], shift[0], start, "mod0")
    big["a_w_in"] = fetch("a_w_in", h0)
    uvpre = mm(h0, big["a_w_in"], mode="nn", name="a_in", outs=[F32], tn=512, tk=1024,
               epi=lambda r, bias: [r + bias], extras=[("row", small["a_b_in"])])
    gated = _spatial_fwd(uvpre, small["a_vn_g"], small["a_vn_b"], small["wc"], small["bias_full"], "a_spatial")
    big["a_w_out"] = fetch("a_w_out", gated)
    y0, x1, h1 = mm(gated, big["a_w_out"], mode="nn", name="a_out", tm=min(s, 1024), tn=d, tk=1024, **resid_ln_epilogue(0))
    ys.append(y0)
    xs.append(x1)
    relu2 = lambda r: [jnp.square(jnp.maximum(r, 0.0))]
    big["up0"] = fetch("up0", h1)
    r0 = mm(h1, big["up0"], mode="nn", name="up0", outs=[MXU_DTYPE], tn=1024, tk=1024, epi=relu2)
    big["down0"] = fetch("down0", r0)
    ys.append(mm(r0, big["down0"], mode="nn", name="down0", outs=[F32], tm=min(s, 1024), tn=1024, tk=2048))
    dils = [dil for _, dil in B_PATTERNS]
    x2, h2, *h2_streams = _resid_ln(xs[1], ys[1], gate[1], lg[1], lb[1], (scale[2], shift[2]), "ln1", [dil for dil in dils if dil > 1])
    h2_streams = [h2] + [a.reshape(s, d) for a in h2_streams]
    xs.append(x2)
    hg, qkvs, o_g, l_g, l_streams = [], [], [], [], []
    big["b_w_qkv"] = fetch("b_w_qkv", h2)
    for g, (_, dil) in enumerate(B_PATTERNS):
        hp = h2_streams[g]
        qkv = mm(hp, big["b_w_qkv"], mode="nn", name=f"qkv{g}", outs=[MXU_DTYPE], tn=768, tk=1024, b_col0=g * 3 * d, n_out=3 * d)
        og, lgv = _attn_fwd(qkv, small["slopes"], dil, f"attn_fwd{g}")
        hg.append(hp)
        qkvs.append(qkv)
        o_g.append(og if dil == 1 else og.reshape(dil, s // dil, d))
        l_g.append(_from_streams(lgv, dil))
        l_streams.append(lgv)
    o_mix = _combine_fwd(o_g, l_g, "combine")
    big["b_w_out"] = fetch("b_w_out", o_mix)
    y2, x3, h3 = mm(o_mix, big["b_w_out"], mode="nn", name="b_out", tm=min(s, 1024), tn=d, tk=1024, **resid_ln_epilogue(2))
    ys.append(y2)
    xs.append(x3)
    big["up1"] = fetch("up1", h3)
    r1 = mm(h3, big["up1"], mode="nn", name="up1", outs=[MXU_DTYPE], tn=1024, tk=1024, epi=relu2)
    big["down1"] = fetch("down1", r1)
    ys.append(mm(r1, big["down1"], mode="nn", name="down1", outs=[F32], tm=min(s, 1024), tn=1024, tk=2048))

    gb, red_ln, red_mod = {}, [None] * 4, [None] * 4

    def mlp_bwd(i, h, r, dyy):
        gb[f"down{i}"] = mm_w(r, dyy, name=f"g_down{i}", outs=[MXU_DTYPE], tn=1024)
        da = mm(dyy, big[f"down{i}"], mode="nt", name=f"d_down{i}", outs=[MXU_DTYPE], tn=1024, tk=1024,
                after=emit(f"down{i}", gb[f"down{i}"]),
                epi=lambda acc, rv: [acc * (2.0 * jnp.sqrt(rv.astype(F32)))], extras=[("full", r)])
        gb[f"up{i}"] = mm_w(h, da, name=f"g_up{i}", outs=[MXU_DTYPE], tn=1024)
        return [mm(da, big[f"up{i}"], mode="nt", name=f"d_up{i}", outs=[F32], tn=1024, tk=1024, after=emit(f"up{i}", gb[f"up{i}"]))]

    def join(sub, dxr, dhs, after=None):
        res = _mod_ln_bwd(dxr, dhs, xs[sub], scale[sub], xs[sub - 1], ys[sub - 1], gate[sub - 1], lg[sub - 1],
                          f"mod_ln_bwd{sub}", after=after)
        red_mod[sub], red_ln[sub - 1] = res[2], res[3]
        return res[0], res[1]

    loss, dxr, dyy, red_ln[3] = _last_ln_loss_bwd(xs[3], ys[3], gate[3], lg[3], lb[3], target, "ln3_loss_bwd")
    dxr, dyy = join(3, dxr, mlp_bwd(1, h3, r1, dyy))
    gb["b_w_out"] = mm_w(o_mix, dyy, name="g_b_out", outs=[MXU_DTYPE], tn=1024, tk=1024)
    do = mm(dyy, big["b_w_out"], mode="nt", name="d_b_out", outs=[F32], tn=1024, tk=1024, after=emit("b_w_out", gb["b_w_out"]))
    parts = _combine_bwd(do, o_mix, l_g, dils, "combine_bwd")
    dhs, gq = [], None
    for g, (_, dil) in enumerate(B_PATTERNS):
        do_g, dd_g = parts[g][0].reshape(s, d), _to_streams(parts[g][1], dil)
        dqkv = _attn_bwd(qkvs[g], do_g, l_streams[g], dd_g, small["slopes"], dil, f"attn_bwd{g}")
        gq = mm_w(hg[g], dqkv, name=f"g_qkv{g}", outs=[MXU_DTYPE], tn=1024, out_col0=g * 3 * d, out_cols=len(B_PATTERNS) * 3 * d, into=gq)
        dh = mm(dqkv, big["b_w_qkv"], mode="nt", name=f"d_qkv{g}", outs=[F32], tn=1024, tk=768, b_col0=g * 3 * d)
        dhs.append(dh if dil == 1 else dh.reshape(dil, s // dil, d))
    gb["b_w_qkv"] = gq
    dxr, dyy = join(2, dxr, dhs, after=emit("b_w_qkv", gb["b_w_qkv"]))
    dxr, dyy = join(1, dxr, mlp_bwd(0, h1, r0, dyy))
    gb["a_w_out"] = mm_w(gated, dyy, name="g_a_out", outs=[MXU_DTYPE], tn=1024)
    dgated = mm(dyy, big["a_w_out"], mode="nt", name="d_a_out", outs=[F32], tn=1024, tk=1024, after=emit("a_w_out", gb["a_w_out"]))
    duv, dws, dbias, dbin, dvg, dvb = _spatial_bwd(uvpre, dgated, small["a_vn_g"], small["a_vn_b"], small["wc"],
                                                   small["wct"], small["bias_full"], "a_spatial_bwd")
    tril = jnp.tril(jnp.ones((CHUNK, CHUNK), bool))
    dws = jnp.where(tril, dws, 0.0).reshape(-1, LANES)
    gb["a_w_in"] = mm_w(h0, duv, name="g_a_in", outs=[MXU_DTYPE], tn=1024, after=emit("a_w_s", dws.astype(MXU_DTYPE)))
    dh = mm(duv, big["a_w_in"], mode="nt", name="d_a_in", outs=[F32], tn=1024, tk=512, after=emit("a_w_in", gb["a_w_in"]))
    dx, red_mod[0] = _mod_bwd(dxr, [dh], xs[0], scale[0], "mod_bwd0")
    dm = [jnp.concatenate([red_mod[i][0], red_mod[i][1], red_ln[i][2]]) for i in range(4)]
    dlg, dlb = [red_ln[i][0] for i in range(4)], [red_ln[i][1] for i in range(4)]

    gsmall = {
        "a_b_in": dbin.reshape(-1), "a_vn_g": dvg.reshape(-1), "a_vn_b": dvb.reshape(-1),
        "a_w_s": dws.reshape(-1),
        "a_b_s": dbias.reshape(CHUNK, A_GROUPS, d // A_GROUPS).sum(-1).T.reshape(-1),
    }
    return loss, dx, gb, jnp.stack(dm), jnp.stack(dlg), jnp.stack(dlb), gsmall


BIG = ("a_w_in", "a_w_out", "up0", "down0", "b_w_qkv", "b_w_out", "up1", "down1")
BIG_KIND = {"a_w_in": "col", "a_w_out": "row", "b_w_qkv": "col", "b_w_out": "row",
            "up0": "col", "up1": "col", "down0": "row", "down1": "row", "a_w_s": "all"}
HALVED = ("a_w_in", "down0", "b_w_qkv")
SCATTER_GROUPS = (("down1", "up1"), ("b_w_out", "b_w_qkv"), ("down0", "up0"), ("a_w_out", "a_w_in"), ("a_w_s",))
SMALL = ("a_b_in", "a_vn_g", "a_vn_b", "a_b_s")


def kernel(x, c, ada_w, ada_b, ln_g, ln_b, a_w_in, a_b_in, a_vn_g, a_vn_b, a_w_s, a_b_s, a_w_out, b_w_qkv, b_w_out, mlp_w_up, mlp_w_down, loss_target, m_ada_w, m_ada_b, m_ln_g, m_ln_b, m_a_w_in, m_a_b_in, m_a_vn_g, m_a_vn_b, m_a_w_s, m_a_b_s, m_a_w_out, m_b_w_qkv, m_b_w_out, m_mlp_w_up, m_mlp_w_down, v_ada_w, v_ada_b, v_ln_g, v_ln_b, v_a_w_in, v_a_b_in, v_a_vn_g, v_a_vn_b, v_a_w_s, v_a_b_s, v_a_w_out, v_b_w_qkv, v_b_w_out, v_mlp_w_up, v_mlp_w_down):
    s, d = x.shape[1], x.shape[2]
    xi, yi, ci = _me()
    q = 2 * xi + yi
    dev = 2 * q + ci
    nsub = 2 * DEPTH
    cs = ada_w.shape[-1]
    ls = ln_g.shape[-1]

    shards = {
        "a_w_in": a_w_in[0], "a_w_out": a_w_out[0], "b_w_qkv": b_w_qkv[0], "b_w_out": b_w_out[0],
        "up0": mlp_w_up[0], "up1": mlp_w_up[1], "down0": mlp_w_down[0], "down1": mlp_w_down[1],
    }
    cast = [shards[k].astype(MXU_DTYPE) for k in BIG]

    pack = jnp.concatenate([c.reshape(-1), ln_g.reshape(-1), ln_b.reshape(-1)]).reshape(-1, LANES)
    got = _all_gather_small(pack, "gather_small", after=cast).reshape(N_DEV, -1)
    c_all = got[:, :d]
    per_chip = got[0::2]
    ln_g_full = per_chip[:, d:d + nsub * ls].reshape(N_CHIPS, nsub, ls).transpose(1, 0, 2).reshape(nsub, d)
    ln_b_full = per_chip[:, d + nsub * ls:].reshape(N_CHIPS, nsub, ls).transpose(1, 0, 2).reshape(nsub, d)
    m_part = _ada_fwd(c_all, ada_w.reshape(nsub, d, cs), ada_b.reshape(nsub, 1, cs), "ada_fwd")
    m_all = _all_gather_small(m_part.reshape(-1, LANES), "gather_mod").reshape(N_DEV, nsub, N_DEV, cs)
    m_mine = lax.dynamic_index_in_dim(m_all[0::2], dev, axis=2, keepdims=False)
    mvec = m_mine.transpose(1, 0, 2).reshape(nsub, 3 * d)

    halved = {BIG.index(k) for k in HALVED}
    send_sems, recv_sems, shard_thru, lands, token = _gather_start(cast, halved, mvec, "gather_start")

    def fetch(k, after):
        w = BIG.index(k)
        shard, gw = _gather_wait(w, shard_thru[w], lands[w], send_sems, recv_sems, after, f"gather_wait_{k}", w in halved)
        if w in halved:
            gw = _assemble_halves(shard, gw, f"assemble_{k}")
        return gw if BIG_KIND[k] == "col" else gw.reshape(1, -1, gw.shape[-1])

    scattering, pending = {}, {}

    def emit(k, g):
        pending[k] = g
        group = next(gr for gr in SCATTER_GROUPS if k in gr)
        if k != group[-1]:
            return None
        scattering[group] = _scatter_start([pending[m] for m in group], [BIG_KIND[m] for m in group], f"scatter_start_{k}")
        return scattering[group][2][0]

    tril = jnp.tril(jnp.ones((CHUNK, CHUNK), bool))
    wc = jnp.where(tril, a_w_s[0], 0.0).astype(MXU_DTYPE)
    heads = jnp.arange(1, B_HEADS + 1, dtype=F32)
    small = {
        "a_b_in": a_b_in, "a_vn_g": a_vn_g, "a_vn_b": a_vn_b,
        "wc": wc, "wct": wc.transpose(0, 2, 1),
        "bias_full": jnp.repeat(a_b_s[0].T, d // A_GROUPS, axis=1),
        "slopes": jnp.exp2(-8.0 * heads / B_HEADS),
    }

    loss_part, grad_x, gb, dm, dlg, dlb, gsmall = _local_step(x[0], loss_target[0], mvec, ln_g_full, ln_b_full, small, fetch, emit, token)

    weights = dict(ada_w=ada_w, ada_b=ada_b, ln_g=ln_g, ln_b=ln_b, a_w_in=a_w_in, a_b_in=a_b_in, a_vn_g=a_vn_g, a_vn_b=a_vn_b,
                   a_w_s=a_w_s, a_b_s=a_b_s, a_w_out=a_w_out, b_w_qkv=b_w_qkv, b_w_out=b_w_out, mlp_w_up=mlp_w_up, mlp_w_down=mlp_w_down)
    ms = dict(ada_w=m_ada_w, ada_b=m_ada_b, ln_g=m_ln_g, ln_b=m_ln_b, a_w_in=m_a_w_in, a_b_in=m_a_b_in, a_vn_g=m_a_vn_g, a_vn_b=m_a_vn_b,
              a_w_s=m_a_w_s, a_b_s=m_a_b_s, a_w_out=m_a_w_out, b_w_qkv=m_b_w_qkv, b_w_out=m_b_w_out, mlp_w_up=m_mlp_w_up, mlp_w_down=m_mlp_w_down)
    vs = dict(ada_w=v_ada_w, ada_b=v_ada_b, ln_g=v_ln_g, ln_b=v_ln_b, a_w_in=v_a_w_in, a_b_in=v_a_b_in, a_vn_g=v_a_vn_g, a_vn_b=v_a_vn_b,
              a_w_s=v_a_w_s, a_b_s=v_a_b_s, a_w_out=v_a_w_out, b_w_qkv=v_b_w_qkv, b_w_out=v_b_w_out, mlp_w_up=v_mlp_w_up, mlp_w_down=v_mlp_w_down)
    grads, updates = {}, {}

    def update(k):
        updates[k] = _adamw(weights[k], grads[k], ms[k], vs[k], f"adamw_{k}")
        return updates[k][0]

    gfull = {}

    def big_group(group, after):
        bufs = []
        for pair in (group[:2], group[2:]):
            bufs += _scatter_wait(*scattering[pair], [BIG_KIND[m] for m in pair], after, f"scatter_wait_{pair[-1]}")
        parts = [[i] for i, k in enumerate(group) if k == "b_w_qkv"] + [[i for i, k in enumerate(group) if k != "b_w_qkv"]]
        for part in parts:
            fulls = _sum_swap([bufs[i] for i in part], f"sum_swap_{group[part[0]]}")
            gfull.update({group[i]: f.reshape(-1, f.shape[-1]) for i, f in zip(part, fulls)})

    big_group(SCATTER_GROUPS[0] + SCATTER_GROUPS[1], grad_x)
    grads["b_w_qkv"], grads["b_w_out"] = gfull["b_w_qkv"][None], gfull["b_w_out"][None]
    update("b_w_out")
    done = update("b_w_qkv")

    pack_b = jnp.concatenate([dm.reshape(-1), dlg.reshape(-1), dlb.reshape(-1)] + [gsmall[k] for k in SMALL] + [loss_part.reshape(1)])
    n_small = pack_b.shape[0]
    pack_b = jnp.pad(pack_b, (0, -n_small % (256 * LANES)))
    got_b = _all_gather_small(pack_b.reshape(-1, LANES), "gather_small_grads", after=[done]).reshape(N_DEV, -1, LANES)
    tot = _sum_slots(got_b, "sum_small").reshape(-1)
    o = 0
    dm_tot = tot[o:o + nsub * 3 * d].reshape(nsub, 3 * d); o += nsub * 3 * d
    dlg_tot = tot[o:o + nsub * d].reshape(nsub, d); o += nsub * d
    dlb_tot = tot[o:o + nsub * d].reshape(nsub, d); o += nsub * d
    g_small = {}
    for k, ref in zip(SMALL, (a_b_in, a_vn_g, a_vn_b, a_b_s)):
        g_small[k] = tot[o:o + ref.size].reshape(ref.shape); o += ref.size
    loss = tot[o]
    assert o + 1 == n_small
    aws = _scatter_wait(*scattering[("a_w_s",)], ["all"], tot, "scatter_wait_a_w_s")[0]
    g_small["a_w_s"] = _sum_slots(aws, "sum_a_w_s").reshape(a_w_s.shape)
    dm_all = got_b.reshape(N_DEV, -1)[:, :nsub * 3 * d].reshape(N_DEV, nsub, 3 * d)
    dm_cols = lax.dynamic_slice_in_dim(dm_all, q * cs, cs, axis=2).transpose(1, 0, 2)
    grads.update({
        "ada_w": _ada_bwd(c_all.T, dm_cols, "ada_bwd").reshape(ada_w.shape),
        "ada_b": lax.dynamic_slice_in_dim(dm_tot, q * cs, cs, axis=1).reshape(ada_b.shape),
        "ln_g": lax.dynamic_slice_in_dim(dlg_tot, q * ls, ls, axis=1).reshape(ln_g.shape),
        "ln_b": lax.dynamic_slice_in_dim(dlb_tot, q * ls, ls, axis=1).reshape(ln_b.shape),
        **g_small,
    })
    for k in ("ada_b", "ln_g", "ln_b", "a_w_s") + SMALL:
        update(k)
    done = update("ada_w")

    big_group(SCATTER_GROUPS[2] + SCATTER_GROUPS[3], done)
    grads.update({
        "a_w_in": gfull["a_w_in"][None], "a_w_out": gfull["a_w_out"][None],
        "mlp_w_up": jnp.stack([gfull["up0"], gfull["up1"]]), "mlp_w_down": jnp.stack([gfull["down0"], gfull["down1"]]),
    })
    for k in ("a_w_in", "a_w_out", "mlp_w_up", "mlp_w_down"):
        update(k)
    names = list(weights)
    return (loss, grad_x[None], *[grads[k] for k in names], *[updates[k][0] for k in names],
            *[updates[k][1] for k in names], *[updates[k][2] for k in names])
```

```python
import functools
import math

import jax
import jax.numpy as jnp
from jax import lax
from jax.experimental import pallas as pl
from jax.experimental.pallas import tpu as pltpu

F32 = jnp.float32
MXU_DTYPE = jnp.bfloat16

DEPTH = 2
CHUNK = 128
A_GROUPS = 16
B_HEADS = 16
HEAD_DIM = 64
B_PATTERNS = ((128, 1), (512, 4), (2048, 16))
SPAN = 128
ALPHA = (2 * DEPTH) ** 0.25
LN_EPS = 1e-5
NEG = -1e30
ATT_SCALE = HEAD_DIM ** -0.5
ADAM_LR, ADAM_B1, ADAM_B2, ADAM_EPS, ADAM_WD, ADAM_STEP = 0.001, 0.9, 0.999, 1e-08, 0.01, 10

N_CHIPS = 4
N_DEV = 8
LANES = 128
SUBLANES = 8
VMEM_LIMIT = 52 * 1024 * 1024
ROW_TILE = 512
MM_ROW_CHUNK = 256
MESH = pl.DeviceIdType.MESH


def _cparams(sem):
    return pltpu.CompilerParams(dimension_semantics=sem, vmem_limit_bytes=VMEM_LIMIT)


def _fold8(v):
    r, c = v.shape
    return jnp.sum(v.reshape(r // SUBLANES, SUBLANES, c), axis=0)


def _gelu(x):
    c = math.sqrt(2.0 / math.pi)
    return 0.5 * x * (1.0 + jnp.tanh(c * (x + 0.044715 * (x * x * x))))


def _gelu_and_grad(x):
    c = math.sqrt(2.0 / math.pi)
    t = jnp.tanh(c * (x + 0.044715 * (x * x * x)))
    return 0.5 * x * (1.0 + t), 0.5 * (1.0 + t) + 0.5 * x * (1.0 - t * t) * c * (1.0 + 3.0 * 0.044715 * x * x)


def _dot(a, b, dims):
    return lax.dot_general(a.astype(MXU_DTYPE), b.astype(MXU_DTYPE), (dims, ((), ())), preferred_element_type=F32)


def _dot_nn(a, b):
    return _dot(a, b, ((1,), (0,)))


def _dot_nt(a, b):
    return _dot(a, b, ((1,), (1,)))


def _dot_tn(a, b):
    return _dot(a, b, ((0,), (0,)))


def _mm(a, b, *, mode, name, outs, tm, tn, tk, epi=None, extras=(), b_col0=0, n_out=None, after=None,
        out_col0=0, out_cols=None, into=None):
    if mode == "nn":
        m, kdim = a.shape
        p, kb, ns = b.shape
        assert kb == kdim and ns % tn == 0 and b_col0 % tn == 0
        n = n_out if n_out is not None else p * ns
        npt, j0 = ns // tn, b_col0 // tn
        a_spec = pl.BlockSpec((tm, tk), lambda i, j, k: (i, k))
        b_spec = pl.BlockSpec((None, tk, tn), lambda i, j, k: ((j + j0) // npt, k, (j + j0) % npt))
        dot = _dot_nn
    elif mode == "nt":
        m, kdim = a.shape
        p, n, ns = b.shape
        assert ns % tk == 0 and b_col0 % tk == 0
        npt, j0 = ns // tk, b_col0 // tk
        a_spec = pl.BlockSpec((tm, tk), lambda i, j, k: (i, k))
        b_spec = pl.BlockSpec((None, tn, tk), lambda i, j, k: ((k + j0) // npt, j, (k + j0) % npt))
        dot = _dot_nt
    else:
        kdim, m = a.shape
        kb, n = b.shape
        assert kb == kdim
        a_spec = pl.BlockSpec((tk, tm), lambda i, j, k: (k, i))
        b_spec = pl.BlockSpec((tk, tn), lambda i, j, k: (k, j))
        dot = _dot_tn
    assert m % tm == 0 and n % tn == 0 and kdim % tk == 0, (name, m, n, kdim, tm, tn, tk)
    nk = kdim // tk
    ex_specs, ex_arrays = [], []
    for kind, arr in extras:
        if kind == "row":
            ex_specs.append(pl.BlockSpec((1, tn), lambda i, j, k: (0, j)))
        else:
            ex_specs.append(pl.BlockSpec((tm, tn), lambda i, j, k: (i, j)))
        ex_arrays.append(arr)
    n_ex, n_o = len(ex_arrays), len(outs)
    deps = [d for d in (after, into) if d is not None]
    n_dep = len(deps)
    j_out = out_col0 // tn
    assert out_col0 % tn == 0 and (into is None or len(outs) == 1)

    def body(a_ref, b_ref, *rest):
        ex_refs, o_refs = rest[:n_ex], rest[n_ex + n_dep:n_ex + n_dep + n_o]
        k = pl.program_id(2)

        chunks = [slice(r0, r0 + min(tm, MM_ROW_CHUNK)) for r0 in range(0, tm, min(tm, MM_ROW_CHUNK))]

        def part(rows):
            return dot(a_ref[:, rows] if mode == "tn" else a_ref[rows, :], b_ref[...])

        def finish(r, rows):
            exs = [e[...] if kind == "row" else e[rows, :] for (kind, _), e in zip(extras, ex_refs)]
            vals = epi(r, *exs) if epi is not None else [r]
            for o, v in zip(o_refs, vals):
                o[rows, :] = v.astype(o.dtype)

        if nk == 1:
            for rows in chunks:
                finish(part(rows), rows)
            return
        acc = rest[n_ex + n_dep + n_o]

        @pl.when(k == 0)
        def _():
            for rows in chunks:
                acc[rows, :] = part(rows)

        @pl.when((k > 0) & (k < nk - 1))
        def _():
            for rows in chunks:
                acc[rows, :] += part(rows)

        @pl.when(k == nk - 1)
        def _():
            for rows in chunks:
                finish(acc[rows, :] + part(rows), rows)

    res = pl.pallas_call(
        body,
        grid=(m // tm, n // tn, nk),
        in_specs=[a_spec, b_spec] + ex_specs + [pl.BlockSpec(memory_space=pl.ANY)] * n_dep,
        out_specs=[pl.BlockSpec((tm, tn), lambda i, j, k: (i, j + j_out)) for _ in outs],
        out_shape=[jax.ShapeDtypeStruct((m, out_cols or n), dt) for dt in outs],
        input_output_aliases={} if into is None else {2 + n_ex + n_dep - 1: 0},
        scratch_shapes=[pltpu.VMEM((tm, tn), F32)] if nk > 1 else [],
        name=name,
        compiler_params=_cparams(("parallel", "parallel", "arbitrary")),
    )(a, b, *ex_arrays, *deps)
    return res if len(outs) > 1 else res[0]


def _rows(body, n_rows, tr, ins, outs, name, scratch=()):
    def spec(kind, shape):
        if kind == "blk":
            return pl.BlockSpec((tr,) + tuple(shape[1:]), lambda i: (i,) + (0,) * (len(shape) - 1))
        if kind == "dep":
            return pl.BlockSpec(memory_space=pl.ANY)
        if kind == "str":
            return pl.BlockSpec((shape[0], tr // shape[0], shape[2]), lambda i: (0, i, 0))
        return pl.BlockSpec(tuple(shape), lambda i: (0,) * len(shape))

    return pl.pallas_call(
        body,
        grid=(n_rows // tr,),
        in_specs=[spec(k, a.shape) for k, a in ins],
        out_specs=[spec(k, s) for k, s, _ in outs],
        out_shape=[jax.ShapeDtypeStruct(tuple(s), d) for _, s, d in outs],
        scratch_shapes=list(scratch),
        name=name,
        compiler_params=_cparams(("arbitrary",)),
    )(*[a for _, a in ins])


def _ln_stats(z):
    mu = jnp.mean(z, axis=-1, keepdims=True)
    zc = z - mu
    var = jnp.mean(zc * zc, axis=-1, keepdims=True)
    rstd = lax.rsqrt(var + LN_EPS)
    return zc * rstd, rstd


def _stream_scratch(c):
    return pltpu.VMEM((c // LANES, ROW_TILE, LANES), F32)


def _streams_in(ref3, scr):
    dil, n, c = ref3.shape
    for r in range(dil):
        for j in range(c // LANES):
            scr.at[j][pl.ds(r, n, stride=dil), :] = ref3[r, :, j * LANES:(j + 1) * LANES].astype(F32)
    return jnp.concatenate([scr[j] for j in range(c // LANES)], axis=1)


def _streams_out(val, ref3, scr):
    dil, n, c = ref3.shape
    for j in range(c // LANES):
        scr[j] = val[:, j * LANES:(j + 1) * LANES].astype(F32)
    for r in range(dil):
        for j in range(c // LANES):
            ref3[r, :, j * LANES:(j + 1) * LANES] = scr.at[j][pl.ds(r, n, stride=dil), :].astype(ref3.dtype)


def _mod(x, scale, shift, after, name):
    s, d = x.shape

    def body(x_ref, sc_ref, sh_ref, dep_ref, h_ref):
        h_ref[...] = (x_ref[...] * (1.0 + sc_ref[...]) + sh_ref[...]).astype(h_ref.dtype)

    return _rows(body, s, ROW_TILE, [("blk", x), ("all", scale), ("all", shift), ("dep", after)], [("blk", (s, d), MXU_DTYPE)], name)[0]


def _resid_ln(x, y, gate, g, b, nxt, name, dils=()):
    s, d = x.shape

    def body(x_ref, y_ref, gate_ref, g_ref, b_ref, sc_ref, sh_ref, xn_ref, h_ref, *rest):
        z = ALPHA * x_ref[...] + gate_ref[...] * y_ref[...]
        xhat, _ = _ln_stats(z)
        xn = xhat * g_ref[...] + b_ref[...]
        xn_ref[...] = xn
        h = xn * (1.0 + sc_ref[...]) + sh_ref[...]
        h_ref[...] = h.astype(h_ref.dtype)
        for hs_ref in rest[:len(dils)]:
            _streams_out(h, hs_ref, rest[-1])

    return _rows(body, s, ROW_TILE,
                 [("blk", x), ("blk", y), ("all", gate), ("all", g), ("all", b), ("all", nxt[0]), ("all", nxt[1])],
                 [("blk", (s, d), F32), ("blk", (s, d), MXU_DTYPE)] + [("str", (dil, s // dil, d), MXU_DTYPE) for dil in dils], name,
                 scratch=[_stream_scratch(d)] if dils else [])


def _mod_bwd(dxr, dhs, x, scale, name, after=None):
    s, d = x.shape
    n_dh = len(dhs)
    n_dep = 0 if after is None else 1

    def body(dxr_ref, *rest):
        dh_refs = rest[:n_dh]
        x_ref, sc_ref, dx_ref, red_ref, a_sh, a_sc = rest[n_dh:n_dh + 2] + rest[n_dh + 2 + n_dep:]
        i = pl.program_id(0)

        @pl.when(i == 0)
        def _():
            a_sh[...] = jnp.zeros_like(a_sh)
            a_sc[...] = jnp.zeros_like(a_sc)

        dh = dh_refs[0][...]
        for r in dh_refs[1:]:
            dh = dh + r[...]
        dx_ref[...] = dxr_ref[...] + dh * (1.0 + sc_ref[...])
        a_sh[...] += _fold8(dh)
        a_sc[...] += _fold8(dh * x_ref[...])

        @pl.when(i == pl.num_programs(0) - 1)
        def _():
            red_ref[...] = jnp.zeros_like(red_ref)
            red_ref[0:1, :] = jnp.sum(a_sh[...], axis=0, keepdims=True)
            red_ref[1:2, :] = jnp.sum(a_sc[...], axis=0, keepdims=True)

    return _rows(body, s, ROW_TILE, [("blk", dxr)] + [("blk", h) for h in dhs] + [("blk", x), ("all", scale)] + [("dep", after)] * n_dep,
                 [("blk", (s, d), F32), ("all", (SUBLANES, d), F32)], name,
                 scratch=[pltpu.VMEM((SUBLANES, d), F32)] * 2)


def _last_ln_loss_bwd(x, y, gate, g, b, target, name):
    s, d = x.shape

    def body(x_ref, y_ref, gate_ref, g_ref, b_ref, t_ref, l_ref, dxr_ref, dyy_ref, red_ref, a_l, a_g, a_b, a_gate):
        i = pl.program_id(0)

        @pl.when(i == 0)
        def _():
            for a in (a_l, a_g, a_b, a_gate):
                a[...] = jnp.zeros_like(a)

        yv = y_ref[...]
        z = ALPHA * x_ref[...] + gate_ref[...] * yv
        xhat, rstd = _ln_stats(z)
        e = xhat * g_ref[...] + b_ref[...] - t_ref[...]
        a_l[...] += _fold8(e * e)
        dxo_v = e * (1.0 / d)
        dxh = dxo_v * g_ref[...]
        dz = rstd * (dxh - jnp.mean(dxh, axis=-1, keepdims=True) - xhat * jnp.mean(dxh * xhat, axis=-1, keepdims=True))
        dxr_ref[...] = ALPHA * dz
        dyy_ref[...] = (gate_ref[...] * dz).astype(dyy_ref.dtype)
        a_g[...] += _fold8(dxo_v * xhat)
        a_b[...] += _fold8(dxo_v)
        a_gate[...] += _fold8(dz * yv)

        @pl.when(i == pl.num_programs(0) - 1)
        def _():
            l_ref[...] = jnp.full(l_ref.shape, 0.5 / d, F32) * jnp.sum(a_l[...])
            red_ref[...] = jnp.zeros_like(red_ref)
            red_ref[0:1, :] = jnp.sum(a_g[...], axis=0, keepdims=True)
            red_ref[1:2, :] = jnp.sum(a_b[...], axis=0, keepdims=True)
            red_ref[2:3, :] = jnp.sum(a_gate[...], axis=0, keepdims=True)

    l, dxr, dyy, red = _rows(
        body, s, ROW_TILE, [("blk", x), ("blk", y), ("all", gate), ("all", g), ("all", b), ("blk", target)],
        [("all", (SUBLANES, LANES), F32), ("blk", (s, d), F32), ("blk", (s, d), MXU_DTYPE), ("all", (SUBLANES, d), F32)], name,
        scratch=[pltpu.VMEM((SUBLANES, d), F32)] * 4)
    return l[0, 0], dxr, dyy, red


def _mod_ln_bwd(dxr, dhs, x, scale, x_in, y, gate, g, name, after=None):
    s, d = x.shape
    n_dh = len(dhs)
    n_dep = 0 if after is None else 1

    def body(dxr_ref, *rest):
        dh_refs = rest[:n_dh]
        x_ref, sc_ref, xin_ref, y_ref, gate_ref, g_ref = rest[n_dh:n_dh + 6]
        dxr_out, dyy_ref, red_mod, red_ln, a_sh, a_sc, a_g, a_b, a_gate = rest[n_dh + 6 + n_dep:n_dh + 15 + n_dep]
        i = pl.program_id(0)

        @pl.when(i == 0)
        def _():
            for a in (a_sh, a_sc, a_g, a_b, a_gate):
                a[...] = jnp.zeros_like(a)

        dh = dh_refs[0][...]
        for r in dh_refs[1:]:
            dh = dh + (r[...] if len(r.shape) == 2 else _streams_in(r, rest[-1]))
        xv = x_ref[...]
        dxo_v = dxr_ref[...] + dh * (1.0 + sc_ref[...])
        a_sh[...] += _fold8(dh)
        a_sc[...] += _fold8(dh * xv)
        yv = y_ref[...]
        z = ALPHA * xin_ref[...] + gate_ref[...] * yv
        xhat, rstd = _ln_stats(z)
        dxh = dxo_v * g_ref[...]
        dz = rstd * (dxh - jnp.mean(dxh, axis=-1, keepdims=True) - xhat * jnp.mean(dxh * xhat, axis=-1, keepdims=True))
        dxr_out[...] = ALPHA * dz
        dyy_ref[...] = (gate_ref[...] * dz).astype(dyy_ref.dtype)
        a_g[...] += _fold8(dxo_v * xhat)
        a_b[...] += _fold8(dxo_v)
        a_gate[...] += _fold8(dz * yv)

        @pl.when(i == pl.num_programs(0) - 1)
        def _():
            red_mod[...] = jnp.zeros_like(red_mod)
            red_mod[0:1, :] = jnp.sum(a_sh[...], axis=0, keepdims=True)
            red_mod[1:2, :] = jnp.sum(a_sc[...], axis=0, keepdims=True)
            red_ln[...] = jnp.zeros_like(red_ln)
            red_ln[0:1, :] = jnp.sum(a_g[...], axis=0, keepdims=True)
            red_ln[1:2, :] = jnp.sum(a_b[...], axis=0, keepdims=True)
            red_ln[2:3, :] = jnp.sum(a_gate[...], axis=0, keepdims=True)

    ins = ([("blk", dxr)] + [("blk" if h.ndim == 2 else "str", h) for h in dhs]
           + [("blk", x), ("all", scale), ("blk", x_in), ("blk", y), ("all", gate), ("all", g)] + [("dep", after)] * n_dep)
    return _rows(body, s, ROW_TILE, ins,
                 [("blk", (s, d), F32), ("blk", (s, d), MXU_DTYPE), ("all", (SUBLANES, d), F32), ("all", (SUBLANES, d), F32)], name,
                 scratch=[pltpu.VMEM((SUBLANES, d), F32)] * 5 + [_stream_scratch(d)] * any(h.ndim == 3 for h in dhs))


def _left_half(shape):
    return lax.broadcasted_iota(jnp.int32, shape, 1) < (LANES // 2)


CHUNKS_PER_STEP = 2


def _chunks_of_step():
    return [slice(i * CHUNK, (i + 1) * CHUNK) for i in range(CHUNKS_PER_STEP)]


def _spatial_z(vn, wc_ref, bias_ref, j):
    vb = vn[:, j * LANES:(j + 1) * LANES]
    z0 = _dot_nn(wc_ref[2 * j], vb)
    z1 = _dot_nn(wc_ref[2 * j + 1], vb)
    return jnp.where(_left_half(z0.shape), z0, z1) + bias_ref[:, j * LANES:(j + 1) * LANES]


def _spatial_fwd(uvpre, vn_g, vn_b, wc, bias_full, name):
    s, d2 = uvpre.shape
    d = d2 // 2

    def body(uv_ref, g_ref, b_ref, wc_ref, bias_ref, out_ref):
        for rows in _chunks_of_step():
            u = _gelu(uv_ref[rows, :d])
            v = _gelu(uv_ref[rows, d:])
            vh, _ = _ln_stats(v)
            vn = vh * g_ref[...] + b_ref[...]
            for j in range(d // LANES):
                z = _spatial_z(vn, wc_ref, bias_ref, j)
                out_ref[rows, j * LANES:(j + 1) * LANES] = (u[:, j * LANES:(j + 1) * LANES] * z).astype(out_ref.dtype)

    return _rows(body, s, CHUNKS_PER_STEP * CHUNK, [("blk", uvpre), ("all", vn_g), ("all", vn_b), ("all", wc), ("all", bias_full)],
                 [("blk", (s, d), MXU_DTYPE)], name)[0]


def _spatial_bwd(uvpre, dgated, vn_g, vn_b, wc, wct, bias_full, name):
    s, d2 = uvpre.shape
    d = d2 // 2

    def body(uv_ref, dg_ref, g_ref, b_ref, wc_ref, wct_ref, bias_ref,
             duv_ref, dws_ref, dbias_ref, dbin_ref, dvg_ref, dvb_ref, dvn_buf, a_bin, a_vg, a_vb):
        i = pl.program_id(0)

        @pl.when(i == 0)
        def _():
            dws_ref[...] = jnp.zeros_like(dws_ref)
            dbias_ref[...] = jnp.zeros_like(dbias_ref)
            a_bin[...] = jnp.zeros_like(a_bin)
            a_vg[...] = jnp.zeros_like(a_vg)
            a_vb[...] = jnp.zeros_like(a_vb)

        for rows in _chunks_of_step():
            u, u_grad = _gelu_and_grad(uv_ref[rows, :d])
            v, v_grad = _gelu_and_grad(uv_ref[rows, d:])
            vh, rstd = _ln_stats(v)
            vn = vh * g_ref[...] + b_ref[...]
            dg = dg_ref[rows, :]
            dzz = dg * u
            dbias_ref[...] += dzz
            for j in range(d // LANES):
                cols = slice(j * LANES, (j + 1) * LANES)
                z = _spatial_z(vn, wc_ref, bias_ref, j)
                dup = dg[:, cols] * z * u_grad[:, cols]
                duv_ref[rows, cols] = dup.astype(duv_ref.dtype)
                a_bin[:, cols] += _fold8(dup)
                dzb = dzz[:, cols]
                left = _left_half(dzb.shape)
                dvn_buf[:, cols] = jnp.where(left, _dot_nn(wct_ref[2 * j], dzb), _dot_nn(wct_ref[2 * j + 1], dzb))
                vb = vn[:, cols]
                dws_ref[2 * j] += _dot_nt(jnp.where(left, dzb, 0.0), vb)
                dws_ref[2 * j + 1] += _dot_nt(jnp.where(left, 0.0, dzb), vb)
            dvn = dvn_buf[...]
            a_vg[...] += _fold8(dvn * vh)
            a_vb[...] += _fold8(dvn)
            dvh = dvn * g_ref[...]
            dv = rstd * (dvh - jnp.mean(dvh, axis=-1, keepdims=True) - vh * jnp.mean(dvh * vh, axis=-1, keepdims=True))
            dvp = dv * v_grad
            duv_ref[rows, d:] = dvp.astype(duv_ref.dtype)
            a_bin[:, d:] += _fold8(dvp)

        @pl.when(i == pl.num_programs(0) - 1)
        def _():
            dbin_ref[...] = jnp.sum(a_bin[...], axis=0, keepdims=True)
            dvg_ref[...] = jnp.sum(a_vg[...], axis=0, keepdims=True)
            dvb_ref[...] = jnp.sum(a_vb[...], axis=0, keepdims=True)

    return _rows(body, s, CHUNKS_PER_STEP * CHUNK,
                 [("blk", uvpre), ("blk", dgated), ("all", vn_g), ("all", vn_b), ("all", wc), ("all", wct), ("all", bias_full)],
                 [("blk", (s, d2), MXU_DTYPE), ("all", (A_GROUPS, CHUNK, CHUNK), F32), ("all", (CHUNK, d), F32),
                  ("all", (1, d2), F32), ("all", (1, d), F32), ("all", (1, d), F32)], name,
                 scratch=[pltpu.VMEM((CHUNK, d), F32), pltpu.VMEM((SUBLANES, d2), F32),
                          pltpu.VMEM((SUBLANES, d), F32), pltpu.VMEM((SUBLANES, d), F32)])


def _head_mask(v, h):
    lane = lax.broadcasted_iota(jnp.int32, v.shape, 1)
    return jnp.where((lane >= h * HEAD_DIM) & (lane < (h + 1) * HEAD_DIM), v, jnp.zeros_like(v))


def _att_bias(slopes, dil):
    qi = lax.broadcasted_iota(jnp.int32, (SPAN, SPAN), 0)
    ki = lax.broadcasted_iota(jnp.int32, (SPAN, SPAN), 1)
    sl = slopes[:, None, None]
    cur = jnp.where(ki <= qi, -sl * (float(dil) * (qi - ki).astype(F32)), NEG)
    prev = jnp.where(ki >= qi, -sl * (float(dil) * (SPAN + qi - ki).astype(F32)), NEG)
    absent = jnp.full_like(prev, NEG)
    pairs = slopes.shape[0] // 2

    def fwd(pv):
        return jnp.concatenate([cur, pv], axis=2).reshape(pairs, 2 * SPAN, 2 * SPAN)

    def bwd(pv):
        return jnp.concatenate([cur.reshape(pairs, 2 * SPAN, SPAN), pv.reshape(pairs, 2 * SPAN, SPAN)], axis=1)

    return jnp.stack([fwd(absent), fwd(prev)]), jnp.stack([bwd(absent), bwd(prev)])


ATT_GROUP = 4


def _att_group(s, dil):
    nb = s // (dil * SPAN)
    grp = min(ATT_GROUP, nb)
    assert nb % grp == 0
    return nb, grp


def _att_specs(s, d, dil, kinds):
    nb, grp = _att_group(s, dil)

    def spec(part, which):
        if which == "group":
            return pl.BlockSpec((grp * SPAN, d), lambda b: (b, part))
        if which == "prev":
            return pl.BlockSpec((SPAN, d), lambda b: (jnp.where((grp * b) % nb == 0, grp * b, grp * b - 1), part))
        return pl.BlockSpec((SPAN, d), lambda b: (jnp.where((grp * b + grp - 1) % nb == nb - 1, grp * b + grp - 1, grp * b + grp), part))

    return [spec(part, which) for part, which in kinds]


def _head_col(v, head):
    return v[:, head:head + 1]


def _expand_heads(w, j):
    shape = (w.shape[0], LANES)
    return jnp.where(_left_half(shape), jnp.broadcast_to(_head_col(w, 2 * j), shape), jnp.broadcast_to(_head_col(w, 2 * j + 1), shape))


def _attn_fwd(qkv, slopes, dil, name):
    s, d3 = qkv.shape
    d = d3 // 3
    nb, grp = _att_group(s, dil)
    table, _ = _att_bias(slopes, dil)

    def body(q_ref, k_ref, kp_ref, v_ref, vp_ref, tb_ref, o_ref, l_ref):
        b = pl.program_id(0)
        left = _left_half((SPAN, LANES))
        lane = lax.broadcasted_iota(jnp.int32, (SPAN, LANES), 1)
        for sub in range(grp):
            rows, before = slice(sub * SPAN, (sub + 1) * SPAN), slice((sub - 1) * SPAN, sub * SPAN)
            variant = jnp.where((grp * b) % nb == 0, 0, 1) if sub == 0 else 1
            lses = jnp.zeros((SPAN, LANES), F32)
            for hp in range(d // LANES):
                cols = slice(hp * LANES, (hp + 1) * LANES)
                q = q_ref[rows, cols]
                q2 = jnp.concatenate([_head_mask(q, 0), _head_mask(q, 1)], axis=0) * ATT_SCALE
                k2 = jnp.concatenate([k_ref[rows, cols], kp_ref[:, cols] if sub == 0 else k_ref[before, cols]], axis=0)
                v2 = jnp.concatenate([v_ref[rows, cols], vp_ref[:, cols] if sub == 0 else v_ref[before, cols]], axis=0)
                sc = _dot_nt(q2, k2) + tb_ref[variant, hp]
                m = jnp.max(sc, axis=-1, keepdims=True)
                p = jnp.exp(sc - m)
                l = jnp.sum(p, axis=-1, keepdims=True)
                r = _dot_nn(p, v2) * (1.0 / l)
                lse = m + jnp.log(l)
                o_ref[rows, cols] = jnp.where(left, r[:SPAN], r[SPAN:])
                lses = jnp.where(lane == 2 * hp, lse[:SPAN], jnp.where(lane == 2 * hp + 1, lse[SPAN:], lses))
            l_ref[rows, :] = lses

    specs = _att_specs(s, d, dil, [(0, "group"), (1, "group"), (1, "prev"), (2, "group"), (2, "prev")])
    return pl.pallas_call(
        body,
        grid=(s // (grp * SPAN),),
        in_specs=specs + [pl.BlockSpec(table.shape, lambda b: (0, 0, 0, 0))],
        out_specs=[pl.BlockSpec((grp * SPAN, d), lambda b: (b, 0)), pl.BlockSpec((grp * SPAN, LANES), lambda b: (b, 0))],
        out_shape=[jax.ShapeDtypeStruct((s, d), F32), jax.ShapeDtypeStruct((s, LANES), F32)],
        name=name,
        compiler_params=_cparams(("parallel",)),
    )(qkv, qkv, qkv, qkv, qkv, table)


def _attn_bwd(qkv, do, lse, dd, slopes, dil, name):
    s, d3 = qkv.shape
    d = d3 // 3
    nb, grp = _att_group(s, dil)
    _, table = _att_bias(slopes, dil)

    def heads_stacked(cur, nxt):
        return jnp.concatenate([_head_mask(cur, 0), _head_mask(cur, 1), _head_mask(nxt, 0), _head_mask(nxt, 1)], axis=0)

    def cols_stacked(cur, nxt, hp):
        return jnp.concatenate([jnp.broadcast_to(_head_col(a, 2 * hp + h), (SPAN, LANES)) for a in (cur, nxt) for h in range(2)], axis=0)

    def body(k_ref, v_ref, q_ref, qn_ref, do_ref, don_ref, l_ref, ln_ref, dd_ref, ddn_ref, tb_ref, out_ref, carry):
        b = pl.program_id(0)

        @pl.when(b == 0)
        def _():
            carry[...] = jnp.zeros_like(carry)

        left = _left_half((SPAN, LANES))
        for sub in range(grp):
            rows, after = slice(sub * SPAN, (sub + 1) * SPAN), slice((sub + 1) * SPAN, (sub + 2) * SPAN)
            last = sub == grp - 1
            variant = jnp.where((grp * b + sub) % nb == nb - 1, 0, 1) if last else 1
            lse_c, dd_c = l_ref[rows, :], dd_ref[rows, :]
            lse_n, dd_n = (ln_ref[...], ddn_ref[...]) if last else (l_ref[after, :], dd_ref[after, :])
            for hp in range(d // LANES):
                cols = slice(hp * LANES, (hp + 1) * LANES)
                k, v = k_ref[rows, cols], v_ref[rows, cols]
                q4 = heads_stacked(q_ref[rows, cols], qn_ref[:, cols] if last else q_ref[after, cols])
                do4 = heads_stacked(do_ref[rows, cols], don_ref[:, cols] if last else do_ref[after, cols])
                sc = _dot_nt(q4 * ATT_SCALE, k) + tb_ref[variant, hp]
                p = jnp.exp(sc - cols_stacked(lse_c, lse_n, hp))
                ds = p * (_dot_nt(do4, v) - cols_stacked(dd_c, dd_n, hp))
                dq4 = _dot_nn(ds, k)
                dq_cur = jnp.where(left, dq4[:SPAN], dq4[SPAN:2 * SPAN]) + carry[:, cols]
                carry[:, cols] = jnp.where(left, dq4[2 * SPAN:3 * SPAN], dq4[3 * SPAN:])
                out_ref[rows, cols] = (dq_cur * ATT_SCALE).astype(out_ref.dtype)
                out_ref[rows, d + hp * LANES:d + (hp + 1) * LANES] = (_dot_tn(ds, q4) * ATT_SCALE).astype(out_ref.dtype)
                out_ref[rows, 2 * d + hp * LANES:2 * d + (hp + 1) * LANES] = _dot_tn(p, do4).astype(out_ref.dtype)

    qkv_specs = _att_specs(s, d, dil, [(1, "group"), (2, "group"), (0, "group"), (0, "next")])
    wide = _att_specs(s, d, dil, [(0, "group"), (0, "next")])
    heads = _att_specs(s, LANES, dil, [(0, "group"), (0, "next")])
    return pl.pallas_call(
        body,
        grid=(s // (grp * SPAN),),
        in_specs=qkv_specs + wide + heads + heads + [pl.BlockSpec(table.shape, lambda b: (0, 0, 0, 0))],
        out_specs=pl.BlockSpec((grp * SPAN, d3), lambda b: (b, 0)),
        out_shape=jax.ShapeDtypeStruct((s, d3), MXU_DTYPE),
        scratch_shapes=[pltpu.VMEM((SPAN, d), F32)],
        name=name,
        compiler_params=_cparams(("arbitrary",)),
    )(qkv, qkv, qkv, qkv, do, do, lse, lse, dd, dd, table)


def _mix_weights(l_refs):
    ls = [r[...] for r in l_refs]
    m = functools.reduce(jnp.maximum, ls)
    es = [jnp.exp(l - m) for l in ls]
    tot = functools.reduce(lambda a, c: a + c, es)
    return [e / tot for e in es]


def _combine_fwd(os_, ls_, name):
    s, d = ls_[0].shape[0], os_[0].shape[-1]
    n = len(os_)
    n_str = sum(o.ndim == 3 for o in os_)

    def body(*refs):
        o_refs, l_refs, out_ref, scrs = refs[:n], refs[n:2 * n], refs[2 * n], list(refs[2 * n + 1:])
        ws = _mix_weights(l_refs)
        os_v = [o if len(o.shape) == 2 else _streams_in(o, scrs.pop()) for o in o_refs]
        for j in range(d // LANES):
            cols = slice(j * LANES, (j + 1) * LANES)
            acc = _expand_heads(ws[0], j) * os_v[0][:, cols]
            for w, o in zip(ws[1:], os_v[1:]):
                acc = acc + _expand_heads(w, j) * o[:, cols]
            out_ref[:, cols] = acc

    return _rows(body, s, ROW_TILE, [("blk" if a.ndim == 2 else "str", a) for a in os_] + [("blk", a) for a in ls_],
                 [("blk", (s, d), F32)], name, scratch=[_stream_scratch(d)] * n_str)[0]


def _combine_bwd(do, o, ls_, dils, name):
    s, d = o.shape
    n = len(ls_)
    sel = (lax.broadcasted_iota(jnp.int32, (d, LANES), 0) // HEAD_DIM == lax.broadcasted_iota(jnp.int32, (d, LANES), 1)).astype(F32)

    def body(do_ref, o_ref, *rest):
        l_refs, sel_ref, outs = rest[:n], rest[n], rest[n + 1:n + 1 + 2 * n]
        ws = _mix_weights(l_refs)
        dov = do_ref[...]
        r = jnp.dot(dov * o_ref[...], sel_ref[...], precision=lax.Precision.HIGHEST, preferred_element_type=F32)
        for g in range(n):
            outs[2 * g + 1][...] = ws[g] * r
            parts = [_expand_heads(ws[g], j) * dov[:, j * LANES:(j + 1) * LANES] for j in range(d // LANES)]
            if dils[g] == 1:
                for j, part in enumerate(parts):
                    outs[2 * g][:, j * LANES:(j + 1) * LANES] = part.astype(outs[2 * g].dtype)
            else:
                _streams_out(jnp.concatenate(parts, axis=1), outs[2 * g], rest[-1])

    outs = []
    for dil in dils:
        outs += [("blk", (s, d), MXU_DTYPE) if dil == 1 else ("str", (dil, s // dil, d), MXU_DTYPE), ("blk", (s, LANES), F32)]
    res = _rows(body, s, ROW_TILE, [("blk", do), ("blk", o)] + [("blk", l) for l in ls_] + [("all", sel)], outs, name,
                scratch=[_stream_scratch(d)])
    return [(res[2 * g], res[2 * g + 1]) for g in range(n)]


def _ada_fwd(c_all, w, b, name):
    nsub, d, cs = w.shape

    def body(c_ref, w_ref, b_ref, o_ref):
        cv = c_ref[...]
        sc = cv * (1.0 / (1.0 + jnp.exp(-cv)))
        o_ref[...] = _dot_nn(sc, w_ref[...]) + b_ref[...]

    return pl.pallas_call(
        body,
        grid=(nsub,),
        in_specs=[pl.BlockSpec(c_all.shape, lambda i: (0, 0)), pl.BlockSpec((None, d, cs), lambda i: (i, 0, 0)),
                  pl.BlockSpec((None, 1, cs), lambda i: (i, 0, 0))],
        out_specs=pl.BlockSpec((None, N_DEV, cs), lambda i: (i, 0, 0)),
        out_shape=jax.ShapeDtypeStruct((nsub, N_DEV, cs), F32),
        name=name,
        compiler_params=_cparams(("parallel",)),
    )(c_all, w, b)


def _ada_bwd(c_all_t, dm, name):
    d, nb = c_all_t.shape
    nsub, _, cs = dm.shape

    def body(c_ref, dm_ref, o_ref):
        cv = c_ref[...]
        sc = cv * (1.0 / (1.0 + jnp.exp(-cv)))
        acc = sc[:, 0:1] * dm_ref[0:1, :]
        for bi in range(1, nb):
            acc = acc + sc[:, bi:bi + 1] * dm_ref[bi:bi + 1, :]
        o_ref[...] = acc

    return pl.pallas_call(
        body,
        grid=(nsub,),
        in_specs=[pl.BlockSpec(c_all_t.shape, lambda i: (0, 0)), pl.BlockSpec((None, nb, cs), lambda i: (i, 0, 0))],
        out_specs=pl.BlockSpec((None, d, cs), lambda i: (i, 0, 0)),
        out_shape=jax.ShapeDtypeStruct((nsub, d, cs), F32),
        name=name,
        compiler_params=_cparams(("parallel",)),
    )(c_all_t, dm)


def _row_tile(r, row_elems, block_elems=256 * 1024):
    t = 2 * SUBLANES
    if r % t:
        return r
    while t * 2 * row_elems <= block_elems and r % (t * 2) == 0:
        t *= 2
    return t


def _adamw(w, g, m, v, name):
    shape = w.shape
    c = shape[-1]
    r = w.size // c
    tr = _row_tile(r, c, 512 * 1024)
    w2, g2, m2, v2 = [a.reshape(r, c) for a in (w, g, m, v)]
    bc1 = 1.0 - ADAM_B1 ** ADAM_STEP
    bc2 = 1.0 - ADAM_B2 ** ADAM_STEP

    def body(w_ref, g_ref, m_ref, v_ref, d_ref, nm_ref, nv_ref):
        gv = g_ref[...]
        nm = ADAM_B1 * m_ref[...] + (1.0 - ADAM_B1) * gv
        nv = ADAM_B2 * v_ref[...] + (1.0 - ADAM_B2) * (gv * gv)
        d_ref[...] = -ADAM_LR * ((nm / bc1) / (jnp.sqrt(nv / bc2) + ADAM_EPS) + ADAM_WD * w_ref[...])
        nm_ref[...] = nm
        nv_ref[...] = nv

    res = _rows(body, r, tr, [("blk", a) for a in (w2, g2, m2, v2)], [("blk", (r, c), F32)] * 3, name)
    return [a.reshape(shape) for a in res]


def _sum_slots(buf, name):
    n, r, c = buf.shape
    tr = _row_tile(r, n * c, 2 * 1024 * 1024)

    def body(b_ref, o_ref):
        acc = b_ref[0].astype(F32)
        for k in range(1, n):
            acc = acc + b_ref[k].astype(F32)
        o_ref[...] = acc

    return pl.pallas_call(
        body,
        grid=(r // tr,),
        in_specs=[pl.BlockSpec((n, tr, c), lambda i: (0, i, 0))],
        out_specs=pl.BlockSpec((tr, c), lambda i: (i, 0)),
        out_shape=jax.ShapeDtypeStruct((r, c), F32),
        name=name,
        compiler_params=_cparams(("parallel",)),
    )(buf)


def _me():
    return lax.axis_index("x"), lax.axis_index("y"), lax.axis_index("c")


def _all_gather_small(blk, name, after=()):
    m_per, n = blk.shape

    def body(x_ref, *rest):
        out_ref, send_sems, recv_sems, local_sem = rest[len(after):]
        x, y, c = _me()
        me, sibling = (x, y, c), (x, y, 1 - c)
        chips = [(1 - x, y), (x, 1 - y), (1 - x, 1 - y)]

        def rows(px, py, pc):
            return out_ref.at[pl.ds((4 * px + 2 * py + pc) * m_per, m_per), :]

        def copy(k, block, to, src=None):
            return pltpu.make_async_remote_copy(
                src_ref=rows(*block) if src is None else src, dst_ref=rows(*block),
                send_sem=send_sems.at[k], recv_sem=recv_sems.at[k], device_id=to, device_id_type=MESH)

        mine = pltpu.make_async_copy(x_ref, rows(*me), local_sem)
        mine.start()
        first = [copy(0, me, sibling, src=x_ref)]
        first += [copy(1 + j, me, (*chip, c), src=x_ref) for j, chip in enumerate(chips)]
        for cp in first:
            cp.start()
        passed = [copy(4 + j, (*chip, c), sibling) for j, chip in enumerate(chips)]
        for j, chip in enumerate(chips):
            copy(1 + j, (*chip, c), me).wait_recv()
            passed[j].start()
        copy(0, sibling, me).wait_recv()
        for j, chip in enumerate(chips):
            copy(4 + j, (*chip, 1 - c), me).wait_recv()
        for cp in first + passed:
            cp.wait_send()
        mine.wait()

    return pl.pallas_call(
        body,
        out_shape=jax.ShapeDtypeStruct((N_DEV * m_per, n), blk.dtype),
        in_specs=[pl.BlockSpec(memory_space=pltpu.VMEM)] + [pl.BlockSpec(memory_space=pl.ANY)] * len(after),
        out_specs=pl.BlockSpec(memory_space=pltpu.VMEM),
        scratch_shapes=[pltpu.SemaphoreType.DMA((7,)), pltpu.SemaphoreType.DMA((7,)), pltpu.SemaphoreType.DMA],
        name=name,
        compiler_params=pltpu.CompilerParams(vmem_limit_bytes=VMEM_LIMIT),
    )(blk, *after)


_HBM = pl.BlockSpec(memory_space=pltpu.HBM)
_SEM = pl.BlockSpec(memory_space=pltpu.SEMAPHORE)
_EFFECT = pltpu.SideEffectType.DATAFLOW_SIDE_EFFECTING


def _other_chips(x, y):
    return [(1 - x, y), (x, 1 - y), (1 - x, 1 - y)]


def _gather_copy(w, j, src_ref, land_ref, send_sems, recv_sems, halved=False):
    x, y, c = _me()
    if halved:
        half = src_ref.shape[0] // 2
        src_ref = src_ref.at[pl.ds(c * half, half), :]
    return pltpu.make_async_remote_copy(
        src_ref=src_ref, dst_ref=land_ref.at[2 * x + y], send_sem=send_sems.at[3 * w + j], recv_sem=recv_sems.at[3 * w + j],
        device_id=(*_other_chips(x, y)[j], c), device_id_type=MESH)


def _gather_start(shards, halved, after, name):
    n = len(shards)
    lands = [lax.empty((N_CHIPS, s.shape[0] // 2 if w in halved else s.shape[0], s.shape[1]), s.dtype) for w, s in enumerate(shards)]

    def body(*refs):
        in_refs, land_refs = refs[:n], refs[n:2 * n]
        send_sems, recv_sems = refs[2 * n + 1], refs[2 * n + 2]
        token = refs[-1]
        for w in range(n):
            for j in range(3):
                _gather_copy(w, j, in_refs[w], land_refs[w], send_sems, recv_sems, w in halved).start()
        token[...] = jnp.zeros_like(token)

    res = pl.pallas_call(
        body,
        out_shape=(pltpu.SemaphoreType.DMA((3 * n,)), pltpu.SemaphoreType.DMA((3 * n,)),
                   *[pltpu.HBM(s.shape, s.dtype) for s in shards], *[pltpu.HBM(l.shape, l.dtype) for l in lands],
                   jax.ShapeDtypeStruct((SUBLANES, LANES), F32)),
        in_specs=[_HBM] * (2 * n) + [pl.BlockSpec(memory_space=pl.ANY)],
        out_specs=(_SEM, _SEM, *[_HBM] * (2 * n), pl.BlockSpec(memory_space=pltpu.VMEM)),
        input_output_aliases={i: 2 + i for i in range(2 * n)},
        name=name,
        compiler_params=pltpu.CompilerParams(has_side_effects=_EFFECT),
    )(*[pltpu.with_memory_space_constraint(a, pltpu.HBM) for a in list(shards) + lands], after)
    return res[0], res[1], res[2:2 + n], res[2 + n:2 + 2 * n], res[-1]


def _gather_wait(w, shard, land, send_sems, recv_sems, after, name, halved=False):
    def body(s_ref, land_ref, send_sems, recv_sems, after_ref, s_out, land_out, stage):
        x, y, _ = _me()
        if not halved:
            pltpu.sync_copy(s_ref, stage)
            pltpu.sync_copy(stage, land_out.at[2 * x + y])
        for j in range(3):
            cp = _gather_copy(w, j, s_ref, land_ref, send_sems, recv_sems, halved)
            cp.wait_send()
            cp.wait_recv()

    return pl.pallas_call(
        body,
        out_shape=(pltpu.HBM(shard.shape, shard.dtype), pltpu.HBM(land.shape, land.dtype)),
        in_specs=(_HBM, _HBM, _SEM, _SEM, pl.BlockSpec(memory_space=pl.ANY)),
        out_specs=(_HBM, _HBM),
        input_output_aliases={0: 0, 1: 1},
        scratch_shapes=[pltpu.VMEM((SUBLANES, LANES) if halved else shard.shape, shard.dtype)],
        name=name,
        compiler_params=pltpu.CompilerParams(has_side_effects=_EFFECT, vmem_limit_bytes=VMEM_LIMIT),
    )(shard, land, send_sems, recv_sems, after)


def _assemble_halves(shard, land, name):
    half = land.shape[1]

    def body(s_ref, land_ref, out_ref, send_sems, recv_sems, local_sems):
        x, y, c = _me()
        own = pltpu.make_async_copy(s_ref, out_ref.at[2 * x + y], local_sems.at[3])
        own.start()
        cps = []
        for j, (ox, oy) in enumerate(_other_chips(x, y)):
            qj = 2 * ox + oy
            mine = out_ref.at[qj, pl.ds(c * half, half), :]
            lc = pltpu.make_async_copy(land_ref.at[qj], mine, local_sems.at[j])
            lc.start()
            rc = pltpu.make_async_remote_copy(
                src_ref=land_ref.at[qj], dst_ref=mine, send_sem=send_sems.at[j], recv_sem=recv_sems.at[j],
                device_id=(x, y, 1 - c), device_id_type=MESH)
            rc.start()
            cps.append((lc, rc))
        for lc, rc in cps:
            rc.wait_recv()
        for lc, rc in cps:
            rc.wait_send()
            lc.wait()
        own.wait()

    vmem = pl.BlockSpec(memory_space=pltpu.VMEM)
    return pl.pallas_call(
        body,
        out_shape=jax.ShapeDtypeStruct((N_CHIPS,) + shard.shape, shard.dtype),
        in_specs=[vmem, vmem],
        out_specs=vmem,
        scratch_shapes=[pltpu.SemaphoreType.DMA((3,)), pltpu.SemaphoreType.DMA((3,)), pltpu.SemaphoreType.DMA((4,))],
        name=name,
        compiler_params=pltpu.CompilerParams(vmem_limit_bytes=VMEM_LIMIT),
    )(shard, land)


def _piece_shape(shape, kind):
    k, nn = shape
    if kind == "all":
        return (k, nn)
    return (k // 2, nn // N_CHIPS) if kind == "col" else (k // N_CHIPS // 2, nn)


def _piece_of(g_ref, kind, tq, tc):
    pr, pc = _piece_shape(g_ref.shape, kind)
    if kind == "all":
        return g_ref
    if kind == "col":
        return g_ref.at[pl.ds(tc * pr, pr), pl.ds(tq * pc, pc)]
    return g_ref.at[pl.ds((2 * tq + tc) * pr, pr), :]


def _scatter_copy(w, r, kind, g_ref, land_ref, send_sems, recv_sems):
    x, y, c = _me()
    tx, ty, tc = (x + ((r >> 2) & 1)) % 2, (y + ((r >> 1) & 1)) % 2, (c + (r & 1)) % 2
    return pltpu.make_async_remote_copy(
        src_ref=_piece_of(g_ref, kind, 2 * tx + ty, tc), dst_ref=land_ref.at[4 * x + 2 * y + c],
        send_sem=send_sems.at[N_DEV * w + r], recv_sem=recv_sems.at[N_DEV * w + r], device_id=(tx, ty, tc), device_id_type=MESH)


def _scatter_start(gs, kinds, name):
    n = len(gs)
    pieces = [_piece_shape(g.shape, kind) for g, kind in zip(gs, kinds)]
    lands = [lax.empty((N_DEV,) + p, g.dtype) for p, g in zip(pieces, gs)]

    def body(*refs):
        g_refs, land_refs, send_sems, recv_sems = refs[:n], refs[n:2 * n], refs[2 * n], refs[2 * n + 1]
        land_outs, stages = refs[3 * n + 2:4 * n + 2], refs[4 * n + 2:]
        x, y, c = _me()
        for w in range(n):
            for r in range(1, N_DEV):
                _scatter_copy(w, r, kinds[w], g_refs[w], land_refs[w], send_sems, recv_sems).start()
        for w in range(n):
            pltpu.sync_copy(_piece_of(g_refs[w], kinds[w], 2 * x + y, c), stages[w])
            pltpu.sync_copy(stages[w], land_outs[w].at[4 * x + 2 * y + c])

    arrays = list(gs) + lands
    res = pl.pallas_call(
        body,
        out_shape=(pltpu.SemaphoreType.DMA((N_DEV * n,)), pltpu.SemaphoreType.DMA((N_DEV * n,)),
                   *[pltpu.HBM(a.shape, a.dtype) for a in arrays]),
        in_specs=[_HBM] * (2 * n),
        out_specs=(_SEM, _SEM, *[_HBM] * (2 * n)),
        input_output_aliases={i: 2 + i for i in range(2 * n)},
        scratch_shapes=[pltpu.VMEM(p, g.dtype) for p, g in zip(pieces, gs)],
        name=name,
        compiler_params=pltpu.CompilerParams(has_side_effects=_EFFECT, vmem_limit_bytes=VMEM_LIMIT),
    )(*[pltpu.with_memory_space_constraint(a, pltpu.HBM) for a in arrays])
    return res[0], res[1], res[2:2 + n], res[2 + n:]


def _scatter_wait(send_sems, recv_sems, gs, lands, kinds, after, name):
    n = len(gs)

    def body(*refs):
        g_refs, land_refs, send_sems, recv_sems = refs[:n], refs[n:2 * n], refs[2 * n], refs[2 * n + 1]
        for w in range(n):
            for r in range(1, N_DEV):
                cp = _scatter_copy(w, r, kinds[w], g_refs[w], land_refs[w], send_sems, recv_sems)
                cp.wait_send()
                cp.wait_recv()

    arrays = list(gs) + list(lands)
    return pl.pallas_call(
        body,
        out_shape=tuple(pltpu.HBM(a.shape, a.dtype) for a in arrays),
        in_specs=(*[_HBM] * (2 * n), _SEM, _SEM, pl.BlockSpec(memory_space=pl.ANY)),
        out_specs=tuple([_HBM] * (2 * n)),
        input_output_aliases={i: i for i in range(2 * n)},
        name=name,
        compiler_params=pltpu.CompilerParams(has_side_effects=_EFFECT),
    )(*arrays, send_sems, recv_sems, after)[n:]


def _sum_swap(bufs, name):
    n = len(bufs)

    def body(*refs):
        in_refs, out_refs = refs[:n], refs[n:2 * n]
        send_sems, recv_sems = refs[2 * n:]
        x, y, c = _me()
        cps = []
        for w in range(n):
            slots, r, _ = bufs[w].shape
            mine = out_refs[w].at[c]
            for r0 in range(0, r, min(r, ROW_TILE)):
                rows = slice(r0, r0 + min(r, ROW_TILE))
                acc = in_refs[w][0, rows, :].astype(F32)
                for k in range(1, slots):
                    acc = acc + in_refs[w][k, rows, :].astype(F32)
                mine[rows, :] = acc
            rc = pltpu.make_async_remote_copy(
                src_ref=mine, dst_ref=mine, send_sem=send_sems.at[w], recv_sem=recv_sems.at[w],
                device_id=(x, y, 1 - c), device_id_type=MESH)
            rc.start()
            cps.append(rc)
        for rc in cps:
            rc.wait_recv()
        for rc in cps:
            rc.wait_send()

    vmem = pl.BlockSpec(memory_space=pltpu.VMEM)
    return pl.pallas_call(
        body,
        out_shape=[jax.ShapeDtypeStruct((2,) + b.shape[1:], F32) for b in bufs],
        in_specs=[vmem] * n,
        out_specs=[vmem] * n,
        scratch_shapes=[pltpu.SemaphoreType.DMA((n,)), pltpu.SemaphoreType.DMA((n,))],
        name=name,
        compiler_params=pltpu.CompilerParams(vmem_limit_bytes=VMEM_LIMIT),
    )(*bufs)


def _to_streams(a, dil):
    if dil == 1:
        return a
    s, c = a.shape
    return a.reshape(s // dil, dil, c).transpose(1, 0, 2).reshape(s, c)


def _from_streams(a, dil):
    if dil == 1:
        return a
    s, c = a.shape
    return a.reshape(dil, s // dil, c).transpose(1, 0, 2).reshape(s, c)


def _mm_tiles(s):
    return min(s, 2048)


def _local_step(x0, target, mvec, ln_g, ln_b, small, fetch, emit, start):
    s, d = x0.shape
    tm = _mm_tiles(s)
    row = lambda v: v.reshape(1, -1)
    shift = [row(mvec[i, :d]) for i in range(4)]
    scale = [row(mvec[i, d:2 * d]) for i in range(4)]
    gate = [row(1.0 + mvec[i, 2 * d:]) for i in range(4)]
    lg = [row(ln_g[i]) for i in range(4)]
    lb = [row(ln_b[i]) for i in range(4)]
    mm = functools.partial(_mm, tm=tm)
    mm_w = functools.partial(_mm, tm=1024, tk=min(s, 2048), mode="tn")

    def resid_ln_epilogue(sub):
        def epi(y, xv, gate_v, g_v, b_v, sc_v, sh_v):
            xhat, _ = _ln_stats(ALPHA * xv + gate_v * y)
            xn = xhat * g_v + b_v
            return [y, xn, xn * (1.0 + sc_v) + sh_v]

        rows = [gate[sub], lg[sub], lb[sub], scale[sub + 1], shift[sub + 1]]
        return dict(outs=[F32, F32, MXU_DTYPE], epi=epi, extras=[("full", xs[sub])] + [("row", r) for r in rows])

    xs, ys, big = [x0], [], {}
    h0 = _mod(x0, scale[0], shift[0], start, "mod0")
    big["a_w_in"] = fetch("a_w_in", h0)
    uvpre = mm(h0, big["a_w_in"], mode="nn", name="a_in", outs=[F32], tn=512, tk=1024,
               epi=lambda r, bias: [r + bias], extras=[("row", small["a_b_in"])])
    gated = _spatial_fwd(uvpre, small["a_vn_g"], small["a_vn_b"], small["wc"], small["bias_full"], "a_spatial")
    big["a_w_out"] = fetch("a_w_out", gated)
    y0, x1, h1 = mm(gated, big["a_w_out"], mode="nn", name="a_out", tm=min(s, 1024), tn=d, tk=1024, **resid_ln_epilogue(0))
    ys.append(y0)
    xs.append(x1)
    relu2 = lambda r: [jnp.square(jnp.maximum(r, 0.0))]
    big["up0"] = fetch("up0", h1)
    r0 = mm(h1, big["up0"], mode="nn", name="up0", outs=[MXU_DTYPE], tn=1024, tk=1024, epi=relu2)
    big["down0"] = fetch("down0", r0)
    ys.append(mm(r0, big["down0"], mode="nn", name="down0", outs=[F32], tm=min(s, 1024), tn=1024, tk=2048))
    dils = [dil for _, dil in B_PATTERNS]
    x2, h2, *h2_streams = _resid_ln(xs[1], ys[1], gate[1], lg[1], lb[1], (scale[2], shift[2]), "ln1", [dil for dil in dils if dil > 1])
    h2_streams = [h2] + [a.reshape(s, d) for a in h2_streams]
    xs.append(x2)
    hg, qkvs, o_g, l_g, l_streams = [], [], [], [], []
    big["b_w_qkv"] = fetch("b_w_qkv", h2)
    for g, (_, dil) in enumerate(B_PATTERNS):
        hp = h2_streams[g]
        qkv = mm(hp, big["b_w_qkv"], mode="nn", name=f"qkv{g}", outs=[MXU_DTYPE], tn=768, tk=1024, b_col0=g * 3 * d, n_out=3 * d)
        og, lgv = _attn_fwd(qkv, small["slopes"], dil, f"attn_fwd{g}")
        hg.append(hp)
        qkvs.append(qkv)
        o_g.append(og if dil == 1 else og.reshape(dil, s // dil, d))
        l_g.append(_from_streams(lgv, dil))
        l_streams.append(lgv)
    o_mix = _combine_fwd(o_g, l_g, "combine")
    big["b_w_out"] = fetch("b_w_out", o_mix)
    y2, x3, h3 = mm(o_mix, big["b_w_out"], mode="nn", name="b_out", tm=min(s, 1024), tn=d, tk=1024, **resid_ln_epilogue(2))
    ys.append(y2)
    xs.append(x3)
    big["up1"] = fetch("up1", h3)
    r1 = mm(h3, big["up1"], mode="nn", name="up1", outs=[MXU_DTYPE], tn=1024, tk=1024, epi=relu2)
    big["down1"] = fetch("down1", r1)
    ys.append(mm(r1, big["down1"], mode="nn", name="down1", outs=[F32], tm=min(s, 1024), tn=1024, tk=2048))

    gb, red_ln, red_mod = {}, [None] * 4, [None] * 4

    def mlp_bwd(i, h, r, dyy):
        gb[f"down{i}"] = mm_w(r, dyy, name=f"g_down{i}", outs=[MXU_DTYPE], tn=1024)
        da = mm(dyy, big[f"down{i}"], mode="nt", name=f"d_down{i}", outs=[MXU_DTYPE], tn=1024, tk=1024,
                after=emit(f"down{i}", gb[f"down{i}"]),
                epi=lambda acc, rv: [acc * (2.0 * jnp.sqrt(rv.astype(F32)))], extras=[("full", r)])
        gb[f"up{i}"] = mm_w(h, da, name=f"g_up{i}", outs=[MXU_DTYPE], tn=1024)
        return [mm(da, big[f"up{i}"], mode="nt", name=f"d_up{i}", outs=[F32], tn=1024, tk=1024, after=emit(f"up{i}", gb[f"up{i}"]))]

    def join(sub, dxr, dhs, after=None):
        res = _mod_ln_bwd(dxr, dhs, xs[sub], scale[sub], xs[sub - 1], ys[sub - 1], gate[sub - 1], lg[sub - 1],
                          f"mod_ln_bwd{sub}", after=after)
        red_mod[sub], red_ln[sub - 1] = res[2], res[3]
        return res[0], res[1]

    loss, dxr, dyy, red_ln[3] = _last_ln_loss_bwd(xs[3], ys[3], gate[3], lg[3], lb[3], target, "ln3_loss_bwd")
    dxr, dyy = join(3, dxr, mlp_bwd(1, h3, r1, dyy))
    gb["b_w_out"] = mm_w(o_mix, dyy, name="g_b_out", outs=[MXU_DTYPE], tn=1024, tk=1024)
    do = mm(dyy, big["b_w_out"], mode="nt", name="d_b_out", outs=[F32], tn=1024, tk=1024, after=emit("b_w_out", gb["b_w_out"]))
    parts = _combine_bwd(do, o_mix, l_g, dils, "combine_bwd")
    dhs, gq = [], None
    for g, (_, dil) in enumerate(B_PATTERNS):
        do_g, dd_g = parts[g][0].reshape(s, d), _to_streams(parts[g][1], dil)
        dqkv = _attn_bwd(qkvs[g], do_g, l_streams[g], dd_g, small["slopes"], dil, f"attn_bwd{g}")
        gq = mm_w(hg[g], dqkv, name=f"g_qkv{g}", outs=[MXU_DTYPE], tn=1024, out_col0=g * 3 * d, out_cols=len(B_PATTERNS) * 3 * d, into=gq)
        dh = mm(dqkv, big["b_w_qkv"], mode="nt", name=f"d_qkv{g}", outs=[F32], tn=1024, tk=768, b_col0=g * 3 * d)
        dhs.append(dh if dil == 1 else dh.reshape(dil, s // dil, d))
    gb["b_w_qkv"] = gq
    dxr, dyy = join(2, dxr, dhs, after=emit("b_w_qkv", gb["b_w_qkv"]))
    dxr, dyy = join(1, dxr, mlp_bwd(0, h1, r0, dyy))
    gb["a_w_out"] = mm_w(gated, dyy, name="g_a_out", outs=[MXU_DTYPE], tn=1024)
    dgated = mm(dyy, big["a_w_out"], mode="nt", name="d_a_out", outs=[F32], tn=1024, tk=1024, after=emit("a_w_out", gb["a_w_out"]))
    duv, dws, dbias, dbin, dvg, dvb = _spatial_bwd(uvpre, dgated, small["a_vn_g"], small["a_vn_b"], small["wc"],
                                                   small["wct"], small["bias_full"], "a_spatial_bwd")
    tril = jnp.tril(jnp.ones((CHUNK, CHUNK), bool))
    dws = jnp.where(tril, dws, 0.0).reshape(-1, LANES)
    gb["a_w_in"] = mm_w(h0, duv, name="g_a_in", outs=[MXU_DTYPE], tn=1024, after=emit("a_w_s", dws.astype(MXU_DTYPE)))
    dh = mm(duv, big["a_w_in"], mode="nt", name="d_a_in", outs=[F32], tn=1024, tk=512, after=emit("a_w_in", gb["a_w_in"]))
    dx, red_mod[0] = _mod_bwd(dxr, [dh], xs[0], scale[0], "mod_bwd0")
    dm = [jnp.concatenate([red_mod[i][0], red_mod[i][1], red_ln[i][2]]) for i in range(4)]
    dlg, dlb = [red_ln[i][0] for i in range(4)], [red_ln[i][1] for i in range(4)]

    gsmall = {
        "a_b_in": dbin.reshape(-1), "a_vn_g": dvg.reshape(-1), "a_vn_b": dvb.reshape(-1),
        "a_w_s": dws.reshape(-1),
        "a_b_s": dbias.reshape(CHUNK, A_GROUPS, d // A_GROUPS).sum(-1).T.reshape(-1),
    }
    return loss, dx, gb, jnp.stack(dm), jnp.stack(dlg), jnp.stack(dlb), gsmall


BIG = ("a_w_in", "a_w_out", "up0", "down0", "b_w_qkv", "b_w_out", "up1", "down1")
BIG_KIND = {"a_w_in": "col", "a_w_out": "row", "b_w_qkv": "col", "b_w_out": "row",
            "up0": "col", "up1": "col", "down0": "row", "down1": "row", "a_w_s": "all"}
HALVED = ("a_w_in", "a_w_out", "down0", "b_w_qkv")
SCATTER_GROUPS = (("down1", "up1"), ("b_w_out", "b_w_qkv"), ("down0", "up0"), ("a_w_out", "a_w_in"), ("a_w_s",))
SMALL = ("a_b_in", "a_vn_g", "a_vn_b", "a_b_s")


def kernel(x, c, ada_w, ada_b, ln_g, ln_b, a_w_in, a_b_in, a_vn_g, a_vn_b, a_w_s, a_b_s, a_w_out, b_w_qkv, b_w_out, mlp_w_up, mlp_w_down, loss_target, m_ada_w, m_ada_b, m_ln_g, m_ln_b, m_a_w_in, m_a_b_in, m_a_vn_g, m_a_vn_b, m_a_w_s, m_a_b_s, m_a_w_out, m_b_w_qkv, m_b_w_out, m_mlp_w_up, m_mlp_w_down, v_ada_w, v_ada_b, v_ln_g, v_ln_b, v_a_w_in, v_a_b_in, v_a_vn_g, v_a_vn_b, v_a_w_s, v_a_b_s, v_a_w_out, v_b_w_qkv, v_b_w_out, v_mlp_w_up, v_mlp_w_down):
    s, d = x.shape[1], x.shape[2]
    xi, yi, ci = _me()
    q = 2 * xi + yi
    dev = 2 * q + ci
    nsub = 2 * DEPTH
    cs = ada_w.shape[-1]
    ls = ln_g.shape[-1]

    shards = {
        "a_w_in": a_w_in[0], "a_w_out": a_w_out[0], "b_w_qkv": b_w_qkv[0], "b_w_out": b_w_out[0],
        "up0": mlp_w_up[0], "up1": mlp_w_up[1], "down0": mlp_w_down[0], "down1": mlp_w_down[1],
    }
    cast = [shards[k].astype(MXU_DTYPE) for k in BIG]

    pack = jnp.concatenate([c.reshape(-1), ln_g.reshape(-1), ln_b.reshape(-1)]).reshape(-1, LANES)
    got = _all_gather_small(pack, "gather_small", after=cast).reshape(N_DEV, -1)
    c_all = got[:, :d]
    per_chip = got[0::2]
    ln_g_full = per_chip[:, d:d + nsub * ls].reshape(N_CHIPS, nsub, ls).transpose(1, 0, 2).reshape(nsub, d)
    ln_b_full = per_chip[:, d + nsub * ls:].reshape(N_CHIPS, nsub, ls).transpose(1, 0, 2).reshape(nsub, d)
    m_part = _ada_fwd(c_all, ada_w.reshape(nsub, d, cs), ada_b.reshape(nsub, 1, cs), "ada_fwd")
    m_all = _all_gather_small(m_part.reshape(-1, LANES), "gather_mod").reshape(N_DEV, nsub, N_DEV, cs)
    m_mine = lax.dynamic_index_in_dim(m_all[0::2], dev, axis=2, keepdims=False)
    mvec = m_mine.transpose(1, 0, 2).reshape(nsub, 3 * d)

    halved = {BIG.index(k) for k in HALVED}
    send_sems, recv_sems, shard_thru, lands, token = _gather_start(cast, halved, mvec, "gather_start")

    def fetch(k, after):
        w = BIG.index(k)
        shard, gw = _gather_wait(w, shard_thru[w], lands[w], send_sems, recv_sems, after, f"gather_wait_{k}", w in halved)
        if w in halved:
            gw = _assemble_halves(shard, gw, f"assemble_{k}")
        return gw if BIG_KIND[k] == "col" else gw.reshape(1, -1, gw.shape[-1])

    scattering, pending = {}, {}

    def emit(k, g):
        pending[k] = g
        group = next(gr for gr in SCATTER_GROUPS if k in gr)
        if k != group[-1]:
            return None
        scattering[group] = _scatter_start([pending[m] for m in group], [BIG_KIND[m] for m in group], f"scatter_start_{k}")
        return scattering[group][2][0]

    tril = jnp.tril(jnp.ones((CHUNK, CHUNK), bool))
    wc = jnp.where(tril, a_w_s[0], 0.0).astype(MXU_DTYPE)
    heads = jnp.arange(1, B_HEADS + 1, dtype=F32)
    small = {
        "a_b_in": a_b_in, "a_vn_g": a_vn_g, "a_vn_b": a_vn_b,
        "wc": wc, "wct": wc.transpose(0, 2, 1),
        "bias_full": jnp.repeat(a_b_s[0].T, d // A_GROUPS, axis=1),
        "slopes": jnp.exp2(-8.0 * heads / B_HEADS),
    }

    loss_part, grad_x, gb, dm, dlg, dlb, gsmall = _local_step(x[0], loss_target[0], mvec, ln_g_full, ln_b_full, small, fetch, emit, token)

    weights = dict(ada_w=ada_w, ada_b=ada_b, ln_g=ln_g, ln_b=ln_b, a_w_in=a_w_in, a_b_in=a_b_in, a_vn_g=a_vn_g, a_vn_b=a_vn_b,
                   a_w_s=a_w_s, a_b_s=a_b_s, a_w_out=a_w_out, b_w_qkv=b_w_qkv, b_w_out=b_w_out, mlp_w_up=mlp_w_up, mlp_w_down=mlp_w_down)
    ms = dict(ada_w=m_ada_w, ada_b=m_ada_b, ln_g=m_ln_g, ln_b=m_ln_b, a_w_in=m_a_w_in, a_b_in=m_a_b_in, a_vn_g=m_a_vn_g, a_vn_b=m_a_vn_b,
              a_w_s=m_a_w_s, a_b_s=m_a_b_s, a_w_out=m_a_w_out, b_w_qkv=m_b_w_qkv, b_w_out=m_b_w_out, mlp_w_up=m_mlp_w_up, mlp_w_down=m_mlp_w_down)
    vs = dict(ada_w=v_ada_w, ada_b=v_ada_b, ln_g=v_ln_g, ln_b=v_ln_b, a_w_in=v_a_w_in, a_b_in=v_a_b_in, a_vn_g=v_a_vn_g, a_vn_b=v_a_vn_b,
              a_w_s=v_a_w_s, a_b_s=v_a_b_s, a_w_out=v_a_w_out, b_w_qkv=v_b_w_qkv, b_w_out=v_b_w_out, mlp_w_up=v_mlp_w_up, mlp_w_down=v_mlp_w_down)
    grads, updates = {}, {}

    def update(k):
        updates[k] = _adamw(weights[k], grads[k], ms[k], vs[k], f"adamw_{k}")
        return updates[k][0]

    gfull = {}

    def big_group(group, after):
        bufs = []
        for pair in (group[:2], group[2:]):
            bufs += _scatter_wait(*scattering[pair], [BIG_KIND[m] for m in pair], after, f"scatter_wait_{pair[-1]}")
        parts = [[i] for i, k in enumerate(group) if k == "b_w_qkv"] + [[i for i, k in enumerate(group) if k != "b_w_qkv"]]
        for part in parts:
            fulls = _sum_swap([bufs[i] for i in part], f"sum_swap_{group[part[0]]}")
            gfull.update({group[i]: f.reshape(-1, f.shape[-1]) for i, f in zip(part, fulls)})

    big_group(SCATTER_GROUPS[0] + SCATTER_GROUPS[1], grad_x)
    grads["b_w_qkv"], grads["b_w_out"] = gfull["b_w_qkv"][None], gfull["b_w_out"][None]
    update("b_w_out")
    done = update("b_w_qkv")

    pack_b = jnp.concatenate([dm.reshape(-1), dlg.reshape(-1), dlb.reshape(-1)] + [gsmall[k] for k in SMALL] + [loss_part.reshape(1)])
    n_small = pack_b.shape[0]
    pack_b = jnp.pad(pack_b, (0, -n_small % (256 * LANES)))
    got_b = _all_gather_small(pack_b.reshape(-1, LANES), "gather_small_grads", after=[done]).reshape(N_DEV, -1, LANES)
    tot = _sum_slots(got_b, "sum_small").reshape(-1)
    o = 0
    dm_tot = tot[o:o + nsub * 3 * d].reshape(nsub, 3 * d); o += nsub * 3 * d
    dlg_tot = tot[o:o + nsub * d].reshape(nsub, d); o += nsub * d
    dlb_tot = tot[o:o + nsub * d].reshape(nsub, d); o += nsub * d
    g_small = {}
    for k, ref in zip(SMALL, (a_b_in, a_vn_g, a_vn_b, a_b_s)):
        g_small[k] = tot[o:o + ref.size].reshape(ref.shape); o += ref.size
    loss = tot[o]
    assert o + 1 == n_small
    aws = _scatter_wait(*scattering[("a_w_s",)], ["all"], tot, "scatter_wait_a_w_s")[0]
    g_small["a_w_s"] = _sum_slots(aws, "sum_a_w_s").reshape(a_w_s.shape)
    dm_all = got_b.reshape(N_DEV, -1)[:, :nsub * 3 * d].reshape(N_DEV, nsub, 3 * d)
    dm_cols = lax.dynamic_slice_in_dim(dm_all, q * cs, cs, axis=2).transpose(1, 0, 2)
    grads.update({
        "ada_w": _ada_bwd(c_all.T, dm_cols, "ada_bwd").reshape(ada_w.shape),
        "ada_b": lax.dynamic_slice_in_dim(dm_tot, q * cs, cs, axis=1).reshape(ada_b.shape),
        "ln_g": lax.dynamic_slice_in_dim(dlg_tot, q * ls, ls, axis=1).reshape(ln_g.shape),
        "ln_b": lax.dynamic_slice_in_dim(dlb_tot, q * ls, ls, axis=1).reshape(ln_b.shape),
        **g_small,
    })
    for k in ("ada_b", "ln_g", "ln_b", "a_w_s") + SMALL:
        update(k)
    done = update("ada_w")

    big_group(SCATTER_GROUPS[2] + SCATTER_GROUPS[3], done)
    grads.update({
        "a_w_in": gfull["a_w_in"][None], "a_w_out": gfull["a_w_out"][None],
        "mlp_w_up": jnp.stack([gfull["up0"], gfull["up1"]]), "mlp_w_down": jnp.stack([gfull["down0"], gfull["down1"]]),
    })
    for k in ("a_w_in", "a_w_out", "mlp_w_up", "mlp_w_down"):
        update(k)
    names = list(weights)
    return (loss, grad_x[None], *[grads[k] for k in names], *[updates[k][0] for k in names],
            *[updates[k][1] for k in names], *[updates[k][2] for k in names])
```

```python
import functools
import math

import jax
import jax.numpy as jnp
from jax import lax
from jax.experimental import pallas as pl
from jax.experimental.pallas import tpu as pltpu

F32 = jnp.float32
MXU_DTYPE = jnp.bfloat16

DEPTH = 2
CHUNK = 128
A_GROUPS = 16
B_HEADS = 16
HEAD_DIM = 64
B_PATTERNS = ((128, 1), (512, 4), (2048, 16))
SPAN = 128
ALPHA = (2 * DEPTH) ** 0.25
LN_EPS = 1e-5
NEG = -1e30
ATT_SCALE = HEAD_DIM ** -0.5
ADAM_LR, ADAM_B1, ADAM_B2, ADAM_EPS, ADAM_WD, ADAM_STEP = 0.001, 0.9, 0.999, 1e-08, 0.01, 10

N_CHIPS = 4
N_DEV = 8
LANES = 128
SUBLANES = 8
VMEM_LIMIT = 52 * 1024 * 1024
ROW_TILE = 512
MM_ROW_CHUNK = 256
MESH = pl.DeviceIdType.MESH


def _cparams(sem):
    return pltpu.CompilerParams(dimension_semantics=sem, vmem_limit_bytes=VMEM_LIMIT)


def _fold8(v):
    r, c = v.shape
    return jnp.sum(v.reshape(r // SUBLANES, SUBLANES, c), axis=0)


def _gelu(x):
    c = math.sqrt(2.0 / math.pi)
    return 0.5 * x * (1.0 + jnp.tanh(c * (x + 0.044715 * (x * x * x))))


def _gelu_and_grad(x):
    c = math.sqrt(2.0 / math.pi)
    t = jnp.tanh(c * (x + 0.044715 * (x * x * x)))
    return 0.5 * x * (1.0 + t), 0.5 * (1.0 + t) + 0.5 * x * (1.0 - t * t) * c * (1.0 + 3.0 * 0.044715 * x * x)


def _dot(a, b, dims):
    return lax.dot_general(a.astype(MXU_DTYPE), b.astype(MXU_DTYPE), (dims, ((), ())), preferred_element_type=F32)


def _dot_nn(a, b):
    return _dot(a, b, ((1,), (0,)))


def _dot_nt(a, b):
    return _dot(a, b, ((1,), (1,)))


def _dot_tn(a, b):
    return _dot(a, b, ((0,), (0,)))


def _mm(a, b, *, mode, name, outs, tm, tn, tk, epi=None, extras=(), b_col0=0, n_out=None, after=None,
        out_col0=0, out_cols=None, into=None):
    if mode == "nn":
        m, kdim = a.shape
        p, kb, ns = b.shape
        assert kb == kdim and ns % tn == 0 and b_col0 % tn == 0
        n = n_out if n_out is not None else p * ns
        npt, j0 = ns // tn, b_col0 // tn
        a_spec = pl.BlockSpec((tm, tk), lambda i, j, k: (i, k))
        b_spec = pl.BlockSpec((None, tk, tn), lambda i, j, k: ((j + j0) // npt, k, (j + j0) % npt))
        dot = _dot_nn
    elif mode == "nt":
        m, kdim = a.shape
        p, n, ns = b.shape
        assert ns % tk == 0 and b_col0 % tk == 0
        npt, j0 = ns // tk, b_col0 // tk
        a_spec = pl.BlockSpec((tm, tk), lambda i, j, k: (i, k))
        b_spec = pl.BlockSpec((None, tn, tk), lambda i, j, k: ((k + j0) // npt, j, (k + j0) % npt))
        dot = _dot_nt
    else:
        kdim, m = a.shape
        kb, n = b.shape
        assert kb == kdim
        a_spec = pl.BlockSpec((tk, tm), lambda i, j, k: (k, i))
        b_spec = pl.BlockSpec((tk, tn), lambda i, j, k: (k, j))
        dot = _dot_tn
    assert m % tm == 0 and n % tn == 0 and kdim % tk == 0, (name, m, n, kdim, tm, tn, tk)
    nk = kdim // tk
    ex_specs, ex_arrays = [], []
    for kind, arr in extras:
        if kind == "row":
            ex_specs.append(pl.BlockSpec((1, tn), lambda i, j, k: (0, j)))
        else:
            ex_specs.append(pl.BlockSpec((tm, tn), lambda i, j, k: (i, j)))
        ex_arrays.append(arr)
    n_ex, n_o = len(ex_arrays), len(outs)
    deps = [d for d in (after, into) if d is not None]
    n_dep = len(deps)
    j_out = out_col0 // tn
    assert out_col0 % tn == 0 and (into is None or len(outs) == 1)

    def body(a_ref, b_ref, *rest):
        ex_refs, o_refs = rest[:n_ex], rest[n_ex + n_dep:n_ex + n_dep + n_o]
        k = pl.program_id(2)

        chunks = [slice(r0, r0 + min(tm, MM_ROW_CHUNK)) for r0 in range(0, tm, min(tm, MM_ROW_CHUNK))]

        def part(rows):
            return dot(a_ref[:, rows] if mode == "tn" else a_ref[rows, :], b_ref[...])

        def finish(r, rows):
            exs = [e[...] if kind == "row" else e[rows, :] for (kind, _), e in zip(extras, ex_refs)]
            vals = epi(r, *exs) if epi is not None else [r]
            for o, v in zip(o_refs, vals):
                o[rows, :] = v.astype(o.dtype)

        if nk == 1:
            for rows in chunks:
                finish(part(rows), rows)
            return
        acc = rest[n_ex + n_dep + n_o]

        @pl.when(k == 0)
        def _():
            for rows in chunks:
                acc[rows, :] = part(rows)

        @pl.when((k > 0) & (k < nk - 1))
        def _():
            for rows in chunks:
                acc[rows, :] += part(rows)

        @pl.when(k == nk - 1)
        def _():
            for rows in chunks:
                finish(acc[rows, :] + part(rows), rows)

    res = pl.pallas_call(
        body,
        grid=(m // tm, n // tn, nk),
        in_specs=[a_spec, b_spec] + ex_specs + [pl.BlockSpec(memory_space=pl.ANY)] * n_dep,
        out_specs=[pl.BlockSpec((tm, tn), lambda i, j, k: (i, j + j_out)) for _ in outs],
        out_shape=[jax.ShapeDtypeStruct((m, out_cols or n), dt) for dt in outs],
        input_output_aliases={} if into is None else {2 + n_ex + n_dep - 1: 0},
        scratch_shapes=[pltpu.VMEM((tm, tn), F32)] if nk > 1 else [],
        name=name,
        compiler_params=_cparams(("parallel", "parallel", "arbitrary")),
    )(a, b, *ex_arrays, *deps)
    return res if len(outs) > 1 else res[0]


def _rows(body, n_rows, tr, ins, outs, name, scratch=()):
    def spec(kind, shape):
        if kind == "blk":
            return pl.BlockSpec((tr,) + tuple(shape[1:]), lambda i: (i,) + (0,) * (len(shape) - 1))
        if kind == "dep":
            return pl.BlockSpec(memory_space=pl.ANY)
        if kind == "str":
            return pl.BlockSpec((shape[0], tr // shape[0], shape[2]), lambda i: (0, i, 0))
        return pl.BlockSpec(tuple(shape), lambda i: (0,) * len(shape))

    return pl.pallas_call(
        body,
        grid=(n_rows // tr,),
        in_specs=[spec(k, a.shape) for k, a in ins],
        out_specs=[spec(k, s) for k, s, _ in outs],
        out_shape=[jax.ShapeDtypeStruct(tuple(s), d) for _, s, d in outs],
        scratch_shapes=list(scratch),
        name=name,
        compiler_params=_cparams(("arbitrary",)),
    )(*[a for _, a in ins])


def _ln_stats(z):
    mu = jnp.mean(z, axis=-1, keepdims=True)
    zc = z - mu
    var = jnp.mean(zc * zc, axis=-1, keepdims=True)
    rstd = lax.rsqrt(var + LN_EPS)
    return zc * rstd, rstd


def _stream_scratch(c):
    return pltpu.VMEM((c // LANES, ROW_TILE, LANES), F32)


def _streams_in(ref3, scr):
    dil, n, c = ref3.shape
    for r in range(dil):
        for j in range(c // LANES):
            scr.at[j][pl.ds(r, n, stride=dil), :] = ref3[r, :, j * LANES:(j + 1) * LANES].astype(F32)
    return jnp.concatenate([scr[j] for j in range(c // LANES)], axis=1)


def _streams_out(val, ref3, scr):
    dil, n, c = ref3.shape
    for j in range(c // LANES):
        scr[j] = val[:, j * LANES:(j + 1) * LANES].astype(F32)
    for r in range(dil):
        for j in range(c // LANES):
            ref3[r, :, j * LANES:(j + 1) * LANES] = scr.at[j][pl.ds(r, n, stride=dil), :].astype(ref3.dtype)


def _mod(x, scale, shift, after, name):
    s, d = x.shape

    def body(x_ref, sc_ref, sh_ref, dep_ref, h_ref):
        h_ref[...] = (x_ref[...] * (1.0 + sc_ref[...]) + sh_ref[...]).astype(h_ref.dtype)

    return _rows(body, s, ROW_TILE, [("blk", x), ("all", scale), ("all", shift), ("dep", after)], [("blk", (s, d), MXU_DTYPE)], name)[0]


def _resid_ln(x, y, gate, g, b, nxt, name, dils=()):
    s, d = x.shape

    def body(x_ref, y_ref, gate_ref, g_ref, b_ref, sc_ref, sh_ref, xn_ref, h_ref, *rest):
        z = ALPHA * x_ref[...] + gate_ref[...] * y_ref[...]
        xhat, _ = _ln_stats(z)
        xn = xhat * g_ref[...] + b_ref[...]
        xn_ref[...] = xn
        h = xn * (1.0 + sc_ref[...]) + sh_ref[...]
        h_ref[...] = h.astype(h_ref.dtype)
        for hs_ref in rest[:len(dils)]:
            _streams_out(h, hs_ref, rest[-1])

    return _rows(body, s, ROW_TILE,
                 [("blk", x), ("blk", y), ("all", gate), ("all", g), ("all", b), ("all", nxt[0]), ("all", nxt[1])],
                 [("blk", (s, d), F32), ("blk", (s, d), MXU_DTYPE)] + [("str", (dil, s // dil, d), MXU_DTYPE) for dil in dils], name,
                 scratch=[_stream_scratch(d)] if dils else [])


def _mod_bwd(dxr, dhs, x, scale, name, after=None):
    s, d = x.shape
    n_dh = len(dhs)
    n_dep = 0 if after is None else 1

    def body(dxr_ref, *rest):
        dh_refs = rest[:n_dh]
        x_ref, sc_ref, dx_ref, red_ref, a_sh, a_sc = rest[n_dh:n_dh + 2] + rest[n_dh + 2 + n_dep:]
        i = pl.program_id(0)

        @pl.when(i == 0)
        def _():
            a_sh[...] = jnp.zeros_like(a_sh)
            a_sc[...] = jnp.zeros_like(a_sc)

        dh = dh_refs[0][...]
        for r in dh_refs[1:]:
            dh = dh + r[...]
        dx_ref[...] = dxr_ref[...] + dh * (1.0 + sc_ref[...])
        a_sh[...] += _fold8(dh)
        a_sc[...] += _fold8(dh * x_ref[...])

        @pl.when(i == pl.num_programs(0) - 1)
        def _():
            red_ref[...] = jnp.zeros_like(red_ref)
            red_ref[0:1, :] = jnp.sum(a_sh[...], axis=0, keepdims=True)
            red_ref[1:2, :] = jnp.sum(a_sc[...], axis=0, keepdims=True)

    return _rows(body, s, ROW_TILE, [("blk", dxr)] + [("blk", h) for h in dhs] + [("blk", x), ("all", scale)] + [("dep", after)] * n_dep,
                 [("blk", (s, d), F32), ("all", (SUBLANES, d), F32)], name,
                 scratch=[pltpu.VMEM((SUBLANES, d), F32)] * 2)


def _last_ln_loss_bwd(x, y, gate, g, b, target, name):
    s, d = x.shape

    def body(x_ref, y_ref, gate_ref, g_ref, b_ref, t_ref, l_ref, dxr_ref, dyy_ref, red_ref, a_l, a_g, a_b, a_gate):
        i = pl.program_id(0)

        @pl.when(i == 0)
        def _():
            for a in (a_l, a_g, a_b, a_gate):
                a[...] = jnp.zeros_like(a)

        yv = y_ref[...]
        z = ALPHA * x_ref[...] + gate_ref[...] * yv
        xhat, rstd = _ln_stats(z)
        e = xhat * g_ref[...] + b_ref[...] - t_ref[...]
        a_l[...] += _fold8(e * e)
        dxo_v = e * (1.0 / d)
        dxh = dxo_v * g_ref[...]
        dz = rstd * (dxh - jnp.mean(dxh, axis=-1, keepdims=True) - xhat * jnp.mean(dxh * xhat, axis=-1, keepdims=True))
        dxr_ref[...] = ALPHA * dz
        dyy_ref[...] = (gate_ref[...] * dz).astype(dyy_ref.dtype)
        a_g[...] += _fold8(dxo_v * xhat)
        a_b[...] += _fold8(dxo_v)
        a_gate[...] += _fold8(dz * yv)

        @pl.when(i == pl.num_programs(0) - 1)
        def _():
            l_ref[...] = jnp.full(l_ref.shape, 0.5 / d, F32) * jnp.sum(a_l[...])
            red_ref[...] = jnp.zeros_like(red_ref)
            red_ref[0:1, :] = jnp.sum(a_g[...], axis=0, keepdims=True)
            red_ref[1:2, :] = jnp.sum(a_b[...], axis=0, keepdims=True)
            red_ref[2:3, :] = jnp.sum(a_gate[...], axis=0, keepdims=True)

    l, dxr, dyy, red = _rows(
        body, s, ROW_TILE, [("blk", x), ("blk", y), ("all", gate), ("all", g), ("all", b), ("blk", target)],
        [("all", (SUBLANES, LANES), F32), ("blk", (s, d), F32), ("blk", (s, d), MXU_DTYPE), ("all", (SUBLANES, d), F32)], name,
        scratch=[pltpu.VMEM((SUBLANES, d), F32)] * 4)
    return l[0, 0], dxr, dyy, red


def _mod_ln_bwd(dxr, dhs, x, scale, x_in, y, gate, g, name, after=None):
    s, d = x.shape
    n_dh = len(dhs)
    n_dep = 0 if after is None else 1

    def body(dxr_ref, *rest):
        dh_refs = rest[:n_dh]
        x_ref, sc_ref, xin_ref, y_ref, gate_ref, g_ref = rest[n_dh:n_dh + 6]
        dxr_out, dyy_ref, red_mod, red_ln, a_sh, a_sc, a_g, a_b, a_gate = rest[n_dh + 6 + n_dep:n_dh + 15 + n_dep]
        i = pl.program_id(0)

        @pl.when(i == 0)
        def _():
            for a in (a_sh, a_sc, a_g, a_b, a_gate):
                a[...] = jnp.zeros_like(a)

        dh = dh_refs[0][...]
        for r in dh_refs[1:]:
            dh = dh + (r[...] if len(r.shape) == 2 else _streams_in(r, rest[-1]))
        xv = x_ref[...]
        dxo_v = dxr_ref[...] + dh * (1.0 + sc_ref[...])
        a_sh[...] += _fold8(dh)
        a_sc[...] += _fold8(dh * xv)
        yv = y_ref[...]
        z = ALPHA * xin_ref[...] + gate_ref[...] * yv
        xhat, rstd = _ln_stats(z)
        dxh = dxo_v * g_ref[...]
        dz = rstd * (dxh - jnp.mean(dxh, axis=-1, keepdims=True) - xhat * jnp.mean(dxh * xhat, axis=-1, keepdims=True))
        dxr_out[...] = ALPHA * dz
        dyy_ref[...] = (gate_ref[...] * dz).astype(dyy_ref.dtype)
        a_g[...] += _fold8(dxo_v * xhat)
        a_b[...] += _fold8(dxo_v)
        a_gate[...] += _fold8(dz * yv)

        @pl.when(i == pl.num_programs(0) - 1)
        def _():
            red_mod[...] = jnp.zeros_like(red_mod)
            red_mod[0:1, :] = jnp.sum(a_sh[...], axis=0, keepdims=True)
            red_mod[1:2, :] = jnp.sum(a_sc[...], axis=0, keepdims=True)
            red_ln[...] = jnp.zeros_like(red_ln)
            red_ln[0:1, :] = jnp.sum(a_g[...], axis=0, keepdims=True)
            red_ln[1:2, :] = jnp.sum(a_b[...], axis=0, keepdims=True)
            red_ln[2:3, :] = jnp.sum(a_gate[...], axis=0, keepdims=True)

    ins = ([("blk", dxr)] + [("blk" if h.ndim == 2 else "str", h) for h in dhs]
           + [("blk", x), ("all", scale), ("blk", x_in), ("blk", y), ("all", gate), ("all", g)] + [("dep", after)] * n_dep)
    return _rows(body, s, ROW_TILE, ins,
                 [("blk", (s, d), F32), ("blk", (s, d), MXU_DTYPE), ("all", (SUBLANES, d), F32), ("all", (SUBLANES, d), F32)], name,
                 scratch=[pltpu.VMEM((SUBLANES, d), F32)] * 5 + [_stream_scratch(d)] * any(h.ndim == 3 for h in dhs))


def _left_half(shape):
    return lax.broadcasted_iota(jnp.int32, shape, 1) < (LANES // 2)


CHUNKS_PER_STEP = 2


def _chunks_of_step():
    return [slice(i * CHUNK, (i + 1) * CHUNK) for i in range(CHUNKS_PER_STEP)]


def _spatial_z(vn, wc_ref, bias_ref, j):
    vb = vn[:, j * LANES:(j + 1) * LANES]
    z0 = _dot_nn(wc_ref[2 * j], vb)
    z1 = _dot_nn(wc_ref[2 * j + 1], vb)
    return jnp.where(_left_half(z0.shape), z0, z1) + bias_ref[:, j * LANES:(j + 1) * LANES]


def _spatial_fwd(uvpre, vn_g, vn_b, wc, bias_full, name):
    s, d2 = uvpre.shape
    d = d2 // 2

    def body(uv_ref, g_ref, b_ref, wc_ref, bias_ref, out_ref):
        for rows in _chunks_of_step():
            u = _gelu(uv_ref[rows, :d])
            v = _gelu(uv_ref[rows, d:])
            vh, _ = _ln_stats(v)
            vn = vh * g_ref[...] + b_ref[...]
            for j in range(d // LANES):
                z = _spatial_z(vn, wc_ref, bias_ref, j)
                out_ref[rows, j * LANES:(j + 1) * LANES] = (u[:, j * LANES:(j + 1) * LANES] * z).astype(out_ref.dtype)

    return _rows(body, s, CHUNKS_PER_STEP * CHUNK, [("blk", uvpre), ("all", vn_g), ("all", vn_b), ("all", wc), ("all", bias_full)],
                 [("blk", (s, d), MXU_DTYPE)], name)[0]


def _spatial_bwd(uvpre, dgated, vn_g, vn_b, wc, wct, bias_full, name):
    s, d2 = uvpre.shape
    d = d2 // 2

    def body(uv_ref, dg_ref, g_ref, b_ref, wc_ref, wct_ref, bias_ref,
             duv_ref, dws_ref, dbias_ref, dbin_ref, dvg_ref, dvb_ref, dvn_buf, a_bin, a_vg, a_vb):
        i = pl.program_id(0)

        @pl.when(i == 0)
        def _():
            dws_ref[...] = jnp.zeros_like(dws_ref)
            dbias_ref[...] = jnp.zeros_like(dbias_ref)
            a_bin[...] = jnp.zeros_like(a_bin)
            a_vg[...] = jnp.zeros_like(a_vg)
            a_vb[...] = jnp.zeros_like(a_vb)

        for rows in _chunks_of_step():
            u, u_grad = _gelu_and_grad(uv_ref[rows, :d])
            v, v_grad = _gelu_and_grad(uv_ref[rows, d:])
            vh, rstd = _ln_stats(v)
            vn = vh * g_ref[...] + b_ref[...]
            dg = dg_ref[rows, :]
            dzz = dg * u
            dbias_ref[...] += dzz
            for j in range(d // LANES):
                cols = slice(j * LANES, (j + 1) * LANES)
                z = _spatial_z(vn, wc_ref, bias_ref, j)
                dup = dg[:, cols] * z * u_grad[:, cols]
                duv_ref[rows, cols] = dup.astype(duv_ref.dtype)
                a_bin[:, cols] += _fold8(dup)
                dzb = dzz[:, cols]
                left = _left_half(dzb.shape)
                dvn_buf[:, cols] = jnp.where(left, _dot_nn(wct_ref[2 * j], dzb), _dot_nn(wct_ref[2 * j + 1], dzb))
                vb = vn[:, cols]
                dws_ref[2 * j] += _dot_nt(jnp.where(left, dzb, 0.0), vb)
                dws_ref[2 * j + 1] += _dot_nt(jnp.where(left, 0.0, dzb), vb)
            dvn = dvn_buf[...]
            a_vg[...] += _fold8(dvn * vh)
            a_vb[...] += _fold8(dvn)
            dvh = dvn * g_ref[...]
            dv = rstd * (dvh - jnp.mean(dvh, axis=-1, keepdims=True) - vh * jnp.mean(dvh * vh, axis=-1, keepdims=True))
            dvp = dv * v_grad
            duv_ref[rows, d:] = dvp.astype(duv_ref.dtype)
            a_bin[:, d:] += _fold8(dvp)

        @pl.when(i == pl.num_programs(0) - 1)
        def _():
            dbin_ref[...] = jnp.sum(a_bin[...], axis=0, keepdims=True)
            dvg_ref[...] = jnp.sum(a_vg[...], axis=0, keepdims=True)
            dvb_ref[...] = jnp.sum(a_vb[...], axis=0, keepdims=True)

    return _rows(body, s, CHUNKS_PER_STEP * CHUNK,
                 [("blk", uvpre), ("blk", dgated), ("all", vn_g), ("all", vn_b), ("all", wc), ("all", wct), ("all", bias_full)],
                 [("blk", (s, d2), MXU_DTYPE), ("all", (A_GROUPS, CHUNK, CHUNK), F32), ("all", (CHUNK, d), F32),
                  ("all", (1, d2), F32), ("all", (1, d), F32), ("all", (1, d), F32)], name,
                 scratch=[pltpu.VMEM((CHUNK, d), F32), pltpu.VMEM((SUBLANES, d2), F32),
                          pltpu.VMEM((SUBLANES, d), F32), pltpu.VMEM((SUBLANES, d), F32)])


def _head_mask(v, h):
    lane = lax.broadcasted_iota(jnp.int32, v.shape, 1)
    return jnp.where((lane >= h * HEAD_DIM) & (lane < (h + 1) * HEAD_DIM), v, jnp.zeros_like(v))


def _att_bias(slopes, dil):
    qi = lax.broadcasted_iota(jnp.int32, (SPAN, SPAN), 0)
    ki = lax.broadcasted_iota(jnp.int32, (SPAN, SPAN), 1)
    sl = slopes[:, None, None]
    cur = jnp.where(ki <= qi, -sl * (float(dil) * (qi - ki).astype(F32)), NEG)
    prev = jnp.where(ki >= qi, -sl * (float(dil) * (SPAN + qi - ki).astype(F32)), NEG)
    absent = jnp.full_like(prev, NEG)
    pairs = slopes.shape[0] // 2

    def fwd(pv):
        return jnp.concatenate([cur, pv], axis=2).reshape(pairs, 2 * SPAN, 2 * SPAN)

    def bwd(pv):
        return jnp.concatenate([cur.reshape(pairs, 2 * SPAN, SPAN), pv.reshape(pairs, 2 * SPAN, SPAN)], axis=1)

    return jnp.stack([fwd(absent), fwd(prev)]), jnp.stack([bwd(absent), bwd(prev)])


ATT_GROUP = 4


def _att_group(s, dil):
    nb = s // (dil * SPAN)
    grp = min(ATT_GROUP, nb)
    assert nb % grp == 0
    return nb, grp


def _att_specs(s, d, dil, kinds):
    nb, grp = _att_group(s, dil)

    def spec(part, which):
        if which == "group":
            return pl.BlockSpec((grp * SPAN, d), lambda b: (b, part))
        if which == "prev":
            return pl.BlockSpec((SPAN, d), lambda b: (jnp.where((grp * b) % nb == 0, grp * b, grp * b - 1), part))
        return pl.BlockSpec((SPAN, d), lambda b: (jnp.where((grp * b + grp - 1) % nb == nb - 1, grp * b + grp - 1, grp * b + grp), part))

    return [spec(part, which) for part, which in kinds]


def _head_col(v, head):
    return v[:, head:head + 1]


def _expand_heads(w, j):
    shape = (w.shape[0], LANES)
    return jnp.where(_left_half(shape), jnp.broadcast_to(_head_col(w, 2 * j), shape), jnp.broadcast_to(_head_col(w, 2 * j + 1), shape))


def _attn_fwd(qkv, slopes, dil, name):
    s, d3 = qkv.shape
    d = d3 // 3
    nb, grp = _att_group(s, dil)
    table, _ = _att_bias(slopes, dil)

    def body(q_ref, k_ref, kp_ref, v_ref, vp_ref, tb_ref, o_ref, l_ref):
        b = pl.program_id(0)
        left = _left_half((SPAN, LANES))
        lane = lax.broadcasted_iota(jnp.int32, (SPAN, LANES), 1)
        for sub in range(grp):
            rows, before = slice(sub * SPAN, (sub + 1) * SPAN), slice((sub - 1) * SPAN, sub * SPAN)
            variant = jnp.where((grp * b) % nb == 0, 0, 1) if sub == 0 else 1
            lses = jnp.zeros((SPAN, LANES), F32)
            for hp in range(d // LANES):
                cols = slice(hp * LANES, (hp + 1) * LANES)
                q = q_ref[rows, cols]
                q2 = jnp.concatenate([_head_mask(q, 0), _head_mask(q, 1)], axis=0) * ATT_SCALE
                k2 = jnp.concatenate([k_ref[rows, cols], kp_ref[:, cols] if sub == 0 else k_ref[before, cols]], axis=0)
                v2 = jnp.concatenate([v_ref[rows, cols], vp_ref[:, cols] if sub == 0 else v_ref[before, cols]], axis=0)
                sc = _dot_nt(q2, k2) + tb_ref[variant, hp]
                m = jnp.max(sc, axis=-1, keepdims=True)
                p = jnp.exp(sc - m)
                l = jnp.sum(p, axis=-1, keepdims=True)
                r = _dot_nn(p, v2) * (1.0 / l)
                lse = m + jnp.log(l)
                o_ref[rows, cols] = jnp.where(left, r[:SPAN], r[SPAN:])
                lses = jnp.where(lane == 2 * hp, lse[:SPAN], jnp.where(lane == 2 * hp + 1, lse[SPAN:], lses))
            l_ref[rows, :] = lses

    specs = _att_specs(s, d, dil, [(0, "group"), (1, "group"), (1, "prev"), (2, "group"), (2, "prev")])
    return pl.pallas_call(
        body,
        grid=(s // (grp * SPAN),),
        in_specs=specs + [pl.BlockSpec(table.shape, lambda b: (0, 0, 0, 0))],
        out_specs=[pl.BlockSpec((grp * SPAN, d), lambda b: (b, 0)), pl.BlockSpec((grp * SPAN, LANES), lambda b: (b, 0))],
        out_shape=[jax.ShapeDtypeStruct((s, d), F32), jax.ShapeDtypeStruct((s, LANES), F32)],
        name=name,
        compiler_params=_cparams(("parallel",)),
    )(qkv, qkv, qkv, qkv, qkv, table)


def _attn_bwd(qkv, do, lse, dd, slopes, dil, name):
    s, d3 = qkv.shape
    d = d3 // 3
    nb, grp = _att_group(s, dil)
    _, table = _att_bias(slopes, dil)

    def cols_stacked(cur, nxt, hp):
        return jnp.concatenate([jnp.broadcast_to(_head_col(a, 2 * hp + h), (SPAN, LANES)) for a in (cur, nxt) for h in range(2)], axis=0)

    def body(k_ref, v_ref, q_ref, qn_ref, do_ref, don_ref, l_ref, ln_ref, dd_ref, ddn_ref, tb_ref, out_ref, carry):
        b = pl.program_id(0)

        @pl.when(b == 0)
        def _():
            carry[...] = jnp.zeros_like(carry)

        wide = 2 * LANES
        head_of_lane = (lax.broadcasted_iota(jnp.int32, (SPAN, wide), 1) % LANES) // HEAD_DIM
        zero = jnp.zeros((SPAN, LANES), k_ref.dtype)

        def heads_stacked2(cur, nxt):
            return jnp.concatenate([jnp.where(head_of_lane == h, a, jnp.zeros_like(a)) for a in (cur, nxt) for h in range(2)], axis=0)

        def block_diagonal(a, b):
            return jnp.concatenate([jnp.concatenate([a, zero], axis=1), jnp.concatenate([zero, b], axis=1)], axis=0)

        for sub in range(grp):
            rows, after = slice(sub * SPAN, (sub + 1) * SPAN), slice((sub + 1) * SPAN, (sub + 2) * SPAN)
            last = sub == grp - 1
            variant = jnp.where((grp * b + sub) % nb == nb - 1, 0, 1) if last else 1
            lse_c, dd_c = l_ref[rows, :], dd_ref[rows, :]
            lse_n, dd_n = (ln_ref[...], ddn_ref[...]) if last else (l_ref[after, :], dd_ref[after, :])
            for hp2 in range(d // wide):
                cols = slice(hp2 * wide, (hp2 + 1) * wide)
                pa, pb = 2 * hp2, 2 * hp2 + 1
                ca, cb = slice(pa * LANES, (pa + 1) * LANES), slice(pb * LANES, (pb + 1) * LANES)
                kbd = block_diagonal(k_ref[rows, ca], k_ref[rows, cb])
                vbd = block_diagonal(v_ref[rows, ca], v_ref[rows, cb])
                q4 = heads_stacked2(q_ref[rows, cols], qn_ref[:, cols] if last else q_ref[after, cols])
                do4 = heads_stacked2(do_ref[rows, cols], don_ref[:, cols] if last else do_ref[after, cols])
                bias = jnp.concatenate([tb_ref[variant, pa], tb_ref[variant, pb]], axis=1)
                lse2 = jnp.concatenate([cols_stacked(lse_c, lse_n, pa), cols_stacked(lse_c, lse_n, pb)], axis=1)
                dd2 = jnp.concatenate([cols_stacked(dd_c, dd_n, pa), cols_stacked(dd_c, dd_n, pb)], axis=1)
                p = jnp.exp(_dot_nt(q4 * ATT_SCALE, kbd) + bias - lse2)
                ds = p * (_dot_nt(do4, vbd) - dd2)
                dq4 = _dot_nn(ds, kbd)
                left = head_of_lane == 0
                dq_cur = jnp.where(left, dq4[:SPAN], dq4[SPAN:2 * SPAN]) + carry[:, cols]
                carry[:, cols] = jnp.where(left, dq4[2 * SPAN:3 * SPAN], dq4[3 * SPAN:])
                out_ref[rows, cols] = (dq_cur * ATT_SCALE).astype(out_ref.dtype)
                for pair, lanes in ((pa, slice(0, LANES)), (pb, slice(LANES, wide))):
                    out_ref[rows, d + pair * LANES:d + (pair + 1) * LANES] = (_dot_tn(ds[:, lanes], q4[:, lanes]) * ATT_SCALE).astype(out_ref.dtype)
                    out_ref[rows, 2 * d + pair * LANES:2 * d + (pair + 1) * LANES] = _dot_tn(p[:, lanes], do4[:, lanes]).astype(out_ref.dtype)

    qkv_specs = _att_specs(s, d, dil, [(1, "group"), (2, "group"), (0, "group"), (0, "next")])
    wide = _att_specs(s, d, dil, [(0, "group"), (0, "next")])
    heads = _att_specs(s, LANES, dil, [(0, "group"), (0, "next")])
    return pl.pallas_call(
        body,
        grid=(s // (grp * SPAN),),
        in_specs=qkv_specs + wide + heads + heads + [pl.BlockSpec(table.shape, lambda b: (0, 0, 0, 0))],
        out_specs=pl.BlockSpec((grp * SPAN, d3), lambda b: (b, 0)),
        out_shape=jax.ShapeDtypeStruct((s, d3), MXU_DTYPE),
        scratch_shapes=[pltpu.VMEM((SPAN, d), F32)],
        name=name,
        compiler_params=_cparams(("arbitrary",)),
    )(qkv, qkv, qkv, qkv, do, do, lse, lse, dd, dd, table)


def _mix_weights(l_refs):
    ls = [r[...] for r in l_refs]
    m = functools.reduce(jnp.maximum, ls)
    es = [jnp.exp(l - m) for l in ls]
    tot = functools.reduce(lambda a, c: a + c, es)
    return [e / tot for e in es]


def _combine_fwd(os_, ls_, name):
    s, d = ls_[0].shape[0], os_[0].shape[-1]
    n = len(os_)
    n_str = sum(o.ndim == 3 for o in os_)

    def body(*refs):
        o_refs, l_refs, out_ref, scrs = refs[:n], refs[n:2 * n], refs[2 * n], list(refs[2 * n + 1:])
        ws = _mix_weights(l_refs)
        os_v = [o if len(o.shape) == 2 else _streams_in(o, scrs.pop()) for o in o_refs]
        for j in range(d // LANES):
            cols = slice(j * LANES, (j + 1) * LANES)
            acc = _expand_heads(ws[0], j) * os_v[0][:, cols]
            for w, o in zip(ws[1:], os_v[1:]):
                acc = acc + _expand_heads(w, j) * o[:, cols]
            out_ref[:, cols] = acc

    return _rows(body, s, ROW_TILE, [("blk" if a.ndim == 2 else "str", a) for a in os_] + [("blk", a) for a in ls_],
                 [("blk", (s, d), F32)], name, scratch=[_stream_scratch(d)] * n_str)[0]


def _combine_bwd(do, o, ls_, dils, name):
    s, d = o.shape
    n = len(ls_)
    sel = (lax.broadcasted_iota(jnp.int32, (d, LANES), 0) // HEAD_DIM == lax.broadcasted_iota(jnp.int32, (d, LANES), 1)).astype(F32)

    def body(do_ref, o_ref, *rest):
        l_refs, sel_ref, outs = rest[:n], rest[n], rest[n + 1:n + 1 + 2 * n]
        ws = _mix_weights(l_refs)
        dov = do_ref[...]
        r = jnp.dot(dov * o_ref[...], sel_ref[...], precision=lax.Precision.HIGHEST, preferred_element_type=F32)
        for g in range(n):
            outs[2 * g + 1][...] = ws[g] * r
            parts = [_expand_heads(ws[g], j) * dov[:, j * LANES:(j + 1) * LANES] for j in range(d // LANES)]
            if dils[g] == 1:
                for j, part in enumerate(parts):
                    outs[2 * g][:, j * LANES:(j + 1) * LANES] = part.astype(outs[2 * g].dtype)
            else:
                _streams_out(jnp.concatenate(parts, axis=1), outs[2 * g], rest[-1])

    outs = []
    for dil in dils:
        outs += [("blk", (s, d), MXU_DTYPE) if dil == 1 else ("str", (dil, s // dil, d), MXU_DTYPE), ("blk", (s, LANES), F32)]
    res = _rows(body, s, ROW_TILE, [("blk", do), ("blk", o)] + [("blk", l) for l in ls_] + [("all", sel)], outs, name,
                scratch=[_stream_scratch(d)])
    return [(res[2 * g], res[2 * g + 1]) for g in range(n)]


def _ada_fwd(c_all, w, b, name):
    nsub, d, cs = w.shape

    def body(c_ref, w_ref, b_ref, o_ref):
        cv = c_ref[...]
        sc = cv * (1.0 / (1.0 + jnp.exp(-cv)))
        o_ref[...] = _dot_nn(sc, w_ref[...]) + b_ref[...]

    return pl.pallas_call(
        body,
        grid=(nsub,),
        in_specs=[pl.BlockSpec(c_all.shape, lambda i: (0, 0)), pl.BlockSpec((None, d, cs), lambda i: (i, 0, 0)),
                  pl.BlockSpec((None, 1, cs), lambda i: (i, 0, 0))],
        out_specs=pl.BlockSpec((None, N_DEV, cs), lambda i: (i, 0, 0)),
        out_shape=jax.ShapeDtypeStruct((nsub, N_DEV, cs), F32),
        name=name,
        compiler_params=_cparams(("parallel",)),
    )(c_all, w, b)


def _ada_bwd(c_all_t, dm, name):
    d, nb = c_all_t.shape
    nsub, _, cs = dm.shape

    def body(c_ref, dm_ref, o_ref):
        cv = c_ref[...]
        sc = cv * (1.0 / (1.0 + jnp.exp(-cv)))
        acc = sc[:, 0:1] * dm_ref[0:1, :]
        for bi in range(1, nb):
            acc = acc + sc[:, bi:bi + 1] * dm_ref[bi:bi + 1, :]
        o_ref[...] = acc

    return pl.pallas_call(
        body,
        grid=(nsub,),
        in_specs=[pl.BlockSpec(c_all_t.shape, lambda i: (0, 0)), pl.BlockSpec((None, nb, cs), lambda i: (i, 0, 0))],
        out_specs=pl.BlockSpec((None, d, cs), lambda i: (i, 0, 0)),
        out_shape=jax.ShapeDtypeStruct((nsub, d, cs), F32),
        name=name,
        compiler_params=_cparams(("parallel",)),
    )(c_all_t, dm)


def _row_tile(r, row_elems, block_elems=256 * 1024):
    t = 2 * SUBLANES
    if r % t:
        return r
    while t * 2 * row_elems <= block_elems and r % (t * 2) == 0:
        t *= 2
    return t


def _adamw(w, g, m, v, name):
    shape = w.shape
    c = shape[-1]
    r = w.size // c
    tr = _row_tile(r, c, 512 * 1024)
    w2, g2, m2, v2 = [a.reshape(r, c) for a in (w, g, m, v)]
    bc1 = 1.0 - ADAM_B1 ** ADAM_STEP
    bc2 = 1.0 - ADAM_B2 ** ADAM_STEP

    def body(w_ref, g_ref, m_ref, v_ref, d_ref, nm_ref, nv_ref):
        gv = g_ref[...]
        nm = ADAM_B1 * m_ref[...] + (1.0 - ADAM_B1) * gv
        nv = ADAM_B2 * v_ref[...] + (1.0 - ADAM_B2) * (gv * gv)
        d_ref[...] = -ADAM_LR * ((nm / bc1) / (jnp.sqrt(nv / bc2) + ADAM_EPS) + ADAM_WD * w_ref[...])
        nm_ref[...] = nm
        nv_ref[...] = nv

    res = _rows(body, r, tr, [("blk", a) for a in (w2, g2, m2, v2)], [("blk", (r, c), F32)] * 3, name)
    return [a.reshape(shape) for a in res]


def _sum_slots(buf, name):
    n, r, c = buf.shape
    tr = _row_tile(r, n * c, 2 * 1024 * 1024)

    def body(b_ref, o_ref):
        acc = b_ref[0].astype(F32)
        for k in range(1, n):
            acc = acc + b_ref[k].astype(F32)
        o_ref[...] = acc

    return pl.pallas_call(
        body,
        grid=(r // tr,),
        in_specs=[pl.BlockSpec((n, tr, c), lambda i: (0, i, 0))],
        out_specs=pl.BlockSpec((tr, c), lambda i: (i, 0)),
        out_shape=jax.ShapeDtypeStruct((r, c), F32),
        name=name,
        compiler_params=_cparams(("parallel",)),
    )(buf)


def _me():
    return lax.axis_index("x"), lax.axis_index("y"), lax.axis_index("c")


def _all_gather_small(blk, name, after=()):
    m_per, n = blk.shape

    def body(x_ref, *rest):
        out_ref, send_sems, recv_sems, local_sem = rest[len(after):]
        x, y, c = _me()
        me, sibling = (x, y, c), (x, y, 1 - c)
        chips = [(1 - x, y), (x, 1 - y), (1 - x, 1 - y)]

        def rows(px, py, pc):
            return out_ref.at[pl.ds((4 * px + 2 * py + pc) * m_per, m_per), :]

        def copy(k, block, to, src=None):
            return pltpu.make_async_remote_copy(
                src_ref=rows(*block) if src is None else src, dst_ref=rows(*block),
                send_sem=send_sems.at[k], recv_sem=recv_sems.at[k], device_id=to, device_id_type=MESH)

        mine = pltpu.make_async_copy(x_ref, rows(*me), local_sem)
        mine.start()
        first = [copy(0, me, sibling, src=x_ref)]
        first += [copy(1 + j, me, (*chip, c), src=x_ref) for j, chip in enumerate(chips)]
        for cp in first:
            cp.start()
        passed = [copy(4 + j, (*chip, c), sibling) for j, chip in enumerate(chips)]
        for j, chip in enumerate(chips):
            copy(1 + j, (*chip, c), me).wait_recv()
            passed[j].start()
        copy(0, sibling, me).wait_recv()
        for j, chip in enumerate(chips):
            copy(4 + j, (*chip, 1 - c), me).wait_recv()
        for cp in first + passed:
            cp.wait_send()
        mine.wait()

    return pl.pallas_call(
        body,
        out_shape=jax.ShapeDtypeStruct((N_DEV * m_per, n), blk.dtype),
        in_specs=[pl.BlockSpec(memory_space=pltpu.VMEM)] + [pl.BlockSpec(memory_space=pl.ANY)] * len(after),
        out_specs=pl.BlockSpec(memory_space=pltpu.VMEM),
        scratch_shapes=[pltpu.SemaphoreType.DMA((7,)), pltpu.SemaphoreType.DMA((7,)), pltpu.SemaphoreType.DMA],
        name=name,
        compiler_params=pltpu.CompilerParams(vmem_limit_bytes=VMEM_LIMIT),
    )(blk, *after)


_HBM = pl.BlockSpec(memory_space=pltpu.HBM)
_SEM = pl.BlockSpec(memory_space=pltpu.SEMAPHORE)
_EFFECT = pltpu.SideEffectType.DATAFLOW_SIDE_EFFECTING


def _other_chips(x, y):
    return [(1 - x, y), (x, 1 - y), (1 - x, 1 - y)]


def _gather_copy(w, j, src_ref, land_ref, send_sems, recv_sems, halved=False):
    x, y, c = _me()
    if halved:
        half = src_ref.shape[0] // 2
        src_ref = src_ref.at[pl.ds(c * half, half), :]
    return pltpu.make_async_remote_copy(
        src_ref=src_ref, dst_ref=land_ref.at[2 * x + y], send_sem=send_sems.at[3 * w + j], recv_sem=recv_sems.at[3 * w + j],
        device_id=(*_other_chips(x, y)[j], c), device_id_type=MESH)


def _gather_start(shards, halved, after, name):
    n = len(shards)
    lands = [lax.empty((N_CHIPS, s.shape[0] // 2 if w in halved else s.shape[0], s.shape[1]), s.dtype) for w, s in enumerate(shards)]

    def body(*refs):
        in_refs, land_refs = refs[:n], refs[n:2 * n]
        send_sems, recv_sems = refs[2 * n + 1], refs[2 * n + 2]
        token = refs[-1]
        for w in range(n):
            for j in range(3):
                _gather_copy(w, j, in_refs[w], land_refs[w], send_sems, recv_sems, w in halved).start()
        token[...] = jnp.zeros_like(token)

    res = pl.pallas_call(
        body,
        out_shape=(pltpu.SemaphoreType.DMA((3 * n,)), pltpu.SemaphoreType.DMA((3 * n,)),
                   *[pltpu.HBM(s.shape, s.dtype) for s in shards], *[pltpu.HBM(l.shape, l.dtype) for l in lands],
                   jax.ShapeDtypeStruct((SUBLANES, LANES), F32)),
        in_specs=[_HBM] * (2 * n) + [pl.BlockSpec(memory_space=pl.ANY)],
        out_specs=(_SEM, _SEM, *[_HBM] * (2 * n), pl.BlockSpec(memory_space=pltpu.VMEM)),
        input_output_aliases={i: 2 + i for i in range(2 * n)},
        name=name,
        compiler_params=pltpu.CompilerParams(has_side_effects=_EFFECT),
    )(*[pltpu.with_memory_space_constraint(a, pltpu.HBM) for a in list(shards) + lands], after)
    return res[0], res[1], res[2:2 + n], res[2 + n:2 + 2 * n], res[-1]


def _gather_wait(w, shard, land, send_sems, recv_sems, after, name, halved=False):
    def body(s_ref, land_ref, send_sems, recv_sems, after_ref, s_out, land_out, stage):
        x, y, _ = _me()
        if not halved:
            pltpu.sync_copy(s_ref, stage)
            pltpu.sync_copy(stage, land_out.at[2 * x + y])
        for j in range(3):
            cp = _gather_copy(w, j, s_ref, land_ref, send_sems, recv_sems, halved)
            cp.wait_send()
            cp.wait_recv()

    return pl.pallas_call(
        body,
        out_shape=(pltpu.HBM(shard.shape, shard.dtype), pltpu.HBM(land.shape, land.dtype)),
        in_specs=(_HBM, _HBM, _SEM, _SEM, pl.BlockSpec(memory_space=pl.ANY)),
        out_specs=(_HBM, _HBM),
        input_output_aliases={0: 0, 1: 1},
        scratch_shapes=[pltpu.VMEM((SUBLANES, LANES) if halved else shard.shape, shard.dtype)],
        name=name,
        compiler_params=pltpu.CompilerParams(has_side_effects=_EFFECT, vmem_limit_bytes=VMEM_LIMIT),
    )(shard, land, send_sems, recv_sems, after)


def _assemble_halves(shard, land, name):
    half = land.shape[1]

    def body(s_ref, land_ref, out_ref, send_sems, recv_sems, local_sems):
        x, y, c = _me()
        own = pltpu.make_async_copy(s_ref, out_ref.at[2 * x + y], local_sems.at[3])
        own.start()
        cps = []
        for j, (ox, oy) in enumerate(_other_chips(x, y)):
            qj = 2 * ox + oy
            mine = out_ref.at[qj, pl.ds(c * half, half), :]
            lc = pltpu.make_async_copy(land_ref.at[qj], mine, local_sems.at[j])
            lc.start()
            rc = pltpu.make_async_remote_copy(
                src_ref=land_ref.at[qj], dst_ref=mine, send_sem=send_sems.at[j], recv_sem=recv_sems.at[j],
                device_id=(x, y, 1 - c), device_id_type=MESH)
            rc.start()
            cps.append((lc, rc))
        for lc, rc in cps:
            rc.wait_recv()
        for lc, rc in cps:
            rc.wait_send()
            lc.wait()
        own.wait()

    vmem = pl.BlockSpec(memory_space=pltpu.VMEM)
    return pl.pallas_call(
        body,
        out_shape=jax.ShapeDtypeStruct((N_CHIPS,) + shard.shape, shard.dtype),
        in_specs=[vmem, vmem],
        out_specs=vmem,
        scratch_shapes=[pltpu.SemaphoreType.DMA((3,)), pltpu.SemaphoreType.DMA((3,)), pltpu.SemaphoreType.DMA((4,))],
        name=name,
        compiler_params=pltpu.CompilerParams(vmem_limit_bytes=VMEM_LIMIT),
    )(shard, land)


def _piece_shape(shape, kind):
    k, nn = shape
    if kind == "all":
        return (k, nn)
    return (k // 2, nn // N_CHIPS) if kind == "col" else (k // N_CHIPS // 2, nn)


def _piece_of(g_ref, kind, tq, tc):
    pr, pc = _piece_shape(g_ref.shape, kind)
    if kind == "all":
        return g_ref
    if kind == "col":
        return g_ref.at[pl.ds(tc * pr, pr), pl.ds(tq * pc, pc)]
    return g_ref.at[pl.ds((2 * tq + tc) * pr, pr), :]


def _scatter_copy(w, r, kind, g_ref, land_ref, send_sems, recv_sems):
    x, y, c = _me()
    tx, ty, tc = (x + ((r >> 2) & 1)) % 2, (y + ((r >> 1) & 1)) % 2, (c + (r & 1)) % 2
    return pltpu.make_async_remote_copy(
        src_ref=_piece_of(g_ref, kind, 2 * tx + ty, tc), dst_ref=land_ref.at[4 * x + 2 * y + c],
        send_sem=send_sems.at[N_DEV * w + r], recv_sem=recv_sems.at[N_DEV * w + r], device_id=(tx, ty, tc), device_id_type=MESH)


def _scatter_start(gs, kinds, name):
    n = len(gs)
    pieces = [_piece_shape(g.shape, kind) for g, kind in zip(gs, kinds)]
    lands = [lax.empty((N_DEV,) + p, g.dtype) for p, g in zip(pieces, gs)]

    def body(*refs):
        g_refs, land_refs, send_sems, recv_sems = refs[:n], refs[n:2 * n], refs[2 * n], refs[2 * n + 1]
        land_outs, stages = refs[3 * n + 2:4 * n + 2], refs[4 * n + 2:]
        x, y, c = _me()
        for w in range(n):
            for r in range(1, N_DEV):
                _scatter_copy(w, r, kinds[w], g_refs[w], land_refs[w], send_sems, recv_sems).start()
        for w in range(n):
            pltpu.sync_copy(_piece_of(g_refs[w], kinds[w], 2 * x + y, c), stages[w])
            pltpu.sync_copy(stages[w], land_outs[w].at[4 * x + 2 * y + c])

    arrays = list(gs) + lands
    res = pl.pallas_call(
        body,
        out_shape=(pltpu.SemaphoreType.DMA((N_DEV * n,)), pltpu.SemaphoreType.DMA((N_DEV * n,)),
                   *[pltpu.HBM(a.shape, a.dtype) for a in arrays]),
        in_specs=[_HBM] * (2 * n),
        out_specs=(_SEM, _SEM, *[_HBM] * (2 * n)),
        input_output_aliases={i: 2 + i for i in range(2 * n)},
        scratch_shapes=[pltpu.VMEM(p, g.dtype) for p, g in zip(pieces, gs)],
        name=name,
        compiler_params=pltpu.CompilerParams(has_side_effects=_EFFECT, vmem_limit_bytes=VMEM_LIMIT),
    )(*[pltpu.with_memory_space_constraint(a, pltpu.HBM) for a in arrays])
    return res[0], res[1], res[2:2 + n], res[2 + n:]


def _scatter_wait(send_sems, recv_sems, gs, lands, kinds, after, name):
    n = len(gs)

    def body(*refs):
        g_refs, land_refs, send_sems, recv_sems = refs[:n], refs[n:2 * n], refs[2 * n], refs[2 * n + 1]
        for w in range(n):
            for r in range(1, N_DEV):
                cp = _scatter_copy(w, r, kinds[w], g_refs[w], land_refs[w], send_sems, recv_sems)
                cp.wait_send()
                cp.wait_recv()

    arrays = list(gs) + list(lands)
    return pl.pallas_call(
        body,
        out_shape=tuple(pltpu.HBM(a.shape, a.dtype) for a in arrays),
        in_specs=(*[_HBM] * (2 * n), _SEM, _SEM, pl.BlockSpec(memory_space=pl.ANY)),
        out_specs=tuple([_HBM] * (2 * n)),
        input_output_aliases={i: i for i in range(2 * n)},
        name=name,
        compiler_params=pltpu.CompilerParams(has_side_effects=_EFFECT),
    )(*arrays, send_sems, recv_sems, after)[n:]


def _sum_swap(bufs, name):
    n = len(bufs)

    def body(*refs):
        in_refs, out_refs = refs[:n], refs[n:2 * n]
        send_sems, recv_sems = refs[2 * n:]
        x, y, c = _me()
        cps = []
        for w in range(n):
            slots, r, _ = bufs[w].shape
            mine = out_refs[w].at[c]
            for r0 in range(0, r, min(r, ROW_TILE)):
                rows = slice(r0, r0 + min(r, ROW_TILE))
                acc = in_refs[w][0, rows, :].astype(F32)
                for k in range(1, slots):
                    acc = acc + in_refs[w][k, rows, :].astype(F32)
                mine[rows, :] = acc
            rc = pltpu.make_async_remote_copy(
                src_ref=mine, dst_ref=mine, send_sem=send_sems.at[w], recv_sem=recv_sems.at[w],
                device_id=(x, y, 1 - c), device_id_type=MESH)
            rc.start()
            cps.append(rc)
        for rc in cps:
            rc.wait_recv()
        for rc in cps:
            rc.wait_send()

    vmem = pl.BlockSpec(memory_space=pltpu.VMEM)
    return pl.pallas_call(
        body,
        out_shape=[jax.ShapeDtypeStruct((2,) + b.shape[1:], F32) for b in bufs],
        in_specs=[vmem] * n,
        out_specs=[vmem] * n,
        scratch_shapes=[pltpu.SemaphoreType.DMA((n,)), pltpu.SemaphoreType.DMA((n,))],
        name=name,
        compiler_params=pltpu.CompilerParams(vmem_limit_bytes=VMEM_LIMIT),
    )(*bufs)


def _to_streams(a, dil):
    if dil == 1:
        return a
    s, c = a.shape
    return a.reshape(s // dil, dil, c).transpose(1, 0, 2).reshape(s, c)


def _from_streams(a, dil):
    if dil == 1:
        return a
    s, c = a.shape
    return a.reshape(dil, s // dil, c).transpose(1, 0, 2).reshape(s, c)


def _mm_tiles(s):
    return min(s, 2048)


def _local_step(x0, target, mvec, ln_g, ln_b, small, fetch, emit, start):
    s, d = x0.shape
    tm = _mm_tiles(s)
    row = lambda v: v.reshape(1, -1)
    shift = [row(mvec[i, :d]) for i in range(4)]
    scale = [row(mvec[i, d:2 * d]) for i in range(4)]
    gate = [row(1.0 + mvec[i, 2 * d:]) for i in range(4)]
    lg = [row(ln_g[i]) for i in range(4)]
    lb = [row(ln_b[i]) for i in range(4)]
    mm = functools.partial(_mm, tm=tm)
    mm_w = functools.partial(_mm, tm=1024, tk=min(s, 2048), mode="tn")

    def resid_ln_epilogue(sub):
        def epi(y, xv, gate_v, g_v, b_v, sc_v, sh_v):
            xhat, _ = _ln_stats(ALPHA * xv + gate_v * y)
            xn = xhat * g_v + b_v
            return [y, xn, xn * (1.0 + sc_v) + sh_v]

        rows = [gate[sub], lg[sub], lb[sub], scale[sub + 1], shift[sub + 1]]
        return dict(outs=[F32, F32, MXU_DTYPE], epi=epi, extras=[("full", xs[sub])] + [("row", r) for r in rows])

    xs, ys, big = [x0], [], {}
    h0 = _mod(x0, scale[0], shift[0], start, "mod0")
    big["a_w_in"] = fetch("a_w_in", h0)
    uvpre = mm(h0, big["a_w_in"], mode="nn", name="a_in", outs=[F32], tn=512, tk=1024,
               epi=lambda r, bias: [r + bias], extras=[("row", small["a_b_in"])])
    gated = _spatial_fwd(uvpre, small["a_vn_g"], small["a_vn_b"], small["wc"], small["bias_full"], "a_spatial")
    big["a_w_out"] = fetch("a_w_out", gated)
    y0, x1, h1 = mm(gated, big["a_w_out"], mode="nn", name="a_out", tm=min(s, 1024), tn=d, tk=1024, **resid_ln_epilogue(0))
    ys.append(y0)
    xs.append(x1)
    relu2 = lambda r: [jnp.square(jnp.maximum(r, 0.0))]
    big["up0"] = fetch("up0", h1)
    r0 = mm(h1, big["up0"], mode="nn", name="up0", outs=[MXU_DTYPE], tn=1024, tk=1024, epi=relu2)
    big["down0"] = fetch("down0", r0)
    ys.append(mm(r0, big["down0"], mode="nn", name="down0", outs=[F32], tm=min(s, 1024), tn=1024, tk=2048))
    dils = [dil for _, dil in B_PATTERNS]
    x2, h2, *h2_streams = _resid_ln(xs[1], ys[1], gate[1], lg[1], lb[1], (scale[2], shift[2]), "ln1", [dil for dil in dils if dil > 1])
    h2_streams = [h2] + [a.reshape(s, d) for a in h2_streams]
    xs.append(x2)
    hg, qkvs, o_g, l_g, l_streams = [], [], [], [], []
    big["b_w_qkv"] = fetch("b_w_qkv", h2)
    for g, (_, dil) in enumerate(B_PATTERNS):
        hp = h2_streams[g]
        qkv = mm(hp, big["b_w_qkv"], mode="nn", name=f"qkv{g}", outs=[MXU_DTYPE], tn=768, tk=1024, b_col0=g * 3 * d, n_out=3 * d)
        og, lgv = _attn_fwd(qkv, small["slopes"], dil, f"attn_fwd{g}")
        hg.append(hp)
        qkvs.append(qkv)
        o_g.append(og if dil == 1 else og.reshape(dil, s // dil, d))
        l_g.append(_from_streams(lgv, dil))
        l_streams.append(lgv)
    o_mix = _combine_fwd(o_g, l_g, "combine")
    big["b_w_out"] = fetch("b_w_out", o_mix)
    y2, x3, h3 = mm(o_mix, big["b_w_out"], mode="nn", name="b_out", tm=min(s, 1024), tn=d, tk=1024, **resid_ln_epilogue(2))
    ys.append(y2)
    xs.append(x3)
    big["up1"] = fetch("up1", h3)
    r1 = mm(h3, big["up1"], mode="nn", name="up1", outs=[MXU_DTYPE], tn=1024, tk=1024, epi=relu2)
    big["down1"] = fetch("down1", r1)
    ys.append(mm(r1, big["down1"], mode="nn", name="down1", outs=[F32], tm=min(s, 1024), tn=1024, tk=2048))

    gb, red_ln, red_mod = {}, [None] * 4, [None] * 4

    def mlp_bwd(i, h, r, dyy):
        gb[f"down{i}"] = mm_w(r, dyy, name=f"g_down{i}", outs=[MXU_DTYPE], tn=1024)
        da = mm(dyy, big[f"down{i}"], mode="nt", name=f"d_down{i}", outs=[MXU_DTYPE], tn=1024, tk=1024,
                after=emit(f"down{i}", gb[f"down{i}"]),
                epi=lambda acc, rv: [acc * (2.0 * jnp.sqrt(rv.astype(F32)))], extras=[("full", r)])
        gb[f"up{i}"] = mm_w(h, da, name=f"g_up{i}", outs=[MXU_DTYPE], tn=1024)
        return [mm(da, big[f"up{i}"], mode="nt", name=f"d_up{i}", outs=[F32], tn=1024, tk=1024, after=emit(f"up{i}", gb[f"up{i}"]))]

    def join(sub, dxr, dhs, after=None):
        res = _mod_ln_bwd(dxr, dhs, xs[sub], scale[sub], xs[sub - 1], ys[sub - 1], gate[sub - 1], lg[sub - 1],
                          f"mod_ln_bwd{sub}", after=after)
        red_mod[sub], red_ln[sub - 1] = res[2], res[3]
        return res[0], res[1]

    loss, dxr, dyy, red_ln[3] = _last_ln_loss_bwd(xs[3], ys[3], gate[3], lg[3], lb[3], target, "ln3_loss_bwd")
    dxr, dyy = join(3, dxr, mlp_bwd(1, h3, r1, dyy))
    gb["b_w_out"] = mm_w(o_mix, dyy, name="g_b_out", outs=[MXU_DTYPE], tn=1024, tk=1024)
    do = mm(dyy, big["b_w_out"], mode="nt", name="d_b_out", outs=[F32], tn=1024, tk=1024, after=emit("b_w_out", gb["b_w_out"]))
    parts = _combine_bwd(do, o_mix, l_g, dils, "combine_bwd")
    dhs, gq = [], None
    for g, (_, dil) in enumerate(B_PATTERNS):
        do_g, dd_g = parts[g][0].reshape(s, d), _to_streams(parts[g][1], dil)
        dqkv = _attn_bwd(qkvs[g], do_g, l_streams[g], dd_g, small["slopes"], dil, f"attn_bwd{g}")
        gq = mm_w(hg[g], dqkv, name=f"g_qkv{g}", outs=[MXU_DTYPE], tn=1024, out_col0=g * 3 * d, out_cols=len(B_PATTERNS) * 3 * d, into=gq)
        dh = mm(dqkv, big["b_w_qkv"], mode="nt", name=f"d_qkv{g}", outs=[F32], tn=1024, tk=768, b_col0=g * 3 * d)
        dhs.append(dh if dil == 1 else dh.reshape(dil, s // dil, d))
    gb["b_w_qkv"] = gq
    dxr, dyy = join(2, dxr, dhs, after=emit("b_w_qkv", gb["b_w_qkv"]))
    dxr, dyy = join(1, dxr, mlp_bwd(0, h1, r0, dyy))
    gb["a_w_out"] = mm_w(gated, dyy, name="g_a_out", outs=[MXU_DTYPE], tn=1024)
    dgated = mm(dyy, big["a_w_out"], mode="nt", name="d_a_out", outs=[F32], tn=1024, tk=1024, after=emit("a_w_out", gb["a_w_out"]))
    duv, dws, dbias, dbin, dvg, dvb = _spatial_bwd(uvpre, dgated, small["a_vn_g"], small["a_vn_b"], small["wc"],
                                                   small["wct"], small["bias_full"], "a_spatial_bwd")
    tril = jnp.tril(jnp.ones((CHUNK, CHUNK), bool))
    dws = jnp.where(tril, dws, 0.0).reshape(-1, LANES)
    gb["a_w_in"] = mm_w(h0, duv, name="g_a_in", outs=[MXU_DTYPE], tn=1024, after=emit("a_w_s", dws.astype(MXU_DTYPE)))
    dh = mm(duv, big["a_w_in"], mode="nt", name="d_a_in", outs=[F32], tn=1024, tk=512, after=emit("a_w_in", gb["a_w_in"]))
    dx, red_mod[0] = _mod_bwd(dxr, [dh], xs[0], scale[0], "mod_bwd0")
    dm = [jnp.concatenate([red_mod[i][0], red_mod[i][1], red_ln[i][2]]) for i in range(4)]
    dlg, dlb = [red_ln[i][0] for i in range(4)], [red_ln[i][1] for i in range(4)]

    gsmall = {
        "a_b_in": dbin.reshape(-1), "a_vn_g": dvg.reshape(-1), "a_vn_b": dvb.reshape(-1),
        "a_w_s": dws.reshape(-1),
        "a_b_s": dbias.reshape(CHUNK, A_GROUPS, d // A_GROUPS).sum(-1).T.reshape(-1),
    }
    return loss, dx, gb, jnp.stack(dm), jnp.stack(dlg), jnp.stack(dlb), gsmall


BIG = ("a_w_in", "a_w_out", "up0", "down0", "b_w_qkv", "b_w_out", "up1", "down1")
BIG_KIND = {"a_w_in": "col", "a_w_out": "row", "b_w_qkv": "col", "b_w_out": "row",
            "up0": "col", "up1": "col", "down0": "row", "down1": "row", "a_w_s": "all"}
HALVED = ("a_w_in", "a_w_out", "down0", "b_w_qkv")
SCATTER_GROUPS = (("down1", "up1"), ("b_w_out", "b_w_qkv"), ("down0", "up0"), ("a_w_out", "a_w_in"), ("a_w_s",))
SMALL = ("a_b_in", "a_vn_g", "a_vn_b", "a_b_s")


def kernel(x, c, ada_w, ada_b, ln_g, ln_b, a_w_in, a_b_in, a_vn_g, a_vn_b, a_w_s, a_b_s, a_w_out, b_w_qkv, b_w_out, mlp_w_up, mlp_w_down, loss_target, m_ada_w, m_ada_b, m_ln_g, m_ln_b, m_a_w_in, m_a_b_in, m_a_vn_g, m_a_vn_b, m_a_w_s, m_a_b_s, m_a_w_out, m_b_w_qkv, m_b_w_out, m_mlp_w_up, m_mlp_w_down, v_ada_w, v_ada_b, v_ln_g, v_ln_b, v_a_w_in, v_a_b_in, v_a_vn_g, v_a_vn_b, v_a_w_s, v_a_b_s, v_a_w_out, v_b_w_qkv, v_b_w_out, v_mlp_w_up, v_mlp_w_down):
    s, d = x.shape[1], x.shape[2]
    xi, yi, ci = _me()
    q = 2 * xi + yi
    dev = 2 * q + ci
    nsub = 2 * DEPTH
    cs = ada_w.shape[-1]
    ls = ln_g.shape[-1]

    shards = {
        "a_w_in": a_w_in[0], "a_w_out": a_w_out[0], "b_w_qkv": b_w_qkv[0], "b_w_out": b_w_out[0],
        "up0": mlp_w_up[0], "up1": mlp_w_up[1], "down0": mlp_w_down[0], "down1": mlp_w_down[1],
    }
    cast = [shards[k].astype(MXU_DTYPE) for k in BIG]

    pack = jnp.concatenate([c.reshape(-1), ln_g.reshape(-1), ln_b.reshape(-1)]).reshape(-1, LANES)
    got = _all_gather_small(pack, "gather_small", after=cast).reshape(N_DEV, -1)
    c_all = got[:, :d]
    per_chip = got[0::2]
    ln_g_full = per_chip[:, d:d + nsub * ls].reshape(N_CHIPS, nsub, ls).transpose(1, 0, 2).reshape(nsub, d)
    ln_b_full = per_chip[:, d + nsub * ls:].reshape(N_CHIPS, nsub, ls).transpose(1, 0, 2).reshape(nsub, d)
    m_part = _ada_fwd(c_all, ada_w.reshape(nsub, d, cs), ada_b.reshape(nsub, 1, cs), "ada_fwd")
    m_all = _all_gather_small(m_part.reshape(-1, LANES), "gather_mod").reshape(N_DEV, nsub, N_DEV, cs)
    m_mine = lax.dynamic_index_in_dim(m_all[0::2], dev, axis=2, keepdims=False)
    mvec = m_mine.transpose(1, 0, 2).reshape(nsub, 3 * d)

    halved = {BIG.index(k) for k in HALVED}
    send_sems, recv_sems, shard_thru, lands, token = _gather_start(cast, halved, mvec, "gather_start")

    def fetch(k, after):
        w = BIG.index(k)
        shard, gw = _gather_wait(w, shard_thru[w], lands[w], send_sems, recv_sems, after, f"gather_wait_{k}", w in halved)
        if w in halved:
            gw = _assemble_halves(shard, gw, f"assemble_{k}")
        return gw if BIG_KIND[k] == "col" else gw.reshape(1, -1, gw.shape[-1])

    scattering, pending = {}, {}

    def emit(k, g):
        pending[k] = g
        group = next(gr for gr in SCATTER_GROUPS if k in gr)
        if k != group[-1]:
            return None
        scattering[group] = _scatter_start([pending[m] for m in group], [BIG_KIND[m] for m in group], f"scatter_start_{k}")
        return scattering[group][2][0]

    tril = jnp.tril(jnp.ones((CHUNK, CHUNK), bool))
    wc = jnp.where(tril, a_w_s[0], 0.0).astype(MXU_DTYPE)
    heads = jnp.arange(1, B_HEADS + 1, dtype=F32)
    small = {
        "a_b_in": a_b_in, "a_vn_g": a_vn_g, "a_vn_b": a_vn_b,
        "wc": wc, "wct": wc.transpose(0, 2, 1),
        "bias_full": jnp.repeat(a_b_s[0].T, d // A_GROUPS, axis=1),
        "slopes": jnp.exp2(-8.0 * heads / B_HEADS),
    }

    loss_part, grad_x, gb, dm, dlg, dlb, gsmall = _local_step(x[0], loss_target[0], mvec, ln_g_full, ln_b_full, small, fetch, emit, token)

    weights = dict(ada_w=ada_w, ada_b=ada_b, ln_g=ln_g, ln_b=ln_b, a_w_in=a_w_in, a_b_in=a_b_in, a_vn_g=a_vn_g, a_vn_b=a_vn_b,
                   a_w_s=a_w_s, a_b_s=a_b_s, a_w_out=a_w_out, b_w_qkv=b_w_qkv, b_w_out=b_w_out, mlp_w_up=mlp_w_up, mlp_w_down=mlp_w_down)
    ms = dict(ada_w=m_ada_w, ada_b=m_ada_b, ln_g=m_ln_g, ln_b=m_ln_b, a_w_in=m_a_w_in, a_b_in=m_a_b_in, a_vn_g=m_a_vn_g, a_vn_b=m_a_vn_b,
              a_w_s=m_a_w_s, a_b_s=m_a_b_s, a_w_out=m_a_w_out, b_w_qkv=m_b_w_qkv, b_w_out=m_b_w_out, mlp_w_up=m_mlp_w_up, mlp_w_down=m_mlp_w_down)
    vs = dict(ada_w=v_ada_w, ada_b=v_ada_b, ln_g=v_ln_g, ln_b=v_ln_b, a_w_in=v_a_w_in, a_b_in=v_a_b_in, a_vn_g=v_a_vn_g, a_vn_b=v_a_vn_b,
              a_w_s=v_a_w_s, a_b_s=v_a_b_s, a_w_out=v_a_w_out, b_w_qkv=v_b_w_qkv, b_w_out=v_b_w_out, mlp_w_up=v_mlp_w_up, mlp_w_down=v_mlp_w_down)
    grads, updates = {}, {}

    def update(k):
        updates[k] = _adamw(weights[k], grads[k], ms[k], vs[k], f"adamw_{k}")
        return updates[k][0]

    gfull = {}

    def big_group(group, after):
        bufs = []
        for pair in (group[:2], group[2:]):
            bufs += _scatter_wait(*scattering[pair], [BIG_KIND[m] for m in pair], after, f"scatter_wait_{pair[-1]}")
        parts = [[i] for i, k in enumerate(group) if k == "b_w_qkv"] + [[i for i, k in enumerate(group) if k != "b_w_qkv"]]
        for part in parts:
            fulls = _sum_swap([bufs[i] for i in part], f"sum_swap_{group[part[0]]}")
            gfull.update({group[i]: f.reshape(-1, f.shape[-1]) for i, f in zip(part, fulls)})

    big_group(SCATTER_GROUPS[0] + SCATTER_GROUPS[1], grad_x)
    grads["b_w_qkv"], grads["b_w_out"] = gfull["b_w_qkv"][None], gfull["b_w_out"][None]
    update("b_w_out")
    done = update("b_w_qkv")

    pack_b = jnp.concatenate([dm.reshape(-1), dlg.reshape(-1), dlb.reshape(-1)] + [gsmall[k] for k in SMALL] + [loss_part.reshape(1)])
    n_small = pack_b.shape[0]
    pack_b = jnp.pad(pack_b, (0, -n_small % (256 * LANES)))
    got_b = _all_gather_small(pack_b.reshape(-1, LANES), "gather_small_grads", after=[done]).reshape(N_DEV, -1, LANES)
    tot = _sum_slots(got_b, "sum_small").reshape(-1)
    o = 0
    dm_tot = tot[o:o + nsub * 3 * d].reshape(nsub, 3 * d); o += nsub * 3 * d
    dlg_tot = tot[o:o + nsub * d].reshape(nsub, d); o += nsub * d
    dlb_tot = tot[o:o + nsub * d].reshape(nsub, d); o += nsub * d
    g_small = {}
    for k, ref in zip(SMALL, (a_b_in, a_vn_g, a_vn_b, a_b_s)):
        g_small[k] = tot[o:o + ref.size].reshape(ref.shape); o += ref.size
    loss = tot[o]
    assert o + 1 == n_small
    aws = _scatter_wait(*scattering[("a_w_s",)], ["all"], tot, "scatter_wait_a_w_s")[0]
    g_small["a_w_s"] = _sum_slots(aws, "sum_a_w_s").reshape(a_w_s.shape)
    dm_all = got_b.reshape(N_DEV, -1)[:, :nsub * 3 * d].reshape(N_DEV, nsub, 3 * d)
    dm_cols = lax.dynamic_slice_in_dim(dm_all, q * cs, cs, axis=2).transpose(1, 0, 2)
    grads.update({
        "ada_w": _ada_bwd(c_all.T, dm_cols, "ada_bwd").reshape(ada_w.shape),
        "ada_b": lax.dynamic_slice_in_dim(dm_tot, q * cs, cs, axis=1).reshape(ada_b.shape),
        "ln_g": lax.dynamic_slice_in_dim(dlg_tot, q * ls, ls, axis=1).reshape(ln_g.shape),
        "ln_b": lax.dynamic_slice_in_dim(dlb_tot, q * ls, ls, axis=1).reshape(ln_b.shape),
        **g_small,
    })
    for k in ("ada_b", "ln_g", "ln_b", "a_w_s") + SMALL:
        update(k)
    done = update("ada_w")

    big_group(SCATTER_GROUPS[2] + SCATTER_GROUPS[3], done)
    grads.update({
        "a_w_in": gfull["a_w_in"][None], "a_w_out": gfull["a_w_out"][None],
        "mlp_w_up": jnp.stack([gfull["up0"], gfull["up1"]]), "mlp_w_down": jnp.stack([gfull["down0"], gfull["down1"]]),
    })
    for k in ("a_w_in", "a_w_out", "mlp_w_up", "mlp_w_down"):
        update(k)
    names = list(weights)
    return (loss, grad_x[None], *[grads[k] for k in names], *[updates[k][0] for k in names],
            *[updates[k][1] for k in names], *[updates[k][2] for k in names])
```

```python
import functools
import math

import jax
import jax.numpy as jnp
from jax import lax
from jax.experimental import pallas as pl
from jax.experimental.pallas import tpu as pltpu

F32 = jnp.float32
MXU_DTYPE = jnp.bfloat16

DEPTH = 2
CHUNK = 128
A_GROUPS = 16
B_HEADS = 16
HEAD_DIM = 64
B_PATTERNS = ((128, 1), (512, 4), (2048, 16))
SPAN = 128
ALPHA = (2 * DEPTH) ** 0.25
LN_EPS = 1e-5
NEG = -1e30
ATT_SCALE = HEAD_DIM ** -0.5
ADAM_LR, ADAM_B1, ADAM_B2, ADAM_EPS, ADAM_WD, ADAM_STEP = 0.001, 0.9, 0.999, 1e-08, 0.01, 10

N_CHIPS = 4
N_DEV = 8
LANES = 128
SUBLANES = 8
VMEM_LIMIT = 52 * 1024 * 1024
ROW_TILE = 512
MM_ROW_CHUNK = 256
MESH = pl.DeviceIdType.MESH


def _cparams(sem):
    return pltpu.CompilerParams(dimension_semantics=sem, vmem_limit_bytes=VMEM_LIMIT)


def _fold8(v):
    r, c = v.shape
    return jnp.sum(v.reshape(r // SUBLANES, SUBLANES, c), axis=0)


def _gelu(x):
    c = math.sqrt(2.0 / math.pi)
    return 0.5 * x * (1.0 + jnp.tanh(c * (x + 0.044715 * (x * x * x))))


def _gelu_and_grad(x):
    c = math.sqrt(2.0 / math.pi)
    t = jnp.tanh(c * (x + 0.044715 * (x * x * x)))
    return 0.5 * x * (1.0 + t), 0.5 * (1.0 + t) + 0.5 * x * (1.0 - t * t) * c * (1.0 + 3.0 * 0.044715 * x * x)


def _dot(a, b, dims):
    return lax.dot_general(a.astype(MXU_DTYPE), b.astype(MXU_DTYPE), (dims, ((), ())), preferred_element_type=F32)


def _dot_nn(a, b):
    return _dot(a, b, ((1,), (0,)))


def _dot_nt(a, b):
    return _dot(a, b, ((1,), (1,)))


def _dot_tn(a, b):
    return _dot(a, b, ((0,), (0,)))


def _mm(a, b, *, mode, name, outs, tm, tn, tk, epi=None, extras=(), b_col0=0, n_out=None, after=None,
        out_col0=0, out_cols=None, into=None):
    if mode == "nn":
        m, kdim = a.shape
        p, kb, ns = b.shape
        assert kb == kdim and ns % tn == 0 and b_col0 % tn == 0
        n = n_out if n_out is not None else p * ns
        npt, j0 = ns // tn, b_col0 // tn
        a_spec = pl.BlockSpec((tm, tk), lambda i, j, k: (i, k))
        b_spec = pl.BlockSpec((None, tk, tn), lambda i, j, k: ((j + j0) // npt, k, (j + j0) % npt))
        dot = _dot_nn
    elif mode == "nt":
        m, kdim = a.shape
        p, n, ns = b.shape
        assert ns % tk == 0 and b_col0 % tk == 0
        npt, j0 = ns // tk, b_col0 // tk
        a_spec = pl.BlockSpec((tm, tk), lambda i, j, k: (i, k))
        b_spec = pl.BlockSpec((None, tn, tk), lambda i, j, k: ((k + j0) // npt, j, (k + j0) % npt))
        dot = _dot_nt
    else:
        kdim, m = a.shape
        kb, n = b.shape
        assert kb == kdim
        a_spec = pl.BlockSpec((tk, tm), lambda i, j, k: (k, i))
        b_spec = pl.BlockSpec((tk, tn), lambda i, j, k: (k, j))
        dot = _dot_tn
    assert m % tm == 0 and n % tn == 0 and kdim % tk == 0, (name, m, n, kdim, tm, tn, tk)
    nk = kdim // tk
    ex_specs, ex_arrays = [], []
    for kind, arr in extras:
        if kind == "row":
            ex_specs.append(pl.BlockSpec((1, tn), lambda i, j, k: (0, j)))
        else:
            ex_specs.append(pl.BlockSpec((tm, tn), lambda i, j, k: (i, j)))
        ex_arrays.append(arr)
    n_ex, n_o = len(ex_arrays), len(outs)
    deps = [d for d in (after, into) if d is not None]
    n_dep = len(deps)
    j_out = out_col0 // tn
    assert out_col0 % tn == 0 and (into is None or len(outs) == 1)

    def body(a_ref, b_ref, *rest):
        ex_refs, o_refs = rest[:n_ex], rest[n_ex + n_dep:n_ex + n_dep + n_o]
        k = pl.program_id(2)

        chunks = [slice(r0, r0 + min(tm, MM_ROW_CHUNK)) for r0 in range(0, tm, min(tm, MM_ROW_CHUNK))]

        def part(rows):
            return dot(a_ref[:, rows] if mode == "tn" else a_ref[rows, :], b_ref[...])

        def finish(r, rows):
            exs = [e[...] if kind == "row" else e[rows, :] for (kind, _), e in zip(extras, ex_refs)]
            vals = epi(r, *exs) if epi is not None else [r]
            for o, v in zip(o_refs, vals):
                o[rows, :] = v.astype(o.dtype)

        if nk == 1:
            for rows in chunks:
                finish(part(rows), rows)
            return
        acc = rest[n_ex + n_dep + n_o]

        @pl.when(k == 0)
        def _():
            for rows in chunks:
                acc[rows, :] = part(rows)

        @pl.when((k > 0) & (k < nk - 1))
        def _():
            for rows in chunks:
                acc[rows, :] += part(rows)

        @pl.when(k == nk - 1)
        def _():
            for rows in chunks:
                finish(acc[rows, :] + part(rows), rows)

    res = pl.pallas_call(
        body,
        grid=(m // tm, n // tn, nk),
        in_specs=[a_spec, b_spec] + ex_specs + [pl.BlockSpec(memory_space=pl.ANY)] * n_dep,
        out_specs=[pl.BlockSpec((tm, tn), lambda i, j, k: (i, j + j_out)) for _ in outs],
        out_shape=[jax.ShapeDtypeStruct((m, out_cols or n), dt) for dt in outs],
        input_output_aliases={} if into is None else {2 + n_ex + n_dep - 1: 0},
        scratch_shapes=[pltpu.VMEM((tm, tn), F32)] if nk > 1 else [],
        name=name,
        compiler_params=_cparams(("parallel", "parallel", "arbitrary")),
    )(a, b, *ex_arrays, *deps)
    return res if len(outs) > 1 else res[0]


def _rows(body, n_rows, tr, ins, outs, name, scratch=()):
    def spec(kind, shape):
        if kind == "blk":
            return pl.BlockSpec((tr,) + tuple(shape[1:]), lambda i: (i,) + (0,) * (len(shape) - 1))
        if kind == "dep":
            return pl.BlockSpec(memory_space=pl.ANY)
        if kind == "str":
            return pl.BlockSpec((shape[0], tr // shape[0], shape[2]), lambda i: (0, i, 0))
        return pl.BlockSpec(tuple(shape), lambda i: (0,) * len(shape))

    return pl.pallas_call(
        body,
        grid=(n_rows // tr,),
        in_specs=[spec(k, a.shape) for k, a in ins],
        out_specs=[spec(k, s) for k, s, _ in outs],
        out_shape=[jax.ShapeDtypeStruct(tuple(s), d) for _, s, d in outs],
        scratch_shapes=list(scratch),
        name=name,
        compiler_params=_cparams(("arbitrary",)),
    )(*[a for _, a in ins])


def _ln_stats(z):
    mu = jnp.mean(z, axis=-1, keepdims=True)
    zc = z - mu
    var = jnp.mean(zc * zc, axis=-1, keepdims=True)
    rstd = lax.rsqrt(var + LN_EPS)
    return zc * rstd, rstd


def _stream_scratch(c):
    return pltpu.VMEM((c // LANES, ROW_TILE, LANES), F32)


def _streams_in(ref3, scr):
    dil, n, c = ref3.shape
    for r in range(dil):
        for j in range(c // LANES):
            scr.at[j][pl.ds(r, n, stride=dil), :] = ref3[r, :, j * LANES:(j + 1) * LANES].astype(F32)
    return jnp.concatenate([scr[j] for j in range(c // LANES)], axis=1)


def _streams_out(val, ref3, scr):
    dil, n, c = ref3.shape
    for j in range(c // LANES):
        scr[j] = val[:, j * LANES:(j + 1) * LANES].astype(F32)
    for r in range(dil):
        for j in range(c // LANES):
            ref3[r, :, j * LANES:(j + 1) * LANES] = scr.at[j][pl.ds(r, n, stride=dil), :].astype(ref3.dtype)


def _mod(x, scale, shift, after, name):
    s, d = x.shape

    def body(x_ref, sc_ref, sh_ref, dep_ref, h_ref):
        h_ref[...] = (x_ref[...] * (1.0 + sc_ref[...]) + sh_ref[...]).astype(h_ref.dtype)

    return _rows(body, s, ROW_TILE, [("blk", x), ("all", scale), ("all", shift), ("dep", after)], [("blk", (s, d), MXU_DTYPE)], name)[0]


def _resid_ln(x, y, gate, g, b, nxt, name, dils=()):
    s, d = x.shape

    def body(x_ref, y_ref, gate_ref, g_ref, b_ref, sc_ref, sh_ref, xn_ref, h_ref, *rest):
        z = ALPHA * x_ref[...] + gate_ref[...] * y_ref[...]
        xhat, _ = _ln_stats(z)
        xn = xhat * g_ref[...] + b_ref[...]
        xn_ref[...] = xn
        h = xn * (1.0 + sc_ref[...]) + sh_ref[...]
        h_ref[...] = h.astype(h_ref.dtype)
        for hs_ref in rest[:len(dils)]:
            _streams_out(h, hs_ref, rest[-1])

    return _rows(body, s, ROW_TILE,
                 [("blk", x), ("blk", y), ("all", gate), ("all", g), ("all", b), ("all", nxt[0]), ("all", nxt[1])],
                 [("blk", (s, d), F32), ("blk", (s, d), MXU_DTYPE)] + [("str", (dil, s // dil, d), MXU_DTYPE) for dil in dils], name,
                 scratch=[_stream_scratch(d)] if dils else [])


def _mod_bwd(dxr, dhs, x, scale, name, after=None):
    s, d = x.shape
    n_dh = len(dhs)
    n_dep = 0 if after is None else 1

    def body(dxr_ref, *rest):
        dh_refs = rest[:n_dh]
        x_ref, sc_ref, dx_ref, red_ref, a_sh, a_sc = rest[n_dh:n_dh + 2] + rest[n_dh + 2 + n_dep:]
        i = pl.program_id(0)

        @pl.when(i == 0)
        def _():
            a_sh[...] = jnp.zeros_like(a_sh)
            a_sc[...] = jnp.zeros_like(a_sc)

        dh = dh_refs[0][...]
        for r in dh_refs[1:]:
            dh = dh + r[...]
        dx_ref[...] = dxr_ref[...] + dh * (1.0 + sc_ref[...])
        a_sh[...] += _fold8(dh)
        a_sc[...] += _fold8(dh * x_ref[...])

        @pl.when(i == pl.num_programs(0) - 1)
        def _():
            red_ref[...] = jnp.zeros_like(red_ref)
            red_ref[0:1, :] = jnp.sum(a_sh[...], axis=0, keepdims=True)
            red_ref[1:2, :] = jnp.sum(a_sc[...], axis=0, keepdims=True)

    return _rows(body, s, ROW_TILE, [("blk", dxr)] + [("blk", h) for h in dhs] + [("blk", x), ("all", scale)] + [("dep", after)] * n_dep,
                 [("blk", (s, d), F32), ("all", (SUBLANES, d), F32)], name,
                 scratch=[pltpu.VMEM((SUBLANES, d), F32)] * 2)


def _last_ln_loss_bwd(x, y, gate, g, b, target, name):
    s, d = x.shape

    def body(x_ref, y_ref, gate_ref, g_ref, b_ref, t_ref, l_ref, dxr_ref, dyy_ref, red_ref, a_l, a_g, a_b, a_gate):
        i = pl.program_id(0)

        @pl.when(i == 0)
        def _():
            for a in (a_l, a_g, a_b, a_gate):
                a[...] = jnp.zeros_like(a)

        yv = y_ref[...]
        z = ALPHA * x_ref[...] + gate_ref[...] * yv
        xhat, rstd = _ln_stats(z)
        e = xhat * g_ref[...] + b_ref[...] - t_ref[...]
        a_l[...] += _fold8(e * e)
        dxo_v = e * (1.0 / d)
        dxh = dxo_v * g_ref[...]
        dz = rstd * (dxh - jnp.mean(dxh, axis=-1, keepdims=True) - xhat * jnp.mean(dxh * xhat, axis=-1, keepdims=True))
        dxr_ref[...] = ALPHA * dz
        dyy_ref[...] = (gate_ref[...] * dz).astype(dyy_ref.dtype)
        a_g[...] += _fold8(dxo_v * xhat)
        a_b[...] += _fold8(dxo_v)
        a_gate[...] += _fold8(dz * yv)

        @pl.when(i == pl.num_programs(0) - 1)
        def _():
            l_ref[...] = jnp.full(l_ref.shape, 0.5 / d, F32) * jnp.sum(a_l[...])
            red_ref[...] = jnp.zeros_like(red_ref)
            red_ref[0:1, :] = jnp.sum(a_g[...], axis=0, keepdims=True)
            red_ref[1:2, :] = jnp.sum(a_b[...], axis=0, keepdims=True)
            red_ref[2:3, :] = jnp.sum(a_gate[...], axis=0, keepdims=True)

    l, dxr, dyy, red = _rows(
        body, s, ROW_TILE, [("blk", x), ("blk", y), ("all", gate), ("all", g), ("all", b), ("blk", target)],
        [("all", (SUBLANES, LANES), F32), ("blk", (s, d), F32), ("blk", (s, d), MXU_DTYPE), ("all", (SUBLANES, d), F32)], name,
        scratch=[pltpu.VMEM((SUBLANES, d), F32)] * 4)
    return l[0, 0], dxr, dyy, red


def _mod_ln_bwd(dxr, dhs, x, scale, x_in, y, gate, g, name, after=None):
    s, d = x.shape
    n_dh = len(dhs)
    n_dep = 0 if after is None else 1

    def body(dxr_ref, *rest):
        dh_refs = rest[:n_dh]
        x_ref, sc_ref, xin_ref, y_ref, gate_ref, g_ref = rest[n_dh:n_dh + 6]
        dxr_out, dyy_ref, red_mod, red_ln, a_sh, a_sc, a_g, a_b, a_gate = rest[n_dh + 6 + n_dep:n_dh + 15 + n_dep]
        i = pl.program_id(0)

        @pl.when(i == 0)
        def _():
            for a in (a_sh, a_sc, a_g, a_b, a_gate):
                a[...] = jnp.zeros_like(a)

        dh = dh_refs[0][...]
        for r in dh_refs[1:]:
            dh = dh + (r[...] if len(r.shape) == 2 else _streams_in(r, rest[-1]))
        xv = x_ref[...]
        dxo_v = dxr_ref[...] + dh * (1.0 + sc_ref[...])
        a_sh[...] += _fold8(dh)
        a_sc[...] += _fold8(dh * xv)
        yv = y_ref[...]
        z = ALPHA * xin_ref[...] + gate_ref[...] * yv
        xhat, rstd = _ln_stats(z)
        dxh = dxo_v * g_ref[...]
        dz = rstd * (dxh - jnp.mean(dxh, axis=-1, keepdims=True) - xhat * jnp.mean(dxh * xhat, axis=-1, keepdims=True))
        dxr_out[...] = ALPHA * dz
        dyy_ref[...] = (gate_ref[...] * dz).astype(dyy_ref.dtype)
        a_g[...] += _fold8(dxo_v * xhat)
        a_b[...] += _fold8(dxo_v)
        a_gate[...] += _fold8(dz * yv)

        @pl.when(i == pl.num_programs(0) - 1)
        def _():
            red_mod[...] = jnp.zeros_like(red_mod)
            red_mod[0:1, :] = jnp.sum(a_sh[...], axis=0, keepdims=True)
            red_mod[1:2, :] = jnp.sum(a_sc[...], axis=0, keepdims=True)
            red_ln[...] = jnp.zeros_like(red_ln)
            red_ln[0:1, :] = jnp.sum(a_g[...], axis=0, keepdims=True)
            red_ln[1:2, :] = jnp.sum(a_b[...], axis=0, keepdims=True)
            red_ln[2:3, :] = jnp.sum(a_gate[...], axis=0, keepdims=True)

    ins = ([("blk", dxr)] + [("blk" if h.ndim == 2 else "str", h) for h in dhs]
           + [("blk", x), ("all", scale), ("blk", x_in), ("blk", y), ("all", gate), ("all", g)] + [("dep", after)] * n_dep)
    return _rows(body, s, ROW_TILE, ins,
                 [("blk", (s, d), F32), ("blk", (s, d), MXU_DTYPE), ("all", (SUBLANES, d), F32), ("all", (SUBLANES, d), F32)], name,
                 scratch=[pltpu.VMEM((SUBLANES, d), F32)] * 5 + [_stream_scratch(d)] * any(h.ndim == 3 for h in dhs))


def _left_half(shape):
    return lax.broadcasted_iota(jnp.int32, shape, 1) < (LANES // 2)


CHUNKS_PER_STEP = 2


def _chunks_of_step():
    return [slice(i * CHUNK, (i + 1) * CHUNK) for i in range(CHUNKS_PER_STEP)]


def _split_groups(v):
    left = _left_half(v.shape)
    return jnp.concatenate([jnp.where(left, v, 0.0), jnp.where(left, 0.0, v)], axis=0)


def _spatial_z(vn, wc_ref, bias_ref, j):
    return _dot_nn(wc_ref[j], _split_groups(vn[:, j * LANES:(j + 1) * LANES])) + bias_ref[:, j * LANES:(j + 1) * LANES]


def _spatial_fwd(uvpre, vn_g, vn_b, wc, bias_full, name):
    s, d2 = uvpre.shape
    d = d2 // 2

    def body(uv_ref, g_ref, b_ref, wc_ref, bias_ref, out_ref):
        for rows in _chunks_of_step():
            u = _gelu(uv_ref[rows, :d])
            v = _gelu(uv_ref[rows, d:])
            vh, _ = _ln_stats(v)
            vn = vh * g_ref[...] + b_ref[...]
            for j in range(d // LANES):
                z = _spatial_z(vn, wc_ref, bias_ref, j)
                out_ref[rows, j * LANES:(j + 1) * LANES] = (u[:, j * LANES:(j + 1) * LANES] * z).astype(out_ref.dtype)

    return _rows(body, s, CHUNKS_PER_STEP * CHUNK, [("blk", uvpre), ("all", vn_g), ("all", vn_b), ("all", wc), ("all", bias_full)],
                 [("blk", (s, d), MXU_DTYPE)], name)[0]


def _spatial_bwd(uvpre, dgated, vn_g, vn_b, wc, wct, bias_full, name):
    s, d2 = uvpre.shape
    d = d2 // 2

    def body(uv_ref, dg_ref, g_ref, b_ref, wc_ref, wct_ref, bias_ref,
             duv_ref, dws_ref, dbias_ref, dbin_ref, dvg_ref, dvb_ref, dvn_buf, a_bin, a_vg, a_vb):
        i = pl.program_id(0)

        @pl.when(i == 0)
        def _():
            dws_ref[...] = jnp.zeros_like(dws_ref)
            dbias_ref[...] = jnp.zeros_like(dbias_ref)
            a_bin[...] = jnp.zeros_like(a_bin)
            a_vg[...] = jnp.zeros_like(a_vg)
            a_vb[...] = jnp.zeros_like(a_vb)

        for rows in _chunks_of_step():
            u, u_grad = _gelu_and_grad(uv_ref[rows, :d])
            v, v_grad = _gelu_and_grad(uv_ref[rows, d:])
            vh, rstd = _ln_stats(v)
            vn = vh * g_ref[...] + b_ref[...]
            dg = dg_ref[rows, :]
            dzz = dg * u
            dbias_ref[...] += dzz
            for j in range(d // LANES):
                cols = slice(j * LANES, (j + 1) * LANES)
                z = _spatial_z(vn, wc_ref, bias_ref, j)
                dup = dg[:, cols] * z * u_grad[:, cols]
                duv_ref[rows, cols] = dup.astype(duv_ref.dtype)
                a_bin[:, cols] += _fold8(dup)
                dz2 = _split_groups(dzz[:, cols])
                dvn_buf[:, cols] = _dot_nn(wct_ref[j], dz2)
                dw2 = _dot_nt(dz2, vn[:, cols])
                dws_ref[2 * j] += dw2[:CHUNK]
                dws_ref[2 * j + 1] += dw2[CHUNK:]
            dvn = dvn_buf[...]
            a_vg[...] += _fold8(dvn * vh)
            a_vb[...] += _fold8(dvn)
            dvh = dvn * g_ref[...]
            dv = rstd * (dvh - jnp.mean(dvh, axis=-1, keepdims=True) - vh * jnp.mean(dvh * vh, axis=-1, keepdims=True))
            dvp = dv * v_grad
            duv_ref[rows, d:] = dvp.astype(duv_ref.dtype)
            a_bin[:, d:] += _fold8(dvp)

        @pl.when(i == pl.num_programs(0) - 1)
        def _():
            dbin_ref[...] = jnp.sum(a_bin[...], axis=0, keepdims=True)
            dvg_ref[...] = jnp.sum(a_vg[...], axis=0, keepdims=True)
            dvb_ref[...] = jnp.sum(a_vb[...], axis=0, keepdims=True)

    return _rows(body, s, CHUNKS_PER_STEP * CHUNK,
                 [("blk", uvpre), ("blk", dgated), ("all", vn_g), ("all", vn_b), ("all", wc), ("all", wct), ("all", bias_full)],
                 [("blk", (s, d2), MXU_DTYPE), ("all", (A_GROUPS, CHUNK, CHUNK), F32), ("all", (CHUNK, d), F32),
                  ("all", (1, d2), F32), ("all", (1, d), F32), ("all", (1, d), F32)], name,
                 scratch=[pltpu.VMEM((CHUNK, d), F32), pltpu.VMEM((SUBLANES, d2), F32),
                          pltpu.VMEM((SUBLANES, d), F32), pltpu.VMEM((SUBLANES, d), F32)])


def _head_mask(v, h):
    lane = lax.broadcasted_iota(jnp.int32, v.shape, 1)
    return jnp.where((lane >= h * HEAD_DIM) & (lane < (h + 1) * HEAD_DIM), v, jnp.zeros_like(v))


def _att_bias(slopes, dil):
    qi = lax.broadcasted_iota(jnp.int32, (SPAN, SPAN), 0)
    ki = lax.broadcasted_iota(jnp.int32, (SPAN, SPAN), 1)
    sl = slopes[:, None, None]
    cur = jnp.where(ki <= qi, -sl * (float(dil) * (qi - ki).astype(F32)), NEG)
    prev = jnp.where(ki >= qi, -sl * (float(dil) * (SPAN + qi - ki).astype(F32)), NEG)
    absent = jnp.full_like(prev, NEG)
    pairs = slopes.shape[0] // 2

    def fwd(pv):
        return jnp.concatenate([cur, pv], axis=2).reshape(pairs, 2 * SPAN, 2 * SPAN)

    def bwd(pv):
        return jnp.concatenate([cur.reshape(pairs, 2 * SPAN, SPAN), pv.reshape(pairs, 2 * SPAN, SPAN)], axis=1)

    return jnp.stack([fwd(absent), fwd(prev)]), jnp.stack([bwd(absent), bwd(prev)])


ATT_GROUP = 4


def _att_group(s, dil):
    nb = s // (dil * SPAN)
    grp = min(ATT_GROUP, nb)
    assert nb % grp == 0
    return nb, grp


def _att_specs(s, d, dil, kinds):
    nb, grp = _att_group(s, dil)

    def spec(part, which):
        if which == "group":
            return pl.BlockSpec((grp * SPAN, d), lambda b: (b, part))
        if which == "prev":
            return pl.BlockSpec((SPAN, d), lambda b: (jnp.where((grp * b) % nb == 0, grp * b, grp * b - 1), part))
        return pl.BlockSpec((SPAN, d), lambda b: (jnp.where((grp * b + grp - 1) % nb == nb - 1, grp * b + grp - 1, grp * b + grp), part))

    return [spec(part, which) for part, which in kinds]


def _head_col(v, head):
    return v[:, head:head + 1]


def _expand_heads(w, j):
    shape = (w.shape[0], LANES)
    return jnp.where(_left_half(shape), jnp.broadcast_to(_head_col(w, 2 * j), shape), jnp.broadcast_to(_head_col(w, 2 * j + 1), shape))


def _attn_fwd(qkv, slopes, dil, name):
    s, d3 = qkv.shape
    d = d3 // 3
    nb, grp = _att_group(s, dil)
    table, _ = _att_bias(slopes, dil)

    def body(q_ref, k_ref, kp_ref, v_ref, vp_ref, tb_ref, o_ref, l_ref):
        b = pl.program_id(0)
        left = _left_half((SPAN, LANES))
        lane = lax.broadcasted_iota(jnp.int32, (SPAN, LANES), 1)
        for sub in range(grp):
            rows, before = slice(sub * SPAN, (sub + 1) * SPAN), slice((sub - 1) * SPAN, sub * SPAN)
            variant = jnp.where((grp * b) % nb == 0, 0, 1) if sub == 0 else 1
            lses = jnp.zeros((SPAN, LANES), F32)
            for hp in range(d // LANES):
                cols = slice(hp * LANES, (hp + 1) * LANES)
                q = q_ref[rows, cols]
                q2 = jnp.concatenate([_head_mask(q, 0), _head_mask(q, 1)], axis=0) * ATT_SCALE
                k2 = jnp.concatenate([k_ref[rows, cols], kp_ref[:, cols] if sub == 0 else k_ref[before, cols]], axis=0)
                v2 = jnp.concatenate([v_ref[rows, cols], vp_ref[:, cols] if sub == 0 else v_ref[before, cols]], axis=0)
                sc = _dot_nt(q2, k2) + tb_ref[variant, hp]
                m = jnp.max(sc, axis=-1, keepdims=True)
                p = jnp.exp(sc - m)
                l = jnp.sum(p, axis=-1, keepdims=True)
                r = _dot_nn(p, v2) * (1.0 / l)
                lse = m + jnp.log(l)
                o_ref[rows, cols] = jnp.where(left, r[:SPAN], r[SPAN:])
                lses = jnp.where(lane == 2 * hp, lse[:SPAN], jnp.where(lane == 2 * hp + 1, lse[SPAN:], lses))
            l_ref[rows, :] = lses

    specs = _att_specs(s, d, dil, [(0, "group"), (1, "group"), (1, "prev"), (2, "group"), (2, "prev")])
    return pl.pallas_call(
        body,
        grid=(s // (grp * SPAN),),
        in_specs=specs + [pl.BlockSpec(table.shape, lambda b: (0, 0, 0, 0))],
        out_specs=[pl.BlockSpec((grp * SPAN, d), lambda b: (b, 0)), pl.BlockSpec((grp * SPAN, LANES), lambda b: (b, 0))],
        out_shape=[jax.ShapeDtypeStruct((s, d), F32), jax.ShapeDtypeStruct((s, LANES), F32)],
        name=name,
        compiler_params=_cparams(("parallel",)),
    )(qkv, qkv, qkv, qkv, qkv, table)


def _attn_bwd(qkv, do, lse, dd, slopes, dil, name):
    s, d3 = qkv.shape
    d = d3 // 3
    nb, grp = _att_group(s, dil)
    _, table = _att_bias(slopes, dil)

    def cols_stacked(cur, nxt, hp):
        return jnp.concatenate([jnp.broadcast_to(_head_col(a, 2 * hp + h), (SPAN, LANES)) for a in (cur, nxt) for h in range(2)], axis=0)

    def body(k_ref, v_ref, q_ref, qn_ref, do_ref, don_ref, l_ref, ln_ref, dd_ref, ddn_ref, tb_ref, out_ref, carry):
        b = pl.program_id(0)

        @pl.when(b == 0)
        def _():
            carry[...] = jnp.zeros_like(carry)

        wide = 2 * LANES
        head_of_lane = (lax.broadcasted_iota(jnp.int32, (SPAN, wide), 1) % LANES) // HEAD_DIM
        zero = jnp.zeros((SPAN, LANES), k_ref.dtype)

        def heads_stacked2(cur, nxt):
            return jnp.concatenate([jnp.where(head_of_lane == h, a, jnp.zeros_like(a)) for a in (cur, nxt) for h in range(2)], axis=0)

        def block_diagonal(a, b):
            return jnp.concatenate([jnp.concatenate([a, zero], axis=1), jnp.concatenate([zero, b], axis=1)], axis=0)

        for sub in range(grp):
            rows, after = slice(sub * SPAN, (sub + 1) * SPAN), slice((sub + 1) * SPAN, (sub + 2) * SPAN)
            last = sub == grp - 1
            variant = jnp.where((grp * b + sub) % nb == nb - 1, 0, 1) if last else 1
            lse_c, dd_c = l_ref[rows, :], dd_ref[rows, :]
            lse_n, dd_n = (ln_ref[...], ddn_ref[...]) if last else (l_ref[after, :], dd_ref[after, :])
            for hp2 in range(d // wide):
                cols = slice(hp2 * wide, (hp2 + 1) * wide)
                pa, pb = 2 * hp2, 2 * hp2 + 1
                ca, cb = slice(pa * LANES, (pa + 1) * LANES), slice(pb * LANES, (pb + 1) * LANES)
                kbd = block_diagonal(k_ref[rows, ca], k_ref[rows, cb])
                vbd = block_diagonal(v_ref[rows, ca], v_ref[rows, cb])
                q4 = heads_stacked2(q_ref[rows, cols], qn_ref[:, cols] if last else q_ref[after, cols])
                do4 = heads_stacked2(do_ref[rows, cols], don_ref[:, cols] if last else do_ref[after, cols])
                bias = jnp.concatenate([tb_ref[variant, pa], tb_ref[variant, pb]], axis=1)
                lse2 = jnp.concatenate([cols_stacked(lse_c, lse_n, pa), cols_stacked(lse_c, lse_n, pb)], axis=1)
                dd2 = jnp.concatenate([cols_stacked(dd_c, dd_n, pa), cols_stacked(dd_c, dd_n, pb)], axis=1)
                p = jnp.exp(_dot_nt(q4 * ATT_SCALE, kbd) + bias - lse2)
                ds = p * (_dot_nt(do4, vbd) - dd2)
                dq4 = _dot_nn(ds, kbd)
                left = head_of_lane == 0
                dq_cur = jnp.where(left, dq4[:SPAN], dq4[SPAN:2 * SPAN]) + carry[:, cols]
                carry[:, cols] = jnp.where(left, dq4[2 * SPAN:3 * SPAN], dq4[3 * SPAN:])
                out_ref[rows, cols] = (dq_cur * ATT_SCALE).astype(out_ref.dtype)
                for pair, lanes in ((pa, slice(0, LANES)), (pb, slice(LANES, wide))):
                    out_ref[rows, d + pair * LANES:d + (pair + 1) * LANES] = (_dot_tn(ds[:, lanes], q4[:, lanes]) * ATT_SCALE).astype(out_ref.dtype)
                    out_ref[rows, 2 * d + pair * LANES:2 * d + (pair + 1) * LANES] = _dot_tn(p[:, lanes], do4[:, lanes]).astype(out_ref.dtype)

    qkv_specs = _att_specs(s, d, dil, [(1, "group"), (2, "group"), (0, "group"), (0, "next")])
    wide = _att_specs(s, d, dil, [(0, "group"), (0, "next")])
    heads = _att_specs(s, LANES, dil, [(0, "group"), (0, "next")])
    return pl.pallas_call(
        body,
        grid=(s // (grp * SPAN),),
        in_specs=qkv_specs + wide + heads + heads + [pl.BlockSpec(table.shape, lambda b: (0, 0, 0, 0))],
        out_specs=pl.BlockSpec((grp * SPAN, d3), lambda b: (b, 0)),
        out_shape=jax.ShapeDtypeStruct((s, d3), MXU_DTYPE),
        scratch_shapes=[pltpu.VMEM((SPAN, d), F32)],
        name=name,
        compiler_params=_cparams(("arbitrary",)),
    )(qkv, qkv, qkv, qkv, do, do, lse, lse, dd, dd, table)


def _mix_weights(l_refs):
    ls = [r[...] for r in l_refs]
    m = functools.reduce(jnp.maximum, ls)
    es = [jnp.exp(l - m) for l in ls]
    tot = functools.reduce(lambda a, c: a + c, es)
    return [e / tot for e in es]


def _combine_fwd(os_, ls_, name):
    s, d = ls_[0].shape[0], os_[0].shape[-1]
    n = len(os_)
    n_str = sum(o.ndim == 3 for o in os_)

    def body(*refs):
        o_refs, l_refs, out_ref, scrs = refs[:n], refs[n:2 * n], refs[2 * n], list(refs[2 * n + 1:])
        ws = _mix_weights(l_refs)
        os_v = [o if len(o.shape) == 2 else _streams_in(o, scrs.pop()) for o in o_refs]
        for j in range(d // LANES):
            cols = slice(j * LANES, (j + 1) * LANES)
            acc = _expand_heads(ws[0], j) * os_v[0][:, cols]
            for w, o in zip(ws[1:], os_v[1:]):
                acc = acc + _expand_heads(w, j) * o[:, cols]
            out_ref[:, cols] = acc

    return _rows(body, s, ROW_TILE, [("blk" if a.ndim == 2 else "str", a) for a in os_] + [("blk", a) for a in ls_],
                 [("blk", (s, d), F32)], name, scratch=[_stream_scratch(d)] * n_str)[0]


def _combine_bwd(do, o, ls_, dils, name):
    s, d = o.shape
    n = len(ls_)
    sel = (lax.broadcasted_iota(jnp.int32, (d, LANES), 0) // HEAD_DIM == lax.broadcasted_iota(jnp.int32, (d, LANES), 1)).astype(F32)

    def body(do_ref, o_ref, *rest):
        l_refs, sel_ref, outs = rest[:n], rest[n], rest[n + 1:n + 1 + 2 * n]
        ws = _mix_weights(l_refs)
        dov = do_ref[...]
        r = jnp.dot(dov * o_ref[...], sel_ref[...], precision=lax.Precision.HIGHEST, preferred_element_type=F32)
        for g in range(n):
            outs[2 * g + 1][...] = ws[g] * r
            parts = [_expand_heads(ws[g], j) * dov[:, j * LANES:(j + 1) * LANES] for j in range(d // LANES)]
            if dils[g] == 1:
                for j, part in enumerate(parts):
                    outs[2 * g][:, j * LANES:(j + 1) * LANES] = part.astype(outs[2 * g].dtype)
            else:
                _streams_out(jnp.concatenate(parts, axis=1), outs[2 * g], rest[-1])

    outs = []
    for dil in dils:
        outs += [("blk", (s, d), MXU_DTYPE) if dil == 1 else ("str", (dil, s // dil, d), MXU_DTYPE), ("blk", (s, LANES), F32)]
    res = _rows(body, s, ROW_TILE, [("blk", do), ("blk", o)] + [("blk", l) for l in ls_] + [("all", sel)], outs, name,
                scratch=[_stream_scratch(d)])
    return [(res[2 * g], res[2 * g + 1]) for g in range(n)]


def _ada_fwd(c_all, w, b, name):
    nsub, d, cs = w.shape

    def body(c_ref, w_ref, b_ref, o_ref):
        cv = c_ref[...]
        sc = cv * (1.0 / (1.0 + jnp.exp(-cv)))
        o_ref[...] = _dot_nn(sc, w_ref[...]) + b_ref[...]

    return pl.pallas_call(
        body,
        grid=(nsub,),
        in_specs=[pl.BlockSpec(c_all.shape, lambda i: (0, 0)), pl.BlockSpec((None, d, cs), lambda i: (i, 0, 0)),
                  pl.BlockSpec((None, 1, cs), lambda i: (i, 0, 0))],
        out_specs=pl.BlockSpec((None, N_DEV, cs), lambda i: (i, 0, 0)),
        out_shape=jax.ShapeDtypeStruct((nsub, N_DEV, cs), F32),
        name=name,
        compiler_params=_cparams(("parallel",)),
    )(c_all, w, b)


def _ada_bwd(c_all_t, dm, name):
    d, nb = c_all_t.shape
    nsub, _, cs = dm.shape

    def body(c_ref, dm_ref, o_ref):
        cv = c_ref[...]
        sc = cv * (1.0 / (1.0 + jnp.exp(-cv)))
        acc = sc[:, 0:1] * dm_ref[0:1, :]
        for bi in range(1, nb):
            acc = acc + sc[:, bi:bi + 1] * dm_ref[bi:bi + 1, :]
        o_ref[...] = acc

    return pl.pallas_call(
        body,
        grid=(nsub,),
        in_specs=[pl.BlockSpec(c_all_t.shape, lambda i: (0, 0)), pl.BlockSpec((None, nb, cs), lambda i: (i, 0, 0))],
        out_specs=pl.BlockSpec((None, d, cs), lambda i: (i, 0, 0)),
        out_shape=jax.ShapeDtypeStruct((nsub, d, cs), F32),
        name=name,
        compiler_params=_cparams(("parallel",)),
    )(c_all_t, dm)


def _row_tile(r, row_elems, block_elems=256 * 1024):
    t = 2 * SUBLANES
    if r % t:
        return r
    while t * 2 * row_elems <= block_elems and r % (t * 2) == 0:
        t *= 2
    return t


def _adamw(w, g, m, v, name):
    shape = w.shape
    c = shape[-1]
    r = w.size // c
    tr = _row_tile(r, c, 512 * 1024)
    w2, g2, m2, v2 = [a.reshape(r, c) for a in (w, g, m, v)]
    bc1 = 1.0 - ADAM_B1 ** ADAM_STEP
    bc2 = 1.0 - ADAM_B2 ** ADAM_STEP

    def body(w_ref, g_ref, m_ref, v_ref, d_ref, nm_ref, nv_ref):
        gv = g_ref[...]
        nm = ADAM_B1 * m_ref[...] + (1.0 - ADAM_B1) * gv
        nv = ADAM_B2 * v_ref[...] + (1.0 - ADAM_B2) * (gv * gv)
        d_ref[...] = -ADAM_LR * ((nm / bc1) / (jnp.sqrt(nv / bc2) + ADAM_EPS) + ADAM_WD * w_ref[...])
        nm_ref[...] = nm
        nv_ref[...] = nv

    res = _rows(body, r, tr, [("blk", a) for a in (w2, g2, m2, v2)], [("blk", (r, c), F32)] * 3, name)
    return [a.reshape(shape) for a in res]


def _sum_slots(buf, name):
    n, r, c = buf.shape
    tr = _row_tile(r, n * c, 2 * 1024 * 1024)

    def body(b_ref, o_ref):
        acc = b_ref[0].astype(F32)
        for k in range(1, n):
            acc = acc + b_ref[k].astype(F32)
        o_ref[...] = acc

    return pl.pallas_call(
        body,
        grid=(r // tr,),
        in_specs=[pl.BlockSpec((n, tr, c), lambda i: (0, i, 0))],
        out_specs=pl.BlockSpec((tr, c), lambda i: (i, 0)),
        out_shape=jax.ShapeDtypeStruct((r, c), F32),
        name=name,
        compiler_params=_cparams(("parallel",)),
    )(buf)


def _me():
    return lax.axis_index("x"), lax.axis_index("y"), lax.axis_index("c")


def _all_gather_small(blk, name, after=()):
    m_per, n = blk.shape

    def body(x_ref, *rest):
        out_ref, send_sems, recv_sems, local_sem = rest[len(after):]
        x, y, c = _me()
        me, sibling = (x, y, c), (x, y, 1 - c)
        chips = [(1 - x, y), (x, 1 - y), (1 - x, 1 - y)]

        def rows(px, py, pc):
            return out_ref.at[pl.ds((4 * px + 2 * py + pc) * m_per, m_per), :]

        def copy(k, block, to, src=None):
            return pltpu.make_async_remote_copy(
                src_ref=rows(*block) if src is None else src, dst_ref=rows(*block),
                send_sem=send_sems.at[k], recv_sem=recv_sems.at[k], device_id=to, device_id_type=MESH)

        mine = pltpu.make_async_copy(x_ref, rows(*me), local_sem)
        mine.start()
        first = [copy(0, me, sibling, src=x_ref)]
        first += [copy(1 + j, me, (*chip, c), src=x_ref) for j, chip in enumerate(chips)]
        for cp in first:
            cp.start()
        passed = [copy(4 + j, (*chip, c), sibling) for j, chip in enumerate(chips)]
        for j, chip in enumerate(chips):
            copy(1 + j, (*chip, c), me).wait_recv()
            passed[j].start()
        copy(0, sibling, me).wait_recv()
        for j, chip in enumerate(chips):
            copy(4 + j, (*chip, 1 - c), me).wait_recv()
        for cp in first + passed:
            cp.wait_send()
        mine.wait()

    return pl.pallas_call(
        body,
        out_shape=jax.ShapeDtypeStruct((N_DEV * m_per, n), blk.dtype),
        in_specs=[pl.BlockSpec(memory_space=pltpu.VMEM)] + [pl.BlockSpec(memory_space=pl.ANY)] * len(after),
        out_specs=pl.BlockSpec(memory_space=pltpu.VMEM),
        scratch_shapes=[pltpu.SemaphoreType.DMA((7,)), pltpu.SemaphoreType.DMA((7,)), pltpu.SemaphoreType.DMA],
        name=name,
        compiler_params=pltpu.CompilerParams(vmem_limit_bytes=VMEM_LIMIT),
    )(blk, *after)


_HBM = pl.BlockSpec(memory_space=pltpu.HBM)
_SEM = pl.BlockSpec(memory_space=pltpu.SEMAPHORE)
_EFFECT = pltpu.SideEffectType.DATAFLOW_SIDE_EFFECTING


def _other_chips(x, y):
    return [(1 - x, y), (x, 1 - y), (1 - x, 1 - y)]


def _gather_copy(w, j, src_ref, land_ref, send_sems, recv_sems, halved=False):
    x, y, c = _me()
    if halved:
        half = src_ref.shape[0] // 2
        src_ref = src_ref.at[pl.ds(c * half, half), :]
    return pltpu.make_async_remote_copy(
        src_ref=src_ref, dst_ref=land_ref.at[2 * x + y], send_sem=send_sems.at[3 * w + j], recv_sem=recv_sems.at[3 * w + j],
        device_id=(*_other_chips(x, y)[j], c), device_id_type=MESH)


def _gather_start(shards, halved, after, name):
    n = len(shards)
    lands = [lax.empty((N_CHIPS, s.shape[0] // 2 if w in halved else s.shape[0], s.shape[1]), s.dtype) for w, s in enumerate(shards)]

    def body(*refs):
        in_refs, land_refs = refs[:n], refs[n:2 * n]
        send_sems, recv_sems = refs[2 * n + 1], refs[2 * n + 2]
        token = refs[-1]
        for w in range(n):
            for j in range(3):
                _gather_copy(w, j, in_refs[w], land_refs[w], send_sems, recv_sems, w in halved).start()
        token[...] = jnp.zeros_like(token)

    res = pl.pallas_call(
        body,
        out_shape=(pltpu.SemaphoreType.DMA((3 * n,)), pltpu.SemaphoreType.DMA((3 * n,)),
                   *[pltpu.HBM(s.shape, s.dtype) for s in shards], *[pltpu.HBM(l.shape, l.dtype) for l in lands],
                   jax.ShapeDtypeStruct((SUBLANES, LANES), F32)),
        in_specs=[_HBM] * (2 * n) + [pl.BlockSpec(memory_space=pl.ANY)],
        out_specs=(_SEM, _SEM, *[_HBM] * (2 * n), pl.BlockSpec(memory_space=pltpu.VMEM)),
        input_output_aliases={i: 2 + i for i in range(2 * n)},
        name=name,
        compiler_params=pltpu.CompilerParams(has_side_effects=_EFFECT),
    )(*[pltpu.with_memory_space_constraint(a, pltpu.HBM) for a in list(shards) + lands], after)
    return res[0], res[1], res[2:2 + n], res[2 + n:2 + 2 * n], res[-1]


def _gather_wait(w, shard, land, send_sems, recv_sems, after, name, halved=False):
    def body(s_ref, land_ref, send_sems, recv_sems, after_ref, s_out, land_out, stage):
        x, y, _ = _me()
        if not halved:
            pltpu.sync_copy(s_ref, stage)
            pltpu.sync_copy(stage, land_out.at[2 * x + y])
        for j in range(3):
            cp = _gather_copy(w, j, s_ref, land_ref, send_sems, recv_sems, halved)
            cp.wait_send()
            cp.wait_recv()

    return pl.pallas_call(
        body,
        out_shape=(pltpu.HBM(shard.shape, shard.dtype), pltpu.HBM(land.shape, land.dtype)),
        in_specs=(_HBM, _HBM, _SEM, _SEM, pl.BlockSpec(memory_space=pl.ANY)),
        out_specs=(_HBM, _HBM),
        input_output_aliases={0: 0, 1: 1},
        scratch_shapes=[pltpu.VMEM((SUBLANES, LANES) if halved else shard.shape, shard.dtype)],
        name=name,
        compiler_params=pltpu.CompilerParams(has_side_effects=_EFFECT, vmem_limit_bytes=VMEM_LIMIT),
    )(shard, land, send_sems, recv_sems, after)


def _assemble_halves(shard, land, name):
    half = land.shape[1]

    def body(s_ref, land_ref, out_ref, send_sems, recv_sems, local_sems):
        x, y, c = _me()
        own = pltpu.make_async_copy(s_ref, out_ref.at[2 * x + y], local_sems.at[3])
        own.start()
        cps = []
        for j, (ox, oy) in enumerate(_other_chips(x, y)):
            qj = 2 * ox + oy
            mine = out_ref.at[qj, pl.ds(c * half, half), :]
            lc = pltpu.make_async_copy(land_ref.at[qj], mine, local_sems.at[j])
            lc.start()
            rc = pltpu.make_async_remote_copy(
                src_ref=land_ref.at[qj], dst_ref=mine, send_sem=send_sems.at[j], recv_sem=recv_sems.at[j],
                device_id=(x, y, 1 - c), device_id_type=MESH)
            rc.start()
            cps.append((lc, rc))
        for lc, rc in cps:
            rc.wait_recv()
        for lc, rc in cps:
            rc.wait_send()
            lc.wait()
        own.wait()

    vmem = pl.BlockSpec(memory_space=pltpu.VMEM)
    return pl.pallas_call(
        body,
        out_shape=jax.ShapeDtypeStruct((N_CHIPS,) + shard.shape, shard.dtype),
        in_specs=[vmem, vmem],
        out_specs=vmem,
        scratch_shapes=[pltpu.SemaphoreType.DMA((3,)), pltpu.SemaphoreType.DMA((3,)), pltpu.SemaphoreType.DMA((4,))],
        name=name,
        compiler_params=pltpu.CompilerParams(vmem_limit_bytes=VMEM_LIMIT),
    )(shard, land)


def _piece_shape(shape, kind):
    k, nn = shape
    if kind == "all":
        return (k, nn)
    return (k // 2, nn // N_CHIPS) if kind == "col" else (k // N_CHIPS // 2, nn)


def _piece_of(g_ref, kind, tq, tc):
    pr, pc = _piece_shape(g_ref.shape, kind)
    if kind == "all":
        return g_ref
    if kind == "col":
        return g_ref.at[pl.ds(tc * pr, pr), pl.ds(tq * pc, pc)]
    return g_ref.at[pl.ds((2 * tq + tc) * pr, pr), :]


def _scatter_copy(w, r, kind, g_ref, land_ref, send_sems, recv_sems):
    x, y, c = _me()
    tx, ty, tc = (x + ((r >> 2) & 1)) % 2, (y + ((r >> 1) & 1)) % 2, (c + (r & 1)) % 2
    return pltpu.make_async_remote_copy(
        src_ref=_piece_of(g_ref, kind, 2 * tx + ty, tc), dst_ref=land_ref.at[4 * x + 2 * y + c],
        send_sem=send_sems.at[N_DEV * w + r], recv_sem=recv_sems.at[N_DEV * w + r], device_id=(tx, ty, tc), device_id_type=MESH)


def _scatter_start(gs, kinds, name):
    n = len(gs)
    pieces = [_piece_shape(g.shape, kind) for g, kind in zip(gs, kinds)]
    lands = [lax.empty((N_DEV,) + p, g.dtype) for p, g in zip(pieces, gs)]

    def body(*refs):
        g_refs, land_refs, send_sems, recv_sems = refs[:n], refs[n:2 * n], refs[2 * n], refs[2 * n + 1]
        land_outs, stages = refs[3 * n + 2:4 * n + 2], refs[4 * n + 2:]
        x, y, c = _me()
        for w in range(n):
            for r in range(1, N_DEV):
                _scatter_copy(w, r, kinds[w], g_refs[w], land_refs[w], send_sems, recv_sems).start()
        for w in range(n):
            pltpu.sync_copy(_piece_of(g_refs[w], kinds[w], 2 * x + y, c), stages[w])
            pltpu.sync_copy(stages[w], land_outs[w].at[4 * x + 2 * y + c])

    arrays = list(gs) + lands
    res = pl.pallas_call(
        body,
        out_shape=(pltpu.SemaphoreType.DMA((N_DEV * n,)), pltpu.SemaphoreType.DMA((N_DEV * n,)),
                   *[pltpu.HBM(a.shape, a.dtype) for a in arrays]),
        in_specs=[_HBM] * (2 * n),
        out_specs=(_SEM, _SEM, *[_HBM] * (2 * n)),
        input_output_aliases={i: 2 + i for i in range(2 * n)},
        scratch_shapes=[pltpu.VMEM(p, g.dtype) for p, g in zip(pieces, gs)],
        name=name,
        compiler_params=pltpu.CompilerParams(has_side_effects=_EFFECT, vmem_limit_bytes=VMEM_LIMIT),
    )(*[pltpu.with_memory_space_constraint(a, pltpu.HBM) for a in arrays])
    return res[0], res[1], res[2:2 + n], res[2 + n:]


def _scatter_wait(send_sems, recv_sems, gs, lands, kinds, after, name):
    n = len(gs)

    def body(*refs):
        g_refs, land_refs, send_sems, recv_sems = refs[:n], refs[n:2 * n], refs[2 * n], refs[2 * n + 1]
        for w in range(n):
            for r in range(1, N_DEV):
                cp = _scatter_copy(w, r, kinds[w], g_refs[w], land_refs[w], send_sems, recv_sems)
                cp.wait_send()
                cp.wait_recv()

    arrays = list(gs) + list(lands)
    return pl.pallas_call(
        body,
        out_shape=tuple(pltpu.HBM(a.shape, a.dtype) for a in arrays),
        in_specs=(*[_HBM] * (2 * n), _SEM, _SEM, pl.BlockSpec(memory_space=pl.ANY)),
        out_specs=tuple([_HBM] * (2 * n)),
        input_output_aliases={i: i for i in range(2 * n)},
        name=name,
        compiler_params=pltpu.CompilerParams(has_side_effects=_EFFECT),
    )(*arrays, send_sems, recv_sems, after)[n:]


def _sum_swap(bufs, name):
    n = len(bufs)

    def body(*refs):
        in_refs, out_refs = refs[:n], refs[n:2 * n]
        send_sems, recv_sems = refs[2 * n:]
        x, y, c = _me()
        cps = []
        for w in range(n):
            slots, r, _ = bufs[w].shape
            mine = out_refs[w].at[c]
            for r0 in range(0, r, min(r, ROW_TILE)):
                rows = slice(r0, r0 + min(r, ROW_TILE))
                acc = in_refs[w][0, rows, :].astype(F32)
                for k in range(1, slots):
                    acc = acc + in_refs[w][k, rows, :].astype(F32)
                mine[rows, :] = acc
            rc = pltpu.make_async_remote_copy(
                src_ref=mine, dst_ref=mine, send_sem=send_sems.at[w], recv_sem=recv_sems.at[w],
                device_id=(x, y, 1 - c), device_id_type=MESH)
            rc.start()
            cps.append(rc)
        for rc in cps:
            rc.wait_recv()
        for rc in cps:
            rc.wait_send()

    vmem = pl.BlockSpec(memory_space=pltpu.VMEM)
    return pl.pallas_call(
        body,
        out_shape=[jax.ShapeDtypeStruct((2,) + b.shape[1:], F32) for b in bufs],
        in_specs=[vmem] * n,
        out_specs=[vmem] * n,
        scratch_shapes=[pltpu.SemaphoreType.DMA((n,)), pltpu.SemaphoreType.DMA((n,))],
        name=name,
        compiler_params=pltpu.CompilerParams(vmem_limit_bytes=VMEM_LIMIT),
    )(*bufs)


def _side_by_side(w):
    g, t, _ = w.shape
    return w.reshape(g // 2, 2, t, t).transpose(0, 2, 1, 3).reshape(g // 2, t, 2 * t)


def _to_streams(a, dil):
    if dil == 1:
        return a
    s, c = a.shape
    return a.reshape(s // dil, dil, c).transpose(1, 0, 2).reshape(s, c)


def _from_streams(a, dil):
    if dil == 1:
        return a
    s, c = a.shape
    return a.reshape(dil, s // dil, c).transpose(1, 0, 2).reshape(s, c)


def _mm_tiles(s):
    return min(s, 2048)


def _local_step(x0, target, mvec, ln_g, ln_b, small, fetch, emit, start):
    s, d = x0.shape
    tm = _mm_tiles(s)
    row = lambda v: v.reshape(1, -1)
    shift = [row(mvec[i, :d]) for i in range(4)]
    scale = [row(mvec[i, d:2 * d]) for i in range(4)]
    gate = [row(1.0 + mvec[i, 2 * d:]) for i in range(4)]
    lg = [row(ln_g[i]) for i in range(4)]
    lb = [row(ln_b[i]) for i in range(4)]
    mm = functools.partial(_mm, tm=tm)
    mm_w = functools.partial(_mm, tm=1024, tk=min(s, 2048), mode="tn")

    def resid_ln_epilogue(sub):
        def epi(y, xv, gate_v, g_v, b_v, sc_v, sh_v):
            xhat, _ = _ln_stats(ALPHA * xv + gate_v * y)
            xn = xhat * g_v + b_v
            return [y, xn, xn * (1.0 + sc_v) + sh_v]

        rows = [gate[sub], lg[sub], lb[sub], scale[sub + 1], shift[sub + 1]]
        return dict(outs=[F32, F32, MXU_DTYPE], epi=epi, extras=[("full", xs[sub])] + [("row", r) for r in rows])

    xs, ys, big = [x0], [], {}
    h0 = _mod(x0, scale[0], shift[0], start, "mod0")
    big["a_w_in"] = fetch("a_w_in", h0)
    uvpre = mm(h0, big["a_w_in"], mode="nn", name="a_in", outs=[F32], tn=512, tk=1024,
               epi=lambda r, bias: [r + bias], extras=[("row", small["a_b_in"])])
    gated = _spatial_fwd(uvpre, small["a_vn_g"], small["a_vn_b"], small["wc"], small["bias_full"], "a_spatial")
    big["a_w_out"] = fetch("a_w_out", gated)
    y0, x1, h1 = mm(gated, big["a_w_out"], mode="nn", name="a_out", tm=min(s, 1024), tn=d, tk=1024, **resid_ln_epilogue(0))
    ys.append(y0)
    xs.append(x1)
    relu2 = lambda r: [jnp.square(jnp.maximum(r, 0.0))]
    big["up0"] = fetch("up0", h1)
    r0 = mm(h1, big["up0"], mode="nn", name="up0", outs=[MXU_DTYPE], tn=1024, tk=1024, epi=relu2)
    big["down0"] = fetch("down0", r0)
    ys.append(mm(r0, big["down0"], mode="nn", name="down0", outs=[F32], tm=min(s, 1024), tn=1024, tk=2048))
    dils = [dil for _, dil in B_PATTERNS]
    x2, h2, *h2_streams = _resid_ln(xs[1], ys[1], gate[1], lg[1], lb[1], (scale[2], shift[2]), "ln1", [dil for dil in dils if dil > 1])
    h2_streams = [h2] + [a.reshape(s, d) for a in h2_streams]
    xs.append(x2)
    hg, qkvs, o_g, l_g, l_streams = [], [], [], [], []
    big["b_w_qkv"] = fetch("b_w_qkv", h2)
    for g, (_, dil) in enumerate(B_PATTERNS):
        hp = h2_streams[g]
        qkv = mm(hp, big["b_w_qkv"], mode="nn", name=f"qkv{g}", outs=[MXU_DTYPE], tn=768, tk=1024, b_col0=g * 3 * d, n_out=3 * d)
        og, lgv = _attn_fwd(qkv, small["slopes"], dil, f"attn_fwd{g}")
        hg.append(hp)
        qkvs.append(qkv)
        o_g.append(og if dil == 1 else og.reshape(dil, s // dil, d))
        l_g.append(_from_streams(lgv, dil))
        l_streams.append(lgv)
    o_mix = _combine_fwd(o_g, l_g, "combine")
    big["b_w_out"] = fetch("b_w_out", o_mix)
    y2, x3, h3 = mm(o_mix, big["b_w_out"], mode="nn", name="b_out", tm=min(s, 1024), tn=d, tk=1024, **resid_ln_epilogue(2))
    ys.append(y2)
    xs.append(x3)
    big["up1"] = fetch("up1", h3)
    r1 = mm(h3, big["up1"], mode="nn", name="up1", outs=[MXU_DTYPE], tn=1024, tk=1024, epi=relu2)
    big["down1"] = fetch("down1", r1)
    ys.append(mm(r1, big["down1"], mode="nn", name="down1", outs=[F32], tm=min(s, 1024), tn=1024, tk=2048))

    gb, red_ln, red_mod = {}, [None] * 4, [None] * 4

    def mlp_bwd(i, h, r, dyy):
        gb[f"down{i}"] = mm_w(r, dyy, name=f"g_down{i}", outs=[MXU_DTYPE], tn=1024)
        da = mm(dyy, big[f"down{i}"], mode="nt", name=f"d_down{i}", outs=[MXU_DTYPE], tn=1024, tk=1024,
                after=emit(f"down{i}", gb[f"down{i}"]),
                epi=lambda acc, rv: [acc * (2.0 * jnp.sqrt(rv.astype(F32)))], extras=[("full", r)])
        gb[f"up{i}"] = mm_w(h, da, name=f"g_up{i}", outs=[MXU_DTYPE], tn=1024)
        return [mm(da, big[f"up{i}"], mode="nt", name=f"d_up{i}", outs=[F32], tn=1024, tk=1024, after=emit(f"up{i}", gb[f"up{i}"]))]

    def join(sub, dxr, dhs, after=None):
        res = _mod_ln_bwd(dxr, dhs, xs[sub], scale[sub], xs[sub - 1], ys[sub - 1], gate[sub - 1], lg[sub - 1],
                          f"mod_ln_bwd{sub}", after=after)
        red_mod[sub], red_ln[sub - 1] = res[2], res[3]
        return res[0], res[1]

    loss, dxr, dyy, red_ln[3] = _last_ln_loss_bwd(xs[3], ys[3], gate[3], lg[3], lb[3], target, "ln3_loss_bwd")
    dxr, dyy = join(3, dxr, mlp_bwd(1, h3, r1, dyy))
    gb["b_w_out"] = mm_w(o_mix, dyy, name="g_b_out", outs=[MXU_DTYPE], tn=1024, tk=1024)
    do = mm(dyy, big["b_w_out"], mode="nt", name="d_b_out", outs=[F32], tn=1024, tk=1024, after=emit("b_w_out", gb["b_w_out"]))
    parts = _combine_bwd(do, o_mix, l_g, dils, "combine_bwd")
    dhs, gq = [], None
    for g, (_, dil) in enumerate(B_PATTERNS):
        do_g, dd_g = parts[g][0].reshape(s, d), _to_streams(parts[g][1], dil)
        dqkv = _attn_bwd(qkvs[g], do_g, l_streams[g], dd_g, small["slopes"], dil, f"attn_bwd{g}")
        gq = mm_w(hg[g], dqkv, name=f"g_qkv{g}", outs=[MXU_DTYPE], tn=1024, out_col0=g * 3 * d, out_cols=len(B_PATTERNS) * 3 * d, into=gq)
        dh = mm(dqkv, big["b_w_qkv"], mode="nt", name=f"d_qkv{g}", outs=[F32], tn=1024, tk=768, b_col0=g * 3 * d)
        dhs.append(dh if dil == 1 else dh.reshape(dil, s // dil, d))
    gb["b_w_qkv"] = gq
    dxr, dyy = join(2, dxr, dhs, after=emit("b_w_qkv", gb["b_w_qkv"]))
    dxr, dyy = join(1, dxr, mlp_bwd(0, h1, r0, dyy))
    gb["a_w_out"] = mm_w(gated, dyy, name="g_a_out", outs=[MXU_DTYPE], tn=1024)
    dgated = mm(dyy, big["a_w_out"], mode="nt", name="d_a_out", outs=[F32], tn=1024, tk=1024, after=emit("a_w_out", gb["a_w_out"]))
    duv, dws, dbias, dbin, dvg, dvb = _spatial_bwd(uvpre, dgated, small["a_vn_g"], small["a_vn_b"], small["wc"],
                                                   small["wct"], small["bias_full"], "a_spatial_bwd")
    tril = jnp.tril(jnp.ones((CHUNK, CHUNK), bool))
    dws = jnp.where(tril, dws, 0.0).reshape(-1, LANES)
    gb["a_w_in"] = mm_w(h0, duv, name="g_a_in", outs=[MXU_DTYPE], tn=1024, after=emit("a_w_s", dws.astype(MXU_DTYPE)))
    dh = mm(duv, big["a_w_in"], mode="nt", name="d_a_in", outs=[F32], tn=1024, tk=512, after=emit("a_w_in", gb["a_w_in"]))
    dx, red_mod[0] = _mod_bwd(dxr, [dh], xs[0], scale[0], "mod_bwd0")
    dm = [jnp.concatenate([red_mod[i][0], red_mod[i][1], red_ln[i][2]]) for i in range(4)]
    dlg, dlb = [red_ln[i][0] for i in range(4)], [red_ln[i][1] for i in range(4)]

    gsmall = {
        "a_b_in": dbin.reshape(-1), "a_vn_g": dvg.reshape(-1), "a_vn_b": dvb.reshape(-1),
        "a_w_s": dws.reshape(-1),
        "a_b_s": dbias.reshape(CHUNK, A_GROUPS, d // A_GROUPS).sum(-1).T.reshape(-1),
    }
    return loss, dx, gb, jnp.stack(dm), jnp.stack(dlg), jnp.stack(dlb), gsmall


BIG = ("a_w_in", "a_w_out", "up0", "down0", "b_w_qkv", "b_w_out", "up1", "down1")
BIG_KIND = {"a_w_in": "col", "a_w_out": "row", "b_w_qkv": "col", "b_w_out": "row",
            "up0": "col", "up1": "col", "down0": "row", "down1": "row", "a_w_s": "all"}
HALVED = ("a_w_in", "a_w_out", "down0", "b_w_qkv")
SCATTER_GROUPS = (("down1", "up1"), ("b_w_out", "b_w_qkv"), ("down0", "up0"), ("a_w_out", "a_w_in"), ("a_w_s",))
SMALL = ("a_b_in", "a_vn_g", "a_vn_b", "a_b_s")


def kernel(x, c, ada_w, ada_b, ln_g, ln_b, a_w_in, a_b_in, a_vn_g, a_vn_b, a_w_s, a_b_s, a_w_out, b_w_qkv, b_w_out, mlp_w_up, mlp_w_down, loss_target, m_ada_w, m_ada_b, m_ln_g, m_ln_b, m_a_w_in, m_a_b_in, m_a_vn_g, m_a_vn_b, m_a_w_s, m_a_b_s, m_a_w_out, m_b_w_qkv, m_b_w_out, m_mlp_w_up, m_mlp_w_down, v_ada_w, v_ada_b, v_ln_g, v_ln_b, v_a_w_in, v_a_b_in, v_a_vn_g, v_a_vn_b, v_a_w_s, v_a_b_s, v_a_w_out, v_b_w_qkv, v_b_w_out, v_mlp_w_up, v_mlp_w_down):
    s, d = x.shape[1], x.shape[2]
    xi, yi, ci = _me()
    q = 2 * xi + yi
    dev = 2 * q + ci
    nsub = 2 * DEPTH
    cs = ada_w.shape[-1]
    ls = ln_g.shape[-1]

    shards = {
        "a_w_in": a_w_in[0], "a_w_out": a_w_out[0], "b_w_qkv": b_w_qkv[0], "b_w_out": b_w_out[0],
        "up0": mlp_w_up[0], "up1": mlp_w_up[1], "down0": mlp_w_down[0], "down1": mlp_w_down[1],
    }
    cast = [shards[k].astype(MXU_DTYPE) for k in BIG]

    pack = jnp.concatenate([c.reshape(-1), ln_g.reshape(-1), ln_b.reshape(-1)]).reshape(-1, LANES)
    got = _all_gather_small(pack, "gather_small", after=cast).reshape(N_DEV, -1)
    c_all = got[:, :d]
    per_chip = got[0::2]
    ln_g_full = per_chip[:, d:d + nsub * ls].reshape(N_CHIPS, nsub, ls).transpose(1, 0, 2).reshape(nsub, d)
    ln_b_full = per_chip[:, d + nsub * ls:].reshape(N_CHIPS, nsub, ls).transpose(1, 0, 2).reshape(nsub, d)
    m_part = _ada_fwd(c_all, ada_w.reshape(nsub, d, cs), ada_b.reshape(nsub, 1, cs), "ada_fwd")
    m_all = _all_gather_small(m_part.reshape(-1, LANES), "gather_mod").reshape(N_DEV, nsub, N_DEV, cs)
    m_mine = lax.dynamic_index_in_dim(m_all[0::2], dev, axis=2, keepdims=False)
    mvec = m_mine.transpose(1, 0, 2).reshape(nsub, 3 * d)

    halved = {BIG.index(k) for k in HALVED}
    send_sems, recv_sems, shard_thru, lands, token = _gather_start(cast, halved, mvec, "gather_start")

    def fetch(k, after):
        w = BIG.index(k)
        shard, gw = _gather_wait(w, shard_thru[w], lands[w], send_sems, recv_sems, after, f"gather_wait_{k}", w in halved)
        if w in halved:
            gw = _assemble_halves(shard, gw, f"assemble_{k}")
        return gw if BIG_KIND[k] == "col" else gw.reshape(1, -1, gw.shape[-1])

    scattering, pending = {}, {}

    def emit(k, g):
        pending[k] = g
        group = next(gr for gr in SCATTER_GROUPS if k in gr)
        if k != group[-1]:
            return None
        scattering[group] = _scatter_start([pending[m] for m in group], [BIG_KIND[m] for m in group], f"scatter_start_{k}")
        return scattering[group][2][0]

    tril = jnp.tril(jnp.ones((CHUNK, CHUNK), bool))
    wc = jnp.where(tril, a_w_s[0], 0.0).astype(MXU_DTYPE)
    heads = jnp.arange(1, B_HEADS + 1, dtype=F32)
    small = {
        "a_b_in": a_b_in, "a_vn_g": a_vn_g, "a_vn_b": a_vn_b,
        "wc": _side_by_side(wc), "wct": _side_by_side(wc.transpose(0, 2, 1)),
        "bias_full": jnp.repeat(a_b_s[0].T, d // A_GROUPS, axis=1),
        "slopes": jnp.exp2(-8.0 * heads / B_HEADS),
    }

    loss_part, grad_x, gb, dm, dlg, dlb, gsmall = _local_step(x[0], loss_target[0], mvec, ln_g_full, ln_b_full, small, fetch, emit, token)

    weights = dict(ada_w=ada_w, ada_b=ada_b, ln_g=ln_g, ln_b=ln_b, a_w_in=a_w_in, a_b_in=a_b_in, a_vn_g=a_vn_g, a_vn_b=a_vn_b,
                   a_w_s=a_w_s, a_b_s=a_b_s, a_w_out=a_w_out, b_w_qkv=b_w_qkv, b_w_out=b_w_out, mlp_w_up=mlp_w_up, mlp_w_down=mlp_w_down)
    ms = dict(ada_w=m_ada_w, ada_b=m_ada_b, ln_g=m_ln_g, ln_b=m_ln_b, a_w_in=m_a_w_in, a_b_in=m_a_b_in, a_vn_g=m_a_vn_g, a_vn_b=m_a_vn_b,
              a_w_s=m_a_w_s, a_b_s=m_a_b_s, a_w_out=m_a_w_out, b_w_qkv=m_b_w_qkv, b_w_out=m_b_w_out, mlp_w_up=m_mlp_w_up, mlp_w_down=m_mlp_w_down)
    vs = dict(ada_w=v_ada_w, ada_b=v_ada_b, ln_g=v_ln_g, ln_b=v_ln_b, a_w_in=v_a_w_in, a_b_in=v_a_b_in, a_vn_g=v_a_vn_g, a_vn_b=v_a_vn_b,
              a_w_s=v_a_w_s, a_b_s=v_a_b_s, a_w_out=v_a_w_out, b_w_qkv=v_b_w_qkv, b_w_out=v_b_w_out, mlp_w_up=v_mlp_w_up, mlp_w_down=v_mlp_w_down)
    grads, updates = {}, {}

    def update(k):
        updates[k] = _adamw(weights[k], grads[k], ms[k], vs[k], f"adamw_{k}")
        return updates[k][0]

    gfull = {}

    def big_group(group, after):
        bufs = []
        for pair in (group[:2], group[2:]):
            bufs += _scatter_wait(*scattering[pair], [BIG_KIND[m] for m in pair], after, f"scatter_wait_{pair[-1]}")
        parts = [[i] for i, k in enumerate(group) if k == "b_w_qkv"] + [[i for i, k in enumerate(group) if k != "b_w_qkv"]]
        for part in parts:
            fulls = _sum_swap([bufs[i] for i in part], f"sum_swap_{group[part[0]]}")
            gfull.update({group[i]: f.reshape(-1, f.shape[-1]) for i, f in zip(part, fulls)})

    big_group(SCATTER_GROUPS[0] + SCATTER_GROUPS[1], grad_x)
    grads["b_w_qkv"], grads["b_w_out"] = gfull["b_w_qkv"][None], gfull["b_w_out"][None]
    update("b_w_out")
    done = update("b_w_qkv")

    pack_b = jnp.concatenate([dm.reshape(-1), dlg.reshape(-1), dlb.reshape(-1)] + [gsmall[k] for k in SMALL] + [loss_part.reshape(1)])
    n_small = pack_b.shape[0]
    pack_b = jnp.pad(pack_b, (0, -n_small % (256 * LANES)))
    got_b = _all_gather_small(pack_b.reshape(-1, LANES), "gather_small_grads", after=[done]).reshape(N_DEV, -1, LANES)
    tot = _sum_slots(got_b, "sum_small").reshape(-1)
    o = 0
    dm_tot = tot[o:o + nsub * 3 * d].reshape(nsub, 3 * d); o += nsub * 3 * d
    dlg_tot = tot[o:o + nsub * d].reshape(nsub, d); o += nsub * d
    dlb_tot = tot[o:o + nsub * d].reshape(nsub, d); o += nsub * d
    g_small = {}
    for k, ref in zip(SMALL, (a_b_in, a_vn_g, a_vn_b, a_b_s)):
        g_small[k] = tot[o:o + ref.size].reshape(ref.shape); o += ref.size
    loss = tot[o]
    assert o + 1 == n_small
    aws = _scatter_wait(*scattering[("a_w_s",)], ["all"], tot, "scatter_wait_a_w_s")[0]
    g_small["a_w_s"] = _sum_slots(aws, "sum_a_w_s").reshape(a_w_s.shape)
    dm_all = got_b.reshape(N_DEV, -1)[:, :nsub * 3 * d].reshape(N_DEV, nsub, 3 * d)
    dm_cols = lax.dynamic_slice_in_dim(dm_all, q * cs, cs, axis=2).transpose(1, 0, 2)
    grads.update({
        "ada_w": _ada_bwd(c_all.T, dm_cols, "ada_bwd").reshape(ada_w.shape),
        "ada_b": lax.dynamic_slice_in_dim(dm_tot, q * cs, cs, axis=1).reshape(ada_b.shape),
        "ln_g": lax.dynamic_slice_in_dim(dlg_tot, q * ls, ls, axis=1).reshape(ln_g.shape),
        "ln_b": lax.dynamic_slice_in_dim(dlb_tot, q * ls, ls, axis=1).reshape(ln_b.shape),
        **g_small,
    })
    for k in ("ada_b", "ln_g", "ln_b", "a_w_s") + SMALL:
        update(k)
    done = update("ada_w")

    big_group(SCATTER_GROUPS[2] + SCATTER_GROUPS[3], done)
    grads.update({
        "a_w_in": gfull["a_w_in"][None], "a_w_out": gfull["a_w_out"][None],
        "mlp_w_up": jnp.stack([gfull["up0"], gfull["up1"]]), "mlp_w_down": jnp.stack([gfull["down0"], gfull["down1"]]),
    })
    for k in ("a_w_in", "a_w_out", "mlp_w_up", "mlp_w_down"):
        update(k)
    names = list(weights)
    return (loss, grad_x[None], *[grads[k] for k in names], *[updates[k][0] for k in names],
            *[updates[k][1] for k in names], *[updates[k][2] for k in names])
```

```python
import functools
import math

import jax
import jax.numpy as jnp
from jax import lax
from jax.experimental import pallas as pl
from jax.experimental.pallas import tpu as pltpu

F32 = jnp.float32
MXU_DTYPE = jnp.bfloat16

DEPTH = 2
CHUNK = 128
A_GROUPS = 16
B_HEADS = 16
HEAD_DIM = 64
B_PATTERNS = ((128, 1), (512, 4), (2048, 16))
SPAN = 128
ALPHA = (2 * DEPTH) ** 0.25
LN_EPS = 1e-5
NEG = -1e30
ATT_SCALE = HEAD_DIM ** -0.5
ADAM_LR, ADAM_B1, ADAM_B2, ADAM_EPS, ADAM_WD, ADAM_STEP = 0.001, 0.9, 0.999, 1e-08, 0.01, 10

N_CHIPS = 4
N_DEV = 8
LANES = 128
SUBLANES = 8
VMEM_LIMIT = 52 * 1024 * 1024
ROW_TILE = 512
MM_ROW_CHUNK = 256
MESH = pl.DeviceIdType.MESH


def _cparams(sem):
    return pltpu.CompilerParams(dimension_semantics=sem, vmem_limit_bytes=VMEM_LIMIT)


def _fold8(v):
    r, c = v.shape
    return jnp.sum(v.reshape(r // SUBLANES, SUBLANES, c), axis=0)


def _gelu(x):
    c = math.sqrt(2.0 / math.pi)
    return 0.5 * x * (1.0 + jnp.tanh(c * (x + 0.044715 * (x * x * x))))


def _gelu_and_grad(x):
    c = math.sqrt(2.0 / math.pi)
    t = jnp.tanh(c * (x + 0.044715 * (x * x * x)))
    return 0.5 * x * (1.0 + t), 0.5 * (1.0 + t) + 0.5 * x * (1.0 - t * t) * c * (1.0 + 3.0 * 0.044715 * x * x)


def _dot(a, b, dims):
    return lax.dot_general(a.astype(MXU_DTYPE), b.astype(MXU_DTYPE), (dims, ((), ())), preferred_element_type=F32)


def _dot_nn(a, b):
    return _dot(a, b, ((1,), (0,)))


def _dot_nt(a, b):
    return _dot(a, b, ((1,), (1,)))


def _dot_tn(a, b):
    return _dot(a, b, ((0,), (0,)))


def _mm(a, b, *, mode, name, outs, tm, tn, tk, epi=None, extras=(), b_col0=0, n_out=None, after=None,
        out_col0=0, out_cols=None, into=None):
    if mode == "nn":
        m, kdim = a.shape
        p, kb, ns = b.shape
        assert kb == kdim and ns % tn == 0 and b_col0 % tn == 0
        n = n_out if n_out is not None else p * ns
        npt, j0 = ns // tn, b_col0 // tn
        a_spec = pl.BlockSpec((tm, tk), lambda i, j, k: (i, k))
        b_spec = pl.BlockSpec((None, tk, tn), lambda i, j, k: ((j + j0) // npt, k, (j + j0) % npt))
        dot = _dot_nn
    elif mode == "nt":
        m, kdim = a.shape
        p, n, ns = b.shape
        assert ns % tk == 0 and b_col0 % tk == 0
        npt, j0 = ns // tk, b_col0 // tk
        a_spec = pl.BlockSpec((tm, tk), lambda i, j, k: (i, k))
        b_spec = pl.BlockSpec((None, tn, tk), lambda i, j, k: ((k + j0) // npt, j, (k + j0) % npt))
        dot = _dot_nt
    else:
        kdim, m = a.shape
        kb, n = b.shape
        assert kb == kdim
        a_spec = pl.BlockSpec((tk, tm), lambda i, j, k: (k, i))
        b_spec = pl.BlockSpec((tk, tn), lambda i, j, k: (k, j))
        dot = _dot_tn
    assert m % tm == 0 and n % tn == 0 and kdim % tk == 0, (name, m, n, kdim, tm, tn, tk)
    nk = kdim // tk
    ex_specs, ex_arrays = [], []
    for kind, arr in extras:
        if kind == "row":
            ex_specs.append(pl.BlockSpec((1, tn), lambda i, j, k: (0, j)))
        else:
            ex_specs.append(pl.BlockSpec((tm, tn), lambda i, j, k: (i, j)))
        ex_arrays.append(arr)
    n_ex, n_o = len(ex_arrays), len(outs)
    deps = [d for d in (after, into) if d is not None]
    n_dep = len(deps)
    j_out = out_col0 // tn
    assert out_col0 % tn == 0 and (into is None or len(outs) == 1)

    def body(a_ref, b_ref, *rest):
        ex_refs, o_refs = rest[:n_ex], rest[n_ex + n_dep:n_ex + n_dep + n_o]
        k = pl.program_id(2)

        chunks = [slice(r0, r0 + min(tm, MM_ROW_CHUNK)) for r0 in range(0, tm, min(tm, MM_ROW_CHUNK))]

        def part(rows):
            return dot(a_ref[:, rows] if mode == "tn" else a_ref[rows, :], b_ref[...])

        def finish(r, rows):
            exs = [e[...] if kind == "row" else e[rows, :] for (kind, _), e in zip(extras, ex_refs)]
            vals = epi(r, *exs) if epi is not None else [r]
            for o, v in zip(o_refs, vals):
                o[rows, :] = v.astype(o.dtype)

        if nk == 1:
            for rows in chunks:
                finish(part(rows), rows)
            return
        acc = rest[n_ex + n_dep + n_o]

        @pl.when(k == 0)
        def _():
            for rows in chunks:
                acc[rows, :] = part(rows)

        @pl.when((k > 0) & (k < nk - 1))
        def _():
            for rows in chunks:
                acc[rows, :] += part(rows)

        @pl.when(k == nk - 1)
        def _():
            for rows in chunks:
                finish(acc[rows, :] + part(rows), rows)

    res = pl.pallas_call(
        body,
        grid=(m // tm, n // tn, nk),
        in_specs=[a_spec, b_spec] + ex_specs + [pl.BlockSpec(memory_space=pl.ANY)] * n_dep,
        out_specs=[pl.BlockSpec((tm, tn), lambda i, j, k: (i, j + j_out)) for _ in outs],
        out_shape=[jax.ShapeDtypeStruct((m, out_cols or n), dt) for dt in outs],
        input_output_aliases={} if into is None else {2 + n_ex + n_dep - 1: 0},
        scratch_shapes=[pltpu.VMEM((tm, tn), F32)] if nk > 1 else [],
        name=name,
        compiler_params=_cparams(("parallel", "parallel", "arbitrary")),
    )(a, b, *ex_arrays, *deps)
    return res if len(outs) > 1 else res[0]


def _rows(body, n_rows, tr, ins, outs, name, scratch=()):
    def spec(kind, shape):
        if kind == "blk":
            return pl.BlockSpec((tr,) + tuple(shape[1:]), lambda i: (i,) + (0,) * (len(shape) - 1))
        if kind == "dep":
            return pl.BlockSpec(memory_space=pl.ANY)
        if kind == "str":
            return pl.BlockSpec((shape[0], tr // shape[0], shape[2]), lambda i: (0, i, 0))
        return pl.BlockSpec(tuple(shape), lambda i: (0,) * len(shape))

    return pl.pallas_call(
        body,
        grid=(n_rows // tr,),
        in_specs=[spec(k, a.shape) for k, a in ins],
        out_specs=[spec(k, s) for k, s, _ in outs],
        out_shape=[jax.ShapeDtypeStruct(tuple(s), d) for _, s, d in outs],
        scratch_shapes=list(scratch),
        name=name,
        compiler_params=_cparams(("arbitrary",)),
    )(*[a for _, a in ins])


def _ln_stats(z):
    mu = jnp.mean(z, axis=-1, keepdims=True)
    zc = z - mu
    var = jnp.mean(zc * zc, axis=-1, keepdims=True)
    rstd = lax.rsqrt(var + LN_EPS)
    return zc * rstd, rstd


def _stream_scratch(c):
    return pltpu.VMEM((c // LANES, ROW_TILE, LANES), F32)


def _streams_in(ref3, scr):
    dil, n, c = ref3.shape
    for r in range(dil):
        for j in range(c // LANES):
            scr.at[j][pl.ds(r, n, stride=dil), :] = ref3[r, :, j * LANES:(j + 1) * LANES].astype(F32)
    return jnp.concatenate([scr[j] for j in range(c // LANES)], axis=1)


def _streams_out(val, ref3, scr):
    dil, n, c = ref3.shape
    for j in range(c // LANES):
        scr[j] = val[:, j * LANES:(j + 1) * LANES].astype(F32)
    for r in range(dil):
        for j in range(c // LANES):
            ref3[r, :, j * LANES:(j + 1) * LANES] = scr.at[j][pl.ds(r, n, stride=dil), :].astype(ref3.dtype)


def _mod(x, scale, shift, after, name):
    s, d = x.shape

    def body(x_ref, sc_ref, sh_ref, dep_ref, h_ref):
        h_ref[...] = (x_ref[...] * (1.0 + sc_ref[...]) + sh_ref[...]).astype(h_ref.dtype)

    return _rows(body, s, ROW_TILE, [("blk", x), ("all", scale), ("all", shift), ("dep", after)], [("blk", (s, d), MXU_DTYPE)], name)[0]


def _resid_ln(x, y, gate, g, b, nxt, name, dils=()):
    s, d = x.shape

    def body(x_ref, y_ref, gate_ref, g_ref, b_ref, sc_ref, sh_ref, xn_ref, h_ref, *rest):
        z = ALPHA * x_ref[...] + gate_ref[...] * y_ref[...]
        xhat, _ = _ln_stats(z)
        xn = xhat * g_ref[...] + b_ref[...]
        xn_ref[...] = xn
        h = xn * (1.0 + sc_ref[...]) + sh_ref[...]
        h_ref[...] = h.astype(h_ref.dtype)
        for hs_ref in rest[:len(dils)]:
            _streams_out(h, hs_ref, rest[-1])

    return _rows(body, s, ROW_TILE,
                 [("blk", x), ("blk", y), ("all", gate), ("all", g), ("all", b), ("all", nxt[0]), ("all", nxt[1])],
                 [("blk", (s, d), F32), ("blk", (s, d), MXU_DTYPE)] + [("str", (dil, s // dil, d), MXU_DTYPE) for dil in dils], name,
                 scratch=[_stream_scratch(d)] if dils else [])


def _mod_bwd(dxr, dhs, x, scale, name, after=None):
    s, d = x.shape
    n_dh = len(dhs)
    n_dep = 0 if after is None else 1

    def body(dxr_ref, *rest):
        dh_refs = rest[:n_dh]
        x_ref, sc_ref, dx_ref, red_ref, a_sh, a_sc = rest[n_dh:n_dh + 2] + rest[n_dh + 2 + n_dep:]
        i = pl.program_id(0)

        @pl.when(i == 0)
        def _():
            a_sh[...] = jnp.zeros_like(a_sh)
            a_sc[...] = jnp.zeros_like(a_sc)

        dh = dh_refs[0][...]
        for r in dh_refs[1:]:
            dh = dh + r[...]
        dx_ref[...] = dxr_ref[...] + dh * (1.0 + sc_ref[...])
        a_sh[...] += _fold8(dh)
        a_sc[...] += _fold8(dh * x_ref[...])

        @pl.when(i == pl.num_programs(0) - 1)
        def _():
            red_ref[...] = jnp.zeros_like(red_ref)
            red_ref[0:1, :] = jnp.sum(a_sh[...], axis=0, keepdims=True)
            red_ref[1:2, :] = jnp.sum(a_sc[...], axis=0, keepdims=True)

    return _rows(body, s, ROW_TILE, [("blk", dxr)] + [("blk", h) for h in dhs] + [("blk", x), ("all", scale)] + [("dep", after)] * n_dep,
                 [("blk", (s, d), F32), ("all", (SUBLANES, d), F32)], name,
                 scratch=[pltpu.VMEM((SUBLANES, d), F32)] * 2)


def _last_ln_loss_bwd(x, y, gate, g, b, target, name):
    s, d = x.shape

    def body(x_ref, y_ref, gate_ref, g_ref, b_ref, t_ref, l_ref, dxr_ref, dyy_ref, red_ref, a_l, a_g, a_b, a_gate):
        i = pl.program_id(0)

        @pl.when(i == 0)
        def _():
            for a in (a_l, a_g, a_b, a_gate):
                a[...] = jnp.zeros_like(a)

        yv = y_ref[...]
        z = ALPHA * x_ref[...] + gate_ref[...] * yv
        xhat, rstd = _ln_stats(z)
        e = xhat * g_ref[...] + b_ref[...] - t_ref[...]
        a_l[...] += _fold8(e * e)
        dxo_v = e * (1.0 / d)
        dxh = dxo_v * g_ref[...]
        dz = rstd * (dxh - jnp.mean(dxh, axis=-1, keepdims=True) - xhat * jnp.mean(dxh * xhat, axis=-1, keepdims=True))
        dxr_ref[...] = ALPHA * dz
        dyy_ref[...] = (gate_ref[...] * dz).astype(dyy_ref.dtype)
        a_g[...] += _fold8(dxo_v * xhat)
        a_b[...] += _fold8(dxo_v)
        a_gate[...] += _fold8(dz * yv)

        @pl.when(i == pl.num_programs(0) - 1)
        def _():
            l_ref[...] = jnp.full(l_ref.shape, 0.5 / d, F32) * jnp.sum(a_l[...])
            red_ref[...] = jnp.zeros_like(red_ref)
            red_ref[0:1, :] = jnp.sum(a_g[...], axis=0, keepdims=True)
            red_ref[1:2, :] = jnp.sum(a_b[...], axis=0, keepdims=True)
            red_ref[2:3, :] = jnp.sum(a_gate[...], axis=0, keepdims=True)

    l, dxr, dyy, red = _rows(
        body, s, ROW_TILE, [("blk", x), ("blk", y), ("all", gate), ("all", g), ("all", b), ("blk", target)],
        [("all", (SUBLANES, LANES), F32), ("blk", (s, d), F32), ("blk", (s, d), MXU_DTYPE), ("all", (SUBLANES, d), F32)], name,
        scratch=[pltpu.VMEM((SUBLANES, d), F32)] * 4)
    return l[0, 0], dxr, dyy, red


def _mod_ln_bwd(dxr, dhs, x, scale, x_in, y, gate, g, name, after=None):
    s, d = x.shape
    n_dh = len(dhs)
    n_dep = 0 if after is None else 1

    def body(dxr_ref, *rest):
        dh_refs = rest[:n_dh]
        x_ref, sc_ref, xin_ref, y_ref, gate_ref, g_ref = rest[n_dh:n_dh + 6]
        dxr_out, dyy_ref, red_mod, red_ln, a_sh, a_sc, a_g, a_b, a_gate = rest[n_dh + 6 + n_dep:n_dh + 15 + n_dep]
        i = pl.program_id(0)

        @pl.when(i == 0)
        def _():
            for a in (a_sh, a_sc, a_g, a_b, a_gate):
                a[...] = jnp.zeros_like(a)

        dh = dh_refs[0][...]
        for r in dh_refs[1:]:
            dh = dh + (r[...] if len(r.shape) == 2 else _streams_in(r, rest[-1]))
        xv = x_ref[...]
        dxo_v = dxr_ref[...] + dh * (1.0 + sc_ref[...])
        a_sh[...] += _fold8(dh)
        a_sc[...] += _fold8(dh * xv)
        yv = y_ref[...]
        z = ALPHA * xin_ref[...] + gate_ref[...] * yv
        xhat, rstd = _ln_stats(z)
        dxh = dxo_v * g_ref[...]
        dz = rstd * (dxh - jnp.mean(dxh, axis=-1, keepdims=True) - xhat * jnp.mean(dxh * xhat, axis=-1, keepdims=True))
        dxr_out[...] = ALPHA * dz
        dyy_ref[...] = (gate_ref[...] * dz).astype(dyy_ref.dtype)
        a_g[...] += _fold8(dxo_v * xhat)
        a_b[...] += _fold8(dxo_v)
        a_gate[...] += _fold8(dz * yv)

        @pl.when(i == pl.num_programs(0) - 1)
        def _():
            red_mod[...] = jnp.zeros_like(red_mod)
            red_mod[0:1, :] = jnp.sum(a_sh[...], axis=0, keepdims=True)
            red_mod[1:2, :] = jnp.sum(a_sc[...], axis=0, keepdims=True)
            red_ln[...] = jnp.zeros_like(red_ln)
            red_ln[0:1, :] = jnp.sum(a_g[...], axis=0, keepdims=True)
            red_ln[1:2, :] = jnp.sum(a_b[...], axis=0, keepdims=True)
            red_ln[2:3, :] = jnp.sum(a_gate[...], axis=0, keepdims=True)

    ins = ([("blk", dxr)] + [("blk" if h.ndim == 2 else "str", h) for h in dhs]
           + [("blk", x), ("all", scale), ("blk", x_in), ("blk", y), ("all", gate), ("all", g)] + [("dep", after)] * n_dep)
    return _rows(body, s, ROW_TILE, ins,
                 [("blk", (s, d), F32), ("blk", (s, d), MXU_DTYPE), ("all", (SUBLANES, d), F32), ("all", (SUBLANES, d), F32)], name,
                 scratch=[pltpu.VMEM((SUBLANES, d), F32)] * 5 + [_stream_scratch(d)] * any(h.ndim == 3 for h in dhs))


def _left_half(shape):
    return lax.broadcasted_iota(jnp.int32, shape, 1) < (LANES // 2)


CHUNKS_PER_STEP = 2


def _chunks_of_step():
    return [slice(i * CHUNK, (i + 1) * CHUNK) for i in range(CHUNKS_PER_STEP)]


def _split_groups(v):
    left = _left_half(v.shape)
    return jnp.concatenate([jnp.where(left, v, 0.0), jnp.where(left, 0.0, v)], axis=0)


def _spatial_z(vn, wc_ref, bias_ref, j):
    return _dot_nn(wc_ref[j], _split_groups(vn[:, j * LANES:(j + 1) * LANES])) + bias_ref[:, j * LANES:(j + 1) * LANES]


def _spatial_fwd(uvpre, vn_g, vn_b, wc, bias_full, name):
    s, d2 = uvpre.shape
    d = d2 // 2

    def body(uv_ref, g_ref, b_ref, wc_ref, bias_ref, out_ref):
        for rows in _chunks_of_step():
            u = _gelu(uv_ref[rows, :d])
            v = _gelu(uv_ref[rows, d:])
            vh, _ = _ln_stats(v)
            vn = vh * g_ref[...] + b_ref[...]
            for j in range(d // LANES):
                z = _spatial_z(vn, wc_ref, bias_ref, j)
                out_ref[rows, j * LANES:(j + 1) * LANES] = (u[:, j * LANES:(j + 1) * LANES] * z).astype(out_ref.dtype)

    return _rows(body, s, CHUNKS_PER_STEP * CHUNK, [("blk", uvpre), ("all", vn_g), ("all", vn_b), ("all", wc), ("all", bias_full)],
                 [("blk", (s, d), MXU_DTYPE)], name)[0]


def _spatial_bwd(uvpre, dgated, vn_g, vn_b, wc, wct, bias_full, name):
    s, d2 = uvpre.shape
    d = d2 // 2

    def body(uv_ref, dg_ref, g_ref, b_ref, wc_ref, wct_ref, bias_ref,
             duv_ref, dws_ref, dbias_ref, dbin_ref, dvg_ref, dvb_ref, dvn_buf, a_bin, a_vg, a_vb):
        i = pl.program_id(0)

        @pl.when(i == 0)
        def _():
            dws_ref[...] = jnp.zeros_like(dws_ref)
            dbias_ref[...] = jnp.zeros_like(dbias_ref)
            a_bin[...] = jnp.zeros_like(a_bin)
            a_vg[...] = jnp.zeros_like(a_vg)
            a_vb[...] = jnp.zeros_like(a_vb)

        for rows in _chunks_of_step():
            u, u_grad = _gelu_and_grad(uv_ref[rows, :d])
            v, v_grad = _gelu_and_grad(uv_ref[rows, d:])
            vh, rstd = _ln_stats(v)
            vn = vh * g_ref[...] + b_ref[...]
            dg = dg_ref[rows, :]
            dzz = dg * u
            dbias_ref[...] += dzz
            for j in range(d // LANES):
                cols = slice(j * LANES, (j + 1) * LANES)
                z = _spatial_z(vn, wc_ref, bias_ref, j)
                dup = dg[:, cols] * z * u_grad[:, cols]
                duv_ref[rows, cols] = dup.astype(duv_ref.dtype)
                a_bin[:, cols] += _fold8(dup)
                dz2 = _split_groups(dzz[:, cols])
                dvn_buf[:, cols] = _dot_nn(wct_ref[j], dz2)
                dw2 = _dot_nt(dz2, vn[:, cols])
                dws_ref[2 * j] += dw2[:CHUNK]
                dws_ref[2 * j + 1] += dw2[CHUNK:]
            dvn = dvn_buf[...]
            a_vg[...] += _fold8(dvn * vh)
            a_vb[...] += _fold8(dvn)
            dvh = dvn * g_ref[...]
            dv = rstd * (dvh - jnp.mean(dvh, axis=-1, keepdims=True) - vh * jnp.mean(dvh * vh, axis=-1, keepdims=True))
            dvp = dv * v_grad
            duv_ref[rows, d:] = dvp.astype(duv_ref.dtype)
            a_bin[:, d:] += _fold8(dvp)

        @pl.when(i == pl.num_programs(0) - 1)
        def _():
            dbin_ref[...] = jnp.sum(a_bin[...], axis=0, keepdims=True)
            dvg_ref[...] = jnp.sum(a_vg[...], axis=0, keepdims=True)
            dvb_ref[...] = jnp.sum(a_vb[...], axis=0, keepdims=True)

    return _rows(body, s, CHUNKS_PER_STEP * CHUNK,
                 [("blk", uvpre), ("blk", dgated), ("all", vn_g), ("all", vn_b), ("all", wc), ("all", wct), ("all", bias_full)],
                 [("blk", (s, d2), MXU_DTYPE), ("all", (A_GROUPS, CHUNK, CHUNK), F32), ("all", (CHUNK, d), F32),
                  ("all", (1, d2), F32), ("all", (1, d), F32), ("all", (1, d), F32)], name,
                 scratch=[pltpu.VMEM((CHUNK, d), F32), pltpu.VMEM((SUBLANES, d2), F32),
                          pltpu.VMEM((SUBLANES, d), F32), pltpu.VMEM((SUBLANES, d), F32)])


def _head_mask(v, h):
    lane = lax.broadcasted_iota(jnp.int32, v.shape, 1)
    return jnp.where((lane >= h * HEAD_DIM) & (lane < (h + 1) * HEAD_DIM), v, jnp.zeros_like(v))


def _att_bias(slopes, dil):
    qi = lax.broadcasted_iota(jnp.int32, (SPAN, SPAN), 0)
    ki = lax.broadcasted_iota(jnp.int32, (SPAN, SPAN), 1)
    sl = slopes[:, None, None]
    cur = jnp.where(ki <= qi, -sl * (float(dil) * (qi - ki).astype(F32)), NEG)
    prev = jnp.where(ki >= qi, -sl * (float(dil) * (SPAN + qi - ki).astype(F32)), NEG)
    absent = jnp.full_like(prev, NEG)
    pairs = slopes.shape[0] // 2

    def fwd(pv):
        return jnp.concatenate([cur, pv], axis=2).reshape(pairs, 2 * SPAN, 2 * SPAN)

    def bwd(pv):
        return jnp.concatenate([cur.reshape(pairs, 2 * SPAN, SPAN), pv.reshape(pairs, 2 * SPAN, SPAN)], axis=1)

    return jnp.stack([fwd(absent), fwd(prev)]), jnp.stack([bwd(absent), bwd(prev)])


ATT_GROUP = 4


def _att_group(s, dil):
    nb = s // (dil * SPAN)
    grp = min(ATT_GROUP, nb)
    assert nb % grp == 0
    return nb, grp


def _att_specs(s, d, dil, kinds):
    nb, grp = _att_group(s, dil)

    def spec(part, which):
        if which == "group":
            return pl.BlockSpec((grp * SPAN, d), lambda b: (b, part))
        if which == "prev":
            return pl.BlockSpec((SPAN, d), lambda b: (jnp.where((grp * b) % nb == 0, grp * b, grp * b - 1), part))
        return pl.BlockSpec((SPAN, d), lambda b: (jnp.where((grp * b + grp - 1) % nb == nb - 1, grp * b + grp - 1, grp * b + grp), part))

    return [spec(part, which) for part, which in kinds]


def _head_col(v, head):
    return v[:, head:head + 1]


def _expand_heads(w, j):
    shape = (w.shape[0], LANES)
    return jnp.where(_left_half(shape), jnp.broadcast_to(_head_col(w, 2 * j), shape), jnp.broadcast_to(_head_col(w, 2 * j + 1), shape))


def _attn_fwd(qkv, slopes, dil, name):
    s, d3 = qkv.shape
    d = d3 // 3
    nb, grp = _att_group(s, dil)
    table, _ = _att_bias(slopes, dil)

    def body(q_ref, k_ref, kp_ref, v_ref, vp_ref, tb_ref, o_ref, l_ref):
        b = pl.program_id(0)
        left = _left_half((SPAN, LANES))
        lane = lax.broadcasted_iota(jnp.int32, (SPAN, LANES), 1)
        for sub in range(grp):
            rows, before = slice(sub * SPAN, (sub + 1) * SPAN), slice((sub - 1) * SPAN, sub * SPAN)
            variant = jnp.where((grp * b) % nb == 0, 0, 1) if sub == 0 else 1
            lses = jnp.zeros((SPAN, LANES), F32)
            for hp in range(d // LANES):
                cols = slice(hp * LANES, (hp + 1) * LANES)
                q = q_ref[rows, cols]
                q2 = jnp.concatenate([_head_mask(q, 0), _head_mask(q, 1)], axis=0) * ATT_SCALE
                k2 = jnp.concatenate([k_ref[rows, cols], kp_ref[:, cols] if sub == 0 else k_ref[before, cols]], axis=0)
                v2 = jnp.concatenate([v_ref[rows, cols], vp_ref[:, cols] if sub == 0 else v_ref[before, cols]], axis=0)
                sc = _dot_nt(q2, k2) + tb_ref[variant, hp]
                m = jnp.max(sc, axis=-1, keepdims=True)
                p = jnp.exp(sc - m)
                l = jnp.sum(p, axis=-1, keepdims=True)
                r = _dot_nn(p, v2) * (1.0 / l)
                lse = m + jnp.log(l)
                o_ref[rows, cols] = jnp.where(left, r[:SPAN], r[SPAN:])
                lses = jnp.where(lane == 2 * hp, lse[:SPAN], jnp.where(lane == 2 * hp + 1, lse[SPAN:], lses))
            l_ref[rows, :] = lses

    specs = _att_specs(s, d, dil, [(0, "group"), (1, "group"), (1, "prev"), (2, "group"), (2, "prev")])
    return pl.pallas_call(
        body,
        grid=(s // (grp * SPAN),),
        in_specs=specs + [pl.BlockSpec(table.shape, lambda b: (0, 0, 0, 0))],
        out_specs=[pl.BlockSpec((grp * SPAN, d), lambda b: (b, 0)), pl.BlockSpec((grp * SPAN, LANES), lambda b: (b, 0))],
        out_shape=[jax.ShapeDtypeStruct((s, d), F32), jax.ShapeDtypeStruct((s, LANES), F32)],
        name=name,
        compiler_params=_cparams(("parallel",)),
    )(qkv, qkv, qkv, qkv, qkv, table)


def _attn_bwd(qkv, do, lse, dd, slopes, dil, name):
    s, d3 = qkv.shape
    d = d3 // 3
    nb, grp = _att_group(s, dil)
    _, table = _att_bias(slopes, dil)

    def cols_stacked(cur, nxt, hp):
        return jnp.concatenate([jnp.broadcast_to(_head_col(a, 2 * hp + h), (SPAN, LANES)) for a in (cur, nxt) for h in range(2)], axis=0)

    def body(k_ref, v_ref, q_ref, qn_ref, do_ref, don_ref, l_ref, ln_ref, dd_ref, ddn_ref, tb_ref, out_ref, carry):
        b = pl.program_id(0)

        @pl.when(b == 0)
        def _():
            carry[...] = jnp.zeros_like(carry)

        wide = 2 * LANES
        head_of_lane = (lax.broadcasted_iota(jnp.int32, (SPAN, wide), 1) % LANES) // HEAD_DIM
        zero = jnp.zeros((SPAN, LANES), k_ref.dtype)

        def heads_stacked2(cur, nxt):
            return jnp.concatenate([jnp.where(head_of_lane == h, a, jnp.zeros_like(a)) for a in (cur, nxt) for h in range(2)], axis=0)

        def block_diagonal(a, b):
            return jnp.concatenate([jnp.concatenate([a, zero], axis=1), jnp.concatenate([zero, b], axis=1)], axis=0)

        for sub in range(grp):
            rows, after = slice(sub * SPAN, (sub + 1) * SPAN), slice((sub + 1) * SPAN, (sub + 2) * SPAN)
            last = sub == grp - 1
            variant = jnp.where((grp * b + sub) % nb == nb - 1, 0, 1) if last else 1
            lse_c, dd_c = l_ref[rows, :], dd_ref[rows, :]
            lse_n, dd_n = (ln_ref[...], ddn_ref[...]) if last else (l_ref[after, :], dd_ref[after, :])
            for hp2 in range(d // wide):
                cols = slice(hp2 * wide, (hp2 + 1) * wide)
                pa, pb = 2 * hp2, 2 * hp2 + 1
                ca, cb = slice(pa * LANES, (pa + 1) * LANES), slice(pb * LANES, (pb + 1) * LANES)
                kbd = block_diagonal(k_ref[rows, ca], k_ref[rows, cb])
                vbd = block_diagonal(v_ref[rows, ca], v_ref[rows, cb])
                q4 = heads_stacked2(q_ref[rows, cols], qn_ref[:, cols] if last else q_ref[after, cols])
                do4 = heads_stacked2(do_ref[rows, cols], don_ref[:, cols] if last else do_ref[after, cols])
                bias = jnp.concatenate([tb_ref[variant, pa], tb_ref[variant, pb]], axis=1)
                lse2 = jnp.concatenate([cols_stacked(lse_c, lse_n, pa), cols_stacked(lse_c, lse_n, pb)], axis=1)
                dd2 = jnp.concatenate([cols_stacked(dd_c, dd_n, pa), cols_stacked(dd_c, dd_n, pb)], axis=1)
                p = jnp.exp(_dot_nt(q4 * ATT_SCALE, kbd) + bias - lse2)
                ds = p * (_dot_nt(do4, vbd) - dd2)
                dq4 = _dot_nn(ds, kbd)
                left = head_of_lane == 0
                dq_cur = jnp.where(left, dq4[:SPAN], dq4[SPAN:2 * SPAN]) + carry[:, cols]
                carry[:, cols] = jnp.where(left, dq4[2 * SPAN:3 * SPAN], dq4[3 * SPAN:])
                out_ref[rows, cols] = (dq_cur * ATT_SCALE).astype(out_ref.dtype)
                for pair, lanes in ((pa, slice(0, LANES)), (pb, slice(LANES, wide))):
                    out_ref[rows, d + pair * LANES:d + (pair + 1) * LANES] = (_dot_tn(ds[:, lanes], q4[:, lanes]) * ATT_SCALE).astype(out_ref.dtype)
                    out_ref[rows, 2 * d + pair * LANES:2 * d + (pair + 1) * LANES] = _dot_tn(p[:, lanes], do4[:, lanes]).astype(out_ref.dtype)

    qkv_specs = _att_specs(s, d, dil, [(1, "group"), (2, "group"), (0, "group"), (0, "next")])
    wide = _att_specs(s, d, dil, [(0, "group"), (0, "next")])
    heads = _att_specs(s, LANES, dil, [(0, "group"), (0, "next")])
    return pl.pallas_call(
        body,
        grid=(s // (grp * SPAN),),
        in_specs=qkv_specs + wide + heads + heads + [pl.BlockSpec(table.shape, lambda b: (0, 0, 0, 0))],
        out_specs=pl.BlockSpec((grp * SPAN, d3), lambda b: (b, 0)),
        out_shape=jax.ShapeDtypeStruct((s, d3), MXU_DTYPE),
        scratch_shapes=[pltpu.VMEM((SPAN, d), F32)],
        name=name,
        compiler_params=_cparams(("arbitrary",)),
    )(qkv, qkv, qkv, qkv, do, do, lse, lse, dd, dd, table)


def _mix_weights(l_refs):
    ls = [r[...] for r in l_refs]
    m = functools.reduce(jnp.maximum, ls)
    es = [jnp.exp(l - m) for l in ls]
    tot = functools.reduce(lambda a, c: a + c, es)
    return [e / tot for e in es]


def _combine_fwd(os_, ls_, name):
    s, d = ls_[0].shape[0], os_[0].shape[-1]
    n = len(os_)
    n_str = sum(o.ndim == 3 for o in os_)

    def body(*refs):
        o_refs, l_refs, out_ref, scrs = refs[:n], refs[n:2 * n], refs[2 * n], list(refs[2 * n + 1:])
        ws = _mix_weights(l_refs)
        os_v = [o if len(o.shape) == 2 else _streams_in(o, scrs.pop()) for o in o_refs]
        for j in range(d // LANES):
            cols = slice(j * LANES, (j + 1) * LANES)
            acc = _expand_heads(ws[0], j) * os_v[0][:, cols]
            for w, o in zip(ws[1:], os_v[1:]):
                acc = acc + _expand_heads(w, j) * o[:, cols]
            out_ref[:, cols] = acc

    return _rows(body, s, ROW_TILE, [("blk" if a.ndim == 2 else "str", a) for a in os_] + [("blk", a) for a in ls_],
                 [("blk", (s, d), F32)], name, scratch=[_stream_scratch(d)] * n_str)[0]


def _combine_bwd(do, o, ls_, dils, name):
    s, d = o.shape
    n = len(ls_)
    sel = (lax.broadcasted_iota(jnp.int32, (d, LANES), 0) // HEAD_DIM == lax.broadcasted_iota(jnp.int32, (d, LANES), 1)).astype(F32)

    def body(do_ref, o_ref, *rest):
        l_refs, sel_ref, outs = rest[:n], rest[n], rest[n + 1:n + 1 + 2 * n]
        ws = _mix_weights(l_refs)
        dov = do_ref[...]
        r = jnp.dot(dov * o_ref[...], sel_ref[...], precision=lax.Precision.HIGHEST, preferred_element_type=F32)
        for g in range(n):
            outs[2 * g + 1][...] = ws[g] * r
            parts = [_expand_heads(ws[g], j) * dov[:, j * LANES:(j + 1) * LANES] for j in range(d // LANES)]
            if dils[g] == 1:
                for j, part in enumerate(parts):
                    outs[2 * g][:, j * LANES:(j + 1) * LANES] = part.astype(outs[2 * g].dtype)
            else:
                _streams_out(jnp.concatenate(parts, axis=1), outs[2 * g], rest[-1])

    outs = []
    for dil in dils:
        outs += [("blk", (s, d), MXU_DTYPE) if dil == 1 else ("str", (dil, s // dil, d), MXU_DTYPE), ("blk", (s, LANES), F32)]
    res = _rows(body, s, ROW_TILE, [("blk", do), ("blk", o)] + [("blk", l) for l in ls_] + [("all", sel)], outs, name,
                scratch=[_stream_scratch(d)])
    return [(res[2 * g], res[2 * g + 1]) for g in range(n)]


def _ada_fwd(c_all, w, b, name):
    nsub, d, cs = w.shape

    def body(c_ref, w_ref, b_ref, o_ref):
        cv = c_ref[...]
        sc = cv * (1.0 / (1.0 + jnp.exp(-cv)))
        o_ref[...] = _dot_nn(sc, w_ref[...]) + b_ref[...]

    return pl.pallas_call(
        body,
        grid=(nsub,),
        in_specs=[pl.BlockSpec(c_all.shape, lambda i: (0, 0)), pl.BlockSpec((None, d, cs), lambda i: (i, 0, 0)),
                  pl.BlockSpec((None, 1, cs), lambda i: (i, 0, 0))],
        out_specs=pl.BlockSpec((None, N_DEV, cs), lambda i: (i, 0, 0)),
        out_shape=jax.ShapeDtypeStruct((nsub, N_DEV, cs), F32),
        name=name,
        compiler_params=_cparams(("parallel",)),
    )(c_all, w, b)


def _ada_bwd(c_all_t, dm, name):
    d, nb = c_all_t.shape
    nsub, _, cs = dm.shape

    def body(c_ref, dm_ref, o_ref):
        cv = c_ref[...]
        sc = cv * (1.0 / (1.0 + jnp.exp(-cv)))
        acc = sc[:, 0:1] * dm_ref[0:1, :]
        for bi in range(1, nb):
            acc = acc + sc[:, bi:bi + 1] * dm_ref[bi:bi + 1, :]
        o_ref[...] = acc

    return pl.pallas_call(
        body,
        grid=(nsub,),
        in_specs=[pl.BlockSpec(c_all_t.shape, lambda i: (0, 0)), pl.BlockSpec((None, nb, cs), lambda i: (i, 0, 0))],
        out_specs=pl.BlockSpec((None, d, cs), lambda i: (i, 0, 0)),
        out_shape=jax.ShapeDtypeStruct((nsub, d, cs), F32),
        name=name,
        compiler_params=_cparams(("parallel",)),
    )(c_all_t, dm)


def _row_tile(r, row_elems, block_elems=256 * 1024):
    t = 2 * SUBLANES
    if r % t:
        return r
    while t * 2 * row_elems <= block_elems and r % (t * 2) == 0:
        t *= 2
    return t


def _adamw(w, g, m, v, name):
    shape = w.shape
    c = shape[-1]
    r = w.size // c
    tr = _row_tile(r, c, 512 * 1024)
    w2, g2, m2, v2 = [a.reshape(r, c) for a in (w, g, m, v)]
    bc1 = 1.0 - ADAM_B1 ** ADAM_STEP
    bc2 = 1.0 - ADAM_B2 ** ADAM_STEP

    def body(w_ref, g_ref, m_ref, v_ref, d_ref, nm_ref, nv_ref):
        gv = g_ref[...]
        nm = ADAM_B1 * m_ref[...] + (1.0 - ADAM_B1) * gv
        nv = ADAM_B2 * v_ref[...] + (1.0 - ADAM_B2) * (gv * gv)
        d_ref[...] = -ADAM_LR * ((nm / bc1) / (jnp.sqrt(nv / bc2) + ADAM_EPS) + ADAM_WD * w_ref[...])
        nm_ref[...] = nm
        nv_ref[...] = nv

    res = _rows(body, r, tr, [("blk", a) for a in (w2, g2, m2, v2)], [("blk", (r, c), F32)] * 3, name)
    return [a.reshape(shape) for a in res]


def _adamw_layers(w, gs, m, v, name):
    nl, k, c = w.shape
    r = nl * k
    tr = _row_tile(k, c, 512 * 1024)
    per = k // tr
    w2, m2, v2 = [a.reshape(r, c) for a in (w, m, v)]
    bc1 = 1.0 - ADAM_B1 ** ADAM_STEP
    bc2 = 1.0 - ADAM_B2 ** ADAM_STEP

    def body(w_ref, m_ref, v_ref, *refs):
        g_refs, (go_ref, d_ref, nm_ref, nv_ref) = refs[:nl], refs[nl:]
        layer = pl.program_id(0) // per
        gv = g_refs[0][...]
        for l in range(1, nl):
            gv = jnp.where(layer == l, g_refs[l][...], gv)
        go_ref[...] = gv
        nm = ADAM_B1 * m_ref[...] + (1.0 - ADAM_B1) * gv
        nv = ADAM_B2 * v_ref[...] + (1.0 - ADAM_B2) * (gv * gv)
        d_ref[...] = -ADAM_LR * ((nm / bc1) / (jnp.sqrt(nv / bc2) + ADAM_EPS) + ADAM_WD * w_ref[...])
        nm_ref[...] = nm
        nv_ref[...] = nv

    blk = pl.BlockSpec((tr, c), lambda i: (i, 0))
    g_specs = [pl.BlockSpec((tr, c), functools.partial(lambda l, i: (jnp.clip(i - l * per, 0, per - 1), 0), l)) for l in range(nl)]
    res = pl.pallas_call(
        body,
        grid=(r // tr,),
        in_specs=[blk] * 3 + g_specs,
        out_specs=[blk] * 4,
        out_shape=[jax.ShapeDtypeStruct((r, c), F32)] * 4,
        name=name,
        compiler_params=_cparams(("arbitrary",)),
    )(w2, m2, v2, *gs)
    return [a.reshape(w.shape) for a in res]


def _sum_slots(buf, name):
    n, r, c = buf.shape
    tr = _row_tile(r, n * c, 2 * 1024 * 1024)

    def body(b_ref, o_ref):
        acc = b_ref[0].astype(F32)
        for k in range(1, n):
            acc = acc + b_ref[k].astype(F32)
        o_ref[...] = acc

    return pl.pallas_call(
        body,
        grid=(r // tr,),
        in_specs=[pl.BlockSpec((n, tr, c), lambda i: (0, i, 0))],
        out_specs=pl.BlockSpec((tr, c), lambda i: (i, 0)),
        out_shape=jax.ShapeDtypeStruct((r, c), F32),
        name=name,
        compiler_params=_cparams(("parallel",)),
    )(buf)


def _me():
    return lax.axis_index("x"), lax.axis_index("y"), lax.axis_index("c")


def _all_gather_small(blk, name, after=()):
    m_per, n = blk.shape

    def body(x_ref, *rest):
        out_ref, send_sems, recv_sems, local_sem = rest[len(after):]
        x, y, c = _me()
        me, sibling = (x, y, c), (x, y, 1 - c)
        chips = [(1 - x, y), (x, 1 - y), (1 - x, 1 - y)]

        def rows(px, py, pc):
            return out_ref.at[pl.ds((4 * px + 2 * py + pc) * m_per, m_per), :]

        def copy(k, block, to, src=None):
            return pltpu.make_async_remote_copy(
                src_ref=rows(*block) if src is None else src, dst_ref=rows(*block),
                send_sem=send_sems.at[k], recv_sem=recv_sems.at[k], device_id=to, device_id_type=MESH)

        mine = pltpu.make_async_copy(x_ref, rows(*me), local_sem)
        mine.start()
        first = [copy(0, me, sibling, src=x_ref)]
        first += [copy(1 + j, me, (*chip, c), src=x_ref) for j, chip in enumerate(chips)]
        for cp in first:
            cp.start()
        passed = [copy(4 + j, (*chip, c), sibling) for j, chip in enumerate(chips)]
        for j, chip in enumerate(chips):
            copy(1 + j, (*chip, c), me).wait_recv()
            passed[j].start()
        copy(0, sibling, me).wait_recv()
        for j, chip in enumerate(chips):
            copy(4 + j, (*chip, 1 - c), me).wait_recv()
        for cp in first + passed:
            cp.wait_send()
        mine.wait()

    return pl.pallas_call(
        body,
        out_shape=jax.ShapeDtypeStruct((N_DEV * m_per, n), blk.dtype),
        in_specs=[pl.BlockSpec(memory_space=pltpu.VMEM)] + [pl.BlockSpec(memory_space=pl.ANY)] * len(after),
        out_specs=pl.BlockSpec(memory_space=pltpu.VMEM),
        scratch_shapes=[pltpu.SemaphoreType.DMA((7,)), pltpu.SemaphoreType.DMA((7,)), pltpu.SemaphoreType.DMA],
        name=name,
        compiler_params=pltpu.CompilerParams(vmem_limit_bytes=VMEM_LIMIT),
    )(blk, *after)


_HBM = pl.BlockSpec(memory_space=pltpu.HBM)
_SEM = pl.BlockSpec(memory_space=pltpu.SEMAPHORE)
_EFFECT = pltpu.SideEffectType.DATAFLOW_SIDE_EFFECTING


def _other_chips(x, y):
    return [(1 - x, y), (x, 1 - y), (1 - x, 1 - y)]


def _gather_copy(w, j, src_ref, land_ref, send_sems, recv_sems, halved=False):
    x, y, c = _me()
    if halved:
        half = src_ref.shape[0] // 2
        src_ref = src_ref.at[pl.ds(c * half, half), :]
    return pltpu.make_async_remote_copy(
        src_ref=src_ref, dst_ref=land_ref.at[2 * x + y], send_sem=send_sems.at[3 * w + j], recv_sem=recv_sems.at[3 * w + j],
        device_id=(*_other_chips(x, y)[j], c), device_id_type=MESH)


def _gather_start(shards, halved, after, name):
    n = len(shards)
    lands = [lax.empty((N_CHIPS, s.shape[0] // 2 if w in halved else s.shape[0], s.shape[1]), s.dtype) for w, s in enumerate(shards)]

    def body(*refs):
        in_refs, land_refs = refs[:n], refs[n:2 * n]
        send_sems, recv_sems = refs[2 * n + 1], refs[2 * n + 2]
        token = refs[-1]
        for w in range(n):
            for j in range(3):
                _gather_copy(w, j, in_refs[w], land_refs[w], send_sems, recv_sems, w in halved).start()
        token[...] = jnp.zeros_like(token)

    res = pl.pallas_call(
        body,
        out_shape=(pltpu.SemaphoreType.DMA((3 * n,)), pltpu.SemaphoreType.DMA((3 * n,)),
                   *[pltpu.HBM(s.shape, s.dtype) for s in shards], *[pltpu.HBM(l.shape, l.dtype) for l in lands],
                   jax.ShapeDtypeStruct((SUBLANES, LANES), F32)),
        in_specs=[_HBM] * (2 * n) + [pl.BlockSpec(memory_space=pl.ANY)],
        out_specs=(_SEM, _SEM, *[_HBM] * (2 * n), pl.BlockSpec(memory_space=pltpu.VMEM)),
        input_output_aliases={i: 2 + i for i in range(2 * n)},
        name=name,
        compiler_params=pltpu.CompilerParams(has_side_effects=_EFFECT),
    )(*[pltpu.with_memory_space_constraint(a, pltpu.HBM) for a in list(shards) + lands], after)
    return res[0], res[1], res[2:2 + n], res[2 + n:2 + 2 * n], res[-1]


def _gather_wait(w, shard, land, send_sems, recv_sems, after, name, halved=False):
    def body(s_ref, land_ref, send_sems, recv_sems, after_ref, s_out, land_out, stage):
        x, y, _ = _me()
        if not halved:
            pltpu.sync_copy(s_ref, stage)
            pltpu.sync_copy(stage, land_out.at[2 * x + y])
        for j in range(3):
            cp = _gather_copy(w, j, s_ref, land_ref, send_sems, recv_sems, halved)
            cp.wait_send()
            cp.wait_recv()

    return pl.pallas_call(
        body,
        out_shape=(pltpu.HBM(shard.shape, shard.dtype), pltpu.HBM(land.shape, land.dtype)),
        in_specs=(_HBM, _HBM, _SEM, _SEM, pl.BlockSpec(memory_space=pl.ANY)),
        out_specs=(_HBM, _HBM),
        input_output_aliases={0: 0, 1: 1},
        scratch_shapes=[pltpu.VMEM((SUBLANES, LANES) if halved else shard.shape, shard.dtype)],
        name=name,
        compiler_params=pltpu.CompilerParams(has_side_effects=_EFFECT, vmem_limit_bytes=VMEM_LIMIT),
    )(shard, land, send_sems, recv_sems, after)


def _assemble_halves(shard, land, name):
    half = land.shape[1]

    def body(s_ref, land_ref, out_ref, send_sems, recv_sems, local_sems):
        x, y, c = _me()
        own = pltpu.make_async_copy(s_ref, out_ref.at[2 * x + y], local_sems.at[3])
        own.start()
        cps = []
        for j, (ox, oy) in enumerate(_other_chips(x, y)):
            qj = 2 * ox + oy
            mine = out_ref.at[qj, pl.ds(c * half, half), :]
            lc = pltpu.make_async_copy(land_ref.at[qj], mine, local_sems.at[j])
            lc.start()
            rc = pltpu.make_async_remote_copy(
                src_ref=land_ref.at[qj], dst_ref=mine, send_sem=send_sems.at[j], recv_sem=recv_sems.at[j],
                device_id=(x, y, 1 - c), device_id_type=MESH)
            rc.start()
            cps.append((lc, rc))
        for lc, rc in cps:
            rc.wait_recv()
        for lc, rc in cps:
            rc.wait_send()
            lc.wait()
        own.wait()

    vmem = pl.BlockSpec(memory_space=pltpu.VMEM)
    return pl.pallas_call(
        body,
        out_shape=jax.ShapeDtypeStruct((N_CHIPS,) + shard.shape, shard.dtype),
        in_specs=[vmem, vmem],
        out_specs=vmem,
        scratch_shapes=[pltpu.SemaphoreType.DMA((3,)), pltpu.SemaphoreType.DMA((3,)), pltpu.SemaphoreType.DMA((4,))],
        name=name,
        compiler_params=pltpu.CompilerParams(vmem_limit_bytes=VMEM_LIMIT),
    )(shard, land)


def _piece_shape(shape, kind):
    k, nn = shape
    if kind == "all":
        return (k, nn)
    return (k // 2, nn // N_CHIPS) if kind == "col" else (k // N_CHIPS // 2, nn)


def _piece_of(g_ref, kind, tq, tc):
    pr, pc = _piece_shape(g_ref.shape, kind)
    if kind == "all":
        return g_ref
    if kind == "col":
        return g_ref.at[pl.ds(tc * pr, pr), pl.ds(tq * pc, pc)]
    return g_ref.at[pl.ds((2 * tq + tc) * pr, pr), :]


def _scatter_copy(w, r, kind, g_ref, land_ref, send_sems, recv_sems):
    x, y, c = _me()
    tx, ty, tc = (x + ((r >> 2) & 1)) % 2, (y + ((r >> 1) & 1)) % 2, (c + (r & 1)) % 2
    return pltpu.make_async_remote_copy(
        src_ref=_piece_of(g_ref, kind, 2 * tx + ty, tc), dst_ref=land_ref.at[4 * x + 2 * y + c],
        send_sem=send_sems.at[N_DEV * w + r], recv_sem=recv_sems.at[N_DEV * w + r], device_id=(tx, ty, tc), device_id_type=MESH)


def _scatter_start(gs, kinds, name):
    n = len(gs)
    pieces = [_piece_shape(g.shape, kind) for g, kind in zip(gs, kinds)]
    lands = [lax.empty((N_DEV,) + p, g.dtype) for p, g in zip(pieces, gs)]

    def body(*refs):
        g_refs, land_refs, send_sems, recv_sems = refs[:n], refs[n:2 * n], refs[2 * n], refs[2 * n + 1]
        land_outs, stages = refs[3 * n + 2:4 * n + 2], refs[4 * n + 2:]
        x, y, c = _me()
        for w in range(n):
            for r in range(1, N_DEV):
                _scatter_copy(w, r, kinds[w], g_refs[w], land_refs[w], send_sems, recv_sems).start()
        for w in range(n):
            pltpu.sync_copy(_piece_of(g_refs[w], kinds[w], 2 * x + y, c), stages[w])
            pltpu.sync_copy(stages[w], land_outs[w].at[4 * x + 2 * y + c])

    arrays = list(gs) + lands
    res = pl.pallas_call(
        body,
        out_shape=(pltpu.SemaphoreType.DMA((N_DEV * n,)), pltpu.SemaphoreType.DMA((N_DEV * n,)),
                   *[pltpu.HBM(a.shape, a.dtype) for a in arrays]),
        in_specs=[_HBM] * (2 * n),
        out_specs=(_SEM, _SEM, *[_HBM] * (2 * n)),
        input_output_aliases={i: 2 + i for i in range(2 * n)},
        scratch_shapes=[pltpu.VMEM(p, g.dtype) for p, g in zip(pieces, gs)],
        name=name,
        compiler_params=pltpu.CompilerParams(has_side_effects=_EFFECT, vmem_limit_bytes=VMEM_LIMIT),
    )(*[pltpu.with_memory_space_constraint(a, pltpu.HBM) for a in arrays])
    return res[0], res[1], res[2:2 + n], res[2 + n:]


def _scatter_wait(send_sems, recv_sems, gs, lands, kinds, after, name):
    n = len(gs)

    def body(*refs):
        g_refs, land_refs, send_sems, recv_sems = refs[:n], refs[n:2 * n], refs[2 * n], refs[2 * n + 1]
        for w in range(n):
            for r in range(1, N_DEV):
                cp = _scatter_copy(w, r, kinds[w], g_refs[w], land_refs[w], send_sems, recv_sems)
                cp.wait_send()
                cp.wait_recv()

    arrays = list(gs) + list(lands)
    return pl.pallas_call(
        body,
        out_shape=tuple(pltpu.HBM(a.shape, a.dtype) for a in arrays),
        in_specs=(*[_HBM] * (2 * n), _SEM, _SEM, pl.BlockSpec(memory_space=pl.ANY)),
        out_specs=tuple([_HBM] * (2 * n)),
        input_output_aliases={i: i for i in range(2 * n)},
        name=name,
        compiler_params=pltpu.CompilerParams(has_side_effects=_EFFECT),
    )(*arrays, send_sems, recv_sems, after)[n:]


def _sum_swap(bufs, name):
    n = len(bufs)

    def body(*refs):
        in_refs, out_refs = refs[:n], refs[n:2 * n]
        send_sems, recv_sems = refs[2 * n:]
        x, y, c = _me()
        cps = []
        for w in range(n):
            slots, r, _ = bufs[w].shape
            mine = out_refs[w].at[c]
            for r0 in range(0, r, min(r, ROW_TILE)):
                rows = slice(r0, r0 + min(r, ROW_TILE))
                acc = in_refs[w][0, rows, :].astype(F32)
                for k in range(1, slots):
                    acc = acc + in_refs[w][k, rows, :].astype(F32)
                mine[rows, :] = acc
            rc = pltpu.make_async_remote_copy(
                src_ref=mine, dst_ref=mine, send_sem=send_sems.at[w], recv_sem=recv_sems.at[w],
                device_id=(x, y, 1 - c), device_id_type=MESH)
            rc.start()
            cps.append(rc)
        for rc in cps:
            rc.wait_recv()
        for rc in cps:
            rc.wait_send()

    vmem = pl.BlockSpec(memory_space=pltpu.VMEM)
    return pl.pallas_call(
        body,
        out_shape=[jax.ShapeDtypeStruct((2,) + b.shape[1:], F32) for b in bufs],
        in_specs=[vmem] * n,
        out_specs=[vmem] * n,
        scratch_shapes=[pltpu.SemaphoreType.DMA((n,)), pltpu.SemaphoreType.DMA((n,))],
        name=name,
        compiler_params=pltpu.CompilerParams(vmem_limit_bytes=VMEM_LIMIT),
    )(*bufs)


def _side_by_side(w):
    g, t, _ = w.shape
    return w.reshape(g // 2, 2, t, t).transpose(0, 2, 1, 3).reshape(g // 2, t, 2 * t)


def _to_streams(a, dil):
    if dil == 1:
        return a
    s, c = a.shape
    return a.reshape(s // dil, dil, c).transpose(1, 0, 2).reshape(s, c)


def _from_streams(a, dil):
    if dil == 1:
        return a
    s, c = a.shape
    return a.reshape(dil, s // dil, c).transpose(1, 0, 2).reshape(s, c)


def _mm_tiles(s):
    return min(s, 2048)


def _local_step(x0, target, mvec, ln_g, ln_b, small, fetch, emit, start):
    s, d = x0.shape
    tm = _mm_tiles(s)
    row = lambda v: v.reshape(1, -1)
    shift = [row(mvec[i, :d]) for i in range(4)]
    scale = [row(mvec[i, d:2 * d]) for i in range(4)]
    gate = [row(1.0 + mvec[i, 2 * d:]) for i in range(4)]
    lg = [row(ln_g[i]) for i in range(4)]
    lb = [row(ln_b[i]) for i in range(4)]
    mm = functools.partial(_mm, tm=tm)
    mm_w = functools.partial(_mm, tm=1024, tk=min(s, 2048), mode="tn")

    def resid_ln_epilogue(sub):
        def epi(y, xv, gate_v, g_v, b_v, sc_v, sh_v):
            xhat, _ = _ln_stats(ALPHA * xv + gate_v * y)
            xn = xhat * g_v + b_v
            return [y, xn, xn * (1.0 + sc_v) + sh_v]

        rows = [gate[sub], lg[sub], lb[sub], scale[sub + 1], shift[sub + 1]]
        return dict(outs=[F32, F32, MXU_DTYPE], epi=epi, extras=[("full", xs[sub])] + [("row", r) for r in rows])

    xs, ys, big = [x0], [], {}
    h0 = _mod(x0, scale[0], shift[0], start, "mod0")
    big["a_w_in"] = fetch("a_w_in", h0)
    uvpre = mm(h0, big["a_w_in"], mode="nn", name="a_in", outs=[F32], tn=512, tk=1024,
               epi=lambda r, bias: [r + bias], extras=[("row", small["a_b_in"])])
    gated = _spatial_fwd(uvpre, small["a_vn_g"], small["a_vn_b"], small["wc"], small["bias_full"], "a_spatial")
    big["a_w_out"] = fetch("a_w_out", gated)
    y0, x1, h1 = mm(gated, big["a_w_out"], mode="nn", name="a_out", tm=min(s, 1024), tn=d, tk=1024, **resid_ln_epilogue(0))
    ys.append(y0)
    xs.append(x1)
    relu2 = lambda r: [jnp.square(jnp.maximum(r, 0.0))]
    big["up0"] = fetch("up0", h1)
    r0 = mm(h1, big["up0"], mode="nn", name="up0", outs=[MXU_DTYPE], tn=1024, tk=1024, epi=relu2)
    big["down0"] = fetch("down0", r0)
    ys.append(mm(r0, big["down0"], mode="nn", name="down0", outs=[F32], tm=min(s, 1024), tn=1024, tk=2048))
    dils = [dil for _, dil in B_PATTERNS]
    x2, h2, *h2_streams = _resid_ln(xs[1], ys[1], gate[1], lg[1], lb[1], (scale[2], shift[2]), "ln1", [dil for dil in dils if dil > 1])
    h2_streams = [h2] + [a.reshape(s, d) for a in h2_streams]
    xs.append(x2)
    hg, qkvs, o_g, l_g, l_streams = [], [], [], [], []
    big["b_w_qkv"] = fetch("b_w_qkv", h2)
    for g, (_, dil) in enumerate(B_PATTERNS):
        hp = h2_streams[g]
        qkv = mm(hp, big["b_w_qkv"], mode="nn", name=f"qkv{g}", outs=[MXU_DTYPE], tn=768, tk=1024, b_col0=g * 3 * d, n_out=3 * d)
        og, lgv = _attn_fwd(qkv, small["slopes"], dil, f"attn_fwd{g}")
        hg.append(hp)
        qkvs.append(qkv)
        o_g.append(og if dil == 1 else og.reshape(dil, s // dil, d))
        l_g.append(_from_streams(lgv, dil))
        l_streams.append(lgv)
    o_mix = _combine_fwd(o_g, l_g, "combine")
    big["b_w_out"] = fetch("b_w_out", o_mix)
    y2, x3, h3 = mm(o_mix, big["b_w_out"], mode="nn", name="b_out", tm=min(s, 1024), tn=d, tk=1024, **resid_ln_epilogue(2))
    ys.append(y2)
    xs.append(x3)
    big["up1"] = fetch("up1", h3)
    r1 = mm(h3, big["up1"], mode="nn", name="up1", outs=[MXU_DTYPE], tn=1024, tk=1024, epi=relu2)
    big["down1"] = fetch("down1", r1)
    ys.append(mm(r1, big["down1"], mode="nn", name="down1", outs=[F32], tm=min(s, 1024), tn=1024, tk=2048))

    gb, red_ln, red_mod = {}, [None] * 4, [None] * 4

    def mlp_bwd(i, h, r, dyy):
        gb[f"down{i}"] = mm_w(r, dyy, name=f"g_down{i}", outs=[MXU_DTYPE], tn=1024)
        da = mm(dyy, big[f"down{i}"], mode="nt", name=f"d_down{i}", outs=[MXU_DTYPE], tn=1024, tk=1024,
                after=emit(f"down{i}", gb[f"down{i}"]),
                epi=lambda acc, rv: [acc * (2.0 * jnp.sqrt(rv.astype(F32)))], extras=[("full", r)])
        gb[f"up{i}"] = mm_w(h, da, name=f"g_up{i}", outs=[MXU_DTYPE], tn=1024)
        return [mm(da, big[f"up{i}"], mode="nt", name=f"d_up{i}", outs=[F32], tn=1024, tk=1024, after=emit(f"up{i}", gb[f"up{i}"]))]

    def join(sub, dxr, dhs, after=None):
        res = _mod_ln_bwd(dxr, dhs, xs[sub], scale[sub], xs[sub - 1], ys[sub - 1], gate[sub - 1], lg[sub - 1],
                          f"mod_ln_bwd{sub}", after=after)
        red_mod[sub], red_ln[sub - 1] = res[2], res[3]
        return res[0], res[1]

    loss, dxr, dyy, red_ln[3] = _last_ln_loss_bwd(xs[3], ys[3], gate[3], lg[3], lb[3], target, "ln3_loss_bwd")
    dxr, dyy = join(3, dxr, mlp_bwd(1, h3, r1, dyy))
    gb["b_w_out"] = mm_w(o_mix, dyy, name="g_b_out", outs=[MXU_DTYPE], tn=1024, tk=1024)
    do = mm(dyy, big["b_w_out"], mode="nt", name="d_b_out", outs=[F32], tn=1024, tk=1024, after=emit("b_w_out", gb["b_w_out"]))
    parts = _combine_bwd(do, o_mix, l_g, dils, "combine_bwd")
    dhs, gq = [], None
    for g, (_, dil) in enumerate(B_PATTERNS):
        do_g, dd_g = parts[g][0].reshape(s, d), _to_streams(parts[g][1], dil)
        dqkv = _attn_bwd(qkvs[g], do_g, l_streams[g], dd_g, small["slopes"], dil, f"attn_bwd{g}")
        gq = mm_w(hg[g], dqkv, name=f"g_qkv{g}", outs=[MXU_DTYPE], tn=1024, out_col0=g * 3 * d, out_cols=len(B_PATTERNS) * 3 * d, into=gq)
        dh = mm(dqkv, big["b_w_qkv"], mode="nt", name=f"d_qkv{g}", outs=[F32], tn=1024, tk=768, b_col0=g * 3 * d)
        dhs.append(dh if dil == 1 else dh.reshape(dil, s // dil, d))
    gb["b_w_qkv"] = gq
    dxr, dyy = join(2, dxr, dhs, after=emit("b_w_qkv", gb["b_w_qkv"]))
    dxr, dyy = join(1, dxr, mlp_bwd(0, h1, r0, dyy))
    gb["a_w_out"] = mm_w(gated, dyy, name="g_a_out", outs=[MXU_DTYPE], tn=1024)
    dgated = mm(dyy, big["a_w_out"], mode="nt", name="d_a_out", outs=[F32], tn=1024, tk=1024, after=emit("a_w_out", gb["a_w_out"]))
    duv, dws, dbias, dbin, dvg, dvb = _spatial_bwd(uvpre, dgated, small["a_vn_g"], small["a_vn_b"], small["wc"],
                                                   small["wct"], small["bias_full"], "a_spatial_bwd")
    tril = jnp.tril(jnp.ones((CHUNK, CHUNK), bool))
    dws = jnp.where(tril, dws, 0.0).reshape(-1, LANES)
    gb["a_w_in"] = mm_w(h0, duv, name="g_a_in", outs=[MXU_DTYPE], tn=1024, after=emit("a_w_s", dws.astype(MXU_DTYPE)))
    dh = mm(duv, big["a_w_in"], mode="nt", name="d_a_in", outs=[F32], tn=1024, tk=512, after=emit("a_w_in", gb["a_w_in"]))
    dx, red_mod[0] = _mod_bwd(dxr, [dh], xs[0], scale[0], "mod_bwd0")
    dm = [jnp.concatenate([red_mod[i][0], red_mod[i][1], red_ln[i][2]]) for i in range(4)]
    dlg, dlb = [red_ln[i][0] for i in range(4)], [red_ln[i][1] for i in range(4)]

    gsmall = {
        "a_b_in": dbin.reshape(-1), "a_vn_g": dvg.reshape(-1), "a_vn_b": dvb.reshape(-1),
        "a_w_s": dws.reshape(-1),
        "a_b_s": dbias.reshape(CHUNK, A_GROUPS, d // A_GROUPS).sum(-1).T.reshape(-1),
    }
    return loss, dx, gb, jnp.stack(dm), jnp.stack(dlg), jnp.stack(dlb), gsmall


BIG = ("a_w_in", "a_w_out", "up0", "down0", "b_w_qkv", "b_w_out", "up1", "down1")
BIG_KIND = {"a_w_in": "col", "a_w_out": "row", "b_w_qkv": "col", "b_w_out": "row",
            "up0": "col", "up1": "col", "down0": "row", "down1": "row", "a_w_s": "all"}
HALVED = ("a_w_in", "a_w_out", "down0", "b_w_qkv")
SCATTER_GROUPS = (("down1", "up1"), ("b_w_out", "b_w_qkv"), ("down0", "up0"), ("a_w_out", "a_w_in"), ("a_w_s",))
SMALL = ("a_b_in", "a_vn_g", "a_vn_b", "a_b_s")


def kernel(x, c, ada_w, ada_b, ln_g, ln_b, a_w_in, a_b_in, a_vn_g, a_vn_b, a_w_s, a_b_s, a_w_out, b_w_qkv, b_w_out, mlp_w_up, mlp_w_down, loss_target, m_ada_w, m_ada_b, m_ln_g, m_ln_b, m_a_w_in, m_a_b_in, m_a_vn_g, m_a_vn_b, m_a_w_s, m_a_b_s, m_a_w_out, m_b_w_qkv, m_b_w_out, m_mlp_w_up, m_mlp_w_down, v_ada_w, v_ada_b, v_ln_g, v_ln_b, v_a_w_in, v_a_b_in, v_a_vn_g, v_a_vn_b, v_a_w_s, v_a_b_s, v_a_w_out, v_b_w_qkv, v_b_w_out, v_mlp_w_up, v_mlp_w_down):
    s, d = x.shape[1], x.shape[2]
    xi, yi, ci = _me()
    q = 2 * xi + yi
    dev = 2 * q + ci
    nsub = 2 * DEPTH
    cs = ada_w.shape[-1]
    ls = ln_g.shape[-1]

    shards = {
        "a_w_in": a_w_in[0], "a_w_out": a_w_out[0], "b_w_qkv": b_w_qkv[0], "b_w_out": b_w_out[0],
        "up0": mlp_w_up[0], "up1": mlp_w_up[1], "down0": mlp_w_down[0], "down1": mlp_w_down[1],
    }
    cast = [shards[k].astype(MXU_DTYPE) for k in BIG]

    pack = jnp.concatenate([c.reshape(-1), ln_g.reshape(-1), ln_b.reshape(-1)]).reshape(-1, LANES)
    got = _all_gather_small(pack, "gather_small", after=cast).reshape(N_DEV, -1)
    c_all = got[:, :d]
    per_chip = got[0::2]
    ln_g_full = per_chip[:, d:d + nsub * ls].reshape(N_CHIPS, nsub, ls).transpose(1, 0, 2).reshape(nsub, d)
    ln_b_full = per_chip[:, d + nsub * ls:].reshape(N_CHIPS, nsub, ls).transpose(1, 0, 2).reshape(nsub, d)
    m_part = _ada_fwd(c_all, ada_w.reshape(nsub, d, cs), ada_b.reshape(nsub, 1, cs), "ada_fwd")
    m_all = _all_gather_small(m_part.reshape(-1, LANES), "gather_mod").reshape(N_DEV, nsub, N_DEV, cs)
    m_mine = lax.dynamic_index_in_dim(m_all[0::2], dev, axis=2, keepdims=False)
    mvec = m_mine.transpose(1, 0, 2).reshape(nsub, 3 * d)

    halved = {BIG.index(k) for k in HALVED}
    send_sems, recv_sems, shard_thru, lands, token = _gather_start(cast, halved, mvec, "gather_start")

    def fetch(k, after):
        w = BIG.index(k)
        shard, gw = _gather_wait(w, shard_thru[w], lands[w], send_sems, recv_sems, after, f"gather_wait_{k}", w in halved)
        if w in halved:
            gw = _assemble_halves(shard, gw, f"assemble_{k}")
        return gw if BIG_KIND[k] == "col" else gw.reshape(1, -1, gw.shape[-1])

    scattering, pending = {}, {}

    def emit(k, g):
        pending[k] = g
        group = next(gr for gr in SCATTER_GROUPS if k in gr)
        if k != group[-1]:
            return None
        scattering[group] = _scatter_start([pending[m] for m in group], [BIG_KIND[m] for m in group], f"scatter_start_{k}")
        return scattering[group][2][0]

    tril = jnp.tril(jnp.ones((CHUNK, CHUNK), bool))
    wc = jnp.where(tril, a_w_s[0], 0.0).astype(MXU_DTYPE)
    heads = jnp.arange(1, B_HEADS + 1, dtype=F32)
    small = {
        "a_b_in": a_b_in, "a_vn_g": a_vn_g, "a_vn_b": a_vn_b,
        "wc": _side_by_side(wc), "wct": _side_by_side(wc.transpose(0, 2, 1)),
        "bias_full": jnp.repeat(a_b_s[0].T, d // A_GROUPS, axis=1),
        "slopes": jnp.exp2(-8.0 * heads / B_HEADS),
    }

    loss_part, grad_x, gb, dm, dlg, dlb, gsmall = _local_step(x[0], loss_target[0], mvec, ln_g_full, ln_b_full, small, fetch, emit, token)

    weights = dict(ada_w=ada_w, ada_b=ada_b, ln_g=ln_g, ln_b=ln_b, a_w_in=a_w_in, a_b_in=a_b_in, a_vn_g=a_vn_g, a_vn_b=a_vn_b,
                   a_w_s=a_w_s, a_b_s=a_b_s, a_w_out=a_w_out, b_w_qkv=b_w_qkv, b_w_out=b_w_out, mlp_w_up=mlp_w_up, mlp_w_down=mlp_w_down)
    ms = dict(ada_w=m_ada_w, ada_b=m_ada_b, ln_g=m_ln_g, ln_b=m_ln_b, a_w_in=m_a_w_in, a_b_in=m_a_b_in, a_vn_g=m_a_vn_g, a_vn_b=m_a_vn_b,
              a_w_s=m_a_w_s, a_b_s=m_a_b_s, a_w_out=m_a_w_out, b_w_qkv=m_b_w_qkv, b_w_out=m_b_w_out, mlp_w_up=m_mlp_w_up, mlp_w_down=m_mlp_w_down)
    vs = dict(ada_w=v_ada_w, ada_b=v_ada_b, ln_g=v_ln_g, ln_b=v_ln_b, a_w_in=v_a_w_in, a_b_in=v_a_b_in, a_vn_g=v_a_vn_g, a_vn_b=v_a_vn_b,
              a_w_s=v_a_w_s, a_b_s=v_a_b_s, a_w_out=v_a_w_out, b_w_qkv=v_b_w_qkv, b_w_out=v_b_w_out, mlp_w_up=v_mlp_w_up, mlp_w_down=v_mlp_w_down)
    grads, updates = {}, {}

    def update(k):
        updates[k] = _adamw(weights[k], grads[k], ms[k], vs[k], f"adamw_{k}")
        return updates[k][0]

    gfull = {}

    def big_group(group, after):
        bufs = []
        for pair in (group[:2], group[2:]):
            bufs += _scatter_wait(*scattering[pair], [BIG_KIND[m] for m in pair], after, f"scatter_wait_{pair[-1]}")
        parts = [[i] for i, k in enumerate(group) if k == "b_w_qkv"] + [[i for i, k in enumerate(group) if k != "b_w_qkv"]]
        for part in parts:
            fulls = _sum_swap([bufs[i] for i in part], f"sum_swap_{group[part[0]]}")
            gfull.update({group[i]: f.reshape(-1, f.shape[-1]) for i, f in zip(part, fulls)})

    big_group(SCATTER_GROUPS[0] + SCATTER_GROUPS[1], grad_x)
    grads["b_w_qkv"], grads["b_w_out"] = gfull["b_w_qkv"][None], gfull["b_w_out"][None]
    update("b_w_out")
    done = update("b_w_qkv")

    pack_b = jnp.concatenate([dm.reshape(-1), dlg.reshape(-1), dlb.reshape(-1)] + [gsmall[k] for k in SMALL] + [loss_part.reshape(1)])
    n_small = pack_b.shape[0]
    pack_b = jnp.pad(pack_b, (0, -n_small % (256 * LANES)))
    got_b = _all_gather_small(pack_b.reshape(-1, LANES), "gather_small_grads", after=[done]).reshape(N_DEV, -1, LANES)
    tot = _sum_slots(got_b, "sum_small").reshape(-1)
    o = 0
    dm_tot = tot[o:o + nsub * 3 * d].reshape(nsub, 3 * d); o += nsub * 3 * d
    dlg_tot = tot[o:o + nsub * d].reshape(nsub, d); o += nsub * d
    dlb_tot = tot[o:o + nsub * d].reshape(nsub, d); o += nsub * d
    g_small = {}
    for k, ref in zip(SMALL, (a_b_in, a_vn_g, a_vn_b, a_b_s)):
        g_small[k] = tot[o:o + ref.size].reshape(ref.shape); o += ref.size
    loss = tot[o]
    assert o + 1 == n_small
    aws = _scatter_wait(*scattering[("a_w_s",)], ["all"], tot, "scatter_wait_a_w_s")[0]
    g_small["a_w_s"] = _sum_slots(aws, "sum_a_w_s").reshape(a_w_s.shape)
    dm_all = got_b.reshape(N_DEV, -1)[:, :nsub * 3 * d].reshape(N_DEV, nsub, 3 * d)
    dm_cols = lax.dynamic_slice_in_dim(dm_all, q * cs, cs, axis=2).transpose(1, 0, 2)
    grads.update({
        "ada_w": _ada_bwd(c_all.T, dm_cols, "ada_bwd").reshape(ada_w.shape),
        "ada_b": lax.dynamic_slice_in_dim(dm_tot, q * cs, cs, axis=1).reshape(ada_b.shape),
        "ln_g": lax.dynamic_slice_in_dim(dlg_tot, q * ls, ls, axis=1).reshape(ln_g.shape),
        "ln_b": lax.dynamic_slice_in_dim(dlb_tot, q * ls, ls, axis=1).reshape(ln_b.shape),
        **g_small,
    })
    for k in ("ada_b", "ln_g", "ln_b", "a_w_s") + SMALL:
        update(k)
    done = update("ada_w")

    big_group(SCATTER_GROUPS[2] + SCATTER_GROUPS[3], done)
    grads.update({"a_w_in": gfull["a_w_in"][None], "a_w_out": gfull["a_w_out"][None]})
    for k in ("a_w_in", "a_w_out"):
        update(k)
    for k, layers in (("mlp_w_up", ("up0", "up1")), ("mlp_w_down", ("down0", "down1"))):
        res = _adamw_layers(weights[k], [gfull[n] for n in layers], ms[k], vs[k], f"adamw_{k}")
        grads[k], updates[k] = res[0], res[1:]
    names = list(weights)
    return (loss, grad_x[None], *[grads[k] for k in names], *[updates[k][0] for k in names],
            *[updates[k][1] for k in names], *[updates[k][2] for k in names])
```

```python
import functools
import math

import jax
import jax.numpy as jnp
from jax import lax
from jax.experimental import pallas as pl
from jax.experimental.pallas import tpu as pltpu

F32 = jnp.float32
MXU_DTYPE = jnp.bfloat16

DEPTH = 2
CHUNK = 128
A_GROUPS = 16
B_HEADS = 16
HEAD_DIM = 64
B_PATTERNS = ((128, 1), (512, 4), (2048, 16))
SPAN = 128
ALPHA = (2 * DEPTH) ** 0.25
LN_EPS = 1e-5
NEG = -1e30
ATT_SCALE = HEAD_DIM ** -0.5
ADAM_LR, ADAM_B1, ADAM_B2, ADAM_EPS, ADAM_WD, ADAM_STEP = 0.001, 0.9, 0.999, 1e-08, 0.01, 10

N_CHIPS = 4
N_DEV = 8
LANES = 128
SUBLANES = 8
VMEM_LIMIT = 52 * 1024 * 1024
ROW_TILE = 512
MM_ROW_CHUNK = 256
MESH = pl.DeviceIdType.MESH


def _cparams(sem):
    return pltpu.CompilerParams(dimension_semantics=sem, vmem_limit_bytes=VMEM_LIMIT)


def _fold8(v):
    r, c = v.shape
    return jnp.sum(v.reshape(r // SUBLANES, SUBLANES, c), axis=0)


def _gelu(x):
    c = math.sqrt(2.0 / math.pi)
    return 0.5 * x * (1.0 + jnp.tanh(c * (x + 0.044715 * (x * x * x))))


def _gelu_and_grad(x):
    c = math.sqrt(2.0 / math.pi)
    t = jnp.tanh(c * (x + 0.044715 * (x * x * x)))
    return 0.5 * x * (1.0 + t), 0.5 * (1.0 + t) + 0.5 * x * (1.0 - t * t) * c * (1.0 + 3.0 * 0.044715 * x * x)


def _dot(a, b, dims):
    return lax.dot_general(a.astype(MXU_DTYPE), b.astype(MXU_DTYPE), (dims, ((), ())), preferred_element_type=F32)


def _dot_nn(a, b):
    return _dot(a, b, ((1,), (0,)))


def _dot_nt(a, b):
    return _dot(a, b, ((1,), (1,)))


def _dot_tn(a, b):
    return _dot(a, b, ((0,), (0,)))


def _mm(a, b, *, mode, name, outs, tm, tn, tk, epi=None, extras=(), b_col0=0, n_out=None, after=None,
        out_col0=0, out_cols=None, into=None):
    if mode == "nn":
        m, kdim = a.shape
        p, kb, ns = b.shape
        assert kb == kdim and ns % tn == 0 and b_col0 % tn == 0
        n = n_out if n_out is not None else p * ns
        npt, j0 = ns // tn, b_col0 // tn
        a_spec = pl.BlockSpec((tm, tk), lambda i, j, k: (i, k))
        b_spec = pl.BlockSpec((None, tk, tn), lambda i, j, k: ((j + j0) // npt, k, (j + j0) % npt))
        dot = _dot_nn
    elif mode == "nt":
        m, kdim = a.shape
        p, n, ns = b.shape
        assert ns % tk == 0 and b_col0 % tk == 0
        npt, j0 = ns // tk, b_col0 // tk
        a_spec = pl.BlockSpec((tm, tk), lambda i, j, k: (i, k))
        b_spec = pl.BlockSpec((None, tn, tk), lambda i, j, k: ((k + j0) // npt, j, (k + j0) % npt))
        dot = _dot_nt
    else:
        kdim, m = a.shape
        kb, n = b.shape
        assert kb == kdim
        a_spec = pl.BlockSpec((tk, tm), lambda i, j, k: (k, i))
        b_spec = pl.BlockSpec((tk, tn), lambda i, j, k: (k, j))
        dot = _dot_tn
    assert m % tm == 0 and n % tn == 0 and kdim % tk == 0, (name, m, n, kdim, tm, tn, tk)
    nk = kdim // tk
    ex_specs, ex_arrays = [], []
    for kind, arr in extras:
        if kind == "row":
            ex_specs.append(pl.BlockSpec((1, tn), lambda i, j, k: (0, j)))
        else:
            ex_specs.append(pl.BlockSpec((tm, tn), lambda i, j, k: (i, j)))
        ex_arrays.append(arr)
    n_ex, n_o = len(ex_arrays), len(outs)
    deps = [d for d in (after, into) if d is not None]
    n_dep = len(deps)
    j_out = out_col0 // tn
    assert out_col0 % tn == 0 and (into is None or len(outs) == 1)

    def body(a_ref, b_ref, *rest):
        ex_refs, o_refs = rest[:n_ex], rest[n_ex + n_dep:n_ex + n_dep + n_o]
        k = pl.program_id(2)

        chunks = [slice(r0, r0 + min(tm, MM_ROW_CHUNK)) for r0 in range(0, tm, min(tm, MM_ROW_CHUNK))]

        def part(rows):
            return dot(a_ref[:, rows] if mode == "tn" else a_ref[rows, :], b_ref[...])

        def finish(r, rows):
            exs = [e[...] if kind == "row" else e[rows, :] for (kind, _), e in zip(extras, ex_refs)]
            vals = epi(r, *exs) if epi is not None else [r]
            for o, v in zip(o_refs, vals):
                o[rows, :] = v.astype(o.dtype)

        if nk == 1:
            for rows in chunks:
                finish(part(rows), rows)
            return
        acc = rest[n_ex + n_dep + n_o]

        @pl.when(k == 0)
        def _():
            for rows in chunks:
                acc[rows, :] = part(rows)

        @pl.when((k > 0) & (k < nk - 1))
        def _():
            for rows in chunks:
                acc[rows, :] += part(rows)

        @pl.when(k == nk - 1)
        def _():
            for rows in chunks:
                finish(acc[rows, :] + part(rows), rows)

    res = pl.pallas_call(
        body,
        grid=(m // tm, n // tn, nk),
        in_specs=[a_spec, b_spec] + ex_specs + [pl.BlockSpec(memory_space=pl.ANY)] * n_dep,
        out_specs=[pl.BlockSpec((tm, tn), lambda i, j, k: (i, j + j_out)) for _ in outs],
        out_shape=[jax.ShapeDtypeStruct((m, out_cols or n), dt) for dt in outs],
        input_output_aliases={} if into is None else {2 + n_ex + n_dep - 1: 0},
        scratch_shapes=[pltpu.VMEM((tm, tn), F32)] if nk > 1 else [],
        name=name,
        compiler_params=_cparams(("parallel", "parallel", "arbitrary")),
    )(a, b, *ex_arrays, *deps)
    return res if len(outs) > 1 else res[0]


def _rows(body, n_rows, tr, ins, outs, name, scratch=()):
    def spec(kind, shape):
        if kind == "blk":
            return pl.BlockSpec((tr,) + tuple(shape[1:]), lambda i: (i,) + (0,) * (len(shape) - 1))
        if kind == "dep":
            return pl.BlockSpec(memory_space=pl.ANY)
        if kind == "str":
            return pl.BlockSpec((shape[0], tr // shape[0], shape[2]), lambda i: (0, i, 0))
        return pl.BlockSpec(tuple(shape), lambda i: (0,) * len(shape))

    return pl.pallas_call(
        body,
        grid=(n_rows // tr,),
        in_specs=[spec(k, a.shape) for k, a in ins],
        out_specs=[spec(k, s) for k, s, _ in outs],
        out_shape=[jax.ShapeDtypeStruct(tuple(s), d) for _, s, d in outs],
        scratch_shapes=list(scratch),
        name=name,
        compiler_params=_cparams(("arbitrary",)),
    )(*[a for _, a in ins])


def _ln_stats(z):
    mu = jnp.mean(z, axis=-1, keepdims=True)
    zc = z - mu
    var = jnp.mean(zc * zc, axis=-1, keepdims=True)
    rstd = lax.rsqrt(var + LN_EPS)
    return zc * rstd, rstd


def _stream_scratch(c):
    return pltpu.VMEM((2 * (c // LANES), ROW_TILE, LANES), F32)


def _streams_in(ref3, scr):
    dil, n, c = ref3.shape
    for r in range(dil):
        for j in range(c // LANES):
            scr.at[j][pl.ds(r, n, stride=dil), :] = ref3[r, :, j * LANES:(j + 1) * LANES].astype(F32)
    return jnp.concatenate([scr[j] for j in range(c // LANES)], axis=1)


def _streams_out(val, ref3, scr):
    dil, n, c = ref3.shape
    nj = c // LANES
    for j in range(nj):
        scr[j] = val[:, j * LANES:(j + 1) * LANES].astype(F32)
    if dil == 16:
        seg = dil * n // 4
        for j in range(nj):
            for s0 in range(4):
                scr[nj + j, s0 * seg:(s0 + 1) * seg, :] = scr.at[j][pl.ds(s0, seg, stride=4), :]
        for r in range(dil):
            s1, s0 = divmod(r, 4)
            for j in range(nj):
                ref3[r, :, j * LANES:(j + 1) * LANES] = scr.at[nj + j][pl.ds(s0 * seg + s1, n, stride=4), :].astype(ref3.dtype)
        return
    for r in range(dil):
        for j in range(nj):
            ref3[r, :, j * LANES:(j + 1) * LANES] = scr.at[j][pl.ds(r, n, stride=dil), :].astype(ref3.dtype)


def _mod(x, scale, shift, after, name):
    s, d = x.shape

    def body(x_ref, sc_ref, sh_ref, dep_ref, h_ref):
        h_ref[...] = (x_ref[...] * (1.0 + sc_ref[...]) + sh_ref[...]).astype(h_ref.dtype)

    return _rows(body, s, ROW_TILE, [("blk", x), ("all", scale), ("all", shift), ("dep", after)], [("blk", (s, d), MXU_DTYPE)], name)[0]


def _resid_ln(x, y, gate, g, b, nxt, name, dils=()):
    s, d = x.shape

    def body(x_ref, y_ref, gate_ref, g_ref, b_ref, sc_ref, sh_ref, xn_ref, h_ref, *rest):
        z = ALPHA * x_ref[...] + gate_ref[...] * y_ref[...]
        xhat, _ = _ln_stats(z)
        xn = xhat * g_ref[...] + b_ref[...]
        xn_ref[...] = xn
        h = xn * (1.0 + sc_ref[...]) + sh_ref[...]
        h_ref[...] = h.astype(h_ref.dtype)
        for hs_ref in rest[:len(dils)]:
            _streams_out(h, hs_ref, rest[-1])

    return _rows(body, s, ROW_TILE,
                 [("blk", x), ("blk", y), ("all", gate), ("all", g), ("all", b), ("all", nxt[0]), ("all", nxt[1])],
                 [("blk", (s, d), F32), ("blk", (s, d), MXU_DTYPE)] + [("str", (dil, s // dil, d), MXU_DTYPE) for dil in dils], name,
                 scratch=[_stream_scratch(d)] if dils else [])


def _mod_bwd(dxr, dhs, x, scale, name, after=None):
    s, d = x.shape
    n_dh = len(dhs)
    n_dep = 0 if after is None else 1

    def body(dxr_ref, *rest):
        dh_refs = rest[:n_dh]
        x_ref, sc_ref, dx_ref, red_ref, a_sh, a_sc = rest[n_dh:n_dh + 2] + rest[n_dh + 2 + n_dep:]
        i = pl.program_id(0)

        @pl.when(i == 0)
        def _():
            a_sh[...] = jnp.zeros_like(a_sh)
            a_sc[...] = jnp.zeros_like(a_sc)

        dh = dh_refs[0][...]
        for r in dh_refs[1:]:
            dh = dh + r[...]
        dx_ref[...] = dxr_ref[...] + dh * (1.0 + sc_ref[...])
        a_sh[...] += _fold8(dh)
        a_sc[...] += _fold8(dh * x_ref[...])

        @pl.when(i == pl.num_programs(0) - 1)
        def _():
            red_ref[...] = jnp.zeros_like(red_ref)
            red_ref[0:1, :] = jnp.sum(a_sh[...], axis=0, keepdims=True)
            red_ref[1:2, :] = jnp.sum(a_sc[...], axis=0, keepdims=True)

    return _rows(body, s, ROW_TILE, [("blk", dxr)] + [("blk", h) for h in dhs] + [("blk", x), ("all", scale)] + [("dep", after)] * n_dep,
                 [("blk", (s, d), F32), ("all", (SUBLANES, d), F32)], name,
                 scratch=[pltpu.VMEM((SUBLANES, d), F32)] * 2)


def _last_ln_loss_bwd(x, y, gate, g, b, target, name):
    s, d = x.shape

    def body(x_ref, y_ref, gate_ref, g_ref, b_ref, t_ref, l_ref, dxr_ref, dyy_ref, red_ref, a_l, a_g, a_b, a_gate):
        i = pl.program_id(0)

        @pl.when(i == 0)
        def _():
            for a in (a_l, a_g, a_b, a_gate):
                a[...] = jnp.zeros_like(a)

        yv = y_ref[...]
        z = ALPHA * x_ref[...] + gate_ref[...] * yv
        xhat, rstd = _ln_stats(z)
        e = xhat * g_ref[...] + b_ref[...] - t_ref[...]
        a_l[...] += _fold8(e * e)
        dxo_v = e * (1.0 / d)
        dxh = dxo_v * g_ref[...]
        dz = rstd * (dxh - jnp.mean(dxh, axis=-1, keepdims=True) - xhat * jnp.mean(dxh * xhat, axis=-1, keepdims=True))
        dxr_ref[...] = ALPHA * dz
        dyy_ref[...] = (gate_ref[...] * dz).astype(dyy_ref.dtype)
        a_g[...] += _fold8(dxo_v * xhat)
        a_b[...] += _fold8(dxo_v)
        a_gate[...] += _fold8(dz * yv)

        @pl.when(i == pl.num_programs(0) - 1)
        def _():
            l_ref[...] = jnp.full(l_ref.shape, 0.5 / d, F32) * jnp.sum(a_l[...])
            red_ref[...] = jnp.zeros_like(red_ref)
            red_ref[0:1, :] = jnp.sum(a_g[...], axis=0, keepdims=True)
            red_ref[1:2, :] = jnp.sum(a_b[...], axis=0, keepdims=True)
            red_ref[2:3, :] = jnp.sum(a_gate[...], axis=0, keepdims=True)

    l, dxr, dyy, red = _rows(
        body, s, ROW_TILE, [("blk", x), ("blk", y), ("all", gate), ("all", g), ("all", b), ("blk", target)],
        [("all", (SUBLANES, LANES), F32), ("blk", (s, d), F32), ("blk", (s, d), MXU_DTYPE), ("all", (SUBLANES, d), F32)], name,
        scratch=[pltpu.VMEM((SUBLANES, d), F32)] * 4)
    return l[0, 0], dxr, dyy, red


def _mod_ln_bwd(dxr, dhs, x, scale, x_in, y, gate, g, name, after=None):
    s, d = x.shape
    n_dh = len(dhs)
    n_dep = 0 if after is None else 1

    def body(dxr_ref, *rest):
        dh_refs = rest[:n_dh]
        x_ref, sc_ref, xin_ref, y_ref, gate_ref, g_ref = rest[n_dh:n_dh + 6]
        dxr_out, dyy_ref, red_mod, red_ln, a_sh, a_sc, a_g, a_b, a_gate = rest[n_dh + 6 + n_dep:n_dh + 15 + n_dep]
        i = pl.program_id(0)

        @pl.when(i == 0)
        def _():
            for a in (a_sh, a_sc, a_g, a_b, a_gate):
                a[...] = jnp.zeros_like(a)

        dh = dh_refs[0][...]
        for r in dh_refs[1:]:
            dh = dh + (r[...] if len(r.shape) == 2 else _streams_in(r, rest[-1]))
        xv = x_ref[...]
        dxo_v = dxr_ref[...] + dh * (1.0 + sc_ref[...])
        a_sh[...] += _fold8(dh)
        a_sc[...] += _fold8(dh * xv)
        yv = y_ref[...]
        z = ALPHA * xin_ref[...] + gate_ref[...] * yv
        xhat, rstd = _ln_stats(z)
        dxh = dxo_v * g_ref[...]
        dz = rstd * (dxh - jnp.mean(dxh, axis=-1, keepdims=True) - xhat * jnp.mean(dxh * xhat, axis=-1, keepdims=True))
        dxr_out[...] = ALPHA * dz
        dyy_ref[...] = (gate_ref[...] * dz).astype(dyy_ref.dtype)
        a_g[...] += _fold8(dxo_v * xhat)
        a_b[...] += _fold8(dxo_v)
        a_gate[...] += _fold8(dz * yv)

        @pl.when(i == pl.num_programs(0) - 1)
        def _():
            red_mod[...] = jnp.zeros_like(red_mod)
            red_mod[0:1, :] = jnp.sum(a_sh[...], axis=0, keepdims=True)
            red_mod[1:2, :] = jnp.sum(a_sc[...], axis=0, keepdims=True)
            red_ln[...] = jnp.zeros_like(red_ln)
            red_ln[0:1, :] = jnp.sum(a_g[...], axis=0, keepdims=True)
            red_ln[1:2, :] = jnp.sum(a_b[...], axis=0, keepdims=True)
            red_ln[2:3, :] = jnp.sum(a_gate[...], axis=0, keepdims=True)

    ins = ([("blk", dxr)] + [("blk" if h.ndim == 2 else "str", h) for h in dhs]
           + [("blk", x), ("all", scale), ("blk", x_in), ("blk", y), ("all", gate), ("all", g)] + [("dep", after)] * n_dep)
    return _rows(body, s, ROW_TILE, ins,
                 [("blk", (s, d), F32), ("blk", (s, d), MXU_DTYPE), ("all", (SUBLANES, d), F32), ("all", (SUBLANES, d), F32)], name,
                 scratch=[pltpu.VMEM((SUBLANES, d), F32)] * 5 + [_stream_scratch(d)] * any(h.ndim == 3 for h in dhs))


def _left_half(shape):
    return lax.broadcasted_iota(jnp.int32, shape, 1) < (LANES // 2)


CHUNKS_PER_STEP = 2


def _chunks_of_step():
    return [slice(i * CHUNK, (i + 1) * CHUNK) for i in range(CHUNKS_PER_STEP)]


def _split_groups(v):
    left = _left_half(v.shape)
    return jnp.concatenate([jnp.where(left, v, 0.0), jnp.where(left, 0.0, v)], axis=0)


def _spatial_z(vn, wc_ref, bias_ref, j):
    return _dot_nn(wc_ref[j], _split_groups(vn[:, j * LANES:(j + 1) * LANES])) + bias_ref[:, j * LANES:(j + 1) * LANES]


def _spatial_fwd(uvpre, vn_g, vn_b, wc, bias_full, name):
    s, d2 = uvpre.shape
    d = d2 // 2

    def body(uv_ref, g_ref, b_ref, wc_ref, bias_ref, out_ref):
        for rows in _chunks_of_step():
            u = _gelu(uv_ref[rows, :d])
            v = _gelu(uv_ref[rows, d:])
            vh, _ = _ln_stats(v)
            vn = vh * g_ref[...] + b_ref[...]
            for j in range(d // LANES):
                z = _spatial_z(vn, wc_ref, bias_ref, j)
                out_ref[rows, j * LANES:(j + 1) * LANES] = (u[:, j * LANES:(j + 1) * LANES] * z).astype(out_ref.dtype)

    return _rows(body, s, CHUNKS_PER_STEP * CHUNK, [("blk", uvpre), ("all", vn_g), ("all", vn_b), ("all", wc), ("all", bias_full)],
                 [("blk", (s, d), MXU_DTYPE)], name)[0]


def _spatial_bwd(uvpre, dgated, vn_g, vn_b, wc, wct, bias_full, name):
    s, d2 = uvpre.shape
    d = d2 // 2

    def body(uv_ref, dg_ref, g_ref, b_ref, wc_ref, wct_ref, bias_ref,
             duv_ref, dws_ref, dbias_ref, dbin_ref, dvg_ref, dvb_ref, dvn_buf, a_bin, a_vg, a_vb):
        i = pl.program_id(0)

        @pl.when(i == 0)
        def _():
            dws_ref[...] = jnp.zeros_like(dws_ref)
            dbias_ref[...] = jnp.zeros_like(dbias_ref)
            a_bin[...] = jnp.zeros_like(a_bin)
            a_vg[...] = jnp.zeros_like(a_vg)
            a_vb[...] = jnp.zeros_like(a_vb)

        for rows in _chunks_of_step():
            u, u_grad = _gelu_and_grad(uv_ref[rows, :d])
            v, v_grad = _gelu_and_grad(uv_ref[rows, d:])
            vh, rstd = _ln_stats(v)
            vn = vh * g_ref[...] + b_ref[...]
            dg = dg_ref[rows, :]
            dzz = dg * u
            dbias_ref[...] += dzz
            for j in range(d // LANES):
                cols = slice(j * LANES, (j + 1) * LANES)
                z = _spatial_z(vn, wc_ref, bias_ref, j)
                dup = dg[:, cols] * z * u_grad[:, cols]
                duv_ref[rows, cols] = dup.astype(duv_ref.dtype)
                a_bin[:, cols] += _fold8(dup)
                dz2 = _split_groups(dzz[:, cols])
                dvn_buf[:, cols] = _dot_nn(wct_ref[j], dz2)
                dw2 = _dot_nt(dz2, vn[:, cols])
                dws_ref[2 * j] += dw2[:CHUNK]
                dws_ref[2 * j + 1] += dw2[CHUNK:]
            dvn = dvn_buf[...]
            a_vg[...] += _fold8(dvn * vh)
            a_vb[...] += _fold8(dvn)
            dvh = dvn * g_ref[...]
            dv = rstd * (dvh - jnp.mean(dvh, axis=-1, keepdims=True) - vh * jnp.mean(dvh * vh, axis=-1, keepdims=True))
            dvp = dv * v_grad
            duv_ref[rows, d:] = dvp.astype(duv_ref.dtype)
            a_bin[:, d:] += _fold8(dvp)

        @pl.when(i == pl.num_programs(0) - 1)
        def _():
            dbin_ref[...] = jnp.sum(a_bin[...], axis=0, keepdims=True)
            dvg_ref[...] = jnp.sum(a_vg[...], axis=0, keepdims=True)
            dvb_ref[...] = jnp.sum(a_vb[...], axis=0, keepdims=True)

    return _rows(body, s, CHUNKS_PER_STEP * CHUNK,
                 [("blk", uvpre), ("blk", dgated), ("all", vn_g), ("all", vn_b), ("all", wc), ("all", wct), ("all", bias_full)],
                 [("blk", (s, d2), MXU_DTYPE), ("all", (A_GROUPS, CHUNK, CHUNK), F32), ("all", (CHUNK, d), F32),
                  ("all", (1, d2), F32), ("all", (1, d), F32), ("all", (1, d), F32)], name,
                 scratch=[pltpu.VMEM((CHUNK, d), F32), pltpu.VMEM((SUBLANES, d2), F32),
                          pltpu.VMEM((SUBLANES, d), F32), pltpu.VMEM((SUBLANES, d), F32)])


def _head_mask(v, h):
    lane = lax.broadcasted_iota(jnp.int32, v.shape, 1)
    return jnp.where((lane >= h * HEAD_DIM) & (lane < (h + 1) * HEAD_DIM), v, jnp.zeros_like(v))


def _att_bias(slopes, dil):
    qi = lax.broadcasted_iota(jnp.int32, (SPAN, SPAN), 0)
    ki = lax.broadcasted_iota(jnp.int32, (SPAN, SPAN), 1)
    sl = slopes[:, None, None]
    cur = jnp.where(ki <= qi, -sl * (float(dil) * (qi - ki).astype(F32)), NEG)
    prev = jnp.where(ki >= qi, -sl * (float(dil) * (SPAN + qi - ki).astype(F32)), NEG)
    absent = jnp.full_like(prev, NEG)
    pairs = slopes.shape[0] // 2

    def fwd(pv):
        return jnp.concatenate([cur, pv], axis=2).reshape(pairs, 2 * SPAN, 2 * SPAN)

    def bwd(pv):
        return jnp.concatenate([cur.reshape(pairs, 2 * SPAN, SPAN), pv.reshape(pairs, 2 * SPAN, SPAN)], axis=1)

    return jnp.stack([fwd(absent), fwd(prev)]), jnp.stack([bwd(absent), bwd(prev)])


ATT_GROUP = 4


def _att_group(s, dil):
    nb = s // (dil * SPAN)
    grp = min(ATT_GROUP, nb)
    assert nb % grp == 0
    return nb, grp


def _att_specs(s, d, dil, kinds):
    nb, grp = _att_group(s, dil)

    def spec(part, which):
        if which == "group":
            return pl.BlockSpec((grp * SPAN, d), lambda b: (b, part))
        if which == "prev":
            return pl.BlockSpec((SPAN, d), lambda b: (jnp.where((grp * b) % nb == 0, grp * b, grp * b - 1), part))
        return pl.BlockSpec((SPAN, d), lambda b: (jnp.where((grp * b + grp - 1) % nb == nb - 1, grp * b + grp - 1, grp * b + grp), part))

    return [spec(part, which) for part, which in kinds]


def _head_col(v, head):
    return v[:, head:head + 1]


def _expand_heads(w, j):
    shape = (w.shape[0], LANES)
    return jnp.where(_left_half(shape), jnp.broadcast_to(_head_col(w, 2 * j), shape), jnp.broadcast_to(_head_col(w, 2 * j + 1), shape))


def _attn_fwd(qkv, slopes, dil, name):
    s, d3 = qkv.shape
    d = d3 // 3
    nb, grp = _att_group(s, dil)
    table, _ = _att_bias(slopes, dil)

    def body(q_ref, k_ref, kp_ref, v_ref, vp_ref, tb_ref, o_ref, l_ref):
        b = pl.program_id(0)
        left = _left_half((SPAN, LANES))
        lane = lax.broadcasted_iota(jnp.int32, (SPAN, LANES), 1)
        for sub in range(grp):
            rows, before = slice(sub * SPAN, (sub + 1) * SPAN), slice((sub - 1) * SPAN, sub * SPAN)
            variant = jnp.where((grp * b) % nb == 0, 0, 1) if sub == 0 else 1
            lses = jnp.zeros((SPAN, LANES), F32)
            for hp in range(d // LANES):
                cols = slice(hp * LANES, (hp + 1) * LANES)
                q = q_ref[rows, cols]
                q2 = jnp.concatenate([_head_mask(q, 0), _head_mask(q, 1)], axis=0) * ATT_SCALE
                k2 = jnp.concatenate([k_ref[rows, cols], kp_ref[:, cols] if sub == 0 else k_ref[before, cols]], axis=0)
                v2 = jnp.concatenate([v_ref[rows, cols], vp_ref[:, cols] if sub == 0 else v_ref[before, cols]], axis=0)
                sc = _dot_nt(q2, k2) + tb_ref[variant, hp]
                m = jnp.max(sc, axis=-1, keepdims=True)
                p = jnp.exp(sc - m)
                l = jnp.sum(p, axis=-1, keepdims=True)
                r = _dot_nn(p, v2) * (1.0 / l)
                lse = m + jnp.log(l)
                o_ref[rows, cols] = jnp.where(left, r[:SPAN], r[SPAN:])
                lses = jnp.where(lane == 2 * hp, lse[:SPAN], jnp.where(lane == 2 * hp + 1, lse[SPAN:], lses))
            l_ref[rows, :] = lses

    specs = _att_specs(s, d, dil, [(0, "group"), (1, "group"), (1, "prev"), (2, "group"), (2, "prev")])
    return pl.pallas_call(
        body,
        grid=(s // (grp * SPAN),),
        in_specs=specs + [pl.BlockSpec(table.shape, lambda b: (0, 0, 0, 0))],
        out_specs=[pl.BlockSpec((grp * SPAN, d), lambda b: (b, 0)), pl.BlockSpec((grp * SPAN, LANES), lambda b: (b, 0))],
        out_shape=[jax.ShapeDtypeStruct((s, d), F32), jax.ShapeDtypeStruct((s, LANES), F32)],
        name=name,
        compiler_params=_cparams(("parallel",)),
    )(qkv, qkv, qkv, qkv, qkv, table)


def _attn_bwd(qkv, do, lse, dd, slopes, dil, name):
    s, d3 = qkv.shape
    d = d3 // 3
    nb, grp = _att_group(s, dil)
    _, table = _att_bias(slopes, dil)

    def cols_stacked(cur, nxt, hp):
        return jnp.concatenate([jnp.broadcast_to(_head_col(a, 2 * hp + h), (SPAN, LANES)) for a in (cur, nxt) for h in range(2)], axis=0)

    def body(k_ref, v_ref, q_ref, qn_ref, do_ref, don_ref, l_ref, ln_ref, dd_ref, ddn_ref, tb_ref, out_ref, carry):
        b = pl.program_id(0)

        @pl.when(b == 0)
        def _():
            carry[...] = jnp.zeros_like(carry)

        wide = 2 * LANES
        head_of_lane = (lax.broadcasted_iota(jnp.int32, (SPAN, wide), 1) % LANES) // HEAD_DIM
        zero = jnp.zeros((SPAN, LANES), k_ref.dtype)

        def heads_stacked2(cur, nxt):
            return jnp.concatenate([jnp.where(head_of_lane == h, a, jnp.zeros_like(a)) for a in (cur, nxt) for h in range(2)], axis=0)

        def block_diagonal(a, b):
            return jnp.concatenate([jnp.concatenate([a, zero], axis=1), jnp.concatenate([zero, b], axis=1)], axis=0)

        for sub in range(grp):
            rows, after = slice(sub * SPAN, (sub + 1) * SPAN), slice((sub + 1) * SPAN, (sub + 2) * SPAN)
            last = sub == grp - 1
            variant = jnp.where((grp * b + sub) % nb == nb - 1, 0, 1) if last else 1
            lse_c, dd_c = l_ref[rows, :], dd_ref[rows, :]
            lse_n, dd_n = (ln_ref[...], ddn_ref[...]) if last else (l_ref[after, :], dd_ref[after, :])
            for hp2 in range(d // wide):
                cols = slice(hp2 * wide, (hp2 + 1) * wide)
                pa, pb = 2 * hp2, 2 * hp2 + 1
                ca, cb = slice(pa * LANES, (pa + 1) * LANES), slice(pb * LANES, (pb + 1) * LANES)
                kbd = block_diagonal(k_ref[rows, ca], k_ref[rows, cb])
                vbd = block_diagonal(v_ref[rows, ca], v_ref[rows, cb])
                q4 = heads_stacked2(q_ref[rows, cols], qn_ref[:, cols] if last else q_ref[after, cols])
                do4 = heads_stacked2(do_ref[rows, cols], don_ref[:, cols] if last else do_ref[after, cols])
                bias = jnp.concatenate([tb_ref[variant, pa], tb_ref[variant, pb]], axis=1)
                lse2 = jnp.concatenate([cols_stacked(lse_c, lse_n, pa), cols_stacked(lse_c, lse_n, pb)], axis=1)
                dd2 = jnp.concatenate([cols_stacked(dd_c, dd_n, pa), cols_stacked(dd_c, dd_n, pb)], axis=1)
                p = jnp.exp(_dot_nt(q4 * ATT_SCALE, kbd) + bias - lse2)
                ds = p * (_dot_nt(do4, vbd) - dd2)
                dq4 = _dot_nn(ds, kbd)
                left = head_of_lane == 0
                dq_cur = jnp.where(left, dq4[:SPAN], dq4[SPAN:2 * SPAN]) + carry[:, cols]
                carry[:, cols] = jnp.where(left, dq4[2 * SPAN:3 * SPAN], dq4[3 * SPAN:])
                out_ref[rows, cols] = (dq_cur * ATT_SCALE).astype(out_ref.dtype)
                for pair, lanes in ((pa, slice(0, LANES)), (pb, slice(LANES, wide))):
                    out_ref[rows, d + pair * LANES:d + (pair + 1) * LANES] = (_dot_tn(ds[:, lanes], q4[:, lanes]) * ATT_SCALE).astype(out_ref.dtype)
                    out_ref[rows, 2 * d + pair * LANES:2 * d + (pair + 1) * LANES] = _dot_tn(p[:, lanes], do4[:, lanes]).astype(out_ref.dtype)

    qkv_specs = _att_specs(s, d, dil, [(1, "group"), (2, "group"), (0, "group"), (0, "next")])
    wide = _att_specs(s, d, dil, [(0, "group"), (0, "next")])
    heads = _att_specs(s, LANES, dil, [(0, "group"), (0, "next")])
    return pl.pallas_call(
        body,
        grid=(s // (grp * SPAN),),
        in_specs=qkv_specs + wide + heads + heads + [pl.BlockSpec(table.shape, lambda b: (0, 0, 0, 0))],
        out_specs=pl.BlockSpec((grp * SPAN, d3), lambda b: (b, 0)),
        out_shape=jax.ShapeDtypeStruct((s, d3), MXU_DTYPE),
        scratch_shapes=[pltpu.VMEM((SPAN, d), F32)],
        name=name,
        compiler_params=_cparams(("arbitrary",)),
    )(qkv, qkv, qkv, qkv, do, do, lse, lse, dd, dd, table)


def _mix_weights(l_refs):
    ls = [r[...] for r in l_refs]
    m = functools.reduce(jnp.maximum, ls)
    es = [jnp.exp(l - m) for l in ls]
    tot = functools.reduce(lambda a, c: a + c, es)
    return [e / tot for e in es]


def _combine_fwd(os_, ls_, name):
    s, d = ls_[0].shape[0], os_[0].shape[-1]
    n = len(os_)
    n_str = sum(o.ndim == 3 for o in os_)

    def body(*refs):
        o_refs, l_refs, out_ref, scrs = refs[:n], refs[n:2 * n], refs[2 * n], list(refs[2 * n + 1:])
        ws = _mix_weights(l_refs)
        os_v = [o if len(o.shape) == 2 else _streams_in(o, scrs.pop()) for o in o_refs]
        for j in range(d // LANES):
            cols = slice(j * LANES, (j + 1) * LANES)
            acc = _expand_heads(ws[0], j) * os_v[0][:, cols]
            for w, o in zip(ws[1:], os_v[1:]):
                acc = acc + _expand_heads(w, j) * o[:, cols]
            out_ref[:, cols] = acc

    return _rows(body, s, ROW_TILE, [("blk" if a.ndim == 2 else "str", a) for a in os_] + [("blk", a) for a in ls_],
                 [("blk", (s, d), F32)], name, scratch=[_stream_scratch(d)] * n_str)[0]


def _combine_bwd(do, o, ls_, dils, name):
    s, d = o.shape
    n = len(ls_)
    sel = (lax.broadcasted_iota(jnp.int32, (d, LANES), 0) // HEAD_DIM == lax.broadcasted_iota(jnp.int32, (d, LANES), 1)).astype(F32)

    def body(do_ref, o_ref, *rest):
        l_refs, sel_ref, outs = rest[:n], rest[n], rest[n + 1:n + 1 + 2 * n]
        ws = _mix_weights(l_refs)
        dov = do_ref[...]
        r = jnp.dot(dov * o_ref[...], sel_ref[...], precision=lax.Precision.HIGHEST, preferred_element_type=F32)
        for g in range(n):
            outs[2 * g + 1][...] = ws[g] * r
            parts = [_expand_heads(ws[g], j) * dov[:, j * LANES:(j + 1) * LANES] for j in range(d // LANES)]
            if dils[g] == 1:
                for j, part in enumerate(parts):
                    outs[2 * g][:, j * LANES:(j + 1) * LANES] = part.astype(outs[2 * g].dtype)
            else:
                _streams_out(jnp.concatenate(parts, axis=1), outs[2 * g], rest[-1])

    outs = []
    for dil in dils:
        outs += [("blk", (s, d), MXU_DTYPE) if dil == 1 else ("str", (dil, s // dil, d), MXU_DTYPE), ("blk", (s, LANES), F32)]
    res = _rows(body, s, ROW_TILE, [("blk", do), ("blk", o)] + [("blk", l) for l in ls_] + [("all", sel)], outs, name,
                scratch=[_stream_scratch(d)])
    return [(res[2 * g], res[2 * g + 1]) for g in range(n)]


def _ada_fwd(c_all, w, b, name):
    nsub, d, cs = w.shape

    def body(c_ref, w_ref, b_ref, o_ref):
        cv = c_ref[...]
        sc = cv * (1.0 / (1.0 + jnp.exp(-cv)))
        o_ref[...] = _dot_nn(sc, w_ref[...]) + b_ref[...]

    return pl.pallas_call(
        body,
        grid=(nsub,),
        in_specs=[pl.BlockSpec(c_all.shape, lambda i: (0, 0)), pl.BlockSpec((None, d, cs), lambda i: (i, 0, 0)),
                  pl.BlockSpec((None, 1, cs), lambda i: (i, 0, 0))],
        out_specs=pl.BlockSpec((None, N_DEV, cs), lambda i: (i, 0, 0)),
        out_shape=jax.ShapeDtypeStruct((nsub, N_DEV, cs), F32),
        name=name,
        compiler_params=_cparams(("parallel",)),
    )(c_all, w, b)


def _ada_bwd(c_all_t, dm, name):
    d, nb = c_all_t.shape
    nsub, _, cs = dm.shape

    def body(c_ref, dm_ref, o_ref):
        cv = c_ref[...]
        sc = cv * (1.0 / (1.0 + jnp.exp(-cv)))
        acc = sc[:, 0:1] * dm_ref[0:1, :]
        for bi in range(1, nb):
            acc = acc + sc[:, bi:bi + 1] * dm_ref[bi:bi + 1, :]
        o_ref[...] = acc

    return pl.pallas_call(
        body,
        grid=(nsub,),
        in_specs=[pl.BlockSpec(c_all_t.shape, lambda i: (0, 0)), pl.BlockSpec((None, nb, cs), lambda i: (i, 0, 0))],
        out_specs=pl.BlockSpec((None, d, cs), lambda i: (i, 0, 0)),
        out_shape=jax.ShapeDtypeStruct((nsub, d, cs), F32),
        name=name,
        compiler_params=_cparams(("parallel",)),
    )(c_all_t, dm)


def _row_tile(r, row_elems, block_elems=256 * 1024):
    t = 2 * SUBLANES
    if r % t:
        return r
    while t * 2 * row_elems <= block_elems and r % (t * 2) == 0:
        t *= 2
    return t


def _adamw(w, g, m, v, name):
    shape = w.shape
    c = shape[-1]
    r = w.size // c
    tr = _row_tile(r, c, 512 * 1024)
    w2, g2, m2, v2 = [a.reshape(r, c) for a in (w, g, m, v)]
    bc1 = 1.0 - ADAM_B1 ** ADAM_STEP
    bc2 = 1.0 - ADAM_B2 ** ADAM_STEP

    def body(w_ref, g_ref, m_ref, v_ref, d_ref, nm_ref, nv_ref):
        gv = g_ref[...]
        nm = ADAM_B1 * m_ref[...] + (1.0 - ADAM_B1) * gv
        nv = ADAM_B2 * v_ref[...] + (1.0 - ADAM_B2) * (gv * gv)
        d_ref[...] = -ADAM_LR * ((nm / bc1) / (jnp.sqrt(nv / bc2) + ADAM_EPS) + ADAM_WD * w_ref[...])
        nm_ref[...] = nm
        nv_ref[...] = nv

    res = _rows(body, r, tr, [("blk", a) for a in (w2, g2, m2, v2)], [("blk", (r, c), F32)] * 3, name)
    return [a.reshape(shape) for a in res]


def _sum_slots(buf, name):
    n, r, c = buf.shape
    tr = _row_tile(r, n * c, 2 * 1024 * 1024)

    def body(b_ref, o_ref):
        acc = b_ref[0].astype(F32)
        for k in range(1, n):
            acc = acc + b_ref[k].astype(F32)
        o_ref[...] = acc

    return pl.pallas_call(
        body,
        grid=(r // tr,),
        in_specs=[pl.BlockSpec((n, tr, c), lambda i: (0, i, 0))],
        out_specs=pl.BlockSpec((tr, c), lambda i: (i, 0)),
        out_shape=jax.ShapeDtypeStruct((r, c), F32),
        name=name,
        compiler_params=_cparams(("parallel",)),
    )(buf)


def _me():
    return lax.axis_index("x"), lax.axis_index("y"), lax.axis_index("c")


def _all_gather_small(blk, name, after=()):
    m_per, n = blk.shape

    def body(x_ref, *rest):
        out_ref, send_sems, recv_sems, local_sem = rest[len(after):]
        x, y, c = _me()
        me, sibling = (x, y, c), (x, y, 1 - c)
        chips = [(1 - x, y), (x, 1 - y), (1 - x, 1 - y)]

        def rows(px, py, pc):
            return out_ref.at[pl.ds((4 * px + 2 * py + pc) * m_per, m_per), :]

        def copy(k, block, to, src=None):
            return pltpu.make_async_remote_copy(
                src_ref=rows(*block) if src is None else src, dst_ref=rows(*block),
                send_sem=send_sems.at[k], recv_sem=recv_sems.at[k], device_id=to, device_id_type=MESH)

        mine = pltpu.make_async_copy(x_ref, rows(*me), local_sem)
        mine.start()
        first = [copy(0, me, sibling, src=x_ref)]
        first += [copy(1 + j, me, (*chip, c), src=x_ref) for j, chip in enumerate(chips)]
        for cp in first:
            cp.start()
        passed = [copy(4 + j, (*chip, c), sibling) for j, chip in enumerate(chips)]
        for j, chip in enumerate(chips):
            copy(1 + j, (*chip, c), me).wait_recv()
            passed[j].start()
        copy(0, sibling, me).wait_recv()
        for j, chip in enumerate(chips):
            copy(4 + j, (*chip, 1 - c), me).wait_recv()
        for cp in first + passed:
            cp.wait_send()
        mine.wait()

    return pl.pallas_call(
        body,
        out_shape=jax.ShapeDtypeStruct((N_DEV * m_per, n), blk.dtype),
        in_specs=[pl.BlockSpec(memory_space=pltpu.VMEM)] + [pl.BlockSpec(memory_space=pl.ANY)] * len(after),
        out_specs=pl.BlockSpec(memory_space=pltpu.VMEM),
        scratch_shapes=[pltpu.SemaphoreType.DMA((7,)), pltpu.SemaphoreType.DMA((7,)), pltpu.SemaphoreType.DMA],
        name=name,
        compiler_params=pltpu.CompilerParams(vmem_limit_bytes=VMEM_LIMIT),
    )(blk, *after)


_HBM = pl.BlockSpec(memory_space=pltpu.HBM)
_SEM = pl.BlockSpec(memory_space=pltpu.SEMAPHORE)
_EFFECT = pltpu.SideEffectType.DATAFLOW_SIDE_EFFECTING


def _other_chips(x, y):
    return [(1 - x, y), (x, 1 - y), (1 - x, 1 - y)]


def _gather_copy(w, j, src_ref, land_ref, send_sems, recv_sems, halved=False):
    x, y, c = _me()
    if halved:
        half = src_ref.shape[0] // 2
        src_ref = src_ref.at[pl.ds(c * half, half), :]
    return pltpu.make_async_remote_copy(
        src_ref=src_ref, dst_ref=land_ref.at[2 * x + y], send_sem=send_sems.at[3 * w + j], recv_sem=recv_sems.at[3 * w + j],
        device_id=(*_other_chips(x, y)[j], c), device_id_type=MESH)


def _gather_start(shards, halved, after, name):
    n = len(shards)
    lands = [lax.empty((N_CHIPS, s.shape[0] // 2 if w in halved else s.shape[0], s.shape[1]), s.dtype) for w, s in enumerate(shards)]

    def body(*refs):
        in_refs, land_refs = refs[:n], refs[n:2 * n]
        send_sems, recv_sems = refs[2 * n + 1], refs[2 * n + 2]
        token = refs[-1]
        for w in range(n):
            for j in range(3):
                _gather_copy(w, j, in_refs[w], land_refs[w], send_sems, recv_sems, w in halved).start()
        token[...] = jnp.zeros_like(token)

    res = pl.pallas_call(
        body,
        out_shape=(pltpu.SemaphoreType.DMA((3 * n,)), pltpu.SemaphoreType.DMA((3 * n,)),
                   *[pltpu.HBM(s.shape, s.dtype) for s in shards], *[pltpu.HBM(l.shape, l.dtype) for l in lands],
                   jax.ShapeDtypeStruct((SUBLANES, LANES), F32)),
        in_specs=[_HBM] * (2 * n) + [pl.BlockSpec(memory_space=pl.ANY)],
        out_specs=(_SEM, _SEM, *[_HBM] * (2 * n), pl.BlockSpec(memory_space=pltpu.VMEM)),
        input_output_aliases={i: 2 + i for i in range(2 * n)},
        name=name,
        compiler_params=pltpu.CompilerParams(has_side_effects=_EFFECT),
    )(*[pltpu.with_memory_space_constraint(a, pltpu.HBM) for a in list(shards) + lands], after)
    return res[0], res[1], res[2:2 + n], res[2 + n:2 + 2 * n], res[-1]


def _gather_wait(w, shard, land, send_sems, recv_sems, after, name, halved=False):
    def body(s_ref, land_ref, send_sems, recv_sems, after_ref, s_out, land_out, stage):
        x, y, _ = _me()
        if not halved:
            pltpu.sync_copy(s_ref, stage)
            pltpu.sync_copy(stage, land_out.at[2 * x + y])
        for j in range(3):
            cp = _gather_copy(w, j, s_ref, land_ref, send_sems, recv_sems, halved)
            cp.wait_send()
            cp.wait_recv()

    return pl.pallas_call(
        body,
        out_shape=(pltpu.HBM(shard.shape, shard.dtype), pltpu.HBM(land.shape, land.dtype)),
        in_specs=(_HBM, _HBM, _SEM, _SEM, pl.BlockSpec(memory_space=pl.ANY)),
        out_specs=(_HBM, _HBM),
        input_output_aliases={0: 0, 1: 1},
        scratch_shapes=[pltpu.VMEM((SUBLANES, LANES) if halved else shard.shape, shard.dtype)],
        name=name,
        compiler_params=pltpu.CompilerParams(has_side_effects=_EFFECT, vmem_limit_bytes=VMEM_LIMIT),
    )(shard, land, send_sems, recv_sems, after)


def _assemble_halves(shard, land, name):
    half = land.shape[1]

    def body(s_ref, land_ref, out_ref, send_sems, recv_sems, local_sems):
        x, y, c = _me()
        own = pltpu.make_async_copy(s_ref, out_ref.at[2 * x + y], local_sems.at[3])
        own.start()
        cps = []
        for j, (ox, oy) in enumerate(_other_chips(x, y)):
            qj = 2 * ox + oy
            mine = out_ref.at[qj, pl.ds(c * half, half), :]
            lc = pltpu.make_async_copy(land_ref.at[qj], mine, local_sems.at[j])
            lc.start()
            rc = pltpu.make_async_remote_copy(
                src_ref=land_ref.at[qj], dst_ref=mine, send_sem=send_sems.at[j], recv_sem=recv_sems.at[j],
                device_id=(x, y, 1 - c), device_id_type=MESH)
            rc.start()
            cps.append((lc, rc))
        for lc, rc in cps:
            rc.wait_recv()
        for lc, rc in cps:
            rc.wait_send()
            lc.wait()
        own.wait()

    vmem = pl.BlockSpec(memory_space=pltpu.VMEM)
    return pl.pallas_call(
        body,
        out_shape=jax.ShapeDtypeStruct((N_CHIPS,) + shard.shape, shard.dtype),
        in_specs=[vmem, vmem],
        out_specs=vmem,
        scratch_shapes=[pltpu.SemaphoreType.DMA((3,)), pltpu.SemaphoreType.DMA((3,)), pltpu.SemaphoreType.DMA((4,))],
        name=name,
        compiler_params=pltpu.CompilerParams(vmem_limit_bytes=VMEM_LIMIT),
    )(shard, land)


def _piece_shape(shape, kind):
    k, nn = shape
    if kind == "all":
        return (k, nn)
    return (k // 2, nn // N_CHIPS) if kind == "col" else (k // N_CHIPS // 2, nn)


def _piece_of(g_ref, kind, tq, tc):
    pr, pc = _piece_shape(g_ref.shape, kind)
    if kind == "all":
        return g_ref
    if kind == "col":
        return g_ref.at[pl.ds(tc * pr, pr), pl.ds(tq * pc, pc)]
    return g_ref.at[pl.ds((2 * tq + tc) * pr, pr), :]


def _scatter_copy(w, r, kind, g_ref, land_ref, send_sems, recv_sems):
    x, y, c = _me()
    tx, ty, tc = (x + ((r >> 2) & 1)) % 2, (y + ((r >> 1) & 1)) % 2, (c + (r & 1)) % 2
    return pltpu.make_async_remote_copy(
        src_ref=_piece_of(g_ref, kind, 2 * tx + ty, tc), dst_ref=land_ref.at[4 * x + 2 * y + c],
        send_sem=send_sems.at[N_DEV * w + r], recv_sem=recv_sems.at[N_DEV * w + r], device_id=(tx, ty, tc), device_id_type=MESH)


def _scatter_start(gs, kinds, name):
    n = len(gs)
    pieces = [_piece_shape(g.shape, kind) for g, kind in zip(gs, kinds)]
    lands = [lax.empty((N_DEV,) + p, g.dtype) for p, g in zip(pieces, gs)]

    def body(*refs):
        g_refs, land_refs, send_sems, recv_sems = refs[:n], refs[n:2 * n], refs[2 * n], refs[2 * n + 1]
        land_outs, stages = refs[3 * n + 2:4 * n + 2], refs[4 * n + 2:]
        x, y, c = _me()
        for w in range(n):
            for r in range(1, N_DEV):
                _scatter_copy(w, r, kinds[w], g_refs[w], land_refs[w], send_sems, recv_sems).start()
        for w in range(n):
            pltpu.sync_copy(_piece_of(g_refs[w], kinds[w], 2 * x + y, c), stages[w])
            pltpu.sync_copy(stages[w], land_outs[w].at[4 * x + 2 * y + c])

    arrays = list(gs) + lands
    res = pl.pallas_call(
        body,
        out_shape=(pltpu.SemaphoreType.DMA((N_DEV * n,)), pltpu.SemaphoreType.DMA((N_DEV * n,)),
                   *[pltpu.HBM(a.shape, a.dtype) for a in arrays]),
        in_specs=[_HBM] * (2 * n),
        out_specs=(_SEM, _SEM, *[_HBM] * (2 * n)),
        input_output_aliases={i: 2 + i for i in range(2 * n)},
        scratch_shapes=[pltpu.VMEM(p, g.dtype) for p, g in zip(pieces, gs)],
        name=name,
        compiler_params=pltpu.CompilerParams(has_side_effects=_EFFECT, vmem_limit_bytes=VMEM_LIMIT),
    )(*[pltpu.with_memory_space_constraint(a, pltpu.HBM) for a in arrays])
    return res[0], res[1], res[2:2 + n], res[2 + n:]


def _scatter_wait(send_sems, recv_sems, gs, lands, kinds, after, name):
    n = len(gs)

    def body(*refs):
        g_refs, land_refs, send_sems, recv_sems = refs[:n], refs[n:2 * n], refs[2 * n], refs[2 * n + 1]
        for w in range(n):
            for r in range(1, N_DEV):
                cp = _scatter_copy(w, r, kinds[w], g_refs[w], land_refs[w], send_sems, recv_sems)
                cp.wait_send()
                cp.wait_recv()

    arrays = list(gs) + list(lands)
    return pl.pallas_call(
        body,
        out_shape=tuple(pltpu.HBM(a.shape, a.dtype) for a in arrays),
        in_specs=(*[_HBM] * (2 * n), _SEM, _SEM, pl.BlockSpec(memory_space=pl.ANY)),
        out_specs=tuple([_HBM] * (2 * n)),
        input_output_aliases={i: i for i in range(2 * n)},
        name=name,
        compiler_params=pltpu.CompilerParams(has_side_effects=_EFFECT),
    )(*arrays, send_sems, recv_sems, after)[n:]


def _sum_swap(bufs, name):
    n = len(bufs)

    def body(*refs):
        in_refs, out_refs = refs[:n], refs[n:2 * n]
        send_sems, recv_sems = refs[2 * n:]
        x, y, c = _me()
        cps = []
        for w in range(n):
            slots, r, _ = bufs[w].shape
            mine = out_refs[w].at[c]
            for r0 in range(0, r, min(r, ROW_TILE)):
                rows = slice(r0, r0 + min(r, ROW_TILE))
                acc = in_refs[w][0, rows, :].astype(F32)
                for k in range(1, slots):
                    acc = acc + in_refs[w][k, rows, :].astype(F32)
                mine[rows, :] = acc
            rc = pltpu.make_async_remote_copy(
                src_ref=mine, dst_ref=mine, send_sem=send_sems.at[w], recv_sem=recv_sems.at[w],
                device_id=(x, y, 1 - c), device_id_type=MESH)
            rc.start()
            cps.append(rc)
        for rc in cps:
            rc.wait_recv()
        for rc in cps:
            rc.wait_send()

    vmem = pl.BlockSpec(memory_space=pltpu.VMEM)
    return pl.pallas_call(
        body,
        out_shape=[jax.ShapeDtypeStruct((2,) + b.shape[1:], F32) for b in bufs],
        in_specs=[vmem] * n,
        out_specs=[vmem] * n,
        scratch_shapes=[pltpu.SemaphoreType.DMA((n,)), pltpu.SemaphoreType.DMA((n,))],
        name=name,
        compiler_params=pltpu.CompilerParams(vmem_limit_bytes=VMEM_LIMIT),
    )(*bufs)


def _side_by_side(w):
    g, t, _ = w.shape
    return w.reshape(g // 2, 2, t, t).transpose(0, 2, 1, 3).reshape(g // 2, t, 2 * t)


def _to_streams(a, dil):
    if dil == 1:
        return a
    s, c = a.shape
    return a.reshape(s // dil, dil, c).transpose(1, 0, 2).reshape(s, c)


def _from_streams(a, dil):
    if dil == 1:
        return a
    s, c = a.shape
    return a.reshape(dil, s // dil, c).transpose(1, 0, 2).reshape(s, c)


def _mm_tiles(s):
    return min(s, 2048)


def _local_step(x0, target, mvec, ln_g, ln_b, small, fetch, emit, start):
    s, d = x0.shape
    tm = _mm_tiles(s)
    row = lambda v: v.reshape(1, -1)
    shift = [row(mvec[i, :d]) for i in range(4)]
    scale = [row(mvec[i, d:2 * d]) for i in range(4)]
    gate = [row(1.0 + mvec[i, 2 * d:]) for i in range(4)]
    lg = [row(ln_g[i]) for i in range(4)]
    lb = [row(ln_b[i]) for i in range(4)]
    mm = functools.partial(_mm, tm=tm)
    mm_w = functools.partial(_mm, tm=1024, tk=min(s, 2048), mode="tn")

    def resid_ln_epilogue(sub):
        def epi(y, xv, gate_v, g_v, b_v, sc_v, sh_v):
            xhat, _ = _ln_stats(ALPHA * xv + gate_v * y)
            xn = xhat * g_v + b_v
            return [y, xn, xn * (1.0 + sc_v) + sh_v]

        rows = [gate[sub], lg[sub], lb[sub], scale[sub + 1], shift[sub + 1]]
        return dict(outs=[F32, F32, MXU_DTYPE], epi=epi, extras=[("full", xs[sub])] + [("row", r) for r in rows])

    xs, ys, big = [x0], [], {}
    h0 = _mod(x0, scale[0], shift[0], start, "mod0")
    big["a_w_in"] = fetch("a_w_in", h0)
    uvpre = mm(h0, big["a_w_in"], mode="nn", name="a_in", outs=[F32], tn=512, tk=1024,
               epi=lambda r, bias: [r + bias], extras=[("row", small["a_b_in"])])
    gated = _spatial_fwd(uvpre, small["a_vn_g"], small["a_vn_b"], small["wc"], small["bias_full"], "a_spatial")
    big["a_w_out"] = fetch("a_w_out", gated)
    y0, x1, h1 = mm(gated, big["a_w_out"], mode="nn", name="a_out", tm=min(s, 1024), tn=d, tk=1024, **resid_ln_epilogue(0))
    ys.append(y0)
    xs.append(x1)
    relu2 = lambda r: [jnp.square(jnp.maximum(r, 0.0))]
    big["up0"] = fetch("up0", h1)
    r0 = mm(h1, big["up0"], mode="nn", name="up0", outs=[MXU_DTYPE], tn=1024, tk=1024, epi=relu2)
    big["down0"] = fetch("down0", r0)
    ys.append(mm(r0, big["down0"], mode="nn", name="down0", outs=[F32], tm=min(s, 1024), tn=1024, tk=2048))
    dils = [dil for _, dil in B_PATTERNS]
    x2, h2, *h2_streams = _resid_ln(xs[1], ys[1], gate[1], lg[1], lb[1], (scale[2], shift[2]), "ln1", [dil for dil in dils if dil > 1])
    h2_streams = [h2] + [a.reshape(s, d) for a in h2_streams]
    xs.append(x2)
    hg, qkvs, o_g, l_g, l_streams = [], [], [], [], []
    big["b_w_qkv"] = fetch("b_w_qkv", h2)
    for g, (_, dil) in enumerate(B_PATTERNS):
        hp = h2_streams[g]
        qkv = mm(hp, big["b_w_qkv"], mode="nn", name=f"qkv{g}", outs=[MXU_DTYPE], tn=768, tk=1024, b_col0=g * 3 * d, n_out=3 * d)
        og, lgv = _attn_fwd(qkv, small["slopes"], dil, f"attn_fwd{g}")
        hg.append(hp)
        qkvs.append(qkv)
        o_g.append(og if dil == 1 else og.reshape(dil, s // dil, d))
        l_g.append(_from_streams(lgv, dil))
        l_streams.append(lgv)
    o_mix = _combine_fwd(o_g, l_g, "combine")
    big["b_w_out"] = fetch("b_w_out", o_mix)
    y2, x3, h3 = mm(o_mix, big["b_w_out"], mode="nn", name="b_out", tm=min(s, 1024), tn=d, tk=1024, **resid_ln_epilogue(2))
    ys.append(y2)
    xs.append(x3)
    big["up1"] = fetch("up1", h3)
    r1 = mm(h3, big["up1"], mode="nn", name="up1", outs=[MXU_DTYPE], tn=1024, tk=1024, epi=relu2)
    big["down1"] = fetch("down1", r1)
    ys.append(mm(r1, big["down1"], mode="nn", name="down1", outs=[F32], tm=min(s, 1024), tn=1024, tk=2048))

    gb, red_ln, red_mod = {}, [None] * 4, [None] * 4

    def mlp_bwd(i, h, r, dyy):
        gb[f"down{i}"] = mm_w(r, dyy, name=f"g_down{i}", outs=[MXU_DTYPE], tn=1024)
        da = mm(dyy, big[f"down{i}"], mode="nt", name=f"d_down{i}", outs=[MXU_DTYPE], tn=1024, tk=1024,
                after=emit(f"down{i}", gb[f"down{i}"]),
                epi=lambda acc, rv: [acc * (2.0 * jnp.sqrt(rv.astype(F32)))], extras=[("full", r)])
        gb[f"up{i}"] = mm_w(h, da, name=f"g_up{i}", outs=[MXU_DTYPE], tn=1024)
        return [mm(da, big[f"up{i}"], mode="nt", name=f"d_up{i}", outs=[F32], tn=1024, tk=1024, after=emit(f"up{i}", gb[f"up{i}"]))]

    def join(sub, dxr, dhs, after=None):
        res = _mod_ln_bwd(dxr, dhs, xs[sub], scale[sub], xs[sub - 1], ys[sub - 1], gate[sub - 1], lg[sub - 1],
                          f"mod_ln_bwd{sub}", after=after)
        red_mod[sub], red_ln[sub - 1] = res[2], res[3]
        return res[0], res[1]

    loss, dxr, dyy, red_ln[3] = _last_ln_loss_bwd(xs[3], ys[3], gate[3], lg[3], lb[3], target, "ln3_loss_bwd")
    dxr, dyy = join(3, dxr, mlp_bwd(1, h3, r1, dyy))
    gb["b_w_out"] = mm_w(o_mix, dyy, name="g_b_out", outs=[MXU_DTYPE], tn=1024, tk=1024)
    do = mm(dyy, big["b_w_out"], mode="nt", name="d_b_out", outs=[F32], tn=1024, tk=1024, after=emit("b_w_out", gb["b_w_out"]))
    parts = _combine_bwd(do, o_mix, l_g, dils, "combine_bwd")
    dhs, gq = [], None
    for g, (_, dil) in enumerate(B_PATTERNS):
        do_g, dd_g = parts[g][0].reshape(s, d), _to_streams(parts[g][1], dil)
        dqkv = _attn_bwd(qkvs[g], do_g, l_streams[g], dd_g, small["slopes"], dil, f"attn_bwd{g}")
        gq = mm_w(hg[g], dqkv, name=f"g_qkv{g}", outs=[MXU_DTYPE], tn=1024, out_col0=g * 3 * d, out_cols=len(B_PATTERNS) * 3 * d, into=gq)
        dh = mm(dqkv, big["b_w_qkv"], mode="nt", name=f"d_qkv{g}", outs=[F32], tn=1024, tk=768, b_col0=g * 3 * d)
        dhs.append(dh if dil == 1 else dh.reshape(dil, s // dil, d))
    gb["b_w_qkv"] = gq
    dxr, dyy = join(2, dxr, dhs, after=emit("b_w_qkv", gb["b_w_qkv"]))
    dxr, dyy = join(1, dxr, mlp_bwd(0, h1, r0, dyy))
    gb["a_w_out"] = mm_w(gated, dyy, name="g_a_out", outs=[MXU_DTYPE], tn=1024)
    dgated = mm(dyy, big["a_w_out"], mode="nt", name="d_a_out", outs=[F32], tn=1024, tk=1024, after=emit("a_w_out", gb["a_w_out"]))
    duv, dws, dbias, dbin, dvg, dvb = _spatial_bwd(uvpre, dgated, small["a_vn_g"], small["a_vn_b"], small["wc"],
                                                   small["wct"], small["bias_full"], "a_spatial_bwd")
    tril = jnp.tril(jnp.ones((CHUNK, CHUNK), bool))
    dws = jnp.where(tril, dws, 0.0).reshape(-1, LANES)
    gb["a_w_in"] = mm_w(h0, duv, name="g_a_in", outs=[MXU_DTYPE], tn=1024, after=emit("a_w_s", dws.astype(MXU_DTYPE)))
    dh = mm(duv, big["a_w_in"], mode="nt", name="d_a_in", outs=[F32], tn=1024, tk=512, after=emit("a_w_in", gb["a_w_in"]))
    dx, red_mod[0] = _mod_bwd(dxr, [dh], xs[0], scale[0], "mod_bwd0")
    dm = [jnp.concatenate([red_mod[i][0], red_mod[i][1], red_ln[i][2]]) for i in range(4)]
    dlg, dlb = [red_ln[i][0] for i in range(4)], [red_ln[i][1] for i in range(4)]

    gsmall = {
        "a_b_in": dbin.reshape(-1), "a_vn_g": dvg.reshape(-1), "a_vn_b": dvb.reshape(-1),
        "a_w_s": dws.reshape(-1),
        "a_b_s": dbias.reshape(CHUNK, A_GROUPS, d // A_GROUPS).sum(-1).T.reshape(-1),
    }
    return loss, dx, gb, jnp.stack(dm), jnp.stack(dlg), jnp.stack(dlb), gsmall


BIG = ("a_w_in", "a_w_out", "up0", "down0", "b_w_qkv", "b_w_out", "up1", "down1")
BIG_KIND = {"a_w_in": "col", "a_w_out": "row", "b_w_qkv": "col", "b_w_out": "row",
            "up0": "col", "up1": "col", "down0": "row", "down1": "row", "a_w_s": "all"}
HALVED = ("a_w_in", "a_w_out", "down0", "b_w_qkv")
SCATTER_GROUPS = (("down1", "up1"), ("b_w_out", "b_w_qkv"), ("down0", "up0"), ("a_w_out", "a_w_in"), ("a_w_s",))
SMALL = ("a_b_in", "a_vn_g", "a_vn_b", "a_b_s")


def kernel(x, c, ada_w, ada_b, ln_g, ln_b, a_w_in, a_b_in, a_vn_g, a_vn_b, a_w_s, a_b_s, a_w_out, b_w_qkv, b_w_out, mlp_w_up, mlp_w_down, loss_target, m_ada_w, m_ada_b, m_ln_g, m_ln_b, m_a_w_in, m_a_b_in, m_a_vn_g, m_a_vn_b, m_a_w_s, m_a_b_s, m_a_w_out, m_b_w_qkv, m_b_w_out, m_mlp_w_up, m_mlp_w_down, v_ada_w, v_ada_b, v_ln_g, v_ln_b, v_a_w_in, v_a_b_in, v_a_vn_g, v_a_vn_b, v_a_w_s, v_a_b_s, v_a_w_out, v_b_w_qkv, v_b_w_out, v_mlp_w_up, v_mlp_w_down):
    s, d = x.shape[1], x.shape[2]
    xi, yi, ci = _me()
    q = 2 * xi + yi
    dev = 2 * q + ci
    nsub = 2 * DEPTH
    cs = ada_w.shape[-1]
    ls = ln_g.shape[-1]

    shards = {
        "a_w_in": a_w_in[0], "a_w_out": a_w_out[0], "b_w_qkv": b_w_qkv[0], "b_w_out": b_w_out[0],
        "up0": mlp_w_up[0], "up1": mlp_w_up[1], "down0": mlp_w_down[0], "down1": mlp_w_down[1],
    }
    cast = [shards[k].astype(MXU_DTYPE) for k in BIG]

    pack = jnp.concatenate([c.reshape(-1), ln_g.reshape(-1), ln_b.reshape(-1)]).reshape(-1, LANES)
    got = _all_gather_small(pack, "gather_small", after=cast).reshape(N_DEV, -1)
    c_all = got[:, :d]
    per_chip = got[0::2]
    ln_g_full = per_chip[:, d:d + nsub * ls].reshape(N_CHIPS, nsub, ls).transpose(1, 0, 2).reshape(nsub, d)
    ln_b_full = per_chip[:, d + nsub * ls:].reshape(N_CHIPS, nsub, ls).transpose(1, 0, 2).reshape(nsub, d)
    m_part = _ada_fwd(c_all, ada_w.reshape(nsub, d, cs), ada_b.reshape(nsub, 1, cs), "ada_fwd")
    m_all = _all_gather_small(m_part.reshape(-1, LANES), "gather_mod").reshape(N_DEV, nsub, N_DEV, cs)
    m_mine = lax.dynamic_index_in_dim(m_all[0::2], dev, axis=2, keepdims=False)
    mvec = m_mine.transpose(1, 0, 2).reshape(nsub, 3 * d)

    halved = {BIG.index(k) for k in HALVED}
    send_sems, recv_sems, shard_thru, lands, token = _gather_start(cast, halved, mvec, "gather_start")

    def fetch(k, after):
        w = BIG.index(k)
        shard, gw = _gather_wait(w, shard_thru[w], lands[w], send_sems, recv_sems, after, f"gather_wait_{k}", w in halved)
        if w in halved:
            gw = _assemble_halves(shard, gw, f"assemble_{k}")
        return gw if BIG_KIND[k] == "col" else gw.reshape(1, -1, gw.shape[-1])

    scattering, pending = {}, {}

    def emit(k, g):
        pending[k] = g
        group = next(gr for gr in SCATTER_GROUPS if k in gr)
        if k != group[-1]:
            return None
        scattering[group] = _scatter_start([pending[m] for m in group], [BIG_KIND[m] for m in group], f"scatter_start_{k}")
        return scattering[group][2][0]

    tril = jnp.tril(jnp.ones((CHUNK, CHUNK), bool))
    wc = jnp.where(tril, a_w_s[0], 0.0).astype(MXU_DTYPE)
    heads = jnp.arange(1, B_HEADS + 1, dtype=F32)
    small = {
        "a_b_in": a_b_in, "a_vn_g": a_vn_g, "a_vn_b": a_vn_b,
        "wc": _side_by_side(wc), "wct": _side_by_side(wc.transpose(0, 2, 1)),
        "bias_full": jnp.repeat(a_b_s[0].T, d // A_GROUPS, axis=1),
        "slopes": jnp.exp2(-8.0 * heads / B_HEADS),
    }

    loss_part, grad_x, gb, dm, dlg, dlb, gsmall = _local_step(x[0], loss_target[0], mvec, ln_g_full, ln_b_full, small, fetch, emit, token)

    weights = dict(ada_w=ada_w, ada_b=ada_b, ln_g=ln_g, ln_b=ln_b, a_w_in=a_w_in, a_b_in=a_b_in, a_vn_g=a_vn_g, a_vn_b=a_vn_b,
                   a_w_s=a_w_s, a_b_s=a_b_s, a_w_out=a_w_out, b_w_qkv=b_w_qkv, b_w_out=b_w_out, mlp_w_up=mlp_w_up, mlp_w_down=mlp_w_down)
    ms = dict(ada_w=m_ada_w, ada_b=m_ada_b, ln_g=m_ln_g, ln_b=m_ln_b, a_w_in=m_a_w_in, a_b_in=m_a_b_in, a_vn_g=m_a_vn_g, a_vn_b=m_a_vn_b,
              a_w_s=m_a_w_s, a_b_s=m_a_b_s, a_w_out=m_a_w_out, b_w_qkv=m_b_w_qkv, b_w_out=m_b_w_out, mlp_w_up=m_mlp_w_up, mlp_w_down=m_mlp_w_down)
    vs = dict(ada_w=v_ada_w, ada_b=v_ada_b, ln_g=v_ln_g, ln_b=v_ln_b, a_w_in=v_a_w_in, a_b_in=v_a_b_in, a_vn_g=v_a_vn_g, a_vn_b=v_a_vn_b,
              a_w_s=v_a_w_s, a_b_s=v_a_b_s, a_w_out=v_a_w_out, b_w_qkv=v_b_w_qkv, b_w_out=v_b_w_out, mlp_w_up=v_mlp_w_up, mlp_w_down=v_mlp_w_down)
    grads, updates = {}, {}

    def update(k):
        updates[k] = _adamw(weights[k], grads[k], ms[k], vs[k], f"adamw_{k}")
        return updates[k][0]

    gfull = {}

    def big_group(group, after):
        bufs = []
        for pair in (group[:2], group[2:]):
            bufs += _scatter_wait(*scattering[pair], [BIG_KIND[m] for m in pair], after, f"scatter_wait_{pair[-1]}")
        parts = [[i] for i, k in enumerate(group) if k == "b_w_qkv"] + [[i for i, k in enumerate(group) if k != "b_w_qkv"]]
        for part in parts:
            fulls = _sum_swap([bufs[i] for i in part], f"sum_swap_{group[part[0]]}")
            gfull.update({group[i]: f.reshape(-1, f.shape[-1]) for i, f in zip(part, fulls)})

    big_group(SCATTER_GROUPS[0] + SCATTER_GROUPS[1], grad_x)
    grads["b_w_qkv"], grads["b_w_out"] = gfull["b_w_qkv"][None], gfull["b_w_out"][None]
    update("b_w_out")
    done = update("b_w_qkv")

    pack_b = jnp.concatenate([dm.reshape(-1), dlg.reshape(-1), dlb.reshape(-1)] + [gsmall[k] for k in SMALL] + [loss_part.reshape(1)])
    n_small = pack_b.shape[0]
    pack_b = jnp.pad(pack_b, (0, -n_small % (256 * LANES)))
    got_b = _all_gather_small(pack_b.reshape(-1, LANES), "gather_small_grads", after=[done]).reshape(N_DEV, -1, LANES)
    tot = _sum_slots(got_b, "sum_small").reshape(-1)
    o = 0
    dm_tot = tot[o:o + nsub * 3 * d].reshape(nsub, 3 * d); o += nsub * 3 * d
    dlg_tot = tot[o:o + nsub * d].reshape(nsub, d); o += nsub * d
    dlb_tot = tot[o:o + nsub * d].reshape(nsub, d); o += nsub * d
    g_small = {}
    for k, ref in zip(SMALL, (a_b_in, a_vn_g, a_vn_b, a_b_s)):
        g_small[k] = tot[o:o + ref.size].reshape(ref.shape); o += ref.size
    loss = tot[o]
    assert o + 1 == n_small
    aws = _scatter_wait(*scattering[("a_w_s",)], ["all"], tot, "scatter_wait_a_w_s")[0]
    g_small["a_w_s"] = _sum_slots(aws, "sum_a_w_s").reshape(a_w_s.shape)
    dm_all = got_b.reshape(N_DEV, -1)[:, :nsub * 3 * d].reshape(N_DEV, nsub, 3 * d)
    dm_cols = lax.dynamic_slice_in_dim(dm_all, q * cs, cs, axis=2).transpose(1, 0, 2)
    grads.update({
        "ada_w": _ada_bwd(c_all.T, dm_cols, "ada_bwd").reshape(ada_w.shape),
        "ada_b": lax.dynamic_slice_in_dim(dm_tot, q * cs, cs, axis=1).reshape(ada_b.shape),
        "ln_g": lax.dynamic_slice_in_dim(dlg_tot, q * ls, ls, axis=1).reshape(ln_g.shape),
        "ln_b": lax.dynamic_slice_in_dim(dlb_tot, q * ls, ls, axis=1).reshape(ln_b.shape),
        **g_small,
    })
    for k in ("ada_b", "ln_g", "ln_b", "a_w_s") + SMALL:
        update(k)
    done = update("ada_w")

    big_group(SCATTER_GROUPS[2] + SCATTER_GROUPS[3], done)
    grads.update({
        "a_w_in": gfull["a_w_in"][None], "a_w_out": gfull["a_w_out"][None],
        "mlp_w_up": jnp.stack([gfull["up0"], gfull["up1"]]), "mlp_w_down": jnp.stack([gfull["down0"], gfull["down1"]]),
    })
    for k in ("a_w_in", "a_w_out", "mlp_w_up", "mlp_w_down"):
        update(k)
    names = list(weights)
    return (loss, grad_x[None], *[grads[k] for k in names], *[updates[k][0] for k in names],
            *[updates[k][1] for k in names], *[updates[k][2] for k in names])
```

```python
import functools
import math

import jax
import jax.numpy as jnp
from jax import lax
from jax.experimental import pallas as pl
from jax.experimental.pallas import tpu as pltpu

F32 = jnp.float32
MXU_DTYPE = jnp.bfloat16

DEPTH = 2
CHUNK = 128
A_GROUPS = 16
B_HEADS = 16
HEAD_DIM = 64
B_PATTERNS = ((128, 1), (512, 4), (2048, 16))
SPAN = 128
ALPHA = (2 * DEPTH) ** 0.25
LN_EPS = 1e-5
NEG = -1e30
ATT_SCALE = HEAD_DIM ** -0.5
ADAM_LR, ADAM_B1, ADAM_B2, ADAM_EPS, ADAM_WD, ADAM_STEP = 0.001, 0.9, 0.999, 1e-08, 0.01, 10

N_CHIPS = 4
N_DEV = 8
LANES = 128
SUBLANES = 8
VMEM_LIMIT = 52 * 1024 * 1024
ROW_TILE = 512
MM_ROW_CHUNK = 256
MESH = pl.DeviceIdType.MESH


def _cparams(sem):
    return pltpu.CompilerParams(dimension_semantics=sem, vmem_limit_bytes=VMEM_LIMIT)


def _fold8(v):
    r, c = v.shape
    return jnp.sum(v.reshape(r // SUBLANES, SUBLANES, c), axis=0)


def _gelu(x):
    c = math.sqrt(2.0 / math.pi)
    return 0.5 * x * (1.0 + jnp.tanh(c * (x + 0.044715 * (x * x * x))))


def _gelu_and_grad(x):
    c = math.sqrt(2.0 / math.pi)
    t = jnp.tanh(c * (x + 0.044715 * (x * x * x)))
    return 0.5 * x * (1.0 + t), 0.5 * (1.0 + t) + 0.5 * x * (1.0 - t * t) * c * (1.0 + 3.0 * 0.044715 * x * x)


def _dot(a, b, dims):
    return lax.dot_general(a.astype(MXU_DTYPE), b.astype(MXU_DTYPE), (dims, ((), ())), preferred_element_type=F32)


def _dot_nn(a, b):
    return _dot(a, b, ((1,), (0,)))


def _dot_nt(a, b):
    return _dot(a, b, ((1,), (1,)))


def _dot_tn(a, b):
    return _dot(a, b, ((0,), (0,)))


def _mm(a, b, *, mode, name, outs, tm, tn, tk, epi=None, extras=(), b_col0=0, n_out=None, after=None,
        out_col0=0, out_cols=None, into=None):
    if mode == "nn":
        m, kdim = a.shape
        p, kb, ns = b.shape
        assert kb == kdim and ns % tn == 0 and b_col0 % tn == 0
        n = n_out if n_out is not None else p * ns
        npt, j0 = ns // tn, b_col0 // tn
        a_spec = pl.BlockSpec((tm, tk), lambda i, j, k: (i, k))
        b_spec = pl.BlockSpec((None, tk, tn), lambda i, j, k: ((j + j0) // npt, k, (j + j0) % npt))
        dot = _dot_nn
    elif mode == "nt":
        m, kdim = a.shape
        p, n, ns = b.shape
        assert ns % tk == 0 and b_col0 % tk == 0
        npt, j0 = ns // tk, b_col0 // tk
        a_spec = pl.BlockSpec((tm, tk), lambda i, j, k: (i, k))
        b_spec = pl.BlockSpec((None, tn, tk), lambda i, j, k: ((k + j0) // npt, j, (k + j0) % npt))
        dot = _dot_nt
    else:
        kdim, m = a.shape
        kb, n = b.shape
        assert kb == kdim
        a_spec = pl.BlockSpec((tk, tm), lambda i, j, k: (k, i))
        b_spec = pl.BlockSpec((tk, tn), lambda i, j, k: (k, j))
        dot = _dot_tn
    assert m % tm == 0 and n % tn == 0 and kdim % tk == 0, (name, m, n, kdim, tm, tn, tk)
    nk = kdim // tk
    ex_specs, ex_arrays = [], []
    for kind, arr in extras:
        if kind == "row":
            ex_specs.append(pl.BlockSpec((1, tn), lambda i, j, k: (0, j)))
        else:
            ex_specs.append(pl.BlockSpec((tm, tn), lambda i, j, k: (i, j)))
        ex_arrays.append(arr)
    n_ex, n_o = len(ex_arrays), len(outs)
    deps = [d for d in (after, into) if d is not None]
    n_dep = len(deps)
    j_out = out_col0 // tn
    assert out_col0 % tn == 0 and (into is None or len(outs) == 1)

    def body(a_ref, b_ref, *rest):
        ex_refs, o_refs = rest[:n_ex], rest[n_ex + n_dep:n_ex + n_dep + n_o]
        k = pl.program_id(2)

        chunks = [slice(r0, r0 + min(tm, MM_ROW_CHUNK)) for r0 in range(0, tm, min(tm, MM_ROW_CHUNK))]

        def part(rows):
            return dot(a_ref[:, rows] if mode == "tn" else a_ref[rows, :], b_ref[...])

        def finish(r, rows):
            exs = [e[...] if kind == "row" else e[rows, :] for (kind, _), e in zip(extras, ex_refs)]
            vals = epi(r, *exs) if epi is not None else [r]
            for o, v in zip(o_refs, vals):
                o[rows, :] = v.astype(o.dtype)

        if nk == 1:
            for rows in chunks:
                finish(part(rows), rows)
            return
        acc = rest[n_ex + n_dep + n_o]

        @pl.when(k == 0)
        def _():
            for rows in chunks:
                acc[rows, :] = part(rows)

        @pl.when((k > 0) & (k < nk - 1))
        def _():
            for rows in chunks:
                acc[rows, :] += part(rows)

        @pl.when(k == nk - 1)
        def _():
            for rows in chunks:
                finish(acc[rows, :] + part(rows), rows)

    res = pl.pallas_call(
        body,
        grid=(m // tm, n // tn, nk),
        in_specs=[a_spec, b_spec] + ex_specs + [pl.BlockSpec(memory_space=pl.ANY)] * n_dep,
        out_specs=[pl.BlockSpec((tm, tn), lambda i, j, k: (i, j + j_out)) for _ in outs],
        out_shape=[jax.ShapeDtypeStruct((m, out_cols or n), dt) for dt in outs],
        input_output_aliases={} if into is None else {2 + n_ex + n_dep - 1: 0},
        scratch_shapes=[pltpu.VMEM((tm, tn), F32)] if nk > 1 else [],
        name=name,
        compiler_params=_cparams(("parallel", "parallel", "arbitrary")),
    )(a, b, *ex_arrays, *deps)
    return res if len(outs) > 1 else res[0]


def _rows(body, n_rows, tr, ins, outs, name, scratch=()):
    def spec(kind, shape):
        if kind == "blk":
            return pl.BlockSpec((tr,) + tuple(shape[1:]), lambda i: (i,) + (0,) * (len(shape) - 1))
        if kind == "dep":
            return pl.BlockSpec(memory_space=pl.ANY)
        if kind == "str":
            return pl.BlockSpec((shape[0], tr // shape[0], shape[2]), lambda i: (0, i, 0))
        return pl.BlockSpec(tuple(shape), lambda i: (0,) * len(shape))

    return pl.pallas_call(
        body,
        grid=(n_rows // tr,),
        in_specs=[spec(k, a.shape) for k, a in ins],
        out_specs=[spec(k, s) for k, s, _ in outs],
        out_shape=[jax.ShapeDtypeStruct(tuple(s), d) for _, s, d in outs],
        scratch_shapes=list(scratch),
        name=name,
        compiler_params=_cparams(("arbitrary",)),
    )(*[a for _, a in ins])


def _ln_stats(z):
    mu = jnp.mean(z, axis=-1, keepdims=True)
    zc = z - mu
    var = jnp.mean(zc * zc, axis=-1, keepdims=True)
    rstd = lax.rsqrt(var + LN_EPS)
    return zc * rstd, rstd


def _stream_scratch(c):
    return pltpu.VMEM((2 * (c // LANES), ROW_TILE, LANES), F32)


def _streams_in(ref3, scr):
    dil, n, c = ref3.shape
    for r in range(dil):
        for j in range(c // LANES):
            scr.at[j][pl.ds(r, n, stride=dil), :] = ref3[r, :, j * LANES:(j + 1) * LANES].astype(F32)
    return jnp.concatenate([scr[j] for j in range(c // LANES)], axis=1)


def _streams_out(val, ref3, scr):
    dil, n, c = ref3.shape
    nj = c // LANES
    for j in range(nj):
        scr[j] = val[:, j * LANES:(j + 1) * LANES].astype(F32)
    if dil == 16:
        seg = dil * n // 4
        for j in range(nj):
            for s0 in range(4):
                scr[nj + j, s0 * seg:(s0 + 1) * seg, :] = scr.at[j][pl.ds(s0, seg, stride=4), :]
        for r in range(dil):
            s1, s0 = divmod(r, 4)
            for j in range(nj):
                ref3[r, :, j * LANES:(j + 1) * LANES] = scr.at[nj + j][pl.ds(s0 * seg + s1, n, stride=4), :].astype(ref3.dtype)
        return
    for r in range(dil):
        for j in range(nj):
            ref3[r, :, j * LANES:(j + 1) * LANES] = scr.at[j][pl.ds(r, n, stride=dil), :].astype(ref3.dtype)


def _mod(x, scale, shift, after, name):
    s, d = x.shape

    def body(x_ref, sc_ref, sh_ref, dep_ref, h_ref):
        h_ref[...] = (x_ref[...] * (1.0 + sc_ref[...]) + sh_ref[...]).astype(h_ref.dtype)

    return _rows(body, s, ROW_TILE, [("blk", x), ("all", scale), ("all", shift), ("dep", after)], [("blk", (s, d), MXU_DTYPE)], name)[0]


def _resid_ln(x, y, gate, g, b, nxt, name, dils=()):
    s, d = x.shape

    def body(x_ref, y_ref, gate_ref, g_ref, b_ref, sc_ref, sh_ref, xn_ref, h_ref, *rest):
        z = ALPHA * x_ref[...] + gate_ref[...] * y_ref[...]
        xhat, _ = _ln_stats(z)
        xn = xhat * g_ref[...] + b_ref[...]
        xn_ref[...] = xn
        h = xn * (1.0 + sc_ref[...]) + sh_ref[...]
        h_ref[...] = h.astype(h_ref.dtype)
        for hs_ref in rest[:len(dils)]:
            _streams_out(h, hs_ref, rest[-1])

    return _rows(body, s, ROW_TILE,
                 [("blk", x), ("blk", y), ("all", gate), ("all", g), ("all", b), ("all", nxt[0]), ("all", nxt[1])],
                 [("blk", (s, d), F32), ("blk", (s, d), MXU_DTYPE)] + [("str", (dil, s // dil, d), MXU_DTYPE) for dil in dils], name,
                 scratch=[_stream_scratch(d)] if dils else [])


def _mod_bwd(dxr, dhs, x, scale, name, after=None):
    s, d = x.shape
    n_dh = len(dhs)
    n_dep = 0 if after is None else 1

    def body(dxr_ref, *rest):
        dh_refs = rest[:n_dh]
        x_ref, sc_ref, dx_ref, red_ref, a_sh, a_sc = rest[n_dh:n_dh + 2] + rest[n_dh + 2 + n_dep:]
        i = pl.program_id(0)

        @pl.when(i == 0)
        def _():
            a_sh[...] = jnp.zeros_like(a_sh)
            a_sc[...] = jnp.zeros_like(a_sc)

        dh = dh_refs[0][...]
        for r in dh_refs[1:]:
            dh = dh + r[...]
        dx_ref[...] = dxr_ref[...] + dh * (1.0 + sc_ref[...])
        a_sh[...] += _fold8(dh)
        a_sc[...] += _fold8(dh * x_ref[...])

        @pl.when(i == pl.num_programs(0) - 1)
        def _():
            red_ref[...] = jnp.zeros_like(red_ref)
            red_ref[0:1, :] = jnp.sum(a_sh[...], axis=0, keepdims=True)
            red_ref[1:2, :] = jnp.sum(a_sc[...], axis=0, keepdims=True)

    return _rows(body, s, ROW_TILE, [("blk", dxr)] + [("blk", h) for h in dhs] + [("blk", x), ("all", scale)] + [("dep", after)] * n_dep,
                 [("blk", (s, d), F32), ("all", (SUBLANES, d), F32)], name,
                 scratch=[pltpu.VMEM((SUBLANES, d), F32)] * 2)


def _last_ln_loss_bwd(x, y, gate, g, b, target, name):
    s, d = x.shape

    def body(x_ref, y_ref, gate_ref, g_ref, b_ref, t_ref, l_ref, dxr_ref, dyy_ref, red_ref, a_l, a_g, a_b, a_gate):
        i = pl.program_id(0)

        @pl.when(i == 0)
        def _():
            for a in (a_l, a_g, a_b, a_gate):
                a[...] = jnp.zeros_like(a)

        yv = y_ref[...]
        z = ALPHA * x_ref[...] + gate_ref[...] * yv
        xhat, rstd = _ln_stats(z)
        e = xhat * g_ref[...] + b_ref[...] - t_ref[...]
        a_l[...] += _fold8(e * e)
        dxo_v = e * (1.0 / d)
        dxh = dxo_v * g_ref[...]
        dz = rstd * (dxh - jnp.mean(dxh, axis=-1, keepdims=True) - xhat * jnp.mean(dxh * xhat, axis=-1, keepdims=True))
        dxr_ref[...] = ALPHA * dz
        dyy_ref[...] = (gate_ref[...] * dz).astype(dyy_ref.dtype)
        a_g[...] += _fold8(dxo_v * xhat)
        a_b[...] += _fold8(dxo_v)
        a_gate[...] += _fold8(dz * yv)

        @pl.when(i == pl.num_programs(0) - 1)
        def _():
            l_ref[...] = jnp.full(l_ref.shape, 0.5 / d, F32) * jnp.sum(a_l[...])
            red_ref[...] = jnp.zeros_like(red_ref)
            red_ref[0:1, :] = jnp.sum(a_g[...], axis=0, keepdims=True)
            red_ref[1:2, :] = jnp.sum(a_b[...], axis=0, keepdims=True)
            red_ref[2:3, :] = jnp.sum(a_gate[...], axis=0, keepdims=True)

    l, dxr, dyy, red = _rows(
        body, s, ROW_TILE, [("blk", x), ("blk", y), ("all", gate), ("all", g), ("all", b), ("blk", target)],
        [("all", (SUBLANES, LANES), F32), ("blk", (s, d), F32), ("blk", (s, d), MXU_DTYPE), ("all", (SUBLANES, d), F32)], name,
        scratch=[pltpu.VMEM((SUBLANES, d), F32)] * 4)
    return l[0, 0], dxr, dyy, red


def _mod_ln_bwd(dxr, dhs, x, scale, x_in, y, gate, g, name, after=None):
    s, d = x.shape
    n_dh = len(dhs)
    n_dep = 0 if after is None else 1

    def body(dxr_ref, *rest):
        dh_refs = rest[:n_dh]
        x_ref, sc_ref, xin_ref, y_ref, gate_ref, g_ref = rest[n_dh:n_dh + 6]
        dxr_out, dyy_ref, red_mod, red_ln, a_sh, a_sc, a_g, a_b, a_gate = rest[n_dh + 6 + n_dep:n_dh + 15 + n_dep]
        i = pl.program_id(0)

        @pl.when(i == 0)
        def _():
            for a in (a_sh, a_sc, a_g, a_b, a_gate):
                a[...] = jnp.zeros_like(a)

        dh = dh_refs[0][...]
        for r in dh_refs[1:]:
            dh = dh + (r[...] if len(r.shape) == 2 else _streams_in(r, rest[-1]))
        xv = x_ref[...]
        dxo_v = dxr_ref[...] + dh * (1.0 + sc_ref[...])
        a_sh[...] += _fold8(dh)
        a_sc[...] += _fold8(dh * xv)
        yv = y_ref[...]
        z = ALPHA * xin_ref[...] + gate_ref[...] * yv
        xhat, rstd = _ln_stats(z)
        dxh = dxo_v * g_ref[...]
        dz = rstd * (dxh - jnp.mean(dxh, axis=-1, keepdims=True) - xhat * jnp.mean(dxh * xhat, axis=-1, keepdims=True))
        dxr_out[...] = ALPHA * dz
        dyy_ref[...] = (gate_ref[...] * dz).astype(dyy_ref.dtype)
        a_g[...] += _fold8(dxo_v * xhat)
        a_b[...] += _fold8(dxo_v)
        a_gate[...] += _fold8(dz * yv)

        @pl.when(i == pl.num_programs(0) - 1)
        def _():
            red_mod[...] = jnp.zeros_like(red_mod)
            red_mod[0:1, :] = jnp.sum(a_sh[...], axis=0, keepdims=True)
            red_mod[1:2, :] = jnp.sum(a_sc[...], axis=0, keepdims=True)
            red_ln[...] = jnp.zeros_like(red_ln)
            red_ln[0:1, :] = jnp.sum(a_g[...], axis=0, keepdims=True)
            red_ln[1:2, :] = jnp.sum(a_b[...], axis=0, keepdims=True)
            red_ln[2:3, :] = jnp.sum(a_gate[...], axis=0, keepdims=True)

    ins = ([("blk", dxr)] + [("blk" if h.ndim == 2 else "str", h) for h in dhs]
           + [("blk", x), ("all", scale), ("blk", x_in), ("blk", y), ("all", gate), ("all", g)] + [("dep", after)] * n_dep)
    return _rows(body, s, ROW_TILE, ins,
                 [("blk", (s, d), F32), ("blk", (s, d), MXU_DTYPE), ("all", (SUBLANES, d), F32), ("all", (SUBLANES, d), F32)], name,
                 scratch=[pltpu.VMEM((SUBLANES, d), F32)] * 5 + [_stream_scratch(d)] * any(h.ndim == 3 for h in dhs))


def _left_half(shape):
    return lax.broadcasted_iota(jnp.int32, shape, 1) < (LANES // 2)


CHUNKS_PER_STEP = 2


def _chunks_of_step():
    return [slice(i * CHUNK, (i + 1) * CHUNK) for i in range(CHUNKS_PER_STEP)]


def _split_groups(v):
    left = _left_half(v.shape)
    return jnp.concatenate([jnp.where(left, v, 0.0), jnp.where(left, 0.0, v)], axis=0)


def _spatial_z(vn, wc_ref, bias_ref, j):
    return _dot_nn(wc_ref[j], _split_groups(vn[:, j * LANES:(j + 1) * LANES])) + bias_ref[:, j * LANES:(j + 1) * LANES]


def _spatial_fwd(uvpre, vn_g, vn_b, wc, bias_full, name):
    s, d2 = uvpre.shape
    d = d2 // 2

    def body(uv_ref, g_ref, b_ref, wc_ref, bias_ref, out_ref):
        for rows in _chunks_of_step():
            u = _gelu(uv_ref[rows, :d])
            v = _gelu(uv_ref[rows, d:])
            vh, _ = _ln_stats(v)
            vn = vh * g_ref[...] + b_ref[...]
            for j in range(d // LANES):
                z = _spatial_z(vn, wc_ref, bias_ref, j)
                out_ref[rows, j * LANES:(j + 1) * LANES] = (u[:, j * LANES:(j + 1) * LANES] * z).astype(out_ref.dtype)

    return _rows(body, s, CHUNKS_PER_STEP * CHUNK, [("blk", uvpre), ("all", vn_g), ("all", vn_b), ("all", wc), ("all", bias_full)],
                 [("blk", (s, d), MXU_DTYPE)], name)[0]


def _spatial_bwd(uvpre, dgated, vn_g, vn_b, wc, wct, bias_full, name):
    s, d2 = uvpre.shape
    d = d2 // 2

    def body(uv_ref, dg_ref, g_ref, b_ref, wc_ref, wct_ref, bias_ref,
             duv_ref, dws_ref, dbias_ref, dbin_ref, dvg_ref, dvb_ref, dvn_buf, a_bin, a_vg, a_vb):
        i = pl.program_id(0)

        @pl.when(i == 0)
        def _():
            dws_ref[...] = jnp.zeros_like(dws_ref)
            dbias_ref[...] = jnp.zeros_like(dbias_ref)
            a_bin[...] = jnp.zeros_like(a_bin)
            a_vg[...] = jnp.zeros_like(a_vg)
            a_vb[...] = jnp.zeros_like(a_vb)

        for rows in _chunks_of_step():
            u, u_grad = _gelu_and_grad(uv_ref[rows, :d])
            v, v_grad = _gelu_and_grad(uv_ref[rows, d:])
            vh, rstd = _ln_stats(v)
            vn = vh * g_ref[...] + b_ref[...]
            dg = dg_ref[rows, :]
            dzz = dg * u
            dbias_ref[...] += dzz
            for j in range(d // LANES):
                cols = slice(j * LANES, (j + 1) * LANES)
                z = _spatial_z(vn, wc_ref, bias_ref, j)
                dup = dg[:, cols] * z * u_grad[:, cols]
                duv_ref[rows, cols] = dup.astype(duv_ref.dtype)
                a_bin[:, cols] += _fold8(dup)
                dz2 = _split_groups(dzz[:, cols])
                dvn_buf[:, cols] = _dot_nn(wct_ref[j], dz2)
                dw2 = _dot_nt(dz2, vn[:, cols])
                dws_ref[2 * j] += dw2[:CHUNK]
                dws_ref[2 * j + 1] += dw2[CHUNK:]
            dvn = dvn_buf[...]
            a_vg[...] += _fold8(dvn * vh)
            a_vb[...] += _fold8(dvn)
            dvh = dvn * g_ref[...]
            dv = rstd * (dvh - jnp.mean(dvh, axis=-1, keepdims=True) - vh * jnp.mean(dvh * vh, axis=-1, keepdims=True))
            dvp = dv * v_grad
            duv_ref[rows, d:] = dvp.astype(duv_ref.dtype)
            a_bin[:, d:] += _fold8(dvp)

        @pl.when(i == pl.num_programs(0) - 1)
        def _():
            dbin_ref[...] = jnp.sum(a_bin[...], axis=0, keepdims=True)
            dvg_ref[...] = jnp.sum(a_vg[...], axis=0, keepdims=True)
            dvb_ref[...] = jnp.sum(a_vb[...], axis=0, keepdims=True)

    return _rows(body, s, CHUNKS_PER_STEP * CHUNK,
                 [("blk", uvpre), ("blk", dgated), ("all", vn_g), ("all", vn_b), ("all", wc), ("all", wct), ("all", bias_full)],
                 [("blk", (s, d2), MXU_DTYPE), ("all", (A_GROUPS, CHUNK, CHUNK), F32), ("all", (CHUNK, d), F32),
                  ("all", (1, d2), F32), ("all", (1, d), F32), ("all", (1, d), F32)], name,
                 scratch=[pltpu.VMEM((CHUNK, d), F32), pltpu.VMEM((SUBLANES, d2), F32),
                          pltpu.VMEM((SUBLANES, d), F32), pltpu.VMEM((SUBLANES, d), F32)])


def _head_mask(v, h):
    lane = lax.broadcasted_iota(jnp.int32, v.shape, 1)
    return jnp.where((lane >= h * HEAD_DIM) & (lane < (h + 1) * HEAD_DIM), v, jnp.zeros_like(v))


def _att_bias(slopes, dil):
    qi = lax.broadcasted_iota(jnp.int32, (SPAN, SPAN), 0)
    ki = lax.broadcasted_iota(jnp.int32, (SPAN, SPAN), 1)
    sl = slopes[:, None, None]
    cur = jnp.where(ki <= qi, -sl * (float(dil) * (qi - ki).astype(F32)), NEG)
    prev = jnp.where(ki >= qi, -sl * (float(dil) * (SPAN + qi - ki).astype(F32)), NEG)
    absent = jnp.full_like(prev, NEG)
    pairs = slopes.shape[0] // 2

    def fwd(pv):
        return jnp.concatenate([cur, pv], axis=2).reshape(pairs, 2 * SPAN, 2 * SPAN)

    def bwd(pv):
        return jnp.concatenate([cur.reshape(pairs, 2 * SPAN, SPAN), pv.reshape(pairs, 2 * SPAN, SPAN)], axis=1)

    return jnp.stack([fwd(absent), fwd(prev)]), jnp.stack([bwd(absent), bwd(prev)])


ATT_GROUP = 4


def _att_group(s, dil):
    nb = s // (dil * SPAN)
    grp = min(ATT_GROUP, nb)
    assert nb % grp == 0
    return nb, grp


def _att_specs(s, d, dil, kinds):
    nb, grp = _att_group(s, dil)

    def spec(part, which):
        if which == "group":
            return pl.BlockSpec((grp * SPAN, d), lambda b: (b, part))
        if which == "prev":
            return pl.BlockSpec((SPAN, d), lambda b: (jnp.where((grp * b) % nb == 0, grp * b, grp * b - 1), part))
        return pl.BlockSpec((SPAN, d), lambda b: (jnp.where((grp * b + grp - 1) % nb == nb - 1, grp * b + grp - 1, grp * b + grp), part))

    return [spec(part, which) for part, which in kinds]


def _head_col(v, head):
    return v[:, head:head + 1]


def _expand_heads(w, j):
    shape = (w.shape[0], LANES)
    return jnp.where(_left_half(shape), jnp.broadcast_to(_head_col(w, 2 * j), shape), jnp.broadcast_to(_head_col(w, 2 * j + 1), shape))


def _attn_fwd(qkv, slopes, dil, name):
    s, d3 = qkv.shape
    d = d3 // 3
    nb, grp = _att_group(s, dil)
    table, _ = _att_bias(slopes, dil)

    def body(q_ref, k_ref, kp_ref, v_ref, vp_ref, tb_ref, o_ref, l_ref):
        b = pl.program_id(0)
        left = _left_half((SPAN, LANES))
        lane = lax.broadcasted_iota(jnp.int32, (SPAN, LANES), 1)
        for sub in range(grp):
            rows, before = slice(sub * SPAN, (sub + 1) * SPAN), slice((sub - 1) * SPAN, sub * SPAN)
            variant = jnp.where((grp * b) % nb == 0, 0, 1) if sub == 0 else 1
            lses = jnp.zeros((SPAN, LANES), F32)
            for hp in range(d // LANES):
                cols = slice(hp * LANES, (hp + 1) * LANES)
                q = q_ref[rows, cols]
                q2 = jnp.concatenate([_head_mask(q, 0), _head_mask(q, 1)], axis=0) * ATT_SCALE
                k2 = jnp.concatenate([k_ref[rows, cols], kp_ref[:, cols] if sub == 0 else k_ref[before, cols]], axis=0)
                v2 = jnp.concatenate([v_ref[rows, cols], vp_ref[:, cols] if sub == 0 else v_ref[before, cols]], axis=0)
                sc = _dot_nt(q2, k2) + tb_ref[variant, hp]
                m = jnp.max(sc, axis=-1, keepdims=True)
                p = jnp.exp(sc - m)
                l = jnp.sum(p, axis=-1, keepdims=True)
                r = _dot_nn(p, v2) * (1.0 / l)
                lse = m + jnp.log(l)
                o_ref[rows, cols] = jnp.where(left, r[:SPAN], r[SPAN:])
                lses = jnp.where(lane == 2 * hp, lse[:SPAN], jnp.where(lane == 2 * hp + 1, lse[SPAN:], lses))
            l_ref[rows, :] = lses

    specs = _att_specs(s, d, dil, [(0, "group"), (1, "group"), (1, "prev"), (2, "group"), (2, "prev")])
    return pl.pallas_call(
        body,
        grid=(s // (grp * SPAN),),
        in_specs=specs + [pl.BlockSpec(table.shape, lambda b: (0, 0, 0, 0))],
        out_specs=[pl.BlockSpec((grp * SPAN, d), lambda b: (b, 0)), pl.BlockSpec((grp * SPAN, LANES), lambda b: (b, 0))],
        out_shape=[jax.ShapeDtypeStruct((s, d), F32), jax.ShapeDtypeStruct((s, LANES), F32)],
        name=name,
        compiler_params=_cparams(("parallel",)),
    )(qkv, qkv, qkv, qkv, qkv, table)


def _attn_bwd(qkv, do, lse, dd, slopes, dil, name):
    s, d3 = qkv.shape
    d = d3 // 3
    nb, grp = _att_group(s, dil)
    _, table = _att_bias(slopes, dil)

    def cols_stacked(cur, nxt, hp):
        return jnp.concatenate([jnp.broadcast_to(_head_col(a, 2 * hp + h), (SPAN, LANES)) for a in (cur, nxt) for h in range(2)], axis=0)

    def body(k_ref, v_ref, q_ref, qn_ref, do_ref, don_ref, l_ref, ln_ref, dd_ref, ddn_ref, tb_ref, out_ref, carry):
        b = pl.program_id(0)

        @pl.when(b == 0)
        def _():
            carry[...] = jnp.zeros_like(carry)

        wide = 2 * LANES
        head_of_lane = (lax.broadcasted_iota(jnp.int32, (SPAN, wide), 1) % LANES) // HEAD_DIM
        zero = jnp.zeros((SPAN, LANES), k_ref.dtype)

        def heads_stacked2(cur, nxt):
            return jnp.concatenate([jnp.where(head_of_lane == h, a, jnp.zeros_like(a)) for a in (cur, nxt) for h in range(2)], axis=0)

        def block_diagonal(a, b):
            return jnp.concatenate([jnp.concatenate([a, zero], axis=1), jnp.concatenate([zero, b], axis=1)], axis=0)

        for sub in range(grp):
            rows, after = slice(sub * SPAN, (sub + 1) * SPAN), slice((sub + 1) * SPAN, (sub + 2) * SPAN)
            last = sub == grp - 1
            variant = jnp.where((grp * b + sub) % nb == nb - 1, 0, 1) if last else 1
            lse_c, dd_c = l_ref[rows, :], dd_ref[rows, :]
            lse_n, dd_n = (ln_ref[...], ddn_ref[...]) if last else (l_ref[after, :], dd_ref[after, :])
            for hp2 in range(d // wide):
                cols = slice(hp2 * wide, (hp2 + 1) * wide)
                pa, pb = 2 * hp2, 2 * hp2 + 1
                ca, cb = slice(pa * LANES, (pa + 1) * LANES), slice(pb * LANES, (pb + 1) * LANES)
                kbd = block_diagonal(k_ref[rows, ca], k_ref[rows, cb])
                vbd = block_diagonal(v_ref[rows, ca], v_ref[rows, cb])
                q4 = heads_stacked2(q_ref[rows, cols], qn_ref[:, cols] if last else q_ref[after, cols])
                do4 = heads_stacked2(do_ref[rows, cols], don_ref[:, cols] if last else do_ref[after, cols])
                bias = jnp.concatenate([tb_ref[variant, pa], tb_ref[variant, pb]], axis=1)
                lse2 = jnp.concatenate([cols_stacked(lse_c, lse_n, pa), cols_stacked(lse_c, lse_n, pb)], axis=1)
                dd2 = jnp.concatenate([cols_stacked(dd_c, dd_n, pa), cols_stacked(dd_c, dd_n, pb)], axis=1)
                p = jnp.exp(_dot_nt(q4 * ATT_SCALE, kbd) + bias - lse2)
                ds = p * (_dot_nt(do4, vbd) - dd2)
                dq4 = _dot_nn(ds, kbd)
                left = head_of_lane == 0
                dq_cur = jnp.where(left, dq4[:SPAN], dq4[SPAN:2 * SPAN]) + carry[:, cols]
                carry[:, cols] = jnp.where(left, dq4[2 * SPAN:3 * SPAN], dq4[3 * SPAN:])
                out_ref[rows, cols] = (dq_cur * ATT_SCALE).astype(out_ref.dtype)
                for pair, lanes in ((pa, slice(0, LANES)), (pb, slice(LANES, wide))):
                    out_ref[rows, d + pair * LANES:d + (pair + 1) * LANES] = (_dot_tn(ds[:, lanes], q4[:, lanes]) * ATT_SCALE).astype(out_ref.dtype)
                    out_ref[rows, 2 * d + pair * LANES:2 * d + (pair + 1) * LANES] = _dot_tn(p[:, lanes], do4[:, lanes]).astype(out_ref.dtype)

    qkv_specs = _att_specs(s, d, dil, [(1, "group"), (2, "group"), (0, "group"), (0, "next")])
    wide = _att_specs(s, d, dil, [(0, "group"), (0, "next")])
    heads = _att_specs(s, LANES, dil, [(0, "group"), (0, "next")])
    return pl.pallas_call(
        body,
        grid=(s // (grp * SPAN),),
        in_specs=qkv_specs + wide + heads + heads + [pl.BlockSpec(table.shape, lambda b: (0, 0, 0, 0))],
        out_specs=pl.BlockSpec((grp * SPAN, d3), lambda b: (b, 0)),
        out_shape=jax.ShapeDtypeStruct((s, d3), MXU_DTYPE),
        scratch_shapes=[pltpu.VMEM((SPAN, d), F32)],
        name=name,
        compiler_params=_cparams(("arbitrary",)),
    )(qkv, qkv, qkv, qkv, do, do, lse, lse, dd, dd, table)


def _mix_weights(l_refs):
    ls = [r[...] for r in l_refs]
    m = functools.reduce(jnp.maximum, ls)
    es = [jnp.exp(l - m) for l in ls]
    tot = functools.reduce(lambda a, c: a + c, es)
    return [e / tot for e in es]


def _combine_fwd(os_, ls_, name):
    s, d = ls_[0].shape[0], os_[0].shape[-1]
    n = len(os_)
    n_str = sum(o.ndim == 3 for o in os_)

    def body(*refs):
        o_refs, l_refs, out_ref, scrs = refs[:n], refs[n:2 * n], refs[2 * n], list(refs[2 * n + 1:])
        ws = _mix_weights(l_refs)
        os_v = [o if len(o.shape) == 2 else _streams_in(o, scrs.pop()) for o in o_refs]
        for j in range(d // LANES):
            cols = slice(j * LANES, (j + 1) * LANES)
            acc = _expand_heads(ws[0], j) * os_v[0][:, cols]
            for w, o in zip(ws[1:], os_v[1:]):
                acc = acc + _expand_heads(w, j) * o[:, cols]
            out_ref[:, cols] = acc

    return _rows(body, s, ROW_TILE, [("blk" if a.ndim == 2 else "str", a) for a in os_] + [("blk", a) for a in ls_],
                 [("blk", (s, d), F32)], name, scratch=[_stream_scratch(d)] * n_str)[0]


def _combine_bwd(do, o, ls_, dils, name):
    s, d = o.shape
    n = len(ls_)
    sel = (lax.broadcasted_iota(jnp.int32, (d, LANES), 0) // HEAD_DIM == lax.broadcasted_iota(jnp.int32, (d, LANES), 1)).astype(F32)

    def body(do_ref, o_ref, *rest):
        l_refs, sel_ref, outs = rest[:n], rest[n], rest[n + 1:n + 1 + 2 * n]
        ws = _mix_weights(l_refs)
        dov = do_ref[...]
        r = jnp.dot(dov * o_ref[...], sel_ref[...], precision=lax.Precision.HIGHEST, preferred_element_type=F32)
        for g in range(n):
            outs[2 * g + 1][...] = ws[g] * r
            parts = [_expand_heads(ws[g], j) * dov[:, j * LANES:(j + 1) * LANES] for j in range(d // LANES)]
            if dils[g] == 1:
                for j, part in enumerate(parts):
                    outs[2 * g][:, j * LANES:(j + 1) * LANES] = part.astype(outs[2 * g].dtype)
            else:
                _streams_out(jnp.concatenate(parts, axis=1), outs[2 * g], rest[-1])

    outs = []
    for dil in dils:
        outs += [("blk", (s, d), MXU_DTYPE) if dil == 1 else ("str", (dil, s // dil, d), MXU_DTYPE), ("blk", (s, LANES), F32)]
    res = _rows(body, s, ROW_TILE, [("blk", do), ("blk", o)] + [("blk", l) for l in ls_] + [("all", sel)], outs, name,
                scratch=[_stream_scratch(d)])
    return [(res[2 * g], res[2 * g + 1]) for g in range(n)]


def _ada_fwd(c_all, w, b, name):
    nsub, d, cs = w.shape

    def body(c_ref, w_ref, b_ref, o_ref):
        cv = c_ref[...]
        sc = cv * (1.0 / (1.0 + jnp.exp(-cv)))
        o_ref[...] = _dot_nn(sc, w_ref[...]) + b_ref[...]

    return pl.pallas_call(
        body,
        grid=(nsub,),
        in_specs=[pl.BlockSpec(c_all.shape, lambda i: (0, 0)), pl.BlockSpec((None, d, cs), lambda i: (i, 0, 0)),
                  pl.BlockSpec((None, 1, cs), lambda i: (i, 0, 0))],
        out_specs=pl.BlockSpec((None, N_DEV, cs), lambda i: (i, 0, 0)),
        out_shape=jax.ShapeDtypeStruct((nsub, N_DEV, cs), F32),
        name=name,
        compiler_params=_cparams(("parallel",)),
    )(c_all, w, b)


def _ada_bwd(c_all_t, dm, name):
    d, nb = c_all_t.shape
    nsub, _, cs = dm.shape

    def body(c_ref, dm_ref, o_ref):
        cv = c_ref[...]
        sc = cv * (1.0 / (1.0 + jnp.exp(-cv)))
        acc = sc[:, 0:1] * dm_ref[0:1, :]
        for bi in range(1, nb):
            acc = acc + sc[:, bi:bi + 1] * dm_ref[bi:bi + 1, :]
        o_ref[...] = acc

    return pl.pallas_call(
        body,
        grid=(nsub,),
        in_specs=[pl.BlockSpec(c_all_t.shape, lambda i: (0, 0)), pl.BlockSpec((None, nb, cs), lambda i: (i, 0, 0))],
        out_specs=pl.BlockSpec((None, d, cs), lambda i: (i, 0, 0)),
        out_shape=jax.ShapeDtypeStruct((nsub, d, cs), F32),
        name=name,
        compiler_params=_cparams(("parallel",)),
    )(c_all_t, dm)


def _row_tile(r, row_elems, block_elems=256 * 1024):
    t = 2 * SUBLANES
    if r % t:
        return r
    while t * 2 * row_elems <= block_elems and r % (t * 2) == 0:
        t *= 2
    return t


def _adamw(w, g, m, v, name):
    shape = w.shape
    c = shape[-1]
    r = w.size // c
    tr = _row_tile(r, c, 512 * 1024)
    w2, g2, m2, v2 = [a.reshape(r, c) for a in (w, g, m, v)]
    bc1 = 1.0 - ADAM_B1 ** ADAM_STEP
    bc2 = 1.0 - ADAM_B2 ** ADAM_STEP

    def body(w_ref, g_ref, m_ref, v_ref, d_ref, nm_ref, nv_ref):
        gv = g_ref[...]
        nm = ADAM_B1 * m_ref[...] + (1.0 - ADAM_B1) * gv
        nv = ADAM_B2 * v_ref[...] + (1.0 - ADAM_B2) * (gv * gv)
        d_ref[...] = -ADAM_LR * ((nm / bc1) / (jnp.sqrt(nv / bc2) + ADAM_EPS) + ADAM_WD * w_ref[...])
        nm_ref[...] = nm
        nv_ref[...] = nv

    res = _rows(body, r, tr, [("blk", a) for a in (w2, g2, m2, v2)], [("blk", (r, c), F32)] * 3, name)
    return [a.reshape(shape) for a in res]


def _adamw_layers(w, gs, m, v, name):
    nl, k, c = w.shape
    r = nl * k
    tr = _row_tile(k, c, 512 * 1024)
    per = k // tr
    w2, m2, v2 = [a.reshape(r, c) for a in (w, m, v)]
    bc1 = 1.0 - ADAM_B1 ** ADAM_STEP
    bc2 = 1.0 - ADAM_B2 ** ADAM_STEP

    def body(w_ref, m_ref, v_ref, *refs):
        g_refs, (go_ref, d_ref, nm_ref, nv_ref) = refs[:nl], refs[nl:]
        layer = pl.program_id(0) // per
        gv = g_refs[0][...]
        for l in range(1, nl):
            gv = jnp.where(layer == l, g_refs[l][...], gv)
        go_ref[...] = gv
        nm = ADAM_B1 * m_ref[...] + (1.0 - ADAM_B1) * gv
        nv = ADAM_B2 * v_ref[...] + (1.0 - ADAM_B2) * (gv * gv)
        d_ref[...] = -ADAM_LR * ((nm / bc1) / (jnp.sqrt(nv / bc2) + ADAM_EPS) + ADAM_WD * w_ref[...])
        nm_ref[...] = nm
        nv_ref[...] = nv

    blk = pl.BlockSpec((tr, c), lambda i: (i, 0))
    g_specs = [pl.BlockSpec((tr, c), functools.partial(lambda l, i: (jnp.clip(i - l * per, 0, per - 1), 0), l)) for l in range(nl)]
    res = pl.pallas_call(
        body,
        grid=(r // tr,),
        in_specs=[blk] * 3 + g_specs,
        out_specs=[blk] * 4,
        out_shape=[jax.ShapeDtypeStruct((r, c), F32)] * 4,
        name=name,
        compiler_params=_cparams(("arbitrary",)),
    )(w2, m2, v2, *gs)
    return [a.reshape(w.shape) for a in res]


def _sum_slots(buf, name):
    n, r, c = buf.shape
    tr = _row_tile(r, n * c, 2 * 1024 * 1024)

    def body(b_ref, o_ref):
        acc = b_ref[0].astype(F32)
        for k in range(1, n):
            acc = acc + b_ref[k].astype(F32)
        o_ref[...] = acc

    return pl.pallas_call(
        body,
        grid=(r // tr,),
        in_specs=[pl.BlockSpec((n, tr, c), lambda i: (0, i, 0))],
        out_specs=pl.BlockSpec((tr, c), lambda i: (i, 0)),
        out_shape=jax.ShapeDtypeStruct((r, c), F32),
        name=name,
        compiler_params=_cparams(("parallel",)),
    )(buf)


def _me():
    return lax.axis_index("x"), lax.axis_index("y"), lax.axis_index("c")


def _all_gather_small(blk, name, after=()):
    m_per, n = blk.shape

    def body(x_ref, *rest):
        out_ref, send_sems, recv_sems, local_sem = rest[len(after):]
        x, y, c = _me()
        me, sibling = (x, y, c), (x, y, 1 - c)
        chips = [(1 - x, y), (x, 1 - y), (1 - x, 1 - y)]

        def rows(px, py, pc):
            return out_ref.at[pl.ds((4 * px + 2 * py + pc) * m_per, m_per), :]

        def copy(k, block, to, src=None):
            return pltpu.make_async_remote_copy(
                src_ref=rows(*block) if src is None else src, dst_ref=rows(*block),
                send_sem=send_sems.at[k], recv_sem=recv_sems.at[k], device_id=to, device_id_type=MESH)

        mine = pltpu.make_async_copy(x_ref, rows(*me), local_sem)
        mine.start()
        first = [copy(0, me, sibling, src=x_ref)]
        first += [copy(1 + j, me, (*chip, c), src=x_ref) for j, chip in enumerate(chips)]
        for cp in first:
            cp.start()
        passed = [copy(4 + j, (*chip, c), sibling) for j, chip in enumerate(chips)]
        for j, chip in enumerate(chips):
            copy(1 + j, (*chip, c), me).wait_recv()
            passed[j].start()
        copy(0, sibling, me).wait_recv()
        for j, chip in enumerate(chips):
            copy(4 + j, (*chip, 1 - c), me).wait_recv()
        for cp in first + passed:
            cp.wait_send()
        mine.wait()

    return pl.pallas_call(
        body,
        out_shape=jax.ShapeDtypeStruct((N_DEV * m_per, n), blk.dtype),
        in_specs=[pl.BlockSpec(memory_space=pltpu.VMEM)] + [pl.BlockSpec(memory_space=pl.ANY)] * len(after),
        out_specs=pl.BlockSpec(memory_space=pltpu.VMEM),
        scratch_shapes=[pltpu.SemaphoreType.DMA((7,)), pltpu.SemaphoreType.DMA((7,)), pltpu.SemaphoreType.DMA],
        name=name,
        compiler_params=pltpu.CompilerParams(vmem_limit_bytes=VMEM_LIMIT),
    )(blk, *after)


_HBM = pl.BlockSpec(memory_space=pltpu.HBM)
_SEM = pl.BlockSpec(memory_space=pltpu.SEMAPHORE)
_EFFECT = pltpu.SideEffectType.DATAFLOW_SIDE_EFFECTING


def _other_chips(x, y):
    return [(1 - x, y), (x, 1 - y), (1 - x, 1 - y)]


def _gather_copy(w, j, src_ref, land_ref, send_sems, recv_sems, halved=False):
    x, y, c = _me()
    if halved:
        half = src_ref.shape[0] // 2
        src_ref = src_ref.at[pl.ds(c * half, half), :]
    return pltpu.make_async_remote_copy(
        src_ref=src_ref, dst_ref=land_ref.at[2 * x + y], send_sem=send_sems.at[3 * w + j], recv_sem=recv_sems.at[3 * w + j],
        device_id=(*_other_chips(x, y)[j], c), device_id_type=MESH)


def _gather_start(shards, halved, after, name):
    n = len(shards)
    lands = [lax.empty((N_CHIPS, s.shape[0] // 2 if w in halved else s.shape[0], s.shape[1]), s.dtype) for w, s in enumerate(shards)]

    def body(*refs):
        in_refs, land_refs = refs[:n], refs[n:2 * n]
        send_sems, recv_sems = refs[2 * n + 1], refs[2 * n + 2]
        token = refs[-1]
        for w in range(n):
            for j in range(3):
                _gather_copy(w, j, in_refs[w], land_refs[w], send_sems, recv_sems, w in halved).start()
        token[...] = jnp.zeros_like(token)

    res = pl.pallas_call(
        body,
        out_shape=(pltpu.SemaphoreType.DMA((3 * n,)), pltpu.SemaphoreType.DMA((3 * n,)),
                   *[pltpu.HBM(s.shape, s.dtype) for s in shards], *[pltpu.HBM(l.shape, l.dtype) for l in lands],
                   jax.ShapeDtypeStruct((SUBLANES, LANES), F32)),
        in_specs=[_HBM] * (2 * n) + [pl.BlockSpec(memory_space=pl.ANY)],
        out_specs=(_SEM, _SEM, *[_HBM] * (2 * n), pl.BlockSpec(memory_space=pltpu.VMEM)),
        input_output_aliases={i: 2 + i for i in range(2 * n)},
        name=name,
        compiler_params=pltpu.CompilerParams(has_side_effects=_EFFECT),
    )(*[pltpu.with_memory_space_constraint(a, pltpu.HBM) for a in list(shards) + lands], after)
    return res[0], res[1], res[2:2 + n], res[2 + n:2 + 2 * n], res[-1]


def _gather_wait(w, shard, land, send_sems, recv_sems, after, name, halved=False):
    def body(s_ref, land_ref, send_sems, recv_sems, after_ref, s_out, land_out, stage):
        x, y, _ = _me()
        if not halved:
            pltpu.sync_copy(s_ref, stage)
            pltpu.sync_copy(stage, land_out.at[2 * x + y])
        for j in range(3):
            cp = _gather_copy(w, j, s_ref, land_ref, send_sems, recv_sems, halved)
            cp.wait_send()
            cp.wait_recv()

    return pl.pallas_call(
        body,
        out_shape=(pltpu.HBM(shard.shape, shard.dtype), pltpu.HBM(land.shape, land.dtype)),
        in_specs=(_HBM, _HBM, _SEM, _SEM, pl.BlockSpec(memory_space=pl.ANY)),
        out_specs=(_HBM, _HBM),
        input_output_aliases={0: 0, 1: 1},
        scratch_shapes=[pltpu.VMEM((SUBLANES, LANES) if halved else shard.shape, shard.dtype)],
        name=name,
        compiler_params=pltpu.CompilerParams(has_side_effects=_EFFECT, vmem_limit_bytes=VMEM_LIMIT),
    )(shard, land, send_sems, recv_sems, after)


def _assemble_halves(shard, land, name):
    half = land.shape[1]

    def body(s_ref, land_ref, out_ref, send_sems, recv_sems, local_sems):
        x, y, c = _me()
        own = pltpu.make_async_copy(s_ref, out_ref.at[2 * x + y], local_sems.at[3])
        own.start()
        cps = []
        for j, (ox, oy) in enumerate(_other_chips(x, y)):
            qj = 2 * ox + oy
            mine = out_ref.at[qj, pl.ds(c * half, half), :]
            lc = pltpu.make_async_copy(land_ref.at[qj], mine, local_sems.at[j])
            lc.start()
            rc = pltpu.make_async_remote_copy(
                src_ref=land_ref.at[qj], dst_ref=mine, send_sem=send_sems.at[j], recv_sem=recv_sems.at[j],
                device_id=(x, y, 1 - c), device_id_type=MESH)
            rc.start()
            cps.append((lc, rc))
        for lc, rc in cps:
            rc.wait_recv()
        for lc, rc in cps:
            rc.wait_send()
            lc.wait()
        own.wait()

    vmem = pl.BlockSpec(memory_space=pltpu.VMEM)
    return pl.pallas_call(
        body,
        out_shape=jax.ShapeDtypeStruct((N_CHIPS,) + shard.shape, shard.dtype),
        in_specs=[vmem, vmem],
        out_specs=vmem,
        scratch_shapes=[pltpu.SemaphoreType.DMA((3,)), pltpu.SemaphoreType.DMA((3,)), pltpu.SemaphoreType.DMA((4,))],
        name=name,
        compiler_params=pltpu.CompilerParams(vmem_limit_bytes=VMEM_LIMIT),
    )(shard, land)


def _piece_shape(shape, kind):
    k, nn = shape
    if kind == "all":
        return (k, nn)
    return (k // 2, nn // N_CHIPS) if kind == "col" else (k // N_CHIPS // 2, nn)


def _piece_of(g_ref, kind, tq, tc):
    pr, pc = _piece_shape(g_ref.shape, kind)
    if kind == "all":
        return g_ref
    if kind == "col":
        return g_ref.at[pl.ds(tc * pr, pr), pl.ds(tq * pc, pc)]
    return g_ref.at[pl.ds((2 * tq + tc) * pr, pr), :]


def _scatter_copy(w, r, kind, g_ref, land_ref, send_sems, recv_sems):
    x, y, c = _me()
    tx, ty, tc = (x + ((r >> 2) & 1)) % 2, (y + ((r >> 1) & 1)) % 2, (c + (r & 1)) % 2
    return pltpu.make_async_remote_copy(
        src_ref=_piece_of(g_ref, kind, 2 * tx + ty, tc), dst_ref=land_ref.at[4 * x + 2 * y + c],
        send_sem=send_sems.at[N_DEV * w + r], recv_sem=recv_sems.at[N_DEV * w + r], device_id=(tx, ty, tc), device_id_type=MESH)


def _scatter_start(gs, kinds, name):
    n = len(gs)
    pieces = [_piece_shape(g.shape, kind) for g, kind in zip(gs, kinds)]
    lands = [lax.empty((N_DEV,) + p, g.dtype) for p, g in zip(pieces, gs)]

    def body(*refs):
        g_refs, land_refs, send_sems, recv_sems = refs[:n], refs[n:2 * n], refs[2 * n], refs[2 * n + 1]
        land_outs, stages = refs[3 * n + 2:4 * n + 2], refs[4 * n + 2:]
        x, y, c = _me()
        for w in range(n):
            for r in range(1, N_DEV):
                _scatter_copy(w, r, kinds[w], g_refs[w], land_refs[w], send_sems, recv_sems).start()
        for w in range(n):
            pltpu.sync_copy(_piece_of(g_refs[w], kinds[w], 2 * x + y, c), stages[w])
            pltpu.sync_copy(stages[w], land_outs[w].at[4 * x + 2 * y + c])

    arrays = list(gs) + lands
    res = pl.pallas_call(
        body,
        out_shape=(pltpu.SemaphoreType.DMA((N_DEV * n,)), pltpu.SemaphoreType.DMA((N_DEV * n,)),
                   *[pltpu.HBM(a.shape, a.dtype) for a in arrays]),
        in_specs=[_HBM] * (2 * n),
        out_specs=(_SEM, _SEM, *[_HBM] * (2 * n)),
        input_output_aliases={i: 2 + i for i in range(2 * n)},
        scratch_shapes=[pltpu.VMEM(p, g.dtype) for p, g in zip(pieces, gs)],
        name=name,
        compiler_params=pltpu.CompilerParams(has_side_effects=_EFFECT, vmem_limit_bytes=VMEM_LIMIT),
    )(*[pltpu.with_memory_space_constraint(a, pltpu.HBM) for a in arrays])
    return res[0], res[1], res[2:2 + n], res[2 + n:]


def _scatter_wait(send_sems, recv_sems, gs, lands, kinds, after, name):
    n = len(gs)

    def body(*refs):
        g_refs, land_refs, send_sems, recv_sems = refs[:n], refs[n:2 * n], refs[2 * n], refs[2 * n + 1]
        for w in range(n):
            for r in range(1, N_DEV):
                cp = _scatter_copy(w, r, kinds[w], g_refs[w], land_refs[w], send_sems, recv_sems)
                cp.wait_send()
                cp.wait_recv()

    arrays = list(gs) + list(lands)
    return pl.pallas_call(
        body,
        out_shape=tuple(pltpu.HBM(a.shape, a.dtype) for a in arrays),
        in_specs=(*[_HBM] * (2 * n), _SEM, _SEM, pl.BlockSpec(memory_space=pl.ANY)),
        out_specs=tuple([_HBM] * (2 * n)),
        input_output_aliases={i: i for i in range(2 * n)},
        name=name,
        compiler_params=pltpu.CompilerParams(has_side_effects=_EFFECT),
    )(*arrays, send_sems, recv_sems, after)[n:]


def _sum_swap(bufs, name):
    n = len(bufs)

    def body(*refs):
        in_refs, out_refs = refs[:n], refs[n:2 * n]
        send_sems, recv_sems = refs[2 * n:]
        x, y, c = _me()
        cps = []
        for w in range(n):
            slots, r, _ = bufs[w].shape
            mine = out_refs[w].at[c]
            for r0 in range(0, r, min(r, ROW_TILE)):
                rows = slice(r0, r0 + min(r, ROW_TILE))
                acc = in_refs[w][0, rows, :].astype(F32)
                for k in range(1, slots):
                    acc = acc + in_refs[w][k, rows, :].astype(F32)
                mine[rows, :] = acc
            rc = pltpu.make_async_remote_copy(
                src_ref=mine, dst_ref=mine, send_sem=send_sems.at[w], recv_sem=recv_sems.at[w],
                device_id=(x, y, 1 - c), device_id_type=MESH)
            rc.start()
            cps.append(rc)
        for rc in cps:
            rc.wait_recv()
        for rc in cps:
            rc.wait_send()

    vmem = pl.BlockSpec(memory_space=pltpu.VMEM)
    return pl.pallas_call(
        body,
        out_shape=[jax.ShapeDtypeStruct((2,) + b.shape[1:], F32) for b in bufs],
        in_specs=[vmem] * n,
        out_specs=[vmem] * n,
        scratch_shapes=[pltpu.SemaphoreType.DMA((n,)), pltpu.SemaphoreType.DMA((n,))],
        name=name,
        compiler_params=pltpu.CompilerParams(vmem_limit_bytes=VMEM_LIMIT),
    )(*bufs)


def _side_by_side(w):
    g, t, _ = w.shape
    return w.reshape(g // 2, 2, t, t).transpose(0, 2, 1, 3).reshape(g // 2, t, 2 * t)


def _to_streams(a, dil):
    if dil == 1:
        return a
    s, c = a.shape
    return a.reshape(s // dil, dil, c).transpose(1, 0, 2).reshape(s, c)


def _from_streams(a, dil):
    if dil == 1:
        return a
    s, c = a.shape
    return a.reshape(dil, s // dil, c).transpose(1, 0, 2).reshape(s, c)


def _mm_tiles(s):
    return min(s, 2048)


def _local_step(x0, target, mvec, ln_g, ln_b, small, fetch, emit, start):
    s, d = x0.shape
    tm = _mm_tiles(s)
    row = lambda v: v.reshape(1, -1)
    shift = [row(mvec[i, :d]) for i in range(4)]
    scale = [row(mvec[i, d:2 * d]) for i in range(4)]
    gate = [row(1.0 + mvec[i, 2 * d:]) for i in range(4)]
    lg = [row(ln_g[i]) for i in range(4)]
    lb = [row(ln_b[i]) for i in range(4)]
    mm = functools.partial(_mm, tm=tm)
    mm_w = functools.partial(_mm, tm=1024, tk=min(s, 2048), mode="tn")

    def resid_ln_epilogue(sub):
        def epi(y, xv, gate_v, g_v, b_v, sc_v, sh_v):
            xhat, _ = _ln_stats(ALPHA * xv + gate_v * y)
            xn = xhat * g_v + b_v
            return [y, xn, xn * (1.0 + sc_v) + sh_v]

        rows = [gate[sub], lg[sub], lb[sub], scale[sub + 1], shift[sub + 1]]
        return dict(outs=[F32, F32, MXU_DTYPE], epi=epi, extras=[("full", xs[sub])] + [("row", r) for r in rows])

    xs, ys, big = [x0], [], {}
    h0 = _mod(x0, scale[0], shift[0], start, "mod0")
    big["a_w_in"] = fetch("a_w_in", h0)
    uvpre = mm(h0, big["a_w_in"], mode="nn", name="a_in", outs=[F32], tn=512, tk=1024,
               epi=lambda r, bias: [r + bias], extras=[("row", small["a_b_in"])])
    gated = _spatial_fwd(uvpre, small["a_vn_g"], small["a_vn_b"], small["wc"], small["bias_full"], "a_spatial")
    big["a_w_out"] = fetch("a_w_out", gated)
    y0, x1, h1 = mm(gated, big["a_w_out"], mode="nn", name="a_out", tm=min(s, 1024), tn=d, tk=1024, **resid_ln_epilogue(0))
    ys.append(y0)
    xs.append(x1)
    relu2 = lambda r: [jnp.square(jnp.maximum(r, 0.0))]
    big["up0"] = fetch("up0", h1)
    r0 = mm(h1, big["up0"], mode="nn", name="up0", outs=[MXU_DTYPE], tn=1024, tk=1024, epi=relu2)
    big["down0"] = fetch("down0", r0)
    ys.append(mm(r0, big["down0"], mode="nn", name="down0", outs=[F32], tm=min(s, 1024), tn=1024, tk=2048))
    dils = [dil for _, dil in B_PATTERNS]
    x2, h2, *h2_streams = _resid_ln(xs[1], ys[1], gate[1], lg[1], lb[1], (scale[2], shift[2]), "ln1", [dil for dil in dils if dil > 1])
    h2_streams = [h2] + [a.reshape(s, d) for a in h2_streams]
    xs.append(x2)
    hg, qkvs, o_g, l_g, l_streams = [], [], [], [], []
    big["b_w_qkv"] = fetch("b_w_qkv", h2)
    for g, (_, dil) in enumerate(B_PATTERNS):
        hp = h2_streams[g]
        qkv = mm(hp, big["b_w_qkv"], mode="nn", name=f"qkv{g}", outs=[MXU_DTYPE], tn=768, tk=1024, b_col0=g * 3 * d, n_out=3 * d)
        og, lgv = _attn_fwd(qkv, small["slopes"], dil, f"attn_fwd{g}")
        hg.append(hp)
        qkvs.append(qkv)
        o_g.append(og if dil == 1 else og.reshape(dil, s // dil, d))
        l_g.append(_from_streams(lgv, dil))
        l_streams.append(lgv)
    o_mix = _combine_fwd(o_g, l_g, "combine")
    big["b_w_out"] = fetch("b_w_out", o_mix)
    y2, x3, h3 = mm(o_mix, big["b_w_out"], mode="nn", name="b_out", tm=min(s, 1024), tn=d, tk=1024, **resid_ln_epilogue(2))
    ys.append(y2)
    xs.append(x3)
    big["up1"] = fetch("up1", h3)
    r1 = mm(h3, big["up1"], mode="nn", name="up1", outs=[MXU_DTYPE], tn=1024, tk=1024, epi=relu2)
    big["down1"] = fetch("down1", r1)
    ys.append(mm(r1, big["down1"], mode="nn", name="down1", outs=[F32], tm=min(s, 1024), tn=1024, tk=2048))

    gb, red_ln, red_mod = {}, [None] * 4, [None] * 4

    def mlp_bwd(i, h, r, dyy):
        gb[f"down{i}"] = mm_w(r, dyy, name=f"g_down{i}", outs=[MXU_DTYPE], tn=1024)
        da = mm(dyy, big[f"down{i}"], mode="nt", name=f"d_down{i}", outs=[MXU_DTYPE], tn=1024, tk=1024,
                after=emit(f"down{i}", gb[f"down{i}"]),
                epi=lambda acc, rv: [acc * (2.0 * jnp.sqrt(rv.astype(F32)))], extras=[("full", r)])
        gb[f"up{i}"] = mm_w(h, da, name=f"g_up{i}", outs=[MXU_DTYPE], tn=1024)
        return [mm(da, big[f"up{i}"], mode="nt", name=f"d_up{i}", outs=[F32], tn=1024, tk=1024, after=emit(f"up{i}", gb[f"up{i}"]))]

    def join(sub, dxr, dhs, after=None):
        res = _mod_ln_bwd(dxr, dhs, xs[sub], scale[sub], xs[sub - 1], ys[sub - 1], gate[sub - 1], lg[sub - 1],
                          f"mod_ln_bwd{sub}", after=after)
        red_mod[sub], red_ln[sub - 1] = res[2], res[3]
        return res[0], res[1]

    loss, dxr, dyy, red_ln[3] = _last_ln_loss_bwd(xs[3], ys[3], gate[3], lg[3], lb[3], target, "ln3_loss_bwd")
    dxr, dyy = join(3, dxr, mlp_bwd(1, h3, r1, dyy))
    gb["b_w_out"] = mm_w(o_mix, dyy, name="g_b_out", outs=[MXU_DTYPE], tn=1024, tk=1024)
    do = mm(dyy, big["b_w_out"], mode="nt", name="d_b_out", outs=[F32], tn=1024, tk=1024, after=emit("b_w_out", gb["b_w_out"]))
    parts = _combine_bwd(do, o_mix, l_g, dils, "combine_bwd")
    dhs, gq = [], None
    for g, (_, dil) in enumerate(B_PATTERNS):
        do_g, dd_g = parts[g][0].reshape(s, d), _to_streams(parts[g][1], dil)
        dqkv = _attn_bwd(qkvs[g], do_g, l_streams[g], dd_g, small["slopes"], dil, f"attn_bwd{g}")
        gq = mm_w(hg[g], dqkv, name=f"g_qkv{g}", outs=[MXU_DTYPE], tn=1024, out_col0=g * 3 * d, out_cols=len(B_PATTERNS) * 3 * d, into=gq)
        dh = mm(dqkv, big["b_w_qkv"], mode="nt", name=f"d_qkv{g}", outs=[F32], tn=1024, tk=768, b_col0=g * 3 * d)
        dhs.append(dh if dil == 1 else dh.reshape(dil, s // dil, d))
    gb["b_w_qkv"] = gq
    dxr, dyy = join(2, dxr, dhs, after=emit("b_w_qkv", gb["b_w_qkv"]))
    dxr, dyy = join(1, dxr, mlp_bwd(0, h1, r0, dyy))
    gb["a_w_out"] = mm_w(gated, dyy, name="g_a_out", outs=[MXU_DTYPE], tn=1024)
    dgated = mm(dyy, big["a_w_out"], mode="nt", name="d_a_out", outs=[F32], tn=1024, tk=1024, after=emit("a_w_out", gb["a_w_out"]))
    duv, dws, dbias, dbin, dvg, dvb = _spatial_bwd(uvpre, dgated, small["a_vn_g"], small["a_vn_b"], small["wc"],
                                                   small["wct"], small["bias_full"], "a_spatial_bwd")
    tril = jnp.tril(jnp.ones((CHUNK, CHUNK), bool))
    dws = jnp.where(tril, dws, 0.0).reshape(-1, LANES)
    gb["a_w_in"] = mm_w(h0, duv, name="g_a_in", outs=[MXU_DTYPE], tn=1024, after=emit("a_w_s", dws.astype(MXU_DTYPE)))
    dh = mm(duv, big["a_w_in"], mode="nt", name="d_a_in", outs=[F32], tn=1024, tk=512, after=emit("a_w_in", gb["a_w_in"]))
    dx, red_mod[0] = _mod_bwd(dxr, [dh], xs[0], scale[0], "mod_bwd0")
    dm = [jnp.concatenate([red_mod[i][0], red_mod[i][1], red_ln[i][2]]) for i in range(4)]
    dlg, dlb = [red_ln[i][0] for i in range(4)], [red_ln[i][1] for i in range(4)]

    gsmall = {
        "a_b_in": dbin.reshape(-1), "a_vn_g": dvg.reshape(-1), "a_vn_b": dvb.reshape(-1),
        "a_w_s": dws.reshape(-1),
        "a_b_s": dbias.reshape(CHUNK, A_GROUPS, d // A_GROUPS).sum(-1).T.reshape(-1),
    }
    return loss, dx, gb, jnp.stack(dm), jnp.stack(dlg), jnp.stack(dlb), gsmall


BIG = ("a_w_in", "a_w_out", "up0", "down0", "b_w_qkv", "b_w_out", "up1", "down1")
BIG_KIND = {"a_w_in": "col", "a_w_out": "row", "b_w_qkv": "col", "b_w_out": "row",
            "up0": "col", "up1": "col", "down0": "row", "down1": "row", "a_w_s": "all"}
HALVED = ("a_w_in", "a_w_out", "down0", "b_w_qkv")
SCATTER_GROUPS = (("down1", "up1"), ("b_w_out", "b_w_qkv"), ("down0", "up0"), ("a_w_out", "a_w_in"), ("a_w_s",))
SMALL = ("a_b_in", "a_vn_g", "a_vn_b", "a_b_s")


def kernel(x, c, ada_w, ada_b, ln_g, ln_b, a_w_in, a_b_in, a_vn_g, a_vn_b, a_w_s, a_b_s, a_w_out, b_w_qkv, b_w_out, mlp_w_up, mlp_w_down, loss_target, m_ada_w, m_ada_b, m_ln_g, m_ln_b, m_a_w_in, m_a_b_in, m_a_vn_g, m_a_vn_b, m_a_w_s, m_a_b_s, m_a_w_out, m_b_w_qkv, m_b_w_out, m_mlp_w_up, m_mlp_w_down, v_ada_w, v_ada_b, v_ln_g, v_ln_b, v_a_w_in, v_a_b_in, v_a_vn_g, v_a_vn_b, v_a_w_s, v_a_b_s, v_a_w_out, v_b_w_qkv, v_b_w_out, v_mlp_w_up, v_mlp_w_down):
    s, d = x.shape[1], x.shape[2]
    xi, yi, ci = _me()
    q = 2 * xi + yi
    dev = 2 * q + ci
    nsub = 2 * DEPTH
    cs = ada_w.shape[-1]
    ls = ln_g.shape[-1]

    shards = {
        "a_w_in": a_w_in[0], "a_w_out": a_w_out[0], "b_w_qkv": b_w_qkv[0], "b_w_out": b_w_out[0],
        "up0": mlp_w_up[0], "up1": mlp_w_up[1], "down0": mlp_w_down[0], "down1": mlp_w_down[1],
    }
    cast = [shards[k].astype(MXU_DTYPE) for k in BIG]

    pack = jnp.concatenate([c.reshape(-1), ln_g.reshape(-1), ln_b.reshape(-1)]).reshape(-1, LANES)
    got = _all_gather_small(pack, "gather_small", after=cast).reshape(N_DEV, -1)
    c_all = got[:, :d]
    per_chip = got[0::2]
    ln_g_full = per_chip[:, d:d + nsub * ls].reshape(N_CHIPS, nsub, ls).transpose(1, 0, 2).reshape(nsub, d)
    ln_b_full = per_chip[:, d + nsub * ls:].reshape(N_CHIPS, nsub, ls).transpose(1, 0, 2).reshape(nsub, d)
    m_part = _ada_fwd(c_all, ada_w.reshape(nsub, d, cs), ada_b.reshape(nsub, 1, cs), "ada_fwd")
    m_all = _all_gather_small(m_part.reshape(-1, LANES), "gather_mod").reshape(N_DEV, nsub, N_DEV, cs)
    m_mine = lax.dynamic_index_in_dim(m_all[0::2], dev, axis=2, keepdims=False)
    mvec = m_mine.transpose(1, 0, 2).reshape(nsub, 3 * d)

    halved = {BIG.index(k) for k in HALVED}
    send_sems, recv_sems, shard_thru, lands, token = _gather_start(cast, halved, mvec, "gather_start")

    def fetch(k, after):
        w = BIG.index(k)
        shard, gw = _gather_wait(w, shard_thru[w], lands[w], send_sems, recv_sems, after, f"gather_wait_{k}", w in halved)
        if w in halved:
            gw = _assemble_halves(shard, gw, f"assemble_{k}")
        return gw if BIG_KIND[k] == "col" else gw.reshape(1, -1, gw.shape[-1])

    scattering, pending = {}, {}

    def emit(k, g):
        pending[k] = g
        group = next(gr for gr in SCATTER_GROUPS if k in gr)
        if k != group[-1]:
            return None
        scattering[group] = _scatter_start([pending[m] for m in group], [BIG_KIND[m] for m in group], f"scatter_start_{k}")
        return scattering[group][2][0]

    tril = jnp.tril(jnp.ones((CHUNK, CHUNK), bool))
    wc = jnp.where(tril, a_w_s[0], 0.0).astype(MXU_DTYPE)
    heads = jnp.arange(1, B_HEADS + 1, dtype=F32)
    small = {
        "a_b_in": a_b_in, "a_vn_g": a_vn_g, "a_vn_b": a_vn_b,
        "wc": _side_by_side(wc), "wct": _side_by_side(wc.transpose(0, 2, 1)),
        "bias_full": jnp.repeat(a_b_s[0].T, d // A_GROUPS, axis=1),
        "slopes": jnp.exp2(-8.0 * heads / B_HEADS),
    }

    loss_part, grad_x, gb, dm, dlg, dlb, gsmall = _local_step(x[0], loss_target[0], mvec, ln_g_full, ln_b_full, small, fetch, emit, token)

    weights = dict(ada_w=ada_w, ada_b=ada_b, ln_g=ln_g, ln_b=ln_b, a_w_in=a_w_in, a_b_in=a_b_in, a_vn_g=a_vn_g, a_vn_b=a_vn_b,
                   a_w_s=a_w_s, a_b_s=a_b_s, a_w_out=a_w_out, b_w_qkv=b_w_qkv, b_w_out=b_w_out, mlp_w_up=mlp_w_up, mlp_w_down=mlp_w_down)
    ms = dict(ada_w=m_ada_w, ada_b=m_ada_b, ln_g=m_ln_g, ln_b=m_ln_b, a_w_in=m_a_w_in, a_b_in=m_a_b_in, a_vn_g=m_a_vn_g, a_vn_b=m_a_vn_b,
              a_w_s=m_a_w_s, a_b_s=m_a_b_s, a_w_out=m_a_w_out, b_w_qkv=m_b_w_qkv, b_w_out=m_b_w_out, mlp_w_up=m_mlp_w_up, mlp_w_down=m_mlp_w_down)
    vs = dict(ada_w=v_ada_w, ada_b=v_ada_b, ln_g=v_ln_g, ln_b=v_ln_b, a_w_in=v_a_w_in, a_b_in=v_a_b_in, a_vn_g=v_a_vn_g, a_vn_b=v_a_vn_b,
              a_w_s=v_a_w_s, a_b_s=v_a_b_s, a_w_out=v_a_w_out, b_w_qkv=v_b_w_qkv, b_w_out=v_b_w_out, mlp_w_up=v_mlp_w_up, mlp_w_down=v_mlp_w_down)
    grads, updates = {}, {}

    def update(k):
        updates[k] = _adamw(weights[k], grads[k], ms[k], vs[k], f"adamw_{k}")
        return updates[k][0]

    gfull = {}

    def big_group(group, after):
        bufs = []
        for pair in (group[:2], group[2:]):
            bufs += _scatter_wait(*scattering[pair], [BIG_KIND[m] for m in pair], after, f"scatter_wait_{pair[-1]}")
        parts = [[i] for i, k in enumerate(group) if k == "b_w_qkv"] + [[i for i, k in enumerate(group) if k != "b_w_qkv"]]
        for part in parts:
            fulls = _sum_swap([bufs[i] for i in part], f"sum_swap_{group[part[0]]}")
            gfull.update({group[i]: f.reshape(-1, f.shape[-1]) for i, f in zip(part, fulls)})

    big_group(SCATTER_GROUPS[0] + SCATTER_GROUPS[1], grad_x)
    grads["b_w_qkv"], grads["b_w_out"] = gfull["b_w_qkv"][None], gfull["b_w_out"][None]
    update("b_w_out")
    done = update("b_w_qkv")

    pack_b = jnp.concatenate([dm.reshape(-1), dlg.reshape(-1), dlb.reshape(-1)] + [gsmall[k] for k in SMALL] + [loss_part.reshape(1)])
    n_small = pack_b.shape[0]
    pack_b = jnp.pad(pack_b, (0, -n_small % (256 * LANES)))
    got_b = _all_gather_small(pack_b.reshape(-1, LANES), "gather_small_grads", after=[done]).reshape(N_DEV, -1, LANES)
    tot = _sum_slots(got_b, "sum_small").reshape(-1)
    o = 0
    dm_tot = tot[o:o + nsub * 3 * d].reshape(nsub, 3 * d); o += nsub * 3 * d
    dlg_tot = tot[o:o + nsub * d].reshape(nsub, d); o += nsub * d
    dlb_tot = tot[o:o + nsub * d].reshape(nsub, d); o += nsub * d
    g_small = {}
    for k, ref in zip(SMALL, (a_b_in, a_vn_g, a_vn_b, a_b_s)):
        g_small[k] = tot[o:o + ref.size].reshape(ref.shape); o += ref.size
    loss = tot[o]
    assert o + 1 == n_small
    aws = _scatter_wait(*scattering[("a_w_s",)], ["all"], tot, "scatter_wait_a_w_s")[0]
    g_small["a_w_s"] = _sum_slots(aws, "sum_a_w_s").reshape(a_w_s.shape)
    dm_all = got_b.reshape(N_DEV, -1)[:, :nsub * 3 * d].reshape(N_DEV, nsub, 3 * d)
    dm_cols = lax.dynamic_slice_in_dim(dm_all, q * cs, cs, axis=2).transpose(1, 0, 2)
    grads.update({
        "ada_w": _ada_bwd(c_all.T, dm_cols, "ada_bwd").reshape(ada_w.shape),
        "ada_b": lax.dynamic_slice_in_dim(dm_tot, q * cs, cs, axis=1).reshape(ada_b.shape),
        "ln_g": lax.dynamic_slice_in_dim(dlg_tot, q * ls, ls, axis=1).reshape(ln_g.shape),
        "ln_b": lax.dynamic_slice_in_dim(dlb_tot, q * ls, ls, axis=1).reshape(ln_b.shape),
        **g_small,
    })
    for k in ("ada_b", "ln_g", "ln_b", "a_w_s") + SMALL:
        update(k)
    done = update("ada_w")

    big_group(SCATTER_GROUPS[2] + SCATTER_GROUPS[3], done)
    grads.update({"a_w_in": gfull["a_w_in"][None], "a_w_out": gfull["a_w_out"][None]})
    for k in ("a_w_in", "a_w_out"):
        update(k)
    for k, layers in (("mlp_w_up", ("up0", "up1")), ("mlp_w_down", ("down0", "down1"))):
        res = _adamw_layers(weights[k], [gfull[n] for n in layers], ms[k], vs[k], f"adamw_{k}")
        grads[k], updates[k] = res[0], res[1:]
    names = list(weights)
    return (loss, grad_x[None], *[grads[k] for k in names], *[updates[k][0] for k in names],
            *[updates[k][1] for k in names], *[updates[k][2] for k in names])
```
